```python
import jax, jax.numpy as jnp
from jax import lax
import numpy as np

D_MODEL = 1024
BATCH = 8
SEQ = 8192
DEPTH = 1

CONV_W = D_MODEL // 2
N_HEADS = 8
HEAD_DIM = (D_MODEL - CONV_W) // N_HEADS
ATTN_W = N_HEADS * HEAD_DIM
IN_COLS = 3 * CONV_W + 3 * ATTN_W
CONV_K = 3
D_FF = 2816
PLE_DIM = 256
DILATED_PAIRS = ((128, 1), (512, 4), (2048, 16))
BLOCK = 128
EPS = 1e-6

kernel_name = 'hybrid_conv_dilated_attn_convffn_ple'


def rmsnorm(a, g):
    af = a.astype(jnp.float32)
    af = af * lax.rsqrt(jnp.mean(af * af, axis=-1, keepdims=True) + EPS)
    return (af * g.astype(jnp.float32)).astype(a.dtype)


def causal_dwconv3(u, w, b):
    up = jnp.pad(u, ((0, 0), (CONV_K - 1, 0), (0, 0)))
    t = u.shape[1]
    return up[:, 0:t] * w[0] + up[:, 1:t + 1] * w[1] + up[:, 2:t + 2] * w[2] + b


def alibi_slopes(n):
    return jnp.exp2(-8.0 * jnp.arange(1, n + 1, dtype=jnp.float32) / n)


def dilated_branch(q, k, v, slopes, window, dilation):
    b, t, h, hd = q.shape
    steps = window // dilation
    L = t // dilation
    nb = -(-L // BLOCK)
    lp = nb * BLOCK

    def to_blocks(a):
        a = a.reshape(b, L, dilation, h, hd).transpose(0, 2, 3, 1, 4)
        a = jnp.pad(a, ((0, 0), (0, 0), (0, 0), (0, lp - L), (0, 0)))
        return a.reshape(b, dilation, h, nb, BLOCK, hd)

    def with_prev(a):
        prev = jnp.pad(a[:, :, :, :-1], ((0, 0), (0, 0), (0, 0), (1, 0), (0, 0), (0, 0)))
        return jnp.concatenate([prev, a], axis=4)

    qb = to_blocks(q)
    kk = with_prev(to_blocks(k))
    vv = with_prev(to_blocks(v))

    s = jnp.einsum('brhnqd,brhnkd->brhnqk', qb, kk) * (hd ** -0.5)
    qi = jnp.arange(BLOCK)[:, None] + BLOCK
    kj = jnp.arange(2 * BLOCK)[None, :]
    step = qi - kj
    band = (step >= 0) & (step <= steps)
    first = (jnp.arange(nb)[:, None, None] == 0) & (kj < BLOCK)[None]
    mask = band[None] & ~first
    dist = (step * dilation).astype(jnp.float32)
    bias = -slopes[:, None, None] * dist[None]
    s = jnp.where(mask, s + bias[:, None], -jnp.inf)
    m = jnp.max(s, axis=-1, keepdims=True)
    e = jnp.exp(s - m)
    den = jnp.sum(e, axis=-1, keepdims=True)
    o = jnp.einsum('brhnqk,brhnkd->brhnqd', e, vv) / den
    lse = (m + jnp.log(den))[..., 0]

    o = o.reshape(b, dilation, h, lp, hd)[:, :, :, :L]
    o = o.transpose(0, 3, 1, 2, 4).reshape(b, t, h, hd)
    lse = lse.reshape(b, dilation, h, lp)[:, :, :, :L]
    lse = lse.transpose(0, 3, 1, 2).reshape(b, t, h)
    return o, lse


def dilated_attention(q, k, v):
    slopes = alibi_slopes(q.shape[2])
    outs, lses = [], []
    for window, dilation in DILATED_PAIRS:
        o, lse = dilated_branch(q, k, v, slopes, window, dilation)
        outs.append(o)
        lses.append(lse)
    wts = jax.nn.softmax(jnp.stack(lses, axis=0), axis=0)
    o = jnp.sum(wts[..., None] * jnp.stack(outs, axis=0), axis=0)
    return o


def _fwd_setup_inputs(seed: int = 0) -> dict:
    key = jax.random.key(seed)
    ks = jax.random.split(key, 24)
    f32 = jnp.float32

    def nrm(k, shape, fan):
        return jax.random.normal(k, shape, f32) * (fan ** -0.5)

    def gain(k, shape):
        return 1.0 + 0.05 * jax.random.normal(k, shape, f32)

    def bias(k, shape):
        return 0.01 * jax.random.normal(k, shape, f32)

    return {
        'x': jax.random.normal(ks[0], (BATCH, SEQ, D_MODEL), f32),
        'p': jax.random.normal(ks[1], (DEPTH, BATCH, SEQ, PLE_DIM), f32),
        'g_mix': gain(ks[2], (DEPTH, D_MODEL)),
        'w_in': nrm(ks[3], (DEPTH, D_MODEL, IN_COLS), D_MODEL),
        'conv_w': nrm(ks[4], (DEPTH, CONV_K, CONV_W), CONV_K),
        'conv_b': bias(ks[5], (DEPTH, CONV_W)),
        'q_norm_g': gain(ks[6], (DEPTH, HEAD_DIM)),
        'k_norm_g': gain(ks[7], (DEPTH, HEAD_DIM)),
        'g_out_conv': gain(ks[8], (DEPTH, CONV_W)),
        'g_out_attn': gain(ks[9], (DEPTH, ATTN_W)),
        'w_out': nrm(ks[10], (DEPTH, CONV_W + ATTN_W, D_MODEL), CONV_W + ATTN_W),
        'g_ffn': gain(ks[11], (DEPTH, D_MODEL)),
        'w_gate': nrm(ks[12], (DEPTH, D_MODEL, D_FF), D_MODEL),
        'w_up': nrm(ks[13], (DEPTH, D_MODEL, D_FF), D_MODEL),
        'ffn_conv_w': nrm(ks[14], (DEPTH, CONV_K, D_FF), CONV_K),
        'ffn_conv_b': bias(ks[15], (DEPTH, D_FF)),
        'w_down': nrm(ks[16], (DEPTH, D_FF, D_MODEL), D_FF),
        'g_ple': gain(ks[17], (DEPTH, D_MODEL)),
        'w_ple_gate': nrm(ks[18], (DEPTH, D_MODEL, D_MODEL), D_MODEL),
        'w_ple_proj': nrm(ks[19], (DEPTH, PLE_DIM, D_MODEL), PLE_DIM),
    }


def _fwd_reference(x, p, g_mix, w_in, conv_w, conv_b, q_norm_g, k_norm_g, g_out_conv,
              g_out_attn, w_out, g_ffn, w_gate, w_up, ffn_conv_w, ffn_conv_b, w_down,
              g_ple, w_ple_gate, w_ple_proj):
    b, t, _ = x.shape
    for i in range(DEPTH):
        h = rmsnorm(x, g_mix[i])
        z = h @ w_in[i]
        zb, zc, zx, zq, zk, zv = jnp.split(
            z, np.cumsum([CONV_W, CONV_W, CONV_W, ATTN_W, ATTN_W]), axis=-1)
        y_c = zb * causal_dwconv3(zc * zx, conv_w[i], conv_b[i])
        q = rmsnorm(zq.reshape(b, t, N_HEADS, HEAD_DIM), q_norm_g[i]).astype(jnp.float32)
        k = rmsnorm(zk.reshape(b, t, N_HEADS, HEAD_DIM), k_norm_g[i]).astype(jnp.float32)
        v = zv.reshape(b, t, N_HEADS, HEAD_DIM).astype(jnp.float32)
        y_a = dilated_attention(q, k, v).reshape(b, t, ATTN_W).astype(x.dtype)
        y = jnp.concatenate([rmsnorm(y_c, g_out_conv[i]), rmsnorm(y_a, g_out_attn[i])], axis=-1)
        x = x + y @ w_out[i]
        h = rmsnorm(x, g_ffn[i])
        gate = causal_dwconv3(h @ w_gate[i], ffn_conv_w[i], ffn_conv_b[i])
        x = x + (jax.nn.silu(gate) * (h @ w_up[i])) @ w_down[i]
        ple_gate = jax.nn.sigmoid(rmsnorm(x, g_ple[i]) @ w_ple_gate[i])
        x = x + ple_gate * (p[i].astype(x.dtype) @ w_ple_proj[i])
    return x


import jax as _jax
import jax.numpy as _jnp

TWIN_FORMAT = 'train_step'
FWD_PARAMS = ['x', 'p', 'g_mix', 'w_in', 'conv_w', 'conv_b', 'q_norm_g', 'k_norm_g', 'g_out_conv', 'g_out_attn', 'w_out', 'g_ffn', 'w_gate', 'w_up', 'ffn_conv_w', 'ffn_conv_b', 'w_down', 'g_ple', 'w_ple_gate', 'w_ple_proj']
TWIN_WEIGHTS = ['g_mix', 'w_in', 'conv_w', 'conv_b', 'q_norm_g', 'k_norm_g', 'g_out_conv', 'g_out_attn', 'w_out', 'g_ffn', 'w_gate', 'w_up', 'ffn_conv_w', 'ffn_conv_b', 'w_down', 'g_ple', 'w_ple_gate', 'w_ple_proj']
TWIN_DIFF_INPUT = 'x'
TWIN_INPUTS = ['x', 'p', 'g_mix', 'w_in', 'conv_w', 'conv_b', 'q_norm_g', 'k_norm_g', 'g_out_conv', 'g_out_attn', 'w_out', 'g_ffn', 'w_gate', 'w_up', 'ffn_conv_w', 'ffn_conv_b', 'w_down', 'g_ple', 'w_ple_gate', 'w_ple_proj', 'loss_target', 'm_g_mix', 'm_w_in', 'm_conv_w', 'm_conv_b', 'm_q_norm_g', 'm_k_norm_g', 'm_g_out_conv', 'm_g_out_attn', 'm_w_out', 'm_g_ffn', 'm_w_gate', 'm_w_up', 'm_ffn_conv_w', 'm_ffn_conv_b', 'm_w_down', 'm_g_ple', 'm_w_ple_gate', 'm_w_ple_proj', 'v_g_mix', 'v_w_in', 'v_conv_w', 'v_conv_b', 'v_q_norm_g', 'v_k_norm_g', 'v_g_out_conv', 'v_g_out_attn', 'v_w_out', 'v_g_ffn', 'v_w_gate', 'v_w_up', 'v_ffn_conv_w', 'v_ffn_conv_b', 'v_w_down', 'v_g_ple', 'v_w_ple_gate', 'v_w_ple_proj']
TWIN_OUTPUTS = ['loss', 'grad_x', 'grad_g_mix', 'grad_w_in', 'grad_conv_w', 'grad_conv_b', 'grad_q_norm_g', 'grad_k_norm_g', 'grad_g_out_conv', 'grad_g_out_attn', 'grad_w_out', 'grad_g_ffn', 'grad_w_gate', 'grad_w_up', 'grad_ffn_conv_w', 'grad_ffn_conv_b', 'grad_w_down', 'grad_g_ple', 'grad_w_ple_gate', 'grad_w_ple_proj', 'delta_g_mix', 'delta_w_in', 'delta_conv_w', 'delta_conv_b', 'delta_q_norm_g', 'delta_k_norm_g', 'delta_g_out_conv', 'delta_g_out_attn', 'delta_w_out', 'delta_g_ffn', 'delta_w_gate', 'delta_w_up', 'delta_ffn_conv_w', 'delta_ffn_conv_b', 'delta_w_down', 'delta_g_ple', 'delta_w_ple_gate', 'delta_w_ple_proj', 'new_m_g_mix', 'new_m_w_in', 'new_m_conv_w', 'new_m_conv_b', 'new_m_q_norm_g', 'new_m_k_norm_g', 'new_m_g_out_conv', 'new_m_g_out_attn', 'new_m_w_out', 'new_m_g_ffn', 'new_m_w_gate', 'new_m_w_up', 'new_m_ffn_conv_w', 'new_m_ffn_conv_b', 'new_m_w_down', 'new_m_g_ple', 'new_m_w_ple_gate', 'new_m_w_ple_proj', 'new_v_g_mix', 'new_v_w_in', 'new_v_conv_w', 'new_v_conv_b', 'new_v_q_norm_g', 'new_v_k_norm_g', 'new_v_g_out_conv', 'new_v_g_out_attn', 'new_v_w_out', 'new_v_g_ffn', 'new_v_w_gate', 'new_v_w_up', 'new_v_ffn_conv_w', 'new_v_ffn_conv_b', 'new_v_w_down', 'new_v_g_ple', 'new_v_w_ple_gate', 'new_v_w_ple_proj']
TWIN_LEAF_KINDS = {'loss': 'loss', 'grad_x': 'grad_x', 'grad_g_mix': 'grad_w', 'grad_w_in': 'grad_w', 'grad_conv_w': 'grad_w', 'grad_conv_b': 'grad_w', 'grad_q_norm_g': 'grad_w', 'grad_k_norm_g': 'grad_w', 'grad_g_out_conv': 'grad_w', 'grad_g_out_attn': 'grad_w', 'grad_w_out': 'grad_w', 'grad_g_ffn': 'grad_w', 'grad_w_gate': 'grad_w', 'grad_w_up': 'grad_w', 'grad_ffn_conv_w': 'grad_w', 'grad_ffn_conv_b': 'grad_w', 'grad_w_down': 'grad_w', 'grad_g_ple': 'grad_w', 'grad_w_ple_gate': 'grad_w', 'grad_w_ple_proj': 'grad_w', 'delta_g_mix': 'delta_w', 'delta_w_in': 'delta_w', 'delta_conv_w': 'delta_w', 'delta_conv_b': 'delta_w', 'delta_q_norm_g': 'delta_w', 'delta_k_norm_g': 'delta_w', 'delta_g_out_conv': 'delta_w', 'delta_g_out_attn': 'delta_w', 'delta_w_out': 'delta_w', 'delta_g_ffn': 'delta_w', 'delta_w_gate': 'delta_w', 'delta_w_up': 'delta_w', 'delta_ffn_conv_w': 'delta_w', 'delta_ffn_conv_b': 'delta_w', 'delta_w_down': 'delta_w', 'delta_g_ple': 'delta_w', 'delta_w_ple_gate': 'delta_w', 'delta_w_ple_proj': 'delta_w', 'new_m_g_mix': 'new_m', 'new_m_w_in': 'new_m', 'new_m_conv_w': 'new_m', 'new_m_conv_b': 'new_m', 'new_m_q_norm_g': 'new_m', 'new_m_k_norm_g': 'new_m', 'new_m_g_out_conv': 'new_m', 'new_m_g_out_attn': 'new_m', 'new_m_w_out': 'new_m', 'new_m_g_ffn': 'new_m', 'new_m_w_gate': 'new_m', 'new_m_w_up': 'new_m', 'new_m_ffn_conv_w': 'new_m', 'new_m_ffn_conv_b': 'new_m', 'new_m_w_down': 'new_m', 'new_m_g_ple': 'new_m', 'new_m_w_ple_gate': 'new_m', 'new_m_w_ple_proj': 'new_m', 'new_v_g_mix': 'new_v', 'new_v_w_in': 'new_v', 'new_v_conv_w': 'new_v', 'new_v_conv_b': 'new_v', 'new_v_q_norm_g': 'new_v', 'new_v_k_norm_g': 'new_v', 'new_v_g_out_conv': 'new_v', 'new_v_g_out_attn': 'new_v', 'new_v_w_out': 'new_v', 'new_v_g_ffn': 'new_v', 'new_v_w_gate': 'new_v', 'new_v_w_up': 'new_v', 'new_v_ffn_conv_w': 'new_v', 'new_v_ffn_conv_b': 'new_v', 'new_v_w_down': 'new_v', 'new_v_g_ple': 'new_v', 'new_v_w_ple_gate': 'new_v', 'new_v_w_ple_proj': 'new_v'}


def _forward(args):
    return _fwd_reference(*[args[k] for k in FWD_PARAMS])


def _output_shape():
    def fwd():
        inp = _fwd_setup_inputs(0)
        return _fwd_reference(*[inp[k] for k in FWD_PARAMS])
    out = _jax.eval_shape(fwd)
    return out.shape, out.dtype

N_MICROBATCH = 1
ADAM_LR = 0.001
ADAM_B1 = 0.9
ADAM_B2 = 0.999
ADAM_EPS = 1e-08
ADAM_WD = 0.01
ADAM_STEP = 10
PER_EXAMPLE_BATCH_AXIS = {'x': 0, 'p': 1, 'loss_target': 0}
SHARED_INPUTS = []
_WEIGHT_DTYPES = {'g_mix': _jnp.float32, 'w_in': _jnp.float32, 'conv_w': _jnp.float32, 'conv_b': _jnp.float32, 'q_norm_g': _jnp.float32, 'k_norm_g': _jnp.float32, 'g_out_conv': _jnp.float32, 'g_out_attn': _jnp.float32, 'w_out': _jnp.float32, 'g_ffn': _jnp.float32, 'w_gate': _jnp.float32, 'w_up': _jnp.float32, 'ffn_conv_w': _jnp.float32, 'ffn_conv_b': _jnp.float32, 'w_down': _jnp.float32, 'g_ple': _jnp.float32, 'w_ple_gate': _jnp.float32, 'w_ple_proj': _jnp.float32}
MOMENT_SCALE = {'g_mix': 1.375473e+00, 'w_in': 7.014482e-01, 'conv_w': 3.003342e+00, 'conv_b': 1.838959e+00, 'q_norm_g': 1.473537e+00, 'k_norm_g': 1.520621e+00, 'g_out_conv': 8.065952e+01, 'g_out_attn': 8.717284e+01, 'w_out': 2.419314e+00, 'g_ffn': 5.199352e+01, 'w_gate': 4.085374e-01, 'w_up': 5.037547e-01, 'ffn_conv_w': 5.948438e+00, 'ffn_conv_b': 7.169878e+00, 'w_down': 7.619331e-01, 'g_ple': 1.937421e+00, 'w_ple_gate': 1.236131e-01, 'w_ple_proj': 8.758698e-01}


def _to_microbatches(a, axis):
    t = _jnp.moveaxis(a, axis, 0)
    t = t.reshape((N_MICROBATCH, t.shape[0] // N_MICROBATCH) + t.shape[1:])
    return _jnp.moveaxis(t, 1, axis + 1)


def setup_inputs(seed: int = 0) -> dict:
    inp = _fwd_setup_inputs(seed)
    key = _jax.random.fold_in(_jax.random.key(seed), 7919)
    shape, _ = _output_shape()
    out = dict(inp)
    out["loss_target"] = _jax.random.normal(_jax.random.fold_in(key, 0), shape, _jnp.float32)
    for i, name in enumerate(TWIN_WEIGHTS):
        w = inp[name].astype(_jnp.float32)
        if MOMENT_SCALE is None:
            s = _jnp.sqrt(_jnp.mean(_jnp.square(w)) + 1e-30)
        else:
            s = MOMENT_SCALE[name]
        km, kv = _jax.random.split(_jax.random.fold_in(key, i + 1))
        out[name] = w
        out["m_" + name] = s * _jax.random.normal(km, w.shape, _jnp.float32)
        out["v_" + name] = (s * s) * _jax.random.uniform(kv, w.shape, _jnp.float32, 0.5, 1.5)
    if N_MICROBATCH > 1:
        for name, axis in PER_EXAMPLE_BATCH_AXIS.items():
            out[name] = _to_microbatches(out[name], axis)
    return {'x': out['x'], 'p': out['p'], 'g_mix': out['g_mix'], 'w_in': out['w_in'], 'conv_w': out['conv_w'], 'conv_b': out['conv_b'], 'q_norm_g': out['q_norm_g'], 'k_norm_g': out['k_norm_g'], 'g_out_conv': out['g_out_conv'], 'g_out_attn': out['g_out_attn'], 'w_out': out['w_out'], 'g_ffn': out['g_ffn'], 'w_gate': out['w_gate'], 'w_up': out['w_up'], 'ffn_conv_w': out['ffn_conv_w'], 'ffn_conv_b': out['ffn_conv_b'], 'w_down': out['w_down'], 'g_ple': out['g_ple'], 'w_ple_gate': out['w_ple_gate'], 'w_ple_proj': out['w_ple_proj'], 'loss_target': out['loss_target'], 'm_g_mix': out['m_g_mix'], 'm_w_in': out['m_w_in'], 'm_conv_w': out['m_conv_w'], 'm_conv_b': out['m_conv_b'], 'm_q_norm_g': out['m_q_norm_g'], 'm_k_norm_g': out['m_k_norm_g'], 'm_g_out_conv': out['m_g_out_conv'], 'm_g_out_attn': out['m_g_out_attn'], 'm_w_out': out['m_w_out'], 'm_g_ffn': out['m_g_ffn'], 'm_w_gate': out['m_w_gate'], 'm_w_up': out['m_w_up'], 'm_ffn_conv_w': out['m_ffn_conv_w'], 'm_ffn_conv_b': out['m_ffn_conv_b'], 'm_w_down': out['m_w_down'], 'm_g_ple': out['m_g_ple'], 'm_w_ple_gate': out['m_w_ple_gate'], 'm_w_ple_proj': out['m_w_ple_proj'], 'v_g_mix': out['v_g_mix'], 'v_w_in': out['v_w_in'], 'v_conv_w': out['v_conv_w'], 'v_conv_b': out['v_conv_b'], 'v_q_norm_g': out['v_q_norm_g'], 'v_k_norm_g': out['v_k_norm_g'], 'v_g_out_conv': out['v_g_out_conv'], 'v_g_out_attn': out['v_g_out_attn'], 'v_w_out': out['v_w_out'], 'v_g_ffn': out['v_g_ffn'], 'v_w_gate': out['v_w_gate'], 'v_w_up': out['v_w_up'], 'v_ffn_conv_w': out['v_ffn_conv_w'], 'v_ffn_conv_b': out['v_ffn_conv_b'], 'v_w_down': out['v_w_down'], 'v_g_ple': out['v_g_ple'], 'v_w_ple_gate': out['v_w_ple_gate'], 'v_w_ple_proj': out['v_w_ple_proj']}


def _loss(weights, diff, rest, loss_target):
    with _jax.named_scope("forward"):
        args = {**rest, TWIN_DIFF_INPUT: diff, **{k: w.astype(_WEIGHT_DTYPES[k]) for k, w in weights.items()}}
        y = _forward(args)
    with _jax.named_scope("loss_head"):
        err = _jnp.square(y.astype(_jnp.float32) - loss_target)
        return 0.5 * _jnp.sum(_jnp.mean(err, axis=-1)) if err.ndim else 0.5 * err


def _adamw(w, g, m, v):
    m = ADAM_B1 * m + (1.0 - ADAM_B1) * g
    v = ADAM_B2 * v + (1.0 - ADAM_B2) * _jnp.square(g)
    m_hat = m / (1.0 - ADAM_B1 ** ADAM_STEP)
    v_hat = v / (1.0 - ADAM_B2 ** ADAM_STEP)
    delta = -ADAM_LR * (m_hat / (_jnp.sqrt(v_hat) + ADAM_EPS) + ADAM_WD * w)
    return delta, m, v


def reference(x, p, g_mix, w_in, conv_w, conv_b, q_norm_g, k_norm_g, g_out_conv, g_out_attn, w_out, g_ffn, w_gate, w_up, ffn_conv_w, ffn_conv_b, w_down, g_ple, w_ple_gate, w_ple_proj, loss_target, m_g_mix, m_w_in, m_conv_w, m_conv_b, m_q_norm_g, m_k_norm_g, m_g_out_conv, m_g_out_attn, m_w_out, m_g_ffn, m_w_gate, m_w_up, m_ffn_conv_w, m_ffn_conv_b, m_w_down, m_g_ple, m_w_ple_gate, m_w_ple_proj, v_g_mix, v_w_in, v_conv_w, v_conv_b, v_q_norm_g, v_k_norm_g, v_g_out_conv, v_g_out_attn, v_w_out, v_g_ffn, v_w_gate, v_w_up, v_ffn_conv_w, v_ffn_conv_b, v_w_down, v_g_ple, v_w_ple_gate, v_w_ple_proj):
    given = dict(x=x, p=p, g_mix=g_mix, w_in=w_in, conv_w=conv_w, conv_b=conv_b, q_norm_g=q_norm_g, k_norm_g=k_norm_g, g_out_conv=g_out_conv, g_out_attn=g_out_attn, w_out=w_out, g_ffn=g_ffn, w_gate=w_gate, w_up=w_up, ffn_conv_w=ffn_conv_w, ffn_conv_b=ffn_conv_b, w_down=w_down, g_ple=g_ple, w_ple_gate=w_ple_gate, w_ple_proj=w_ple_proj, loss_target=loss_target, m_g_mix=m_g_mix, m_w_in=m_w_in, m_conv_w=m_conv_w, m_conv_b=m_conv_b, m_q_norm_g=m_q_norm_g, m_k_norm_g=m_k_norm_g, m_g_out_conv=m_g_out_conv, m_g_out_attn=m_g_out_attn, m_w_out=m_w_out, m_g_ffn=m_g_ffn, m_w_gate=m_w_gate, m_w_up=m_w_up, m_ffn_conv_w=m_ffn_conv_w, m_ffn_conv_b=m_ffn_conv_b, m_w_down=m_w_down, m_g_ple=m_g_ple, m_w_ple_gate=m_w_ple_gate, m_w_ple_proj=m_w_ple_proj, v_g_mix=v_g_mix, v_w_in=v_w_in, v_conv_w=v_conv_w, v_conv_b=v_conv_b, v_q_norm_g=v_q_norm_g, v_k_norm_g=v_k_norm_g, v_g_out_conv=v_g_out_conv, v_g_out_attn=v_g_out_attn, v_w_out=v_w_out, v_g_ffn=v_g_ffn, v_w_gate=v_w_gate, v_w_up=v_w_up, v_ffn_conv_w=v_ffn_conv_w, v_ffn_conv_b=v_ffn_conv_b, v_w_down=v_w_down, v_g_ple=v_g_ple, v_w_ple_gate=v_w_ple_gate, v_w_ple_proj=v_w_ple_proj)
    weights = {n: given[n] for n in TWIN_WEIGHTS}
    shared = {n: given[n] for n in SHARED_INPUTS}
    per_example = {n: given[n] for n in ['x', 'p']}
    grad_fn = _jax.value_and_grad(_loss, argnums=(0, 1))

    def one_microbatch(ex, loss_target):
        ex = dict(ex)
        diff = ex.pop(TWIN_DIFF_INPUT)
        return grad_fn(weights, diff, {**shared, **ex}, loss_target)

    if N_MICROBATCH == 1:
        loss, (grad_w, grad_x) = one_microbatch(per_example, given["loss_target"])
    else:
        def body(carry, xs):
            loss_sum, grad_sum = carry
            l_k, (gw_k, gx_k) = one_microbatch(xs[0], xs[1])
            with _jax.named_scope("update"):
                return (loss_sum + l_k, _jax.tree.map(_jnp.add, grad_sum, gw_k)), gx_k

        init = (_jnp.zeros((), _jnp.float32), _jax.tree.map(_jnp.zeros_like, weights))
        (loss, grad_w), grad_x = _jax.lax.scan(body, init, (per_example, given["loss_target"]))
    with _jax.named_scope("update"):
        delta_w, new_m, new_v = {}, {}, {}
        for n in TWIN_WEIGHTS:
            delta_w[n], new_m[n], new_v[n] = _adamw(weights[n], grad_w[n], given["m_" + n], given["v_" + n])
    return (loss, grad_x, *[grad_w[n] for n in TWIN_WEIGHTS], *[delta_w[n] for n in TWIN_WEIGHTS],
            *[new_m[n] for n in TWIN_WEIGHTS], *[new_v[n] for n in TWIN_WEIGHTS])
```

```python
import functools

import jax
import jax.numpy as jnp
from jax import lax
from jax.experimental import pallas as pl
from jax.experimental.pallas import tpu as pltpu

F32 = jnp.float32
BF16 = jnp.bfloat16

D_MODEL = 1024
CONV_W = 512
ATTN_W = 512
HEAD_DIM = 64
D_FF = 2816
PLE_DIM = 256
IN_COLS = 3 * CONV_W + 3 * ATTN_W
EPS = 1e-6
QK_BLOCK = 128
DILATIONS = (1, 4, 16)
ATTN_SCALE = HEAD_DIM ** -0.5

ADAM_LR = 0.001
ADAM_B1 = 0.9
ADAM_B2 = 0.999
ADAM_EPS = 1e-08
ADAM_WD = 0.01
ADAM_STEP = 10

N_DEV = 8
N_CHIP = 4
V7X_VMEM_LIMIT = 56 * 1024 * 1024
FF_CHUNKS = 2

BIG_ROWS = (("w_in", 384), ("w_out", 128), ("w_gate", 352), ("w_up", 352), ("w_down", 352),
            ("w_ple_gate", 128), ("w_ple_proj", 32))
BIG_TOTAL = sum(r for _, r in BIG_ROWS)
SMALL_ROWS = 24


def _cparams(*sem):
    return pltpu.CompilerParams(dimension_semantics=sem, vmem_limit_bytes=V7X_VMEM_LIMIT)


def _mm(a, b):
    return jnp.dot(a, b, preferred_element_type=F32)


def _mm_nt(a, b):
    return lax.dot_general(a, b, (((1,), (1,)), ((), ())), preferred_element_type=F32)


def _mm_tn(a, b):
    return lax.dot_general(a, b, (((0,), (0,)), ((), ())), preferred_element_type=F32)


def _full(shape):
    nd = len(shape)
    return pl.BlockSpec(shape, lambda *_: (0,) * nd)


def _rms_stats(x):
    r = lax.rsqrt(jnp.mean(x * x, axis=-1, keepdims=True) + EPS)
    return r, x * r


def _rms_bwd(dy, xhat, r, g):
    gd = dy * g
    return r * (gd - xhat * jnp.mean(gd * xhat, axis=-1, keepdims=True))


def _seg_sum64(v, bd_ref):
    outs = []
    for c in range(0, v.shape[1], 256):
        vc = v[:, c:c + 256]
        hi = vc.astype(BF16)
        lo = (vc - hi.astype(F32)).astype(BF16)
        outs.append(_mm(hi, bd_ref[...]) + _mm(lo, bd_ref[...]))
    return outs[0] if len(outs) == 1 else jnp.concatenate(outs, axis=1)


def _shift_rows(u, k, edge_rows):
    row = lax.broadcasted_iota(jnp.int32, u.shape, 0)
    out = pltpu.roll(u, k, 0)
    for j in range(k):
        out = jnp.where(row == j, edge_rows[k - 1 - j], out)
    return out


def _shift_rows_up(u, k, edge_rows):
    n = u.shape[0]
    row = lax.broadcasted_iota(jnp.int32, u.shape, 0)
    out = pltpu.roll(u, n - k, 0)
    for j in range(k):
        out = jnp.where(row == n - k + j, edge_rows[j], out)
    return out


def _conv_fwd(u, c1, c2, w_ref, b_ref):
    u1 = _shift_rows(u, 1, (c1,))
    u2 = _shift_rows(u, 2, (c1, c2))
    y = u2 * w_ref[0:1, :] + u1 * w_ref[1:2, :] + u * w_ref[2:3, :] + b_ref[...]
    return y, u1, u2


def _conv_bwd_input(dy, n1row, n2row, w_ref):
    d1 = _shift_rows_up(dy, 1, (n1row,))
    d2 = _shift_rows_up(dy, 2, (n1row, n2row))
    return dy * w_ref[2:3, :] + d1 * w_ref[1:2, :] + d2 * w_ref[0:1, :]


def _sigmoid(x):
    return 1.0 / (1.0 + jnp.exp(-x))


def _inproj_fwd(x, g_mix, w_in, conv_w, conv_b, qg, kg, bd, tm):
    t = x.shape[0]

    def body(x_ref, g_ref, w_ref, cw_ref, cb_ref, qg_ref, kg_ref, bd_ref,
             zc_ref, zqk_ref, yc_ref, q_ref, k_ref, v_ref, carry_ref):
        @pl.when(pl.program_id(0) == 0)
        def _():
            carry_ref[...] = jnp.zeros_like(carry_ref)

        _, xhat = _rms_stats(x_ref[...])
        h = (xhat * g_ref[...]).astype(BF16)
        zconv = _mm(h, w_ref[:, 0:3 * CONV_W])
        zc_ref[...] = zconv
        u = zconv[:, CONV_W:2 * CONV_W] * zconv[:, 2 * CONV_W:3 * CONV_W]
        cv, _, _ = _conv_fwd(u, carry_ref[7:8, :], carry_ref[6:7, :], cw_ref, cb_ref)
        yc_ref[...] = zconv[:, 0:CONV_W] * cv
        carry_ref[...] = u[tm - 8:tm, :]

        zqk = _mm(h, w_ref[:, 3 * CONV_W:3 * CONV_W + 2 * ATTN_W])
        zqk_ref[...] = zqk
        for j, (gain_ref, out_ref, scale) in enumerate(((qg_ref, q_ref, ATTN_SCALE), (kg_ref, k_ref, 1.0))):
            z = zqk[:, j * ATTN_W:(j + 1) * ATTN_W]
            r = lax.rsqrt(_seg_sum64(z * z, bd_ref) * (1.0 / HEAD_DIM) + EPS)
            out_ref[...] = (z * r * gain_ref[...] * scale).astype(BF16)
        v_ref[...] = _mm(h, w_ref[:, 3 * CONV_W + 2 * ATTN_W:IN_COLS]).astype(BF16)

    def blk(c):
        return pl.BlockSpec((tm, c), lambda i: (i, 0))

    return pl.pallas_call(
        body, name="inproj_fwd", grid=(t // tm,),
        in_specs=[blk(D_MODEL), _full((1, D_MODEL)), _full((D_MODEL, IN_COLS)), _full((3, CONV_W)),
                  _full((1, CONV_W)), _full((1, ATTN_W)), _full((1, ATTN_W)), _full((256, 256))],
        out_specs=[blk(3 * CONV_W), blk(2 * ATTN_W), blk(CONV_W), blk(ATTN_W), blk(ATTN_W), blk(ATTN_W)],
        out_shape=[jax.ShapeDtypeStruct((t, 3 * CONV_W), F32), jax.ShapeDtypeStruct((t, 2 * ATTN_W), F32),
                   jax.ShapeDtypeStruct((t, CONV_W), F32), jax.ShapeDtypeStruct((t, ATTN_W), BF16),
                   jax.ShapeDtypeStruct((t, ATTN_W), BF16), jax.ShapeDtypeStruct((t, ATTN_W), BF16)],
        scratch_shapes=[pltpu.VMEM((8, CONV_W), F32)],
        compiler_params=_cparams("arbitrary"),
    )(x, g_mix, w_in, conv_w, conv_b, qg, kg, bd)


def _attn_masks(dil):
    qi = lax.broadcasted_iota(jnp.int32, (QK_BLOCK, QK_BLOCK), 0)
    kj = lax.broadcasted_iota(jnp.int32, (QK_BLOCK, QK_BLOCK), 1)
    step_cur = qi - kj
    step_prev = step_cur + QK_BLOCK
    valid_cur = step_cur >= 0
    valid_prev = step_prev <= QK_BLOCK
    dist_cur = (step_cur * dil).astype(F32)
    dist_prev = (step_prev * dil).astype(F32)
    return valid_cur, valid_prev, dist_cur, dist_prev, kj < HEAD_DIM


def _attn_chunk(t, dil):
    length = t // dil
    return min(length, 1024)


def _attn_fwd(q, k, v, slopes, dil):
    t = q.shape[0]
    length = t // dil
    chunk = _attn_chunk(t, dil)
    nch = length // chunk
    nb = chunk // QK_BLOCK
    view = (length, dil * ATTN_W)

    def body(q_ref, k_ref, v_ref, kh_ref, vh_ref, sl_ref, o_ref, l_ref, kbuf, vbuf):
        c = pl.program_id(2)
        kbuf[0:QK_BLOCK, :] = kh_ref[...]
        kbuf[QK_BLOCK:, :] = k_ref[...]
        vbuf[0:QK_BLOCK, :] = vh_ref[...]
        vbuf[QK_BLOCK:, :] = v_ref[...]
        valid_cur, valid_prev, dist_cur, dist_prev, head0 = _attn_masks(dil)

        def blk(j, carry):
            off = pl.multiple_of(j * QK_BLOCK, QK_BLOCK)
            qb = q_ref[pl.ds(off, QK_BLOCK), :]
            kp = kbuf[pl.ds(off, QK_BLOCK), :]
            kc = kbuf[pl.ds(off + QK_BLOCK, QK_BLOCK), :]
            vp = vbuf[pl.ds(off, QK_BLOCK), :]
            vc = vbuf[pl.ds(off + QK_BLOCK, QK_BLOCK), :]
            has_prev = jnp.logical_or(c > 0, j > 0)
            vprev = jnp.logical_and(valid_prev, has_prev)
            outs, lses = [], []
            for hh in range(2):
                sl = sl_ref[0, hh:hh + 1, :]
                hm = head0 if hh == 0 else jnp.logical_not(head0)
                qm = jnp.where(hm, qb, jnp.zeros_like(qb))
                sc = jnp.where(valid_cur, _mm_nt(qm, kc) - sl * dist_cur, -jnp.inf)
                sp = jnp.where(vprev, _mm_nt(qm, kp) - sl * dist_prev, -jnp.inf)
                m = jnp.maximum(jnp.max(sc, axis=-1, keepdims=True), jnp.max(sp, axis=-1, keepdims=True))
                ec = jnp.exp(sc - m)
                ep = jnp.exp(sp - m)
                den = jnp.sum(ec, axis=-1, keepdims=True) + jnp.sum(ep, axis=-1, keepdims=True)
                outs.append((_mm(ec.astype(BF16), vc) + _mm(ep.astype(BF16), vp)) / den)
                lses.append(m + jnp.log(den))
            o_ref[pl.ds(off, QK_BLOCK), :] = jnp.where(head0, outs[0], outs[1])
            l_ref[pl.ds(off, QK_BLOCK), :] = jnp.where(head0, lses[0], lses[1])
            return carry

        lax.fori_loop(0, nb, blk, 0)

    def cmap(p, r, c):
        return (c, r * 4 + p)

    def hmap(p, r, c):
        return (jnp.maximum(c * nb - 1, 0), r * 4 + p)

    main = pl.BlockSpec((chunk, QK_BLOCK), cmap)
    halo = pl.BlockSpec((QK_BLOCK, QK_BLOCK), hmap)
    o, lse = pl.pallas_call(
        body, name=f"attn_fwd_d{dil}", grid=(4, dil, nch),
        in_specs=[main, main, main, halo, halo, pl.BlockSpec((1, 2, QK_BLOCK), lambda p, r, c: (p, 0, 0))],
        out_specs=[main, main],
        out_shape=[jax.ShapeDtypeStruct(view, F32), jax.ShapeDtypeStruct(view, F32)],
        scratch_shapes=[pltpu.VMEM((chunk + QK_BLOCK, QK_BLOCK), BF16), pltpu.VMEM((chunk + QK_BLOCK, QK_BLOCK), BF16)],
        compiler_params=_cparams("arbitrary", "arbitrary", "arbitrary"),
    )(q.reshape(view), k.reshape(view), v.reshape(view), k.reshape(view), v.reshape(view), slopes)
    return o.reshape(t, ATTN_W), lse.reshape(t, ATTN_W)


def _outproj_fwd(os_, ls_, yc, x, goc, goa, w_out, tm):
    t = x.shape[0]

    def body(o1, o2, o3, l1, l2, l3, yc_ref, x_ref, goc_ref, goa_ref, w_ref, x1_ref, ya_ref, lse_ref):
        la, lb, lc = l1[...], l2[...], l3[...]
        mx = jnp.maximum(jnp.maximum(la, lb), lc)
        wa, wb, wc = jnp.exp(la - mx), jnp.exp(lb - mx), jnp.exp(lc - mx)
        sw = wa + wb + wc
        ya = (wa * o1[...] + wb * o2[...] + wc * o3[...]) / sw
        ya_ref[...] = ya
        lse_ref[...] = mx + jnp.log(sw)
        _, ychat = _rms_stats(yc_ref[...])
        _, yahat = _rms_stats(ya)
        acc = _mm((ychat * goc_ref[...]).astype(BF16), w_ref[0:CONV_W, :])
        acc += _mm((yahat * goa_ref[...]).astype(BF16), w_ref[CONV_W:, :])
        x1_ref[...] = x_ref[...] + acc

    def blk(c):
        return pl.BlockSpec((tm, c), lambda i: (i, 0))

    return pl.pallas_call(
        body, name="outproj_fwd", grid=(t // tm,),
        in_specs=[blk(ATTN_W)] * 6 + [blk(CONV_W), blk(D_MODEL), _full((1, CONV_W)), _full((1, ATTN_W)),
                                      _full((D_MODEL, D_MODEL))],
        out_specs=[blk(D_MODEL), blk(ATTN_W), blk(ATTN_W)],
        out_shape=[jax.ShapeDtypeStruct((t, D_MODEL), F32), jax.ShapeDtypeStruct((t, ATTN_W), F32),
                   jax.ShapeDtypeStruct((t, ATTN_W), F32)],
        compiler_params=_cparams("parallel"),
    )(*os_, *ls_, yc, x, goc, goa, w_out)


def _ffn_fwd(x1, g_ffn, w_gate, w_up, w_down, fcw, fcb, tm):
    t = x1.shape[0]

    def body(x_ref, g_ref, wg_ref, wu_ref, wd_ref, cw_ref, cb_ref, gp_ref, up_ref, x2_ref, carry_ref):
        @pl.when(pl.program_id(0) == 0)
        def _():
            carry_ref[...] = jnp.zeros_like(carry_ref)

        xv = x_ref[...]
        _, xhat = _rms_stats(xv)
        h = (xhat * g_ref[...]).astype(BF16)
        gp = _mm(h, wg_ref[...])
        gp_ref[...] = gp
        gate, _, _ = _conv_fwd(gp, carry_ref[7:8, :], carry_ref[6:7, :], cw_ref, cb_ref)
        carry_ref[...] = gp[tm - 8:tm, :]
        up = _mm(h, wu_ref[...])
        up_ref[...] = up
        a = (gate * _sigmoid(gate) * up).astype(BF16)
        x2_ref[...] = xv + _mm(a, wd_ref[...])

    def blk(c):
        return pl.BlockSpec((tm, c), lambda i: (i, 0))

    return pl.pallas_call(
        body, name="ffn_fwd", grid=(t // tm,),
        in_specs=[blk(D_MODEL), _full((1, D_MODEL)), _full((D_MODEL, D_FF)), _full((D_MODEL, D_FF)),
                  _full((D_FF, D_MODEL)), _full((3, D_FF)), _full((1, D_FF))],
        out_specs=[blk(D_FF), blk(D_FF), blk(D_MODEL)],
        out_shape=[jax.ShapeDtypeStruct((t, D_FF), F32), jax.ShapeDtypeStruct((t, D_FF), F32),
                   jax.ShapeDtypeStruct((t, D_MODEL), F32)],
        scratch_shapes=[pltpu.VMEM((8, D_FF), F32)],
        compiler_params=_cparams("arbitrary"),
    )(x1, g_ffn, w_gate, w_up, w_down, fcw, fcb)


def _ple_fwd_bwd(x2, p, target, g_ple, w_pg, w_pp, tm):
    t = x2.shape[0]

    def body(x_ref, p_ref, t_ref, g_ref, wg_ref, wp_ref, dx_ref, loss_ref, dwg_ref, dwp_ref, dg_ref):
        @pl.when(pl.program_id(0) == 0)
        def _():
            loss_ref[...] = jnp.zeros_like(loss_ref)
            dwg_ref[...] = jnp.zeros_like(dwg_ref)
            dwp_ref[...] = jnp.zeros_like(dwp_ref)
            dg_ref[...] = jnp.zeros_like(dg_ref)

        xv = x_ref[...]
        r, xhat = _rms_stats(xv)
        g = g_ref[...]
        h = (xhat * g).astype(BF16)
        pg = _sigmoid(_mm(h, wg_ref[...]))
        pb = p_ref[...].astype(BF16)
        pp = _mm(pb, wp_ref[...])
        err = xv + pg * pp - t_ref[...]
        loss_ref[...] += 0.5 * jnp.sum(jnp.mean(err * err, axis=-1, keepdims=True))
        dx3 = err * (1.0 / D_MODEL)
        d_pp = (dx3 * pg).astype(BF16)
        d_pre = (dx3 * pp * pg * (1.0 - pg)).astype(BF16)
        dwp_ref[...] += _mm_tn(pb, d_pp)
        dwg_ref[...] += _mm_tn(h, d_pre)
        dh = _mm_nt(d_pre, wg_ref[...])
        dg_ref[...] += jnp.sum(dh * xhat, axis=0, keepdims=True)
        dx_ref[...] = dx3 + _rms_bwd(dh, xhat, r, g)

    def blk(c):
        return pl.BlockSpec((tm, c), lambda i: (i, 0))

    return pl.pallas_call(
        body, name="ple_fwd_bwd", grid=(t // tm,),
        in_specs=[blk(D_MODEL), blk(PLE_DIM), blk(D_MODEL), _full((1, D_MODEL)), _full((D_MODEL, D_MODEL)),
                  _full((PLE_DIM, D_MODEL))],
        out_specs=[blk(D_MODEL), _full((8, 128)), _full((D_MODEL, D_MODEL)), _full((PLE_DIM, D_MODEL)),
                   _full((1, D_MODEL))],
        out_shape=[jax.ShapeDtypeStruct((t, D_MODEL), F32), jax.ShapeDtypeStruct((8, 128), F32),
                   jax.ShapeDtypeStruct((D_MODEL, D_MODEL), F32), jax.ShapeDtypeStruct((PLE_DIM, D_MODEL), F32),
                   jax.ShapeDtypeStruct((1, D_MODEL), F32)],
        compiler_params=_cparams("arbitrary"),
    )(x2, p, target, g_ple, w_pg, w_pp)


def _ffn_bwd(dx2, x1, g_ffn, gp, up, w_gate, w_up, w_down, fcw, fcb, tm):
    t = x1.shape[0]
    nblk = t // tm
    fc = D_FF // FF_CHUNKS

    def body(dx_ref, x_ref, g_ref, gp_ref, gph_ref, up_ref, wg_ref, wu_ref, wd_ref, cw_ref, cb_ref,
             dh_ref, dwd_ref, dwu_ref, dwg_ref, dcw_ref, dcb_ref, carry_ref):
        i = pl.program_id(1)

        @pl.when(i == 0)
        def _():
            carry_ref[...] = jnp.zeros_like(carry_ref)
            dwd_ref[...] = jnp.zeros_like(dwd_ref)
            dwu_ref[...] = jnp.zeros_like(dwu_ref)
            dwg_ref[...] = jnp.zeros_like(dwg_ref)
            dcw_ref[...] = jnp.zeros_like(dcw_ref)
            dcb_ref[...] = jnp.zeros_like(dcb_ref)

        keep = (i < nblk - 1).astype(F32)
        dxb = dx_ref[...].astype(BF16)
        _, xhat = _rms_stats(x_ref[...])
        h = (xhat * g_ref[...]).astype(BF16)
        gp_v = gp_ref[...]
        gate, gp1, gp2 = _conv_fwd(gp_v, gph_ref[7:8, :] * keep, gph_ref[6:7, :] * keep, cw_ref, cb_ref)
        s = _sigmoid(gate)
        silu = gate * s
        up_v = up_ref[...]
        da = _mm_nt(dxb, wd_ref[...])
        dwd_ref[...] += _mm_tn((silu * up_v).astype(BF16), dxb)
        d_up = (da * silu).astype(BF16)
        d_gate = da * up_v * (s * (1.0 + gate * (1.0 - s)))
        dwu_ref[...] += _mm_tn(h, d_up)
        d_gp = _conv_bwd_input(d_gate, carry_ref[0:1, :], carry_ref[1:2, :], cw_ref).astype(BF16)
        carry_ref[...] = d_gate[0:8, :]
        dcw_ref[0:1, :] += jnp.sum(d_gate * gp2, axis=0, keepdims=True)
        dcw_ref[1:2, :] += jnp.sum(d_gate * gp1, axis=0, keepdims=True)
        dcw_ref[2:3, :] += jnp.sum(d_gate * gp_v, axis=0, keepdims=True)
        dcb_ref[...] += jnp.sum(d_gate, axis=0, keepdims=True)
        dwg_ref[...] += _mm_tn(h, d_gp)
        dh_ref[...] = _mm_nt(d_gp, wg_ref[...]) + _mm_nt(d_up, wu_ref[...])

    def rev(i):
        return nblk - 1 - i

    one = pl.Buffered(1)
    in_specs = [
        pl.BlockSpec((tm, D_MODEL), lambda j, i: (rev(i), 0)),
        pl.BlockSpec((tm, D_MODEL), lambda j, i: (rev(i), 0)),
        _full((1, D_MODEL)),
        pl.BlockSpec((tm, fc), lambda j, i: (rev(i), j)),
        pl.BlockSpec((8, fc), lambda j, i: (jnp.maximum(rev(i) * (tm // 8) - 1, 0), j)),
        pl.BlockSpec((tm, fc), lambda j, i: (rev(i), j)),
        pl.BlockSpec((D_MODEL, fc), lambda j, i: (0, j), pipeline_mode=one),
        pl.BlockSpec((D_MODEL, fc), lambda j, i: (0, j), pipeline_mode=one),
        pl.BlockSpec((fc, D_MODEL), lambda j, i: (j, 0), pipeline_mode=one),
        pl.BlockSpec((3, fc), lambda j, i: (0, j)),
        pl.BlockSpec((1, fc), lambda j, i: (0, j)),
    ]
    out_specs = [
        pl.BlockSpec((None, tm, D_MODEL), lambda j, i: (j, rev(i), 0)),
        pl.BlockSpec((fc, D_MODEL), lambda j, i: (j, 0), pipeline_mode=one),
        pl.BlockSpec((D_MODEL, fc), lambda j, i: (0, j), pipeline_mode=one),
        pl.BlockSpec((D_MODEL, fc), lambda j, i: (0, j), pipeline_mode=one),
        pl.BlockSpec((3, fc), lambda j, i: (0, j)),
        pl.BlockSpec((1, fc), lambda j, i: (0, j)),
    ]
    return pl.pallas_call(
        body, name="ffn_bwd", grid=(FF_CHUNKS, nblk), in_specs=in_specs, out_specs=out_specs,
        out_shape=[jax.ShapeDtypeStruct((FF_CHUNKS, t, D_MODEL), F32), jax.ShapeDtypeStruct((D_FF, D_MODEL), F32),
                   jax.ShapeDtypeStruct((D_MODEL, D_FF), F32), jax.ShapeDtypeStruct((D_MODEL, D_FF), F32),
                   jax.ShapeDtypeStruct((3, D_FF), F32), jax.ShapeDtypeStruct((1, D_FF), F32)],
        scratch_shapes=[pltpu.VMEM((8, fc), F32)],
        compiler_params=_cparams("arbitrary", "arbitrary"),
    )(dx2, x1, g_ffn, gp, gp, up, w_gate, w_up, w_down, fcw, fcb)


def _outproj_bwd(dh2, dx2, x1, g_ffn, w_out, yc, ya, goc, goa, zconv, conv_w, conv_b, bd, tm):
    t = x1.shape[0]
    nblk = t // tm

    def body(dh_ref, dx2_ref, x1_ref, g_ref, w_ref, yc_ref, ya_ref, goc_ref, goa_ref, zc_ref, zch_ref, cw_ref, cb_ref,
             bd_ref, dx1_ref, dya_ref, dd_ref, dzc_ref, dw_ref, dg_ref, dgoc_ref, dgoa_ref, dcw_ref, dcb_ref,
             carry_ref):
        i = pl.program_id(0)

        @pl.when(i == 0)
        def _():
            carry_ref[...] = jnp.zeros_like(carry_ref)
            for ref in (dw_ref, dg_ref, dgoc_ref, dgoa_ref, dcw_ref, dcb_ref):
                ref[...] = jnp.zeros_like(ref)

        keep = (i < nblk - 1).astype(F32)
        dh2_v = dh_ref[0]
        for j in range(1, FF_CHUNKS):
            dh2_v = dh2_v + dh_ref[j]
        r, xhat = _rms_stats(x1_ref[...])
        dg_ref[...] += jnp.sum(dh2_v * xhat, axis=0, keepdims=True)
        dx1 = dx2_ref[...] + _rms_bwd(dh2_v, xhat, r, g_ref[...])
        dx1_ref[...] = dx1
        dx1b = dx1.astype(BF16)
        dy = _mm_nt(dx1b, w_ref[...])

        yc_v = yc_ref[...]
        rc, ychat = _rms_stats(yc_v)
        dw_ref[0:CONV_W, :] += _mm_tn((ychat * goc_ref[...]).astype(BF16), dx1b)
        dyc = dy[:, 0:CONV_W]
        dgoc_ref[...] += jnp.sum(dyc * ychat, axis=0, keepdims=True)
        d_yc = _rms_bwd(dyc, ychat, rc, goc_ref[...])

        ya_v = ya_ref[...]
        ra, yahat = _rms_stats(ya_v)
        dw_ref[CONV_W:, :] += _mm_tn((yahat * goa_ref[...]).astype(BF16), dx1b)
        dya = dy[:, CONV_W:]
        dgoa_ref[...] += jnp.sum(dya * yahat, axis=0, keepdims=True)
        d_ya = _rms_bwd(dya, yahat, ra, goa_ref[...])
        dya_ref[...] = d_ya
        dd_ref[...] = _seg_sum64(d_ya * ya_v, bd_ref)

        zb = zc_ref[:, 0:CONV_W]
        zc = zc_ref[:, CONV_W:2 * CONV_W]
        zx = zc_ref[:, 2 * CONV_W:3 * CONV_W]
        u = zc * zx
        uh = zch_ref[:, CONV_W:2 * CONV_W] * zch_ref[:, 2 * CONV_W:3 * CONV_W] * keep
        cv, u1, u2 = _conv_fwd(u, uh[7:8, :], uh[6:7, :], cw_ref, cb_ref)
        d_cv = d_yc * zb
        d_u = _conv_bwd_input(d_cv, carry_ref[0:1, :], carry_ref[1:2, :], cw_ref)
        carry_ref[...] = d_cv[0:8, :]
        dcw_ref[0:1, :] += jnp.sum(d_cv * u2, axis=0, keepdims=True)
        dcw_ref[1:2, :] += jnp.sum(d_cv * u1, axis=0, keepdims=True)
        dcw_ref[2:3, :] += jnp.sum(d_cv * u, axis=0, keepdims=True)
        dcb_ref[...] += jnp.sum(d_cv, axis=0, keepdims=True)
        dzc_ref[:, 0:CONV_W] = d_yc * cv
        dzc_ref[:, CONV_W:2 * CONV_W] = d_u * zx
        dzc_ref[:, 2 * CONV_W:3 * CONV_W] = d_u * zc

    def rev(i):
        return nblk - 1 - i

    def blk(c):
        return pl.BlockSpec((tm, c), lambda i: (rev(i), 0))

    in_specs = [
        pl.BlockSpec((FF_CHUNKS, tm, D_MODEL), lambda i: (0, rev(i), 0)),
        blk(D_MODEL), blk(D_MODEL), _full((1, D_MODEL)), _full((D_MODEL, D_MODEL)),
        blk(CONV_W), blk(ATTN_W), _full((1, CONV_W)), _full((1, ATTN_W)),
        blk(3 * CONV_W),
        pl.BlockSpec((8, 3 * CONV_W), lambda i: (jnp.maximum(rev(i) * (tm // 8) - 1, 0), 0)),
        _full((3, CONV_W)), _full((1, CONV_W)), _full((256, 256)),
    ]
    out_specs = [blk(D_MODEL), blk(ATTN_W), blk(ATTN_W), blk(3 * CONV_W), _full((D_MODEL, D_MODEL)),
                 _full((1, D_MODEL)), _full((1, CONV_W)), _full((1, ATTN_W)), _full((3, CONV_W)), _full((1, CONV_W))]
    return pl.pallas_call(
        body, name="outproj_bwd", grid=(nblk,), in_specs=in_specs, out_specs=out_specs,
        out_shape=[jax.ShapeDtypeStruct((t, D_MODEL), F32), jax.ShapeDtypeStruct((t, ATTN_W), F32),
                   jax.ShapeDtypeStruct((t, ATTN_W), F32), jax.ShapeDtypeStruct((t, 3 * CONV_W), F32),
                   jax.ShapeDtypeStruct((D_MODEL, D_MODEL), F32), jax.ShapeDtypeStruct((1, D_MODEL), F32),
                   jax.ShapeDtypeStruct((1, CONV_W), F32), jax.ShapeDtypeStruct((1, ATTN_W), F32),
                   jax.ShapeDtypeStruct((3, CONV_W), F32), jax.ShapeDtypeStruct((1, CONV_W), F32)],
        scratch_shapes=[pltpu.VMEM((8, CONV_W), F32)],
        compiler_params=_cparams("arbitrary"),
    )(dh2, dx2, x1, g_ffn, w_out, yc, ya, goc, goa, zconv, zconv, conv_w, conv_b, bd)


def _attn_bwd(q, k, v, dya, lse, dd, slopes, dil):
    t = q.shape[0]
    length = t // dil
    chunk = _attn_chunk(t, dil)
    nch = length // chunk
    nb = chunk // QK_BLOCK
    nblocks = length // QK_BLOCK
    view = (length, dil * ATTN_W)
    ext = chunk + QK_BLOCK

    def body(q_ref, dy_ref, l_ref, d_ref, k_ref, v_ref, qn_ref, dyn_ref, ln_ref, dn_ref, kh_ref, vh_ref, sl_ref,
             dq_ref, dk_ref, dv_ref, qbuf, dybuf, lbuf, dbuf, kbuf, vbuf, dkacc, dvacc):
        c = pl.program_id(2)
        qbuf[0:chunk, :] = q_ref[...]
        qbuf[chunk:, :] = qn_ref[...]
        dybuf[0:chunk, :] = dy_ref[...].astype(BF16)
        dybuf[chunk:, :] = dyn_ref[...].astype(BF16)
        lbuf[0:chunk, :] = l_ref[...]
        lbuf[chunk:, :] = ln_ref[...]
        dbuf[0:chunk, :] = d_ref[...]
        dbuf[chunk:, :] = dn_ref[...]
        kbuf[0:QK_BLOCK, :] = kh_ref[...]
        kbuf[QK_BLOCK:, :] = k_ref[...]
        vbuf[0:QK_BLOCK, :] = vh_ref[...]
        vbuf[QK_BLOCK:, :] = v_ref[...]
        valid_cur, valid_prev, dist_cur, dist_prev, head0 = _attn_masks(dil)

        def pair(qb, dyb, lv, dv_, kb, vb, valid, dist):
            dq = jnp.zeros((QK_BLOCK, QK_BLOCK), F32)
            dk = jnp.zeros((QK_BLOCK, QK_BLOCK), F32)
            dvv = jnp.zeros((QK_BLOCK, QK_BLOCK), F32)
            for hh in range(2):
                sl = sl_ref[0, hh:hh + 1, :]
                hm = head0 if hh == 0 else jnp.logical_not(head0)
                col = hh * HEAD_DIM
                qm = jnp.where(hm, qb, jnp.zeros_like(qb))
                dym = jnp.where(hm, dyb, jnp.zeros_like(dyb))
                s = jnp.where(valid, _mm_nt(qm, kb) - sl * dist, -jnp.inf)
                prob = jnp.exp(s - lv[:, col:col + 1])
                ds = (prob * (_mm_nt(dym, vb) - dv_[:, col:col + 1])).astype(BF16)
                dvv += _mm_tn(prob.astype(BF16), dym)
                dk += _mm_tn(ds, qm)
                dq += jnp.where(hm, _mm(ds, kb), 0.0)
            return dq, dk, dvv

        def blk(j, carry):
            off = pl.multiple_of(j * QK_BLOCK, QK_BLOCK)
            nxt = pl.multiple_of(off + QK_BLOCK, QK_BLOCK)
            qb = qbuf[pl.ds(off, QK_BLOCK), :]
            dyb = dybuf[pl.ds(off, QK_BLOCK), :]
            lv = lbuf[pl.ds(off, QK_BLOCK), :]
            dv_ = dbuf[pl.ds(off, QK_BLOCK), :]
            dq_c, dk_c, dv_c = pair(qb, dyb, lv, dv_, kbuf[pl.ds(nxt, QK_BLOCK), :], vbuf[pl.ds(nxt, QK_BLOCK), :],
                                    valid_cur, dist_cur)
            dkacc[pl.ds(nxt, QK_BLOCK), :] = dk_c
            dvacc[pl.ds(nxt, QK_BLOCK), :] = dv_c
            has_prev = jnp.logical_or(c > 0, j > 0)
            dq_p, dk_p, dv_p = pair(qb, dyb, lv, dv_, kbuf[pl.ds(off, QK_BLOCK), :], vbuf[pl.ds(off, QK_BLOCK), :],
                                    jnp.logical_and(valid_prev, has_prev), dist_prev)

            @pl.when(j > 0)
            def _():
                dkacc[pl.ds(off, QK_BLOCK), :] += dk_p
                dvacc[pl.ds(off, QK_BLOCK), :] += dv_p

            dq_ref[pl.ds(off, QK_BLOCK), :] = (dq_c + dq_p) * ATTN_SCALE
            return carry

        lax.fori_loop(0, nb, blk, 0)

        @pl.when(c < nch - 1)
        def _():
            _, dk_p, dv_p = pair(qbuf[chunk:, :], dybuf[chunk:, :], lbuf[chunk:, :], dbuf[chunk:, :],
                                 kbuf[chunk:, :], vbuf[chunk:, :], valid_prev, dist_prev)
            dkacc[chunk:, :] += dk_p
            dvacc[chunk:, :] += dv_p

        dk_ref[...] = dkacc[QK_BLOCK:, :]
        dv_ref[...] = dvacc[QK_BLOCK:, :]

    def cmap(p, r, c):
        return (c, r * 4 + p)

    def before(p, r, c):
        return (jnp.maximum(c * nb - 1, 0), r * 4 + p)

    def after(p, r, c):
        return (jnp.minimum((c + 1) * nb, nblocks - 1), r * 4 + p)

    main = pl.BlockSpec((chunk, QK_BLOCK), cmap)
    hb = pl.BlockSpec((QK_BLOCK, QK_BLOCK), before)
    ha = pl.BlockSpec((QK_BLOCK, QK_BLOCK), after)
    qv, kv, vv = q.reshape(view), k.reshape(view), v.reshape(view)
    dyv, lv, ddv = dya.reshape(view), lse.reshape(view), dd.reshape(view)
    outs = pl.pallas_call(
        body, name=f"attn_bwd_d{dil}", grid=(4, dil, nch),
        in_specs=[main] * 6 + [ha] * 4 + [hb] * 2 + [pl.BlockSpec((1, 2, QK_BLOCK), lambda p, r, c: (p, 0, 0))],
        out_specs=[main] * 3,
        out_shape=[jax.ShapeDtypeStruct(view, F32)] * 3,
        scratch_shapes=[pltpu.VMEM((ext, QK_BLOCK), BF16), pltpu.VMEM((ext, QK_BLOCK), BF16),
                        pltpu.VMEM((ext, QK_BLOCK), F32), pltpu.VMEM((ext, QK_BLOCK), F32),
                        pltpu.VMEM((ext, QK_BLOCK), BF16), pltpu.VMEM((ext, QK_BLOCK), BF16),
                        pltpu.VMEM((ext, QK_BLOCK), F32), pltpu.VMEM((ext, QK_BLOCK), F32)],
        compiler_params=_cparams("arbitrary", "arbitrary", "arbitrary"),
    )(qv, dyv, lv, ddv, kv, vv, qv, dyv, lv, ddv, kv, vv, slopes)
    return [o.reshape(t, ATTN_W) for o in outs]


def _inproj_bwd(dqs, dks, dvs, dzconv, zqk, x, dx1, g_mix, w_in, qg, kg, bd, tm):
    t = x.shape[0]

    def body(dq1, dq2, dq3, dk1, dk2, dk3, dv1, dv2, dv3, dzc_ref, zqk_ref, x_ref, dx1_ref, g_ref, w_ref, qg_ref,
             kg_ref, bd_ref, dx_ref, dw_ref, dg_ref, dqg_ref, dkg_ref):
        @pl.when(pl.program_id(0) == 0)
        def _():
            for ref in (dw_ref, dg_ref, dqg_ref, dkg_ref):
                ref[...] = jnp.zeros_like(ref)

        parts = [dzc_ref[...].astype(BF16)]
        for j, (da, db, dc, gain_ref, dgain_ref) in enumerate(((dq1, dq2, dq3, qg_ref, dqg_ref),
                                                                (dk1, dk2, dk3, kg_ref, dkg_ref))):
            dn = da[...] + db[...] + dc[...]
            z = zqk_ref[:, j * ATTN_W:(j + 1) * ATTN_W]
            r = lax.rsqrt(_seg_sum64(z * z, bd_ref) * (1.0 / HEAD_DIM) + EPS)
            zhat = z * r
            dgain_ref[...] += jnp.sum(dn * zhat, axis=0, keepdims=True)
            gd = dn * gain_ref[...]
            parts.append((r * (gd - zhat * (_seg_sum64(gd * zhat, bd_ref) * (1.0 / HEAD_DIM)))).astype(BF16))
        parts.append((dv1[...] + dv2[...] + dv3[...]).astype(BF16))
        dz = jnp.concatenate(parts, axis=1)

        r, xhat = _rms_stats(x_ref[...])
        g = g_ref[...]
        dw_ref[...] += _mm_tn((xhat * g).astype(BF16), dz)
        dh = _mm_nt(dz, w_ref[...])
        dg_ref[...] += jnp.sum(dh * xhat, axis=0, keepdims=True)
        dx_ref[...] = dx1_ref[...] + _rms_bwd(dh, xhat, r, g)

    def blk(c):
        return pl.BlockSpec((tm, c), lambda i: (i, 0))

    return pl.pallas_call(
        body, name="inproj_bwd", grid=(t // tm,),
        in_specs=[blk(ATTN_W)] * 9 + [blk(3 * CONV_W), blk(2 * ATTN_W), blk(D_MODEL), blk(D_MODEL), _full((1, D_MODEL)),
                                      _full((D_MODEL, IN_COLS)), _full((1, ATTN_W)), _full((1, ATTN_W)),
                                      _full((256, 256))],
        out_specs=[blk(D_MODEL), _full((D_MODEL, IN_COLS)), _full((1, D_MODEL)), _full((1, ATTN_W)),
                   _full((1, ATTN_W))],
        out_shape=[jax.ShapeDtypeStruct((t, D_MODEL), F32), jax.ShapeDtypeStruct((D_MODEL, IN_COLS), F32),
                   jax.ShapeDtypeStruct((1, D_MODEL), F32), jax.ShapeDtypeStruct((1, ATTN_W), F32),
                   jax.ShapeDtypeStruct((1, ATTN_W), F32)],
        compiler_params=_cparams("arbitrary"),
    )(*dqs, *dks, *dvs, dzconv, zqk, x, dx1, g_mix, w_in, qg, kg, bd)


def _local_step(x, p, target, w, tms):
    bd = jnp.kron(jnp.eye(4, dtype=F32), jnp.ones((HEAD_DIM, HEAD_DIM), F32)).astype(BF16)
    qg = jnp.tile(w["q_norm_g"], (1, 8))
    kg = jnp.tile(w["k_norm_g"], (1, 8))
    slopes = jnp.exp2(-jnp.arange(1, 9, dtype=F32))
    slopes = jnp.broadcast_to(slopes.reshape(4, 2, 1), (4, 2, QK_BLOCK))

    zconv, zqk, yc, q, k, v = _inproj_fwd(x, w["g_mix"], w["w_in"], w["conv_w"], w["conv_b"], qg, kg, bd, tms[0])
    branches = [_attn_fwd(q, k, v, slopes, d) for d in DILATIONS]
    x1, ya, lse = _outproj_fwd([b[0] for b in branches], [b[1] for b in branches], yc, x, w["g_out_conv"],
                               w["g_out_attn"], w["w_out"], tms[0])
    gp, up, x2 = _ffn_fwd(x1, w["g_ffn"], w["w_gate"], w["w_up"], w["w_down"], w["ffn_conv_w"], w["ffn_conv_b"], tms[1])
    dx2, loss, dw_pg, dw_pp, dg_ple = _ple_fwd_bwd(x2, p, target, w["g_ple"], w["w_ple_gate"], w["w_ple_proj"], tms[0])
    dh2, dw_down, dw_up, dw_gate, dfcw, dfcb = _ffn_bwd(dx2, x1, w["g_ffn"], gp, up, w["w_gate"], w["w_up"],
                                                        w["w_down"], w["ffn_conv_w"], w["ffn_conv_b"], tms[1])
    dx1, dya, dd, dzconv, dw_out, dg_ffn, dgoc, dgoa, dcw, dcb = _outproj_bwd(
        dh2, dx2, x1, w["g_ffn"], w["w_out"], yc, ya, w["g_out_conv"], w["g_out_attn"], zconv, w["conv_w"],
        w["conv_b"], bd, tms[1])
    grads_qkv = [_attn_bwd(q, k, v, dya, lse, dd, slopes, d) for d in DILATIONS]
    dx, dw_in, dg_mix, dqg, dkg = _inproj_bwd([g[0] for g in grads_qkv], [g[1] for g in grads_qkv],
                                              [g[2] for g in grads_qkv], dzconv, zqk, x, dx1, w["g_mix"], w["w_in"],
                                              qg, kg, bd, tms[1])
    grads = {
        "g_mix": dg_mix, "w_in": dw_in, "conv_w": dcw, "conv_b": dcb,
        "q_norm_g": dqg.reshape(8, HEAD_DIM).sum(0, keepdims=True),
        "k_norm_g": dkg.reshape(8, HEAD_DIM).sum(0, keepdims=True),
        "g_out_conv": dgoc, "g_out_attn": dgoa, "w_out": dw_out, "g_ffn": dg_ffn, "w_gate": dw_gate, "w_up": dw_up,
        "ffn_conv_w": dfcw, "ffn_conv_b": dfcb, "w_down": dw_down, "g_ple": dg_ple, "w_ple_gate": dw_pg,
        "w_ple_proj": dw_pp,
    }
    return loss[0, 0], dx, grads


ANY = pl.BlockSpec(memory_space=pl.ANY)
MESH = pl.DeviceIdType.MESH


def _all_gather(shards, name):
    n = len(shards)

    def body(*refs):
        ins, outs = refs[:n], refs[n:2 * n]
        send_sems, recv_sems, local_sems = refs[2 * n:]
        x, y, c = lax.axis_index("x"), lax.axis_index("y"), lax.axis_index("c")
        me, sibling = (x, y, c), (x, y, 1 - c)
        chips = [(1 - x, y), (x, 1 - y), (1 - x, 1 - y)]

        def slot(dev):
            return 4 * dev[0] + 2 * dev[1] + dev[2]

        def copy(b, k, block, to, src=None):
            dst = outs[b].at[slot(block)]
            return pltpu.make_async_remote_copy(
                src_ref=dst if src is None else src, dst_ref=dst, send_sem=send_sems.at[b, k],
                recv_sem=recv_sems.at[b, k], device_id=to, device_id_type=MESH)

        mine = [pltpu.make_async_copy(ins[b], outs[b].at[slot(me)], local_sems.at[b]) for b in range(n)]
        first, passed = [], []
        for b in range(n):
            mine[b].start()
            first.append(copy(b, 0, me, sibling, src=ins[b]))
            first += [copy(b, 1 + j, me, (*chip, c), src=ins[b]) for j, chip in enumerate(chips)]
        for cp in first:
            cp.start()
        for j, chip in enumerate(chips):
            for b in range(n):
                copy(b, 1 + j, (*chip, c), me).wait_recv()
                fwd = copy(b, 4 + j, (*chip, c), sibling)
                fwd.start()
                passed.append(fwd)
        for b in range(n):
            copy(b, 0, sibling, me).wait_recv()
            for j, chip in enumerate(chips):
                copy(b, 4 + j, (*chip, 1 - c), me).wait_recv()
        for cp in first + passed:
            cp.wait_send()
        for cp in mine:
            cp.wait()

    return pl.pallas_call(
        body, name=name,
        in_specs=[ANY] * n, out_specs=[ANY] * n,
        out_shape=[jax.ShapeDtypeStruct((N_DEV,) + s.shape, s.dtype) for s in shards],
        scratch_shapes=[pltpu.SemaphoreType.DMA((n, 7)), pltpu.SemaphoreType.DMA((n, 7)),
                        pltpu.SemaphoreType.DMA((n,))],
    )(*shards)


def _sibling_exchange(g):
    def body(g_ref, land_ref, send_sems, recv_sems):
        x, y, c = lax.axis_index("x"), lax.axis_index("y"), lax.axis_index("c")
        copies = [pltpu.make_async_remote_copy(
            src_ref=g_ref.at[k, 1 - c], dst_ref=land_ref.at[k], send_sem=send_sems.at[k], recv_sem=recv_sems.at[k],
            device_id=(x, y, 1 - c), device_id_type=MESH) for k in range(N_CHIP)]
        for cp in copies:
            cp.start()
        for cp in copies:
            cp.wait()

    return pl.pallas_call(
        body, name="rs_sibling_exchange", in_specs=[ANY], out_specs=ANY,
        out_shape=jax.ShapeDtypeStruct((N_CHIP,) + g.shape[2:], g.dtype),
        scratch_shapes=[pltpu.SemaphoreType.DMA((N_CHIP,)), pltpu.SemaphoreType.DMA((N_CHIP,))],
    )(g)


def _pair_sum(g, land, core):
    rows, cols = land.shape[1:]
    tr = 216

    def body(c_ref, g_ref, l_ref, o_ref):
        o_ref[...] = g_ref[...] + l_ref[...]

    return pl.pallas_call(
        body, name="rs_pair_sum",
        grid_spec=pltpu.PrefetchScalarGridSpec(
            num_scalar_prefetch=1, grid=(N_CHIP, rows // tr),
            in_specs=[pl.BlockSpec((None, None, tr, cols), lambda k, i, c_ref: (k, c_ref[0], i, 0)),
                      pl.BlockSpec((None, tr, cols), lambda k, i, c_ref: (k, i, 0))],
            out_specs=pl.BlockSpec((None, tr, cols), lambda k, i, c_ref: (k, i, 0))),
        out_shape=jax.ShapeDtypeStruct(land.shape, land.dtype),
        compiler_params=_cparams("parallel", "parallel"),
    )(core, g, land)


def _chip_exchange(part):
    def body(p_ref, land_ref, send_sems, recv_sems, local_sem):
        x, y, c = lax.axis_index("x"), lax.axis_index("y"), lax.axis_index("c")
        mine = 2 * x + y
        chips = [(1 - x, y), (x, 1 - y), (1 - x, 1 - y)]
        own = pltpu.make_async_copy(p_ref.at[mine], land_ref.at[mine], local_sem)
        own.start()
        copies = [pltpu.make_async_remote_copy(
            src_ref=p_ref.at[2 * cx + cy], dst_ref=land_ref.at[mine], send_sem=send_sems.at[j],
            recv_sem=recv_sems.at[j], device_id=(cx, cy, c), device_id_type=MESH) for j, (cx, cy) in enumerate(chips)]
        for cp in copies:
            cp.start()
        for j, (cx, cy) in enumerate(chips):
            pltpu.make_async_remote_copy(
                src_ref=p_ref.at[mine], dst_ref=land_ref.at[2 * cx + cy], send_sem=send_sems.at[j],
                recv_sem=recv_sems.at[j], device_id=(cx, cy, c), device_id_type=MESH).wait_recv()
        for cp in copies:
            cp.wait_send()
        own.wait()

    return pl.pallas_call(
        body, name="rs_chip_exchange", in_specs=[ANY], out_specs=ANY,
        out_shape=jax.ShapeDtypeStruct(part.shape, part.dtype),
        scratch_shapes=[pltpu.SemaphoreType.DMA((3,)), pltpu.SemaphoreType.DMA((3,)), pltpu.SemaphoreType.DMA],
    )(part)


def _adamw(parts, w, m, v, name):
    k, rows, cols = parts.shape
    tr = 216 if rows % 216 == 0 else rows
    c1 = 1.0 / (1.0 - ADAM_B1 ** ADAM_STEP)
    c2 = 1.0 / (1.0 - ADAM_B2 ** ADAM_STEP)

    def body(p_ref, w_ref, m_ref, v_ref, g_ref, d_ref, nm_ref, nv_ref):
        g = p_ref[0]
        for j in range(1, k):
            g = g + p_ref[j]
        g_ref[...] = g
        nm = ADAM_B1 * m_ref[...] + (1.0 - ADAM_B1) * g
        nv = ADAM_B2 * v_ref[...] + (1.0 - ADAM_B2) * (g * g)
        nm_ref[...] = nm
        nv_ref[...] = nv
        d_ref[...] = -ADAM_LR * ((nm * c1) / (jnp.sqrt(nv * c2) + ADAM_EPS) + ADAM_WD * w_ref[...])

    blk = pl.BlockSpec((tr, cols), lambda i: (i, 0))
    return pl.pallas_call(
        body, name=name, grid=(rows // tr,),
        in_specs=[pl.BlockSpec((k, tr, cols), lambda i: (0, i, 0)), blk, blk, blk],
        out_specs=[blk] * 4, out_shape=[jax.ShapeDtypeStruct((rows, cols), F32)] * 4,
        compiler_params=_cparams("parallel"),
    )(parts, w, m, v)


COL_SHARDED = ("w_in", "w_gate", "w_up", "w_ple_proj")
REPLICATED = (("g_mix", 1024), ("conv_b", 512), ("q_norm_g", 64), ("k_norm_g", 64), ("g_out_conv", 512),
              ("g_out_attn", 512), ("g_ffn", 1024), ("ffn_conv_b", 2816), ("g_ple", 1024))
CONV_SHARDED = (("conv_w", CONV_W), ("ffn_conv_w", D_FF))


def _pack_rows(arrays):
    return jnp.concatenate([a.reshape(-1, 1024) for a in arrays], axis=0)


def _unpack_big(flat, shard_shapes):
    out, r0 = {}, 0
    for name, rows in BIG_ROWS:
        out[name] = flat[r0:r0 + rows].reshape(shard_shapes[name])
        r0 += rows
    return out


def _gathered_to_full(gathered, shard_shapes):
    out, r0 = {}, 0
    for name, rows in BIG_ROWS:
        a = gathered[:, r0:r0 + rows].reshape((N_DEV,) + shard_shapes[name])
        if name in COL_SHARDED:
            a = a.transpose(1, 0, 2)
            out[name] = a.reshape(a.shape[0], -1)
        else:
            out[name] = a.reshape(-1, a.shape[2])
        r0 += rows
    return out


def _full_to_stacked(grads, shard_shapes):
    parts = []
    for name, rows in BIG_ROWS:
        sr, sc = shard_shapes[name]
        a = grads[name]
        if name in COL_SHARDED:
            a = a.reshape(sr, N_DEV, sc).transpose(1, 0, 2)
        else:
            a = a.reshape(N_DEV, sr, sc)
        parts.append(a.reshape(N_DEV, rows, 1024))
    return jnp.concatenate(parts, axis=1)


def _pad_rows(vec, rows):
    return jnp.pad(vec, (0, rows * 1024 - vec.shape[0])).reshape(rows, 1024)


def kernel(x, p, g_mix, w_in, conv_w, conv_b, q_norm_g, k_norm_g, g_out_conv, g_out_attn, w_out, g_ffn, w_gate, w_up, ffn_conv_w, ffn_conv_b, w_down, g_ple, w_ple_gate, w_ple_proj, loss_target, m_g_mix, m_w_in, m_conv_w, m_conv_b, m_q_norm_g, m_k_norm_g, m_g_out_conv, m_g_out_attn, m_w_out, m_g_ffn, m_w_gate, m_w_up, m_ffn_conv_w, m_ffn_conv_b, m_w_down, m_g_ple, m_w_ple_gate, m_w_ple_proj, v_g_mix, v_w_in, v_conv_w, v_conv_b, v_q_norm_g, v_k_norm_g, v_g_out_conv, v_g_out_attn, v_w_out, v_g_ffn, v_w_gate, v_w_up, v_ffn_conv_w, v_ffn_conv_b, v_w_down, v_g_ple, v_w_ple_gate, v_w_ple_proj):
    args = dict(locals())
    names = ["g_mix", "w_in", "conv_w", "conv_b", "q_norm_g", "k_norm_g", "g_out_conv", "g_out_attn", "w_out", "g_ffn",
             "w_gate", "w_up", "ffn_conv_w", "ffn_conv_b", "w_down", "g_ple", "w_ple_gate", "w_ple_proj"]
    big = [n for n, _ in BIG_ROWS]
    conv = [n for n, _ in CONV_SHARDED]
    wts = {n: (args[n][0] if n in big or n in conv else args[n]) for n in names}
    mom = {n: (args["m_" + n][0] if n in big or n in conv else args["m_" + n]) for n in names}
    var = {n: (args["v_" + n][0] if n in big or n in conv else args["v_" + n]) for n in names}
    shard_shapes = {n: wts[n].shape for n in big}
    dev = 4 * lax.axis_index("x") + 2 * lax.axis_index("y") + lax.axis_index("c")
    core = lax.axis_index("c").astype(jnp.int32).reshape(1)

    conv_local = _pad_rows(jnp.concatenate([wts[n].reshape(-1) for n in conv]), 8).reshape(8, 1024)
    gathered, conv_all = _all_gather([_pack_rows([wts[n].astype(BF16) for n in big]), conv_local], "gather_weights")
    full = dict(wts)
    full.update(_gathered_to_full(gathered, shard_shapes))
    off = 0
    for n, width in CONV_SHARDED:
        sc = width // N_DEV
        a = conv_all.reshape(N_DEV, -1)[:, off:off + 3 * sc].reshape(N_DEV, 3, sc)
        full[n] = a.transpose(1, 0, 2).reshape(3, width)
        off += 3 * sc

    loss, dx, grads = _local_step(x[0], p[0, 0], loss_target[0], full, (512, 256))

    stacked = _full_to_stacked(grads, shard_shapes).reshape(N_CHIP, 2, BIG_TOTAL, 1024)
    landed = _sibling_exchange(stacked)
    contributions = _chip_exchange(_pair_sum(stacked, landed, core))

    small = jnp.concatenate([grads[n].reshape(-1) for n, _ in REPLICATED] + [grads[n].reshape(-1) for n in conv]
                            + [loss.reshape(1)])
    (small_all,) = _all_gather([_pad_rows(small, SMALL_ROWS)], "gather_small_grads")

    g_big, d_big, m_big, v_big = _adamw(contributions, _pack_rows([wts[n] for n in big]),
                                        _pack_rows([mom[n] for n in big]), _pack_rows([var[n] for n in big]),
                                        "adamw_large")
    n_rep = sum(s for _, s in REPLICATED)
    conv_sizes = [3 * w_ // N_DEV for _, w_ in CONV_SHARDED]

    def small_state(src):
        flat = jnp.concatenate([src[n].reshape(-1) for n, _ in REPLICATED] + [src[n].reshape(-1) for n in conv])
        return _pad_rows(flat, 16)

    rep_all = small_all.reshape(N_DEV, -1)[:, :n_rep]
    conv_parts, off = [], n_rep
    for (n, width), size in zip(CONV_SHARDED, conv_sizes):
        sc = width // N_DEV
        a = small_all.reshape(N_DEV, -1)[:, off:off + 3 * width].reshape(N_DEV, 3, width)
        conv_parts.append(lax.dynamic_slice(a, (0, 0, dev * sc), (N_DEV, 3, sc)).reshape(N_DEV, size))
        off += 3 * width
    loss_total = jnp.sum(small_all.reshape(N_DEV, -1)[:, off])
    small_parts = jnp.concatenate([rep_all] + conv_parts, axis=1)
    small_parts = jnp.pad(small_parts, ((0, 0), (0, 16 * 1024 - small_parts.shape[1]))).reshape(N_DEV, 16, 1024)
    g_sm, d_sm, m_sm, v_sm = _adamw(small_parts, small_state(wts), small_state(mom), small_state(var), "adamw_small")

    def unpack(big_flat, small_flat, like):
        out = _unpack_big(big_flat, shard_shapes)
        flat, o = small_flat.reshape(-1), 0
        for n, s in list(REPLICATED) + [(n, sz) for (n, _), sz in zip(CONV_SHARDED, conv_sizes)]:
            out[n] = flat[o:o + s]
            o += s
        return [out[n].reshape(like[n].shape) for n in names]

    like = {n: args[n] for n in names}
    return (loss_total, dx[None], *unpack(g_big, g_sm, like), *unpack(d_big, d_sm, like),
            *unpack(m_big, m_sm, like), *unpack(v_big, v_sm, like))
```

```python
import functools

import jax
import jax.numpy as jnp
from jax import lax
from jax.experimental import pallas as pl
from jax.experimental.pallas import tpu as pltpu

F32 = jnp.float32
BF16 = jnp.bfloat16

D_MODEL = 1024
CONV_W = 512
ATTN_W = 512
HEAD_DIM = 64
D_FF = 2816
PLE_DIM = 256
IN_COLS = 3 * CONV_W + 3 * ATTN_W
EPS = 1e-6
QK_BLOCK = 128
DILATIONS = (1, 4, 16)
ATTN_SCALE = HEAD_DIM ** -0.5

ADAM_LR = 0.001
ADAM_B1 = 0.9
ADAM_B2 = 0.999
ADAM_EPS = 1e-08
ADAM_WD = 0.01
ADAM_STEP = 10

N_DEV = 8
N_CHIP = 4
V7X_VMEM_LIMIT = 56 * 1024 * 1024
FF_CHUNKS = 2

BIG_ROWS = (("w_in", 384), ("w_out", 128), ("w_gate", 352), ("w_up", 352), ("w_down", 352),
            ("w_ple_gate", 128), ("w_ple_proj", 32))
BIG_TOTAL = sum(r for _, r in BIG_ROWS)
SMALL_ROWS = 24


def _cparams(*sem):
    return pltpu.CompilerParams(dimension_semantics=sem, vmem_limit_bytes=V7X_VMEM_LIMIT)


def _mm(a, b):
    return jnp.dot(a, b, preferred_element_type=F32)


def _mm_nt(a, b):
    return lax.dot_general(a, b, (((1,), (1,)), ((), ())), preferred_element_type=F32)


def _mm_tn(a, b):
    return lax.dot_general(a, b, (((0,), (0,)), ((), ())), preferred_element_type=F32)


def _full(shape):
    nd = len(shape)
    return pl.BlockSpec(shape, lambda *_: (0,) * nd)


def _rms_stats(x):
    r = lax.rsqrt(jnp.mean(x * x, axis=-1, keepdims=True) + EPS)
    return r, x * r


def _rms_bwd(dy, xhat, r, g):
    gd = dy * g
    return r * (gd - xhat * jnp.mean(gd * xhat, axis=-1, keepdims=True))


def _seg_sum64(v, bd_ref):
    outs = []
    for c in range(0, v.shape[1], 256):
        vc = v[:, c:c + 256]
        hi = vc.astype(BF16)
        lo = (vc - hi.astype(F32)).astype(BF16)
        outs.append(_mm(hi, bd_ref[...]) + _mm(lo, bd_ref[...]))
    return outs[0] if len(outs) == 1 else jnp.concatenate(outs, axis=1)


def _shift_rows(u, k, edge_rows):
    row = lax.broadcasted_iota(jnp.int32, u.shape, 0)
    out = pltpu.roll(u, k, 0)
    for j in range(k):
        out = jnp.where(row == j, edge_rows[k - 1 - j], out)
    return out


def _shift_rows_up(u, k, edge_rows):
    n = u.shape[0]
    row = lax.broadcasted_iota(jnp.int32, u.shape, 0)
    out = pltpu.roll(u, n - k, 0)
    for j in range(k):
        out = jnp.where(row == n - k + j, edge_rows[j], out)
    return out


def _conv_fwd(u, c1, c2, w_ref, b_ref):
    u1 = _shift_rows(u, 1, (c1,))
    u2 = _shift_rows(u, 2, (c1, c2))
    y = u2 * w_ref[0:1, :] + u1 * w_ref[1:2, :] + u * w_ref[2:3, :] + b_ref[...]
    return y, u1, u2


def _conv_bwd_input(dy, n1row, n2row, w_ref):
    d1 = _shift_rows_up(dy, 1, (n1row,))
    d2 = _shift_rows_up(dy, 2, (n1row, n2row))
    return dy * w_ref[2:3, :] + d1 * w_ref[1:2, :] + d2 * w_ref[0:1, :]


def _sigmoid(x):
    return 1.0 / (1.0 + jnp.exp(-x))


def _inproj_fwd(x, g_mix, w_in, conv_w, conv_b, qg, kg, bd, tm):
    t = x.shape[0]

    def body(x_ref, g_ref, w_ref, cw_ref, cb_ref, qg_ref, kg_ref, bd_ref,
             zc_ref, zqk_ref, yc_ref, q_ref, k_ref, v_ref, carry_ref):
        @pl.when(pl.program_id(0) == 0)
        def _():
            carry_ref[...] = jnp.zeros_like(carry_ref)

        _, xhat = _rms_stats(x_ref[...])
        h = (xhat * g_ref[...]).astype(BF16)
        zconv = _mm(h, w_ref[:, 0:3 * CONV_W])
        zc_ref[...] = zconv
        u = zconv[:, CONV_W:2 * CONV_W] * zconv[:, 2 * CONV_W:3 * CONV_W]
        cv, _, _ = _conv_fwd(u, carry_ref[7:8, :], carry_ref[6:7, :], cw_ref, cb_ref)
        yc_ref[...] = zconv[:, 0:CONV_W] * cv
        carry_ref[...] = u[tm - 8:tm, :]

        zqk = _mm(h, w_ref[:, 3 * CONV_W:3 * CONV_W + 2 * ATTN_W])
        zqk_ref[...] = zqk
        for j, (gain_ref, out_ref, scale) in enumerate(((qg_ref, q_ref, ATTN_SCALE), (kg_ref, k_ref, 1.0))):
            z = zqk[:, j * ATTN_W:(j + 1) * ATTN_W]
            r = lax.rsqrt(_seg_sum64(z * z, bd_ref) * (1.0 / HEAD_DIM) + EPS)
            out_ref[...] = z * r * gain_ref[...] * scale
        v_ref[...] = _mm(h, w_ref[:, 3 * CONV_W + 2 * ATTN_W:IN_COLS])

    def blk(c):
        return pl.BlockSpec((tm, c), lambda i: (i, 0))

    return pl.pallas_call(
        body, name="inproj_fwd", grid=(t // tm,),
        in_specs=[blk(D_MODEL), _full((1, D_MODEL)), _full((D_MODEL, IN_COLS)), _full((3, CONV_W)),
                  _full((1, CONV_W)), _full((1, ATTN_W)), _full((1, ATTN_W)), _full((256, 256))],
        out_specs=[blk(3 * CONV_W), blk(2 * ATTN_W), blk(CONV_W), blk(ATTN_W), blk(ATTN_W), blk(ATTN_W)],
        out_shape=[jax.ShapeDtypeStruct((t, 3 * CONV_W), F32), jax.ShapeDtypeStruct((t, 2 * ATTN_W), F32),
                   jax.ShapeDtypeStruct((t, CONV_W), F32), jax.ShapeDtypeStruct((t, ATTN_W), F32),
                   jax.ShapeDtypeStruct((t, ATTN_W), F32), jax.ShapeDtypeStruct((t, ATTN_W), F32)],
        scratch_shapes=[pltpu.VMEM((8, CONV_W), F32)],
        compiler_params=_cparams("arbitrary"),
    )(x, g_mix, w_in, conv_w, conv_b, qg, kg, bd)


SUPER = 16 * QK_BLOCK
KEYS = 2 * QK_BLOCK


def _rows(start, size, dil):
    return pl.ds(start, size) if dil == 1 else pl.ds(start, size, stride=dil)


def _attn_bias(sl_ref, dil):
    qi = lax.broadcasted_iota(jnp.int32, (KEYS, KEYS), 0)
    kj = lax.broadcasted_iota(jnp.int32, (KEYS, KEYS), 1)
    step = jnp.bitwise_and(qi, QK_BLOCK - 1) + QK_BLOCK - kj
    slope = jnp.where(qi < QK_BLOCK, sl_ref[0, 0:1, 0:1], sl_ref[0, 1:2, 0:1])
    bias = jnp.where(jnp.logical_and(step >= 0, step <= QK_BLOCK), -slope * (step * dil).astype(F32), -jnp.inf)
    return bias, kj >= QK_BLOCK


def _unit_start(u, dil):
    if dil == 1:
        return pl.multiple_of(u * QK_BLOCK, QK_BLOCK)
    if dil == 4:
        return jnp.bitwise_and(u, 3) + (u // 4) * (4 * QK_BLOCK)
    return u


def _stack_heads(a, head0):
    zero = jnp.zeros_like(a)
    return jnp.concatenate([jnp.where(head0, a, zero), jnp.where(head0, zero, a)], axis=0)


def _attn_fwd(q, k, v, slopes):
    t = q.shape[0]
    nsb = t // SUPER

    def body(q_ref, kc_ref, kp_ref, vc_ref, vp_ref, sl_ref, o_ref, l_ref, kk, vv, ob, lb):
        s = pl.program_id(1)
        kk[0:SUPER, :] = kp_ref[...]
        kk[SUPER:, :] = kc_ref[...]
        vv[0:SUPER, :] = vp_ref[...]
        vv[SUPER:, :] = vc_ref[...]
        head0 = lax.broadcasted_iota(jnp.int32, (QK_BLOCK, QK_BLOCK), 1) < HEAD_DIM

        for b, dil in enumerate(DILATIONS):
            bias, own_half = _attn_bias(sl_ref, dil)

            def unit(u, carry, b=b, dil=dil, bias=bias, own_half=own_half):
                start = _unit_start(u, dil)
                first_key = SUPER + start - QK_BLOCK * dil
                q2 = _stack_heads(q_ref[_rows(start, QK_BLOCK, dil), :].astype(BF16), head0)
                k2 = kk[_rows(first_key, KEYS, dil), :].astype(BF16)
                v2 = vv[_rows(first_key, KEYS, dil), :].astype(BF16)
                has_prev = jnp.logical_or(s > 0, start >= QK_BLOCK * dil)
                sc = jnp.where(jnp.logical_or(own_half, has_prev), _mm_nt(q2, k2) + bias, -jnp.inf)
                m = jnp.max(sc, axis=-1, keepdims=True)
                e = jnp.exp(sc - m)
                den = jnp.sum(e, axis=-1, keepdims=True)
                o2 = _mm(e.astype(BF16), v2) / den
                l2 = m + jnp.log(den)
                ob[b, _rows(start, QK_BLOCK, dil), :] = jnp.where(head0, o2[0:QK_BLOCK], o2[QK_BLOCK:])
                lb[b, _rows(start, QK_BLOCK, dil), :] = jnp.where(head0, l2[0:QK_BLOCK], l2[QK_BLOCK:])
                return carry

            lax.fori_loop(0, SUPER // QK_BLOCK, unit, 0)

        def merge(i, carry):
            rows = pl.ds(pl.multiple_of(i * 256, 256), 256)
            la, lb_, lc = lb[0, rows, :], lb[1, rows, :], lb[2, rows, :]
            mx = jnp.maximum(jnp.maximum(la, lb_), lc)
            wa, wb, wc = jnp.exp(la - mx), jnp.exp(lb_ - mx), jnp.exp(lc - mx)
            sw = wa + wb + wc
            o_ref[rows, :] = (wa * ob[0, rows, :] + wb * ob[1, rows, :] + wc * ob[2, rows, :]) / sw
            l_ref[rows, :] = mx + jnp.log(sw)
            return carry

        lax.fori_loop(0, SUPER // 256, merge, 0)

    cur = pl.BlockSpec((SUPER, QK_BLOCK), lambda p, s: (s, p))
    prev = pl.BlockSpec((SUPER, QK_BLOCK), lambda p, s: (jnp.maximum(s - 1, 0), p))
    return pl.pallas_call(
        body, name="attn_fwd", grid=(4, nsb),
        in_specs=[cur, cur, prev, cur, prev, pl.BlockSpec((1, 2, QK_BLOCK), lambda p, s: (p, 0, 0))],
        out_specs=[cur, cur],
        out_shape=[jax.ShapeDtypeStruct((t, ATTN_W), F32), jax.ShapeDtypeStruct((t, ATTN_W), F32)],
        scratch_shapes=[pltpu.VMEM((2 * SUPER, QK_BLOCK), F32), pltpu.VMEM((2 * SUPER, QK_BLOCK), F32),
                        pltpu.VMEM((3, SUPER, QK_BLOCK), F32), pltpu.VMEM((3, SUPER, QK_BLOCK), F32)],
        compiler_params=_cparams("parallel", "arbitrary"),
    )(q, k, k, v, v, slopes)


def _outproj_fwd(ya, yc, x, goc, goa, w_out, tm):
    t = x.shape[0]

    def body(ya_ref, yc_ref, x_ref, goc_ref, goa_ref, w_ref, x1_ref):
        _, ychat = _rms_stats(yc_ref[...])
        _, yahat = _rms_stats(ya_ref[...])
        acc = _mm((ychat * goc_ref[...]).astype(BF16), w_ref[0:CONV_W, :])
        acc += _mm((yahat * goa_ref[...]).astype(BF16), w_ref[CONV_W:, :])
        x1_ref[...] = x_ref[...] + acc

    def blk(c):
        return pl.BlockSpec((tm, c), lambda i: (i, 0))

    return pl.pallas_call(
        body, name="outproj_fwd", grid=(t // tm,),
        in_specs=[blk(ATTN_W), blk(CONV_W), blk(D_MODEL), _full((1, CONV_W)), _full((1, ATTN_W)),
                  _full((D_MODEL, D_MODEL))],
        out_specs=blk(D_MODEL),
        out_shape=jax.ShapeDtypeStruct((t, D_MODEL), F32),
        compiler_params=_cparams("parallel"),
    )(ya, yc, x, goc, goa, w_out)


def _ffn_fwd(x1, g_ffn, w_gate, w_up, w_down, fcw, fcb, tm):
    t = x1.shape[0]

    def body(x_ref, g_ref, wg_ref, wu_ref, wd_ref, cw_ref, cb_ref, gp_ref, up_ref, x2_ref, carry_ref):
        @pl.when(pl.program_id(0) == 0)
        def _():
            carry_ref[...] = jnp.zeros_like(carry_ref)

        xv = x_ref[...]
        _, xhat = _rms_stats(xv)
        h = (xhat * g_ref[...]).astype(BF16)
        gp = _mm(h, wg_ref[...])
        gp_ref[...] = gp
        gate, _, _ = _conv_fwd(gp, carry_ref[7:8, :], carry_ref[6:7, :], cw_ref, cb_ref)
        carry_ref[...] = gp[tm - 8:tm, :]
        up = _mm(h, wu_ref[...])
        up_ref[...] = up
        a = (gate * _sigmoid(gate) * up).astype(BF16)
        x2_ref[...] = xv + _mm(a, wd_ref[...])

    def blk(c):
        return pl.BlockSpec((tm, c), lambda i: (i, 0))

    return pl.pallas_call(
        body, name="ffn_fwd", grid=(t // tm,),
        in_specs=[blk(D_MODEL), _full((1, D_MODEL)), _full((D_MODEL, D_FF)), _full((D_MODEL, D_FF)),
                  _full((D_FF, D_MODEL)), _full((3, D_FF)), _full((1, D_FF))],
        out_specs=[blk(D_FF), blk(D_FF), blk(D_MODEL)],
        out_shape=[jax.ShapeDtypeStruct((t, D_FF), F32), jax.ShapeDtypeStruct((t, D_FF), F32),
                   jax.ShapeDtypeStruct((t, D_MODEL), F32)],
        scratch_shapes=[pltpu.VMEM((8, D_FF), F32)],
        compiler_params=_cparams("arbitrary"),
    )(x1, g_ffn, w_gate, w_up, w_down, fcw, fcb)


def _ple_fwd_bwd(x2, p, target, g_ple, w_pg, w_pp, tm):
    t = x2.shape[0]

    def body(x_ref, p_ref, t_ref, g_ref, wg_ref, wp_ref, dx_ref, loss_ref, dwg_ref, dwp_ref, dg_ref):
        @pl.when(pl.program_id(0) == 0)
        def _():
            loss_ref[...] = jnp.zeros_like(loss_ref)
            dwg_ref[...] = jnp.zeros_like(dwg_ref)
            dwp_ref[...] = jnp.zeros_like(dwp_ref)
            dg_ref[...] = jnp.zeros_like(dg_ref)

        xv = x_ref[...]
        r, xhat = _rms_stats(xv)
        g = g_ref[...]
        h = (xhat * g).astype(BF16)
        pg = _sigmoid(_mm(h, wg_ref[...]))
        pb = p_ref[...].astype(BF16)
        pp = _mm(pb, wp_ref[...])
        err = xv + pg * pp - t_ref[...]
        loss_ref[...] += 0.5 * jnp.sum(jnp.mean(err * err, axis=-1, keepdims=True))
        dx3 = err * (1.0 / D_MODEL)
        d_pp = (dx3 * pg).astype(BF16)
        d_pre = (dx3 * pp * pg * (1.0 - pg)).astype(BF16)
        dwp_ref[...] += _mm_tn(pb, d_pp)
        dwg_ref[...] += _mm_tn(h, d_pre)
        dh = _mm_nt(d_pre, wg_ref[...])
        dg_ref[...] += jnp.sum(dh * xhat, axis=0, keepdims=True)
        dx_ref[...] = dx3 + _rms_bwd(dh, xhat, r, g)

    def blk(c):
        return pl.BlockSpec((tm, c), lambda i: (i, 0))

    return pl.pallas_call(
        body, name="ple_fwd_bwd", grid=(t // tm,),
        in_specs=[blk(D_MODEL), blk(PLE_DIM), blk(D_MODEL), _full((1, D_MODEL)), _full((D_MODEL, D_MODEL)),
                  _full((PLE_DIM, D_MODEL))],
        out_specs=[blk(D_MODEL), _full((8, 128)), _full((D_MODEL, D_MODEL)), _full((PLE_DIM, D_MODEL)),
                   _full((1, D_MODEL))],
        out_shape=[jax.ShapeDtypeStruct((t, D_MODEL), F32), jax.ShapeDtypeStruct((8, 128), F32),
                   jax.ShapeDtypeStruct((D_MODEL, D_MODEL), F32), jax.ShapeDtypeStruct((PLE_DIM, D_MODEL), F32),
                   jax.ShapeDtypeStruct((1, D_MODEL), F32)],
        compiler_params=_cparams("arbitrary"),
    )(x2, p, target, g_ple, w_pg, w_pp)


def _ffn_bwd(dx2, x1, g_ffn, gp, up, w_gate, w_up, w_down, fcw, fcb, tm):
    t = x1.shape[0]
    nblk = t // tm
    fc = D_FF // FF_CHUNKS

    def body(dx_ref, x_ref, g_ref, gp_ref, gph_ref, up_ref, wg_ref, wu_ref, wd_ref, cw_ref, cb_ref,
             dh_ref, dwd_ref, dwu_ref, dwg_ref, dcw_ref, dcb_ref, carry_ref):
        i = pl.program_id(1)

        @pl.when(i == 0)
        def _():
            carry_ref[...] = jnp.zeros_like(carry_ref)
            dwd_ref[...] = jnp.zeros_like(dwd_ref)
            dwu_ref[...] = jnp.zeros_like(dwu_ref)
            dwg_ref[...] = jnp.zeros_like(dwg_ref)
            dcw_ref[...] = jnp.zeros_like(dcw_ref)
            dcb_ref[...] = jnp.zeros_like(dcb_ref)

        keep = (i < nblk - 1).astype(F32)
        dxb = dx_ref[...].astype(BF16)
        _, xhat = _rms_stats(x_ref[...])
        h = (xhat * g_ref[...]).astype(BF16)
        gp_v = gp_ref[...]
        gate, gp1, gp2 = _conv_fwd(gp_v, gph_ref[7:8, :] * keep, gph_ref[6:7, :] * keep, cw_ref, cb_ref)
        s = _sigmoid(gate)
        silu = gate * s
        up_v = up_ref[...]
        da = _mm_nt(dxb, wd_ref[...])
        dwd_ref[...] += _mm_tn((silu * up_v).astype(BF16), dxb)
        d_up = (da * silu).astype(BF16)
        d_gate = da * up_v * (s * (1.0 + gate * (1.0 - s)))
        dwu_ref[...] += _mm_tn(h, d_up)
        d_gp = _conv_bwd_input(d_gate, carry_ref[0:1, :], carry_ref[1:2, :], cw_ref).astype(BF16)
        carry_ref[...] = d_gate[0:8, :]
        dcw_ref[0:1, :] += jnp.sum(d_gate * gp2, axis=0, keepdims=True)
        dcw_ref[1:2, :] += jnp.sum(d_gate * gp1, axis=0, keepdims=True)
        dcw_ref[2:3, :] += jnp.sum(d_gate * gp_v, axis=0, keepdims=True)
        dcb_ref[...] += jnp.sum(d_gate, axis=0, keepdims=True)
        dwg_ref[...] += _mm_tn(h, d_gp)
        dh_ref[...] = _mm_nt(d_gp, wg_ref[...]) + _mm_nt(d_up, wu_ref[...])

    def rev(i):
        return nblk - 1 - i

    one = pl.Buffered(1)
    in_specs = [
        pl.BlockSpec((tm, D_MODEL), lambda j, i: (rev(i), 0)),
        pl.BlockSpec((tm, D_MODEL), lambda j, i: (rev(i), 0)),
        _full((1, D_MODEL)),
        pl.BlockSpec((tm, fc), lambda j, i: (rev(i), j)),
        pl.BlockSpec((8, fc), lambda j, i: (jnp.maximum(rev(i) * (tm // 8) - 1, 0), j)),
        pl.BlockSpec((tm, fc), lambda j, i: (rev(i), j)),
        pl.BlockSpec((D_MODEL, fc), lambda j, i: (0, j), pipeline_mode=one),
        pl.BlockSpec((D_MODEL, fc), lambda j, i: (0, j), pipeline_mode=one),
        pl.BlockSpec((fc, D_MODEL), lambda j, i: (j, 0), pipeline_mode=one),
        pl.BlockSpec((3, fc), lambda j, i: (0, j)),
        pl.BlockSpec((1, fc), lambda j, i: (0, j)),
    ]
    out_specs = [
        pl.BlockSpec((None, tm, D_MODEL), lambda j, i: (j, rev(i), 0)),
        pl.BlockSpec((fc, D_MODEL), lambda j, i: (j, 0), pipeline_mode=one),
        pl.BlockSpec((D_MODEL, fc), lambda j, i: (0, j), pipeline_mode=one),
        pl.BlockSpec((D_MODEL, fc), lambda j, i: (0, j), pipeline_mode=one),
        pl.BlockSpec((3, fc), lambda j, i: (0, j)),
        pl.BlockSpec((1, fc), lambda j, i: (0, j)),
    ]
    return pl.pallas_call(
        body, name="ffn_bwd", grid=(FF_CHUNKS, nblk), in_specs=in_specs, out_specs=out_specs,
        out_shape=[jax.ShapeDtypeStruct((FF_CHUNKS, t, D_MODEL), F32), jax.ShapeDtypeStruct((D_FF, D_MODEL), F32),
                   jax.ShapeDtypeStruct((D_MODEL, D_FF), F32), jax.ShapeDtypeStruct((D_MODEL, D_FF), F32),
                   jax.ShapeDtypeStruct((3, D_FF), F32), jax.ShapeDtypeStruct((1, D_FF), F32)],
        scratch_shapes=[pltpu.VMEM((8, fc), F32)],
        compiler_params=_cparams("arbitrary", "arbitrary"),
    )(dx2, x1, g_ffn, gp, gp, up, w_gate, w_up, w_down, fcw, fcb)


def _outproj_bwd(dh2, dx2, x1, g_ffn, w_out, yc, ya, goc, goa, zconv, conv_w, conv_b, bd, tm):
    t = x1.shape[0]
    nblk = t // tm

    def body(dh_ref, dx2_ref, x1_ref, g_ref, w_ref, yc_ref, ya_ref, goc_ref, goa_ref, zc_ref, zch_ref, cw_ref, cb_ref,
             bd_ref, dx1_ref, dya_ref, dd_ref, dzc_ref, dw_ref, dg_ref, dgoc_ref, dgoa_ref, dcw_ref, dcb_ref,
             carry_ref):
        i = pl.program_id(0)

        @pl.when(i == 0)
        def _():
            carry_ref[...] = jnp.zeros_like(carry_ref)
            for ref in (dw_ref, dg_ref, dgoc_ref, dgoa_ref, dcw_ref, dcb_ref):
                ref[...] = jnp.zeros_like(ref)

        keep = (i < nblk - 1).astype(F32)
        dh2_v = dh_ref[0]
        for j in range(1, FF_CHUNKS):
            dh2_v = dh2_v + dh_ref[j]
        r, xhat = _rms_stats(x1_ref[...])
        dg_ref[...] += jnp.sum(dh2_v * xhat, axis=0, keepdims=True)
        dx1 = dx2_ref[...] + _rms_bwd(dh2_v, xhat, r, g_ref[...])
        dx1_ref[...] = dx1
        dx1b = dx1.astype(BF16)
        dy = _mm_nt(dx1b, w_ref[...])

        yc_v = yc_ref[...]
        rc, ychat = _rms_stats(yc_v)
        dw_ref[0:CONV_W, :] += _mm_tn((ychat * goc_ref[...]).astype(BF16), dx1b)
        dyc = dy[:, 0:CONV_W]
        dgoc_ref[...] += jnp.sum(dyc * ychat, axis=0, keepdims=True)
        d_yc = _rms_bwd(dyc, ychat, rc, goc_ref[...])

        ya_v = ya_ref[...]
        ra, yahat = _rms_stats(ya_v)
        dw_ref[CONV_W:, :] += _mm_tn((yahat * goa_ref[...]).astype(BF16), dx1b)
        dya = dy[:, CONV_W:]
        dgoa_ref[...] += jnp.sum(dya * yahat, axis=0, keepdims=True)
        d_ya = _rms_bwd(dya, yahat, ra, goa_ref[...])
        dya_ref[...] = d_ya
        dd_ref[...] = _seg_sum64(d_ya * ya_v, bd_ref)

        zb = zc_ref[:, 0:CONV_W]
        zc = zc_ref[:, CONV_W:2 * CONV_W]
        zx = zc_ref[:, 2 * CONV_W:3 * CONV_W]
        u = zc * zx
        uh = zch_ref[:, CONV_W:2 * CONV_W] * zch_ref[:, 2 * CONV_W:3 * CONV_W] * keep
        cv, u1, u2 = _conv_fwd(u, uh[7:8, :], uh[6:7, :], cw_ref, cb_ref)
        d_cv = d_yc * zb
        d_u = _conv_bwd_input(d_cv, carry_ref[0:1, :], carry_ref[1:2, :], cw_ref)
        carry_ref[...] = d_cv[0:8, :]
        dcw_ref[0:1, :] += jnp.sum(d_cv * u2, axis=0, keepdims=True)
        dcw_ref[1:2, :] += jnp.sum(d_cv * u1, axis=0, keepdims=True)
        dcw_ref[2:3, :] += jnp.sum(d_cv * u, axis=0, keepdims=True)
        dcb_ref[...] += jnp.sum(d_cv, axis=0, keepdims=True)
        dzc_ref[:, 0:CONV_W] = d_yc * cv
        dzc_ref[:, CONV_W:2 * CONV_W] = d_u * zx
        dzc_ref[:, 2 * CONV_W:3 * CONV_W] = d_u * zc

    def rev(i):
        return nblk - 1 - i

    def blk(c):
        return pl.BlockSpec((tm, c), lambda i: (rev(i), 0))

    in_specs = [
        pl.BlockSpec((FF_CHUNKS, tm, D_MODEL), lambda i: (0, rev(i), 0)),
        blk(D_MODEL), blk(D_MODEL), _full((1, D_MODEL)), _full((D_MODEL, D_MODEL)),
        blk(CONV_W), blk(ATTN_W), _full((1, CONV_W)), _full((1, ATTN_W)),
        blk(3 * CONV_W),
        pl.BlockSpec((8, 3 * CONV_W), lambda i: (jnp.maximum(rev(i) * (tm // 8) - 1, 0), 0)),
        _full((3, CONV_W)), _full((1, CONV_W)), _full((256, 256)),
    ]
    out_specs = [blk(D_MODEL), blk(ATTN_W), blk(ATTN_W), blk(3 * CONV_W), _full((D_MODEL, D_MODEL)),
                 _full((1, D_MODEL)), _full((1, CONV_W)), _full((1, ATTN_W)), _full((3, CONV_W)), _full((1, CONV_W))]
    return pl.pallas_call(
        body, name="outproj_bwd", grid=(nblk,), in_specs=in_specs, out_specs=out_specs,
        out_shape=[jax.ShapeDtypeStruct((t, D_MODEL), F32), jax.ShapeDtypeStruct((t, ATTN_W), F32),
                   jax.ShapeDtypeStruct((t, ATTN_W), F32), jax.ShapeDtypeStruct((t, 3 * CONV_W), F32),
                   jax.ShapeDtypeStruct((D_MODEL, D_MODEL), F32), jax.ShapeDtypeStruct((1, D_MODEL), F32),
                   jax.ShapeDtypeStruct((1, CONV_W), F32), jax.ShapeDtypeStruct((1, ATTN_W), F32),
                   jax.ShapeDtypeStruct((3, CONV_W), F32), jax.ShapeDtypeStruct((1, CONV_W), F32)],
        scratch_shapes=[pltpu.VMEM((8, CONV_W), F32)],
        compiler_params=_cparams("arbitrary"),
    )(dh2, dx2, x1, g_ffn, w_out, yc, ya, goc, goa, zconv, zconv, conv_w, conv_b, bd)


def _attn_bwd(q, k, v, dya, lse, dd, slopes):
    t = q.shape[0]
    nsb = t // SUPER

    def body(q_ref, kc_ref, kp_ref, vc_ref, vp_ref, dy_ref, l_ref, d_ref, sl_ref, dq_ref, dk_ref, dv_ref,
             kk, vv, dkacc, dvacc):
        s = pl.program_id(1)

        @pl.when(s == 0)
        def _():
            dkacc[...] = jnp.zeros_like(dkacc)
            dvacc[...] = jnp.zeros_like(dvacc)

        dkacc[0:SUPER, :] = dkacc[SUPER:, :]
        dvacc[0:SUPER, :] = dvacc[SUPER:, :]
        dkacc[SUPER:, :] = jnp.zeros((SUPER, QK_BLOCK), F32)
        dvacc[SUPER:, :] = jnp.zeros((SUPER, QK_BLOCK), F32)

        @pl.when(s < nsb)
        def _():
            kk[0:SUPER, :] = kp_ref[...]
            kk[SUPER:, :] = kc_ref[...]
            vv[0:SUPER, :] = vp_ref[...]
            vv[SUPER:, :] = vc_ref[...]
            head0 = lax.broadcasted_iota(jnp.int32, (QK_BLOCK, QK_BLOCK), 1) < HEAD_DIM

            for b, dil in enumerate(DILATIONS):
                bias, own_half = _attn_bias(sl_ref, dil)

                def unit(u, carry, b=b, dil=dil, bias=bias, own_half=own_half):
                    start = _unit_start(u, dil)
                    first_key = SUPER + start - QK_BLOCK * dil
                    qrows = _rows(start, QK_BLOCK, dil)
                    krows = _rows(first_key, KEYS, dil)
                    q2 = _stack_heads(q_ref[qrows, :].astype(BF16), head0)
                    dy2 = _stack_heads(dy_ref[qrows, :].astype(BF16), head0)
                    lv, dv_ = l_ref[qrows, :], d_ref[qrows, :]
                    l2 = jnp.concatenate([lv[:, 0:1], lv[:, HEAD_DIM:HEAD_DIM + 1]], axis=0)
                    d2 = jnp.concatenate([dv_[:, 0:1], dv_[:, HEAD_DIM:HEAD_DIM + 1]], axis=0)
                    k2 = kk[krows, :].astype(BF16)
                    v2 = vv[krows, :].astype(BF16)
                    has_prev = jnp.logical_or(s > 0, start >= QK_BLOCK * dil)
                    sc = jnp.where(jnp.logical_or(own_half, has_prev), _mm_nt(q2, k2) + bias, -jnp.inf)
                    prob = jnp.exp(sc - l2)
                    ds = (prob * (_mm_nt(dy2, v2) - d2)).astype(BF16)
                    dvacc[krows, :] += _mm_tn(prob.astype(BF16), dy2)
                    dkacc[krows, :] += _mm_tn(ds, q2)
                    dq2 = _mm(ds, k2)
                    dq = jnp.where(head0, dq2[0:QK_BLOCK], dq2[QK_BLOCK:]) * ATTN_SCALE
                    if b == 0:
                        dq_ref[qrows, :] = dq
                    else:
                        dq_ref[qrows, :] += dq
                    return carry

                lax.fori_loop(0, SUPER // QK_BLOCK, unit, 0)

        dk_ref[...] = dkacc[0:SUPER, :]
        dv_ref[...] = dvacc[0:SUPER, :]

    def cur_map(p, s):
        return (jnp.minimum(s, nsb - 1), p)

    def prev_map(p, s):
        return (jnp.clip(s - 1, 0, nsb - 1), p)

    cur = pl.BlockSpec((SUPER, QK_BLOCK), cur_map)
    prev = pl.BlockSpec((SUPER, QK_BLOCK), prev_map)
    return pl.pallas_call(
        body, name="attn_bwd", grid=(4, nsb + 1),
        in_specs=[cur, cur, prev, cur, prev, cur, cur, cur, pl.BlockSpec((1, 2, QK_BLOCK), lambda p, s: (p, 0, 0))],
        out_specs=[cur, prev, prev],
        out_shape=[jax.ShapeDtypeStruct((t, ATTN_W), F32)] * 3,
        scratch_shapes=[pltpu.VMEM((2 * SUPER, QK_BLOCK), F32)] * 4,
        compiler_params=_cparams("parallel", "arbitrary"),
    )(q, k, k, v, v, dya, lse, dd, slopes)


def _attn_bwd_per_branch_unused(q, k, v, dya, lse, dd, slopes, dil):
    t = q.shape[0]
    length = t // dil
    chunk = _attn_chunk(t, dil)
    nch = length // chunk
    nb = chunk // QK_BLOCK
    nblocks = length // QK_BLOCK
    view = (length, dil * ATTN_W)
    ext = chunk + QK_BLOCK

    def body(q_ref, dy_ref, l_ref, d_ref, k_ref, v_ref, qn_ref, dyn_ref, ln_ref, dn_ref, kh_ref, vh_ref, sl_ref,
             dq_ref, dk_ref, dv_ref, qbuf, dybuf, lbuf, dbuf, kbuf, vbuf, dkacc, dvacc):
        c = pl.program_id(2)
        qbuf[0:chunk, :] = q_ref[...]
        qbuf[chunk:, :] = qn_ref[...]
        dybuf[0:chunk, :] = dy_ref[...].astype(BF16)
        dybuf[chunk:, :] = dyn_ref[...].astype(BF16)
        lbuf[0:chunk, :] = l_ref[...]
        lbuf[chunk:, :] = ln_ref[...]
        dbuf[0:chunk, :] = d_ref[...]
        dbuf[chunk:, :] = dn_ref[...]
        kbuf[0:QK_BLOCK, :] = kh_ref[...]
        kbuf[QK_BLOCK:, :] = k_ref[...]
        vbuf[0:QK_BLOCK, :] = vh_ref[...]
        vbuf[QK_BLOCK:, :] = v_ref[...]
        valid_cur, valid_prev, dist_cur, dist_prev, head0 = _attn_masks(dil)

        def pair(qb, dyb, lv, dv_, kb, vb, valid, dist):
            dq = jnp.zeros((QK_BLOCK, QK_BLOCK), F32)
            dk = jnp.zeros((QK_BLOCK, QK_BLOCK), F32)
            dvv = jnp.zeros((QK_BLOCK, QK_BLOCK), F32)
            for hh in range(2):
                sl = sl_ref[0, hh:hh + 1, :]
                hm = head0 if hh == 0 else jnp.logical_not(head0)
                col = hh * HEAD_DIM
                qm = jnp.where(hm, qb, jnp.zeros_like(qb))
                dym = jnp.where(hm, dyb, jnp.zeros_like(dyb))
                s = jnp.where(valid, _mm_nt(qm, kb) - sl * dist, -jnp.inf)
                prob = jnp.exp(s - lv[:, col:col + 1])
                ds = (prob * (_mm_nt(dym, vb) - dv_[:, col:col + 1])).astype(BF16)
                dvv += _mm_tn(prob.astype(BF16), dym)
                dk += _mm_tn(ds, qm)
                dq += jnp.where(hm, _mm(ds, kb), 0.0)
            return dq, dk, dvv

        def blk(j, carry):
            off = pl.multiple_of(j * QK_BLOCK, QK_BLOCK)
            nxt = pl.multiple_of(off + QK_BLOCK, QK_BLOCK)
            qb = qbuf[pl.ds(off, QK_BLOCK), :]
            dyb = dybuf[pl.ds(off, QK_BLOCK), :]
            lv = lbuf[pl.ds(off, QK_BLOCK), :]
            dv_ = dbuf[pl.ds(off, QK_BLOCK), :]
            dq_c, dk_c, dv_c = pair(qb, dyb, lv, dv_, kbuf[pl.ds(nxt, QK_BLOCK), :], vbuf[pl.ds(nxt, QK_BLOCK), :],
                                    valid_cur, dist_cur)
            dkacc[pl.ds(nxt, QK_BLOCK), :] = dk_c
            dvacc[pl.ds(nxt, QK_BLOCK), :] = dv_c
            has_prev = jnp.logical_or(c > 0, j > 0)
            dq_p, dk_p, dv_p = pair(qb, dyb, lv, dv_, kbuf[pl.ds(off, QK_BLOCK), :], vbuf[pl.ds(off, QK_BLOCK), :],
                                    jnp.logical_and(valid_prev, has_prev), dist_prev)

            @pl.when(j > 0)
            def _():
                dkacc[pl.ds(off, QK_BLOCK), :] += dk_p
                dvacc[pl.ds(off, QK_BLOCK), :] += dv_p

            dq_ref[pl.ds(off, QK_BLOCK), :] = (dq_c + dq_p) * ATTN_SCALE
            return carry

        lax.fori_loop(0, nb, blk, 0)

        @pl.when(c < nch - 1)
        def _():
            _, dk_p, dv_p = pair(qbuf[chunk:, :], dybuf[chunk:, :], lbuf[chunk:, :], dbuf[chunk:, :],
                                 kbuf[chunk:, :], vbuf[chunk:, :], valid_prev, dist_prev)
            dkacc[chunk:, :] += dk_p
            dvacc[chunk:, :] += dv_p

        dk_ref[...] = dkacc[QK_BLOCK:, :]
        dv_ref[...] = dvacc[QK_BLOCK:, :]

    def cmap(p, r, c):
        return (c, r * 4 + p)

    def before(p, r, c):
        return (jnp.maximum(c * nb - 1, 0), r * 4 + p)

    def after(p, r, c):
        return (jnp.minimum((c + 1) * nb, nblocks - 1), r * 4 + p)

    main = pl.BlockSpec((chunk, QK_BLOCK), cmap)
    hb = pl.BlockSpec((QK_BLOCK, QK_BLOCK), before)
    ha = pl.BlockSpec((QK_BLOCK, QK_BLOCK), after)
    qv, kv, vv = q.reshape(view), k.reshape(view), v.reshape(view)
    dyv, lv, ddv = dya.reshape(view), lse.reshape(view), dd.reshape(view)
    outs = pl.pallas_call(
        body, name=f"attn_bwd_d{dil}", grid=(4, dil, nch),
        in_specs=[main] * 6 + [ha] * 4 + [hb] * 2 + [pl.BlockSpec((1, 2, QK_BLOCK), lambda p, r, c: (p, 0, 0))],
        out_specs=[main] * 3,
        out_shape=[jax.ShapeDtypeStruct(view, F32)] * 3,
        scratch_shapes=[pltpu.VMEM((ext, QK_BLOCK), BF16), pltpu.VMEM((ext, QK_BLOCK), BF16),
                        pltpu.VMEM((ext, QK_BLOCK), F32), pltpu.VMEM((ext, QK_BLOCK), F32),
                        pltpu.VMEM((ext, QK_BLOCK), BF16), pltpu.VMEM((ext, QK_BLOCK), BF16),
                        pltpu.VMEM((ext, QK_BLOCK), F32), pltpu.VMEM((ext, QK_BLOCK), F32)],
        compiler_params=_cparams("arbitrary", "arbitrary", "arbitrary"),
    )(qv, dyv, lv, ddv, kv, vv, qv, dyv, lv, ddv, kv, vv, slopes)
    return [o.reshape(t, ATTN_W) for o in outs]


def _inproj_bwd(dq, dk, dv, dzconv, zqk, x, dx1, g_mix, w_in, qg, kg, bd, tm):
    t = x.shape[0]

    def body(dq_ref, dk_ref, dv_ref, dzc_ref, zqk_ref, x_ref, dx1_ref, g_ref, w_ref, qg_ref,
             kg_ref, bd_ref, dx_ref, dw_ref, dg_ref, dqg_ref, dkg_ref):
        @pl.when(pl.program_id(0) == 0)
        def _():
            for ref in (dw_ref, dg_ref, dqg_ref, dkg_ref):
                ref[...] = jnp.zeros_like(ref)

        parts = [dzc_ref[...].astype(BF16)]
        for j, (dn_ref, gain_ref, dgain_ref) in enumerate(((dq_ref, qg_ref, dqg_ref), (dk_ref, kg_ref, dkg_ref))):
            dn = dn_ref[...]
            z = zqk_ref[:, j * ATTN_W:(j + 1) * ATTN_W]
            r = lax.rsqrt(_seg_sum64(z * z, bd_ref) * (1.0 / HEAD_DIM) + EPS)
            zhat = z * r
            dgain_ref[...] += jnp.sum(dn * zhat, axis=0, keepdims=True)
            gd = dn * gain_ref[...]
            parts.append((r * (gd - zhat * (_seg_sum64(gd * zhat, bd_ref) * (1.0 / HEAD_DIM)))).astype(BF16))
        parts.append(dv_ref[...].astype(BF16))
        dz = jnp.concatenate(parts, axis=1)

        r, xhat = _rms_stats(x_ref[...])
        g = g_ref[...]
        dw_ref[...] += _mm_tn((xhat * g).astype(BF16), dz)
        dh = _mm_nt(dz, w_ref[...])
        dg_ref[...] += jnp.sum(dh * xhat, axis=0, keepdims=True)
        dx_ref[...] = dx1_ref[...] + _rms_bwd(dh, xhat, r, g)

    def blk(c):
        return pl.BlockSpec((tm, c), lambda i: (i, 0))

    return pl.pallas_call(
        body, name="inproj_bwd", grid=(t // tm,),
        in_specs=[blk(ATTN_W)] * 3 + [blk(3 * CONV_W), blk(2 * ATTN_W), blk(D_MODEL), blk(D_MODEL), _full((1, D_MODEL)),
                                      _full((D_MODEL, IN_COLS)), _full((1, ATTN_W)), _full((1, ATTN_W)),
                                      _full((256, 256))],
        out_specs=[blk(D_MODEL), _full((D_MODEL, IN_COLS)), _full((1, D_MODEL)), _full((1, ATTN_W)),
                   _full((1, ATTN_W))],
        out_shape=[jax.ShapeDtypeStruct((t, D_MODEL), F32), jax.ShapeDtypeStruct((D_MODEL, IN_COLS), F32),
                   jax.ShapeDtypeStruct((1, D_MODEL), F32), jax.ShapeDtypeStruct((1, ATTN_W), F32),
                   jax.ShapeDtypeStruct((1, ATTN_W), F32)],
        compiler_params=_cparams("arbitrary"),
    )(dq, dk, dv, dzconv, zqk, x, dx1, g_mix, w_in, qg, kg, bd)


def _local_step(x, p, target, w, tms):
    bd = jnp.kron(jnp.eye(4, dtype=F32), jnp.ones((HEAD_DIM, HEAD_DIM), F32)).astype(BF16)
    qg = jnp.tile(w["q_norm_g"], (1, 8))
    kg = jnp.tile(w["k_norm_g"], (1, 8))
    slopes = jnp.exp2(-jnp.arange(1, 9, dtype=F32))
    slopes = jnp.broadcast_to(slopes.reshape(4, 2, 1), (4, 2, QK_BLOCK))

    zconv, zqk, yc, q, k, v = _inproj_fwd(x, w["g_mix"], w["w_in"], w["conv_w"], w["conv_b"], qg, kg, bd, tms[0])
    ya, lse = _attn_fwd(q, k, v, slopes)
    x1 = _outproj_fwd(ya, yc, x, w["g_out_conv"], w["g_out_attn"], w["w_out"], tms[0])
    gp, up, x2 = _ffn_fwd(x1, w["g_ffn"], w["w_gate"], w["w_up"], w["w_down"], w["ffn_conv_w"], w["ffn_conv_b"], tms[1])
    dx2, loss, dw_pg, dw_pp, dg_ple = _ple_fwd_bwd(x2, p, target, w["g_ple"], w["w_ple_gate"], w["w_ple_proj"], tms[0])
    dh2, dw_down, dw_up, dw_gate, dfcw, dfcb = _ffn_bwd(dx2, x1, w["g_ffn"], gp, up, w["w_gate"], w["w_up"],
                                                        w["w_down"], w["ffn_conv_w"], w["ffn_conv_b"], tms[1])
    dx1, dya, dd, dzconv, dw_out, dg_ffn, dgoc, dgoa, dcw, dcb = _outproj_bwd(
        dh2, dx2, x1, w["g_ffn"], w["w_out"], yc, ya, w["g_out_conv"], w["g_out_attn"], zconv, w["conv_w"],
        w["conv_b"], bd, tms[1])
    dq, dk, dv = _attn_bwd(q, k, v, dya, lse, dd, slopes)
    dx, dw_in, dg_mix, dqg, dkg = _inproj_bwd(dq, dk, dv, dzconv, zqk, x, dx1, w["g_mix"], w["w_in"], qg, kg, bd,
                                              tms[1])
    grads = {
        "g_mix": dg_mix, "w_in": dw_in, "conv_w": dcw, "conv_b": dcb,
        "q_norm_g": dqg.reshape(8, HEAD_DIM).sum(0, keepdims=True),
        "k_norm_g": dkg.reshape(8, HEAD_DIM).sum(0, keepdims=True),
        "g_out_conv": dgoc, "g_out_attn": dgoa, "w_out": dw_out, "g_ffn": dg_ffn, "w_gate": dw_gate, "w_up": dw_up,
        "ffn_conv_w": dfcw, "ffn_conv_b": dfcb, "w_down": dw_down, "g_ple": dg_ple, "w_ple_gate": dw_pg,
        "w_ple_proj": dw_pp,
    }
    return loss[0, 0], dx, grads


ANY = pl.BlockSpec(memory_space=pl.ANY)
MESH = pl.DeviceIdType.MESH


def _all_gather(shards, name):
    n = len(shards)

    def body(*refs):
        ins, outs = refs[:n], refs[n:2 * n]
        send_sems, recv_sems, local_sems = refs[2 * n:]
        x, y, c = lax.axis_index("x"), lax.axis_index("y"), lax.axis_index("c")
        me, sibling = (x, y, c), (x, y, 1 - c)
        chips = [(1 - x, y), (x, 1 - y), (1 - x, 1 - y)]

        def slot(dev):
            return 4 * dev[0] + 2 * dev[1] + dev[2]

        def copy(b, k, block, to, src=None):
            dst = outs[b].at[slot(block)]
            return pltpu.make_async_remote_copy(
                src_ref=dst if src is None else src, dst_ref=dst, send_sem=send_sems.at[b, k],
                recv_sem=recv_sems.at[b, k], device_id=to, device_id_type=MESH)

        mine = [pltpu.make_async_copy(ins[b], outs[b].at[slot(me)], local_sems.at[b]) for b in range(n)]
        first, passed = [], []
        for b in range(n):
            mine[b].start()
            first.append(copy(b, 0, me, sibling, src=ins[b]))
            first += [copy(b, 1 + j, me, (*chip, c), src=ins[b]) for j, chip in enumerate(chips)]
        for cp in first:
            cp.start()
        for j, chip in enumerate(chips):
            for b in range(n):
                copy(b, 1 + j, (*chip, c), me).wait_recv()
                fwd = copy(b, 4 + j, (*chip, c), sibling)
                fwd.start()
                passed.append(fwd)
        for b in range(n):
            copy(b, 0, sibling, me).wait_recv()
            for j, chip in enumerate(chips):
                copy(b, 4 + j, (*chip, 1 - c), me).wait_recv()
        for cp in first + passed:
            cp.wait_send()
        for cp in mine:
            cp.wait()

    return pl.pallas_call(
        body, name=name,
        in_specs=[ANY] * n, out_specs=[ANY] * n,
        out_shape=[jax.ShapeDtypeStruct((N_DEV,) + s.shape, s.dtype) for s in shards],
        scratch_shapes=[pltpu.SemaphoreType.DMA((n, 7)), pltpu.SemaphoreType.DMA((n, 7)),
                        pltpu.SemaphoreType.DMA((n,))],
    )(*shards)


def _sibling_exchange(g):
    def body(g_ref, land_ref, send_sems, recv_sems):
        x, y, c = lax.axis_index("x"), lax.axis_index("y"), lax.axis_index("c")
        copies = [pltpu.make_async_remote_copy(
            src_ref=g_ref.at[k, 1 - c], dst_ref=land_ref.at[k], send_sem=send_sems.at[k], recv_sem=recv_sems.at[k],
            device_id=(x, y, 1 - c), device_id_type=MESH) for k in range(N_CHIP)]
        for cp in copies:
            cp.start()
        for cp in copies:
            cp.wait()

    return pl.pallas_call(
        body, name="rs_sibling_exchange", in_specs=[ANY], out_specs=ANY,
        out_shape=jax.ShapeDtypeStruct((N_CHIP,) + g.shape[2:], g.dtype),
        scratch_shapes=[pltpu.SemaphoreType.DMA((N_CHIP,)), pltpu.SemaphoreType.DMA((N_CHIP,))],
    )(g)


def _pair_sum(g, land, core):
    rows, cols = land.shape[1:]
    tr = 216

    def body(c_ref, g_ref, l_ref, o_ref):
        o_ref[...] = g_ref[...] + l_ref[...]

    return pl.pallas_call(
        body, name="rs_pair_sum",
        grid_spec=pltpu.PrefetchScalarGridSpec(
            num_scalar_prefetch=1, grid=(N_CHIP, rows // tr),
            in_specs=[pl.BlockSpec((None, None, tr, cols), lambda k, i, c_ref: (k, c_ref[0], i, 0)),
                      pl.BlockSpec((None, tr, cols), lambda k, i, c_ref: (k, i, 0))],
            out_specs=pl.BlockSpec((None, tr, cols), lambda k, i, c_ref: (k, i, 0))),
        out_shape=jax.ShapeDtypeStruct(land.shape, land.dtype),
        compiler_params=_cparams("parallel", "parallel"),
    )(core, g, land)


def _chip_exchange(part):
    def body(p_ref, land_ref, send_sems, recv_sems, local_sem):
        x, y, c = lax.axis_index("x"), lax.axis_index("y"), lax.axis_index("c")
        mine = 2 * x + y
        chips = [(1 - x, y), (x, 1 - y), (1 - x, 1 - y)]
        own = pltpu.make_async_copy(p_ref.at[mine], land_ref.at[mine], local_sem)
        own.start()
        copies = [pltpu.make_async_remote_copy(
            src_ref=p_ref.at[2 * cx + cy], dst_ref=land_ref.at[mine], send_sem=send_sems.at[j],
            recv_sem=recv_sems.at[j], device_id=(cx, cy, c), device_id_type=MESH) for j, (cx, cy) in enumerate(chips)]
        for cp in copies:
            cp.start()
        for j, (cx, cy) in enumerate(chips):
            pltpu.make_async_remote_copy(
                src_ref=p_ref.at[mine], dst_ref=land_ref.at[2 * cx + cy], send_sem=send_sems.at[j],
                recv_sem=recv_sems.at[j], device_id=(cx, cy, c), device_id_type=MESH).wait_recv()
        for cp in copies:
            cp.wait_send()
        own.wait()

    return pl.pallas_call(
        body, name="rs_chip_exchange", in_specs=[ANY], out_specs=ANY,
        out_shape=jax.ShapeDtypeStruct(part.shape, part.dtype),
        scratch_shapes=[pltpu.SemaphoreType.DMA((3,)), pltpu.SemaphoreType.DMA((3,)), pltpu.SemaphoreType.DMA],
    )(part)


def _adamw(parts, w, m, v, name):
    k, rows, cols = parts.shape
    tr = 216 if rows % 216 == 0 else rows
    c1 = 1.0 / (1.0 - ADAM_B1 ** ADAM_STEP)
    c2 = 1.0 / (1.0 - ADAM_B2 ** ADAM_STEP)

    def body(p_ref, w_ref, m_ref, v_ref, g_ref, d_ref, nm_ref, nv_ref):
        g = p_ref[0]
        for j in range(1, k):
            g = g + p_ref[j]
        g_ref[...] = g
        nm = ADAM_B1 * m_ref[...] + (1.0 - ADAM_B1) * g
        nv = ADAM_B2 * v_ref[...] + (1.0 - ADAM_B2) * (g * g)
        nm_ref[...] = nm
        nv_ref[...] = nv
        d_ref[...] = -ADAM_LR * ((nm * c1) / (jnp.sqrt(nv * c2) + ADAM_EPS) + ADAM_WD * w_ref[...])

    blk = pl.BlockSpec((tr, cols), lambda i: (i, 0))
    return pl.pallas_call(
        body, name=name, grid=(rows // tr,),
        in_specs=[pl.BlockSpec((k, tr, cols), lambda i: (0, i, 0)), blk, blk, blk],
        out_specs=[blk] * 4, out_shape=[jax.ShapeDtypeStruct((rows, cols), F32)] * 4,
        compiler_params=_cparams("parallel"),
    )(parts, w, m, v)


COL_SHARDED = ("w_in", "w_gate", "w_up", "w_ple_proj")
REPLICATED = (("g_mix", 1024), ("conv_b", 512), ("q_norm_g", 64), ("k_norm_g", 64), ("g_out_conv", 512),
              ("g_out_attn", 512), ("g_ffn", 1024), ("ffn_conv_b", 2816), ("g_ple", 1024))
CONV_SHARDED = (("conv_w", CONV_W), ("ffn_conv_w", D_FF))


def _pack_rows(arrays):
    return jnp.concatenate([a.reshape(-1, 1024) for a in arrays], axis=0)


def _unpack_big(flat, shard_shapes):
    out, r0 = {}, 0
    for name, rows in BIG_ROWS:
        out[name] = flat[r0:r0 + rows].reshape(shard_shapes[name])
        r0 += rows
    return out


def _gathered_to_full(gathered, shard_shapes):
    out, r0 = {}, 0
    for name, rows in BIG_ROWS:
        a = gathered[:, r0:r0 + rows].reshape((N_DEV,) + shard_shapes[name])
        if name in COL_SHARDED:
            a = a.transpose(1, 0, 2)
            out[name] = a.reshape(a.shape[0], -1)
        else:
            out[name] = a.reshape(-1, a.shape[2])
        r0 += rows
    return out


def _full_to_stacked(grads, shard_shapes):
    parts = []
    for name, rows in BIG_ROWS:
        sr, sc = shard_shapes[name]
        a = grads[name]
        if name in COL_SHARDED:
            a = a.reshape(sr, N_DEV, sc).transpose(1, 0, 2)
        else:
            a = a.reshape(N_DEV, sr, sc)
        parts.append(a.reshape(N_DEV, rows, 1024))
    return jnp.concatenate(parts, axis=1)


def _pad_rows(vec, rows):
    return jnp.pad(vec, (0, rows * 1024 - vec.shape[0])).reshape(rows, 1024)


def kernel(x, p, g_mix, w_in, conv_w, conv_b, q_norm_g, k_norm_g, g_out_conv, g_out_attn, w_out, g_ffn, w_gate, w_up, ffn_conv_w, ffn_conv_b, w_down, g_ple, w_ple_gate, w_ple_proj, loss_target, m_g_mix, m_w_in, m_conv_w, m_conv_b, m_q_norm_g, m_k_norm_g, m_g_out_conv, m_g_out_attn, m_w_out, m_g_ffn, m_w_gate, m_w_up, m_ffn_conv_w, m_ffn_conv_b, m_w_down, m_g_ple, m_w_ple_gate, m_w_ple_proj, v_g_mix, v_w_in, v_conv_w, v_conv_b, v_q_norm_g, v_k_norm_g, v_g_out_conv, v_g_out_attn, v_w_out, v_g_ffn, v_w_gate, v_w_up, v_ffn_conv_w, v_ffn_conv_b, v_w_down, v_g_ple, v_w_ple_gate, v_w_ple_proj):
    args = dict(locals())
    names = ["g_mix", "w_in", "conv_w", "conv_b", "q_norm_g", "k_norm_g", "g_out_conv", "g_out_attn", "w_out", "g_ffn",
             "w_gate", "w_up", "ffn_conv_w", "ffn_conv_b", "w_down", "g_ple", "w_ple_gate", "w_ple_proj"]
    big = [n for n, _ in BIG_ROWS]
    conv = [n for n, _ in CONV_SHARDED]
    wts = {n: (args[n][0] if n in big or n in conv else args[n]) for n in names}
    mom = {n: (args["m_" + n][0] if n in big or n in conv else args["m_" + n]) for n in names}
    var = {n: (args["v_" + n][0] if n in big or n in conv else args["v_" + n]) for n in names}
    shard_shapes = {n: wts[n].shape for n in big}
    dev = 4 * lax.axis_index("x") + 2 * lax.axis_index("y") + lax.axis_index("c")
    core = lax.axis_index("c").astype(jnp.int32).reshape(1)

    conv_local = _pad_rows(jnp.concatenate([wts[n].reshape(-1) for n in conv]), 8).reshape(8, 1024)
    gathered, conv_all = _all_gather([_pack_rows([wts[n].astype(BF16) for n in big]), conv_local], "gather_weights")
    full = dict(wts)
    full.update(_gathered_to_full(gathered, shard_shapes))
    off = 0
    for n, width in CONV_SHARDED:
        sc = width // N_DEV
        a = conv_all.reshape(N_DEV, -1)[:, off:off + 3 * sc].reshape(N_DEV, 3, sc)
        full[n] = a.transpose(1, 0, 2).reshape(3, width)
        off += 3 * sc

    loss, dx, grads = _local_step(x[0], p[0, 0], loss_target[0], full, (512, 256))

    stacked = _full_to_stacked(grads, shard_shapes).reshape(N_CHIP, 2, BIG_TOTAL, 1024)
    landed = _sibling_exchange(stacked)
    contributions = _chip_exchange(_pair_sum(stacked, landed, core))

    small = jnp.concatenate([grads[n].reshape(-1) for n, _ in REPLICATED] + [grads[n].reshape(-1) for n in conv]
                            + [loss.reshape(1)])
    (small_all,) = _all_gather([_pad_rows(small, SMALL_ROWS)], "gather_small_grads")

    g_big, d_big, m_big, v_big = _adamw(contributions, _pack_rows([wts[n] for n in big]),
                                        _pack_rows([mom[n] for n in big]), _pack_rows([var[n] for n in big]),
                                        "adamw_large")
    n_rep = sum(s for _, s in REPLICATED)
    conv_sizes = [3 * w_ // N_DEV for _, w_ in CONV_SHARDED]

    def small_state(src):
        flat = jnp.concatenate([src[n].reshape(-1) for n, _ in REPLICATED] + [src[n].reshape(-1) for n in conv])
        return _pad_rows(flat, 16)

    rep_all = small_all.reshape(N_DEV, -1)[:, :n_rep]
    conv_parts, off = [], n_rep
    for (n, width), size in zip(CONV_SHARDED, conv_sizes):
        sc = width // N_DEV
        a = small_all.reshape(N_DEV, -1)[:, off:off + 3 * width].reshape(N_DEV, 3, width)
        conv_parts.append(lax.dynamic_slice(a, (0, 0, dev * sc), (N_DEV, 3, sc)).reshape(N_DEV, size))
        off += 3 * width
    loss_total = jnp.sum(small_all.reshape(N_DEV, -1)[:, off])
    small_parts = jnp.concatenate([rep_all] + conv_parts, axis=1)
    small_parts = jnp.pad(small_parts, ((0, 0), (0, 16 * 1024 - small_parts.shape[1]))).reshape(N_DEV, 16, 1024)
    g_sm, d_sm, m_sm, v_sm = _adamw(small_parts, small_state(wts), small_state(mom), small_state(var), "adamw_small")

    def unpack(big_flat, small_flat, like):
        out = _unpack_big(big_flat, shard_shapes)
        flat, o = small_flat.reshape(-1), 0
        for n, s in list(REPLICATED) + [(n, sz) for (n, _), sz in zip(CONV_SHARDED, conv_sizes)]:
            out[n] = flat[o:o + s]
            o += s
        return [out[n].reshape(like[n].shape) for n in names]

    like = {n: args[n] for n in names}
    return (loss_total, dx[None], *unpack(g_big, g_sm, like), *unpack(d_big, d_sm, like),
            *unpack(m_big, m_sm, like), *unpack(v_big, v_sm, like))
```

```python
import functools

import jax
import jax.numpy as jnp
from jax import lax
from jax.experimental import pallas as pl
from jax.experimental.pallas import tpu as pltpu

F32 = jnp.float32
BF16 = jnp.bfloat16

D_MODEL = 1024
CONV_W = 512
ATTN_W = 512
HEAD_DIM = 64
D_FF = 2816
PLE_DIM = 256
IN_COLS = 3 * CONV_W + 3 * ATTN_W
EPS = 1e-6
QK_BLOCK = 128
DILATIONS = (1, 4, 16)
ATTN_SCALE = HEAD_DIM ** -0.5

ADAM_LR = 0.001
ADAM_B1 = 0.9
ADAM_B2 = 0.999
ADAM_EPS = 1e-08
ADAM_WD = 0.01
ADAM_STEP = 10

N_DEV = 8
N_CHIP = 4
V7X_VMEM_LIMIT = 56 * 1024 * 1024
FF_CHUNKS = 2

BIG_ROWS = (("w_in", 384), ("w_out", 128), ("w_gate", 352), ("w_up", 352), ("w_down", 352),
            ("w_ple_gate", 128), ("w_ple_proj", 32))
BIG_TOTAL = sum(r for _, r in BIG_ROWS)
SMALL_ROWS = 24


def _cparams(*sem):
    return pltpu.CompilerParams(dimension_semantics=sem, vmem_limit_bytes=V7X_VMEM_LIMIT)


def _mm(a, b):
    return jnp.dot(a, b, preferred_element_type=F32)


def _mm_nt(a, b):
    return lax.dot_general(a, b, (((1,), (1,)), ((), ())), preferred_element_type=F32)


def _mm_tn(a, b):
    return lax.dot_general(a, b, (((0,), (0,)), ((), ())), preferred_element_type=F32)


def _full(shape):
    nd = len(shape)
    return pl.BlockSpec(shape, lambda *_: (0,) * nd)


def _rms_stats(x):
    r = lax.rsqrt(jnp.mean(x * x, axis=-1, keepdims=True) + EPS)
    return r, x * r


def _rms_bwd(dy, xhat, r, g):
    gd = dy * g
    return r * (gd - xhat * jnp.mean(gd * xhat, axis=-1, keepdims=True))


def _seg_sum64(v, bd_ref):
    outs = []
    for c in range(0, v.shape[1], 256):
        vc = v[:, c:c + 256]
        hi = vc.astype(BF16)
        lo = (vc - hi.astype(F32)).astype(BF16)
        outs.append(_mm(hi, bd_ref[...]) + _mm(lo, bd_ref[...]))
    return outs[0] if len(outs) == 1 else jnp.concatenate(outs, axis=1)


def _shift_rows(u, k, edge_rows):
    row = lax.broadcasted_iota(jnp.int32, u.shape, 0)
    out = pltpu.roll(u, k, 0)
    for j in range(k):
        out = jnp.where(row == j, edge_rows[k - 1 - j], out)
    return out


def _shift_rows_up(u, k, edge_rows):
    n = u.shape[0]
    row = lax.broadcasted_iota(jnp.int32, u.shape, 0)
    out = pltpu.roll(u, n - k, 0)
    for j in range(k):
        out = jnp.where(row == n - k + j, edge_rows[j], out)
    return out


def _conv_fwd(u, c1, c2, w_ref, b_ref):
    u1 = _shift_rows(u, 1, (c1,))
    u2 = _shift_rows(u, 2, (c1, c2))
    y = u2 * w_ref[0:1, :] + u1 * w_ref[1:2, :] + u * w_ref[2:3, :] + b_ref[...]
    return y, u1, u2


def _conv_bwd_input(dy, n1row, n2row, w_ref):
    d1 = _shift_rows_up(dy, 1, (n1row,))
    d2 = _shift_rows_up(dy, 2, (n1row, n2row))
    return dy * w_ref[2:3, :] + d1 * w_ref[1:2, :] + d2 * w_ref[0:1, :]


def _sigmoid(x):
    return 1.0 / (1.0 + jnp.exp(-x))


def _inproj_fwd(x, g_mix, w_in, conv_w, conv_b, qg, kg, bd, tm):
    t = x.shape[0]

    def body(x_ref, g_ref, w_ref, cw_ref, cb_ref, qg_ref, kg_ref, bd_ref,
             zc_ref, zqk_ref, yc_ref, q_ref, k_ref, v_ref, carry_ref):
        @pl.when(pl.program_id(0) == 0)
        def _():
            carry_ref[...] = jnp.zeros_like(carry_ref)

        _, xhat = _rms_stats(x_ref[...])
        h = (xhat * g_ref[...]).astype(BF16)
        zconv = _mm(h, w_ref[:, 0:3 * CONV_W])
        zc_ref[...] = zconv
        u = zconv[:, CONV_W:2 * CONV_W] * zconv[:, 2 * CONV_W:3 * CONV_W]
        cv, _, _ = _conv_fwd(u, carry_ref[7:8, :], carry_ref[6:7, :], cw_ref, cb_ref)
        yc_ref[...] = zconv[:, 0:CONV_W] * cv
        carry_ref[...] = u[tm - 8:tm, :]

        zqk = _mm(h, w_ref[:, 3 * CONV_W:3 * CONV_W + 2 * ATTN_W])
        zqk_ref[...] = zqk
        for j, (gain_ref, out_ref, scale) in enumerate(((qg_ref, q_ref, ATTN_SCALE), (kg_ref, k_ref, 1.0))):
            z = zqk[:, j * ATTN_W:(j + 1) * ATTN_W]
            r = lax.rsqrt(_seg_sum64(z * z, bd_ref) * (1.0 / HEAD_DIM) + EPS)
            out_ref[...] = z * r * gain_ref[...] * scale
        v_ref[...] = _mm(h, w_ref[:, 3 * CONV_W + 2 * ATTN_W:IN_COLS])

    def blk(c):
        return pl.BlockSpec((tm, c), lambda i: (i, 0))

    return pl.pallas_call(
        body, name="inproj_fwd", grid=(t // tm,),
        in_specs=[blk(D_MODEL), _full((1, D_MODEL)), _full((D_MODEL, IN_COLS)), _full((3, CONV_W)),
                  _full((1, CONV_W)), _full((1, ATTN_W)), _full((1, ATTN_W)), _full((256, 256))],
        out_specs=[blk(3 * CONV_W), blk(2 * ATTN_W), blk(CONV_W), blk(ATTN_W), blk(ATTN_W), blk(ATTN_W)],
        out_shape=[jax.ShapeDtypeStruct((t, 3 * CONV_W), F32), jax.ShapeDtypeStruct((t, 2 * ATTN_W), F32),
                   jax.ShapeDtypeStruct((t, CONV_W), F32), jax.ShapeDtypeStruct((t, ATTN_W), F32),
                   jax.ShapeDtypeStruct((t, ATTN_W), F32), jax.ShapeDtypeStruct((t, ATTN_W), F32)],
        scratch_shapes=[pltpu.VMEM((8, CONV_W), F32)],
        compiler_params=_cparams("arbitrary"),
    )(x, g_mix, w_in, conv_w, conv_b, qg, kg, bd)


SUPER = 16 * QK_BLOCK
KEYS = 2 * QK_BLOCK


def _rows(start, size, dil):
    return pl.ds(start, size) if dil == 1 else pl.ds(start, size, stride=dil)


def _attn_bias(sl_ref, dil):
    qi = lax.broadcasted_iota(jnp.int32, (KEYS, KEYS), 0)
    kj = lax.broadcasted_iota(jnp.int32, (KEYS, KEYS), 1)
    step = jnp.bitwise_and(qi, QK_BLOCK - 1) + QK_BLOCK - kj
    slope = jnp.where(qi < QK_BLOCK, sl_ref[0, 0:1, 0:1], sl_ref[0, 1:2, 0:1])
    bias = jnp.where(jnp.logical_and(step >= 0, step <= QK_BLOCK), -slope * (step * dil).astype(F32), -jnp.inf)
    return bias, kj >= QK_BLOCK


def _unit_start(u, dil):
    if dil == 1:
        return pl.multiple_of(u * QK_BLOCK, QK_BLOCK)
    if dil == 4:
        return jnp.bitwise_and(u, 3) + (u // 4) * (4 * QK_BLOCK)
    return u


def _stack_heads(a, head0):
    zero = jnp.zeros_like(a)
    return jnp.concatenate([jnp.where(head0, a, zero), jnp.where(head0, zero, a)], axis=0)


def _attn_fwd(q, k, v, slopes):
    t = q.shape[0]
    nsb = t // SUPER

    def body(q_ref, kc_ref, kp_ref, vc_ref, vp_ref, sl_ref, o_ref, l_ref, kk, vv, ob, lb):
        s = pl.program_id(1)
        kk[0:SUPER, :] = kp_ref[...]
        kk[SUPER:, :] = kc_ref[...]
        vv[0:SUPER, :] = vp_ref[...]
        vv[SUPER:, :] = vc_ref[...]
        head0 = lax.broadcasted_iota(jnp.int32, (QK_BLOCK, QK_BLOCK), 1) < HEAD_DIM

        for b, dil in enumerate(DILATIONS):
            bias, own_half = _attn_bias(sl_ref, dil)

            def unit(u, carry, b=b, dil=dil, bias=bias, own_half=own_half):
                start = _unit_start(u, dil)
                first_key = SUPER + start - QK_BLOCK * dil
                q2 = _stack_heads(q_ref[_rows(start, QK_BLOCK, dil), :].astype(BF16), head0)
                k2 = kk[_rows(first_key, KEYS, dil), :].astype(BF16)
                v2 = vv[_rows(first_key, KEYS, dil), :].astype(BF16)
                has_prev = jnp.logical_or(s > 0, start >= QK_BLOCK * dil)
                sc = jnp.where(jnp.logical_or(own_half, has_prev), _mm_nt(q2, k2) + bias, -jnp.inf)
                m = jnp.max(sc, axis=-1, keepdims=True)
                e = jnp.exp(sc - m)
                den = jnp.sum(e, axis=-1, keepdims=True)
                o2 = _mm(e.astype(BF16), v2) / den
                l2 = m + jnp.log(den)
                ob[b, _rows(start, QK_BLOCK, dil), :] = jnp.where(head0, o2[0:QK_BLOCK], o2[QK_BLOCK:])
                lb[b, _rows(start, QK_BLOCK, dil), :] = jnp.where(head0, l2[0:QK_BLOCK], l2[QK_BLOCK:])
                return carry

            lax.fori_loop(0, SUPER // QK_BLOCK, unit, 0)

        def merge(i, carry):
            rows = pl.ds(pl.multiple_of(i * 256, 256), 256)
            la, lb_, lc = lb[0, rows, :], lb[1, rows, :], lb[2, rows, :]
            mx = jnp.maximum(jnp.maximum(la, lb_), lc)
            wa, wb, wc = jnp.exp(la - mx), jnp.exp(lb_ - mx), jnp.exp(lc - mx)
            sw = wa + wb + wc
            o_ref[rows, :] = (wa * ob[0, rows, :] + wb * ob[1, rows, :] + wc * ob[2, rows, :]) / sw
            l_ref[rows, :] = mx + jnp.log(sw)
            return carry

        lax.fori_loop(0, SUPER // 256, merge, 0)

    cur = pl.BlockSpec((SUPER, QK_BLOCK), lambda p, s: (s, p))
    prev = pl.BlockSpec((SUPER, QK_BLOCK), lambda p, s: (jnp.maximum(s - 1, 0), p))
    return pl.pallas_call(
        body, name="attn_fwd", grid=(4, nsb),
        in_specs=[cur, cur, prev, cur, prev, pl.BlockSpec((1, 2, QK_BLOCK), lambda p, s: (p, 0, 0))],
        out_specs=[cur, cur],
        out_shape=[jax.ShapeDtypeStruct((t, ATTN_W), F32), jax.ShapeDtypeStruct((t, ATTN_W), F32)],
        scratch_shapes=[pltpu.VMEM((2 * SUPER, QK_BLOCK), F32), pltpu.VMEM((2 * SUPER, QK_BLOCK), F32),
                        pltpu.VMEM((3, SUPER, QK_BLOCK), F32), pltpu.VMEM((3, SUPER, QK_BLOCK), F32)],
        compiler_params=_cparams("parallel", "arbitrary"),
    )(q, k, k, v, v, slopes)


def _outproj_fwd(ya, yc, x, goc, goa, w_out, tm):
    t = x.shape[0]

    def body(ya_ref, yc_ref, x_ref, goc_ref, goa_ref, w_ref, x1_ref):
        _, ychat = _rms_stats(yc_ref[...])
        _, yahat = _rms_stats(ya_ref[...])
        acc = _mm((ychat * goc_ref[...]).astype(BF16), w_ref[0:CONV_W, :])
        acc += _mm((yahat * goa_ref[...]).astype(BF16), w_ref[CONV_W:, :])
        x1_ref[...] = x_ref[...] + acc

    def blk(c):
        return pl.BlockSpec((tm, c), lambda i: (i, 0))

    return pl.pallas_call(
        body, name="outproj_fwd", grid=(t // tm,),
        in_specs=[blk(ATTN_W), blk(CONV_W), blk(D_MODEL), _full((1, CONV_W)), _full((1, ATTN_W)),
                  _full((D_MODEL, D_MODEL))],
        out_specs=blk(D_MODEL),
        out_shape=jax.ShapeDtypeStruct((t, D_MODEL), F32),
        compiler_params=_cparams("parallel"),
    )(ya, yc, x, goc, goa, w_out)


def _ffn_fwd(x1, g_ffn, w_gate, w_up, w_down, fcw, fcb, tm):
    t = x1.shape[0]

    def body(x_ref, g_ref, wg_ref, wu_ref, wd_ref, cw_ref, cb_ref, gp_ref, up_ref, x2_ref, carry_ref):
        @pl.when(pl.program_id(0) == 0)
        def _():
            carry_ref[...] = jnp.zeros_like(carry_ref)

        xv = x_ref[...]
        _, xhat = _rms_stats(xv)
        h = (xhat * g_ref[...]).astype(BF16)
        gp = _mm(h, wg_ref[...])
        gp_ref[...] = gp
        gate, _, _ = _conv_fwd(gp, carry_ref[7:8, :], carry_ref[6:7, :], cw_ref, cb_ref)
        carry_ref[...] = gp[tm - 8:tm, :]
        up = _mm(h, wu_ref[...])
        up_ref[...] = up
        a = (gate * _sigmoid(gate) * up).astype(BF16)
        x2_ref[...] = xv + _mm(a, wd_ref[...])

    def blk(c):
        return pl.BlockSpec((tm, c), lambda i: (i, 0))

    return pl.pallas_call(
        body, name="ffn_fwd", grid=(t // tm,),
        in_specs=[blk(D_MODEL), _full((1, D_MODEL)), _full((D_MODEL, D_FF)), _full((D_MODEL, D_FF)),
                  _full((D_FF, D_MODEL)), _full((3, D_FF)), _full((1, D_FF))],
        out_specs=[blk(D_FF), blk(D_FF), blk(D_MODEL)],
        out_shape=[jax.ShapeDtypeStruct((t, D_FF), F32), jax.ShapeDtypeStruct((t, D_FF), F32),
                   jax.ShapeDtypeStruct((t, D_MODEL), F32)],
        scratch_shapes=[pltpu.VMEM((8, D_FF), F32)],
        compiler_params=_cparams("arbitrary"),
    )(x1, g_ffn, w_gate, w_up, w_down, fcw, fcb)


def _ple_fwd_bwd(x2, p, target, g_ple, w_pg, w_pp, tm):
    t = x2.shape[0]

    def body(x_ref, p_ref, t_ref, g_ref, wg_ref, wp_ref, dx_ref, loss_ref, dwg_ref, dwp_ref, dg_ref):
        @pl.when(pl.program_id(0) == 0)
        def _():
            loss_ref[...] = jnp.zeros_like(loss_ref)
            dwg_ref[...] = jnp.zeros_like(dwg_ref)
            dwp_ref[...] = jnp.zeros_like(dwp_ref)
            dg_ref[...] = jnp.zeros_like(dg_ref)

        xv = x_ref[...]
        r, xhat = _rms_stats(xv)
        g = g_ref[...]
        h = (xhat * g).astype(BF16)
        pg = _sigmoid(_mm(h, wg_ref[...]))
        pb = p_ref[...].astype(BF16)
        pp = _mm(pb, wp_ref[...])
        err = xv + pg * pp - t_ref[...]
        loss_ref[...] += 0.5 * jnp.sum(jnp.mean(err * err, axis=-1, keepdims=True))
        dx3 = err * (1.0 / D_MODEL)
        d_pp = (dx3 * pg).astype(BF16)
        d_pre = (dx3 * pp * pg * (1.0 - pg)).astype(BF16)
        dwp_ref[...] += _mm_tn(pb, d_pp)
        dwg_ref[...] += _mm_tn(h, d_pre)
        dh = _mm_nt(d_pre, wg_ref[...])
        dg_ref[...] += jnp.sum(dh * xhat, axis=0, keepdims=True)
        dx_ref[...] = dx3 + _rms_bwd(dh, xhat, r, g)

    def blk(c):
        return pl.BlockSpec((tm, c), lambda i: (i, 0))

    return pl.pallas_call(
        body, name="ple_fwd_bwd", grid=(t // tm,),
        in_specs=[blk(D_MODEL), blk(PLE_DIM), blk(D_MODEL), _full((1, D_MODEL)), _full((D_MODEL, D_MODEL)),
                  _full((PLE_DIM, D_MODEL))],
        out_specs=[blk(D_MODEL), _full((8, 128)), _full((D_MODEL, D_MODEL)), _full((PLE_DIM, D_MODEL)),
                   _full((1, D_MODEL))],
        out_shape=[jax.ShapeDtypeStruct((t, D_MODEL), F32), jax.ShapeDtypeStruct((8, 128), F32),
                   jax.ShapeDtypeStruct((D_MODEL, D_MODEL), F32), jax.ShapeDtypeStruct((PLE_DIM, D_MODEL), F32),
                   jax.ShapeDtypeStruct((1, D_MODEL), F32)],
        compiler_params=_cparams("arbitrary"),
    )(x2, p, target, g_ple, w_pg, w_pp)


def _ffn_bwd(dx2, x1, g_ffn, gp, up, w_gate, w_up, w_down, fcw, fcb, tm):
    t = x1.shape[0]
    nblk = t // tm
    fc = D_FF // FF_CHUNKS

    def body(dx_ref, x_ref, g_ref, gp_ref, gph_ref, up_ref, wg_ref, wu_ref, wd_ref, cw_ref, cb_ref,
             dh_ref, dwd_ref, dwu_ref, dwg_ref, dcw_ref, dcb_ref, carry_ref):
        i = pl.program_id(1)

        @pl.when(i == 0)
        def _():
            carry_ref[...] = jnp.zeros_like(carry_ref)
            dwd_ref[...] = jnp.zeros_like(dwd_ref)
            dwu_ref[...] = jnp.zeros_like(dwu_ref)
            dwg_ref[...] = jnp.zeros_like(dwg_ref)
            dcw_ref[...] = jnp.zeros_like(dcw_ref)
            dcb_ref[...] = jnp.zeros_like(dcb_ref)

        keep = (i < nblk - 1).astype(F32)
        dxb = dx_ref[...].astype(BF16)
        _, xhat = _rms_stats(x_ref[...])
        h = (xhat * g_ref[...]).astype(BF16)
        gp_v = gp_ref[...]
        gate, gp1, gp2 = _conv_fwd(gp_v, gph_ref[7:8, :] * keep, gph_ref[6:7, :] * keep, cw_ref, cb_ref)
        s = _sigmoid(gate)
        silu = gate * s
        up_v = up_ref[...]
        da = _mm_nt(dxb, wd_ref[...])
        dwd_ref[...] += _mm_tn((silu * up_v).astype(BF16), dxb)
        d_up = (da * silu).astype(BF16)
        d_gate = da * up_v * (s * (1.0 + gate * (1.0 - s)))
        dwu_ref[...] += _mm_tn(h, d_up)
        d_gp = _conv_bwd_input(d_gate, carry_ref[0:1, :], carry_ref[1:2, :], cw_ref).astype(BF16)
        carry_ref[...] = d_gate[0:8, :]
        dcw_ref[0:1, :] += jnp.sum(d_gate * gp2, axis=0, keepdims=True)
        dcw_ref[1:2, :] += jnp.sum(d_gate * gp1, axis=0, keepdims=True)
        dcw_ref[2:3, :] += jnp.sum(d_gate * gp_v, axis=0, keepdims=True)
        dcb_ref[...] += jnp.sum(d_gate, axis=0, keepdims=True)
        dwg_ref[...] += _mm_tn(h, d_gp)
        dh_ref[...] = _mm_nt(d_gp, wg_ref[...]) + _mm_nt(d_up, wu_ref[...])

    def rev(i):
        return nblk - 1 - i

    one = pl.Buffered(1)
    in_specs = [
        pl.BlockSpec((tm, D_MODEL), lambda j, i: (rev(i), 0)),
        pl.BlockSpec((tm, D_MODEL), lambda j, i: (rev(i), 0)),
        _full((1, D_MODEL)),
        pl.BlockSpec((tm, fc), lambda j, i: (rev(i), j)),
        pl.BlockSpec((8, fc), lambda j, i: (jnp.maximum(rev(i) * (tm // 8) - 1, 0), j)),
        pl.BlockSpec((tm, fc), lambda j, i: (rev(i), j)),
        pl.BlockSpec((D_MODEL, fc), lambda j, i: (0, j), pipeline_mode=one),
        pl.BlockSpec((D_MODEL, fc), lambda j, i: (0, j), pipeline_mode=one),
        pl.BlockSpec((fc, D_MODEL), lambda j, i: (j, 0), pipeline_mode=one),
        pl.BlockSpec((3, fc), lambda j, i: (0, j)),
        pl.BlockSpec((1, fc), lambda j, i: (0, j)),
    ]
    out_specs = [
        pl.BlockSpec((None, tm, D_MODEL), lambda j, i: (j, rev(i), 0)),
        pl.BlockSpec((fc, D_MODEL), lambda j, i: (j, 0), pipeline_mode=one),
        pl.BlockSpec((D_MODEL, fc), lambda j, i: (0, j), pipeline_mode=one),
        pl.BlockSpec((D_MODEL, fc), lambda j, i: (0, j), pipeline_mode=one),
        pl.BlockSpec((3, fc), lambda j, i: (0, j)),
        pl.BlockSpec((1, fc), lambda j, i: (0, j)),
    ]
    return pl.pallas_call(
        body, name="ffn_bwd", grid=(FF_CHUNKS, nblk), in_specs=in_specs, out_specs=out_specs,
        out_shape=[jax.ShapeDtypeStruct((FF_CHUNKS, t, D_MODEL), F32), jax.ShapeDtypeStruct((D_FF, D_MODEL), F32),
                   jax.ShapeDtypeStruct((D_MODEL, D_FF), F32), jax.ShapeDtypeStruct((D_MODEL, D_FF), F32),
                   jax.ShapeDtypeStruct((3, D_FF), F32), jax.ShapeDtypeStruct((1, D_FF), F32)],
        scratch_shapes=[pltpu.VMEM((8, fc), F32)],
        compiler_params=_cparams("arbitrary", "arbitrary"),
    )(dx2, x1, g_ffn, gp, gp, up, w_gate, w_up, w_down, fcw, fcb)


def _outproj_bwd(dh2, dx2, x1, g_ffn, w_out, yc, ya, goc, goa, zconv, conv_w, conv_b, bd, tm):
    t = x1.shape[0]
    nblk = t // tm

    def body(dh_ref, dx2_ref, x1_ref, g_ref, w_ref, yc_ref, ya_ref, goc_ref, goa_ref, zc_ref, zch_ref, cw_ref, cb_ref,
             bd_ref, dx1_ref, dya_ref, dd_ref, dzc_ref, dw_ref, dg_ref, dgoc_ref, dgoa_ref, dcw_ref, dcb_ref,
             carry_ref):
        i = pl.program_id(0)

        @pl.when(i == 0)
        def _():
            carry_ref[...] = jnp.zeros_like(carry_ref)
            for ref in (dw_ref, dg_ref, dgoc_ref, dgoa_ref, dcw_ref, dcb_ref):
                ref[...] = jnp.zeros_like(ref)

        keep = (i < nblk - 1).astype(F32)
        dh2_v = dh_ref[0]
        for j in range(1, FF_CHUNKS):
            dh2_v = dh2_v + dh_ref[j]
        r, xhat = _rms_stats(x1_ref[...])
        dg_ref[...] += jnp.sum(dh2_v * xhat, axis=0, keepdims=True)
        dx1 = dx2_ref[...] + _rms_bwd(dh2_v, xhat, r, g_ref[...])
        dx1_ref[...] = dx1
        dx1b = dx1.astype(BF16)
        dy = _mm_nt(dx1b, w_ref[...])

        yc_v = yc_ref[...]
        rc, ychat = _rms_stats(yc_v)
        dw_ref[0:CONV_W, :] += _mm_tn((ychat * goc_ref[...]).astype(BF16), dx1b)
        dyc = dy[:, 0:CONV_W]
        dgoc_ref[...] += jnp.sum(dyc * ychat, axis=0, keepdims=True)
        d_yc = _rms_bwd(dyc, ychat, rc, goc_ref[...])

        ya_v = ya_ref[...]
        ra, yahat = _rms_stats(ya_v)
        dw_ref[CONV_W:, :] += _mm_tn((yahat * goa_ref[...]).astype(BF16), dx1b)
        dya = dy[:, CONV_W:]
        dgoa_ref[...] += jnp.sum(dya * yahat, axis=0, keepdims=True)
        d_ya = _rms_bwd(dya, yahat, ra, goa_ref[...])
        dya_ref[...] = d_ya
        dd_ref[...] = _seg_sum64(d_ya * ya_v, bd_ref)

        zb = zc_ref[:, 0:CONV_W]
        zc = zc_ref[:, CONV_W:2 * CONV_W]
        zx = zc_ref[:, 2 * CONV_W:3 * CONV_W]
        u = zc * zx
        uh = zch_ref[:, CONV_W:2 * CONV_W] * zch_ref[:, 2 * CONV_W:3 * CONV_W] * keep
        cv, u1, u2 = _conv_fwd(u, uh[7:8, :], uh[6:7, :], cw_ref, cb_ref)
        d_cv = d_yc * zb
        d_u = _conv_bwd_input(d_cv, carry_ref[0:1, :], carry_ref[1:2, :], cw_ref)
        carry_ref[...] = d_cv[0:8, :]
        dcw_ref[0:1, :] += jnp.sum(d_cv * u2, axis=0, keepdims=True)
        dcw_ref[1:2, :] += jnp.sum(d_cv * u1, axis=0, keepdims=True)
        dcw_ref[2:3, :] += jnp.sum(d_cv * u, axis=0, keepdims=True)
        dcb_ref[...] += jnp.sum(d_cv, axis=0, keepdims=True)
        dzc_ref[:, 0:CONV_W] = d_yc * cv
        dzc_ref[:, CONV_W:2 * CONV_W] = d_u * zx
        dzc_ref[:, 2 * CONV_W:3 * CONV_W] = d_u * zc

    def rev(i):
        return nblk - 1 - i

    def blk(c):
        return pl.BlockSpec((tm, c), lambda i: (rev(i), 0))

    in_specs = [
        pl.BlockSpec((FF_CHUNKS, tm, D_MODEL), lambda i: (0, rev(i), 0)),
        blk(D_MODEL), blk(D_MODEL), _full((1, D_MODEL)), _full((D_MODEL, D_MODEL)),
        blk(CONV_W), blk(ATTN_W), _full((1, CONV_W)), _full((1, ATTN_W)),
        blk(3 * CONV_W),
        pl.BlockSpec((8, 3 * CONV_W), lambda i: (jnp.maximum(rev(i) * (tm // 8) - 1, 0), 0)),
        _full((3, CONV_W)), _full((1, CONV_W)), _full((256, 256)),
    ]
    out_specs = [blk(D_MODEL), blk(ATTN_W), blk(ATTN_W), blk(3 * CONV_W), _full((D_MODEL, D_MODEL)),
                 _full((1, D_MODEL)), _full((1, CONV_W)), _full((1, ATTN_W)), _full((3, CONV_W)), _full((1, CONV_W))]
    return pl.pallas_call(
        body, name="outproj_bwd", grid=(nblk,), in_specs=in_specs, out_specs=out_specs,
        out_shape=[jax.ShapeDtypeStruct((t, D_MODEL), F32), jax.ShapeDtypeStruct((t, ATTN_W), F32),
                   jax.ShapeDtypeStruct((t, ATTN_W), F32), jax.ShapeDtypeStruct((t, 3 * CONV_W), F32),
                   jax.ShapeDtypeStruct((D_MODEL, D_MODEL), F32), jax.ShapeDtypeStruct((1, D_MODEL), F32),
                   jax.ShapeDtypeStruct((1, CONV_W), F32), jax.ShapeDtypeStruct((1, ATTN_W), F32),
                   jax.ShapeDtypeStruct((3, CONV_W), F32), jax.ShapeDtypeStruct((1, CONV_W), F32)],
        scratch_shapes=[pltpu.VMEM((8, CONV_W), F32)],
        compiler_params=_cparams("arbitrary"),
    )(dh2, dx2, x1, g_ffn, w_out, yc, ya, goc, goa, zconv, zconv, conv_w, conv_b, bd)


def _attn_bwd(q, k, v, dya, lse, dd, slopes):
    t = q.shape[0]
    nsb = t // SUPER

    def body(q_ref, kc_ref, kp_ref, vc_ref, vp_ref, dy_ref, l_ref, d_ref, sl_ref, dq_ref, dk_ref, dv_ref,
             kk, vv, dkacc, dvacc):
        s = pl.program_id(1)

        @pl.when(s == 0)
        def _():
            dkacc[...] = jnp.zeros_like(dkacc)
            dvacc[...] = jnp.zeros_like(dvacc)

        dkacc[0:SUPER, :] = dkacc[SUPER:, :]
        dvacc[0:SUPER, :] = dvacc[SUPER:, :]
        dkacc[SUPER:, :] = jnp.zeros((SUPER, QK_BLOCK), F32)
        dvacc[SUPER:, :] = jnp.zeros((SUPER, QK_BLOCK), F32)

        @pl.when(s < nsb)
        def _():
            kk[0:SUPER, :] = kp_ref[...]
            kk[SUPER:, :] = kc_ref[...]
            vv[0:SUPER, :] = vp_ref[...]
            vv[SUPER:, :] = vc_ref[...]
            head0 = lax.broadcasted_iota(jnp.int32, (QK_BLOCK, QK_BLOCK), 1) < HEAD_DIM

            for b, dil in enumerate(DILATIONS):
                bias, own_half = _attn_bias(sl_ref, dil)

                def unit(u, carry, b=b, dil=dil, bias=bias, own_half=own_half):
                    start = _unit_start(u, dil)
                    first_key = SUPER + start - QK_BLOCK * dil
                    qrows = _rows(start, QK_BLOCK, dil)
                    krows = _rows(first_key, KEYS, dil)
                    q2 = _stack_heads(q_ref[qrows, :].astype(BF16), head0)
                    dy2 = _stack_heads(dy_ref[qrows, :].astype(BF16), head0)
                    lv, dv_ = l_ref[qrows, :], d_ref[qrows, :]
                    l2 = jnp.concatenate([lv[:, 0:1], lv[:, HEAD_DIM:HEAD_DIM + 1]], axis=0)
                    d2 = jnp.concatenate([dv_[:, 0:1], dv_[:, HEAD_DIM:HEAD_DIM + 1]], axis=0)
                    k2 = kk[krows, :].astype(BF16)
                    v2 = vv[krows, :].astype(BF16)
                    has_prev = jnp.logical_or(s > 0, start >= QK_BLOCK * dil)
                    sc = jnp.where(jnp.logical_or(own_half, has_prev), _mm_nt(q2, k2) + bias, -jnp.inf)
                    prob = jnp.exp(sc - l2)
                    ds = (prob * (_mm_nt(dy2, v2) - d2)).astype(BF16)
                    dvacc[krows, :] += _mm_tn(prob.astype(BF16), dy2)
                    dkacc[krows, :] += _mm_tn(ds, q2)
                    dq2 = _mm(ds, k2)
                    dq = jnp.where(head0, dq2[0:QK_BLOCK], dq2[QK_BLOCK:]) * ATTN_SCALE
                    if b == 0:
                        dq_ref[qrows, :] = dq
                    else:
                        dq_ref[qrows, :] += dq
                    return carry

                lax.fori_loop(0, SUPER // QK_BLOCK, unit, 0)

        dk_ref[...] = dkacc[0:SUPER, :]
        dv_ref[...] = dvacc[0:SUPER, :]

    def cur_map(p, s):
        return (jnp.minimum(s, nsb - 1), p)

    def prev_map(p, s):
        return (jnp.clip(s - 1, 0, nsb - 1), p)

    cur = pl.BlockSpec((SUPER, QK_BLOCK), cur_map)
    prev = pl.BlockSpec((SUPER, QK_BLOCK), prev_map)
    return pl.pallas_call(
        body, name="attn_bwd", grid=(4, nsb + 1),
        in_specs=[cur, cur, prev, cur, prev, cur, cur, cur, pl.BlockSpec((1, 2, QK_BLOCK), lambda p, s: (p, 0, 0))],
        out_specs=[cur, prev, prev],
        out_shape=[jax.ShapeDtypeStruct((t, ATTN_W), F32)] * 3,
        scratch_shapes=[pltpu.VMEM((2 * SUPER, QK_BLOCK), F32)] * 4,
        compiler_params=_cparams("parallel", "arbitrary"),
    )(q, k, k, v, v, dya, lse, dd, slopes)


def _attn_bwd_per_branch_unused(q, k, v, dya, lse, dd, slopes, dil):
    t = q.shape[0]
    length = t // dil
    chunk = _attn_chunk(t, dil)
    nch = length // chunk
    nb = chunk // QK_BLOCK
    nblocks = length // QK_BLOCK
    view = (length, dil * ATTN_W)
    ext = chunk + QK_BLOCK

    def body(q_ref, dy_ref, l_ref, d_ref, k_ref, v_ref, qn_ref, dyn_ref, ln_ref, dn_ref, kh_ref, vh_ref, sl_ref,
             dq_ref, dk_ref, dv_ref, qbuf, dybuf, lbuf, dbuf, kbuf, vbuf, dkacc, dvacc):
        c = pl.program_id(2)
        qbuf[0:chunk, :] = q_ref[...]
        qbuf[chunk:, :] = qn_ref[...]
        dybuf[0:chunk, :] = dy_ref[...].astype(BF16)
        dybuf[chunk:, :] = dyn_ref[...].astype(BF16)
        lbuf[0:chunk, :] = l_ref[...]
        lbuf[chunk:, :] = ln_ref[...]
        dbuf[0:chunk, :] = d_ref[...]
        dbuf[chunk:, :] = dn_ref[...]
        kbuf[0:QK_BLOCK, :] = kh_ref[...]
        kbuf[QK_BLOCK:, :] = k_ref[...]
        vbuf[0:QK_BLOCK, :] = vh_ref[...]
        vbuf[QK_BLOCK:, :] = v_ref[...]
        valid_cur, valid_prev, dist_cur, dist_prev, head0 = _attn_masks(dil)

        def pair(qb, dyb, lv, dv_, kb, vb, valid, dist):
            dq = jnp.zeros((QK_BLOCK, QK_BLOCK), F32)
            dk = jnp.zeros((QK_BLOCK, QK_BLOCK), F32)
            dvv = jnp.zeros((QK_BLOCK, QK_BLOCK), F32)
            for hh in range(2):
                sl = sl_ref[0, hh:hh + 1, :]
                hm = head0 if hh == 0 else jnp.logical_not(head0)
                col = hh * HEAD_DIM
                qm = jnp.where(hm, qb, jnp.zeros_like(qb))
                dym = jnp.where(hm, dyb, jnp.zeros_like(dyb))
                s = jnp.where(valid, _mm_nt(qm, kb) - sl * dist, -jnp.inf)
                prob = jnp.exp(s - lv[:, col:col + 1])
                ds = (prob * (_mm_nt(dym, vb) - dv_[:, col:col + 1])).astype(BF16)
                dvv += _mm_tn(prob.astype(BF16), dym)
                dk += _mm_tn(ds, qm)
                dq += jnp.where(hm, _mm(ds, kb), 0.0)
            return dq, dk, dvv

        def blk(j, carry):
            off = pl.multiple_of(j * QK_BLOCK, QK_BLOCK)
            nxt = pl.multiple_of(off + QK_BLOCK, QK_BLOCK)
            qb = qbuf[pl.ds(off, QK_BLOCK), :]
            dyb = dybuf[pl.ds(off, QK_BLOCK), :]
            lv = lbuf[pl.ds(off, QK_BLOCK), :]
            dv_ = dbuf[pl.ds(off, QK_BLOCK), :]
            dq_c, dk_c, dv_c = pair(qb, dyb, lv, dv_, kbuf[pl.ds(nxt, QK_BLOCK), :], vbuf[pl.ds(nxt, QK_BLOCK), :],
                                    valid_cur, dist_cur)
            dkacc[pl.ds(nxt, QK_BLOCK), :] = dk_c
            dvacc[pl.ds(nxt, QK_BLOCK), :] = dv_c
            has_prev = jnp.logical_or(c > 0, j > 0)
            dq_p, dk_p, dv_p = pair(qb, dyb, lv, dv_, kbuf[pl.ds(off, QK_BLOCK), :], vbuf[pl.ds(off, QK_BLOCK), :],
                                    jnp.logical_and(valid_prev, has_prev), dist_prev)

            @pl.when(j > 0)
            def _():
                dkacc[pl.ds(off, QK_BLOCK), :] += dk_p
                dvacc[pl.ds(off, QK_BLOCK), :] += dv_p

            dq_ref[pl.ds(off, QK_BLOCK), :] = (dq_c + dq_p) * ATTN_SCALE
            return carry

        lax.fori_loop(0, nb, blk, 0)

        @pl.when(c < nch - 1)
        def _():
            _, dk_p, dv_p = pair(qbuf[chunk:, :], dybuf[chunk:, :], lbuf[chunk:, :], dbuf[chunk:, :],
                                 kbuf[chunk:, :], vbuf[chunk:, :], valid_prev, dist_prev)
            dkacc[chunk:, :] += dk_p
            dvacc[chunk:, :] += dv_p

        dk_ref[...] = dkacc[QK_BLOCK:, :]
        dv_ref[...] = dvacc[QK_BLOCK:, :]

    def cmap(p, r, c):
        return (c, r * 4 + p)

    def before(p, r, c):
        return (jnp.maximum(c * nb - 1, 0), r * 4 + p)

    def after(p, r, c):
        return (jnp.minimum((c + 1) * nb, nblocks - 1), r * 4 + p)

    main = pl.BlockSpec((chunk, QK_BLOCK), cmap)
    hb = pl.BlockSpec((QK_BLOCK, QK_BLOCK), before)
    ha = pl.BlockSpec((QK_BLOCK, QK_BLOCK), after)
    qv, kv, vv = q.reshape(view), k.reshape(view), v.reshape(view)
    dyv, lv, ddv = dya.reshape(view), lse.reshape(view), dd.reshape(view)
    outs = pl.pallas_call(
        body, name=f"attn_bwd_d{dil}", grid=(4, dil, nch),
        in_specs=[main] * 6 + [ha] * 4 + [hb] * 2 + [pl.BlockSpec((1, 2, QK_BLOCK), lambda p, r, c: (p, 0, 0))],
        out_specs=[main] * 3,
        out_shape=[jax.ShapeDtypeStruct(view, F32)] * 3,
        scratch_shapes=[pltpu.VMEM((ext, QK_BLOCK), BF16), pltpu.VMEM((ext, QK_BLOCK), BF16),
                        pltpu.VMEM((ext, QK_BLOCK), F32), pltpu.VMEM((ext, QK_BLOCK), F32),
                        pltpu.VMEM((ext, QK_BLOCK), BF16), pltpu.VMEM((ext, QK_BLOCK), BF16),
                        pltpu.VMEM((ext, QK_BLOCK), F32), pltpu.VMEM((ext, QK_BLOCK), F32)],
        compiler_params=_cparams("arbitrary", "arbitrary", "arbitrary"),
    )(qv, dyv, lv, ddv, kv, vv, qv, dyv, lv, ddv, kv, vv, slopes)
    return [o.reshape(t, ATTN_W) for o in outs]


def _inproj_bwd(dq, dk, dv, dzconv, zqk, x, dx1, g_mix, w_in, qg, kg, bd, tm):
    t = x.shape[0]

    def body(dq_ref, dk_ref, dv_ref, dzc_ref, zqk_ref, x_ref, dx1_ref, g_ref, w_ref, qg_ref,
             kg_ref, bd_ref, dx_ref, dw_ref, dg_ref, dqg_ref, dkg_ref):
        @pl.when(pl.program_id(0) == 0)
        def _():
            for ref in (dw_ref, dg_ref, dqg_ref, dkg_ref):
                ref[...] = jnp.zeros_like(ref)

        parts = [dzc_ref[...].astype(BF16)]
        for j, (dn_ref, gain_ref, dgain_ref) in enumerate(((dq_ref, qg_ref, dqg_ref), (dk_ref, kg_ref, dkg_ref))):
            dn = dn_ref[...]
            z = zqk_ref[:, j * ATTN_W:(j + 1) * ATTN_W]
            r = lax.rsqrt(_seg_sum64(z * z, bd_ref) * (1.0 / HEAD_DIM) + EPS)
            zhat = z * r
            dgain_ref[...] += jnp.sum(dn * zhat, axis=0, keepdims=True)
            gd = dn * gain_ref[...]
            parts.append((r * (gd - zhat * (_seg_sum64(gd * zhat, bd_ref) * (1.0 / HEAD_DIM)))).astype(BF16))
        parts.append(dv_ref[...].astype(BF16))
        dz = jnp.concatenate(parts, axis=1)

        r, xhat = _rms_stats(x_ref[...])
        g = g_ref[...]
        dw_ref[...] += _mm_tn((xhat * g).astype(BF16), dz)
        dh = _mm_nt(dz, w_ref[...])
        dg_ref[...] += jnp.sum(dh * xhat, axis=0, keepdims=True)
        dx_ref[...] = dx1_ref[...] + _rms_bwd(dh, xhat, r, g)

    def blk(c):
        return pl.BlockSpec((tm, c), lambda i: (i, 0))

    return pl.pallas_call(
        body, name="inproj_bwd", grid=(t // tm,),
        in_specs=[blk(ATTN_W)] * 3 + [blk(3 * CONV_W), blk(2 * ATTN_W), blk(D_MODEL), blk(D_MODEL), _full((1, D_MODEL)),
                                      _full((D_MODEL, IN_COLS)), _full((1, ATTN_W)), _full((1, ATTN_W)),
                                      _full((256, 256))],
        out_specs=[blk(D_MODEL), _full((D_MODEL, IN_COLS)), _full((1, D_MODEL)), _full((1, ATTN_W)),
                   _full((1, ATTN_W))],
        out_shape=[jax.ShapeDtypeStruct((t, D_MODEL), F32), jax.ShapeDtypeStruct((D_MODEL, IN_COLS), F32),
                   jax.ShapeDtypeStruct((1, D_MODEL), F32), jax.ShapeDtypeStruct((1, ATTN_W), F32),
                   jax.ShapeDtypeStruct((1, ATTN_W), F32)],
        compiler_params=_cparams("arbitrary"),
    )(dq, dk, dv, dzconv, zqk, x, dx1, g_mix, w_in, qg, kg, bd)


def _local_step(x, p, target, w, tms):
    bd = jnp.kron(jnp.eye(4, dtype=F32), jnp.ones((HEAD_DIM, HEAD_DIM), F32)).astype(BF16)
    qg = jnp.tile(w["q_norm_g"], (1, 8))
    kg = jnp.tile(w["k_norm_g"], (1, 8))
    slopes = jnp.exp2(-jnp.arange(1, 9, dtype=F32))
    slopes = jnp.broadcast_to(slopes.reshape(4, 2, 1), (4, 2, QK_BLOCK))

    zconv, zqk, yc, q, k, v = _inproj_fwd(x, w["g_mix"], w["w_in"], w["conv_w"], w["conv_b"], qg, kg, bd, tms[0])
    ya, lse = _attn_fwd(q, k, v, slopes)
    x1 = _outproj_fwd(ya, yc, x, w["g_out_conv"], w["g_out_attn"], w["w_out"], tms[0])
    gp, up, x2 = _ffn_fwd(x1, w["g_ffn"], w["w_gate"], w["w_up"], w["w_down"], w["ffn_conv_w"], w["ffn_conv_b"], tms[1])
    dx2, loss, dw_pg, dw_pp, dg_ple = _ple_fwd_bwd(x2, p, target, w["g_ple"], w["w_ple_gate"], w["w_ple_proj"], tms[0])
    dh2, dw_down, dw_up, dw_gate, dfcw, dfcb = _ffn_bwd(dx2, x1, w["g_ffn"], gp, up, w["w_gate"], w["w_up"],
                                                        w["w_down"], w["ffn_conv_w"], w["ffn_conv_b"], tms[1])
    dx1, dya, dd, dzconv, dw_out, dg_ffn, dgoc, dgoa, dcw, dcb = _outproj_bwd(
        dh2, dx2, x1, w["g_ffn"], w["w_out"], yc, ya, w["g_out_conv"], w["g_out_attn"], zconv, w["conv_w"],
        w["conv_b"], bd, tms[1])
    dq, dk, dv = _attn_bwd(q, k, v, dya, lse, dd, slopes)
    dx, dw_in, dg_mix, dqg, dkg = _inproj_bwd(dq, dk, dv, dzconv, zqk, x, dx1, w["g_mix"], w["w_in"], qg, kg, bd,
                                              tms[1])
    grads = {
        "g_mix": dg_mix, "w_in": dw_in, "conv_w": dcw, "conv_b": dcb,
        "q_norm_g": dqg.reshape(8, HEAD_DIM).sum(0, keepdims=True),
        "k_norm_g": dkg.reshape(8, HEAD_DIM).sum(0, keepdims=True),
        "g_out_conv": dgoc, "g_out_attn": dgoa, "w_out": dw_out, "g_ffn": dg_ffn, "w_gate": dw_gate, "w_up": dw_up,
        "ffn_conv_w": dfcw, "ffn_conv_b": dfcb, "w_down": dw_down, "g_ple": dg_ple, "w_ple_gate": dw_pg,
        "w_ple_proj": dw_pp,
    }
    return loss[0, 0], dx, grads


ANY = pl.BlockSpec(memory_space=pl.ANY)
MESH = pl.DeviceIdType.MESH


def _all_gather(shards, name):
    n = len(shards)

    def body(*refs):
        ins, outs = refs[:n], refs[n:2 * n]
        send_sems, recv_sems, local_sems = refs[2 * n:]
        x, y, c = lax.axis_index("x"), lax.axis_index("y"), lax.axis_index("c")
        me, sibling = (x, y, c), (x, y, 1 - c)
        chips = [(1 - x, y), (x, 1 - y), (1 - x, 1 - y)]

        def slot(dev):
            return 4 * dev[0] + 2 * dev[1] + dev[2]

        def copy(b, k, block, to, src=None):
            dst = outs[b].at[slot(block)]
            return pltpu.make_async_remote_copy(
                src_ref=dst if src is None else src, dst_ref=dst, send_sem=send_sems.at[b, k],
                recv_sem=recv_sems.at[b, k], device_id=to, device_id_type=MESH)

        mine = [pltpu.make_async_copy(ins[b], outs[b].at[slot(me)], local_sems.at[b]) for b in range(n)]
        first, passed = [], []
        for b in range(n):
            mine[b].start()
            first.append(copy(b, 0, me, sibling, src=ins[b]))
            first += [copy(b, 1 + j, me, (*chip, c), src=ins[b]) for j, chip in enumerate(chips)]
        for cp in first:
            cp.start()
        for j, chip in enumerate(chips):
            for b in range(n):
                copy(b, 1 + j, (*chip, c), me).wait_recv()
                fwd = copy(b, 4 + j, (*chip, c), sibling)
                fwd.start()
                passed.append(fwd)
        for b in range(n):
            copy(b, 0, sibling, me).wait_recv()
            for j, chip in enumerate(chips):
                copy(b, 4 + j, (*chip, 1 - c), me).wait_recv()
        for cp in first + passed:
            cp.wait_send()
        for cp in mine:
            cp.wait()

    return pl.pallas_call(
        body, name=name,
        in_specs=[ANY] * n, out_specs=[ANY] * n,
        out_shape=[jax.ShapeDtypeStruct((N_DEV,) + s.shape, s.dtype) for s in shards],
        scratch_shapes=[pltpu.SemaphoreType.DMA((n, 7)), pltpu.SemaphoreType.DMA((n, 7)),
                        pltpu.SemaphoreType.DMA((n,))],
    )(*shards)


def _sibling_exchange(g):
    def body(g_ref, land_ref, send_sems, recv_sems):
        x, y, c = lax.axis_index("x"), lax.axis_index("y"), lax.axis_index("c")
        copies = [pltpu.make_async_remote_copy(
            src_ref=g_ref.at[k, 1 - c], dst_ref=land_ref.at[k], send_sem=send_sems.at[k], recv_sem=recv_sems.at[k],
            device_id=(x, y, 1 - c), device_id_type=MESH) for k in range(N_CHIP)]
        for cp in copies:
            cp.start()
        for cp in copies:
            cp.wait()

    return pl.pallas_call(
        body, name="rs_sibling_exchange", in_specs=[ANY], out_specs=ANY,
        out_shape=jax.ShapeDtypeStruct((N_CHIP,) + g.shape[2:], g.dtype),
        scratch_shapes=[pltpu.SemaphoreType.DMA((N_CHIP,)), pltpu.SemaphoreType.DMA((N_CHIP,))],
    )(g)


def _pair_sum(g, land, core):
    rows, cols = land.shape[1:]
    tr = 432

    def body(c_ref, g_ref, l_ref, o_ref):
        o_ref[...] = (g_ref[...].astype(F32) + l_ref[...].astype(F32)).astype(o_ref.dtype)

    return pl.pallas_call(
        body, name="rs_pair_sum",
        grid_spec=pltpu.PrefetchScalarGridSpec(
            num_scalar_prefetch=1, grid=(N_CHIP, rows // tr),
            in_specs=[pl.BlockSpec((None, None, tr, cols), lambda k, i, c_ref: (k, c_ref[0], i, 0)),
                      pl.BlockSpec((None, tr, cols), lambda k, i, c_ref: (k, i, 0))],
            out_specs=pl.BlockSpec((None, tr, cols), lambda k, i, c_ref: (k, i, 0))),
        out_shape=jax.ShapeDtypeStruct(land.shape, land.dtype),
        compiler_params=_cparams("parallel", "parallel"),
    )(core, g, land)


def _chip_exchange(part):
    def body(p_ref, land_ref, send_sems, recv_sems, local_sem):
        x, y, c = lax.axis_index("x"), lax.axis_index("y"), lax.axis_index("c")
        mine = 2 * x + y
        chips = [(1 - x, y), (x, 1 - y), (1 - x, 1 - y)]
        own = pltpu.make_async_copy(p_ref.at[mine], land_ref.at[mine], local_sem)
        own.start()
        copies = [pltpu.make_async_remote_copy(
            src_ref=p_ref.at[2 * cx + cy], dst_ref=land_ref.at[mine], send_sem=send_sems.at[j],
            recv_sem=recv_sems.at[j], device_id=(cx, cy, c), device_id_type=MESH) for j, (cx, cy) in enumerate(chips)]
        for cp in copies:
            cp.start()
        for j, (cx, cy) in enumerate(chips):
            pltpu.make_async_remote_copy(
                src_ref=p_ref.at[mine], dst_ref=land_ref.at[2 * cx + cy], send_sem=send_sems.at[j],
                recv_sem=recv_sems.at[j], device_id=(cx, cy, c), device_id_type=MESH).wait_recv()
        for cp in copies:
            cp.wait_send()
        own.wait()

    return pl.pallas_call(
        body, name="rs_chip_exchange", in_specs=[ANY], out_specs=ANY,
        out_shape=jax.ShapeDtypeStruct(part.shape, part.dtype),
        scratch_shapes=[pltpu.SemaphoreType.DMA((3,)), pltpu.SemaphoreType.DMA((3,)), pltpu.SemaphoreType.DMA],
    )(part)


def _adamw(parts, w, m, v, name):
    k, rows, cols = parts.shape
    tr = 432 if rows % 432 == 0 else rows
    c1 = 1.0 / (1.0 - ADAM_B1 ** ADAM_STEP)
    c2 = 1.0 / (1.0 - ADAM_B2 ** ADAM_STEP)

    def body(p_ref, w_ref, m_ref, v_ref, g_ref, d_ref, nm_ref, nv_ref):
        g = p_ref[0].astype(F32)
        for j in range(1, k):
            g = g + p_ref[j].astype(F32)
        g_ref[...] = g
        nm = ADAM_B1 * m_ref[...] + (1.0 - ADAM_B1) * g
        nv = ADAM_B2 * v_ref[...] + (1.0 - ADAM_B2) * (g * g)
        nm_ref[...] = nm
        nv_ref[...] = nv
        d_ref[...] = -ADAM_LR * ((nm * c1) / (jnp.sqrt(nv * c2) + ADAM_EPS) + ADAM_WD * w_ref[...])

    blk = pl.BlockSpec((tr, cols), lambda i: (i, 0))
    return pl.pallas_call(
        body, name=name, grid=(rows // tr,),
        in_specs=[pl.BlockSpec((k, tr, cols), lambda i: (0, i, 0)), blk, blk, blk],
        out_specs=[blk] * 4, out_shape=[jax.ShapeDtypeStruct((rows, cols), F32)] * 4,
        compiler_params=_cparams("parallel"),
    )(parts, w, m, v)


COL_SHARDED = ("w_in", "w_gate", "w_up", "w_ple_proj")
REPLICATED = (("g_mix", 1024), ("conv_b", 512), ("q_norm_g", 64), ("k_norm_g", 64), ("g_out_conv", 512),
              ("g_out_attn", 512), ("g_ffn", 1024), ("ffn_conv_b", 2816), ("g_ple", 1024))
CONV_SHARDED = (("conv_w", CONV_W), ("ffn_conv_w", D_FF))


def _pack_rows(arrays):
    return jnp.concatenate([a.reshape(-1, 1024) for a in arrays], axis=0)


def _unpack_big(flat, shard_shapes):
    out, r0 = {}, 0
    for name, rows in BIG_ROWS:
        out[name] = flat[r0:r0 + rows].reshape(shard_shapes[name])
        r0 += rows
    return out


def _gathered_to_full(gathered, shard_shapes):
    out, r0 = {}, 0
    for name, rows in BIG_ROWS:
        a = gathered[:, r0:r0 + rows].reshape((N_DEV,) + shard_shapes[name])
        if name in COL_SHARDED:
            a = a.transpose(1, 0, 2)
            out[name] = a.reshape(a.shape[0], -1)
        else:
            out[name] = a.reshape(-1, a.shape[2])
        r0 += rows
    return out


def _full_to_stacked(grads, shard_shapes):
    parts = []
    for name, rows in BIG_ROWS:
        sr, sc = shard_shapes[name]
        a = grads[name]
        if name in COL_SHARDED:
            a = a.reshape(sr, N_DEV, sc).transpose(1, 0, 2)
        else:
            a = a.reshape(N_DEV, sr, sc)
        parts.append(a.reshape(N_DEV, rows, 1024))
    return jnp.concatenate(parts, axis=1)


def _pad_rows(vec, rows):
    return jnp.pad(vec, (0, rows * 1024 - vec.shape[0])).reshape(rows, 1024)


def kernel(x, p, g_mix, w_in, conv_w, conv_b, q_norm_g, k_norm_g, g_out_conv, g_out_attn, w_out, g_ffn, w_gate, w_up, ffn_conv_w, ffn_conv_b, w_down, g_ple, w_ple_gate, w_ple_proj, loss_target, m_g_mix, m_w_in, m_conv_w, m_conv_b, m_q_norm_g, m_k_norm_g, m_g_out_conv, m_g_out_attn, m_w_out, m_g_ffn, m_w_gate, m_w_up, m_ffn_conv_w, m_ffn_conv_b, m_w_down, m_g_ple, m_w_ple_gate, m_w_ple_proj, v_g_mix, v_w_in, v_conv_w, v_conv_b, v_q_norm_g, v_k_norm_g, v_g_out_conv, v_g_out_attn, v_w_out, v_g_ffn, v_w_gate, v_w_up, v_ffn_conv_w, v_ffn_conv_b, v_w_down, v_g_ple, v_w_ple_gate, v_w_ple_proj):
    args = dict(locals())
    names = ["g_mix", "w_in", "conv_w", "conv_b", "q_norm_g", "k_norm_g", "g_out_conv", "g_out_attn", "w_out", "g_ffn",
             "w_gate", "w_up", "ffn_conv_w", "ffn_conv_b", "w_down", "g_ple", "w_ple_gate", "w_ple_proj"]
    big = [n for n, _ in BIG_ROWS]
    conv = [n for n, _ in CONV_SHARDED]
    wts = {n: (args[n][0] if n in big or n in conv else args[n]) for n in names}
    mom = {n: (args["m_" + n][0] if n in big or n in conv else args["m_" + n]) for n in names}
    var = {n: (args["v_" + n][0] if n in big or n in conv else args["v_" + n]) for n in names}
    shard_shapes = {n: wts[n].shape for n in big}
    dev = 4 * lax.axis_index("x") + 2 * lax.axis_index("y") + lax.axis_index("c")
    core = lax.axis_index("c").astype(jnp.int32).reshape(1)

    conv_local = _pad_rows(jnp.concatenate([wts[n].reshape(-1) for n in conv]), 8).reshape(8, 1024)
    gathered, conv_all = _all_gather([_pack_rows([wts[n].astype(BF16) for n in big]), conv_local], "gather_weights")
    full = dict(wts)
    full.update(_gathered_to_full(gathered, shard_shapes))
    off = 0
    for n, width in CONV_SHARDED:
        sc = width // N_DEV
        a = conv_all.reshape(N_DEV, -1)[:, off:off + 3 * sc].reshape(N_DEV, 3, sc)
        full[n] = a.transpose(1, 0, 2).reshape(3, width)
        off += 3 * sc

    loss, dx, grads = _local_step(x[0], p[0, 0], loss_target[0], full, (512, 256))

    stacked = _full_to_stacked(grads, shard_shapes).astype(BF16).reshape(N_CHIP, 2, BIG_TOTAL, 1024)
    landed = _sibling_exchange(stacked)
    contributions = _chip_exchange(_pair_sum(stacked, landed, core))

    small = jnp.concatenate([grads[n].reshape(-1) for n, _ in REPLICATED] + [grads[n].reshape(-1) for n in conv]
                            + [loss.reshape(1)])
    (small_all,) = _all_gather([_pad_rows(small, SMALL_ROWS)], "gather_small_grads")

    g_big, d_big, m_big, v_big = _adamw(contributions, _pack_rows([wts[n] for n in big]),
                                        _pack_rows([mom[n] for n in big]), _pack_rows([var[n] for n in big]),
                                        "adamw_large")
    n_rep = sum(s for _, s in REPLICATED)
    conv_sizes = [3 * w_ // N_DEV for _, w_ in CONV_SHARDED]

    def small_state(src):
        flat = jnp.concatenate([src[n].reshape(-1) for n, _ in REPLICATED] + [src[n].reshape(-1) for n in conv])
        return _pad_rows(flat, 16)

    rep_all = small_all.reshape(N_DEV, -1)[:, :n_rep]
    conv_parts, off = [], n_rep
    for (n, width), size in zip(CONV_SHARDED, conv_sizes):
        sc = width // N_DEV
        a = small_all.reshape(N_DEV, -1)[:, off:off + 3 * width].reshape(N_DEV, 3, width)
        conv_parts.append(lax.dynamic_slice(a, (0, 0, dev * sc), (N_DEV, 3, sc)).reshape(N_DEV, size))
        off += 3 * width
    loss_total = jnp.sum(small_all.reshape(N_DEV, -1)[:, off])
    small_parts = jnp.concatenate([rep_all] + conv_parts, axis=1)
    small_parts = jnp.pad(small_parts, ((0, 0), (0, 16 * 1024 - small_parts.shape[1]))).reshape(N_DEV, 16, 1024)
    g_sm, d_sm, m_sm, v_sm = _adamw(small_parts, small_state(wts), small_state(mom), small_state(var), "adamw_small")

    def unpack(big_flat, small_flat, like):
        out = _unpack_big(big_flat, shard_shapes)
        flat, o = small_flat.reshape(-1), 0
        for n, s in list(REPLICATED) + [(n, sz) for (n, _), sz in zip(CONV_SHARDED, conv_sizes)]:
            out[n] = flat[o:o + s]
            o += s
        return [out[n].reshape(like[n].shape) for n in names]

    like = {n: args[n] for n in names}
    return (loss_total, dx[None], *unpack(g_big, g_sm, like), *unpack(d_big, d_sm, like),
            *unpack(m_big, m_sm, like), *unpack(v_big, v_sm, like))
```

```python
import functools

import jax
import jax.numpy as jnp
from jax import lax
from jax.experimental import pallas as pl
from jax.experimental.pallas import tpu as pltpu

F32 = jnp.float32
BF16 = jnp.bfloat16

D_MODEL = 1024
CONV_W = 512
ATTN_W = 512
HEAD_DIM = 64
D_FF = 2816
PLE_DIM = 256
IN_COLS = 3 * CONV_W + 3 * ATTN_W
EPS = 1e-6
QK_BLOCK = 128
DILATIONS = (1, 4, 16)
ATTN_SCALE = HEAD_DIM ** -0.5

ADAM_LR = 0.001
ADAM_B1 = 0.9
ADAM_B2 = 0.999
ADAM_EPS = 1e-08
ADAM_WD = 0.01
ADAM_STEP = 10

N_DEV = 8
N_CHIP = 4
V7X_VMEM_LIMIT = 56 * 1024 * 1024
FF_CHUNKS = 2

BIG_ROWS = (("w_in", 384), ("w_out", 128), ("w_gate", 352), ("w_up", 352), ("w_down", 352),
            ("w_ple_gate", 128), ("w_ple_proj", 32))
BIG_TOTAL = sum(r for _, r in BIG_ROWS)
SMALL_ROWS = 24


def _cparams(*sem):
    return pltpu.CompilerParams(dimension_semantics=sem, vmem_limit_bytes=V7X_VMEM_LIMIT)


def _mm(a, b):
    return jnp.dot(a, b, preferred_element_type=F32)


def _mm_nt(a, b):
    return lax.dot_general(a, b, (((1,), (1,)), ((), ())), preferred_element_type=F32)


def _mm_tn(a, b):
    return lax.dot_general(a, b, (((0,), (0,)), ((), ())), preferred_element_type=F32)


def _full(shape):
    nd = len(shape)
    return pl.BlockSpec(shape, lambda *_: (0,) * nd)


def _rms_stats(x):
    r = lax.rsqrt(jnp.mean(x * x, axis=-1, keepdims=True) + EPS)
    return r, x * r


def _rms_bwd(dy, xhat, r, g):
    gd = dy * g
    return r * (gd - xhat * jnp.mean(gd * xhat, axis=-1, keepdims=True))


def _seg_sum64(v, bd_ref):
    outs = []
    for c in range(0, v.shape[1], 256):
        vc = v[:, c:c + 256]
        hi = vc.astype(BF16)
        lo = (vc - hi.astype(F32)).astype(BF16)
        outs.append(_mm(hi, bd_ref[...]) + _mm(lo, bd_ref[...]))
    return outs[0] if len(outs) == 1 else jnp.concatenate(outs, axis=1)


def _shift_rows(u, k, edge_rows):
    row = lax.broadcasted_iota(jnp.int32, u.shape, 0)
    out = pltpu.roll(u, k, 0)
    for j in range(k):
        out = jnp.where(row == j, edge_rows[k - 1 - j], out)
    return out


def _shift_rows_up(u, k, edge_rows):
    n = u.shape[0]
    row = lax.broadcasted_iota(jnp.int32, u.shape, 0)
    out = pltpu.roll(u, n - k, 0)
    for j in range(k):
        out = jnp.where(row == n - k + j, edge_rows[j], out)
    return out


def _conv_fwd(u, c1, c2, w_ref, b_ref):
    u1 = _shift_rows(u, 1, (c1,))
    u2 = _shift_rows(u, 2, (c1, c2))
    y = u2 * w_ref[0:1, :] + u1 * w_ref[1:2, :] + u * w_ref[2:3, :] + b_ref[...]
    return y, u1, u2


def _conv_bwd_input(dy, n1row, n2row, w_ref):
    d1 = _shift_rows_up(dy, 1, (n1row,))
    d2 = _shift_rows_up(dy, 2, (n1row, n2row))
    return dy * w_ref[2:3, :] + d1 * w_ref[1:2, :] + d2 * w_ref[0:1, :]


def _sigmoid(x):
    return 1.0 / (1.0 + jnp.exp(-x))


def _inproj_fwd(x, g_mix, w_in, conv_w, conv_b, qg, kg, bd, tm):
    t = x.shape[0]

    def body(x_ref, g_ref, w_ref, cw_ref, cb_ref, qg_ref, kg_ref, bd_ref,
             zc_ref, zqk_ref, yc_ref, q_ref, k_ref, v_ref, carry_ref):
        @pl.when(pl.program_id(0) == 0)
        def _():
            carry_ref[...] = jnp.zeros_like(carry_ref)

        _, xhat = _rms_stats(x_ref[...])
        h = (xhat * g_ref[...]).astype(BF16)
        zconv = _mm(h, w_ref[:, 0:3 * CONV_W])
        zc_ref[...] = zconv
        u = zconv[:, CONV_W:2 * CONV_W] * zconv[:, 2 * CONV_W:3 * CONV_W]
        cv, _, _ = _conv_fwd(u, carry_ref[7:8, :], carry_ref[6:7, :], cw_ref, cb_ref)
        yc_ref[...] = zconv[:, 0:CONV_W] * cv
        carry_ref[...] = u[tm - 8:tm, :]

        zqk = _mm(h, w_ref[:, 3 * CONV_W:3 * CONV_W + 2 * ATTN_W])
        zqk_ref[...] = zqk
        for j, (gain_ref, out_ref, scale) in enumerate(((qg_ref, q_ref, ATTN_SCALE), (kg_ref, k_ref, 1.0))):
            z = zqk[:, j * ATTN_W:(j + 1) * ATTN_W]
            r = lax.rsqrt(_seg_sum64(z * z, bd_ref) * (1.0 / HEAD_DIM) + EPS)
            out_ref[...] = z * r * gain_ref[...] * scale
        v_ref[...] = _mm(h, w_ref[:, 3 * CONV_W + 2 * ATTN_W:IN_COLS])

    def blk(c):
        return pl.BlockSpec((tm, c), lambda i: (i, 0))

    return pl.pallas_call(
        body, name="inproj_fwd", grid=(t // tm,),
        in_specs=[blk(D_MODEL), _full((1, D_MODEL)), _full((D_MODEL, IN_COLS)), _full((3, CONV_W)),
                  _full((1, CONV_W)), _full((1, ATTN_W)), _full((1, ATTN_W)), _full((256, 256))],
        out_specs=[blk(3 * CONV_W), blk(2 * ATTN_W), blk(CONV_W), blk(ATTN_W), blk(ATTN_W), blk(ATTN_W)],
        out_shape=[jax.ShapeDtypeStruct((t, 3 * CONV_W), F32), jax.ShapeDtypeStruct((t, 2 * ATTN_W), F32),
                   jax.ShapeDtypeStruct((t, CONV_W), F32), jax.ShapeDtypeStruct((t, ATTN_W), F32),
                   jax.ShapeDtypeStruct((t, ATTN_W), F32), jax.ShapeDtypeStruct((t, ATTN_W), F32)],
        scratch_shapes=[pltpu.VMEM((8, CONV_W), F32)],
        compiler_params=_cparams("arbitrary"),
    )(x, g_mix, w_in, conv_w, conv_b, qg, kg, bd)


SUPER = 16 * QK_BLOCK
KEYS = 2 * QK_BLOCK


def _rows(start, size, dil):
    return pl.ds(start, size) if dil == 1 else pl.ds(start, size, stride=dil)


def _attn_bias(sl_ref, dil):
    qi = lax.broadcasted_iota(jnp.int32, (KEYS, KEYS), 0)
    kj = lax.broadcasted_iota(jnp.int32, (KEYS, KEYS), 1)
    step = jnp.bitwise_and(qi, QK_BLOCK - 1) + QK_BLOCK - kj
    slope = jnp.where(qi < QK_BLOCK, sl_ref[0, 0:1, 0:1], sl_ref[0, 1:2, 0:1])
    bias = jnp.where(jnp.logical_and(step >= 0, step <= QK_BLOCK), -slope * (step * dil).astype(F32), -jnp.inf)
    return bias, kj >= QK_BLOCK


def _unit_start(u, dil):
    if dil == 1:
        return pl.multiple_of(u * QK_BLOCK, QK_BLOCK)
    if dil == 4:
        return jnp.bitwise_and(u, 3) + (u // 4) * (4 * QK_BLOCK)
    return u


def _stack_heads(a, head0):
    zero = jnp.zeros_like(a)
    return jnp.concatenate([jnp.where(head0, a, zero), jnp.where(head0, zero, a)], axis=0)


def _attn_fwd(q, k, v, slopes):
    t = q.shape[0]
    nsb = t // SUPER

    def body(q_ref, kc_ref, kp_ref, vc_ref, vp_ref, sl_ref, o_ref, l_ref, kk, vv, ob, lb):
        s = pl.program_id(1)
        kk[0:SUPER, :] = kp_ref[...]
        kk[SUPER:, :] = kc_ref[...]
        vv[0:SUPER, :] = vp_ref[...]
        vv[SUPER:, :] = vc_ref[...]
        head0 = lax.broadcasted_iota(jnp.int32, (QK_BLOCK, QK_BLOCK), 1) < HEAD_DIM

        for b, dil in enumerate(DILATIONS):
            bias, own_half = _attn_bias(sl_ref, dil)

            def unit(u, carry, b=b, dil=dil, bias=bias, own_half=own_half):
                start = _unit_start(u, dil)
                first_key = SUPER + start - QK_BLOCK * dil
                q2 = _stack_heads(q_ref[_rows(start, QK_BLOCK, dil), :].astype(BF16), head0)
                k2 = kk[_rows(first_key, KEYS, dil), :].astype(BF16)
                v2 = vv[_rows(first_key, KEYS, dil), :].astype(BF16)
                has_prev = jnp.logical_or(s > 0, start >= QK_BLOCK * dil)
                sc = jnp.where(jnp.logical_or(own_half, has_prev), _mm_nt(q2, k2) + bias, -jnp.inf)
                m = jnp.max(sc, axis=-1, keepdims=True)
                e = jnp.exp(sc - m)
                den = jnp.sum(e, axis=-1, keepdims=True)
                o2 = _mm(e.astype(BF16), v2) / den
                l2 = m + jnp.log(den)
                ob[b, _rows(start, QK_BLOCK, dil), :] = jnp.where(head0, o2[0:QK_BLOCK], o2[QK_BLOCK:])
                lb[b, _rows(start, QK_BLOCK, dil), :] = jnp.where(head0, l2[0:QK_BLOCK], l2[QK_BLOCK:])
                return carry

            lax.fori_loop(0, SUPER // QK_BLOCK, unit, 0, unroll=4)

        def merge(i, carry):
            rows = pl.ds(pl.multiple_of(i * 256, 256), 256)
            la, lb_, lc = lb[0, rows, :], lb[1, rows, :], lb[2, rows, :]
            mx = jnp.maximum(jnp.maximum(la, lb_), lc)
            wa, wb, wc = jnp.exp(la - mx), jnp.exp(lb_ - mx), jnp.exp(lc - mx)
            sw = wa + wb + wc
            o_ref[rows, :] = (wa * ob[0, rows, :] + wb * ob[1, rows, :] + wc * ob[2, rows, :]) / sw
            l_ref[rows, :] = mx + jnp.log(sw)
            return carry

        lax.fori_loop(0, SUPER // 256, merge, 0)

    cur = pl.BlockSpec((SUPER, QK_BLOCK), lambda p, s: (s, p))
    prev = pl.BlockSpec((SUPER, QK_BLOCK), lambda p, s: (jnp.maximum(s - 1, 0), p))
    return pl.pallas_call(
        body, name="attn_fwd", grid=(4, nsb),
        in_specs=[cur, cur, prev, cur, prev, pl.BlockSpec((1, 2, QK_BLOCK), lambda p, s: (p, 0, 0))],
        out_specs=[cur, cur],
        out_shape=[jax.ShapeDtypeStruct((t, ATTN_W), F32), jax.ShapeDtypeStruct((t, ATTN_W), F32)],
        scratch_shapes=[pltpu.VMEM((2 * SUPER, QK_BLOCK), F32), pltpu.VMEM((2 * SUPER, QK_BLOCK), F32),
                        pltpu.VMEM((3, SUPER, QK_BLOCK), F32), pltpu.VMEM((3, SUPER, QK_BLOCK), F32)],
        compiler_params=_cparams("parallel", "arbitrary"),
    )(q, k, k, v, v, slopes)


def _outproj_fwd(ya, yc, x, goc, goa, w_out, tm):
    t = x.shape[0]

    def body(ya_ref, yc_ref, x_ref, goc_ref, goa_ref, w_ref, x1_ref):
        _, ychat = _rms_stats(yc_ref[...])
        _, yahat = _rms_stats(ya_ref[...])
        acc = _mm((ychat * goc_ref[...]).astype(BF16), w_ref[0:CONV_W, :])
        acc += _mm((yahat * goa_ref[...]).astype(BF16), w_ref[CONV_W:, :])
        x1_ref[...] = x_ref[...] + acc

    def blk(c):
        return pl.BlockSpec((tm, c), lambda i: (i, 0))

    return pl.pallas_call(
        body, name="outproj_fwd", grid=(t // tm,),
        in_specs=[blk(ATTN_W), blk(CONV_W), blk(D_MODEL), _full((1, CONV_W)), _full((1, ATTN_W)),
                  _full((D_MODEL, D_MODEL))],
        out_specs=blk(D_MODEL),
        out_shape=jax.ShapeDtypeStruct((t, D_MODEL), F32),
        compiler_params=_cparams("parallel"),
    )(ya, yc, x, goc, goa, w_out)


def _ffn_fwd(x1, g_ffn, w_gate, w_up, w_down, fcw, fcb, tm):
    t = x1.shape[0]

    def body(x_ref, g_ref, wg_ref, wu_ref, wd_ref, cw_ref, cb_ref, gp_ref, up_ref, x2_ref, carry_ref):
        @pl.when(pl.program_id(0) == 0)
        def _():
            carry_ref[...] = jnp.zeros_like(carry_ref)

        xv = x_ref[...]
        _, xhat = _rms_stats(xv)
        h = (xhat * g_ref[...]).astype(BF16)
        gp = _mm(h, wg_ref[...])
        gp_ref[...] = gp
        gate, _, _ = _conv_fwd(gp, carry_ref[7:8, :], carry_ref[6:7, :], cw_ref, cb_ref)
        carry_ref[...] = gp[tm - 8:tm, :]
        up = _mm(h, wu_ref[...])
        up_ref[...] = up
        a = (gate * _sigmoid(gate) * up).astype(BF16)
        x2_ref[...] = xv + _mm(a, wd_ref[...])

    def blk(c):
        return pl.BlockSpec((tm, c), lambda i: (i, 0))

    return pl.pallas_call(
        body, name="ffn_fwd", grid=(t // tm,),
        in_specs=[blk(D_MODEL), _full((1, D_MODEL)), _full((D_MODEL, D_FF)), _full((D_MODEL, D_FF)),
                  _full((D_FF, D_MODEL)), _full((3, D_FF)), _full((1, D_FF))],
        out_specs=[blk(D_FF), blk(D_FF), blk(D_MODEL)],
        out_shape=[jax.ShapeDtypeStruct((t, D_FF), F32), jax.ShapeDtypeStruct((t, D_FF), F32),
                   jax.ShapeDtypeStruct((t, D_MODEL), F32)],
        scratch_shapes=[pltpu.VMEM((8, D_FF), F32)],
        compiler_params=_cparams("arbitrary"),
    )(x1, g_ffn, w_gate, w_up, w_down, fcw, fcb)


def _ple_fwd_bwd(x2, p, target, g_ple, w_pg, w_pp, tm):
    t = x2.shape[0]

    def body(x_ref, p_ref, t_ref, g_ref, wg_ref, wp_ref, dx_ref, loss_ref, dwg_ref, dwp_ref, dg_ref):
        @pl.when(pl.program_id(0) == 0)
        def _():
            loss_ref[...] = jnp.zeros_like(loss_ref)
            dwg_ref[...] = jnp.zeros_like(dwg_ref)
            dwp_ref[...] = jnp.zeros_like(dwp_ref)
            dg_ref[...] = jnp.zeros_like(dg_ref)

        xv = x_ref[...]
        r, xhat = _rms_stats(xv)
        g = g_ref[...]
        h = (xhat * g).astype(BF16)
        pg = _sigmoid(_mm(h, wg_ref[...]))
        pb = p_ref[...].astype(BF16)
        pp = _mm(pb, wp_ref[...])
        err = xv + pg * pp - t_ref[...]
        loss_ref[...] += 0.5 * jnp.sum(jnp.mean(err * err, axis=-1, keepdims=True))
        dx3 = err * (1.0 / D_MODEL)
        d_pp = (dx3 * pg).astype(BF16)
        d_pre = (dx3 * pp * pg * (1.0 - pg)).astype(BF16)
        dwp_ref[...] += _mm_tn(pb, d_pp)
        dwg_ref[...] += _mm_tn(h, d_pre)
        dh = _mm_nt(d_pre, wg_ref[...])
        dg_ref[...] += jnp.sum(dh * xhat, axis=0, keepdims=True)
        dx_ref[...] = dx3 + _rms_bwd(dh, xhat, r, g)

    def blk(c):
        return pl.BlockSpec((tm, c), lambda i: (i, 0))

    return pl.pallas_call(
        body, name="ple_fwd_bwd", grid=(t // tm,),
        in_specs=[blk(D_MODEL), blk(PLE_DIM), blk(D_MODEL), _full((1, D_MODEL)), _full((D_MODEL, D_MODEL)),
                  _full((PLE_DIM, D_MODEL))],
        out_specs=[blk(D_MODEL), _full((8, 128)), _full((D_MODEL, D_MODEL)), _full((PLE_DIM, D_MODEL)),
                   _full((1, D_MODEL))],
        out_shape=[jax.ShapeDtypeStruct((t, D_MODEL), F32), jax.ShapeDtypeStruct((8, 128), F32),
                   jax.ShapeDtypeStruct((D_MODEL, D_MODEL), F32), jax.ShapeDtypeStruct((PLE_DIM, D_MODEL), F32),
                   jax.ShapeDtypeStruct((1, D_MODEL), F32)],
        compiler_params=_cparams("arbitrary"),
    )(x2, p, target, g_ple, w_pg, w_pp)


def _ffn_bwd(dx2, x1, g_ffn, gp, up, w_gate, w_up, w_down, fcw, fcb, tm):
    t = x1.shape[0]
    nblk = t // tm
    fc = D_FF // FF_CHUNKS

    def body(dx_ref, x_ref, g_ref, gp_ref, gph_ref, up_ref, wg_ref, wu_ref, wd_ref, cw_ref, cb_ref,
             dh_ref, dwd_ref, dwu_ref, dwg_ref, dcw_ref, dcb_ref, carry_ref):
        i = pl.program_id(1)

        @pl.when(i == 0)
        def _():
            carry_ref[...] = jnp.zeros_like(carry_ref)
            dwd_ref[...] = jnp.zeros_like(dwd_ref)
            dwu_ref[...] = jnp.zeros_like(dwu_ref)
            dwg_ref[...] = jnp.zeros_like(dwg_ref)
            dcw_ref[...] = jnp.zeros_like(dcw_ref)
            dcb_ref[...] = jnp.zeros_like(dcb_ref)

        keep = (i < nblk - 1).astype(F32)
        dxb = dx_ref[...].astype(BF16)
        _, xhat = _rms_stats(x_ref[...])
        h = (xhat * g_ref[...]).astype(BF16)
        gp_v = gp_ref[...]
        gate, gp1, gp2 = _conv_fwd(gp_v, gph_ref[7:8, :] * keep, gph_ref[6:7, :] * keep, cw_ref, cb_ref)
        s = _sigmoid(gate)
        silu = gate * s
        up_v = up_ref[...]
        da = _mm_nt(dxb, wd_ref[...])
        dwd_ref[...] += _mm_tn((silu * up_v).astype(BF16), dxb)
        d_up = (da * silu).astype(BF16)
        d_gate = da * up_v * (s * (1.0 + gate * (1.0 - s)))
        dwu_ref[...] += _mm_tn(h, d_up)
        d_gp = _conv_bwd_input(d_gate, carry_ref[0:1, :], carry_ref[1:2, :], cw_ref).astype(BF16)
        carry_ref[...] = d_gate[0:8, :]
        dcw_ref[0:1, :] += jnp.sum(d_gate * gp2, axis=0, keepdims=True)
        dcw_ref[1:2, :] += jnp.sum(d_gate * gp1, axis=0, keepdims=True)
        dcw_ref[2:3, :] += jnp.sum(d_gate * gp_v, axis=0, keepdims=True)
        dcb_ref[...] += jnp.sum(d_gate, axis=0, keepdims=True)
        dwg_ref[...] += _mm_tn(h, d_gp)
        dh_ref[...] = _mm_nt(d_gp, wg_ref[...]) + _mm_nt(d_up, wu_ref[...])

    def rev(i):
        return nblk - 1 - i

    one = pl.Buffered(1)
    in_specs = [
        pl.BlockSpec((tm, D_MODEL), lambda j, i: (rev(i), 0)),
        pl.BlockSpec((tm, D_MODEL), lambda j, i: (rev(i), 0)),
        _full((1, D_MODEL)),
        pl.BlockSpec((tm, fc), lambda j, i: (rev(i), j)),
        pl.BlockSpec((8, fc), lambda j, i: (jnp.maximum(rev(i) * (tm // 8) - 1, 0), j)),
        pl.BlockSpec((tm, fc), lambda j, i: (rev(i), j)),
        pl.BlockSpec((D_MODEL, fc), lambda j, i: (0, j), pipeline_mode=one),
        pl.BlockSpec((D_MODEL, fc), lambda j, i: (0, j), pipeline_mode=one),
        pl.BlockSpec((fc, D_MODEL), lambda j, i: (j, 0), pipeline_mode=one),
        pl.BlockSpec((3, fc), lambda j, i: (0, j)),
        pl.BlockSpec((1, fc), lambda j, i: (0, j)),
    ]
    out_specs = [
        pl.BlockSpec((None, tm, D_MODEL), lambda j, i: (j, rev(i), 0)),
        pl.BlockSpec((fc, D_MODEL), lambda j, i: (j, 0), pipeline_mode=one),
        pl.BlockSpec((D_MODEL, fc), lambda j, i: (0, j), pipeline_mode=one),
        pl.BlockSpec((D_MODEL, fc), lambda j, i: (0, j), pipeline_mode=one),
        pl.BlockSpec((3, fc), lambda j, i: (0, j)),
        pl.BlockSpec((1, fc), lambda j, i: (0, j)),
    ]
    return pl.pallas_call(
        body, name="ffn_bwd", grid=(FF_CHUNKS, nblk), in_specs=in_specs, out_specs=out_specs,
        out_shape=[jax.ShapeDtypeStruct((FF_CHUNKS, t, D_MODEL), F32), jax.ShapeDtypeStruct((D_FF, D_MODEL), F32),
                   jax.ShapeDtypeStruct((D_MODEL, D_FF), F32), jax.ShapeDtypeStruct((D_MODEL, D_FF), F32),
                   jax.ShapeDtypeStruct((3, D_FF), F32), jax.ShapeDtypeStruct((1, D_FF), F32)],
        scratch_shapes=[pltpu.VMEM((8, fc), F32)],
        compiler_params=_cparams("arbitrary", "arbitrary"),
    )(dx2, x1, g_ffn, gp, gp, up, w_gate, w_up, w_down, fcw, fcb)


def _outproj_bwd(dh2, dx2, x1, g_ffn, w_out, yc, ya, goc, goa, zconv, conv_w, conv_b, bd, tm):
    t = x1.shape[0]
    nblk = t // tm

    def body(dh_ref, dx2_ref, x1_ref, g_ref, w_ref, yc_ref, ya_ref, goc_ref, goa_ref, zc_ref, zch_ref, cw_ref, cb_ref,
             bd_ref, dx1_ref, dya_ref, dd_ref, dzc_ref, dw_ref, dg_ref, dgoc_ref, dgoa_ref, dcw_ref, dcb_ref,
             carry_ref):
        i = pl.program_id(0)

        @pl.when(i == 0)
        def _():
            carry_ref[...] = jnp.zeros_like(carry_ref)
            for ref in (dw_ref, dg_ref, dgoc_ref, dgoa_ref, dcw_ref, dcb_ref):
                ref[...] = jnp.zeros_like(ref)

        keep = (i < nblk - 1).astype(F32)
        dh2_v = dh_ref[0]
        for j in range(1, FF_CHUNKS):
            dh2_v = dh2_v + dh_ref[j]
        r, xhat = _rms_stats(x1_ref[...])
        dg_ref[...] += jnp.sum(dh2_v * xhat, axis=0, keepdims=True)
        dx1 = dx2_ref[...] + _rms_bwd(dh2_v, xhat, r, g_ref[...])
        dx1_ref[...] = dx1
        dx1b = dx1.astype(BF16)
        dy = _mm_nt(dx1b, w_ref[...])

        yc_v = yc_ref[...]
        rc, ychat = _rms_stats(yc_v)
        dw_ref[0:CONV_W, :] += _mm_tn((ychat * goc_ref[...]).astype(BF16), dx1b)
        dyc = dy[:, 0:CONV_W]
        dgoc_ref[...] += jnp.sum(dyc * ychat, axis=0, keepdims=True)
        d_yc = _rms_bwd(dyc, ychat, rc, goc_ref[...])

        ya_v = ya_ref[...]
        ra, yahat = _rms_stats(ya_v)
        dw_ref[CONV_W:, :] += _mm_tn((yahat * goa_ref[...]).astype(BF16), dx1b)
        dya = dy[:, CONV_W:]
        dgoa_ref[...] += jnp.sum(dya * yahat, axis=0, keepdims=True)
        d_ya = _rms_bwd(dya, yahat, ra, goa_ref[...])
        dya_ref[...] = d_ya
        dd_ref[...] = _seg_sum64(d_ya * ya_v, bd_ref)

        zb = zc_ref[:, 0:CONV_W]
        zc = zc_ref[:, CONV_W:2 * CONV_W]
        zx = zc_ref[:, 2 * CONV_W:3 * CONV_W]
        u = zc * zx
        uh = zch_ref[:, CONV_W:2 * CONV_W] * zch_ref[:, 2 * CONV_W:3 * CONV_W] * keep
        cv, u1, u2 = _conv_fwd(u, uh[7:8, :], uh[6:7, :], cw_ref, cb_ref)
        d_cv = d_yc * zb
        d_u = _conv_bwd_input(d_cv, carry_ref[0:1, :], carry_ref[1:2, :], cw_ref)
        carry_ref[...] = d_cv[0:8, :]
        dcw_ref[0:1, :] += jnp.sum(d_cv * u2, axis=0, keepdims=True)
        dcw_ref[1:2, :] += jnp.sum(d_cv * u1, axis=0, keepdims=True)
        dcw_ref[2:3, :] += jnp.sum(d_cv * u, axis=0, keepdims=True)
        dcb_ref[...] += jnp.sum(d_cv, axis=0, keepdims=True)
        dzc_ref[:, 0:CONV_W] = d_yc * cv
        dzc_ref[:, CONV_W:2 * CONV_W] = d_u * zx
        dzc_ref[:, 2 * CONV_W:3 * CONV_W] = d_u * zc

    def rev(i):
        return nblk - 1 - i

    def blk(c):
        return pl.BlockSpec((tm, c), lambda i: (rev(i), 0))

    in_specs = [
        pl.BlockSpec((FF_CHUNKS, tm, D_MODEL), lambda i: (0, rev(i), 0)),
        blk(D_MODEL), blk(D_MODEL), _full((1, D_MODEL)), _full((D_MODEL, D_MODEL)),
        blk(CONV_W), blk(ATTN_W), _full((1, CONV_W)), _full((1, ATTN_W)),
        blk(3 * CONV_W),
        pl.BlockSpec((8, 3 * CONV_W), lambda i: (jnp.maximum(rev(i) * (tm // 8) - 1, 0), 0)),
        _full((3, CONV_W)), _full((1, CONV_W)), _full((256, 256)),
    ]
    out_specs = [blk(D_MODEL), blk(ATTN_W), blk(ATTN_W), blk(3 * CONV_W), _full((D_MODEL, D_MODEL)),
                 _full((1, D_MODEL)), _full((1, CONV_W)), _full((1, ATTN_W)), _full((3, CONV_W)), _full((1, CONV_W))]
    return pl.pallas_call(
        body, name="outproj_bwd", grid=(nblk,), in_specs=in_specs, out_specs=out_specs,
        out_shape=[jax.ShapeDtypeStruct((t, D_MODEL), F32), jax.ShapeDtypeStruct((t, ATTN_W), F32),
                   jax.ShapeDtypeStruct((t, ATTN_W), F32), jax.ShapeDtypeStruct((t, 3 * CONV_W), F32),
                   jax.ShapeDtypeStruct((D_MODEL, D_MODEL), F32), jax.ShapeDtypeStruct((1, D_MODEL), F32),
                   jax.ShapeDtypeStruct((1, CONV_W), F32), jax.ShapeDtypeStruct((1, ATTN_W), F32),
                   jax.ShapeDtypeStruct((3, CONV_W), F32), jax.ShapeDtypeStruct((1, CONV_W), F32)],
        scratch_shapes=[pltpu.VMEM((8, CONV_W), F32)],
        compiler_params=_cparams("arbitrary"),
    )(dh2, dx2, x1, g_ffn, w_out, yc, ya, goc, goa, zconv, zconv, conv_w, conv_b, bd)


def _attn_bwd(q, k, v, dya, lse, dd, slopes):
    t = q.shape[0]
    nsb = t // SUPER

    def body(q_ref, kc_ref, kp_ref, vc_ref, vp_ref, dy_ref, l_ref, d_ref, sl_ref, dq_ref, dk_ref, dv_ref,
             kk, vv, dkacc, dvacc):
        s = pl.program_id(1)

        @pl.when(s == 0)
        def _():
            dkacc[...] = jnp.zeros_like(dkacc)
            dvacc[...] = jnp.zeros_like(dvacc)

        dkacc[0:SUPER, :] = dkacc[SUPER:, :]
        dvacc[0:SUPER, :] = dvacc[SUPER:, :]
        dkacc[SUPER:, :] = jnp.zeros((SUPER, QK_BLOCK), F32)
        dvacc[SUPER:, :] = jnp.zeros((SUPER, QK_BLOCK), F32)

        @pl.when(s < nsb)
        def _():
            kk[0:SUPER, :] = kp_ref[...]
            kk[SUPER:, :] = kc_ref[...]
            vv[0:SUPER, :] = vp_ref[...]
            vv[SUPER:, :] = vc_ref[...]
            head0 = lax.broadcasted_iota(jnp.int32, (QK_BLOCK, QK_BLOCK), 1) < HEAD_DIM

            for b, dil in enumerate(DILATIONS):
                bias, own_half = _attn_bias(sl_ref, dil)

                def unit(u, carry, b=b, dil=dil, bias=bias, own_half=own_half):
                    start = _unit_start(u, dil)
                    first_key = SUPER + start - QK_BLOCK * dil
                    qrows = _rows(start, QK_BLOCK, dil)
                    krows = _rows(first_key, KEYS, dil)
                    q2 = _stack_heads(q_ref[qrows, :].astype(BF16), head0)
                    dy2 = _stack_heads(dy_ref[qrows, :].astype(BF16), head0)
                    lv, dv_ = l_ref[qrows, :], d_ref[qrows, :]
                    l2 = jnp.concatenate([lv[:, 0:1], lv[:, HEAD_DIM:HEAD_DIM + 1]], axis=0)
                    d2 = jnp.concatenate([dv_[:, 0:1], dv_[:, HEAD_DIM:HEAD_DIM + 1]], axis=0)
                    k2 = kk[krows, :].astype(BF16)
                    v2 = vv[krows, :].astype(BF16)
                    has_prev = jnp.logical_or(s > 0, start >= QK_BLOCK * dil)
                    sc = jnp.where(jnp.logical_or(own_half, has_prev), _mm_nt(q2, k2) + bias, -jnp.inf)
                    prob = jnp.exp(sc - l2)
                    ds = (prob * (_mm_nt(dy2, v2) - d2)).astype(BF16)
                    dvacc[krows, :] += _mm_tn(prob.astype(BF16), dy2)
                    dkacc[krows, :] += _mm_tn(ds, q2)
                    dq2 = _mm(ds, k2)
                    dq = jnp.where(head0, dq2[0:QK_BLOCK], dq2[QK_BLOCK:]) * ATTN_SCALE
                    if b == 0:
                        dq_ref[qrows, :] = dq
                    else:
                        dq_ref[qrows, :] += dq
                    return carry

                lax.fori_loop(0, SUPER // QK_BLOCK, unit, 0, unroll=2)

        dk_ref[...] = dkacc[0:SUPER, :]
        dv_ref[...] = dvacc[0:SUPER, :]

    def cur_map(p, s):
        return (jnp.minimum(s, nsb - 1), p)

    def prev_map(p, s):
        return (jnp.clip(s - 1, 0, nsb - 1), p)

    cur = pl.BlockSpec((SUPER, QK_BLOCK), cur_map)
    prev = pl.BlockSpec((SUPER, QK_BLOCK), prev_map)
    return pl.pallas_call(
        body, name="attn_bwd", grid=(4, nsb + 1),
        in_specs=[cur, cur, prev, cur, prev, cur, cur, cur, pl.BlockSpec((1, 2, QK_BLOCK), lambda p, s: (p, 0, 0))],
        out_specs=[cur, prev, prev],
        out_shape=[jax.ShapeDtypeStruct((t, ATTN_W), F32)] * 3,
        scratch_shapes=[pltpu.VMEM((2 * SUPER, QK_BLOCK), F32)] * 4,
        compiler_params=_cparams("parallel", "arbitrary"),
    )(q, k, k, v, v, dya, lse, dd, slopes)


def _attn_bwd_per_branch_unused(q, k, v, dya, lse, dd, slopes, dil):
    t = q.shape[0]
    length = t // dil
    chunk = _attn_chunk(t, dil)
    nch = length // chunk
    nb = chunk // QK_BLOCK
    nblocks = length // QK_BLOCK
    view = (length, dil * ATTN_W)
    ext = chunk + QK_BLOCK

    def body(q_ref, dy_ref, l_ref, d_ref, k_ref, v_ref, qn_ref, dyn_ref, ln_ref, dn_ref, kh_ref, vh_ref, sl_ref,
             dq_ref, dk_ref, dv_ref, qbuf, dybuf, lbuf, dbuf, kbuf, vbuf, dkacc, dvacc):
        c = pl.program_id(2)
        qbuf[0:chunk, :] = q_ref[...]
        qbuf[chunk:, :] = qn_ref[...]
        dybuf[0:chunk, :] = dy_ref[...].astype(BF16)
        dybuf[chunk:, :] = dyn_ref[...].astype(BF16)
        lbuf[0:chunk, :] = l_ref[...]
        lbuf[chunk:, :] = ln_ref[...]
        dbuf[0:chunk, :] = d_ref[...]
        dbuf[chunk:, :] = dn_ref[...]
        kbuf[0:QK_BLOCK, :] = kh_ref[...]
        kbuf[QK_BLOCK:, :] = k_ref[...]
        vbuf[0:QK_BLOCK, :] = vh_ref[...]
        vbuf[QK_BLOCK:, :] = v_ref[...]
        valid_cur, valid_prev, dist_cur, dist_prev, head0 = _attn_masks(dil)

        def pair(qb, dyb, lv, dv_, kb, vb, valid, dist):
            dq = jnp.zeros((QK_BLOCK, QK_BLOCK), F32)
            dk = jnp.zeros((QK_BLOCK, QK_BLOCK), F32)
            dvv = jnp.zeros((QK_BLOCK, QK_BLOCK), F32)
            for hh in range(2):
                sl = sl_ref[0, hh:hh + 1, :]
                hm = head0 if hh == 0 else jnp.logical_not(head0)
                col = hh * HEAD_DIM
                qm = jnp.where(hm, qb, jnp.zeros_like(qb))
                dym = jnp.where(hm, dyb, jnp.zeros_like(dyb))
                s = jnp.where(valid, _mm_nt(qm, kb) - sl * dist, -jnp.inf)
                prob = jnp.exp(s - lv[:, col:col + 1])
                ds = (prob * (_mm_nt(dym, vb) - dv_[:, col:col + 1])).astype(BF16)
                dvv += _mm_tn(prob.astype(BF16), dym)
                dk += _mm_tn(ds, qm)
                dq += jnp.where(hm, _mm(ds, kb), 0.0)
            return dq, dk, dvv

        def blk(j, carry):
            off = pl.multiple_of(j * QK_BLOCK, QK_BLOCK)
            nxt = pl.multiple_of(off + QK_BLOCK, QK_BLOCK)
            qb = qbuf[pl.ds(off, QK_BLOCK), :]
            dyb = dybuf[pl.ds(off, QK_BLOCK), :]
            lv = lbuf[pl.ds(off, QK_BLOCK), :]
            dv_ = dbuf[pl.ds(off, QK_BLOCK), :]
            dq_c, dk_c, dv_c = pair(qb, dyb, lv, dv_, kbuf[pl.ds(nxt, QK_BLOCK), :], vbuf[pl.ds(nxt, QK_BLOCK), :],
                                    valid_cur, dist_cur)
            dkacc[pl.ds(nxt, QK_BLOCK), :] = dk_c
            dvacc[pl.ds(nxt, QK_BLOCK), :] = dv_c
            has_prev = jnp.logical_or(c > 0, j > 0)
            dq_p, dk_p, dv_p = pair(qb, dyb, lv, dv_, kbuf[pl.ds(off, QK_BLOCK), :], vbuf[pl.ds(off, QK_BLOCK), :],
                                    jnp.logical_and(valid_prev, has_prev), dist_prev)

            @pl.when(j > 0)
            def _():
                dkacc[pl.ds(off, QK_BLOCK), :] += dk_p
                dvacc[pl.ds(off, QK_BLOCK), :] += dv_p

            dq_ref[pl.ds(off, QK_BLOCK), :] = (dq_c + dq_p) * ATTN_SCALE
            return carry

        lax.fori_loop(0, nb, blk, 0)

        @pl.when(c < nch - 1)
        def _():
            _, dk_p, dv_p = pair(qbuf[chunk:, :], dybuf[chunk:, :], lbuf[chunk:, :], dbuf[chunk:, :],
                                 kbuf[chunk:, :], vbuf[chunk:, :], valid_prev, dist_prev)
            dkacc[chunk:, :] += dk_p
            dvacc[chunk:, :] += dv_p

        dk_ref[...] = dkacc[QK_BLOCK:, :]
        dv_ref[...] = dvacc[QK_BLOCK:, :]

    def cmap(p, r, c):
        return (c, r * 4 + p)

    def before(p, r, c):
        return (jnp.maximum(c * nb - 1, 0), r * 4 + p)

    def after(p, r, c):
        return (jnp.minimum((c + 1) * nb, nblocks - 1), r * 4 + p)

    main = pl.BlockSpec((chunk, QK_BLOCK), cmap)
    hb = pl.BlockSpec((QK_BLOCK, QK_BLOCK), before)
    ha = pl.BlockSpec((QK_BLOCK, QK_BLOCK), after)
    qv, kv, vv = q.reshape(view), k.reshape(view), v.reshape(view)
    dyv, lv, ddv = dya.reshape(view), lse.reshape(view), dd.reshape(view)
    outs = pl.pallas_call(
        body, name=f"attn_bwd_d{dil}", grid=(4, dil, nch),
        in_specs=[main] * 6 + [ha] * 4 + [hb] * 2 + [pl.BlockSpec((1, 2, QK_BLOCK), lambda p, r, c: (p, 0, 0))],
        out_specs=[main] * 3,
        out_shape=[jax.ShapeDtypeStruct(view, F32)] * 3,
        scratch_shapes=[pltpu.VMEM((ext, QK_BLOCK), BF16), pltpu.VMEM((ext, QK_BLOCK), BF16),
                        pltpu.VMEM((ext, QK_BLOCK), F32), pltpu.VMEM((ext, QK_BLOCK), F32),
                        pltpu.VMEM((ext, QK_BLOCK), BF16), pltpu.VMEM((ext, QK_BLOCK), BF16),
                        pltpu.VMEM((ext, QK_BLOCK), F32), pltpu.VMEM((ext, QK_BLOCK), F32)],
        compiler_params=_cparams("arbitrary", "arbitrary", "arbitrary"),
    )(qv, dyv, lv, ddv, kv, vv, qv, dyv, lv, ddv, kv, vv, slopes)
    return [o.reshape(t, ATTN_W) for o in outs]


def _inproj_bwd(dq, dk, dv, dzconv, zqk, x, dx1, g_mix, w_in, qg, kg, bd, tm):
    t = x.shape[0]

    def body(dq_ref, dk_ref, dv_ref, dzc_ref, zqk_ref, x_ref, dx1_ref, g_ref, w_ref, qg_ref,
             kg_ref, bd_ref, dx_ref, dw_ref, dg_ref, dqg_ref, dkg_ref):
        @pl.when(pl.program_id(0) == 0)
        def _():
            for ref in (dw_ref, dg_ref, dqg_ref, dkg_ref):
                ref[...] = jnp.zeros_like(ref)

        parts = [dzc_ref[...].astype(BF16)]
        for j, (dn_ref, gain_ref, dgain_ref) in enumerate(((dq_ref, qg_ref, dqg_ref), (dk_ref, kg_ref, dkg_ref))):
            dn = dn_ref[...]
            z = zqk_ref[:, j * ATTN_W:(j + 1) * ATTN_W]
            r = lax.rsqrt(_seg_sum64(z * z, bd_ref) * (1.0 / HEAD_DIM) + EPS)
            zhat = z * r
            dgain_ref[...] += jnp.sum(dn * zhat, axis=0, keepdims=True)
            gd = dn * gain_ref[...]
            parts.append((r * (gd - zhat * (_seg_sum64(gd * zhat, bd_ref) * (1.0 / HEAD_DIM)))).astype(BF16))
        parts.append(dv_ref[...].astype(BF16))
        dz = jnp.concatenate(parts, axis=1)

        r, xhat = _rms_stats(x_ref[...])
        g = g_ref[...]
        dw_ref[...] += _mm_tn((xhat * g).astype(BF16), dz)
        dh = _mm_nt(dz, w_ref[...])
        dg_ref[...] += jnp.sum(dh * xhat, axis=0, keepdims=True)
        dx_ref[...] = dx1_ref[...] + _rms_bwd(dh, xhat, r, g)

    def blk(c):
        return pl.BlockSpec((tm, c), lambda i: (i, 0))

    return pl.pallas_call(
        body, name="inproj_bwd", grid=(t // tm,),
        in_specs=[blk(ATTN_W)] * 3 + [blk(3 * CONV_W), blk(2 * ATTN_W), blk(D_MODEL), blk(D_MODEL), _full((1, D_MODEL)),
                                      _full((D_MODEL, IN_COLS)), _full((1, ATTN_W)), _full((1, ATTN_W)),
                                      _full((256, 256))],
        out_specs=[blk(D_MODEL), _full((D_MODEL, IN_COLS)), _full((1, D_MODEL)), _full((1, ATTN_W)),
                   _full((1, ATTN_W))],
        out_shape=[jax.ShapeDtypeStruct((t, D_MODEL), F32), jax.ShapeDtypeStruct((D_MODEL, IN_COLS), F32),
                   jax.ShapeDtypeStruct((1, D_MODEL), F32), jax.ShapeDtypeStruct((1, ATTN_W), F32),
                   jax.ShapeDtypeStruct((1, ATTN_W), F32)],
        compiler_params=_cparams("arbitrary"),
    )(dq, dk, dv, dzconv, zqk, x, dx1, g_mix, w_in, qg, kg, bd)


def _local_step(x, p, target, w, tms):
    bd = jnp.kron(jnp.eye(4, dtype=F32), jnp.ones((HEAD_DIM, HEAD_DIM), F32)).astype(BF16)
    qg = jnp.tile(w["q_norm_g"], (1, 8))
    kg = jnp.tile(w["k_norm_g"], (1, 8))
    slopes = jnp.exp2(-jnp.arange(1, 9, dtype=F32))
    slopes = jnp.broadcast_to(slopes.reshape(4, 2, 1), (4, 2, QK_BLOCK))

    zconv, zqk, yc, q, k, v = _inproj_fwd(x, w["g_mix"], w["w_in"], w["conv_w"], w["conv_b"], qg, kg, bd, tms[0])
    ya, lse = _attn_fwd(q, k, v, slopes)
    x1 = _outproj_fwd(ya, yc, x, w["g_out_conv"], w["g_out_attn"], w["w_out"], tms[0])
    gp, up, x2 = _ffn_fwd(x1, w["g_ffn"], w["w_gate"], w["w_up"], w["w_down"], w["ffn_conv_w"], w["ffn_conv_b"], tms[1])
    dx2, loss, dw_pg, dw_pp, dg_ple = _ple_fwd_bwd(x2, p, target, w["g_ple"], w["w_ple_gate"], w["w_ple_proj"], tms[0])
    dh2, dw_down, dw_up, dw_gate, dfcw, dfcb = _ffn_bwd(dx2, x1, w["g_ffn"], gp, up, w["w_gate"], w["w_up"],
                                                        w["w_down"], w["ffn_conv_w"], w["ffn_conv_b"], tms[1])
    dx1, dya, dd, dzconv, dw_out, dg_ffn, dgoc, dgoa, dcw, dcb = _outproj_bwd(
        dh2, dx2, x1, w["g_ffn"], w["w_out"], yc, ya, w["g_out_conv"], w["g_out_attn"], zconv, w["conv_w"],
        w["conv_b"], bd, tms[1])
    dq, dk, dv = _attn_bwd(q, k, v, dya, lse, dd, slopes)
    dx, dw_in, dg_mix, dqg, dkg = _inproj_bwd(dq, dk, dv, dzconv, zqk, x, dx1, w["g_mix"], w["w_in"], qg, kg, bd,
                                              tms[1])
    grads = {
        "g_mix": dg_mix, "w_in": dw_in, "conv_w": dcw, "conv_b": dcb,
        "q_norm_g": dqg.reshape(8, HEAD_DIM).sum(0, keepdims=True),
        "k_norm_g": dkg.reshape(8, HEAD_DIM).sum(0, keepdims=True),
        "g_out_conv": dgoc, "g_out_attn": dgoa, "w_out": dw_out, "g_ffn": dg_ffn, "w_gate": dw_gate, "w_up": dw_up,
        "ffn_conv_w": dfcw, "ffn_conv_b": dfcb, "w_down": dw_down, "g_ple": dg_ple, "w_ple_gate": dw_pg,
        "w_ple_proj": dw_pp,
    }
    return loss[0, 0], dx, grads


ANY = pl.BlockSpec(memory_space=pl.ANY)
MESH = pl.DeviceIdType.MESH


def _all_gather(shards, name):
    n = len(shards)

    def body(*refs):
        ins, outs = refs[:n], refs[n:2 * n]
        send_sems, recv_sems, local_sems = refs[2 * n:]
        x, y, c = lax.axis_index("x"), lax.axis_index("y"), lax.axis_index("c")
        me, sibling = (x, y, c), (x, y, 1 - c)
        chips = [(1 - x, y), (x, 1 - y), (1 - x, 1 - y)]

        def slot(dev):
            return 4 * dev[0] + 2 * dev[1] + dev[2]

        def copy(b, k, block, to, src=None):
            dst = outs[b].at[slot(block)]
            return pltpu.make_async_remote_copy(
                src_ref=dst if src is None else src, dst_ref=dst, send_sem=send_sems.at[b, k],
                recv_sem=recv_sems.at[b, k], device_id=to, device_id_type=MESH)

        mine = [pltpu.make_async_copy(ins[b], outs[b].at[slot(me)], local_sems.at[b]) for b in range(n)]
        first, passed = [], []
        for b in range(n):
            mine[b].start()
            first.append(copy(b, 0, me, sibling, src=ins[b]))
            first += [copy(b, 1 + j, me, (*chip, c), src=ins[b]) for j, chip in enumerate(chips)]
        for cp in first:
            cp.start()
        for j, chip in enumerate(chips):
            for b in range(n):
                copy(b, 1 + j, (*chip, c), me).wait_recv()
                fwd = copy(b, 4 + j, (*chip, c), sibling)
                fwd.start()
                passed.append(fwd)
        for b in range(n):
            copy(b, 0, sibling, me).wait_recv()
            for j, chip in enumerate(chips):
                copy(b, 4 + j, (*chip, 1 - c), me).wait_recv()
        for cp in first + passed:
            cp.wait_send()
        for cp in mine:
            cp.wait()

    return pl.pallas_call(
        body, name=name,
        in_specs=[ANY] * n, out_specs=[ANY] * n,
        out_shape=[jax.ShapeDtypeStruct((N_DEV,) + s.shape, s.dtype) for s in shards],
        scratch_shapes=[pltpu.SemaphoreType.DMA((n, 7)), pltpu.SemaphoreType.DMA((n, 7)),
                        pltpu.SemaphoreType.DMA((n,))],
    )(*shards)


def _sibling_exchange(g):
    def body(g_ref, land_ref, send_sems, recv_sems):
        x, y, c = lax.axis_index("x"), lax.axis_index("y"), lax.axis_index("c")
        copies = [pltpu.make_async_remote_copy(
            src_ref=g_ref.at[k, 1 - c], dst_ref=land_ref.at[k], send_sem=send_sems.at[k], recv_sem=recv_sems.at[k],
            device_id=(x, y, 1 - c), device_id_type=MESH) for k in range(N_CHIP)]
        for cp in copies:
            cp.start()
        for cp in copies:
            cp.wait()

    return pl.pallas_call(
        body, name="rs_sibling_exchange", in_specs=[ANY], out_specs=ANY,
        out_shape=jax.ShapeDtypeStruct((N_CHIP,) + g.shape[2:], g.dtype),
        scratch_shapes=[pltpu.SemaphoreType.DMA((N_CHIP,)), pltpu.SemaphoreType.DMA((N_CHIP,))],
    )(g)


def _pair_sum(g, land, core):
    rows, cols = land.shape[1:]
    tr = 432

    def body(c_ref, g_ref, l_ref, o_ref):
        o_ref[...] = (g_ref[...].astype(F32) + l_ref[...].astype(F32)).astype(o_ref.dtype)

    return pl.pallas_call(
        body, name="rs_pair_sum",
        grid_spec=pltpu.PrefetchScalarGridSpec(
            num_scalar_prefetch=1, grid=(N_CHIP, rows // tr),
            in_specs=[pl.BlockSpec((None, None, tr, cols), lambda k, i, c_ref: (k, c_ref[0], i, 0)),
                      pl.BlockSpec((None, tr, cols), lambda k, i, c_ref: (k, i, 0))],
            out_specs=pl.BlockSpec((None, tr, cols), lambda k, i, c_ref: (k, i, 0))),
        out_shape=jax.ShapeDtypeStruct(land.shape, land.dtype),
        compiler_params=_cparams("parallel", "parallel"),
    )(core, g, land)


def _chip_exchange(part):
    def body(p_ref, land_ref, send_sems, recv_sems, local_sem):
        x, y, c = lax.axis_index("x"), lax.axis_index("y"), lax.axis_index("c")
        mine = 2 * x + y
        chips = [(1 - x, y), (x, 1 - y), (1 - x, 1 - y)]
        own = pltpu.make_async_copy(p_ref.at[mine], land_ref.at[mine], local_sem)
        own.start()
        copies = [pltpu.make_async_remote_copy(
            src_ref=p_ref.at[2 * cx + cy], dst_ref=land_ref.at[mine], send_sem=send_sems.at[j],
            recv_sem=recv_sems.at[j], device_id=(cx, cy, c), device_id_type=MESH) for j, (cx, cy) in enumerate(chips)]
        for cp in copies:
            cp.start()
        for j, (cx, cy) in enumerate(chips):
            pltpu.make_async_remote_copy(
                src_ref=p_ref.at[mine], dst_ref=land_ref.at[2 * cx + cy], send_sem=send_sems.at[j],
                recv_sem=recv_sems.at[j], device_id=(cx, cy, c), device_id_type=MESH).wait_recv()
        for cp in copies:
            cp.wait_send()
        own.wait()

    return pl.pallas_call(
        body, name="rs_chip_exchange", in_specs=[ANY], out_specs=ANY,
        out_shape=jax.ShapeDtypeStruct(part.shape, part.dtype),
        scratch_shapes=[pltpu.SemaphoreType.DMA((3,)), pltpu.SemaphoreType.DMA((3,)), pltpu.SemaphoreType.DMA],
    )(part)


def _adamw(parts, w, m, v, name):
    k, rows, cols = parts.shape
    tr = 432 if rows % 432 == 0 else rows
    c1 = 1.0 / (1.0 - ADAM_B1 ** ADAM_STEP)
    c2 = 1.0 / (1.0 - ADAM_B2 ** ADAM_STEP)

    def body(p_ref, w_ref, m_ref, v_ref, g_ref, d_ref, nm_ref, nv_ref):
        g = p_ref[0].astype(F32)
        for j in range(1, k):
            g = g + p_ref[j].astype(F32)
        g_ref[...] = g
        nm = ADAM_B1 * m_ref[...] + (1.0 - ADAM_B1) * g
        nv = ADAM_B2 * v_ref[...] + (1.0 - ADAM_B2) * (g * g)
        nm_ref[...] = nm
        nv_ref[...] = nv
        d_ref[...] = -ADAM_LR * ((nm * c1) / (jnp.sqrt(nv * c2) + ADAM_EPS) + ADAM_WD * w_ref[...])

    blk = pl.BlockSpec((tr, cols), lambda i: (i, 0))
    return pl.pallas_call(
        body, name=name, grid=(rows // tr,),
        in_specs=[pl.BlockSpec((k, tr, cols), lambda i: (0, i, 0)), blk, blk, blk],
        out_specs=[blk] * 4, out_shape=[jax.ShapeDtypeStruct((rows, cols), F32)] * 4,
        compiler_params=_cparams("parallel"),
    )(parts, w, m, v)


COL_SHARDED = ("w_in", "w_gate", "w_up", "w_ple_proj")
REPLICATED = (("g_mix", 1024), ("conv_b", 512), ("q_norm_g", 64), ("k_norm_g", 64), ("g_out_conv", 512),
              ("g_out_attn", 512), ("g_ffn", 1024), ("ffn_conv_b", 2816), ("g_ple", 1024))
CONV_SHARDED = (("conv_w", CONV_W), ("ffn_conv_w", D_FF))


def _pack_rows(arrays):
    return jnp.concatenate([a.reshape(-1, 1024) for a in arrays], axis=0)


def _unpack_big(flat, shard_shapes):
    out, r0 = {}, 0
    for name, rows in BIG_ROWS:
        out[name] = flat[r0:r0 + rows].reshape(shard_shapes[name])
        r0 += rows
    return out


def _gathered_to_full(gathered, shard_shapes):
    out, r0 = {}, 0
    for name, rows in BIG_ROWS:
        a = gathered[:, r0:r0 + rows].reshape((N_DEV,) + shard_shapes[name])
        if name in COL_SHARDED:
            a = a.transpose(1, 0, 2)
            out[name] = a.reshape(a.shape[0], -1)
        else:
            out[name] = a.reshape(-1, a.shape[2])
        r0 += rows
    return out


def _full_to_stacked(grads, shard_shapes):
    parts = []
    for name, rows in BIG_ROWS:
        sr, sc = shard_shapes[name]
        a = grads[name]
        if name in COL_SHARDED:
            a = a.reshape(sr, N_DEV, sc).transpose(1, 0, 2)
        else:
            a = a.reshape(N_DEV, sr, sc)
        parts.append(a.reshape(N_DEV, rows, 1024))
    return jnp.concatenate(parts, axis=1)


def _pad_rows(vec, rows):
    return jnp.pad(vec, (0, rows * 1024 - vec.shape[0])).reshape(rows, 1024)


def kernel(x, p, g_mix, w_in, conv_w, conv_b, q_norm_g, k_norm_g, g_out_conv, g_out_attn, w_out, g_ffn, w_gate, w_up, ffn_conv_w, ffn_conv_b, w_down, g_ple, w_ple_gate, w_ple_proj, loss_target, m_g_mix, m_w_in, m_conv_w, m_conv_b, m_q_norm_g, m_k_norm_g, m_g_out_conv, m_g_out_attn, m_w_out, m_g_ffn, m_w_gate, m_w_up, m_ffn_conv_w, m_ffn_conv_b, m_w_down, m_g_ple, m_w_ple_gate, m_w_ple_proj, v_g_mix, v_w_in, v_conv_w, v_conv_b, v_q_norm_g, v_k_norm_g, v_g_out_conv, v_g_out_attn, v_w_out, v_g_ffn, v_w_gate, v_w_up, v_ffn_conv_w, v_ffn_conv_b, v_w_down, v_g_ple, v_w_ple_gate, v_w_ple_proj):
    args = dict(locals())
    names = ["g_mix", "w_in", "conv_w", "conv_b", "q_norm_g", "k_norm_g", "g_out_conv", "g_out_attn", "w_out", "g_ffn",
             "w_gate", "w_up", "ffn_conv_w", "ffn_conv_b", "w_down", "g_ple", "w_ple_gate", "w_ple_proj"]
    big = [n for n, _ in BIG_ROWS]
    conv = [n for n, _ in CONV_SHARDED]
    wts = {n: (args[n][0] if n in big or n in conv else args[n]) for n in names}
    mom = {n: (args["m_" + n][0] if n in big or n in conv else args["m_" + n]) for n in names}
    var = {n: (args["v_" + n][0] if n in big or n in conv else args["v_" + n]) for n in names}
    shard_shapes = {n: wts[n].shape for n in big}
    dev = 4 * lax.axis_index("x") + 2 * lax.axis_index("y") + lax.axis_index("c")
    core = lax.axis_index("c").astype(jnp.int32).reshape(1)

    conv_local = _pad_rows(jnp.concatenate([wts[n].reshape(-1) for n in conv]), 8).reshape(8, 1024)
    gathered, conv_all = _all_gather([_pack_rows([wts[n].astype(BF16) for n in big]), conv_local], "gather_weights")
    full = dict(wts)
    full.update(_gathered_to_full(gathered, shard_shapes))
    off = 0
    for n, width in CONV_SHARDED:
        sc = width // N_DEV
        a = conv_all.reshape(N_DEV, -1)[:, off:off + 3 * sc].reshape(N_DEV, 3, sc)
        full[n] = a.transpose(1, 0, 2).reshape(3, width)
        off += 3 * sc

    loss, dx, grads = _local_step(x[0], p[0, 0], loss_target[0], full, (512, 256))

    stacked = _full_to_stacked(grads, shard_shapes).astype(BF16).reshape(N_CHIP, 2, BIG_TOTAL, 1024)
    landed = _sibling_exchange(stacked)
    contributions = _chip_exchange(_pair_sum(stacked, landed, core))

    small = jnp.concatenate([grads[n].reshape(-1) for n, _ in REPLICATED] + [grads[n].reshape(-1) for n in conv]
                            + [loss.reshape(1)])
    (small_all,) = _all_gather([_pad_rows(small, SMALL_ROWS)], "gather_small_grads")

    g_big, d_big, m_big, v_big = _adamw(contributions, _pack_rows([wts[n] for n in big]),
                                        _pack_rows([mom[n] for n in big]), _pack_rows([var[n] for n in big]),
                                        "adamw_large")
    n_rep = sum(s for _, s in REPLICATED)
    conv_sizes = [3 * w_ // N_DEV for _, w_ in CONV_SHARDED]

    def small_state(src):
        flat = jnp.concatenate([src[n].reshape(-1) for n, _ in REPLICATED] + [src[n].reshape(-1) for n in conv])
        return _pad_rows(flat, 16)

    rep_all = small_all.reshape(N_DEV, -1)[:, :n_rep]
    conv_parts, off = [], n_rep
    for (n, width), size in zip(CONV_SHARDED, conv_sizes):
        sc = width // N_DEV
        a = small_all.reshape(N_DEV, -1)[:, off:off + 3 * width].reshape(N_DEV, 3, width)
        conv_parts.append(lax.dynamic_slice(a, (0, 0, dev * sc), (N_DEV, 3, sc)).reshape(N_DEV, size))
        off += 3 * width
    loss_total = jnp.sum(small_all.reshape(N_DEV, -1)[:, off])
    small_parts = jnp.concatenate([rep_all] + conv_parts, axis=1)
    small_parts = jnp.pad(small_parts, ((0, 0), (0, 16 * 1024 - small_parts.shape[1]))).reshape(N_DEV, 16, 1024)
    g_sm, d_sm, m_sm, v_sm = _adamw(small_parts, small_state(wts), small_state(mom), small_state(var), "adamw_small")

    def unpack(big_flat, small_flat, like):
        out = _unpack_big(big_flat, shard_shapes)
        flat, o = small_flat.reshape(-1), 0
        for n, s in list(REPLICATED) + [(n, sz) for (n, _), sz in zip(CONV_SHARDED, conv_sizes)]:
            out[n] = flat[o:o + s]
            o += s
        return [out[n].reshape(like[n].shape) for n in names]

    like = {n: args[n] for n in names}
    return (loss_total, dx[None], *unpack(g_big, g_sm, like), *unpack(d_big, d_sm, like),
            *unpack(m_big, m_sm, like), *unpack(v_big, v_sm, like))
```

```python
import functools

import jax
import jax.numpy as jnp
from jax import lax
from jax.experimental import pallas as pl
from jax.experimental.pallas import tpu as pltpu

F32 = jnp.float32
BF16 = jnp.bfloat16

D_MODEL = 1024
CONV_W = 512
ATTN_W = 512
HEAD_DIM = 64
D_FF = 2816
PLE_DIM = 256
IN_COLS = 3 * CONV_W + 3 * ATTN_W
EPS = 1e-6
QK_BLOCK = 128
DILATIONS = (1, 4, 16)
ATTN_SCALE = HEAD_DIM ** -0.5

ADAM_LR = 0.001
ADAM_B1 = 0.9
ADAM_B2 = 0.999
ADAM_EPS = 1e-08
ADAM_WD = 0.01
ADAM_STEP = 10

N_DEV = 8
N_CHIP = 4
V7X_VMEM_LIMIT = 56 * 1024 * 1024
FF_CHUNKS = 2

BIG_ROWS = (("w_in", 384), ("w_out", 128), ("w_gate", 352), ("w_up", 352), ("w_down", 352),
            ("w_ple_gate", 128), ("w_ple_proj", 32))
BIG_TOTAL = sum(r for _, r in BIG_ROWS)
SMALL_ROWS = 24


def _cparams(*sem):
    return pltpu.CompilerParams(dimension_semantics=sem, vmem_limit_bytes=V7X_VMEM_LIMIT)


def _mm(a, b):
    return jnp.dot(a, b, preferred_element_type=F32)


def _mm_nt(a, b):
    return lax.dot_general(a, b, (((1,), (1,)), ((), ())), preferred_element_type=F32)


def _mm_tn(a, b):
    return lax.dot_general(a, b, (((0,), (0,)), ((), ())), preferred_element_type=F32)


def _full(shape):
    nd = len(shape)
    return pl.BlockSpec(shape, lambda *_: (0,) * nd)


def _rms_stats(x):
    r = lax.rsqrt(jnp.mean(x * x, axis=-1, keepdims=True) + EPS)
    return r, x * r


def _rms_bwd(dy, xhat, r, g):
    gd = dy * g
    return r * (gd - xhat * jnp.mean(gd * xhat, axis=-1, keepdims=True))


def _seg_sum64(v, bd_ref):
    outs = []
    for c in range(0, v.shape[1], 256):
        vc = v[:, c:c + 256]
        hi = vc.astype(BF16)
        lo = (vc - hi.astype(F32)).astype(BF16)
        outs.append(_mm(hi, bd_ref[...]) + _mm(lo, bd_ref[...]))
    return outs[0] if len(outs) == 1 else jnp.concatenate(outs, axis=1)


def _shift_rows(u, k, edge_rows):
    row = lax.broadcasted_iota(jnp.int32, u.shape, 0)
    out = pltpu.roll(u, k, 0)
    for j in range(k):
        out = jnp.where(row == j, edge_rows[k - 1 - j], out)
    return out


def _shift_rows_up(u, k, edge_rows):
    n = u.shape[0]
    row = lax.broadcasted_iota(jnp.int32, u.shape, 0)
    out = pltpu.roll(u, n - k, 0)
    for j in range(k):
        out = jnp.where(row == n - k + j, edge_rows[j], out)
    return out


def _conv_fwd(u, c1, c2, w_ref, b_ref):
    u1 = _shift_rows(u, 1, (c1,))
    u2 = _shift_rows(u, 2, (c1, c2))
    y = u2 * w_ref[0:1, :] + u1 * w_ref[1:2, :] + u * w_ref[2:3, :] + b_ref[...]
    return y, u1, u2


def _conv_bwd_input(dy, n1row, n2row, w_ref):
    d1 = _shift_rows_up(dy, 1, (n1row,))
    d2 = _shift_rows_up(dy, 2, (n1row, n2row))
    return dy * w_ref[2:3, :] + d1 * w_ref[1:2, :] + d2 * w_ref[0:1, :]


def _sigmoid(x):
    return 1.0 / (1.0 + jnp.exp(-x))


def _inproj_fwd(x, g_mix, w_in, conv_w, conv_b, qg, kg, bd, tm):
    t = x.shape[0]

    def body(x_ref, g_ref, w_ref, cw_ref, cb_ref, qg_ref, kg_ref, bd_ref,
             zc_ref, zqk_ref, yc_ref, q_ref, k_ref, v_ref, carry_ref):
        @pl.when(pl.program_id(0) == 0)
        def _():
            carry_ref[...] = jnp.zeros_like(carry_ref)

        _, xhat = _rms_stats(x_ref[...])
        h = (xhat * g_ref[...]).astype(BF16)
        zconv = _mm(h, w_ref[:, 0:3 * CONV_W])
        zc_ref[...] = zconv
        u = zconv[:, CONV_W:2 * CONV_W] * zconv[:, 2 * CONV_W:3 * CONV_W]
        cv, _, _ = _conv_fwd(u, carry_ref[7:8, :], carry_ref[6:7, :], cw_ref, cb_ref)
        yc_ref[...] = zconv[:, 0:CONV_W] * cv
        carry_ref[...] = u[tm - 8:tm, :]

        zqk = _mm(h, w_ref[:, 3 * CONV_W:3 * CONV_W + 2 * ATTN_W])
        zqk_ref[...] = zqk
        for j, (gain_ref, out_ref, scale) in enumerate(((qg_ref, q_ref, ATTN_SCALE), (kg_ref, k_ref, 1.0))):
            z = zqk[:, j * ATTN_W:(j + 1) * ATTN_W]
            r = lax.rsqrt(_seg_sum64(z * z, bd_ref) * (1.0 / HEAD_DIM) + EPS)
            out_ref[...] = z * r * gain_ref[...] * scale
        v_ref[...] = _mm(h, w_ref[:, 3 * CONV_W + 2 * ATTN_W:IN_COLS])

    def blk(c):
        return pl.BlockSpec((tm, c), lambda i: (i, 0))

    return pl.pallas_call(
        body, name="inproj_fwd", grid=(t // tm,),
        in_specs=[blk(D_MODEL), _full((1, D_MODEL)), _full((D_MODEL, IN_COLS)), _full((3, CONV_W)),
                  _full((1, CONV_W)), _full((1, ATTN_W)), _full((1, ATTN_W)), _full((256, 256))],
        out_specs=[blk(3 * CONV_W), blk(2 * ATTN_W), blk(CONV_W), blk(ATTN_W), blk(ATTN_W), blk(ATTN_W)],
        out_shape=[jax.ShapeDtypeStruct((t, 3 * CONV_W), F32), jax.ShapeDtypeStruct((t, 2 * ATTN_W), F32),
                   jax.ShapeDtypeStruct((t, CONV_W), F32), jax.ShapeDtypeStruct((t, ATTN_W), F32),
                   jax.ShapeDtypeStruct((t, ATTN_W), F32), jax.ShapeDtypeStruct((t, ATTN_W), F32)],
        scratch_shapes=[pltpu.VMEM((8, CONV_W), F32)],
        compiler_params=_cparams("arbitrary"),
    )(x, g_mix, w_in, conv_w, conv_b, qg, kg, bd)


SUPER = 16 * QK_BLOCK
KEYS = 2 * QK_BLOCK


def _rows(start, size, dil):
    return pl.ds(start, size) if dil == 1 else pl.ds(start, size, stride=dil)


def _attn_bias(sl_ref, dil):
    qi = lax.broadcasted_iota(jnp.int32, (KEYS, KEYS), 0)
    kj = lax.broadcasted_iota(jnp.int32, (KEYS, KEYS), 1)
    step = jnp.bitwise_and(qi, QK_BLOCK - 1) + QK_BLOCK - kj
    slope = jnp.where(qi < QK_BLOCK, sl_ref[0, 0:1, 0:1], sl_ref[0, 1:2, 0:1])
    bias = jnp.where(jnp.logical_and(step >= 0, step <= QK_BLOCK), -slope * (step * dil).astype(F32), -jnp.inf)
    return bias, kj >= QK_BLOCK


def _unit_start(u, dil):
    if dil == 1:
        return pl.multiple_of(u * QK_BLOCK, QK_BLOCK)
    if dil == 4:
        return jnp.bitwise_and(u, 3) + (u // 4) * (4 * QK_BLOCK)
    return u


def _stack_heads(a, head0):
    zero = jnp.zeros_like(a)
    return jnp.concatenate([jnp.where(head0, a, zero), jnp.where(head0, zero, a)], axis=0)


def _attn_fwd(q, k, v, slopes):
    t = q.shape[0]
    nsb = t // SUPER

    def body(q_ref, kc_ref, kp_ref, vc_ref, vp_ref, sl_ref, o_ref, l_ref, kk, vv, ob, lb):
        s = pl.program_id(1)
        kk[0:SUPER, :] = kp_ref[...]
        kk[SUPER:, :] = kc_ref[...]
        vv[0:SUPER, :] = vp_ref[...]
        vv[SUPER:, :] = vc_ref[...]
        head0 = lax.broadcasted_iota(jnp.int32, (QK_BLOCK, QK_BLOCK), 1) < HEAD_DIM

        for b, dil in enumerate(DILATIONS):
            bias, own_half = _attn_bias(sl_ref, dil)

            def unit(u, carry, b=b, dil=dil, bias=bias, own_half=own_half):
                start = _unit_start(u, dil)
                first_key = SUPER + start - QK_BLOCK * dil
                q2 = _stack_heads(q_ref[_rows(start, QK_BLOCK, dil), :].astype(BF16), head0)
                k2 = kk[_rows(first_key, KEYS, dil), :].astype(BF16)
                v2 = vv[_rows(first_key, KEYS, dil), :].astype(BF16)
                has_prev = jnp.logical_or(s > 0, start >= QK_BLOCK * dil)
                sc = jnp.where(jnp.logical_or(own_half, has_prev), _mm_nt(q2, k2) + bias, -jnp.inf)
                m = jnp.max(sc, axis=-1, keepdims=True)
                e = jnp.exp(sc - m)
                den = jnp.sum(e, axis=-1, keepdims=True)
                o2 = _mm(e.astype(BF16), v2) / den
                l2 = m + jnp.log(den)
                ob[b, _rows(start, QK_BLOCK, dil), :] = jnp.where(head0, o2[0:QK_BLOCK], o2[QK_BLOCK:])
                lb[b, _rows(start, QK_BLOCK, dil), :] = jnp.where(head0, l2[0:QK_BLOCK], l2[QK_BLOCK:])
                return carry

            lax.fori_loop(0, SUPER // QK_BLOCK, unit, 0, unroll=4)

        def merge(i, carry):
            rows = pl.ds(pl.multiple_of(i * 256, 256), 256)
            la, lb_, lc = lb[0, rows, :], lb[1, rows, :], lb[2, rows, :]
            mx = jnp.maximum(jnp.maximum(la, lb_), lc)
            wa, wb, wc = jnp.exp(la - mx), jnp.exp(lb_ - mx), jnp.exp(lc - mx)
            sw = wa + wb + wc
            o_ref[rows, :] = (wa * ob[0, rows, :] + wb * ob[1, rows, :] + wc * ob[2, rows, :]) / sw
            l_ref[rows, :] = mx + jnp.log(sw)
            return carry

        lax.fori_loop(0, SUPER // 256, merge, 0)

    cur = pl.BlockSpec((SUPER, QK_BLOCK), lambda p, s: (s, p))
    prev = pl.BlockSpec((SUPER, QK_BLOCK), lambda p, s: (jnp.maximum(s - 1, 0), p))
    return pl.pallas_call(
        body, name="attn_fwd", grid=(4, nsb),
        in_specs=[cur, cur, prev, cur, prev, pl.BlockSpec((1, 2, QK_BLOCK), lambda p, s: (p, 0, 0))],
        out_specs=[cur, cur],
        out_shape=[jax.ShapeDtypeStruct((t, ATTN_W), F32), jax.ShapeDtypeStruct((t, ATTN_W), F32)],
        scratch_shapes=[pltpu.VMEM((2 * SUPER, QK_BLOCK), F32), pltpu.VMEM((2 * SUPER, QK_BLOCK), F32),
                        pltpu.VMEM((3, SUPER, QK_BLOCK), F32), pltpu.VMEM((3, SUPER, QK_BLOCK), F32)],
        compiler_params=_cparams("parallel", "arbitrary"),
    )(q, k, k, v, v, slopes)


def _outproj_fwd(ya, yc, x, goc, goa, w_out, tm):
    t = x.shape[0]

    def body(ya_ref, yc_ref, x_ref, goc_ref, goa_ref, w_ref, x1_ref):
        _, ychat = _rms_stats(yc_ref[...])
        _, yahat = _rms_stats(ya_ref[...])
        acc = _mm((ychat * goc_ref[...]).astype(BF16), w_ref[0:CONV_W, :])
        acc += _mm((yahat * goa_ref[...]).astype(BF16), w_ref[CONV_W:, :])
        x1_ref[...] = x_ref[...] + acc

    def blk(c):
        return pl.BlockSpec((tm, c), lambda i: (i, 0))

    return pl.pallas_call(
        body, name="outproj_fwd", grid=(t // tm,),
        in_specs=[blk(ATTN_W), blk(CONV_W), blk(D_MODEL), _full((1, CONV_W)), _full((1, ATTN_W)),
                  _full((D_MODEL, D_MODEL))],
        out_specs=blk(D_MODEL),
        out_shape=jax.ShapeDtypeStruct((t, D_MODEL), F32),
        compiler_params=_cparams("parallel"),
    )(ya, yc, x, goc, goa, w_out)


def _ffn_fwd(x1, g_ffn, w_gate, w_up, w_down, fcw, fcb, tm):
    t = x1.shape[0]

    def body(x_ref, g_ref, wg_ref, wu_ref, wd_ref, cw_ref, cb_ref, gp_ref, up_ref, x2_ref, carry_ref):
        @pl.when(pl.program_id(0) == 0)
        def _():
            carry_ref[...] = jnp.zeros_like(carry_ref)

        xv = x_ref[...]
        _, xhat = _rms_stats(xv)
        h = (xhat * g_ref[...]).astype(BF16)
        gp = _mm(h, wg_ref[...])
        gp_ref[...] = gp
        gate, _, _ = _conv_fwd(gp, carry_ref[7:8, :], carry_ref[6:7, :], cw_ref, cb_ref)
        carry_ref[...] = gp[tm - 8:tm, :]
        up = _mm(h, wu_ref[...])
        up_ref[...] = up
        a = (gate * _sigmoid(gate) * up).astype(BF16)
        x2_ref[...] = xv + _mm(a, wd_ref[...])

    def blk(c):
        return pl.BlockSpec((tm, c), lambda i: (i, 0))

    return pl.pallas_call(
        body, name="ffn_fwd", grid=(t // tm,),
        in_specs=[blk(D_MODEL), _full((1, D_MODEL)), _full((D_MODEL, D_FF)), _full((D_MODEL, D_FF)),
                  _full((D_FF, D_MODEL)), _full((3, D_FF)), _full((1, D_FF))],
        out_specs=[blk(D_FF), blk(D_FF), blk(D_MODEL)],
        out_shape=[jax.ShapeDtypeStruct((t, D_FF), F32), jax.ShapeDtypeStruct((t, D_FF), F32),
                   jax.ShapeDtypeStruct((t, D_MODEL), F32)],
        scratch_shapes=[pltpu.VMEM((8, D_FF), F32)],
        compiler_params=_cparams("arbitrary"),
    )(x1, g_ffn, w_gate, w_up, w_down, fcw, fcb)


def _ple_fwd_bwd(x2, p, target, g_ple, w_pg, w_pp, tm):
    t = x2.shape[0]

    def body(x_ref, p_ref, t_ref, g_ref, wg_ref, wp_ref, dx_ref, loss_ref, dwg_ref, dwp_ref, dg_ref):
        @pl.when(pl.program_id(0) == 0)
        def _():
            loss_ref[...] = jnp.zeros_like(loss_ref)
            dwg_ref[...] = jnp.zeros_like(dwg_ref)
            dwp_ref[...] = jnp.zeros_like(dwp_ref)
            dg_ref[...] = jnp.zeros_like(dg_ref)

        xv = x_ref[...]
        r, xhat = _rms_stats(xv)
        g = g_ref[...]
        h = (xhat * g).astype(BF16)
        pg = _sigmoid(_mm(h, wg_ref[...]))
        pb = p_ref[...].astype(BF16)
        pp = _mm(pb, wp_ref[...])
        err = xv + pg * pp - t_ref[...]
        loss_ref[...] += 0.5 * jnp.sum(jnp.mean(err * err, axis=-1, keepdims=True))
        dx3 = err * (1.0 / D_MODEL)
        d_pp = (dx3 * pg).astype(BF16)
        d_pre = (dx3 * pp * pg * (1.0 - pg)).astype(BF16)
        dwp_ref[...] += _mm_tn(pb, d_pp)
        dwg_ref[...] += _mm_tn(h, d_pre)
        dh = _mm_nt(d_pre, wg_ref[...])
        dg_ref[...] += jnp.sum(dh * xhat, axis=0, keepdims=True)
        dx_ref[...] = dx3 + _rms_bwd(dh, xhat, r, g)

    def blk(c):
        return pl.BlockSpec((tm, c), lambda i: (i, 0))

    return pl.pallas_call(
        body, name="ple_fwd_bwd", grid=(t // tm,),
        in_specs=[blk(D_MODEL), blk(PLE_DIM), blk(D_MODEL), _full((1, D_MODEL)), _full((D_MODEL, D_MODEL)),
                  _full((PLE_DIM, D_MODEL))],
        out_specs=[blk(D_MODEL), _full((8, 128)), _full((D_MODEL, D_MODEL)), _full((PLE_DIM, D_MODEL)),
                   _full((1, D_MODEL))],
        out_shape=[jax.ShapeDtypeStruct((t, D_MODEL), F32), jax.ShapeDtypeStruct((8, 128), F32),
                   jax.ShapeDtypeStruct((D_MODEL, D_MODEL), F32), jax.ShapeDtypeStruct((PLE_DIM, D_MODEL), F32),
                   jax.ShapeDtypeStruct((1, D_MODEL), F32)],
        compiler_params=_cparams("arbitrary"),
    )(x2, p, target, g_ple, w_pg, w_pp)


def _ffn_bwd(dx2, x1, g_ffn, gp, up, w_gate, w_up, w_down, fcw, fcb, tm):
    t = x1.shape[0]
    nblk = t // tm
    fc = D_FF // FF_CHUNKS

    def body(dx_ref, x_ref, g_ref, gp_ref, gph_ref, up_ref, wg_ref, wu_ref, wd_ref, cw_ref, cb_ref,
             dh_ref, dwd_ref, dwu_ref, dwg_ref, dcw_ref, dcb_ref, carry_ref):
        i = pl.program_id(1)

        @pl.when(i == 0)
        def _():
            carry_ref[...] = jnp.zeros_like(carry_ref)
            dwd_ref[...] = jnp.zeros_like(dwd_ref)
            dwu_ref[...] = jnp.zeros_like(dwu_ref)
            dwg_ref[...] = jnp.zeros_like(dwg_ref)
            dcw_ref[...] = jnp.zeros_like(dcw_ref)
            dcb_ref[...] = jnp.zeros_like(dcb_ref)

        keep = (i < nblk - 1).astype(F32)
        dxb = dx_ref[...].astype(BF16)
        _, xhat = _rms_stats(x_ref[...])
        h = (xhat * g_ref[...]).astype(BF16)
        gp_v = gp_ref[...]
        gate, gp1, gp2 = _conv_fwd(gp_v, gph_ref[7:8, :] * keep, gph_ref[6:7, :] * keep, cw_ref, cb_ref)
        s = _sigmoid(gate)
        silu = gate * s
        up_v = up_ref[...]
        da = _mm_nt(dxb, wd_ref[...])
        dwd_ref[...] += _mm_tn((silu * up_v).astype(BF16), dxb)
        d_up = (da * silu).astype(BF16)
        d_gate = da * up_v * (s * (1.0 + gate * (1.0 - s)))
        dwu_ref[...] += _mm_tn(h, d_up)
        d_gp = _conv_bwd_input(d_gate, carry_ref[0:1, :], carry_ref[1:2, :], cw_ref).astype(BF16)
        carry_ref[...] = d_gate[0:8, :]
        dcw_ref[0:1, :] += jnp.sum(d_gate * gp2, axis=0, keepdims=True)
        dcw_ref[1:2, :] += jnp.sum(d_gate * gp1, axis=0, keepdims=True)
        dcw_ref[2:3, :] += jnp.sum(d_gate * gp_v, axis=0, keepdims=True)
        dcb_ref[...] += jnp.sum(d_gate, axis=0, keepdims=True)
        dwg_ref[...] += _mm_tn(h, d_gp)
        dh_ref[...] = _mm_nt(d_gp, wg_ref[...]) + _mm_nt(d_up, wu_ref[...])

    def rev(i):
        return nblk - 1 - i

    one = pl.Buffered(1)
    in_specs = [
        pl.BlockSpec((tm, D_MODEL), lambda j, i: (rev(i), 0)),
        pl.BlockSpec((tm, D_MODEL), lambda j, i: (rev(i), 0)),
        _full((1, D_MODEL)),
        pl.BlockSpec((tm, fc), lambda j, i: (rev(i), j)),
        pl.BlockSpec((8, fc), lambda j, i: (jnp.maximum(rev(i) * (tm // 8) - 1, 0), j)),
        pl.BlockSpec((tm, fc), lambda j, i: (rev(i), j)),
        pl.BlockSpec((D_MODEL, fc), lambda j, i: (0, j), pipeline_mode=one),
        pl.BlockSpec((D_MODEL, fc), lambda j, i: (0, j), pipeline_mode=one),
        pl.BlockSpec((fc, D_MODEL), lambda j, i: (j, 0), pipeline_mode=one),
        pl.BlockSpec((3, fc), lambda j, i: (0, j)),
        pl.BlockSpec((1, fc), lambda j, i: (0, j)),
    ]
    out_specs = [
        pl.BlockSpec((None, tm, D_MODEL), lambda j, i: (j, rev(i), 0)),
        pl.BlockSpec((fc, D_MODEL), lambda j, i: (j, 0), pipeline_mode=one),
        pl.BlockSpec((D_MODEL, fc), lambda j, i: (0, j), pipeline_mode=one),
        pl.BlockSpec((D_MODEL, fc), lambda j, i: (0, j), pipeline_mode=one),
        pl.BlockSpec((3, fc), lambda j, i: (0, j)),
        pl.BlockSpec((1, fc), lambda j, i: (0, j)),
    ]
    return pl.pallas_call(
        body, name="ffn_bwd", grid=(FF_CHUNKS, nblk), in_specs=in_specs, out_specs=out_specs,
        out_shape=[jax.ShapeDtypeStruct((FF_CHUNKS, t, D_MODEL), F32), jax.ShapeDtypeStruct((D_FF, D_MODEL), F32),
                   jax.ShapeDtypeStruct((D_MODEL, D_FF), F32), jax.ShapeDtypeStruct((D_MODEL, D_FF), F32),
                   jax.ShapeDtypeStruct((3, D_FF), F32), jax.ShapeDtypeStruct((1, D_FF), F32)],
        scratch_shapes=[pltpu.VMEM((8, fc), F32)],
        compiler_params=_cparams("arbitrary", "arbitrary"),
    )(dx2, x1, g_ffn, gp, gp, up, w_gate, w_up, w_down, fcw, fcb)


def _outproj_bwd(dh2, dx2, x1, g_ffn, w_out, yc, ya, goc, goa, zconv, conv_w, conv_b, bd, tm):
    t = x1.shape[0]
    nblk = t // tm

    def body(dh_ref, dx2_ref, x1_ref, g_ref, w_ref, yc_ref, ya_ref, goc_ref, goa_ref, zc_ref, zch_ref, cw_ref, cb_ref,
             bd_ref, dx1_ref, dya_ref, dd_ref, dzc_ref, dw_ref, dg_ref, dgoc_ref, dgoa_ref, dcw_ref, dcb_ref,
             carry_ref):
        i = pl.program_id(0)

        @pl.when(i == 0)
        def _():
            carry_ref[...] = jnp.zeros_like(carry_ref)
            for ref in (dw_ref, dg_ref, dgoc_ref, dgoa_ref, dcw_ref, dcb_ref):
                ref[...] = jnp.zeros_like(ref)

        keep = (i < nblk - 1).astype(F32)
        dh2_v = dh_ref[0]
        for j in range(1, FF_CHUNKS):
            dh2_v = dh2_v + dh_ref[j]
        r, xhat = _rms_stats(x1_ref[...])
        dg_ref[...] += jnp.sum(dh2_v * xhat, axis=0, keepdims=True)
        dx1 = dx2_ref[...] + _rms_bwd(dh2_v, xhat, r, g_ref[...])
        dx1_ref[...] = dx1
        dx1b = dx1.astype(BF16)
        dy = _mm_nt(dx1b, w_ref[...])

        yc_v = yc_ref[...]
        rc, ychat = _rms_stats(yc_v)
        dw_ref[0:CONV_W, :] += _mm_tn((ychat * goc_ref[...]).astype(BF16), dx1b)
        dyc = dy[:, 0:CONV_W]
        dgoc_ref[...] += jnp.sum(dyc * ychat, axis=0, keepdims=True)
        d_yc = _rms_bwd(dyc, ychat, rc, goc_ref[...])

        ya_v = ya_ref[...]
        ra, yahat = _rms_stats(ya_v)
        dw_ref[CONV_W:, :] += _mm_tn((yahat * goa_ref[...]).astype(BF16), dx1b)
        dya = dy[:, CONV_W:]
        dgoa_ref[...] += jnp.sum(dya * yahat, axis=0, keepdims=True)
        d_ya = _rms_bwd(dya, yahat, ra, goa_ref[...])
        dya_ref[...] = d_ya
        dd_ref[...] = _seg_sum64(d_ya * ya_v, bd_ref)

        zb = zc_ref[:, 0:CONV_W]
        zc = zc_ref[:, CONV_W:2 * CONV_W]
        zx = zc_ref[:, 2 * CONV_W:3 * CONV_W]
        u = zc * zx
        uh = zch_ref[:, CONV_W:2 * CONV_W] * zch_ref[:, 2 * CONV_W:3 * CONV_W] * keep
        cv, u1, u2 = _conv_fwd(u, uh[7:8, :], uh[6:7, :], cw_ref, cb_ref)
        d_cv = d_yc * zb
        d_u = _conv_bwd_input(d_cv, carry_ref[0:1, :], carry_ref[1:2, :], cw_ref)
        carry_ref[...] = d_cv[0:8, :]
        dcw_ref[0:1, :] += jnp.sum(d_cv * u2, axis=0, keepdims=True)
        dcw_ref[1:2, :] += jnp.sum(d_cv * u1, axis=0, keepdims=True)
        dcw_ref[2:3, :] += jnp.sum(d_cv * u, axis=0, keepdims=True)
        dcb_ref[...] += jnp.sum(d_cv, axis=0, keepdims=True)
        dzc_ref[:, 0:CONV_W] = d_yc * cv
        dzc_ref[:, CONV_W:2 * CONV_W] = d_u * zx
        dzc_ref[:, 2 * CONV_W:3 * CONV_W] = d_u * zc

    def rev(i):
        return nblk - 1 - i

    def blk(c):
        return pl.BlockSpec((tm, c), lambda i: (rev(i), 0))

    in_specs = [
        pl.BlockSpec((FF_CHUNKS, tm, D_MODEL), lambda i: (0, rev(i), 0)),
        blk(D_MODEL), blk(D_MODEL), _full((1, D_MODEL)), _full((D_MODEL, D_MODEL)),
        blk(CONV_W), blk(ATTN_W), _full((1, CONV_W)), _full((1, ATTN_W)),
        blk(3 * CONV_W),
        pl.BlockSpec((8, 3 * CONV_W), lambda i: (jnp.maximum(rev(i) * (tm // 8) - 1, 0), 0)),
        _full((3, CONV_W)), _full((1, CONV_W)), _full((256, 256)),
    ]
    out_specs = [blk(D_MODEL), blk(ATTN_W), blk(ATTN_W), blk(3 * CONV_W), _full((D_MODEL, D_MODEL)),
                 _full((1, D_MODEL)), _full((1, CONV_W)), _full((1, ATTN_W)), _full((3, CONV_W)), _full((1, CONV_W))]
    return pl.pallas_call(
        body, name="outproj_bwd", grid=(nblk,), in_specs=in_specs, out_specs=out_specs,
        out_shape=[jax.ShapeDtypeStruct((t, D_MODEL), F32), jax.ShapeDtypeStruct((t, ATTN_W), F32),
                   jax.ShapeDtypeStruct((t, ATTN_W), F32), jax.ShapeDtypeStruct((t, 3 * CONV_W), F32),
                   jax.ShapeDtypeStruct((D_MODEL, D_MODEL), F32), jax.ShapeDtypeStruct((1, D_MODEL), F32),
                   jax.ShapeDtypeStruct((1, CONV_W), F32), jax.ShapeDtypeStruct((1, ATTN_W), F32),
                   jax.ShapeDtypeStruct((3, CONV_W), F32), jax.ShapeDtypeStruct((1, CONV_W), F32)],
        scratch_shapes=[pltpu.VMEM((8, CONV_W), F32)],
        compiler_params=_cparams("arbitrary"),
    )(dh2, dx2, x1, g_ffn, w_out, yc, ya, goc, goa, zconv, zconv, conv_w, conv_b, bd)


def _attn_bwd(q, k, v, dya, lse, dd, slopes):
    t = q.shape[0]
    nsb = t // SUPER

    def body(q_ref, kc_ref, kp_ref, vc_ref, vp_ref, dy_ref, l_ref, d_ref, sl_ref, dq_ref, dk_ref, dv_ref,
             kk, vv, dkacc, dvacc):
        s = pl.program_id(1)

        @pl.when(s == 0)
        def _():
            dkacc[...] = jnp.zeros_like(dkacc)
            dvacc[...] = jnp.zeros_like(dvacc)

        dkacc[0:SUPER, :] = dkacc[SUPER:, :]
        dvacc[0:SUPER, :] = dvacc[SUPER:, :]
        dkacc[SUPER:, :] = jnp.zeros((SUPER, QK_BLOCK), F32)
        dvacc[SUPER:, :] = jnp.zeros((SUPER, QK_BLOCK), F32)

        @pl.when(s < nsb)
        def _():
            kk[0:SUPER, :] = kp_ref[...]
            kk[SUPER:, :] = kc_ref[...]
            vv[0:SUPER, :] = vp_ref[...]
            vv[SUPER:, :] = vc_ref[...]
            head0 = lax.broadcasted_iota(jnp.int32, (QK_BLOCK, QK_BLOCK), 1) < HEAD_DIM

            for b, dil in enumerate(DILATIONS):
                bias, own_half = _attn_bias(sl_ref, dil)

                def unit(u, carry, b=b, dil=dil, bias=bias, own_half=own_half):
                    start = _unit_start(u, dil)
                    first_key = SUPER + start - QK_BLOCK * dil
                    qrows = _rows(start, QK_BLOCK, dil)
                    krows = _rows(first_key, KEYS, dil)
                    q2 = _stack_heads(q_ref[qrows, :].astype(BF16), head0)
                    dy2 = _stack_heads(dy_ref[qrows, :].astype(BF16), head0)
                    lv, dv_ = l_ref[qrows, :], d_ref[qrows, :]
                    l2 = jnp.concatenate([lv[:, 0:1], lv[:, HEAD_DIM:HEAD_DIM + 1]], axis=0)
                    d2 = jnp.concatenate([dv_[:, 0:1], dv_[:, HEAD_DIM:HEAD_DIM + 1]], axis=0)
                    k2 = kk[krows, :].astype(BF16)
                    v2 = vv[krows, :].astype(BF16)
                    has_prev = jnp.logical_or(s > 0, start >= QK_BLOCK * dil)
                    sc = jnp.where(jnp.logical_or(own_half, has_prev), _mm_nt(q2, k2) + bias, -jnp.inf)
                    prob = jnp.exp(sc - l2)
                    ds = (prob * (_mm_nt(dy2, v2) - d2)).astype(BF16)
                    dvacc[krows, :] += _mm_tn(prob.astype(BF16), dy2)
                    dkacc[krows, :] += _mm_tn(ds, q2)
                    dq2 = _mm(ds, k2)
                    dq = jnp.where(head0, dq2[0:QK_BLOCK], dq2[QK_BLOCK:]) * ATTN_SCALE
                    if b == 0:
                        dq_ref[qrows, :] = dq
                    else:
                        dq_ref[qrows, :] += dq
                    return carry

                lax.fori_loop(0, SUPER // QK_BLOCK, unit, 0, unroll=4)

        dk_ref[...] = dkacc[0:SUPER, :]
        dv_ref[...] = dvacc[0:SUPER, :]

    def cur_map(p, s):
        return (jnp.minimum(s, nsb - 1), p)

    def prev_map(p, s):
        return (jnp.clip(s - 1, 0, nsb - 1), p)

    cur = pl.BlockSpec((SUPER, QK_BLOCK), cur_map)
    prev = pl.BlockSpec((SUPER, QK_BLOCK), prev_map)
    return pl.pallas_call(
        body, name="attn_bwd", grid=(4, nsb + 1),
        in_specs=[cur, cur, prev, cur, prev, cur, cur, cur, pl.BlockSpec((1, 2, QK_BLOCK), lambda p, s: (p, 0, 0))],
        out_specs=[cur, prev, prev],
        out_shape=[jax.ShapeDtypeStruct((t, ATTN_W), F32)] * 3,
        scratch_shapes=[pltpu.VMEM((2 * SUPER, QK_BLOCK), F32)] * 4,
        compiler_params=_cparams("parallel", "arbitrary"),
    )(q, k, k, v, v, dya, lse, dd, slopes)


def _attn_bwd_per_branch_unused(q, k, v, dya, lse, dd, slopes, dil):
    t = q.shape[0]
    length = t // dil
    chunk = _attn_chunk(t, dil)
    nch = length // chunk
    nb = chunk // QK_BLOCK
    nblocks = length // QK_BLOCK
    view = (length, dil * ATTN_W)
    ext = chunk + QK_BLOCK

    def body(q_ref, dy_ref, l_ref, d_ref, k_ref, v_ref, qn_ref, dyn_ref, ln_ref, dn_ref, kh_ref, vh_ref, sl_ref,
             dq_ref, dk_ref, dv_ref, qbuf, dybuf, lbuf, dbuf, kbuf, vbuf, dkacc, dvacc):
        c = pl.program_id(2)
        qbuf[0:chunk, :] = q_ref[...]
        qbuf[chunk:, :] = qn_ref[...]
        dybuf[0:chunk, :] = dy_ref[...].astype(BF16)
        dybuf[chunk:, :] = dyn_ref[...].astype(BF16)
        lbuf[0:chunk, :] = l_ref[...]
        lbuf[chunk:, :] = ln_ref[...]
        dbuf[0:chunk, :] = d_ref[...]
        dbuf[chunk:, :] = dn_ref[...]
        kbuf[0:QK_BLOCK, :] = kh_ref[...]
        kbuf[QK_BLOCK:, :] = k_ref[...]
        vbuf[0:QK_BLOCK, :] = vh_ref[...]
        vbuf[QK_BLOCK:, :] = v_ref[...]
        valid_cur, valid_prev, dist_cur, dist_prev, head0 = _attn_masks(dil)

        def pair(qb, dyb, lv, dv_, kb, vb, valid, dist):
            dq = jnp.zeros((QK_BLOCK, QK_BLOCK), F32)
            dk = jnp.zeros((QK_BLOCK, QK_BLOCK), F32)
            dvv = jnp.zeros((QK_BLOCK, QK_BLOCK), F32)
            for hh in range(2):
                sl = sl_ref[0, hh:hh + 1, :]
                hm = head0 if hh == 0 else jnp.logical_not(head0)
                col = hh * HEAD_DIM
                qm = jnp.where(hm, qb, jnp.zeros_like(qb))
                dym = jnp.where(hm, dyb, jnp.zeros_like(dyb))
                s = jnp.where(valid, _mm_nt(qm, kb) - sl * dist, -jnp.inf)
                prob = jnp.exp(s - lv[:, col:col + 1])
                ds = (prob * (_mm_nt(dym, vb) - dv_[:, col:col + 1])).astype(BF16)
                dvv += _mm_tn(prob.astype(BF16), dym)
                dk += _mm_tn(ds, qm)
                dq += jnp.where(hm, _mm(ds, kb), 0.0)
            return dq, dk, dvv

        def blk(j, carry):
            off = pl.multiple_of(j * QK_BLOCK, QK_BLOCK)
            nxt = pl.multiple_of(off + QK_BLOCK, QK_BLOCK)
            qb = qbuf[pl.ds(off, QK_BLOCK), :]
            dyb = dybuf[pl.ds(off, QK_BLOCK), :]
            lv = lbuf[pl.ds(off, QK_BLOCK), :]
            dv_ = dbuf[pl.ds(off, QK_BLOCK), :]
            dq_c, dk_c, dv_c = pair(qb, dyb, lv, dv_, kbuf[pl.ds(nxt, QK_BLOCK), :], vbuf[pl.ds(nxt, QK_BLOCK), :],
                                    valid_cur, dist_cur)
            dkacc[pl.ds(nxt, QK_BLOCK), :] = dk_c
            dvacc[pl.ds(nxt, QK_BLOCK), :] = dv_c
            has_prev = jnp.logical_or(c > 0, j > 0)
            dq_p, dk_p, dv_p = pair(qb, dyb, lv, dv_, kbuf[pl.ds(off, QK_BLOCK), :], vbuf[pl.ds(off, QK_BLOCK), :],
                                    jnp.logical_and(valid_prev, has_prev), dist_prev)

            @pl.when(j > 0)
            def _():
                dkacc[pl.ds(off, QK_BLOCK), :] += dk_p
                dvacc[pl.ds(off, QK_BLOCK), :] += dv_p

            dq_ref[pl.ds(off, QK_BLOCK), :] = (dq_c + dq_p) * ATTN_SCALE
            return carry

        lax.fori_loop(0, nb, blk, 0)

        @pl.when(c < nch - 1)
        def _():
            _, dk_p, dv_p = pair(qbuf[chunk:, :], dybuf[chunk:, :], lbuf[chunk:, :], dbuf[chunk:, :],
                                 kbuf[chunk:, :], vbuf[chunk:, :], valid_prev, dist_prev)
            dkacc[chunk:, :] += dk_p
            dvacc[chunk:, :] += dv_p

        dk_ref[...] = dkacc[QK_BLOCK:, :]
        dv_ref[...] = dvacc[QK_BLOCK:, :]

    def cmap(p, r, c):
        return (c, r * 4 + p)

    def before(p, r, c):
        return (jnp.maximum(c * nb - 1, 0), r * 4 + p)

    def after(p, r, c):
        return (jnp.minimum((c + 1) * nb, nblocks - 1), r * 4 + p)

    main = pl.BlockSpec((chunk, QK_BLOCK), cmap)
    hb = pl.BlockSpec((QK_BLOCK, QK_BLOCK), before)
    ha = pl.BlockSpec((QK_BLOCK, QK_BLOCK), after)
    qv, kv, vv = q.reshape(view), k.reshape(view), v.reshape(view)
    dyv, lv, ddv = dya.reshape(view), lse.reshape(view), dd.reshape(view)
    outs = pl.pallas_call(
        body, name=f"attn_bwd_d{dil}", grid=(4, dil, nch),
        in_specs=[main] * 6 + [ha] * 4 + [hb] * 2 + [pl.BlockSpec((1, 2, QK_BLOCK), lambda p, r, c: (p, 0, 0))],
        out_specs=[main] * 3,
        out_shape=[jax.ShapeDtypeStruct(view, F32)] * 3,
        scratch_shapes=[pltpu.VMEM((ext, QK_BLOCK), BF16), pltpu.VMEM((ext, QK_BLOCK), BF16),
                        pltpu.VMEM((ext, QK_BLOCK), F32), pltpu.VMEM((ext, QK_BLOCK), F32),
                        pltpu.VMEM((ext, QK_BLOCK), BF16), pltpu.VMEM((ext, QK_BLOCK), BF16),
                        pltpu.VMEM((ext, QK_BLOCK), F32), pltpu.VMEM((ext, QK_BLOCK), F32)],
        compiler_params=_cparams("arbitrary", "arbitrary", "arbitrary"),
    )(qv, dyv, lv, ddv, kv, vv, qv, dyv, lv, ddv, kv, vv, slopes)
    return [o.reshape(t, ATTN_W) for o in outs]


def _inproj_bwd(dq, dk, dv, dzconv, zqk, x, dx1, g_mix, w_in, qg, kg, bd, tm):
    t = x.shape[0]

    def body(dq_ref, dk_ref, dv_ref, dzc_ref, zqk_ref, x_ref, dx1_ref, g_ref, w_ref, qg_ref,
             kg_ref, bd_ref, dx_ref, dw_ref, dg_ref, dqg_ref, dkg_ref):
        @pl.when(pl.program_id(0) == 0)
        def _():
            for ref in (dw_ref, dg_ref, dqg_ref, dkg_ref):
                ref[...] = jnp.zeros_like(ref)

        parts = [dzc_ref[...].astype(BF16)]
        for j, (dn_ref, gain_ref, dgain_ref) in enumerate(((dq_ref, qg_ref, dqg_ref), (dk_ref, kg_ref, dkg_ref))):
            dn = dn_ref[...]
            z = zqk_ref[:, j * ATTN_W:(j + 1) * ATTN_W]
            r = lax.rsqrt(_seg_sum64(z * z, bd_ref) * (1.0 / HEAD_DIM) + EPS)
            zhat = z * r
            dgain_ref[...] += jnp.sum(dn * zhat, axis=0, keepdims=True)
            gd = dn * gain_ref[...]
            parts.append((r * (gd - zhat * (_seg_sum64(gd * zhat, bd_ref) * (1.0 / HEAD_DIM)))).astype(BF16))
        parts.append(dv_ref[...].astype(BF16))
        dz = jnp.concatenate(parts, axis=1)

        r, xhat = _rms_stats(x_ref[...])
        g = g_ref[...]
        dw_ref[...] += _mm_tn((xhat * g).astype(BF16), dz)
        dh = _mm_nt(dz, w_ref[...])
        dg_ref[...] += jnp.sum(dh * xhat, axis=0, keepdims=True)
        dx_ref[...] = dx1_ref[...] + _rms_bwd(dh, xhat, r, g)

    def blk(c):
        return pl.BlockSpec((tm, c), lambda i: (i, 0))

    return pl.pallas_call(
        body, name="inproj_bwd", grid=(t // tm,),
        in_specs=[blk(ATTN_W)] * 3 + [blk(3 * CONV_W), blk(2 * ATTN_W), blk(D_MODEL), blk(D_MODEL), _full((1, D_MODEL)),
                                      _full((D_MODEL, IN_COLS)), _full((1, ATTN_W)), _full((1, ATTN_W)),
                                      _full((256, 256))],
        out_specs=[blk(D_MODEL), _full((D_MODEL, IN_COLS)), _full((1, D_MODEL)), _full((1, ATTN_W)),
                   _full((1, ATTN_W))],
        out_shape=[jax.ShapeDtypeStruct((t, D_MODEL), F32), jax.ShapeDtypeStruct((D_MODEL, IN_COLS), F32),
                   jax.ShapeDtypeStruct((1, D_MODEL), F32), jax.ShapeDtypeStruct((1, ATTN_W), F32),
                   jax.ShapeDtypeStruct((1, ATTN_W), F32)],
        compiler_params=_cparams("arbitrary"),
    )(dq, dk, dv, dzconv, zqk, x, dx1, g_mix, w_in, qg, kg, bd)


def _local_step(x, p, target, w, tms):
    bd = jnp.kron(jnp.eye(4, dtype=F32), jnp.ones((HEAD_DIM, HEAD_DIM), F32)).astype(BF16)
    qg = jnp.tile(w["q_norm_g"], (1, 8))
    kg = jnp.tile(w["k_norm_g"], (1, 8))
    slopes = jnp.exp2(-jnp.arange(1, 9, dtype=F32))
    slopes = jnp.broadcast_to(slopes.reshape(4, 2, 1), (4, 2, QK_BLOCK))

    zconv, zqk, yc, q, k, v = _inproj_fwd(x, w["g_mix"], w["w_in"], w["conv_w"], w["conv_b"], qg, kg, bd, tms[0])
    ya, lse = _attn_fwd(q, k, v, slopes)
    x1 = _outproj_fwd(ya, yc, x, w["g_out_conv"], w["g_out_attn"], w["w_out"], tms[0])
    gp, up, x2 = _ffn_fwd(x1, w["g_ffn"], w["w_gate"], w["w_up"], w["w_down"], w["ffn_conv_w"], w["ffn_conv_b"], tms[1])
    dx2, loss, dw_pg, dw_pp, dg_ple = _ple_fwd_bwd(x2, p, target, w["g_ple"], w["w_ple_gate"], w["w_ple_proj"], tms[0])
    dh2, dw_down, dw_up, dw_gate, dfcw, dfcb = _ffn_bwd(dx2, x1, w["g_ffn"], gp, up, w["w_gate"], w["w_up"],
                                                        w["w_down"], w["ffn_conv_w"], w["ffn_conv_b"], tms[1])
    dx1, dya, dd, dzconv, dw_out, dg_ffn, dgoc, dgoa, dcw, dcb = _outproj_bwd(
        dh2, dx2, x1, w["g_ffn"], w["w_out"], yc, ya, w["g_out_conv"], w["g_out_attn"], zconv, w["conv_w"],
        w["conv_b"], bd, tms[1])
    dq, dk, dv = _attn_bwd(q, k, v, dya, lse, dd, slopes)
    dx, dw_in, dg_mix, dqg, dkg = _inproj_bwd(dq, dk, dv, dzconv, zqk, x, dx1, w["g_mix"], w["w_in"], qg, kg, bd,
                                              tms[1])
    grads = {
        "g_mix": dg_mix, "w_in": dw_in, "conv_w": dcw, "conv_b": dcb,
        "q_norm_g": dqg.reshape(8, HEAD_DIM).sum(0, keepdims=True),
        "k_norm_g": dkg.reshape(8, HEAD_DIM).sum(0, keepdims=True),
        "g_out_conv": dgoc, "g_out_attn": dgoa, "w_out": dw_out, "g_ffn": dg_ffn, "w_gate": dw_gate, "w_up": dw_up,
        "ffn_conv_w": dfcw, "ffn_conv_b": dfcb, "w_down": dw_down, "g_ple": dg_ple, "w_ple_gate": dw_pg,
        "w_ple_proj": dw_pp,
    }
    return loss[0, 0], dx, grads


ANY = pl.BlockSpec(memory_space=pl.ANY)
MESH = pl.DeviceIdType.MESH


def _all_gather(shards, name):
    n = len(shards)

    def body(*refs):
        ins, outs = refs[:n], refs[n:2 * n]
        send_sems, recv_sems, local_sems = refs[2 * n:]
        x, y, c = lax.axis_index("x"), lax.axis_index("y"), lax.axis_index("c")
        me, sibling = (x, y, c), (x, y, 1 - c)
        chips = [(1 - x, y), (x, 1 - y), (1 - x, 1 - y)]

        def slot(dev):
            return 4 * dev[0] + 2 * dev[1] + dev[2]

        def copy(b, k, block, to, src=None):
            dst = outs[b].at[slot(block)]
            return pltpu.make_async_remote_copy(
                src_ref=dst if src is None else src, dst_ref=dst, send_sem=send_sems.at[b, k],
                recv_sem=recv_sems.at[b, k], device_id=to, device_id_type=MESH)

        mine = [pltpu.make_async_copy(ins[b], outs[b].at[slot(me)], local_sems.at[b]) for b in range(n)]
        first, passed = [], []
        for b in range(n):
            mine[b].start()
            first.append(copy(b, 0, me, sibling, src=ins[b]))
            first += [copy(b, 1 + j, me, (*chip, c), src=ins[b]) for j, chip in enumerate(chips)]
        for cp in first:
            cp.start()
        for j, chip in enumerate(chips):
            for b in range(n):
                copy(b, 1 + j, (*chip, c), me).wait_recv()
                fwd = copy(b, 4 + j, (*chip, c), sibling)
                fwd.start()
                passed.append(fwd)
        for b in range(n):
            copy(b, 0, sibling, me).wait_recv()
            for j, chip in enumerate(chips):
                copy(b, 4 + j, (*chip, 1 - c), me).wait_recv()
        for cp in first + passed:
            cp.wait_send()
        for cp in mine:
            cp.wait()

    return pl.pallas_call(
        body, name=name,
        in_specs=[ANY] * n, out_specs=[ANY] * n,
        out_shape=[jax.ShapeDtypeStruct((N_DEV,) + s.shape, s.dtype) for s in shards],
        scratch_shapes=[pltpu.SemaphoreType.DMA((n, 7)), pltpu.SemaphoreType.DMA((n, 7)),
                        pltpu.SemaphoreType.DMA((n,))],
    )(*shards)


def _row_tile(rows):
    for tr in range(min(rows, 512), 15, -16):
        if rows % tr == 0:
            return tr
    return rows


def _sibling_exchange(gs):
    n = len(gs)

    def body(*refs):
        g_refs, land_refs = refs[:n], refs[n:2 * n]
        send_sems, recv_sems = refs[2 * n:]
        x, y, c = lax.axis_index("x"), lax.axis_index("y"), lax.axis_index("c")
        copies = [pltpu.make_async_remote_copy(
            src_ref=g_refs[b].at[k, 1 - c], dst_ref=land_refs[b].at[k], send_sem=send_sems.at[b, k],
            recv_sem=recv_sems.at[b, k], device_id=(x, y, 1 - c), device_id_type=MESH)
            for b in range(n) for k in range(N_CHIP)]
        for cp in copies:
            cp.start()
        for cp in copies:
            cp.wait()

    return pl.pallas_call(
        body, name="rs_sibling_exchange", in_specs=[ANY] * n, out_specs=[ANY] * n,
        out_shape=[jax.ShapeDtypeStruct((N_CHIP,) + g.shape[2:], g.dtype) for g in gs],
        scratch_shapes=[pltpu.SemaphoreType.DMA((n, N_CHIP)), pltpu.SemaphoreType.DMA((n, N_CHIP))],
    )(*gs)


def _pair_sum(g, land, core, name):
    rows, cols = land.shape[1:]
    tr = _row_tile(rows)

    def body(c_ref, g_ref, l_ref, o_ref):
        o_ref[...] = (g_ref[...].astype(F32) + l_ref[...].astype(F32)).astype(o_ref.dtype)

    return pl.pallas_call(
        body, name=f"rs_pair_sum_{name}",
        grid_spec=pltpu.PrefetchScalarGridSpec(
            num_scalar_prefetch=1, grid=(N_CHIP, rows // tr),
            in_specs=[pl.BlockSpec((None, None, tr, cols), lambda k, i, c_ref: (k, c_ref[0], i, 0)),
                      pl.BlockSpec((None, tr, cols), lambda k, i, c_ref: (k, i, 0))],
            out_specs=pl.BlockSpec((None, tr, cols), lambda k, i, c_ref: (k, i, 0))),
        out_shape=jax.ShapeDtypeStruct(land.shape, land.dtype),
        compiler_params=_cparams("parallel", "parallel"),
    )(core, g, land)


def _chip_exchange(parts):
    n = len(parts)

    def body(*refs):
        p_refs, land_refs = refs[:n], refs[n:2 * n]
        send_sems, recv_sems, local_sems = refs[2 * n:]
        x, y, c = lax.axis_index("x"), lax.axis_index("y"), lax.axis_index("c")
        mine = 2 * x + y
        chips = [(1 - x, y), (x, 1 - y), (1 - x, 1 - y)]
        own = [pltpu.make_async_copy(p_refs[b].at[mine], land_refs[b].at[mine], local_sems.at[b]) for b in range(n)]
        for cp in own:
            cp.start()
        copies = [pltpu.make_async_remote_copy(
            src_ref=p_refs[b].at[2 * cx + cy], dst_ref=land_refs[b].at[mine], send_sem=send_sems.at[b, j],
            recv_sem=recv_sems.at[b, j], device_id=(cx, cy, c), device_id_type=MESH)
            for b in range(n) for j, (cx, cy) in enumerate(chips)]
        for cp in copies:
            cp.start()
        for b in range(n):
            for j, (cx, cy) in enumerate(chips):
                pltpu.make_async_remote_copy(
                    src_ref=p_refs[b].at[mine], dst_ref=land_refs[b].at[2 * cx + cy], send_sem=send_sems.at[b, j],
                    recv_sem=recv_sems.at[b, j], device_id=(cx, cy, c), device_id_type=MESH).wait_recv()
        for cp in copies:
            cp.wait_send()
        for cp in own:
            cp.wait()

    return pl.pallas_call(
        body, name="rs_chip_exchange", in_specs=[ANY] * n, out_specs=[ANY] * n,
        out_shape=[jax.ShapeDtypeStruct(p.shape, p.dtype) for p in parts],
        scratch_shapes=[pltpu.SemaphoreType.DMA((n, 3)), pltpu.SemaphoreType.DMA((n, 3)),
                        pltpu.SemaphoreType.DMA((n,))],
    )(*parts)


def _adamw(parts, w, m, v, name):
    k, rows, cols = parts.shape
    tr = _row_tile(rows)
    c1 = 1.0 / (1.0 - ADAM_B1 ** ADAM_STEP)
    c2 = 1.0 / (1.0 - ADAM_B2 ** ADAM_STEP)

    def body(p_ref, w_ref, m_ref, v_ref, g_ref, d_ref, nm_ref, nv_ref):
        g = p_ref[0].astype(F32)
        for j in range(1, k):
            g = g + p_ref[j].astype(F32)
        g_ref[...] = g
        nm = ADAM_B1 * m_ref[...] + (1.0 - ADAM_B1) * g
        nv = ADAM_B2 * v_ref[...] + (1.0 - ADAM_B2) * (g * g)
        nm_ref[...] = nm
        nv_ref[...] = nv
        d_ref[...] = -ADAM_LR * ((nm * c1) / (jnp.sqrt(nv * c2) + ADAM_EPS) + ADAM_WD * w_ref[...])

    blk = pl.BlockSpec((tr, cols), lambda i: (i, 0))
    return pl.pallas_call(
        body, name=name, grid=(rows // tr,),
        in_specs=[pl.BlockSpec((k, tr, cols), lambda i: (0, i, 0)), blk, blk, blk],
        out_specs=[blk] * 4, out_shape=[jax.ShapeDtypeStruct((rows, cols), F32)] * 4,
        compiler_params=_cparams("parallel"),
    )(parts, w, m, v)


COL_SHARDED = ("w_in", "w_gate", "w_up", "w_ple_proj")
REPLICATED = (("g_mix", 1024), ("conv_b", 512), ("q_norm_g", 64), ("k_norm_g", 64), ("g_out_conv", 512),
              ("g_out_attn", 512), ("g_ffn", 1024), ("ffn_conv_b", 2816), ("g_ple", 1024))
CONV_SHARDED = (("conv_w", CONV_W), ("ffn_conv_w", D_FF))


def _gathered_to_full(name, gathered):
    if name in COL_SHARDED:
        return gathered.transpose(1, 0, 2).reshape(gathered.shape[1], -1)
    return gathered.reshape(-1, gathered.shape[2])


def _full_to_stacked(name, grad, shard_shape):
    sr, sc = shard_shape
    if name in COL_SHARDED:
        a = grad.reshape(sr, N_DEV, sc).transpose(1, 0, 2)
    else:
        a = grad.reshape(N_DEV, sr, sc)
    return a.astype(BF16).reshape(N_CHIP, 2, sr, sc)


def _pad_rows(vec, rows):
    return jnp.pad(vec, (0, rows * 1024 - vec.shape[0])).reshape(rows, 1024)


def kernel(x, p, g_mix, w_in, conv_w, conv_b, q_norm_g, k_norm_g, g_out_conv, g_out_attn, w_out, g_ffn, w_gate, w_up, ffn_conv_w, ffn_conv_b, w_down, g_ple, w_ple_gate, w_ple_proj, loss_target, m_g_mix, m_w_in, m_conv_w, m_conv_b, m_q_norm_g, m_k_norm_g, m_g_out_conv, m_g_out_attn, m_w_out, m_g_ffn, m_w_gate, m_w_up, m_ffn_conv_w, m_ffn_conv_b, m_w_down, m_g_ple, m_w_ple_gate, m_w_ple_proj, v_g_mix, v_w_in, v_conv_w, v_conv_b, v_q_norm_g, v_k_norm_g, v_g_out_conv, v_g_out_attn, v_w_out, v_g_ffn, v_w_gate, v_w_up, v_ffn_conv_w, v_ffn_conv_b, v_w_down, v_g_ple, v_w_ple_gate, v_w_ple_proj):
    args = dict(locals())
    names = ["g_mix", "w_in", "conv_w", "conv_b", "q_norm_g", "k_norm_g", "g_out_conv", "g_out_attn", "w_out", "g_ffn",
             "w_gate", "w_up", "ffn_conv_w", "ffn_conv_b", "w_down", "g_ple", "w_ple_gate", "w_ple_proj"]
    big = [n for n, _ in BIG_ROWS]
    conv = [n for n, _ in CONV_SHARDED]
    wts = {n: (args[n][0] if n in big or n in conv else args[n]) for n in names}
    mom = {n: (args["m_" + n][0] if n in big or n in conv else args["m_" + n]) for n in names}
    var = {n: (args["v_" + n][0] if n in big or n in conv else args["v_" + n]) for n in names}
    shard_shapes = {n: wts[n].shape for n in big}
    dev = 4 * lax.axis_index("x") + 2 * lax.axis_index("y") + lax.axis_index("c")
    core = lax.axis_index("c").astype(jnp.int32).reshape(1)

    conv_local = _pad_rows(jnp.concatenate([wts[n].reshape(-1) for n in conv]), 8).reshape(8, 1024)
    *gathered, conv_all = _all_gather([wts[n].astype(BF16) for n in big] + [conv_local], "gather_weights")
    full = dict(wts)
    full.update({n: _gathered_to_full(n, g) for n, g in zip(big, gathered)})
    off = 0
    for n, width in CONV_SHARDED:
        sc = width // N_DEV
        a = conv_all.reshape(N_DEV, -1)[:, off:off + 3 * sc].reshape(N_DEV, 3, sc)
        full[n] = a.transpose(1, 0, 2).reshape(3, width)
        off += 3 * sc

    loss, dx, grads = _local_step(x[0], p[0, 0], loss_target[0], full, (512, 256))

    stacked = [_full_to_stacked(n, grads[n], shard_shapes[n]) for n in big]
    landed = _sibling_exchange(stacked)
    contributions = _chip_exchange([_pair_sum(g, l, core, n) for n, g, l in zip(big, stacked, landed)])

    small = jnp.concatenate([grads[n].reshape(-1) for n, _ in REPLICATED] + [grads[n].reshape(-1) for n in conv]
                            + [loss.reshape(1)])
    (small_all,) = _all_gather([_pad_rows(small, SMALL_ROWS)], "gather_small_grads")

    big_out = {n: _adamw(c, wts[n], mom[n], var[n], f"adamw_{n}") for n, c in zip(big, contributions)}
    n_rep = sum(s for _, s in REPLICATED)
    conv_sizes = [3 * w_ // N_DEV for _, w_ in CONV_SHARDED]

    def small_state(src):
        flat = jnp.concatenate([src[n].reshape(-1) for n, _ in REPLICATED] + [src[n].reshape(-1) for n in conv])
        return _pad_rows(flat, 16)

    rep_all = small_all.reshape(N_DEV, -1)[:, :n_rep]
    conv_parts, off = [], n_rep
    for (n, width), size in zip(CONV_SHARDED, conv_sizes):
        sc = width // N_DEV
        a = small_all.reshape(N_DEV, -1)[:, off:off + 3 * width].reshape(N_DEV, 3, width)
        conv_parts.append(lax.dynamic_slice(a, (0, 0, dev * sc), (N_DEV, 3, sc)).reshape(N_DEV, size))
        off += 3 * width
    loss_total = jnp.sum(small_all.reshape(N_DEV, -1)[:, off])
    small_parts = jnp.concatenate([rep_all] + conv_parts, axis=1)
    small_parts = jnp.pad(small_parts, ((0, 0), (0, 16 * 1024 - small_parts.shape[1]))).reshape(N_DEV, 16, 1024)
    g_sm, d_sm, m_sm, v_sm = _adamw(small_parts, small_state(wts), small_state(mom), small_state(var), "adamw_small")

    def unpack(which, small_flat):
        out = {n: big_out[n][which] for n in big}
        flat, o = small_flat.reshape(-1), 0
        for n, s in list(REPLICATED) + [(n, sz) for (n, _), sz in zip(CONV_SHARDED, conv_sizes)]:
            out[n] = flat[o:o + s]
            o += s
        return [out[n].reshape(args[n].shape) for n in names]

    return (loss_total, dx[None], *unpack(0, g_sm), *unpack(1, d_sm), *unpack(2, m_sm), *unpack(3, v_sm))
```

```python
import functools

import jax
import jax.numpy as jnp
from jax import lax
from jax.experimental import pallas as pl
from jax.experimental.pallas import tpu as pltpu

F32 = jnp.float32
BF16 = jnp.bfloat16

D_MODEL = 1024
CONV_W = 512
ATTN_W = 512
HEAD_DIM = 64
D_FF = 2816
PLE_DIM = 256
IN_COLS = 3 * CONV_W + 3 * ATTN_W
EPS = 1e-6
QK_BLOCK = 128
DILATIONS = (1, 4, 16)
ATTN_SCALE = HEAD_DIM ** -0.5

ADAM_LR = 0.001
ADAM_B1 = 0.9
ADAM_B2 = 0.999
ADAM_EPS = 1e-08
ADAM_WD = 0.01
ADAM_STEP = 10

N_DEV = 8
N_CHIP = 4
V7X_VMEM_LIMIT = 56 * 1024 * 1024
FF_CHUNKS = 2

BIG_ROWS = (("w_in", 384), ("w_out", 128), ("w_gate", 352), ("w_up", 352), ("w_down", 352),
            ("w_ple_gate", 128), ("w_ple_proj", 32))
BIG_TOTAL = sum(r for _, r in BIG_ROWS)
SMALL_ROWS = 24


def _cparams(*sem):
    return pltpu.CompilerParams(dimension_semantics=sem, vmem_limit_bytes=V7X_VMEM_LIMIT)


def _mm(a, b):
    return jnp.dot(a, b, preferred_element_type=F32)


def _mm_nt(a, b):
    return lax.dot_general(a, b, (((1,), (1,)), ((), ())), preferred_element_type=F32)


def _mm_tn(a, b):
    return lax.dot_general(a, b, (((0,), (0,)), ((), ())), preferred_element_type=F32)


def _full(shape):
    nd = len(shape)
    return pl.BlockSpec(shape, lambda *_: (0,) * nd)


def _rms_stats(x):
    r = lax.rsqrt(jnp.mean(x * x, axis=-1, keepdims=True) + EPS)
    return r, x * r


def _rms_bwd(dy, xhat, r, g):
    gd = dy * g
    return r * (gd - xhat * jnp.mean(gd * xhat, axis=-1, keepdims=True))


def _seg_sum64(v, bd_ref):
    outs = []
    for c in range(0, v.shape[1], 256):
        vc = v[:, c:c + 256]
        hi = vc.astype(BF16)
        lo = (vc - hi.astype(F32)).astype(BF16)
        outs.append(_mm(hi, bd_ref[...]) + _mm(lo, bd_ref[...]))
    return outs[0] if len(outs) == 1 else jnp.concatenate(outs, axis=1)


def _shift_rows(u, k, edge_rows):
    row = lax.broadcasted_iota(jnp.int32, u.shape, 0)
    out = pltpu.roll(u, k, 0)
    for j in range(k):
        out = jnp.where(row == j, edge_rows[k - 1 - j], out)
    return out


def _shift_rows_up(u, k, edge_rows):
    n = u.shape[0]
    row = lax.broadcasted_iota(jnp.int32, u.shape, 0)
    out = pltpu.roll(u, n - k, 0)
    for j in range(k):
        out = jnp.where(row == n - k + j, edge_rows[j], out)
    return out


def _conv_fwd(u, c1, c2, w_ref, b_ref):
    u1 = _shift_rows(u, 1, (c1,))
    u2 = _shift_rows(u, 2, (c1, c2))
    y = u2 * w_ref[0:1, :] + u1 * w_ref[1:2, :] + u * w_ref[2:3, :] + b_ref[...]
    return y, u1, u2


def _conv_bwd_input(dy, n1row, n2row, w_ref):
    d1 = _shift_rows_up(dy, 1, (n1row,))
    d2 = _shift_rows_up(dy, 2, (n1row, n2row))
    return dy * w_ref[2:3, :] + d1 * w_ref[1:2, :] + d2 * w_ref[0:1, :]


def _sigmoid(x):
    return 1.0 / (1.0 + jnp.exp(-x))


def _inproj_fwd(x, g_mix, w_in, conv_w, conv_b, qg, kg, bd, tm):
    t = x.shape[0]

    def body(x_ref, g_ref, w_ref, cw_ref, cb_ref, qg_ref, kg_ref, bd_ref,
             zc_ref, zqk_ref, yc_ref, q_ref, k_ref, v_ref, carry_ref):
        @pl.when(pl.program_id(0) == 0)
        def _():
            carry_ref[...] = jnp.zeros_like(carry_ref)

        _, xhat = _rms_stats(x_ref[...])
        h = (xhat * g_ref[...]).astype(BF16)
        zconv = _mm(h, w_ref[:, 0:3 * CONV_W])
        zc_ref[...] = zconv
        u = zconv[:, CONV_W:2 * CONV_W] * zconv[:, 2 * CONV_W:3 * CONV_W]
        cv, _, _ = _conv_fwd(u, carry_ref[7:8, :], carry_ref[6:7, :], cw_ref, cb_ref)
        yc_ref[...] = zconv[:, 0:CONV_W] * cv
        carry_ref[...] = u[tm - 8:tm, :]

        zqk = _mm(h, w_ref[:, 3 * CONV_W:3 * CONV_W + 2 * ATTN_W])
        zqk_ref[...] = zqk
        for j, (gain_ref, out_ref, scale) in enumerate(((qg_ref, q_ref, ATTN_SCALE), (kg_ref, k_ref, 1.0))):
            z = zqk[:, j * ATTN_W:(j + 1) * ATTN_W]
            r = lax.rsqrt(_seg_sum64(z * z, bd_ref) * (1.0 / HEAD_DIM) + EPS)
            out_ref[...] = z * r * gain_ref[...] * scale
        v_ref[...] = _mm(h, w_ref[:, 3 * CONV_W + 2 * ATTN_W:IN_COLS])

    def blk(c):
        return pl.BlockSpec((tm, c), lambda i: (i, 0))

    return pl.pallas_call(
        body, name="inproj_fwd", grid=(t // tm,),
        in_specs=[blk(D_MODEL), _full((1, D_MODEL)), _full((D_MODEL, IN_COLS)), _full((3, CONV_W)),
                  _full((1, CONV_W)), _full((1, ATTN_W)), _full((1, ATTN_W)), _full((256, 256))],
        out_specs=[blk(3 * CONV_W), blk(2 * ATTN_W), blk(CONV_W), blk(ATTN_W), blk(ATTN_W), blk(ATTN_W)],
        out_shape=[jax.ShapeDtypeStruct((t, 3 * CONV_W), F32), jax.ShapeDtypeStruct((t, 2 * ATTN_W), F32),
                   jax.ShapeDtypeStruct((t, CONV_W), F32), jax.ShapeDtypeStruct((t, ATTN_W), F32),
                   jax.ShapeDtypeStruct((t, ATTN_W), F32), jax.ShapeDtypeStruct((t, ATTN_W), F32)],
        scratch_shapes=[pltpu.VMEM((8, CONV_W), F32)],
        compiler_params=_cparams("arbitrary"),
    )(x, g_mix, w_in, conv_w, conv_b, qg, kg, bd)


SUPER = 16 * QK_BLOCK
KEYS = 2 * QK_BLOCK


def _rows(start, size, dil):
    return pl.ds(start, size) if dil == 1 else pl.ds(start, size, stride=dil)


def _attn_bias(sl_ref, dil):
    qi = lax.broadcasted_iota(jnp.int32, (KEYS, KEYS), 0)
    kj = lax.broadcasted_iota(jnp.int32, (KEYS, KEYS), 1)
    step = jnp.bitwise_and(qi, QK_BLOCK - 1) + QK_BLOCK - kj
    slope = jnp.where(qi < QK_BLOCK, sl_ref[0, 0:1, 0:1], sl_ref[0, 1:2, 0:1])
    bias = jnp.where(jnp.logical_and(step >= 0, step <= QK_BLOCK), -slope * (step * dil).astype(F32), -jnp.inf)
    return bias, kj >= QK_BLOCK


def _unit_start(u, dil):
    if dil == 1:
        return pl.multiple_of(u * QK_BLOCK, QK_BLOCK)
    if dil == 4:
        return jnp.bitwise_and(u, 3) + (u // 4) * (4 * QK_BLOCK)
    return u


def _stack_heads(a, head0):
    zero = jnp.zeros_like(a)
    return jnp.concatenate([jnp.where(head0, a, zero), jnp.where(head0, zero, a)], axis=0)


def _attn_fwd(q, k, v, slopes):
    t = q.shape[0]
    nsb = t // SUPER

    def body(q_ref, kc_ref, kp_ref, vc_ref, vp_ref, sl_ref, o_ref, l_ref, kk, vv, ob, lb):
        s = pl.program_id(1)
        kk[0:SUPER, :] = kp_ref[...]
        kk[SUPER:, :] = kc_ref[...]
        vv[0:SUPER, :] = vp_ref[...]
        vv[SUPER:, :] = vc_ref[...]
        head0 = lax.broadcasted_iota(jnp.int32, (QK_BLOCK, QK_BLOCK), 1) < HEAD_DIM

        for b, dil in enumerate(DILATIONS):
            bias, own_half = _attn_bias(sl_ref, dil)

            def unit(u, carry, b=b, dil=dil, bias=bias, own_half=own_half):
                start = _unit_start(u, dil)
                first_key = SUPER + start - QK_BLOCK * dil
                q2 = _stack_heads(q_ref[_rows(start, QK_BLOCK, dil), :].astype(BF16), head0)
                k2 = kk[_rows(first_key, KEYS, dil), :].astype(BF16)
                v2 = vv[_rows(first_key, KEYS, dil), :].astype(BF16)
                has_prev = jnp.logical_or(s > 0, start >= QK_BLOCK * dil)
                sc = jnp.where(jnp.logical_or(own_half, has_prev), _mm_nt(q2, k2) + bias, -jnp.inf)
                m = jnp.max(sc, axis=-1, keepdims=True)
                e = jnp.exp(sc - m)
                den = jnp.sum(e, axis=-1, keepdims=True)
                o2 = _mm(e.astype(BF16), v2) / den
                l2 = m + jnp.log(den)
                ob[b, _rows(start, QK_BLOCK, dil), :] = jnp.where(head0, o2[0:QK_BLOCK], o2[QK_BLOCK:])
                lb[b, _rows(start, QK_BLOCK, dil), :] = jnp.where(head0, l2[0:QK_BLOCK], l2[QK_BLOCK:])
                return carry

            lax.fori_loop(0, SUPER // QK_BLOCK, unit, 0, unroll=4)

        def merge(i, carry):
            rows = pl.ds(pl.multiple_of(i * 256, 256), 256)
            la, lb_, lc = lb[0, rows, :], lb[1, rows, :], lb[2, rows, :]
            mx = jnp.maximum(jnp.maximum(la, lb_), lc)
            wa, wb, wc = jnp.exp(la - mx), jnp.exp(lb_ - mx), jnp.exp(lc - mx)
            sw = wa + wb + wc
            o_ref[rows, :] = (wa * ob[0, rows, :] + wb * ob[1, rows, :] + wc * ob[2, rows, :]) / sw
            l_ref[rows, :] = mx + jnp.log(sw)
            return carry

        lax.fori_loop(0, SUPER // 256, merge, 0)

    cur = pl.BlockSpec((SUPER, QK_BLOCK), lambda p, s: (s, p))
    prev = pl.BlockSpec((SUPER, QK_BLOCK), lambda p, s: (jnp.maximum(s - 1, 0), p))
    return pl.pallas_call(
        body, name="attn_fwd", grid=(4, nsb),
        in_specs=[cur, cur, prev, cur, prev, pl.BlockSpec((1, 2, QK_BLOCK), lambda p, s: (p, 0, 0))],
        out_specs=[cur, cur],
        out_shape=[jax.ShapeDtypeStruct((t, ATTN_W), F32), jax.ShapeDtypeStruct((t, ATTN_W), F32)],
        scratch_shapes=[pltpu.VMEM((2 * SUPER, QK_BLOCK), F32), pltpu.VMEM((2 * SUPER, QK_BLOCK), F32),
                        pltpu.VMEM((3, SUPER, QK_BLOCK), F32), pltpu.VMEM((3, SUPER, QK_BLOCK), F32)],
        compiler_params=_cparams("parallel", "arbitrary"),
    )(q, k, k, v, v, slopes)


def _outproj_fwd(ya, yc, x, goc, goa, w_out, tm):
    t = x.shape[0]

    def body(ya_ref, yc_ref, x_ref, goc_ref, goa_ref, w_ref, x1_ref):
        _, ychat = _rms_stats(yc_ref[...])
        _, yahat = _rms_stats(ya_ref[...])
        acc = _mm((ychat * goc_ref[...]).astype(BF16), w_ref[0:CONV_W, :])
        acc += _mm((yahat * goa_ref[...]).astype(BF16), w_ref[CONV_W:, :])
        x1_ref[...] = x_ref[...] + acc

    def blk(c):
        return pl.BlockSpec((tm, c), lambda i: (i, 0))

    return pl.pallas_call(
        body, name="outproj_fwd", grid=(t // tm,),
        in_specs=[blk(ATTN_W), blk(CONV_W), blk(D_MODEL), _full((1, CONV_W)), _full((1, ATTN_W)),
                  _full((D_MODEL, D_MODEL))],
        out_specs=blk(D_MODEL),
        out_shape=jax.ShapeDtypeStruct((t, D_MODEL), F32),
        compiler_params=_cparams("parallel"),
    )(ya, yc, x, goc, goa, w_out)


def _ffn_fwd(x1, g_ffn, w_gate, w_up, w_down, fcw, fcb, tm):
    t = x1.shape[0]

    def body(x_ref, g_ref, wg_ref, wu_ref, wd_ref, cw_ref, cb_ref, gp_ref, up_ref, x2_ref, carry_ref):
        @pl.when(pl.program_id(0) == 0)
        def _():
            carry_ref[...] = jnp.zeros_like(carry_ref)

        xv = x_ref[...]
        _, xhat = _rms_stats(xv)
        h = (xhat * g_ref[...]).astype(BF16)
        gp = _mm(h, wg_ref[...])
        gp_ref[...] = gp
        gate, _, _ = _conv_fwd(gp, carry_ref[7:8, :], carry_ref[6:7, :], cw_ref, cb_ref)
        carry_ref[...] = gp[tm - 8:tm, :]
        up = _mm(h, wu_ref[...])
        up_ref[...] = up
        a = (gate * _sigmoid(gate) * up).astype(BF16)
        x2_ref[...] = xv + _mm(a, wd_ref[...])

    def blk(c):
        return pl.BlockSpec((tm, c), lambda i: (i, 0))

    return pl.pallas_call(
        body, name="ffn_fwd", grid=(t // tm,),
        in_specs=[blk(D_MODEL), _full((1, D_MODEL)), _full((D_MODEL, D_FF)), _full((D_MODEL, D_FF)),
                  _full((D_FF, D_MODEL)), _full((3, D_FF)), _full((1, D_FF))],
        out_specs=[blk(D_FF), blk(D_FF), blk(D_MODEL)],
        out_shape=[jax.ShapeDtypeStruct((t, D_FF), F32), jax.ShapeDtypeStruct((t, D_FF), F32),
                   jax.ShapeDtypeStruct((t, D_MODEL), F32)],
        scratch_shapes=[pltpu.VMEM((8, D_FF), F32)],
        compiler_params=_cparams("arbitrary"),
    )(x1, g_ffn, w_gate, w_up, w_down, fcw, fcb)


def _ple_fwd_bwd(x2, p, target, g_ple, w_pg, w_pp, tm):
    t = x2.shape[0]

    def body(x_ref, p_ref, t_ref, g_ref, wg_ref, wp_ref, dx_ref, loss_ref, dwg_ref, dwp_ref, dg_ref):
        @pl.when(pl.program_id(0) == 0)
        def _():
            loss_ref[...] = jnp.zeros_like(loss_ref)
            dwg_ref[...] = jnp.zeros_like(dwg_ref)
            dwp_ref[...] = jnp.zeros_like(dwp_ref)
            dg_ref[...] = jnp.zeros_like(dg_ref)

        xv = x_ref[...]
        r, xhat = _rms_stats(xv)
        g = g_ref[...]
        h = (xhat * g).astype(BF16)
        pg = _sigmoid(_mm(h, wg_ref[...]))
        pb = p_ref[...].astype(BF16)
        pp = _mm(pb, wp_ref[...])
        err = xv + pg * pp - t_ref[...]
        loss_ref[...] += 0.5 * jnp.sum(jnp.mean(err * err, axis=-1, keepdims=True))
        dx3 = err * (1.0 / D_MODEL)
        d_pp = (dx3 * pg).astype(BF16)
        d_pre = (dx3 * pp * pg * (1.0 - pg)).astype(BF16)
        dwp_ref[...] += _mm_tn(pb, d_pp)
        dwg_ref[...] += _mm_tn(h, d_pre)
        dh = _mm_nt(d_pre, wg_ref[...])
        dg_ref[...] += jnp.sum(dh * xhat, axis=0, keepdims=True)
        dx_ref[...] = dx3 + _rms_bwd(dh, xhat, r, g)

    def blk(c):
        return pl.BlockSpec((tm, c), lambda i: (i, 0))

    return pl.pallas_call(
        body, name="ple_fwd_bwd", grid=(t // tm,),
        in_specs=[blk(D_MODEL), blk(PLE_DIM), blk(D_MODEL), _full((1, D_MODEL)), _full((D_MODEL, D_MODEL)),
                  _full((PLE_DIM, D_MODEL))],
        out_specs=[blk(D_MODEL), _full((8, 128)), _full((D_MODEL, D_MODEL)), _full((PLE_DIM, D_MODEL)),
                   _full((1, D_MODEL))],
        out_shape=[jax.ShapeDtypeStruct((t, D_MODEL), F32), jax.ShapeDtypeStruct((8, 128), F32),
                   jax.ShapeDtypeStruct((D_MODEL, D_MODEL), F32), jax.ShapeDtypeStruct((PLE_DIM, D_MODEL), F32),
                   jax.ShapeDtypeStruct((1, D_MODEL), F32)],
        compiler_params=_cparams("arbitrary"),
    )(x2, p, target, g_ple, w_pg, w_pp)


def _ffn_bwd(dx2, x1, g_ffn, gp, up, w_gate, w_up, w_down, fcw, fcb, tm):
    t = x1.shape[0]
    nblk = t // tm
    fc = D_FF // FF_CHUNKS

    def body(dx_ref, x_ref, g_ref, gp_ref, gph_ref, up_ref, wg_ref, wu_ref, wd_ref, cw_ref, cb_ref,
             dh_ref, dwd_ref, dwu_ref, dwg_ref, dcw_ref, dcb_ref, carry_ref):
        i = pl.program_id(1)

        @pl.when(i == 0)
        def _():
            carry_ref[...] = jnp.zeros_like(carry_ref)
            dwd_ref[...] = jnp.zeros_like(dwd_ref)
            dwu_ref[...] = jnp.zeros_like(dwu_ref)
            dwg_ref[...] = jnp.zeros_like(dwg_ref)
            dcw_ref[...] = jnp.zeros_like(dcw_ref)
            dcb_ref[...] = jnp.zeros_like(dcb_ref)

        keep = (i < nblk - 1).astype(F32)
        dxb = dx_ref[...].astype(BF16)
        _, xhat = _rms_stats(x_ref[...])
        h = (xhat * g_ref[...]).astype(BF16)
        gp_v = gp_ref[...]
        gate, gp1, gp2 = _conv_fwd(gp_v, gph_ref[7:8, :] * keep, gph_ref[6:7, :] * keep, cw_ref, cb_ref)
        s = _sigmoid(gate)
        silu = gate * s
        up_v = up_ref[...]
        da = _mm_nt(dxb, wd_ref[...])
        dwd_ref[...] += _mm_tn((silu * up_v).astype(BF16), dxb)
        d_up = (da * silu).astype(BF16)
        d_gate = da * up_v * (s * (1.0 + gate * (1.0 - s)))
        dwu_ref[...] += _mm_tn(h, d_up)
        d_gp = _conv_bwd_input(d_gate, carry_ref[0:1, :], carry_ref[1:2, :], cw_ref).astype(BF16)
        carry_ref[...] = d_gate[0:8, :]
        dcw_ref[0:1, :] += jnp.sum(d_gate * gp2, axis=0, keepdims=True)
        dcw_ref[1:2, :] += jnp.sum(d_gate * gp1, axis=0, keepdims=True)
        dcw_ref[2:3, :] += jnp.sum(d_gate * gp_v, axis=0, keepdims=True)
        dcb_ref[...] += jnp.sum(d_gate, axis=0, keepdims=True)
        dwg_ref[...] += _mm_tn(h, d_gp)
        dh_ref[...] = _mm_nt(d_gp, wg_ref[...]) + _mm_nt(d_up, wu_ref[...])

    def rev(i):
        return nblk - 1 - i

    one = pl.Buffered(1)
    in_specs = [
        pl.BlockSpec((tm, D_MODEL), lambda j, i: (rev(i), 0)),
        pl.BlockSpec((tm, D_MODEL), lambda j, i: (rev(i), 0)),
        _full((1, D_MODEL)),
        pl.BlockSpec((tm, fc), lambda j, i: (rev(i), j)),
        pl.BlockSpec((8, fc), lambda j, i: (jnp.maximum(rev(i) * (tm // 8) - 1, 0), j)),
        pl.BlockSpec((tm, fc), lambda j, i: (rev(i), j)),
        pl.BlockSpec((D_MODEL, fc), lambda j, i: (0, j), pipeline_mode=one),
        pl.BlockSpec((D_MODEL, fc), lambda j, i: (0, j), pipeline_mode=one),
        pl.BlockSpec((fc, D_MODEL), lambda j, i: (j, 0), pipeline_mode=one),
        pl.BlockSpec((3, fc), lambda j, i: (0, j)),
        pl.BlockSpec((1, fc), lambda j, i: (0, j)),
    ]
    out_specs = [
        pl.BlockSpec((None, tm, D_MODEL), lambda j, i: (j, rev(i), 0)),
        pl.BlockSpec((fc, D_MODEL), lambda j, i: (j, 0), pipeline_mode=one),
        pl.BlockSpec((D_MODEL, fc), lambda j, i: (0, j), pipeline_mode=one),
        pl.BlockSpec((D_MODEL, fc), lambda j, i: (0, j), pipeline_mode=one),
        pl.BlockSpec((3, fc), lambda j, i: (0, j)),
        pl.BlockSpec((1, fc), lambda j, i: (0, j)),
    ]
    return pl.pallas_call(
        body, name="ffn_bwd", grid=(FF_CHUNKS, nblk), in_specs=in_specs, out_specs=out_specs,
        out_shape=[jax.ShapeDtypeStruct((FF_CHUNKS, t, D_MODEL), F32), jax.ShapeDtypeStruct((D_FF, D_MODEL), F32),
                   jax.ShapeDtypeStruct((D_MODEL, D_FF), F32), jax.ShapeDtypeStruct((D_MODEL, D_FF), F32),
                   jax.ShapeDtypeStruct((3, D_FF), F32), jax.ShapeDtypeStruct((1, D_FF), F32)],
        scratch_shapes=[pltpu.VMEM((8, fc), F32)],
        compiler_params=_cparams("arbitrary", "arbitrary"),
    )(dx2, x1, g_ffn, gp, gp, up, w_gate, w_up, w_down, fcw, fcb)


def _outproj_bwd(dh2, dx2, x1, g_ffn, w_out, yc, ya, goc, goa, zconv, conv_w, conv_b, bd, tm):
    t = x1.shape[0]
    nblk = t // tm

    def body(dh_ref, dx2_ref, x1_ref, g_ref, w_ref, yc_ref, ya_ref, goc_ref, goa_ref, zc_ref, zch_ref, cw_ref, cb_ref,
             bd_ref, dx1_ref, dya_ref, dd_ref, dzc_ref, dw_ref, dg_ref, dgoc_ref, dgoa_ref, dcw_ref, dcb_ref,
             carry_ref):
        i = pl.program_id(0)

        @pl.when(i == 0)
        def _():
            carry_ref[...] = jnp.zeros_like(carry_ref)
            for ref in (dw_ref, dg_ref, dgoc_ref, dgoa_ref, dcw_ref, dcb_ref):
                ref[...] = jnp.zeros_like(ref)

        keep = (i < nblk - 1).astype(F32)
        dh2_v = dh_ref[0]
        for j in range(1, FF_CHUNKS):
            dh2_v = dh2_v + dh_ref[j]
        r, xhat = _rms_stats(x1_ref[...])
        dg_ref[...] += jnp.sum(dh2_v * xhat, axis=0, keepdims=True)
        dx1 = dx2_ref[...] + _rms_bwd(dh2_v, xhat, r, g_ref[...])
        dx1_ref[...] = dx1
        dx1b = dx1.astype(BF16)
        dy = _mm_nt(dx1b, w_ref[...])

        yc_v = yc_ref[...]
        rc, ychat = _rms_stats(yc_v)
        dw_ref[0:CONV_W, :] += _mm_tn((ychat * goc_ref[...]).astype(BF16), dx1b)
        dyc = dy[:, 0:CONV_W]
        dgoc_ref[...] += jnp.sum(dyc * ychat, axis=0, keepdims=True)
        d_yc = _rms_bwd(dyc, ychat, rc, goc_ref[...])

        ya_v = ya_ref[...]
        ra, yahat = _rms_stats(ya_v)
        dw_ref[CONV_W:, :] += _mm_tn((yahat * goa_ref[...]).astype(BF16), dx1b)
        dya = dy[:, CONV_W:]
        dgoa_ref[...] += jnp.sum(dya * yahat, axis=0, keepdims=True)
        d_ya = _rms_bwd(dya, yahat, ra, goa_ref[...])
        dya_ref[...] = d_ya
        dd_ref[...] = _seg_sum64(d_ya * ya_v, bd_ref)

        zb = zc_ref[:, 0:CONV_W]
        zc = zc_ref[:, CONV_W:2 * CONV_W]
        zx = zc_ref[:, 2 * CONV_W:3 * CONV_W]
        u = zc * zx
        uh = zch_ref[:, CONV_W:2 * CONV_W] * zch_ref[:, 2 * CONV_W:3 * CONV_W] * keep
        cv, u1, u2 = _conv_fwd(u, uh[7:8, :], uh[6:7, :], cw_ref, cb_ref)
        d_cv = d_yc * zb
        d_u = _conv_bwd_input(d_cv, carry_ref[0:1, :], carry_ref[1:2, :], cw_ref)
        carry_ref[...] = d_cv[0:8, :]
        dcw_ref[0:1, :] += jnp.sum(d_cv * u2, axis=0, keepdims=True)
        dcw_ref[1:2, :] += jnp.sum(d_cv * u1, axis=0, keepdims=True)
        dcw_ref[2:3, :] += jnp.sum(d_cv * u, axis=0, keepdims=True)
        dcb_ref[...] += jnp.sum(d_cv, axis=0, keepdims=True)
        dzc_ref[:, 0:CONV_W] = d_yc * cv
        dzc_ref[:, CONV_W:2 * CONV_W] = d_u * zx
        dzc_ref[:, 2 * CONV_W:3 * CONV_W] = d_u * zc

    def rev(i):
        return nblk - 1 - i

    def blk(c):
        return pl.BlockSpec((tm, c), lambda i: (rev(i), 0))

    in_specs = [
        pl.BlockSpec((FF_CHUNKS, tm, D_MODEL), lambda i: (0, rev(i), 0)),
        blk(D_MODEL), blk(D_MODEL), _full((1, D_MODEL)), _full((D_MODEL, D_MODEL)),
        blk(CONV_W), blk(ATTN_W), _full((1, CONV_W)), _full((1, ATTN_W)),
        blk(3 * CONV_W),
        pl.BlockSpec((8, 3 * CONV_W), lambda i: (jnp.maximum(rev(i) * (tm // 8) - 1, 0), 0)),
        _full((3, CONV_W)), _full((1, CONV_W)), _full((256, 256)),
    ]
    out_specs = [blk(D_MODEL), blk(ATTN_W), blk(ATTN_W), blk(3 * CONV_W), _full((D_MODEL, D_MODEL)),
                 _full((1, D_MODEL)), _full((1, CONV_W)), _full((1, ATTN_W)), _full((3, CONV_W)), _full((1, CONV_W))]
    return pl.pallas_call(
        body, name="outproj_bwd", grid=(nblk,), in_specs=in_specs, out_specs=out_specs,
        out_shape=[jax.ShapeDtypeStruct((t, D_MODEL), F32), jax.ShapeDtypeStruct((t, ATTN_W), F32),
                   jax.ShapeDtypeStruct((t, ATTN_W), F32), jax.ShapeDtypeStruct((t, 3 * CONV_W), F32),
                   jax.ShapeDtypeStruct((D_MODEL, D_MODEL), F32), jax.ShapeDtypeStruct((1, D_MODEL), F32),
                   jax.ShapeDtypeStruct((1, CONV_W), F32), jax.ShapeDtypeStruct((1, ATTN_W), F32),
                   jax.ShapeDtypeStruct((3, CONV_W), F32), jax.ShapeDtypeStruct((1, CONV_W), F32)],
        scratch_shapes=[pltpu.VMEM((8, CONV_W), F32)],
        compiler_params=_cparams("arbitrary"),
    )(dh2, dx2, x1, g_ffn, w_out, yc, ya, goc, goa, zconv, zconv, conv_w, conv_b, bd)


def _attn_bwd(q, k, v, dya, lse, dd, slopes):
    t = q.shape[0]
    nsb = t // SUPER

    def body(q_ref, kc_ref, kp_ref, vc_ref, vp_ref, dy_ref, l_ref, d_ref, sl_ref, dq_ref, dk_ref, dv_ref,
             kk, vv, dkacc, dvacc):
        s = pl.program_id(1)

        @pl.when(s == 0)
        def _():
            dkacc[...] = jnp.zeros_like(dkacc)
            dvacc[...] = jnp.zeros_like(dvacc)

        dkacc[0:SUPER, :] = dkacc[SUPER:, :]
        dvacc[0:SUPER, :] = dvacc[SUPER:, :]
        dkacc[SUPER:, :] = jnp.zeros((SUPER, QK_BLOCK), F32)
        dvacc[SUPER:, :] = jnp.zeros((SUPER, QK_BLOCK), F32)

        @pl.when(s < nsb)
        def _():
            kk[0:SUPER, :] = kp_ref[...]
            kk[SUPER:, :] = kc_ref[...]
            vv[0:SUPER, :] = vp_ref[...]
            vv[SUPER:, :] = vc_ref[...]
            head0 = lax.broadcasted_iota(jnp.int32, (QK_BLOCK, QK_BLOCK), 1) < HEAD_DIM

            for b, dil in enumerate(DILATIONS):
                bias, own_half = _attn_bias(sl_ref, dil)

                def unit(u, carry, b=b, dil=dil, bias=bias, own_half=own_half):
                    start = _unit_start(u, dil)
                    first_key = SUPER + start - QK_BLOCK * dil
                    qrows = _rows(start, QK_BLOCK, dil)
                    krows = _rows(first_key, KEYS, dil)
                    q2 = _stack_heads(q_ref[qrows, :].astype(BF16), head0)
                    dy2 = _stack_heads(dy_ref[qrows, :].astype(BF16), head0)
                    lv, dv_ = l_ref[qrows, :], d_ref[qrows, :]
                    l2 = jnp.concatenate([lv[:, 0:1], lv[:, HEAD_DIM:HEAD_DIM + 1]], axis=0)
                    d2 = jnp.concatenate([dv_[:, 0:1], dv_[:, HEAD_DIM:HEAD_DIM + 1]], axis=0)
                    k2 = kk[krows, :].astype(BF16)
                    v2 = vv[krows, :].astype(BF16)
                    has_prev = jnp.logical_or(s > 0, start >= QK_BLOCK * dil)
                    sc = jnp.where(jnp.logical_or(own_half, has_prev), _mm_nt(q2, k2) + bias, -jnp.inf)
                    prob = jnp.exp(sc - l2)
                    ds = (prob * (_mm_nt(dy2, v2) - d2)).astype(BF16)
                    dvacc[krows, :] += _mm_tn(prob.astype(BF16), dy2)
                    dkacc[krows, :] += _mm_tn(ds, q2)
                    dq2 = _mm(ds, k2)
                    dq = jnp.where(head0, dq2[0:QK_BLOCK], dq2[QK_BLOCK:]) * ATTN_SCALE
                    if b == 0:
                        dq_ref[qrows, :] = dq
                    else:
                        dq_ref[qrows, :] += dq
                    return carry

                lax.fori_loop(0, SUPER // QK_BLOCK, unit, 0, unroll=4)

        dk_ref[...] = dkacc[0:SUPER, :]
        dv_ref[...] = dvacc[0:SUPER, :]

    def cur_map(p, s):
        return (jnp.minimum(s, nsb - 1), p)

    def prev_map(p, s):
        return (jnp.clip(s - 1, 0, nsb - 1), p)

    cur = pl.BlockSpec((SUPER, QK_BLOCK), cur_map)
    prev = pl.BlockSpec((SUPER, QK_BLOCK), prev_map)
    return pl.pallas_call(
        body, name="attn_bwd", grid=(4, nsb + 1),
        in_specs=[cur, cur, prev, cur, prev, cur, cur, cur, pl.BlockSpec((1, 2, QK_BLOCK), lambda p, s: (p, 0, 0))],
        out_specs=[cur, prev, prev],
        out_shape=[jax.ShapeDtypeStruct((t, ATTN_W), F32)] * 3,
        scratch_shapes=[pltpu.VMEM((2 * SUPER, QK_BLOCK), F32)] * 4,
        compiler_params=_cparams("parallel", "arbitrary"),
    )(q, k, k, v, v, dya, lse, dd, slopes)


def _attn_bwd_per_branch_unused(q, k, v, dya, lse, dd, slopes, dil):
    t = q.shape[0]
    length = t // dil
    chunk = _attn_chunk(t, dil)
    nch = length // chunk
    nb = chunk // QK_BLOCK
    nblocks = length // QK_BLOCK
    view = (length, dil * ATTN_W)
    ext = chunk + QK_BLOCK

    def body(q_ref, dy_ref, l_ref, d_ref, k_ref, v_ref, qn_ref, dyn_ref, ln_ref, dn_ref, kh_ref, vh_ref, sl_ref,
             dq_ref, dk_ref, dv_ref, qbuf, dybuf, lbuf, dbuf, kbuf, vbuf, dkacc, dvacc):
        c = pl.program_id(2)
        qbuf[0:chunk, :] = q_ref[...]
        qbuf[chunk:, :] = qn_ref[...]
        dybuf[0:chunk, :] = dy_ref[...].astype(BF16)
        dybuf[chunk:, :] = dyn_ref[...].astype(BF16)
        lbuf[0:chunk, :] = l_ref[...]
        lbuf[chunk:, :] = ln_ref[...]
        dbuf[0:chunk, :] = d_ref[...]
        dbuf[chunk:, :] = dn_ref[...]
        kbuf[0:QK_BLOCK, :] = kh_ref[...]
        kbuf[QK_BLOCK:, :] = k_ref[...]
        vbuf[0:QK_BLOCK, :] = vh_ref[...]
        vbuf[QK_BLOCK:, :] = v_ref[...]
        valid_cur, valid_prev, dist_cur, dist_prev, head0 = _attn_masks(dil)

        def pair(qb, dyb, lv, dv_, kb, vb, valid, dist):
            dq = jnp.zeros((QK_BLOCK, QK_BLOCK), F32)
            dk = jnp.zeros((QK_BLOCK, QK_BLOCK), F32)
            dvv = jnp.zeros((QK_BLOCK, QK_BLOCK), F32)
            for hh in range(2):
                sl = sl_ref[0, hh:hh + 1, :]
                hm = head0 if hh == 0 else jnp.logical_not(head0)
                col = hh * HEAD_DIM
                qm = jnp.where(hm, qb, jnp.zeros_like(qb))
                dym = jnp.where(hm, dyb, jnp.zeros_like(dyb))
                s = jnp.where(valid, _mm_nt(qm, kb) - sl * dist, -jnp.inf)
                prob = jnp.exp(s - lv[:, col:col + 1])
                ds = (prob * (_mm_nt(dym, vb) - dv_[:, col:col + 1])).astype(BF16)
                dvv += _mm_tn(prob.astype(BF16), dym)
                dk += _mm_tn(ds, qm)
                dq += jnp.where(hm, _mm(ds, kb), 0.0)
            return dq, dk, dvv

        def blk(j, carry):
            off = pl.multiple_of(j * QK_BLOCK, QK_BLOCK)
            nxt = pl.multiple_of(off + QK_BLOCK, QK_BLOCK)
            qb = qbuf[pl.ds(off, QK_BLOCK), :]
            dyb = dybuf[pl.ds(off, QK_BLOCK), :]
            lv = lbuf[pl.ds(off, QK_BLOCK), :]
            dv_ = dbuf[pl.ds(off, QK_BLOCK), :]
            dq_c, dk_c, dv_c = pair(qb, dyb, lv, dv_, kbuf[pl.ds(nxt, QK_BLOCK), :], vbuf[pl.ds(nxt, QK_BLOCK), :],
                                    valid_cur, dist_cur)
            dkacc[pl.ds(nxt, QK_BLOCK), :] = dk_c
            dvacc[pl.ds(nxt, QK_BLOCK), :] = dv_c
            has_prev = jnp.logical_or(c > 0, j > 0)
            dq_p, dk_p, dv_p = pair(qb, dyb, lv, dv_, kbuf[pl.ds(off, QK_BLOCK), :], vbuf[pl.ds(off, QK_BLOCK), :],
                                    jnp.logical_and(valid_prev, has_prev), dist_prev)

            @pl.when(j > 0)
            def _():
                dkacc[pl.ds(off, QK_BLOCK), :] += dk_p
                dvacc[pl.ds(off, QK_BLOCK), :] += dv_p

            dq_ref[pl.ds(off, QK_BLOCK), :] = (dq_c + dq_p) * ATTN_SCALE
            return carry

        lax.fori_loop(0, nb, blk, 0)

        @pl.when(c < nch - 1)
        def _():
            _, dk_p, dv_p = pair(qbuf[chunk:, :], dybuf[chunk:, :], lbuf[chunk:, :], dbuf[chunk:, :],
                                 kbuf[chunk:, :], vbuf[chunk:, :], valid_prev, dist_prev)
            dkacc[chunk:, :] += dk_p
            dvacc[chunk:, :] += dv_p

        dk_ref[...] = dkacc[QK_BLOCK:, :]
        dv_ref[...] = dvacc[QK_BLOCK:, :]

    def cmap(p, r, c):
        return (c, r * 4 + p)

    def before(p, r, c):
        return (jnp.maximum(c * nb - 1, 0), r * 4 + p)

    def after(p, r, c):
        return (jnp.minimum((c + 1) * nb, nblocks - 1), r * 4 + p)

    main = pl.BlockSpec((chunk, QK_BLOCK), cmap)
    hb = pl.BlockSpec((QK_BLOCK, QK_BLOCK), before)
    ha = pl.BlockSpec((QK_BLOCK, QK_BLOCK), after)
    qv, kv, vv = q.reshape(view), k.reshape(view), v.reshape(view)
    dyv, lv, ddv = dya.reshape(view), lse.reshape(view), dd.reshape(view)
    outs = pl.pallas_call(
        body, name=f"attn_bwd_d{dil}", grid=(4, dil, nch),
        in_specs=[main] * 6 + [ha] * 4 + [hb] * 2 + [pl.BlockSpec((1, 2, QK_BLOCK), lambda p, r, c: (p, 0, 0))],
        out_specs=[main] * 3,
        out_shape=[jax.ShapeDtypeStruct(view, F32)] * 3,
        scratch_shapes=[pltpu.VMEM((ext, QK_BLOCK), BF16), pltpu.VMEM((ext, QK_BLOCK), BF16),
                        pltpu.VMEM((ext, QK_BLOCK), F32), pltpu.VMEM((ext, QK_BLOCK), F32),
                        pltpu.VMEM((ext, QK_BLOCK), BF16), pltpu.VMEM((ext, QK_BLOCK), BF16),
                        pltpu.VMEM((ext, QK_BLOCK), F32), pltpu.VMEM((ext, QK_BLOCK), F32)],
        compiler_params=_cparams("arbitrary", "arbitrary", "arbitrary"),
    )(qv, dyv, lv, ddv, kv, vv, qv, dyv, lv, ddv, kv, vv, slopes)
    return [o.reshape(t, ATTN_W) for o in outs]


def _inproj_bwd(dq, dk, dv, dzconv, zqk, x, dx1, g_mix, w_in, qg, kg, bd, tm):
    t = x.shape[0]

    def body(dq_ref, dk_ref, dv_ref, dzc_ref, zqk_ref, x_ref, dx1_ref, g_ref, w_ref, qg_ref,
             kg_ref, bd_ref, dx_ref, dw_ref, dg_ref, dqg_ref, dkg_ref):
        @pl.when(pl.program_id(0) == 0)
        def _():
            for ref in (dw_ref, dg_ref, dqg_ref, dkg_ref):
                ref[...] = jnp.zeros_like(ref)

        parts = [dzc_ref[...].astype(BF16)]
        for j, (dn_ref, gain_ref, dgain_ref) in enumerate(((dq_ref, qg_ref, dqg_ref), (dk_ref, kg_ref, dkg_ref))):
            dn = dn_ref[...]
            z = zqk_ref[:, j * ATTN_W:(j + 1) * ATTN_W]
            r = lax.rsqrt(_seg_sum64(z * z, bd_ref) * (1.0 / HEAD_DIM) + EPS)
            zhat = z * r
            dgain_ref[...] += jnp.sum(dn * zhat, axis=0, keepdims=True)
            gd = dn * gain_ref[...]
            parts.append((r * (gd - zhat * (_seg_sum64(gd * zhat, bd_ref) * (1.0 / HEAD_DIM)))).astype(BF16))
        parts.append(dv_ref[...].astype(BF16))
        dz = jnp.concatenate(parts, axis=1)

        r, xhat = _rms_stats(x_ref[...])
        g = g_ref[...]
        dw_ref[...] += _mm_tn((xhat * g).astype(BF16), dz)
        dh = _mm_nt(dz, w_ref[...])
        dg_ref[...] += jnp.sum(dh * xhat, axis=0, keepdims=True)
        dx_ref[...] = dx1_ref[...] + _rms_bwd(dh, xhat, r, g)

    def blk(c):
        return pl.BlockSpec((tm, c), lambda i: (i, 0))

    return pl.pallas_call(
        body, name="inproj_bwd", grid=(t // tm,),
        in_specs=[blk(ATTN_W)] * 3 + [blk(3 * CONV_W), blk(2 * ATTN_W), blk(D_MODEL), blk(D_MODEL), _full((1, D_MODEL)),
                                      _full((D_MODEL, IN_COLS)), _full((1, ATTN_W)), _full((1, ATTN_W)),
                                      _full((256, 256))],
        out_specs=[blk(D_MODEL), _full((D_MODEL, IN_COLS)), _full((1, D_MODEL)), _full((1, ATTN_W)),
                   _full((1, ATTN_W))],
        out_shape=[jax.ShapeDtypeStruct((t, D_MODEL), F32), jax.ShapeDtypeStruct((D_MODEL, IN_COLS), F32),
                   jax.ShapeDtypeStruct((1, D_MODEL), F32), jax.ShapeDtypeStruct((1, ATTN_W), F32),
                   jax.ShapeDtypeStruct((1, ATTN_W), F32)],
        compiler_params=_cparams("arbitrary"),
    )(dq, dk, dv, dzconv, zqk, x, dx1, g_mix, w_in, qg, kg, bd)


def _local_step(x, p, target, w, tms, late_weights=None):
    bd = jnp.kron(jnp.eye(4, dtype=F32), jnp.ones((HEAD_DIM, HEAD_DIM), F32)).astype(BF16)
    qg = jnp.tile(w["q_norm_g"], (1, 8))
    kg = jnp.tile(w["k_norm_g"], (1, 8))
    slopes = jnp.exp2(-jnp.arange(1, 9, dtype=F32))
    slopes = jnp.broadcast_to(slopes.reshape(4, 2, 1), (4, 2, QK_BLOCK))

    zconv, zqk, yc, q, k, v = _inproj_fwd(x, w["g_mix"], w["w_in"], w["conv_w"], w["conv_b"], qg, kg, bd, tms[0])
    ya, lse = _attn_fwd(q, k, v, slopes)
    if late_weights is not None:
        w = {**w, **late_weights(lse)}
    x1 = _outproj_fwd(ya, yc, x, w["g_out_conv"], w["g_out_attn"], w["w_out"], tms[0])
    gp, up, x2 = _ffn_fwd(x1, w["g_ffn"], w["w_gate"], w["w_up"], w["w_down"], w["ffn_conv_w"], w["ffn_conv_b"], tms[1])
    dx2, loss, dw_pg, dw_pp, dg_ple = _ple_fwd_bwd(x2, p, target, w["g_ple"], w["w_ple_gate"], w["w_ple_proj"], tms[0])
    dh2, dw_down, dw_up, dw_gate, dfcw, dfcb = _ffn_bwd(dx2, x1, w["g_ffn"], gp, up, w["w_gate"], w["w_up"],
                                                        w["w_down"], w["ffn_conv_w"], w["ffn_conv_b"], tms[1])
    dx1, dya, dd, dzconv, dw_out, dg_ffn, dgoc, dgoa, dcw, dcb = _outproj_bwd(
        dh2, dx2, x1, w["g_ffn"], w["w_out"], yc, ya, w["g_out_conv"], w["g_out_attn"], zconv, w["conv_w"],
        w["conv_b"], bd, tms[1])
    dq, dk, dv = _attn_bwd(q, k, v, dya, lse, dd, slopes)
    dx, dw_in, dg_mix, dqg, dkg = _inproj_bwd(dq, dk, dv, dzconv, zqk, x, dx1, w["g_mix"], w["w_in"], qg, kg, bd,
                                              tms[1])
    grads = {
        "g_mix": dg_mix, "w_in": dw_in, "conv_w": dcw, "conv_b": dcb,
        "q_norm_g": dqg.reshape(8, HEAD_DIM).sum(0, keepdims=True),
        "k_norm_g": dkg.reshape(8, HEAD_DIM).sum(0, keepdims=True),
        "g_out_conv": dgoc, "g_out_attn": dgoa, "w_out": dw_out, "g_ffn": dg_ffn, "w_gate": dw_gate, "w_up": dw_up,
        "ffn_conv_w": dfcw, "ffn_conv_b": dfcb, "w_down": dw_down, "g_ple": dg_ple, "w_ple_gate": dw_pg,
        "w_ple_proj": dw_pp,
    }
    return loss[0, 0], dx, grads


ANY = pl.BlockSpec(memory_space=pl.ANY)
MESH = pl.DeviceIdType.MESH


def _all_gather(shards, name):
    n = len(shards)

    def body(*refs):
        ins, outs = refs[:n], refs[n:2 * n]
        send_sems, recv_sems, local_sems = refs[2 * n:]
        x, y, c = lax.axis_index("x"), lax.axis_index("y"), lax.axis_index("c")
        me, sibling = (x, y, c), (x, y, 1 - c)
        chips = [(1 - x, y), (x, 1 - y), (1 - x, 1 - y)]

        def slot(dev):
            return 4 * dev[0] + 2 * dev[1] + dev[2]

        def copy(b, k, block, to, src=None):
            dst = outs[b].at[slot(block)]
            return pltpu.make_async_remote_copy(
                src_ref=dst if src is None else src, dst_ref=dst, send_sem=send_sems.at[b, k],
                recv_sem=recv_sems.at[b, k], device_id=to, device_id_type=MESH)

        mine = [pltpu.make_async_copy(ins[b], outs[b].at[slot(me)], local_sems.at[b]) for b in range(n)]
        first, passed = [], []
        for b in range(n):
            mine[b].start()
            first.append(copy(b, 0, me, sibling, src=ins[b]))
            first += [copy(b, 1 + j, me, (*chip, c), src=ins[b]) for j, chip in enumerate(chips)]
        for cp in first:
            cp.start()
        for j, chip in enumerate(chips):
            for b in range(n):
                copy(b, 1 + j, (*chip, c), me).wait_recv()
                fwd = copy(b, 4 + j, (*chip, c), sibling)
                fwd.start()
                passed.append(fwd)
        for b in range(n):
            copy(b, 0, sibling, me).wait_recv()
            for j, chip in enumerate(chips):
                copy(b, 4 + j, (*chip, 1 - c), me).wait_recv()
        for cp in first + passed:
            cp.wait_send()
        for cp in mine:
            cp.wait()

    return pl.pallas_call(
        body, name=name,
        in_specs=[ANY] * n, out_specs=[ANY] * n,
        out_shape=[jax.ShapeDtypeStruct((N_DEV,) + s.shape, s.dtype) for s in shards],
        scratch_shapes=[pltpu.SemaphoreType.DMA((n, 7)), pltpu.SemaphoreType.DMA((n, 7)),
                        pltpu.SemaphoreType.DMA((n,))],
    )(*shards)


HBM = pl.BlockSpec(memory_space=pltpu.HBM)
SEM = pl.BlockSpec(memory_space=pltpu.SEMAPHORE)
EFFECT = pltpu.SideEffectType.DATAFLOW_SIDE_EFFECTING
FLIPS = ((0, 0, 1), (0, 1, 0), (0, 1, 1), (1, 0, 0), (1, 0, 1), (1, 1, 0), (1, 1, 1))


def _flip_peers():
    pos = (lax.axis_index("x"), lax.axis_index("y"), lax.axis_index("c"))
    return [tuple(1 - a if f else a for a, f in zip(pos, flip)) for flip in FLIPS]


def _hbm(a):
    return pltpu.with_memory_space_constraint(a, pltpu.HBM)


def _gather_start(shards, after):
    n = len(shards)

    def body(*refs):
        ins, lands = refs[:n], refs[n:2 * n]
        send_sems, recv_sems = refs[2 * n + 1], refs[2 * n + 2]
        token = refs[-1]
        slot = 4 * lax.axis_index("x") + 2 * lax.axis_index("y") + lax.axis_index("c")
        for b in range(n):
            for k, peer in enumerate(_flip_peers()):
                pltpu.make_async_remote_copy(
                    src_ref=ins[b], dst_ref=lands[b].at[slot], send_sem=send_sems.at[7 * b + k],
                    recv_sem=recv_sems.at[7 * b + k], device_id=peer, device_id_type=MESH).start()
        token[...] = jnp.zeros_like(token)

    lands = [lax.empty((N_DEV,) + s.shape, s.dtype) for s in shards]
    outs = pl.pallas_call(
        body, name="gather_late_weights_start",
        in_specs=[HBM] * (2 * n) + [ANY],
        out_specs=[SEM, SEM] + [HBM] * (2 * n) + [pl.BlockSpec(memory_space=pltpu.VMEM)],
        out_shape=[pltpu.SemaphoreType.DMA((7 * n,)), pltpu.SemaphoreType.DMA((7 * n,))]
        + [pltpu.HBM(s.shape, s.dtype) for s in shards] + [pltpu.HBM(l.shape, l.dtype) for l in lands]
        + [jax.ShapeDtypeStruct((8, 128), F32)],
        input_output_aliases={i: 2 + i for i in range(2 * n)},
        compiler_params=pltpu.CompilerParams(has_side_effects=EFFECT),
    )(*[_hbm(s) for s in shards], *[_hbm(l) for l in lands], after)
    return outs[0], outs[1], outs[2:2 + n], outs[2 + n:2 + 2 * n], outs[-1]


def _gather_wait(send_sems, recv_sems, shards, lands, after):
    n = len(shards)

    def body(*refs):
        ins, zones = refs[:n], refs[n:2 * n]
        send_ref, recv_ref = refs[2 * n], refs[2 * n + 1]
        slot = 4 * lax.axis_index("x") + 2 * lax.axis_index("y") + lax.axis_index("c")
        for b in range(n):
            for k, peer in enumerate(_flip_peers()):
                copy = pltpu.make_async_remote_copy(
                    src_ref=ins[b], dst_ref=zones[b].at[slot], send_sem=send_ref.at[7 * b + k],
                    recv_sem=recv_ref.at[7 * b + k], device_id=peer, device_id_type=MESH)
                copy.wait_send()
                copy.wait_recv()

    outs = pl.pallas_call(
        body, name="gather_late_weights_wait",
        in_specs=[HBM] * (2 * n) + [SEM, SEM, ANY],
        out_specs=[HBM] * (2 * n),
        out_shape=[pltpu.HBM(s.shape, s.dtype) for s in shards] + [pltpu.HBM(l.shape, l.dtype) for l in lands],
        input_output_aliases={i: i for i in range(2 * n)},
        compiler_params=pltpu.CompilerParams(has_side_effects=EFFECT),
    )(*shards, *lands, send_sems, recv_sems, after)
    return outs[n:]


def _row_tile(rows):
    for tr in range(min(rows, 512), 15, -16):
        if rows % tr == 0:
            return tr
    return rows


def _sibling_exchange(gs):
    n = len(gs)

    def body(*refs):
        g_refs, land_refs = refs[:n], refs[n:2 * n]
        send_sems, recv_sems = refs[2 * n:]
        x, y, c = lax.axis_index("x"), lax.axis_index("y"), lax.axis_index("c")
        copies = [pltpu.make_async_remote_copy(
            src_ref=g_refs[b].at[k, 1 - c], dst_ref=land_refs[b].at[k], send_sem=send_sems.at[b, k],
            recv_sem=recv_sems.at[b, k], device_id=(x, y, 1 - c), device_id_type=MESH)
            for b in range(n) for k in range(N_CHIP)]
        for cp in copies:
            cp.start()
        for cp in copies:
            cp.wait()

    return pl.pallas_call(
        body, name="rs_sibling_exchange", in_specs=[ANY] * n, out_specs=[ANY] * n,
        out_shape=[jax.ShapeDtypeStruct((N_CHIP,) + g.shape[2:], g.dtype) for g in gs],
        scratch_shapes=[pltpu.SemaphoreType.DMA((n, N_CHIP)), pltpu.SemaphoreType.DMA((n, N_CHIP))],
    )(*gs)


def _pair_sum(g, land, core, name):
    rows, cols = land.shape[1:]
    tr = _row_tile(rows)

    def body(c_ref, g_ref, l_ref, o_ref):
        o_ref[...] = (g_ref[...].astype(F32) + l_ref[...].astype(F32)).astype(o_ref.dtype)

    return pl.pallas_call(
        body, name=f"rs_pair_sum_{name}",
        grid_spec=pltpu.PrefetchScalarGridSpec(
            num_scalar_prefetch=1, grid=(N_CHIP, rows // tr),
            in_specs=[pl.BlockSpec((None, None, tr, cols), lambda k, i, c_ref: (k, c_ref[0], i, 0)),
                      pl.BlockSpec((None, tr, cols), lambda k, i, c_ref: (k, i, 0))],
            out_specs=pl.BlockSpec((None, tr, cols), lambda k, i, c_ref: (k, i, 0))),
        out_shape=jax.ShapeDtypeStruct(land.shape, land.dtype),
        compiler_params=_cparams("parallel", "parallel"),
    )(core, g, land)


def _chip_exchange(parts):
    n = len(parts)

    def body(*refs):
        p_refs, land_refs = refs[:n], refs[n:2 * n]
        send_sems, recv_sems, local_sems = refs[2 * n:]
        x, y, c = lax.axis_index("x"), lax.axis_index("y"), lax.axis_index("c")
        mine = 2 * x + y
        chips = [(1 - x, y), (x, 1 - y), (1 - x, 1 - y)]
        own = [pltpu.make_async_copy(p_refs[b].at[mine], land_refs[b].at[mine], local_sems.at[b]) for b in range(n)]
        for cp in own:
            cp.start()
        copies = [pltpu.make_async_remote_copy(
            src_ref=p_refs[b].at[2 * cx + cy], dst_ref=land_refs[b].at[mine], send_sem=send_sems.at[b, j],
            recv_sem=recv_sems.at[b, j], device_id=(cx, cy, c), device_id_type=MESH)
            for b in range(n) for j, (cx, cy) in enumerate(chips)]
        for cp in copies:
            cp.start()
        for b in range(n):
            for j, (cx, cy) in enumerate(chips):
                pltpu.make_async_remote_copy(
                    src_ref=p_refs[b].at[mine], dst_ref=land_refs[b].at[2 * cx + cy], send_sem=send_sems.at[b, j],
                    recv_sem=recv_sems.at[b, j], device_id=(cx, cy, c), device_id_type=MESH).wait_recv()
        for cp in copies:
            cp.wait_send()
        for cp in own:
            cp.wait()

    return pl.pallas_call(
        body, name="rs_chip_exchange", in_specs=[ANY] * n, out_specs=[ANY] * n,
        out_shape=[jax.ShapeDtypeStruct(p.shape, p.dtype) for p in parts],
        scratch_shapes=[pltpu.SemaphoreType.DMA((n, 3)), pltpu.SemaphoreType.DMA((n, 3)),
                        pltpu.SemaphoreType.DMA((n,))],
    )(*parts)


def _adamw(parts, w, m, v, name):
    k, rows, cols = parts.shape
    tr = _row_tile(rows)
    c1 = 1.0 / (1.0 - ADAM_B1 ** ADAM_STEP)
    c2 = 1.0 / (1.0 - ADAM_B2 ** ADAM_STEP)

    def body(p_ref, w_ref, m_ref, v_ref, g_ref, d_ref, nm_ref, nv_ref):
        g = p_ref[0].astype(F32)
        for j in range(1, k):
            g = g + p_ref[j].astype(F32)
        g_ref[...] = g
        nm = ADAM_B1 * m_ref[...] + (1.0 - ADAM_B1) * g
        nv = ADAM_B2 * v_ref[...] + (1.0 - ADAM_B2) * (g * g)
        nm_ref[...] = nm
        nv_ref[...] = nv
        d_ref[...] = -ADAM_LR * ((nm * c1) / (jnp.sqrt(nv * c2) + ADAM_EPS) + ADAM_WD * w_ref[...])

    blk = pl.BlockSpec((tr, cols), lambda i: (i, 0))
    return pl.pallas_call(
        body, name=name, grid=(rows // tr,),
        in_specs=[pl.BlockSpec((k, tr, cols), lambda i: (0, i, 0)), blk, blk, blk],
        out_specs=[blk] * 4, out_shape=[jax.ShapeDtypeStruct((rows, cols), F32)] * 4,
        compiler_params=_cparams("parallel"),
    )(parts, w, m, v)


COL_SHARDED = ("w_in", "w_gate", "w_up", "w_ple_proj")
REPLICATED = (("g_mix", 1024), ("conv_b", 512), ("q_norm_g", 64), ("k_norm_g", 64), ("g_out_conv", 512),
              ("g_out_attn", 512), ("g_ffn", 1024), ("ffn_conv_b", 2816), ("g_ple", 1024))
CONV_SHARDED = (("conv_w", CONV_W), ("ffn_conv_w", D_FF))


def _gathered_to_full(name, gathered):
    if name in COL_SHARDED:
        return gathered.transpose(1, 0, 2).reshape(gathered.shape[1], -1)
    return gathered.reshape(-1, gathered.shape[2])


def _full_to_stacked(name, grad, shard_shape):
    sr, sc = shard_shape
    if name in COL_SHARDED:
        a = grad.reshape(sr, N_DEV, sc).transpose(1, 0, 2)
    else:
        a = grad.reshape(N_DEV, sr, sc)
    return a.astype(BF16).reshape(N_CHIP, 2, sr, sc)


def _pad_rows(vec, rows):
    return jnp.pad(vec, (0, rows * 1024 - vec.shape[0])).reshape(rows, 1024)


def kernel(x, p, g_mix, w_in, conv_w, conv_b, q_norm_g, k_norm_g, g_out_conv, g_out_attn, w_out, g_ffn, w_gate, w_up, ffn_conv_w, ffn_conv_b, w_down, g_ple, w_ple_gate, w_ple_proj, loss_target, m_g_mix, m_w_in, m_conv_w, m_conv_b, m_q_norm_g, m_k_norm_g, m_g_out_conv, m_g_out_attn, m_w_out, m_g_ffn, m_w_gate, m_w_up, m_ffn_conv_w, m_ffn_conv_b, m_w_down, m_g_ple, m_w_ple_gate, m_w_ple_proj, v_g_mix, v_w_in, v_conv_w, v_conv_b, v_q_norm_g, v_k_norm_g, v_g_out_conv, v_g_out_attn, v_w_out, v_g_ffn, v_w_gate, v_w_up, v_ffn_conv_w, v_ffn_conv_b, v_w_down, v_g_ple, v_w_ple_gate, v_w_ple_proj):
    args = dict(locals())
    names = ["g_mix", "w_in", "conv_w", "conv_b", "q_norm_g", "k_norm_g", "g_out_conv", "g_out_attn", "w_out", "g_ffn",
             "w_gate", "w_up", "ffn_conv_w", "ffn_conv_b", "w_down", "g_ple", "w_ple_gate", "w_ple_proj"]
    big = [n for n, _ in BIG_ROWS]
    conv = [n for n, _ in CONV_SHARDED]
    wts = {n: (args[n][0] if n in big or n in conv else args[n]) for n in names}
    mom = {n: (args["m_" + n][0] if n in big or n in conv else args["m_" + n]) for n in names}
    var = {n: (args["v_" + n][0] if n in big or n in conv else args["v_" + n]) for n in names}
    shard_shapes = {n: wts[n].shape for n in big}
    dev = 4 * lax.axis_index("x") + 2 * lax.axis_index("y") + lax.axis_index("c")
    core = lax.axis_index("c").astype(jnp.int32).reshape(1)

    conv_local = _pad_rows(jnp.concatenate([wts[n].reshape(-1) for n in conv]), 8).reshape(8, 1024)
    late = [n for n in big if n != "w_in"]
    w_in_all, conv_all = _all_gather([wts["w_in"].astype(BF16), conv_local], "gather_weights")
    late_shards = [wts[n].astype(BF16) for n in late]
    send_sems, recv_sems, shards_thru, lands_thru, token = _gather_start(late_shards, w_in_all)
    full = dict(wts)
    full["w_in"] = _gathered_to_full("w_in", w_in_all)
    full["g_mix"] = wts["g_mix"] + token[0:1, 0:1]

    def late_weights(after):
        lands = _gather_wait(send_sems, recv_sems, shards_thru, lands_thru, after)
        return {n: _gathered_to_full(n, lax.dynamic_update_slice(land, shard[None], (dev, 0, 0)))
                for n, land, shard in zip(late, lands, late_shards)}

    off = 0
    for n, width in CONV_SHARDED:
        sc = width // N_DEV
        a = conv_all.reshape(N_DEV, -1)[:, off:off + 3 * sc].reshape(N_DEV, 3, sc)
        full[n] = a.transpose(1, 0, 2).reshape(3, width)
        off += 3 * sc

    loss, dx, grads = _local_step(x[0], p[0, 0], loss_target[0], full, (512, 256), late_weights)

    stacked = [_full_to_stacked(n, grads[n], shard_shapes[n]) for n in big]
    landed = _sibling_exchange(stacked)
    contributions = _chip_exchange([_pair_sum(g, l, core, n) for n, g, l in zip(big, stacked, landed)])

    small = jnp.concatenate([grads[n].reshape(-1) for n, _ in REPLICATED] + [grads[n].reshape(-1) for n in conv]
                            + [loss.reshape(1)])
    (small_all,) = _all_gather([_pad_rows(small, SMALL_ROWS)], "gather_small_grads")

    big_out = {n: _adamw(c, wts[n], mom[n], var[n], f"adamw_{n}") for n, c in zip(big, contributions)}
    n_rep = sum(s for _, s in REPLICATED)
    conv_sizes = [3 * w_ // N_DEV for _, w_ in CONV_SHARDED]

    def small_state(src):
        flat = jnp.concatenate([src[n].reshape(-1) for n, _ in REPLICATED] + [src[n].reshape(-1) for n in conv])
        return _pad_rows(flat, 16)

    rep_all = small_all.reshape(N_DEV, -1)[:, :n_rep]
    conv_parts, off = [], n_rep
    for (n, width), size in zip(CONV_SHARDED, conv_sizes):
        sc = width // N_DEV
        a = small_all.reshape(N_DEV, -1)[:, off:off + 3 * width].reshape(N_DEV, 3, width)
        conv_parts.append(lax.dynamic_slice(a, (0, 0, dev * sc), (N_DEV, 3, sc)).reshape(N_DEV, size))
        off += 3 * width
    loss_total = jnp.sum(small_all.reshape(N_DEV, -1)[:, off])
    small_parts = jnp.concatenate([rep_all] + conv_parts, axis=1)
    small_parts = jnp.pad(small_parts, ((0, 0), (0, 16 * 1024 - small_parts.shape[1]))).reshape(N_DEV, 16, 1024)
    g_sm, d_sm, m_sm, v_sm = _adamw(small_parts, small_state(wts), small_state(mom), small_state(var), "adamw_small")

    def unpack(which, small_flat):
        out = {n: big_out[n][which] for n in big}
        flat, o = small_flat.reshape(-1), 0
        for n, s in list(REPLICATED) + [(n, sz) for (n, _), sz in zip(CONV_SHARDED, conv_sizes)]:
            out[n] = flat[o:o + s]
            o += s
        return [out[n].reshape(args[n].shape) for n in names]

    return (loss_total, dx[None], *unpack(0, g_sm), *unpack(1, d_sm), *unpack(2, m_sm), *unpack(3, v_sm))
```

```python
import functools

import jax
import jax.numpy as jnp
from jax import lax
from jax.experimental import pallas as pl
from jax.experimental.pallas import tpu as pltpu

F32 = jnp.float32
BF16 = jnp.bfloat16

D_MODEL = 1024
CONV_W = 512
ATTN_W = 512
HEAD_DIM = 64
D_FF = 2816
PLE_DIM = 256
IN_COLS = 3 * CONV_W + 3 * ATTN_W
EPS = 1e-6
QK_BLOCK = 128
DILATIONS = (1, 4, 16)
ATTN_SCALE = HEAD_DIM ** -0.5

ADAM_LR = 0.001
ADAM_B1 = 0.9
ADAM_B2 = 0.999
ADAM_EPS = 1e-08
ADAM_WD = 0.01
ADAM_STEP = 10

N_DEV = 8
N_CHIP = 4
V7X_VMEM_LIMIT = 56 * 1024 * 1024
FF_CHUNKS = 2

BIG_ROWS = (("w_in", 384), ("w_out", 128), ("w_gate", 352), ("w_up", 352), ("w_down", 352),
            ("w_ple_gate", 128), ("w_ple_proj", 32))
BIG_TOTAL = sum(r for _, r in BIG_ROWS)
SMALL_ROWS = 24


def _cparams(*sem):
    return pltpu.CompilerParams(dimension_semantics=sem, vmem_limit_bytes=V7X_VMEM_LIMIT)


def _mm(a, b):
    return jnp.dot(a, b, preferred_element_type=F32)


def _mm_nt(a, b):
    return lax.dot_general(a, b, (((1,), (1,)), ((), ())), preferred_element_type=F32)


def _mm_tn(a, b):
    return lax.dot_general(a, b, (((0,), (0,)), ((), ())), preferred_element_type=F32)


def _full(shape):
    nd = len(shape)
    return pl.BlockSpec(shape, lambda *_: (0,) * nd)


def _rms_stats(x):
    r = lax.rsqrt(jnp.mean(x * x, axis=-1, keepdims=True) + EPS)
    return r, x * r


def _rms_bwd(dy, xhat, r, g):
    gd = dy * g
    return r * (gd - xhat * jnp.mean(gd * xhat, axis=-1, keepdims=True))


def _seg_sum64(v, bd_ref):
    outs = []
    for c in range(0, v.shape[1], 256):
        vc = v[:, c:c + 256]
        hi = vc.astype(BF16)
        lo = (vc - hi.astype(F32)).astype(BF16)
        outs.append(_mm(hi, bd_ref[...]) + _mm(lo, bd_ref[...]))
    return outs[0] if len(outs) == 1 else jnp.concatenate(outs, axis=1)


def _shift_rows(u, k, edge_rows):
    row = lax.broadcasted_iota(jnp.int32, u.shape, 0)
    out = pltpu.roll(u, k, 0)
    for j in range(k):
        out = jnp.where(row == j, edge_rows[k - 1 - j], out)
    return out


def _shift_rows_up(u, k, edge_rows):
    n = u.shape[0]
    row = lax.broadcasted_iota(jnp.int32, u.shape, 0)
    out = pltpu.roll(u, n - k, 0)
    for j in range(k):
        out = jnp.where(row == n - k + j, edge_rows[j], out)
    return out


def _conv_fwd(u, c1, c2, w_ref, b_ref):
    u1 = _shift_rows(u, 1, (c1,))
    u2 = _shift_rows(u, 2, (c1, c2))
    y = u2 * w_ref[0:1, :] + u1 * w_ref[1:2, :] + u * w_ref[2:3, :] + b_ref[...]
    return y, u1, u2


def _conv_bwd_input(dy, n1row, n2row, w_ref):
    d1 = _shift_rows_up(dy, 1, (n1row,))
    d2 = _shift_rows_up(dy, 2, (n1row, n2row))
    return dy * w_ref[2:3, :] + d1 * w_ref[1:2, :] + d2 * w_ref[0:1, :]


def _sigmoid(x):
    return 1.0 / (1.0 + jnp.exp(-x))


def _inproj_fwd(x, g_mix, w_in, conv_w, conv_b, qg, kg, bd, tm):
    t = x.shape[0]

    def body(x_ref, g_ref, w_ref, cw_ref, cb_ref, qg_ref, kg_ref, bd_ref,
             zc_ref, zqk_ref, yc_ref, q_ref, k_ref, v_ref, carry_ref):
        @pl.when(pl.program_id(0) == 0)
        def _():
            carry_ref[...] = jnp.zeros_like(carry_ref)

        _, xhat = _rms_stats(x_ref[...])
        h = (xhat * g_ref[...]).astype(BF16)
        zconv = _mm(h, w_ref[:, 0:3 * CONV_W])
        zc_ref[...] = zconv
        u = zconv[:, CONV_W:2 * CONV_W] * zconv[:, 2 * CONV_W:3 * CONV_W]
        cv, _, _ = _conv_fwd(u, carry_ref[7:8, :], carry_ref[6:7, :], cw_ref, cb_ref)
        yc_ref[...] = zconv[:, 0:CONV_W] * cv
        carry_ref[...] = u[tm - 8:tm, :]

        zqk = _mm(h, w_ref[:, 3 * CONV_W:3 * CONV_W + 2 * ATTN_W])
        zqk_ref[...] = zqk
        for j, (gain_ref, out_ref, scale) in enumerate(((qg_ref, q_ref, ATTN_SCALE), (kg_ref, k_ref, 1.0))):
            z = zqk[:, j * ATTN_W:(j + 1) * ATTN_W]
            r = lax.rsqrt(_seg_sum64(z * z, bd_ref) * (1.0 / HEAD_DIM) + EPS)
            out_ref[...] = z * r * gain_ref[...] * scale
        v_ref[...] = _mm(h, w_ref[:, 3 * CONV_W + 2 * ATTN_W:IN_COLS])

    def blk(c):
        return pl.BlockSpec((tm, c), lambda i: (i, 0))

    return pl.pallas_call(
        body, name="inproj_fwd", grid=(t // tm,),
        in_specs=[blk(D_MODEL), _full((1, D_MODEL)), _full((D_MODEL, IN_COLS)), _full((3, CONV_W)),
                  _full((1, CONV_W)), _full((1, ATTN_W)), _full((1, ATTN_W)), _full((256, 256))],
        out_specs=[blk(3 * CONV_W), blk(2 * ATTN_W), blk(CONV_W), blk(ATTN_W), blk(ATTN_W), blk(ATTN_W)],
        out_shape=[jax.ShapeDtypeStruct((t, 3 * CONV_W), F32), jax.ShapeDtypeStruct((t, 2 * ATTN_W), F32),
                   jax.ShapeDtypeStruct((t, CONV_W), F32), jax.ShapeDtypeStruct((t, ATTN_W), F32),
                   jax.ShapeDtypeStruct((t, ATTN_W), F32), jax.ShapeDtypeStruct((t, ATTN_W), F32)],
        scratch_shapes=[pltpu.VMEM((8, CONV_W), F32)],
        compiler_params=_cparams("arbitrary"),
    )(x, g_mix, w_in, conv_w, conv_b, qg, kg, bd)


SUPER = 16 * QK_BLOCK
KEYS = 2 * QK_BLOCK


def _rows(start, size, dil):
    return pl.ds(start, size) if dil == 1 else pl.ds(start, size, stride=dil)


def _attn_bias(sl_ref, dil):
    qi = lax.broadcasted_iota(jnp.int32, (KEYS, KEYS), 0)
    kj = lax.broadcasted_iota(jnp.int32, (KEYS, KEYS), 1)
    step = jnp.bitwise_and(qi, QK_BLOCK - 1) + QK_BLOCK - kj
    slope = jnp.where(qi < QK_BLOCK, sl_ref[0, 0:1, 0:1], sl_ref[0, 1:2, 0:1])
    bias = jnp.where(jnp.logical_and(step >= 0, step <= QK_BLOCK), -slope * (step * dil).astype(F32), -jnp.inf)
    return bias, kj >= QK_BLOCK


def _unit_start(u, dil):
    if dil == 1:
        return pl.multiple_of(u * QK_BLOCK, QK_BLOCK)
    if dil == 4:
        return jnp.bitwise_and(u, 3) + (u // 4) * (4 * QK_BLOCK)
    return u


def _stack_heads(a, head0):
    zero = jnp.zeros_like(a)
    return jnp.concatenate([jnp.where(head0, a, zero), jnp.where(head0, zero, a)], axis=0)


def _attn_fwd(q, k, v, slopes):
    t = q.shape[0]
    nsb = t // SUPER

    def body(q_ref, kc_ref, kp_ref, vc_ref, vp_ref, sl_ref, o_ref, l_ref, kk, vv, ob, lb):
        s = pl.program_id(1)
        kk[0:SUPER, :] = kp_ref[...]
        kk[SUPER:, :] = kc_ref[...]
        vv[0:SUPER, :] = vp_ref[...]
        vv[SUPER:, :] = vc_ref[...]
        head0 = lax.broadcasted_iota(jnp.int32, (QK_BLOCK, QK_BLOCK), 1) < HEAD_DIM

        for b, dil in enumerate(DILATIONS):
            bias, own_half = _attn_bias(sl_ref, dil)

            def unit(u, carry, b=b, dil=dil, bias=bias, own_half=own_half):
                start = _unit_start(u, dil)
                first_key = SUPER + start - QK_BLOCK * dil
                q2 = _stack_heads(q_ref[_rows(start, QK_BLOCK, dil), :].astype(BF16), head0)
                k2 = kk[_rows(first_key, KEYS, dil), :].astype(BF16)
                v2 = vv[_rows(first_key, KEYS, dil), :].astype(BF16)
                has_prev = jnp.logical_or(s > 0, start >= QK_BLOCK * dil)
                sc = jnp.where(jnp.logical_or(own_half, has_prev), _mm_nt(q2, k2) + bias, -jnp.inf)
                m = jnp.max(sc, axis=-1, keepdims=True)
                e = jnp.exp(sc - m)
                den = jnp.sum(e, axis=-1, keepdims=True)
                o2 = _mm(e.astype(BF16), v2) / den
                l2 = m + jnp.log(den)
                ob[b, _rows(start, QK_BLOCK, dil), :] = jnp.where(head0, o2[0:QK_BLOCK], o2[QK_BLOCK:])
                lb[b, _rows(start, QK_BLOCK, dil), :] = jnp.where(head0, l2[0:QK_BLOCK], l2[QK_BLOCK:])
                return carry

            lax.fori_loop(0, SUPER // QK_BLOCK, unit, 0, unroll=4)

        def merge(i, carry):
            rows = pl.ds(pl.multiple_of(i * 256, 256), 256)
            la, lb_, lc = lb[0, rows, :], lb[1, rows, :], lb[2, rows, :]
            mx = jnp.maximum(jnp.maximum(la, lb_), lc)
            wa, wb, wc = jnp.exp(la - mx), jnp.exp(lb_ - mx), jnp.exp(lc - mx)
            sw = wa + wb + wc
            o_ref[rows, :] = (wa * ob[0, rows, :] + wb * ob[1, rows, :] + wc * ob[2, rows, :]) / sw
            l_ref[rows, :] = mx + jnp.log(sw)
            return carry

        lax.fori_loop(0, SUPER // 256, merge, 0)

    cur = pl.BlockSpec((SUPER, QK_BLOCK), lambda p, s: (s, p))
    prev = pl.BlockSpec((SUPER, QK_BLOCK), lambda p, s: (jnp.maximum(s - 1, 0), p))
    return pl.pallas_call(
        body, name="attn_fwd", grid=(4, nsb),
        in_specs=[cur, cur, prev, cur, prev, pl.BlockSpec((1, 2, QK_BLOCK), lambda p, s: (p, 0, 0))],
        out_specs=[cur, cur],
        out_shape=[jax.ShapeDtypeStruct((t, ATTN_W), F32), jax.ShapeDtypeStruct((t, ATTN_W), F32)],
        scratch_shapes=[pltpu.VMEM((2 * SUPER, QK_BLOCK), F32), pltpu.VMEM((2 * SUPER, QK_BLOCK), F32),
                        pltpu.VMEM((3, SUPER, QK_BLOCK), F32), pltpu.VMEM((3, SUPER, QK_BLOCK), F32)],
        compiler_params=_cparams("parallel", "arbitrary"),
    )(q, k, k, v, v, slopes)


def _outproj_fwd(ya, yc, x, goc, goa, w_out, tm):
    t = x.shape[0]

    def body(ya_ref, yc_ref, x_ref, goc_ref, goa_ref, w_ref, x1_ref):
        _, ychat = _rms_stats(yc_ref[...])
        _, yahat = _rms_stats(ya_ref[...])
        acc = _mm((ychat * goc_ref[...]).astype(BF16), w_ref[0:CONV_W, :])
        acc += _mm((yahat * goa_ref[...]).astype(BF16), w_ref[CONV_W:, :])
        x1_ref[...] = x_ref[...] + acc

    def blk(c):
        return pl.BlockSpec((tm, c), lambda i: (i, 0))

    return pl.pallas_call(
        body, name="outproj_fwd", grid=(t // tm,),
        in_specs=[blk(ATTN_W), blk(CONV_W), blk(D_MODEL), _full((1, CONV_W)), _full((1, ATTN_W)),
                  _full((D_MODEL, D_MODEL))],
        out_specs=blk(D_MODEL),
        out_shape=jax.ShapeDtypeStruct((t, D_MODEL), F32),
        compiler_params=_cparams("parallel"),
    )(ya, yc, x, goc, goa, w_out)


def _ffn_fwd(x1, g_ffn, w_gate, w_up, w_down, fcw, fcb, tm):
    t = x1.shape[0]

    def body(x_ref, g_ref, wg_ref, wu_ref, wd_ref, cw_ref, cb_ref, gp_ref, up_ref, x2_ref, carry_ref):
        @pl.when(pl.program_id(0) == 0)
        def _():
            carry_ref[...] = jnp.zeros_like(carry_ref)

        xv = x_ref[...]
        _, xhat = _rms_stats(xv)
        h = (xhat * g_ref[...]).astype(BF16)
        gp = _mm(h, wg_ref[...])
        gp_ref[...] = gp
        gate, _, _ = _conv_fwd(gp, carry_ref[7:8, :], carry_ref[6:7, :], cw_ref, cb_ref)
        carry_ref[...] = gp[tm - 8:tm, :]
        up = _mm(h, wu_ref[...])
        up_ref[...] = up
        a = (gate * _sigmoid(gate) * up).astype(BF16)
        x2_ref[...] = xv + _mm(a, wd_ref[...])

    def blk(c):
        return pl.BlockSpec((tm, c), lambda i: (i, 0))

    return pl.pallas_call(
        body, name="ffn_fwd", grid=(t // tm,),
        in_specs=[blk(D_MODEL), _full((1, D_MODEL)), _full((D_MODEL, D_FF)), _full((D_MODEL, D_FF)),
                  _full((D_FF, D_MODEL)), _full((3, D_FF)), _full((1, D_FF))],
        out_specs=[blk(D_FF), blk(D_FF), blk(D_MODEL)],
        out_shape=[jax.ShapeDtypeStruct((t, D_FF), F32), jax.ShapeDtypeStruct((t, D_FF), F32),
                   jax.ShapeDtypeStruct((t, D_MODEL), F32)],
        scratch_shapes=[pltpu.VMEM((8, D_FF), F32)],
        compiler_params=_cparams("arbitrary"),
    )(x1, g_ffn, w_gate, w_up, w_down, fcw, fcb)


def _ple_fwd_bwd(x2, p, target, g_ple, w_pg, w_pp, tm):
    t = x2.shape[0]

    def body(x_ref, p_ref, t_ref, g_ref, wg_ref, wp_ref, dx_ref, loss_ref, dwg_ref, dwp_ref, dg_ref):
        @pl.when(pl.program_id(0) == 0)
        def _():
            loss_ref[...] = jnp.zeros_like(loss_ref)
            dwg_ref[...] = jnp.zeros_like(dwg_ref)
            dwp_ref[...] = jnp.zeros_like(dwp_ref)
            dg_ref[...] = jnp.zeros_like(dg_ref)

        xv = x_ref[...]
        r, xhat = _rms_stats(xv)
        g = g_ref[...]
        h = (xhat * g).astype(BF16)
        pg = _sigmoid(_mm(h, wg_ref[...]))
        pb = p_ref[...].astype(BF16)
        pp = _mm(pb, wp_ref[...])
        err = xv + pg * pp - t_ref[...]
        loss_ref[...] += 0.5 * jnp.sum(jnp.mean(err * err, axis=-1, keepdims=True))
        dx3 = err * (1.0 / D_MODEL)
        d_pp = (dx3 * pg).astype(BF16)
        d_pre = (dx3 * pp * pg * (1.0 - pg)).astype(BF16)
        dwp_ref[...] += _mm_tn(pb, d_pp)
        dwg_ref[...] += _mm_tn(h, d_pre)
        dh = _mm_nt(d_pre, wg_ref[...])
        dg_ref[...] += jnp.sum(dh * xhat, axis=0, keepdims=True)
        dx_ref[...] = dx3 + _rms_bwd(dh, xhat, r, g)

    def blk(c):
        return pl.BlockSpec((tm, c), lambda i: (i, 0))

    return pl.pallas_call(
        body, name="ple_fwd_bwd", grid=(t // tm,),
        in_specs=[blk(D_MODEL), blk(PLE_DIM), blk(D_MODEL), _full((1, D_MODEL)), _full((D_MODEL, D_MODEL)),
                  _full((PLE_DIM, D_MODEL))],
        out_specs=[blk(D_MODEL), _full((8, 128)), _full((D_MODEL, D_MODEL)), _full((PLE_DIM, D_MODEL)),
                   _full((1, D_MODEL))],
        out_shape=[jax.ShapeDtypeStruct((t, D_MODEL), F32), jax.ShapeDtypeStruct((8, 128), F32),
                   jax.ShapeDtypeStruct((D_MODEL, D_MODEL), F32), jax.ShapeDtypeStruct((PLE_DIM, D_MODEL), F32),
                   jax.ShapeDtypeStruct((1, D_MODEL), F32)],
        compiler_params=_cparams("arbitrary"),
    )(x2, p, target, g_ple, w_pg, w_pp)


def _ffn_bwd(dx2, x1, g_ffn, gp, up, w_gate, w_up, w_down, fcw, fcb, tm):
    t = x1.shape[0]
    nblk = t // tm
    fc = D_FF // FF_CHUNKS

    def body(dx_ref, x_ref, g_ref, gp_ref, gph_ref, up_ref, wg_ref, wu_ref, wd_ref, cw_ref, cb_ref,
             dh_ref, dwd_ref, dwu_ref, dwg_ref, dcw_ref, dcb_ref, carry_ref):
        i = pl.program_id(1)

        @pl.when(i == 0)
        def _():
            carry_ref[...] = jnp.zeros_like(carry_ref)
            dwd_ref[...] = jnp.zeros_like(dwd_ref)
            dwu_ref[...] = jnp.zeros_like(dwu_ref)
            dwg_ref[...] = jnp.zeros_like(dwg_ref)
            dcw_ref[...] = jnp.zeros_like(dcw_ref)
            dcb_ref[...] = jnp.zeros_like(dcb_ref)

        keep = (i < nblk - 1).astype(F32)
        dxb = dx_ref[...].astype(BF16)
        _, xhat = _rms_stats(x_ref[...])
        h = (xhat * g_ref[...]).astype(BF16)
        gp_v = gp_ref[...]
        gate, gp1, gp2 = _conv_fwd(gp_v, gph_ref[7:8, :] * keep, gph_ref[6:7, :] * keep, cw_ref, cb_ref)
        s = _sigmoid(gate)
        silu = gate * s
        up_v = up_ref[...]
        da = _mm_nt(dxb, wd_ref[...])
        dwd_ref[...] += _mm_tn((silu * up_v).astype(BF16), dxb)
        d_up = (da * silu).astype(BF16)
        d_gate = da * up_v * (s * (1.0 + gate * (1.0 - s)))
        dwu_ref[...] += _mm_tn(h, d_up)
        d_gp = _conv_bwd_input(d_gate, carry_ref[0:1, :], carry_ref[1:2, :], cw_ref).astype(BF16)
        carry_ref[...] = d_gate[0:8, :]
        dcw_ref[0:1, :] += jnp.sum(d_gate * gp2, axis=0, keepdims=True)
        dcw_ref[1:2, :] += jnp.sum(d_gate * gp1, axis=0, keepdims=True)
        dcw_ref[2:3, :] += jnp.sum(d_gate * gp_v, axis=0, keepdims=True)
        dcb_ref[...] += jnp.sum(d_gate, axis=0, keepdims=True)
        dwg_ref[...] += _mm_tn(h, d_gp)
        dh_ref[...] = _mm_nt(d_gp, wg_ref[...]) + _mm_nt(d_up, wu_ref[...])

    def rev(i):
        return nblk - 1 - i

    one = pl.Buffered(1)
    in_specs = [
        pl.BlockSpec((tm, D_MODEL), lambda j, i: (rev(i), 0)),
        pl.BlockSpec((tm, D_MODEL), lambda j, i: (rev(i), 0)),
        _full((1, D_MODEL)),
        pl.BlockSpec((tm, fc), lambda j, i: (rev(i), j)),
        pl.BlockSpec((8, fc), lambda j, i: (jnp.maximum(rev(i) * (tm // 8) - 1, 0), j)),
        pl.BlockSpec((tm, fc), lambda j, i: (rev(i), j)),
        pl.BlockSpec((D_MODEL, fc), lambda j, i: (0, j), pipeline_mode=one),
        pl.BlockSpec((D_MODEL, fc), lambda j, i: (0, j), pipeline_mode=one),
        pl.BlockSpec((fc, D_MODEL), lambda j, i: (j, 0), pipeline_mode=one),
        pl.BlockSpec((3, fc), lambda j, i: (0, j)),
        pl.BlockSpec((1, fc), lambda j, i: (0, j)),
    ]
    out_specs = [
        pl.BlockSpec((None, tm, D_MODEL), lambda j, i: (j, rev(i), 0)),
        pl.BlockSpec((fc, D_MODEL), lambda j, i: (j, 0), pipeline_mode=one),
        pl.BlockSpec((D_MODEL, fc), lambda j, i: (0, j), pipeline_mode=one),
        pl.BlockSpec((D_MODEL, fc), lambda j, i: (0, j), pipeline_mode=one),
        pl.BlockSpec((3, fc), lambda j, i: (0, j)),
        pl.BlockSpec((1, fc), lambda j, i: (0, j)),
    ]
    return pl.pallas_call(
        body, name="ffn_bwd", grid=(FF_CHUNKS, nblk), in_specs=in_specs, out_specs=out_specs,
        out_shape=[jax.ShapeDtypeStruct((FF_CHUNKS, t, D_MODEL), F32), jax.ShapeDtypeStruct((D_FF, D_MODEL), F32),
                   jax.ShapeDtypeStruct((D_MODEL, D_FF), F32), jax.ShapeDtypeStruct((D_MODEL, D_FF), F32),
                   jax.ShapeDtypeStruct((3, D_FF), F32), jax.ShapeDtypeStruct((1, D_FF), F32)],
        scratch_shapes=[pltpu.VMEM((8, fc), F32)],
        compiler_params=_cparams("arbitrary", "arbitrary"),
    )(dx2, x1, g_ffn, gp, gp, up, w_gate, w_up, w_down, fcw, fcb)


def _outproj_bwd(dh2, dx2, x1, g_ffn, w_out, yc, ya, goc, goa, zconv, conv_w, conv_b, bd, tm):
    t = x1.shape[0]
    nblk = t // tm

    def body(dh_ref, dx2_ref, x1_ref, g_ref, w_ref, yc_ref, ya_ref, goc_ref, goa_ref, zc_ref, zch_ref, cw_ref, cb_ref,
             bd_ref, dx1_ref, dya_ref, dd_ref, dzc_ref, dw_ref, dg_ref, dgoc_ref, dgoa_ref, dcw_ref, dcb_ref,
             carry_ref):
        i = pl.program_id(0)

        @pl.when(i == 0)
        def _():
            carry_ref[...] = jnp.zeros_like(carry_ref)
            for ref in (dw_ref, dg_ref, dgoc_ref, dgoa_ref, dcw_ref, dcb_ref):
                ref[...] = jnp.zeros_like(ref)

        keep = (i < nblk - 1).astype(F32)
        dh2_v = dh_ref[0]
        for j in range(1, FF_CHUNKS):
            dh2_v = dh2_v + dh_ref[j]
        r, xhat = _rms_stats(x1_ref[...])
        dg_ref[...] += jnp.sum(dh2_v * xhat, axis=0, keepdims=True)
        dx1 = dx2_ref[...] + _rms_bwd(dh2_v, xhat, r, g_ref[...])
        dx1_ref[...] = dx1
        dx1b = dx1.astype(BF16)
        dy = _mm_nt(dx1b, w_ref[...])

        yc_v = yc_ref[...]
        rc, ychat = _rms_stats(yc_v)
        dw_ref[0:CONV_W, :] += _mm_tn((ychat * goc_ref[...]).astype(BF16), dx1b)
        dyc = dy[:, 0:CONV_W]
        dgoc_ref[...] += jnp.sum(dyc * ychat, axis=0, keepdims=True)
        d_yc = _rms_bwd(dyc, ychat, rc, goc_ref[...])

        ya_v = ya_ref[...]
        ra, yahat = _rms_stats(ya_v)
        dw_ref[CONV_W:, :] += _mm_tn((yahat * goa_ref[...]).astype(BF16), dx1b)
        dya = dy[:, CONV_W:]
        dgoa_ref[...] += jnp.sum(dya * yahat, axis=0, keepdims=True)
        d_ya = _rms_bwd(dya, yahat, ra, goa_ref[...])
        dya_ref[...] = d_ya
        dd_ref[...] = _seg_sum64(d_ya * ya_v, bd_ref)

        zb = zc_ref[:, 0:CONV_W]
        zc = zc_ref[:, CONV_W:2 * CONV_W]
        zx = zc_ref[:, 2 * CONV_W:3 * CONV_W]
        u = zc * zx
        uh = zch_ref[:, CONV_W:2 * CONV_W] * zch_ref[:, 2 * CONV_W:3 * CONV_W] * keep
        cv, u1, u2 = _conv_fwd(u, uh[7:8, :], uh[6:7, :], cw_ref, cb_ref)
        d_cv = d_yc * zb
        d_u = _conv_bwd_input(d_cv, carry_ref[0:1, :], carry_ref[1:2, :], cw_ref)
        carry_ref[...] = d_cv[0:8, :]
        dcw_ref[0:1, :] += jnp.sum(d_cv * u2, axis=0, keepdims=True)
        dcw_ref[1:2, :] += jnp.sum(d_cv * u1, axis=0, keepdims=True)
        dcw_ref[2:3, :] += jnp.sum(d_cv * u, axis=0, keepdims=True)
        dcb_ref[...] += jnp.sum(d_cv, axis=0, keepdims=True)
        dzc_ref[:, 0:CONV_W] = d_yc * cv
        dzc_ref[:, CONV_W:2 * CONV_W] = d_u * zx
        dzc_ref[:, 2 * CONV_W:3 * CONV_W] = d_u * zc

    def rev(i):
        return nblk - 1 - i

    def blk(c):
        return pl.BlockSpec((tm, c), lambda i: (rev(i), 0))

    in_specs = [
        pl.BlockSpec((FF_CHUNKS, tm, D_MODEL), lambda i: (0, rev(i), 0)),
        blk(D_MODEL), blk(D_MODEL), _full((1, D_MODEL)), _full((D_MODEL, D_MODEL)),
        blk(CONV_W), blk(ATTN_W), _full((1, CONV_W)), _full((1, ATTN_W)),
        blk(3 * CONV_W),
        pl.BlockSpec((8, 3 * CONV_W), lambda i: (jnp.maximum(rev(i) * (tm // 8) - 1, 0), 0)),
        _full((3, CONV_W)), _full((1, CONV_W)), _full((256, 256)),
    ]
    out_specs = [blk(D_MODEL), blk(ATTN_W), blk(ATTN_W), blk(3 * CONV_W), _full((D_MODEL, D_MODEL)),
                 _full((1, D_MODEL)), _full((1, CONV_W)), _full((1, ATTN_W)), _full((3, CONV_W)), _full((1, CONV_W))]
    return pl.pallas_call(
        body, name="outproj_bwd", grid=(nblk,), in_specs=in_specs, out_specs=out_specs,
        out_shape=[jax.ShapeDtypeStruct((t, D_MODEL), F32), jax.ShapeDtypeStruct((t, ATTN_W), F32),
                   jax.ShapeDtypeStruct((t, ATTN_W), F32), jax.ShapeDtypeStruct((t, 3 * CONV_W), F32),
                   jax.ShapeDtypeStruct((D_MODEL, D_MODEL), F32), jax.ShapeDtypeStruct((1, D_MODEL), F32),
                   jax.ShapeDtypeStruct((1, CONV_W), F32), jax.ShapeDtypeStruct((1, ATTN_W), F32),
                   jax.ShapeDtypeStruct((3, CONV_W), F32), jax.ShapeDtypeStruct((1, CONV_W), F32)],
        scratch_shapes=[pltpu.VMEM((8, CONV_W), F32)],
        compiler_params=_cparams("arbitrary"),
    )(dh2, dx2, x1, g_ffn, w_out, yc, ya, goc, goa, zconv, zconv, conv_w, conv_b, bd)


def _attn_bwd(q, k, v, dya, lse, dd, slopes):
    t = q.shape[0]
    nsb = t // SUPER

    def body(q_ref, kc_ref, kp_ref, vc_ref, vp_ref, dy_ref, l_ref, d_ref, sl_ref, dq_ref, dk_ref, dv_ref,
             kk, vv, dkacc, dvacc):
        s = pl.program_id(1)

        @pl.when(s == 0)
        def _():
            dkacc[...] = jnp.zeros_like(dkacc)
            dvacc[...] = jnp.zeros_like(dvacc)

        dkacc[0:SUPER, :] = dkacc[SUPER:, :]
        dvacc[0:SUPER, :] = dvacc[SUPER:, :]
        dkacc[SUPER:, :] = jnp.zeros((SUPER, QK_BLOCK), F32)
        dvacc[SUPER:, :] = jnp.zeros((SUPER, QK_BLOCK), F32)

        @pl.when(s < nsb)
        def _():
            kk[0:SUPER, :] = kp_ref[...]
            kk[SUPER:, :] = kc_ref[...]
            vv[0:SUPER, :] = vp_ref[...]
            vv[SUPER:, :] = vc_ref[...]
            head0 = lax.broadcasted_iota(jnp.int32, (QK_BLOCK, QK_BLOCK), 1) < HEAD_DIM

            for b, dil in enumerate(DILATIONS):
                bias, own_half = _attn_bias(sl_ref, dil)

                def unit(u, carry, b=b, dil=dil, bias=bias, own_half=own_half):
                    start = _unit_start(u, dil)
                    first_key = SUPER + start - QK_BLOCK * dil
                    qrows = _rows(start, QK_BLOCK, dil)
                    krows = _rows(first_key, KEYS, dil)
                    q2 = _stack_heads(q_ref[qrows, :].astype(BF16), head0)
                    dy2 = _stack_heads(dy_ref[qrows, :].astype(BF16), head0)
                    lv, dv_ = l_ref[qrows, :], d_ref[qrows, :]
                    l2 = jnp.concatenate([lv[:, 0:1], lv[:, HEAD_DIM:HEAD_DIM + 1]], axis=0)
                    d2 = jnp.concatenate([dv_[:, 0:1], dv_[:, HEAD_DIM:HEAD_DIM + 1]], axis=0)
                    k2 = kk[krows, :].astype(BF16)
                    v2 = vv[krows, :].astype(BF16)
                    has_prev = jnp.logical_or(s > 0, start >= QK_BLOCK * dil)
                    sc = jnp.where(jnp.logical_or(own_half, has_prev), _mm_nt(q2, k2) + bias, -jnp.inf)
                    prob = jnp.exp(sc - l2)
                    ds = (prob * (_mm_nt(dy2, v2) - d2)).astype(BF16)
                    dvacc[krows, :] += _mm_tn(prob.astype(BF16), dy2)
                    dkacc[krows, :] += _mm_tn(ds, q2)
                    dq2 = _mm(ds, k2)
                    dq = jnp.where(head0, dq2[0:QK_BLOCK], dq2[QK_BLOCK:]) * ATTN_SCALE
                    if b == 0:
                        dq_ref[qrows, :] = dq
                    else:
                        dq_ref[qrows, :] += dq
                    return carry

                lax.fori_loop(0, SUPER // QK_BLOCK, unit, 0, unroll=4)

        dk_ref[...] = dkacc[0:SUPER, :]
        dv_ref[...] = dvacc[0:SUPER, :]

    def cur_map(p, s):
        return (jnp.minimum(s, nsb - 1), p)

    def prev_map(p, s):
        return (jnp.clip(s - 1, 0, nsb - 1), p)

    cur = pl.BlockSpec((SUPER, QK_BLOCK), cur_map)
    prev = pl.BlockSpec((SUPER, QK_BLOCK), prev_map)
    return pl.pallas_call(
        body, name="attn_bwd", grid=(4, nsb + 1),
        in_specs=[cur, cur, prev, cur, prev, cur, cur, cur, pl.BlockSpec((1, 2, QK_BLOCK), lambda p, s: (p, 0, 0))],
        out_specs=[cur, prev, prev],
        out_shape=[jax.ShapeDtypeStruct((t, ATTN_W), F32)] * 3,
        scratch_shapes=[pltpu.VMEM((2 * SUPER, QK_BLOCK), F32)] * 4,
        compiler_params=_cparams("parallel", "arbitrary"),
    )(q, k, k, v, v, dya, lse, dd, slopes)


def _attn_bwd_per_branch_unused(q, k, v, dya, lse, dd, slopes, dil):
    t = q.shape[0]
    length = t // dil
    chunk = _attn_chunk(t, dil)
    nch = length // chunk
    nb = chunk // QK_BLOCK
    nblocks = length // QK_BLOCK
    view = (length, dil * ATTN_W)
    ext = chunk + QK_BLOCK

    def body(q_ref, dy_ref, l_ref, d_ref, k_ref, v_ref, qn_ref, dyn_ref, ln_ref, dn_ref, kh_ref, vh_ref, sl_ref,
             dq_ref, dk_ref, dv_ref, qbuf, dybuf, lbuf, dbuf, kbuf, vbuf, dkacc, dvacc):
        c = pl.program_id(2)
        qbuf[0:chunk, :] = q_ref[...]
        qbuf[chunk:, :] = qn_ref[...]
        dybuf[0:chunk, :] = dy_ref[...].astype(BF16)
        dybuf[chunk:, :] = dyn_ref[...].astype(BF16)
        lbuf[0:chunk, :] = l_ref[...]
        lbuf[chunk:, :] = ln_ref[...]
        dbuf[0:chunk, :] = d_ref[...]
        dbuf[chunk:, :] = dn_ref[...]
        kbuf[0:QK_BLOCK, :] = kh_ref[...]
        kbuf[QK_BLOCK:, :] = k_ref[...]
        vbuf[0:QK_BLOCK, :] = vh_ref[...]
        vbuf[QK_BLOCK:, :] = v_ref[...]
        valid_cur, valid_prev, dist_cur, dist_prev, head0 = _attn_masks(dil)

        def pair(qb, dyb, lv, dv_, kb, vb, valid, dist):
            dq = jnp.zeros((QK_BLOCK, QK_BLOCK), F32)
            dk = jnp.zeros((QK_BLOCK, QK_BLOCK), F32)
            dvv = jnp.zeros((QK_BLOCK, QK_BLOCK), F32)
            for hh in range(2):
                sl = sl_ref[0, hh:hh + 1, :]
                hm = head0 if hh == 0 else jnp.logical_not(head0)
                col = hh * HEAD_DIM
                qm = jnp.where(hm, qb, jnp.zeros_like(qb))
                dym = jnp.where(hm, dyb, jnp.zeros_like(dyb))
                s = jnp.where(valid, _mm_nt(qm, kb) - sl * dist, -jnp.inf)
                prob = jnp.exp(s - lv[:, col:col + 1])
                ds = (prob * (_mm_nt(dym, vb) - dv_[:, col:col + 1])).astype(BF16)
                dvv += _mm_tn(prob.astype(BF16), dym)
                dk += _mm_tn(ds, qm)
                dq += jnp.where(hm, _mm(ds, kb), 0.0)
            return dq, dk, dvv

        def blk(j, carry):
            off = pl.multiple_of(j * QK_BLOCK, QK_BLOCK)
            nxt = pl.multiple_of(off + QK_BLOCK, QK_BLOCK)
            qb = qbuf[pl.ds(off, QK_BLOCK), :]
            dyb = dybuf[pl.ds(off, QK_BLOCK), :]
            lv = lbuf[pl.ds(off, QK_BLOCK), :]
            dv_ = dbuf[pl.ds(off, QK_BLOCK), :]
            dq_c, dk_c, dv_c = pair(qb, dyb, lv, dv_, kbuf[pl.ds(nxt, QK_BLOCK), :], vbuf[pl.ds(nxt, QK_BLOCK), :],
                                    valid_cur, dist_cur)
            dkacc[pl.ds(nxt, QK_BLOCK), :] = dk_c
            dvacc[pl.ds(nxt, QK_BLOCK), :] = dv_c
            has_prev = jnp.logical_or(c > 0, j > 0)
            dq_p, dk_p, dv_p = pair(qb, dyb, lv, dv_, kbuf[pl.ds(off, QK_BLOCK), :], vbuf[pl.ds(off, QK_BLOCK), :],
                                    jnp.logical_and(valid_prev, has_prev), dist_prev)

            @pl.when(j > 0)
            def _():
                dkacc[pl.ds(off, QK_BLOCK), :] += dk_p
                dvacc[pl.ds(off, QK_BLOCK), :] += dv_p

            dq_ref[pl.ds(off, QK_BLOCK), :] = (dq_c + dq_p) * ATTN_SCALE
            return carry

        lax.fori_loop(0, nb, blk, 0)

        @pl.when(c < nch - 1)
        def _():
            _, dk_p, dv_p = pair(qbuf[chunk:, :], dybuf[chunk:, :], lbuf[chunk:, :], dbuf[chunk:, :],
                                 kbuf[chunk:, :], vbuf[chunk:, :], valid_prev, dist_prev)
            dkacc[chunk:, :] += dk_p
            dvacc[chunk:, :] += dv_p

        dk_ref[...] = dkacc[QK_BLOCK:, :]
        dv_ref[...] = dvacc[QK_BLOCK:, :]

    def cmap(p, r, c):
        return (c, r * 4 + p)

    def before(p, r, c):
        return (jnp.maximum(c * nb - 1, 0), r * 4 + p)

    def after(p, r, c):
        return (jnp.minimum((c + 1) * nb, nblocks - 1), r * 4 + p)

    main = pl.BlockSpec((chunk, QK_BLOCK), cmap)
    hb = pl.BlockSpec((QK_BLOCK, QK_BLOCK), before)
    ha = pl.BlockSpec((QK_BLOCK, QK_BLOCK), after)
    qv, kv, vv = q.reshape(view), k.reshape(view), v.reshape(view)
    dyv, lv, ddv = dya.reshape(view), lse.reshape(view), dd.reshape(view)
    outs = pl.pallas_call(
        body, name=f"attn_bwd_d{dil}", grid=(4, dil, nch),
        in_specs=[main] * 6 + [ha] * 4 + [hb] * 2 + [pl.BlockSpec((1, 2, QK_BLOCK), lambda p, r, c: (p, 0, 0))],
        out_specs=[main] * 3,
        out_shape=[jax.ShapeDtypeStruct(view, F32)] * 3,
        scratch_shapes=[pltpu.VMEM((ext, QK_BLOCK), BF16), pltpu.VMEM((ext, QK_BLOCK), BF16),
                        pltpu.VMEM((ext, QK_BLOCK), F32), pltpu.VMEM((ext, QK_BLOCK), F32),
                        pltpu.VMEM((ext, QK_BLOCK), BF16), pltpu.VMEM((ext, QK_BLOCK), BF16),
                        pltpu.VMEM((ext, QK_BLOCK), F32), pltpu.VMEM((ext, QK_BLOCK), F32)],
        compiler_params=_cparams("arbitrary", "arbitrary", "arbitrary"),
    )(qv, dyv, lv, ddv, kv, vv, qv, dyv, lv, ddv, kv, vv, slopes)
    return [o.reshape(t, ATTN_W) for o in outs]


def _inproj_bwd(dq, dk, dv, dzconv, zqk, x, dx1, g_mix, w_in, qg, kg, bd, tm):
    t = x.shape[0]

    def body(dq_ref, dk_ref, dv_ref, dzc_ref, zqk_ref, x_ref, dx1_ref, g_ref, w_ref, qg_ref,
             kg_ref, bd_ref, dx_ref, dw_ref, dg_ref, dqg_ref, dkg_ref):
        @pl.when(pl.program_id(0) == 0)
        def _():
            for ref in (dw_ref, dg_ref, dqg_ref, dkg_ref):
                ref[...] = jnp.zeros_like(ref)

        parts = [dzc_ref[...].astype(BF16)]
        for j, (dn_ref, gain_ref, dgain_ref) in enumerate(((dq_ref, qg_ref, dqg_ref), (dk_ref, kg_ref, dkg_ref))):
            dn = dn_ref[...]
            z = zqk_ref[:, j * ATTN_W:(j + 1) * ATTN_W]
            r = lax.rsqrt(_seg_sum64(z * z, bd_ref) * (1.0 / HEAD_DIM) + EPS)
            zhat = z * r
            dgain_ref[...] += jnp.sum(dn * zhat, axis=0, keepdims=True)
            gd = dn * gain_ref[...]
            parts.append((r * (gd - zhat * (_seg_sum64(gd * zhat, bd_ref) * (1.0 / HEAD_DIM)))).astype(BF16))
        parts.append(dv_ref[...].astype(BF16))
        dz = jnp.concatenate(parts, axis=1)

        r, xhat = _rms_stats(x_ref[...])
        g = g_ref[...]
        dw_ref[...] += _mm_tn((xhat * g).astype(BF16), dz)
        dh = _mm_nt(dz, w_ref[...])
        dg_ref[...] += jnp.sum(dh * xhat, axis=0, keepdims=True)
        dx_ref[...] = dx1_ref[...] + _rms_bwd(dh, xhat, r, g)

    def blk(c):
        return pl.BlockSpec((tm, c), lambda i: (i, 0))

    return pl.pallas_call(
        body, name="inproj_bwd", grid=(t // tm,),
        in_specs=[blk(ATTN_W)] * 3 + [blk(3 * CONV_W), blk(2 * ATTN_W), blk(D_MODEL), blk(D_MODEL), _full((1, D_MODEL)),
                                      _full((D_MODEL, IN_COLS)), _full((1, ATTN_W)), _full((1, ATTN_W)),
                                      _full((256, 256))],
        out_specs=[blk(D_MODEL), _full((D_MODEL, IN_COLS)), _full((1, D_MODEL)), _full((1, ATTN_W)),
                   _full((1, ATTN_W))],
        out_shape=[jax.ShapeDtypeStruct((t, D_MODEL), F32), jax.ShapeDtypeStruct((D_MODEL, IN_COLS), F32),
                   jax.ShapeDtypeStruct((1, D_MODEL), F32), jax.ShapeDtypeStruct((1, ATTN_W), F32),
                   jax.ShapeDtypeStruct((1, ATTN_W), F32)],
        compiler_params=_cparams("arbitrary"),
    )(dq, dk, dv, dzconv, zqk, x, dx1, g_mix, w_in, qg, kg, bd)


def _ordered_after(a, token):
    return a if token is None else a + token[0:1, 0:1].reshape((1,) * a.ndim)


def _local_step(x, p, target, w, tms, hooks=None):
    hooks = hooks or {}
    bd = jnp.kron(jnp.eye(4, dtype=F32), jnp.ones((HEAD_DIM, HEAD_DIM), F32)).astype(BF16)
    qg = jnp.tile(w["q_norm_g"], (1, 8))
    kg = jnp.tile(w["k_norm_g"], (1, 8))
    slopes = jnp.exp2(-jnp.arange(1, 9, dtype=F32))
    slopes = jnp.broadcast_to(slopes.reshape(4, 2, 1), (4, 2, QK_BLOCK))

    zconv, zqk, yc, q, k, v = _inproj_fwd(x, w["g_mix"], w["w_in"], w["conv_w"], w["conv_b"], qg, kg, bd, tms[0])
    ya, lse = _attn_fwd(q, k, v, slopes)
    if "late_weights" in hooks:
        w = {**w, **hooks["late_weights"](lse)}
    x1 = _outproj_fwd(ya, yc, x, w["g_out_conv"], w["g_out_attn"], w["w_out"], tms[0])
    gp, up, x2 = _ffn_fwd(x1, w["g_ffn"], w["w_gate"], w["w_up"], w["w_down"], w["ffn_conv_w"], w["ffn_conv_b"], tms[1])
    dx2, loss, dw_pg, dw_pp, dg_ple = _ple_fwd_bwd(x2, p, target, w["g_ple"], w["w_ple_gate"], w["w_ple_proj"], tms[0])
    dh2, dw_down, dw_up, dw_gate, dfcw, dfcb = _ffn_bwd(dx2, x1, w["g_ffn"], gp, up, w["w_gate"], w["w_up"],
                                                        w["w_down"], w["ffn_conv_w"], w["ffn_conv_b"], tms[1])
    token = None
    if "ffn_grads" in hooks:
        token = hooks["ffn_grads"]({"w_ple_gate": dw_pg, "w_ple_proj": dw_pp, "w_down": dw_down, "w_up": dw_up,
                                    "w_gate": dw_gate})
    dx1, dya, dd, dzconv, dw_out, dg_ffn, dgoc, dgoa, dcw, dcb = _outproj_bwd(
        dh2, dx2, x1, _ordered_after(w["g_ffn"], token), w["w_out"], yc, ya, w["g_out_conv"], w["g_out_attn"], zconv,
        w["conv_w"], w["conv_b"], bd, tms[1])
    token = hooks["outproj_done"](dx1) if "outproj_done" in hooks else None
    dq, dk, dv = _attn_bwd(q, k, v, dya, lse, dd, _ordered_after(slopes, token))
    dx, dw_in, dg_mix, dqg, dkg = _inproj_bwd(dq, dk, dv, dzconv, zqk, x, dx1, w["g_mix"], w["w_in"], qg, kg, bd,
                                              tms[1])
    grads = {
        "g_mix": dg_mix, "w_in": dw_in, "conv_w": dcw, "conv_b": dcb,
        "q_norm_g": dqg.reshape(8, HEAD_DIM).sum(0, keepdims=True),
        "k_norm_g": dkg.reshape(8, HEAD_DIM).sum(0, keepdims=True),
        "g_out_conv": dgoc, "g_out_attn": dgoa, "w_out": dw_out, "g_ffn": dg_ffn, "w_gate": dw_gate, "w_up": dw_up,
        "ffn_conv_w": dfcw, "ffn_conv_b": dfcb, "w_down": dw_down, "g_ple": dg_ple, "w_ple_gate": dw_pg,
        "w_ple_proj": dw_pp,
    }
    return loss[0, 0], dx, grads


ANY = pl.BlockSpec(memory_space=pl.ANY)
MESH = pl.DeviceIdType.MESH


def _all_gather(shards, name):
    n = len(shards)

    def body(*refs):
        ins, outs = refs[:n], refs[n:2 * n]
        send_sems, recv_sems, local_sems = refs[2 * n:]
        x, y, c = lax.axis_index("x"), lax.axis_index("y"), lax.axis_index("c")
        me, sibling = (x, y, c), (x, y, 1 - c)
        chips = [(1 - x, y), (x, 1 - y), (1 - x, 1 - y)]

        def slot(dev):
            return 4 * dev[0] + 2 * dev[1] + dev[2]

        def copy(b, k, block, to, src=None):
            dst = outs[b].at[slot(block)]
            return pltpu.make_async_remote_copy(
                src_ref=dst if src is None else src, dst_ref=dst, send_sem=send_sems.at[b, k],
                recv_sem=recv_sems.at[b, k], device_id=to, device_id_type=MESH)

        mine = [pltpu.make_async_copy(ins[b], outs[b].at[slot(me)], local_sems.at[b]) for b in range(n)]
        first, passed = [], []
        for b in range(n):
            mine[b].start()
            first.append(copy(b, 0, me, sibling, src=ins[b]))
            first += [copy(b, 1 + j, me, (*chip, c), src=ins[b]) for j, chip in enumerate(chips)]
        for cp in first:
            cp.start()
        for j, chip in enumerate(chips):
            for b in range(n):
                copy(b, 1 + j, (*chip, c), me).wait_recv()
                fwd = copy(b, 4 + j, (*chip, c), sibling)
                fwd.start()
                passed.append(fwd)
        for b in range(n):
            copy(b, 0, sibling, me).wait_recv()
            for j, chip in enumerate(chips):
                copy(b, 4 + j, (*chip, 1 - c), me).wait_recv()
        for cp in first + passed:
            cp.wait_send()
        for cp in mine:
            cp.wait()

    return pl.pallas_call(
        body, name=name,
        in_specs=[ANY] * n, out_specs=[ANY] * n,
        out_shape=[jax.ShapeDtypeStruct((N_DEV,) + s.shape, s.dtype) for s in shards],
        scratch_shapes=[pltpu.SemaphoreType.DMA((n, 7)), pltpu.SemaphoreType.DMA((n, 7)),
                        pltpu.SemaphoreType.DMA((n,))],
    )(*shards)


HBM = pl.BlockSpec(memory_space=pltpu.HBM)
SEM = pl.BlockSpec(memory_space=pltpu.SEMAPHORE)
EFFECT = pltpu.SideEffectType.DATAFLOW_SIDE_EFFECTING
FLIPS = ((0, 0, 1), (0, 1, 0), (0, 1, 1), (1, 0, 0), (1, 0, 1), (1, 1, 0), (1, 1, 1))


def _flip_peers():
    pos = (lax.axis_index("x"), lax.axis_index("y"), lax.axis_index("c"))
    return [tuple(1 - a if f else a for a, f in zip(pos, flip)) for flip in FLIPS]


def _hbm(a):
    return pltpu.with_memory_space_constraint(a, pltpu.HBM)


def _split_start(name, srcs, lands, plan, n_copies, after):
    n, m = len(srcs), len(lands)

    def body(*refs):
        send_sems, recv_sems, token = refs[n + m + 1], refs[n + m + 2], refs[-1]
        for i, (src, dst, peer) in enumerate(plan(refs[:n], refs[n:n + m])):
            pltpu.make_async_remote_copy(src_ref=src, dst_ref=dst, send_sem=send_sems.at[i], recv_sem=recv_sems.at[i],
                                         device_id=peer, device_id_type=MESH).start()
        token[...] = jnp.zeros_like(token)

    outs = pl.pallas_call(
        body, name=name + "_start",
        in_specs=[HBM] * (n + m) + [ANY],
        out_specs=[SEM, SEM] + [HBM] * (n + m) + [pl.BlockSpec(memory_space=pltpu.VMEM)],
        out_shape=[pltpu.SemaphoreType.DMA((n_copies,)), pltpu.SemaphoreType.DMA((n_copies,))]
        + [pltpu.HBM(a.shape, a.dtype) for a in list(srcs) + list(lands)] + [jax.ShapeDtypeStruct((8, 128), F32)],
        input_output_aliases={i: 2 + i for i in range(n + m)},
        compiler_params=pltpu.CompilerParams(has_side_effects=EFFECT),
    )(*[_hbm(a) for a in list(srcs) + list(lands)], after)
    return (outs[0], outs[1], outs[2:2 + n], outs[2 + n:2 + n + m]), outs[-1]


def _split_wait(name, started, plan, after):
    send_sems, recv_sems, srcs, lands = started
    n, m = len(srcs), len(lands)

    def body(*refs):
        send_ref, recv_ref = refs[n + m], refs[n + m + 1]
        for i, (src, dst, peer) in enumerate(plan(refs[:n], refs[n:n + m])):
            copy = pltpu.make_async_remote_copy(src_ref=src, dst_ref=dst, send_sem=send_ref.at[i],
                                                recv_sem=recv_ref.at[i], device_id=peer, device_id_type=MESH)
            copy.wait_send()
            copy.wait_recv()

    outs = pl.pallas_call(
        body, name=name + "_wait",
        in_specs=[HBM] * (n + m) + [SEM, SEM, ANY],
        out_specs=[HBM] * (n + m),
        out_shape=[pltpu.HBM(a.shape, a.dtype) for a in list(srcs) + list(lands)],
        input_output_aliases={i: i for i in range(n + m)},
        compiler_params=pltpu.CompilerParams(has_side_effects=EFFECT),
    )(*srcs, *lands, send_sems, recv_sems, after)
    return outs[:n], outs[n:]


def _gather_plan(srcs, lands):
    slot = 4 * lax.axis_index("x") + 2 * lax.axis_index("y") + lax.axis_index("c")
    return [(src, land.at[slot], peer) for src, land in zip(srcs, lands) for peer in _flip_peers()]


def _sibling_plan(srcs, lands):
    x, y, c = lax.axis_index("x"), lax.axis_index("y"), lax.axis_index("c")
    return [(src.at[k, 1 - c], land.at[k], (x, y, 1 - c)) for src, land in zip(srcs, lands) for k in range(N_CHIP)]


def _chip_plan(srcs, lands):
    x, y, c = lax.axis_index("x"), lax.axis_index("y"), lax.axis_index("c")
    return [(src.at[2 * cx + cy], land.at[2 * x + y], (cx, cy, c))
            for src, land in zip(srcs, lands) for cx, cy in ((1 - x, y), (x, 1 - y), (1 - x, 1 - y))]


def _row_tile(rows):
    for tr in range(min(rows, 512), 15, -16):
        if rows % tr == 0:
            return tr
    return rows


def _sibling_exchange(gs):
    n = len(gs)

    def body(*refs):
        g_refs, land_refs = refs[:n], refs[n:2 * n]
        send_sems, recv_sems = refs[2 * n:]
        x, y, c = lax.axis_index("x"), lax.axis_index("y"), lax.axis_index("c")
        copies = [pltpu.make_async_remote_copy(
            src_ref=g_refs[b].at[k, 1 - c], dst_ref=land_refs[b].at[k], send_sem=send_sems.at[b, k],
            recv_sem=recv_sems.at[b, k], device_id=(x, y, 1 - c), device_id_type=MESH)
            for b in range(n) for k in range(N_CHIP)]
        for cp in copies:
            cp.start()
        for cp in copies:
            cp.wait()

    return pl.pallas_call(
        body, name="rs_sibling_exchange", in_specs=[ANY] * n, out_specs=[ANY] * n,
        out_shape=[jax.ShapeDtypeStruct((N_CHIP,) + g.shape[2:], g.dtype) for g in gs],
        scratch_shapes=[pltpu.SemaphoreType.DMA((n, N_CHIP)), pltpu.SemaphoreType.DMA((n, N_CHIP))],
    )(*gs)


def _pair_sum(g, land, core, name):
    rows, cols = land.shape[1:]
    tr = _row_tile(rows)

    def body(c_ref, g_ref, l_ref, o_ref):
        o_ref[...] = (g_ref[...].astype(F32) + l_ref[...].astype(F32)).astype(o_ref.dtype)

    return pl.pallas_call(
        body, name=f"rs_pair_sum_{name}",
        grid_spec=pltpu.PrefetchScalarGridSpec(
            num_scalar_prefetch=1, grid=(N_CHIP, rows // tr),
            in_specs=[pl.BlockSpec((None, None, tr, cols), lambda k, i, c_ref: (k, c_ref[0], i, 0)),
                      pl.BlockSpec((None, tr, cols), lambda k, i, c_ref: (k, i, 0))],
            out_specs=pl.BlockSpec((None, tr, cols), lambda k, i, c_ref: (k, i, 0))),
        out_shape=jax.ShapeDtypeStruct(land.shape, land.dtype),
        compiler_params=_cparams("parallel", "parallel"),
    )(core, g, land)


def _chip_exchange(parts):
    n = len(parts)

    def body(*refs):
        p_refs, land_refs = refs[:n], refs[n:2 * n]
        send_sems, recv_sems, local_sems = refs[2 * n:]
        x, y, c = lax.axis_index("x"), lax.axis_index("y"), lax.axis_index("c")
        mine = 2 * x + y
        chips = [(1 - x, y), (x, 1 - y), (1 - x, 1 - y)]
        own = [pltpu.make_async_copy(p_refs[b].at[mine], land_refs[b].at[mine], local_sems.at[b]) for b in range(n)]
        for cp in own:
            cp.start()
        copies = [pltpu.make_async_remote_copy(
            src_ref=p_refs[b].at[2 * cx + cy], dst_ref=land_refs[b].at[mine], send_sem=send_sems.at[b, j],
            recv_sem=recv_sems.at[b, j], device_id=(cx, cy, c), device_id_type=MESH)
            for b in range(n) for j, (cx, cy) in enumerate(chips)]
        for cp in copies:
            cp.start()
        for b in range(n):
            for j, (cx, cy) in enumerate(chips):
                pltpu.make_async_remote_copy(
                    src_ref=p_refs[b].at[mine], dst_ref=land_refs[b].at[2 * cx + cy], send_sem=send_sems.at[b, j],
                    recv_sem=recv_sems.at[b, j], device_id=(cx, cy, c), device_id_type=MESH).wait_recv()
        for cp in copies:
            cp.wait_send()
        for cp in own:
            cp.wait()

    return pl.pallas_call(
        body, name="rs_chip_exchange", in_specs=[ANY] * n, out_specs=[ANY] * n,
        out_shape=[jax.ShapeDtypeStruct(p.shape, p.dtype) for p in parts],
        scratch_shapes=[pltpu.SemaphoreType.DMA((n, 3)), pltpu.SemaphoreType.DMA((n, 3)),
                        pltpu.SemaphoreType.DMA((n,))],
    )(*parts)


def _adamw(parts, w, m, v, name):
    k, rows, cols = parts.shape
    tr = _row_tile(rows)
    c1 = 1.0 / (1.0 - ADAM_B1 ** ADAM_STEP)
    c2 = 1.0 / (1.0 - ADAM_B2 ** ADAM_STEP)

    def body(p_ref, w_ref, m_ref, v_ref, g_ref, d_ref, nm_ref, nv_ref):
        g = p_ref[0].astype(F32)
        for j in range(1, k):
            g = g + p_ref[j].astype(F32)
        g_ref[...] = g
        nm = ADAM_B1 * m_ref[...] + (1.0 - ADAM_B1) * g
        nv = ADAM_B2 * v_ref[...] + (1.0 - ADAM_B2) * (g * g)
        nm_ref[...] = nm
        nv_ref[...] = nv
        d_ref[...] = -ADAM_LR * ((nm * c1) / (jnp.sqrt(nv * c2) + ADAM_EPS) + ADAM_WD * w_ref[...])

    blk = pl.BlockSpec((tr, cols), lambda i: (i, 0))
    return pl.pallas_call(
        body, name=name, grid=(rows // tr,),
        in_specs=[pl.BlockSpec((k, tr, cols), lambda i: (0, i, 0)), blk, blk, blk],
        out_specs=[blk] * 4, out_shape=[jax.ShapeDtypeStruct((rows, cols), F32)] * 4,
        compiler_params=_cparams("parallel"),
    )(parts, w, m, v)


COL_SHARDED = ("w_in", "w_gate", "w_up", "w_ple_proj")
REPLICATED = (("g_mix", 1024), ("conv_b", 512), ("q_norm_g", 64), ("k_norm_g", 64), ("g_out_conv", 512),
              ("g_out_attn", 512), ("g_ffn", 1024), ("ffn_conv_b", 2816), ("g_ple", 1024))
CONV_SHARDED = (("conv_w", CONV_W), ("ffn_conv_w", D_FF))


def _gathered_to_full(name, gathered):
    if name in COL_SHARDED:
        return gathered.transpose(1, 0, 2).reshape(gathered.shape[1], -1)
    return gathered.reshape(-1, gathered.shape[2])


def _full_to_stacked(name, grad, shard_shape):
    sr, sc = shard_shape
    if name in COL_SHARDED:
        a = grad.reshape(sr, N_DEV, sc).transpose(1, 0, 2)
    else:
        a = grad.reshape(N_DEV, sr, sc)
    return a.astype(BF16).reshape(N_CHIP, 2, sr, sc)


def _pad_rows(vec, rows):
    return jnp.pad(vec, (0, rows * 1024 - vec.shape[0])).reshape(rows, 1024)


def kernel(x, p, g_mix, w_in, conv_w, conv_b, q_norm_g, k_norm_g, g_out_conv, g_out_attn, w_out, g_ffn, w_gate, w_up, ffn_conv_w, ffn_conv_b, w_down, g_ple, w_ple_gate, w_ple_proj, loss_target, m_g_mix, m_w_in, m_conv_w, m_conv_b, m_q_norm_g, m_k_norm_g, m_g_out_conv, m_g_out_attn, m_w_out, m_g_ffn, m_w_gate, m_w_up, m_ffn_conv_w, m_ffn_conv_b, m_w_down, m_g_ple, m_w_ple_gate, m_w_ple_proj, v_g_mix, v_w_in, v_conv_w, v_conv_b, v_q_norm_g, v_k_norm_g, v_g_out_conv, v_g_out_attn, v_w_out, v_g_ffn, v_w_gate, v_w_up, v_ffn_conv_w, v_ffn_conv_b, v_w_down, v_g_ple, v_w_ple_gate, v_w_ple_proj):
    args = dict(locals())
    names = ["g_mix", "w_in", "conv_w", "conv_b", "q_norm_g", "k_norm_g", "g_out_conv", "g_out_attn", "w_out", "g_ffn",
             "w_gate", "w_up", "ffn_conv_w", "ffn_conv_b", "w_down", "g_ple", "w_ple_gate", "w_ple_proj"]
    big = [n for n, _ in BIG_ROWS]
    conv = [n for n, _ in CONV_SHARDED]
    wts = {n: (args[n][0] if n in big or n in conv else args[n]) for n in names}
    mom = {n: (args["m_" + n][0] if n in big or n in conv else args["m_" + n]) for n in names}
    var = {n: (args["v_" + n][0] if n in big or n in conv else args["v_" + n]) for n in names}
    shard_shapes = {n: wts[n].shape for n in big}
    dev = 4 * lax.axis_index("x") + 2 * lax.axis_index("y") + lax.axis_index("c")
    core = lax.axis_index("c").astype(jnp.int32).reshape(1)

    conv_local = _pad_rows(jnp.concatenate([wts[n].reshape(-1) for n in conv]), 8).reshape(8, 1024)
    late = [n for n in big if n != "w_in"]
    w_in_all, conv_all = _all_gather([wts["w_in"].astype(BF16), conv_local], "gather_weights")
    late_shards = [wts[n].astype(BF16) for n in late]
    gathering, token = _split_start("gather_late_weights", late_shards,
                                    [lax.empty((N_DEV,) + s.shape, BF16) for s in late_shards], _gather_plan,
                                    7 * len(late), w_in_all)
    full = dict(wts)
    full["w_in"] = _gathered_to_full("w_in", w_in_all)
    full["g_mix"] = _ordered_after(wts["g_mix"], token)
    flying = {}

    def late_weights(after):
        shards, lands = _split_wait("gather_late_weights", gathering, _gather_plan, after)
        return {n: _gathered_to_full(n, lax.dynamic_update_slice(land, shard[None], (dev, 0, 0)))
                for n, land, shard in zip(late, lands, shards)}

    early = ["w_ple_gate", "w_ple_proj", "w_down", "w_up", "w_gate"]

    def ffn_grads(g):
        stacked = [_full_to_stacked(n, g[n], shard_shapes[n]) for n in early]
        flying["sibling"], tok = _split_start("rs_sibling_early", stacked,
                                              [lax.empty((N_CHIP,) + s.shape[2:], BF16) for s in stacked],
                                              _sibling_plan, N_CHIP * len(early), g["w_down"])
        return tok

    def outproj_done(after):
        stacked, landed = _split_wait("rs_sibling_early", flying["sibling"], _sibling_plan, after)
        parts = [_pair_sum(g, l, core, n) for n, g, l in zip(early, stacked, landed)]
        flying["chip"], tok = _split_start("rs_chip_early", parts, [lax.empty(q.shape, BF16) for q in parts],
                                           _chip_plan, 3 * len(early), landed[0])
        return tok

    off = 0
    for n, width in CONV_SHARDED:
        sc = width // N_DEV
        a = conv_all.reshape(N_DEV, -1)[:, off:off + 3 * sc].reshape(N_DEV, 3, sc)
        full[n] = a.transpose(1, 0, 2).reshape(3, width)
        off += 3 * sc

    loss, dx, grads = _local_step(x[0], p[0, 0], loss_target[0], full, (512, 256),
                                  {"late_weights": late_weights, "ffn_grads": ffn_grads, "outproj_done": outproj_done})

    last = [n for n in big if n not in early]
    stacked = [_full_to_stacked(n, grads[n], shard_shapes[n]) for n in last]
    landed = _sibling_exchange(stacked)
    contributions = dict(zip(last, _chip_exchange([_pair_sum(g, l, core, n)
                                                   for n, g, l in zip(last, stacked, landed)])))
    chip = 2 * lax.axis_index("x") + lax.axis_index("y")
    parts, arrived = _split_wait("rs_chip_early", flying["chip"], _chip_plan, dx)
    for n, part, land in zip(early, parts, arrived):
        own = lax.dynamic_slice(part, (chip, 0, 0), (1,) + part.shape[1:])
        contributions[n] = lax.dynamic_update_slice(land, own, (chip, 0, 0))
    contributions = [contributions[n] for n in big]

    small = jnp.concatenate([grads[n].reshape(-1) for n, _ in REPLICATED] + [grads[n].reshape(-1) for n in conv]
                            + [loss.reshape(1)])
    (small_all,) = _all_gather([_pad_rows(small, SMALL_ROWS)], "gather_small_grads")

    big_out = {n: _adamw(c, wts[n], mom[n], var[n], f"adamw_{n}") for n, c in zip(big, contributions)}
    n_rep = sum(s for _, s in REPLICATED)
    conv_sizes = [3 * w_ // N_DEV for _, w_ in CONV_SHARDED]

    def small_state(src):
        flat = jnp.concatenate([src[n].reshape(-1) for n, _ in REPLICATED] + [src[n].reshape(-1) for n in conv])
        return _pad_rows(flat, 16)

    rep_all = small_all.reshape(N_DEV, -1)[:, :n_rep]
    conv_parts, off = [], n_rep
    for (n, width), size in zip(CONV_SHARDED, conv_sizes):
        sc = width // N_DEV
        a = small_all.reshape(N_DEV, -1)[:, off:off + 3 * width].reshape(N_DEV, 3, width)
        conv_parts.append(lax.dynamic_slice(a, (0, 0, dev * sc), (N_DEV, 3, sc)).reshape(N_DEV, size))
        off += 3 * width
    loss_total = jnp.sum(small_all.reshape(N_DEV, -1)[:, off])
    small_parts = jnp.concatenate([rep_all] + conv_parts, axis=1)
    small_parts = jnp.pad(small_parts, ((0, 0), (0, 16 * 1024 - small_parts.shape[1]))).reshape(N_DEV, 16, 1024)
    g_sm, d_sm, m_sm, v_sm = _adamw(small_parts, small_state(wts), small_state(mom), small_state(var), "adamw_small")

    def unpack(which, small_flat):
        out = {n: big_out[n][which] for n in big}
        flat, o = small_flat.reshape(-1), 0
        for n, s in list(REPLICATED) + [(n, sz) for (n, _), sz in zip(CONV_SHARDED, conv_sizes)]:
            out[n] = flat[o:o + s]
            o += s
        return [out[n].reshape(args[n].shape) for n in names]

    return (loss_total, dx[None], *unpack(0, g_sm), *unpack(1, d_sm), *unpack(2, m_sm), *unpack(3, v_sm))
```

```python
import functools

import jax
import jax.numpy as jnp
from jax import lax
from jax.experimental import pallas as pl
from jax.experimental.pallas import tpu as pltpu

F32 = jnp.float32
BF16 = jnp.bfloat16

D_MODEL = 1024
CONV_W = 512
ATTN_W = 512
HEAD_DIM = 64
D_FF = 2816
PLE_DIM = 256
IN_COLS = 3 * CONV_W + 3 * ATTN_W
EPS = 1e-6
QK_BLOCK = 128
DILATIONS = (1, 4, 16)
ATTN_SCALE = HEAD_DIM ** -0.5

ADAM_LR = 0.001
ADAM_B1 = 0.9
ADAM_B2 = 0.999
ADAM_EPS = 1e-08
ADAM_WD = 0.01
ADAM_STEP = 10

N_DEV = 8
N_CHIP = 4
V7X_VMEM_LIMIT = 56 * 1024 * 1024
FF_CHUNKS = 2

BIG_ROWS = (("w_in", 384), ("w_out", 128), ("w_gate", 352), ("w_up", 352), ("w_down", 352),
            ("w_ple_gate", 128), ("w_ple_proj", 32))
BIG_TOTAL = sum(r for _, r in BIG_ROWS)
SMALL_ROWS = 24


def _cparams(*sem):
    return pltpu.CompilerParams(dimension_semantics=sem, vmem_limit_bytes=V7X_VMEM_LIMIT)


def _mm(a, b):
    return jnp.dot(a, b, preferred_element_type=F32)


def _mm_nt(a, b):
    return lax.dot_general(a, b, (((1,), (1,)), ((), ())), preferred_element_type=F32)


def _mm_tn(a, b):
    return lax.dot_general(a, b, (((0,), (0,)), ((), ())), preferred_element_type=F32)


def _full(shape):
    nd = len(shape)
    return pl.BlockSpec(shape, lambda *_: (0,) * nd)


def _rms_stats(x):
    r = lax.rsqrt(jnp.mean(x * x, axis=-1, keepdims=True) + EPS)
    return r, x * r


def _rms_bwd(dy, xhat, r, g):
    gd = dy * g
    return r * (gd - xhat * jnp.mean(gd * xhat, axis=-1, keepdims=True))


def _seg_sum64(v, bd_ref):
    outs = []
    for c in range(0, v.shape[1], 256):
        vc = v[:, c:c + 256]
        hi = vc.astype(BF16)
        lo = (vc - hi.astype(F32)).astype(BF16)
        outs.append(_mm(hi, bd_ref[...]) + _mm(lo, bd_ref[...]))
    return outs[0] if len(outs) == 1 else jnp.concatenate(outs, axis=1)


def _shift_rows(u, k, edge_rows):
    out = pltpu.roll(u, k, 0)
    row = lax.broadcasted_iota(jnp.int32, (8, u.shape[1]), 0)
    head = out[0:8]
    for j in range(k):
        head = jnp.where(row == j, edge_rows[k - 1 - j], head)
    return jnp.concatenate([head, out[8:]], axis=0)


def _shift_rows_up(u, k, edge_rows):
    n = u.shape[0]
    out = pltpu.roll(u, n - k, 0)
    row = lax.broadcasted_iota(jnp.int32, (8, u.shape[1]), 0)
    tail = out[n - 8:n]
    for j in range(k):
        tail = jnp.where(row == 8 - k + j, edge_rows[j], tail)
    return jnp.concatenate([out[0:n - 8], tail], axis=0)


def _conv_fwd(u, c1, c2, w_ref, b_ref):
    u1 = _shift_rows(u, 1, (c1,))
    u2 = _shift_rows(u, 2, (c1, c2))
    y = u2 * w_ref[0:1, :] + u1 * w_ref[1:2, :] + u * w_ref[2:3, :] + b_ref[...]
    return y, u1, u2


def _conv_bwd_input(dy, n1row, n2row, w_ref):
    d1 = _shift_rows_up(dy, 1, (n1row,))
    d2 = _shift_rows_up(dy, 2, (n1row, n2row))
    return dy * w_ref[2:3, :] + d1 * w_ref[1:2, :] + d2 * w_ref[0:1, :]


def _sigmoid(x):
    return 1.0 / (1.0 + jnp.exp(-x))


def _inproj_fwd(x, g_mix, w_in, conv_w, conv_b, qg, kg, bd, tm):
    t = x.shape[0]

    def body(x_ref, g_ref, w_ref, cw_ref, cb_ref, qg_ref, kg_ref, bd_ref,
             zc_ref, zqk_ref, yc_ref, q_ref, k_ref, v_ref, carry_ref):
        @pl.when(pl.program_id(0) == 0)
        def _():
            carry_ref[...] = jnp.zeros_like(carry_ref)

        _, xhat = _rms_stats(x_ref[...])
        h = (xhat * g_ref[...]).astype(BF16)
        zconv = _mm(h, w_ref[:, 0:3 * CONV_W])
        zc_ref[...] = zconv
        u = zconv[:, CONV_W:2 * CONV_W] * zconv[:, 2 * CONV_W:3 * CONV_W]
        cv, _, _ = _conv_fwd(u, carry_ref[7:8, :], carry_ref[6:7, :], cw_ref, cb_ref)
        yc_ref[...] = zconv[:, 0:CONV_W] * cv
        carry_ref[...] = u[tm - 8:tm, :]

        zqk = _mm(h, w_ref[:, 3 * CONV_W:3 * CONV_W + 2 * ATTN_W])
        zqk_ref[...] = zqk
        for j, (gain_ref, out_ref, scale) in enumerate(((qg_ref, q_ref, ATTN_SCALE), (kg_ref, k_ref, 1.0))):
            z = zqk[:, j * ATTN_W:(j + 1) * ATTN_W]
            r = lax.rsqrt(_seg_sum64(z * z, bd_ref) * (1.0 / HEAD_DIM) + EPS)
            out_ref[...] = z * r * gain_ref[...] * scale
        v_ref[...] = _mm(h, w_ref[:, 3 * CONV_W + 2 * ATTN_W:IN_COLS])

    def blk(c):
        return pl.BlockSpec((tm, c), lambda i: (i, 0))

    return pl.pallas_call(
        body, name="inproj_fwd", grid=(t // tm,),
        in_specs=[blk(D_MODEL), _full((1, D_MODEL)), _full((D_MODEL, IN_COLS)), _full((3, CONV_W)),
                  _full((1, CONV_W)), _full((1, ATTN_W)), _full((1, ATTN_W)), _full((256, 256))],
        out_specs=[blk(3 * CONV_W), blk(2 * ATTN_W), blk(CONV_W), blk(ATTN_W), blk(ATTN_W), blk(ATTN_W)],
        out_shape=[jax.ShapeDtypeStruct((t, 3 * CONV_W), F32), jax.ShapeDtypeStruct((t, 2 * ATTN_W), F32),
                   jax.ShapeDtypeStruct((t, CONV_W), F32), jax.ShapeDtypeStruct((t, ATTN_W), F32),
                   jax.ShapeDtypeStruct((t, ATTN_W), F32), jax.ShapeDtypeStruct((t, ATTN_W), F32)],
        scratch_shapes=[pltpu.VMEM((8, CONV_W), F32)],
        compiler_params=_cparams("arbitrary"),
    )(x, g_mix, w_in, conv_w, conv_b, qg, kg, bd)


SUPER = 16 * QK_BLOCK
KEYS = 2 * QK_BLOCK


def _rows(start, size, dil):
    return pl.ds(start, size) if dil == 1 else pl.ds(start, size, stride=dil)


def _attn_bias(sl_ref, dil):
    qi = lax.broadcasted_iota(jnp.int32, (KEYS, KEYS), 0)
    kj = lax.broadcasted_iota(jnp.int32, (KEYS, KEYS), 1)
    step = jnp.bitwise_and(qi, QK_BLOCK - 1) + QK_BLOCK - kj
    slope = jnp.where(qi < QK_BLOCK, sl_ref[0, 0:1, 0:1], sl_ref[0, 1:2, 0:1])
    bias = jnp.where(jnp.logical_and(step >= 0, step <= QK_BLOCK), -slope * (step * dil).astype(F32), -jnp.inf)
    return bias, kj >= QK_BLOCK


def _unit_start(u, dil):
    if dil == 1:
        return pl.multiple_of(u * QK_BLOCK, QK_BLOCK)
    if dil == 4:
        return jnp.bitwise_and(u, 3) + (u // 4) * (4 * QK_BLOCK)
    return u


def _stack_heads(a, head0):
    zero = jnp.zeros_like(a)
    return jnp.concatenate([jnp.where(head0, a, zero), jnp.where(head0, zero, a)], axis=0)


def _attn_fwd(q, k, v, slopes):
    t = q.shape[0]
    nsb = t // SUPER

    def body(q_ref, kc_ref, kp_ref, vc_ref, vp_ref, sl_ref, o_ref, l_ref, kk, vv, ob, lb):
        s = pl.program_id(1)
        kk[0:SUPER, :] = kp_ref[...]
        kk[SUPER:, :] = kc_ref[...]
        vv[0:SUPER, :] = vp_ref[...]
        vv[SUPER:, :] = vc_ref[...]
        head0 = lax.broadcasted_iota(jnp.int32, (QK_BLOCK, QK_BLOCK), 1) < HEAD_DIM

        for b, dil in enumerate(DILATIONS):
            bias, own_half = _attn_bias(sl_ref, dil)

            def unit(u, carry, b=b, dil=dil, bias=bias, own_half=own_half):
                start = _unit_start(u, dil)
                first_key = SUPER + start - QK_BLOCK * dil
                q2 = _stack_heads(q_ref[_rows(start, QK_BLOCK, dil), :].astype(BF16), head0)
                k2 = kk[_rows(first_key, KEYS, dil), :].astype(BF16)
                v2 = vv[_rows(first_key, KEYS, dil), :].astype(BF16)
                has_prev = jnp.logical_or(s > 0, start >= QK_BLOCK * dil)
                sc = jnp.where(jnp.logical_or(own_half, has_prev), _mm_nt(q2, k2) + bias, -jnp.inf)
                m = jnp.max(sc, axis=-1, keepdims=True)
                e = jnp.exp(sc - m)
                den = jnp.sum(e, axis=-1, keepdims=True)
                o2 = _mm(e.astype(BF16), v2) / den
                l2 = m + jnp.log(den)
                ob[b, _rows(start, QK_BLOCK, dil), :] = jnp.where(head0, o2[0:QK_BLOCK], o2[QK_BLOCK:])
                lb[b, _rows(start, QK_BLOCK, dil), :] = jnp.where(head0, l2[0:QK_BLOCK], l2[QK_BLOCK:])
                return carry

            lax.fori_loop(0, SUPER // QK_BLOCK, unit, 0, unroll=4)

        def merge(i, carry):
            rows = pl.ds(pl.multiple_of(i * 256, 256), 256)
            la, lb_, lc = lb[0, rows, :], lb[1, rows, :], lb[2, rows, :]
            mx = jnp.maximum(jnp.maximum(la, lb_), lc)
            wa, wb, wc = jnp.exp(la - mx), jnp.exp(lb_ - mx), jnp.exp(lc - mx)
            sw = wa + wb + wc
            o_ref[rows, :] = (wa * ob[0, rows, :] + wb * ob[1, rows, :] + wc * ob[2, rows, :]) / sw
            l_ref[rows, :] = mx + jnp.log(sw)
            return carry

        lax.fori_loop(0, SUPER // 256, merge, 0)

    cur = pl.BlockSpec((SUPER, QK_BLOCK), lambda p, s: (s, p))
    prev = pl.BlockSpec((SUPER, QK_BLOCK), lambda p, s: (jnp.maximum(s - 1, 0), p))
    return pl.pallas_call(
        body, name="attn_fwd", grid=(4, nsb),
        in_specs=[cur, cur, prev, cur, prev, pl.BlockSpec((1, 2, QK_BLOCK), lambda p, s: (p, 0, 0))],
        out_specs=[cur, cur],
        out_shape=[jax.ShapeDtypeStruct((t, ATTN_W), F32), jax.ShapeDtypeStruct((t, ATTN_W), F32)],
        scratch_shapes=[pltpu.VMEM((2 * SUPER, QK_BLOCK), F32), pltpu.VMEM((2 * SUPER, QK_BLOCK), F32),
                        pltpu.VMEM((3, SUPER, QK_BLOCK), F32), pltpu.VMEM((3, SUPER, QK_BLOCK), F32)],
        compiler_params=_cparams("parallel", "arbitrary"),
    )(q, k, k, v, v, slopes)


def _outproj_fwd(ya, yc, x, goc, goa, w_out, tm):
    t = x.shape[0]

    def body(ya_ref, yc_ref, x_ref, goc_ref, goa_ref, w_ref, x1_ref):
        _, ychat = _rms_stats(yc_ref[...])
        _, yahat = _rms_stats(ya_ref[...])
        acc = _mm((ychat * goc_ref[...]).astype(BF16), w_ref[0:CONV_W, :])
        acc += _mm((yahat * goa_ref[...]).astype(BF16), w_ref[CONV_W:, :])
        x1_ref[...] = x_ref[...] + acc

    def blk(c):
        return pl.BlockSpec((tm, c), lambda i: (i, 0))

    return pl.pallas_call(
        body, name="outproj_fwd", grid=(t // tm,),
        in_specs=[blk(ATTN_W), blk(CONV_W), blk(D_MODEL), _full((1, CONV_W)), _full((1, ATTN_W)),
                  _full((D_MODEL, D_MODEL))],
        out_specs=blk(D_MODEL),
        out_shape=jax.ShapeDtypeStruct((t, D_MODEL), F32),
        compiler_params=_cparams("parallel"),
    )(ya, yc, x, goc, goa, w_out)


def _ffn_fwd(x1, g_ffn, w_gate, w_up, w_down, fcw, fcb, tm):
    t = x1.shape[0]

    def body(x_ref, g_ref, wg_ref, wu_ref, wd_ref, cw_ref, cb_ref, gp_ref, up_ref, h_ref, x2_ref, carry_ref):
        @pl.when(pl.program_id(0) == 0)
        def _():
            carry_ref[...] = jnp.zeros_like(carry_ref)

        xv = x_ref[...]
        _, xhat = _rms_stats(xv)
        h = (xhat * g_ref[...]).astype(BF16)
        h_ref[...] = h
        gp = _mm(h, wg_ref[...])
        gp_ref[...] = gp.astype(BF16)
        gate, _, _ = _conv_fwd(gp, carry_ref[7:8, :], carry_ref[6:7, :], cw_ref, cb_ref)
        carry_ref[...] = gp[tm - 8:tm, :]
        up = _mm(h, wu_ref[...])
        up_ref[...] = up.astype(BF16)
        a = (gate * _sigmoid(gate) * up).astype(BF16)
        x2_ref[...] = xv + _mm(a, wd_ref[...])

    def blk(c):
        return pl.BlockSpec((tm, c), lambda i: (i, 0))

    return pl.pallas_call(
        body, name="ffn_fwd", grid=(t // tm,),
        in_specs=[blk(D_MODEL), _full((1, D_MODEL)), _full((D_MODEL, D_FF)), _full((D_MODEL, D_FF)),
                  _full((D_FF, D_MODEL)), _full((3, D_FF)), _full((1, D_FF))],
        out_specs=[blk(D_FF), blk(D_FF), blk(D_MODEL), blk(D_MODEL)],
        out_shape=[jax.ShapeDtypeStruct((t, D_FF), BF16), jax.ShapeDtypeStruct((t, D_FF), BF16),
                   jax.ShapeDtypeStruct((t, D_MODEL), BF16), jax.ShapeDtypeStruct((t, D_MODEL), F32)],
        scratch_shapes=[pltpu.VMEM((8, D_FF), F32)],
        compiler_params=_cparams("arbitrary"),
    )(x1, g_ffn, w_gate, w_up, w_down, fcw, fcb)


def _ple_fwd_bwd(x2, p, target, g_ple, w_pg, w_pp, tm):
    t = x2.shape[0]

    def body(x_ref, p_ref, t_ref, g_ref, wg_ref, wp_ref, dx_ref, dxb_ref, loss_ref, dwg_ref, dwp_ref, dg_ref):
        @pl.when(pl.program_id(0) == 0)
        def _():
            loss_ref[...] = jnp.zeros_like(loss_ref)
            dwg_ref[...] = jnp.zeros_like(dwg_ref)
            dwp_ref[...] = jnp.zeros_like(dwp_ref)
            dg_ref[...] = jnp.zeros_like(dg_ref)

        xv = x_ref[...]
        r, xhat = _rms_stats(xv)
        g = g_ref[...]
        h = (xhat * g).astype(BF16)
        pg = _sigmoid(_mm(h, wg_ref[...]))
        pb = p_ref[...].astype(BF16)
        pp = _mm(pb, wp_ref[...])
        err = xv + pg * pp - t_ref[...]
        loss_ref[...] += 0.5 * jnp.sum(jnp.mean(err * err, axis=-1, keepdims=True))
        dx3 = err * (1.0 / D_MODEL)
        d_pp = (dx3 * pg).astype(BF16)
        d_pre = (dx3 * pp * pg * (1.0 - pg)).astype(BF16)
        dwp_ref[...] += _mm_tn(pb, d_pp)
        dwg_ref[...] += _mm_tn(h, d_pre)
        dh = _mm_nt(d_pre, wg_ref[...])
        dg_ref[...] += jnp.sum(dh * xhat, axis=0, keepdims=True)
        dx2 = dx3 + _rms_bwd(dh, xhat, r, g)
        dx_ref[...] = dx2
        dxb_ref[...] = dx2.astype(BF16)

    def blk(c):
        return pl.BlockSpec((tm, c), lambda i: (i, 0))

    return pl.pallas_call(
        body, name="ple_fwd_bwd", grid=(t // tm,),
        in_specs=[blk(D_MODEL), blk(PLE_DIM), blk(D_MODEL), _full((1, D_MODEL)), _full((D_MODEL, D_MODEL)),
                  _full((PLE_DIM, D_MODEL))],
        out_specs=[blk(D_MODEL), blk(D_MODEL), _full((8, 128)), _full((D_MODEL, D_MODEL)),
                   _full((PLE_DIM, D_MODEL)), _full((1, D_MODEL))],
        out_shape=[jax.ShapeDtypeStruct((t, D_MODEL), F32), jax.ShapeDtypeStruct((t, D_MODEL), BF16),
                   jax.ShapeDtypeStruct((8, 128), F32),
                   jax.ShapeDtypeStruct((D_MODEL, D_MODEL), F32), jax.ShapeDtypeStruct((PLE_DIM, D_MODEL), F32),
                   jax.ShapeDtypeStruct((1, D_MODEL), F32)],
        compiler_params=_cparams("arbitrary"),
    )(x2, p, target, g_ple, w_pg, w_pp)


def _ffn_bwd(dx2, h2, gp, up, w_gate, w_up, w_down, fcw, fcb, tm):
    t = dx2.shape[0]
    nblk = t // tm
    fc = D_FF // FF_CHUNKS
    half = tm // 2

    def body(dx_ref, h_ref, gp_ref, gph_ref, up_ref, wg_ref, wu_ref, wd_ref, cw_ref, cb_ref,
             dh_ref, dwd_ref, dwu_ref, dwg_ref, dcw_ref, dcb_ref, carry_ref, a_scr, dup_scr, dgp_scr):
        i = pl.program_id(1)

        @pl.when(i == 0)
        def _():
            carry_ref[...] = jnp.zeros_like(carry_ref)
            dwd_ref[...] = jnp.zeros_like(dwd_ref)
            dwu_ref[...] = jnp.zeros_like(dwu_ref)
            dwg_ref[...] = jnp.zeros_like(dwg_ref)
            dcw_ref[...] = jnp.zeros_like(dcw_ref)
            dcb_ref[...] = jnp.zeros_like(dcb_ref)

        keep = (i < nblk - 1).astype(F32)
        later = carry_ref[...]
        for hf in (1, 0):
            rows = slice(hf * half, (hf + 1) * half)
            dxb = dx_ref[rows, :]
            gp_v = gp_ref[rows, :].astype(F32)
            if hf == 1:
                before = gp_ref[half - 16:half, :].astype(F32)
            else:
                before = gph_ref[...].astype(F32) * keep
            gate, gp1, gp2 = _conv_fwd(gp_v, before[15:16, :], before[14:15, :], cw_ref, cb_ref)
            s = _sigmoid(gate)
            silu = gate * s
            up_v = up_ref[rows, :].astype(F32)
            da = _mm_nt(dxb, wd_ref[...])
            a_scr[rows, :] = (silu * up_v).astype(BF16)
            d_up = (da * silu).astype(BF16)
            dup_scr[rows, :] = d_up
            d_gate = da * up_v * (s * (1.0 + gate * (1.0 - s)))
            d_gp = _conv_bwd_input(d_gate, later[0:1, :], later[1:2, :], cw_ref).astype(BF16)
            dgp_scr[rows, :] = d_gp
            later = d_gate[0:8, :]
            dcw_ref[0:1, :] += jnp.sum(d_gate * gp2, axis=0, keepdims=True)
            dcw_ref[1:2, :] += jnp.sum(d_gate * gp1, axis=0, keepdims=True)
            dcw_ref[2:3, :] += jnp.sum(d_gate * gp_v, axis=0, keepdims=True)
            dcb_ref[...] += jnp.sum(d_gate, axis=0, keepdims=True)
            dh_ref[rows, :] = (_mm_nt(d_gp, wg_ref[...]) + _mm_nt(d_up, wu_ref[...])).astype(BF16)
        carry_ref[...] = later
        dwd_ref[...] += _mm_tn(a_scr[...], dx_ref[...])
        dwu_ref[...] += _mm_tn(h_ref[...], dup_scr[...])
        dwg_ref[...] += _mm_tn(h_ref[...], dgp_scr[...])

    def rev(i):
        return nblk - 1 - i

    one = pl.Buffered(1)
    in_specs = [
        pl.BlockSpec((tm, D_MODEL), lambda j, i: (rev(i), 0)),
        pl.BlockSpec((tm, D_MODEL), lambda j, i: (rev(i), 0)),
        pl.BlockSpec((tm, fc), lambda j, i: (rev(i), j)),
        pl.BlockSpec((16, fc), lambda j, i: (jnp.maximum(rev(i) * (tm // 16) - 1, 0), j)),
        pl.BlockSpec((tm, fc), lambda j, i: (rev(i), j)),
        pl.BlockSpec((D_MODEL, fc), lambda j, i: (0, j), pipeline_mode=one),
        pl.BlockSpec((D_MODEL, fc), lambda j, i: (0, j), pipeline_mode=one),
        pl.BlockSpec((fc, D_MODEL), lambda j, i: (j, 0), pipeline_mode=one),
        pl.BlockSpec((3, fc), lambda j, i: (0, j)),
        pl.BlockSpec((1, fc), lambda j, i: (0, j)),
    ]
    out_specs = [
        pl.BlockSpec((None, tm, D_MODEL), lambda j, i: (j, rev(i), 0)),
        pl.BlockSpec((fc, D_MODEL), lambda j, i: (j, 0), pipeline_mode=one),
        pl.BlockSpec((D_MODEL, fc), lambda j, i: (0, j), pipeline_mode=one),
        pl.BlockSpec((D_MODEL, fc), lambda j, i: (0, j), pipeline_mode=one),
        pl.BlockSpec((3, fc), lambda j, i: (0, j)),
        pl.BlockSpec((1, fc), lambda j, i: (0, j)),
    ]
    return pl.pallas_call(
        body, name="ffn_bwd", grid=(FF_CHUNKS, nblk), in_specs=in_specs, out_specs=out_specs,
        out_shape=[jax.ShapeDtypeStruct((FF_CHUNKS, t, D_MODEL), BF16), jax.ShapeDtypeStruct((D_FF, D_MODEL), F32),
                   jax.ShapeDtypeStruct((D_MODEL, D_FF), F32), jax.ShapeDtypeStruct((D_MODEL, D_FF), F32),
                   jax.ShapeDtypeStruct((3, D_FF), F32), jax.ShapeDtypeStruct((1, D_FF), F32)],
        scratch_shapes=[pltpu.VMEM((8, fc), F32), pltpu.VMEM((tm, fc), BF16), pltpu.VMEM((tm, fc), BF16),
                        pltpu.VMEM((tm, fc), BF16)],
        compiler_params=_cparams("arbitrary", "arbitrary"),
    )(dx2, h2, gp, gp, up, w_gate, w_up, w_down, fcw, fcb)


def _outproj_bwd(dh2, dx2, x1, g_ffn, w_out, yc, ya, goc, goa, zconv, conv_w, conv_b, bd, tm):
    t = x1.shape[0]
    nblk = t // tm

    def body(dh_ref, dx2_ref, x1_ref, g_ref, w_ref, yc_ref, ya_ref, goc_ref, goa_ref, zc_ref, zch_ref, cw_ref, cb_ref,
             bd_ref, dx1_ref, dya_ref, dd_ref, dzc_ref, dw_ref, dg_ref, dgoc_ref, dgoa_ref, dcw_ref, dcb_ref,
             carry_ref):
        i = pl.program_id(0)

        @pl.when(i == 0)
        def _():
            carry_ref[...] = jnp.zeros_like(carry_ref)
            for ref in (dw_ref, dg_ref, dgoc_ref, dgoa_ref, dcw_ref, dcb_ref):
                ref[...] = jnp.zeros_like(ref)

        keep = (i < nblk - 1).astype(F32)
        dh2_v = dh_ref[0].astype(F32)
        for j in range(1, FF_CHUNKS):
            dh2_v = dh2_v + dh_ref[j].astype(F32)
        r, xhat = _rms_stats(x1_ref[...])
        dg_ref[...] += jnp.sum(dh2_v * xhat, axis=0, keepdims=True)
        dx1 = dx2_ref[...] + _rms_bwd(dh2_v, xhat, r, g_ref[...])
        dx1_ref[...] = dx1
        dx1b = dx1.astype(BF16)
        dy = _mm_nt(dx1b, w_ref[...])

        yc_v = yc_ref[...]
        rc, ychat = _rms_stats(yc_v)
        dw_ref[0:CONV_W, :] += _mm_tn((ychat * goc_ref[...]).astype(BF16), dx1b)
        dyc = dy[:, 0:CONV_W]
        dgoc_ref[...] += jnp.sum(dyc * ychat, axis=0, keepdims=True)
        d_yc = _rms_bwd(dyc, ychat, rc, goc_ref[...])

        ya_v = ya_ref[...]
        ra, yahat = _rms_stats(ya_v)
        dw_ref[CONV_W:, :] += _mm_tn((yahat * goa_ref[...]).astype(BF16), dx1b)
        dya = dy[:, CONV_W:]
        dgoa_ref[...] += jnp.sum(dya * yahat, axis=0, keepdims=True)
        d_ya = _rms_bwd(dya, yahat, ra, goa_ref[...])
        dya_ref[...] = d_ya
        dd_ref[...] = _seg_sum64(d_ya * ya_v, bd_ref)

        zb = zc_ref[:, 0:CONV_W]
        zc = zc_ref[:, CONV_W:2 * CONV_W]
        zx = zc_ref[:, 2 * CONV_W:3 * CONV_W]
        u = zc * zx
        uh = zch_ref[:, CONV_W:2 * CONV_W] * zch_ref[:, 2 * CONV_W:3 * CONV_W] * keep
        cv, u1, u2 = _conv_fwd(u, uh[7:8, :], uh[6:7, :], cw_ref, cb_ref)
        d_cv = d_yc * zb
        d_u = _conv_bwd_input(d_cv, carry_ref[0:1, :], carry_ref[1:2, :], cw_ref)
        carry_ref[...] = d_cv[0:8, :]
        dcw_ref[0:1, :] += jnp.sum(d_cv * u2, axis=0, keepdims=True)
        dcw_ref[1:2, :] += jnp.sum(d_cv * u1, axis=0, keepdims=True)
        dcw_ref[2:3, :] += jnp.sum(d_cv * u, axis=0, keepdims=True)
        dcb_ref[...] += jnp.sum(d_cv, axis=0, keepdims=True)
        dzc_ref[:, 0:CONV_W] = d_yc * cv
        dzc_ref[:, CONV_W:2 * CONV_W] = d_u * zx
        dzc_ref[:, 2 * CONV_W:3 * CONV_W] = d_u * zc

    def rev(i):
        return nblk - 1 - i

    def blk(c):
        return pl.BlockSpec((tm, c), lambda i: (rev(i), 0))

    in_specs = [
        pl.BlockSpec((FF_CHUNKS, tm, D_MODEL), lambda i: (0, rev(i), 0)),
        blk(D_MODEL), blk(D_MODEL), _full((1, D_MODEL)), _full((D_MODEL, D_MODEL)),
        blk(CONV_W), blk(ATTN_W), _full((1, CONV_W)), _full((1, ATTN_W)),
        blk(3 * CONV_W),
        pl.BlockSpec((8, 3 * CONV_W), lambda i: (jnp.maximum(rev(i) * (tm // 8) - 1, 0), 0)),
        _full((3, CONV_W)), _full((1, CONV_W)), _full((256, 256)),
    ]
    out_specs = [blk(D_MODEL), blk(ATTN_W), blk(ATTN_W), blk(3 * CONV_W), _full((D_MODEL, D_MODEL)),
                 _full((1, D_MODEL)), _full((1, CONV_W)), _full((1, ATTN_W)), _full((3, CONV_W)), _full((1, CONV_W))]
    return pl.pallas_call(
        body, name="outproj_bwd", grid=(nblk,), in_specs=in_specs, out_specs=out_specs,
        out_shape=[jax.ShapeDtypeStruct((t, D_MODEL), F32), jax.ShapeDtypeStruct((t, ATTN_W), F32),
                   jax.ShapeDtypeStruct((t, ATTN_W), F32), jax.ShapeDtypeStruct((t, 3 * CONV_W), F32),
                   jax.ShapeDtypeStruct((D_MODEL, D_MODEL), F32), jax.ShapeDtypeStruct((1, D_MODEL), F32),
                   jax.ShapeDtypeStruct((1, CONV_W), F32), jax.ShapeDtypeStruct((1, ATTN_W), F32),
                   jax.ShapeDtypeStruct((3, CONV_W), F32), jax.ShapeDtypeStruct((1, CONV_W), F32)],
        scratch_shapes=[pltpu.VMEM((8, CONV_W), F32)],
        compiler_params=_cparams("arbitrary"),
    )(dh2, dx2, x1, g_ffn, w_out, yc, ya, goc, goa, zconv, zconv, conv_w, conv_b, bd)


def _attn_bwd(q, k, v, dya, lse, dd, slopes):
    t = q.shape[0]
    nsb = t // SUPER

    def body(q_ref, kc_ref, kp_ref, vc_ref, vp_ref, dy_ref, l_ref, d_ref, sl_ref, dq_ref, dk_ref, dv_ref,
             kk, vv, dkacc, dvacc):
        s = pl.program_id(1)

        @pl.when(s == 0)
        def _():
            dkacc[...] = jnp.zeros_like(dkacc)
            dvacc[...] = jnp.zeros_like(dvacc)

        dkacc[0:SUPER, :] = dkacc[SUPER:, :]
        dvacc[0:SUPER, :] = dvacc[SUPER:, :]
        dkacc[SUPER:, :] = jnp.zeros((SUPER, QK_BLOCK), F32)
        dvacc[SUPER:, :] = jnp.zeros((SUPER, QK_BLOCK), F32)

        @pl.when(s < nsb)
        def _():
            kk[0:SUPER, :] = kp_ref[...]
            kk[SUPER:, :] = kc_ref[...]
            vv[0:SUPER, :] = vp_ref[...]
            vv[SUPER:, :] = vc_ref[...]
            head0 = lax.broadcasted_iota(jnp.int32, (QK_BLOCK, QK_BLOCK), 1) < HEAD_DIM

            for b, dil in enumerate(DILATIONS):
                bias, own_half = _attn_bias(sl_ref, dil)

                def unit(u, carry, b=b, dil=dil, bias=bias, own_half=own_half):
                    start = _unit_start(u, dil)
                    first_key = SUPER + start - QK_BLOCK * dil
                    qrows = _rows(start, QK_BLOCK, dil)
                    krows = _rows(first_key, KEYS, dil)
                    q2 = _stack_heads(q_ref[qrows, :].astype(BF16), head0)
                    dy2 = _stack_heads(dy_ref[qrows, :].astype(BF16), head0)
                    lv, dv_ = l_ref[qrows, :], d_ref[qrows, :]
                    l2 = jnp.concatenate([lv[:, 0:1], lv[:, HEAD_DIM:HEAD_DIM + 1]], axis=0)
                    d2 = jnp.concatenate([dv_[:, 0:1], dv_[:, HEAD_DIM:HEAD_DIM + 1]], axis=0)
                    k2 = kk[krows, :].astype(BF16)
                    v2 = vv[krows, :].astype(BF16)
                    has_prev = jnp.logical_or(s > 0, start >= QK_BLOCK * dil)
                    sc = jnp.where(jnp.logical_or(own_half, has_prev), _mm_nt(q2, k2) + bias, -jnp.inf)
                    prob = jnp.exp(sc - l2)
                    ds = (prob * (_mm_nt(dy2, v2) - d2)).astype(BF16)
                    dvacc[krows, :] += _mm_tn(prob.astype(BF16), dy2)
                    dkacc[krows, :] += _mm_tn(ds, q2)
                    dq2 = _mm(ds, k2)
                    dq = jnp.where(head0, dq2[0:QK_BLOCK], dq2[QK_BLOCK:]) * ATTN_SCALE
                    if b == 0:
                        dq_ref[qrows, :] = dq
                    else:
                        dq_ref[qrows, :] += dq
                    return carry

                lax.fori_loop(0, SUPER // QK_BLOCK, unit, 0, unroll=4)

        dk_ref[...] = dkacc[0:SUPER, :]
        dv_ref[...] = dvacc[0:SUPER, :]

    def cur_map(p, s):
        return (jnp.minimum(s, nsb - 1), p)

    def prev_map(p, s):
        return (jnp.clip(s - 1, 0, nsb - 1), p)

    cur = pl.BlockSpec((SUPER, QK_BLOCK), cur_map)
    prev = pl.BlockSpec((SUPER, QK_BLOCK), prev_map)
    return pl.pallas_call(
        body, name="attn_bwd", grid=(4, nsb + 1),
        in_specs=[cur, cur, prev, cur, prev, cur, cur, cur, pl.BlockSpec((1, 2, QK_BLOCK), lambda p, s: (p, 0, 0))],
        out_specs=[cur, prev, prev],
        out_shape=[jax.ShapeDtypeStruct((t, ATTN_W), F32)] * 3,
        scratch_shapes=[pltpu.VMEM((2 * SUPER, QK_BLOCK), F32)] * 4,
        compiler_params=_cparams("parallel", "arbitrary"),
    )(q, k, k, v, v, dya, lse, dd, slopes)


def _attn_bwd_per_branch_unused(q, k, v, dya, lse, dd, slopes, dil):
    t = q.shape[0]
    length = t // dil
    chunk = _attn_chunk(t, dil)
    nch = length // chunk
    nb = chunk // QK_BLOCK
    nblocks = length // QK_BLOCK
    view = (length, dil * ATTN_W)
    ext = chunk + QK_BLOCK

    def body(q_ref, dy_ref, l_ref, d_ref, k_ref, v_ref, qn_ref, dyn_ref, ln_ref, dn_ref, kh_ref, vh_ref, sl_ref,
             dq_ref, dk_ref, dv_ref, qbuf, dybuf, lbuf, dbuf, kbuf, vbuf, dkacc, dvacc):
        c = pl.program_id(2)
        qbuf[0:chunk, :] = q_ref[...]
        qbuf[chunk:, :] = qn_ref[...]
        dybuf[0:chunk, :] = dy_ref[...].astype(BF16)
        dybuf[chunk:, :] = dyn_ref[...].astype(BF16)
        lbuf[0:chunk, :] = l_ref[...]
        lbuf[chunk:, :] = ln_ref[...]
        dbuf[0:chunk, :] = d_ref[...]
        dbuf[chunk:, :] = dn_ref[...]
        kbuf[0:QK_BLOCK, :] = kh_ref[...]
        kbuf[QK_BLOCK:, :] = k_ref[...]
        vbuf[0:QK_BLOCK, :] = vh_ref[...]
        vbuf[QK_BLOCK:, :] = v_ref[...]
        valid_cur, valid_prev, dist_cur, dist_prev, head0 = _attn_masks(dil)

        def pair(qb, dyb, lv, dv_, kb, vb, valid, dist):
            dq = jnp.zeros((QK_BLOCK, QK_BLOCK), F32)
            dk = jnp.zeros((QK_BLOCK, QK_BLOCK), F32)
            dvv = jnp.zeros((QK_BLOCK, QK_BLOCK), F32)
            for hh in range(2):
                sl = sl_ref[0, hh:hh + 1, :]
                hm = head0 if hh == 0 else jnp.logical_not(head0)
                col = hh * HEAD_DIM
                qm = jnp.where(hm, qb, jnp.zeros_like(qb))
                dym = jnp.where(hm, dyb, jnp.zeros_like(dyb))
                s = jnp.where(valid, _mm_nt(qm, kb) - sl * dist, -jnp.inf)
                prob = jnp.exp(s - lv[:, col:col + 1])
                ds = (prob * (_mm_nt(dym, vb) - dv_[:, col:col + 1])).astype(BF16)
                dvv += _mm_tn(prob.astype(BF16), dym)
                dk += _mm_tn(ds, qm)
                dq += jnp.where(hm, _mm(ds, kb), 0.0)
            return dq, dk, dvv

        def blk(j, carry):
            off = pl.multiple_of(j * QK_BLOCK, QK_BLOCK)
            nxt = pl.multiple_of(off + QK_BLOCK, QK_BLOCK)
            qb = qbuf[pl.ds(off, QK_BLOCK), :]
            dyb = dybuf[pl.ds(off, QK_BLOCK), :]
            lv = lbuf[pl.ds(off, QK_BLOCK), :]
            dv_ = dbuf[pl.ds(off, QK_BLOCK), :]
            dq_c, dk_c, dv_c = pair(qb, dyb, lv, dv_, kbuf[pl.ds(nxt, QK_BLOCK), :], vbuf[pl.ds(nxt, QK_BLOCK), :],
                                    valid_cur, dist_cur)
            dkacc[pl.ds(nxt, QK_BLOCK), :] = dk_c
            dvacc[pl.ds(nxt, QK_BLOCK), :] = dv_c
            has_prev = jnp.logical_or(c > 0, j > 0)
            dq_p, dk_p, dv_p = pair(qb, dyb, lv, dv_, kbuf[pl.ds(off, QK_BLOCK), :], vbuf[pl.ds(off, QK_BLOCK), :],
                                    jnp.logical_and(valid_prev, has_prev), dist_prev)

            @pl.when(j > 0)
            def _():
                dkacc[pl.ds(off, QK_BLOCK), :] += dk_p
                dvacc[pl.ds(off, QK_BLOCK), :] += dv_p

            dq_ref[pl.ds(off, QK_BLOCK), :] = (dq_c + dq_p) * ATTN_SCALE
            return carry

        lax.fori_loop(0, nb, blk, 0)

        @pl.when(c < nch - 1)
        def _():
            _, dk_p, dv_p = pair(qbuf[chunk:, :], dybuf[chunk:, :], lbuf[chunk:, :], dbuf[chunk:, :],
                                 kbuf[chunk:, :], vbuf[chunk:, :], valid_prev, dist_prev)
            dkacc[chunk:, :] += dk_p
            dvacc[chunk:, :] += dv_p

        dk_ref[...] = dkacc[QK_BLOCK:, :]
        dv_ref[...] = dvacc[QK_BLOCK:, :]

    def cmap(p, r, c):
        return (c, r * 4 + p)

    def before(p, r, c):
        return (jnp.maximum(c * nb - 1, 0), r * 4 + p)

    def after(p, r, c):
        return (jnp.minimum((c + 1) * nb, nblocks - 1), r * 4 + p)

    main = pl.BlockSpec((chunk, QK_BLOCK), cmap)
    hb = pl.BlockSpec((QK_BLOCK, QK_BLOCK), before)
    ha = pl.BlockSpec((QK_BLOCK, QK_BLOCK), after)
    qv, kv, vv = q.reshape(view), k.reshape(view), v.reshape(view)
    dyv, lv, ddv = dya.reshape(view), lse.reshape(view), dd.reshape(view)
    outs = pl.pallas_call(
        body, name=f"attn_bwd_d{dil}", grid=(4, dil, nch),
        in_specs=[main] * 6 + [ha] * 4 + [hb] * 2 + [pl.BlockSpec((1, 2, QK_BLOCK), lambda p, r, c: (p, 0, 0))],
        out_specs=[main] * 3,
        out_shape=[jax.ShapeDtypeStruct(view, F32)] * 3,
        scratch_shapes=[pltpu.VMEM((ext, QK_BLOCK), BF16), pltpu.VMEM((ext, QK_BLOCK), BF16),
                        pltpu.VMEM((ext, QK_BLOCK), F32), pltpu.VMEM((ext, QK_BLOCK), F32),
                        pltpu.VMEM((ext, QK_BLOCK), BF16), pltpu.VMEM((ext, QK_BLOCK), BF16),
                        pltpu.VMEM((ext, QK_BLOCK), F32), pltpu.VMEM((ext, QK_BLOCK), F32)],
        compiler_params=_cparams("arbitrary", "arbitrary", "arbitrary"),
    )(qv, dyv, lv, ddv, kv, vv, qv, dyv, lv, ddv, kv, vv, slopes)
    return [o.reshape(t, ATTN_W) for o in outs]


def _inproj_bwd(dq, dk, dv, dzconv, zqk, x, dx1, g_mix, w_in, qg, kg, bd, tm):
    t = x.shape[0]

    def body(dq_ref, dk_ref, dv_ref, dzc_ref, zqk_ref, x_ref, dx1_ref, g_ref, w_ref, qg_ref,
             kg_ref, bd_ref, dx_ref, dw_ref, dg_ref, dqg_ref, dkg_ref):
        @pl.when(pl.program_id(0) == 0)
        def _():
            for ref in (dw_ref, dg_ref, dqg_ref, dkg_ref):
                ref[...] = jnp.zeros_like(ref)

        parts = [dzc_ref[...].astype(BF16)]
        for j, (dn_ref, gain_ref, dgain_ref) in enumerate(((dq_ref, qg_ref, dqg_ref), (dk_ref, kg_ref, dkg_ref))):
            dn = dn_ref[...]
            z = zqk_ref[:, j * ATTN_W:(j + 1) * ATTN_W]
            r = lax.rsqrt(_seg_sum64(z * z, bd_ref) * (1.0 / HEAD_DIM) + EPS)
            zhat = z * r
            dgain_ref[...] += jnp.sum(dn * zhat, axis=0, keepdims=True)
            gd = dn * gain_ref[...]
            parts.append((r * (gd - zhat * (_seg_sum64(gd * zhat, bd_ref) * (1.0 / HEAD_DIM)))).astype(BF16))
        parts.append(dv_ref[...].astype(BF16))
        dz = jnp.concatenate(parts, axis=1)

        r, xhat = _rms_stats(x_ref[...])
        g = g_ref[...]
        dw_ref[...] += _mm_tn((xhat * g).astype(BF16), dz)
        dh = _mm_nt(dz, w_ref[...])
        dg_ref[...] += jnp.sum(dh * xhat, axis=0, keepdims=True)
        dx_ref[...] = dx1_ref[...] + _rms_bwd(dh, xhat, r, g)

    def blk(c):
        return pl.BlockSpec((tm, c), lambda i: (i, 0))

    return pl.pallas_call(
        body, name="inproj_bwd", grid=(t // tm,),
        in_specs=[blk(ATTN_W)] * 3 + [blk(3 * CONV_W), blk(2 * ATTN_W), blk(D_MODEL), blk(D_MODEL), _full((1, D_MODEL)),
                                      _full((D_MODEL, IN_COLS)), _full((1, ATTN_W)), _full((1, ATTN_W)),
                                      _full((256, 256))],
        out_specs=[blk(D_MODEL), _full((D_MODEL, IN_COLS)), _full((1, D_MODEL)), _full((1, ATTN_W)),
                   _full((1, ATTN_W))],
        out_shape=[jax.ShapeDtypeStruct((t, D_MODEL), F32), jax.ShapeDtypeStruct((D_MODEL, IN_COLS), F32),
                   jax.ShapeDtypeStruct((1, D_MODEL), F32), jax.ShapeDtypeStruct((1, ATTN_W), F32),
                   jax.ShapeDtypeStruct((1, ATTN_W), F32)],
        compiler_params=_cparams("arbitrary"),
    )(dq, dk, dv, dzconv, zqk, x, dx1, g_mix, w_in, qg, kg, bd)


def _ordered_after(a, token):
    return a if token is None else a + token[0:1, 0:1].reshape((1,) * a.ndim)


def _local_step(x, p, target, w, tms, hooks=None):
    hooks = hooks or {}
    bd = jnp.kron(jnp.eye(4, dtype=F32), jnp.ones((HEAD_DIM, HEAD_DIM), F32)).astype(BF16)
    qg = jnp.tile(w["q_norm_g"], (1, 8))
    kg = jnp.tile(w["k_norm_g"], (1, 8))
    slopes = jnp.exp2(-jnp.arange(1, 9, dtype=F32))
    slopes = jnp.broadcast_to(slopes.reshape(4, 2, 1), (4, 2, QK_BLOCK))

    zconv, zqk, yc, q, k, v = _inproj_fwd(x, w["g_mix"], w["w_in"], w["conv_w"], w["conv_b"], qg, kg, bd, tms[0])
    ya, lse = _attn_fwd(q, k, v, slopes)
    if "late_weights" in hooks:
        w = {**w, **hooks["late_weights"](lse)}
    x1 = _outproj_fwd(ya, yc, x, w["g_out_conv"], w["g_out_attn"], w["w_out"], tms[0])
    gp, up, h2, x2 = _ffn_fwd(x1, w["g_ffn"], w["w_gate"], w["w_up"], w["w_down"], w["ffn_conv_w"], w["ffn_conv_b"], tms[1])
    dx2, dx2b, loss, dw_pg, dw_pp, dg_ple = _ple_fwd_bwd(x2, p, target, w["g_ple"], w["w_ple_gate"], w["w_ple_proj"], tms[0])
    dh2, dw_down, dw_up, dw_gate, dfcw, dfcb = _ffn_bwd(dx2b, h2, gp, up, w["w_gate"], w["w_up"], w["w_down"],
                                                        w["ffn_conv_w"], w["ffn_conv_b"], tms[0])
    token = None
    if "ffn_grads" in hooks:
        token = hooks["ffn_grads"]({"w_ple_gate": dw_pg, "w_ple_proj": dw_pp, "w_down": dw_down, "w_up": dw_up,
                                    "w_gate": dw_gate})
    dx1, dya, dd, dzconv, dw_out, dg_ffn, dgoc, dgoa, dcw, dcb = _outproj_bwd(
        dh2, dx2, x1, _ordered_after(w["g_ffn"], token), w["w_out"], yc, ya, w["g_out_conv"], w["g_out_attn"], zconv,
        w["conv_w"], w["conv_b"], bd, tms[1])
    token = hooks["outproj_done"](dx1) if "outproj_done" in hooks else None
    dq, dk, dv = _attn_bwd(q, k, v, dya, lse, dd, _ordered_after(slopes, token))
    dx, dw_in, dg_mix, dqg, dkg = _inproj_bwd(dq, dk, dv, dzconv, zqk, x, dx1, w["g_mix"], w["w_in"], qg, kg, bd,
                                              tms[1])
    grads = {
        "g_mix": dg_mix, "w_in": dw_in, "conv_w": dcw, "conv_b": dcb,
        "q_norm_g": dqg.reshape(8, HEAD_DIM).sum(0, keepdims=True),
        "k_norm_g": dkg.reshape(8, HEAD_DIM).sum(0, keepdims=True),
        "g_out_conv": dgoc, "g_out_attn": dgoa, "w_out": dw_out, "g_ffn": dg_ffn, "w_gate": dw_gate, "w_up": dw_up,
        "ffn_conv_w": dfcw, "ffn_conv_b": dfcb, "w_down": dw_down, "g_ple": dg_ple, "w_ple_gate": dw_pg,
        "w_ple_proj": dw_pp,
    }
    return loss[0, 0], dx, grads


ANY = pl.BlockSpec(memory_space=pl.ANY)
MESH = pl.DeviceIdType.MESH


def _all_gather(shards, name):
    n = len(shards)

    def body(*refs):
        ins, outs = refs[:n], refs[n:2 * n]
        send_sems, recv_sems, local_sems = refs[2 * n:]
        x, y, c = lax.axis_index("x"), lax.axis_index("y"), lax.axis_index("c")
        me, sibling = (x, y, c), (x, y, 1 - c)
        chips = [(1 - x, y), (x, 1 - y), (1 - x, 1 - y)]

        def slot(dev):
            return 4 * dev[0] + 2 * dev[1] + dev[2]

        def copy(b, k, block, to, src=None):
            dst = outs[b].at[slot(block)]
            return pltpu.make_async_remote_copy(
                src_ref=dst if src is None else src, dst_ref=dst, send_sem=send_sems.at[b, k],
                recv_sem=recv_sems.at[b, k], device_id=to, device_id_type=MESH)

        mine = [pltpu.make_async_copy(ins[b], outs[b].at[slot(me)], local_sems.at[b]) for b in range(n)]
        first, passed = [], []
        for b in range(n):
            mine[b].start()
            first.append(copy(b, 0, me, sibling, src=ins[b]))
            first += [copy(b, 1 + j, me, (*chip, c), src=ins[b]) for j, chip in enumerate(chips)]
        for cp in first:
            cp.start()
        for j, chip in enumerate(chips):
            for b in range(n):
                copy(b, 1 + j, (*chip, c), me).wait_recv()
                fwd = copy(b, 4 + j, (*chip, c), sibling)
                fwd.start()
                passed.append(fwd)
        for b in range(n):
            copy(b, 0, sibling, me).wait_recv()
            for j, chip in enumerate(chips):
                copy(b, 4 + j, (*chip, 1 - c), me).wait_recv()
        for cp in first + passed:
            cp.wait_send()
        for cp in mine:
            cp.wait()

    return pl.pallas_call(
        body, name=name,
        in_specs=[ANY] * n, out_specs=[ANY] * n,
        out_shape=[jax.ShapeDtypeStruct((N_DEV,) + s.shape, s.dtype) for s in shards],
        scratch_shapes=[pltpu.SemaphoreType.DMA((n, 7)), pltpu.SemaphoreType.DMA((n, 7)),
                        pltpu.SemaphoreType.DMA((n,))],
    )(*shards)


HBM = pl.BlockSpec(memory_space=pltpu.HBM)
SEM = pl.BlockSpec(memory_space=pltpu.SEMAPHORE)
EFFECT = pltpu.SideEffectType.DATAFLOW_SIDE_EFFECTING
FLIPS = ((0, 0, 1), (0, 1, 0), (0, 1, 1), (1, 0, 0), (1, 0, 1), (1, 1, 0), (1, 1, 1))


def _flip_peers():
    pos = (lax.axis_index("x"), lax.axis_index("y"), lax.axis_index("c"))
    return [tuple(1 - a if f else a for a, f in zip(pos, flip)) for flip in FLIPS]


def _hbm(a):
    return pltpu.with_memory_space_constraint(a, pltpu.HBM)


def _split_start(name, srcs, lands, plan, n_copies, after):
    n, m = len(srcs), len(lands)

    def body(*refs):
        send_sems, recv_sems, token = refs[n + m + 1], refs[n + m + 2], refs[-1]
        for i, (src, dst, peer) in enumerate(plan(refs[:n], refs[n:n + m])):
            pltpu.make_async_remote_copy(src_ref=src, dst_ref=dst, send_sem=send_sems.at[i], recv_sem=recv_sems.at[i],
                                         device_id=peer, device_id_type=MESH).start()
        token[...] = jnp.zeros_like(token)

    outs = pl.pallas_call(
        body, name=name + "_start",
        in_specs=[HBM] * (n + m) + [ANY],
        out_specs=[SEM, SEM] + [HBM] * (n + m) + [pl.BlockSpec(memory_space=pltpu.VMEM)],
        out_shape=[pltpu.SemaphoreType.DMA((n_copies,)), pltpu.SemaphoreType.DMA((n_copies,))]
        + [pltpu.HBM(a.shape, a.dtype) for a in list(srcs) + list(lands)] + [jax.ShapeDtypeStruct((8, 128), F32)],
        input_output_aliases={i: 2 + i for i in range(n + m)},
        compiler_params=pltpu.CompilerParams(has_side_effects=EFFECT),
    )(*[_hbm(a) for a in list(srcs) + list(lands)], after)
    return (outs[0], outs[1], outs[2:2 + n], outs[2 + n:2 + n + m]), outs[-1]


def _split_wait(name, started, plan, after):
    send_sems, recv_sems, srcs, lands = started
    n, m = len(srcs), len(lands)

    def body(*refs):
        send_ref, recv_ref = refs[n + m], refs[n + m + 1]
        for i, (src, dst, peer) in enumerate(plan(refs[:n], refs[n:n + m])):
            copy = pltpu.make_async_remote_copy(src_ref=src, dst_ref=dst, send_sem=send_ref.at[i],
                                                recv_sem=recv_ref.at[i], device_id=peer, device_id_type=MESH)
            copy.wait_send()
            copy.wait_recv()

    outs = pl.pallas_call(
        body, name=name + "_wait",
        in_specs=[HBM] * (n + m) + [SEM, SEM, ANY],
        out_specs=[HBM] * (n + m),
        out_shape=[pltpu.HBM(a.shape, a.dtype) for a in list(srcs) + list(lands)],
        input_output_aliases={i: i for i in range(n + m)},
        compiler_params=pltpu.CompilerParams(has_side_effects=EFFECT),
    )(*srcs, *lands, send_sems, recv_sems, after)
    return outs[:n], outs[n:]


def _gather_plan(srcs, lands):
    slot = 4 * lax.axis_index("x") + 2 * lax.axis_index("y") + lax.axis_index("c")
    return [(src, land.at[slot], peer) for src, land in zip(srcs, lands) for peer in _flip_peers()]


def _sibling_plan(srcs, lands):
    x, y, c = lax.axis_index("x"), lax.axis_index("y"), lax.axis_index("c")
    return [(src.at[k, 1 - c], land.at[k], (x, y, 1 - c)) for src, land in zip(srcs, lands) for k in range(N_CHIP)]


def _chip_plan(srcs, lands):
    x, y, c = lax.axis_index("x"), lax.axis_index("y"), lax.axis_index("c")
    return [(src.at[2 * cx + cy], land.at[2 * x + y], (cx, cy, c))
            for src, land in zip(srcs, lands) for cx, cy in ((1 - x, y), (x, 1 - y), (1 - x, 1 - y))]


def _row_tile(rows):
    for tr in range(min(rows, 512), 15, -16):
        if rows % tr == 0:
            return tr
    return rows


def _sibling_exchange(gs):
    n = len(gs)

    def body(*refs):
        g_refs, land_refs = refs[:n], refs[n:2 * n]
        send_sems, recv_sems = refs[2 * n:]
        x, y, c = lax.axis_index("x"), lax.axis_index("y"), lax.axis_index("c")
        copies = [pltpu.make_async_remote_copy(
            src_ref=g_refs[b].at[k, 1 - c], dst_ref=land_refs[b].at[k], send_sem=send_sems.at[b, k],
            recv_sem=recv_sems.at[b, k], device_id=(x, y, 1 - c), device_id_type=MESH)
            for b in range(n) for k in range(N_CHIP)]
        for cp in copies:
            cp.start()
        for cp in copies:
            cp.wait()

    return pl.pallas_call(
        body, name="rs_sibling_exchange", in_specs=[ANY] * n, out_specs=[ANY] * n,
        out_shape=[jax.ShapeDtypeStruct((N_CHIP,) + g.shape[2:], g.dtype) for g in gs],
        scratch_shapes=[pltpu.SemaphoreType.DMA((n, N_CHIP)), pltpu.SemaphoreType.DMA((n, N_CHIP))],
    )(*gs)


def _pair_sum(g, land, core, name):
    rows, cols = land.shape[1:]
    tr = _row_tile(rows)

    def body(c_ref, g_ref, l_ref, o_ref):
        o_ref[...] = (g_ref[...].astype(F32) + l_ref[...].astype(F32)).astype(o_ref.dtype)

    return pl.pallas_call(
        body, name=f"rs_pair_sum_{name}",
        grid_spec=pltpu.PrefetchScalarGridSpec(
            num_scalar_prefetch=1, grid=(N_CHIP, rows // tr),
            in_specs=[pl.BlockSpec((None, None, tr, cols), lambda k, i, c_ref: (k, c_ref[0], i, 0)),
                      pl.BlockSpec((None, tr, cols), lambda k, i, c_ref: (k, i, 0))],
            out_specs=pl.BlockSpec((None, tr, cols), lambda k, i, c_ref: (k, i, 0))),
        out_shape=jax.ShapeDtypeStruct(land.shape, land.dtype),
        compiler_params=_cparams("parallel", "parallel"),
    )(core, g, land)


def _chip_exchange(parts):
    n = len(parts)

    def body(*refs):
        p_refs, land_refs = refs[:n], refs[n:2 * n]
        send_sems, recv_sems, local_sems = refs[2 * n:]
        x, y, c = lax.axis_index("x"), lax.axis_index("y"), lax.axis_index("c")
        mine = 2 * x + y
        chips = [(1 - x, y), (x, 1 - y), (1 - x, 1 - y)]
        own = [pltpu.make_async_copy(p_refs[b].at[mine], land_refs[b].at[mine], local_sems.at[b]) for b in range(n)]
        for cp in own:
            cp.start()
        copies = [pltpu.make_async_remote_copy(
            src_ref=p_refs[b].at[2 * cx + cy], dst_ref=land_refs[b].at[mine], send_sem=send_sems.at[b, j],
            recv_sem=recv_sems.at[b, j], device_id=(cx, cy, c), device_id_type=MESH)
            for b in range(n) for j, (cx, cy) in enumerate(chips)]
        for cp in copies:
            cp.start()
        for b in range(n):
            for j, (cx, cy) in enumerate(chips):
                pltpu.make_async_remote_copy(
                    src_ref=p_refs[b].at[mine], dst_ref=land_refs[b].at[2 * cx + cy], send_sem=send_sems.at[b, j],
                    recv_sem=recv_sems.at[b, j], device_id=(cx, cy, c), device_id_type=MESH).wait_recv()
        for cp in copies:
            cp.wait_send()
        for cp in own:
            cp.wait()

    return pl.pallas_call(
        body, name="rs_chip_exchange", in_specs=[ANY] * n, out_specs=[ANY] * n,
        out_shape=[jax.ShapeDtypeStruct(p.shape, p.dtype) for p in parts],
        scratch_shapes=[pltpu.SemaphoreType.DMA((n, 3)), pltpu.SemaphoreType.DMA((n, 3)),
                        pltpu.SemaphoreType.DMA((n,))],
    )(*parts)


def _adamw(parts, w, m, v, name):
    k, rows, cols = parts.shape
    tr = _row_tile(rows)
    c1 = 1.0 / (1.0 - ADAM_B1 ** ADAM_STEP)
    c2 = 1.0 / (1.0 - ADAM_B2 ** ADAM_STEP)

    def body(p_ref, w_ref, m_ref, v_ref, g_ref, d_ref, nm_ref, nv_ref):
        g = p_ref[0].astype(F32)
        for j in range(1, k):
            g = g + p_ref[j].astype(F32)
        g_ref[...] = g
        nm = ADAM_B1 * m_ref[...] + (1.0 - ADAM_B1) * g
        nv = ADAM_B2 * v_ref[...] + (1.0 - ADAM_B2) * (g * g)
        nm_ref[...] = nm
        nv_ref[...] = nv
        d_ref[...] = -ADAM_LR * ((nm * c1) / (jnp.sqrt(nv * c2) + ADAM_EPS) + ADAM_WD * w_ref[...])

    blk = pl.BlockSpec((tr, cols), lambda i: (i, 0))
    return pl.pallas_call(
        body, name=name, grid=(rows // tr,),
        in_specs=[pl.BlockSpec((k, tr, cols), lambda i: (0, i, 0)), blk, blk, blk],
        out_specs=[blk] * 4, out_shape=[jax.ShapeDtypeStruct((rows, cols), F32)] * 4,
        compiler_params=_cparams("parallel"),
    )(parts, w, m, v)


COL_SHARDED = ("w_in", "w_gate", "w_up", "w_ple_proj")
REPLICATED = (("g_mix", 1024), ("conv_b", 512), ("q_norm_g", 64), ("k_norm_g", 64), ("g_out_conv", 512),
              ("g_out_attn", 512), ("g_ffn", 1024), ("ffn_conv_b", 2816), ("g_ple", 1024))
CONV_SHARDED = (("conv_w", CONV_W), ("ffn_conv_w", D_FF))


def _gathered_to_full(name, gathered):
    if name in COL_SHARDED:
        return gathered.transpose(1, 0, 2).reshape(gathered.shape[1], -1)
    return gathered.reshape(-1, gathered.shape[2])


def _full_to_stacked(name, grad, shard_shape):
    sr, sc = shard_shape
    if name in COL_SHARDED:
        a = grad.reshape(sr, N_DEV, sc).transpose(1, 0, 2)
    else:
        a = grad.reshape(N_DEV, sr, sc)
    return a.astype(BF16).reshape(N_CHIP, 2, sr, sc)


def _pad_rows(vec, rows):
    return jnp.pad(vec, (0, rows * 1024 - vec.shape[0])).reshape(rows, 1024)


def kernel(x, p, g_mix, w_in, conv_w, conv_b, q_norm_g, k_norm_g, g_out_conv, g_out_attn, w_out, g_ffn, w_gate, w_up, ffn_conv_w, ffn_conv_b, w_down, g_ple, w_ple_gate, w_ple_proj, loss_target, m_g_mix, m_w_in, m_conv_w, m_conv_b, m_q_norm_g, m_k_norm_g, m_g_out_conv, m_g_out_attn, m_w_out, m_g_ffn, m_w_gate, m_w_up, m_ffn_conv_w, m_ffn_conv_b, m_w_down, m_g_ple, m_w_ple_gate, m_w_ple_proj, v_g_mix, v_w_in, v_conv_w, v_conv_b, v_q_norm_g, v_k_norm_g, v_g_out_conv, v_g_out_attn, v_w_out, v_g_ffn, v_w_gate, v_w_up, v_ffn_conv_w, v_ffn_conv_b, v_w_down, v_g_ple, v_w_ple_gate, v_w_ple_proj):
    args = dict(locals())
    names = ["g_mix", "w_in", "conv_w", "conv_b", "q_norm_g", "k_norm_g", "g_out_conv", "g_out_attn", "w_out", "g_ffn",
             "w_gate", "w_up", "ffn_conv_w", "ffn_conv_b", "w_down", "g_ple", "w_ple_gate", "w_ple_proj"]
    big = [n for n, _ in BIG_ROWS]
    conv = [n for n, _ in CONV_SHARDED]
    wts = {n: (args[n][0] if n in big or n in conv else args[n]) for n in names}
    mom = {n: (args["m_" + n][0] if n in big or n in conv else args["m_" + n]) for n in names}
    var = {n: (args["v_" + n][0] if n in big or n in conv else args["v_" + n]) for n in names}
    shard_shapes = {n: wts[n].shape for n in big}
    dev = 4 * lax.axis_index("x") + 2 * lax.axis_index("y") + lax.axis_index("c")
    core = lax.axis_index("c").astype(jnp.int32).reshape(1)

    conv_local = _pad_rows(jnp.concatenate([wts[n].reshape(-1) for n in conv]), 8).reshape(8, 1024)
    late = [n for n in big if n != "w_in"]
    w_in_all, conv_all = _all_gather([wts["w_in"].astype(BF16), conv_local], "gather_weights")
    late_shards = [wts[n].astype(BF16) for n in late]
    gathering, token = _split_start("gather_late_weights", late_shards,
                                    [lax.empty((N_DEV,) + s.shape, BF16) for s in late_shards], _gather_plan,
                                    7 * len(late), w_in_all)
    full = dict(wts)
    full["w_in"] = _gathered_to_full("w_in", w_in_all)
    full["g_mix"] = _ordered_after(wts["g_mix"], token)
    flying = {}

    def late_weights(after):
        shards, lands = _split_wait("gather_late_weights", gathering, _gather_plan, after)
        return {n: _gathered_to_full(n, lax.dynamic_update_slice(land, shard[None], (dev, 0, 0)))
                for n, land, shard in zip(late, lands, shards)}

    early = ["w_ple_gate", "w_ple_proj", "w_down", "w_up", "w_gate"]

    def ffn_grads(g):
        stacked = [_full_to_stacked(n, g[n], shard_shapes[n]) for n in early]
        flying["sibling"], tok = _split_start("rs_sibling_early", stacked,
                                              [lax.empty((N_CHIP,) + s.shape[2:], BF16) for s in stacked],
                                              _sibling_plan, N_CHIP * len(early), g["w_down"])
        return tok

    def outproj_done(after):
        stacked, landed = _split_wait("rs_sibling_early", flying["sibling"], _sibling_plan, after)
        parts = [_pair_sum(g, l, core, n) for n, g, l in zip(early, stacked, landed)]
        flying["chip"], tok = _split_start("rs_chip_early", parts, [lax.empty(q.shape, BF16) for q in parts],
                                           _chip_plan, 3 * len(early), landed[0])
        return tok

    off = 0
    for n, width in CONV_SHARDED:
        sc = width // N_DEV
        a = conv_all.reshape(N_DEV, -1)[:, off:off + 3 * sc].reshape(N_DEV, 3, sc)
        full[n] = a.transpose(1, 0, 2).reshape(3, width)
        off += 3 * sc

    loss, dx, grads = _local_step(x[0], p[0, 0], loss_target[0], full, (512, 256),
                                  {"late_weights": late_weights, "ffn_grads": ffn_grads, "outproj_done": outproj_done})

    last = [n for n in big if n not in early]
    stacked = [_full_to_stacked(n, grads[n], shard_shapes[n]) for n in last]
    landed = _sibling_exchange(stacked)
    contributions = dict(zip(last, _chip_exchange([_pair_sum(g, l, core, n)
                                                   for n, g, l in zip(last, stacked, landed)])))
    chip = 2 * lax.axis_index("x") + lax.axis_index("y")
    parts, arrived = _split_wait("rs_chip_early", flying["chip"], _chip_plan, dx)
    for n, part, land in zip(early, parts, arrived):
        own = lax.dynamic_slice(part, (chip, 0, 0), (1,) + part.shape[1:])
        contributions[n] = lax.dynamic_update_slice(land, own, (chip, 0, 0))
    contributions = [contributions[n] for n in big]

    small = jnp.concatenate([grads[n].reshape(-1) for n, _ in REPLICATED] + [grads[n].reshape(-1) for n in conv]
                            + [loss.reshape(1)])
    (small_all,) = _all_gather([_pad_rows(small, SMALL_ROWS)], "gather_small_grads")

    big_out = {n: _adamw(c, wts[n], mom[n], var[n], f"adamw_{n}") for n, c in zip(big, contributions)}
    n_rep = sum(s for _, s in REPLICATED)
    conv_sizes = [3 * w_ // N_DEV for _, w_ in CONV_SHARDED]

    def small_state(src):
        flat = jnp.concatenate([src[n].reshape(-1) for n, _ in REPLICATED] + [src[n].reshape(-1) for n in conv])
        return _pad_rows(flat, 16)

    rep_all = small_all.reshape(N_DEV, -1)[:, :n_rep]
    conv_parts, off = [], n_rep
    for (n, width), size in zip(CONV_SHARDED, conv_sizes):
        sc = width // N_DEV
        a = small_all.reshape(N_DEV, -1)[:, off:off + 3 * width].reshape(N_DEV, 3, width)
        conv_parts.append(lax.dynamic_slice(a, (0, 0, dev * sc), (N_DEV, 3, sc)).reshape(N_DEV, size))
        off += 3 * width
    loss_total = jnp.sum(small_all.reshape(N_DEV, -1)[:, off])
    small_parts = jnp.concatenate([rep_all] + conv_parts, axis=1)
    small_parts = jnp.pad(small_parts, ((0, 0), (0, 16 * 1024 - small_parts.shape[1]))).reshape(N_DEV, 16, 1024)
    g_sm, d_sm, m_sm, v_sm = _adamw(small_parts, small_state(wts), small_state(mom), small_state(var), "adamw_small")

    def unpack(which, small_flat):
        out = {n: big_out[n][which] for n in big}
        flat, o = small_flat.reshape(-1), 0
        for n, s in list(REPLICATED) + [(n, sz) for (n, _), sz in zip(CONV_SHARDED, conv_sizes)]:
            out[n] = flat[o:o + s]
            o += s
        return [out[n].reshape(args[n].shape) for n in names]

    return (loss_total, dx[None], *unpack(0, g_sm), *unpack(1, d_sm), *unpack(2, m_sm), *unpack(3, v_sm))
```

```python
import functools

import jax
import jax.numpy as jnp
from jax import lax
from jax.experimental import pallas as pl
from jax.experimental.pallas import tpu as pltpu

F32 = jnp.float32
BF16 = jnp.bfloat16

D_MODEL = 1024
CONV_W = 512
ATTN_W = 512
HEAD_DIM = 64
D_FF = 2816
PLE_DIM = 256
IN_COLS = 3 * CONV_W + 3 * ATTN_W
EPS = 1e-6
QK_BLOCK = 128
DILATIONS = (1, 4, 16)
ATTN_SCALE = HEAD_DIM ** -0.5

ADAM_LR = 0.001
ADAM_B1 = 0.9
ADAM_B2 = 0.999
ADAM_EPS = 1e-08
ADAM_WD = 0.01
ADAM_STEP = 10

N_DEV = 8
N_CHIP = 4
V7X_VMEM_LIMIT = 56 * 1024 * 1024
FF_CHUNKS = 2

BIG_ROWS = (("w_in", 384), ("w_out", 128), ("w_gate", 352), ("w_up", 352), ("w_down", 352),
            ("w_ple_gate", 128), ("w_ple_proj", 32))
BIG_TOTAL = sum(r for _, r in BIG_ROWS)
SMALL_ROWS = 24


def _cparams(*sem):
    return pltpu.CompilerParams(dimension_semantics=sem, vmem_limit_bytes=V7X_VMEM_LIMIT)


def _mm(a, b):
    return jnp.dot(a, b, preferred_element_type=F32)


def _mm_nt(a, b):
    return lax.dot_general(a, b, (((1,), (1,)), ((), ())), preferred_element_type=F32)


def _mm_tn(a, b):
    return lax.dot_general(a, b, (((0,), (0,)), ((), ())), preferred_element_type=F32)


def _full(shape):
    nd = len(shape)
    return pl.BlockSpec(shape, lambda *_: (0,) * nd)


def _rms_stats(x):
    r = lax.rsqrt(jnp.mean(x * x, axis=-1, keepdims=True) + EPS)
    return r, x * r


def _rms_bwd(dy, xhat, r, g):
    gd = dy * g
    return r * (gd - xhat * jnp.mean(gd * xhat, axis=-1, keepdims=True))


def _seg_sum64(v, bd_ref):
    outs = []
    for c in range(0, v.shape[1], 256):
        vc = v[:, c:c + 256]
        hi = vc.astype(BF16)
        lo = (vc - hi.astype(F32)).astype(BF16)
        outs.append(_mm(hi, bd_ref[...]) + _mm(lo, bd_ref[...]))
    return outs[0] if len(outs) == 1 else jnp.concatenate(outs, axis=1)


def _shift_rows(u, k, edge_rows):
    out = pltpu.roll(u, k, 0)
    row = lax.broadcasted_iota(jnp.int32, (8, u.shape[1]), 0)
    head = out[0:8]
    for j in range(k):
        head = jnp.where(row == j, edge_rows[k - 1 - j], head)
    return jnp.concatenate([head, out[8:]], axis=0)


def _shift_rows_up(u, k, edge_rows):
    n = u.shape[0]
    out = pltpu.roll(u, n - k, 0)
    row = lax.broadcasted_iota(jnp.int32, (8, u.shape[1]), 0)
    tail = out[n - 8:n]
    for j in range(k):
        tail = jnp.where(row == 8 - k + j, edge_rows[j], tail)
    return jnp.concatenate([out[0:n - 8], tail], axis=0)


def _conv_fwd(u, c1, c2, w_ref, b_ref):
    u1 = _shift_rows(u, 1, (c1,))
    u2 = _shift_rows(u, 2, (c1, c2))
    y = u2 * w_ref[0:1, :] + u1 * w_ref[1:2, :] + u * w_ref[2:3, :] + b_ref[...]
    return y, u1, u2


def _conv_bwd_input(dy, n1row, n2row, w_ref):
    d1 = _shift_rows_up(dy, 1, (n1row,))
    d2 = _shift_rows_up(dy, 2, (n1row, n2row))
    return dy * w_ref[2:3, :] + d1 * w_ref[1:2, :] + d2 * w_ref[0:1, :]


def _sigmoid(x):
    return 1.0 / (1.0 + jnp.exp(-x))


def _inproj_fwd(x, g_mix, w_in, conv_w, conv_b, qg, kg, bd, tm):
    t = x.shape[0]

    def body(x_ref, g_ref, w_ref, cw_ref, cb_ref, qg_ref, kg_ref, bd_ref,
             zc_ref, zqk_ref, yc_ref, q_ref, k_ref, v_ref, carry_ref):
        @pl.when(pl.program_id(0) == 0)
        def _():
            carry_ref[...] = jnp.zeros_like(carry_ref)

        _, xhat = _rms_stats(x_ref[...])
        h = (xhat * g_ref[...]).astype(BF16)
        zconv = _mm(h, w_ref[:, 0:3 * CONV_W])
        zc_ref[...] = zconv
        u = zconv[:, CONV_W:2 * CONV_W] * zconv[:, 2 * CONV_W:3 * CONV_W]
        cv, _, _ = _conv_fwd(u, carry_ref[7:8, :], carry_ref[6:7, :], cw_ref, cb_ref)
        yc_ref[...] = zconv[:, 0:CONV_W] * cv
        carry_ref[...] = u[tm - 8:tm, :]

        zqk = _mm(h, w_ref[:, 3 * CONV_W:3 * CONV_W + 2 * ATTN_W])
        zqk_ref[...] = zqk
        for j, (gain_ref, out_ref, scale) in enumerate(((qg_ref, q_ref, ATTN_SCALE), (kg_ref, k_ref, 1.0))):
            z = zqk[:, j * ATTN_W:(j + 1) * ATTN_W]
            r = lax.rsqrt(_seg_sum64(z * z, bd_ref) * (1.0 / HEAD_DIM) + EPS)
            out_ref[...] = z * r * gain_ref[...] * scale
        v_ref[...] = _mm(h, w_ref[:, 3 * CONV_W + 2 * ATTN_W:IN_COLS])

    def blk(c):
        return pl.BlockSpec((tm, c), lambda i: (i, 0))

    return pl.pallas_call(
        body, name="inproj_fwd", grid=(t // tm,),
        in_specs=[blk(D_MODEL), _full((1, D_MODEL)), _full((D_MODEL, IN_COLS)), _full((3, CONV_W)),
                  _full((1, CONV_W)), _full((1, ATTN_W)), _full((1, ATTN_W)), _full((256, 256))],
        out_specs=[blk(3 * CONV_W), blk(2 * ATTN_W), blk(CONV_W), blk(ATTN_W), blk(ATTN_W), blk(ATTN_W)],
        out_shape=[jax.ShapeDtypeStruct((t, 3 * CONV_W), F32), jax.ShapeDtypeStruct((t, 2 * ATTN_W), F32),
                   jax.ShapeDtypeStruct((t, CONV_W), F32), jax.ShapeDtypeStruct((t, ATTN_W), F32),
                   jax.ShapeDtypeStruct((t, ATTN_W), F32), jax.ShapeDtypeStruct((t, ATTN_W), F32)],
        scratch_shapes=[pltpu.VMEM((8, CONV_W), F32)],
        compiler_params=_cparams("arbitrary"),
    )(x, g_mix, w_in, conv_w, conv_b, qg, kg, bd)


SUPER = 16 * QK_BLOCK
KEYS = 2 * QK_BLOCK


def _rows(start, size, dil):
    return pl.ds(start, size) if dil == 1 else pl.ds(start, size, stride=dil)


def _attn_bias(sl_ref, dil):
    qi = lax.broadcasted_iota(jnp.int32, (KEYS, KEYS), 0)
    kj = lax.broadcasted_iota(jnp.int32, (KEYS, KEYS), 1)
    step = jnp.bitwise_and(qi, QK_BLOCK - 1) + QK_BLOCK - kj
    slope = jnp.where(qi < QK_BLOCK, sl_ref[0, 0:1, 0:1], sl_ref[0, 1:2, 0:1])
    bias = jnp.where(jnp.logical_and(step >= 0, step <= QK_BLOCK), -slope * (step * dil).astype(F32), -jnp.inf)
    return bias, kj >= QK_BLOCK


def _unit_start(u, dil):
    if dil == 1:
        return pl.multiple_of(u * QK_BLOCK, QK_BLOCK)
    if dil == 4:
        return jnp.bitwise_and(u, 3) + (u // 4) * (4 * QK_BLOCK)
    return u


def _stack_heads(a, head0):
    zero = jnp.zeros_like(a)
    return jnp.concatenate([jnp.where(head0, a, zero), jnp.where(head0, zero, a)], axis=0)


def _attn_fwd(q, k, v, slopes):
    t = q.shape[0]
    nsb = t // SUPER

    def body(q_ref, kc_ref, kp_ref, vc_ref, vp_ref, sl_ref, o_ref, l_ref, kk, vv, ob, lb):
        s = pl.program_id(1)
        kk[0:SUPER, :] = kp_ref[...]
        kk[SUPER:, :] = kc_ref[...]
        vv[0:SUPER, :] = vp_ref[...]
        vv[SUPER:, :] = vc_ref[...]
        head0 = lax.broadcasted_iota(jnp.int32, (QK_BLOCK, QK_BLOCK), 1) < HEAD_DIM

        for b, dil in enumerate(DILATIONS):
            bias, own_half = _attn_bias(sl_ref, dil)

            def unit(u, carry, b=b, dil=dil, bias=bias, own_half=own_half):
                start = _unit_start(u, dil)
                first_key = SUPER + start - QK_BLOCK * dil
                q2 = _stack_heads(q_ref[_rows(start, QK_BLOCK, dil), :].astype(BF16), head0)
                k2 = kk[_rows(first_key, KEYS, dil), :].astype(BF16)
                v2 = vv[_rows(first_key, KEYS, dil), :].astype(BF16)
                has_prev = jnp.logical_or(s > 0, start >= QK_BLOCK * dil)
                sc = jnp.where(jnp.logical_or(own_half, has_prev), _mm_nt(q2, k2) + bias, -jnp.inf)
                m = jnp.max(sc, axis=-1, keepdims=True)
                e = jnp.exp(sc - m)
                den = jnp.sum(e, axis=-1, keepdims=True)
                o2 = _mm(e.astype(BF16), v2) / den
                l2 = m + jnp.log(den)
                ob[b, _rows(start, QK_BLOCK, dil), :] = jnp.where(head0, o2[0:QK_BLOCK], o2[QK_BLOCK:])
                lb[b, _rows(start, QK_BLOCK, dil), :] = jnp.where(head0, l2[0:QK_BLOCK], l2[QK_BLOCK:])
                return carry

            lax.fori_loop(0, SUPER // QK_BLOCK, unit, 0, unroll=16)

        def merge(i, carry):
            rows = pl.ds(pl.multiple_of(i * 256, 256), 256)
            la, lb_, lc = lb[0, rows, :], lb[1, rows, :], lb[2, rows, :]
            mx = jnp.maximum(jnp.maximum(la, lb_), lc)
            wa, wb, wc = jnp.exp(la - mx), jnp.exp(lb_ - mx), jnp.exp(lc - mx)
            sw = wa + wb + wc
            o_ref[rows, :] = (wa * ob[0, rows, :] + wb * ob[1, rows, :] + wc * ob[2, rows, :]) / sw
            l_ref[rows, :] = mx + jnp.log(sw)
            return carry

        lax.fori_loop(0, SUPER // 256, merge, 0)

    cur = pl.BlockSpec((SUPER, QK_BLOCK), lambda p, s: (s, p))
    prev = pl.BlockSpec((SUPER, QK_BLOCK), lambda p, s: (jnp.maximum(s - 1, 0), p))
    return pl.pallas_call(
        body, name="attn_fwd", grid=(4, nsb),
        in_specs=[cur, cur, prev, cur, prev, pl.BlockSpec((1, 2, QK_BLOCK), lambda p, s: (p, 0, 0))],
        out_specs=[cur, cur],
        out_shape=[jax.ShapeDtypeStruct((t, ATTN_W), F32), jax.ShapeDtypeStruct((t, ATTN_W), F32)],
        scratch_shapes=[pltpu.VMEM((2 * SUPER, QK_BLOCK), F32), pltpu.VMEM((2 * SUPER, QK_BLOCK), F32),
                        pltpu.VMEM((3, SUPER, QK_BLOCK), F32), pltpu.VMEM((3, SUPER, QK_BLOCK), F32)],
        compiler_params=_cparams("parallel", "arbitrary"),
    )(q, k, k, v, v, slopes)


def _outproj_fwd(ya, yc, x, goc, goa, w_out, tm):
    t = x.shape[0]

    def body(ya_ref, yc_ref, x_ref, goc_ref, goa_ref, w_ref, x1_ref):
        _, ychat = _rms_stats(yc_ref[...])
        _, yahat = _rms_stats(ya_ref[...])
        acc = _mm((ychat * goc_ref[...]).astype(BF16), w_ref[0:CONV_W, :])
        acc += _mm((yahat * goa_ref[...]).astype(BF16), w_ref[CONV_W:, :])
        x1_ref[...] = x_ref[...] + acc

    def blk(c):
        return pl.BlockSpec((tm, c), lambda i: (i, 0))

    return pl.pallas_call(
        body, name="outproj_fwd", grid=(t // tm,),
        in_specs=[blk(ATTN_W), blk(CONV_W), blk(D_MODEL), _full((1, CONV_W)), _full((1, ATTN_W)),
                  _full((D_MODEL, D_MODEL))],
        out_specs=blk(D_MODEL),
        out_shape=jax.ShapeDtypeStruct((t, D_MODEL), F32),
        compiler_params=_cparams("parallel"),
    )(ya, yc, x, goc, goa, w_out)


def _ffn_fwd(x1, g_ffn, w_gate, w_up, w_down, fcw, fcb, tm):
    t = x1.shape[0]

    def body(x_ref, g_ref, wg_ref, wu_ref, wd_ref, cw_ref, cb_ref, gp_ref, up_ref, h_ref, x2_ref, carry_ref):
        @pl.when(pl.program_id(0) == 0)
        def _():
            carry_ref[...] = jnp.zeros_like(carry_ref)

        xv = x_ref[...]
        _, xhat = _rms_stats(xv)
        h = (xhat * g_ref[...]).astype(BF16)
        h_ref[...] = h
        gp = _mm(h, wg_ref[...])
        gp_ref[...] = gp.astype(BF16)
        gate, _, _ = _conv_fwd(gp, carry_ref[7:8, :], carry_ref[6:7, :], cw_ref, cb_ref)
        carry_ref[...] = gp[tm - 8:tm, :]
        up = _mm(h, wu_ref[...])
        up_ref[...] = up.astype(BF16)
        a = (gate * _sigmoid(gate) * up).astype(BF16)
        x2_ref[...] = xv + _mm(a, wd_ref[...])

    def blk(c):
        return pl.BlockSpec((tm, c), lambda i: (i, 0))

    return pl.pallas_call(
        body, name="ffn_fwd", grid=(t // tm,),
        in_specs=[blk(D_MODEL), _full((1, D_MODEL)), _full((D_MODEL, D_FF)), _full((D_MODEL, D_FF)),
                  _full((D_FF, D_MODEL)), _full((3, D_FF)), _full((1, D_FF))],
        out_specs=[blk(D_FF), blk(D_FF), blk(D_MODEL), blk(D_MODEL)],
        out_shape=[jax.ShapeDtypeStruct((t, D_FF), BF16), jax.ShapeDtypeStruct((t, D_FF), BF16),
                   jax.ShapeDtypeStruct((t, D_MODEL), BF16), jax.ShapeDtypeStruct((t, D_MODEL), F32)],
        scratch_shapes=[pltpu.VMEM((8, D_FF), F32)],
        compiler_params=_cparams("arbitrary"),
    )(x1, g_ffn, w_gate, w_up, w_down, fcw, fcb)


def _ple_fwd_bwd(x2, p, target, g_ple, w_pg, w_pp, tm):
    t = x2.shape[0]

    def body(x_ref, p_ref, t_ref, g_ref, wg_ref, wp_ref, dx_ref, dxb_ref, loss_ref, dwg_ref, dwp_ref, dg_ref):
        @pl.when(pl.program_id(0) == 0)
        def _():
            loss_ref[...] = jnp.zeros_like(loss_ref)
            dwg_ref[...] = jnp.zeros_like(dwg_ref)
            dwp_ref[...] = jnp.zeros_like(dwp_ref)
            dg_ref[...] = jnp.zeros_like(dg_ref)

        xv = x_ref[...]
        r, xhat = _rms_stats(xv)
        g = g_ref[...]
        h = (xhat * g).astype(BF16)
        pg = _sigmoid(_mm(h, wg_ref[...]))
        pb = p_ref[...].astype(BF16)
        pp = _mm(pb, wp_ref[...])
        err = xv + pg * pp - t_ref[...]
        loss_ref[...] += 0.5 * jnp.sum(jnp.mean(err * err, axis=-1, keepdims=True))
        dx3 = err * (1.0 / D_MODEL)
        d_pp = (dx3 * pg).astype(BF16)
        d_pre = (dx3 * pp * pg * (1.0 - pg)).astype(BF16)
        dwp_ref[...] += _mm_tn(pb, d_pp)
        dwg_ref[...] += _mm_tn(h, d_pre)
        dh = _mm_nt(d_pre, wg_ref[...])
        dg_ref[...] += jnp.sum(dh * xhat, axis=0, keepdims=True)
        dx2 = dx3 + _rms_bwd(dh, xhat, r, g)
        dx_ref[...] = dx2
        dxb_ref[...] = dx2.astype(BF16)

    def blk(c):
        return pl.BlockSpec((tm, c), lambda i: (i, 0))

    return pl.pallas_call(
        body, name="ple_fwd_bwd", grid=(t // tm,),
        in_specs=[blk(D_MODEL), blk(PLE_DIM), blk(D_MODEL), _full((1, D_MODEL)), _full((D_MODEL, D_MODEL)),
                  _full((PLE_DIM, D_MODEL))],
        out_specs=[blk(D_MODEL), blk(D_MODEL), _full((8, 128)), _full((D_MODEL, D_MODEL)),
                   _full((PLE_DIM, D_MODEL)), _full((1, D_MODEL))],
        out_shape=[jax.ShapeDtypeStruct((t, D_MODEL), F32), jax.ShapeDtypeStruct((t, D_MODEL), BF16),
                   jax.ShapeDtypeStruct((8, 128), F32),
                   jax.ShapeDtypeStruct((D_MODEL, D_MODEL), F32), jax.ShapeDtypeStruct((PLE_DIM, D_MODEL), F32),
                   jax.ShapeDtypeStruct((1, D_MODEL), F32)],
        compiler_params=_cparams("arbitrary"),
    )(x2, p, target, g_ple, w_pg, w_pp)


def _ffn_bwd(dx2, h2, gp, up, w_gate, w_up, w_down, fcw, fcb, tm):
    t = dx2.shape[0]
    nblk = t // tm
    fc = D_FF // FF_CHUNKS
    half = tm // 2

    def body(dx_ref, h_ref, gp_ref, gph_ref, up_ref, wg_ref, wu_ref, wd_ref, cw_ref, cb_ref,
             dh_ref, dwd_ref, dwu_ref, dwg_ref, dcw_ref, dcb_ref, carry_ref, a_scr, dup_scr, dgp_scr):
        i = pl.program_id(1)

        @pl.when(i == 0)
        def _():
            carry_ref[...] = jnp.zeros_like(carry_ref)
            dwd_ref[...] = jnp.zeros_like(dwd_ref)
            dwu_ref[...] = jnp.zeros_like(dwu_ref)
            dwg_ref[...] = jnp.zeros_like(dwg_ref)
            dcw_ref[...] = jnp.zeros_like(dcw_ref)
            dcb_ref[...] = jnp.zeros_like(dcb_ref)

        keep = (i < nblk - 1).astype(F32)
        later = carry_ref[...]
        for hf in (1, 0):
            rows = slice(hf * half, (hf + 1) * half)
            dxb = dx_ref[rows, :]
            gp_v = gp_ref[rows, :].astype(F32)
            if hf == 1:
                before = gp_ref[half - 16:half, :].astype(F32)
            else:
                before = gph_ref[...].astype(F32) * keep
            gate, gp1, gp2 = _conv_fwd(gp_v, before[15:16, :], before[14:15, :], cw_ref, cb_ref)
            s = _sigmoid(gate)
            silu = gate * s
            up_v = up_ref[rows, :].astype(F32)
            da = _mm_nt(dxb, wd_ref[...])
            a_scr[rows, :] = (silu * up_v).astype(BF16)
            d_up = (da * silu).astype(BF16)
            dup_scr[rows, :] = d_up
            d_gate = da * up_v * (s * (1.0 + gate * (1.0 - s)))
            d_gp = _conv_bwd_input(d_gate, later[0:1, :], later[1:2, :], cw_ref).astype(BF16)
            dgp_scr[rows, :] = d_gp
            later = d_gate[0:8, :]
            dcw_ref[0:1, :] += jnp.sum(d_gate * gp2, axis=0, keepdims=True)
            dcw_ref[1:2, :] += jnp.sum(d_gate * gp1, axis=0, keepdims=True)
            dcw_ref[2:3, :] += jnp.sum(d_gate * gp_v, axis=0, keepdims=True)
            dcb_ref[...] += jnp.sum(d_gate, axis=0, keepdims=True)
            dh_ref[rows, :] = (_mm_nt(d_gp, wg_ref[...]) + _mm_nt(d_up, wu_ref[...])).astype(BF16)
        carry_ref[...] = later
        dwd_ref[...] += _mm_tn(a_scr[...], dx_ref[...])
        dwu_ref[...] += _mm_tn(h_ref[...], dup_scr[...])
        dwg_ref[...] += _mm_tn(h_ref[...], dgp_scr[...])

    def rev(i):
        return nblk - 1 - i

    one = pl.Buffered(1)
    in_specs = [
        pl.BlockSpec((tm, D_MODEL), lambda j, i: (rev(i), 0)),
        pl.BlockSpec((tm, D_MODEL), lambda j, i: (rev(i), 0)),
        pl.BlockSpec((tm, fc), lambda j, i: (rev(i), j)),
        pl.BlockSpec((16, fc), lambda j, i: (jnp.maximum(rev(i) * (tm // 16) - 1, 0), j)),
        pl.BlockSpec((tm, fc), lambda j, i: (rev(i), j)),
        pl.BlockSpec((D_MODEL, fc), lambda j, i: (0, j), pipeline_mode=one),
        pl.BlockSpec((D_MODEL, fc), lambda j, i: (0, j), pipeline_mode=one),
        pl.BlockSpec((fc, D_MODEL), lambda j, i: (j, 0), pipeline_mode=one),
        pl.BlockSpec((3, fc), lambda j, i: (0, j)),
        pl.BlockSpec((1, fc), lambda j, i: (0, j)),
    ]
    out_specs = [
        pl.BlockSpec((None, tm, D_MODEL), lambda j, i: (j, rev(i), 0)),
        pl.BlockSpec((fc, D_MODEL), lambda j, i: (j, 0), pipeline_mode=one),
        pl.BlockSpec((D_MODEL, fc), lambda j, i: (0, j), pipeline_mode=one),
        pl.BlockSpec((D_MODEL, fc), lambda j, i: (0, j), pipeline_mode=one),
        pl.BlockSpec((3, fc), lambda j, i: (0, j)),
        pl.BlockSpec((1, fc), lambda j, i: (0, j)),
    ]
    return pl.pallas_call(
        body, name="ffn_bwd", grid=(FF_CHUNKS, nblk), in_specs=in_specs, out_specs=out_specs,
        out_shape=[jax.ShapeDtypeStruct((FF_CHUNKS, t, D_MODEL), BF16), jax.ShapeDtypeStruct((D_FF, D_MODEL), F32),
                   jax.ShapeDtypeStruct((D_MODEL, D_FF), F32), jax.ShapeDtypeStruct((D_MODEL, D_FF), F32),
                   jax.ShapeDtypeStruct((3, D_FF), F32), jax.ShapeDtypeStruct((1, D_FF), F32)],
        scratch_shapes=[pltpu.VMEM((8, fc), F32), pltpu.VMEM((tm, fc), BF16), pltpu.VMEM((tm, fc), BF16),
                        pltpu.VMEM((tm, fc), BF16)],
        compiler_params=_cparams("arbitrary", "arbitrary"),
    )(dx2, h2, gp, gp, up, w_gate, w_up, w_down, fcw, fcb)


def _outproj_bwd(dh2, dx2, x1, g_ffn, w_out, yc, ya, goc, goa, zconv, conv_w, conv_b, bd, tm):
    t = x1.shape[0]
    nblk = t // tm

    def body(dh_ref, dx2_ref, x1_ref, g_ref, w_ref, yc_ref, ya_ref, goc_ref, goa_ref, zc_ref, zch_ref, cw_ref, cb_ref,
             bd_ref, dx1_ref, dya_ref, dd_ref, dzc_ref, dw_ref, dg_ref, dgoc_ref, dgoa_ref, dcw_ref, dcb_ref,
             carry_ref):
        i = pl.program_id(0)

        @pl.when(i == 0)
        def _():
            carry_ref[...] = jnp.zeros_like(carry_ref)
            for ref in (dw_ref, dg_ref, dgoc_ref, dgoa_ref, dcw_ref, dcb_ref):
                ref[...] = jnp.zeros_like(ref)

        keep = (i < nblk - 1).astype(F32)
        dh2_v = dh_ref[0].astype(F32)
        for j in range(1, FF_CHUNKS):
            dh2_v = dh2_v + dh_ref[j].astype(F32)
        r, xhat = _rms_stats(x1_ref[...])
        dg_ref[...] += jnp.sum(dh2_v * xhat, axis=0, keepdims=True)
        dx1 = dx2_ref[...] + _rms_bwd(dh2_v, xhat, r, g_ref[...])
        dx1_ref[...] = dx1
        dx1b = dx1.astype(BF16)
        dy = _mm_nt(dx1b, w_ref[...])

        yc_v = yc_ref[...]
        rc, ychat = _rms_stats(yc_v)
        dw_ref[0:CONV_W, :] += _mm_tn((ychat * goc_ref[...]).astype(BF16), dx1b)
        dyc = dy[:, 0:CONV_W]
        dgoc_ref[...] += jnp.sum(dyc * ychat, axis=0, keepdims=True)
        d_yc = _rms_bwd(dyc, ychat, rc, goc_ref[...])

        ya_v = ya_ref[...]
        ra, yahat = _rms_stats(ya_v)
        dw_ref[CONV_W:, :] += _mm_tn((yahat * goa_ref[...]).astype(BF16), dx1b)
        dya = dy[:, CONV_W:]
        dgoa_ref[...] += jnp.sum(dya * yahat, axis=0, keepdims=True)
        d_ya = _rms_bwd(dya, yahat, ra, goa_ref[...])
        dya_ref[...] = d_ya
        dd_ref[...] = _seg_sum64(d_ya * ya_v, bd_ref)

        zb = zc_ref[:, 0:CONV_W]
        zc = zc_ref[:, CONV_W:2 * CONV_W]
        zx = zc_ref[:, 2 * CONV_W:3 * CONV_W]
        u = zc * zx
        uh = zch_ref[:, CONV_W:2 * CONV_W] * zch_ref[:, 2 * CONV_W:3 * CONV_W] * keep
        cv, u1, u2 = _conv_fwd(u, uh[7:8, :], uh[6:7, :], cw_ref, cb_ref)
        d_cv = d_yc * zb
        d_u = _conv_bwd_input(d_cv, carry_ref[0:1, :], carry_ref[1:2, :], cw_ref)
        carry_ref[...] = d_cv[0:8, :]
        dcw_ref[0:1, :] += jnp.sum(d_cv * u2, axis=0, keepdims=True)
        dcw_ref[1:2, :] += jnp.sum(d_cv * u1, axis=0, keepdims=True)
        dcw_ref[2:3, :] += jnp.sum(d_cv * u, axis=0, keepdims=True)
        dcb_ref[...] += jnp.sum(d_cv, axis=0, keepdims=True)
        dzc_ref[:, 0:CONV_W] = d_yc * cv
        dzc_ref[:, CONV_W:2 * CONV_W] = d_u * zx
        dzc_ref[:, 2 * CONV_W:3 * CONV_W] = d_u * zc

    def rev(i):
        return nblk - 1 - i

    def blk(c):
        return pl.BlockSpec((tm, c), lambda i: (rev(i), 0))

    in_specs = [
        pl.BlockSpec((FF_CHUNKS, tm, D_MODEL), lambda i: (0, rev(i), 0)),
        blk(D_MODEL), blk(D_MODEL), _full((1, D_MODEL)), _full((D_MODEL, D_MODEL)),
        blk(CONV_W), blk(ATTN_W), _full((1, CONV_W)), _full((1, ATTN_W)),
        blk(3 * CONV_W),
        pl.BlockSpec((8, 3 * CONV_W), lambda i: (jnp.maximum(rev(i) * (tm // 8) - 1, 0), 0)),
        _full((3, CONV_W)), _full((1, CONV_W)), _full((256, 256)),
    ]
    out_specs = [blk(D_MODEL), blk(ATTN_W), blk(ATTN_W), blk(3 * CONV_W), _full((D_MODEL, D_MODEL)),
                 _full((1, D_MODEL)), _full((1, CONV_W)), _full((1, ATTN_W)), _full((3, CONV_W)), _full((1, CONV_W))]
    return pl.pallas_call(
        body, name="outproj_bwd", grid=(nblk,), in_specs=in_specs, out_specs=out_specs,
        out_shape=[jax.ShapeDtypeStruct((t, D_MODEL), F32), jax.ShapeDtypeStruct((t, ATTN_W), F32),
                   jax.ShapeDtypeStruct((t, ATTN_W), F32), jax.ShapeDtypeStruct((t, 3 * CONV_W), F32),
                   jax.ShapeDtypeStruct((D_MODEL, D_MODEL), F32), jax.ShapeDtypeStruct((1, D_MODEL), F32),
                   jax.ShapeDtypeStruct((1, CONV_W), F32), jax.ShapeDtypeStruct((1, ATTN_W), F32),
                   jax.ShapeDtypeStruct((3, CONV_W), F32), jax.ShapeDtypeStruct((1, CONV_W), F32)],
        scratch_shapes=[pltpu.VMEM((8, CONV_W), F32)],
        compiler_params=_cparams("arbitrary"),
    )(dh2, dx2, x1, g_ffn, w_out, yc, ya, goc, goa, zconv, zconv, conv_w, conv_b, bd)


def _attn_bwd(q, k, v, dya, lse, dd, slopes):
    t = q.shape[0]
    nsb = t // SUPER

    def body(q_ref, kc_ref, kp_ref, vc_ref, vp_ref, dy_ref, l_ref, d_ref, sl_ref, dq_ref, dk_ref, dv_ref,
             kk, vv, dkacc, dvacc):
        s = pl.program_id(1)

        @pl.when(s == 0)
        def _():
            dkacc[...] = jnp.zeros_like(dkacc)
            dvacc[...] = jnp.zeros_like(dvacc)

        dkacc[0:SUPER, :] = dkacc[SUPER:, :]
        dvacc[0:SUPER, :] = dvacc[SUPER:, :]
        dkacc[SUPER:, :] = jnp.zeros((SUPER, QK_BLOCK), F32)
        dvacc[SUPER:, :] = jnp.zeros((SUPER, QK_BLOCK), F32)

        @pl.when(s < nsb)
        def _():
            kk[0:SUPER, :] = kp_ref[...]
            kk[SUPER:, :] = kc_ref[...]
            vv[0:SUPER, :] = vp_ref[...]
            vv[SUPER:, :] = vc_ref[...]
            head0 = lax.broadcasted_iota(jnp.int32, (QK_BLOCK, QK_BLOCK), 1) < HEAD_DIM

            for b, dil in enumerate(DILATIONS):
                bias, own_half = _attn_bias(sl_ref, dil)

                def unit(u, carry, b=b, dil=dil, bias=bias, own_half=own_half):
                    start = _unit_start(u, dil)
                    first_key = SUPER + start - QK_BLOCK * dil
                    qrows = _rows(start, QK_BLOCK, dil)
                    krows = _rows(first_key, KEYS, dil)
                    q2 = _stack_heads(q_ref[qrows, :].astype(BF16), head0)
                    dy2 = _stack_heads(dy_ref[qrows, :].astype(BF16), head0)
                    lv, dv_ = l_ref[qrows, :], d_ref[qrows, :]
                    l2 = jnp.concatenate([lv[:, 0:1], lv[:, HEAD_DIM:HEAD_DIM + 1]], axis=0)
                    d2 = jnp.concatenate([dv_[:, 0:1], dv_[:, HEAD_DIM:HEAD_DIM + 1]], axis=0)
                    k2 = kk[krows, :].astype(BF16)
                    v2 = vv[krows, :].astype(BF16)
                    has_prev = jnp.logical_or(s > 0, start >= QK_BLOCK * dil)
                    sc = jnp.where(jnp.logical_or(own_half, has_prev), _mm_nt(q2, k2) + bias, -jnp.inf)
                    prob = jnp.exp(sc - l2)
                    ds = (prob * (_mm_nt(dy2, v2) - d2)).astype(BF16)
                    dvacc[krows, :] += _mm_tn(prob.astype(BF16), dy2)
                    dkacc[krows, :] += _mm_tn(ds, q2)
                    dq2 = _mm(ds, k2)
                    dq = jnp.where(head0, dq2[0:QK_BLOCK], dq2[QK_BLOCK:]) * ATTN_SCALE
                    if b == 0:
                        dq_ref[qrows, :] = dq
                    else:
                        dq_ref[qrows, :] += dq
                    return carry

                lax.fori_loop(0, SUPER // QK_BLOCK, unit, 0, unroll=8)

        dk_ref[...] = dkacc[0:SUPER, :]
        dv_ref[...] = dvacc[0:SUPER, :]

    def cur_map(p, s):
        return (jnp.minimum(s, nsb - 1), p)

    def prev_map(p, s):
        return (jnp.clip(s - 1, 0, nsb - 1), p)

    cur = pl.BlockSpec((SUPER, QK_BLOCK), cur_map)
    prev = pl.BlockSpec((SUPER, QK_BLOCK), prev_map)
    return pl.pallas_call(
        body, name="attn_bwd", grid=(4, nsb + 1),
        in_specs=[cur, cur, prev, cur, prev, cur, cur, cur, pl.BlockSpec((1, 2, QK_BLOCK), lambda p, s: (p, 0, 0))],
        out_specs=[cur, prev, prev],
        out_shape=[jax.ShapeDtypeStruct((t, ATTN_W), F32)] * 3,
        scratch_shapes=[pltpu.VMEM((2 * SUPER, QK_BLOCK), F32)] * 4,
        compiler_params=_cparams("parallel", "arbitrary"),
    )(q, k, k, v, v, dya, lse, dd, slopes)


def _attn_bwd_per_branch_unused(q, k, v, dya, lse, dd, slopes, dil):
    t = q.shape[0]
    length = t // dil
    chunk = _attn_chunk(t, dil)
    nch = length // chunk
    nb = chunk // QK_BLOCK
    nblocks = length // QK_BLOCK
    view = (length, dil * ATTN_W)
    ext = chunk + QK_BLOCK

    def body(q_ref, dy_ref, l_ref, d_ref, k_ref, v_ref, qn_ref, dyn_ref, ln_ref, dn_ref, kh_ref, vh_ref, sl_ref,
             dq_ref, dk_ref, dv_ref, qbuf, dybuf, lbuf, dbuf, kbuf, vbuf, dkacc, dvacc):
        c = pl.program_id(2)
        qbuf[0:chunk, :] = q_ref[...]
        qbuf[chunk:, :] = qn_ref[...]
        dybuf[0:chunk, :] = dy_ref[...].astype(BF16)
        dybuf[chunk:, :] = dyn_ref[...].astype(BF16)
        lbuf[0:chunk, :] = l_ref[...]
        lbuf[chunk:, :] = ln_ref[...]
        dbuf[0:chunk, :] = d_ref[...]
        dbuf[chunk:, :] = dn_ref[...]
        kbuf[0:QK_BLOCK, :] = kh_ref[...]
        kbuf[QK_BLOCK:, :] = k_ref[...]
        vbuf[0:QK_BLOCK, :] = vh_ref[...]
        vbuf[QK_BLOCK:, :] = v_ref[...]
        valid_cur, valid_prev, dist_cur, dist_prev, head0 = _attn_masks(dil)

        def pair(qb, dyb, lv, dv_, kb, vb, valid, dist):
            dq = jnp.zeros((QK_BLOCK, QK_BLOCK), F32)
            dk = jnp.zeros((QK_BLOCK, QK_BLOCK), F32)
            dvv = jnp.zeros((QK_BLOCK, QK_BLOCK), F32)
            for hh in range(2):
                sl = sl_ref[0, hh:hh + 1, :]
                hm = head0 if hh == 0 else jnp.logical_not(head0)
                col = hh * HEAD_DIM
                qm = jnp.where(hm, qb, jnp.zeros_like(qb))
                dym = jnp.where(hm, dyb, jnp.zeros_like(dyb))
                s = jnp.where(valid, _mm_nt(qm, kb) - sl * dist, -jnp.inf)
                prob = jnp.exp(s - lv[:, col:col + 1])
                ds = (prob * (_mm_nt(dym, vb) - dv_[:, col:col + 1])).astype(BF16)
                dvv += _mm_tn(prob.astype(BF16), dym)
                dk += _mm_tn(ds, qm)
                dq += jnp.where(hm, _mm(ds, kb), 0.0)
            return dq, dk, dvv

        def blk(j, carry):
            off = pl.multiple_of(j * QK_BLOCK, QK_BLOCK)
            nxt = pl.multiple_of(off + QK_BLOCK, QK_BLOCK)
            qb = qbuf[pl.ds(off, QK_BLOCK), :]
            dyb = dybuf[pl.ds(off, QK_BLOCK), :]
            lv = lbuf[pl.ds(off, QK_BLOCK), :]
            dv_ = dbuf[pl.ds(off, QK_BLOCK), :]
            dq_c, dk_c, dv_c = pair(qb, dyb, lv, dv_, kbuf[pl.ds(nxt, QK_BLOCK), :], vbuf[pl.ds(nxt, QK_BLOCK), :],
                                    valid_cur, dist_cur)
            dkacc[pl.ds(nxt, QK_BLOCK), :] = dk_c
            dvacc[pl.ds(nxt, QK_BLOCK), :] = dv_c
            has_prev = jnp.logical_or(c > 0, j > 0)
            dq_p, dk_p, dv_p = pair(qb, dyb, lv, dv_, kbuf[pl.ds(off, QK_BLOCK), :], vbuf[pl.ds(off, QK_BLOCK), :],
                                    jnp.logical_and(valid_prev, has_prev), dist_prev)

            @pl.when(j > 0)
            def _():
                dkacc[pl.ds(off, QK_BLOCK), :] += dk_p
                dvacc[pl.ds(off, QK_BLOCK), :] += dv_p

            dq_ref[pl.ds(off, QK_BLOCK), :] = (dq_c + dq_p) * ATTN_SCALE
            return carry

        lax.fori_loop(0, nb, blk, 0)

        @pl.when(c < nch - 1)
        def _():
            _, dk_p, dv_p = pair(qbuf[chunk:, :], dybuf[chunk:, :], lbuf[chunk:, :], dbuf[chunk:, :],
                                 kbuf[chunk:, :], vbuf[chunk:, :], valid_prev, dist_prev)
            dkacc[chunk:, :] += dk_p
            dvacc[chunk:, :] += dv_p

        dk_ref[...] = dkacc[QK_BLOCK:, :]
        dv_ref[...] = dvacc[QK_BLOCK:, :]

    def cmap(p, r, c):
        return (c, r * 4 + p)

    def before(p, r, c):
        return (jnp.maximum(c * nb - 1, 0), r * 4 + p)

    def after(p, r, c):
        return (jnp.minimum((c + 1) * nb, nblocks - 1), r * 4 + p)

    main = pl.BlockSpec((chunk, QK_BLOCK), cmap)
    hb = pl.BlockSpec((QK_BLOCK, QK_BLOCK), before)
    ha = pl.BlockSpec((QK_BLOCK, QK_BLOCK), after)
    qv, kv, vv = q.reshape(view), k.reshape(view), v.reshape(view)
    dyv, lv, ddv = dya.reshape(view), lse.reshape(view), dd.reshape(view)
    outs = pl.pallas_call(
        body, name=f"attn_bwd_d{dil}", grid=(4, dil, nch),
        in_specs=[main] * 6 + [ha] * 4 + [hb] * 2 + [pl.BlockSpec((1, 2, QK_BLOCK), lambda p, r, c: (p, 0, 0))],
        out_specs=[main] * 3,
        out_shape=[jax.ShapeDtypeStruct(view, F32)] * 3,
        scratch_shapes=[pltpu.VMEM((ext, QK_BLOCK), BF16), pltpu.VMEM((ext, QK_BLOCK), BF16),
                        pltpu.VMEM((ext, QK_BLOCK), F32), pltpu.VMEM((ext, QK_BLOCK), F32),
                        pltpu.VMEM((ext, QK_BLOCK), BF16), pltpu.VMEM((ext, QK_BLOCK), BF16),
                        pltpu.VMEM((ext, QK_BLOCK), F32), pltpu.VMEM((ext, QK_BLOCK), F32)],
        compiler_params=_cparams("arbitrary", "arbitrary", "arbitrary"),
    )(qv, dyv, lv, ddv, kv, vv, qv, dyv, lv, ddv, kv, vv, slopes)
    return [o.reshape(t, ATTN_W) for o in outs]


def _inproj_bwd(dq, dk, dv, dzconv, zqk, x, dx1, g_mix, w_in, qg, kg, bd, tm):
    t = x.shape[0]

    def body(dq_ref, dk_ref, dv_ref, dzc_ref, zqk_ref, x_ref, dx1_ref, g_ref, w_ref, qg_ref,
             kg_ref, bd_ref, dx_ref, dw_ref, dg_ref, dqg_ref, dkg_ref):
        @pl.when(pl.program_id(0) == 0)
        def _():
            for ref in (dw_ref, dg_ref, dqg_ref, dkg_ref):
                ref[...] = jnp.zeros_like(ref)

        parts = [dzc_ref[...].astype(BF16)]
        for j, (dn_ref, gain_ref, dgain_ref) in enumerate(((dq_ref, qg_ref, dqg_ref), (dk_ref, kg_ref, dkg_ref))):
            dn = dn_ref[...]
            z = zqk_ref[:, j * ATTN_W:(j + 1) * ATTN_W]
            r = lax.rsqrt(_seg_sum64(z * z, bd_ref) * (1.0 / HEAD_DIM) + EPS)
            zhat = z * r
            dgain_ref[...] += jnp.sum(dn * zhat, axis=0, keepdims=True)
            gd = dn * gain_ref[...]
            parts.append((r * (gd - zhat * (_seg_sum64(gd * zhat, bd_ref) * (1.0 / HEAD_DIM)))).astype(BF16))
        parts.append(dv_ref[...].astype(BF16))
        dz = jnp.concatenate(parts, axis=1)

        r, xhat = _rms_stats(x_ref[...])
        g = g_ref[...]
        dw_ref[...] += _mm_tn((xhat * g).astype(BF16), dz)
        dh = _mm_nt(dz, w_ref[...])
        dg_ref[...] += jnp.sum(dh * xhat, axis=0, keepdims=True)
        dx_ref[...] = dx1_ref[...] + _rms_bwd(dh, xhat, r, g)

    def blk(c):
        return pl.BlockSpec((tm, c), lambda i: (i, 0))

    return pl.pallas_call(
        body, name="inproj_bwd", grid=(t // tm,),
        in_specs=[blk(ATTN_W)] * 3 + [blk(3 * CONV_W), blk(2 * ATTN_W), blk(D_MODEL), blk(D_MODEL), _full((1, D_MODEL)),
                                      _full((D_MODEL, IN_COLS)), _full((1, ATTN_W)), _full((1, ATTN_W)),
                                      _full((256, 256))],
        out_specs=[blk(D_MODEL), _full((D_MODEL, IN_COLS)), _full((1, D_MODEL)), _full((1, ATTN_W)),
                   _full((1, ATTN_W))],
        out_shape=[jax.ShapeDtypeStruct((t, D_MODEL), F32), jax.ShapeDtypeStruct((D_MODEL, IN_COLS), F32),
                   jax.ShapeDtypeStruct((1, D_MODEL), F32), jax.ShapeDtypeStruct((1, ATTN_W), F32),
                   jax.ShapeDtypeStruct((1, ATTN_W), F32)],
        compiler_params=_cparams("arbitrary"),
    )(dq, dk, dv, dzconv, zqk, x, dx1, g_mix, w_in, qg, kg, bd)


def _ordered_after(a, token):
    return a if token is None else a + token[0:1, 0:1].reshape((1,) * a.ndim)


def _local_step(x, p, target, w, tms, hooks=None):
    hooks = hooks or {}
    bd = jnp.kron(jnp.eye(4, dtype=F32), jnp.ones((HEAD_DIM, HEAD_DIM), F32)).astype(BF16)
    qg = jnp.tile(w["q_norm_g"], (1, 8))
    kg = jnp.tile(w["k_norm_g"], (1, 8))
    slopes = jnp.exp2(-jnp.arange(1, 9, dtype=F32))
    slopes = jnp.broadcast_to(slopes.reshape(4, 2, 1), (4, 2, QK_BLOCK))

    zconv, zqk, yc, q, k, v = _inproj_fwd(x, w["g_mix"], w["w_in"], w["conv_w"], w["conv_b"], qg, kg, bd, tms[0])
    ya, lse = _attn_fwd(q, k, v, slopes)
    if "late_weights" in hooks:
        w = {**w, **hooks["late_weights"](lse)}
    x1 = _outproj_fwd(ya, yc, x, w["g_out_conv"], w["g_out_attn"], w["w_out"], tms[0])
    gp, up, h2, x2 = _ffn_fwd(x1, w["g_ffn"], w["w_gate"], w["w_up"], w["w_down"], w["ffn_conv_w"], w["ffn_conv_b"], tms[1])
    dx2, dx2b, loss, dw_pg, dw_pp, dg_ple = _ple_fwd_bwd(x2, p, target, w["g_ple"], w["w_ple_gate"], w["w_ple_proj"], tms[0])
    dh2, dw_down, dw_up, dw_gate, dfcw, dfcb = _ffn_bwd(dx2b, h2, gp, up, w["w_gate"], w["w_up"], w["w_down"],
                                                        w["ffn_conv_w"], w["ffn_conv_b"], tms[0])
    token = None
    if "ffn_grads" in hooks:
        token = hooks["ffn_grads"]({"w_ple_gate": dw_pg, "w_ple_proj": dw_pp, "w_down": dw_down, "w_up": dw_up,
                                    "w_gate": dw_gate})
    dx1, dya, dd, dzconv, dw_out, dg_ffn, dgoc, dgoa, dcw, dcb = _outproj_bwd(
        dh2, dx2, x1, _ordered_after(w["g_ffn"], token), w["w_out"], yc, ya, w["g_out_conv"], w["g_out_attn"], zconv,
        w["conv_w"], w["conv_b"], bd, tms[1])
    token = hooks["outproj_done"](dx1) if "outproj_done" in hooks else None
    dq, dk, dv = _attn_bwd(q, k, v, dya, lse, dd, _ordered_after(slopes, token))
    dx, dw_in, dg_mix, dqg, dkg = _inproj_bwd(dq, dk, dv, dzconv, zqk, x, dx1, w["g_mix"], w["w_in"], qg, kg, bd,
                                              tms[1])
    grads = {
        "g_mix": dg_mix, "w_in": dw_in, "conv_w": dcw, "conv_b": dcb,
        "q_norm_g": dqg.reshape(8, HEAD_DIM).sum(0, keepdims=True),
        "k_norm_g": dkg.reshape(8, HEAD_DIM).sum(0, keepdims=True),
        "g_out_conv": dgoc, "g_out_attn": dgoa, "w_out": dw_out, "g_ffn": dg_ffn, "w_gate": dw_gate, "w_up": dw_up,
        "ffn_conv_w": dfcw, "ffn_conv_b": dfcb, "w_down": dw_down, "g_ple": dg_ple, "w_ple_gate": dw_pg,
        "w_ple_proj": dw_pp,
    }
    return loss[0, 0], dx, grads


ANY = pl.BlockSpec(memory_space=pl.ANY)
MESH = pl.DeviceIdType.MESH


def _all_gather(shards, name):
    n = len(shards)

    def body(*refs):
        ins, outs = refs[:n], refs[n:2 * n]
        send_sems, recv_sems, local_sems = refs[2 * n:]
        x, y, c = lax.axis_index("x"), lax.axis_index("y"), lax.axis_index("c")
        me, sibling = (x, y, c), (x, y, 1 - c)
        chips = [(1 - x, y), (x, 1 - y), (1 - x, 1 - y)]

        def slot(dev):
            return 4 * dev[0] + 2 * dev[1] + dev[2]

        def copy(b, k, block, to, src=None):
            dst = outs[b].at[slot(block)]
            return pltpu.make_async_remote_copy(
                src_ref=dst if src is None else src, dst_ref=dst, send_sem=send_sems.at[b, k],
                recv_sem=recv_sems.at[b, k], device_id=to, device_id_type=MESH)

        mine = [pltpu.make_async_copy(ins[b], outs[b].at[slot(me)], local_sems.at[b]) for b in range(n)]
        first, passed = [], []
        for b in range(n):
            mine[b].start()
            first.append(copy(b, 0, me, sibling, src=ins[b]))
            first += [copy(b, 1 + j, me, (*chip, c), src=ins[b]) for j, chip in enumerate(chips)]
        for cp in first:
            cp.start()
        for j, chip in enumerate(chips):
            for b in range(n):
                copy(b, 1 + j, (*chip, c), me).wait_recv()
                fwd = copy(b, 4 + j, (*chip, c), sibling)
                fwd.start()
                passed.append(fwd)
        for b in range(n):
            copy(b, 0, sibling, me).wait_recv()
            for j, chip in enumerate(chips):
                copy(b, 4 + j, (*chip, 1 - c), me).wait_recv()
        for cp in first + passed:
            cp.wait_send()
        for cp in mine:
            cp.wait()

    return pl.pallas_call(
        body, name=name,
        in_specs=[ANY] * n, out_specs=[ANY] * n,
        out_shape=[jax.ShapeDtypeStruct((N_DEV,) + s.shape, s.dtype) for s in shards],
        scratch_shapes=[pltpu.SemaphoreType.DMA((n, 7)), pltpu.SemaphoreType.DMA((n, 7)),
                        pltpu.SemaphoreType.DMA((n,))],
    )(*shards)


HBM = pl.BlockSpec(memory_space=pltpu.HBM)
SEM = pl.BlockSpec(memory_space=pltpu.SEMAPHORE)
EFFECT = pltpu.SideEffectType.DATAFLOW_SIDE_EFFECTING
FLIPS = ((0, 0, 1), (0, 1, 0), (0, 1, 1), (1, 0, 0), (1, 0, 1), (1, 1, 0), (1, 1, 1))


def _flip_peers():
    pos = (lax.axis_index("x"), lax.axis_index("y"), lax.axis_index("c"))
    return [tuple(1 - a if f else a for a, f in zip(pos, flip)) for flip in FLIPS]


def _hbm(a):
    return pltpu.with_memory_space_constraint(a, pltpu.HBM)


def _split_start(name, srcs, lands, plan, n_copies, after):
    n, m = len(srcs), len(lands)

    def body(*refs):
        send_sems, recv_sems, token = refs[n + m + 1], refs[n + m + 2], refs[-1]
        for i, (src, dst, peer) in enumerate(plan(refs[:n], refs[n:n + m])):
            pltpu.make_async_remote_copy(src_ref=src, dst_ref=dst, send_sem=send_sems.at[i], recv_sem=recv_sems.at[i],
                                         device_id=peer, device_id_type=MESH).start()
        token[...] = jnp.zeros_like(token)

    outs = pl.pallas_call(
        body, name=name + "_start",
        in_specs=[HBM] * (n + m) + [ANY],
        out_specs=[SEM, SEM] + [HBM] * (n + m) + [pl.BlockSpec(memory_space=pltpu.VMEM)],
        out_shape=[pltpu.SemaphoreType.DMA((n_copies,)), pltpu.SemaphoreType.DMA((n_copies,))]
        + [pltpu.HBM(a.shape, a.dtype) for a in list(srcs) + list(lands)] + [jax.ShapeDtypeStruct((8, 128), F32)],
        input_output_aliases={i: 2 + i for i in range(n + m)},
        compiler_params=pltpu.CompilerParams(has_side_effects=EFFECT),
    )(*[_hbm(a) for a in list(srcs) + list(lands)], after)
    return (outs[0], outs[1], outs[2:2 + n], outs[2 + n:2 + n + m]), outs[-1]


def _split_wait(name, started, plan, after):
    send_sems, recv_sems, srcs, lands = started
    n, m = len(srcs), len(lands)

    def body(*refs):
        send_ref, recv_ref = refs[n + m], refs[n + m + 1]
        for i, (src, dst, peer) in enumerate(plan(refs[:n], refs[n:n + m])):
            copy = pltpu.make_async_remote_copy(src_ref=src, dst_ref=dst, send_sem=send_ref.at[i],
                                                recv_sem=recv_ref.at[i], device_id=peer, device_id_type=MESH)
            copy.wait_send()
            copy.wait_recv()

    outs = pl.pallas_call(
        body, name=name + "_wait",
        in_specs=[HBM] * (n + m) + [SEM, SEM, ANY],
        out_specs=[HBM] * (n + m),
        out_shape=[pltpu.HBM(a.shape, a.dtype) for a in list(srcs) + list(lands)],
        input_output_aliases={i: i for i in range(n + m)},
        compiler_params=pltpu.CompilerParams(has_side_effects=EFFECT),
    )(*srcs, *lands, send_sems, recv_sems, after)
    return outs[:n], outs[n:]


def _gather_plan(srcs, lands):
    slot = 4 * lax.axis_index("x") + 2 * lax.axis_index("y") + lax.axis_index("c")
    return [(src, land.at[slot], peer) for src, land in zip(srcs, lands) for peer in _flip_peers()]


def _sibling_plan(srcs, lands):
    x, y, c = lax.axis_index("x"), lax.axis_index("y"), lax.axis_index("c")
    return [(src.at[k, 1 - c], land.at[k], (x, y, 1 - c)) for src, land in zip(srcs, lands) for k in range(N_CHIP)]


def _chip_plan(srcs, lands):
    x, y, c = lax.axis_index("x"), lax.axis_index("y"), lax.axis_index("c")
    return [(src.at[2 * cx + cy], land.at[2 * x + y], (cx, cy, c))
            for src, land in zip(srcs, lands) for cx, cy in ((1 - x, y), (x, 1 - y), (1 - x, 1 - y))]


def _row_tile(rows):
    for tr in range(min(rows, 512), 15, -16):
        if rows % tr == 0:
            return tr
    return rows


def _sibling_exchange(gs):
    n = len(gs)

    def body(*refs):
        g_refs, land_refs = refs[:n], refs[n:2 * n]
        send_sems, recv_sems = refs[2 * n:]
        x, y, c = lax.axis_index("x"), lax.axis_index("y"), lax.axis_index("c")
        copies = [pltpu.make_async_remote_copy(
            src_ref=g_refs[b].at[k, 1 - c], dst_ref=land_refs[b].at[k], send_sem=send_sems.at[b, k],
            recv_sem=recv_sems.at[b, k], device_id=(x, y, 1 - c), device_id_type=MESH)
            for b in range(n) for k in range(N_CHIP)]
        for cp in copies:
            cp.start()
        for cp in copies:
            cp.wait()

    return pl.pallas_call(
        body, name="rs_sibling_exchange", in_specs=[ANY] * n, out_specs=[ANY] * n,
        out_shape=[jax.ShapeDtypeStruct((N_CHIP,) + g.shape[2:], g.dtype) for g in gs],
        scratch_shapes=[pltpu.SemaphoreType.DMA((n, N_CHIP)), pltpu.SemaphoreType.DMA((n, N_CHIP))],
    )(*gs)


def _pair_sum(g, land, core, name):
    rows, cols = land.shape[1:]
    tr = _row_tile(rows)

    def body(c_ref, g_ref, l_ref, o_ref):
        o_ref[...] = (g_ref[...].astype(F32) + l_ref[...].astype(F32)).astype(o_ref.dtype)

    return pl.pallas_call(
        body, name=f"rs_pair_sum_{name}",
        grid_spec=pltpu.PrefetchScalarGridSpec(
            num_scalar_prefetch=1, grid=(N_CHIP, rows // tr),
            in_specs=[pl.BlockSpec((None, None, tr, cols), lambda k, i, c_ref: (k, c_ref[0], i, 0)),
                      pl.BlockSpec((None, tr, cols), lambda k, i, c_ref: (k, i, 0))],
            out_specs=pl.BlockSpec((None, tr, cols), lambda k, i, c_ref: (k, i, 0))),
        out_shape=jax.ShapeDtypeStruct(land.shape, land.dtype),
        compiler_params=_cparams("parallel", "parallel"),
    )(core, g, land)


def _chip_exchange(parts):
    n = len(parts)

    def body(*refs):
        p_refs, land_refs = refs[:n], refs[n:2 * n]
        send_sems, recv_sems, local_sems = refs[2 * n:]
        x, y, c = lax.axis_index("x"), lax.axis_index("y"), lax.axis_index("c")
        mine = 2 * x + y
        chips = [(1 - x, y), (x, 1 - y), (1 - x, 1 - y)]
        own = [pltpu.make_async_copy(p_refs[b].at[mine], land_refs[b].at[mine], local_sems.at[b]) for b in range(n)]
        for cp in own:
            cp.start()
        copies = [pltpu.make_async_remote_copy(
            src_ref=p_refs[b].at[2 * cx + cy], dst_ref=land_refs[b].at[mine], send_sem=send_sems.at[b, j],
            recv_sem=recv_sems.at[b, j], device_id=(cx, cy, c), device_id_type=MESH)
            for b in range(n) for j, (cx, cy) in enumerate(chips)]
        for cp in copies:
            cp.start()
        for b in range(n):
            for j, (cx, cy) in enumerate(chips):
                pltpu.make_async_remote_copy(
                    src_ref=p_refs[b].at[mine], dst_ref=land_refs[b].at[2 * cx + cy], send_sem=send_sems.at[b, j],
                    recv_sem=recv_sems.at[b, j], device_id=(cx, cy, c), device_id_type=MESH).wait_recv()
        for cp in copies:
            cp.wait_send()
        for cp in own:
            cp.wait()

    return pl.pallas_call(
        body, name="rs_chip_exchange", in_specs=[ANY] * n, out_specs=[ANY] * n,
        out_shape=[jax.ShapeDtypeStruct(p.shape, p.dtype) for p in parts],
        scratch_shapes=[pltpu.SemaphoreType.DMA((n, 3)), pltpu.SemaphoreType.DMA((n, 3)),
                        pltpu.SemaphoreType.DMA((n,))],
    )(*parts)


def _adamw(parts, w, m, v, name):
    k, rows, cols = parts.shape
    tr = _row_tile(rows)
    c1 = 1.0 / (1.0 - ADAM_B1 ** ADAM_STEP)
    c2 = 1.0 / (1.0 - ADAM_B2 ** ADAM_STEP)

    def body(p_ref, w_ref, m_ref, v_ref, g_ref, d_ref, nm_ref, nv_ref):
        g = p_ref[0].astype(F32)
        for j in range(1, k):
            g = g + p_ref[j].astype(F32)
        g_ref[...] = g
        nm = ADAM_B1 * m_ref[...] + (1.0 - ADAM_B1) * g
        nv = ADAM_B2 * v_ref[...] + (1.0 - ADAM_B2) * (g * g)
        nm_ref[...] = nm
        nv_ref[...] = nv
        d_ref[...] = -ADAM_LR * ((nm * c1) / (jnp.sqrt(nv * c2) + ADAM_EPS) + ADAM_WD * w_ref[...])

    blk = pl.BlockSpec((tr, cols), lambda i: (i, 0))
    return pl.pallas_call(
        body, name=name, grid=(rows // tr,),
        in_specs=[pl.BlockSpec((k, tr, cols), lambda i: (0, i, 0)), blk, blk, blk],
        out_specs=[blk] * 4, out_shape=[jax.ShapeDtypeStruct((rows, cols), F32)] * 4,
        compiler_params=_cparams("parallel"),
    )(parts, w, m, v)


COL_SHARDED = ("w_in", "w_gate", "w_up", "w_ple_proj")
REPLICATED = (("g_mix", 1024), ("conv_b", 512), ("q_norm_g", 64), ("k_norm_g", 64), ("g_out_conv", 512),
              ("g_out_attn", 512), ("g_ffn", 1024), ("ffn_conv_b", 2816), ("g_ple", 1024))
CONV_SHARDED = (("conv_w", CONV_W), ("ffn_conv_w", D_FF))


def _gathered_to_full(name, gathered):
    if name in COL_SHARDED:
        return gathered.transpose(1, 0, 2).reshape(gathered.shape[1], -1)
    return gathered.reshape(-1, gathered.shape[2])


def _full_to_stacked(name, grad, shard_shape):
    sr, sc = shard_shape
    if name in COL_SHARDED:
        a = grad.reshape(sr, N_DEV, sc).transpose(1, 0, 2)
    else:
        a = grad.reshape(N_DEV, sr, sc)
    return a.astype(BF16).reshape(N_CHIP, 2, sr, sc)


def _pad_rows(vec, rows):
    return jnp.pad(vec, (0, rows * 1024 - vec.shape[0])).reshape(rows, 1024)


def kernel(x, p, g_mix, w_in, conv_w, conv_b, q_norm_g, k_norm_g, g_out_conv, g_out_attn, w_out, g_ffn, w_gate, w_up, ffn_conv_w, ffn_conv_b, w_down, g_ple, w_ple_gate, w_ple_proj, loss_target, m_g_mix, m_w_in, m_conv_w, m_conv_b, m_q_norm_g, m_k_norm_g, m_g_out_conv, m_g_out_attn, m_w_out, m_g_ffn, m_w_gate, m_w_up, m_ffn_conv_w, m_ffn_conv_b, m_w_down, m_g_ple, m_w_ple_gate, m_w_ple_proj, v_g_mix, v_w_in, v_conv_w, v_conv_b, v_q_norm_g, v_k_norm_g, v_g_out_conv, v_g_out_attn, v_w_out, v_g_ffn, v_w_gate, v_w_up, v_ffn_conv_w, v_ffn_conv_b, v_w_down, v_g_ple, v_w_ple_gate, v_w_ple_proj):
    args = dict(locals())
    names = ["g_mix", "w_in", "conv_w", "conv_b", "q_norm_g", "k_norm_g", "g_out_conv", "g_out_attn", "w_out", "g_ffn",
             "w_gate", "w_up", "ffn_conv_w", "ffn_conv_b", "w_down", "g_ple", "w_ple_gate", "w_ple_proj"]
    big = [n for n, _ in BIG_ROWS]
    conv = [n for n, _ in CONV_SHARDED]
    wts = {n: (args[n][0] if n in big or n in conv else args[n]) for n in names}
    mom = {n: (args["m_" + n][0] if n in big or n in conv else args["m_" + n]) for n in names}
    var = {n: (args["v_" + n][0] if n in big or n in conv else args["v_" + n]) for n in names}
    shard_shapes = {n: wts[n].shape for n in big}
    dev = 4 * lax.axis_index("x") + 2 * lax.axis_index("y") + lax.axis_index("c")
    core = lax.axis_index("c").astype(jnp.int32).reshape(1)

    conv_local = _pad_rows(jnp.concatenate([wts[n].reshape(-1) for n in conv]), 8).reshape(8, 1024)
    late = [n for n in big if n != "w_in"]
    w_in_all, conv_all = _all_gather([wts["w_in"].astype(BF16), conv_local], "gather_weights")
    late_shards = [wts[n].astype(BF16) for n in late]
    gathering, token = _split_start("gather_late_weights", late_shards,
                                    [lax.empty((N_DEV,) + s.shape, BF16) for s in late_shards], _gather_plan,
                                    7 * len(late), w_in_all)
    full = dict(wts)
    full["w_in"] = _gathered_to_full("w_in", w_in_all)
    full["g_mix"] = _ordered_after(wts["g_mix"], token)
    flying = {}

    def late_weights(after):
        shards, lands = _split_wait("gather_late_weights", gathering, _gather_plan, after)
        return {n: _gathered_to_full(n, lax.dynamic_update_slice(land, shard[None], (dev, 0, 0)))
                for n, land, shard in zip(late, lands, shards)}

    early = ["w_ple_gate", "w_ple_proj", "w_down", "w_up", "w_gate"]

    def ffn_grads(g):
        stacked = [_full_to_stacked(n, g[n], shard_shapes[n]) for n in early]
        flying["sibling"], tok = _split_start("rs_sibling_early", stacked,
                                              [lax.empty((N_CHIP,) + s.shape[2:], BF16) for s in stacked],
                                              _sibling_plan, N_CHIP * len(early), g["w_down"])
        return tok

    def outproj_done(after):
        stacked, landed = _split_wait("rs_sibling_early", flying["sibling"], _sibling_plan, after)
        parts = [_pair_sum(g, l, core, n) for n, g, l in zip(early, stacked, landed)]
        flying["chip"], tok = _split_start("rs_chip_early", parts, [lax.empty(q.shape, BF16) for q in parts],
                                           _chip_plan, 3 * len(early), landed[0])
        return tok

    off = 0
    for n, width in CONV_SHARDED:
        sc = width // N_DEV
        a = conv_all.reshape(N_DEV, -1)[:, off:off + 3 * sc].reshape(N_DEV, 3, sc)
        full[n] = a.transpose(1, 0, 2).reshape(3, width)
        off += 3 * sc

    loss, dx, grads = _local_step(x[0], p[0, 0], loss_target[0], full, (512, 256),
                                  {"late_weights": late_weights, "ffn_grads": ffn_grads, "outproj_done": outproj_done})

    last = [n for n in big if n not in early]
    stacked = [_full_to_stacked(n, grads[n], shard_shapes[n]) for n in last]
    landed = _sibling_exchange(stacked)
    contributions = dict(zip(last, _chip_exchange([_pair_sum(g, l, core, n)
                                                   for n, g, l in zip(last, stacked, landed)])))
    chip = 2 * lax.axis_index("x") + lax.axis_index("y")
    parts, arrived = _split_wait("rs_chip_early", flying["chip"], _chip_plan, dx)
    for n, part, land in zip(early, parts, arrived):
        own = lax.dynamic_slice(part, (chip, 0, 0), (1,) + part.shape[1:])
        contributions[n] = lax.dynamic_update_slice(land, own, (chip, 0, 0))
    contributions = [contributions[n] for n in big]

    small = jnp.concatenate([grads[n].reshape(-1) for n, _ in REPLICATED] + [grads[n].reshape(-1) for n in conv]
                            + [loss.reshape(1)])
    (small_all,) = _all_gather([_pad_rows(small, SMALL_ROWS)], "gather_small_grads")

    big_out = {n: _adamw(c, wts[n], mom[n], var[n], f"adamw_{n}") for n, c in zip(big, contributions)}
    n_rep = sum(s for _, s in REPLICATED)
    conv_sizes = [3 * w_ // N_DEV for _, w_ in CONV_SHARDED]

    def small_state(src):
        flat = jnp.concatenate([src[n].reshape(-1) for n, _ in REPLICATED] + [src[n].reshape(-1) for n in conv])
        return _pad_rows(flat, 16)

    rep_all = small_all.reshape(N_DEV, -1)[:, :n_rep]
    conv_parts, off = [], n_rep
    for (n, width), size in zip(CONV_SHARDED, conv_sizes):
        sc = width // N_DEV
        a = small_all.reshape(N_DEV, -1)[:, off:off + 3 * width].reshape(N_DEV, 3, width)
        conv_parts.append(lax.dynamic_slice(a, (0, 0, dev * sc), (N_DEV, 3, sc)).reshape(N_DEV, size))
        off += 3 * width
    loss_total = jnp.sum(small_all.reshape(N_DEV, -1)[:, off])
    small_parts = jnp.concatenate([rep_all] + conv_parts, axis=1)
    small_parts = jnp.pad(small_parts, ((0, 0), (0, 16 * 1024 - small_parts.shape[1]))).reshape(N_DEV, 16, 1024)
    g_sm, d_sm, m_sm, v_sm = _adamw(small_parts, small_state(wts), small_state(mom), small_state(var), "adamw_small")

    def unpack(which, small_flat):
        out = {n: big_out[n][which] for n in big}
        flat, o = small_flat.reshape(-1), 0
        for n, s in list(REPLICATED) + [(n, sz) for (n, _), sz in zip(CONV_SHARDED, conv_sizes)]:
            out[n] = flat[o:o + s]
            o += s
        return [out[n].reshape(args[n].shape) for n in names]

    return (loss_total, dx[None], *unpack(0, g_sm), *unpack(1, d_sm), *unpack(2, m_sm), *unpack(3, v_sm))
```

```python
import functools

import jax
import jax.numpy as jnp
from jax import lax
from jax.experimental import pallas as pl
from jax.experimental.pallas import tpu as pltpu

F32 = jnp.float32
BF16 = jnp.bfloat16

D_MODEL = 1024
CONV_W = 512
ATTN_W = 512
HEAD_DIM = 64
D_FF = 2816
PLE_DIM = 256
IN_COLS = 3 * CONV_W + 3 * ATTN_W
EPS = 1e-6
QK_BLOCK = 128
DILATIONS = (1, 4, 16)
ATTN_SCALE = HEAD_DIM ** -0.5

ADAM_LR = 0.001
ADAM_B1 = 0.9
ADAM_B2 = 0.999
ADAM_EPS = 1e-08
ADAM_WD = 0.01
ADAM_STEP = 10

N_DEV = 8
N_CHIP = 4
V7X_VMEM_LIMIT = 56 * 1024 * 1024
FF_CHUNKS = 2

BIG_ROWS = (("w_in", 384), ("w_out", 128), ("w_gate", 352), ("w_up", 352), ("w_down", 352),
            ("w_ple_gate", 128), ("w_ple_proj", 32))
BIG_TOTAL = sum(r for _, r in BIG_ROWS)
SMALL_ROWS = 24


def _cparams(*sem):
    return pltpu.CompilerParams(dimension_semantics=sem, vmem_limit_bytes=V7X_VMEM_LIMIT)


def _mm(a, b):
    return jnp.dot(a, b, preferred_element_type=F32)


def _mm_nt(a, b):
    return lax.dot_general(a, b, (((1,), (1,)), ((), ())), preferred_element_type=F32)


def _mm_tn(a, b):
    return lax.dot_general(a, b, (((0,), (0,)), ((), ())), preferred_element_type=F32)


def _full(shape):
    nd = len(shape)
    return pl.BlockSpec(shape, lambda *_: (0,) * nd)


def _rms_stats(x):
    r = lax.rsqrt(jnp.mean(x * x, axis=-1, keepdims=True) + EPS)
    return r, x * r


def _rms_bwd(dy, xhat, r, g):
    gd = dy * g
    return r * (gd - xhat * jnp.mean(gd * xhat, axis=-1, keepdims=True))


def _seg_sum64(v, bd_ref):
    outs = []
    for c in range(0, v.shape[1], 256):
        vc = v[:, c:c + 256]
        hi = vc.astype(BF16)
        lo = (vc - hi.astype(F32)).astype(BF16)
        outs.append(_mm(hi, bd_ref[...]) + _mm(lo, bd_ref[...]))
    return outs[0] if len(outs) == 1 else jnp.concatenate(outs, axis=1)


def _shift_rows(u, k, edge_rows):
    out = pltpu.roll(u, k, 0)
    row = lax.broadcasted_iota(jnp.int32, (8, u.shape[1]), 0)
    head = out[0:8]
    for j in range(k):
        head = jnp.where(row == j, edge_rows[k - 1 - j], head)
    return jnp.concatenate([head, out[8:]], axis=0)


def _shift_rows_up(u, k, edge_rows):
    n = u.shape[0]
    out = pltpu.roll(u, n - k, 0)
    row = lax.broadcasted_iota(jnp.int32, (8, u.shape[1]), 0)
    tail = out[n - 8:n]
    for j in range(k):
        tail = jnp.where(row == 8 - k + j, edge_rows[j], tail)
    return jnp.concatenate([out[0:n - 8], tail], axis=0)


def _conv_fwd(u, c1, c2, w_ref, b_ref):
    u1 = _shift_rows(u, 1, (c1,))
    u2 = _shift_rows(u, 2, (c1, c2))
    y = u2 * w_ref[0:1, :] + u1 * w_ref[1:2, :] + u * w_ref[2:3, :] + b_ref[...]
    return y, u1, u2


def _conv_bwd_input(dy, n1row, n2row, w_ref):
    d1 = _shift_rows_up(dy, 1, (n1row,))
    d2 = _shift_rows_up(dy, 2, (n1row, n2row))
    return dy * w_ref[2:3, :] + d1 * w_ref[1:2, :] + d2 * w_ref[0:1, :]


def _sigmoid(x):
    return 1.0 / (1.0 + jnp.exp(-x))


def _inproj_fwd(x, g_mix, w_in, conv_w, conv_b, qg, kg, bd, tm):
    t = x.shape[0]

    def body(x_ref, g_ref, w_ref, cw_ref, cb_ref, qg_ref, kg_ref, bd_ref,
             zc_ref, zqk_ref, yc_ref, q_ref, k_ref, v_ref, carry_ref):
        @pl.when(pl.program_id(0) == 0)
        def _():
            carry_ref[...] = jnp.zeros_like(carry_ref)

        _, xhat = _rms_stats(x_ref[...])
        h = (xhat * g_ref[...]).astype(BF16)
        zconv = _mm(h, w_ref[:, 0:3 * CONV_W])
        zc_ref[...] = zconv
        u = zconv[:, CONV_W:2 * CONV_W] * zconv[:, 2 * CONV_W:3 * CONV_W]
        cv, _, _ = _conv_fwd(u, carry_ref[7:8, :], carry_ref[6:7, :], cw_ref, cb_ref)
        yc_ref[...] = zconv[:, 0:CONV_W] * cv
        carry_ref[...] = u[tm - 8:tm, :]

        zqk = _mm(h, w_ref[:, 3 * CONV_W:3 * CONV_W + 2 * ATTN_W])
        zqk_ref[...] = zqk
        for j, (gain_ref, out_ref, scale) in enumerate(((qg_ref, q_ref, ATTN_SCALE), (kg_ref, k_ref, 1.0))):
            z = zqk[:, j * ATTN_W:(j + 1) * ATTN_W]
            r = lax.rsqrt(_seg_sum64(z * z, bd_ref) * (1.0 / HEAD_DIM) + EPS)
            out_ref[...] = z * r * gain_ref[...] * scale
        v_ref[...] = _mm(h, w_ref[:, 3 * CONV_W + 2 * ATTN_W:IN_COLS])

    def blk(c):
        return pl.BlockSpec((tm, c), lambda i: (i, 0))

    return pl.pallas_call(
        body, name="inproj_fwd", grid=(t // tm,),
        in_specs=[blk(D_MODEL), _full((1, D_MODEL)), _full((D_MODEL, IN_COLS)), _full((3, CONV_W)),
                  _full((1, CONV_W)), _full((1, ATTN_W)), _full((1, ATTN_W)), _full((256, 256))],
        out_specs=[blk(3 * CONV_W), blk(2 * ATTN_W), blk(CONV_W), blk(ATTN_W), blk(ATTN_W), blk(ATTN_W)],
        out_shape=[jax.ShapeDtypeStruct((t, 3 * CONV_W), F32), jax.ShapeDtypeStruct((t, 2 * ATTN_W), F32),
                   jax.ShapeDtypeStruct((t, CONV_W), F32), jax.ShapeDtypeStruct((t, ATTN_W), F32),
                   jax.ShapeDtypeStruct((t, ATTN_W), F32), jax.ShapeDtypeStruct((t, ATTN_W), F32)],
        scratch_shapes=[pltpu.VMEM((8, CONV_W), F32)],
        compiler_params=_cparams("arbitrary"),
    )(x, g_mix, w_in, conv_w, conv_b, qg, kg, bd)


SUPER = 16 * QK_BLOCK
KEYS = 2 * QK_BLOCK


def _rows(start, size, dil):
    return pl.ds(start, size) if dil == 1 else pl.ds(start, size, stride=dil)


def _attn_bias(sl_ref, dil):
    qi = lax.broadcasted_iota(jnp.int32, (KEYS, KEYS), 0)
    kj = lax.broadcasted_iota(jnp.int32, (KEYS, KEYS), 1)
    step = jnp.bitwise_and(qi, QK_BLOCK - 1) + QK_BLOCK - kj
    slope = jnp.where(qi < QK_BLOCK, sl_ref[0, 0:1, 0:1], sl_ref[0, 1:2, 0:1])
    bias = jnp.where(jnp.logical_and(step >= 0, step <= QK_BLOCK), -slope * (step * dil).astype(F32), -jnp.inf)
    return bias, kj >= QK_BLOCK


def _unit_start(u, dil):
    if dil == 1:
        return pl.multiple_of(u * QK_BLOCK, QK_BLOCK)
    if dil == 4:
        return jnp.bitwise_and(u, 3) + (u // 4) * (4 * QK_BLOCK)
    return u


def _stack_heads(a, head0):
    zero = jnp.zeros_like(a)
    return jnp.concatenate([jnp.where(head0, a, zero), jnp.where(head0, zero, a)], axis=0)


def _attn_fwd(q, k, v, slopes):
    t = q.shape[0]
    nsb = t // SUPER

    def body(q_ref, kc_ref, kp_ref, vc_ref, vp_ref, sl_ref, o_ref, l_ref, kk, vv, ob, lb):
        s = pl.program_id(1)
        kk[0:SUPER, :] = kp_ref[...]
        kk[SUPER:, :] = kc_ref[...]
        vv[0:SUPER, :] = vp_ref[...]
        vv[SUPER:, :] = vc_ref[...]
        head0 = lax.broadcasted_iota(jnp.int32, (QK_BLOCK, QK_BLOCK), 1) < HEAD_DIM

        for b, dil in enumerate(DILATIONS):
            bias, own_half = _attn_bias(sl_ref, dil)

            def unit(u, carry, b=b, dil=dil, bias=bias, own_half=own_half):
                start = _unit_start(u, dil)
                first_key = SUPER + start - QK_BLOCK * dil
                q2 = _stack_heads(q_ref[_rows(start, QK_BLOCK, dil), :].astype(BF16), head0)
                k2 = kk[_rows(first_key, KEYS, dil), :].astype(BF16)
                v2 = vv[_rows(first_key, KEYS, dil), :].astype(BF16)
                has_prev = jnp.logical_or(s > 0, start >= QK_BLOCK * dil)
                sc = jnp.where(jnp.logical_or(own_half, has_prev), _mm_nt(q2, k2) + bias, -jnp.inf)
                m = jnp.max(sc, axis=-1, keepdims=True)
                e = jnp.exp(sc - m)
                den = jnp.sum(e, axis=-1, keepdims=True)
                o2 = _mm(e.astype(BF16), v2) / den
                l2 = m + jnp.log(den)
                ob[b, _rows(start, QK_BLOCK, dil), :] = jnp.where(head0, o2[0:QK_BLOCK], o2[QK_BLOCK:])
                lb[b, _rows(start, QK_BLOCK, dil), :] = jnp.where(head0, l2[0:QK_BLOCK], l2[QK_BLOCK:])
                return carry

            lax.fori_loop(0, SUPER // QK_BLOCK, unit, 0, unroll=16)

        def merge(i, carry):
            rows = pl.ds(pl.multiple_of(i * 256, 256), 256)
            la, lb_, lc = lb[0, rows, :], lb[1, rows, :], lb[2, rows, :]
            mx = jnp.maximum(jnp.maximum(la, lb_), lc)
            wa, wb, wc = jnp.exp(la - mx), jnp.exp(lb_ - mx), jnp.exp(lc - mx)
            sw = wa + wb + wc
            o_ref[rows, :] = (wa * ob[0, rows, :] + wb * ob[1, rows, :] + wc * ob[2, rows, :]) / sw
            l_ref[rows, :] = mx + jnp.log(sw)
            return carry

        lax.fori_loop(0, SUPER // 256, merge, 0)

    cur = pl.BlockSpec((SUPER, QK_BLOCK), lambda p, s: (s, p))
    prev = pl.BlockSpec((SUPER, QK_BLOCK), lambda p, s: (jnp.maximum(s - 1, 0), p))
    return pl.pallas_call(
        body, name="attn_fwd", grid=(4, nsb),
        in_specs=[cur, cur, prev, cur, prev, pl.BlockSpec((1, 2, QK_BLOCK), lambda p, s: (p, 0, 0))],
        out_specs=[cur, cur],
        out_shape=[jax.ShapeDtypeStruct((t, ATTN_W), F32), jax.ShapeDtypeStruct((t, ATTN_W), F32)],
        scratch_shapes=[pltpu.VMEM((2 * SUPER, QK_BLOCK), F32), pltpu.VMEM((2 * SUPER, QK_BLOCK), F32),
                        pltpu.VMEM((3, SUPER, QK_BLOCK), F32), pltpu.VMEM((3, SUPER, QK_BLOCK), F32)],
        compiler_params=_cparams("parallel", "arbitrary"),
    )(q, k, k, v, v, slopes)


def _outproj_fwd(ya, yc, x, goc, goa, w_out, tm):
    t = x.shape[0]

    def body(ya_ref, yc_ref, x_ref, goc_ref, goa_ref, w_ref, x1_ref):
        _, ychat = _rms_stats(yc_ref[...])
        _, yahat = _rms_stats(ya_ref[...])
        acc = _mm((ychat * goc_ref[...]).astype(BF16), w_ref[0:CONV_W, :])
        acc += _mm((yahat * goa_ref[...]).astype(BF16), w_ref[CONV_W:, :])
        x1_ref[...] = x_ref[...] + acc

    def blk(c):
        return pl.BlockSpec((tm, c), lambda i: (i, 0))

    return pl.pallas_call(
        body, name="outproj_fwd", grid=(t // tm,),
        in_specs=[blk(ATTN_W), blk(CONV_W), blk(D_MODEL), _full((1, CONV_W)), _full((1, ATTN_W)),
                  _full((D_MODEL, D_MODEL))],
        out_specs=blk(D_MODEL),
        out_shape=jax.ShapeDtypeStruct((t, D_MODEL), F32),
        compiler_params=_cparams("parallel"),
    )(ya, yc, x, goc, goa, w_out)


def _ffn_fwd(x1, g_ffn, w_gate, w_up, w_down, fcw, fcb, tm):
    t = x1.shape[0]

    def body(x_ref, g_ref, wg_ref, wu_ref, wd_ref, cw_ref, cb_ref, gp_ref, up_ref, h_ref, x2_ref, carry_ref):
        @pl.when(pl.program_id(0) == 0)
        def _():
            carry_ref[...] = jnp.zeros_like(carry_ref)

        xv = x_ref[...]
        _, xhat = _rms_stats(xv)
        h = (xhat * g_ref[...]).astype(BF16)
        h_ref[...] = h
        gp = _mm(h, wg_ref[...])
        gp_ref[...] = gp.astype(BF16)
        gate, _, _ = _conv_fwd(gp, carry_ref[7:8, :], carry_ref[6:7, :], cw_ref, cb_ref)
        carry_ref[...] = gp[tm - 8:tm, :]
        up = _mm(h, wu_ref[...])
        up_ref[...] = up.astype(BF16)
        a = (gate * _sigmoid(gate) * up).astype(BF16)
        x2_ref[...] = xv + _mm(a, wd_ref[...])

    def blk(c):
        return pl.BlockSpec((tm, c), lambda i: (i, 0))

    return pl.pallas_call(
        body, name="ffn_fwd", grid=(t // tm,),
        in_specs=[blk(D_MODEL), _full((1, D_MODEL)), _full((D_MODEL, D_FF)), _full((D_MODEL, D_FF)),
                  _full((D_FF, D_MODEL)), _full((3, D_FF)), _full((1, D_FF))],
        out_specs=[blk(D_FF), blk(D_FF), blk(D_MODEL), blk(D_MODEL)],
        out_shape=[jax.ShapeDtypeStruct((t, D_FF), BF16), jax.ShapeDtypeStruct((t, D_FF), BF16),
                   jax.ShapeDtypeStruct((t, D_MODEL), BF16), jax.ShapeDtypeStruct((t, D_MODEL), F32)],
        scratch_shapes=[pltpu.VMEM((8, D_FF), F32)],
        compiler_params=_cparams("arbitrary"),
    )(x1, g_ffn, w_gate, w_up, w_down, fcw, fcb)


def _ple_fwd_bwd(x2, p, target, g_ple, w_pg, w_pp, tm):
    t = x2.shape[0]

    def body(x_ref, p_ref, t_ref, g_ref, wg_ref, wp_ref, dx_ref, dxb_ref, loss_ref, dwg_ref, dwp_ref, dg_ref):
        @pl.when(pl.program_id(0) == 0)
        def _():
            loss_ref[...] = jnp.zeros_like(loss_ref)
            dwg_ref[...] = jnp.zeros_like(dwg_ref)
            dwp_ref[...] = jnp.zeros_like(dwp_ref)
            dg_ref[...] = jnp.zeros_like(dg_ref)

        xv = x_ref[...]
        r, xhat = _rms_stats(xv)
        g = g_ref[...]
        h = (xhat * g).astype(BF16)
        pg = _sigmoid(_mm(h, wg_ref[...]))
        pb = p_ref[...].astype(BF16)
        pp = _mm(pb, wp_ref[...])
        err = xv + pg * pp - t_ref[...]
        loss_ref[...] += 0.5 * jnp.sum(jnp.mean(err * err, axis=-1, keepdims=True))
        dx3 = err * (1.0 / D_MODEL)
        d_pp = (dx3 * pg).astype(BF16)
        d_pre = (dx3 * pp * pg * (1.0 - pg)).astype(BF16)
        dwp_ref[...] += _mm_tn(pb, d_pp)
        dwg_ref[...] += _mm_tn(h, d_pre)
        dh = _mm_nt(d_pre, wg_ref[...])
        dg_ref[...] += jnp.sum(dh * xhat, axis=0, keepdims=True)
        dx2 = dx3 + _rms_bwd(dh, xhat, r, g)
        dx_ref[...] = dx2
        dxb_ref[...] = dx2.astype(BF16)

    def blk(c):
        return pl.BlockSpec((tm, c), lambda i: (i, 0))

    return pl.pallas_call(
        body, name="ple_fwd_bwd", grid=(t // tm,),
        in_specs=[blk(D_MODEL), blk(PLE_DIM), blk(D_MODEL), _full((1, D_MODEL)), _full((D_MODEL, D_MODEL)),
                  _full((PLE_DIM, D_MODEL))],
        out_specs=[blk(D_MODEL), blk(D_MODEL), _full((8, 128)), _full((D_MODEL, D_MODEL)),
                   _full((PLE_DIM, D_MODEL)), _full((1, D_MODEL))],
        out_shape=[jax.ShapeDtypeStruct((t, D_MODEL), F32), jax.ShapeDtypeStruct((t, D_MODEL), BF16),
                   jax.ShapeDtypeStruct((8, 128), F32),
                   jax.ShapeDtypeStruct((D_MODEL, D_MODEL), F32), jax.ShapeDtypeStruct((PLE_DIM, D_MODEL), F32),
                   jax.ShapeDtypeStruct((1, D_MODEL), F32)],
        compiler_params=_cparams("arbitrary"),
    )(x2, p, target, g_ple, w_pg, w_pp)


def _ffn_bwd(dx2, h2, gp, up, w_gate, w_up, w_down, fcw, fcb, tm):
    t = dx2.shape[0]
    nblk = t // tm
    fc = D_FF // FF_CHUNKS
    half = tm // 2

    def body(dx_ref, h_ref, gp_ref, gph_ref, up_ref, wg_ref, wu_ref, wd_ref, cw_ref, cb_ref,
             dh_ref, dwd_ref, dwu_ref, dwg_ref, dcw_ref, dcb_ref, carry_ref, a_scr, dup_scr, dgp_scr):
        i = pl.program_id(1)

        @pl.when(i == 0)
        def _():
            carry_ref[...] = jnp.zeros_like(carry_ref)
            dwd_ref[...] = jnp.zeros_like(dwd_ref)
            dwu_ref[...] = jnp.zeros_like(dwu_ref)
            dwg_ref[...] = jnp.zeros_like(dwg_ref)
            dcw_ref[...] = jnp.zeros_like(dcw_ref)
            dcb_ref[...] = jnp.zeros_like(dcb_ref)

        keep = (i < nblk - 1).astype(F32)
        later = carry_ref[...]
        for hf in (1, 0):
            rows = slice(hf * half, (hf + 1) * half)
            dxb = dx_ref[rows, :]
            gp_v = gp_ref[rows, :].astype(F32)
            if hf == 1:
                before = gp_ref[half - 16:half, :].astype(F32)
            else:
                before = gph_ref[...].astype(F32) * keep
            gate, gp1, gp2 = _conv_fwd(gp_v, before[15:16, :], before[14:15, :], cw_ref, cb_ref)
            s = _sigmoid(gate)
            silu = gate * s
            up_v = up_ref[rows, :].astype(F32)
            da = _mm_nt(dxb, wd_ref[...])
            a_scr[rows, :] = (silu * up_v).astype(BF16)
            d_up = (da * silu).astype(BF16)
            dup_scr[rows, :] = d_up
            d_gate = da * up_v * (s * (1.0 + gate * (1.0 - s)))
            d_gp = _conv_bwd_input(d_gate, later[0:1, :], later[1:2, :], cw_ref).astype(BF16)
            dgp_scr[rows, :] = d_gp
            later = d_gate[0:8, :]
            dcw_ref[0:1, :] += jnp.sum(d_gate * gp2, axis=0, keepdims=True)
            dcw_ref[1:2, :] += jnp.sum(d_gate * gp1, axis=0, keepdims=True)
            dcw_ref[2:3, :] += jnp.sum(d_gate * gp_v, axis=0, keepdims=True)
            dcb_ref[...] += jnp.sum(d_gate, axis=0, keepdims=True)
            dh_ref[rows, :] = (_mm_nt(d_gp, wg_ref[...]) + _mm_nt(d_up, wu_ref[...])).astype(BF16)
        carry_ref[...] = later
        dwd_ref[...] += _mm_tn(a_scr[...], dx_ref[...])
        dwu_ref[...] += _mm_tn(h_ref[...], dup_scr[...])
        dwg_ref[...] += _mm_tn(h_ref[...], dgp_scr[...])

    def rev(i):
        return nblk - 1 - i

    one = pl.Buffered(1)
    in_specs = [
        pl.BlockSpec((tm, D_MODEL), lambda j, i: (rev(i), 0)),
        pl.BlockSpec((tm, D_MODEL), lambda j, i: (rev(i), 0)),
        pl.BlockSpec((tm, fc), lambda j, i: (rev(i), j)),
        pl.BlockSpec((16, fc), lambda j, i: (jnp.maximum(rev(i) * (tm // 16) - 1, 0), j)),
        pl.BlockSpec((tm, fc), lambda j, i: (rev(i), j)),
        pl.BlockSpec((D_MODEL, fc), lambda j, i: (0, j), pipeline_mode=one),
        pl.BlockSpec((D_MODEL, fc), lambda j, i: (0, j), pipeline_mode=one),
        pl.BlockSpec((fc, D_MODEL), lambda j, i: (j, 0), pipeline_mode=one),
        pl.BlockSpec((3, fc), lambda j, i: (0, j)),
        pl.BlockSpec((1, fc), lambda j, i: (0, j)),
    ]
    out_specs = [
        pl.BlockSpec((None, tm, D_MODEL), lambda j, i: (j, rev(i), 0)),
        pl.BlockSpec((fc, D_MODEL), lambda j, i: (j, 0), pipeline_mode=one),
        pl.BlockSpec((D_MODEL, fc), lambda j, i: (0, j), pipeline_mode=one),
        pl.BlockSpec((D_MODEL, fc), lambda j, i: (0, j), pipeline_mode=one),
        pl.BlockSpec((3, fc), lambda j, i: (0, j)),
        pl.BlockSpec((1, fc), lambda j, i: (0, j)),
    ]
    return pl.pallas_call(
        body, name="ffn_bwd", grid=(FF_CHUNKS, nblk), in_specs=in_specs, out_specs=out_specs,
        out_shape=[jax.ShapeDtypeStruct((FF_CHUNKS, t, D_MODEL), BF16), jax.ShapeDtypeStruct((D_FF, D_MODEL), F32),
                   jax.ShapeDtypeStruct((D_MODEL, D_FF), F32), jax.ShapeDtypeStruct((D_MODEL, D_FF), F32),
                   jax.ShapeDtypeStruct((3, D_FF), F32), jax.ShapeDtypeStruct((1, D_FF), F32)],
        scratch_shapes=[pltpu.VMEM((8, fc), F32), pltpu.VMEM((tm, fc), BF16), pltpu.VMEM((tm, fc), BF16),
                        pltpu.VMEM((tm, fc), BF16)],
        compiler_params=_cparams("arbitrary", "arbitrary"),
    )(dx2, h2, gp, gp, up, w_gate, w_up, w_down, fcw, fcb)


def _outproj_bwd(dh2, dx2, x1, g_ffn, w_out, yc, ya, goc, goa, zconv, conv_w, conv_b, bd, tm):
    t = x1.shape[0]
    nblk = t // tm

    def body(dh_ref, dx2_ref, x1_ref, g_ref, w_ref, yc_ref, ya_ref, goc_ref, goa_ref, zc_ref, zch_ref, cw_ref, cb_ref,
             bd_ref, dx1_ref, dya_ref, dd_ref, dzc_ref, dw_ref, dg_ref, dgoc_ref, dgoa_ref, dcw_ref, dcb_ref,
             carry_ref):
        i = pl.program_id(0)

        @pl.when(i == 0)
        def _():
            carry_ref[...] = jnp.zeros_like(carry_ref)
            for ref in (dw_ref, dg_ref, dgoc_ref, dgoa_ref, dcw_ref, dcb_ref):
                ref[...] = jnp.zeros_like(ref)

        keep = (i < nblk - 1).astype(F32)
        dh2_v = dh_ref[0].astype(F32)
        for j in range(1, FF_CHUNKS):
            dh2_v = dh2_v + dh_ref[j].astype(F32)
        r, xhat = _rms_stats(x1_ref[...])
        dg_ref[...] += jnp.sum(dh2_v * xhat, axis=0, keepdims=True)
        dx1 = dx2_ref[...] + _rms_bwd(dh2_v, xhat, r, g_ref[...])
        dx1_ref[...] = dx1
        dx1b = dx1.astype(BF16)
        dy = _mm_nt(dx1b, w_ref[...])

        yc_v = yc_ref[...]
        rc, ychat = _rms_stats(yc_v)
        dw_ref[0:CONV_W, :] += _mm_tn((ychat * goc_ref[...]).astype(BF16), dx1b)
        dyc = dy[:, 0:CONV_W]
        dgoc_ref[...] += jnp.sum(dyc * ychat, axis=0, keepdims=True)
        d_yc = _rms_bwd(dyc, ychat, rc, goc_ref[...])

        ya_v = ya_ref[...]
        ra, yahat = _rms_stats(ya_v)
        dw_ref[CONV_W:, :] += _mm_tn((yahat * goa_ref[...]).astype(BF16), dx1b)
        dya = dy[:, CONV_W:]
        dgoa_ref[...] += jnp.sum(dya * yahat, axis=0, keepdims=True)
        d_ya = _rms_bwd(dya, yahat, ra, goa_ref[...])
        dya_ref[...] = d_ya
        dd_ref[...] = _seg_sum64(d_ya * ya_v, bd_ref)

        zb = zc_ref[:, 0:CONV_W]
        zc = zc_ref[:, CONV_W:2 * CONV_W]
        zx = zc_ref[:, 2 * CONV_W:3 * CONV_W]
        u = zc * zx
        uh = zch_ref[:, CONV_W:2 * CONV_W] * zch_ref[:, 2 * CONV_W:3 * CONV_W] * keep
        cv, u1, u2 = _conv_fwd(u, uh[7:8, :], uh[6:7, :], cw_ref, cb_ref)
        d_cv = d_yc * zb
        d_u = _conv_bwd_input(d_cv, carry_ref[0:1, :], carry_ref[1:2, :], cw_ref)
        carry_ref[...] = d_cv[0:8, :]
        dcw_ref[0:1, :] += jnp.sum(d_cv * u2, axis=0, keepdims=True)
        dcw_ref[1:2, :] += jnp.sum(d_cv * u1, axis=0, keepdims=True)
        dcw_ref[2:3, :] += jnp.sum(d_cv * u, axis=0, keepdims=True)
        dcb_ref[...] += jnp.sum(d_cv, axis=0, keepdims=True)
        dzc_ref[:, 0:CONV_W] = (d_yc * cv).astype(BF16)
        dzc_ref[:, CONV_W:2 * CONV_W] = (d_u * zx).astype(BF16)
        dzc_ref[:, 2 * CONV_W:3 * CONV_W] = (d_u * zc).astype(BF16)

    def rev(i):
        return nblk - 1 - i

    def blk(c):
        return pl.BlockSpec((tm, c), lambda i: (rev(i), 0))

    in_specs = [
        pl.BlockSpec((FF_CHUNKS, tm, D_MODEL), lambda i: (0, rev(i), 0)),
        blk(D_MODEL), blk(D_MODEL), _full((1, D_MODEL)), _full((D_MODEL, D_MODEL)),
        blk(CONV_W), blk(ATTN_W), _full((1, CONV_W)), _full((1, ATTN_W)),
        blk(3 * CONV_W),
        pl.BlockSpec((8, 3 * CONV_W), lambda i: (jnp.maximum(rev(i) * (tm // 8) - 1, 0), 0)),
        _full((3, CONV_W)), _full((1, CONV_W)), _full((256, 256)),
    ]
    out_specs = [blk(D_MODEL), blk(ATTN_W), blk(ATTN_W), blk(3 * CONV_W), _full((D_MODEL, D_MODEL)),
                 _full((1, D_MODEL)), _full((1, CONV_W)), _full((1, ATTN_W)), _full((3, CONV_W)), _full((1, CONV_W))]
    return pl.pallas_call(
        body, name="outproj_bwd", grid=(nblk,), in_specs=in_specs, out_specs=out_specs,
        out_shape=[jax.ShapeDtypeStruct((t, D_MODEL), F32), jax.ShapeDtypeStruct((t, ATTN_W), F32),
                   jax.ShapeDtypeStruct((t, ATTN_W), F32), jax.ShapeDtypeStruct((t, 3 * CONV_W), BF16),
                   jax.ShapeDtypeStruct((D_MODEL, D_MODEL), F32), jax.ShapeDtypeStruct((1, D_MODEL), F32),
                   jax.ShapeDtypeStruct((1, CONV_W), F32), jax.ShapeDtypeStruct((1, ATTN_W), F32),
                   jax.ShapeDtypeStruct((3, CONV_W), F32), jax.ShapeDtypeStruct((1, CONV_W), F32)],
        scratch_shapes=[pltpu.VMEM((8, CONV_W), F32)],
        compiler_params=_cparams("arbitrary"),
    )(dh2, dx2, x1, g_ffn, w_out, yc, ya, goc, goa, zconv, zconv, conv_w, conv_b, bd)


def _attn_bwd(q, k, v, dya, lse, dd, slopes):
    t = q.shape[0]
    nsb = t // SUPER

    def body(q_ref, kc_ref, kp_ref, vc_ref, vp_ref, dy_ref, l_ref, d_ref, sl_ref, dq_ref, dk_ref, dv_ref,
             kk, vv, dkacc, dvacc):
        s = pl.program_id(1)

        @pl.when(s == 0)
        def _():
            dkacc[...] = jnp.zeros_like(dkacc)
            dvacc[...] = jnp.zeros_like(dvacc)

        dkacc[0:SUPER, :] = dkacc[SUPER:, :]
        dvacc[0:SUPER, :] = dvacc[SUPER:, :]
        dkacc[SUPER:, :] = jnp.zeros((SUPER, QK_BLOCK), F32)
        dvacc[SUPER:, :] = jnp.zeros((SUPER, QK_BLOCK), F32)

        @pl.when(s < nsb)
        def _():
            kk[0:SUPER, :] = kp_ref[...]
            kk[SUPER:, :] = kc_ref[...]
            vv[0:SUPER, :] = vp_ref[...]
            vv[SUPER:, :] = vc_ref[...]
            head0 = lax.broadcasted_iota(jnp.int32, (QK_BLOCK, QK_BLOCK), 1) < HEAD_DIM

            for b, dil in enumerate(DILATIONS):
                bias, own_half = _attn_bias(sl_ref, dil)

                def unit(u, carry, b=b, dil=dil, bias=bias, own_half=own_half):
                    start = _unit_start(u, dil)
                    first_key = SUPER + start - QK_BLOCK * dil
                    qrows = _rows(start, QK_BLOCK, dil)
                    krows = _rows(first_key, KEYS, dil)
                    q2 = _stack_heads(q_ref[qrows, :].astype(BF16), head0)
                    dy2 = _stack_heads(dy_ref[qrows, :].astype(BF16), head0)
                    lv, dv_ = l_ref[qrows, :], d_ref[qrows, :]
                    l2 = jnp.concatenate([lv[:, 0:1], lv[:, HEAD_DIM:HEAD_DIM + 1]], axis=0)
                    d2 = jnp.concatenate([dv_[:, 0:1], dv_[:, HEAD_DIM:HEAD_DIM + 1]], axis=0)
                    k2 = kk[krows, :].astype(BF16)
                    v2 = vv[krows, :].astype(BF16)
                    has_prev = jnp.logical_or(s > 0, start >= QK_BLOCK * dil)
                    sc = jnp.where(jnp.logical_or(own_half, has_prev), _mm_nt(q2, k2) + bias, -jnp.inf)
                    prob = jnp.exp(sc - l2)
                    ds = (prob * (_mm_nt(dy2, v2) - d2)).astype(BF16)
                    dvacc[krows, :] += _mm_tn(prob.astype(BF16), dy2)
                    dkacc[krows, :] += _mm_tn(ds, q2)
                    dq2 = _mm(ds, k2)
                    dq = jnp.where(head0, dq2[0:QK_BLOCK], dq2[QK_BLOCK:]) * ATTN_SCALE
                    if b == 0:
                        dq_ref[qrows, :] = dq
                    else:
                        dq_ref[qrows, :] += dq
                    return carry

                lax.fori_loop(0, SUPER // QK_BLOCK, unit, 0, unroll=8)

        dk_ref[...] = dkacc[0:SUPER, :]
        dv_ref[...] = dvacc[0:SUPER, :].astype(BF16)

    def cur_map(p, s):
        return (jnp.minimum(s, nsb - 1), p)

    def prev_map(p, s):
        return (jnp.clip(s - 1, 0, nsb - 1), p)

    cur = pl.BlockSpec((SUPER, QK_BLOCK), cur_map)
    prev = pl.BlockSpec((SUPER, QK_BLOCK), prev_map)
    return pl.pallas_call(
        body, name="attn_bwd", grid=(4, nsb + 1),
        in_specs=[cur, cur, prev, cur, prev, cur, cur, cur, pl.BlockSpec((1, 2, QK_BLOCK), lambda p, s: (p, 0, 0))],
        out_specs=[cur, prev, prev],
        out_shape=[jax.ShapeDtypeStruct((t, ATTN_W), F32), jax.ShapeDtypeStruct((t, ATTN_W), F32),
                   jax.ShapeDtypeStruct((t, ATTN_W), BF16)],
        scratch_shapes=[pltpu.VMEM((2 * SUPER, QK_BLOCK), F32)] * 4,
        compiler_params=_cparams("parallel", "arbitrary"),
    )(q, k, k, v, v, dya, lse, dd, slopes)


def _attn_bwd_per_branch_unused(q, k, v, dya, lse, dd, slopes, dil):
    t = q.shape[0]
    length = t // dil
    chunk = _attn_chunk(t, dil)
    nch = length // chunk
    nb = chunk // QK_BLOCK
    nblocks = length // QK_BLOCK
    view = (length, dil * ATTN_W)
    ext = chunk + QK_BLOCK

    def body(q_ref, dy_ref, l_ref, d_ref, k_ref, v_ref, qn_ref, dyn_ref, ln_ref, dn_ref, kh_ref, vh_ref, sl_ref,
             dq_ref, dk_ref, dv_ref, qbuf, dybuf, lbuf, dbuf, kbuf, vbuf, dkacc, dvacc):
        c = pl.program_id(2)
        qbuf[0:chunk, :] = q_ref[...]
        qbuf[chunk:, :] = qn_ref[...]
        dybuf[0:chunk, :] = dy_ref[...].astype(BF16)
        dybuf[chunk:, :] = dyn_ref[...].astype(BF16)
        lbuf[0:chunk, :] = l_ref[...]
        lbuf[chunk:, :] = ln_ref[...]
        dbuf[0:chunk, :] = d_ref[...]
        dbuf[chunk:, :] = dn_ref[...]
        kbuf[0:QK_BLOCK, :] = kh_ref[...]
        kbuf[QK_BLOCK:, :] = k_ref[...]
        vbuf[0:QK_BLOCK, :] = vh_ref[...]
        vbuf[QK_BLOCK:, :] = v_ref[...]
        valid_cur, valid_prev, dist_cur, dist_prev, head0 = _attn_masks(dil)

        def pair(qb, dyb, lv, dv_, kb, vb, valid, dist):
            dq = jnp.zeros((QK_BLOCK, QK_BLOCK), F32)
            dk = jnp.zeros((QK_BLOCK, QK_BLOCK), F32)
            dvv = jnp.zeros((QK_BLOCK, QK_BLOCK), F32)
            for hh in range(2):
                sl = sl_ref[0, hh:hh + 1, :]
                hm = head0 if hh == 0 else jnp.logical_not(head0)
                col = hh * HEAD_DIM
                qm = jnp.where(hm, qb, jnp.zeros_like(qb))
                dym = jnp.where(hm, dyb, jnp.zeros_like(dyb))
                s = jnp.where(valid, _mm_nt(qm, kb) - sl * dist, -jnp.inf)
                prob = jnp.exp(s - lv[:, col:col + 1])
                ds = (prob * (_mm_nt(dym, vb) - dv_[:, col:col + 1])).astype(BF16)
                dvv += _mm_tn(prob.astype(BF16), dym)
                dk += _mm_tn(ds, qm)
                dq += jnp.where(hm, _mm(ds, kb), 0.0)
            return dq, dk, dvv

        def blk(j, carry):
            off = pl.multiple_of(j * QK_BLOCK, QK_BLOCK)
            nxt = pl.multiple_of(off + QK_BLOCK, QK_BLOCK)
            qb = qbuf[pl.ds(off, QK_BLOCK), :]
            dyb = dybuf[pl.ds(off, QK_BLOCK), :]
            lv = lbuf[pl.ds(off, QK_BLOCK), :]
            dv_ = dbuf[pl.ds(off, QK_BLOCK), :]
            dq_c, dk_c, dv_c = pair(qb, dyb, lv, dv_, kbuf[pl.ds(nxt, QK_BLOCK), :], vbuf[pl.ds(nxt, QK_BLOCK), :],
                                    valid_cur, dist_cur)
            dkacc[pl.ds(nxt, QK_BLOCK), :] = dk_c
            dvacc[pl.ds(nxt, QK_BLOCK), :] = dv_c
            has_prev = jnp.logical_or(c > 0, j > 0)
            dq_p, dk_p, dv_p = pair(qb, dyb, lv, dv_, kbuf[pl.ds(off, QK_BLOCK), :], vbuf[pl.ds(off, QK_BLOCK), :],
                                    jnp.logical_and(valid_prev, has_prev), dist_prev)

            @pl.when(j > 0)
            def _():
                dkacc[pl.ds(off, QK_BLOCK), :] += dk_p
                dvacc[pl.ds(off, QK_BLOCK), :] += dv_p

            dq_ref[pl.ds(off, QK_BLOCK), :] = (dq_c + dq_p) * ATTN_SCALE
            return carry

        lax.fori_loop(0, nb, blk, 0)

        @pl.when(c < nch - 1)
        def _():
            _, dk_p, dv_p = pair(qbuf[chunk:, :], dybuf[chunk:, :], lbuf[chunk:, :], dbuf[chunk:, :],
                                 kbuf[chunk:, :], vbuf[chunk:, :], valid_prev, dist_prev)
            dkacc[chunk:, :] += dk_p
            dvacc[chunk:, :] += dv_p

        dk_ref[...] = dkacc[QK_BLOCK:, :]
        dv_ref[...] = dvacc[QK_BLOCK:, :]

    def cmap(p, r, c):
        return (c, r * 4 + p)

    def before(p, r, c):
        return (jnp.maximum(c * nb - 1, 0), r * 4 + p)

    def after(p, r, c):
        return (jnp.minimum((c + 1) * nb, nblocks - 1), r * 4 + p)

    main = pl.BlockSpec((chunk, QK_BLOCK), cmap)
    hb = pl.BlockSpec((QK_BLOCK, QK_BLOCK), before)
    ha = pl.BlockSpec((QK_BLOCK, QK_BLOCK), after)
    qv, kv, vv = q.reshape(view), k.reshape(view), v.reshape(view)
    dyv, lv, ddv = dya.reshape(view), lse.reshape(view), dd.reshape(view)
    outs = pl.pallas_call(
        body, name=f"attn_bwd_d{dil}", grid=(4, dil, nch),
        in_specs=[main] * 6 + [ha] * 4 + [hb] * 2 + [pl.BlockSpec((1, 2, QK_BLOCK), lambda p, r, c: (p, 0, 0))],
        out_specs=[main] * 3,
        out_shape=[jax.ShapeDtypeStruct(view, F32)] * 3,
        scratch_shapes=[pltpu.VMEM((ext, QK_BLOCK), BF16), pltpu.VMEM((ext, QK_BLOCK), BF16),
                        pltpu.VMEM((ext, QK_BLOCK), F32), pltpu.VMEM((ext, QK_BLOCK), F32),
                        pltpu.VMEM((ext, QK_BLOCK), BF16), pltpu.VMEM((ext, QK_BLOCK), BF16),
                        pltpu.VMEM((ext, QK_BLOCK), F32), pltpu.VMEM((ext, QK_BLOCK), F32)],
        compiler_params=_cparams("arbitrary", "arbitrary", "arbitrary"),
    )(qv, dyv, lv, ddv, kv, vv, qv, dyv, lv, ddv, kv, vv, slopes)
    return [o.reshape(t, ATTN_W) for o in outs]


def _inproj_bwd(dq, dk, dv, dzconv, zqk, x, dx1, g_mix, w_in, qg, kg, bd, tm):
    t = x.shape[0]

    def body(dq_ref, dk_ref, dv_ref, dzc_ref, zqk_ref, x_ref, dx1_ref, g_ref, w_ref, qg_ref,
             kg_ref, bd_ref, dx_ref, dw_ref, dg_ref, dqg_ref, dkg_ref):
        @pl.when(pl.program_id(0) == 0)
        def _():
            for ref in (dw_ref, dg_ref, dqg_ref, dkg_ref):
                ref[...] = jnp.zeros_like(ref)

        parts = [dzc_ref[...]]
        for j, (dn_ref, gain_ref, dgain_ref) in enumerate(((dq_ref, qg_ref, dqg_ref), (dk_ref, kg_ref, dkg_ref))):
            dn = dn_ref[...]
            z = zqk_ref[:, j * ATTN_W:(j + 1) * ATTN_W]
            r = lax.rsqrt(_seg_sum64(z * z, bd_ref) * (1.0 / HEAD_DIM) + EPS)
            zhat = z * r
            dgain_ref[...] += jnp.sum(dn * zhat, axis=0, keepdims=True)
            gd = dn * gain_ref[...]
            parts.append((r * (gd - zhat * (_seg_sum64(gd * zhat, bd_ref) * (1.0 / HEAD_DIM)))).astype(BF16))
        parts.append(dv_ref[...].astype(BF16))
        dz = jnp.concatenate(parts, axis=1)

        r, xhat = _rms_stats(x_ref[...])
        g = g_ref[...]
        dw_ref[...] += _mm_tn((xhat * g).astype(BF16), dz)
        dh = _mm_nt(dz, w_ref[...])
        dg_ref[...] += jnp.sum(dh * xhat, axis=0, keepdims=True)
        dx_ref[...] = dx1_ref[...] + _rms_bwd(dh, xhat, r, g)

    def blk(c):
        return pl.BlockSpec((tm, c), lambda i: (i, 0))

    return pl.pallas_call(
        body, name="inproj_bwd", grid=(t // tm,),
        in_specs=[blk(ATTN_W)] * 3 + [blk(3 * CONV_W), blk(2 * ATTN_W), blk(D_MODEL), blk(D_MODEL), _full((1, D_MODEL)),
                                      _full((D_MODEL, IN_COLS)), _full((1, ATTN_W)), _full((1, ATTN_W)),
                                      _full((256, 256))],
        out_specs=[blk(D_MODEL), _full((D_MODEL, IN_COLS)), _full((1, D_MODEL)), _full((1, ATTN_W)),
                   _full((1, ATTN_W))],
        out_shape=[jax.ShapeDtypeStruct((t, D_MODEL), F32), jax.ShapeDtypeStruct((D_MODEL, IN_COLS), F32),
                   jax.ShapeDtypeStruct((1, D_MODEL), F32), jax.ShapeDtypeStruct((1, ATTN_W), F32),
                   jax.ShapeDtypeStruct((1, ATTN_W), F32)],
        compiler_params=_cparams("arbitrary"),
    )(dq, dk, dv, dzconv, zqk, x, dx1, g_mix, w_in, qg, kg, bd)


def _ordered_after(a, token):
    return a if token is None else a + token[0:1, 0:1].reshape((1,) * a.ndim)


def _local_step(x, p, target, w, tms, hooks=None):
    hooks = hooks or {}
    bd = jnp.kron(jnp.eye(4, dtype=F32), jnp.ones((HEAD_DIM, HEAD_DIM), F32)).astype(BF16)
    qg = jnp.tile(w["q_norm_g"], (1, 8))
    kg = jnp.tile(w["k_norm_g"], (1, 8))
    slopes = jnp.exp2(-jnp.arange(1, 9, dtype=F32))
    slopes = jnp.broadcast_to(slopes.reshape(4, 2, 1), (4, 2, QK_BLOCK))

    zconv, zqk, yc, q, k, v = _inproj_fwd(x, w["g_mix"], w["w_in"], w["conv_w"], w["conv_b"], qg, kg, bd, tms[0])
    ya, lse = _attn_fwd(q, k, v, slopes)
    if "late_weights" in hooks:
        w = {**w, **hooks["late_weights"](lse)}
    x1 = _outproj_fwd(ya, yc, x, w["g_out_conv"], w["g_out_attn"], w["w_out"], tms[0])
    gp, up, h2, x2 = _ffn_fwd(x1, w["g_ffn"], w["w_gate"], w["w_up"], w["w_down"], w["ffn_conv_w"], w["ffn_conv_b"], tms[1])
    dx2, dx2b, loss, dw_pg, dw_pp, dg_ple = _ple_fwd_bwd(x2, p, target, w["g_ple"], w["w_ple_gate"], w["w_ple_proj"], tms[0])
    dh2, dw_down, dw_up, dw_gate, dfcw, dfcb = _ffn_bwd(dx2b, h2, gp, up, w["w_gate"], w["w_up"], w["w_down"],
                                                        w["ffn_conv_w"], w["ffn_conv_b"], tms[0])
    token = None
    if "ffn_grads" in hooks:
        token = hooks["ffn_grads"]({"w_ple_gate": dw_pg, "w_ple_proj": dw_pp, "w_down": dw_down, "w_up": dw_up,
                                    "w_gate": dw_gate})
    dx1, dya, dd, dzconv, dw_out, dg_ffn, dgoc, dgoa, dcw, dcb = _outproj_bwd(
        dh2, dx2, x1, _ordered_after(w["g_ffn"], token), w["w_out"], yc, ya, w["g_out_conv"], w["g_out_attn"], zconv,
        w["conv_w"], w["conv_b"], bd, tms[1])
    token = hooks["outproj_done"](dx1) if "outproj_done" in hooks else None
    dq, dk, dv = _attn_bwd(q, k, v, dya, lse, dd, _ordered_after(slopes, token))
    dx, dw_in, dg_mix, dqg, dkg = _inproj_bwd(dq, dk, dv, dzconv, zqk, x, dx1, w["g_mix"], w["w_in"], qg, kg, bd,
                                              tms[0])
    grads = {
        "g_mix": dg_mix, "w_in": dw_in, "conv_w": dcw, "conv_b": dcb,
        "q_norm_g": dqg.reshape(8, HEAD_DIM).sum(0, keepdims=True),
        "k_norm_g": dkg.reshape(8, HEAD_DIM).sum(0, keepdims=True),
        "g_out_conv": dgoc, "g_out_attn": dgoa, "w_out": dw_out, "g_ffn": dg_ffn, "w_gate": dw_gate, "w_up": dw_up,
        "ffn_conv_w": dfcw, "ffn_conv_b": dfcb, "w_down": dw_down, "g_ple": dg_ple, "w_ple_gate": dw_pg,
        "w_ple_proj": dw_pp,
    }
    return loss[0, 0], dx, grads


ANY = pl.BlockSpec(memory_space=pl.ANY)
MESH = pl.DeviceIdType.MESH


def _all_gather(shards, name):
    n = len(shards)

    def body(*refs):
        ins, outs = refs[:n], refs[n:2 * n]
        send_sems, recv_sems, local_sems = refs[2 * n:]
        x, y, c = lax.axis_index("x"), lax.axis_index("y"), lax.axis_index("c")
        me, sibling = (x, y, c), (x, y, 1 - c)
        chips = [(1 - x, y), (x, 1 - y), (1 - x, 1 - y)]

        def slot(dev):
            return 4 * dev[0] + 2 * dev[1] + dev[2]

        def copy(b, k, block, to, src=None):
            dst = outs[b].at[slot(block)]
            return pltpu.make_async_remote_copy(
                src_ref=dst if src is None else src, dst_ref=dst, send_sem=send_sems.at[b, k],
                recv_sem=recv_sems.at[b, k], device_id=to, device_id_type=MESH)

        mine = [pltpu.make_async_copy(ins[b], outs[b].at[slot(me)], local_sems.at[b]) for b in range(n)]
        first, passed = [], []
        for b in range(n):
            mine[b].start()
            first.append(copy(b, 0, me, sibling, src=ins[b]))
            first += [copy(b, 1 + j, me, (*chip, c), src=ins[b]) for j, chip in enumerate(chips)]
        for cp in first:
            cp.start()
        for j, chip in enumerate(chips):
            for b in range(n):
                copy(b, 1 + j, (*chip, c), me).wait_recv()
                fwd = copy(b, 4 + j, (*chip, c), sibling)
                fwd.start()
                passed.append(fwd)
        for b in range(n):
            copy(b, 0, sibling, me).wait_recv()
            for j, chip in enumerate(chips):
                copy(b, 4 + j, (*chip, 1 - c), me).wait_recv()
        for cp in first + passed:
            cp.wait_send()
        for cp in mine:
            cp.wait()

    return pl.pallas_call(
        body, name=name,
        in_specs=[ANY] * n, out_specs=[ANY] * n,
        out_shape=[jax.ShapeDtypeStruct((N_DEV,) + s.shape, s.dtype) for s in shards],
        scratch_shapes=[pltpu.SemaphoreType.DMA((n, 7)), pltpu.SemaphoreType.DMA((n, 7)),
                        pltpu.SemaphoreType.DMA((n,))],
    )(*shards)


HBM = pl.BlockSpec(memory_space=pltpu.HBM)
SEM = pl.BlockSpec(memory_space=pltpu.SEMAPHORE)
EFFECT = pltpu.SideEffectType.DATAFLOW_SIDE_EFFECTING
FLIPS = ((0, 0, 1), (0, 1, 0), (0, 1, 1), (1, 0, 0), (1, 0, 1), (1, 1, 0), (1, 1, 1))


def _flip_peers():
    pos = (lax.axis_index("x"), lax.axis_index("y"), lax.axis_index("c"))
    return [tuple(1 - a if f else a for a, f in zip(pos, flip)) for flip in FLIPS]


def _hbm(a):
    return pltpu.with_memory_space_constraint(a, pltpu.HBM)


def _split_start(name, srcs, lands, plan, n_copies, after):
    n, m = len(srcs), len(lands)

    def body(*refs):
        send_sems, recv_sems, token = refs[n + m + 1], refs[n + m + 2], refs[-1]
        for i, (src, dst, peer) in enumerate(plan(refs[:n], refs[n:n + m])):
            pltpu.make_async_remote_copy(src_ref=src, dst_ref=dst, send_sem=send_sems.at[i], recv_sem=recv_sems.at[i],
                                         device_id=peer, device_id_type=MESH).start()
        token[...] = jnp.zeros_like(token)

    outs = pl.pallas_call(
        body, name=name + "_start",
        in_specs=[HBM] * (n + m) + [ANY],
        out_specs=[SEM, SEM] + [HBM] * (n + m) + [pl.BlockSpec(memory_space=pltpu.VMEM)],
        out_shape=[pltpu.SemaphoreType.DMA((n_copies,)), pltpu.SemaphoreType.DMA((n_copies,))]
        + [pltpu.HBM(a.shape, a.dtype) for a in list(srcs) + list(lands)] + [jax.ShapeDtypeStruct((8, 128), F32)],
        input_output_aliases={i: 2 + i for i in range(n + m)},
        compiler_params=pltpu.CompilerParams(has_side_effects=EFFECT),
    )(*[_hbm(a) for a in list(srcs) + list(lands)], after)
    return (outs[0], outs[1], outs[2:2 + n], outs[2 + n:2 + n + m]), outs[-1]


def _split_wait(name, started, plan, after):
    send_sems, recv_sems, srcs, lands = started
    n, m = len(srcs), len(lands)

    def body(*refs):
        send_ref, recv_ref = refs[n + m], refs[n + m + 1]
        for i, (src, dst, peer) in enumerate(plan(refs[:n], refs[n:n + m])):
            copy = pltpu.make_async_remote_copy(src_ref=src, dst_ref=dst, send_sem=send_ref.at[i],
                                                recv_sem=recv_ref.at[i], device_id=peer, device_id_type=MESH)
            copy.wait_send()
            copy.wait_recv()

    outs = pl.pallas_call(
        body, name=name + "_wait",
        in_specs=[HBM] * (n + m) + [SEM, SEM, ANY],
        out_specs=[HBM] * (n + m),
        out_shape=[pltpu.HBM(a.shape, a.dtype) for a in list(srcs) + list(lands)],
        input_output_aliases={i: i for i in range(n + m)},
        compiler_params=pltpu.CompilerParams(has_side_effects=EFFECT),
    )(*srcs, *lands, send_sems, recv_sems, after)
    return outs[:n], outs[n:]


def _gather_plan(srcs, lands):
    slot = 4 * lax.axis_index("x") + 2 * lax.axis_index("y") + lax.axis_index("c")
    return [(src, land.at[slot], peer) for src, land in zip(srcs, lands) for peer in _flip_peers()]


def _sibling_plan(srcs, lands):
    x, y, c = lax.axis_index("x"), lax.axis_index("y"), lax.axis_index("c")
    return [(src.at[k, 1 - c], land.at[k], (x, y, 1 - c)) for src, land in zip(srcs, lands) for k in range(N_CHIP)]


def _chip_plan(srcs, lands):
    x, y, c = lax.axis_index("x"), lax.axis_index("y"), lax.axis_index("c")
    return [(src.at[2 * cx + cy], land.at[2 * x + y], (cx, cy, c))
            for src, land in zip(srcs, lands) for cx, cy in ((1 - x, y), (x, 1 - y), (1 - x, 1 - y))]


def _row_tile(rows):
    for tr in range(min(rows, 512), 15, -16):
        if rows % tr == 0:
            return tr
    return rows


def _sibling_exchange(gs):
    n = len(gs)

    def body(*refs):
        g_refs, land_refs = refs[:n], refs[n:2 * n]
        send_sems, recv_sems = refs[2 * n:]
        x, y, c = lax.axis_index("x"), lax.axis_index("y"), lax.axis_index("c")
        copies = [pltpu.make_async_remote_copy(
            src_ref=g_refs[b].at[k, 1 - c], dst_ref=land_refs[b].at[k], send_sem=send_sems.at[b, k],
            recv_sem=recv_sems.at[b, k], device_id=(x, y, 1 - c), device_id_type=MESH)
            for b in range(n) for k in range(N_CHIP)]
        for cp in copies:
            cp.start()
        for cp in copies:
            cp.wait()

    return pl.pallas_call(
        body, name="rs_sibling_exchange", in_specs=[ANY] * n, out_specs=[ANY] * n,
        out_shape=[jax.ShapeDtypeStruct((N_CHIP,) + g.shape[2:], g.dtype) for g in gs],
        scratch_shapes=[pltpu.SemaphoreType.DMA((n, N_CHIP)), pltpu.SemaphoreType.DMA((n, N_CHIP))],
    )(*gs)


def _pair_sum(g, land, core, name):
    rows, cols = land.shape[1:]
    tr = _row_tile(rows)

    def body(c_ref, g_ref, l_ref, o_ref):
        o_ref[...] = (g_ref[...].astype(F32) + l_ref[...].astype(F32)).astype(o_ref.dtype)

    return pl.pallas_call(
        body, name=f"rs_pair_sum_{name}",
        grid_spec=pltpu.PrefetchScalarGridSpec(
            num_scalar_prefetch=1, grid=(N_CHIP, rows // tr),
            in_specs=[pl.BlockSpec((None, None, tr, cols), lambda k, i, c_ref: (k, c_ref[0], i, 0)),
                      pl.BlockSpec((None, tr, cols), lambda k, i, c_ref: (k, i, 0))],
            out_specs=pl.BlockSpec((None, tr, cols), lambda k, i, c_ref: (k, i, 0))),
        out_shape=jax.ShapeDtypeStruct(land.shape, land.dtype),
        compiler_params=_cparams("parallel", "parallel"),
    )(core, g, land)


def _chip_exchange(parts):
    n = len(parts)

    def body(*refs):
        p_refs, land_refs = refs[:n], refs[n:2 * n]
        send_sems, recv_sems, local_sems = refs[2 * n:]
        x, y, c = lax.axis_index("x"), lax.axis_index("y"), lax.axis_index("c")
        mine = 2 * x + y
        chips = [(1 - x, y), (x, 1 - y), (1 - x, 1 - y)]
        own = [pltpu.make_async_copy(p_refs[b].at[mine], land_refs[b].at[mine], local_sems.at[b]) for b in range(n)]
        for cp in own:
            cp.start()
        copies = [pltpu.make_async_remote_copy(
            src_ref=p_refs[b].at[2 * cx + cy], dst_ref=land_refs[b].at[mine], send_sem=send_sems.at[b, j],
            recv_sem=recv_sems.at[b, j], device_id=(cx, cy, c), device_id_type=MESH)
            for b in range(n) for j, (cx, cy) in enumerate(chips)]
        for cp in copies:
            cp.start()
        for b in range(n):
            for j, (cx, cy) in enumerate(chips):
                pltpu.make_async_remote_copy(
                    src_ref=p_refs[b].at[mine], dst_ref=land_refs[b].at[2 * cx + cy], send_sem=send_sems.at[b, j],
                    recv_sem=recv_sems.at[b, j], device_id=(cx, cy, c), device_id_type=MESH).wait_recv()
        for cp in copies:
            cp.wait_send()
        for cp in own:
            cp.wait()

    return pl.pallas_call(
        body, name="rs_chip_exchange", in_specs=[ANY] * n, out_specs=[ANY] * n,
        out_shape=[jax.ShapeDtypeStruct(p.shape, p.dtype) for p in parts],
        scratch_shapes=[pltpu.SemaphoreType.DMA((n, 3)), pltpu.SemaphoreType.DMA((n, 3)),
                        pltpu.SemaphoreType.DMA((n,))],
    )(*parts)


def _adamw(parts, w, m, v, name):
    k, rows, cols = parts.shape
    tr = _row_tile(rows)
    c1 = 1.0 / (1.0 - ADAM_B1 ** ADAM_STEP)
    c2 = 1.0 / (1.0 - ADAM_B2 ** ADAM_STEP)

    def body(p_ref, w_ref, m_ref, v_ref, g_ref, d_ref, nm_ref, nv_ref):
        g = p_ref[0].astype(F32)
        for j in range(1, k):
            g = g + p_ref[j].astype(F32)
        g_ref[...] = g
        nm = ADAM_B1 * m_ref[...] + (1.0 - ADAM_B1) * g
        nv = ADAM_B2 * v_ref[...] + (1.0 - ADAM_B2) * (g * g)
        nm_ref[...] = nm
        nv_ref[...] = nv
        d_ref[...] = -ADAM_LR * ((nm * c1) / (jnp.sqrt(nv * c2) + ADAM_EPS) + ADAM_WD * w_ref[...])

    blk = pl.BlockSpec((tr, cols), lambda i: (i, 0))
    return pl.pallas_call(
        body, name=name, grid=(rows // tr,),
        in_specs=[pl.BlockSpec((k, tr, cols), lambda i: (0, i, 0)), blk, blk, blk],
        out_specs=[blk] * 4, out_shape=[jax.ShapeDtypeStruct((rows, cols), F32)] * 4,
        compiler_params=_cparams("parallel"),
    )(parts, w, m, v)


COL_SHARDED = ("w_in", "w_gate", "w_up", "w_ple_proj")
REPLICATED = (("g_mix", 1024), ("conv_b", 512), ("q_norm_g", 64), ("k_norm_g", 64), ("g_out_conv", 512),
              ("g_out_attn", 512), ("g_ffn", 1024), ("ffn_conv_b", 2816), ("g_ple", 1024))
CONV_SHARDED = (("conv_w", CONV_W), ("ffn_conv_w", D_FF))


def _gathered_to_full(name, gathered):
    if name in COL_SHARDED:
        return gathered.transpose(1, 0, 2).reshape(gathered.shape[1], -1)
    return gathered.reshape(-1, gathered.shape[2])


def _full_to_stacked(name, grad, shard_shape):
    sr, sc = shard_shape
    if name in COL_SHARDED:
        a = grad.reshape(sr, N_DEV, sc).transpose(1, 0, 2)
    else:
        a = grad.reshape(N_DEV, sr, sc)
    return a.astype(BF16).reshape(N_CHIP, 2, sr, sc)


def _pad_rows(vec, rows):
    return jnp.pad(vec, (0, rows * 1024 - vec.shape[0])).reshape(rows, 1024)


def kernel(x, p, g_mix, w_in, conv_w, conv_b, q_norm_g, k_norm_g, g_out_conv, g_out_attn, w_out, g_ffn, w_gate, w_up, ffn_conv_w, ffn_conv_b, w_down, g_ple, w_ple_gate, w_ple_proj, loss_target, m_g_mix, m_w_in, m_conv_w, m_conv_b, m_q_norm_g, m_k_norm_g, m_g_out_conv, m_g_out_attn, m_w_out, m_g_ffn, m_w_gate, m_w_up, m_ffn_conv_w, m_ffn_conv_b, m_w_down, m_g_ple, m_w_ple_gate, m_w_ple_proj, v_g_mix, v_w_in, v_conv_w, v_conv_b, v_q_norm_g, v_k_norm_g, v_g_out_conv, v_g_out_attn, v_w_out, v_g_ffn, v_w_gate, v_w_up, v_ffn_conv_w, v_ffn_conv_b, v_w_down, v_g_ple, v_w_ple_gate, v_w_ple_proj):
    args = dict(locals())
    names = ["g_mix", "w_in", "conv_w", "conv_b", "q_norm_g", "k_norm_g", "g_out_conv", "g_out_attn", "w_out", "g_ffn",
             "w_gate", "w_up", "ffn_conv_w", "ffn_conv_b", "w_down", "g_ple", "w_ple_gate", "w_ple_proj"]
    big = [n for n, _ in BIG_ROWS]
    conv = [n for n, _ in CONV_SHARDED]
    wts = {n: (args[n][0] if n in big or n in conv else args[n]) for n in names}
    mom = {n: (args["m_" + n][0] if n in big or n in conv else args["m_" + n]) for n in names}
    var = {n: (args["v_" + n][0] if n in big or n in conv else args["v_" + n]) for n in names}
    shard_shapes = {n: wts[n].shape for n in big}
    dev = 4 * lax.axis_index("x") + 2 * lax.axis_index("y") + lax.axis_index("c")
    core = lax.axis_index("c").astype(jnp.int32).reshape(1)

    conv_local = _pad_rows(jnp.concatenate([wts[n].reshape(-1) for n in conv]), 8).reshape(8, 1024)
    late = [n for n in big if n != "w_in"]
    w_in_all, conv_all = _all_gather([wts["w_in"].astype(BF16), conv_local], "gather_weights")
    late_shards = [wts[n].astype(BF16) for n in late]
    gathering, token = _split_start("gather_late_weights", late_shards,
                                    [lax.empty((N_DEV,) + s.shape, BF16) for s in late_shards], _gather_plan,
                                    7 * len(late), w_in_all)
    full = dict(wts)
    full["w_in"] = _gathered_to_full("w_in", w_in_all)
    full["g_mix"] = _ordered_after(wts["g_mix"], token)
    flying = {}

    def late_weights(after):
        shards, lands = _split_wait("gather_late_weights", gathering, _gather_plan, after)
        return {n: _gathered_to_full(n, lax.dynamic_update_slice(land, shard[None], (dev, 0, 0)))
                for n, land, shard in zip(late, lands, shards)}

    early = ["w_ple_gate", "w_ple_proj", "w_down", "w_up", "w_gate"]

    def ffn_grads(g):
        stacked = [_full_to_stacked(n, g[n], shard_shapes[n]) for n in early]
        flying["sibling"], tok = _split_start("rs_sibling_early", stacked,
                                              [lax.empty((N_CHIP,) + s.shape[2:], BF16) for s in stacked],
                                              _sibling_plan, N_CHIP * len(early), g["w_down"])
        return tok

    def outproj_done(after):
        stacked, landed = _split_wait("rs_sibling_early", flying["sibling"], _sibling_plan, after)
        parts = [_pair_sum(g, l, core, n) for n, g, l in zip(early, stacked, landed)]
        flying["chip"], tok = _split_start("rs_chip_early", parts, [lax.empty(q.shape, BF16) for q in parts],
                                           _chip_plan, 3 * len(early), landed[0])
        return tok

    off = 0
    for n, width in CONV_SHARDED:
        sc = width // N_DEV
        a = conv_all.reshape(N_DEV, -1)[:, off:off + 3 * sc].reshape(N_DEV, 3, sc)
        full[n] = a.transpose(1, 0, 2).reshape(3, width)
        off += 3 * sc

    loss, dx, grads = _local_step(x[0], p[0, 0], loss_target[0], full, (512, 256),
                                  {"late_weights": late_weights, "ffn_grads": ffn_grads, "outproj_done": outproj_done})

    chip = 2 * lax.axis_index("x") + lax.axis_index("y")

    def with_own_slab(parts, arrived):
        return [lax.dynamic_update_slice(land, lax.dynamic_slice(part, (chip, 0, 0), (1,) + part.shape[1:]),
                                         (chip, 0, 0)) for part, land in zip(parts, arrived)]

    last = [n for n in big if n not in early]
    stacked = [_full_to_stacked(n, grads[n], shard_shapes[n]) for n in last]
    flying["sibling_last"], tok = _split_start("rs_sibling_last", stacked,
                                               [lax.empty((N_CHIP,) + s.shape[2:], BF16) for s in stacked],
                                               _sibling_plan, N_CHIP * len(last), dx)
    parts, arrived = _split_wait("rs_chip_early", flying["chip"], _chip_plan, tok)
    big_out = {n: _adamw(c, wts[n], mom[n], var[n], f"adamw_{n}")
               for n, c in zip(early, with_own_slab(parts, arrived))}
    stacked, landed = _split_wait("rs_sibling_last", flying["sibling_last"], _sibling_plan, big_out[early[-1]][0])
    parts = [_pair_sum(g, l, core, n) for n, g, l in zip(last, stacked, landed)]
    flying["chip_last"], tok = _split_start("rs_chip_last", parts, [lax.empty(q.shape, BF16) for q in parts],
                                            _chip_plan, 3 * len(last), landed[0])

    small = jnp.concatenate([grads[n].reshape(-1) for n, _ in REPLICATED] + [grads[n].reshape(-1) for n in conv]
                            + [loss.reshape(1)])
    (small_all,) = _all_gather([_ordered_after(_pad_rows(small, SMALL_ROWS), tok)], "gather_small_grads")
    n_rep = sum(s for _, s in REPLICATED)
    conv_sizes = [3 * w_ // N_DEV for _, w_ in CONV_SHARDED]

    def small_state(src):
        flat = jnp.concatenate([src[n].reshape(-1) for n, _ in REPLICATED] + [src[n].reshape(-1) for n in conv])
        return _pad_rows(flat, 16)

    rep_all = small_all.reshape(N_DEV, -1)[:, :n_rep]
    conv_parts, off = [], n_rep
    for (n, width), size in zip(CONV_SHARDED, conv_sizes):
        sc = width // N_DEV
        a = small_all.reshape(N_DEV, -1)[:, off:off + 3 * width].reshape(N_DEV, 3, width)
        conv_parts.append(lax.dynamic_slice(a, (0, 0, dev * sc), (N_DEV, 3, sc)).reshape(N_DEV, size))
        off += 3 * width
    loss_total = jnp.sum(small_all.reshape(N_DEV, -1)[:, off])
    small_parts = jnp.concatenate([rep_all] + conv_parts, axis=1)
    small_parts = jnp.pad(small_parts, ((0, 0), (0, 16 * 1024 - small_parts.shape[1]))).reshape(N_DEV, 16, 1024)
    g_sm, d_sm, m_sm, v_sm = _adamw(small_parts, small_state(wts), small_state(mom), small_state(var), "adamw_small")
    parts, arrived = _split_wait("rs_chip_last", flying["chip_last"], _chip_plan, g_sm)
    big_out.update({n: _adamw(c, wts[n], mom[n], var[n], f"adamw_{n}")
                    for n, c in zip(last, with_own_slab(parts, arrived))})

    def unpack(which, small_flat):
        out = {n: big_out[n][which] for n in big}
        flat, o = small_flat.reshape(-1), 0
        for n, s in list(REPLICATED) + [(n, sz) for (n, _), sz in zip(CONV_SHARDED, conv_sizes)]:
            out[n] = flat[o:o + s]
            o += s
        return [out[n].reshape(args[n].shape) for n in names]

    return (loss_total, dx[None], *unpack(0, g_sm), *unpack(1, d_sm), *unpack(2, m_sm), *unpack(3, v_sm))
```

```python
import functools

import jax
import jax.numpy as jnp
from jax import lax
from jax.experimental import pallas as pl
from jax.experimental.pallas import tpu as pltpu

F32 = jnp.float32
BF16 = jnp.bfloat16

D_MODEL = 1024
CONV_W = 512
ATTN_W = 512
HEAD_DIM = 64
D_FF = 2816
PLE_DIM = 256
IN_COLS = 3 * CONV_W + 3 * ATTN_W
EPS = 1e-6
QK_BLOCK = 128
DILATIONS = (1, 4, 16)
ATTN_SCALE = HEAD_DIM ** -0.5

ADAM_LR = 0.001
ADAM_B1 = 0.9
ADAM_B2 = 0.999
ADAM_EPS = 1e-08
ADAM_WD = 0.01
ADAM_STEP = 10

N_DEV = 8
N_CHIP = 4
V7X_VMEM_LIMIT = 56 * 1024 * 1024
FF_CHUNKS = 2
FFN_BWD_PARTS = 1

BIG_ROWS = (("w_in", 384), ("w_out", 128), ("w_gate", 352), ("w_up", 352), ("w_down", 352),
            ("w_ple_gate", 128), ("w_ple_proj", 32))
BIG_TOTAL = sum(r for _, r in BIG_ROWS)
SMALL_ROWS = 24


def _cparams(*sem):
    return pltpu.CompilerParams(dimension_semantics=sem, vmem_limit_bytes=V7X_VMEM_LIMIT)


def _mm(a, b):
    return jnp.dot(a, b, preferred_element_type=F32)


def _mm_nt(a, b):
    return lax.dot_general(a, b, (((1,), (1,)), ((), ())), preferred_element_type=F32)


def _mm_tn(a, b):
    return lax.dot_general(a, b, (((0,), (0,)), ((), ())), preferred_element_type=F32)


def _full(shape):
    nd = len(shape)
    return pl.BlockSpec(shape, lambda *_: (0,) * nd)


def _rms_stats(x):
    r = lax.rsqrt(jnp.mean(x * x, axis=-1, keepdims=True) + EPS)
    return r, x * r


def _rms_bwd(dy, xhat, r, g):
    gd = dy * g
    return r * (gd - xhat * jnp.mean(gd * xhat, axis=-1, keepdims=True))


def _seg_sum64(v, bd_ref):
    outs = []
    for c in range(0, v.shape[1], 256):
        vc = v[:, c:c + 256]
        hi = vc.astype(BF16)
        lo = (vc - hi.astype(F32)).astype(BF16)
        outs.append(_mm(hi, bd_ref[...]) + _mm(lo, bd_ref[...]))
    return outs[0] if len(outs) == 1 else jnp.concatenate(outs, axis=1)


def _shift_rows(u, k, edge_rows):
    out = pltpu.roll(u, k, 0)
    row = lax.broadcasted_iota(jnp.int32, (8, u.shape[1]), 0)
    head = out[0:8]
    for j in range(k):
        head = jnp.where(row == j, edge_rows[k - 1 - j], head)
    return jnp.concatenate([head, out[8:]], axis=0)


def _shift_rows_up(u, k, edge_rows):
    n = u.shape[0]
    out = pltpu.roll(u, n - k, 0)
    row = lax.broadcasted_iota(jnp.int32, (8, u.shape[1]), 0)
    tail = out[n - 8:n]
    for j in range(k):
        tail = jnp.where(row == 8 - k + j, edge_rows[j], tail)
    return jnp.concatenate([out[0:n - 8], tail], axis=0)


def _conv_fwd(u, c1, c2, w_ref, b_ref):
    u1 = _shift_rows(u, 1, (c1,))
    u2 = _shift_rows(u, 2, (c1, c2))
    y = u2 * w_ref[0:1, :] + u1 * w_ref[1:2, :] + u * w_ref[2:3, :] + b_ref[...]
    return y, u1, u2


def _conv_bwd_input(dy, n1row, n2row, w_ref):
    d1 = _shift_rows_up(dy, 1, (n1row,))
    d2 = _shift_rows_up(dy, 2, (n1row, n2row))
    return dy * w_ref[2:3, :] + d1 * w_ref[1:2, :] + d2 * w_ref[0:1, :]


def _sigmoid(x):
    return 1.0 / (1.0 + jnp.exp(-x))


def _inproj_fwd(x, g_mix, w_in, conv_w, conv_b, qg, kg, bd, tm):
    t = x.shape[0]

    def body(x_ref, g_ref, w_ref, cw_ref, cb_ref, qg_ref, kg_ref, bd_ref,
             zc_ref, zqk_ref, yc_ref, q_ref, k_ref, v_ref, carry_ref):
        @pl.when(pl.program_id(0) == 0)
        def _():
            carry_ref[...] = jnp.zeros_like(carry_ref)

        _, xhat = _rms_stats(x_ref[...])
        h = (xhat * g_ref[...]).astype(BF16)
        zconv = _mm(h, w_ref[:, 0:3 * CONV_W])
        zc_ref[...] = zconv.astype(BF16)
        u = zconv[:, CONV_W:2 * CONV_W] * zconv[:, 2 * CONV_W:3 * CONV_W]
        cv, _, _ = _conv_fwd(u, carry_ref[7:8, :], carry_ref[6:7, :], cw_ref, cb_ref)
        yc_ref[...] = (zconv[:, 0:CONV_W] * cv).astype(BF16)
        carry_ref[...] = u[tm - 8:tm, :]

        zqk = _mm(h, w_ref[:, 3 * CONV_W:3 * CONV_W + 2 * ATTN_W])
        zqk_ref[...] = zqk.astype(BF16)
        for j, (gain_ref, out_ref, scale) in enumerate(((qg_ref, q_ref, ATTN_SCALE), (kg_ref, k_ref, 1.0))):
            z = zqk[:, j * ATTN_W:(j + 1) * ATTN_W]
            r = lax.rsqrt(_seg_sum64(z * z, bd_ref) * (1.0 / HEAD_DIM) + EPS)
            out_ref[...] = z * r * gain_ref[...] * scale
        v_ref[...] = _mm(h, w_ref[:, 3 * CONV_W + 2 * ATTN_W:IN_COLS])

    def blk(c):
        return pl.BlockSpec((tm, c), lambda i: (i, 0))

    return pl.pallas_call(
        body, name="inproj_fwd", grid=(t // tm,),
        in_specs=[blk(D_MODEL), _full((1, D_MODEL)), _full((D_MODEL, IN_COLS)), _full((3, CONV_W)),
                  _full((1, CONV_W)), _full((1, ATTN_W)), _full((1, ATTN_W)), _full((256, 256))],
        out_specs=[blk(3 * CONV_W), blk(2 * ATTN_W), blk(CONV_W), blk(ATTN_W), blk(ATTN_W), blk(ATTN_W)],
        out_shape=[jax.ShapeDtypeStruct((t, 3 * CONV_W), BF16), jax.ShapeDtypeStruct((t, 2 * ATTN_W), BF16),
                   jax.ShapeDtypeStruct((t, CONV_W), BF16), jax.ShapeDtypeStruct((t, ATTN_W), F32),
                   jax.ShapeDtypeStruct((t, ATTN_W), F32), jax.ShapeDtypeStruct((t, ATTN_W), F32)],
        scratch_shapes=[pltpu.VMEM((8, CONV_W), F32)],
        compiler_params=_cparams("arbitrary"),
    )(x, g_mix, w_in, conv_w, conv_b, qg, kg, bd)


SUPER = 16 * QK_BLOCK
KEYS = 2 * QK_BLOCK


def _rows(start, size, dil):
    return pl.ds(start, size) if dil == 1 else pl.ds(start, size, stride=dil)


def _attn_bias(sl_ref, dil):
    qi = lax.broadcasted_iota(jnp.int32, (KEYS, KEYS), 0)
    kj = lax.broadcasted_iota(jnp.int32, (KEYS, KEYS), 1)
    step = jnp.bitwise_and(qi, QK_BLOCK - 1) + QK_BLOCK - kj
    slope = jnp.where(qi < QK_BLOCK, sl_ref[0, 0:1, 0:1], sl_ref[0, 1:2, 0:1])
    bias = jnp.where(jnp.logical_and(step >= 0, step <= QK_BLOCK), -slope * (step * dil).astype(F32), -jnp.inf)
    return bias, kj >= QK_BLOCK


def _unit_start(u, dil):
    if dil == 1:
        return pl.multiple_of(u * QK_BLOCK, QK_BLOCK)
    if dil == 4:
        return jnp.bitwise_and(u, 3) + (u // 4) * (4 * QK_BLOCK)
    return u


def _stack_heads(a, head0):
    zero = jnp.zeros_like(a)
    return jnp.concatenate([jnp.where(head0, a, zero), jnp.where(head0, zero, a)], axis=0)


def _attn_fwd(q, k, v, slopes):
    t = q.shape[0]
    nsb = t // SUPER

    def body(q_ref, kc_ref, kp_ref, vc_ref, vp_ref, sl_ref, o_ref, l_ref, kk, vv, ob, lb):
        s = pl.program_id(1)
        kk[0:SUPER, :] = kp_ref[...]
        kk[SUPER:, :] = kc_ref[...]
        vv[0:SUPER, :] = vp_ref[...]
        vv[SUPER:, :] = vc_ref[...]
        head0 = lax.broadcasted_iota(jnp.int32, (QK_BLOCK, QK_BLOCK), 1) < HEAD_DIM

        for b, dil in enumerate(DILATIONS):
            bias, own_half = _attn_bias(sl_ref, dil)

            def unit(u, carry, b=b, dil=dil, bias=bias, own_half=own_half):
                start = _unit_start(u, dil)
                first_key = SUPER + start - QK_BLOCK * dil
                q2 = _stack_heads(q_ref[_rows(start, QK_BLOCK, dil), :].astype(BF16), head0)
                k2 = kk[_rows(first_key, KEYS, dil), :].astype(BF16)
                v2 = vv[_rows(first_key, KEYS, dil), :].astype(BF16)
                has_prev = jnp.logical_or(s > 0, start >= QK_BLOCK * dil)
                sc = jnp.where(jnp.logical_or(own_half, has_prev), _mm_nt(q2, k2) + bias, -jnp.inf)
                m = jnp.max(sc, axis=-1, keepdims=True)
                e = jnp.exp(sc - m)
                den = jnp.sum(e, axis=-1, keepdims=True)
                o2 = _mm(e.astype(BF16), v2) / den
                l2 = m + jnp.log(den)
                ob[b, _rows(start, QK_BLOCK, dil), :] = jnp.where(head0, o2[0:QK_BLOCK], o2[QK_BLOCK:])
                lb[b, _rows(start, QK_BLOCK, dil), :] = jnp.where(head0, l2[0:QK_BLOCK], l2[QK_BLOCK:])
                return carry

            lax.fori_loop(0, SUPER // QK_BLOCK, unit, 0, unroll=16)

        def merge(i, carry):
            rows = pl.ds(pl.multiple_of(i * 256, 256), 256)
            la, lb_, lc = lb[0, rows, :], lb[1, rows, :], lb[2, rows, :]
            mx = jnp.maximum(jnp.maximum(la, lb_), lc)
            wa, wb, wc = jnp.exp(la - mx), jnp.exp(lb_ - mx), jnp.exp(lc - mx)
            sw = wa + wb + wc
            o_ref[rows, :] = ((wa * ob[0, rows, :] + wb * ob[1, rows, :] + wc * ob[2, rows, :]) / sw).astype(BF16)
            l_ref[rows, :] = mx + jnp.log(sw)
            return carry

        lax.fori_loop(0, SUPER // 256, merge, 0)

    cur = pl.BlockSpec((SUPER, QK_BLOCK), lambda p, s: (s, p))
    prev = pl.BlockSpec((SUPER, QK_BLOCK), lambda p, s: (jnp.maximum(s - 1, 0), p))
    return pl.pallas_call(
        body, name="attn_fwd", grid=(4, nsb),
        in_specs=[cur, cur, prev, cur, prev, pl.BlockSpec((1, 2, QK_BLOCK), lambda p, s: (p, 0, 0))],
        out_specs=[cur, cur],
        out_shape=[jax.ShapeDtypeStruct((t, ATTN_W), BF16), jax.ShapeDtypeStruct((t, ATTN_W), F32)],
        scratch_shapes=[pltpu.VMEM((2 * SUPER, QK_BLOCK), F32), pltpu.VMEM((2 * SUPER, QK_BLOCK), F32),
                        pltpu.VMEM((3, SUPER, QK_BLOCK), F32), pltpu.VMEM((3, SUPER, QK_BLOCK), F32)],
        compiler_params=_cparams("parallel", "arbitrary"),
    )(q, k, k, v, v, slopes)


def _outproj_fwd(ya, yc, x, goc, goa, w_out, tm):
    t = x.shape[0]

    def body(ya_ref, yc_ref, x_ref, goc_ref, goa_ref, w_ref, x1_ref):
        _, ychat = _rms_stats(yc_ref[...].astype(F32))
        _, yahat = _rms_stats(ya_ref[...].astype(F32))
        acc = _mm((ychat * goc_ref[...]).astype(BF16), w_ref[0:CONV_W, :])
        acc += _mm((yahat * goa_ref[...]).astype(BF16), w_ref[CONV_W:, :])
        x1_ref[...] = x_ref[...] + acc

    def blk(c):
        return pl.BlockSpec((tm, c), lambda i: (i, 0))

    return pl.pallas_call(
        body, name="outproj_fwd", grid=(t // tm,),
        in_specs=[blk(ATTN_W), blk(CONV_W), blk(D_MODEL), _full((1, CONV_W)), _full((1, ATTN_W)),
                  _full((D_MODEL, D_MODEL))],
        out_specs=blk(D_MODEL),
        out_shape=jax.ShapeDtypeStruct((t, D_MODEL), F32),
        compiler_params=_cparams("parallel"),
    )(ya, yc, x, goc, goa, w_out)


def _ffn_fwd(x1, g_ffn, w_gate, w_up, w_down, fcw, fcb, tm):
    t = x1.shape[0]

    def body(x_ref, g_ref, wg_ref, wu_ref, wd_ref, cw_ref, cb_ref, gp_ref, up_ref, h_ref, x2_ref, carry_ref):
        @pl.when(pl.program_id(0) == 0)
        def _():
            carry_ref[...] = jnp.zeros_like(carry_ref)

        xv = x_ref[...]
        _, xhat = _rms_stats(xv)
        h = (xhat * g_ref[...]).astype(BF16)
        h_ref[...] = h
        gp = _mm(h, wg_ref[...])
        gp_ref[...] = gp.astype(BF16)
        gate, _, _ = _conv_fwd(gp, carry_ref[7:8, :], carry_ref[6:7, :], cw_ref, cb_ref)
        carry_ref[...] = gp[tm - 8:tm, :]
        up = _mm(h, wu_ref[...])
        up_ref[...] = up.astype(BF16)
        a = (gate * _sigmoid(gate) * up).astype(BF16)
        x2_ref[...] = xv + _mm(a, wd_ref[...])

    def blk(c):
        return pl.BlockSpec((tm, c), lambda i: (i, 0))

    return pl.pallas_call(
        body, name="ffn_fwd", grid=(t // tm,),
        in_specs=[blk(D_MODEL), _full((1, D_MODEL)), _full((D_MODEL, D_FF)), _full((D_MODEL, D_FF)),
                  _full((D_FF, D_MODEL)), _full((3, D_FF)), _full((1, D_FF))],
        out_specs=[blk(D_FF), blk(D_FF), blk(D_MODEL), blk(D_MODEL)],
        out_shape=[jax.ShapeDtypeStruct((t, D_FF), BF16), jax.ShapeDtypeStruct((t, D_FF), BF16),
                   jax.ShapeDtypeStruct((t, D_MODEL), BF16), jax.ShapeDtypeStruct((t, D_MODEL), F32)],
        scratch_shapes=[pltpu.VMEM((8, D_FF), F32)],
        compiler_params=_cparams("arbitrary"),
    )(x1, g_ffn, w_gate, w_up, w_down, fcw, fcb)


def _ple_fwd_bwd(x2, p, target, g_ple, w_pg, w_pp, tm):
    t = x2.shape[0]

    def body(x_ref, p_ref, t_ref, g_ref, wg_ref, wp_ref, dx_ref, dxb_ref, loss_ref, dwg_ref, dwp_ref, dg_ref):
        @pl.when(pl.program_id(0) == 0)
        def _():
            loss_ref[...] = jnp.zeros_like(loss_ref)
            dwg_ref[...] = jnp.zeros_like(dwg_ref)
            dwp_ref[...] = jnp.zeros_like(dwp_ref)
            dg_ref[...] = jnp.zeros_like(dg_ref)

        xv = x_ref[...]
        r, xhat = _rms_stats(xv)
        g = g_ref[...]
        h = (xhat * g).astype(BF16)
        pg = _sigmoid(_mm(h, wg_ref[...]))
        pb = p_ref[...].astype(BF16)
        pp = _mm(pb, wp_ref[...])
        err = xv + pg * pp - t_ref[...]
        loss_ref[...] += 0.5 * jnp.sum(jnp.mean(err * err, axis=-1, keepdims=True))
        dx3 = err * (1.0 / D_MODEL)
        d_pp = (dx3 * pg).astype(BF16)
        d_pre = (dx3 * pp * pg * (1.0 - pg)).astype(BF16)
        dwp_ref[...] += _mm_tn(pb, d_pp)
        dwg_ref[...] += _mm_tn(h, d_pre)
        dh = _mm_nt(d_pre, wg_ref[...])
        dg_ref[...] += jnp.sum(dh * xhat, axis=0, keepdims=True)
        dx2 = dx3 + _rms_bwd(dh, xhat, r, g)
        dx_ref[...] = dx2
        dxb_ref[...] = dx2.astype(BF16)

    def blk(c):
        return pl.BlockSpec((tm, c), lambda i: (i, 0))

    return pl.pallas_call(
        body, name="ple_fwd_bwd", grid=(t // tm,),
        in_specs=[blk(D_MODEL), blk(PLE_DIM), blk(D_MODEL), _full((1, D_MODEL)), _full((D_MODEL, D_MODEL)),
                  _full((PLE_DIM, D_MODEL))],
        out_specs=[blk(D_MODEL), blk(D_MODEL), _full((8, 128)), _full((D_MODEL, D_MODEL)),
                   _full((PLE_DIM, D_MODEL)), _full((1, D_MODEL))],
        out_shape=[jax.ShapeDtypeStruct((t, D_MODEL), F32), jax.ShapeDtypeStruct((t, D_MODEL), BF16),
                   jax.ShapeDtypeStruct((8, 128), F32),
                   jax.ShapeDtypeStruct((D_MODEL, D_MODEL), F32), jax.ShapeDtypeStruct((PLE_DIM, D_MODEL), F32),
                   jax.ShapeDtypeStruct((1, D_MODEL), F32)],
        compiler_params=_cparams("arbitrary"),
    )(x2, p, target, g_ple, w_pg, w_pp)


def _ffn_bwd(dx2, h2, gp, up, w_gate, w_up, w_down, fcw, fcb, tm):
    t = dx2.shape[0]
    nblk = t // tm
    fc = D_FF // FF_CHUNKS
    half = tm // FFN_BWD_PARTS

    def body(dx_ref, h_ref, gp_ref, gph_ref, up_ref, wg_ref, wu_ref, wd_ref, cw_ref, cb_ref,
             dh_ref, dwd_ref, dwu_ref, dwg_ref, dcw_ref, dcb_ref, carry_ref, a_scr, dup_scr, dgp_scr):
        i = pl.program_id(1)

        @pl.when(i == 0)
        def _():
            carry_ref[...] = jnp.zeros_like(carry_ref)
            dwd_ref[...] = jnp.zeros_like(dwd_ref)
            dwu_ref[...] = jnp.zeros_like(dwu_ref)
            dwg_ref[...] = jnp.zeros_like(dwg_ref)
            dcw_ref[...] = jnp.zeros_like(dcw_ref)
            dcb_ref[...] = jnp.zeros_like(dcb_ref)

        keep = (i < nblk - 1).astype(F32)
        later = carry_ref[...]
        for hf in reversed(range(FFN_BWD_PARTS)):
            rows = slice(hf * half, (hf + 1) * half)
            dxb = dx_ref[rows, :]
            gp_v = gp_ref[rows, :].astype(F32)
            if hf > 0:
                before = gp_ref[hf * half - 16:hf * half, :].astype(F32)
            else:
                before = gph_ref[...].astype(F32) * keep
            gate, gp1, gp2 = _conv_fwd(gp_v, before[15:16, :], before[14:15, :], cw_ref, cb_ref)
            s = _sigmoid(gate)
            silu = gate * s
            up_v = up_ref[rows, :].astype(F32)
            da = _mm_nt(dxb, wd_ref[...])
            a_scr[rows, :] = (silu * up_v).astype(BF16)
            d_up = (da * silu).astype(BF16)
            dup_scr[rows, :] = d_up
            d_gate = da * up_v * (s * (1.0 + gate * (1.0 - s)))
            d_gp = _conv_bwd_input(d_gate, later[0:1, :], later[1:2, :], cw_ref).astype(BF16)
            dgp_scr[rows, :] = d_gp
            later = d_gate[0:8, :]
            dcw_ref[0:1, :] += jnp.sum(d_gate * gp2, axis=0, keepdims=True)
            dcw_ref[1:2, :] += jnp.sum(d_gate * gp1, axis=0, keepdims=True)
            dcw_ref[2:3, :] += jnp.sum(d_gate * gp_v, axis=0, keepdims=True)
            dcb_ref[...] += jnp.sum(d_gate, axis=0, keepdims=True)
            dh_ref[rows, :] = (_mm_nt(d_gp, wg_ref[...]) + _mm_nt(d_up, wu_ref[...])).astype(BF16)
        carry_ref[...] = later
        dwd_ref[...] += _mm_tn(a_scr[...], dx_ref[...])
        dwu_ref[...] += _mm_tn(h_ref[...], dup_scr[...])
        dwg_ref[...] += _mm_tn(h_ref[...], dgp_scr[...])

    def rev(i):
        return nblk - 1 - i

    one = pl.Buffered(1)
    in_specs = [
        pl.BlockSpec((tm, D_MODEL), lambda j, i: (rev(i), 0)),
        pl.BlockSpec((tm, D_MODEL), lambda j, i: (rev(i), 0)),
        pl.BlockSpec((tm, fc), lambda j, i: (rev(i), j)),
        pl.BlockSpec((16, fc), lambda j, i: (jnp.maximum(rev(i) * (tm // 16) - 1, 0), j)),
        pl.BlockSpec((tm, fc), lambda j, i: (rev(i), j)),
        pl.BlockSpec((D_MODEL, fc), lambda j, i: (0, j), pipeline_mode=one),
        pl.BlockSpec((D_MODEL, fc), lambda j, i: (0, j), pipeline_mode=one),
        pl.BlockSpec((fc, D_MODEL), lambda j, i: (j, 0), pipeline_mode=one),
        pl.BlockSpec((3, fc), lambda j, i: (0, j)),
        pl.BlockSpec((1, fc), lambda j, i: (0, j)),
    ]
    out_specs = [
        pl.BlockSpec((None, tm, D_MODEL), lambda j, i: (j, rev(i), 0)),
        pl.BlockSpec((fc, D_MODEL), lambda j, i: (j, 0), pipeline_mode=one),
        pl.BlockSpec((D_MODEL, fc), lambda j, i: (0, j), pipeline_mode=one),
        pl.BlockSpec((D_MODEL, fc), lambda j, i: (0, j), pipeline_mode=one),
        pl.BlockSpec((3, fc), lambda j, i: (0, j)),
        pl.BlockSpec((1, fc), lambda j, i: (0, j)),
    ]
    return pl.pallas_call(
        body, name="ffn_bwd", grid=(FF_CHUNKS, nblk), in_specs=in_specs, out_specs=out_specs,
        out_shape=[jax.ShapeDtypeStruct((FF_CHUNKS, t, D_MODEL), BF16), jax.ShapeDtypeStruct((D_FF, D_MODEL), F32),
                   jax.ShapeDtypeStruct((D_MODEL, D_FF), F32), jax.ShapeDtypeStruct((D_MODEL, D_FF), F32),
                   jax.ShapeDtypeStruct((3, D_FF), F32), jax.ShapeDtypeStruct((1, D_FF), F32)],
        scratch_shapes=[pltpu.VMEM((8, fc), F32), pltpu.VMEM((tm, fc), BF16), pltpu.VMEM((tm, fc), BF16),
                        pltpu.VMEM((tm, fc), BF16)],
        compiler_params=_cparams("arbitrary", "arbitrary"),
    )(dx2, h2, gp, gp, up, w_gate, w_up, w_down, fcw, fcb)


def _outproj_bwd(dh2, dx2, x1, g_ffn, w_out, yc, ya, goc, goa, zconv, conv_w, conv_b, bd, tm):
    t = x1.shape[0]
    nblk = t // tm

    def body(dh_ref, dx2_ref, x1_ref, g_ref, w_ref, yc_ref, ya_ref, goc_ref, goa_ref, zc_ref, zch_ref, cw_ref, cb_ref,
             bd_ref, dx1_ref, dya_ref, dd_ref, dzc_ref, dw_ref, dg_ref, dgoc_ref, dgoa_ref, dcw_ref, dcb_ref,
             carry_ref):
        i = pl.program_id(0)

        @pl.when(i == 0)
        def _():
            carry_ref[...] = jnp.zeros_like(carry_ref)
            for ref in (dw_ref, dg_ref, dgoc_ref, dgoa_ref, dcw_ref, dcb_ref):
                ref[...] = jnp.zeros_like(ref)

        keep = (i < nblk - 1).astype(F32)
        dh2_v = dh_ref[0].astype(F32)
        for j in range(1, FF_CHUNKS):
            dh2_v = dh2_v + dh_ref[j].astype(F32)
        r, xhat = _rms_stats(x1_ref[...])
        dg_ref[...] += jnp.sum(dh2_v * xhat, axis=0, keepdims=True)
        dx1 = dx2_ref[...] + _rms_bwd(dh2_v, xhat, r, g_ref[...])
        dx1_ref[...] = dx1
        dx1b = dx1.astype(BF16)
        dy = _mm_nt(dx1b, w_ref[...])

        yc_v = yc_ref[...].astype(F32)
        rc, ychat = _rms_stats(yc_v)
        dw_ref[0:CONV_W, :] += _mm_tn((ychat * goc_ref[...]).astype(BF16), dx1b)
        dyc = dy[:, 0:CONV_W]
        dgoc_ref[...] += jnp.sum(dyc * ychat, axis=0, keepdims=True)
        d_yc = _rms_bwd(dyc, ychat, rc, goc_ref[...])

        ya_v = ya_ref[...].astype(F32)
        ra, yahat = _rms_stats(ya_v)
        dw_ref[CONV_W:, :] += _mm_tn((yahat * goa_ref[...]).astype(BF16), dx1b)
        dya = dy[:, CONV_W:]
        dgoa_ref[...] += jnp.sum(dya * yahat, axis=0, keepdims=True)
        d_ya = _rms_bwd(dya, yahat, ra, goa_ref[...])
        dya_ref[...] = d_ya
        dd_ref[...] = _seg_sum64(d_ya * ya_v, bd_ref)

        zb = zc_ref[:, 0:CONV_W].astype(F32)
        zc = zc_ref[:, CONV_W:2 * CONV_W].astype(F32)
        zx = zc_ref[:, 2 * CONV_W:3 * CONV_W].astype(F32)
        u = zc * zx
        uh = (zch_ref[:, CONV_W:2 * CONV_W].astype(F32) * zch_ref[:, 2 * CONV_W:3 * CONV_W].astype(F32)) * keep
        cv, u1, u2 = _conv_fwd(u, uh[15:16, :], uh[14:15, :], cw_ref, cb_ref)
        d_cv = d_yc * zb
        d_u = _conv_bwd_input(d_cv, carry_ref[0:1, :], carry_ref[1:2, :], cw_ref)
        carry_ref[...] = d_cv[0:8, :]
        dcw_ref[0:1, :] += jnp.sum(d_cv * u2, axis=0, keepdims=True)
        dcw_ref[1:2, :] += jnp.sum(d_cv * u1, axis=0, keepdims=True)
        dcw_ref[2:3, :] += jnp.sum(d_cv * u, axis=0, keepdims=True)
        dcb_ref[...] += jnp.sum(d_cv, axis=0, keepdims=True)
        dzc_ref[:, 0:CONV_W] = (d_yc * cv).astype(BF16)
        dzc_ref[:, CONV_W:2 * CONV_W] = (d_u * zx).astype(BF16)
        dzc_ref[:, 2 * CONV_W:3 * CONV_W] = (d_u * zc).astype(BF16)

    def rev(i):
        return nblk - 1 - i

    def blk(c):
        return pl.BlockSpec((tm, c), lambda i: (rev(i), 0))

    in_specs = [
        pl.BlockSpec((FF_CHUNKS, tm, D_MODEL), lambda i: (0, rev(i), 0)),
        blk(D_MODEL), blk(D_MODEL), _full((1, D_MODEL)), _full((D_MODEL, D_MODEL)),
        blk(CONV_W), blk(ATTN_W), _full((1, CONV_W)), _full((1, ATTN_W)),
        blk(3 * CONV_W),
        pl.BlockSpec((16, 3 * CONV_W), lambda i: (jnp.maximum(rev(i) * (tm // 16) - 1, 0), 0)),
        _full((3, CONV_W)), _full((1, CONV_W)), _full((256, 256)),
    ]
    out_specs = [blk(D_MODEL), blk(ATTN_W), blk(ATTN_W), blk(3 * CONV_W), _full((D_MODEL, D_MODEL)),
                 _full((1, D_MODEL)), _full((1, CONV_W)), _full((1, ATTN_W)), _full((3, CONV_W)), _full((1, CONV_W))]
    return pl.pallas_call(
        body, name="outproj_bwd", grid=(nblk,), in_specs=in_specs, out_specs=out_specs,
        out_shape=[jax.ShapeDtypeStruct((t, D_MODEL), F32), jax.ShapeDtypeStruct((t, ATTN_W), F32),
                   jax.ShapeDtypeStruct((t, ATTN_W), F32), jax.ShapeDtypeStruct((t, 3 * CONV_W), BF16),
                   jax.ShapeDtypeStruct((D_MODEL, D_MODEL), F32), jax.ShapeDtypeStruct((1, D_MODEL), F32),
                   jax.ShapeDtypeStruct((1, CONV_W), F32), jax.ShapeDtypeStruct((1, ATTN_W), F32),
                   jax.ShapeDtypeStruct((3, CONV_W), F32), jax.ShapeDtypeStruct((1, CONV_W), F32)],
        scratch_shapes=[pltpu.VMEM((8, CONV_W), F32)],
        compiler_params=_cparams("arbitrary"),
    )(dh2, dx2, x1, g_ffn, w_out, yc, ya, goc, goa, zconv, zconv, conv_w, conv_b, bd)


def _attn_bwd(q, k, v, dya, lse, dd, slopes):
    t = q.shape[0]
    nsb = t // SUPER

    def body(q_ref, kc_ref, kp_ref, vc_ref, vp_ref, dy_ref, l_ref, d_ref, sl_ref, dq_ref, dk_ref, dv_ref,
             kk, vv, dkacc, dvacc):
        s = pl.program_id(1)

        @pl.when(s == 0)
        def _():
            dkacc[...] = jnp.zeros_like(dkacc)
            dvacc[...] = jnp.zeros_like(dvacc)

        dkacc[0:SUPER, :] = dkacc[SUPER:, :]
        dvacc[0:SUPER, :] = dvacc[SUPER:, :]
        dkacc[SUPER:, :] = jnp.zeros((SUPER, QK_BLOCK), F32)
        dvacc[SUPER:, :] = jnp.zeros((SUPER, QK_BLOCK), F32)

        @pl.when(s < nsb)
        def _():
            kk[0:SUPER, :] = kp_ref[...]
            kk[SUPER:, :] = kc_ref[...]
            vv[0:SUPER, :] = vp_ref[...]
            vv[SUPER:, :] = vc_ref[...]
            head0 = lax.broadcasted_iota(jnp.int32, (QK_BLOCK, QK_BLOCK), 1) < HEAD_DIM

            for b, dil in enumerate(DILATIONS):
                bias, own_half = _attn_bias(sl_ref, dil)

                def unit(u, carry, b=b, dil=dil, bias=bias, own_half=own_half):
                    start = _unit_start(u, dil)
                    first_key = SUPER + start - QK_BLOCK * dil
                    qrows = _rows(start, QK_BLOCK, dil)
                    krows = _rows(first_key, KEYS, dil)
                    q2 = _stack_heads(q_ref[qrows, :].astype(BF16), head0)
                    dy2 = _stack_heads(dy_ref[qrows, :].astype(BF16), head0)
                    lv, dv_ = l_ref[qrows, :], d_ref[qrows, :]
                    l2 = jnp.concatenate([lv[:, 0:1], lv[:, HEAD_DIM:HEAD_DIM + 1]], axis=0)
                    d2 = jnp.concatenate([dv_[:, 0:1], dv_[:, HEAD_DIM:HEAD_DIM + 1]], axis=0)
                    k2 = kk[krows, :].astype(BF16)
                    v2 = vv[krows, :].astype(BF16)
                    has_prev = jnp.logical_or(s > 0, start >= QK_BLOCK * dil)
                    sc = jnp.where(jnp.logical_or(own_half, has_prev), _mm_nt(q2, k2) + bias, -jnp.inf)
                    prob = jnp.exp(sc - l2)
                    ds = (prob * (_mm_nt(dy2, v2) - d2)).astype(BF16)
                    dvacc[krows, :] += _mm_tn(prob.astype(BF16), dy2)
                    dkacc[krows, :] += _mm_tn(ds, q2)
                    dq2 = _mm(ds, k2)
                    dq = jnp.where(head0, dq2[0:QK_BLOCK], dq2[QK_BLOCK:]) * ATTN_SCALE
                    if b == 0:
                        dq_ref[qrows, :] = dq
                    else:
                        dq_ref[qrows, :] += dq
                    return carry

                lax.fori_loop(0, SUPER // QK_BLOCK, unit, 0, unroll=8)

        dk_ref[...] = dkacc[0:SUPER, :]
        dv_ref[...] = dvacc[0:SUPER, :].astype(BF16)

    def cur_map(p, s):
        return (jnp.minimum(s, nsb - 1), p)

    def prev_map(p, s):
        return (jnp.clip(s - 1, 0, nsb - 1), p)

    cur = pl.BlockSpec((SUPER, QK_BLOCK), cur_map)
    prev = pl.BlockSpec((SUPER, QK_BLOCK), prev_map)
    return pl.pallas_call(
        body, name="attn_bwd", grid=(4, nsb + 1),
        in_specs=[cur, cur, prev, cur, prev, cur, cur, cur, pl.BlockSpec((1, 2, QK_BLOCK), lambda p, s: (p, 0, 0))],
        out_specs=[cur, prev, prev],
        out_shape=[jax.ShapeDtypeStruct((t, ATTN_W), F32), jax.ShapeDtypeStruct((t, ATTN_W), F32),
                   jax.ShapeDtypeStruct((t, ATTN_W), BF16)],
        scratch_shapes=[pltpu.VMEM((2 * SUPER, QK_BLOCK), F32)] * 4,
        compiler_params=_cparams("parallel", "arbitrary"),
    )(q, k, k, v, v, dya, lse, dd, slopes)


def _attn_bwd_per_branch_unused(q, k, v, dya, lse, dd, slopes, dil):
    t = q.shape[0]
    length = t // dil
    chunk = _attn_chunk(t, dil)
    nch = length // chunk
    nb = chunk // QK_BLOCK
    nblocks = length // QK_BLOCK
    view = (length, dil * ATTN_W)
    ext = chunk + QK_BLOCK

    def body(q_ref, dy_ref, l_ref, d_ref, k_ref, v_ref, qn_ref, dyn_ref, ln_ref, dn_ref, kh_ref, vh_ref, sl_ref,
             dq_ref, dk_ref, dv_ref, qbuf, dybuf, lbuf, dbuf, kbuf, vbuf, dkacc, dvacc):
        c = pl.program_id(2)
        qbuf[0:chunk, :] = q_ref[...]
        qbuf[chunk:, :] = qn_ref[...]
        dybuf[0:chunk, :] = dy_ref[...].astype(BF16)
        dybuf[chunk:, :] = dyn_ref[...].astype(BF16)
        lbuf[0:chunk, :] = l_ref[...]
        lbuf[chunk:, :] = ln_ref[...]
        dbuf[0:chunk, :] = d_ref[...]
        dbuf[chunk:, :] = dn_ref[...]
        kbuf[0:QK_BLOCK, :] = kh_ref[...]
        kbuf[QK_BLOCK:, :] = k_ref[...]
        vbuf[0:QK_BLOCK, :] = vh_ref[...]
        vbuf[QK_BLOCK:, :] = v_ref[...]
        valid_cur, valid_prev, dist_cur, dist_prev, head0 = _attn_masks(dil)

        def pair(qb, dyb, lv, dv_, kb, vb, valid, dist):
            dq = jnp.zeros((QK_BLOCK, QK_BLOCK), F32)
            dk = jnp.zeros((QK_BLOCK, QK_BLOCK), F32)
            dvv = jnp.zeros((QK_BLOCK, QK_BLOCK), F32)
            for hh in range(2):
                sl = sl_ref[0, hh:hh + 1, :]
                hm = head0 if hh == 0 else jnp.logical_not(head0)
                col = hh * HEAD_DIM
                qm = jnp.where(hm, qb, jnp.zeros_like(qb))
                dym = jnp.where(hm, dyb, jnp.zeros_like(dyb))
                s = jnp.where(valid, _mm_nt(qm, kb) - sl * dist, -jnp.inf)
                prob = jnp.exp(s - lv[:, col:col + 1])
                ds = (prob * (_mm_nt(dym, vb) - dv_[:, col:col + 1])).astype(BF16)
                dvv += _mm_tn(prob.astype(BF16), dym)
                dk += _mm_tn(ds, qm)
                dq += jnp.where(hm, _mm(ds, kb), 0.0)
            return dq, dk, dvv

        def blk(j, carry):
            off = pl.multiple_of(j * QK_BLOCK, QK_BLOCK)
            nxt = pl.multiple_of(off + QK_BLOCK, QK_BLOCK)
            qb = qbuf[pl.ds(off, QK_BLOCK), :]
            dyb = dybuf[pl.ds(off, QK_BLOCK), :]
            lv = lbuf[pl.ds(off, QK_BLOCK), :]
            dv_ = dbuf[pl.ds(off, QK_BLOCK), :]
            dq_c, dk_c, dv_c = pair(qb, dyb, lv, dv_, kbuf[pl.ds(nxt, QK_BLOCK), :], vbuf[pl.ds(nxt, QK_BLOCK), :],
                                    valid_cur, dist_cur)
            dkacc[pl.ds(nxt, QK_BLOCK), :] = dk_c
            dvacc[pl.ds(nxt, QK_BLOCK), :] = dv_c
            has_prev = jnp.logical_or(c > 0, j > 0)
            dq_p, dk_p, dv_p = pair(qb, dyb, lv, dv_, kbuf[pl.ds(off, QK_BLOCK), :], vbuf[pl.ds(off, QK_BLOCK), :],
                                    jnp.logical_and(valid_prev, has_prev), dist_prev)

            @pl.when(j > 0)
            def _():
                dkacc[pl.ds(off, QK_BLOCK), :] += dk_p
                dvacc[pl.ds(off, QK_BLOCK), :] += dv_p

            dq_ref[pl.ds(off, QK_BLOCK), :] = (dq_c + dq_p) * ATTN_SCALE
            return carry

        lax.fori_loop(0, nb, blk, 0)

        @pl.when(c < nch - 1)
        def _():
            _, dk_p, dv_p = pair(qbuf[chunk:, :], dybuf[chunk:, :], lbuf[chunk:, :], dbuf[chunk:, :],
                                 kbuf[chunk:, :], vbuf[chunk:, :], valid_prev, dist_prev)
            dkacc[chunk:, :] += dk_p
            dvacc[chunk:, :] += dv_p

        dk_ref[...] = dkacc[QK_BLOCK:, :]
        dv_ref[...] = dvacc[QK_BLOCK:, :]

    def cmap(p, r, c):
        return (c, r * 4 + p)

    def before(p, r, c):
        return (jnp.maximum(c * nb - 1, 0), r * 4 + p)

    def after(p, r, c):
        return (jnp.minimum((c + 1) * nb, nblocks - 1), r * 4 + p)

    main = pl.BlockSpec((chunk, QK_BLOCK), cmap)
    hb = pl.BlockSpec((QK_BLOCK, QK_BLOCK), before)
    ha = pl.BlockSpec((QK_BLOCK, QK_BLOCK), after)
    qv, kv, vv = q.reshape(view), k.reshape(view), v.reshape(view)
    dyv, lv, ddv = dya.reshape(view), lse.reshape(view), dd.reshape(view)
    outs = pl.pallas_call(
        body, name=f"attn_bwd_d{dil}", grid=(4, dil, nch),
        in_specs=[main] * 6 + [ha] * 4 + [hb] * 2 + [pl.BlockSpec((1, 2, QK_BLOCK), lambda p, r, c: (p, 0, 0))],
        out_specs=[main] * 3,
        out_shape=[jax.ShapeDtypeStruct(view, F32)] * 3,
        scratch_shapes=[pltpu.VMEM((ext, QK_BLOCK), BF16), pltpu.VMEM((ext, QK_BLOCK), BF16),
                        pltpu.VMEM((ext, QK_BLOCK), F32), pltpu.VMEM((ext, QK_BLOCK), F32),
                        pltpu.VMEM((ext, QK_BLOCK), BF16), pltpu.VMEM((ext, QK_BLOCK), BF16),
                        pltpu.VMEM((ext, QK_BLOCK), F32), pltpu.VMEM((ext, QK_BLOCK), F32)],
        compiler_params=_cparams("arbitrary", "arbitrary", "arbitrary"),
    )(qv, dyv, lv, ddv, kv, vv, qv, dyv, lv, ddv, kv, vv, slopes)
    return [o.reshape(t, ATTN_W) for o in outs]


def _inproj_bwd(dq, dk, dv, dzconv, zqk, x, dx1, g_mix, w_in, qg, kg, bd, tm):
    t = x.shape[0]

    def body(dq_ref, dk_ref, dv_ref, dzc_ref, zqk_ref, x_ref, dx1_ref, g_ref, w_ref, qg_ref,
             kg_ref, bd_ref, dx_ref, dw_ref, dg_ref, dqg_ref, dkg_ref):
        @pl.when(pl.program_id(0) == 0)
        def _():
            for ref in (dw_ref, dg_ref, dqg_ref, dkg_ref):
                ref[...] = jnp.zeros_like(ref)

        parts = [dzc_ref[...]]
        for j, (dn_ref, gain_ref, dgain_ref) in enumerate(((dq_ref, qg_ref, dqg_ref), (dk_ref, kg_ref, dkg_ref))):
            dn = dn_ref[...]
            z = zqk_ref[:, j * ATTN_W:(j + 1) * ATTN_W].astype(F32)
            r = lax.rsqrt(_seg_sum64(z * z, bd_ref) * (1.0 / HEAD_DIM) + EPS)
            zhat = z * r
            dgain_ref[...] += jnp.sum(dn * zhat, axis=0, keepdims=True)
            gd = dn * gain_ref[...]
            parts.append((r * (gd - zhat * (_seg_sum64(gd * zhat, bd_ref) * (1.0 / HEAD_DIM)))).astype(BF16))
        parts.append(dv_ref[...].astype(BF16))
        dz = jnp.concatenate(parts, axis=1)

        r, xhat = _rms_stats(x_ref[...])
        g = g_ref[...]
        dw_ref[...] += _mm_tn((xhat * g).astype(BF16), dz)
        dh = _mm_nt(dz, w_ref[...])
        dg_ref[...] += jnp.sum(dh * xhat, axis=0, keepdims=True)
        dx_ref[...] = dx1_ref[...] + _rms_bwd(dh, xhat, r, g)

    def blk(c):
        return pl.BlockSpec((tm, c), lambda i: (i, 0))

    return pl.pallas_call(
        body, name="inproj_bwd", grid=(t // tm,),
        in_specs=[blk(ATTN_W)] * 3 + [blk(3 * CONV_W), blk(2 * ATTN_W), blk(D_MODEL), blk(D_MODEL), _full((1, D_MODEL)),
                                      _full((D_MODEL, IN_COLS)), _full((1, ATTN_W)), _full((1, ATTN_W)),
                                      _full((256, 256))],
        out_specs=[blk(D_MODEL), _full((D_MODEL, IN_COLS)), _full((1, D_MODEL)), _full((1, ATTN_W)),
                   _full((1, ATTN_W))],
        out_shape=[jax.ShapeDtypeStruct((t, D_MODEL), F32), jax.ShapeDtypeStruct((D_MODEL, IN_COLS), F32),
                   jax.ShapeDtypeStruct((1, D_MODEL), F32), jax.ShapeDtypeStruct((1, ATTN_W), F32),
                   jax.ShapeDtypeStruct((1, ATTN_W), F32)],
        compiler_params=_cparams("arbitrary"),
    )(dq, dk, dv, dzconv, zqk, x, dx1, g_mix, w_in, qg, kg, bd)


def _ordered_after(a, token):
    return a if token is None else a + token[0:1, 0:1].reshape((1,) * a.ndim)


def _local_step(x, p, target, w, tms, hooks=None):
    hooks = hooks or {}
    bd = jnp.kron(jnp.eye(4, dtype=F32), jnp.ones((HEAD_DIM, HEAD_DIM), F32)).astype(BF16)
    qg = jnp.tile(w["q_norm_g"], (1, 8))
    kg = jnp.tile(w["k_norm_g"], (1, 8))
    slopes = jnp.exp2(-jnp.arange(1, 9, dtype=F32))
    slopes = jnp.broadcast_to(slopes.reshape(4, 2, 1), (4, 2, QK_BLOCK))

    zconv, zqk, yc, q, k, v = _inproj_fwd(x, w["g_mix"], w["w_in"], w["conv_w"], w["conv_b"], qg, kg, bd, tms[0])
    ya, lse = _attn_fwd(q, k, v, slopes)
    if "late_weights" in hooks:
        w = {**w, **hooks["late_weights"](lse)}
    x1 = _outproj_fwd(ya, yc, x, w["g_out_conv"], w["g_out_attn"], w["w_out"], tms[0])
    gp, up, h2, x2 = _ffn_fwd(x1, w["g_ffn"], w["w_gate"], w["w_up"], w["w_down"], w["ffn_conv_w"], w["ffn_conv_b"], tms[1])
    dx2, dx2b, loss, dw_pg, dw_pp, dg_ple = _ple_fwd_bwd(x2, p, target, w["g_ple"], w["w_ple_gate"], w["w_ple_proj"], tms[0])
    dh2, dw_down, dw_up, dw_gate, dfcw, dfcb = _ffn_bwd(dx2b, h2, gp, up, w["w_gate"], w["w_up"], w["w_down"],
                                                        w["ffn_conv_w"], w["ffn_conv_b"], tms[0])
    token = None
    if "ffn_grads" in hooks:
        token = hooks["ffn_grads"]({"w_ple_gate": dw_pg, "w_ple_proj": dw_pp, "w_down": dw_down, "w_up": dw_up,
                                    "w_gate": dw_gate})
    dx1, dya, dd, dzconv, dw_out, dg_ffn, dgoc, dgoa, dcw, dcb = _outproj_bwd(
        dh2, dx2, x1, _ordered_after(w["g_ffn"], token), w["w_out"], yc, ya, w["g_out_conv"], w["g_out_attn"], zconv,
        w["conv_w"], w["conv_b"], bd, tms[1])
    token = hooks["outproj_done"](dx1) if "outproj_done" in hooks else None
    dq, dk, dv = _attn_bwd(q, k, v, dya, lse, dd, _ordered_after(slopes, token))
    dx, dw_in, dg_mix, dqg, dkg = _inproj_bwd(dq, dk, dv, dzconv, zqk, x, dx1, w["g_mix"], w["w_in"], qg, kg, bd,
                                              tms[0])
    grads = {
        "g_mix": dg_mix, "w_in": dw_in, "conv_w": dcw, "conv_b": dcb,
        "q_norm_g": dqg.reshape(8, HEAD_DIM).sum(0, keepdims=True),
        "k_norm_g": dkg.reshape(8, HEAD_DIM).sum(0, keepdims=True),
        "g_out_conv": dgoc, "g_out_attn": dgoa, "w_out": dw_out, "g_ffn": dg_ffn, "w_gate": dw_gate, "w_up": dw_up,
        "ffn_conv_w": dfcw, "ffn_conv_b": dfcb, "w_down": dw_down, "g_ple": dg_ple, "w_ple_gate": dw_pg,
        "w_ple_proj": dw_pp,
    }
    return loss[0, 0], dx, grads


ANY = pl.BlockSpec(memory_space=pl.ANY)
MESH = pl.DeviceIdType.MESH


def _all_gather(shards, name):
    n = len(shards)

    def body(*refs):
        ins, outs = refs[:n], refs[n:2 * n]
        send_sems, recv_sems, local_sems = refs[2 * n:]
        x, y, c = lax.axis_index("x"), lax.axis_index("y"), lax.axis_index("c")
        me, sibling = (x, y, c), (x, y, 1 - c)
        chips = [(1 - x, y), (x, 1 - y), (1 - x, 1 - y)]

        def slot(dev):
            return 4 * dev[0] + 2 * dev[1] + dev[2]

        def copy(b, k, block, to, src=None):
            dst = outs[b].at[slot(block)]
            return pltpu.make_async_remote_copy(
                src_ref=dst if src is None else src, dst_ref=dst, send_sem=send_sems.at[b, k],
                recv_sem=recv_sems.at[b, k], device_id=to, device_id_type=MESH)

        mine = [pltpu.make_async_copy(ins[b], outs[b].at[slot(me)], local_sems.at[b]) for b in range(n)]
        first, passed = [], []
        for b in range(n):
            mine[b].start()
            first.append(copy(b, 0, me, sibling, src=ins[b]))
            first += [copy(b, 1 + j, me, (*chip, c), src=ins[b]) for j, chip in enumerate(chips)]
        for cp in first:
            cp.start()
        for j, chip in enumerate(chips):
            for b in range(n):
                copy(b, 1 + j, (*chip, c), me).wait_recv()
                fwd = copy(b, 4 + j, (*chip, c), sibling)
                fwd.start()
                passed.append(fwd)
        for b in range(n):
            copy(b, 0, sibling, me).wait_recv()
            for j, chip in enumerate(chips):
                copy(b, 4 + j, (*chip, 1 - c), me).wait_recv()
        for cp in first + passed:
            cp.wait_send()
        for cp in mine:
            cp.wait()

    return pl.pallas_call(
        body, name=name,
        in_specs=[ANY] * n, out_specs=[ANY] * n,
        out_shape=[jax.ShapeDtypeStruct((N_DEV,) + s.shape, s.dtype) for s in shards],
        scratch_shapes=[pltpu.SemaphoreType.DMA((n, 7)), pltpu.SemaphoreType.DMA((n, 7)),
                        pltpu.SemaphoreType.DMA((n,))],
    )(*shards)


HBM = pl.BlockSpec(memory_space=pltpu.HBM)
SEM = pl.BlockSpec(memory_space=pltpu.SEMAPHORE)
EFFECT = pltpu.SideEffectType.DATAFLOW_SIDE_EFFECTING
FLIPS = ((0, 0, 1), (0, 1, 0), (0, 1, 1), (1, 0, 0), (1, 0, 1), (1, 1, 0), (1, 1, 1))


def _flip_peers():
    pos = (lax.axis_index("x"), lax.axis_index("y"), lax.axis_index("c"))
    return [tuple(1 - a if f else a for a, f in zip(pos, flip)) for flip in FLIPS]


def _hbm(a):
    return pltpu.with_memory_space_constraint(a, pltpu.HBM)


def _split_start(name, srcs, lands, plan, n_copies, after):
    n, m = len(srcs), len(lands)

    def body(*refs):
        send_sems, recv_sems, token = refs[n + m + 1], refs[n + m + 2], refs[-1]
        for i, (src, dst, peer) in enumerate(plan(refs[:n], refs[n:n + m])):
            pltpu.make_async_remote_copy(src_ref=src, dst_ref=dst, send_sem=send_sems.at[i], recv_sem=recv_sems.at[i],
                                         device_id=peer, device_id_type=MESH).start()
        token[...] = jnp.zeros_like(token)

    outs = pl.pallas_call(
        body, name=name + "_start",
        in_specs=[HBM] * (n + m) + [ANY],
        out_specs=[SEM, SEM] + [HBM] * (n + m) + [pl.BlockSpec(memory_space=pltpu.VMEM)],
        out_shape=[pltpu.SemaphoreType.DMA((n_copies,)), pltpu.SemaphoreType.DMA((n_copies,))]
        + [pltpu.HBM(a.shape, a.dtype) for a in list(srcs) + list(lands)] + [jax.ShapeDtypeStruct((8, 128), F32)],
        input_output_aliases={i: 2 + i for i in range(n + m)},
        compiler_params=pltpu.CompilerParams(has_side_effects=EFFECT),
    )(*[_hbm(a) for a in list(srcs) + list(lands)], after)
    return (outs[0], outs[1], outs[2:2 + n], outs[2 + n:2 + n + m]), outs[-1]


def _split_wait(name, started, plan, after):
    send_sems, recv_sems, srcs, lands = started
    n, m = len(srcs), len(lands)

    def body(*refs):
        send_ref, recv_ref = refs[n + m], refs[n + m + 1]
        for i, (src, dst, peer) in enumerate(plan(refs[:n], refs[n:n + m])):
            copy = pltpu.make_async_remote_copy(src_ref=src, dst_ref=dst, send_sem=send_ref.at[i],
                                                recv_sem=recv_ref.at[i], device_id=peer, device_id_type=MESH)
            copy.wait_send()
            copy.wait_recv()

    outs = pl.pallas_call(
        body, name=name + "_wait",
        in_specs=[HBM] * (n + m) + [SEM, SEM, ANY],
        out_specs=[HBM] * (n + m),
        out_shape=[pltpu.HBM(a.shape, a.dtype) for a in list(srcs) + list(lands)],
        input_output_aliases={i: i for i in range(n + m)},
        compiler_params=pltpu.CompilerParams(has_side_effects=EFFECT),
    )(*srcs, *lands, send_sems, recv_sems, after)
    return outs[:n], outs[n:]


def _gather_plan(srcs, lands):
    slot = 4 * lax.axis_index("x") + 2 * lax.axis_index("y") + lax.axis_index("c")
    return [(src, land.at[slot], peer) for src, land in zip(srcs, lands) for peer in _flip_peers()]


def _sibling_plan(srcs, lands):
    x, y, c = lax.axis_index("x"), lax.axis_index("y"), lax.axis_index("c")
    return [(src.at[k, 1 - c], land.at[k], (x, y, 1 - c)) for src, land in zip(srcs, lands) for k in range(N_CHIP)]


def _chip_plan(srcs, lands):
    x, y, c = lax.axis_index("x"), lax.axis_index("y"), lax.axis_index("c")
    return [(src.at[2 * cx + cy], land.at[2 * x + y], (cx, cy, c))
            for src, land in zip(srcs, lands) for cx, cy in ((1 - x, y), (x, 1 - y), (1 - x, 1 - y))]


def _row_tile(rows):
    for tr in range(min(rows, 512), 15, -16):
        if rows % tr == 0:
            return tr
    return rows


def _sibling_exchange(gs):
    n = len(gs)

    def body(*refs):
        g_refs, land_refs = refs[:n], refs[n:2 * n]
        send_sems, recv_sems = refs[2 * n:]
        x, y, c = lax.axis_index("x"), lax.axis_index("y"), lax.axis_index("c")
        copies = [pltpu.make_async_remote_copy(
            src_ref=g_refs[b].at[k, 1 - c], dst_ref=land_refs[b].at[k], send_sem=send_sems.at[b, k],
            recv_sem=recv_sems.at[b, k], device_id=(x, y, 1 - c), device_id_type=MESH)
            for b in range(n) for k in range(N_CHIP)]
        for cp in copies:
            cp.start()
        for cp in copies:
            cp.wait()

    return pl.pallas_call(
        body, name="rs_sibling_exchange", in_specs=[ANY] * n, out_specs=[ANY] * n,
        out_shape=[jax.ShapeDtypeStruct((N_CHIP,) + g.shape[2:], g.dtype) for g in gs],
        scratch_shapes=[pltpu.SemaphoreType.DMA((n, N_CHIP)), pltpu.SemaphoreType.DMA((n, N_CHIP))],
    )(*gs)


def _pair_sum(g, land, core, name):
    rows, cols = land.shape[1:]
    tr = _row_tile(rows)

    def body(c_ref, g_ref, l_ref, o_ref):
        o_ref[...] = (g_ref[...].astype(F32) + l_ref[...].astype(F32)).astype(o_ref.dtype)

    return pl.pallas_call(
        body, name=f"rs_pair_sum_{name}",
        grid_spec=pltpu.PrefetchScalarGridSpec(
            num_scalar_prefetch=1, grid=(N_CHIP, rows // tr),
            in_specs=[pl.BlockSpec((None, None, tr, cols), lambda k, i, c_ref: (k, c_ref[0], i, 0)),
                      pl.BlockSpec((None, tr, cols), lambda k, i, c_ref: (k, i, 0))],
            out_specs=pl.BlockSpec((None, tr, cols), lambda k, i, c_ref: (k, i, 0))),
        out_shape=jax.ShapeDtypeStruct(land.shape, land.dtype),
        compiler_params=_cparams("parallel", "parallel"),
    )(core, g, land)


def _chip_exchange(parts):
    n = len(parts)

    def body(*refs):
        p_refs, land_refs = refs[:n], refs[n:2 * n]
        send_sems, recv_sems, local_sems = refs[2 * n:]
        x, y, c = lax.axis_index("x"), lax.axis_index("y"), lax.axis_index("c")
        mine = 2 * x + y
        chips = [(1 - x, y), (x, 1 - y), (1 - x, 1 - y)]
        own = [pltpu.make_async_copy(p_refs[b].at[mine], land_refs[b].at[mine], local_sems.at[b]) for b in range(n)]
        for cp in own:
            cp.start()
        copies = [pltpu.make_async_remote_copy(
            src_ref=p_refs[b].at[2 * cx + cy], dst_ref=land_refs[b].at[mine], send_sem=send_sems.at[b, j],
            recv_sem=recv_sems.at[b, j], device_id=(cx, cy, c), device_id_type=MESH)
            for b in range(n) for j, (cx, cy) in enumerate(chips)]
        for cp in copies:
            cp.start()
        for b in range(n):
            for j, (cx, cy) in enumerate(chips):
                pltpu.make_async_remote_copy(
                    src_ref=p_refs[b].at[mine], dst_ref=land_refs[b].at[2 * cx + cy], send_sem=send_sems.at[b, j],
                    recv_sem=recv_sems.at[b, j], device_id=(cx, cy, c), device_id_type=MESH).wait_recv()
        for cp in copies:
            cp.wait_send()
        for cp in own:
            cp.wait()

    return pl.pallas_call(
        body, name="rs_chip_exchange", in_specs=[ANY] * n, out_specs=[ANY] * n,
        out_shape=[jax.ShapeDtypeStruct(p.shape, p.dtype) for p in parts],
        scratch_shapes=[pltpu.SemaphoreType.DMA((n, 3)), pltpu.SemaphoreType.DMA((n, 3)),
                        pltpu.SemaphoreType.DMA((n,))],
    )(*parts)


def _adamw(parts, w, m, v, name):
    k, rows, cols = parts.shape
    tr = _row_tile(rows)
    c1 = 1.0 / (1.0 - ADAM_B1 ** ADAM_STEP)
    c2 = 1.0 / (1.0 - ADAM_B2 ** ADAM_STEP)

    def body(p_ref, w_ref, m_ref, v_ref, g_ref, d_ref, nm_ref, nv_ref):
        g = p_ref[0].astype(F32)
        for j in range(1, k):
            g = g + p_ref[j].astype(F32)
        g_ref[...] = g
        nm = ADAM_B1 * m_ref[...] + (1.0 - ADAM_B1) * g
        nv = ADAM_B2 * v_ref[...] + (1.0 - ADAM_B2) * (g * g)
        nm_ref[...] = nm
        nv_ref[...] = nv
        d_ref[...] = -ADAM_LR * ((nm * c1) / (jnp.sqrt(nv * c2) + ADAM_EPS) + ADAM_WD * w_ref[...])

    blk = pl.BlockSpec((tr, cols), lambda i: (i, 0))
    return pl.pallas_call(
        body, name=name, grid=(rows // tr,),
        in_specs=[pl.BlockSpec((k, tr, cols), lambda i: (0, i, 0)), blk, blk, blk],
        out_specs=[blk] * 4, out_shape=[jax.ShapeDtypeStruct((rows, cols), F32)] * 4,
        compiler_params=_cparams("parallel"),
    )(parts, w, m, v)


COL_SHARDED = ("w_in", "w_gate", "w_up", "w_ple_proj")
REPLICATED = (("g_mix", 1024), ("conv_b", 512), ("q_norm_g", 64), ("k_norm_g", 64), ("g_out_conv", 512),
              ("g_out_attn", 512), ("g_ffn", 1024), ("ffn_conv_b", 2816), ("g_ple", 1024))
CONV_SHARDED = (("conv_w", CONV_W), ("ffn_conv_w", D_FF))


def _gathered_to_full(name, gathered):
    if name in COL_SHARDED:
        return gathered.transpose(1, 0, 2).reshape(gathered.shape[1], -1)
    return gathered.reshape(-1, gathered.shape[2])


def _full_to_stacked(name, grad, shard_shape):
    sr, sc = shard_shape
    if name in COL_SHARDED:
        a = grad.reshape(sr, N_DEV, sc).transpose(1, 0, 2)
    else:
        a = grad.reshape(N_DEV, sr, sc)
    return a.astype(BF16).reshape(N_CHIP, 2, sr, sc)


def _pad_rows(vec, rows):
    return jnp.pad(vec, (0, rows * 1024 - vec.shape[0])).reshape(rows, 1024)


def kernel(x, p, g_mix, w_in, conv_w, conv_b, q_norm_g, k_norm_g, g_out_conv, g_out_attn, w_out, g_ffn, w_gate, w_up, ffn_conv_w, ffn_conv_b, w_down, g_ple, w_ple_gate, w_ple_proj, loss_target, m_g_mix, m_w_in, m_conv_w, m_conv_b, m_q_norm_g, m_k_norm_g, m_g_out_conv, m_g_out_attn, m_w_out, m_g_ffn, m_w_gate, m_w_up, m_ffn_conv_w, m_ffn_conv_b, m_w_down, m_g_ple, m_w_ple_gate, m_w_ple_proj, v_g_mix, v_w_in, v_conv_w, v_conv_b, v_q_norm_g, v_k_norm_g, v_g_out_conv, v_g_out_attn, v_w_out, v_g_ffn, v_w_gate, v_w_up, v_ffn_conv_w, v_ffn_conv_b, v_w_down, v_g_ple, v_w_ple_gate, v_w_ple_proj):
    args = dict(locals())
    names = ["g_mix", "w_in", "conv_w", "conv_b", "q_norm_g", "k_norm_g", "g_out_conv", "g_out_attn", "w_out", "g_ffn",
             "w_gate", "w_up", "ffn_conv_w", "ffn_conv_b", "w_down", "g_ple", "w_ple_gate", "w_ple_proj"]
    big = [n for n, _ in BIG_ROWS]
    conv = [n for n, _ in CONV_SHARDED]
    wts = {n: (args[n][0] if n in big or n in conv else args[n]) for n in names}
    mom = {n: (args["m_" + n][0] if n in big or n in conv else args["m_" + n]) for n in names}
    var = {n: (args["v_" + n][0] if n in big or n in conv else args["v_" + n]) for n in names}
    shard_shapes = {n: wts[n].shape for n in big}
    dev = 4 * lax.axis_index("x") + 2 * lax.axis_index("y") + lax.axis_index("c")
    core = lax.axis_index("c").astype(jnp.int32).reshape(1)

    conv_local = _pad_rows(jnp.concatenate([wts[n].reshape(-1) for n in conv]), 8).reshape(8, 1024)
    late = [n for n in big if n != "w_in"]
    w_in_all, conv_all = _all_gather([wts["w_in"].astype(BF16), conv_local], "gather_weights")
    late_shards = [wts[n].astype(BF16) for n in late]
    gathering, token = _split_start("gather_late_weights", late_shards,
                                    [lax.empty((N_DEV,) + s.shape, BF16) for s in late_shards], _gather_plan,
                                    7 * len(late), w_in_all)
    full = dict(wts)
    full["w_in"] = _gathered_to_full("w_in", w_in_all)
    full["g_mix"] = _ordered_after(wts["g_mix"], token)
    flying = {}

    def late_weights(after):
        shards, lands = _split_wait("gather_late_weights", gathering, _gather_plan, after)
        return {n: _gathered_to_full(n, lax.dynamic_update_slice(land, shard[None], (dev, 0, 0)))
                for n, land, shard in zip(late, lands, shards)}

    early = ["w_ple_gate", "w_ple_proj", "w_down", "w_up", "w_gate"]

    def ffn_grads(g):
        stacked = [_full_to_stacked(n, g[n], shard_shapes[n]) for n in early]
        flying["sibling"], tok = _split_start("rs_sibling_early", stacked,
                                              [lax.empty((N_CHIP,) + s.shape[2:], BF16) for s in stacked],
                                              _sibling_plan, N_CHIP * len(early), g["w_down"])
        return tok

    def outproj_done(after):
        stacked, landed = _split_wait("rs_sibling_early", flying["sibling"], _sibling_plan, after)
        parts = [_pair_sum(g, l, core, n) for n, g, l in zip(early, stacked, landed)]
        flying["chip"], tok = _split_start("rs_chip_early", parts, [lax.empty(q.shape, BF16) for q in parts],
                                           _chip_plan, 3 * len(early), landed[0])
        return tok

    off = 0
    for n, width in CONV_SHARDED:
        sc = width // N_DEV
        a = conv_all.reshape(N_DEV, -1)[:, off:off + 3 * sc].reshape(N_DEV, 3, sc)
        full[n] = a.transpose(1, 0, 2).reshape(3, width)
        off += 3 * sc

    loss, dx, grads = _local_step(x[0], p[0, 0], loss_target[0], full, (512, 256),
                                  {"late_weights": late_weights, "ffn_grads": ffn_grads, "outproj_done": outproj_done})

    chip = 2 * lax.axis_index("x") + lax.axis_index("y")

    def with_own_slab(parts, arrived):
        return [lax.dynamic_update_slice(land, lax.dynamic_slice(part, (chip, 0, 0), (1,) + part.shape[1:]),
                                         (chip, 0, 0)) for part, land in zip(parts, arrived)]

    last = [n for n in big if n not in early]
    stacked = [_full_to_stacked(n, grads[n], shard_shapes[n]) for n in last]
    flying["sibling_last"], tok = _split_start("rs_sibling_last", stacked,
                                               [lax.empty((N_CHIP,) + s.shape[2:], BF16) for s in stacked],
                                               _sibling_plan, N_CHIP * len(last), dx)
    parts, arrived = _split_wait("rs_chip_early", flying["chip"], _chip_plan, tok)
    big_out = {n: _adamw(c, wts[n], mom[n], var[n], f"adamw_{n}")
               for n, c in zip(early, with_own_slab(parts, arrived))}
    stacked, landed = _split_wait("rs_sibling_last", flying["sibling_last"], _sibling_plan, big_out[early[-1]][0])
    parts = [_pair_sum(g, l, core, n) for n, g, l in zip(last, stacked, landed)]
    flying["chip_last"], tok = _split_start("rs_chip_last", parts, [lax.empty(q.shape, BF16) for q in parts],
                                            _chip_plan, 3 * len(last), landed[0])

    small = jnp.concatenate([grads[n].reshape(-1) for n, _ in REPLICATED] + [grads[n].reshape(-1) for n in conv]
                            + [loss.reshape(1)])
    (small_all,) = _all_gather([_ordered_after(_pad_rows(small, SMALL_ROWS), tok)], "gather_small_grads")
    n_rep = sum(s for _, s in REPLICATED)
    conv_sizes = [3 * w_ // N_DEV for _, w_ in CONV_SHARDED]

    def small_state(src):
        flat = jnp.concatenate([src[n].reshape(-1) for n, _ in REPLICATED] + [src[n].reshape(-1) for n in conv])
        return _pad_rows(flat, 16)

    rep_all = small_all.reshape(N_DEV, -1)[:, :n_rep]
    conv_parts, off = [], n_rep
    for (n, width), size in zip(CONV_SHARDED, conv_sizes):
        sc = width // N_DEV
        a = small_all.reshape(N_DEV, -1)[:, off:off + 3 * width].reshape(N_DEV, 3, width)
        conv_parts.append(lax.dynamic_slice(a, (0, 0, dev * sc), (N_DEV, 3, sc)).reshape(N_DEV, size))
        off += 3 * width
    loss_total = jnp.sum(small_all.reshape(N_DEV, -1)[:, off])
    small_parts = jnp.concatenate([rep_all] + conv_parts, axis=1)
    small_parts = jnp.pad(small_parts, ((0, 0), (0, 16 * 1024 - small_parts.shape[1]))).reshape(N_DEV, 16, 1024)
    g_sm, d_sm, m_sm, v_sm = _adamw(small_parts, small_state(wts), small_state(mom), small_state(var), "adamw_small")
    parts, arrived = _split_wait("rs_chip_last", flying["chip_last"], _chip_plan, g_sm)
    big_out.update({n: _adamw(c, wts[n], mom[n], var[n], f"adamw_{n}")
                    for n, c in zip(last, with_own_slab(parts, arrived))})

    def unpack(which, small_flat):
        out = {n: big_out[n][which] for n in big}
        flat, o = small_flat.reshape(-1), 0
        for n, s in list(REPLICATED) + [(n, sz) for (n, _), sz in zip(CONV_SHARDED, conv_sizes)]:
            out[n] = flat[o:o + s]
            o += s
        return [out[n].reshape(args[n].shape) for n in names]

    return (loss_total, dx[None], *unpack(0, g_sm), *unpack(1, d_sm), *unpack(2, m_sm), *unpack(3, v_sm))
```

```python
import functools

import jax
import jax.numpy as jnp
from jax import lax
from jax.experimental import pallas as pl
from jax.experimental.pallas import tpu as pltpu

F32 = jnp.float32
BF16 = jnp.bfloat16

D_MODEL = 1024
CONV_W = 512
ATTN_W = 512
HEAD_DIM = 64
D_FF = 2816
PLE_DIM = 256
IN_COLS = 3 * CONV_W + 3 * ATTN_W
EPS = 1e-6
QK_BLOCK = 128
DILATIONS = (1, 4, 16)
ATTN_SCALE = HEAD_DIM ** -0.5

ADAM_LR = 0.001
ADAM_B1 = 0.9
ADAM_B2 = 0.999
ADAM_EPS = 1e-08
ADAM_WD = 0.01
ADAM_STEP = 10

N_DEV = 8
N_CHIP = 4
V7X_VMEM_LIMIT = 56 * 1024 * 1024
FF_CHUNKS = 2
FFN_BWD_PARTS = 1

BIG_ROWS = (("w_in", 384), ("w_out", 128), ("w_gate", 352), ("w_up", 352), ("w_down", 352),
            ("w_ple_gate", 128), ("w_ple_proj", 32))
BIG_TOTAL = sum(r for _, r in BIG_ROWS)
SMALL_ROWS = 24


def _cparams(*sem):
    return pltpu.CompilerParams(dimension_semantics=sem, vmem_limit_bytes=V7X_VMEM_LIMIT)


def _mm(a, b):
    return jnp.dot(a, b, preferred_element_type=F32)


def _mm_nt(a, b):
    return lax.dot_general(a, b, (((1,), (1,)), ((), ())), preferred_element_type=F32)


def _mm_tn(a, b):
    return lax.dot_general(a, b, (((0,), (0,)), ((), ())), preferred_element_type=F32)


def _full(shape):
    nd = len(shape)
    return pl.BlockSpec(shape, lambda *_: (0,) * nd)


def _rms_stats(x):
    r = lax.rsqrt(jnp.mean(x * x, axis=-1, keepdims=True) + EPS)
    return r, x * r


def _rms_bwd(dy, xhat, r, g):
    gd = dy * g
    return r * (gd - xhat * jnp.mean(gd * xhat, axis=-1, keepdims=True))


def _seg_sum64(v, bd_ref):
    outs = []
    for c in range(0, v.shape[1], 256):
        vc = v[:, c:c + 256]
        hi = vc.astype(BF16)
        lo = (vc - hi.astype(F32)).astype(BF16)
        outs.append(_mm(hi, bd_ref[...]) + _mm(lo, bd_ref[...]))
    return outs[0] if len(outs) == 1 else jnp.concatenate(outs, axis=1)


def _shift_rows(u, k, edge_rows):
    out = pltpu.roll(u, k, 0)
    row = lax.broadcasted_iota(jnp.int32, (8, u.shape[1]), 0)
    head = out[0:8]
    for j in range(k):
        head = jnp.where(row == j, edge_rows[k - 1 - j], head)
    return jnp.concatenate([head, out[8:]], axis=0)


def _shift_rows_up(u, k, edge_rows):
    n = u.shape[0]
    out = pltpu.roll(u, n - k, 0)
    row = lax.broadcasted_iota(jnp.int32, (8, u.shape[1]), 0)
    tail = out[n - 8:n]
    for j in range(k):
        tail = jnp.where(row == 8 - k + j, edge_rows[j], tail)
    return jnp.concatenate([out[0:n - 8], tail], axis=0)


def _conv_fwd(u, c1, c2, w_ref, b_ref):
    u1 = _shift_rows(u, 1, (c1,))
    u2 = _shift_rows(u, 2, (c1, c2))
    y = u2 * w_ref[0:1, :] + u1 * w_ref[1:2, :] + u * w_ref[2:3, :] + b_ref[...]
    return y, u1, u2


def _conv_bwd_input(dy, n1row, n2row, w_ref):
    d1 = _shift_rows_up(dy, 1, (n1row,))
    d2 = _shift_rows_up(dy, 2, (n1row, n2row))
    return dy * w_ref[2:3, :] + d1 * w_ref[1:2, :] + d2 * w_ref[0:1, :]


def _sigmoid(x):
    return 1.0 / (1.0 + jnp.exp(-x))


def _inproj_fwd(x, g_mix, w_in, conv_w, conv_b, qg, kg, bd, tm):
    t = x.shape[0]

    def body(x_ref, g_ref, w_ref, cw_ref, cb_ref, qg_ref, kg_ref, bd_ref,
             zc_ref, zqk_ref, yc_ref, q_ref, k_ref, v_ref, carry_ref):
        @pl.when(pl.program_id(0) == 0)
        def _():
            carry_ref[...] = jnp.zeros_like(carry_ref)

        _, xhat = _rms_stats(x_ref[...])
        h = (xhat * g_ref[...]).astype(BF16)
        zconv = _mm(h, w_ref[:, 0:3 * CONV_W])
        zc_ref[...] = zconv.astype(BF16)
        u = zconv[:, CONV_W:2 * CONV_W] * zconv[:, 2 * CONV_W:3 * CONV_W]
        cv, _, _ = _conv_fwd(u, carry_ref[7:8, :], carry_ref[6:7, :], cw_ref, cb_ref)
        yc_ref[...] = (zconv[:, 0:CONV_W] * cv).astype(BF16)
        carry_ref[...] = u[tm - 8:tm, :]

        zqk = _mm(h, w_ref[:, 3 * CONV_W:3 * CONV_W + 2 * ATTN_W])
        zqk_ref[...] = zqk.astype(BF16)
        for j, (gain_ref, out_ref, scale) in enumerate(((qg_ref, q_ref, ATTN_SCALE), (kg_ref, k_ref, 1.0))):
            z = zqk[:, j * ATTN_W:(j + 1) * ATTN_W]
            r = lax.rsqrt(_seg_sum64(z * z, bd_ref) * (1.0 / HEAD_DIM) + EPS)
            out_ref[...] = z * r * gain_ref[...] * scale
        v_ref[...] = _mm(h, w_ref[:, 3 * CONV_W + 2 * ATTN_W:IN_COLS])

    def blk(c):
        return pl.BlockSpec((tm, c), lambda i: (i, 0))

    return pl.pallas_call(
        body, name="inproj_fwd", grid=(t // tm,),
        in_specs=[blk(D_MODEL), _full((1, D_MODEL)), _full((D_MODEL, IN_COLS)), _full((3, CONV_W)),
                  _full((1, CONV_W)), _full((1, ATTN_W)), _full((1, ATTN_W)), _full((256, 256))],
        out_specs=[blk(3 * CONV_W), blk(2 * ATTN_W), blk(CONV_W), blk(ATTN_W), blk(ATTN_W), blk(ATTN_W)],
        out_shape=[jax.ShapeDtypeStruct((t, 3 * CONV_W), BF16), jax.ShapeDtypeStruct((t, 2 * ATTN_W), BF16),
                   jax.ShapeDtypeStruct((t, CONV_W), BF16), jax.ShapeDtypeStruct((t, ATTN_W), F32),
                   jax.ShapeDtypeStruct((t, ATTN_W), F32), jax.ShapeDtypeStruct((t, ATTN_W), F32)],
        scratch_shapes=[pltpu.VMEM((8, CONV_W), F32)],
        compiler_params=_cparams("arbitrary"),
    )(x, g_mix, w_in, conv_w, conv_b, qg, kg, bd)


SUPER = 16 * QK_BLOCK
KEYS = 2 * QK_BLOCK


def _rows(start, size, dil):
    return pl.ds(start, size) if dil == 1 else pl.ds(start, size, stride=dil)


def _attn_bias(sl_ref, dil):
    qi = lax.broadcasted_iota(jnp.int32, (KEYS, KEYS), 0)
    kj = lax.broadcasted_iota(jnp.int32, (KEYS, KEYS), 1)
    step = jnp.bitwise_and(qi, QK_BLOCK - 1) + QK_BLOCK - kj
    slope = jnp.where(qi < QK_BLOCK, sl_ref[0, 0:1, 0:1], sl_ref[0, 1:2, 0:1])
    bias = jnp.where(jnp.logical_and(step >= 0, step <= QK_BLOCK), -slope * (step * dil).astype(F32), -jnp.inf)
    return bias, kj >= QK_BLOCK


def _unit_start(u, dil):
    if dil == 1:
        return pl.multiple_of(u * QK_BLOCK, QK_BLOCK)
    if dil == 4:
        return jnp.bitwise_and(u, 3) + (u // 4) * (4 * QK_BLOCK)
    return u


def _stack_heads(a, head0):
    zero = jnp.zeros_like(a)
    return jnp.concatenate([jnp.where(head0, a, zero), jnp.where(head0, zero, a)], axis=0)


def _attn_fwd(q, k, v, slopes):
    t = q.shape[0]
    nsb = t // SUPER

    def body(q_ref, kc_ref, kp_ref, vc_ref, vp_ref, sl_ref, o_ref, l_ref, kk, vv, ob, lb):
        s = pl.program_id(1)
        kk[0:SUPER, :] = kp_ref[...]
        kk[SUPER:, :] = kc_ref[...]
        vv[0:SUPER, :] = vp_ref[...]
        vv[SUPER:, :] = vc_ref[...]
        head0 = lax.broadcasted_iota(jnp.int32, (QK_BLOCK, QK_BLOCK), 1) < HEAD_DIM

        for b, dil in enumerate(DILATIONS):
            bias, own_half = _attn_bias(sl_ref, dil)

            def unit(u, carry, b=b, dil=dil, bias=bias, own_half=own_half):
                start = _unit_start(u, dil)
                first_key = SUPER + start - QK_BLOCK * dil
                q2 = _stack_heads(q_ref[_rows(start, QK_BLOCK, dil), :].astype(BF16), head0)
                k2 = kk[_rows(first_key, KEYS, dil), :].astype(BF16)
                v2 = vv[_rows(first_key, KEYS, dil), :].astype(BF16)
                has_prev = jnp.logical_or(s > 0, start >= QK_BLOCK * dil)
                sc = jnp.where(jnp.logical_or(own_half, has_prev), _mm_nt(q2, k2) + bias, -jnp.inf)
                m = jnp.max(sc, axis=-1, keepdims=True)
                e = jnp.exp(sc - m)
                den = jnp.sum(e, axis=-1, keepdims=True)
                o2 = _mm(e.astype(BF16), v2) / den
                l2 = m + jnp.log(den)
                ob[b, _rows(start, QK_BLOCK, dil), :] = jnp.where(head0, o2[0:QK_BLOCK], o2[QK_BLOCK:])
                lb[b, _rows(start, QK_BLOCK, dil), :] = jnp.where(head0, l2[0:QK_BLOCK], l2[QK_BLOCK:])
                return carry

            lax.fori_loop(0, SUPER // QK_BLOCK, unit, 0, unroll=16)

        def merge(i, carry):
            rows = pl.ds(pl.multiple_of(i * 256, 256), 256)
            la, lb_, lc = lb[0, rows, :], lb[1, rows, :], lb[2, rows, :]
            mx = jnp.maximum(jnp.maximum(la, lb_), lc)
            wa, wb, wc = jnp.exp(la - mx), jnp.exp(lb_ - mx), jnp.exp(lc - mx)
            sw = wa + wb + wc
            o_ref[rows, :] = ((wa * ob[0, rows, :] + wb * ob[1, rows, :] + wc * ob[2, rows, :]) / sw).astype(BF16)
            l_ref[rows, :] = mx + jnp.log(sw)
            return carry

        lax.fori_loop(0, SUPER // 256, merge, 0)

    cur = pl.BlockSpec((SUPER, QK_BLOCK), lambda p, s: (s, p))
    prev = pl.BlockSpec((SUPER, QK_BLOCK), lambda p, s: (jnp.maximum(s - 1, 0), p))
    return pl.pallas_call(
        body, name="attn_fwd", grid=(4, nsb),
        in_specs=[cur, cur, prev, cur, prev, pl.BlockSpec((1, 2, QK_BLOCK), lambda p, s: (p, 0, 0))],
        out_specs=[cur, cur],
        out_shape=[jax.ShapeDtypeStruct((t, ATTN_W), BF16), jax.ShapeDtypeStruct((t, ATTN_W), F32)],
        scratch_shapes=[pltpu.VMEM((2 * SUPER, QK_BLOCK), F32), pltpu.VMEM((2 * SUPER, QK_BLOCK), F32),
                        pltpu.VMEM((3, SUPER, QK_BLOCK), F32), pltpu.VMEM((3, SUPER, QK_BLOCK), F32)],
        compiler_params=_cparams("parallel", "arbitrary"),
    )(q, k, k, v, v, slopes)


def _outproj_fwd(ya, yc, x, goc, goa, w_out, tm):
    t = x.shape[0]

    def body(ya_ref, yc_ref, x_ref, goc_ref, goa_ref, w_ref, x1_ref):
        _, ychat = _rms_stats(yc_ref[...].astype(F32))
        _, yahat = _rms_stats(ya_ref[...].astype(F32))
        acc = _mm((ychat * goc_ref[...]).astype(BF16), w_ref[0:CONV_W, :])
        acc += _mm((yahat * goa_ref[...]).astype(BF16), w_ref[CONV_W:, :])
        x1_ref[...] = x_ref[...] + acc

    def blk(c):
        return pl.BlockSpec((tm, c), lambda i: (i, 0))

    return pl.pallas_call(
        body, name="outproj_fwd", grid=(t // tm,),
        in_specs=[blk(ATTN_W), blk(CONV_W), blk(D_MODEL), _full((1, CONV_W)), _full((1, ATTN_W)),
                  _full((D_MODEL, D_MODEL))],
        out_specs=blk(D_MODEL),
        out_shape=jax.ShapeDtypeStruct((t, D_MODEL), F32),
        compiler_params=_cparams("parallel"),
    )(ya, yc, x, goc, goa, w_out)


def _ffn_fwd(x1, g_ffn, w_gate, w_up, w_down, fcw, fcb, tm):
    t = x1.shape[0]

    def body(x_ref, g_ref, wg_ref, wu_ref, wd_ref, cw_ref, cb_ref, gp_ref, up_ref, h_ref, x2_ref, carry_ref):
        @pl.when(pl.program_id(0) == 0)
        def _():
            carry_ref[...] = jnp.zeros_like(carry_ref)

        xv = x_ref[...]
        _, xhat = _rms_stats(xv)
        h = (xhat * g_ref[...]).astype(BF16)
        h_ref[...] = h
        gp = _mm(h, wg_ref[...])
        gp_ref[...] = gp.astype(BF16)
        gate, _, _ = _conv_fwd(gp, carry_ref[7:8, :], carry_ref[6:7, :], cw_ref, cb_ref)
        carry_ref[...] = gp[tm - 8:tm, :]
        up = _mm(h, wu_ref[...])
        up_ref[...] = up.astype(BF16)
        a = (gate * _sigmoid(gate) * up).astype(BF16)
        x2_ref[...] = xv + _mm(a, wd_ref[...])

    def blk(c):
        return pl.BlockSpec((tm, c), lambda i: (i, 0))

    return pl.pallas_call(
        body, name="ffn_fwd", grid=(t // tm,),
        in_specs=[blk(D_MODEL), _full((1, D_MODEL)), _full((D_MODEL, D_FF)), _full((D_MODEL, D_FF)),
                  _full((D_FF, D_MODEL)), _full((3, D_FF)), _full((1, D_FF))],
        out_specs=[blk(D_FF), blk(D_FF), blk(D_MODEL), blk(D_MODEL)],
        out_shape=[jax.ShapeDtypeStruct((t, D_FF), BF16), jax.ShapeDtypeStruct((t, D_FF), BF16),
                   jax.ShapeDtypeStruct((t, D_MODEL), BF16), jax.ShapeDtypeStruct((t, D_MODEL), F32)],
        scratch_shapes=[pltpu.VMEM((8, D_FF), F32)],
        compiler_params=_cparams("arbitrary"),
    )(x1, g_ffn, w_gate, w_up, w_down, fcw, fcb)


def _ple_fwd_bwd(x2, p, target, g_ple, w_pg, w_pp, tm):
    t = x2.shape[0]

    def body(x_ref, p_ref, t_ref, g_ref, wg_ref, wp_ref, dx_ref, dxb_ref, loss_ref, dwg_ref, dwp_ref, dg_ref):
        @pl.when(pl.program_id(0) == 0)
        def _():
            loss_ref[...] = jnp.zeros_like(loss_ref)
            dwg_ref[...] = jnp.zeros_like(dwg_ref)
            dwp_ref[...] = jnp.zeros_like(dwp_ref)
            dg_ref[...] = jnp.zeros_like(dg_ref)

        xv = x_ref[...]
        r, xhat = _rms_stats(xv)
        g = g_ref[...]
        h = (xhat * g).astype(BF16)
        pg = _sigmoid(_mm(h, wg_ref[...]))
        pb = p_ref[...].astype(BF16)
        pp = _mm(pb, wp_ref[...])
        err = xv + pg * pp - t_ref[...]
        loss_ref[...] += 0.5 * jnp.sum(jnp.mean(err * err, axis=-1, keepdims=True))
        dx3 = err * (1.0 / D_MODEL)
        d_pp = (dx3 * pg).astype(BF16)
        d_pre = (dx3 * pp * pg * (1.0 - pg)).astype(BF16)
        dwp_ref[...] += _mm_tn(pb, d_pp)
        dwg_ref[...] += _mm_tn(h, d_pre)
        dh = _mm_nt(d_pre, wg_ref[...])
        dg_ref[...] += jnp.sum(dh * xhat, axis=0, keepdims=True)
        dx2 = dx3 + _rms_bwd(dh, xhat, r, g)
        dx_ref[...] = dx2
        dxb_ref[...] = dx2.astype(BF16)

    def blk(c):
        return pl.BlockSpec((tm, c), lambda i: (i, 0))

    return pl.pallas_call(
        body, name="ple_fwd_bwd", grid=(t // tm,),
        in_specs=[blk(D_MODEL), blk(PLE_DIM), blk(D_MODEL), _full((1, D_MODEL)), _full((D_MODEL, D_MODEL)),
                  _full((PLE_DIM, D_MODEL))],
        out_specs=[blk(D_MODEL), blk(D_MODEL), _full((8, 128)), _full((D_MODEL, D_MODEL)),
                   _full((PLE_DIM, D_MODEL)), _full((1, D_MODEL))],
        out_shape=[jax.ShapeDtypeStruct((t, D_MODEL), F32), jax.ShapeDtypeStruct((t, D_MODEL), BF16),
                   jax.ShapeDtypeStruct((8, 128), F32),
                   jax.ShapeDtypeStruct((D_MODEL, D_MODEL), F32), jax.ShapeDtypeStruct((PLE_DIM, D_MODEL), F32),
                   jax.ShapeDtypeStruct((1, D_MODEL), F32)],
        compiler_params=_cparams("arbitrary"),
    )(x2, p, target, g_ple, w_pg, w_pp)


def _ffn_bwd(dx2, h2, gp, up, w_gate, w_up, w_down, fcw, fcb, tm):
    t = dx2.shape[0]
    nblk = t // tm
    fc = D_FF // FF_CHUNKS
    half = tm // FFN_BWD_PARTS

    def body(dx_ref, h_ref, gp_ref, gph_ref, up_ref, wg_ref, wu_ref, wd_ref, cw_ref, cb_ref,
             dh_ref, dwd_ref, dwu_ref, dwg_ref, dcw_ref, dcb_ref, carry_ref, a_scr, dup_scr, dgp_scr):
        i = pl.program_id(1)

        @pl.when(i == 0)
        def _():
            carry_ref[...] = jnp.zeros_like(carry_ref)
            dwd_ref[...] = jnp.zeros_like(dwd_ref)
            dwu_ref[...] = jnp.zeros_like(dwu_ref)
            dwg_ref[...] = jnp.zeros_like(dwg_ref)
            dcw_ref[...] = jnp.zeros_like(dcw_ref)
            dcb_ref[...] = jnp.zeros_like(dcb_ref)

        keep = (i < nblk - 1).astype(F32)
        later = carry_ref[...]
        for hf in reversed(range(FFN_BWD_PARTS)):
            rows = slice(hf * half, (hf + 1) * half)
            dxb = dx_ref[rows, :]
            gp_v = gp_ref[rows, :].astype(F32)
            if hf > 0:
                before = gp_ref[hf * half - 16:hf * half, :].astype(F32)
            else:
                before = gph_ref[...].astype(F32) * keep
            gate, gp1, gp2 = _conv_fwd(gp_v, before[15:16, :], before[14:15, :], cw_ref, cb_ref)
            s = _sigmoid(gate)
            silu = gate * s
            up_v = up_ref[rows, :].astype(F32)
            da = _mm_nt(dxb, wd_ref[...])
            a_scr[rows, :] = (silu * up_v).astype(BF16)
            d_up = (da * silu).astype(BF16)
            dup_scr[rows, :] = d_up
            d_gate = da * up_v * (s * (1.0 + gate * (1.0 - s)))
            d_gp = _conv_bwd_input(d_gate, later[0:1, :], later[1:2, :], cw_ref).astype(BF16)
            dgp_scr[rows, :] = d_gp
            later = d_gate[0:8, :]
            dcw_ref[0:1, :] += jnp.sum(d_gate * gp2, axis=0, keepdims=True)
            dcw_ref[1:2, :] += jnp.sum(d_gate * gp1, axis=0, keepdims=True)
            dcw_ref[2:3, :] += jnp.sum(d_gate * gp_v, axis=0, keepdims=True)
            dcb_ref[...] += jnp.sum(d_gate, axis=0, keepdims=True)
            dh_ref[rows, :] = (_mm_nt(d_gp, wg_ref[...]) + _mm_nt(d_up, wu_ref[...])).astype(BF16)
        carry_ref[...] = later
        dwd_ref[...] += _mm_tn(a_scr[...], dx_ref[...])
        dwu_ref[...] += _mm_tn(h_ref[...], dup_scr[...])
        dwg_ref[...] += _mm_tn(h_ref[...], dgp_scr[...])

    def rev(i):
        return nblk - 1 - i

    one = pl.Buffered(1)
    in_specs = [
        pl.BlockSpec((tm, D_MODEL), lambda j, i: (rev(i), 0)),
        pl.BlockSpec((tm, D_MODEL), lambda j, i: (rev(i), 0)),
        pl.BlockSpec((tm, fc), lambda j, i: (rev(i), j)),
        pl.BlockSpec((16, fc), lambda j, i: (jnp.maximum(rev(i) * (tm // 16) - 1, 0), j)),
        pl.BlockSpec((tm, fc), lambda j, i: (rev(i), j)),
        pl.BlockSpec((D_MODEL, fc), lambda j, i: (0, j), pipeline_mode=one),
        pl.BlockSpec((D_MODEL, fc), lambda j, i: (0, j), pipeline_mode=one),
        pl.BlockSpec((fc, D_MODEL), lambda j, i: (j, 0), pipeline_mode=one),
        pl.BlockSpec((3, fc), lambda j, i: (0, j)),
        pl.BlockSpec((1, fc), lambda j, i: (0, j)),
    ]
    out_specs = [
        pl.BlockSpec((None, tm, D_MODEL), lambda j, i: (j, rev(i), 0)),
        pl.BlockSpec((fc, D_MODEL), lambda j, i: (j, 0), pipeline_mode=one),
        pl.BlockSpec((D_MODEL, fc), lambda j, i: (0, j), pipeline_mode=one),
        pl.BlockSpec((D_MODEL, fc), lambda j, i: (0, j), pipeline_mode=one),
        pl.BlockSpec((3, fc), lambda j, i: (0, j)),
        pl.BlockSpec((1, fc), lambda j, i: (0, j)),
    ]
    return pl.pallas_call(
        body, name="ffn_bwd", grid=(FF_CHUNKS, nblk), in_specs=in_specs, out_specs=out_specs,
        out_shape=[jax.ShapeDtypeStruct((FF_CHUNKS, t, D_MODEL), BF16), jax.ShapeDtypeStruct((D_FF, D_MODEL), F32),
                   jax.ShapeDtypeStruct((D_MODEL, D_FF), F32), jax.ShapeDtypeStruct((D_MODEL, D_FF), F32),
                   jax.ShapeDtypeStruct((3, D_FF), F32), jax.ShapeDtypeStruct((1, D_FF), F32)],
        scratch_shapes=[pltpu.VMEM((8, fc), F32), pltpu.VMEM((tm, fc), BF16), pltpu.VMEM((tm, fc), BF16),
                        pltpu.VMEM((tm, fc), BF16)],
        compiler_params=_cparams("arbitrary", "arbitrary"),
    )(dx2, h2, gp, gp, up, w_gate, w_up, w_down, fcw, fcb)


def _outproj_bwd(dh2, dx2, x1, g_ffn, w_out, yc, ya, goc, goa, zconv, conv_w, conv_b, bd, tm):
    t = x1.shape[0]
    nblk = t // tm

    def body(dh_ref, dx2_ref, x1_ref, g_ref, w_ref, yc_ref, ya_ref, goc_ref, goa_ref, zc_ref, zch_ref, cw_ref, cb_ref,
             bd_ref, dx1_ref, dya_ref, dd_ref, dzc_ref, dw_ref, dg_ref, dgoc_ref, dgoa_ref, dcw_ref, dcb_ref,
             carry_ref):
        i = pl.program_id(0)

        @pl.when(i == 0)
        def _():
            carry_ref[...] = jnp.zeros_like(carry_ref)
            for ref in (dw_ref, dg_ref, dgoc_ref, dgoa_ref, dcw_ref, dcb_ref):
                ref[...] = jnp.zeros_like(ref)

        keep = (i < nblk - 1).astype(F32)
        dh2_v = dh_ref[0].astype(F32)
        for j in range(1, FF_CHUNKS):
            dh2_v = dh2_v + dh_ref[j].astype(F32)
        r, xhat = _rms_stats(x1_ref[...])
        dg_ref[...] += jnp.sum(dh2_v * xhat, axis=0, keepdims=True)
        dx1 = dx2_ref[...] + _rms_bwd(dh2_v, xhat, r, g_ref[...])
        dx1_ref[...] = dx1
        dx1b = dx1.astype(BF16)
        dy = _mm_nt(dx1b, w_ref[...])

        yc_v = yc_ref[...].astype(F32)
        rc, ychat = _rms_stats(yc_v)
        dw_ref[0:CONV_W, :] += _mm_tn((ychat * goc_ref[...]).astype(BF16), dx1b)
        dyc = dy[:, 0:CONV_W]
        dgoc_ref[...] += jnp.sum(dyc * ychat, axis=0, keepdims=True)
        d_yc = _rms_bwd(dyc, ychat, rc, goc_ref[...])

        ya_v = ya_ref[...].astype(F32)
        ra, yahat = _rms_stats(ya_v)
        dw_ref[CONV_W:, :] += _mm_tn((yahat * goa_ref[...]).astype(BF16), dx1b)
        dya = dy[:, CONV_W:]
        dgoa_ref[...] += jnp.sum(dya * yahat, axis=0, keepdims=True)
        d_ya = _rms_bwd(dya, yahat, ra, goa_ref[...])
        dya_ref[...] = d_ya
        dd_ref[...] = _seg_sum64(d_ya * ya_v, bd_ref)

        zb = zc_ref[:, 0:CONV_W].astype(F32)
        zc = zc_ref[:, CONV_W:2 * CONV_W].astype(F32)
        zx = zc_ref[:, 2 * CONV_W:3 * CONV_W].astype(F32)
        u = zc * zx
        uh = (zch_ref[:, CONV_W:2 * CONV_W].astype(F32) * zch_ref[:, 2 * CONV_W:3 * CONV_W].astype(F32)) * keep
        cv, u1, u2 = _conv_fwd(u, uh[15:16, :], uh[14:15, :], cw_ref, cb_ref)
        d_cv = d_yc * zb
        d_u = _conv_bwd_input(d_cv, carry_ref[0:1, :], carry_ref[1:2, :], cw_ref)
        carry_ref[...] = d_cv[0:8, :]
        dcw_ref[0:1, :] += jnp.sum(d_cv * u2, axis=0, keepdims=True)
        dcw_ref[1:2, :] += jnp.sum(d_cv * u1, axis=0, keepdims=True)
        dcw_ref[2:3, :] += jnp.sum(d_cv * u, axis=0, keepdims=True)
        dcb_ref[...] += jnp.sum(d_cv, axis=0, keepdims=True)
        dzc_ref[:, 0:CONV_W] = (d_yc * cv).astype(BF16)
        dzc_ref[:, CONV_W:2 * CONV_W] = (d_u * zx).astype(BF16)
        dzc_ref[:, 2 * CONV_W:3 * CONV_W] = (d_u * zc).astype(BF16)

    def rev(i):
        return nblk - 1 - i

    def blk(c):
        return pl.BlockSpec((tm, c), lambda i: (rev(i), 0))

    in_specs = [
        pl.BlockSpec((FF_CHUNKS, tm, D_MODEL), lambda i: (0, rev(i), 0)),
        blk(D_MODEL), blk(D_MODEL), _full((1, D_MODEL)), _full((D_MODEL, D_MODEL)),
        blk(CONV_W), blk(ATTN_W), _full((1, CONV_W)), _full((1, ATTN_W)),
        blk(3 * CONV_W),
        pl.BlockSpec((16, 3 * CONV_W), lambda i: (jnp.maximum(rev(i) * (tm // 16) - 1, 0), 0)),
        _full((3, CONV_W)), _full((1, CONV_W)), _full((256, 256)),
    ]
    out_specs = [blk(D_MODEL), blk(ATTN_W), blk(ATTN_W), blk(3 * CONV_W), _full((D_MODEL, D_MODEL)),
                 _full((1, D_MODEL)), _full((1, CONV_W)), _full((1, ATTN_W)), _full((3, CONV_W)), _full((1, CONV_W))]
    return pl.pallas_call(
        body, name="outproj_bwd", grid=(nblk,), in_specs=in_specs, out_specs=out_specs,
        out_shape=[jax.ShapeDtypeStruct((t, D_MODEL), F32), jax.ShapeDtypeStruct((t, ATTN_W), F32),
                   jax.ShapeDtypeStruct((t, ATTN_W), F32), jax.ShapeDtypeStruct((t, 3 * CONV_W), BF16),
                   jax.ShapeDtypeStruct((D_MODEL, D_MODEL), F32), jax.ShapeDtypeStruct((1, D_MODEL), F32),
                   jax.ShapeDtypeStruct((1, CONV_W), F32), jax.ShapeDtypeStruct((1, ATTN_W), F32),
                   jax.ShapeDtypeStruct((3, CONV_W), F32), jax.ShapeDtypeStruct((1, CONV_W), F32)],
        scratch_shapes=[pltpu.VMEM((8, CONV_W), F32)],
        compiler_params=_cparams("arbitrary"),
    )(dh2, dx2, x1, g_ffn, w_out, yc, ya, goc, goa, zconv, zconv, conv_w, conv_b, bd)


def _attn_bwd(q, k, v, dya, lse, dd, slopes):
    t = q.shape[0]
    nsb = t // SUPER

    def body(q_ref, kc_ref, kp_ref, vc_ref, vp_ref, dy_ref, l_ref, d_ref, sl_ref, dq_ref, dk_ref, dv_ref,
             kk, vv, dkacc, dvacc):
        s = pl.program_id(1)

        @pl.when(s == 0)
        def _():
            dkacc[...] = jnp.zeros_like(dkacc)
            dvacc[...] = jnp.zeros_like(dvacc)

        dkacc[0:SUPER, :] = dkacc[SUPER:, :]
        dvacc[0:SUPER, :] = dvacc[SUPER:, :]
        dkacc[SUPER:, :] = jnp.zeros((SUPER, QK_BLOCK), F32)
        dvacc[SUPER:, :] = jnp.zeros((SUPER, QK_BLOCK), F32)

        @pl.when(s < nsb)
        def _():
            kk[0:SUPER, :] = kp_ref[...]
            kk[SUPER:, :] = kc_ref[...]
            vv[0:SUPER, :] = vp_ref[...]
            vv[SUPER:, :] = vc_ref[...]
            head0 = lax.broadcasted_iota(jnp.int32, (QK_BLOCK, QK_BLOCK), 1) < HEAD_DIM

            for b, dil in enumerate(DILATIONS):
                bias, own_half = _attn_bias(sl_ref, dil)

                def unit(u, carry, b=b, dil=dil, bias=bias, own_half=own_half):
                    start = _unit_start(u, dil)
                    first_key = SUPER + start - QK_BLOCK * dil
                    qrows = _rows(start, QK_BLOCK, dil)
                    krows = _rows(first_key, KEYS, dil)
                    q2 = _stack_heads(q_ref[qrows, :].astype(BF16), head0)
                    dy2 = _stack_heads(dy_ref[qrows, :].astype(BF16), head0)
                    lv, dv_ = l_ref[qrows, :], d_ref[qrows, :]
                    l2 = jnp.concatenate([lv[:, 0:1], lv[:, HEAD_DIM:HEAD_DIM + 1]], axis=0)
                    d2 = jnp.concatenate([dv_[:, 0:1], dv_[:, HEAD_DIM:HEAD_DIM + 1]], axis=0)
                    k2 = kk[krows, :].astype(BF16)
                    v2 = vv[krows, :].astype(BF16)
                    has_prev = jnp.logical_or(s > 0, start >= QK_BLOCK * dil)
                    sc = jnp.where(jnp.logical_or(own_half, has_prev), _mm_nt(q2, k2) + bias, -jnp.inf)
                    prob = jnp.exp(sc - l2)
                    ds = (prob * (_mm_nt(dy2, v2) - d2)).astype(BF16)
                    dvacc[krows, :] += _mm_tn(prob.astype(BF16), dy2)
                    dkacc[krows, :] += _mm_tn(ds, q2)
                    dq2 = _mm(ds, k2)
                    dq = jnp.where(head0, dq2[0:QK_BLOCK], dq2[QK_BLOCK:]) * ATTN_SCALE
                    if b == 0:
                        dq_ref[qrows, :] = dq
                    else:
                        dq_ref[qrows, :] += dq
                    return carry

                lax.fori_loop(0, SUPER // QK_BLOCK, unit, 0, unroll=8)

        dk_ref[...] = dkacc[0:SUPER, :]
        dv_ref[...] = dvacc[0:SUPER, :].astype(BF16)

    def cur_map(p, s):
        return (jnp.minimum(s, nsb - 1), p)

    def prev_map(p, s):
        return (jnp.clip(s - 1, 0, nsb - 1), p)

    cur = pl.BlockSpec((SUPER, QK_BLOCK), cur_map)
    prev = pl.BlockSpec((SUPER, QK_BLOCK), prev_map)
    return pl.pallas_call(
        body, name="attn_bwd", grid=(4, nsb + 1),
        in_specs=[cur, cur, prev, cur, prev, cur, cur, cur, pl.BlockSpec((1, 2, QK_BLOCK), lambda p, s: (p, 0, 0))],
        out_specs=[cur, prev, prev],
        out_shape=[jax.ShapeDtypeStruct((t, ATTN_W), F32), jax.ShapeDtypeStruct((t, ATTN_W), F32),
                   jax.ShapeDtypeStruct((t, ATTN_W), BF16)],
        scratch_shapes=[pltpu.VMEM((2 * SUPER, QK_BLOCK), F32)] * 4,
        compiler_params=_cparams("parallel", "arbitrary"),
    )(q, k, k, v, v, dya, lse, dd, slopes)


def _attn_bwd_per_branch_unused(q, k, v, dya, lse, dd, slopes, dil):
    t = q.shape[0]
    length = t // dil
    chunk = _attn_chunk(t, dil)
    nch = length // chunk
    nb = chunk // QK_BLOCK
    nblocks = length // QK_BLOCK
    view = (length, dil * ATTN_W)
    ext = chunk + QK_BLOCK

    def body(q_ref, dy_ref, l_ref, d_ref, k_ref, v_ref, qn_ref, dyn_ref, ln_ref, dn_ref, kh_ref, vh_ref, sl_ref,
             dq_ref, dk_ref, dv_ref, qbuf, dybuf, lbuf, dbuf, kbuf, vbuf, dkacc, dvacc):
        c = pl.program_id(2)
        qbuf[0:chunk, :] = q_ref[...]
        qbuf[chunk:, :] = qn_ref[...]
        dybuf[0:chunk, :] = dy_ref[...].astype(BF16)
        dybuf[chunk:, :] = dyn_ref[...].astype(BF16)
        lbuf[0:chunk, :] = l_ref[...]
        lbuf[chunk:, :] = ln_ref[...]
        dbuf[0:chunk, :] = d_ref[...]
        dbuf[chunk:, :] = dn_ref[...]
        kbuf[0:QK_BLOCK, :] = kh_ref[...]
        kbuf[QK_BLOCK:, :] = k_ref[...]
        vbuf[0:QK_BLOCK, :] = vh_ref[...]
        vbuf[QK_BLOCK:, :] = v_ref[...]
        valid_cur, valid_prev, dist_cur, dist_prev, head0 = _attn_masks(dil)

        def pair(qb, dyb, lv, dv_, kb, vb, valid, dist):
            dq = jnp.zeros((QK_BLOCK, QK_BLOCK), F32)
            dk = jnp.zeros((QK_BLOCK, QK_BLOCK), F32)
            dvv = jnp.zeros((QK_BLOCK, QK_BLOCK), F32)
            for hh in range(2):
                sl = sl_ref[0, hh:hh + 1, :]
                hm = head0 if hh == 0 else jnp.logical_not(head0)
                col = hh * HEAD_DIM
                qm = jnp.where(hm, qb, jnp.zeros_like(qb))
                dym = jnp.where(hm, dyb, jnp.zeros_like(dyb))
                s = jnp.where(valid, _mm_nt(qm, kb) - sl * dist, -jnp.inf)
                prob = jnp.exp(s - lv[:, col:col + 1])
                ds = (prob * (_mm_nt(dym, vb) - dv_[:, col:col + 1])).astype(BF16)
                dvv += _mm_tn(prob.astype(BF16), dym)
                dk += _mm_tn(ds, qm)
                dq += jnp.where(hm, _mm(ds, kb), 0.0)
            return dq, dk, dvv

        def blk(j, carry):
            off = pl.multiple_of(j * QK_BLOCK, QK_BLOCK)
            nxt = pl.multiple_of(off + QK_BLOCK, QK_BLOCK)
            qb = qbuf[pl.ds(off, QK_BLOCK), :]
            dyb = dybuf[pl.ds(off, QK_BLOCK), :]
            lv = lbuf[pl.ds(off, QK_BLOCK), :]
            dv_ = dbuf[pl.ds(off, QK_BLOCK), :]
            dq_c, dk_c, dv_c = pair(qb, dyb, lv, dv_, kbuf[pl.ds(nxt, QK_BLOCK), :], vbuf[pl.ds(nxt, QK_BLOCK), :],
                                    valid_cur, dist_cur)
            dkacc[pl.ds(nxt, QK_BLOCK), :] = dk_c
            dvacc[pl.ds(nxt, QK_BLOCK), :] = dv_c
            has_prev = jnp.logical_or(c > 0, j > 0)
            dq_p, dk_p, dv_p = pair(qb, dyb, lv, dv_, kbuf[pl.ds(off, QK_BLOCK), :], vbuf[pl.ds(off, QK_BLOCK), :],
                                    jnp.logical_and(valid_prev, has_prev), dist_prev)

            @pl.when(j > 0)
            def _():
                dkacc[pl.ds(off, QK_BLOCK), :] += dk_p
                dvacc[pl.ds(off, QK_BLOCK), :] += dv_p

            dq_ref[pl.ds(off, QK_BLOCK), :] = (dq_c + dq_p) * ATTN_SCALE
            return carry

        lax.fori_loop(0, nb, blk, 0)

        @pl.when(c < nch - 1)
        def _():
            _, dk_p, dv_p = pair(qbuf[chunk:, :], dybuf[chunk:, :], lbuf[chunk:, :], dbuf[chunk:, :],
                                 kbuf[chunk:, :], vbuf[chunk:, :], valid_prev, dist_prev)
            dkacc[chunk:, :] += dk_p
            dvacc[chunk:, :] += dv_p

        dk_ref[...] = dkacc[QK_BLOCK:, :]
        dv_ref[...] = dvacc[QK_BLOCK:, :]

    def cmap(p, r, c):
        return (c, r * 4 + p)

    def before(p, r, c):
        return (jnp.maximum(c * nb - 1, 0), r * 4 + p)

    def after(p, r, c):
        return (jnp.minimum((c + 1) * nb, nblocks - 1), r * 4 + p)

    main = pl.BlockSpec((chunk, QK_BLOCK), cmap)
    hb = pl.BlockSpec((QK_BLOCK, QK_BLOCK), before)
    ha = pl.BlockSpec((QK_BLOCK, QK_BLOCK), after)
    qv, kv, vv = q.reshape(view), k.reshape(view), v.reshape(view)
    dyv, lv, ddv = dya.reshape(view), lse.reshape(view), dd.reshape(view)
    outs = pl.pallas_call(
        body, name=f"attn_bwd_d{dil}", grid=(4, dil, nch),
        in_specs=[main] * 6 + [ha] * 4 + [hb] * 2 + [pl.BlockSpec((1, 2, QK_BLOCK), lambda p, r, c: (p, 0, 0))],
        out_specs=[main] * 3,
        out_shape=[jax.ShapeDtypeStruct(view, F32)] * 3,
        scratch_shapes=[pltpu.VMEM((ext, QK_BLOCK), BF16), pltpu.VMEM((ext, QK_BLOCK), BF16),
                        pltpu.VMEM((ext, QK_BLOCK), F32), pltpu.VMEM((ext, QK_BLOCK), F32),
                        pltpu.VMEM((ext, QK_BLOCK), BF16), pltpu.VMEM((ext, QK_BLOCK), BF16),
                        pltpu.VMEM((ext, QK_BLOCK), F32), pltpu.VMEM((ext, QK_BLOCK), F32)],
        compiler_params=_cparams("arbitrary", "arbitrary", "arbitrary"),
    )(qv, dyv, lv, ddv, kv, vv, qv, dyv, lv, ddv, kv, vv, slopes)
    return [o.reshape(t, ATTN_W) for o in outs]


def _inproj_bwd(dq, dk, dv, dzconv, zqk, x, dx1, g_mix, w_in, qg, kg, bd, tm):
    t = x.shape[0]

    def body(dq_ref, dk_ref, dv_ref, dzc_ref, zqk_ref, x_ref, dx1_ref, g_ref, w_ref, qg_ref,
             kg_ref, bd_ref, dx_ref, dw_ref, dg_ref, dqg_ref, dkg_ref):
        @pl.when(pl.program_id(0) == 0)
        def _():
            for ref in (dw_ref, dg_ref, dqg_ref, dkg_ref):
                ref[...] = jnp.zeros_like(ref)

        parts = [dzc_ref[...]]
        for j, (dn_ref, gain_ref, dgain_ref) in enumerate(((dq_ref, qg_ref, dqg_ref), (dk_ref, kg_ref, dkg_ref))):
            dn = dn_ref[...]
            z = zqk_ref[:, j * ATTN_W:(j + 1) * ATTN_W].astype(F32)
            r = lax.rsqrt(_seg_sum64(z * z, bd_ref) * (1.0 / HEAD_DIM) + EPS)
            zhat = z * r
            dgain_ref[...] += jnp.sum(dn * zhat, axis=0, keepdims=True)
            gd = dn * gain_ref[...]
            parts.append((r * (gd - zhat * (_seg_sum64(gd * zhat, bd_ref) * (1.0 / HEAD_DIM)))).astype(BF16))
        parts.append(dv_ref[...].astype(BF16))
        dz = jnp.concatenate(parts, axis=1)

        r, xhat = _rms_stats(x_ref[...])
        g = g_ref[...]
        dw_ref[...] += _mm_tn((xhat * g).astype(BF16), dz)
        dh = _mm_nt(dz, w_ref[...])
        dg_ref[...] += jnp.sum(dh * xhat, axis=0, keepdims=True)
        dx_ref[...] = dx1_ref[...] + _rms_bwd(dh, xhat, r, g)

    def blk(c):
        return pl.BlockSpec((tm, c), lambda i: (i, 0))

    return pl.pallas_call(
        body, name="inproj_bwd", grid=(t // tm,),
        in_specs=[blk(ATTN_W)] * 3 + [blk(3 * CONV_W), blk(2 * ATTN_W), blk(D_MODEL), blk(D_MODEL), _full((1, D_MODEL)),
                                      _full((D_MODEL, IN_COLS)), _full((1, ATTN_W)), _full((1, ATTN_W)),
                                      _full((256, 256))],
        out_specs=[blk(D_MODEL), _full((D_MODEL, IN_COLS)), _full((1, D_MODEL)), _full((1, ATTN_W)),
                   _full((1, ATTN_W))],
        out_shape=[jax.ShapeDtypeStruct((t, D_MODEL), F32), jax.ShapeDtypeStruct((D_MODEL, IN_COLS), F32),
                   jax.ShapeDtypeStruct((1, D_MODEL), F32), jax.ShapeDtypeStruct((1, ATTN_W), F32),
                   jax.ShapeDtypeStruct((1, ATTN_W), F32)],
        compiler_params=_cparams("arbitrary"),
    )(dq, dk, dv, dzconv, zqk, x, dx1, g_mix, w_in, qg, kg, bd)


def _ordered_after(a, token):
    return a if token is None else a + token[0:1, 0:1].reshape((1,) * a.ndim)


def _local_step(x, p, target, w, tms, hooks=None):
    hooks = hooks or {}
    bd = jnp.kron(jnp.eye(4, dtype=F32), jnp.ones((HEAD_DIM, HEAD_DIM), F32)).astype(BF16)
    qg = jnp.tile(w["q_norm_g"], (1, 8))
    kg = jnp.tile(w["k_norm_g"], (1, 8))
    slopes = jnp.exp2(-jnp.arange(1, 9, dtype=F32))
    slopes = jnp.broadcast_to(slopes.reshape(4, 2, 1), (4, 2, QK_BLOCK))

    zconv, zqk, yc, q, k, v = _inproj_fwd(x, w["g_mix"], w["w_in"], w["conv_w"], w["conv_b"], qg, kg, bd, tms[0])
    ya, lse = _attn_fwd(q, k, v, slopes)
    if "late_weights" in hooks:
        w = {**w, **hooks["late_weights"](lse)}
    x1 = _outproj_fwd(ya, yc, x, w["g_out_conv"], w["g_out_attn"], w["w_out"], tms[0])
    gp, up, h2, x2 = _ffn_fwd(x1, w["g_ffn"], w["w_gate"], w["w_up"], w["w_down"], w["ffn_conv_w"], w["ffn_conv_b"], tms[1])
    dx2, dx2b, loss, dw_pg, dw_pp, dg_ple = _ple_fwd_bwd(x2, p, target, w["g_ple"], w["w_ple_gate"], w["w_ple_proj"], tms[0])
    dh2, dw_down, dw_up, dw_gate, dfcw, dfcb = _ffn_bwd(dx2b, h2, gp, up, w["w_gate"], w["w_up"], w["w_down"],
                                                        w["ffn_conv_w"], w["ffn_conv_b"], tms[0])
    token = None
    if "ffn_grads" in hooks:
        token = hooks["ffn_grads"]({"w_ple_gate": dw_pg, "w_ple_proj": dw_pp, "w_down": dw_down, "w_up": dw_up,
                                    "w_gate": dw_gate})
    dx1, dya, dd, dzconv, dw_out, dg_ffn, dgoc, dgoa, dcw, dcb = _outproj_bwd(
        dh2, dx2, x1, _ordered_after(w["g_ffn"], token), w["w_out"], yc, ya, w["g_out_conv"], w["g_out_attn"], zconv,
        w["conv_w"], w["conv_b"], bd, tms[1])
    token = hooks["outproj_done"](dx1) if "outproj_done" in hooks else None
    dq, dk, dv = _attn_bwd(q, k, v, dya, lse, dd, _ordered_after(slopes, token))
    dx, dw_in, dg_mix, dqg, dkg = _inproj_bwd(dq, dk, dv, dzconv, zqk, x, dx1, w["g_mix"], w["w_in"], qg, kg, bd,
                                              tms[0])
    grads = {
        "g_mix": dg_mix, "w_in": dw_in, "conv_w": dcw, "conv_b": dcb,
        "q_norm_g": dqg.reshape(8, HEAD_DIM).sum(0, keepdims=True),
        "k_norm_g": dkg.reshape(8, HEAD_DIM).sum(0, keepdims=True),
        "g_out_conv": dgoc, "g_out_attn": dgoa, "w_out": dw_out, "g_ffn": dg_ffn, "w_gate": dw_gate, "w_up": dw_up,
        "ffn_conv_w": dfcw, "ffn_conv_b": dfcb, "w_down": dw_down, "g_ple": dg_ple, "w_ple_gate": dw_pg,
        "w_ple_proj": dw_pp,
    }
    return loss, dx, grads


ANY = pl.BlockSpec(memory_space=pl.ANY)
MESH = pl.DeviceIdType.MESH


def _all_gather(shards, name):
    n = len(shards)

    def body(*refs):
        ins, outs = refs[:n], refs[n:2 * n]
        send_sems, recv_sems, local_sems = refs[2 * n:]
        x, y, c = lax.axis_index("x"), lax.axis_index("y"), lax.axis_index("c")
        me, sibling = (x, y, c), (x, y, 1 - c)
        chips = [(1 - x, y), (x, 1 - y), (1 - x, 1 - y)]

        def slot(dev):
            return 4 * dev[0] + 2 * dev[1] + dev[2]

        def copy(b, k, block, to, src=None):
            dst = outs[b].at[slot(block)]
            return pltpu.make_async_remote_copy(
                src_ref=dst if src is None else src, dst_ref=dst, send_sem=send_sems.at[b, k],
                recv_sem=recv_sems.at[b, k], device_id=to, device_id_type=MESH)

        mine = [pltpu.make_async_copy(ins[b], outs[b].at[slot(me)], local_sems.at[b]) for b in range(n)]
        first, passed = [], []
        for b in range(n):
            mine[b].start()
            first.append(copy(b, 0, me, sibling, src=ins[b]))
            first += [copy(b, 1 + j, me, (*chip, c), src=ins[b]) for j, chip in enumerate(chips)]
        for cp in first:
            cp.start()
        for j, chip in enumerate(chips):
            for b in range(n):
                copy(b, 1 + j, (*chip, c), me).wait_recv()
                fwd = copy(b, 4 + j, (*chip, c), sibling)
                fwd.start()
                passed.append(fwd)
        for b in range(n):
            copy(b, 0, sibling, me).wait_recv()
            for j, chip in enumerate(chips):
                copy(b, 4 + j, (*chip, 1 - c), me).wait_recv()
        for cp in first + passed:
            cp.wait_send()
        for cp in mine:
            cp.wait()

    return pl.pallas_call(
        body, name=name,
        in_specs=[ANY] * n, out_specs=[ANY] * n,
        out_shape=[jax.ShapeDtypeStruct((N_DEV,) + s.shape, s.dtype) for s in shards],
        scratch_shapes=[pltpu.SemaphoreType.DMA((n, 7)), pltpu.SemaphoreType.DMA((n, 7)),
                        pltpu.SemaphoreType.DMA((n,))],
    )(*shards)


HBM = pl.BlockSpec(memory_space=pltpu.HBM)
SEM = pl.BlockSpec(memory_space=pltpu.SEMAPHORE)
EFFECT = pltpu.SideEffectType.DATAFLOW_SIDE_EFFECTING
FLIPS = ((0, 0, 1), (0, 1, 0), (0, 1, 1), (1, 0, 0), (1, 0, 1), (1, 1, 0), (1, 1, 1))


def _flip_peers():
    pos = (lax.axis_index("x"), lax.axis_index("y"), lax.axis_index("c"))
    return [tuple(1 - a if f else a for a, f in zip(pos, flip)) for flip in FLIPS]


def _hbm(a):
    return pltpu.with_memory_space_constraint(a, pltpu.HBM)


def _split_start(name, srcs, lands, plan, n_copies, after):
    n, m = len(srcs), len(lands)

    def body(*refs):
        send_sems, recv_sems, token = refs[n + m + 1], refs[n + m + 2], refs[-1]
        for i, (src, dst, peer) in enumerate(plan(refs[:n], refs[n:n + m])):
            pltpu.make_async_remote_copy(src_ref=src, dst_ref=dst, send_sem=send_sems.at[i], recv_sem=recv_sems.at[i],
                                         device_id=peer, device_id_type=MESH).start()
        token[...] = jnp.zeros_like(token)

    outs = pl.pallas_call(
        body, name=name + "_start",
        in_specs=[HBM] * (n + m) + [ANY],
        out_specs=[SEM, SEM] + [HBM] * (n + m) + [pl.BlockSpec(memory_space=pltpu.VMEM)],
        out_shape=[pltpu.SemaphoreType.DMA((n_copies,)), pltpu.SemaphoreType.DMA((n_copies,))]
        + [pltpu.HBM(a.shape, a.dtype) for a in list(srcs) + list(lands)] + [jax.ShapeDtypeStruct((8, 128), F32)],
        input_output_aliases={i: 2 + i for i in range(n + m)},
        compiler_params=pltpu.CompilerParams(has_side_effects=EFFECT),
    )(*[_hbm(a) for a in list(srcs) + list(lands)], after)
    return (outs[0], outs[1], outs[2:2 + n], outs[2 + n:2 + n + m]), outs[-1]


def _split_wait(name, started, plan, after):
    send_sems, recv_sems, srcs, lands = started
    n, m = len(srcs), len(lands)

    def body(*refs):
        send_ref, recv_ref = refs[n + m], refs[n + m + 1]
        for i, (src, dst, peer) in enumerate(plan(refs[:n], refs[n:n + m])):
            copy = pltpu.make_async_remote_copy(src_ref=src, dst_ref=dst, send_sem=send_ref.at[i],
                                                recv_sem=recv_ref.at[i], device_id=peer, device_id_type=MESH)
            copy.wait_send()
            copy.wait_recv()

    outs = pl.pallas_call(
        body, name=name + "_wait",
        in_specs=[HBM] * (n + m) + [SEM, SEM, ANY],
        out_specs=[HBM] * (n + m),
        out_shape=[pltpu.HBM(a.shape, a.dtype) for a in list(srcs) + list(lands)],
        input_output_aliases={i: i for i in range(n + m)},
        compiler_params=pltpu.CompilerParams(has_side_effects=EFFECT),
    )(*srcs, *lands, send_sems, recv_sems, after)
    return outs[:n], outs[n:]


def _gather_plan(srcs, lands):
    slot = 4 * lax.axis_index("x") + 2 * lax.axis_index("y") + lax.axis_index("c")
    return [(src, land.at[slot], peer) for src, land in zip(srcs, lands) for peer in _flip_peers()]


def _sibling_plan(srcs, lands):
    x, y, c = lax.axis_index("x"), lax.axis_index("y"), lax.axis_index("c")
    return [(src.at[k, 1 - c], land.at[k], (x, y, 1 - c)) for src, land in zip(srcs, lands) for k in range(N_CHIP)]


def _chip_plan(srcs, lands):
    x, y, c = lax.axis_index("x"), lax.axis_index("y"), lax.axis_index("c")
    return [(src.at[2 * cx + cy], land.at[2 * x + y], (cx, cy, c))
            for src, land in zip(srcs, lands) for cx, cy in ((1 - x, y), (x, 1 - y), (1 - x, 1 - y))]


def _row_tile(rows):
    for tr in range(min(rows, 512), 15, -16):
        if rows % tr == 0:
            return tr
    return rows


def _sibling_exchange(gs):
    n = len(gs)

    def body(*refs):
        g_refs, land_refs = refs[:n], refs[n:2 * n]
        send_sems, recv_sems = refs[2 * n:]
        x, y, c = lax.axis_index("x"), lax.axis_index("y"), lax.axis_index("c")
        copies = [pltpu.make_async_remote_copy(
            src_ref=g_refs[b].at[k, 1 - c], dst_ref=land_refs[b].at[k], send_sem=send_sems.at[b, k],
            recv_sem=recv_sems.at[b, k], device_id=(x, y, 1 - c), device_id_type=MESH)
            for b in range(n) for k in range(N_CHIP)]
        for cp in copies:
            cp.start()
        for cp in copies:
            cp.wait()

    return pl.pallas_call(
        body, name="rs_sibling_exchange", in_specs=[ANY] * n, out_specs=[ANY] * n,
        out_shape=[jax.ShapeDtypeStruct((N_CHIP,) + g.shape[2:], g.dtype) for g in gs],
        scratch_shapes=[pltpu.SemaphoreType.DMA((n, N_CHIP)), pltpu.SemaphoreType.DMA((n, N_CHIP))],
    )(*gs)


def _pair_sum(g, land, core, name):
    rows, cols = land.shape[1:]
    tr = _row_tile(rows)

    def body(c_ref, g_ref, l_ref, o_ref):
        o_ref[...] = (g_ref[...].astype(F32) + l_ref[...].astype(F32)).astype(o_ref.dtype)

    return pl.pallas_call(
        body, name=f"rs_pair_sum_{name}",
        grid_spec=pltpu.PrefetchScalarGridSpec(
            num_scalar_prefetch=1, grid=(N_CHIP, rows // tr),
            in_specs=[pl.BlockSpec((None, None, tr, cols), lambda k, i, c_ref: (k, c_ref[0], i, 0)),
                      pl.BlockSpec((None, tr, cols), lambda k, i, c_ref: (k, i, 0))],
            out_specs=pl.BlockSpec((None, tr, cols), lambda k, i, c_ref: (k, i, 0))),
        out_shape=jax.ShapeDtypeStruct(land.shape, land.dtype),
        compiler_params=_cparams("parallel", "parallel"),
    )(core, g, land)


def _chip_exchange(parts):
    n = len(parts)

    def body(*refs):
        p_refs, land_refs = refs[:n], refs[n:2 * n]
        send_sems, recv_sems, local_sems = refs[2 * n:]
        x, y, c = lax.axis_index("x"), lax.axis_index("y"), lax.axis_index("c")
        mine = 2 * x + y
        chips = [(1 - x, y), (x, 1 - y), (1 - x, 1 - y)]
        own = [pltpu.make_async_copy(p_refs[b].at[mine], land_refs[b].at[mine], local_sems.at[b]) for b in range(n)]
        for cp in own:
            cp.start()
        copies = [pltpu.make_async_remote_copy(
            src_ref=p_refs[b].at[2 * cx + cy], dst_ref=land_refs[b].at[mine], send_sem=send_sems.at[b, j],
            recv_sem=recv_sems.at[b, j], device_id=(cx, cy, c), device_id_type=MESH)
            for b in range(n) for j, (cx, cy) in enumerate(chips)]
        for cp in copies:
            cp.start()
        for b in range(n):
            for j, (cx, cy) in enumerate(chips):
                pltpu.make_async_remote_copy(
                    src_ref=p_refs[b].at[mine], dst_ref=land_refs[b].at[2 * cx + cy], send_sem=send_sems.at[b, j],
                    recv_sem=recv_sems.at[b, j], device_id=(cx, cy, c), device_id_type=MESH).wait_recv()
        for cp in copies:
            cp.wait_send()
        for cp in own:
            cp.wait()

    return pl.pallas_call(
        body, name="rs_chip_exchange", in_specs=[ANY] * n, out_specs=[ANY] * n,
        out_shape=[jax.ShapeDtypeStruct(p.shape, p.dtype) for p in parts],
        scratch_shapes=[pltpu.SemaphoreType.DMA((n, 3)), pltpu.SemaphoreType.DMA((n, 3)),
                        pltpu.SemaphoreType.DMA((n,))],
    )(*parts)


def _adamw(parts, w, m, v, name):
    k, rows, cols = parts.shape
    tr = _row_tile(rows)
    c1 = 1.0 / (1.0 - ADAM_B1 ** ADAM_STEP)
    c2 = 1.0 / (1.0 - ADAM_B2 ** ADAM_STEP)

    def body(p_ref, w_ref, m_ref, v_ref, g_ref, d_ref, nm_ref, nv_ref):
        g = p_ref[0].astype(F32)
        for j in range(1, k):
            g = g + p_ref[j].astype(F32)
        g_ref[...] = g
        nm = ADAM_B1 * m_ref[...] + (1.0 - ADAM_B1) * g
        nv = ADAM_B2 * v_ref[...] + (1.0 - ADAM_B2) * (g * g)
        nm_ref[...] = nm
        nv_ref[...] = nv
        d_ref[...] = -ADAM_LR * ((nm * c1) / (jnp.sqrt(nv * c2) + ADAM_EPS) + ADAM_WD * w_ref[...])

    blk = pl.BlockSpec((tr, cols), lambda i: (i, 0))
    return pl.pallas_call(
        body, name=name, grid=(rows // tr,),
        in_specs=[pl.BlockSpec((k, tr, cols), lambda i: (0, i, 0)), blk, blk, blk],
        out_specs=[blk] * 4, out_shape=[jax.ShapeDtypeStruct((rows, cols), F32)] * 4,
        compiler_params=_cparams("parallel"),
    )(parts, w, m, v)


SMALL_LAYOUT = (("g_mix", 0, 1024), ("conv_b", 1, 512), ("q_norm_g", 2, 64), ("k_norm_g", 3, 64),
                ("g_out_conv", 4, 512), ("g_out_attn", 5, 512), ("g_ffn", 6, 1024), ("ffn_conv_b", 7, 2816),
                ("g_ple", 10, 1024))
CONV_W_ROW = 11
FFN_CONV_W_ROW = 14
LOSS_ROW = 23


def _row_pieces(cols):
    return [(c, min(1024, cols - c)) for c in range(0, cols, 1024)]


def _pack_small(grads, loss_tile):
    names = [n for n, _, _ in SMALL_LAYOUT]

    def body(*refs):
        ins, cw_ref, fcw_ref, loss_ref, out_ref = refs[:len(names)], refs[-4], refs[-3], refs[-2], refs[-1]
        out_ref[...] = jnp.zeros_like(out_ref)
        for ref, (_, row, cols) in zip(ins, SMALL_LAYOUT):
            for j, (c, width) in enumerate(_row_pieces(cols)):
                out_ref[row + j:row + j + 1, 0:width] = ref[:, c:c + width]
        for k in range(3):
            out_ref[CONV_W_ROW + k:CONV_W_ROW + k + 1, 0:CONV_W] = cw_ref[k:k + 1, :]
            for j, (c, width) in enumerate(_row_pieces(D_FF)):
                row = FFN_CONV_W_ROW + 3 * k + j
                out_ref[row:row + 1, 0:width] = fcw_ref[k:k + 1, c:c + width]
        out_ref[LOSS_ROW:LOSS_ROW + 1, 0:128] = loss_ref[0:1, :]

    return pl.pallas_call(
        body, name="pack_small_grads", out_shape=jax.ShapeDtypeStruct((SMALL_ROWS, 1024), F32),
    )(*[grads[n] for n in names], grads["conv_w"], grads["ffn_conv_w"], loss_tile)


def _adamw_small(arrived, conv_parts, fconv_parts, wts, mom, var):
    names = [n for n, _, _ in SMALL_LAYOUT] + ["conv_w", "ffn_conv_w"]
    c1 = 1.0 / (1.0 - ADAM_B1 ** ADAM_STEP)
    c2 = 1.0 / (1.0 - ADAM_B2 ** ADAM_STEP)
    n = len(names)

    def body(*refs):
        land, cw_ref, fcw_ref = refs[0], refs[1], refs[2]
        state = refs[3:3 + 3 * n]
        outs = refs[3 + 3 * n:]

        def total(piece):
            acc = piece(0)
            for d in range(1, N_DEV):
                acc = acc + piece(d)
            return acc

        for i, name in enumerate(names):
            if name == "conv_w":
                g = total(lambda d: cw_ref[d])
            elif name == "ffn_conv_w":
                g = total(lambda d: fcw_ref[d])
            else:
                _, row, cols = SMALL_LAYOUT[i]
                pieces = [total(lambda d, j=j, width=width: land[d, row + j:row + j + 1, 0:width])
                          for j, (_, width) in enumerate(_row_pieces(cols))]
                g = pieces[0] if len(pieces) == 1 else jnp.concatenate(pieces, axis=1)
            w_ref, m_ref, v_ref = state[3 * i:3 * i + 3]
            nm = ADAM_B1 * m_ref[...] + (1.0 - ADAM_B1) * g
            nv = ADAM_B2 * v_ref[...] + (1.0 - ADAM_B2) * (g * g)
            outs[4 * i][...] = g
            outs[4 * i + 1][...] = -ADAM_LR * ((nm * c1) / (jnp.sqrt(nv * c2) + ADAM_EPS) + ADAM_WD * w_ref[...])
            outs[4 * i + 2][...] = nm
            outs[4 * i + 3][...] = nv
        outs[-1][...] = total(lambda d: land[d, LOSS_ROW:LOSS_ROW + 1, 0:128])

    state = [a[nm_] for nm_ in names for a in (wts, mom, var)]
    shapes = [jax.ShapeDtypeStruct(wts[nm_].shape, F32) for nm_ in names for _ in range(4)]
    outs = pl.pallas_call(
        body, name="adamw_small", out_shape=shapes + [jax.ShapeDtypeStruct((1, 128), F32)],
    )(arrived, conv_parts, fconv_parts, *state)
    return {nm_: tuple(outs[4 * i:4 * i + 4]) for i, nm_ in enumerate(names)}, outs[-1][0, 0]


COL_SHARDED = ("w_in", "w_gate", "w_up", "w_ple_proj")
REPLICATED = (("g_mix", 1024), ("conv_b", 512), ("q_norm_g", 64), ("k_norm_g", 64), ("g_out_conv", 512),
              ("g_out_attn", 512), ("g_ffn", 1024), ("ffn_conv_b", 2816), ("g_ple", 1024))
CONV_SHARDED = (("conv_w", CONV_W), ("ffn_conv_w", D_FF))


def _gathered_to_full(name, gathered):
    if name in COL_SHARDED:
        return gathered.transpose(1, 0, 2).reshape(gathered.shape[1], -1)
    return gathered.reshape(-1, gathered.shape[2])


def _full_to_stacked(name, grad, shard_shape):
    sr, sc = shard_shape
    if name in COL_SHARDED:
        a = grad.reshape(sr, N_DEV, sc).transpose(1, 0, 2)
    else:
        a = grad.reshape(N_DEV, sr, sc)
    return a.astype(BF16).reshape(N_CHIP, 2, sr, sc)


def _pad_rows(vec, rows):
    return jnp.pad(vec, (0, rows * 1024 - vec.shape[0])).reshape(rows, 1024)


def kernel(x, p, g_mix, w_in, conv_w, conv_b, q_norm_g, k_norm_g, g_out_conv, g_out_attn, w_out, g_ffn, w_gate, w_up, ffn_conv_w, ffn_conv_b, w_down, g_ple, w_ple_gate, w_ple_proj, loss_target, m_g_mix, m_w_in, m_conv_w, m_conv_b, m_q_norm_g, m_k_norm_g, m_g_out_conv, m_g_out_attn, m_w_out, m_g_ffn, m_w_gate, m_w_up, m_ffn_conv_w, m_ffn_conv_b, m_w_down, m_g_ple, m_w_ple_gate, m_w_ple_proj, v_g_mix, v_w_in, v_conv_w, v_conv_b, v_q_norm_g, v_k_norm_g, v_g_out_conv, v_g_out_attn, v_w_out, v_g_ffn, v_w_gate, v_w_up, v_ffn_conv_w, v_ffn_conv_b, v_w_down, v_g_ple, v_w_ple_gate, v_w_ple_proj):
    args = dict(locals())
    names = ["g_mix", "w_in", "conv_w", "conv_b", "q_norm_g", "k_norm_g", "g_out_conv", "g_out_attn", "w_out", "g_ffn",
             "w_gate", "w_up", "ffn_conv_w", "ffn_conv_b", "w_down", "g_ple", "w_ple_gate", "w_ple_proj"]
    big = [n for n, _ in BIG_ROWS]
    conv = [n for n, _ in CONV_SHARDED]
    wts = {n: (args[n][0] if n in big or n in conv else args[n]) for n in names}
    mom = {n: (args["m_" + n][0] if n in big or n in conv else args["m_" + n]) for n in names}
    var = {n: (args["v_" + n][0] if n in big or n in conv else args["v_" + n]) for n in names}
    shard_shapes = {n: wts[n].shape for n in big}
    dev = 4 * lax.axis_index("x") + 2 * lax.axis_index("y") + lax.axis_index("c")
    core = lax.axis_index("c").astype(jnp.int32).reshape(1)

    conv_local = _pad_rows(jnp.concatenate([wts[n].reshape(-1) for n in conv]), 8).reshape(8, 1024)
    late = [n for n in big if n != "w_in"]
    w_in_all, conv_all = _all_gather([wts["w_in"].astype(BF16), conv_local], "gather_weights")
    late_shards = [wts[n].astype(BF16) for n in late]
    gathering, token = _split_start("gather_late_weights", late_shards,
                                    [lax.empty((N_DEV,) + s.shape, BF16) for s in late_shards], _gather_plan,
                                    7 * len(late), w_in_all)
    full = dict(wts)
    full["w_in"] = _gathered_to_full("w_in", w_in_all)
    full["g_mix"] = _ordered_after(wts["g_mix"], token)
    flying = {}

    def late_weights(after):
        shards, lands = _split_wait("gather_late_weights", gathering, _gather_plan, after)
        return {n: _gathered_to_full(n, lax.dynamic_update_slice(land, shard[None], (dev, 0, 0)))
                for n, land, shard in zip(late, lands, shards)}

    early = ["w_ple_gate", "w_ple_proj", "w_down", "w_up", "w_gate"]

    def ffn_grads(g):
        stacked = [_full_to_stacked(n, g[n], shard_shapes[n]) for n in early]
        flying["sibling"], tok = _split_start("rs_sibling_early", stacked,
                                              [lax.empty((N_CHIP,) + s.shape[2:], BF16) for s in stacked],
                                              _sibling_plan, N_CHIP * len(early), g["w_down"])
        return tok

    def outproj_done(after):
        stacked, landed = _split_wait("rs_sibling_early", flying["sibling"], _sibling_plan, after)
        parts = [_pair_sum(g, l, core, n) for n, g, l in zip(early, stacked, landed)]
        flying["chip"], tok = _split_start("rs_chip_early", parts, [lax.empty(q.shape, BF16) for q in parts],
                                           _chip_plan, 3 * len(early), landed[0])
        return tok

    off = 0
    for n, width in CONV_SHARDED:
        sc = width // N_DEV
        a = conv_all.reshape(N_DEV, -1)[:, off:off + 3 * sc].reshape(N_DEV, 3, sc)
        full[n] = a.transpose(1, 0, 2).reshape(3, width)
        off += 3 * sc

    loss, dx, grads = _local_step(x[0], p[0, 0], loss_target[0], full, (512, 256),
                                  {"late_weights": late_weights, "ffn_grads": ffn_grads, "outproj_done": outproj_done})

    chip = 2 * lax.axis_index("x") + lax.axis_index("y")

    def with_own_slab(parts, arrived):
        return [lax.dynamic_update_slice(land, lax.dynamic_slice(part, (chip, 0, 0), (1,) + part.shape[1:]),
                                         (chip, 0, 0)) for part, land in zip(parts, arrived)]

    last = [n for n in big if n not in early]
    stacked = [_full_to_stacked(n, grads[n], shard_shapes[n]) for n in last]
    flying["sibling_last"], tok = _split_start("rs_sibling_last", stacked,
                                               [lax.empty((N_CHIP,) + s.shape[2:], BF16) for s in stacked],
                                               _sibling_plan, N_CHIP * len(last), dx)
    (small_all,) = _all_gather([_ordered_after(_pack_small(grads, loss), tok)], "gather_small_grads")
    stacked, landed = _split_wait("rs_sibling_last", flying["sibling_last"], _sibling_plan, small_all)
    parts = [_pair_sum(g, l, core, n) for n, g, l in zip(last, stacked, landed)]
    flying["chip_last"], tok = _split_start("rs_chip_last", parts, [lax.empty(q.shape, BF16) for q in parts],
                                            _chip_plan, 3 * len(last), landed[0])

    parts, arrived = _split_wait("rs_chip_early", flying["chip"], _chip_plan, tok)
    out = {n: _adamw(c, wts[n], mom[n], var[n], f"adamw_{n}") for n, c in zip(early, with_own_slab(parts, arrived))}
    taps = small_all[:, CONV_W_ROW:CONV_W_ROW + 3, 0:CONV_W]
    ftaps = small_all[:, FFN_CONV_W_ROW:FFN_CONV_W_ROW + 9, :].reshape(N_DEV, 3, 3 * 1024)
    small_out, loss_total = _adamw_small(
        small_all, lax.dynamic_slice(taps, (0, 0, dev * (CONV_W // N_DEV)), (N_DEV, 3, CONV_W // N_DEV)),
        lax.dynamic_slice(ftaps, (0, 0, dev * (D_FF // N_DEV)), (N_DEV, 3, D_FF // N_DEV)), wts, mom, var)
    out.update(small_out)
    parts, arrived = _split_wait("rs_chip_last", flying["chip_last"], _chip_plan, small_out["g_mix"][0])
    out.update({n: _adamw(c, wts[n], mom[n], var[n], f"adamw_{n}") for n, c in zip(last, with_own_slab(parts, arrived))})
    return (loss_total, dx[None], *[out[n][which].reshape(args[n].shape) for which in range(4) for n in names])
```

```python
import functools

import jax
import jax.numpy as jnp
from jax import lax
from jax.experimental import pallas as pl
from jax.experimental.pallas import tpu as pltpu

F32 = jnp.float32
BF16 = jnp.bfloat16

D_MODEL = 1024
CONV_W = 512
ATTN_W = 512
HEAD_DIM = 64
D_FF = 2816
PLE_DIM = 256
IN_COLS = 3 * CONV_W + 3 * ATTN_W
EPS = 1e-6
QK_BLOCK = 128
DILATIONS = (1, 4, 16)
ATTN_SCALE = HEAD_DIM ** -0.5

ADAM_LR = 0.001
ADAM_B1 = 0.9
ADAM_B2 = 0.999
ADAM_EPS = 1e-08
ADAM_WD = 0.01
ADAM_STEP = 10

N_DEV = 8
N_CHIP = 4
V7X_VMEM_LIMIT = 56 * 1024 * 1024
FF_CHUNKS = 2
FFN_BWD_PARTS = 1

BIG_ROWS = (("w_in", 384), ("w_out", 128), ("w_gate", 352), ("w_up", 352), ("w_down", 352),
            ("w_ple_gate", 128), ("w_ple_proj", 32))
BIG_TOTAL = sum(r for _, r in BIG_ROWS)
SMALL_ROWS = 24


def _cparams(*sem):
    return pltpu.CompilerParams(dimension_semantics=sem, vmem_limit_bytes=V7X_VMEM_LIMIT)


def _mm(a, b):
    return jnp.dot(a, b, preferred_element_type=F32)


def _mm_nt(a, b):
    return lax.dot_general(a, b, (((1,), (1,)), ((), ())), preferred_element_type=F32)


def _mm_tn(a, b):
    return lax.dot_general(a, b, (((0,), (0,)), ((), ())), preferred_element_type=F32)


def _full(shape):
    nd = len(shape)
    return pl.BlockSpec(shape, lambda *_: (0,) * nd)


def _rms_stats(x):
    r = lax.rsqrt(jnp.mean(x * x, axis=-1, keepdims=True) + EPS)
    return r, x * r


def _rms_bwd(dy, xhat, r, g):
    gd = dy * g
    return r * (gd - xhat * jnp.mean(gd * xhat, axis=-1, keepdims=True))


def _seg_sum64(v, bd_ref):
    outs = []
    for c in range(0, v.shape[1], 256):
        vc = v[:, c:c + 256]
        hi = vc.astype(BF16)
        lo = (vc - hi.astype(F32)).astype(BF16)
        outs.append(_mm(hi, bd_ref[...]) + _mm(lo, bd_ref[...]))
    return outs[0] if len(outs) == 1 else jnp.concatenate(outs, axis=1)


def _shift_rows(u, k, edge_rows):
    out = pltpu.roll(u, k, 0)
    row = lax.broadcasted_iota(jnp.int32, (8, u.shape[1]), 0)
    head = out[0:8]
    for j in range(k):
        head = jnp.where(row == j, edge_rows[k - 1 - j], head)
    return jnp.concatenate([head, out[8:]], axis=0)


def _shift_rows_up(u, k, edge_rows):
    n = u.shape[0]
    out = pltpu.roll(u, n - k, 0)
    row = lax.broadcasted_iota(jnp.int32, (8, u.shape[1]), 0)
    tail = out[n - 8:n]
    for j in range(k):
        tail = jnp.where(row == 8 - k + j, edge_rows[j], tail)
    return jnp.concatenate([out[0:n - 8], tail], axis=0)


def _conv_fwd(u, c1, c2, w_ref, b_ref):
    u1 = _shift_rows(u, 1, (c1,))
    u2 = _shift_rows(u, 2, (c1, c2))
    y = u2 * w_ref[0:1, :] + u1 * w_ref[1:2, :] + u * w_ref[2:3, :] + b_ref[...]
    return y, u1, u2


def _conv_bwd_input(dy, n1row, n2row, w_ref):
    d1 = _shift_rows_up(dy, 1, (n1row,))
    d2 = _shift_rows_up(dy, 2, (n1row, n2row))
    return dy * w_ref[2:3, :] + d1 * w_ref[1:2, :] + d2 * w_ref[0:1, :]


def _sigmoid(x):
    return 1.0 / (1.0 + jnp.exp(-x))


def _inproj_fwd(x, g_mix, w_in, conv_w, conv_b, qg, kg, bd, tm):
    t = x.shape[0]

    def body(x_ref, g_ref, w_ref, cw_ref, cb_ref, qg_ref, kg_ref, bd_ref,
             zc_ref, zqk_ref, yc_ref, q_ref, k_ref, v_ref, carry_ref):
        @pl.when(pl.program_id(0) == 0)
        def _():
            carry_ref[...] = jnp.zeros_like(carry_ref)

        _, xhat = _rms_stats(x_ref[...])
        h = (xhat * g_ref[...]).astype(BF16)
        zconv = _mm(h, w_ref[:, 0:3 * CONV_W])
        zc_ref[...] = zconv.astype(BF16)
        u = zconv[:, CONV_W:2 * CONV_W] * zconv[:, 2 * CONV_W:3 * CONV_W]
        cv, _, _ = _conv_fwd(u, carry_ref[7:8, :], carry_ref[6:7, :], cw_ref, cb_ref)
        yc_ref[...] = (zconv[:, 0:CONV_W] * cv).astype(BF16)
        carry_ref[...] = u[tm - 8:tm, :]

        zqk = _mm(h, w_ref[:, 3 * CONV_W:3 * CONV_W + 2 * ATTN_W])
        zqk_ref[...] = zqk.astype(BF16)
        for j, (gain_ref, out_ref, scale) in enumerate(((qg_ref, q_ref, ATTN_SCALE), (kg_ref, k_ref, 1.0))):
            z = zqk[:, j * ATTN_W:(j + 1) * ATTN_W]
            r = lax.rsqrt(_seg_sum64(z * z, bd_ref) * (1.0 / HEAD_DIM) + EPS)
            out_ref[...] = z * r * gain_ref[...] * scale
        v_ref[...] = _mm(h, w_ref[:, 3 * CONV_W + 2 * ATTN_W:IN_COLS])

    def blk(c):
        return pl.BlockSpec((tm, c), lambda i: (i, 0))

    return pl.pallas_call(
        body, name="inproj_fwd", grid=(t // tm,),
        in_specs=[blk(D_MODEL), _full((1, D_MODEL)), _full((D_MODEL, IN_COLS)), _full((3, CONV_W)),
                  _full((1, CONV_W)), _full((1, ATTN_W)), _full((1, ATTN_W)), _full((256, 256))],
        out_specs=[blk(3 * CONV_W), blk(2 * ATTN_W), blk(CONV_W), blk(ATTN_W), blk(ATTN_W), blk(ATTN_W)],
        out_shape=[jax.ShapeDtypeStruct((t, 3 * CONV_W), BF16), jax.ShapeDtypeStruct((t, 2 * ATTN_W), BF16),
                   jax.ShapeDtypeStruct((t, CONV_W), BF16), jax.ShapeDtypeStruct((t, ATTN_W), F32),
                   jax.ShapeDtypeStruct((t, ATTN_W), F32), jax.ShapeDtypeStruct((t, ATTN_W), F32)],
        scratch_shapes=[pltpu.VMEM((8, CONV_W), F32)],
        compiler_params=_cparams("arbitrary"),
    )(x, g_mix, w_in, conv_w, conv_b, qg, kg, bd)


SUPER = 16 * QK_BLOCK
KEYS = 2 * QK_BLOCK


def _rows(start, size, dil):
    return pl.ds(start, size) if dil == 1 else pl.ds(start, size, stride=dil)


def _attn_bias(sl_ref, dil):
    qi = lax.broadcasted_iota(jnp.int32, (KEYS, KEYS), 0)
    kj = lax.broadcasted_iota(jnp.int32, (KEYS, KEYS), 1)
    step = jnp.bitwise_and(qi, QK_BLOCK - 1) + QK_BLOCK - kj
    slope = jnp.where(qi < QK_BLOCK, sl_ref[0, 0:1, 0:1], sl_ref[0, 1:2, 0:1])
    bias = jnp.where(jnp.logical_and(step >= 0, step <= QK_BLOCK), -slope * (step * dil).astype(F32), -jnp.inf)
    return bias, kj >= QK_BLOCK


def _unit_start(u, dil):
    if dil == 1:
        return pl.multiple_of(u * QK_BLOCK, QK_BLOCK)
    if dil == 4:
        return jnp.bitwise_and(u, 3) + (u // 4) * (4 * QK_BLOCK)
    return u


def _stack_heads(a, head0):
    zero = jnp.zeros_like(a)
    return jnp.concatenate([jnp.where(head0, a, zero), jnp.where(head0, zero, a)], axis=0)


def _attn_fwd(q, k, v, slopes):
    t = q.shape[0]
    nsb = t // SUPER

    def body(q_ref, kc_ref, kp_ref, vc_ref, vp_ref, sl_ref, o_ref, l_ref, kk, vv, ob, lb):
        s = pl.program_id(1)
        kk[0:SUPER, :] = kp_ref[...]
        kk[SUPER:, :] = kc_ref[...]
        vv[0:SUPER, :] = vp_ref[...]
        vv[SUPER:, :] = vc_ref[...]
        head0 = lax.broadcasted_iota(jnp.int32, (QK_BLOCK, QK_BLOCK), 1) < HEAD_DIM

        for b, dil in enumerate(DILATIONS):
            bias, own_half = _attn_bias(sl_ref, dil)

            def unit(u, carry, b=b, dil=dil, bias=bias, own_half=own_half):
                start = _unit_start(u, dil)
                first_key = SUPER + start - QK_BLOCK * dil
                q2 = _stack_heads(q_ref[_rows(start, QK_BLOCK, dil), :].astype(BF16), head0)
                k2 = kk[_rows(first_key, KEYS, dil), :].astype(BF16)
                v2 = vv[_rows(first_key, KEYS, dil), :].astype(BF16)
                has_prev = jnp.logical_or(s > 0, start >= QK_BLOCK * dil)
                sc = jnp.where(jnp.logical_or(own_half, has_prev), _mm_nt(q2, k2) + bias, -jnp.inf)
                m = jnp.max(sc, axis=-1, keepdims=True)
                e = jnp.exp(sc - m)
                den = jnp.sum(e, axis=-1, keepdims=True)
                o2 = _mm(e.astype(BF16), v2) / den
                l2 = m + jnp.log(den)
                ob[b, _rows(start, QK_BLOCK, dil), :] = jnp.where(head0, o2[0:QK_BLOCK], o2[QK_BLOCK:])
                lb[b, _rows(start, QK_BLOCK, dil), :] = jnp.where(head0, l2[0:QK_BLOCK], l2[QK_BLOCK:])
                return carry

            lax.fori_loop(0, SUPER // QK_BLOCK, unit, 0, unroll=16)

        def merge(i, carry):
            rows = pl.ds(pl.multiple_of(i * 256, 256), 256)
            la, lb_, lc = lb[0, rows, :], lb[1, rows, :], lb[2, rows, :]
            mx = jnp.maximum(jnp.maximum(la, lb_), lc)
            wa, wb, wc = jnp.exp(la - mx), jnp.exp(lb_ - mx), jnp.exp(lc - mx)
            sw = wa + wb + wc
            o_ref[rows, :] = ((wa * ob[0, rows, :] + wb * ob[1, rows, :] + wc * ob[2, rows, :]) / sw).astype(BF16)
            l_ref[rows, :] = mx + jnp.log(sw)
            return carry

        lax.fori_loop(0, SUPER // 256, merge, 0)

    cur = pl.BlockSpec((SUPER, QK_BLOCK), lambda p, s: (s, p))
    prev = pl.BlockSpec((SUPER, QK_BLOCK), lambda p, s: (jnp.maximum(s - 1, 0), p))
    return pl.pallas_call(
        body, name="attn_fwd", grid=(4, nsb),
        in_specs=[cur, cur, prev, cur, prev, pl.BlockSpec((1, 2, QK_BLOCK), lambda p, s: (p, 0, 0))],
        out_specs=[cur, cur],
        out_shape=[jax.ShapeDtypeStruct((t, ATTN_W), BF16), jax.ShapeDtypeStruct((t, ATTN_W), F32)],
        scratch_shapes=[pltpu.VMEM((2 * SUPER, QK_BLOCK), F32), pltpu.VMEM((2 * SUPER, QK_BLOCK), F32),
                        pltpu.VMEM((3, SUPER, QK_BLOCK), F32), pltpu.VMEM((3, SUPER, QK_BLOCK), F32)],
        compiler_params=_cparams("parallel", "arbitrary"),
    )(q, k, k, v, v, slopes)


def _outproj_fwd(ya, yc, x, goc, goa, w_out, tm):
    t = x.shape[0]

    def body(ya_ref, yc_ref, x_ref, goc_ref, goa_ref, w_ref, x1_ref):
        _, ychat = _rms_stats(yc_ref[...].astype(F32))
        _, yahat = _rms_stats(ya_ref[...].astype(F32))
        acc = _mm((ychat * goc_ref[...]).astype(BF16), w_ref[0:CONV_W, :])
        acc += _mm((yahat * goa_ref[...]).astype(BF16), w_ref[CONV_W:, :])
        x1_ref[...] = x_ref[...] + acc

    def blk(c):
        return pl.BlockSpec((tm, c), lambda i: (i, 0))

    return pl.pallas_call(
        body, name="outproj_fwd", grid=(t // tm,),
        in_specs=[blk(ATTN_W), blk(CONV_W), blk(D_MODEL), _full((1, CONV_W)), _full((1, ATTN_W)),
                  _full((D_MODEL, D_MODEL))],
        out_specs=blk(D_MODEL),
        out_shape=jax.ShapeDtypeStruct((t, D_MODEL), F32),
        compiler_params=_cparams("parallel"),
    )(ya, yc, x, goc, goa, w_out)


def _ffn_fwd(x1, g_ffn, w_gate, w_up, w_down, fcw, fcb, tm):
    t = x1.shape[0]
    stacked = w_gate.ndim == 3
    shard = D_FF // N_DEV

    def body(x_ref, g_ref, wg_in, wu_in, wd_ref, cw_ref, cb_ref, gp_ref, up_ref, h_ref, x2_ref, *rest):
        carry_ref = rest[-1]
        wg_ref, wu_ref = (rest[0], rest[1]) if stacked else (wg_in, wu_in)

        @pl.when(pl.program_id(0) == 0)
        def _():
            carry_ref[...] = jnp.zeros_like(carry_ref)
            if stacked:
                for k in range(N_DEV):
                    wg_ref[:, k * shard:(k + 1) * shard] = wg_in[k]
                    wu_ref[:, k * shard:(k + 1) * shard] = wu_in[k]

        xv = x_ref[...]
        _, xhat = _rms_stats(xv)
        h = (xhat * g_ref[...]).astype(BF16)
        h_ref[...] = h
        gp = _mm(h, wg_ref[...])
        gp_ref[...] = gp.astype(BF16)
        gate, _, _ = _conv_fwd(gp, carry_ref[7:8, :], carry_ref[6:7, :], cw_ref, cb_ref)
        carry_ref[...] = gp[tm - 8:tm, :]
        up = _mm(h, wu_ref[...])
        up_ref[...] = up.astype(BF16)
        a = (gate * _sigmoid(gate) * up).astype(BF16)
        x2_ref[...] = xv + _mm(a, wd_ref[...])

    def blk(c):
        return pl.BlockSpec((tm, c), lambda i: (i, 0))

    laid_out = [_full((D_MODEL, D_FF))] * 2 if stacked else []
    return pl.pallas_call(
        body, name="ffn_fwd", grid=(t // tm,),
        in_specs=[blk(D_MODEL), _full((1, D_MODEL)), _full(w_gate.shape), _full(w_up.shape),
                  _full((D_FF, D_MODEL)), _full((3, D_FF)), _full((1, D_FF))],
        out_specs=[blk(D_FF), blk(D_FF), blk(D_MODEL), blk(D_MODEL)] + laid_out,
        out_shape=[jax.ShapeDtypeStruct((t, D_FF), BF16), jax.ShapeDtypeStruct((t, D_FF), BF16),
                   jax.ShapeDtypeStruct((t, D_MODEL), BF16), jax.ShapeDtypeStruct((t, D_MODEL), F32)]
        + [jax.ShapeDtypeStruct((D_MODEL, D_FF), BF16)] * len(laid_out),
        scratch_shapes=[pltpu.VMEM((8, D_FF), F32)],
        compiler_params=_cparams("arbitrary"),
    )(x1, g_ffn, w_gate, w_up, w_down, fcw, fcb)


def _ple_fwd_bwd(x2, p, target, g_ple, w_pg, w_pp, tm):
    t = x2.shape[0]

    def body(x_ref, p_ref, t_ref, g_ref, wg_ref, wp_ref, dx_ref, dxb_ref, loss_ref, dwg_ref, dwp_ref, dg_ref):
        @pl.when(pl.program_id(0) == 0)
        def _():
            loss_ref[...] = jnp.zeros_like(loss_ref)
            dwg_ref[...] = jnp.zeros_like(dwg_ref)
            dwp_ref[...] = jnp.zeros_like(dwp_ref)
            dg_ref[...] = jnp.zeros_like(dg_ref)

        xv = x_ref[...]
        r, xhat = _rms_stats(xv)
        g = g_ref[...]
        h = (xhat * g).astype(BF16)
        pg = _sigmoid(_mm(h, wg_ref[...]))
        pb = p_ref[...].astype(BF16)
        pp = _mm(pb, wp_ref[...])
        err = xv + pg * pp - t_ref[...]
        loss_ref[...] += 0.5 * jnp.sum(jnp.mean(err * err, axis=-1, keepdims=True))
        dx3 = err * (1.0 / D_MODEL)
        d_pp = (dx3 * pg).astype(BF16)
        d_pre = (dx3 * pp * pg * (1.0 - pg)).astype(BF16)
        dwp_ref[...] += _mm_tn(pb, d_pp)
        dwg_ref[...] += _mm_tn(h, d_pre)
        dh = _mm_nt(d_pre, wg_ref[...])
        dg_ref[...] += jnp.sum(dh * xhat, axis=0, keepdims=True)
        dx2 = dx3 + _rms_bwd(dh, xhat, r, g)
        dx_ref[...] = dx2
        dxb_ref[...] = dx2.astype(BF16)

    def blk(c):
        return pl.BlockSpec((tm, c), lambda i: (i, 0))

    return pl.pallas_call(
        body, name="ple_fwd_bwd", grid=(t // tm,),
        in_specs=[blk(D_MODEL), blk(PLE_DIM), blk(D_MODEL), _full((1, D_MODEL)), _full((D_MODEL, D_MODEL)),
                  _full((PLE_DIM, D_MODEL))],
        out_specs=[blk(D_MODEL), blk(D_MODEL), _full((8, 128)), _full((D_MODEL, D_MODEL)),
                   _full((PLE_DIM, D_MODEL)), _full((1, D_MODEL))],
        out_shape=[jax.ShapeDtypeStruct((t, D_MODEL), F32), jax.ShapeDtypeStruct((t, D_MODEL), BF16),
                   jax.ShapeDtypeStruct((8, 128), F32),
                   jax.ShapeDtypeStruct((D_MODEL, D_MODEL), F32), jax.ShapeDtypeStruct((PLE_DIM, D_MODEL), F32),
                   jax.ShapeDtypeStruct((1, D_MODEL), F32)],
        compiler_params=_cparams("arbitrary"),
    )(x2, p, target, g_ple, w_pg, w_pp)


def _ffn_bwd(dx2, h2, gp, up, w_gate, w_up, w_down, fcw, fcb, tm):
    t = dx2.shape[0]
    nblk = t // tm
    fc = D_FF // FF_CHUNKS
    half = tm // FFN_BWD_PARTS

    def body(dx_ref, h_ref, gp_ref, gph_ref, up_ref, wg_ref, wu_ref, wd_ref, cw_ref, cb_ref,
             dh_ref, dwd_ref, dwu_ref, dwg_ref, dcw_ref, dcb_ref, carry_ref, a_scr, dup_scr, dgp_scr):
        i = pl.program_id(1)

        @pl.when(i == 0)
        def _():
            carry_ref[...] = jnp.zeros_like(carry_ref)
            dwd_ref[...] = jnp.zeros_like(dwd_ref)
            dwu_ref[...] = jnp.zeros_like(dwu_ref)
            dwg_ref[...] = jnp.zeros_like(dwg_ref)
            dcw_ref[...] = jnp.zeros_like(dcw_ref)
            dcb_ref[...] = jnp.zeros_like(dcb_ref)

        keep = (i < nblk - 1).astype(F32)
        later = carry_ref[...]
        for hf in reversed(range(FFN_BWD_PARTS)):
            rows = slice(hf * half, (hf + 1) * half)
            dxb = dx_ref[rows, :]
            gp_v = gp_ref[rows, :].astype(F32)
            if hf > 0:
                before = gp_ref[hf * half - 16:hf * half, :].astype(F32)
            else:
                before = gph_ref[...].astype(F32) * keep
            gate, gp1, gp2 = _conv_fwd(gp_v, before[15:16, :], before[14:15, :], cw_ref, cb_ref)
            s = _sigmoid(gate)
            silu = gate * s
            up_v = up_ref[rows, :].astype(F32)
            da = _mm_nt(dxb, wd_ref[...])
            a_scr[rows, :] = (silu * up_v).astype(BF16)
            d_up = (da * silu).astype(BF16)
            dup_scr[rows, :] = d_up
            d_gate = da * up_v * (s * (1.0 + gate * (1.0 - s)))
            d_gp = _conv_bwd_input(d_gate, later[0:1, :], later[1:2, :], cw_ref).astype(BF16)
            dgp_scr[rows, :] = d_gp
            later = d_gate[0:8, :]
            dcw_ref[0:1, :] += jnp.sum(d_gate * gp2, axis=0, keepdims=True)
            dcw_ref[1:2, :] += jnp.sum(d_gate * gp1, axis=0, keepdims=True)
            dcw_ref[2:3, :] += jnp.sum(d_gate * gp_v, axis=0, keepdims=True)
            dcb_ref[...] += jnp.sum(d_gate, axis=0, keepdims=True)
            dh_ref[rows, :] = (_mm_nt(d_gp, wg_ref[...]) + _mm_nt(d_up, wu_ref[...])).astype(BF16)
        carry_ref[...] = later
        dwd_ref[...] += _mm_tn(a_scr[...], dx_ref[...])
        dwu_ref[...] += _mm_tn(h_ref[...], dup_scr[...])
        dwg_ref[...] += _mm_tn(h_ref[...], dgp_scr[...])

    def rev(i):
        return nblk - 1 - i

    one = pl.Buffered(1)
    in_specs = [
        pl.BlockSpec((tm, D_MODEL), lambda j, i: (rev(i), 0)),
        pl.BlockSpec((tm, D_MODEL), lambda j, i: (rev(i), 0)),
        pl.BlockSpec((tm, fc), lambda j, i: (rev(i), j)),
        pl.BlockSpec((16, fc), lambda j, i: (jnp.maximum(rev(i) * (tm // 16) - 1, 0), j)),
        pl.BlockSpec((tm, fc), lambda j, i: (rev(i), j)),
        pl.BlockSpec((D_MODEL, fc), lambda j, i: (0, j), pipeline_mode=one),
        pl.BlockSpec((D_MODEL, fc), lambda j, i: (0, j), pipeline_mode=one),
        pl.BlockSpec((fc, D_MODEL), lambda j, i: (j, 0), pipeline_mode=one),
        pl.BlockSpec((3, fc), lambda j, i: (0, j)),
        pl.BlockSpec((1, fc), lambda j, i: (0, j)),
    ]
    out_specs = [
        pl.BlockSpec((None, tm, D_MODEL), lambda j, i: (j, rev(i), 0)),
        pl.BlockSpec((fc, D_MODEL), lambda j, i: (j, 0), pipeline_mode=one),
        pl.BlockSpec((D_MODEL, fc), lambda j, i: (0, j), pipeline_mode=one),
        pl.BlockSpec((D_MODEL, fc), lambda j, i: (0, j), pipeline_mode=one),
        pl.BlockSpec((3, fc), lambda j, i: (0, j)),
        pl.BlockSpec((1, fc), lambda j, i: (0, j)),
    ]
    return pl.pallas_call(
        body, name="ffn_bwd", grid=(FF_CHUNKS, nblk), in_specs=in_specs, out_specs=out_specs,
        out_shape=[jax.ShapeDtypeStruct((FF_CHUNKS, t, D_MODEL), BF16), jax.ShapeDtypeStruct((D_FF, D_MODEL), F32),
                   jax.ShapeDtypeStruct((D_MODEL, D_FF), F32), jax.ShapeDtypeStruct((D_MODEL, D_FF), F32),
                   jax.ShapeDtypeStruct((3, D_FF), F32), jax.ShapeDtypeStruct((1, D_FF), F32)],
        scratch_shapes=[pltpu.VMEM((8, fc), F32), pltpu.VMEM((tm, fc), BF16), pltpu.VMEM((tm, fc), BF16),
                        pltpu.VMEM((tm, fc), BF16)],
        compiler_params=_cparams("arbitrary", "arbitrary"),
    )(dx2, h2, gp, gp, up, w_gate, w_up, w_down, fcw, fcb)


def _outproj_bwd(dh2, dx2, x1, g_ffn, w_out, yc, ya, goc, goa, zconv, conv_w, conv_b, bd, tm):
    t = x1.shape[0]
    nblk = t // tm

    def body(dh_ref, dx2_ref, x1_ref, g_ref, w_ref, yc_ref, ya_ref, goc_ref, goa_ref, zc_ref, zch_ref, cw_ref, cb_ref,
             bd_ref, dx1_ref, dya_ref, dd_ref, dzc_ref, dw_ref, dg_ref, dgoc_ref, dgoa_ref, dcw_ref, dcb_ref,
             carry_ref):
        i = pl.program_id(0)

        @pl.when(i == 0)
        def _():
            carry_ref[...] = jnp.zeros_like(carry_ref)
            for ref in (dw_ref, dg_ref, dgoc_ref, dgoa_ref, dcw_ref, dcb_ref):
                ref[...] = jnp.zeros_like(ref)

        keep = (i < nblk - 1).astype(F32)
        dh2_v = dh_ref[0].astype(F32)
        for j in range(1, FF_CHUNKS):
            dh2_v = dh2_v + dh_ref[j].astype(F32)
        r, xhat = _rms_stats(x1_ref[...])
        dg_ref[...] += jnp.sum(dh2_v * xhat, axis=0, keepdims=True)
        dx1 = dx2_ref[...] + _rms_bwd(dh2_v, xhat, r, g_ref[...])
        dx1_ref[...] = dx1
        dx1b = dx1.astype(BF16)
        dy = _mm_nt(dx1b, w_ref[...])

        yc_v = yc_ref[...].astype(F32)
        rc, ychat = _rms_stats(yc_v)
        dw_ref[0:CONV_W, :] += _mm_tn((ychat * goc_ref[...]).astype(BF16), dx1b)
        dyc = dy[:, 0:CONV_W]
        dgoc_ref[...] += jnp.sum(dyc * ychat, axis=0, keepdims=True)
        d_yc = _rms_bwd(dyc, ychat, rc, goc_ref[...])

        ya_v = ya_ref[...].astype(F32)
        ra, yahat = _rms_stats(ya_v)
        dw_ref[CONV_W:, :] += _mm_tn((yahat * goa_ref[...]).astype(BF16), dx1b)
        dya = dy[:, CONV_W:]
        dgoa_ref[...] += jnp.sum(dya * yahat, axis=0, keepdims=True)
        d_ya = _rms_bwd(dya, yahat, ra, goa_ref[...])
        dya_ref[...] = d_ya
        dd_ref[...] = _seg_sum64(d_ya * ya_v, bd_ref)

        zb = zc_ref[:, 0:CONV_W].astype(F32)
        zc = zc_ref[:, CONV_W:2 * CONV_W].astype(F32)
        zx = zc_ref[:, 2 * CONV_W:3 * CONV_W].astype(F32)
        u = zc * zx
        uh = (zch_ref[:, CONV_W:2 * CONV_W].astype(F32) * zch_ref[:, 2 * CONV_W:3 * CONV_W].astype(F32)) * keep
        cv, u1, u2 = _conv_fwd(u, uh[15:16, :], uh[14:15, :], cw_ref, cb_ref)
        d_cv = d_yc * zb
        d_u = _conv_bwd_input(d_cv, carry_ref[0:1, :], carry_ref[1:2, :], cw_ref)
        carry_ref[...] = d_cv[0:8, :]
        dcw_ref[0:1, :] += jnp.sum(d_cv * u2, axis=0, keepdims=True)
        dcw_ref[1:2, :] += jnp.sum(d_cv * u1, axis=0, keepdims=True)
        dcw_ref[2:3, :] += jnp.sum(d_cv * u, axis=0, keepdims=True)
        dcb_ref[...] += jnp.sum(d_cv, axis=0, keepdims=True)
        dzc_ref[:, 0:CONV_W] = (d_yc * cv).astype(BF16)
        dzc_ref[:, CONV_W:2 * CONV_W] = (d_u * zx).astype(BF16)
        dzc_ref[:, 2 * CONV_W:3 * CONV_W] = (d_u * zc).astype(BF16)

    def rev(i):
        return nblk - 1 - i

    def blk(c):
        return pl.BlockSpec((tm, c), lambda i: (rev(i), 0))

    in_specs = [
        pl.BlockSpec((FF_CHUNKS, tm, D_MODEL), lambda i: (0, rev(i), 0)),
        blk(D_MODEL), blk(D_MODEL), _full((1, D_MODEL)), _full((D_MODEL, D_MODEL)),
        blk(CONV_W), blk(ATTN_W), _full((1, CONV_W)), _full((1, ATTN_W)),
        blk(3 * CONV_W),
        pl.BlockSpec((16, 3 * CONV_W), lambda i: (jnp.maximum(rev(i) * (tm // 16) - 1, 0), 0)),
        _full((3, CONV_W)), _full((1, CONV_W)), _full((256, 256)),
    ]
    out_specs = [blk(D_MODEL), blk(ATTN_W), blk(ATTN_W), blk(3 * CONV_W), _full((D_MODEL, D_MODEL)),
                 _full((1, D_MODEL)), _full((1, CONV_W)), _full((1, ATTN_W)), _full((3, CONV_W)), _full((1, CONV_W))]
    return pl.pallas_call(
        body, name="outproj_bwd", grid=(nblk,), in_specs=in_specs, out_specs=out_specs,
        out_shape=[jax.ShapeDtypeStruct((t, D_MODEL), F32), jax.ShapeDtypeStruct((t, ATTN_W), F32),
                   jax.ShapeDtypeStruct((t, ATTN_W), F32), jax.ShapeDtypeStruct((t, 3 * CONV_W), BF16),
                   jax.ShapeDtypeStruct((D_MODEL, D_MODEL), F32), jax.ShapeDtypeStruct((1, D_MODEL), F32),
                   jax.ShapeDtypeStruct((1, CONV_W), F32), jax.ShapeDtypeStruct((1, ATTN_W), F32),
                   jax.ShapeDtypeStruct((3, CONV_W), F32), jax.ShapeDtypeStruct((1, CONV_W), F32)],
        scratch_shapes=[pltpu.VMEM((8, CONV_W), F32)],
        compiler_params=_cparams("arbitrary"),
    )(dh2, dx2, x1, g_ffn, w_out, yc, ya, goc, goa, zconv, zconv, conv_w, conv_b, bd)


def _attn_bwd(q, k, v, dya, lse, dd, slopes):
    t = q.shape[0]
    nsb = t // SUPER

    def body(q_ref, kc_ref, kp_ref, vc_ref, vp_ref, dy_ref, l_ref, d_ref, sl_ref, dq_ref, dk_ref, dv_ref,
             kk, vv, dkacc, dvacc):
        s = pl.program_id(1)

        @pl.when(s == 0)
        def _():
            dkacc[...] = jnp.zeros_like(dkacc)
            dvacc[...] = jnp.zeros_like(dvacc)

        dkacc[0:SUPER, :] = dkacc[SUPER:, :]
        dvacc[0:SUPER, :] = dvacc[SUPER:, :]
        dkacc[SUPER:, :] = jnp.zeros((SUPER, QK_BLOCK), F32)
        dvacc[SUPER:, :] = jnp.zeros((SUPER, QK_BLOCK), F32)

        @pl.when(s < nsb)
        def _():
            kk[0:SUPER, :] = kp_ref[...]
            kk[SUPER:, :] = kc_ref[...]
            vv[0:SUPER, :] = vp_ref[...]
            vv[SUPER:, :] = vc_ref[...]
            head0 = lax.broadcasted_iota(jnp.int32, (QK_BLOCK, QK_BLOCK), 1) < HEAD_DIM

            for b, dil in enumerate(DILATIONS):
                bias, own_half = _attn_bias(sl_ref, dil)

                def unit(u, carry, b=b, dil=dil, bias=bias, own_half=own_half):
                    start = _unit_start(u, dil)
                    first_key = SUPER + start - QK_BLOCK * dil
                    qrows = _rows(start, QK_BLOCK, dil)
                    krows = _rows(first_key, KEYS, dil)
                    q2 = _stack_heads(q_ref[qrows, :].astype(BF16), head0)
                    dy2 = _stack_heads(dy_ref[qrows, :].astype(BF16), head0)
                    lv, dv_ = l_ref[qrows, :], d_ref[qrows, :]
                    l2 = jnp.concatenate([lv[:, 0:1], lv[:, HEAD_DIM:HEAD_DIM + 1]], axis=0)
                    d2 = jnp.concatenate([dv_[:, 0:1], dv_[:, HEAD_DIM:HEAD_DIM + 1]], axis=0)
                    k2 = kk[krows, :].astype(BF16)
                    v2 = vv[krows, :].astype(BF16)
                    has_prev = jnp.logical_or(s > 0, start >= QK_BLOCK * dil)
                    sc = jnp.where(jnp.logical_or(own_half, has_prev), _mm_nt(q2, k2) + bias, -jnp.inf)
                    prob = jnp.exp(sc - l2)
                    ds = (prob * (_mm_nt(dy2, v2) - d2)).astype(BF16)
                    dvacc[krows, :] += _mm_tn(prob.astype(BF16), dy2)
                    dkacc[krows, :] += _mm_tn(ds, q2)
                    dq2 = _mm(ds, k2)
                    dq = jnp.where(head0, dq2[0:QK_BLOCK], dq2[QK_BLOCK:]) * ATTN_SCALE
                    if b == 0:
                        dq_ref[qrows, :] = dq
                    else:
                        dq_ref[qrows, :] += dq
                    return carry

                lax.fori_loop(0, SUPER // QK_BLOCK, unit, 0, unroll=8)

        dk_ref[...] = dkacc[0:SUPER, :]
        dv_ref[...] = dvacc[0:SUPER, :].astype(BF16)

    def cur_map(p, s):
        return (jnp.minimum(s, nsb - 1), p)

    def prev_map(p, s):
        return (jnp.clip(s - 1, 0, nsb - 1), p)

    cur = pl.BlockSpec((SUPER, QK_BLOCK), cur_map)
    prev = pl.BlockSpec((SUPER, QK_BLOCK), prev_map)
    return pl.pallas_call(
        body, name="attn_bwd", grid=(4, nsb + 1),
        in_specs=[cur, cur, prev, cur, prev, cur, cur, cur, pl.BlockSpec((1, 2, QK_BLOCK), lambda p, s: (p, 0, 0))],
        out_specs=[cur, prev, prev],
        out_shape=[jax.ShapeDtypeStruct((t, ATTN_W), F32), jax.ShapeDtypeStruct((t, ATTN_W), F32),
                   jax.ShapeDtypeStruct((t, ATTN_W), BF16)],
        scratch_shapes=[pltpu.VMEM((2 * SUPER, QK_BLOCK), F32)] * 4,
        compiler_params=_cparams("parallel", "arbitrary"),
    )(q, k, k, v, v, dya, lse, dd, slopes)


def _attn_bwd_per_branch_unused(q, k, v, dya, lse, dd, slopes, dil):
    t = q.shape[0]
    length = t // dil
    chunk = _attn_chunk(t, dil)
    nch = length // chunk
    nb = chunk // QK_BLOCK
    nblocks = length // QK_BLOCK
    view = (length, dil * ATTN_W)
    ext = chunk + QK_BLOCK

    def body(q_ref, dy_ref, l_ref, d_ref, k_ref, v_ref, qn_ref, dyn_ref, ln_ref, dn_ref, kh_ref, vh_ref, sl_ref,
             dq_ref, dk_ref, dv_ref, qbuf, dybuf, lbuf, dbuf, kbuf, vbuf, dkacc, dvacc):
        c = pl.program_id(2)
        qbuf[0:chunk, :] = q_ref[...]
        qbuf[chunk:, :] = qn_ref[...]
        dybuf[0:chunk, :] = dy_ref[...].astype(BF16)
        dybuf[chunk:, :] = dyn_ref[...].astype(BF16)
        lbuf[0:chunk, :] = l_ref[...]
        lbuf[chunk:, :] = ln_ref[...]
        dbuf[0:chunk, :] = d_ref[...]
        dbuf[chunk:, :] = dn_ref[...]
        kbuf[0:QK_BLOCK, :] = kh_ref[...]
        kbuf[QK_BLOCK:, :] = k_ref[...]
        vbuf[0:QK_BLOCK, :] = vh_ref[...]
        vbuf[QK_BLOCK:, :] = v_ref[...]
        valid_cur, valid_prev, dist_cur, dist_prev, head0 = _attn_masks(dil)

        def pair(qb, dyb, lv, dv_, kb, vb, valid, dist):
            dq = jnp.zeros((QK_BLOCK, QK_BLOCK), F32)
            dk = jnp.zeros((QK_BLOCK, QK_BLOCK), F32)
            dvv = jnp.zeros((QK_BLOCK, QK_BLOCK), F32)
            for hh in range(2):
                sl = sl_ref[0, hh:hh + 1, :]
                hm = head0 if hh == 0 else jnp.logical_not(head0)
                col = hh * HEAD_DIM
                qm = jnp.where(hm, qb, jnp.zeros_like(qb))
                dym = jnp.where(hm, dyb, jnp.zeros_like(dyb))
                s = jnp.where(valid, _mm_nt(qm, kb) - sl * dist, -jnp.inf)
                prob = jnp.exp(s - lv[:, col:col + 1])
                ds = (prob * (_mm_nt(dym, vb) - dv_[:, col:col + 1])).astype(BF16)
                dvv += _mm_tn(prob.astype(BF16), dym)
                dk += _mm_tn(ds, qm)
                dq += jnp.where(hm, _mm(ds, kb), 0.0)
            return dq, dk, dvv

        def blk(j, carry):
            off = pl.multiple_of(j * QK_BLOCK, QK_BLOCK)
            nxt = pl.multiple_of(off + QK_BLOCK, QK_BLOCK)
            qb = qbuf[pl.ds(off, QK_BLOCK), :]
            dyb = dybuf[pl.ds(off, QK_BLOCK), :]
            lv = lbuf[pl.ds(off, QK_BLOCK), :]
            dv_ = dbuf[pl.ds(off, QK_BLOCK), :]
            dq_c, dk_c, dv_c = pair(qb, dyb, lv, dv_, kbuf[pl.ds(nxt, QK_BLOCK), :], vbuf[pl.ds(nxt, QK_BLOCK), :],
                                    valid_cur, dist_cur)
            dkacc[pl.ds(nxt, QK_BLOCK), :] = dk_c
            dvacc[pl.ds(nxt, QK_BLOCK), :] = dv_c
            has_prev = jnp.logical_or(c > 0, j > 0)
            dq_p, dk_p, dv_p = pair(qb, dyb, lv, dv_, kbuf[pl.ds(off, QK_BLOCK), :], vbuf[pl.ds(off, QK_BLOCK), :],
                                    jnp.logical_and(valid_prev, has_prev), dist_prev)

            @pl.when(j > 0)
            def _():
                dkacc[pl.ds(off, QK_BLOCK), :] += dk_p
                dvacc[pl.ds(off, QK_BLOCK), :] += dv_p

            dq_ref[pl.ds(off, QK_BLOCK), :] = (dq_c + dq_p) * ATTN_SCALE
            return carry

        lax.fori_loop(0, nb, blk, 0)

        @pl.when(c < nch - 1)
        def _():
            _, dk_p, dv_p = pair(qbuf[chunk:, :], dybuf[chunk:, :], lbuf[chunk:, :], dbuf[chunk:, :],
                                 kbuf[chunk:, :], vbuf[chunk:, :], valid_prev, dist_prev)
            dkacc[chunk:, :] += dk_p
            dvacc[chunk:, :] += dv_p

        dk_ref[...] = dkacc[QK_BLOCK:, :]
        dv_ref[...] = dvacc[QK_BLOCK:, :]

    def cmap(p, r, c):
        return (c, r * 4 + p)

    def before(p, r, c):
        return (jnp.maximum(c * nb - 1, 0), r * 4 + p)

    def after(p, r, c):
        return (jnp.minimum((c + 1) * nb, nblocks - 1), r * 4 + p)

    main = pl.BlockSpec((chunk, QK_BLOCK), cmap)
    hb = pl.BlockSpec((QK_BLOCK, QK_BLOCK), before)
    ha = pl.BlockSpec((QK_BLOCK, QK_BLOCK), after)
    qv, kv, vv = q.reshape(view), k.reshape(view), v.reshape(view)
    dyv, lv, ddv = dya.reshape(view), lse.reshape(view), dd.reshape(view)
    outs = pl.pallas_call(
        body, name=f"attn_bwd_d{dil}", grid=(4, dil, nch),
        in_specs=[main] * 6 + [ha] * 4 + [hb] * 2 + [pl.BlockSpec((1, 2, QK_BLOCK), lambda p, r, c: (p, 0, 0))],
        out_specs=[main] * 3,
        out_shape=[jax.ShapeDtypeStruct(view, F32)] * 3,
        scratch_shapes=[pltpu.VMEM((ext, QK_BLOCK), BF16), pltpu.VMEM((ext, QK_BLOCK), BF16),
                        pltpu.VMEM((ext, QK_BLOCK), F32), pltpu.VMEM((ext, QK_BLOCK), F32),
                        pltpu.VMEM((ext, QK_BLOCK), BF16), pltpu.VMEM((ext, QK_BLOCK), BF16),
                        pltpu.VMEM((ext, QK_BLOCK), F32), pltpu.VMEM((ext, QK_BLOCK), F32)],
        compiler_params=_cparams("arbitrary", "arbitrary", "arbitrary"),
    )(qv, dyv, lv, ddv, kv, vv, qv, dyv, lv, ddv, kv, vv, slopes)
    return [o.reshape(t, ATTN_W) for o in outs]


def _inproj_bwd(dq, dk, dv, dzconv, zqk, x, dx1, g_mix, w_in, qg, kg, bd, tm):
    t = x.shape[0]

    def body(dq_ref, dk_ref, dv_ref, dzc_ref, zqk_ref, x_ref, dx1_ref, g_ref, w_ref, qg_ref,
             kg_ref, bd_ref, dx_ref, dw_ref, dg_ref, dqg_ref, dkg_ref):
        @pl.when(pl.program_id(0) == 0)
        def _():
            for ref in (dw_ref, dg_ref, dqg_ref, dkg_ref):
                ref[...] = jnp.zeros_like(ref)

        parts = [dzc_ref[...]]
        for j, (dn_ref, gain_ref, dgain_ref) in enumerate(((dq_ref, qg_ref, dqg_ref), (dk_ref, kg_ref, dkg_ref))):
            dn = dn_ref[...]
            z = zqk_ref[:, j * ATTN_W:(j + 1) * ATTN_W].astype(F32)
            r = lax.rsqrt(_seg_sum64(z * z, bd_ref) * (1.0 / HEAD_DIM) + EPS)
            zhat = z * r
            dgain_ref[...] += jnp.sum(dn * zhat, axis=0, keepdims=True)
            gd = dn * gain_ref[...]
            parts.append((r * (gd - zhat * (_seg_sum64(gd * zhat, bd_ref) * (1.0 / HEAD_DIM)))).astype(BF16))
        parts.append(dv_ref[...].astype(BF16))
        dz = jnp.concatenate(parts, axis=1)

        r, xhat = _rms_stats(x_ref[...])
        g = g_ref[...]
        dw_ref[...] += _mm_tn((xhat * g).astype(BF16), dz)
        dh = _mm_nt(dz, w_ref[...])
        dg_ref[...] += jnp.sum(dh * xhat, axis=0, keepdims=True)
        dx_ref[...] = dx1_ref[...] + _rms_bwd(dh, xhat, r, g)

    def blk(c):
        return pl.BlockSpec((tm, c), lambda i: (i, 0))

    return pl.pallas_call(
        body, name="inproj_bwd", grid=(t // tm,),
        in_specs=[blk(ATTN_W)] * 3 + [blk(3 * CONV_W), blk(2 * ATTN_W), blk(D_MODEL), blk(D_MODEL), _full((1, D_MODEL)),
                                      _full((D_MODEL, IN_COLS)), _full((1, ATTN_W)), _full((1, ATTN_W)),
                                      _full((256, 256))],
        out_specs=[blk(D_MODEL), _full((D_MODEL, IN_COLS)), _full((1, D_MODEL)), _full((1, ATTN_W)),
                   _full((1, ATTN_W))],
        out_shape=[jax.ShapeDtypeStruct((t, D_MODEL), F32), jax.ShapeDtypeStruct((D_MODEL, IN_COLS), F32),
                   jax.ShapeDtypeStruct((1, D_MODEL), F32), jax.ShapeDtypeStruct((1, ATTN_W), F32),
                   jax.ShapeDtypeStruct((1, ATTN_W), F32)],
        compiler_params=_cparams("arbitrary"),
    )(dq, dk, dv, dzconv, zqk, x, dx1, g_mix, w_in, qg, kg, bd)


def _ordered_after(a, token):
    return a if token is None else a + token[0:1, 0:1].reshape((1,) * a.ndim)


def _local_step(x, p, target, w, tms, hooks=None):
    hooks = hooks or {}
    bd = jnp.kron(jnp.eye(4, dtype=F32), jnp.ones((HEAD_DIM, HEAD_DIM), F32)).astype(BF16)
    qg = jnp.tile(w["q_norm_g"], (1, 8))
    kg = jnp.tile(w["k_norm_g"], (1, 8))
    slopes = jnp.exp2(-jnp.arange(1, 9, dtype=F32))
    slopes = jnp.broadcast_to(slopes.reshape(4, 2, 1), (4, 2, QK_BLOCK))

    zconv, zqk, yc, q, k, v = _inproj_fwd(x, w["g_mix"], w["w_in"], w["conv_w"], w["conv_b"], qg, kg, bd, tms[0])
    ya, lse = _attn_fwd(q, k, v, slopes)
    if "late_weights" in hooks:
        w = {**w, **hooks["late_weights"](lse)}
    x1 = _outproj_fwd(ya, yc, x, w["g_out_conv"], w["g_out_attn"], w["w_out"], tms[0])
    gp, up, h2, x2, *laid_out = _ffn_fwd(x1, w["g_ffn"], w["w_gate"], w["w_up"], w["w_down"], w["ffn_conv_w"],
                                         w["ffn_conv_b"], tms[1])
    if laid_out:
        w = {**w, "w_gate": laid_out[0], "w_up": laid_out[1]}
    dx2, dx2b, loss, dw_pg, dw_pp, dg_ple = _ple_fwd_bwd(x2, p, target, w["g_ple"], w["w_ple_gate"], w["w_ple_proj"], tms[0])
    dh2, dw_down, dw_up, dw_gate, dfcw, dfcb = _ffn_bwd(dx2b, h2, gp, up, w["w_gate"], w["w_up"], w["w_down"],
                                                        w["ffn_conv_w"], w["ffn_conv_b"], tms[0])
    token = None
    if "ffn_grads" in hooks:
        token = hooks["ffn_grads"]({"w_ple_gate": dw_pg, "w_ple_proj": dw_pp, "w_down": dw_down, "w_up": dw_up,
                                    "w_gate": dw_gate})
    dx1, dya, dd, dzconv, dw_out, dg_ffn, dgoc, dgoa, dcw, dcb = _outproj_bwd(
        dh2, dx2, x1, _ordered_after(w["g_ffn"], token), w["w_out"], yc, ya, w["g_out_conv"], w["g_out_attn"], zconv,
        w["conv_w"], w["conv_b"], bd, tms[1])
    token = hooks["outproj_done"](dx1) if "outproj_done" in hooks else None
    dq, dk, dv = _attn_bwd(q, k, v, dya, lse, dd, _ordered_after(slopes, token))
    dx, dw_in, dg_mix, dqg, dkg = _inproj_bwd(dq, dk, dv, dzconv, zqk, x, dx1, w["g_mix"], w["w_in"], qg, kg, bd,
                                              tms[0])
    grads = {
        "g_mix": dg_mix, "w_in": dw_in, "conv_w": dcw, "conv_b": dcb,
        "q_norm_g": dqg.reshape(8, HEAD_DIM).sum(0, keepdims=True),
        "k_norm_g": dkg.reshape(8, HEAD_DIM).sum(0, keepdims=True),
        "g_out_conv": dgoc, "g_out_attn": dgoa, "w_out": dw_out, "g_ffn": dg_ffn, "w_gate": dw_gate, "w_up": dw_up,
        "ffn_conv_w": dfcw, "ffn_conv_b": dfcb, "w_down": dw_down, "g_ple": dg_ple, "w_ple_gate": dw_pg,
        "w_ple_proj": dw_pp,
    }
    return loss, dx, grads


ANY = pl.BlockSpec(memory_space=pl.ANY)
MESH = pl.DeviceIdType.MESH


def _all_gather(shards, name):
    n = len(shards)

    def body(*refs):
        ins, outs = refs[:n], refs[n:2 * n]
        send_sems, recv_sems, local_sems = refs[2 * n:]
        x, y, c = lax.axis_index("x"), lax.axis_index("y"), lax.axis_index("c")
        me, sibling = (x, y, c), (x, y, 1 - c)
        chips = [(1 - x, y), (x, 1 - y), (1 - x, 1 - y)]

        def slot(dev):
            return 4 * dev[0] + 2 * dev[1] + dev[2]

        def copy(b, k, block, to, src=None):
            dst = outs[b].at[slot(block)]
            return pltpu.make_async_remote_copy(
                src_ref=dst if src is None else src, dst_ref=dst, send_sem=send_sems.at[b, k],
                recv_sem=recv_sems.at[b, k], device_id=to, device_id_type=MESH)

        mine = [pltpu.make_async_copy(ins[b], outs[b].at[slot(me)], local_sems.at[b]) for b in range(n)]
        first, passed = [], []
        for b in range(n):
            mine[b].start()
            first.append(copy(b, 0, me, sibling, src=ins[b]))
            first += [copy(b, 1 + j, me, (*chip, c), src=ins[b]) for j, chip in enumerate(chips)]
        for cp in first:
            cp.start()
        for j, chip in enumerate(chips):
            for b in range(n):
                copy(b, 1 + j, (*chip, c), me).wait_recv()
                fwd = copy(b, 4 + j, (*chip, c), sibling)
                fwd.start()
                passed.append(fwd)
        for b in range(n):
            copy(b, 0, sibling, me).wait_recv()
            for j, chip in enumerate(chips):
                copy(b, 4 + j, (*chip, 1 - c), me).wait_recv()
        for cp in first + passed:
            cp.wait_send()
        for cp in mine:
            cp.wait()

    return pl.pallas_call(
        body, name=name,
        in_specs=[ANY] * n, out_specs=[ANY] * n,
        out_shape=[jax.ShapeDtypeStruct((N_DEV,) + s.shape, s.dtype) for s in shards],
        scratch_shapes=[pltpu.SemaphoreType.DMA((n, 7)), pltpu.SemaphoreType.DMA((n, 7)),
                        pltpu.SemaphoreType.DMA((n,))],
    )(*shards)


HBM = pl.BlockSpec(memory_space=pltpu.HBM)
SEM = pl.BlockSpec(memory_space=pltpu.SEMAPHORE)
EFFECT = pltpu.SideEffectType.DATAFLOW_SIDE_EFFECTING
FLIPS = ((0, 0, 1), (0, 1, 0), (0, 1, 1), (1, 0, 0), (1, 0, 1), (1, 1, 0), (1, 1, 1))


def _flip_peers():
    pos = (lax.axis_index("x"), lax.axis_index("y"), lax.axis_index("c"))
    return [tuple(1 - a if f else a for a, f in zip(pos, flip)) for flip in FLIPS]


def _hbm(a):
    return pltpu.with_memory_space_constraint(a, pltpu.HBM)


def _split_start(name, srcs, lands, plan, n_copies, after):
    n, m = len(srcs), len(lands)

    def body(*refs):
        send_sems, recv_sems, token = refs[n + m + 1], refs[n + m + 2], refs[-1]
        for i, (src, dst, peer) in enumerate(plan(refs[:n], refs[n:n + m])):
            pltpu.make_async_remote_copy(src_ref=src, dst_ref=dst, send_sem=send_sems.at[i], recv_sem=recv_sems.at[i],
                                         device_id=peer, device_id_type=MESH).start()
        token[...] = jnp.zeros_like(token)

    outs = pl.pallas_call(
        body, name=name + "_start",
        in_specs=[HBM] * (n + m) + [ANY],
        out_specs=[SEM, SEM] + [HBM] * (n + m) + [pl.BlockSpec(memory_space=pltpu.VMEM)],
        out_shape=[pltpu.SemaphoreType.DMA((n_copies,)), pltpu.SemaphoreType.DMA((n_copies,))]
        + [pltpu.HBM(a.shape, a.dtype) for a in list(srcs) + list(lands)] + [jax.ShapeDtypeStruct((8, 128), F32)],
        input_output_aliases={i: 2 + i for i in range(n + m)},
        compiler_params=pltpu.CompilerParams(has_side_effects=EFFECT),
    )(*[_hbm(a) for a in list(srcs) + list(lands)], after)
    return (outs[0], outs[1], outs[2:2 + n], outs[2 + n:2 + n + m]), outs[-1]


def _split_wait(name, started, plan, after):
    send_sems, recv_sems, srcs, lands = started
    n, m = len(srcs), len(lands)

    def body(*refs):
        send_ref, recv_ref = refs[n + m], refs[n + m + 1]
        for i, (src, dst, peer) in enumerate(plan(refs[:n], refs[n:n + m])):
            copy = pltpu.make_async_remote_copy(src_ref=src, dst_ref=dst, send_sem=send_ref.at[i],
                                                recv_sem=recv_ref.at[i], device_id=peer, device_id_type=MESH)
            copy.wait_send()
            copy.wait_recv()

    outs = pl.pallas_call(
        body, name=name + "_wait",
        in_specs=[HBM] * (n + m) + [SEM, SEM, ANY],
        out_specs=[HBM] * (n + m),
        out_shape=[pltpu.HBM(a.shape, a.dtype) for a in list(srcs) + list(lands)],
        input_output_aliases={i: i for i in range(n + m)},
        compiler_params=pltpu.CompilerParams(has_side_effects=EFFECT),
    )(*srcs, *lands, send_sems, recv_sems, after)
    return outs[:n], outs[n:]


def _gather_plan(srcs, lands):
    slot = 4 * lax.axis_index("x") + 2 * lax.axis_index("y") + lax.axis_index("c")
    return [(src, land.at[slot], peer) for src, land in zip(srcs, lands) for peer in _flip_peers()]


def _sibling_plan(srcs, lands):
    x, y, c = lax.axis_index("x"), lax.axis_index("y"), lax.axis_index("c")
    return [(src.at[k, 1 - c], land.at[k], (x, y, 1 - c)) for src, land in zip(srcs, lands) for k in range(N_CHIP)]


def _chip_plan(srcs, lands):
    x, y, c = lax.axis_index("x"), lax.axis_index("y"), lax.axis_index("c")
    return [(src.at[2 * cx + cy], land.at[2 * x + y], (cx, cy, c))
            for src, land in zip(srcs, lands) for cx, cy in ((1 - x, y), (x, 1 - y), (1 - x, 1 - y))]


def _row_tile(rows):
    for tr in range(min(rows, 512), 15, -16):
        if rows % tr == 0:
            return tr
    return rows


def _sibling_exchange(gs):
    n = len(gs)

    def body(*refs):
        g_refs, land_refs = refs[:n], refs[n:2 * n]
        send_sems, recv_sems = refs[2 * n:]
        x, y, c = lax.axis_index("x"), lax.axis_index("y"), lax.axis_index("c")
        copies = [pltpu.make_async_remote_copy(
            src_ref=g_refs[b].at[k, 1 - c], dst_ref=land_refs[b].at[k], send_sem=send_sems.at[b, k],
            recv_sem=recv_sems.at[b, k], device_id=(x, y, 1 - c), device_id_type=MESH)
            for b in range(n) for k in range(N_CHIP)]
        for cp in copies:
            cp.start()
        for cp in copies:
            cp.wait()

    return pl.pallas_call(
        body, name="rs_sibling_exchange", in_specs=[ANY] * n, out_specs=[ANY] * n,
        out_shape=[jax.ShapeDtypeStruct((N_CHIP,) + g.shape[2:], g.dtype) for g in gs],
        scratch_shapes=[pltpu.SemaphoreType.DMA((n, N_CHIP)), pltpu.SemaphoreType.DMA((n, N_CHIP))],
    )(*gs)


def _pair_sum(g, land, core, name):
    rows, cols = land.shape[1:]
    tr = _row_tile(rows)

    def body(c_ref, g_ref, l_ref, o_ref):
        o_ref[...] = (g_ref[...].astype(F32) + l_ref[...].astype(F32)).astype(o_ref.dtype)

    return pl.pallas_call(
        body, name=f"rs_pair_sum_{name}",
        grid_spec=pltpu.PrefetchScalarGridSpec(
            num_scalar_prefetch=1, grid=(N_CHIP, rows // tr),
            in_specs=[pl.BlockSpec((None, None, tr, cols), lambda k, i, c_ref: (k, c_ref[0], i, 0)),
                      pl.BlockSpec((None, tr, cols), lambda k, i, c_ref: (k, i, 0))],
            out_specs=pl.BlockSpec((None, tr, cols), lambda k, i, c_ref: (k, i, 0))),
        out_shape=jax.ShapeDtypeStruct(land.shape, land.dtype),
        compiler_params=_cparams("parallel", "parallel"),
    )(core, g, land)


def _chip_exchange(parts):
    n = len(parts)

    def body(*refs):
        p_refs, land_refs = refs[:n], refs[n:2 * n]
        send_sems, recv_sems, local_sems = refs[2 * n:]
        x, y, c = lax.axis_index("x"), lax.axis_index("y"), lax.axis_index("c")
        mine = 2 * x + y
        chips = [(1 - x, y), (x, 1 - y), (1 - x, 1 - y)]
        own = [pltpu.make_async_copy(p_refs[b].at[mine], land_refs[b].at[mine], local_sems.at[b]) for b in range(n)]
        for cp in own:
            cp.start()
        copies = [pltpu.make_async_remote_copy(
            src_ref=p_refs[b].at[2 * cx + cy], dst_ref=land_refs[b].at[mine], send_sem=send_sems.at[b, j],
            recv_sem=recv_sems.at[b, j], device_id=(cx, cy, c), device_id_type=MESH)
            for b in range(n) for j, (cx, cy) in enumerate(chips)]
        for cp in copies:
            cp.start()
        for b in range(n):
            for j, (cx, cy) in enumerate(chips):
                pltpu.make_async_remote_copy(
                    src_ref=p_refs[b].at[mine], dst_ref=land_refs[b].at[2 * cx + cy], send_sem=send_sems.at[b, j],
                    recv_sem=recv_sems.at[b, j], device_id=(cx, cy, c), device_id_type=MESH).wait_recv()
        for cp in copies:
            cp.wait_send()
        for cp in own:
            cp.wait()

    return pl.pallas_call(
        body, name="rs_chip_exchange", in_specs=[ANY] * n, out_specs=[ANY] * n,
        out_shape=[jax.ShapeDtypeStruct(p.shape, p.dtype) for p in parts],
        scratch_shapes=[pltpu.SemaphoreType.DMA((n, 3)), pltpu.SemaphoreType.DMA((n, 3)),
                        pltpu.SemaphoreType.DMA((n,))],
    )(*parts)


def _adamw(own, arrived, chip, w, m, v, name):
    k, rows, cols = arrived.shape
    tr = _row_tile(rows)
    c1 = 1.0 / (1.0 - ADAM_B1 ** ADAM_STEP)
    c2 = 1.0 / (1.0 - ADAM_B2 ** ADAM_STEP)

    def body(chip_ref, o_ref, p_ref, w_ref, m_ref, v_ref, g_ref, d_ref, nm_ref, nv_ref):
        def slab(j):
            return jnp.where(chip_ref[0] == j, o_ref[j], p_ref[j]).astype(F32)

        g = slab(0)
        for j in range(1, k):
            g = g + slab(j)
        g_ref[...] = g
        nm = ADAM_B1 * m_ref[...] + (1.0 - ADAM_B1) * g
        nv = ADAM_B2 * v_ref[...] + (1.0 - ADAM_B2) * (g * g)
        nm_ref[...] = nm
        nv_ref[...] = nv
        d_ref[...] = -ADAM_LR * ((nm * c1) / (jnp.sqrt(nv * c2) + ADAM_EPS) + ADAM_WD * w_ref[...])

    blk = pl.BlockSpec((tr, cols), lambda i, c: (i, 0))
    stack = pl.BlockSpec((k, tr, cols), lambda i, c: (0, i, 0))
    return pl.pallas_call(
        body, name=name,
        grid_spec=pltpu.PrefetchScalarGridSpec(num_scalar_prefetch=1, grid=(rows // tr,),
                                               in_specs=[stack, stack, blk, blk, blk], out_specs=[blk] * 4),
        out_shape=[jax.ShapeDtypeStruct((rows, cols), F32)] * 4,
        compiler_params=_cparams("parallel"),
    )(chip, own, arrived, w, m, v)


SMALL_LAYOUT = (("g_mix", 0, 1024), ("conv_b", 1, 512), ("q_norm_g", 2, 64), ("k_norm_g", 3, 64),
                ("g_out_conv", 4, 512), ("g_out_attn", 5, 512), ("g_ffn", 6, 1024), ("ffn_conv_b", 7, 2816),
                ("g_ple", 10, 1024))
CONV_W_ROW = 11
FFN_CONV_W_ROW = 14
LOSS_ROW = 23


def _row_pieces(cols):
    return [(c, min(1024, cols - c)) for c in range(0, cols, 1024)]


def _pack_small(grads, loss_tile):
    names = [n for n, _, _ in SMALL_LAYOUT]

    def body(*refs):
        ins, cw_ref, fcw_ref, loss_ref, out_ref = refs[:len(names)], refs[-4], refs[-3], refs[-2], refs[-1]
        out_ref[...] = jnp.zeros_like(out_ref)
        for ref, (_, row, cols) in zip(ins, SMALL_LAYOUT):
            for j, (c, width) in enumerate(_row_pieces(cols)):
                out_ref[row + j:row + j + 1, 0:width] = ref[:, c:c + width]
        for k in range(3):
            out_ref[CONV_W_ROW + k:CONV_W_ROW + k + 1, 0:CONV_W] = cw_ref[k:k + 1, :]
            for j, (c, width) in enumerate(_row_pieces(D_FF)):
                row = FFN_CONV_W_ROW + 3 * k + j
                out_ref[row:row + 1, 0:width] = fcw_ref[k:k + 1, c:c + width]
        out_ref[LOSS_ROW:LOSS_ROW + 1, 0:128] = loss_ref[0:1, :]

    return pl.pallas_call(
        body, name="pack_small_grads", out_shape=jax.ShapeDtypeStruct((SMALL_ROWS, 1024), F32),
    )(*[grads[n] for n in names], grads["conv_w"], grads["ffn_conv_w"], loss_tile)


def _adamw_small(arrived, conv_parts, fconv_parts, wts, mom, var):
    names = [n for n, _, _ in SMALL_LAYOUT] + ["conv_w", "ffn_conv_w"]
    c1 = 1.0 / (1.0 - ADAM_B1 ** ADAM_STEP)
    c2 = 1.0 / (1.0 - ADAM_B2 ** ADAM_STEP)
    n = len(names)

    def body(*refs):
        land, cw_ref, fcw_ref = refs[0], refs[1], refs[2]
        state = refs[3:3 + 3 * n]
        outs = refs[3 + 3 * n:]

        def total(piece):
            acc = piece(0)
            for d in range(1, N_DEV):
                acc = acc + piece(d)
            return acc

        for i, name in enumerate(names):
            if name == "conv_w":
                g = total(lambda d: cw_ref[d])
            elif name == "ffn_conv_w":
                g = total(lambda d: fcw_ref[d])
            else:
                _, row, cols = SMALL_LAYOUT[i]
                pieces = [total(lambda d, j=j, width=width: land[d, row + j:row + j + 1, 0:width])
                          for j, (_, width) in enumerate(_row_pieces(cols))]
                g = pieces[0] if len(pieces) == 1 else jnp.concatenate(pieces, axis=1)
            w_ref, m_ref, v_ref = state[3 * i:3 * i + 3]
            nm = ADAM_B1 * m_ref[...] + (1.0 - ADAM_B1) * g
            nv = ADAM_B2 * v_ref[...] + (1.0 - ADAM_B2) * (g * g)
            outs[4 * i][...] = g
            outs[4 * i + 1][...] = -ADAM_LR * ((nm * c1) / (jnp.sqrt(nv * c2) + ADAM_EPS) + ADAM_WD * w_ref[...])
            outs[4 * i + 2][...] = nm
            outs[4 * i + 3][...] = nv
        outs[-1][...] = total(lambda d: land[d, LOSS_ROW:LOSS_ROW + 1, 0:128])

    state = [a[nm_] for nm_ in names for a in (wts, mom, var)]
    shapes = [jax.ShapeDtypeStruct(wts[nm_].shape, F32) for nm_ in names for _ in range(4)]
    outs = pl.pallas_call(
        body, name="adamw_small", out_shape=shapes + [jax.ShapeDtypeStruct((1, 128), F32)],
    )(arrived, conv_parts, fconv_parts, *state)
    return {nm_: tuple(outs[4 * i:4 * i + 4]) for i, nm_ in enumerate(names)}, outs[-1][0, 0]


COL_SHARDED = ("w_in", "w_gate", "w_up", "w_ple_proj")
REPLICATED = (("g_mix", 1024), ("conv_b", 512), ("q_norm_g", 64), ("k_norm_g", 64), ("g_out_conv", 512),
              ("g_out_attn", 512), ("g_ffn", 1024), ("ffn_conv_b", 2816), ("g_ple", 1024))
CONV_SHARDED = (("conv_w", CONV_W), ("ffn_conv_w", D_FF))


def _gathered_to_full(name, gathered):
    if name in COL_SHARDED:
        return gathered.transpose(1, 0, 2).reshape(gathered.shape[1], -1)
    return gathered.reshape(-1, gathered.shape[2])


def _full_to_stacked(name, grad, shard_shape):
    sr, sc = shard_shape
    if name in COL_SHARDED:
        a = grad.reshape(sr, N_DEV, sc).transpose(1, 0, 2)
    else:
        a = grad.reshape(N_DEV, sr, sc)
    return a.astype(BF16).reshape(N_CHIP, 2, sr, sc)


def _pad_rows(vec, rows):
    return jnp.pad(vec, (0, rows * 1024 - vec.shape[0])).reshape(rows, 1024)


def kernel(x, p, g_mix, w_in, conv_w, conv_b, q_norm_g, k_norm_g, g_out_conv, g_out_attn, w_out, g_ffn, w_gate, w_up, ffn_conv_w, ffn_conv_b, w_down, g_ple, w_ple_gate, w_ple_proj, loss_target, m_g_mix, m_w_in, m_conv_w, m_conv_b, m_q_norm_g, m_k_norm_g, m_g_out_conv, m_g_out_attn, m_w_out, m_g_ffn, m_w_gate, m_w_up, m_ffn_conv_w, m_ffn_conv_b, m_w_down, m_g_ple, m_w_ple_gate, m_w_ple_proj, v_g_mix, v_w_in, v_conv_w, v_conv_b, v_q_norm_g, v_k_norm_g, v_g_out_conv, v_g_out_attn, v_w_out, v_g_ffn, v_w_gate, v_w_up, v_ffn_conv_w, v_ffn_conv_b, v_w_down, v_g_ple, v_w_ple_gate, v_w_ple_proj):
    args = dict(locals())
    names = ["g_mix", "w_in", "conv_w", "conv_b", "q_norm_g", "k_norm_g", "g_out_conv", "g_out_attn", "w_out", "g_ffn",
             "w_gate", "w_up", "ffn_conv_w", "ffn_conv_b", "w_down", "g_ple", "w_ple_gate", "w_ple_proj"]
    big = [n for n, _ in BIG_ROWS]
    conv = [n for n, _ in CONV_SHARDED]
    wts = {n: (args[n][0] if n in big or n in conv else args[n]) for n in names}
    mom = {n: (args["m_" + n][0] if n in big or n in conv else args["m_" + n]) for n in names}
    var = {n: (args["v_" + n][0] if n in big or n in conv else args["v_" + n]) for n in names}
    shard_shapes = {n: wts[n].shape for n in big}
    dev = 4 * lax.axis_index("x") + 2 * lax.axis_index("y") + lax.axis_index("c")
    core = lax.axis_index("c").astype(jnp.int32).reshape(1)

    conv_local = _pad_rows(jnp.concatenate([wts[n].reshape(-1) for n in conv]), 8).reshape(8, 1024)
    late = [n for n in big if n != "w_in"]
    w_in_all, conv_all = _all_gather([wts["w_in"].astype(BF16), conv_local], "gather_weights")
    late_shards = [wts[n].astype(BF16) for n in late]
    gathering, token = _split_start("gather_late_weights", late_shards,
                                    [lax.empty((N_DEV,) + s.shape, BF16) for s in late_shards], _gather_plan,
                                    7 * len(late), w_in_all)
    full = dict(wts)
    full["w_in"] = _gathered_to_full("w_in", w_in_all)
    full["g_mix"] = _ordered_after(wts["g_mix"], token)
    flying = {}

    def late_weights(after):
        shards, lands = _split_wait("gather_late_weights", gathering, _gather_plan, after)
        whole = {n: lax.dynamic_update_slice(land, shard[None], (dev, 0, 0))
                 for n, land, shard in zip(late, lands, shards)}
        return {n: a if n in ("w_gate", "w_up") else _gathered_to_full(n, a) for n, a in whole.items()}

    early = ["w_ple_gate", "w_ple_proj", "w_down", "w_up", "w_gate"]

    def ffn_grads(g):
        stacked = [_full_to_stacked(n, g[n], shard_shapes[n]) for n in early]
        flying["sibling"], tok = _split_start("rs_sibling_early", stacked,
                                              [lax.empty((N_CHIP,) + s.shape[2:], BF16) for s in stacked],
                                              _sibling_plan, N_CHIP * len(early), g["w_down"])
        return tok

    def outproj_done(after):
        stacked, landed = _split_wait("rs_sibling_early", flying["sibling"], _sibling_plan, after)
        parts = [_pair_sum(g, l, core, n) for n, g, l in zip(early, stacked, landed)]
        flying["chip"], tok = _split_start("rs_chip_early", parts, [lax.empty(q.shape, BF16) for q in parts],
                                           _chip_plan, 3 * len(early), landed[0])
        return tok

    off = 0
    for n, width in CONV_SHARDED:
        sc = width // N_DEV
        a = conv_all.reshape(N_DEV, -1)[:, off:off + 3 * sc].reshape(N_DEV, 3, sc)
        full[n] = a.transpose(1, 0, 2).reshape(3, width)
        off += 3 * sc

    loss, dx, grads = _local_step(x[0], p[0, 0], loss_target[0], full, (512, 256),
                                  {"late_weights": late_weights, "ffn_grads": ffn_grads, "outproj_done": outproj_done})

    chip = (2 * lax.axis_index("x") + lax.axis_index("y")).astype(jnp.int32).reshape(1)

    def adamw_of(group, parts, arrived):
        return {n: _adamw(own, got, chip, wts[n], mom[n], var[n], f"adamw_{n}")
                for n, own, got in zip(group, parts, arrived)}

    last = [n for n in big if n not in early]
    stacked = [_full_to_stacked(n, grads[n], shard_shapes[n]) for n in last]
    flying["sibling_last"], tok = _split_start("rs_sibling_last", stacked,
                                               [lax.empty((N_CHIP,) + s.shape[2:], BF16) for s in stacked],
                                               _sibling_plan, N_CHIP * len(last), dx)
    (small_all,) = _all_gather([_ordered_after(_pack_small(grads, loss), tok)], "gather_small_grads")
    stacked, landed = _split_wait("rs_sibling_last", flying["sibling_last"], _sibling_plan, small_all)
    parts = [_pair_sum(g, l, core, n) for n, g, l in zip(last, stacked, landed)]
    flying["chip_last"], tok = _split_start("rs_chip_last", parts, [lax.empty(q.shape, BF16) for q in parts],
                                            _chip_plan, 3 * len(last), landed[0])

    parts, arrived = _split_wait("rs_chip_early", flying["chip"], _chip_plan, tok)
    out = adamw_of(early, parts, arrived)
    taps = small_all[:, CONV_W_ROW:CONV_W_ROW + 3, 0:CONV_W]
    ftaps = small_all[:, FFN_CONV_W_ROW:FFN_CONV_W_ROW + 9, :].reshape(N_DEV, 3, 3 * 1024)
    small_out, loss_total = _adamw_small(
        small_all, lax.dynamic_slice(taps, (0, 0, dev * (CONV_W // N_DEV)), (N_DEV, 3, CONV_W // N_DEV)),
        lax.dynamic_slice(ftaps, (0, 0, dev * (D_FF // N_DEV)), (N_DEV, 3, D_FF // N_DEV)), wts, mom, var)
    out.update(small_out)
    parts, arrived = _split_wait("rs_chip_last", flying["chip_last"], _chip_plan, small_out["g_mix"][0])
    out.update(adamw_of(last, parts, arrived))
    return (loss_total, dx[None], *[out[n][which].reshape(args[n].shape) for which in range(4) for n in names])
```

```python
import functools

import jax
import jax.numpy as jnp
from jax import lax
from jax.experimental import pallas as pl
from jax.experimental.pallas import tpu as pltpu

F32 = jnp.float32
BF16 = jnp.bfloat16

D_MODEL = 1024
CONV_W = 512
ATTN_W = 512
HEAD_DIM = 64
D_FF = 2816
PLE_DIM = 256
IN_COLS = 3 * CONV_W + 3 * ATTN_W
EPS = 1e-6
QK_BLOCK = 128
DILATIONS = (1, 4, 16)
ATTN_SCALE = HEAD_DIM ** -0.5

ADAM_LR = 0.001
ADAM_B1 = 0.9
ADAM_B2 = 0.999
ADAM_EPS = 1e-08
ADAM_WD = 0.01
ADAM_STEP = 10

N_DEV = 8
N_CHIP = 4
V7X_VMEM_LIMIT = 56 * 1024 * 1024
FF_CHUNKS = 2
FFN_BWD_PARTS = 2

BIG_ROWS = (("w_in", 384), ("w_out", 128), ("w_gate", 352), ("w_up", 352), ("w_down", 352),
            ("w_ple_gate", 128), ("w_ple_proj", 32))
BIG_TOTAL = sum(r for _, r in BIG_ROWS)
SMALL_ROWS = 24


def _cparams(*sem):
    return pltpu.CompilerParams(dimension_semantics=sem, vmem_limit_bytes=V7X_VMEM_LIMIT)


def _mm(a, b):
    return jnp.dot(a, b, preferred_element_type=F32)


def _mm_nt(a, b):
    return lax.dot_general(a, b, (((1,), (1,)), ((), ())), preferred_element_type=F32)


def _mm_tn(a, b):
    return lax.dot_general(a, b, (((0,), (0,)), ((), ())), preferred_element_type=F32)


def _full(shape):
    nd = len(shape)
    return pl.BlockSpec(shape, lambda *_: (0,) * nd)


def _rms_stats(x):
    r = lax.rsqrt(jnp.mean(x * x, axis=-1, keepdims=True) + EPS)
    return r, x * r


def _rms_bwd(dy, xhat, r, g):
    gd = dy * g
    return r * (gd - xhat * jnp.mean(gd * xhat, axis=-1, keepdims=True))


def _seg_sum64(v, bd_ref):
    outs = []
    for c in range(0, v.shape[1], 256):
        vc = v[:, c:c + 256]
        hi = vc.astype(BF16)
        lo = (vc - hi.astype(F32)).astype(BF16)
        outs.append(_mm(hi, bd_ref[...]) + _mm(lo, bd_ref[...]))
    return outs[0] if len(outs) == 1 else jnp.concatenate(outs, axis=1)


def _shift_rows(u, k, edge_rows):
    out = pltpu.roll(u, k, 0)
    row = lax.broadcasted_iota(jnp.int32, (8, u.shape[1]), 0)
    head = out[0:8]
    for j in range(k):
        head = jnp.where(row == j, edge_rows[k - 1 - j], head)
    return jnp.concatenate([head, out[8:]], axis=0)


def _shift_rows_up(u, k, edge_rows):
    n = u.shape[0]
    out = pltpu.roll(u, n - k, 0)
    row = lax.broadcasted_iota(jnp.int32, (8, u.shape[1]), 0)
    tail = out[n - 8:n]
    for j in range(k):
        tail = jnp.where(row == 8 - k + j, edge_rows[j], tail)
    return jnp.concatenate([out[0:n - 8], tail], axis=0)


def _conv_fwd(u, c1, c2, w_ref, b_ref):
    u1 = _shift_rows(u, 1, (c1,))
    u2 = _shift_rows(u, 2, (c1, c2))
    y = u2 * w_ref[0:1, :] + u1 * w_ref[1:2, :] + u * w_ref[2:3, :] + b_ref[...]
    return y, u1, u2


def _conv_bwd_input(dy, n1row, n2row, w_ref):
    d1 = _shift_rows_up(dy, 1, (n1row,))
    d2 = _shift_rows_up(dy, 2, (n1row, n2row))
    return dy * w_ref[2:3, :] + d1 * w_ref[1:2, :] + d2 * w_ref[0:1, :]


def _sigmoid(x):
    return 1.0 / (1.0 + jnp.exp(-x))


def _inproj_fwd(x, g_mix, w_in, conv_w, conv_b, qg, kg, bd, tm):
    t = x.shape[0]

    def body(x_ref, g_ref, w_ref, cw_ref, cb_ref, qg_ref, kg_ref, bd_ref,
             zc_ref, zqk_ref, yc_ref, q_ref, k_ref, v_ref, carry_ref):
        @pl.when(pl.program_id(0) == 0)
        def _():
            carry_ref[...] = jnp.zeros_like(carry_ref)

        _, xhat = _rms_stats(x_ref[...])
        h = (xhat * g_ref[...]).astype(BF16)
        zconv = _mm(h, w_ref[:, 0:3 * CONV_W])
        zc_ref[...] = zconv.astype(BF16)
        u = zconv[:, CONV_W:2 * CONV_W] * zconv[:, 2 * CONV_W:3 * CONV_W]
        cv, _, _ = _conv_fwd(u, carry_ref[7:8, :], carry_ref[6:7, :], cw_ref, cb_ref)
        yc_ref[...] = (zconv[:, 0:CONV_W] * cv).astype(BF16)
        carry_ref[...] = u[tm - 8:tm, :]

        zqk = _mm(h, w_ref[:, 3 * CONV_W:3 * CONV_W + 2 * ATTN_W])
        zqk_ref[...] = zqk.astype(BF16)
        for j, (gain_ref, out_ref, scale) in enumerate(((qg_ref, q_ref, ATTN_SCALE), (kg_ref, k_ref, 1.0))):
            z = zqk[:, j * ATTN_W:(j + 1) * ATTN_W]
            r = lax.rsqrt(_seg_sum64(z * z, bd_ref) * (1.0 / HEAD_DIM) + EPS)
            out_ref[...] = z * r * gain_ref[...] * scale
        v_ref[...] = _mm(h, w_ref[:, 3 * CONV_W + 2 * ATTN_W:IN_COLS])

    def blk(c):
        return pl.BlockSpec((tm, c), lambda i: (i, 0))

    return pl.pallas_call(
        body, name="inproj_fwd", grid=(t // tm,),
        in_specs=[blk(D_MODEL), _full((1, D_MODEL)), _full((D_MODEL, IN_COLS)), _full((3, CONV_W)),
                  _full((1, CONV_W)), _full((1, ATTN_W)), _full((1, ATTN_W)), _full((256, 256))],
        out_specs=[blk(3 * CONV_W), blk(2 * ATTN_W), blk(CONV_W), blk(ATTN_W), blk(ATTN_W), blk(ATTN_W)],
        out_shape=[jax.ShapeDtypeStruct((t, 3 * CONV_W), BF16), jax.ShapeDtypeStruct((t, 2 * ATTN_W), BF16),
                   jax.ShapeDtypeStruct((t, CONV_W), BF16), jax.ShapeDtypeStruct((t, ATTN_W), F32),
                   jax.ShapeDtypeStruct((t, ATTN_W), F32), jax.ShapeDtypeStruct((t, ATTN_W), F32)],
        scratch_shapes=[pltpu.VMEM((8, CONV_W), F32)],
        compiler_params=_cparams("arbitrary"),
    )(x, g_mix, w_in, conv_w, conv_b, qg, kg, bd)


SUPER = 16 * QK_BLOCK
KEYS = 2 * QK_BLOCK


def _rows(start, size, dil):
    return pl.ds(start, size) if dil == 1 else pl.ds(start, size, stride=dil)


def _attn_bias(sl_ref, dil):
    qi = lax.broadcasted_iota(jnp.int32, (KEYS, KEYS), 0)
    kj = lax.broadcasted_iota(jnp.int32, (KEYS, KEYS), 1)
    step = jnp.bitwise_and(qi, QK_BLOCK - 1) + QK_BLOCK - kj
    slope = jnp.where(qi < QK_BLOCK, sl_ref[0, 0:1, 0:1], sl_ref[0, 1:2, 0:1])
    bias = jnp.where(jnp.logical_and(step >= 0, step <= QK_BLOCK), -slope * (step * dil).astype(F32), -jnp.inf)
    return bias, kj >= QK_BLOCK


def _unit_start(u, dil):
    if dil == 1:
        return pl.multiple_of(u * QK_BLOCK, QK_BLOCK)
    if dil == 4:
        return jnp.bitwise_and(u, 3) + (u // 4) * (4 * QK_BLOCK)
    return u


def _stack_heads(a, head0):
    zero = jnp.zeros_like(a)
    return jnp.concatenate([jnp.where(head0, a, zero), jnp.where(head0, zero, a)], axis=0)


def _attn_fwd(q, k, v, slopes):
    t = q.shape[0]
    nsb = t // SUPER

    def body(q_ref, kc_ref, kp_ref, vc_ref, vp_ref, sl_ref, o_ref, l_ref, kk, vv, ob, lb):
        s = pl.program_id(1)
        kk[0:SUPER, :] = kp_ref[...]
        kk[SUPER:, :] = kc_ref[...]
        vv[0:SUPER, :] = vp_ref[...]
        vv[SUPER:, :] = vc_ref[...]
        head0 = lax.broadcasted_iota(jnp.int32, (QK_BLOCK, QK_BLOCK), 1) < HEAD_DIM

        for b, dil in enumerate(DILATIONS):
            bias, own_half = _attn_bias(sl_ref, dil)

            def unit(u, carry, b=b, dil=dil, bias=bias, own_half=own_half):
                start = _unit_start(u, dil)
                first_key = SUPER + start - QK_BLOCK * dil
                q2 = _stack_heads(q_ref[_rows(start, QK_BLOCK, dil), :].astype(BF16), head0)
                k2 = kk[_rows(first_key, KEYS, dil), :].astype(BF16)
                v2 = vv[_rows(first_key, KEYS, dil), :].astype(BF16)
                has_prev = jnp.logical_or(s > 0, start >= QK_BLOCK * dil)
                sc = jnp.where(jnp.logical_or(own_half, has_prev), _mm_nt(q2, k2) + bias, -jnp.inf)
                m = jnp.max(sc, axis=-1, keepdims=True)
                e = jnp.exp(sc - m)
                den = jnp.sum(e, axis=-1, keepdims=True)
                o2 = _mm(e.astype(BF16), v2) / den
                l2 = m + jnp.log(den)
                ob[b, _rows(start, QK_BLOCK, dil), :] = jnp.where(head0, o2[0:QK_BLOCK], o2[QK_BLOCK:])
                lb[b, _rows(start, QK_BLOCK, dil), :] = jnp.where(head0, l2[0:QK_BLOCK], l2[QK_BLOCK:])
                return carry

            lax.fori_loop(0, SUPER // QK_BLOCK, unit, 0, unroll=16)

        def merge(i, carry):
            rows = pl.ds(pl.multiple_of(i * 256, 256), 256)
            la, lb_, lc = lb[0, rows, :], lb[1, rows, :], lb[2, rows, :]
            mx = jnp.maximum(jnp.maximum(la, lb_), lc)
            wa, wb, wc = jnp.exp(la - mx), jnp.exp(lb_ - mx), jnp.exp(lc - mx)
            sw = wa + wb + wc
            o_ref[rows, :] = ((wa * ob[0, rows, :] + wb * ob[1, rows, :] + wc * ob[2, rows, :]) / sw).astype(BF16)
            l_ref[rows, :] = mx + jnp.log(sw)
            return carry

        lax.fori_loop(0, SUPER // 256, merge, 0)

    cur = pl.BlockSpec((SUPER, QK_BLOCK), lambda p, s: (s, p))
    prev = pl.BlockSpec((SUPER, QK_BLOCK), lambda p, s: (jnp.maximum(s - 1, 0), p))
    return pl.pallas_call(
        body, name="attn_fwd", grid=(4, nsb),
        in_specs=[cur, cur, prev, cur, prev, pl.BlockSpec((1, 2, QK_BLOCK), lambda p, s: (p, 0, 0))],
        out_specs=[cur, cur],
        out_shape=[jax.ShapeDtypeStruct((t, ATTN_W), BF16), jax.ShapeDtypeStruct((t, ATTN_W), F32)],
        scratch_shapes=[pltpu.VMEM((2 * SUPER, QK_BLOCK), F32), pltpu.VMEM((2 * SUPER, QK_BLOCK), F32),
                        pltpu.VMEM((3, SUPER, QK_BLOCK), F32), pltpu.VMEM((3, SUPER, QK_BLOCK), F32)],
        compiler_params=_cparams("parallel", "arbitrary"),
    )(q, k, k, v, v, slopes)


def _outproj_fwd(ya, yc, x, goc, goa, w_out, tm):
    t = x.shape[0]

    def body(ya_ref, yc_ref, x_ref, goc_ref, goa_ref, w_ref, x1_ref):
        _, ychat = _rms_stats(yc_ref[...].astype(F32))
        _, yahat = _rms_stats(ya_ref[...].astype(F32))
        acc = _mm((ychat * goc_ref[...]).astype(BF16), w_ref[0:CONV_W, :])
        acc += _mm((yahat * goa_ref[...]).astype(BF16), w_ref[CONV_W:, :])
        x1_ref[...] = x_ref[...] + acc

    def blk(c):
        return pl.BlockSpec((tm, c), lambda i: (i, 0))

    return pl.pallas_call(
        body, name="outproj_fwd", grid=(t // tm,),
        in_specs=[blk(ATTN_W), blk(CONV_W), blk(D_MODEL), _full((1, CONV_W)), _full((1, ATTN_W)),
                  _full((D_MODEL, D_MODEL))],
        out_specs=blk(D_MODEL),
        out_shape=jax.ShapeDtypeStruct((t, D_MODEL), F32),
        compiler_params=_cparams("parallel"),
    )(ya, yc, x, goc, goa, w_out)


def _ffn_fwd(x1, g_ffn, w_gate_t, w_up_t, w_down, fcw, fcb, tm):
    t = x1.shape[0]

    def body(x_ref, g_ref, wg_ref, wu_ref, wd_ref, cw_ref, cb_ref, gp_ref, up_ref, h_ref, x2_ref, carry_ref):
        @pl.when(pl.program_id(0) == 0)
        def _():
            carry_ref[...] = jnp.zeros_like(carry_ref)

        xv = x_ref[...]
        _, xhat = _rms_stats(xv)
        h = (xhat * g_ref[...]).astype(BF16)
        h_ref[...] = h
        gp = _mm_nt(h, wg_ref[...])
        gp_ref[...] = gp.astype(BF16)
        gate, _, _ = _conv_fwd(gp, carry_ref[7:8, :], carry_ref[6:7, :], cw_ref, cb_ref)
        carry_ref[...] = gp[tm - 8:tm, :]
        up = _mm_nt(h, wu_ref[...])
        up_ref[...] = up.astype(BF16)
        a = (gate * _sigmoid(gate) * up).astype(BF16)
        x2_ref[...] = xv + _mm(a, wd_ref[...])

    def blk(c):
        return pl.BlockSpec((tm, c), lambda i: (i, 0))

    return pl.pallas_call(
        body, name="ffn_fwd", grid=(t // tm,),
        in_specs=[blk(D_MODEL), _full((1, D_MODEL)), _full((D_FF, D_MODEL)), _full((D_FF, D_MODEL)),
                  _full((D_FF, D_MODEL)), _full((3, D_FF)), _full((1, D_FF))],
        out_specs=[blk(D_FF), blk(D_FF), blk(D_MODEL), blk(D_MODEL)],
        out_shape=[jax.ShapeDtypeStruct((t, D_FF), BF16), jax.ShapeDtypeStruct((t, D_FF), BF16),
                   jax.ShapeDtypeStruct((t, D_MODEL), BF16), jax.ShapeDtypeStruct((t, D_MODEL), F32)],
        scratch_shapes=[pltpu.VMEM((8, D_FF), F32)],
        compiler_params=_cparams("arbitrary"),
    )(x1, g_ffn, w_gate_t, w_up_t, w_down, fcw, fcb)


def _ple_fwd_bwd(x2, p, target, g_ple, w_pg, w_pp, tm):
    t = x2.shape[0]

    def body(x_ref, p_ref, t_ref, g_ref, wg_ref, wp_ref, dx_ref, dxb_ref, loss_ref, dwg_ref, dwp_ref, dg_ref):
        @pl.when(pl.program_id(0) == 0)
        def _():
            loss_ref[...] = jnp.zeros_like(loss_ref)
            dwg_ref[...] = jnp.zeros_like(dwg_ref)
            dwp_ref[...] = jnp.zeros_like(dwp_ref)
            dg_ref[...] = jnp.zeros_like(dg_ref)

        xv = x_ref[...]
        r, xhat = _rms_stats(xv)
        g = g_ref[...]
        h = (xhat * g).astype(BF16)
        pg = _sigmoid(_mm(h, wg_ref[...]))
        pb = p_ref[...].astype(BF16)
        pp = _mm(pb, wp_ref[...])
        err = xv + pg * pp - t_ref[...]
        loss_ref[...] += 0.5 * jnp.sum(jnp.mean(err * err, axis=-1, keepdims=True))
        dx3 = err * (1.0 / D_MODEL)
        d_pp = (dx3 * pg).astype(BF16)
        d_pre = (dx3 * pp * pg * (1.0 - pg)).astype(BF16)
        dwp_ref[...] += _mm_tn(pb, d_pp)
        dwg_ref[...] += _mm_tn(h, d_pre)
        dh = _mm_nt(d_pre, wg_ref[...])
        dg_ref[...] += jnp.sum(dh * xhat, axis=0, keepdims=True)
        dx2 = dx3 + _rms_bwd(dh, xhat, r, g)
        dx_ref[...] = dx2
        dxb_ref[...] = dx2.astype(BF16)

    def blk(c):
        return pl.BlockSpec((tm, c), lambda i: (i, 0))

    return pl.pallas_call(
        body, name="ple_fwd_bwd", grid=(t // tm,),
        in_specs=[blk(D_MODEL), blk(PLE_DIM), blk(D_MODEL), _full((1, D_MODEL)), _full((D_MODEL, D_MODEL)),
                  _full((PLE_DIM, D_MODEL))],
        out_specs=[blk(D_MODEL), blk(D_MODEL), _full((8, 128)), _full((D_MODEL, D_MODEL)),
                   _full((PLE_DIM, D_MODEL)), _full((1, D_MODEL))],
        out_shape=[jax.ShapeDtypeStruct((t, D_MODEL), F32), jax.ShapeDtypeStruct((t, D_MODEL), BF16),
                   jax.ShapeDtypeStruct((8, 128), F32),
                   jax.ShapeDtypeStruct((D_MODEL, D_MODEL), F32), jax.ShapeDtypeStruct((PLE_DIM, D_MODEL), F32),
                   jax.ShapeDtypeStruct((1, D_MODEL), F32)],
        compiler_params=_cparams("arbitrary"),
    )(x2, p, target, g_ple, w_pg, w_pp)


def _ffn_bwd(dx2, h2, gp, up, w_gate, w_up, w_down, fcw, fcb, tm):
    t = dx2.shape[0]
    nblk = t // tm
    fc = D_FF // FF_CHUNKS
    half = tm // FFN_BWD_PARTS

    def body(dx_ref, h_ref, gp_ref, gph_ref, up_ref, wg_ref, wu_ref, wd_ref, cw_ref, cb_ref,
             dh_ref, dwd_ref, dwu_ref, dwg_ref, dcw_ref, dcb_ref, carry_ref, a_scr, dup_scr, dgp_scr):
        i = pl.program_id(1)

        @pl.when(i == 0)
        def _():
            carry_ref[...] = jnp.zeros_like(carry_ref)
            dwd_ref[...] = jnp.zeros_like(dwd_ref)
            dwu_ref[...] = jnp.zeros_like(dwu_ref)
            dwg_ref[...] = jnp.zeros_like(dwg_ref)
            dcw_ref[...] = jnp.zeros_like(dcw_ref)
            dcb_ref[...] = jnp.zeros_like(dcb_ref)

        keep = (i < nblk - 1).astype(F32)
        later = carry_ref[...]
        for hf in reversed(range(FFN_BWD_PARTS)):
            rows = slice(hf * half, (hf + 1) * half)
            dxb = dx_ref[rows, :]
            gp_v = gp_ref[rows, :].astype(F32)
            if hf > 0:
                before = gp_ref[hf * half - 16:hf * half, :].astype(F32)
            else:
                before = gph_ref[...].astype(F32) * keep
            gate, gp1, gp2 = _conv_fwd(gp_v, before[15:16, :], before[14:15, :], cw_ref, cb_ref)
            s = _sigmoid(gate)
            silu = gate * s
            up_v = up_ref[rows, :].astype(F32)
            da = _mm_nt(dxb, wd_ref[...])
            a_scr[rows, :] = (silu * up_v).astype(BF16)
            d_up = (da * silu).astype(BF16)
            dup_scr[rows, :] = d_up
            d_gate = da * up_v * (s * (1.0 + gate * (1.0 - s)))
            d_gp = _conv_bwd_input(d_gate, later[0:1, :], later[1:2, :], cw_ref).astype(BF16)
            dgp_scr[rows, :] = d_gp
            later = d_gate[0:8, :]
            dcw_ref[0:1, :] += jnp.sum(d_gate * gp2, axis=0, keepdims=True)
            dcw_ref[1:2, :] += jnp.sum(d_gate * gp1, axis=0, keepdims=True)
            dcw_ref[2:3, :] += jnp.sum(d_gate * gp_v, axis=0, keepdims=True)
            dcb_ref[...] += jnp.sum(d_gate, axis=0, keepdims=True)
            dh_ref[rows, :] = (_mm(d_gp, wg_ref[...]) + _mm(d_up, wu_ref[...])).astype(BF16)
        carry_ref[...] = later
        dwd_ref[...] += _mm_tn(a_scr[...], dx_ref[...])
        dwu_ref[...] += _mm_tn(dup_scr[...], h_ref[...])
        dwg_ref[...] += _mm_tn(dgp_scr[...], h_ref[...])

    def rev(i):
        return nblk - 1 - i

    one = pl.Buffered(1)
    in_specs = [
        pl.BlockSpec((tm, D_MODEL), lambda j, i: (rev(i), 0)),
        pl.BlockSpec((tm, D_MODEL), lambda j, i: (rev(i), 0)),
        pl.BlockSpec((tm, fc), lambda j, i: (rev(i), j)),
        pl.BlockSpec((16, fc), lambda j, i: (jnp.maximum(rev(i) * (tm // 16) - 1, 0), j)),
        pl.BlockSpec((tm, fc), lambda j, i: (rev(i), j)),
        pl.BlockSpec((fc, D_MODEL), lambda j, i: (j, 0), pipeline_mode=one),
        pl.BlockSpec((fc, D_MODEL), lambda j, i: (j, 0), pipeline_mode=one),
        pl.BlockSpec((fc, D_MODEL), lambda j, i: (j, 0), pipeline_mode=one),
        pl.BlockSpec((3, fc), lambda j, i: (0, j)),
        pl.BlockSpec((1, fc), lambda j, i: (0, j)),
    ]
    out_specs = [
        pl.BlockSpec((None, tm, D_MODEL), lambda j, i: (j, rev(i), 0)),
        pl.BlockSpec((fc, D_MODEL), lambda j, i: (j, 0), pipeline_mode=one),
        pl.BlockSpec((fc, D_MODEL), lambda j, i: (j, 0), pipeline_mode=one),
        pl.BlockSpec((fc, D_MODEL), lambda j, i: (j, 0), pipeline_mode=one),
        pl.BlockSpec((3, fc), lambda j, i: (0, j)),
        pl.BlockSpec((1, fc), lambda j, i: (0, j)),
    ]
    return pl.pallas_call(
        body, name="ffn_bwd", grid=(FF_CHUNKS, nblk), in_specs=in_specs, out_specs=out_specs,
        out_shape=[jax.ShapeDtypeStruct((FF_CHUNKS, t, D_MODEL), BF16), jax.ShapeDtypeStruct((D_FF, D_MODEL), F32),
                   jax.ShapeDtypeStruct((D_FF, D_MODEL), F32), jax.ShapeDtypeStruct((D_FF, D_MODEL), F32),
                   jax.ShapeDtypeStruct((3, D_FF), F32), jax.ShapeDtypeStruct((1, D_FF), F32)],
        scratch_shapes=[pltpu.VMEM((8, fc), F32), pltpu.VMEM((tm, fc), BF16), pltpu.VMEM((tm, fc), BF16),
                        pltpu.VMEM((tm, fc), BF16)],
        compiler_params=_cparams("arbitrary", "arbitrary"),
    )(dx2, h2, gp, gp, up, w_gate, w_up, w_down, fcw, fcb)


def _outproj_bwd(dh2, dx2, x1, g_ffn, w_out, yc, ya, goc, goa, zconv, conv_w, conv_b, bd, tm):
    t = x1.shape[0]
    nblk = t // tm

    def body(dh_ref, dx2_ref, x1_ref, g_ref, w_ref, yc_ref, ya_ref, goc_ref, goa_ref, zc_ref, zch_ref, cw_ref, cb_ref,
             bd_ref, dx1_ref, dya_ref, dd_ref, dzc_ref, dw_ref, dg_ref, dgoc_ref, dgoa_ref, dcw_ref, dcb_ref,
             carry_ref):
        i = pl.program_id(0)

        @pl.when(i == 0)
        def _():
            carry_ref[...] = jnp.zeros_like(carry_ref)
            for ref in (dw_ref, dg_ref, dgoc_ref, dgoa_ref, dcw_ref, dcb_ref):
                ref[...] = jnp.zeros_like(ref)

        keep = (i < nblk - 1).astype(F32)
        dh2_v = dh_ref[0].astype(F32)
        for j in range(1, FF_CHUNKS):
            dh2_v = dh2_v + dh_ref[j].astype(F32)
        r, xhat = _rms_stats(x1_ref[...])
        dg_ref[...] += jnp.sum(dh2_v * xhat, axis=0, keepdims=True)
        dx1 = dx2_ref[...] + _rms_bwd(dh2_v, xhat, r, g_ref[...])
        dx1_ref[...] = dx1
        dx1b = dx1.astype(BF16)
        dy = _mm_nt(dx1b, w_ref[...])

        yc_v = yc_ref[...].astype(F32)
        rc, ychat = _rms_stats(yc_v)
        dw_ref[0:CONV_W, :] += _mm_tn((ychat * goc_ref[...]).astype(BF16), dx1b)
        dyc = dy[:, 0:CONV_W]
        dgoc_ref[...] += jnp.sum(dyc * ychat, axis=0, keepdims=True)
        d_yc = _rms_bwd(dyc, ychat, rc, goc_ref[...])

        ya_v = ya_ref[...].astype(F32)
        ra, yahat = _rms_stats(ya_v)
        dw_ref[CONV_W:, :] += _mm_tn((yahat * goa_ref[...]).astype(BF16), dx1b)
        dya = dy[:, CONV_W:]
        dgoa_ref[...] += jnp.sum(dya * yahat, axis=0, keepdims=True)
        d_ya = _rms_bwd(dya, yahat, ra, goa_ref[...])
        dya_ref[...] = d_ya
        dd_ref[...] = _seg_sum64(d_ya * ya_v, bd_ref)

        zb = zc_ref[:, 0:CONV_W].astype(F32)
        zc = zc_ref[:, CONV_W:2 * CONV_W].astype(F32)
        zx = zc_ref[:, 2 * CONV_W:3 * CONV_W].astype(F32)
        u = zc * zx
        uh = (zch_ref[:, CONV_W:2 * CONV_W].astype(F32) * zch_ref[:, 2 * CONV_W:3 * CONV_W].astype(F32)) * keep
        cv, u1, u2 = _conv_fwd(u, uh[15:16, :], uh[14:15, :], cw_ref, cb_ref)
        d_cv = d_yc * zb
        d_u = _conv_bwd_input(d_cv, carry_ref[0:1, :], carry_ref[1:2, :], cw_ref)
        carry_ref[...] = d_cv[0:8, :]
        dcw_ref[0:1, :] += jnp.sum(d_cv * u2, axis=0, keepdims=True)
        dcw_ref[1:2, :] += jnp.sum(d_cv * u1, axis=0, keepdims=True)
        dcw_ref[2:3, :] += jnp.sum(d_cv * u, axis=0, keepdims=True)
        dcb_ref[...] += jnp.sum(d_cv, axis=0, keepdims=True)
        dzc_ref[:, 0:CONV_W] = (d_yc * cv).astype(BF16)
        dzc_ref[:, CONV_W:2 * CONV_W] = (d_u * zx).astype(BF16)
        dzc_ref[:, 2 * CONV_W:3 * CONV_W] = (d_u * zc).astype(BF16)

    def rev(i):
        return nblk - 1 - i

    def blk(c):
        return pl.BlockSpec((tm, c), lambda i: (rev(i), 0))

    in_specs = [
        pl.BlockSpec((FF_CHUNKS, tm, D_MODEL), lambda i: (0, rev(i), 0)),
        blk(D_MODEL), blk(D_MODEL), _full((1, D_MODEL)), _full((D_MODEL, D_MODEL)),
        blk(CONV_W), blk(ATTN_W), _full((1, CONV_W)), _full((1, ATTN_W)),
        blk(3 * CONV_W),
        pl.BlockSpec((16, 3 * CONV_W), lambda i: (jnp.maximum(rev(i) * (tm // 16) - 1, 0), 0)),
        _full((3, CONV_W)), _full((1, CONV_W)), _full((256, 256)),
    ]
    out_specs = [blk(D_MODEL), blk(ATTN_W), blk(ATTN_W), blk(3 * CONV_W), _full((D_MODEL, D_MODEL)),
                 _full((1, D_MODEL)), _full((1, CONV_W)), _full((1, ATTN_W)), _full((3, CONV_W)), _full((1, CONV_W))]
    return pl.pallas_call(
        body, name="outproj_bwd", grid=(nblk,), in_specs=in_specs, out_specs=out_specs,
        out_shape=[jax.ShapeDtypeStruct((t, D_MODEL), F32), jax.ShapeDtypeStruct((t, ATTN_W), F32),
                   jax.ShapeDtypeStruct((t, ATTN_W), F32), jax.ShapeDtypeStruct((t, 3 * CONV_W), BF16),
                   jax.ShapeDtypeStruct((D_MODEL, D_MODEL), F32), jax.ShapeDtypeStruct((1, D_MODEL), F32),
                   jax.ShapeDtypeStruct((1, CONV_W), F32), jax.ShapeDtypeStruct((1, ATTN_W), F32),
                   jax.ShapeDtypeStruct((3, CONV_W), F32), jax.ShapeDtypeStruct((1, CONV_W), F32)],
        scratch_shapes=[pltpu.VMEM((8, CONV_W), F32)],
        compiler_params=_cparams("arbitrary"),
    )(dh2, dx2, x1, g_ffn, w_out, yc, ya, goc, goa, zconv, zconv, conv_w, conv_b, bd)


def _attn_bwd(q, k, v, dya, lse, dd, slopes):
    t = q.shape[0]
    nsb = t // SUPER

    def body(q_ref, kc_ref, kp_ref, vc_ref, vp_ref, dy_ref, l_ref, d_ref, sl_ref, dq_ref, dk_ref, dv_ref,
             kk, vv, dkacc, dvacc):
        s = pl.program_id(1)

        @pl.when(s == 0)
        def _():
            dkacc[...] = jnp.zeros_like(dkacc)
            dvacc[...] = jnp.zeros_like(dvacc)

        dkacc[0:SUPER, :] = dkacc[SUPER:, :]
        dvacc[0:SUPER, :] = dvacc[SUPER:, :]
        dkacc[SUPER:, :] = jnp.zeros((SUPER, QK_BLOCK), F32)
        dvacc[SUPER:, :] = jnp.zeros((SUPER, QK_BLOCK), F32)

        @pl.when(s < nsb)
        def _():
            kk[0:SUPER, :] = kp_ref[...]
            kk[SUPER:, :] = kc_ref[...]
            vv[0:SUPER, :] = vp_ref[...]
            vv[SUPER:, :] = vc_ref[...]
            head0 = lax.broadcasted_iota(jnp.int32, (QK_BLOCK, QK_BLOCK), 1) < HEAD_DIM

            for b, dil in enumerate(DILATIONS):
                bias, own_half = _attn_bias(sl_ref, dil)

                def unit(u, carry, b=b, dil=dil, bias=bias, own_half=own_half):
                    start = _unit_start(u, dil)
                    first_key = SUPER + start - QK_BLOCK * dil
                    qrows = _rows(start, QK_BLOCK, dil)
                    krows = _rows(first_key, KEYS, dil)
                    q2 = _stack_heads(q_ref[qrows, :].astype(BF16), head0)
                    dy2 = _stack_heads(dy_ref[qrows, :].astype(BF16), head0)
                    lv, dv_ = l_ref[qrows, :], d_ref[qrows, :]
                    l2 = jnp.concatenate([lv[:, 0:1], lv[:, HEAD_DIM:HEAD_DIM + 1]], axis=0)
                    d2 = jnp.concatenate([dv_[:, 0:1], dv_[:, HEAD_DIM:HEAD_DIM + 1]], axis=0)
                    k2 = kk[krows, :].astype(BF16)
                    v2 = vv[krows, :].astype(BF16)
                    has_prev = jnp.logical_or(s > 0, start >= QK_BLOCK * dil)
                    sc = jnp.where(jnp.logical_or(own_half, has_prev), _mm_nt(q2, k2) + bias, -jnp.inf)
                    prob = jnp.exp(sc - l2)
                    ds = (prob * (_mm_nt(dy2, v2) - d2)).astype(BF16)
                    dvacc[krows, :] += _mm_tn(prob.astype(BF16), dy2)
                    dkacc[krows, :] += _mm_tn(ds, q2)
                    dq2 = _mm(ds, k2)
                    dq = jnp.where(head0, dq2[0:QK_BLOCK], dq2[QK_BLOCK:]) * ATTN_SCALE
                    if b == 0:
                        dq_ref[qrows, :] = dq
                    else:
                        dq_ref[qrows, :] += dq
                    return carry

                lax.fori_loop(0, SUPER // QK_BLOCK, unit, 0, unroll=8)

        dk_ref[...] = dkacc[0:SUPER, :]
        dv_ref[...] = dvacc[0:SUPER, :].astype(BF16)

    def cur_map(p, s):
        return (jnp.minimum(s, nsb - 1), p)

    def prev_map(p, s):
        return (jnp.clip(s - 1, 0, nsb - 1), p)

    cur = pl.BlockSpec((SUPER, QK_BLOCK), cur_map)
    prev = pl.BlockSpec((SUPER, QK_BLOCK), prev_map)
    return pl.pallas_call(
        body, name="attn_bwd", grid=(4, nsb + 1),
        in_specs=[cur, cur, prev, cur, prev, cur, cur, cur, pl.BlockSpec((1, 2, QK_BLOCK), lambda p, s: (p, 0, 0))],
        out_specs=[cur, prev, prev],
        out_shape=[jax.ShapeDtypeStruct((t, ATTN_W), F32), jax.ShapeDtypeStruct((t, ATTN_W), F32),
                   jax.ShapeDtypeStruct((t, ATTN_W), BF16)],
        scratch_shapes=[pltpu.VMEM((2 * SUPER, QK_BLOCK), F32)] * 4,
        compiler_params=_cparams("parallel", "arbitrary"),
    )(q, k, k, v, v, dya, lse, dd, slopes)


def _attn_bwd_per_branch_unused(q, k, v, dya, lse, dd, slopes, dil):
    t = q.shape[0]
    length = t // dil
    chunk = _attn_chunk(t, dil)
    nch = length // chunk
    nb = chunk // QK_BLOCK
    nblocks = length // QK_BLOCK
    view = (length, dil * ATTN_W)
    ext = chunk + QK_BLOCK

    def body(q_ref, dy_ref, l_ref, d_ref, k_ref, v_ref, qn_ref, dyn_ref, ln_ref, dn_ref, kh_ref, vh_ref, sl_ref,
             dq_ref, dk_ref, dv_ref, qbuf, dybuf, lbuf, dbuf, kbuf, vbuf, dkacc, dvacc):
        c = pl.program_id(2)
        qbuf[0:chunk, :] = q_ref[...]
        qbuf[chunk:, :] = qn_ref[...]
        dybuf[0:chunk, :] = dy_ref[...].astype(BF16)
        dybuf[chunk:, :] = dyn_ref[...].astype(BF16)
        lbuf[0:chunk, :] = l_ref[...]
        lbuf[chunk:, :] = ln_ref[...]
        dbuf[0:chunk, :] = d_ref[...]
        dbuf[chunk:, :] = dn_ref[...]
        kbuf[0:QK_BLOCK, :] = kh_ref[...]
        kbuf[QK_BLOCK:, :] = k_ref[...]
        vbuf[0:QK_BLOCK, :] = vh_ref[...]
        vbuf[QK_BLOCK:, :] = v_ref[...]
        valid_cur, valid_prev, dist_cur, dist_prev, head0 = _attn_masks(dil)

        def pair(qb, dyb, lv, dv_, kb, vb, valid, dist):
            dq = jnp.zeros((QK_BLOCK, QK_BLOCK), F32)
            dk = jnp.zeros((QK_BLOCK, QK_BLOCK), F32)
            dvv = jnp.zeros((QK_BLOCK, QK_BLOCK), F32)
            for hh in range(2):
                sl = sl_ref[0, hh:hh + 1, :]
                hm = head0 if hh == 0 else jnp.logical_not(head0)
                col = hh * HEAD_DIM
                qm = jnp.where(hm, qb, jnp.zeros_like(qb))
                dym = jnp.where(hm, dyb, jnp.zeros_like(dyb))
                s = jnp.where(valid, _mm_nt(qm, kb) - sl * dist, -jnp.inf)
                prob = jnp.exp(s - lv[:, col:col + 1])
                ds = (prob * (_mm_nt(dym, vb) - dv_[:, col:col + 1])).astype(BF16)
                dvv += _mm_tn(prob.astype(BF16), dym)
                dk += _mm_tn(ds, qm)
                dq += jnp.where(hm, _mm(ds, kb), 0.0)
            return dq, dk, dvv

        def blk(j, carry):
            off = pl.multiple_of(j * QK_BLOCK, QK_BLOCK)
            nxt = pl.multiple_of(off + QK_BLOCK, QK_BLOCK)
            qb = qbuf[pl.ds(off, QK_BLOCK), :]
            dyb = dybuf[pl.ds(off, QK_BLOCK), :]
            lv = lbuf[pl.ds(off, QK_BLOCK), :]
            dv_ = dbuf[pl.ds(off, QK_BLOCK), :]
            dq_c, dk_c, dv_c = pair(qb, dyb, lv, dv_, kbuf[pl.ds(nxt, QK_BLOCK), :], vbuf[pl.ds(nxt, QK_BLOCK), :],
                                    valid_cur, dist_cur)
            dkacc[pl.ds(nxt, QK_BLOCK), :] = dk_c
            dvacc[pl.ds(nxt, QK_BLOCK), :] = dv_c
            has_prev = jnp.logical_or(c > 0, j > 0)
            dq_p, dk_p, dv_p = pair(qb, dyb, lv, dv_, kbuf[pl.ds(off, QK_BLOCK), :], vbuf[pl.ds(off, QK_BLOCK), :],
                                    jnp.logical_and(valid_prev, has_prev), dist_prev)

            @pl.when(j > 0)
            def _():
                dkacc[pl.ds(off, QK_BLOCK), :] += dk_p
                dvacc[pl.ds(off, QK_BLOCK), :] += dv_p

            dq_ref[pl.ds(off, QK_BLOCK), :] = (dq_c + dq_p) * ATTN_SCALE
            return carry

        lax.fori_loop(0, nb, blk, 0)

        @pl.when(c < nch - 1)
        def _():
            _, dk_p, dv_p = pair(qbuf[chunk:, :], dybuf[chunk:, :], lbuf[chunk:, :], dbuf[chunk:, :],
                                 kbuf[chunk:, :], vbuf[chunk:, :], valid_prev, dist_prev)
            dkacc[chunk:, :] += dk_p
            dvacc[chunk:, :] += dv_p

        dk_ref[...] = dkacc[QK_BLOCK:, :]
        dv_ref[...] = dvacc[QK_BLOCK:, :]

    def cmap(p, r, c):
        return (c, r * 4 + p)

    def before(p, r, c):
        return (jnp.maximum(c * nb - 1, 0), r * 4 + p)

    def after(p, r, c):
        return (jnp.minimum((c + 1) * nb, nblocks - 1), r * 4 + p)

    main = pl.BlockSpec((chunk, QK_BLOCK), cmap)
    hb = pl.BlockSpec((QK_BLOCK, QK_BLOCK), before)
    ha = pl.BlockSpec((QK_BLOCK, QK_BLOCK), after)
    qv, kv, vv = q.reshape(view), k.reshape(view), v.reshape(view)
    dyv, lv, ddv = dya.reshape(view), lse.reshape(view), dd.reshape(view)
    outs = pl.pallas_call(
        body, name=f"attn_bwd_d{dil}", grid=(4, dil, nch),
        in_specs=[main] * 6 + [ha] * 4 + [hb] * 2 + [pl.BlockSpec((1, 2, QK_BLOCK), lambda p, r, c: (p, 0, 0))],
        out_specs=[main] * 3,
        out_shape=[jax.ShapeDtypeStruct(view, F32)] * 3,
        scratch_shapes=[pltpu.VMEM((ext, QK_BLOCK), BF16), pltpu.VMEM((ext, QK_BLOCK), BF16),
                        pltpu.VMEM((ext, QK_BLOCK), F32), pltpu.VMEM((ext, QK_BLOCK), F32),
                        pltpu.VMEM((ext, QK_BLOCK), BF16), pltpu.VMEM((ext, QK_BLOCK), BF16),
                        pltpu.VMEM((ext, QK_BLOCK), F32), pltpu.VMEM((ext, QK_BLOCK), F32)],
        compiler_params=_cparams("arbitrary", "arbitrary", "arbitrary"),
    )(qv, dyv, lv, ddv, kv, vv, qv, dyv, lv, ddv, kv, vv, slopes)
    return [o.reshape(t, ATTN_W) for o in outs]


def _inproj_bwd(dq, dk, dv, dzconv, zqk, x, dx1, g_mix, w_in, qg, kg, bd, tm):
    t = x.shape[0]

    def body(dq_ref, dk_ref, dv_ref, dzc_ref, zqk_ref, x_ref, dx1_ref, g_ref, w_ref, qg_ref,
             kg_ref, bd_ref, dx_ref, dw_ref, dg_ref, dqg_ref, dkg_ref):
        @pl.when(pl.program_id(0) == 0)
        def _():
            for ref in (dw_ref, dg_ref, dqg_ref, dkg_ref):
                ref[...] = jnp.zeros_like(ref)

        parts = [dzc_ref[...]]
        for j, (dn_ref, gain_ref, dgain_ref) in enumerate(((dq_ref, qg_ref, dqg_ref), (dk_ref, kg_ref, dkg_ref))):
            dn = dn_ref[...]
            z = zqk_ref[:, j * ATTN_W:(j + 1) * ATTN_W].astype(F32)
            r = lax.rsqrt(_seg_sum64(z * z, bd_ref) * (1.0 / HEAD_DIM) + EPS)
            zhat = z * r
            dgain_ref[...] += jnp.sum(dn * zhat, axis=0, keepdims=True)
            gd = dn * gain_ref[...]
            parts.append((r * (gd - zhat * (_seg_sum64(gd * zhat, bd_ref) * (1.0 / HEAD_DIM)))).astype(BF16))
        parts.append(dv_ref[...].astype(BF16))
        dz = jnp.concatenate(parts, axis=1)

        r, xhat = _rms_stats(x_ref[...])
        g = g_ref[...]
        dw_ref[...] += _mm_tn((xhat * g).astype(BF16), dz)
        dh = _mm_nt(dz, w_ref[...])
        dg_ref[...] += jnp.sum(dh * xhat, axis=0, keepdims=True)
        dx_ref[...] = dx1_ref[...] + _rms_bwd(dh, xhat, r, g)

    def blk(c):
        return pl.BlockSpec((tm, c), lambda i: (i, 0))

    return pl.pallas_call(
        body, name="inproj_bwd", grid=(t // tm,),
        in_specs=[blk(ATTN_W)] * 3 + [blk(3 * CONV_W), blk(2 * ATTN_W), blk(D_MODEL), blk(D_MODEL), _full((1, D_MODEL)),
                                      _full((D_MODEL, IN_COLS)), _full((1, ATTN_W)), _full((1, ATTN_W)),
                                      _full((256, 256))],
        out_specs=[blk(D_MODEL), _full((D_MODEL, IN_COLS)), _full((1, D_MODEL)), _full((1, ATTN_W)),
                   _full((1, ATTN_W))],
        out_shape=[jax.ShapeDtypeStruct((t, D_MODEL), F32), jax.ShapeDtypeStruct((D_MODEL, IN_COLS), F32),
                   jax.ShapeDtypeStruct((1, D_MODEL), F32), jax.ShapeDtypeStruct((1, ATTN_W), F32),
                   jax.ShapeDtypeStruct((1, ATTN_W), F32)],
        compiler_params=_cparams("arbitrary"),
    )(dq, dk, dv, dzconv, zqk, x, dx1, g_mix, w_in, qg, kg, bd)


def _ordered_after(a, token):
    return a if token is None else a + token[0:1, 0:1].reshape((1,) * a.ndim)


def _local_step(x, p, target, w, tms, hooks=None):
    hooks = hooks or {}
    bd = jnp.kron(jnp.eye(4, dtype=F32), jnp.ones((HEAD_DIM, HEAD_DIM), F32)).astype(BF16)
    qg = jnp.tile(w["q_norm_g"], (1, 8))
    kg = jnp.tile(w["k_norm_g"], (1, 8))
    slopes = jnp.exp2(-jnp.arange(1, 9, dtype=F32))
    slopes = jnp.broadcast_to(slopes.reshape(4, 2, 1), (4, 2, QK_BLOCK))

    zconv, zqk, yc, q, k, v = _inproj_fwd(x, w["g_mix"], w["w_in"], w["conv_w"], w["conv_b"], qg, kg, bd, tms[0])
    ya, lse = _attn_fwd(q, k, v, slopes)
    if "late_weights" in hooks:
        w = {**w, **hooks["late_weights"](lse)}
    x1 = _outproj_fwd(ya, yc, x, w["g_out_conv"], w["g_out_attn"], w["w_out"], tms[0])
    gp, up, h2, x2 = _ffn_fwd(x1, w["g_ffn"], w["w_gate"], w["w_up"], w["w_down"], w["ffn_conv_w"], w["ffn_conv_b"],
                              tms[1])
    dx2, dx2b, loss, dw_pg, dw_pp, dg_ple = _ple_fwd_bwd(x2, p, target, w["g_ple"], w["w_ple_gate"], w["w_ple_proj"], tms[0])
    dh2, dw_down, dw_up, dw_gate, dfcw, dfcb = _ffn_bwd(dx2b, h2, gp, up, w["w_gate"], w["w_up"], w["w_down"],
                                                        w["ffn_conv_w"], w["ffn_conv_b"], tms[0])
    token = None
    if "ffn_grads" in hooks:
        token = hooks["ffn_grads"]({"w_ple_gate": dw_pg, "w_ple_proj": dw_pp, "w_down": dw_down, "w_up": dw_up,
                                    "w_gate": dw_gate})
    dx1, dya, dd, dzconv, dw_out, dg_ffn, dgoc, dgoa, dcw, dcb = _outproj_bwd(
        dh2, dx2, x1, _ordered_after(w["g_ffn"], token), w["w_out"], yc, ya, w["g_out_conv"], w["g_out_attn"], zconv,
        w["conv_w"], w["conv_b"], bd, tms[1])
    token = hooks["outproj_done"](dx1) if "outproj_done" in hooks else None
    dq, dk, dv = _attn_bwd(q, k, v, dya, lse, dd, _ordered_after(slopes, token))
    dx, dw_in, dg_mix, dqg, dkg = _inproj_bwd(dq, dk, dv, dzconv, zqk, x, dx1, w["g_mix"], w["w_in"], qg, kg, bd,
                                              tms[0])
    grads = {
        "g_mix": dg_mix, "w_in": dw_in, "conv_w": dcw, "conv_b": dcb,
        "q_norm_g": dqg.reshape(8, HEAD_DIM).sum(0, keepdims=True),
        "k_norm_g": dkg.reshape(8, HEAD_DIM).sum(0, keepdims=True),
        "g_out_conv": dgoc, "g_out_attn": dgoa, "w_out": dw_out, "g_ffn": dg_ffn, "w_gate": dw_gate, "w_up": dw_up,
        "ffn_conv_w": dfcw, "ffn_conv_b": dfcb, "w_down": dw_down, "g_ple": dg_ple, "w_ple_gate": dw_pg,
        "w_ple_proj": dw_pp,
    }
    return loss, dx, grads


ANY = pl.BlockSpec(memory_space=pl.ANY)
MESH = pl.DeviceIdType.MESH


def _all_gather(shards, name):
    n = len(shards)

    def body(*refs):
        ins, outs = refs[:n], refs[n:2 * n]
        send_sems, recv_sems, local_sems = refs[2 * n:]
        x, y, c = lax.axis_index("x"), lax.axis_index("y"), lax.axis_index("c")
        me, sibling = (x, y, c), (x, y, 1 - c)
        chips = [(1 - x, y), (x, 1 - y), (1 - x, 1 - y)]

        def slot(dev):
            return 4 * dev[0] + 2 * dev[1] + dev[2]

        def copy(b, k, block, to, src=None):
            dst = outs[b].at[slot(block)]
            return pltpu.make_async_remote_copy(
                src_ref=dst if src is None else src, dst_ref=dst, send_sem=send_sems.at[b, k],
                recv_sem=recv_sems.at[b, k], device_id=to, device_id_type=MESH)

        mine = [pltpu.make_async_copy(ins[b], outs[b].at[slot(me)], local_sems.at[b]) for b in range(n)]
        first, passed = [], []
        for b in range(n):
            mine[b].start()
            first.append(copy(b, 0, me, sibling, src=ins[b]))
            first += [copy(b, 1 + j, me, (*chip, c), src=ins[b]) for j, chip in enumerate(chips)]
        for cp in first:
            cp.start()
        for j, chip in enumerate(chips):
            for b in range(n):
                copy(b, 1 + j, (*chip, c), me).wait_recv()
                fwd = copy(b, 4 + j, (*chip, c), sibling)
                fwd.start()
                passed.append(fwd)
        for b in range(n):
            copy(b, 0, sibling, me).wait_recv()
            for j, chip in enumerate(chips):
                copy(b, 4 + j, (*chip, 1 - c), me).wait_recv()
        for cp in first + passed:
            cp.wait_send()
        for cp in mine:
            cp.wait()

    return pl.pallas_call(
        body, name=name,
        in_specs=[ANY] * n, out_specs=[ANY] * n,
        out_shape=[jax.ShapeDtypeStruct((N_DEV,) + s.shape, s.dtype) for s in shards],
        scratch_shapes=[pltpu.SemaphoreType.DMA((n, 7)), pltpu.SemaphoreType.DMA((n, 7)),
                        pltpu.SemaphoreType.DMA((n,))],
    )(*shards)


HBM = pl.BlockSpec(memory_space=pltpu.HBM)
SEM = pl.BlockSpec(memory_space=pltpu.SEMAPHORE)
EFFECT = pltpu.SideEffectType.DATAFLOW_SIDE_EFFECTING
FLIPS = ((0, 0, 1), (0, 1, 0), (0, 1, 1), (1, 0, 0), (1, 0, 1), (1, 1, 0), (1, 1, 1))


def _flip_peers():
    pos = (lax.axis_index("x"), lax.axis_index("y"), lax.axis_index("c"))
    return [tuple(1 - a if f else a for a, f in zip(pos, flip)) for flip in FLIPS]


def _hbm(a):
    return pltpu.with_memory_space_constraint(a, pltpu.HBM)


def _split_start(name, srcs, lands, plan, n_copies, after):
    n, m = len(srcs), len(lands)

    def body(*refs):
        send_sems, recv_sems, token = refs[n + m + 1], refs[n + m + 2], refs[-1]
        for i, (src, dst, peer) in enumerate(plan(refs[:n], refs[n:n + m])):
            pltpu.make_async_remote_copy(src_ref=src, dst_ref=dst, send_sem=send_sems.at[i], recv_sem=recv_sems.at[i],
                                         device_id=peer, device_id_type=MESH).start()
        token[...] = jnp.zeros_like(token)

    outs = pl.pallas_call(
        body, name=name + "_start",
        in_specs=[HBM] * (n + m) + [ANY],
        out_specs=[SEM, SEM] + [HBM] * (n + m) + [pl.BlockSpec(memory_space=pltpu.VMEM)],
        out_shape=[pltpu.SemaphoreType.DMA((n_copies,)), pltpu.SemaphoreType.DMA((n_copies,))]
        + [pltpu.HBM(a.shape, a.dtype) for a in list(srcs) + list(lands)] + [jax.ShapeDtypeStruct((8, 128), F32)],
        input_output_aliases={i: 2 + i for i in range(n + m)},
        compiler_params=pltpu.CompilerParams(has_side_effects=EFFECT),
    )(*[_hbm(a) for a in list(srcs) + list(lands)], after)
    return (outs[0], outs[1], outs[2:2 + n], outs[2 + n:2 + n + m]), outs[-1]


def _split_wait(name, started, plan, after):
    send_sems, recv_sems, srcs, lands = started
    n, m = len(srcs), len(lands)

    def body(*refs):
        send_ref, recv_ref = refs[n + m], refs[n + m + 1]
        for i, (src, dst, peer) in enumerate(plan(refs[:n], refs[n:n + m])):
            copy = pltpu.make_async_remote_copy(src_ref=src, dst_ref=dst, send_sem=send_ref.at[i],
                                                recv_sem=recv_ref.at[i], device_id=peer, device_id_type=MESH)
            copy.wait_send()
            copy.wait_recv()

    outs = pl.pallas_call(
        body, name=name + "_wait",
        in_specs=[HBM] * (n + m) + [SEM, SEM, ANY],
        out_specs=[HBM] * (n + m),
        out_shape=[pltpu.HBM(a.shape, a.dtype) for a in list(srcs) + list(lands)],
        input_output_aliases={i: i for i in range(n + m)},
        compiler_params=pltpu.CompilerParams(has_side_effects=EFFECT),
    )(*srcs, *lands, send_sems, recv_sems, after)
    return outs[:n], outs[n:]


def _gather_plan(srcs, lands):
    slot = 4 * lax.axis_index("x") + 2 * lax.axis_index("y") + lax.axis_index("c")
    return [(src, land.at[slot], peer) for src, land in zip(srcs, lands) for peer in _flip_peers()]


def _sibling_plan(srcs, lands):
    x, y, c = lax.axis_index("x"), lax.axis_index("y"), lax.axis_index("c")
    return [(src.at[k, 1 - c], land.at[k], (x, y, 1 - c)) for src, land in zip(srcs, lands) for k in range(N_CHIP)]


def _chip_plan(srcs, lands):
    x, y, c = lax.axis_index("x"), lax.axis_index("y"), lax.axis_index("c")
    return [(src.at[2 * cx + cy], land.at[2 * x + y], (cx, cy, c))
            for src, land in zip(srcs, lands) for cx, cy in ((1 - x, y), (x, 1 - y), (1 - x, 1 - y))]


def _row_tile(rows):
    for tr in range(min(rows, 512), 15, -16):
        if rows % tr == 0:
            return tr
    return rows


def _sibling_exchange(gs):
    n = len(gs)

    def body(*refs):
        g_refs, land_refs = refs[:n], refs[n:2 * n]
        send_sems, recv_sems = refs[2 * n:]
        x, y, c = lax.axis_index("x"), lax.axis_index("y"), lax.axis_index("c")
        copies = [pltpu.make_async_remote_copy(
            src_ref=g_refs[b].at[k, 1 - c], dst_ref=land_refs[b].at[k], send_sem=send_sems.at[b, k],
            recv_sem=recv_sems.at[b, k], device_id=(x, y, 1 - c), device_id_type=MESH)
            for b in range(n) for k in range(N_CHIP)]
        for cp in copies:
            cp.start()
        for cp in copies:
            cp.wait()

    return pl.pallas_call(
        body, name="rs_sibling_exchange", in_specs=[ANY] * n, out_specs=[ANY] * n,
        out_shape=[jax.ShapeDtypeStruct((N_CHIP,) + g.shape[2:], g.dtype) for g in gs],
        scratch_shapes=[pltpu.SemaphoreType.DMA((n, N_CHIP)), pltpu.SemaphoreType.DMA((n, N_CHIP))],
    )(*gs)


def _pair_sum(g, land, core, name):
    rows, cols = land.shape[1:]
    tr = _row_tile(rows)

    def body(c_ref, g_ref, l_ref, o_ref):
        o_ref[...] = (g_ref[...].astype(F32) + l_ref[...].astype(F32)).astype(o_ref.dtype)

    return pl.pallas_call(
        body, name=f"rs_pair_sum_{name}",
        grid_spec=pltpu.PrefetchScalarGridSpec(
            num_scalar_prefetch=1, grid=(N_CHIP, rows // tr),
            in_specs=[pl.BlockSpec((None, None, tr, cols), lambda k, i, c_ref: (k, c_ref[0], i, 0)),
                      pl.BlockSpec((None, tr, cols), lambda k, i, c_ref: (k, i, 0))],
            out_specs=pl.BlockSpec((None, tr, cols), lambda k, i, c_ref: (k, i, 0))),
        out_shape=jax.ShapeDtypeStruct(land.shape, land.dtype),
        compiler_params=_cparams("parallel", "parallel"),
    )(core, g, land)


def _chip_exchange(parts):
    n = len(parts)

    def body(*refs):
        p_refs, land_refs = refs[:n], refs[n:2 * n]
        send_sems, recv_sems, local_sems = refs[2 * n:]
        x, y, c = lax.axis_index("x"), lax.axis_index("y"), lax.axis_index("c")
        mine = 2 * x + y
        chips = [(1 - x, y), (x, 1 - y), (1 - x, 1 - y)]
        own = [pltpu.make_async_copy(p_refs[b].at[mine], land_refs[b].at[mine], local_sems.at[b]) for b in range(n)]
        for cp in own:
            cp.start()
        copies = [pltpu.make_async_remote_copy(
            src_ref=p_refs[b].at[2 * cx + cy], dst_ref=land_refs[b].at[mine], send_sem=send_sems.at[b, j],
            recv_sem=recv_sems.at[b, j], device_id=(cx, cy, c), device_id_type=MESH)
            for b in range(n) for j, (cx, cy) in enumerate(chips)]
        for cp in copies:
            cp.start()
        for b in range(n):
            for j, (cx, cy) in enumerate(chips):
                pltpu.make_async_remote_copy(
                    src_ref=p_refs[b].at[mine], dst_ref=land_refs[b].at[2 * cx + cy], send_sem=send_sems.at[b, j],
                    recv_sem=recv_sems.at[b, j], device_id=(cx, cy, c), device_id_type=MESH).wait_recv()
        for cp in copies:
            cp.wait_send()
        for cp in own:
            cp.wait()

    return pl.pallas_call(
        body, name="rs_chip_exchange", in_specs=[ANY] * n, out_specs=[ANY] * n,
        out_shape=[jax.ShapeDtypeStruct(p.shape, p.dtype) for p in parts],
        scratch_shapes=[pltpu.SemaphoreType.DMA((n, 3)), pltpu.SemaphoreType.DMA((n, 3)),
                        pltpu.SemaphoreType.DMA((n,))],
    )(*parts)


def _adamw(own, arrived, chip, w, m, v, name):
    k, rows, cols = arrived.shape
    tr = _row_tile(rows)
    c1 = 1.0 / (1.0 - ADAM_B1 ** ADAM_STEP)
    c2 = 1.0 / (1.0 - ADAM_B2 ** ADAM_STEP)

    def body(chip_ref, o_ref, p_ref, w_ref, m_ref, v_ref, g_ref, d_ref, nm_ref, nv_ref):
        def slab(j):
            return jnp.where(chip_ref[0] == j, o_ref[j], p_ref[j]).astype(F32)

        g = slab(0)
        for j in range(1, k):
            g = g + slab(j)
        g_ref[...] = g
        nm = ADAM_B1 * m_ref[...] + (1.0 - ADAM_B1) * g
        nv = ADAM_B2 * v_ref[...] + (1.0 - ADAM_B2) * (g * g)
        nm_ref[...] = nm
        nv_ref[...] = nv
        d_ref[...] = -ADAM_LR * ((nm * c1) / (jnp.sqrt(nv * c2) + ADAM_EPS) + ADAM_WD * w_ref[...])

    blk = pl.BlockSpec((tr, cols), lambda i, c: (i, 0))
    stack = pl.BlockSpec((k, tr, cols), lambda i, c: (0, i, 0))
    return pl.pallas_call(
        body, name=name,
        grid_spec=pltpu.PrefetchScalarGridSpec(num_scalar_prefetch=1, grid=(rows // tr,),
                                               in_specs=[stack, stack, blk, blk, blk], out_specs=[blk] * 4),
        out_shape=[jax.ShapeDtypeStruct((rows, cols), F32)] * 4,
        compiler_params=_cparams("parallel"),
    )(chip, own, arrived, w, m, v)


SMALL_LAYOUT = (("g_mix", 0, 1024), ("conv_b", 1, 512), ("q_norm_g", 2, 64), ("k_norm_g", 3, 64),
                ("g_out_conv", 4, 512), ("g_out_attn", 5, 512), ("g_ffn", 6, 1024), ("ffn_conv_b", 7, 2816),
                ("g_ple", 10, 1024))
CONV_W_ROW = 11
FFN_CONV_W_ROW = 14
LOSS_ROW = 23


def _row_pieces(cols):
    return [(c, min(1024, cols - c)) for c in range(0, cols, 1024)]


def _pack_small(grads, loss_tile):
    names = [n for n, _, _ in SMALL_LAYOUT]

    def body(*refs):
        ins, cw_ref, fcw_ref, loss_ref, out_ref = refs[:len(names)], refs[-4], refs[-3], refs[-2], refs[-1]
        out_ref[...] = jnp.zeros_like(out_ref)
        for ref, (_, row, cols) in zip(ins, SMALL_LAYOUT):
            for j, (c, width) in enumerate(_row_pieces(cols)):
                out_ref[row + j:row + j + 1, 0:width] = ref[:, c:c + width]
        for k in range(3):
            out_ref[CONV_W_ROW + k:CONV_W_ROW + k + 1, 0:CONV_W] = cw_ref[k:k + 1, :]
            for j, (c, width) in enumerate(_row_pieces(D_FF)):
                row = FFN_CONV_W_ROW + 3 * k + j
                out_ref[row:row + 1, 0:width] = fcw_ref[k:k + 1, c:c + width]
        out_ref[LOSS_ROW:LOSS_ROW + 1, 0:128] = loss_ref[0:1, :]

    return pl.pallas_call(
        body, name="pack_small_grads", out_shape=jax.ShapeDtypeStruct((SMALL_ROWS, 1024), F32),
    )(*[grads[n] for n in names], grads["conv_w"], grads["ffn_conv_w"], loss_tile)


def _adamw_small(arrived, conv_parts, fconv_parts, wts, mom, var):
    names = [n for n, _, _ in SMALL_LAYOUT] + ["conv_w", "ffn_conv_w"]
    c1 = 1.0 / (1.0 - ADAM_B1 ** ADAM_STEP)
    c2 = 1.0 / (1.0 - ADAM_B2 ** ADAM_STEP)
    n = len(names)

    def body(*refs):
        land, cw_ref, fcw_ref = refs[0], refs[1], refs[2]
        state = refs[3:3 + 3 * n]
        outs = refs[3 + 3 * n:]

        def total(piece):
            acc = piece(0)
            for d in range(1, N_DEV):
                acc = acc + piece(d)
            return acc

        for i, name in enumerate(names):
            if name == "conv_w":
                g = total(lambda d: cw_ref[d])
            elif name == "ffn_conv_w":
                g = total(lambda d: fcw_ref[d])
            else:
                _, row, cols = SMALL_LAYOUT[i]
                pieces = [total(lambda d, j=j, width=width: land[d, row + j:row + j + 1, 0:width])
                          for j, (_, width) in enumerate(_row_pieces(cols))]
                g = pieces[0] if len(pieces) == 1 else jnp.concatenate(pieces, axis=1)
            w_ref, m_ref, v_ref = state[3 * i:3 * i + 3]
            nm = ADAM_B1 * m_ref[...] + (1.0 - ADAM_B1) * g
            nv = ADAM_B2 * v_ref[...] + (1.0 - ADAM_B2) * (g * g)
            outs[4 * i][...] = g
            outs[4 * i + 1][...] = -ADAM_LR * ((nm * c1) / (jnp.sqrt(nv * c2) + ADAM_EPS) + ADAM_WD * w_ref[...])
            outs[4 * i + 2][...] = nm
            outs[4 * i + 3][...] = nv
        outs[-1][...] = total(lambda d: land[d, LOSS_ROW:LOSS_ROW + 1, 0:128])

    state = [a[nm_] for nm_ in names for a in (wts, mom, var)]
    shapes = [jax.ShapeDtypeStruct(wts[nm_].shape, F32) for nm_ in names for _ in range(4)]
    outs = pl.pallas_call(
        body, name="adamw_small", out_shape=shapes + [jax.ShapeDtypeStruct((1, 128), F32)],
    )(arrived, conv_parts, fconv_parts, *state)
    return {nm_: tuple(outs[4 * i:4 * i + 4]) for i, nm_ in enumerate(names)}, outs[-1][0, 0]


COL_SHARDED = ("w_in", "w_ple_proj")
TRANSPOSED = ("w_gate", "w_up")
REPLICATED = (("g_mix", 1024), ("conv_b", 512), ("q_norm_g", 64), ("k_norm_g", 64), ("g_out_conv", 512),
              ("g_out_attn", 512), ("g_ffn", 1024), ("ffn_conv_b", 2816), ("g_ple", 1024))
CONV_SHARDED = (("conv_w", CONV_W), ("ffn_conv_w", D_FF))


def _gathered_to_full(name, gathered):
    if name in COL_SHARDED:
        return gathered.transpose(1, 0, 2).reshape(gathered.shape[1], -1)
    return gathered.reshape(-1, gathered.shape[2])


def _full_to_stacked(name, grad, shard_shape):
    sr, sc = shard_shape
    if name in COL_SHARDED:
        a = grad.reshape(sr, N_DEV, sc).transpose(1, 0, 2)
    else:
        a = grad.reshape(N_DEV, sr, sc)
    return a.astype(BF16).reshape(N_CHIP, 2, sr, sc)


def _pad_rows(vec, rows):
    return jnp.pad(vec, (0, rows * 1024 - vec.shape[0])).reshape(rows, 1024)


def kernel(x, p, g_mix, w_in, conv_w, conv_b, q_norm_g, k_norm_g, g_out_conv, g_out_attn, w_out, g_ffn, w_gate, w_up, ffn_conv_w, ffn_conv_b, w_down, g_ple, w_ple_gate, w_ple_proj, loss_target, m_g_mix, m_w_in, m_conv_w, m_conv_b, m_q_norm_g, m_k_norm_g, m_g_out_conv, m_g_out_attn, m_w_out, m_g_ffn, m_w_gate, m_w_up, m_ffn_conv_w, m_ffn_conv_b, m_w_down, m_g_ple, m_w_ple_gate, m_w_ple_proj, v_g_mix, v_w_in, v_conv_w, v_conv_b, v_q_norm_g, v_k_norm_g, v_g_out_conv, v_g_out_attn, v_w_out, v_g_ffn, v_w_gate, v_w_up, v_ffn_conv_w, v_ffn_conv_b, v_w_down, v_g_ple, v_w_ple_gate, v_w_ple_proj):
    args = dict(locals())
    names = ["g_mix", "w_in", "conv_w", "conv_b", "q_norm_g", "k_norm_g", "g_out_conv", "g_out_attn", "w_out", "g_ffn",
             "w_gate", "w_up", "ffn_conv_w", "ffn_conv_b", "w_down", "g_ple", "w_ple_gate", "w_ple_proj"]
    big = [n for n, _ in BIG_ROWS]
    conv = [n for n, _ in CONV_SHARDED]
    def local(prefix):
        out = {n: (args[prefix + n][0] if n in big or n in conv else args[prefix + n]) for n in names}
        out.update({n: out[n].T for n in TRANSPOSED})
        return out

    wts, mom, var = local(""), local("m_"), local("v_")
    shard_shapes = {n: wts[n].shape for n in big}
    dev = 4 * lax.axis_index("x") + 2 * lax.axis_index("y") + lax.axis_index("c")
    core = lax.axis_index("c").astype(jnp.int32).reshape(1)

    conv_local = _pad_rows(jnp.concatenate([wts[n].reshape(-1) for n in conv]), 8).reshape(8, 1024)
    late = [n for n in big if n != "w_in"]
    w_in_all, conv_all = _all_gather([wts["w_in"].astype(BF16), conv_local], "gather_weights")
    late_shards = [wts[n].astype(BF16) for n in late]
    gathering, token = _split_start("gather_late_weights", late_shards,
                                    [lax.empty((N_DEV,) + s.shape, BF16) for s in late_shards], _gather_plan,
                                    7 * len(late), w_in_all)
    full = dict(wts)
    full["w_in"] = _gathered_to_full("w_in", w_in_all)
    full["g_mix"] = _ordered_after(wts["g_mix"], token)
    flying = {}

    def late_weights(after):
        shards, lands = _split_wait("gather_late_weights", gathering, _gather_plan, after)
        return {n: _gathered_to_full(n, lax.dynamic_update_slice(land, shard[None], (dev, 0, 0)))
                for n, land, shard in zip(late, lands, shards)}

    early = ["w_ple_gate", "w_ple_proj", "w_down", "w_up", "w_gate"]

    def ffn_grads(g):
        stacked = [_full_to_stacked(n, g[n], shard_shapes[n]) for n in early]
        flying["sibling"], tok = _split_start("rs_sibling_early", stacked,
                                              [lax.empty((N_CHIP,) + s.shape[2:], BF16) for s in stacked],
                                              _sibling_plan, N_CHIP * len(early), g["w_down"])
        return tok

    def outproj_done(after):
        stacked, landed = _split_wait("rs_sibling_early", flying["sibling"], _sibling_plan, after)
        parts = [_pair_sum(g, l, core, n) for n, g, l in zip(early, stacked, landed)]
        flying["chip"], tok = _split_start("rs_chip_early", parts, [lax.empty(q.shape, BF16) for q in parts],
                                           _chip_plan, 3 * len(early), landed[0])
        return tok

    off = 0
    for n, width in CONV_SHARDED:
        sc = width // N_DEV
        a = conv_all.reshape(N_DEV, -1)[:, off:off + 3 * sc].reshape(N_DEV, 3, sc)
        full[n] = a.transpose(1, 0, 2).reshape(3, width)
        off += 3 * sc

    loss, dx, grads = _local_step(x[0], p[0, 0], loss_target[0], full, (512, 256),
                                  {"late_weights": late_weights, "ffn_grads": ffn_grads, "outproj_done": outproj_done})

    chip = (2 * lax.axis_index("x") + lax.axis_index("y")).astype(jnp.int32).reshape(1)

    def adamw_of(group, parts, arrived):
        return {n: _adamw(own, got, chip, wts[n], mom[n], var[n], f"adamw_{n}")
                for n, own, got in zip(group, parts, arrived)}

    last = [n for n in big if n not in early]
    stacked = [_full_to_stacked(n, grads[n], shard_shapes[n]) for n in last]
    flying["sibling_last"], tok = _split_start("rs_sibling_last", stacked,
                                               [lax.empty((N_CHIP,) + s.shape[2:], BF16) for s in stacked],
                                               _sibling_plan, N_CHIP * len(last), dx)
    (small_all,) = _all_gather([_ordered_after(_pack_small(grads, loss), tok)], "gather_small_grads")
    stacked, landed = _split_wait("rs_sibling_last", flying["sibling_last"], _sibling_plan, small_all)
    parts = [_pair_sum(g, l, core, n) for n, g, l in zip(last, stacked, landed)]
    flying["chip_last"], tok = _split_start("rs_chip_last", parts, [lax.empty(q.shape, BF16) for q in parts],
                                            _chip_plan, 3 * len(last), landed[0])

    parts, arrived = _split_wait("rs_chip_early", flying["chip"], _chip_plan, tok)
    out = adamw_of(early, parts, arrived)
    taps = small_all[:, CONV_W_ROW:CONV_W_ROW + 3, 0:CONV_W]
    ftaps = small_all[:, FFN_CONV_W_ROW:FFN_CONV_W_ROW + 9, :].reshape(N_DEV, 3, 3 * 1024)
    small_out, loss_total = _adamw_small(
        small_all, lax.dynamic_slice(taps, (0, 0, dev * (CONV_W // N_DEV)), (N_DEV, 3, CONV_W // N_DEV)),
        lax.dynamic_slice(ftaps, (0, 0, dev * (D_FF // N_DEV)), (N_DEV, 3, D_FF // N_DEV)), wts, mom, var)
    out.update(small_out)
    parts, arrived = _split_wait("rs_chip_last", flying["chip_last"], _chip_plan, small_out["g_mix"][0])
    out.update(adamw_of(last, parts, arrived))
    def result(n, which):
        a = out[n][which]
        return (a.T if n in TRANSPOSED else a).reshape(args[n].shape)

    return (loss_total, dx[None], *[result(n, which) for which in range(4) for n in names])
```

```python
import functools

import jax
import jax.numpy as jnp
from jax import lax
from jax.experimental import pallas as pl
from jax.experimental.pallas import tpu as pltpu

F32 = jnp.float32
BF16 = jnp.bfloat16

D_MODEL = 1024
CONV_W = 512
ATTN_W = 512
HEAD_DIM = 64
D_FF = 2816
PLE_DIM = 256
IN_COLS = 3 * CONV_W + 3 * ATTN_W
EPS = 1e-6
QK_BLOCK = 128
DILATIONS = (1, 4, 16)
ATTN_SCALE = HEAD_DIM ** -0.5

ADAM_LR = 0.001
ADAM_B1 = 0.9
ADAM_B2 = 0.999
ADAM_EPS = 1e-08
ADAM_WD = 0.01
ADAM_STEP = 10

N_DEV = 8
N_CHIP = 4
V7X_VMEM_LIMIT = 56 * 1024 * 1024
FF_CHUNKS = 2
V7X_VMEM_LIMIT_LARGE = 62 * 1024 * 1024
FFN_BWD_PARTS = 1

BIG_ROWS = (("w_in", 384), ("w_out", 128), ("w_gate", 352), ("w_up", 352), ("w_down", 352),
            ("w_ple_gate", 128), ("w_ple_proj", 32))
BIG_TOTAL = sum(r for _, r in BIG_ROWS)
SMALL_ROWS = 24


def _cparams(*sem, vmem=V7X_VMEM_LIMIT):
    return pltpu.CompilerParams(dimension_semantics=sem, vmem_limit_bytes=vmem)


def _mm(a, b):
    return jnp.dot(a, b, preferred_element_type=F32)


def _mm_nt(a, b):
    return lax.dot_general(a, b, (((1,), (1,)), ((), ())), preferred_element_type=F32)


def _mm_tn(a, b):
    return lax.dot_general(a, b, (((0,), (0,)), ((), ())), preferred_element_type=F32)


def _full(shape):
    nd = len(shape)
    return pl.BlockSpec(shape, lambda *_: (0,) * nd)


def _rms_stats(x):
    r = lax.rsqrt(jnp.mean(x * x, axis=-1, keepdims=True) + EPS)
    return r, x * r


def _rms_bwd(dy, xhat, r, g):
    gd = dy * g
    return r * (gd - xhat * jnp.mean(gd * xhat, axis=-1, keepdims=True))


def _seg_sum64(v, bd_ref):
    outs = []
    for c in range(0, v.shape[1], 256):
        vc = v[:, c:c + 256]
        hi = vc.astype(BF16)
        lo = (vc - hi.astype(F32)).astype(BF16)
        outs.append(_mm(hi, bd_ref[...]) + _mm(lo, bd_ref[...]))
    return outs[0] if len(outs) == 1 else jnp.concatenate(outs, axis=1)


def _shift_rows(u, k, edge_rows):
    out = pltpu.roll(u, k, 0)
    row = lax.broadcasted_iota(jnp.int32, (8, u.shape[1]), 0)
    head = out[0:8]
    for j in range(k):
        head = jnp.where(row == j, edge_rows[k - 1 - j], head)
    return jnp.concatenate([head, out[8:]], axis=0)


def _shift_rows_up(u, k, edge_rows):
    n = u.shape[0]
    out = pltpu.roll(u, n - k, 0)
    row = lax.broadcasted_iota(jnp.int32, (8, u.shape[1]), 0)
    tail = out[n - 8:n]
    for j in range(k):
        tail = jnp.where(row == 8 - k + j, edge_rows[j], tail)
    return jnp.concatenate([out[0:n - 8], tail], axis=0)


def _conv_fwd(u, c1, c2, w_ref, b_ref):
    u1 = _shift_rows(u, 1, (c1,))
    u2 = _shift_rows(u, 2, (c1, c2))
    y = u2 * w_ref[0:1, :] + u1 * w_ref[1:2, :] + u * w_ref[2:3, :] + b_ref[...]
    return y, u1, u2


def _conv_bwd_input(dy, n1row, n2row, w_ref):
    d1 = _shift_rows_up(dy, 1, (n1row,))
    d2 = _shift_rows_up(dy, 2, (n1row, n2row))
    return dy * w_ref[2:3, :] + d1 * w_ref[1:2, :] + d2 * w_ref[0:1, :]


def _sigmoid(x):
    return 1.0 / (1.0 + jnp.exp(-x))


def _inproj_fwd(x, g_mix, w_in, conv_w, conv_b, qg, kg, bd, tm):
    t = x.shape[0]

    def body(x_ref, g_ref, w_ref, cw_ref, cb_ref, qg_ref, kg_ref, bd_ref,
             zc_ref, zqk_ref, yc_ref, q_ref, k_ref, v_ref, carry_ref):
        @pl.when(pl.program_id(0) == 0)
        def _():
            carry_ref[...] = jnp.zeros_like(carry_ref)

        _, xhat = _rms_stats(x_ref[...])
        h = (xhat * g_ref[...]).astype(BF16)
        zconv = _mm(h, w_ref[:, 0:3 * CONV_W])
        zc_ref[...] = zconv.astype(BF16)
        u = zconv[:, CONV_W:2 * CONV_W] * zconv[:, 2 * CONV_W:3 * CONV_W]
        cv, _, _ = _conv_fwd(u, carry_ref[7:8, :], carry_ref[6:7, :], cw_ref, cb_ref)
        yc_ref[...] = (zconv[:, 0:CONV_W] * cv).astype(BF16)
        carry_ref[...] = u[tm - 8:tm, :]

        zqk = _mm(h, w_ref[:, 3 * CONV_W:3 * CONV_W + 2 * ATTN_W])
        zqk_ref[...] = zqk.astype(BF16)
        for j, (gain_ref, out_ref, scale) in enumerate(((qg_ref, q_ref, ATTN_SCALE), (kg_ref, k_ref, 1.0))):
            z = zqk[:, j * ATTN_W:(j + 1) * ATTN_W]
            r = lax.rsqrt(_seg_sum64(z * z, bd_ref) * (1.0 / HEAD_DIM) + EPS)
            out_ref[...] = z * r * gain_ref[...] * scale
        v_ref[...] = _mm(h, w_ref[:, 3 * CONV_W + 2 * ATTN_W:IN_COLS])

    def blk(c):
        return pl.BlockSpec((tm, c), lambda i: (i, 0))

    return pl.pallas_call(
        body, name="inproj_fwd", grid=(t // tm,),
        in_specs=[blk(D_MODEL), _full((1, D_MODEL)), _full((D_MODEL, IN_COLS)), _full((3, CONV_W)),
                  _full((1, CONV_W)), _full((1, ATTN_W)), _full((1, ATTN_W)), _full((256, 256))],
        out_specs=[blk(3 * CONV_W), blk(2 * ATTN_W), blk(CONV_W), blk(ATTN_W), blk(ATTN_W), blk(ATTN_W)],
        out_shape=[jax.ShapeDtypeStruct((t, 3 * CONV_W), BF16), jax.ShapeDtypeStruct((t, 2 * ATTN_W), BF16),
                   jax.ShapeDtypeStruct((t, CONV_W), BF16), jax.ShapeDtypeStruct((t, ATTN_W), F32),
                   jax.ShapeDtypeStruct((t, ATTN_W), F32), jax.ShapeDtypeStruct((t, ATTN_W), F32)],
        scratch_shapes=[pltpu.VMEM((8, CONV_W), F32)],
        compiler_params=_cparams("arbitrary"),
    )(x, g_mix, w_in, conv_w, conv_b, qg, kg, bd)


SUPER = 16 * QK_BLOCK
KEYS = 2 * QK_BLOCK


def _rows(start, size, dil):
    return pl.ds(start, size) if dil == 1 else pl.ds(start, size, stride=dil)


def _attn_bias(sl_ref, dil):
    qi = lax.broadcasted_iota(jnp.int32, (KEYS, KEYS), 0)
    kj = lax.broadcasted_iota(jnp.int32, (KEYS, KEYS), 1)
    step = jnp.bitwise_and(qi, QK_BLOCK - 1) + QK_BLOCK - kj
    slope = jnp.where(qi < QK_BLOCK, sl_ref[0, 0:1, 0:1], sl_ref[0, 1:2, 0:1])
    bias = jnp.where(jnp.logical_and(step >= 0, step <= QK_BLOCK), -slope * (step * dil).astype(F32), -jnp.inf)
    return bias, kj >= QK_BLOCK


def _unit_start(u, dil):
    if dil == 1:
        return pl.multiple_of(u * QK_BLOCK, QK_BLOCK)
    if dil == 4:
        return jnp.bitwise_and(u, 3) + (u // 4) * (4 * QK_BLOCK)
    return u


def _stack_heads(a, head0):
    zero = jnp.zeros_like(a)
    return jnp.concatenate([jnp.where(head0, a, zero), jnp.where(head0, zero, a)], axis=0)


def _attn_fwd(q, k, v, slopes):
    t = q.shape[0]
    nsb = t // SUPER

    def body(q_ref, kc_ref, kp_ref, vc_ref, vp_ref, sl_ref, o_ref, l_ref, kk, vv, ob, lb):
        s = pl.program_id(1)
        kk[0:SUPER, :] = kp_ref[...]
        kk[SUPER:, :] = kc_ref[...]
        vv[0:SUPER, :] = vp_ref[...]
        vv[SUPER:, :] = vc_ref[...]
        head0 = lax.broadcasted_iota(jnp.int32, (QK_BLOCK, QK_BLOCK), 1) < HEAD_DIM

        for b, dil in enumerate(DILATIONS):
            bias, own_half = _attn_bias(sl_ref, dil)

            def unit(u, carry, b=b, dil=dil, bias=bias, own_half=own_half):
                start = _unit_start(u, dil)
                first_key = SUPER + start - QK_BLOCK * dil
                q2 = _stack_heads(q_ref[_rows(start, QK_BLOCK, dil), :].astype(BF16), head0)
                k2 = kk[_rows(first_key, KEYS, dil), :].astype(BF16)
                v2 = vv[_rows(first_key, KEYS, dil), :].astype(BF16)
                has_prev = jnp.logical_or(s > 0, start >= QK_BLOCK * dil)
                sc = jnp.where(jnp.logical_or(own_half, has_prev), _mm_nt(q2, k2) + bias, -jnp.inf)
                m = jnp.max(sc, axis=-1, keepdims=True)
                e = jnp.exp(sc - m)
                den = jnp.sum(e, axis=-1, keepdims=True)
                o2 = _mm(e.astype(BF16), v2) / den
                l2 = m + jnp.log(den)
                ob[b, _rows(start, QK_BLOCK, dil), :] = jnp.where(head0, o2[0:QK_BLOCK], o2[QK_BLOCK:])
                lb[b, _rows(start, QK_BLOCK, dil), :] = jnp.where(head0, l2[0:QK_BLOCK], l2[QK_BLOCK:])
                return carry

            lax.fori_loop(0, SUPER // QK_BLOCK, unit, 0, unroll=16)

        def merge(i, carry):
            rows = pl.ds(pl.multiple_of(i * 256, 256), 256)
            la, lb_, lc = lb[0, rows, :], lb[1, rows, :], lb[2, rows, :]
            mx = jnp.maximum(jnp.maximum(la, lb_), lc)
            wa, wb, wc = jnp.exp(la - mx), jnp.exp(lb_ - mx), jnp.exp(lc - mx)
            sw = wa + wb + wc
            o_ref[rows, :] = ((wa * ob[0, rows, :] + wb * ob[1, rows, :] + wc * ob[2, rows, :]) / sw).astype(BF16)
            l_ref[rows, :] = mx + jnp.log(sw)
            return carry

        lax.fori_loop(0, SUPER // 256, merge, 0)

    cur = pl.BlockSpec((SUPER, QK_BLOCK), lambda p, s: (s, p))
    prev = pl.BlockSpec((SUPER, QK_BLOCK), lambda p, s: (jnp.maximum(s - 1, 0), p))
    return pl.pallas_call(
        body, name="attn_fwd", grid=(4, nsb),
        in_specs=[cur, cur, prev, cur, prev, pl.BlockSpec((1, 2, QK_BLOCK), lambda p, s: (p, 0, 0))],
        out_specs=[cur, cur],
        out_shape=[jax.ShapeDtypeStruct((t, ATTN_W), BF16), jax.ShapeDtypeStruct((t, ATTN_W), F32)],
        scratch_shapes=[pltpu.VMEM((2 * SUPER, QK_BLOCK), F32), pltpu.VMEM((2 * SUPER, QK_BLOCK), F32),
                        pltpu.VMEM((3, SUPER, QK_BLOCK), F32), pltpu.VMEM((3, SUPER, QK_BLOCK), F32)],
        compiler_params=_cparams("parallel", "arbitrary"),
    )(q, k, k, v, v, slopes)


def _outproj_fwd(ya, yc, x, goc, goa, w_out, tm):
    t = x.shape[0]

    def body(ya_ref, yc_ref, x_ref, goc_ref, goa_ref, w_ref, x1_ref):
        _, ychat = _rms_stats(yc_ref[...].astype(F32))
        _, yahat = _rms_stats(ya_ref[...].astype(F32))
        acc = _mm((ychat * goc_ref[...]).astype(BF16), w_ref[0:CONV_W, :])
        acc += _mm((yahat * goa_ref[...]).astype(BF16), w_ref[CONV_W:, :])
        x1_ref[...] = x_ref[...] + acc

    def blk(c):
        return pl.BlockSpec((tm, c), lambda i: (i, 0))

    return pl.pallas_call(
        body, name="outproj_fwd", grid=(t // tm,),
        in_specs=[blk(ATTN_W), blk(CONV_W), blk(D_MODEL), _full((1, CONV_W)), _full((1, ATTN_W)),
                  _full((D_MODEL, D_MODEL))],
        out_specs=blk(D_MODEL),
        out_shape=jax.ShapeDtypeStruct((t, D_MODEL), F32),
        compiler_params=_cparams("parallel"),
    )(ya, yc, x, goc, goa, w_out)


def _ffn_fwd(x1, g_ffn, w_gate_t, w_up_t, w_down, fcw, fcb, tm):
    t = x1.shape[0]

    def body(x_ref, g_ref, wg_ref, wu_ref, wd_ref, cw_ref, cb_ref, gp_ref, up_ref, h_ref, x2_ref, carry_ref):
        @pl.when(pl.program_id(0) == 0)
        def _():
            carry_ref[...] = jnp.zeros_like(carry_ref)

        xv = x_ref[...]
        _, xhat = _rms_stats(xv)
        h = (xhat * g_ref[...]).astype(BF16)
        h_ref[...] = h
        gp = _mm_nt(h, wg_ref[...])
        gp_ref[...] = gp.astype(BF16)
        gate, _, _ = _conv_fwd(gp, carry_ref[7:8, :], carry_ref[6:7, :], cw_ref, cb_ref)
        carry_ref[...] = gp[tm - 8:tm, :]
        up = _mm_nt(h, wu_ref[...])
        up_ref[...] = up.astype(BF16)
        a = (gate * _sigmoid(gate) * up).astype(BF16)
        x2_ref[...] = xv + _mm(a, wd_ref[...])

    def blk(c):
        return pl.BlockSpec((tm, c), lambda i: (i, 0))

    return pl.pallas_call(
        body, name="ffn_fwd", grid=(t // tm,),
        in_specs=[blk(D_MODEL), _full((1, D_MODEL)), _full((D_FF, D_MODEL)), _full((D_FF, D_MODEL)),
                  _full((D_FF, D_MODEL)), _full((3, D_FF)), _full((1, D_FF))],
        out_specs=[blk(D_FF), blk(D_FF), blk(D_MODEL), blk(D_MODEL)],
        out_shape=[jax.ShapeDtypeStruct((t, D_FF), BF16), jax.ShapeDtypeStruct((t, D_FF), BF16),
                   jax.ShapeDtypeStruct((t, D_MODEL), BF16), jax.ShapeDtypeStruct((t, D_MODEL), F32)],
        scratch_shapes=[pltpu.VMEM((8, D_FF), F32)],
        compiler_params=_cparams("arbitrary"),
    )(x1, g_ffn, w_gate_t, w_up_t, w_down, fcw, fcb)


def _ple_fwd_bwd(x2, p, target, g_ple, w_pg, w_pp, tm):
    t = x2.shape[0]

    def body(x_ref, p_ref, t_ref, g_ref, wg_ref, wp_ref, dx_ref, dxb_ref, loss_ref, dwg_ref, dwp_ref, dg_ref):
        @pl.when(pl.program_id(0) == 0)
        def _():
            loss_ref[...] = jnp.zeros_like(loss_ref)
            dwg_ref[...] = jnp.zeros_like(dwg_ref)
            dwp_ref[...] = jnp.zeros_like(dwp_ref)
            dg_ref[...] = jnp.zeros_like(dg_ref)

        xv = x_ref[...]
        r, xhat = _rms_stats(xv)
        g = g_ref[...]
        h = (xhat * g).astype(BF16)
        pg = _sigmoid(_mm(h, wg_ref[...]))
        pb = p_ref[...].astype(BF16)
        pp = _mm(pb, wp_ref[...])
        err = xv + pg * pp - t_ref[...]
        loss_ref[...] += 0.5 * jnp.sum(jnp.mean(err * err, axis=-1, keepdims=True))
        dx3 = err * (1.0 / D_MODEL)
        d_pp = (dx3 * pg).astype(BF16)
        d_pre = (dx3 * pp * pg * (1.0 - pg)).astype(BF16)
        dwp_ref[...] += _mm_tn(pb, d_pp)
        dwg_ref[...] += _mm_tn(h, d_pre)
        dh = _mm_nt(d_pre, wg_ref[...])
        dg_ref[...] += jnp.sum(dh * xhat, axis=0, keepdims=True)
        dx2 = dx3 + _rms_bwd(dh, xhat, r, g)
        dx_ref[...] = dx2
        dxb_ref[...] = dx2.astype(BF16)

    def blk(c):
        return pl.BlockSpec((tm, c), lambda i: (i, 0))

    return pl.pallas_call(
        body, name="ple_fwd_bwd", grid=(t // tm,),
        in_specs=[blk(D_MODEL), blk(PLE_DIM), blk(D_MODEL), _full((1, D_MODEL)), _full((D_MODEL, D_MODEL)),
                  _full((PLE_DIM, D_MODEL))],
        out_specs=[blk(D_MODEL), blk(D_MODEL), _full((8, 128)), _full((D_MODEL, D_MODEL)),
                   _full((PLE_DIM, D_MODEL)), _full((1, D_MODEL))],
        out_shape=[jax.ShapeDtypeStruct((t, D_MODEL), F32), jax.ShapeDtypeStruct((t, D_MODEL), BF16),
                   jax.ShapeDtypeStruct((8, 128), F32),
                   jax.ShapeDtypeStruct((D_MODEL, D_MODEL), F32), jax.ShapeDtypeStruct((PLE_DIM, D_MODEL), F32),
                   jax.ShapeDtypeStruct((1, D_MODEL), F32)],
        compiler_params=_cparams("arbitrary"),
    )(x2, p, target, g_ple, w_pg, w_pp)


def _ffn_bwd(dx2, h2, gp, up, w_gate, w_up, w_down, fcw, fcb, tm):
    t = dx2.shape[0]
    nblk = t // tm
    fc = D_FF // FF_CHUNKS
    half = tm // FFN_BWD_PARTS

    def body(dx_ref, h_ref, gp_ref, gph_ref, up_ref, wg_ref, wu_ref, wd_ref, cw_ref, cb_ref,
             dh_ref, dwd_ref, dwu_ref, dwg_ref, dcw_ref, dcb_ref, carry_ref, a_scr, dup_scr, dgp_scr):
        i = pl.program_id(1)

        @pl.when(i == 0)
        def _():
            carry_ref[...] = jnp.zeros_like(carry_ref)
            dwd_ref[...] = jnp.zeros_like(dwd_ref)
            dwu_ref[...] = jnp.zeros_like(dwu_ref)
            dwg_ref[...] = jnp.zeros_like(dwg_ref)
            dcw_ref[...] = jnp.zeros_like(dcw_ref)
            dcb_ref[...] = jnp.zeros_like(dcb_ref)

        keep = (i < nblk - 1).astype(F32)
        later = carry_ref[...]
        for hf in reversed(range(FFN_BWD_PARTS)):
            rows = slice(hf * half, (hf + 1) * half)
            dxb = dx_ref[rows, :]
            gp_v = gp_ref[rows, :].astype(F32)
            if hf > 0:
                before = gp_ref[hf * half - 16:hf * half, :].astype(F32)
            else:
                before = gph_ref[...].astype(F32) * keep
            gate, gp1, gp2 = _conv_fwd(gp_v, before[15:16, :], before[14:15, :], cw_ref, cb_ref)
            s = _sigmoid(gate)
            silu = gate * s
            up_v = up_ref[rows, :].astype(F32)
            da = _mm_nt(dxb, wd_ref[...])
            a_scr[rows, :] = (silu * up_v).astype(BF16)
            d_up = (da * silu).astype(BF16)
            dup_scr[rows, :] = d_up
            d_gate = da * up_v * (s * (1.0 + gate * (1.0 - s)))
            d_gp = _conv_bwd_input(d_gate, later[0:1, :], later[1:2, :], cw_ref).astype(BF16)
            dgp_scr[rows, :] = d_gp
            later = d_gate[0:8, :]
            dcw_ref[0:1, :] += jnp.sum(d_gate * gp2, axis=0, keepdims=True)
            dcw_ref[1:2, :] += jnp.sum(d_gate * gp1, axis=0, keepdims=True)
            dcw_ref[2:3, :] += jnp.sum(d_gate * gp_v, axis=0, keepdims=True)
            dcb_ref[...] += jnp.sum(d_gate, axis=0, keepdims=True)
            dh_ref[rows, :] = (_mm(d_gp, wg_ref[...]) + _mm(d_up, wu_ref[...])).astype(BF16)
        carry_ref[...] = later
        dwd_ref[...] += _mm_tn(a_scr[...], dx_ref[...])
        dwu_ref[...] += _mm_tn(dup_scr[...], h_ref[...])
        dwg_ref[...] += _mm_tn(dgp_scr[...], h_ref[...])

    def rev(i):
        return nblk - 1 - i

    one = pl.Buffered(1)
    in_specs = [
        pl.BlockSpec((tm, D_MODEL), lambda j, i: (rev(i), 0)),
        pl.BlockSpec((tm, D_MODEL), lambda j, i: (rev(i), 0)),
        pl.BlockSpec((tm, fc), lambda j, i: (rev(i), j)),
        pl.BlockSpec((16, fc), lambda j, i: (jnp.maximum(rev(i) * (tm // 16) - 1, 0), j)),
        pl.BlockSpec((tm, fc), lambda j, i: (rev(i), j)),
        pl.BlockSpec((fc, D_MODEL), lambda j, i: (j, 0), pipeline_mode=one),
        pl.BlockSpec((fc, D_MODEL), lambda j, i: (j, 0), pipeline_mode=one),
        pl.BlockSpec((fc, D_MODEL), lambda j, i: (j, 0), pipeline_mode=one),
        pl.BlockSpec((3, fc), lambda j, i: (0, j)),
        pl.BlockSpec((1, fc), lambda j, i: (0, j)),
    ]
    out_specs = [
        pl.BlockSpec((None, tm, D_MODEL), lambda j, i: (j, rev(i), 0)),
        pl.BlockSpec((fc, D_MODEL), lambda j, i: (j, 0), pipeline_mode=one),
        pl.BlockSpec((fc, D_MODEL), lambda j, i: (j, 0), pipeline_mode=one),
        pl.BlockSpec((fc, D_MODEL), lambda j, i: (j, 0), pipeline_mode=one),
        pl.BlockSpec((3, fc), lambda j, i: (0, j)),
        pl.BlockSpec((1, fc), lambda j, i: (0, j)),
    ]
    return pl.pallas_call(
        body, name="ffn_bwd", grid=(FF_CHUNKS, nblk), in_specs=in_specs, out_specs=out_specs,
        out_shape=[jax.ShapeDtypeStruct((FF_CHUNKS, t, D_MODEL), BF16), jax.ShapeDtypeStruct((D_FF, D_MODEL), F32),
                   jax.ShapeDtypeStruct((D_FF, D_MODEL), F32), jax.ShapeDtypeStruct((D_FF, D_MODEL), F32),
                   jax.ShapeDtypeStruct((3, D_FF), F32), jax.ShapeDtypeStruct((1, D_FF), F32)],
        scratch_shapes=[pltpu.VMEM((8, fc), F32), pltpu.VMEM((tm, fc), BF16), pltpu.VMEM((tm, fc), BF16),
                        pltpu.VMEM((tm, fc), BF16)],
        compiler_params=_cparams("arbitrary", "arbitrary", vmem=V7X_VMEM_LIMIT_LARGE),
    )(dx2, h2, gp, gp, up, w_gate, w_up, w_down, fcw, fcb)


def _outproj_bwd(dh2, dx2, x1, g_ffn, w_out, yc, ya, goc, goa, zconv, conv_w, conv_b, bd, tm):
    t = x1.shape[0]
    nblk = t // tm

    def body(dh_ref, dx2_ref, x1_ref, g_ref, w_ref, yc_ref, ya_ref, goc_ref, goa_ref, zc_ref, zch_ref, cw_ref, cb_ref,
             bd_ref, dx1_ref, dya_ref, dd_ref, dzc_ref, dw_ref, dg_ref, dgoc_ref, dgoa_ref, dcw_ref, dcb_ref,
             carry_ref):
        i = pl.program_id(0)

        @pl.when(i == 0)
        def _():
            carry_ref[...] = jnp.zeros_like(carry_ref)
            for ref in (dw_ref, dg_ref, dgoc_ref, dgoa_ref, dcw_ref, dcb_ref):
                ref[...] = jnp.zeros_like(ref)

        keep = (i < nblk - 1).astype(F32)
        dh2_v = dh_ref[0].astype(F32)
        for j in range(1, FF_CHUNKS):
            dh2_v = dh2_v + dh_ref[j].astype(F32)
        r, xhat = _rms_stats(x1_ref[...])
        dg_ref[...] += jnp.sum(dh2_v * xhat, axis=0, keepdims=True)
        dx1 = dx2_ref[...] + _rms_bwd(dh2_v, xhat, r, g_ref[...])
        dx1_ref[...] = dx1
        dx1b = dx1.astype(BF16)
        dy = _mm_nt(dx1b, w_ref[...])

        yc_v = yc_ref[...].astype(F32)
        rc, ychat = _rms_stats(yc_v)
        dw_ref[0:CONV_W, :] += _mm_tn((ychat * goc_ref[...]).astype(BF16), dx1b)
        dyc = dy[:, 0:CONV_W]
        dgoc_ref[...] += jnp.sum(dyc * ychat, axis=0, keepdims=True)
        d_yc = _rms_bwd(dyc, ychat, rc, goc_ref[...])

        ya_v = ya_ref[...].astype(F32)
        ra, yahat = _rms_stats(ya_v)
        dw_ref[CONV_W:, :] += _mm_tn((yahat * goa_ref[...]).astype(BF16), dx1b)
        dya = dy[:, CONV_W:]
        dgoa_ref[...] += jnp.sum(dya * yahat, axis=0, keepdims=True)
        d_ya = _rms_bwd(dya, yahat, ra, goa_ref[...])
        dya_ref[...] = d_ya
        dd_ref[...] = _seg_sum64(d_ya * ya_v, bd_ref)

        zb = zc_ref[:, 0:CONV_W].astype(F32)
        zc = zc_ref[:, CONV_W:2 * CONV_W].astype(F32)
        zx = zc_ref[:, 2 * CONV_W:3 * CONV_W].astype(F32)
        u = zc * zx
        uh = (zch_ref[:, CONV_W:2 * CONV_W].astype(F32) * zch_ref[:, 2 * CONV_W:3 * CONV_W].astype(F32)) * keep
        cv, u1, u2 = _conv_fwd(u, uh[15:16, :], uh[14:15, :], cw_ref, cb_ref)
        d_cv = d_yc * zb
        d_u = _conv_bwd_input(d_cv, carry_ref[0:1, :], carry_ref[1:2, :], cw_ref)
        carry_ref[...] = d_cv[0:8, :]
        dcw_ref[0:1, :] += jnp.sum(d_cv * u2, axis=0, keepdims=True)
        dcw_ref[1:2, :] += jnp.sum(d_cv * u1, axis=0, keepdims=True)
        dcw_ref[2:3, :] += jnp.sum(d_cv * u, axis=0, keepdims=True)
        dcb_ref[...] += jnp.sum(d_cv, axis=0, keepdims=True)
        dzc_ref[:, 0:CONV_W] = (d_yc * cv).astype(BF16)
        dzc_ref[:, CONV_W:2 * CONV_W] = (d_u * zx).astype(BF16)
        dzc_ref[:, 2 * CONV_W:3 * CONV_W] = (d_u * zc).astype(BF16)

    def rev(i):
        return nblk - 1 - i

    def blk(c):
        return pl.BlockSpec((tm, c), lambda i: (rev(i), 0))

    in_specs = [
        pl.BlockSpec((FF_CHUNKS, tm, D_MODEL), lambda i: (0, rev(i), 0)),
        blk(D_MODEL), blk(D_MODEL), _full((1, D_MODEL)), _full((D_MODEL, D_MODEL)),
        blk(CONV_W), blk(ATTN_W), _full((1, CONV_W)), _full((1, ATTN_W)),
        blk(3 * CONV_W),
        pl.BlockSpec((16, 3 * CONV_W), lambda i: (jnp.maximum(rev(i) * (tm // 16) - 1, 0), 0)),
        _full((3, CONV_W)), _full((1, CONV_W)), _full((256, 256)),
    ]
    out_specs = [blk(D_MODEL), blk(ATTN_W), blk(ATTN_W), blk(3 * CONV_W), _full((D_MODEL, D_MODEL)),
                 _full((1, D_MODEL)), _full((1, CONV_W)), _full((1, ATTN_W)), _full((3, CONV_W)), _full((1, CONV_W))]
    return pl.pallas_call(
        body, name="outproj_bwd", grid=(nblk,), in_specs=in_specs, out_specs=out_specs,
        out_shape=[jax.ShapeDtypeStruct((t, D_MODEL), F32), jax.ShapeDtypeStruct((t, ATTN_W), F32),
                   jax.ShapeDtypeStruct((t, ATTN_W), F32), jax.ShapeDtypeStruct((t, 3 * CONV_W), BF16),
                   jax.ShapeDtypeStruct((D_MODEL, D_MODEL), F32), jax.ShapeDtypeStruct((1, D_MODEL), F32),
                   jax.ShapeDtypeStruct((1, CONV_W), F32), jax.ShapeDtypeStruct((1, ATTN_W), F32),
                   jax.ShapeDtypeStruct((3, CONV_W), F32), jax.ShapeDtypeStruct((1, CONV_W), F32)],
        scratch_shapes=[pltpu.VMEM((8, CONV_W), F32)],
        compiler_params=_cparams("arbitrary"),
    )(dh2, dx2, x1, g_ffn, w_out, yc, ya, goc, goa, zconv, zconv, conv_w, conv_b, bd)


def _attn_bwd(q, k, v, dya, lse, dd, slopes):
    t = q.shape[0]
    nsb = t // SUPER

    def body(q_ref, kc_ref, kp_ref, vc_ref, vp_ref, dy_ref, l_ref, d_ref, sl_ref, dq_ref, dk_ref, dv_ref,
             kk, vv, dkacc, dvacc):
        s = pl.program_id(1)

        @pl.when(s == 0)
        def _():
            dkacc[...] = jnp.zeros_like(dkacc)
            dvacc[...] = jnp.zeros_like(dvacc)

        dkacc[0:SUPER, :] = dkacc[SUPER:, :]
        dvacc[0:SUPER, :] = dvacc[SUPER:, :]
        dkacc[SUPER:, :] = jnp.zeros((SUPER, QK_BLOCK), F32)
        dvacc[SUPER:, :] = jnp.zeros((SUPER, QK_BLOCK), F32)

        @pl.when(s < nsb)
        def _():
            kk[0:SUPER, :] = kp_ref[...]
            kk[SUPER:, :] = kc_ref[...]
            vv[0:SUPER, :] = vp_ref[...]
            vv[SUPER:, :] = vc_ref[...]
            head0 = lax.broadcasted_iota(jnp.int32, (QK_BLOCK, QK_BLOCK), 1) < HEAD_DIM

            for b, dil in enumerate(DILATIONS):
                bias, own_half = _attn_bias(sl_ref, dil)

                def unit(u, carry, b=b, dil=dil, bias=bias, own_half=own_half):
                    start = _unit_start(u, dil)
                    first_key = SUPER + start - QK_BLOCK * dil
                    qrows = _rows(start, QK_BLOCK, dil)
                    krows = _rows(first_key, KEYS, dil)
                    q2 = _stack_heads(q_ref[qrows, :].astype(BF16), head0)
                    dy2 = _stack_heads(dy_ref[qrows, :].astype(BF16), head0)
                    lv, dv_ = l_ref[qrows, :], d_ref[qrows, :]
                    l2 = jnp.concatenate([lv[:, 0:1], lv[:, HEAD_DIM:HEAD_DIM + 1]], axis=0)
                    d2 = jnp.concatenate([dv_[:, 0:1], dv_[:, HEAD_DIM:HEAD_DIM + 1]], axis=0)
                    k2 = kk[krows, :].astype(BF16)
                    v2 = vv[krows, :].astype(BF16)
                    has_prev = jnp.logical_or(s > 0, start >= QK_BLOCK * dil)
                    sc = jnp.where(jnp.logical_or(own_half, has_prev), _mm_nt(q2, k2) + bias, -jnp.inf)
                    prob = jnp.exp(sc - l2)
                    ds = (prob * (_mm_nt(dy2, v2) - d2)).astype(BF16)
                    dvacc[krows, :] += _mm_tn(prob.astype(BF16), dy2)
                    dkacc[krows, :] += _mm_tn(ds, q2)
                    dq2 = _mm(ds, k2)
                    dq = jnp.where(head0, dq2[0:QK_BLOCK], dq2[QK_BLOCK:]) * ATTN_SCALE
                    if b == 0:
                        dq_ref[qrows, :] = dq
                    else:
                        dq_ref[qrows, :] += dq
                    return carry

                lax.fori_loop(0, SUPER // QK_BLOCK, unit, 0, unroll=8)

        dk_ref[...] = dkacc[0:SUPER, :]
        dv_ref[...] = dvacc[0:SUPER, :].astype(BF16)

    def cur_map(p, s):
        return (jnp.minimum(s, nsb - 1), p)

    def prev_map(p, s):
        return (jnp.clip(s - 1, 0, nsb - 1), p)

    cur = pl.BlockSpec((SUPER, QK_BLOCK), cur_map)
    prev = pl.BlockSpec((SUPER, QK_BLOCK), prev_map)
    return pl.pallas_call(
        body, name="attn_bwd", grid=(4, nsb + 1),
        in_specs=[cur, cur, prev, cur, prev, cur, cur, cur, pl.BlockSpec((1, 2, QK_BLOCK), lambda p, s: (p, 0, 0))],
        out_specs=[cur, prev, prev],
        out_shape=[jax.ShapeDtypeStruct((t, ATTN_W), F32), jax.ShapeDtypeStruct((t, ATTN_W), F32),
                   jax.ShapeDtypeStruct((t, ATTN_W), BF16)],
        scratch_shapes=[pltpu.VMEM((2 * SUPER, QK_BLOCK), F32)] * 4,
        compiler_params=_cparams("parallel", "arbitrary"),
    )(q, k, k, v, v, dya, lse, dd, slopes)


def _attn_bwd_per_branch_unused(q, k, v, dya, lse, dd, slopes, dil):
    t = q.shape[0]
    length = t // dil
    chunk = _attn_chunk(t, dil)
    nch = length // chunk
    nb = chunk // QK_BLOCK
    nblocks = length // QK_BLOCK
    view = (length, dil * ATTN_W)
    ext = chunk + QK_BLOCK

    def body(q_ref, dy_ref, l_ref, d_ref, k_ref, v_ref, qn_ref, dyn_ref, ln_ref, dn_ref, kh_ref, vh_ref, sl_ref,
             dq_ref, dk_ref, dv_ref, qbuf, dybuf, lbuf, dbuf, kbuf, vbuf, dkacc, dvacc):
        c = pl.program_id(2)
        qbuf[0:chunk, :] = q_ref[...]
        qbuf[chunk:, :] = qn_ref[...]
        dybuf[0:chunk, :] = dy_ref[...].astype(BF16)
        dybuf[chunk:, :] = dyn_ref[...].astype(BF16)
        lbuf[0:chunk, :] = l_ref[...]
        lbuf[chunk:, :] = ln_ref[...]
        dbuf[0:chunk, :] = d_ref[...]
        dbuf[chunk:, :] = dn_ref[...]
        kbuf[0:QK_BLOCK, :] = kh_ref[...]
        kbuf[QK_BLOCK:, :] = k_ref[...]
        vbuf[0:QK_BLOCK, :] = vh_ref[...]
        vbuf[QK_BLOCK:, :] = v_ref[...]
        valid_cur, valid_prev, dist_cur, dist_prev, head0 = _attn_masks(dil)

        def pair(qb, dyb, lv, dv_, kb, vb, valid, dist):
            dq = jnp.zeros((QK_BLOCK, QK_BLOCK), F32)
            dk = jnp.zeros((QK_BLOCK, QK_BLOCK), F32)
            dvv = jnp.zeros((QK_BLOCK, QK_BLOCK), F32)
            for hh in range(2):
                sl = sl_ref[0, hh:hh + 1, :]
                hm = head0 if hh == 0 else jnp.logical_not(head0)
                col = hh * HEAD_DIM
                qm = jnp.where(hm, qb, jnp.zeros_like(qb))
                dym = jnp.where(hm, dyb, jnp.zeros_like(dyb))
                s = jnp.where(valid, _mm_nt(qm, kb) - sl * dist, -jnp.inf)
                prob = jnp.exp(s - lv[:, col:col + 1])
                ds = (prob * (_mm_nt(dym, vb) - dv_[:, col:col + 1])).astype(BF16)
                dvv += _mm_tn(prob.astype(BF16), dym)
                dk += _mm_tn(ds, qm)
                dq += jnp.where(hm, _mm(ds, kb), 0.0)
            return dq, dk, dvv

        def blk(j, carry):
            off = pl.multiple_of(j * QK_BLOCK, QK_BLOCK)
            nxt = pl.multiple_of(off + QK_BLOCK, QK_BLOCK)
            qb = qbuf[pl.ds(off, QK_BLOCK), :]
            dyb = dybuf[pl.ds(off, QK_BLOCK), :]
            lv = lbuf[pl.ds(off, QK_BLOCK), :]
            dv_ = dbuf[pl.ds(off, QK_BLOCK), :]
            dq_c, dk_c, dv_c = pair(qb, dyb, lv, dv_, kbuf[pl.ds(nxt, QK_BLOCK), :], vbuf[pl.ds(nxt, QK_BLOCK), :],
                                    valid_cur, dist_cur)
            dkacc[pl.ds(nxt, QK_BLOCK), :] = dk_c
            dvacc[pl.ds(nxt, QK_BLOCK), :] = dv_c
            has_prev = jnp.logical_or(c > 0, j > 0)
            dq_p, dk_p, dv_p = pair(qb, dyb, lv, dv_, kbuf[pl.ds(off, QK_BLOCK), :], vbuf[pl.ds(off, QK_BLOCK), :],
                                    jnp.logical_and(valid_prev, has_prev), dist_prev)

            @pl.when(j > 0)
            def _():
                dkacc[pl.ds(off, QK_BLOCK), :] += dk_p
                dvacc[pl.ds(off, QK_BLOCK), :] += dv_p

            dq_ref[pl.ds(off, QK_BLOCK), :] = (dq_c + dq_p) * ATTN_SCALE
            return carry

        lax.fori_loop(0, nb, blk, 0)

        @pl.when(c < nch - 1)
        def _():
            _, dk_p, dv_p = pair(qbuf[chunk:, :], dybuf[chunk:, :], lbuf[chunk:, :], dbuf[chunk:, :],
                                 kbuf[chunk:, :], vbuf[chunk:, :], valid_prev, dist_prev)
            dkacc[chunk:, :] += dk_p
            dvacc[chunk:, :] += dv_p

        dk_ref[...] = dkacc[QK_BLOCK:, :]
        dv_ref[...] = dvacc[QK_BLOCK:, :]

    def cmap(p, r, c):
        return (c, r * 4 + p)

    def before(p, r, c):
        return (jnp.maximum(c * nb - 1, 0), r * 4 + p)

    def after(p, r, c):
        return (jnp.minimum((c + 1) * nb, nblocks - 1), r * 4 + p)

    main = pl.BlockSpec((chunk, QK_BLOCK), cmap)
    hb = pl.BlockSpec((QK_BLOCK, QK_BLOCK), before)
    ha = pl.BlockSpec((QK_BLOCK, QK_BLOCK), after)
    qv, kv, vv = q.reshape(view), k.reshape(view), v.reshape(view)
    dyv, lv, ddv = dya.reshape(view), lse.reshape(view), dd.reshape(view)
    outs = pl.pallas_call(
        body, name=f"attn_bwd_d{dil}", grid=(4, dil, nch),
        in_specs=[main] * 6 + [ha] * 4 + [hb] * 2 + [pl.BlockSpec((1, 2, QK_BLOCK), lambda p, r, c: (p, 0, 0))],
        out_specs=[main] * 3,
        out_shape=[jax.ShapeDtypeStruct(view, F32)] * 3,
        scratch_shapes=[pltpu.VMEM((ext, QK_BLOCK), BF16), pltpu.VMEM((ext, QK_BLOCK), BF16),
                        pltpu.VMEM((ext, QK_BLOCK), F32), pltpu.VMEM((ext, QK_BLOCK), F32),
                        pltpu.VMEM((ext, QK_BLOCK), BF16), pltpu.VMEM((ext, QK_BLOCK), BF16),
                        pltpu.VMEM((ext, QK_BLOCK), F32), pltpu.VMEM((ext, QK_BLOCK), F32)],
        compiler_params=_cparams("arbitrary", "arbitrary", "arbitrary"),
    )(qv, dyv, lv, ddv, kv, vv, qv, dyv, lv, ddv, kv, vv, slopes)
    return [o.reshape(t, ATTN_W) for o in outs]


def _inproj_bwd(dq, dk, dv, dzconv, zqk, x, dx1, g_mix, w_in, qg, kg, bd, tm):
    t = x.shape[0]

    def body(dq_ref, dk_ref, dv_ref, dzc_ref, zqk_ref, x_ref, dx1_ref, g_ref, w_ref, qg_ref,
             kg_ref, bd_ref, dx_ref, dw_ref, dg_ref, dqg_ref, dkg_ref):
        @pl.when(pl.program_id(0) == 0)
        def _():
            for ref in (dw_ref, dg_ref, dqg_ref, dkg_ref):
                ref[...] = jnp.zeros_like(ref)

        parts = [dzc_ref[...]]
        for j, (dn_ref, gain_ref, dgain_ref) in enumerate(((dq_ref, qg_ref, dqg_ref), (dk_ref, kg_ref, dkg_ref))):
            dn = dn_ref[...]
            z = zqk_ref[:, j * ATTN_W:(j + 1) * ATTN_W].astype(F32)
            r = lax.rsqrt(_seg_sum64(z * z, bd_ref) * (1.0 / HEAD_DIM) + EPS)
            zhat = z * r
            dgain_ref[...] += jnp.sum(dn * zhat, axis=0, keepdims=True)
            gd = dn * gain_ref[...]
            parts.append((r * (gd - zhat * (_seg_sum64(gd * zhat, bd_ref) * (1.0 / HEAD_DIM)))).astype(BF16))
        parts.append(dv_ref[...].astype(BF16))
        dz = jnp.concatenate(parts, axis=1)

        r, xhat = _rms_stats(x_ref[...])
        g = g_ref[...]
        dw_ref[...] += _mm_tn((xhat * g).astype(BF16), dz)
        dh = _mm_nt(dz, w_ref[...])
        dg_ref[...] += jnp.sum(dh * xhat, axis=0, keepdims=True)
        dx_ref[...] = dx1_ref[...] + _rms_bwd(dh, xhat, r, g)

    def blk(c):
        return pl.BlockSpec((tm, c), lambda i: (i, 0))

    return pl.pallas_call(
        body, name="inproj_bwd", grid=(t // tm,),
        in_specs=[blk(ATTN_W)] * 3 + [blk(3 * CONV_W), blk(2 * ATTN_W), blk(D_MODEL), blk(D_MODEL), _full((1, D_MODEL)),
                                      _full((D_MODEL, IN_COLS)), _full((1, ATTN_W)), _full((1, ATTN_W)),
                                      _full((256, 256))],
        out_specs=[blk(D_MODEL), _full((D_MODEL, IN_COLS)), _full((1, D_MODEL)), _full((1, ATTN_W)),
                   _full((1, ATTN_W))],
        out_shape=[jax.ShapeDtypeStruct((t, D_MODEL), F32), jax.ShapeDtypeStruct((D_MODEL, IN_COLS), F32),
                   jax.ShapeDtypeStruct((1, D_MODEL), F32), jax.ShapeDtypeStruct((1, ATTN_W), F32),
                   jax.ShapeDtypeStruct((1, ATTN_W), F32)],
        compiler_params=_cparams("arbitrary"),
    )(dq, dk, dv, dzconv, zqk, x, dx1, g_mix, w_in, qg, kg, bd)


def _ordered_after(a, token):
    return a if token is None else a + token[0:1, 0:1].reshape((1,) * a.ndim)


def _local_step(x, p, target, w, tms, hooks=None):
    hooks = hooks or {}
    bd = jnp.kron(jnp.eye(4, dtype=F32), jnp.ones((HEAD_DIM, HEAD_DIM), F32)).astype(BF16)
    qg = jnp.tile(w["q_norm_g"], (1, 8))
    kg = jnp.tile(w["k_norm_g"], (1, 8))
    slopes = jnp.exp2(-jnp.arange(1, 9, dtype=F32))
    slopes = jnp.broadcast_to(slopes.reshape(4, 2, 1), (4, 2, QK_BLOCK))

    zconv, zqk, yc, q, k, v = _inproj_fwd(x, w["g_mix"], w["w_in"], w["conv_w"], w["conv_b"], qg, kg, bd, tms[0])
    ya, lse = _attn_fwd(q, k, v, slopes)
    if "late_weights" in hooks:
        w = {**w, **hooks["late_weights"](lse)}
    x1 = _outproj_fwd(ya, yc, x, w["g_out_conv"], w["g_out_attn"], w["w_out"], tms[0])
    gp, up, h2, x2 = _ffn_fwd(x1, w["g_ffn"], w["w_gate"], w["w_up"], w["w_down"], w["ffn_conv_w"], w["ffn_conv_b"],
                              tms[1])
    dx2, dx2b, loss, dw_pg, dw_pp, dg_ple = _ple_fwd_bwd(x2, p, target, w["g_ple"], w["w_ple_gate"], w["w_ple_proj"], tms[0])
    dh2, dw_down, dw_up, dw_gate, dfcw, dfcb = _ffn_bwd(dx2b, h2, gp, up, w["w_gate"], w["w_up"], w["w_down"],
                                                        w["ffn_conv_w"], w["ffn_conv_b"], tms[0])
    token = None
    if "ffn_grads" in hooks:
        token = hooks["ffn_grads"]({"w_ple_gate": dw_pg, "w_ple_proj": dw_pp, "w_down": dw_down, "w_up": dw_up,
                                    "w_gate": dw_gate})
    dx1, dya, dd, dzconv, dw_out, dg_ffn, dgoc, dgoa, dcw, dcb = _outproj_bwd(
        dh2, dx2, x1, _ordered_after(w["g_ffn"], token), w["w_out"], yc, ya, w["g_out_conv"], w["g_out_attn"], zconv,
        w["conv_w"], w["conv_b"], bd, tms[1])
    token = hooks["outproj_done"](dx1) if "outproj_done" in hooks else None
    dq, dk, dv = _attn_bwd(q, k, v, dya, lse, dd, _ordered_after(slopes, token))
    dx, dw_in, dg_mix, dqg, dkg = _inproj_bwd(dq, dk, dv, dzconv, zqk, x, dx1, w["g_mix"], w["w_in"], qg, kg, bd,
                                              tms[0])
    grads = {
        "g_mix": dg_mix, "w_in": dw_in, "conv_w": dcw, "conv_b": dcb,
        "q_norm_g": dqg.reshape(8, HEAD_DIM).sum(0, keepdims=True),
        "k_norm_g": dkg.reshape(8, HEAD_DIM).sum(0, keepdims=True),
        "g_out_conv": dgoc, "g_out_attn": dgoa, "w_out": dw_out, "g_ffn": dg_ffn, "w_gate": dw_gate, "w_up": dw_up,
        "ffn_conv_w": dfcw, "ffn_conv_b": dfcb, "w_down": dw_down, "g_ple": dg_ple, "w_ple_gate": dw_pg,
        "w_ple_proj": dw_pp,
    }
    return loss, dx, grads


ANY = pl.BlockSpec(memory_space=pl.ANY)
MESH = pl.DeviceIdType.MESH


def _all_gather(shards, name):
    n = len(shards)

    def body(*refs):
        ins, outs = refs[:n], refs[n:2 * n]
        send_sems, recv_sems, local_sems = refs[2 * n:]
        x, y, c = lax.axis_index("x"), lax.axis_index("y"), lax.axis_index("c")
        me, sibling = (x, y, c), (x, y, 1 - c)
        chips = [(1 - x, y), (x, 1 - y), (1 - x, 1 - y)]

        def slot(dev):
            return 4 * dev[0] + 2 * dev[1] + dev[2]

        def copy(b, k, block, to, src=None):
            dst = outs[b].at[slot(block)]
            return pltpu.make_async_remote_copy(
                src_ref=dst if src is None else src, dst_ref=dst, send_sem=send_sems.at[b, k],
                recv_sem=recv_sems.at[b, k], device_id=to, device_id_type=MESH)

        mine = [pltpu.make_async_copy(ins[b], outs[b].at[slot(me)], local_sems.at[b]) for b in range(n)]
        first, passed = [], []
        for b in range(n):
            mine[b].start()
            first.append(copy(b, 0, me, sibling, src=ins[b]))
            first += [copy(b, 1 + j, me, (*chip, c), src=ins[b]) for j, chip in enumerate(chips)]
        for cp in first:
            cp.start()
        for j, chip in enumerate(chips):
            for b in range(n):
                copy(b, 1 + j, (*chip, c), me).wait_recv()
                fwd = copy(b, 4 + j, (*chip, c), sibling)
                fwd.start()
                passed.append(fwd)
        for b in range(n):
            copy(b, 0, sibling, me).wait_recv()
            for j, chip in enumerate(chips):
                copy(b, 4 + j, (*chip, 1 - c), me).wait_recv()
        for cp in first + passed:
            cp.wait_send()
        for cp in mine:
            cp.wait()

    return pl.pallas_call(
        body, name=name,
        in_specs=[ANY] * n, out_specs=[ANY] * n,
        out_shape=[jax.ShapeDtypeStruct((N_DEV,) + s.shape, s.dtype) for s in shards],
        scratch_shapes=[pltpu.SemaphoreType.DMA((n, 7)), pltpu.SemaphoreType.DMA((n, 7)),
                        pltpu.SemaphoreType.DMA((n,))],
    )(*shards)


HBM = pl.BlockSpec(memory_space=pltpu.HBM)
SEM = pl.BlockSpec(memory_space=pltpu.SEMAPHORE)
EFFECT = pltpu.SideEffectType.DATAFLOW_SIDE_EFFECTING
FLIPS = ((0, 0, 1), (0, 1, 0), (0, 1, 1), (1, 0, 0), (1, 0, 1), (1, 1, 0), (1, 1, 1))


def _flip_peers():
    pos = (lax.axis_index("x"), lax.axis_index("y"), lax.axis_index("c"))
    return [tuple(1 - a if f else a for a, f in zip(pos, flip)) for flip in FLIPS]


def _hbm(a):
    return pltpu.with_memory_space_constraint(a, pltpu.HBM)


def _split_start(name, srcs, lands, plan, n_copies, after):
    n, m = len(srcs), len(lands)

    def body(*refs):
        send_sems, recv_sems, token = refs[n + m + 1], refs[n + m + 2], refs[-1]
        for i, (src, dst, peer) in enumerate(plan(refs[:n], refs[n:n + m])):
            pltpu.make_async_remote_copy(src_ref=src, dst_ref=dst, send_sem=send_sems.at[i], recv_sem=recv_sems.at[i],
                                         device_id=peer, device_id_type=MESH).start()
        token[...] = jnp.zeros_like(token)

    outs = pl.pallas_call(
        body, name=name + "_start",
        in_specs=[HBM] * (n + m) + [ANY],
        out_specs=[SEM, SEM] + [HBM] * (n + m) + [pl.BlockSpec(memory_space=pltpu.VMEM)],
        out_shape=[pltpu.SemaphoreType.DMA((n_copies,)), pltpu.SemaphoreType.DMA((n_copies,))]
        + [pltpu.HBM(a.shape, a.dtype) for a in list(srcs) + list(lands)] + [jax.ShapeDtypeStruct((8, 128), F32)],
        input_output_aliases={i: 2 + i for i in range(n + m)},
        compiler_params=pltpu.CompilerParams(has_side_effects=EFFECT),
    )(*[_hbm(a) for a in list(srcs) + list(lands)], after)
    return (outs[0], outs[1], outs[2:2 + n], outs[2 + n:2 + n + m]), outs[-1]


def _split_wait(name, started, plan, after):
    send_sems, recv_sems, srcs, lands = started
    n, m = len(srcs), len(lands)

    def body(*refs):
        send_ref, recv_ref = refs[n + m], refs[n + m + 1]
        for i, (src, dst, peer) in enumerate(plan(refs[:n], refs[n:n + m])):
            copy = pltpu.make_async_remote_copy(src_ref=src, dst_ref=dst, send_sem=send_ref.at[i],
                                                recv_sem=recv_ref.at[i], device_id=peer, device_id_type=MESH)
            copy.wait_send()
            copy.wait_recv()

    outs = pl.pallas_call(
        body, name=name + "_wait",
        in_specs=[HBM] * (n + m) + [SEM, SEM, ANY],
        out_specs=[HBM] * (n + m),
        out_shape=[pltpu.HBM(a.shape, a.dtype) for a in list(srcs) + list(lands)],
        input_output_aliases={i: i for i in range(n + m)},
        compiler_params=pltpu.CompilerParams(has_side_effects=EFFECT),
    )(*srcs, *lands, send_sems, recv_sems, after)
    return outs[:n], outs[n:]


def _gather_plan(srcs, lands):
    slot = 4 * lax.axis_index("x") + 2 * lax.axis_index("y") + lax.axis_index("c")
    return [(src, land.at[slot], peer) for src, land in zip(srcs, lands) for peer in _flip_peers()]


def _sibling_plan(srcs, lands):
    x, y, c = lax.axis_index("x"), lax.axis_index("y"), lax.axis_index("c")
    return [(src.at[k, 1 - c], land.at[k], (x, y, 1 - c)) for src, land in zip(srcs, lands) for k in range(N_CHIP)]


def _chip_plan(srcs, lands):
    x, y, c = lax.axis_index("x"), lax.axis_index("y"), lax.axis_index("c")
    return [(src.at[2 * cx + cy], land.at[2 * x + y], (cx, cy, c))
            for src, land in zip(srcs, lands) for cx, cy in ((1 - x, y), (x, 1 - y), (1 - x, 1 - y))]


def _row_tile(rows):
    for tr in range(min(rows, 512), 15, -16):
        if rows % tr == 0:
            return tr
    return rows


def _sibling_exchange(gs):
    n = len(gs)

    def body(*refs):
        g_refs, land_refs = refs[:n], refs[n:2 * n]
        send_sems, recv_sems = refs[2 * n:]
        x, y, c = lax.axis_index("x"), lax.axis_index("y"), lax.axis_index("c")
        copies = [pltpu.make_async_remote_copy(
            src_ref=g_refs[b].at[k, 1 - c], dst_ref=land_refs[b].at[k], send_sem=send_sems.at[b, k],
            recv_sem=recv_sems.at[b, k], device_id=(x, y, 1 - c), device_id_type=MESH)
            for b in range(n) for k in range(N_CHIP)]
        for cp in copies:
            cp.start()
        for cp in copies:
            cp.wait()

    return pl.pallas_call(
        body, name="rs_sibling_exchange", in_specs=[ANY] * n, out_specs=[ANY] * n,
        out_shape=[jax.ShapeDtypeStruct((N_CHIP,) + g.shape[2:], g.dtype) for g in gs],
        scratch_shapes=[pltpu.SemaphoreType.DMA((n, N_CHIP)), pltpu.SemaphoreType.DMA((n, N_CHIP))],
    )(*gs)


def _pair_sum(g, land, core, name):
    rows, cols = land.shape[1:]
    tr = _row_tile(rows)

    def body(c_ref, g_ref, l_ref, o_ref):
        o_ref[...] = (g_ref[...].astype(F32) + l_ref[...].astype(F32)).astype(o_ref.dtype)

    return pl.pallas_call(
        body, name=f"rs_pair_sum_{name}",
        grid_spec=pltpu.PrefetchScalarGridSpec(
            num_scalar_prefetch=1, grid=(N_CHIP, rows // tr),
            in_specs=[pl.BlockSpec((None, None, tr, cols), lambda k, i, c_ref: (k, c_ref[0], i, 0)),
                      pl.BlockSpec((None, tr, cols), lambda k, i, c_ref: (k, i, 0))],
            out_specs=pl.BlockSpec((None, tr, cols), lambda k, i, c_ref: (k, i, 0))),
        out_shape=jax.ShapeDtypeStruct(land.shape, land.dtype),
        compiler_params=_cparams("parallel", "parallel"),
    )(core, g, land)


def _chip_exchange(parts):
    n = len(parts)

    def body(*refs):
        p_refs, land_refs = refs[:n], refs[n:2 * n]
        send_sems, recv_sems, local_sems = refs[2 * n:]
        x, y, c = lax.axis_index("x"), lax.axis_index("y"), lax.axis_index("c")
        mine = 2 * x + y
        chips = [(1 - x, y), (x, 1 - y), (1 - x, 1 - y)]
        own = [pltpu.make_async_copy(p_refs[b].at[mine], land_refs[b].at[mine], local_sems.at[b]) for b in range(n)]
        for cp in own:
            cp.start()
        copies = [pltpu.make_async_remote_copy(
            src_ref=p_refs[b].at[2 * cx + cy], dst_ref=land_refs[b].at[mine], send_sem=send_sems.at[b, j],
            recv_sem=recv_sems.at[b, j], device_id=(cx, cy, c), device_id_type=MESH)
            for b in range(n) for j, (cx, cy) in enumerate(chips)]
        for cp in copies:
            cp.start()
        for b in range(n):
            for j, (cx, cy) in enumerate(chips):
                pltpu.make_async_remote_copy(
                    src_ref=p_refs[b].at[mine], dst_ref=land_refs[b].at[2 * cx + cy], send_sem=send_sems.at[b, j],
                    recv_sem=recv_sems.at[b, j], device_id=(cx, cy, c), device_id_type=MESH).wait_recv()
        for cp in copies:
            cp.wait_send()
        for cp in own:
            cp.wait()

    return pl.pallas_call(
        body, name="rs_chip_exchange", in_specs=[ANY] * n, out_specs=[ANY] * n,
        out_shape=[jax.ShapeDtypeStruct(p.shape, p.dtype) for p in parts],
        scratch_shapes=[pltpu.SemaphoreType.DMA((n, 3)), pltpu.SemaphoreType.DMA((n, 3)),
                        pltpu.SemaphoreType.DMA((n,))],
    )(*parts)


def _adamw(own, arrived, chip, w, m, v, name):
    k, rows, cols = arrived.shape
    tr = _row_tile(rows)
    c1 = 1.0 / (1.0 - ADAM_B1 ** ADAM_STEP)
    c2 = 1.0 / (1.0 - ADAM_B2 ** ADAM_STEP)

    def body(chip_ref, o_ref, p_ref, w_ref, m_ref, v_ref, g_ref, d_ref, nm_ref, nv_ref):
        def slab(j):
            return jnp.where(chip_ref[0] == j, o_ref[j], p_ref[j]).astype(F32)

        g = slab(0)
        for j in range(1, k):
            g = g + slab(j)
        g_ref[...] = g
        nm = ADAM_B1 * m_ref[...] + (1.0 - ADAM_B1) * g
        nv = ADAM_B2 * v_ref[...] + (1.0 - ADAM_B2) * (g * g)
        nm_ref[...] = nm
        nv_ref[...] = nv
        d_ref[...] = -ADAM_LR * ((nm * c1) / (jnp.sqrt(nv * c2) + ADAM_EPS) + ADAM_WD * w_ref[...])

    blk = pl.BlockSpec((tr, cols), lambda i, c: (i, 0))
    stack = pl.BlockSpec((k, tr, cols), lambda i, c: (0, i, 0))
    return pl.pallas_call(
        body, name=name,
        grid_spec=pltpu.PrefetchScalarGridSpec(num_scalar_prefetch=1, grid=(rows // tr,),
                                               in_specs=[stack, stack, blk, blk, blk], out_specs=[blk] * 4),
        out_shape=[jax.ShapeDtypeStruct((rows, cols), F32)] * 4,
        compiler_params=_cparams("parallel"),
    )(chip, own, arrived, w, m, v)


SMALL_LAYOUT = (("g_mix", 0, 1024), ("conv_b", 1, 512), ("q_norm_g", 2, 64), ("k_norm_g", 3, 64),
                ("g_out_conv", 4, 512), ("g_out_attn", 5, 512), ("g_ffn", 6, 1024), ("ffn_conv_b", 7, 2816),
                ("g_ple", 10, 1024))
CONV_W_ROW = 11
FFN_CONV_W_ROW = 14
LOSS_ROW = 23


def _row_pieces(cols):
    return [(c, min(1024, cols - c)) for c in range(0, cols, 1024)]


def _pack_small(grads, loss_tile):
    names = [n for n, _, _ in SMALL_LAYOUT]

    def body(*refs):
        ins, cw_ref, fcw_ref, loss_ref, out_ref = refs[:len(names)], refs[-4], refs[-3], refs[-2], refs[-1]
        out_ref[...] = jnp.zeros_like(out_ref)
        for ref, (_, row, cols) in zip(ins, SMALL_LAYOUT):
            for j, (c, width) in enumerate(_row_pieces(cols)):
                out_ref[row + j:row + j + 1, 0:width] = ref[:, c:c + width]
        for k in range(3):
            out_ref[CONV_W_ROW + k:CONV_W_ROW + k + 1, 0:CONV_W] = cw_ref[k:k + 1, :]
            for j, (c, width) in enumerate(_row_pieces(D_FF)):
                row = FFN_CONV_W_ROW + 3 * k + j
                out_ref[row:row + 1, 0:width] = fcw_ref[k:k + 1, c:c + width]
        out_ref[LOSS_ROW:LOSS_ROW + 1, 0:128] = loss_ref[0:1, :]

    return pl.pallas_call(
        body, name="pack_small_grads", out_shape=jax.ShapeDtypeStruct((SMALL_ROWS, 1024), F32),
    )(*[grads[n] for n in names], grads["conv_w"], grads["ffn_conv_w"], loss_tile)


def _adamw_small(arrived, conv_parts, fconv_parts, wts, mom, var):
    names = [n for n, _, _ in SMALL_LAYOUT] + ["conv_w", "ffn_conv_w"]
    c1 = 1.0 / (1.0 - ADAM_B1 ** ADAM_STEP)
    c2 = 1.0 / (1.0 - ADAM_B2 ** ADAM_STEP)
    n = len(names)

    def body(*refs):
        land, cw_ref, fcw_ref = refs[0], refs[1], refs[2]
        state = refs[3:3 + 3 * n]
        outs = refs[3 + 3 * n:]

        def total(piece):
            acc = piece(0)
            for d in range(1, N_DEV):
                acc = acc + piece(d)
            return acc

        for i, name in enumerate(names):
            if name == "conv_w":
                g = total(lambda d: cw_ref[d])
            elif name == "ffn_conv_w":
                g = total(lambda d: fcw_ref[d])
            else:
                _, row, cols = SMALL_LAYOUT[i]
                pieces = [total(lambda d, j=j, width=width: land[d, row + j:row + j + 1, 0:width])
                          for j, (_, width) in enumerate(_row_pieces(cols))]
                g = pieces[0] if len(pieces) == 1 else jnp.concatenate(pieces, axis=1)
            w_ref, m_ref, v_ref = state[3 * i:3 * i + 3]
            nm = ADAM_B1 * m_ref[...] + (1.0 - ADAM_B1) * g
            nv = ADAM_B2 * v_ref[...] + (1.0 - ADAM_B2) * (g * g)
            outs[4 * i][...] = g
            outs[4 * i + 1][...] = -ADAM_LR * ((nm * c1) / (jnp.sqrt(nv * c2) + ADAM_EPS) + ADAM_WD * w_ref[...])
            outs[4 * i + 2][...] = nm
            outs[4 * i + 3][...] = nv
        outs[-1][...] = total(lambda d: land[d, LOSS_ROW:LOSS_ROW + 1, 0:128])

    state = [a[nm_] for nm_ in names for a in (wts, mom, var)]
    shapes = [jax.ShapeDtypeStruct(wts[nm_].shape, F32) for nm_ in names for _ in range(4)]
    outs = pl.pallas_call(
        body, name="adamw_small", out_shape=shapes + [jax.ShapeDtypeStruct((1, 128), F32)],
    )(arrived, conv_parts, fconv_parts, *state)
    return {nm_: tuple(outs[4 * i:4 * i + 4]) for i, nm_ in enumerate(names)}, outs[-1][0, 0]


COL_SHARDED = ("w_in", "w_ple_proj")
TRANSPOSED = ("w_gate", "w_up")
REPLICATED = (("g_mix", 1024), ("conv_b", 512), ("q_norm_g", 64), ("k_norm_g", 64), ("g_out_conv", 512),
              ("g_out_attn", 512), ("g_ffn", 1024), ("ffn_conv_b", 2816), ("g_ple", 1024))
CONV_SHARDED = (("conv_w", CONV_W), ("ffn_conv_w", D_FF))


def _gathered_to_full(name, gathered):
    if name in COL_SHARDED:
        return gathered.transpose(1, 0, 2).reshape(gathered.shape[1], -1)
    return gathered.reshape(-1, gathered.shape[2])


def _full_to_stacked(name, grad, shard_shape):
    sr, sc = shard_shape
    if name in COL_SHARDED:
        a = grad.reshape(sr, N_DEV, sc).transpose(1, 0, 2)
    else:
        a = grad.reshape(N_DEV, sr, sc)
    return a.astype(BF16).reshape(N_CHIP, 2, sr, sc)


def _pad_rows(vec, rows):
    return jnp.pad(vec, (0, rows * 1024 - vec.shape[0])).reshape(rows, 1024)


def kernel(x, p, g_mix, w_in, conv_w, conv_b, q_norm_g, k_norm_g, g_out_conv, g_out_attn, w_out, g_ffn, w_gate, w_up, ffn_conv_w, ffn_conv_b, w_down, g_ple, w_ple_gate, w_ple_proj, loss_target, m_g_mix, m_w_in, m_conv_w, m_conv_b, m_q_norm_g, m_k_norm_g, m_g_out_conv, m_g_out_attn, m_w_out, m_g_ffn, m_w_gate, m_w_up, m_ffn_conv_w, m_ffn_conv_b, m_w_down, m_g_ple, m_w_ple_gate, m_w_ple_proj, v_g_mix, v_w_in, v_conv_w, v_conv_b, v_q_norm_g, v_k_norm_g, v_g_out_conv, v_g_out_attn, v_w_out, v_g_ffn, v_w_gate, v_w_up, v_ffn_conv_w, v_ffn_conv_b, v_w_down, v_g_ple, v_w_ple_gate, v_w_ple_proj):
    args = dict(locals())
    names = ["g_mix", "w_in", "conv_w", "conv_b", "q_norm_g", "k_norm_g", "g_out_conv", "g_out_attn", "w_out", "g_ffn",
             "w_gate", "w_up", "ffn_conv_w", "ffn_conv_b", "w_down", "g_ple", "w_ple_gate", "w_ple_proj"]
    big = [n for n, _ in BIG_ROWS]
    conv = [n for n, _ in CONV_SHARDED]
    def local(prefix):
        out = {n: (args[prefix + n][0] if n in big or n in conv else args[prefix + n]) for n in names}
        out.update({n: out[n].T for n in TRANSPOSED})
        return out

    wts, mom, var = local(""), local("m_"), local("v_")
    shard_shapes = {n: wts[n].shape for n in big}
    dev = 4 * lax.axis_index("x") + 2 * lax.axis_index("y") + lax.axis_index("c")
    core = lax.axis_index("c").astype(jnp.int32).reshape(1)

    conv_local = _pad_rows(jnp.concatenate([wts[n].reshape(-1) for n in conv]), 8).reshape(8, 1024)
    late = [n for n in big if n != "w_in"]
    w_in_all, conv_all = _all_gather([wts["w_in"].astype(BF16), conv_local], "gather_weights")
    late_shards = [wts[n].astype(BF16) for n in late]
    gathering, token = _split_start("gather_late_weights", late_shards,
                                    [lax.empty((N_DEV,) + s.shape, BF16) for s in late_shards], _gather_plan,
                                    7 * len(late), w_in_all)
    full = dict(wts)
    full["w_in"] = _gathered_to_full("w_in", w_in_all)
    full["g_mix"] = _ordered_after(wts["g_mix"], token)
    flying = {}

    def late_weights(after):
        shards, lands = _split_wait("gather_late_weights", gathering, _gather_plan, after)
        return {n: _gathered_to_full(n, lax.dynamic_update_slice(land, shard[None], (dev, 0, 0)))
                for n, land, shard in zip(late, lands, shards)}

    early = ["w_ple_gate", "w_ple_proj", "w_down", "w_up", "w_gate"]

    def ffn_grads(g):
        stacked = [_full_to_stacked(n, g[n], shard_shapes[n]) for n in early]
        flying["sibling"], tok = _split_start("rs_sibling_early", stacked,
                                              [lax.empty((N_CHIP,) + s.shape[2:], BF16) for s in stacked],
                                              _sibling_plan, N_CHIP * len(early), g["w_down"])
        return tok

    def outproj_done(after):
        stacked, landed = _split_wait("rs_sibling_early", flying["sibling"], _sibling_plan, after)
        parts = [_pair_sum(g, l, core, n) for n, g, l in zip(early, stacked, landed)]
        flying["chip"], tok = _split_start("rs_chip_early", parts, [lax.empty(q.shape, BF16) for q in parts],
                                           _chip_plan, 3 * len(early), landed[0])
        return tok

    off = 0
    for n, width in CONV_SHARDED:
        sc = width // N_DEV
        a = conv_all.reshape(N_DEV, -1)[:, off:off + 3 * sc].reshape(N_DEV, 3, sc)
        full[n] = a.transpose(1, 0, 2).reshape(3, width)
        off += 3 * sc

    loss, dx, grads = _local_step(x[0], p[0, 0], loss_target[0], full, (512, 256),
                                  {"late_weights": late_weights, "ffn_grads": ffn_grads, "outproj_done": outproj_done})

    chip = (2 * lax.axis_index("x") + lax.axis_index("y")).astype(jnp.int32).reshape(1)

    def adamw_of(group, parts, arrived):
        return {n: _adamw(own, got, chip, wts[n], mom[n], var[n], f"adamw_{n}")
                for n, own, got in zip(group, parts, arrived)}

    last = [n for n in big if n not in early]
    stacked = [_full_to_stacked(n, grads[n], shard_shapes[n]) for n in last]
    flying["sibling_last"], tok = _split_start("rs_sibling_last", stacked,
                                               [lax.empty((N_CHIP,) + s.shape[2:], BF16) for s in stacked],
                                               _sibling_plan, N_CHIP * len(last), dx)
    (small_all,) = _all_gather([_ordered_after(_pack_small(grads, loss), tok)], "gather_small_grads")
    stacked, landed = _split_wait("rs_sibling_last", flying["sibling_last"], _sibling_plan, small_all)
    parts = [_pair_sum(g, l, core, n) for n, g, l in zip(last, stacked, landed)]
    flying["chip_last"], tok = _split_start("rs_chip_last", parts, [lax.empty(q.shape, BF16) for q in parts],
                                            _chip_plan, 3 * len(last), landed[0])

    parts, arrived = _split_wait("rs_chip_early", flying["chip"], _chip_plan, tok)
    out = adamw_of(early, parts, arrived)
    taps = small_all[:, CONV_W_ROW:CONV_W_ROW + 3, 0:CONV_W]
    ftaps = small_all[:, FFN_CONV_W_ROW:FFN_CONV_W_ROW + 9, :].reshape(N_DEV, 3, 3 * 1024)
    small_out, loss_total = _adamw_small(
        small_all, lax.dynamic_slice(taps, (0, 0, dev * (CONV_W // N_DEV)), (N_DEV, 3, CONV_W // N_DEV)),
        lax.dynamic_slice(ftaps, (0, 0, dev * (D_FF // N_DEV)), (N_DEV, 3, D_FF // N_DEV)), wts, mom, var)
    out.update(small_out)
    parts, arrived = _split_wait("rs_chip_last", flying["chip_last"], _chip_plan, small_out["g_mix"][0])
    out.update(adamw_of(last, parts, arrived))
    def result(n, which):
        a = out[n][which]
        return (a.T if n in TRANSPOSED else a).reshape(args[n].shape)

    return (loss_total, dx[None], *[result(n, which) for which in range(4) for n in names])
```

```python
import functools

import jax
import jax.numpy as jnp
from jax import lax
from jax.experimental import pallas as pl
from jax.experimental.pallas import tpu as pltpu

F32 = jnp.float32
BF16 = jnp.bfloat16

D_MODEL = 1024
CONV_W = 512
ATTN_W = 512
HEAD_DIM = 64
D_FF = 2816
PLE_DIM = 256
IN_COLS = 3 * CONV_W + 3 * ATTN_W
EPS = 1e-6
QK_BLOCK = 128
DILATIONS = (1, 4, 16)
ATTN_SCALE = HEAD_DIM ** -0.5

ADAM_LR = 0.001
ADAM_B1 = 0.9
ADAM_B2 = 0.999
ADAM_EPS = 1e-08
ADAM_WD = 0.01
ADAM_STEP = 10

N_DEV = 8
N_CHIP = 4
V7X_VMEM_LIMIT = 56 * 1024 * 1024
FF_CHUNKS = 2
V7X_VMEM_LIMIT_LARGE = 62 * 1024 * 1024
FFN_BWD_PARTS = 1

BIG_ROWS = (("w_in", 384), ("w_out", 128), ("w_gate", 352), ("w_up", 352), ("w_down", 352),
            ("w_ple_gate", 128), ("w_ple_proj", 32))
BIG_TOTAL = sum(r for _, r in BIG_ROWS)
SMALL_ROWS = 24


def _cparams(*sem, vmem=V7X_VMEM_LIMIT):
    return pltpu.CompilerParams(dimension_semantics=sem, vmem_limit_bytes=vmem)


def _mm(a, b):
    return jnp.dot(a, b, preferred_element_type=F32)


def _mm_nt(a, b):
    return lax.dot_general(a, b, (((1,), (1,)), ((), ())), preferred_element_type=F32)


def _mm_tn(a, b):
    return lax.dot_general(a, b, (((0,), (0,)), ((), ())), preferred_element_type=F32)


def _full(shape):
    nd = len(shape)
    return pl.BlockSpec(shape, lambda *_: (0,) * nd)


def _rms_stats(x):
    r = lax.rsqrt(jnp.mean(x * x, axis=-1, keepdims=True) + EPS)
    return r, x * r


def _rms_bwd(dy, xhat, r, g):
    gd = dy * g
    return r * (gd - xhat * jnp.mean(gd * xhat, axis=-1, keepdims=True))


def _seg_sum64(v, bd_ref):
    outs = []
    for c in range(0, v.shape[1], 256):
        vc = v[:, c:c + 256]
        hi = vc.astype(BF16)
        lo = (vc - hi.astype(F32)).astype(BF16)
        outs.append(_mm(hi, bd_ref[...]) + _mm(lo, bd_ref[...]))
    return outs[0] if len(outs) == 1 else jnp.concatenate(outs, axis=1)


def _shift_rows(u, k, edge_rows):
    out = pltpu.roll(u, k, 0)
    row = lax.broadcasted_iota(jnp.int32, (8, u.shape[1]), 0)
    head = out[0:8]
    for j in range(k):
        head = jnp.where(row == j, edge_rows[k - 1 - j], head)
    return jnp.concatenate([head, out[8:]], axis=0)


def _shift_rows_up(u, k, edge_rows):
    n = u.shape[0]
    out = pltpu.roll(u, n - k, 0)
    row = lax.broadcasted_iota(jnp.int32, (8, u.shape[1]), 0)
    tail = out[n - 8:n]
    for j in range(k):
        tail = jnp.where(row == 8 - k + j, edge_rows[j], tail)
    return jnp.concatenate([out[0:n - 8], tail], axis=0)


def _conv_fwd(u, c1, c2, w_ref, b_ref):
    u1 = _shift_rows(u, 1, (c1,))
    u2 = _shift_rows(u, 2, (c1, c2))
    y = u2 * w_ref[0:1, :] + u1 * w_ref[1:2, :] + u * w_ref[2:3, :] + b_ref[...]
    return y, u1, u2


def _conv_bwd_input(dy, n1row, n2row, w_ref):
    d1 = _shift_rows_up(dy, 1, (n1row,))
    d2 = _shift_rows_up(dy, 2, (n1row, n2row))
    return dy * w_ref[2:3, :] + d1 * w_ref[1:2, :] + d2 * w_ref[0:1, :]


def _sigmoid(x):
    return 1.0 / (1.0 + jnp.exp(-x))


def _inproj_fwd(x, g_mix, w_in, conv_w, conv_b, qg, kg, bd, tm):
    t = x.shape[0]

    def body(x_ref, g_ref, w_ref, cw_ref, cb_ref, qg_ref, kg_ref, bd_ref,
             zc_ref, zqk_ref, yc_ref, q_ref, k_ref, v_ref, carry_ref):
        @pl.when(pl.program_id(0) == 0)
        def _():
            carry_ref[...] = jnp.zeros_like(carry_ref)

        _, xhat = _rms_stats(x_ref[...])
        h = (xhat * g_ref[...]).astype(BF16)
        zconv = _mm(h, w_ref[:, 0:3 * CONV_W])
        zc_ref[...] = zconv.astype(BF16)
        u = zconv[:, CONV_W:2 * CONV_W] * zconv[:, 2 * CONV_W:3 * CONV_W]
        cv, _, _ = _conv_fwd(u, carry_ref[7:8, :], carry_ref[6:7, :], cw_ref, cb_ref)
        yc_ref[...] = (zconv[:, 0:CONV_W] * cv).astype(BF16)
        carry_ref[...] = u[tm - 8:tm, :]

        zqk = _mm(h, w_ref[:, 3 * CONV_W:3 * CONV_W + 2 * ATTN_W])
        zqk_ref[...] = zqk.astype(BF16)
        for j, (gain_ref, out_ref, scale) in enumerate(((qg_ref, q_ref, ATTN_SCALE), (kg_ref, k_ref, 1.0))):
            z = zqk[:, j * ATTN_W:(j + 1) * ATTN_W]
            r = lax.rsqrt(_seg_sum64(z * z, bd_ref) * (1.0 / HEAD_DIM) + EPS)
            out_ref[...] = z * r * gain_ref[...] * scale
        v_ref[...] = _mm(h, w_ref[:, 3 * CONV_W + 2 * ATTN_W:IN_COLS])

    def blk(c):
        return pl.BlockSpec((tm, c), lambda i: (i, 0))

    return pl.pallas_call(
        body, name="inproj_fwd", grid=(t // tm,),
        in_specs=[blk(D_MODEL), _full((1, D_MODEL)), _full((D_MODEL, IN_COLS)), _full((3, CONV_W)),
                  _full((1, CONV_W)), _full((1, ATTN_W)), _full((1, ATTN_W)), _full((256, 256))],
        out_specs=[blk(3 * CONV_W), blk(2 * ATTN_W), blk(CONV_W), blk(ATTN_W), blk(ATTN_W), blk(ATTN_W)],
        out_shape=[jax.ShapeDtypeStruct((t, 3 * CONV_W), BF16), jax.ShapeDtypeStruct((t, 2 * ATTN_W), BF16),
                   jax.ShapeDtypeStruct((t, CONV_W), BF16), jax.ShapeDtypeStruct((t, ATTN_W), F32),
                   jax.ShapeDtypeStruct((t, ATTN_W), F32), jax.ShapeDtypeStruct((t, ATTN_W), F32)],
        scratch_shapes=[pltpu.VMEM((8, CONV_W), F32)],
        compiler_params=_cparams("arbitrary"),
    )(x, g_mix, w_in, conv_w, conv_b, qg, kg, bd)


SUPER = 16 * QK_BLOCK
KEYS = 2 * QK_BLOCK


def _rows(start, size, dil):
    return pl.ds(start, size) if dil == 1 else pl.ds(start, size, stride=dil)


def _attn_bias(sl_ref, dil):
    qi = lax.broadcasted_iota(jnp.int32, (KEYS, KEYS), 0)
    kj = lax.broadcasted_iota(jnp.int32, (KEYS, KEYS), 1)
    step = jnp.bitwise_and(qi, QK_BLOCK - 1) + QK_BLOCK - kj
    slope = jnp.where(qi < QK_BLOCK, sl_ref[0, 0:1, 0:1], sl_ref[0, 1:2, 0:1])
    bias = jnp.where(jnp.logical_and(step >= 0, step <= QK_BLOCK), -slope * (step * dil).astype(F32), -jnp.inf)
    return bias, kj >= QK_BLOCK


def _unit_start(u, dil):
    if dil == 1:
        return pl.multiple_of(u * QK_BLOCK, QK_BLOCK)
    if dil == 4:
        return jnp.bitwise_and(u, 3) + (u // 4) * (4 * QK_BLOCK)
    return u


def _stack_heads(a, head0):
    zero = jnp.zeros_like(a)
    return jnp.concatenate([jnp.where(head0, a, zero), jnp.where(head0, zero, a)], axis=0)


def _attn_fwd(q, k, v, slopes):
    t = q.shape[0]
    nsb = t // SUPER

    def body(q_ref, kc_ref, kp_ref, vc_ref, vp_ref, sl_ref, o_ref, l_ref, kk, vv, ob, lb):
        s = pl.program_id(1)
        kk[0:SUPER, :] = kp_ref[...]
        kk[SUPER:, :] = kc_ref[...]
        vv[0:SUPER, :] = vp_ref[...]
        vv[SUPER:, :] = vc_ref[...]
        head0 = lax.broadcasted_iota(jnp.int32, (QK_BLOCK, QK_BLOCK), 1) < HEAD_DIM

        for b, dil in enumerate(DILATIONS):
            bias, own_half = _attn_bias(sl_ref, dil)

            def unit(u, carry, b=b, dil=dil, bias=bias, own_half=own_half):
                start = _unit_start(u, dil)
                first_key = SUPER + start - QK_BLOCK * dil
                q2 = _stack_heads(q_ref[_rows(start, QK_BLOCK, dil), :].astype(BF16), head0)
                k2 = kk[_rows(first_key, KEYS, dil), :].astype(BF16)
                v2 = vv[_rows(first_key, KEYS, dil), :].astype(BF16)
                has_prev = jnp.logical_or(s > 0, start >= QK_BLOCK * dil)
                sc = jnp.where(jnp.logical_or(own_half, has_prev), _mm_nt(q2, k2) + bias, -jnp.inf)
                m = jnp.max(sc, axis=-1, keepdims=True)
                e = jnp.exp(sc - m)
                den = jnp.sum(e, axis=-1, keepdims=True)
                o2 = _mm(e.astype(BF16), v2) / den
                l2 = m + jnp.log(den)
                ob[b, _rows(start, QK_BLOCK, dil), :] = jnp.where(head0, o2[0:QK_BLOCK], o2[QK_BLOCK:])
                lb[b, _rows(start, QK_BLOCK, dil), :] = jnp.where(head0, l2[0:QK_BLOCK], l2[QK_BLOCK:])
                return carry

            lax.fori_loop(0, SUPER // QK_BLOCK, unit, 0, unroll=16)

        def merge(i, carry):
            rows = pl.ds(pl.multiple_of(i * 256, 256), 256)
            la, lb_, lc = lb[0, rows, :], lb[1, rows, :], lb[2, rows, :]
            mx = jnp.maximum(jnp.maximum(la, lb_), lc)
            wa, wb, wc = jnp.exp(la - mx), jnp.exp(lb_ - mx), jnp.exp(lc - mx)
            sw = wa + wb + wc
            o_ref[rows, :] = ((wa * ob[0, rows, :] + wb * ob[1, rows, :] + wc * ob[2, rows, :]) / sw).astype(BF16)
            l_ref[rows, :] = mx + jnp.log(sw)
            return carry

        lax.fori_loop(0, SUPER // 256, merge, 0)

    cur = pl.BlockSpec((SUPER, QK_BLOCK), lambda p, s: (s, p))
    prev = pl.BlockSpec((SUPER, QK_BLOCK), lambda p, s: (jnp.maximum(s - 1, 0), p))
    return pl.pallas_call(
        body, name="attn_fwd", grid=(4, nsb),
        in_specs=[cur, cur, prev, cur, prev, pl.BlockSpec((1, 2, QK_BLOCK), lambda p, s: (p, 0, 0))],
        out_specs=[cur, cur],
        out_shape=[jax.ShapeDtypeStruct((t, ATTN_W), BF16), jax.ShapeDtypeStruct((t, ATTN_W), F32)],
        scratch_shapes=[pltpu.VMEM((2 * SUPER, QK_BLOCK), F32), pltpu.VMEM((2 * SUPER, QK_BLOCK), F32),
                        pltpu.VMEM((3, SUPER, QK_BLOCK), F32), pltpu.VMEM((3, SUPER, QK_BLOCK), F32)],
        compiler_params=_cparams("parallel", "arbitrary"),
    )(q, k, k, v, v, slopes)


def _outproj_fwd(ya, yc, x, goc, goa, w_out, tm):
    t = x.shape[0]

    def body(ya_ref, yc_ref, x_ref, goc_ref, goa_ref, w_ref, x1_ref):
        _, ychat = _rms_stats(yc_ref[...].astype(F32))
        _, yahat = _rms_stats(ya_ref[...].astype(F32))
        acc = _mm((ychat * goc_ref[...]).astype(BF16), w_ref[0:CONV_W, :])
        acc += _mm((yahat * goa_ref[...]).astype(BF16), w_ref[CONV_W:, :])
        x1_ref[...] = x_ref[...] + acc

    def blk(c):
        return pl.BlockSpec((tm, c), lambda i: (i, 0))

    return pl.pallas_call(
        body, name="outproj_fwd", grid=(t // tm,),
        in_specs=[blk(ATTN_W), blk(CONV_W), blk(D_MODEL), _full((1, CONV_W)), _full((1, ATTN_W)),
                  _full((D_MODEL, D_MODEL))],
        out_specs=blk(D_MODEL),
        out_shape=jax.ShapeDtypeStruct((t, D_MODEL), F32),
        compiler_params=_cparams("parallel"),
    )(ya, yc, x, goc, goa, w_out)


def _ffn_fwd(x1, g_ffn, w_gate_t, w_up_t, w_down, fcw, fcb, tm):
    t = x1.shape[0]

    def body(x_ref, g_ref, wg_ref, wu_ref, wd_ref, cw_ref, cb_ref, gp_ref, up_ref, h_ref, x2_ref, carry_ref):
        @pl.when(pl.program_id(0) == 0)
        def _():
            carry_ref[...] = jnp.zeros_like(carry_ref)

        xv = x_ref[...]
        _, xhat = _rms_stats(xv)
        h = (xhat * g_ref[...]).astype(BF16)
        h_ref[...] = h
        gp = _mm_nt(h, wg_ref[...])
        gp_ref[...] = gp.astype(BF16)
        gate, _, _ = _conv_fwd(gp, carry_ref[7:8, :], carry_ref[6:7, :], cw_ref, cb_ref)
        carry_ref[...] = gp[tm - 8:tm, :]
        up = _mm_nt(h, wu_ref[...])
        up_ref[...] = up.astype(BF16)
        a = (gate * _sigmoid(gate) * up).astype(BF16)
        x2_ref[...] = xv + _mm(a, wd_ref[...])

    def blk(c):
        return pl.BlockSpec((tm, c), lambda i: (i, 0))

    return pl.pallas_call(
        body, name="ffn_fwd", grid=(t // tm,),
        in_specs=[blk(D_MODEL), _full((1, D_MODEL)), _full((D_FF, D_MODEL)), _full((D_FF, D_MODEL)),
                  _full((D_FF, D_MODEL)), _full((3, D_FF)), _full((1, D_FF))],
        out_specs=[blk(D_FF), blk(D_FF), blk(D_MODEL), blk(D_MODEL)],
        out_shape=[jax.ShapeDtypeStruct((t, D_FF), BF16), jax.ShapeDtypeStruct((t, D_FF), BF16),
                   jax.ShapeDtypeStruct((t, D_MODEL), BF16), jax.ShapeDtypeStruct((t, D_MODEL), F32)],
        scratch_shapes=[pltpu.VMEM((8, D_FF), F32)],
        compiler_params=_cparams("arbitrary"),
    )(x1, g_ffn, w_gate_t, w_up_t, w_down, fcw, fcb)


def _ple_fwd_bwd(x2, p, target, g_ple, w_pg, w_pp, tm):
    t = x2.shape[0]

    def body(x_ref, p_ref, t_ref, g_ref, wg_ref, wp_ref, dx_ref, dxb_ref, loss_ref, dwg_ref, dwp_ref, dg_ref):
        @pl.when(pl.program_id(0) == 0)
        def _():
            loss_ref[...] = jnp.zeros_like(loss_ref)
            dwg_ref[...] = jnp.zeros_like(dwg_ref)
            dwp_ref[...] = jnp.zeros_like(dwp_ref)
            dg_ref[...] = jnp.zeros_like(dg_ref)

        xv = x_ref[...]
        r, xhat = _rms_stats(xv)
        g = g_ref[...]
        h = (xhat * g).astype(BF16)
        pg = _sigmoid(_mm(h, wg_ref[...]))
        pb = p_ref[...].astype(BF16)
        pp = _mm(pb, wp_ref[...])
        err = xv + pg * pp - t_ref[...]
        loss_ref[...] += 0.5 * jnp.sum(jnp.mean(err * err, axis=-1, keepdims=True))
        dx3 = err * (1.0 / D_MODEL)
        d_pp = (dx3 * pg).astype(BF16)
        d_pre = (dx3 * pp * pg * (1.0 - pg)).astype(BF16)
        dwp_ref[...] += _mm_tn(pb, d_pp)
        dwg_ref[...] += _mm_tn(h, d_pre)
        dh = _mm_nt(d_pre, wg_ref[...])
        dg_ref[...] += jnp.sum(dh * xhat, axis=0, keepdims=True)
        dx2 = dx3 + _rms_bwd(dh, xhat, r, g)
        dx_ref[...] = dx2
        dxb_ref[...] = dx2.astype(BF16)

    def blk(c):
        return pl.BlockSpec((tm, c), lambda i: (i, 0))

    return pl.pallas_call(
        body, name="ple_fwd_bwd", grid=(t // tm,),
        in_specs=[blk(D_MODEL), blk(PLE_DIM), blk(D_MODEL), _full((1, D_MODEL)), _full((D_MODEL, D_MODEL)),
                  _full((PLE_DIM, D_MODEL))],
        out_specs=[blk(D_MODEL), blk(D_MODEL), _full((8, 128)), _full((D_MODEL, D_MODEL)),
                   _full((PLE_DIM, D_MODEL)), _full((1, D_MODEL))],
        out_shape=[jax.ShapeDtypeStruct((t, D_MODEL), F32), jax.ShapeDtypeStruct((t, D_MODEL), BF16),
                   jax.ShapeDtypeStruct((8, 128), F32),
                   jax.ShapeDtypeStruct((D_MODEL, D_MODEL), F32), jax.ShapeDtypeStruct((PLE_DIM, D_MODEL), F32),
                   jax.ShapeDtypeStruct((1, D_MODEL), F32)],
        compiler_params=_cparams("arbitrary"),
    )(x2, p, target, g_ple, w_pg, w_pp)


def _ffn_bwd(dx2, h2, gp, up, w_gate, w_up, w_down, fcw, fcb, tm):
    t = dx2.shape[0]
    nblk = t // tm
    fc = D_FF // FF_CHUNKS
    half = tm // FFN_BWD_PARTS

    def body(dx_ref, h_ref, gp_ref, gph_ref, up_ref, wg_ref, wu_ref, wd_ref, cw_ref, cb_ref,
             dh_ref, dwd_hbm, dwu_hbm, dwg_hbm, dcw_ref, dcb_ref, carry_ref, a_scr, dup_scr, dgp_scr,
             dwd_acc, dwu_acc, dwg_acc, stage, stage_sem):
        i = pl.program_id(1)

        @pl.when(i == 0)
        def _():
            carry_ref[...] = jnp.zeros_like(carry_ref)
            dwd_acc[...] = jnp.zeros_like(dwd_acc)
            dwu_acc[...] = jnp.zeros_like(dwu_acc)
            dwg_acc[...] = jnp.zeros_like(dwg_acc)
            dcw_ref[...] = jnp.zeros_like(dcw_ref)
            dcb_ref[...] = jnp.zeros_like(dcb_ref)

        keep = (i < nblk - 1).astype(F32)
        later = carry_ref[...]
        for hf in reversed(range(FFN_BWD_PARTS)):
            rows = slice(hf * half, (hf + 1) * half)
            dxb = dx_ref[rows, :]
            gp_v = gp_ref[rows, :].astype(F32)
            if hf > 0:
                before = gp_ref[hf * half - 16:hf * half, :].astype(F32)
            else:
                before = gph_ref[...].astype(F32) * keep
            gate, gp1, gp2 = _conv_fwd(gp_v, before[15:16, :], before[14:15, :], cw_ref, cb_ref)
            s = _sigmoid(gate)
            silu = gate * s
            up_v = up_ref[rows, :].astype(F32)
            da = _mm_nt(dxb, wd_ref[...])
            a_scr[rows, :] = (silu * up_v).astype(BF16)
            d_up = (da * silu).astype(BF16)
            dup_scr[rows, :] = d_up
            d_gate = da * up_v * (s * (1.0 + gate * (1.0 - s)))
            d_gp = _conv_bwd_input(d_gate, later[0:1, :], later[1:2, :], cw_ref).astype(BF16)
            dgp_scr[rows, :] = d_gp
            later = d_gate[0:8, :]
            dcw_ref[0:1, :] += jnp.sum(d_gate * gp2, axis=0, keepdims=True)
            dcw_ref[1:2, :] += jnp.sum(d_gate * gp1, axis=0, keepdims=True)
            dcw_ref[2:3, :] += jnp.sum(d_gate * gp_v, axis=0, keepdims=True)
            dcb_ref[...] += jnp.sum(d_gate, axis=0, keepdims=True)
            dh_ref[rows, :] = (_mm(d_gp, wg_ref[...]) + _mm(d_up, wu_ref[...])).astype(BF16)
        carry_ref[...] = later
        dwd_acc[...] += _mm_tn(a_scr[...], dx_ref[...])
        dwu_acc[...] += _mm_tn(h_ref[...], dup_scr[...])
        dwg_acc[...] += _mm_tn(h_ref[...], dgp_scr[...])

        @pl.when(i == nblk - 1)
        def _():
            rows = pl.ds(pl.multiple_of(pl.program_id(0) * fc, 16), fc)
            for acc, out, flip in ((dwd_acc, dwd_hbm, False), (dwu_acc, dwu_hbm, True), (dwg_acc, dwg_hbm, True)):
                stage[...] = (acc[...].T if flip else acc[...]).astype(BF16)
                copy = pltpu.make_async_copy(stage, out.at[rows, :], stage_sem)
                copy.start()
                copy.wait()

    def rev(i):
        return nblk - 1 - i

    one = pl.Buffered(1)
    in_specs = [
        pl.BlockSpec((tm, D_MODEL), lambda j, i: (rev(i), 0)),
        pl.BlockSpec((tm, D_MODEL), lambda j, i: (rev(i), 0)),
        pl.BlockSpec((tm, fc), lambda j, i: (rev(i), j)),
        pl.BlockSpec((16, fc), lambda j, i: (jnp.maximum(rev(i) * (tm // 16) - 1, 0), j)),
        pl.BlockSpec((tm, fc), lambda j, i: (rev(i), j)),
        pl.BlockSpec((fc, D_MODEL), lambda j, i: (j, 0), pipeline_mode=one),
        pl.BlockSpec((fc, D_MODEL), lambda j, i: (j, 0), pipeline_mode=one),
        pl.BlockSpec((fc, D_MODEL), lambda j, i: (j, 0), pipeline_mode=one),
        pl.BlockSpec((3, fc), lambda j, i: (0, j)),
        pl.BlockSpec((1, fc), lambda j, i: (0, j)),
    ]
    out_specs = [
        pl.BlockSpec((None, tm, D_MODEL), lambda j, i: (j, rev(i), 0)),
        ANY, ANY, ANY,
        pl.BlockSpec((3, fc), lambda j, i: (0, j)),
        pl.BlockSpec((1, fc), lambda j, i: (0, j)),
    ]
    return pl.pallas_call(
        body, name="ffn_bwd", grid=(FF_CHUNKS, nblk), in_specs=in_specs, out_specs=out_specs,
        out_shape=[jax.ShapeDtypeStruct((FF_CHUNKS, t, D_MODEL), BF16), jax.ShapeDtypeStruct((D_FF, D_MODEL), BF16),
                   jax.ShapeDtypeStruct((D_FF, D_MODEL), BF16), jax.ShapeDtypeStruct((D_FF, D_MODEL), BF16),
                   jax.ShapeDtypeStruct((3, D_FF), F32), jax.ShapeDtypeStruct((1, D_FF), F32)],
        scratch_shapes=[pltpu.VMEM((8, fc), F32), pltpu.VMEM((tm, fc), BF16), pltpu.VMEM((tm, fc), BF16),
                        pltpu.VMEM((tm, fc), BF16), pltpu.VMEM((fc, D_MODEL), F32), pltpu.VMEM((D_MODEL, fc), F32),
                        pltpu.VMEM((D_MODEL, fc), F32), pltpu.VMEM((fc, D_MODEL), BF16), pltpu.SemaphoreType.DMA],
        compiler_params=_cparams("arbitrary", "arbitrary", vmem=V7X_VMEM_LIMIT_LARGE),
    )(dx2, h2, gp, gp, up, w_gate, w_up, w_down, fcw, fcb)


def _outproj_bwd(dh2, dx2, x1, g_ffn, w_out, yc, ya, goc, goa, zconv, conv_w, conv_b, bd, tm):
    t = x1.shape[0]
    nblk = t // tm

    def body(dh_ref, dx2_ref, x1_ref, g_ref, w_ref, yc_ref, ya_ref, goc_ref, goa_ref, zc_ref, zch_ref, cw_ref, cb_ref,
             bd_ref, dx1_ref, dya_ref, dd_ref, dzc_ref, dw_ref, dg_ref, dgoc_ref, dgoa_ref, dcw_ref, dcb_ref,
             carry_ref):
        i = pl.program_id(0)

        @pl.when(i == 0)
        def _():
            carry_ref[...] = jnp.zeros_like(carry_ref)
            for ref in (dw_ref, dg_ref, dgoc_ref, dgoa_ref, dcw_ref, dcb_ref):
                ref[...] = jnp.zeros_like(ref)

        keep = (i < nblk - 1).astype(F32)
        dh2_v = dh_ref[0].astype(F32)
        for j in range(1, FF_CHUNKS):
            dh2_v = dh2_v + dh_ref[j].astype(F32)
        r, xhat = _rms_stats(x1_ref[...])
        dg_ref[...] += jnp.sum(dh2_v * xhat, axis=0, keepdims=True)
        dx1 = dx2_ref[...] + _rms_bwd(dh2_v, xhat, r, g_ref[...])
        dx1_ref[...] = dx1
        dx1b = dx1.astype(BF16)
        dy = _mm_nt(dx1b, w_ref[...])

        yc_v = yc_ref[...].astype(F32)
        rc, ychat = _rms_stats(yc_v)
        dw_ref[0:CONV_W, :] += _mm_tn((ychat * goc_ref[...]).astype(BF16), dx1b)
        dyc = dy[:, 0:CONV_W]
        dgoc_ref[...] += jnp.sum(dyc * ychat, axis=0, keepdims=True)
        d_yc = _rms_bwd(dyc, ychat, rc, goc_ref[...])

        ya_v = ya_ref[...].astype(F32)
        ra, yahat = _rms_stats(ya_v)
        dw_ref[CONV_W:, :] += _mm_tn((yahat * goa_ref[...]).astype(BF16), dx1b)
        dya = dy[:, CONV_W:]
        dgoa_ref[...] += jnp.sum(dya * yahat, axis=0, keepdims=True)
        d_ya = _rms_bwd(dya, yahat, ra, goa_ref[...])
        dya_ref[...] = d_ya
        dd_ref[...] = _seg_sum64(d_ya * ya_v, bd_ref)

        zb = zc_ref[:, 0:CONV_W].astype(F32)
        zc = zc_ref[:, CONV_W:2 * CONV_W].astype(F32)
        zx = zc_ref[:, 2 * CONV_W:3 * CONV_W].astype(F32)
        u = zc * zx
        uh = (zch_ref[:, CONV_W:2 * CONV_W].astype(F32) * zch_ref[:, 2 * CONV_W:3 * CONV_W].astype(F32)) * keep
        cv, u1, u2 = _conv_fwd(u, uh[15:16, :], uh[14:15, :], cw_ref, cb_ref)
        d_cv = d_yc * zb
        d_u = _conv_bwd_input(d_cv, carry_ref[0:1, :], carry_ref[1:2, :], cw_ref)
        carry_ref[...] = d_cv[0:8, :]
        dcw_ref[0:1, :] += jnp.sum(d_cv * u2, axis=0, keepdims=True)
        dcw_ref[1:2, :] += jnp.sum(d_cv * u1, axis=0, keepdims=True)
        dcw_ref[2:3, :] += jnp.sum(d_cv * u, axis=0, keepdims=True)
        dcb_ref[...] += jnp.sum(d_cv, axis=0, keepdims=True)
        dzc_ref[:, 0:CONV_W] = (d_yc * cv).astype(BF16)
        dzc_ref[:, CONV_W:2 * CONV_W] = (d_u * zx).astype(BF16)
        dzc_ref[:, 2 * CONV_W:3 * CONV_W] = (d_u * zc).astype(BF16)

    def rev(i):
        return nblk - 1 - i

    def blk(c):
        return pl.BlockSpec((tm, c), lambda i: (rev(i), 0))

    in_specs = [
        pl.BlockSpec((FF_CHUNKS, tm, D_MODEL), lambda i: (0, rev(i), 0)),
        blk(D_MODEL), blk(D_MODEL), _full((1, D_MODEL)), _full((D_MODEL, D_MODEL)),
        blk(CONV_W), blk(ATTN_W), _full((1, CONV_W)), _full((1, ATTN_W)),
        blk(3 * CONV_W),
        pl.BlockSpec((16, 3 * CONV_W), lambda i: (jnp.maximum(rev(i) * (tm // 16) - 1, 0), 0)),
        _full((3, CONV_W)), _full((1, CONV_W)), _full((256, 256)),
    ]
    out_specs = [blk(D_MODEL), blk(ATTN_W), blk(ATTN_W), blk(3 * CONV_W), _full((D_MODEL, D_MODEL)),
                 _full((1, D_MODEL)), _full((1, CONV_W)), _full((1, ATTN_W)), _full((3, CONV_W)), _full((1, CONV_W))]
    return pl.pallas_call(
        body, name="outproj_bwd", grid=(nblk,), in_specs=in_specs, out_specs=out_specs,
        out_shape=[jax.ShapeDtypeStruct((t, D_MODEL), F32), jax.ShapeDtypeStruct((t, ATTN_W), F32),
                   jax.ShapeDtypeStruct((t, ATTN_W), F32), jax.ShapeDtypeStruct((t, 3 * CONV_W), BF16),
                   jax.ShapeDtypeStruct((D_MODEL, D_MODEL), F32), jax.ShapeDtypeStruct((1, D_MODEL), F32),
                   jax.ShapeDtypeStruct((1, CONV_W), F32), jax.ShapeDtypeStruct((1, ATTN_W), F32),
                   jax.ShapeDtypeStruct((3, CONV_W), F32), jax.ShapeDtypeStruct((1, CONV_W), F32)],
        scratch_shapes=[pltpu.VMEM((8, CONV_W), F32)],
        compiler_params=_cparams("arbitrary"),
    )(dh2, dx2, x1, g_ffn, w_out, yc, ya, goc, goa, zconv, zconv, conv_w, conv_b, bd)


def _attn_bwd(q, k, v, dya, lse, dd, slopes):
    t = q.shape[0]
    nsb = t // SUPER

    def body(q_ref, kc_ref, kp_ref, vc_ref, vp_ref, dy_ref, l_ref, d_ref, sl_ref, dq_ref, dk_ref, dv_ref,
             kk, vv, dkacc, dvacc):
        s = pl.program_id(1)

        @pl.when(s == 0)
        def _():
            dkacc[...] = jnp.zeros_like(dkacc)
            dvacc[...] = jnp.zeros_like(dvacc)

        dkacc[0:SUPER, :] = dkacc[SUPER:, :]
        dvacc[0:SUPER, :] = dvacc[SUPER:, :]
        dkacc[SUPER:, :] = jnp.zeros((SUPER, QK_BLOCK), F32)
        dvacc[SUPER:, :] = jnp.zeros((SUPER, QK_BLOCK), F32)

        @pl.when(s < nsb)
        def _():
            kk[0:SUPER, :] = kp_ref[...]
            kk[SUPER:, :] = kc_ref[...]
            vv[0:SUPER, :] = vp_ref[...]
            vv[SUPER:, :] = vc_ref[...]
            head0 = lax.broadcasted_iota(jnp.int32, (QK_BLOCK, QK_BLOCK), 1) < HEAD_DIM

            for b, dil in enumerate(DILATIONS):
                bias, own_half = _attn_bias(sl_ref, dil)

                def unit(u, carry, b=b, dil=dil, bias=bias, own_half=own_half):
                    start = _unit_start(u, dil)
                    first_key = SUPER + start - QK_BLOCK * dil
                    qrows = _rows(start, QK_BLOCK, dil)
                    krows = _rows(first_key, KEYS, dil)
                    q2 = _stack_heads(q_ref[qrows, :].astype(BF16), head0)
                    dy2 = _stack_heads(dy_ref[qrows, :].astype(BF16), head0)
                    lv, dv_ = l_ref[qrows, :], d_ref[qrows, :]
                    l2 = jnp.concatenate([lv[:, 0:1], lv[:, HEAD_DIM:HEAD_DIM + 1]], axis=0)
                    d2 = jnp.concatenate([dv_[:, 0:1], dv_[:, HEAD_DIM:HEAD_DIM + 1]], axis=0)
                    k2 = kk[krows, :].astype(BF16)
                    v2 = vv[krows, :].astype(BF16)
                    has_prev = jnp.logical_or(s > 0, start >= QK_BLOCK * dil)
                    sc = jnp.where(jnp.logical_or(own_half, has_prev), _mm_nt(q2, k2) + bias, -jnp.inf)
                    prob = jnp.exp(sc - l2)
                    ds = (prob * (_mm_nt(dy2, v2) - d2)).astype(BF16)
                    dvacc[krows, :] += _mm_tn(prob.astype(BF16), dy2)
                    dkacc[krows, :] += _mm_tn(ds, q2)
                    dq2 = _mm(ds, k2)
                    dq = jnp.where(head0, dq2[0:QK_BLOCK], dq2[QK_BLOCK:]) * ATTN_SCALE
                    if b == 0:
                        dq_ref[qrows, :] = dq
                    else:
                        dq_ref[qrows, :] += dq
                    return carry

                lax.fori_loop(0, SUPER // QK_BLOCK, unit, 0, unroll=8)

        dk_ref[...] = dkacc[0:SUPER, :]
        dv_ref[...] = dvacc[0:SUPER, :].astype(BF16)

    def cur_map(p, s):
        return (jnp.minimum(s, nsb - 1), p)

    def prev_map(p, s):
        return (jnp.clip(s - 1, 0, nsb - 1), p)

    cur = pl.BlockSpec((SUPER, QK_BLOCK), cur_map)
    prev = pl.BlockSpec((SUPER, QK_BLOCK), prev_map)
    return pl.pallas_call(
        body, name="attn_bwd", grid=(4, nsb + 1),
        in_specs=[cur, cur, prev, cur, prev, cur, cur, cur, pl.BlockSpec((1, 2, QK_BLOCK), lambda p, s: (p, 0, 0))],
        out_specs=[cur, prev, prev],
        out_shape=[jax.ShapeDtypeStruct((t, ATTN_W), F32), jax.ShapeDtypeStruct((t, ATTN_W), F32),
                   jax.ShapeDtypeStruct((t, ATTN_W), BF16)],
        scratch_shapes=[pltpu.VMEM((2 * SUPER, QK_BLOCK), F32)] * 4,
        compiler_params=_cparams("parallel", "arbitrary"),
    )(q, k, k, v, v, dya, lse, dd, slopes)


def _attn_bwd_per_branch_unused(q, k, v, dya, lse, dd, slopes, dil):
    t = q.shape[0]
    length = t // dil
    chunk = _attn_chunk(t, dil)
    nch = length // chunk
    nb = chunk // QK_BLOCK
    nblocks = length // QK_BLOCK
    view = (length, dil * ATTN_W)
    ext = chunk + QK_BLOCK

    def body(q_ref, dy_ref, l_ref, d_ref, k_ref, v_ref, qn_ref, dyn_ref, ln_ref, dn_ref, kh_ref, vh_ref, sl_ref,
             dq_ref, dk_ref, dv_ref, qbuf, dybuf, lbuf, dbuf, kbuf, vbuf, dkacc, dvacc):
        c = pl.program_id(2)
        qbuf[0:chunk, :] = q_ref[...]
        qbuf[chunk:, :] = qn_ref[...]
        dybuf[0:chunk, :] = dy_ref[...].astype(BF16)
        dybuf[chunk:, :] = dyn_ref[...].astype(BF16)
        lbuf[0:chunk, :] = l_ref[...]
        lbuf[chunk:, :] = ln_ref[...]
        dbuf[0:chunk, :] = d_ref[...]
        dbuf[chunk:, :] = dn_ref[...]
        kbuf[0:QK_BLOCK, :] = kh_ref[...]
        kbuf[QK_BLOCK:, :] = k_ref[...]
        vbuf[0:QK_BLOCK, :] = vh_ref[...]
        vbuf[QK_BLOCK:, :] = v_ref[...]
        valid_cur, valid_prev, dist_cur, dist_prev, head0 = _attn_masks(dil)

        def pair(qb, dyb, lv, dv_, kb, vb, valid, dist):
            dq = jnp.zeros((QK_BLOCK, QK_BLOCK), F32)
            dk = jnp.zeros((QK_BLOCK, QK_BLOCK), F32)
            dvv = jnp.zeros((QK_BLOCK, QK_BLOCK), F32)
            for hh in range(2):
                sl = sl_ref[0, hh:hh + 1, :]
                hm = head0 if hh == 0 else jnp.logical_not(head0)
                col = hh * HEAD_DIM
                qm = jnp.where(hm, qb, jnp.zeros_like(qb))
                dym = jnp.where(hm, dyb, jnp.zeros_like(dyb))
                s = jnp.where(valid, _mm_nt(qm, kb) - sl * dist, -jnp.inf)
                prob = jnp.exp(s - lv[:, col:col + 1])
                ds = (prob * (_mm_nt(dym, vb) - dv_[:, col:col + 1])).astype(BF16)
                dvv += _mm_tn(prob.astype(BF16), dym)
                dk += _mm_tn(ds, qm)
                dq += jnp.where(hm, _mm(ds, kb), 0.0)
            return dq, dk, dvv

        def blk(j, carry):
            off = pl.multiple_of(j * QK_BLOCK, QK_BLOCK)
            nxt = pl.multiple_of(off + QK_BLOCK, QK_BLOCK)
            qb = qbuf[pl.ds(off, QK_BLOCK), :]
            dyb = dybuf[pl.ds(off, QK_BLOCK), :]
            lv = lbuf[pl.ds(off, QK_BLOCK), :]
            dv_ = dbuf[pl.ds(off, QK_BLOCK), :]
            dq_c, dk_c, dv_c = pair(qb, dyb, lv, dv_, kbuf[pl.ds(nxt, QK_BLOCK), :], vbuf[pl.ds(nxt, QK_BLOCK), :],
                                    valid_cur, dist_cur)
            dkacc[pl.ds(nxt, QK_BLOCK), :] = dk_c
            dvacc[pl.ds(nxt, QK_BLOCK), :] = dv_c
            has_prev = jnp.logical_or(c > 0, j > 0)
            dq_p, dk_p, dv_p = pair(qb, dyb, lv, dv_, kbuf[pl.ds(off, QK_BLOCK), :], vbuf[pl.ds(off, QK_BLOCK), :],
                                    jnp.logical_and(valid_prev, has_prev), dist_prev)

            @pl.when(j > 0)
            def _():
                dkacc[pl.ds(off, QK_BLOCK), :] += dk_p
                dvacc[pl.ds(off, QK_BLOCK), :] += dv_p

            dq_ref[pl.ds(off, QK_BLOCK), :] = (dq_c + dq_p) * ATTN_SCALE
            return carry

        lax.fori_loop(0, nb, blk, 0)

        @pl.when(c < nch - 1)
        def _():
            _, dk_p, dv_p = pair(qbuf[chunk:, :], dybuf[chunk:, :], lbuf[chunk:, :], dbuf[chunk:, :],
                                 kbuf[chunk:, :], vbuf[chunk:, :], valid_prev, dist_prev)
            dkacc[chunk:, :] += dk_p
            dvacc[chunk:, :] += dv_p

        dk_ref[...] = dkacc[QK_BLOCK:, :]
        dv_ref[...] = dvacc[QK_BLOCK:, :]

    def cmap(p, r, c):
        return (c, r * 4 + p)

    def before(p, r, c):
        return (jnp.maximum(c * nb - 1, 0), r * 4 + p)

    def after(p, r, c):
        return (jnp.minimum((c + 1) * nb, nblocks - 1), r * 4 + p)

    main = pl.BlockSpec((chunk, QK_BLOCK), cmap)
    hb = pl.BlockSpec((QK_BLOCK, QK_BLOCK), before)
    ha = pl.BlockSpec((QK_BLOCK, QK_BLOCK), after)
    qv, kv, vv = q.reshape(view), k.reshape(view), v.reshape(view)
    dyv, lv, ddv = dya.reshape(view), lse.reshape(view), dd.reshape(view)
    outs = pl.pallas_call(
        body, name=f"attn_bwd_d{dil}", grid=(4, dil, nch),
        in_specs=[main] * 6 + [ha] * 4 + [hb] * 2 + [pl.BlockSpec((1, 2, QK_BLOCK), lambda p, r, c: (p, 0, 0))],
        out_specs=[main] * 3,
        out_shape=[jax.ShapeDtypeStruct(view, F32)] * 3,
        scratch_shapes=[pltpu.VMEM((ext, QK_BLOCK), BF16), pltpu.VMEM((ext, QK_BLOCK), BF16),
                        pltpu.VMEM((ext, QK_BLOCK), F32), pltpu.VMEM((ext, QK_BLOCK), F32),
                        pltpu.VMEM((ext, QK_BLOCK), BF16), pltpu.VMEM((ext, QK_BLOCK), BF16),
                        pltpu.VMEM((ext, QK_BLOCK), F32), pltpu.VMEM((ext, QK_BLOCK), F32)],
        compiler_params=_cparams("arbitrary", "arbitrary", "arbitrary"),
    )(qv, dyv, lv, ddv, kv, vv, qv, dyv, lv, ddv, kv, vv, slopes)
    return [o.reshape(t, ATTN_W) for o in outs]


def _inproj_bwd(dq, dk, dv, dzconv, zqk, x, dx1, g_mix, w_in, qg, kg, bd, tm):
    t = x.shape[0]

    def body(dq_ref, dk_ref, dv_ref, dzc_ref, zqk_ref, x_ref, dx1_ref, g_ref, w_ref, qg_ref,
             kg_ref, bd_ref, dx_ref, dw_ref, dg_ref, dqg_ref, dkg_ref):
        @pl.when(pl.program_id(0) == 0)
        def _():
            for ref in (dw_ref, dg_ref, dqg_ref, dkg_ref):
                ref[...] = jnp.zeros_like(ref)

        parts = [dzc_ref[...]]
        for j, (dn_ref, gain_ref, dgain_ref) in enumerate(((dq_ref, qg_ref, dqg_ref), (dk_ref, kg_ref, dkg_ref))):
            dn = dn_ref[...]
            z = zqk_ref[:, j * ATTN_W:(j + 1) * ATTN_W].astype(F32)
            r = lax.rsqrt(_seg_sum64(z * z, bd_ref) * (1.0 / HEAD_DIM) + EPS)
            zhat = z * r
            dgain_ref[...] += jnp.sum(dn * zhat, axis=0, keepdims=True)
            gd = dn * gain_ref[...]
            parts.append((r * (gd - zhat * (_seg_sum64(gd * zhat, bd_ref) * (1.0 / HEAD_DIM)))).astype(BF16))
        parts.append(dv_ref[...].astype(BF16))
        dz = jnp.concatenate(parts, axis=1)

        r, xhat = _rms_stats(x_ref[...])
        g = g_ref[...]
        dw_ref[...] += _mm_tn((xhat * g).astype(BF16), dz)
        dh = _mm_nt(dz, w_ref[...])
        dg_ref[...] += jnp.sum(dh * xhat, axis=0, keepdims=True)
        dx_ref[...] = dx1_ref[...] + _rms_bwd(dh, xhat, r, g)

    def blk(c):
        return pl.BlockSpec((tm, c), lambda i: (i, 0))

    return pl.pallas_call(
        body, name="inproj_bwd", grid=(t // tm,),
        in_specs=[blk(ATTN_W)] * 3 + [blk(3 * CONV_W), blk(2 * ATTN_W), blk(D_MODEL), blk(D_MODEL), _full((1, D_MODEL)),
                                      _full((D_MODEL, IN_COLS)), _full((1, ATTN_W)), _full((1, ATTN_W)),
                                      _full((256, 256))],
        out_specs=[blk(D_MODEL), _full((D_MODEL, IN_COLS)), _full((1, D_MODEL)), _full((1, ATTN_W)),
                   _full((1, ATTN_W))],
        out_shape=[jax.ShapeDtypeStruct((t, D_MODEL), F32), jax.ShapeDtypeStruct((D_MODEL, IN_COLS), F32),
                   jax.ShapeDtypeStruct((1, D_MODEL), F32), jax.ShapeDtypeStruct((1, ATTN_W), F32),
                   jax.ShapeDtypeStruct((1, ATTN_W), F32)],
        compiler_params=_cparams("arbitrary"),
    )(dq, dk, dv, dzconv, zqk, x, dx1, g_mix, w_in, qg, kg, bd)


def _ordered_after(a, token):
    return a if token is None else a + token[0:1, 0:1].reshape((1,) * a.ndim)


def _local_step(x, p, target, w, tms, hooks=None):
    hooks = hooks or {}
    bd = jnp.kron(jnp.eye(4, dtype=F32), jnp.ones((HEAD_DIM, HEAD_DIM), F32)).astype(BF16)
    qg = jnp.tile(w["q_norm_g"], (1, 8))
    kg = jnp.tile(w["k_norm_g"], (1, 8))
    slopes = jnp.exp2(-jnp.arange(1, 9, dtype=F32))
    slopes = jnp.broadcast_to(slopes.reshape(4, 2, 1), (4, 2, QK_BLOCK))

    zconv, zqk, yc, q, k, v = _inproj_fwd(x, w["g_mix"], w["w_in"], w["conv_w"], w["conv_b"], qg, kg, bd, tms[0])
    ya, lse = _attn_fwd(q, k, v, slopes)
    if "late_weights" in hooks:
        w = {**w, **hooks["late_weights"](lse)}
    x1 = _outproj_fwd(ya, yc, x, w["g_out_conv"], w["g_out_attn"], w["w_out"], tms[0])
    gp, up, h2, x2 = _ffn_fwd(x1, w["g_ffn"], w["w_gate"], w["w_up"], w["w_down"], w["ffn_conv_w"], w["ffn_conv_b"],
                              tms[1])
    dx2, dx2b, loss, dw_pg, dw_pp, dg_ple = _ple_fwd_bwd(x2, p, target, w["g_ple"], w["w_ple_gate"], w["w_ple_proj"], tms[0])
    dh2, dw_down, dw_up, dw_gate, dfcw, dfcb = _ffn_bwd(dx2b, h2, gp, up, w["w_gate"], w["w_up"], w["w_down"],
                                                        w["ffn_conv_w"], w["ffn_conv_b"], tms[0])
    token = None
    if "ffn_grads" in hooks:
        token = hooks["ffn_grads"]({"w_ple_gate": dw_pg, "w_ple_proj": dw_pp, "w_down": dw_down, "w_up": dw_up,
                                    "w_gate": dw_gate})
    dx1, dya, dd, dzconv, dw_out, dg_ffn, dgoc, dgoa, dcw, dcb = _outproj_bwd(
        dh2, dx2, x1, _ordered_after(w["g_ffn"], token), w["w_out"], yc, ya, w["g_out_conv"], w["g_out_attn"], zconv,
        w["conv_w"], w["conv_b"], bd, tms[1])
    token = hooks["outproj_done"](dx1) if "outproj_done" in hooks else None
    dq, dk, dv = _attn_bwd(q, k, v, dya, lse, dd, _ordered_after(slopes, token))
    dx, dw_in, dg_mix, dqg, dkg = _inproj_bwd(dq, dk, dv, dzconv, zqk, x, dx1, w["g_mix"], w["w_in"], qg, kg, bd,
                                              tms[0])
    grads = {
        "g_mix": dg_mix, "w_in": dw_in, "conv_w": dcw, "conv_b": dcb,
        "q_norm_g": dqg.reshape(8, HEAD_DIM).sum(0, keepdims=True),
        "k_norm_g": dkg.reshape(8, HEAD_DIM).sum(0, keepdims=True),
        "g_out_conv": dgoc, "g_out_attn": dgoa, "w_out": dw_out, "g_ffn": dg_ffn, "w_gate": dw_gate, "w_up": dw_up,
        "ffn_conv_w": dfcw, "ffn_conv_b": dfcb, "w_down": dw_down, "g_ple": dg_ple, "w_ple_gate": dw_pg,
        "w_ple_proj": dw_pp,
    }
    return loss, dx, grads


ANY = pl.BlockSpec(memory_space=pl.ANY)
MESH = pl.DeviceIdType.MESH


def _all_gather(shards, name):
    n = len(shards)

    def body(*refs):
        ins, outs = refs[:n], refs[n:2 * n]
        send_sems, recv_sems, local_sems = refs[2 * n:]
        x, y, c = lax.axis_index("x"), lax.axis_index("y"), lax.axis_index("c")
        me, sibling = (x, y, c), (x, y, 1 - c)
        chips = [(1 - x, y), (x, 1 - y), (1 - x, 1 - y)]

        def slot(dev):
            return 4 * dev[0] + 2 * dev[1] + dev[2]

        def copy(b, k, block, to, src=None):
            dst = outs[b].at[slot(block)]
            return pltpu.make_async_remote_copy(
                src_ref=dst if src is None else src, dst_ref=dst, send_sem=send_sems.at[b, k],
                recv_sem=recv_sems.at[b, k], device_id=to, device_id_type=MESH)

        mine = [pltpu.make_async_copy(ins[b], outs[b].at[slot(me)], local_sems.at[b]) for b in range(n)]
        first, passed = [], []
        for b in range(n):
            mine[b].start()
            first.append(copy(b, 0, me, sibling, src=ins[b]))
            first += [copy(b, 1 + j, me, (*chip, c), src=ins[b]) for j, chip in enumerate(chips)]
        for cp in first:
            cp.start()
        for j, chip in enumerate(chips):
            for b in range(n):
                copy(b, 1 + j, (*chip, c), me).wait_recv()
                fwd = copy(b, 4 + j, (*chip, c), sibling)
                fwd.start()
                passed.append(fwd)
        for b in range(n):
            copy(b, 0, sibling, me).wait_recv()
            for j, chip in enumerate(chips):
                copy(b, 4 + j, (*chip, 1 - c), me).wait_recv()
        for cp in first + passed:
            cp.wait_send()
        for cp in mine:
            cp.wait()

    return pl.pallas_call(
        body, name=name,
        in_specs=[ANY] * n, out_specs=[ANY] * n,
        out_shape=[jax.ShapeDtypeStruct((N_DEV,) + s.shape, s.dtype) for s in shards],
        scratch_shapes=[pltpu.SemaphoreType.DMA((n, 7)), pltpu.SemaphoreType.DMA((n, 7)),
                        pltpu.SemaphoreType.DMA((n,))],
    )(*shards)


HBM = pl.BlockSpec(memory_space=pltpu.HBM)
SEM = pl.BlockSpec(memory_space=pltpu.SEMAPHORE)
EFFECT = pltpu.SideEffectType.DATAFLOW_SIDE_EFFECTING
FLIPS = ((0, 0, 1), (0, 1, 0), (0, 1, 1), (1, 0, 0), (1, 0, 1), (1, 1, 0), (1, 1, 1))


def _flip_peers():
    pos = (lax.axis_index("x"), lax.axis_index("y"), lax.axis_index("c"))
    return [tuple(1 - a if f else a for a, f in zip(pos, flip)) for flip in FLIPS]


def _hbm(a):
    return pltpu.with_memory_space_constraint(a, pltpu.HBM)


def _split_start(name, srcs, lands, plan, n_copies, after):
    n, m = len(srcs), len(lands)

    def body(*refs):
        send_sems, recv_sems, token = refs[n + m + 1], refs[n + m + 2], refs[-1]
        for i, (src, dst, peer) in enumerate(plan(refs[:n], refs[n:n + m])):
            pltpu.make_async_remote_copy(src_ref=src, dst_ref=dst, send_sem=send_sems.at[i], recv_sem=recv_sems.at[i],
                                         device_id=peer, device_id_type=MESH).start()
        token[...] = jnp.zeros_like(token)

    outs = pl.pallas_call(
        body, name=name + "_start",
        in_specs=[HBM] * (n + m) + [ANY],
        out_specs=[SEM, SEM] + [HBM] * (n + m) + [pl.BlockSpec(memory_space=pltpu.VMEM)],
        out_shape=[pltpu.SemaphoreType.DMA((n_copies,)), pltpu.SemaphoreType.DMA((n_copies,))]
        + [pltpu.HBM(a.shape, a.dtype) for a in list(srcs) + list(lands)] + [jax.ShapeDtypeStruct((8, 128), F32)],
        input_output_aliases={i: 2 + i for i in range(n + m)},
        compiler_params=pltpu.CompilerParams(has_side_effects=EFFECT),
    )(*[_hbm(a) for a in list(srcs) + list(lands)], after)
    return (outs[0], outs[1], outs[2:2 + n], outs[2 + n:2 + n + m]), outs[-1]


def _split_wait(name, started, plan, after):
    send_sems, recv_sems, srcs, lands = started
    n, m = len(srcs), len(lands)

    def body(*refs):
        send_ref, recv_ref = refs[n + m], refs[n + m + 1]
        for i, (src, dst, peer) in enumerate(plan(refs[:n], refs[n:n + m])):
            copy = pltpu.make_async_remote_copy(src_ref=src, dst_ref=dst, send_sem=send_ref.at[i],
                                                recv_sem=recv_ref.at[i], device_id=peer, device_id_type=MESH)
            copy.wait_send()
            copy.wait_recv()

    outs = pl.pallas_call(
        body, name=name + "_wait",
        in_specs=[HBM] * (n + m) + [SEM, SEM, ANY],
        out_specs=[HBM] * (n + m),
        out_shape=[pltpu.HBM(a.shape, a.dtype) for a in list(srcs) + list(lands)],
        input_output_aliases={i: i for i in range(n + m)},
        compiler_params=pltpu.CompilerParams(has_side_effects=EFFECT),
    )(*srcs, *lands, send_sems, recv_sems, after)
    return outs[:n], outs[n:]


def _gather_plan(srcs, lands):
    slot = 4 * lax.axis_index("x") + 2 * lax.axis_index("y") + lax.axis_index("c")
    return [(src, land.at[slot], peer) for src, land in zip(srcs, lands) for peer in _flip_peers()]


def _sibling_plan(srcs, lands):
    x, y, c = lax.axis_index("x"), lax.axis_index("y"), lax.axis_index("c")
    return [(src.at[k, 1 - c], land.at[k], (x, y, 1 - c)) for src, land in zip(srcs, lands) for k in range(N_CHIP)]


def _chip_plan(srcs, lands):
    x, y, c = lax.axis_index("x"), lax.axis_index("y"), lax.axis_index("c")
    return [(src.at[2 * cx + cy], land.at[2 * x + y], (cx, cy, c))
            for src, land in zip(srcs, lands) for cx, cy in ((1 - x, y), (x, 1 - y), (1 - x, 1 - y))]


def _row_tile(rows):
    for tr in range(min(rows, 512), 15, -16):
        if rows % tr == 0:
            return tr
    return rows


def _sibling_exchange(gs):
    n = len(gs)

    def body(*refs):
        g_refs, land_refs = refs[:n], refs[n:2 * n]
        send_sems, recv_sems = refs[2 * n:]
        x, y, c = lax.axis_index("x"), lax.axis_index("y"), lax.axis_index("c")
        copies = [pltpu.make_async_remote_copy(
            src_ref=g_refs[b].at[k, 1 - c], dst_ref=land_refs[b].at[k], send_sem=send_sems.at[b, k],
            recv_sem=recv_sems.at[b, k], device_id=(x, y, 1 - c), device_id_type=MESH)
            for b in range(n) for k in range(N_CHIP)]
        for cp in copies:
            cp.start()
        for cp in copies:
            cp.wait()

    return pl.pallas_call(
        body, name="rs_sibling_exchange", in_specs=[ANY] * n, out_specs=[ANY] * n,
        out_shape=[jax.ShapeDtypeStruct((N_CHIP,) + g.shape[2:], g.dtype) for g in gs],
        scratch_shapes=[pltpu.SemaphoreType.DMA((n, N_CHIP)), pltpu.SemaphoreType.DMA((n, N_CHIP))],
    )(*gs)


def _pair_sum(g, land, core, name):
    rows, cols = land.shape[1:]
    tr = _row_tile(rows)

    def body(c_ref, g_ref, l_ref, o_ref):
        o_ref[...] = (g_ref[...].astype(F32) + l_ref[...].astype(F32)).astype(o_ref.dtype)

    return pl.pallas_call(
        body, name=f"rs_pair_sum_{name}",
        grid_spec=pltpu.PrefetchScalarGridSpec(
            num_scalar_prefetch=1, grid=(N_CHIP, rows // tr),
            in_specs=[pl.BlockSpec((None, None, tr, cols), lambda k, i, c_ref: (k, c_ref[0], i, 0)),
                      pl.BlockSpec((None, tr, cols), lambda k, i, c_ref: (k, i, 0))],
            out_specs=pl.BlockSpec((None, tr, cols), lambda k, i, c_ref: (k, i, 0))),
        out_shape=jax.ShapeDtypeStruct(land.shape, land.dtype),
        compiler_params=_cparams("parallel", "parallel"),
    )(core, g, land)


def _chip_exchange(parts):
    n = len(parts)

    def body(*refs):
        p_refs, land_refs = refs[:n], refs[n:2 * n]
        send_sems, recv_sems, local_sems = refs[2 * n:]
        x, y, c = lax.axis_index("x"), lax.axis_index("y"), lax.axis_index("c")
        mine = 2 * x + y
        chips = [(1 - x, y), (x, 1 - y), (1 - x, 1 - y)]
        own = [pltpu.make_async_copy(p_refs[b].at[mine], land_refs[b].at[mine], local_sems.at[b]) for b in range(n)]
        for cp in own:
            cp.start()
        copies = [pltpu.make_async_remote_copy(
            src_ref=p_refs[b].at[2 * cx + cy], dst_ref=land_refs[b].at[mine], send_sem=send_sems.at[b, j],
            recv_sem=recv_sems.at[b, j], device_id=(cx, cy, c), device_id_type=MESH)
            for b in range(n) for j, (cx, cy) in enumerate(chips)]
        for cp in copies:
            cp.start()
        for b in range(n):
            for j, (cx, cy) in enumerate(chips):
                pltpu.make_async_remote_copy(
                    src_ref=p_refs[b].at[mine], dst_ref=land_refs[b].at[2 * cx + cy], send_sem=send_sems.at[b, j],
                    recv_sem=recv_sems.at[b, j], device_id=(cx, cy, c), device_id_type=MESH).wait_recv()
        for cp in copies:
            cp.wait_send()
        for cp in own:
            cp.wait()

    return pl.pallas_call(
        body, name="rs_chip_exchange", in_specs=[ANY] * n, out_specs=[ANY] * n,
        out_shape=[jax.ShapeDtypeStruct(p.shape, p.dtype) for p in parts],
        scratch_shapes=[pltpu.SemaphoreType.DMA((n, 3)), pltpu.SemaphoreType.DMA((n, 3)),
                        pltpu.SemaphoreType.DMA((n,))],
    )(*parts)


def _adamw(own, arrived, chip, w, m, v, name):
    k, rows, cols = arrived.shape
    tr = _row_tile(rows)
    c1 = 1.0 / (1.0 - ADAM_B1 ** ADAM_STEP)
    c2 = 1.0 / (1.0 - ADAM_B2 ** ADAM_STEP)

    def body(chip_ref, o_ref, p_ref, w_ref, m_ref, v_ref, g_ref, d_ref, nm_ref, nv_ref):
        def slab(j):
            return jnp.where(chip_ref[0] == j, o_ref[j], p_ref[j]).astype(F32)

        g = slab(0)
        for j in range(1, k):
            g = g + slab(j)
        g_ref[...] = g
        nm = ADAM_B1 * m_ref[...] + (1.0 - ADAM_B1) * g
        nv = ADAM_B2 * v_ref[...] + (1.0 - ADAM_B2) * (g * g)
        nm_ref[...] = nm
        nv_ref[...] = nv
        d_ref[...] = -ADAM_LR * ((nm * c1) / (jnp.sqrt(nv * c2) + ADAM_EPS) + ADAM_WD * w_ref[...])

    blk = pl.BlockSpec((tr, cols), lambda i, c: (i, 0))
    stack = pl.BlockSpec((k, tr, cols), lambda i, c: (0, i, 0))
    return pl.pallas_call(
        body, name=name,
        grid_spec=pltpu.PrefetchScalarGridSpec(num_scalar_prefetch=1, grid=(rows // tr,),
                                               in_specs=[stack, stack, blk, blk, blk], out_specs=[blk] * 4),
        out_shape=[jax.ShapeDtypeStruct((rows, cols), F32)] * 4,
        compiler_params=_cparams("parallel"),
    )(chip, own, arrived, w, m, v)


SMALL_LAYOUT = (("g_mix", 0, 1024), ("conv_b", 1, 512), ("q_norm_g", 2, 64), ("k_norm_g", 3, 64),
                ("g_out_conv", 4, 512), ("g_out_attn", 5, 512), ("g_ffn", 6, 1024), ("ffn_conv_b", 7, 2816),
                ("g_ple", 10, 1024))
CONV_W_ROW = 11
FFN_CONV_W_ROW = 14
LOSS_ROW = 23


def _row_pieces(cols):
    return [(c, min(1024, cols - c)) for c in range(0, cols, 1024)]


def _pack_small(grads, loss_tile):
    names = [n for n, _, _ in SMALL_LAYOUT]

    def body(*refs):
        ins, cw_ref, fcw_ref, loss_ref, out_ref = refs[:len(names)], refs[-4], refs[-3], refs[-2], refs[-1]
        out_ref[...] = jnp.zeros_like(out_ref)
        for ref, (_, row, cols) in zip(ins, SMALL_LAYOUT):
            for j, (c, width) in enumerate(_row_pieces(cols)):
                out_ref[row + j:row + j + 1, 0:width] = ref[:, c:c + width]
        for k in range(3):
            out_ref[CONV_W_ROW + k:CONV_W_ROW + k + 1, 0:CONV_W] = cw_ref[k:k + 1, :]
            for j, (c, width) in enumerate(_row_pieces(D_FF)):
                row = FFN_CONV_W_ROW + 3 * k + j
                out_ref[row:row + 1, 0:width] = fcw_ref[k:k + 1, c:c + width]
        out_ref[LOSS_ROW:LOSS_ROW + 1, 0:128] = loss_ref[0:1, :]

    return pl.pallas_call(
        body, name="pack_small_grads", out_shape=jax.ShapeDtypeStruct((SMALL_ROWS, 1024), F32),
    )(*[grads[n] for n in names], grads["conv_w"], grads["ffn_conv_w"], loss_tile)


def _adamw_small(arrived, conv_parts, fconv_parts, wts, mom, var):
    names = [n for n, _, _ in SMALL_LAYOUT] + ["conv_w", "ffn_conv_w"]
    c1 = 1.0 / (1.0 - ADAM_B1 ** ADAM_STEP)
    c2 = 1.0 / (1.0 - ADAM_B2 ** ADAM_STEP)
    n = len(names)

    def body(*refs):
        land, cw_ref, fcw_ref = refs[0], refs[1], refs[2]
        state = refs[3:3 + 3 * n]
        outs = refs[3 + 3 * n:]

        def total(piece):
            acc = piece(0)
            for d in range(1, N_DEV):
                acc = acc + piece(d)
            return acc

        for i, name in enumerate(names):
            if name == "conv_w":
                g = total(lambda d: cw_ref[d])
            elif name == "ffn_conv_w":
                g = total(lambda d: fcw_ref[d])
            else:
                _, row, cols = SMALL_LAYOUT[i]
                pieces = [total(lambda d, j=j, width=width: land[d, row + j:row + j + 1, 0:width])
                          for j, (_, width) in enumerate(_row_pieces(cols))]
                g = pieces[0] if len(pieces) == 1 else jnp.concatenate(pieces, axis=1)
            w_ref, m_ref, v_ref = state[3 * i:3 * i + 3]
            nm = ADAM_B1 * m_ref[...] + (1.0 - ADAM_B1) * g
            nv = ADAM_B2 * v_ref[...] + (1.0 - ADAM_B2) * (g * g)
            outs[4 * i][...] = g
            outs[4 * i + 1][...] = -ADAM_LR * ((nm * c1) / (jnp.sqrt(nv * c2) + ADAM_EPS) + ADAM_WD * w_ref[...])
            outs[4 * i + 2][...] = nm
            outs[4 * i + 3][...] = nv
        outs[-1][...] = total(lambda d: land[d, LOSS_ROW:LOSS_ROW + 1, 0:128])

    state = [a[nm_] for nm_ in names for a in (wts, mom, var)]
    shapes = [jax.ShapeDtypeStruct(wts[nm_].shape, F32) for nm_ in names for _ in range(4)]
    outs = pl.pallas_call(
        body, name="adamw_small", out_shape=shapes + [jax.ShapeDtypeStruct((1, 128), F32)],
    )(arrived, conv_parts, fconv_parts, *state)
    return {nm_: tuple(outs[4 * i:4 * i + 4]) for i, nm_ in enumerate(names)}, outs[-1][0, 0]


COL_SHARDED = ("w_in", "w_ple_proj")
TRANSPOSED = ("w_gate", "w_up")
REPLICATED = (("g_mix", 1024), ("conv_b", 512), ("q_norm_g", 64), ("k_norm_g", 64), ("g_out_conv", 512),
              ("g_out_attn", 512), ("g_ffn", 1024), ("ffn_conv_b", 2816), ("g_ple", 1024))
CONV_SHARDED = (("conv_w", CONV_W), ("ffn_conv_w", D_FF))


def _gathered_to_full(name, gathered):
    if name in COL_SHARDED:
        return gathered.transpose(1, 0, 2).reshape(gathered.shape[1], -1)
    return gathered.reshape(-1, gathered.shape[2])


def _full_to_stacked(name, grad, shard_shape):
    sr, sc = shard_shape
    if name in COL_SHARDED:
        a = grad.reshape(sr, N_DEV, sc).transpose(1, 0, 2)
    else:
        a = grad.reshape(N_DEV, sr, sc)
    return a.astype(BF16).reshape(N_CHIP, 2, sr, sc)


def _pad_rows(vec, rows):
    return jnp.pad(vec, (0, rows * 1024 - vec.shape[0])).reshape(rows, 1024)


def kernel(x, p, g_mix, w_in, conv_w, conv_b, q_norm_g, k_norm_g, g_out_conv, g_out_attn, w_out, g_ffn, w_gate, w_up, ffn_conv_w, ffn_conv_b, w_down, g_ple, w_ple_gate, w_ple_proj, loss_target, m_g_mix, m_w_in, m_conv_w, m_conv_b, m_q_norm_g, m_k_norm_g, m_g_out_conv, m_g_out_attn, m_w_out, m_g_ffn, m_w_gate, m_w_up, m_ffn_conv_w, m_ffn_conv_b, m_w_down, m_g_ple, m_w_ple_gate, m_w_ple_proj, v_g_mix, v_w_in, v_conv_w, v_conv_b, v_q_norm_g, v_k_norm_g, v_g_out_conv, v_g_out_attn, v_w_out, v_g_ffn, v_w_gate, v_w_up, v_ffn_conv_w, v_ffn_conv_b, v_w_down, v_g_ple, v_w_ple_gate, v_w_ple_proj):
    args = dict(locals())
    names = ["g_mix", "w_in", "conv_w", "conv_b", "q_norm_g", "k_norm_g", "g_out_conv", "g_out_attn", "w_out", "g_ffn",
             "w_gate", "w_up", "ffn_conv_w", "ffn_conv_b", "w_down", "g_ple", "w_ple_gate", "w_ple_proj"]
    big = [n for n, _ in BIG_ROWS]
    conv = [n for n, _ in CONV_SHARDED]
    def local(prefix):
        out = {n: (args[prefix + n][0] if n in big or n in conv else args[prefix + n]) for n in names}
        out.update({n: out[n].T for n in TRANSPOSED})
        return out

    wts, mom, var = local(""), local("m_"), local("v_")
    shard_shapes = {n: wts[n].shape for n in big}
    dev = 4 * lax.axis_index("x") + 2 * lax.axis_index("y") + lax.axis_index("c")
    core = lax.axis_index("c").astype(jnp.int32).reshape(1)

    conv_local = _pad_rows(jnp.concatenate([wts[n].reshape(-1) for n in conv]), 8).reshape(8, 1024)
    late = [n for n in big if n != "w_in"]
    w_in_all, conv_all = _all_gather([wts["w_in"].astype(BF16), conv_local], "gather_weights")
    late_shards = [wts[n].astype(BF16) for n in late]
    gathering, token = _split_start("gather_late_weights", late_shards,
                                    [lax.empty((N_DEV,) + s.shape, BF16) for s in late_shards], _gather_plan,
                                    7 * len(late), w_in_all)
    full = dict(wts)
    full["w_in"] = _gathered_to_full("w_in", w_in_all)
    full["g_mix"] = _ordered_after(wts["g_mix"], token)
    flying = {}

    def late_weights(after):
        shards, lands = _split_wait("gather_late_weights", gathering, _gather_plan, after)
        return {n: _gathered_to_full(n, lax.dynamic_update_slice(land, shard[None], (dev, 0, 0)))
                for n, land, shard in zip(late, lands, shards)}

    early = ["w_ple_gate", "w_ple_proj", "w_down", "w_up", "w_gate"]

    def ffn_grads(g):
        stacked = [_full_to_stacked(n, g[n], shard_shapes[n]) for n in early]
        flying["sibling"], tok = _split_start("rs_sibling_early", stacked,
                                              [lax.empty((N_CHIP,) + s.shape[2:], BF16) for s in stacked],
                                              _sibling_plan, N_CHIP * len(early), g["w_down"])
        return tok

    def outproj_done(after):
        stacked, landed = _split_wait("rs_sibling_early", flying["sibling"], _sibling_plan, after)
        parts = [_pair_sum(g, l, core, n) for n, g, l in zip(early, stacked, landed)]
        flying["chip"], tok = _split_start("rs_chip_early", parts, [lax.empty(q.shape, BF16) for q in parts],
                                           _chip_plan, 3 * len(early), landed[0])
        return tok

    off = 0
    for n, width in CONV_SHARDED:
        sc = width // N_DEV
        a = conv_all.reshape(N_DEV, -1)[:, off:off + 3 * sc].reshape(N_DEV, 3, sc)
        full[n] = a.transpose(1, 0, 2).reshape(3, width)
        off += 3 * sc

    loss, dx, grads = _local_step(x[0], p[0, 0], loss_target[0], full, (512, 256),
                                  {"late_weights": late_weights, "ffn_grads": ffn_grads, "outproj_done": outproj_done})

    chip = (2 * lax.axis_index("x") + lax.axis_index("y")).astype(jnp.int32).reshape(1)

    def adamw_of(group, parts, arrived):
        return {n: _adamw(own, got, chip, wts[n], mom[n], var[n], f"adamw_{n}")
                for n, own, got in zip(group, parts, arrived)}

    last = [n for n in big if n not in early]
    stacked = [_full_to_stacked(n, grads[n], shard_shapes[n]) for n in last]
    flying["sibling_last"], tok = _split_start("rs_sibling_last", stacked,
                                               [lax.empty((N_CHIP,) + s.shape[2:], BF16) for s in stacked],
                                               _sibling_plan, N_CHIP * len(last), dx)
    (small_all,) = _all_gather([_ordered_after(_pack_small(grads, loss), tok)], "gather_small_grads")
    stacked, landed = _split_wait("rs_sibling_last", flying["sibling_last"], _sibling_plan, small_all)
    parts = [_pair_sum(g, l, core, n) for n, g, l in zip(last, stacked, landed)]
    flying["chip_last"], tok = _split_start("rs_chip_last", parts, [lax.empty(q.shape, BF16) for q in parts],
                                            _chip_plan, 3 * len(last), landed[0])

    parts, arrived = _split_wait("rs_chip_early", flying["chip"], _chip_plan, tok)
    out = adamw_of(early, parts, arrived)
    taps = small_all[:, CONV_W_ROW:CONV_W_ROW + 3, 0:CONV_W]
    ftaps = small_all[:, FFN_CONV_W_ROW:FFN_CONV_W_ROW + 9, :].reshape(N_DEV, 3, 3 * 1024)
    small_out, loss_total = _adamw_small(
        small_all, lax.dynamic_slice(taps, (0, 0, dev * (CONV_W // N_DEV)), (N_DEV, 3, CONV_W // N_DEV)),
        lax.dynamic_slice(ftaps, (0, 0, dev * (D_FF // N_DEV)), (N_DEV, 3, D_FF // N_DEV)), wts, mom, var)
    out.update(small_out)
    parts, arrived = _split_wait("rs_chip_last", flying["chip_last"], _chip_plan, small_out["g_mix"][0])
    out.update(adamw_of(last, parts, arrived))
    def result(n, which):
        a = out[n][which]
        return (a.T if n in TRANSPOSED else a).reshape(args[n].shape)

    return (loss_total, dx[None], *[result(n, which) for which in range(4) for n in names])
```

```python
import functools

import jax
import jax.numpy as jnp
from jax import lax
from jax.experimental import pallas as pl
from jax.experimental.pallas import tpu as pltpu

F32 = jnp.float32
BF16 = jnp.bfloat16

D_MODEL = 1024
CONV_W = 512
ATTN_W = 512
HEAD_DIM = 64
D_FF = 2816
PLE_DIM = 256
IN_COLS = 3 * CONV_W + 3 * ATTN_W
EPS = 1e-6
QK_BLOCK = 128
DILATIONS = (1, 4, 16)
ATTN_SCALE = HEAD_DIM ** -0.5

ADAM_LR = 0.001
ADAM_B1 = 0.9
ADAM_B2 = 0.999
ADAM_EPS = 1e-08
ADAM_WD = 0.01
ADAM_STEP = 10

N_DEV = 8
N_CHIP = 4
V7X_VMEM_LIMIT = 56 * 1024 * 1024
FF_CHUNKS = 2
V7X_VMEM_LIMIT_LARGE = 62 * 1024 * 1024
FFN_BWD_PARTS = 1

BIG_ROWS = (("w_in", 384), ("w_out", 128), ("w_gate", 352), ("w_up", 352), ("w_down", 352),
            ("w_ple_gate", 128), ("w_ple_proj", 32))
BIG_TOTAL = sum(r for _, r in BIG_ROWS)
SMALL_ROWS = 24


def _cparams(*sem, vmem=V7X_VMEM_LIMIT):
    return pltpu.CompilerParams(dimension_semantics=sem, vmem_limit_bytes=vmem)


def _mm(a, b):
    return jnp.dot(a, b, preferred_element_type=F32)


def _mm_nt(a, b):
    return lax.dot_general(a, b, (((1,), (1,)), ((), ())), preferred_element_type=F32)


def _mm_tn(a, b):
    return lax.dot_general(a, b, (((0,), (0,)), ((), ())), preferred_element_type=F32)


def _full(shape):
    nd = len(shape)
    return pl.BlockSpec(shape, lambda *_: (0,) * nd)


def _rms_stats(x):
    r = lax.rsqrt(jnp.mean(x * x, axis=-1, keepdims=True) + EPS)
    return r, x * r


def _rms_bwd(dy, xhat, r, g):
    gd = dy * g
    return r * (gd - xhat * jnp.mean(gd * xhat, axis=-1, keepdims=True))


def _seg_sum64(v, bd_ref):
    outs = []
    for c in range(0, v.shape[1], 256):
        vc = v[:, c:c + 256]
        hi = vc.astype(BF16)
        lo = (vc - hi.astype(F32)).astype(BF16)
        outs.append(_mm(hi, bd_ref[...]) + _mm(lo, bd_ref[...]))
    return outs[0] if len(outs) == 1 else jnp.concatenate(outs, axis=1)


def _shift_rows(u, k, edge_rows):
    out = pltpu.roll(u, k, 0)
    row = lax.broadcasted_iota(jnp.int32, (8, u.shape[1]), 0)
    head = out[0:8]
    for j in range(k):
        head = jnp.where(row == j, edge_rows[k - 1 - j], head)
    return jnp.concatenate([head, out[8:]], axis=0)


def _shift_rows_up(u, k, edge_rows):
    n = u.shape[0]
    out = pltpu.roll(u, n - k, 0)
    row = lax.broadcasted_iota(jnp.int32, (8, u.shape[1]), 0)
    tail = out[n - 8:n]
    for j in range(k):
        tail = jnp.where(row == 8 - k + j, edge_rows[j], tail)
    return jnp.concatenate([out[0:n - 8], tail], axis=0)


def _conv_fwd(u, c1, c2, w_ref, b_ref):
    u1 = _shift_rows(u, 1, (c1,))
    u2 = _shift_rows(u, 2, (c1, c2))
    y = u2 * w_ref[0:1, :] + u1 * w_ref[1:2, :] + u * w_ref[2:3, :] + b_ref[...]
    return y, u1, u2


def _conv_bwd_input(dy, n1row, n2row, w_ref):
    d1 = _shift_rows_up(dy, 1, (n1row,))
    d2 = _shift_rows_up(dy, 2, (n1row, n2row))
    return dy * w_ref[2:3, :] + d1 * w_ref[1:2, :] + d2 * w_ref[0:1, :]


def _sigmoid(x):
    return 1.0 / (1.0 + jnp.exp(-x))


def _inproj_fwd(x, g_mix, w_in, conv_w, conv_b, qg, kg, bd, tm):
    t = x.shape[0]

    def body(x_ref, g_ref, w_ref, cw_ref, cb_ref, qg_ref, kg_ref, bd_ref,
             zc_ref, zqk_ref, yc_ref, q_ref, k_ref, v_ref, carry_ref):
        @pl.when(pl.program_id(0) == 0)
        def _():
            carry_ref[...] = jnp.zeros_like(carry_ref)

        _, xhat = _rms_stats(x_ref[...])
        h = (xhat * g_ref[...]).astype(BF16)
        zconv = _mm(h, w_ref[:, 0:3 * CONV_W])
        zc_ref[...] = zconv.astype(BF16)
        u = zconv[:, CONV_W:2 * CONV_W] * zconv[:, 2 * CONV_W:3 * CONV_W]
        cv, _, _ = _conv_fwd(u, carry_ref[7:8, :], carry_ref[6:7, :], cw_ref, cb_ref)
        yc_ref[...] = (zconv[:, 0:CONV_W] * cv).astype(BF16)
        carry_ref[...] = u[tm - 8:tm, :]

        zqk = _mm(h, w_ref[:, 3 * CONV_W:3 * CONV_W + 2 * ATTN_W])
        zqk_ref[...] = zqk.astype(BF16)
        for j, (gain_ref, out_ref, scale) in enumerate(((qg_ref, q_ref, ATTN_SCALE), (kg_ref, k_ref, 1.0))):
            z = zqk[:, j * ATTN_W:(j + 1) * ATTN_W]
            r = lax.rsqrt(_seg_sum64(z * z, bd_ref) * (1.0 / HEAD_DIM) + EPS)
            out_ref[...] = z * r * gain_ref[...] * scale
        v_ref[...] = _mm(h, w_ref[:, 3 * CONV_W + 2 * ATTN_W:IN_COLS])

    def blk(c):
        return pl.BlockSpec((tm, c), lambda i: (i, 0))

    return pl.pallas_call(
        body, name="inproj_fwd", grid=(t // tm,),
        in_specs=[blk(D_MODEL), _full((1, D_MODEL)), _full((D_MODEL, IN_COLS)), _full((3, CONV_W)),
                  _full((1, CONV_W)), _full((1, ATTN_W)), _full((1, ATTN_W)), _full((256, 256))],
        out_specs=[blk(3 * CONV_W), blk(2 * ATTN_W), blk(CONV_W), blk(ATTN_W), blk(ATTN_W), blk(ATTN_W)],
        out_shape=[jax.ShapeDtypeStruct((t, 3 * CONV_W), BF16), jax.ShapeDtypeStruct((t, 2 * ATTN_W), BF16),
                   jax.ShapeDtypeStruct((t, CONV_W), BF16), jax.ShapeDtypeStruct((t, ATTN_W), F32),
                   jax.ShapeDtypeStruct((t, ATTN_W), F32), jax.ShapeDtypeStruct((t, ATTN_W), F32)],
        scratch_shapes=[pltpu.VMEM((8, CONV_W), F32)],
        compiler_params=_cparams("arbitrary"),
    )(x, g_mix, w_in, conv_w, conv_b, qg, kg, bd)


SUPER = 16 * QK_BLOCK
KEYS = 2 * QK_BLOCK


def _rows(start, size, dil):
    return pl.ds(start, size) if dil == 1 else pl.ds(start, size, stride=dil)


def _attn_bias(sl_ref, dil):
    qi = lax.broadcasted_iota(jnp.int32, (KEYS, KEYS), 0)
    kj = lax.broadcasted_iota(jnp.int32, (KEYS, KEYS), 1)
    step = jnp.bitwise_and(qi, QK_BLOCK - 1) + QK_BLOCK - kj
    slope = jnp.where(qi < QK_BLOCK, sl_ref[0, 0:1, 0:1], sl_ref[0, 1:2, 0:1])
    bias = jnp.where(jnp.logical_and(step >= 0, step <= QK_BLOCK), -slope * (step * dil).astype(F32), -jnp.inf)
    return bias, kj >= QK_BLOCK


def _unit_start(u, dil):
    if dil == 1:
        return pl.multiple_of(u * QK_BLOCK, QK_BLOCK)
    if dil == 4:
        return jnp.bitwise_and(u, 3) + (u // 4) * (4 * QK_BLOCK)
    return u


def _stack_heads(a, head0):
    zero = jnp.zeros_like(a)
    return jnp.concatenate([jnp.where(head0, a, zero), jnp.where(head0, zero, a)], axis=0)


def _attn_fwd(q, k, v, slopes):
    t = q.shape[0]
    nsb = t // SUPER

    def body(q_ref, kc_ref, kp_ref, vc_ref, vp_ref, sl_ref, o_ref, l_ref, kk, vv, ob, lb):
        s = pl.program_id(1)
        kk[0:SUPER, :] = kp_ref[...]
        kk[SUPER:, :] = kc_ref[...]
        vv[0:SUPER, :] = vp_ref[...]
        vv[SUPER:, :] = vc_ref[...]
        head0 = lax.broadcasted_iota(jnp.int32, (QK_BLOCK, QK_BLOCK), 1) < HEAD_DIM

        for b, dil in enumerate(DILATIONS):
            bias, own_half = _attn_bias(sl_ref, dil)

            def unit(u, carry, b=b, dil=dil, bias=bias, own_half=own_half):
                start = _unit_start(u, dil)
                first_key = SUPER + start - QK_BLOCK * dil
                q2 = _stack_heads(q_ref[_rows(start, QK_BLOCK, dil), :].astype(BF16), head0)
                k2 = kk[_rows(first_key, KEYS, dil), :].astype(BF16)
                v2 = vv[_rows(first_key, KEYS, dil), :].astype(BF16)
                has_prev = jnp.logical_or(s > 0, start >= QK_BLOCK * dil)
                sc = jnp.where(jnp.logical_or(own_half, has_prev), _mm_nt(q2, k2) + bias, -jnp.inf)
                m = jnp.max(sc, axis=-1, keepdims=True)
                e = jnp.exp(sc - m)
                den = jnp.sum(e, axis=-1, keepdims=True)
                o2 = _mm(e.astype(BF16), v2) / den
                l2 = m + jnp.log(den)
                ob[b, _rows(start, QK_BLOCK, dil), :] = jnp.where(head0, o2[0:QK_BLOCK], o2[QK_BLOCK:])
                lb[b, _rows(start, QK_BLOCK, dil), :] = jnp.where(head0, l2[0:QK_BLOCK], l2[QK_BLOCK:])
                return carry

            lax.fori_loop(0, SUPER // QK_BLOCK, unit, 0, unroll=16)

        def merge(i, carry):
            rows = pl.ds(pl.multiple_of(i * 256, 256), 256)
            la, lb_, lc = lb[0, rows, :], lb[1, rows, :], lb[2, rows, :]
            mx = jnp.maximum(jnp.maximum(la, lb_), lc)
            wa, wb, wc = jnp.exp(la - mx), jnp.exp(lb_ - mx), jnp.exp(lc - mx)
            sw = wa + wb + wc
            o_ref[rows, :] = ((wa * ob[0, rows, :] + wb * ob[1, rows, :] + wc * ob[2, rows, :]) / sw).astype(BF16)
            l_ref[rows, :] = mx + jnp.log(sw)
            return carry

        lax.fori_loop(0, SUPER // 256, merge, 0)

    cur = pl.BlockSpec((SUPER, QK_BLOCK), lambda p, s: (s, p))
    prev = pl.BlockSpec((SUPER, QK_BLOCK), lambda p, s: (jnp.maximum(s - 1, 0), p))
    return pl.pallas_call(
        body, name="attn_fwd", grid=(4, nsb),
        in_specs=[cur, cur, prev, cur, prev, pl.BlockSpec((1, 2, QK_BLOCK), lambda p, s: (p, 0, 0))],
        out_specs=[cur, cur],
        out_shape=[jax.ShapeDtypeStruct((t, ATTN_W), BF16), jax.ShapeDtypeStruct((t, ATTN_W), F32)],
        scratch_shapes=[pltpu.VMEM((2 * SUPER, QK_BLOCK), F32), pltpu.VMEM((2 * SUPER, QK_BLOCK), F32),
                        pltpu.VMEM((3, SUPER, QK_BLOCK), F32), pltpu.VMEM((3, SUPER, QK_BLOCK), F32)],
        compiler_params=_cparams("parallel", "arbitrary"),
    )(q, k, k, v, v, slopes)


def _outproj_fwd(ya, yc, x, goc, goa, w_out, tm):
    t = x.shape[0]

    def body(ya_ref, yc_ref, x_ref, goc_ref, goa_ref, w_ref, x1_ref):
        _, ychat = _rms_stats(yc_ref[...].astype(F32))
        _, yahat = _rms_stats(ya_ref[...].astype(F32))
        acc = _mm((ychat * goc_ref[...]).astype(BF16), w_ref[0:CONV_W, :])
        acc += _mm((yahat * goa_ref[...]).astype(BF16), w_ref[CONV_W:, :])
        x1_ref[...] = x_ref[...] + acc

    def blk(c):
        return pl.BlockSpec((tm, c), lambda i: (i, 0))

    return pl.pallas_call(
        body, name="outproj_fwd", grid=(t // tm,),
        in_specs=[blk(ATTN_W), blk(CONV_W), blk(D_MODEL), _full((1, CONV_W)), _full((1, ATTN_W)),
                  _full((D_MODEL, D_MODEL))],
        out_specs=blk(D_MODEL),
        out_shape=jax.ShapeDtypeStruct((t, D_MODEL), F32),
        compiler_params=_cparams("parallel"),
    )(ya, yc, x, goc, goa, w_out)


def _ffn_fwd(x1, g_ffn, w_gate_t, w_up_t, w_down, fcw, fcb, tm):
    t = x1.shape[0]

    def body(x_ref, g_ref, wg_ref, wu_ref, wd_ref, cw_ref, cb_ref, gp_ref, up_ref, h_ref, x2_ref, carry_ref):
        @pl.when(pl.program_id(0) == 0)
        def _():
            carry_ref[...] = jnp.zeros_like(carry_ref)

        xv = x_ref[...]
        _, xhat = _rms_stats(xv)
        h = (xhat * g_ref[...]).astype(BF16)
        h_ref[...] = h
        gp = _mm_nt(h, wg_ref[...])
        gp_ref[...] = gp.astype(BF16)
        gate, _, _ = _conv_fwd(gp, carry_ref[7:8, :], carry_ref[6:7, :], cw_ref, cb_ref)
        carry_ref[...] = gp[tm - 8:tm, :]
        up = _mm_nt(h, wu_ref[...])
        up_ref[...] = up.astype(BF16)
        a = (gate * _sigmoid(gate) * up).astype(BF16)
        x2_ref[...] = xv + _mm(a, wd_ref[...])

    def blk(c):
        return pl.BlockSpec((tm, c), lambda i: (i, 0))

    return pl.pallas_call(
        body, name="ffn_fwd", grid=(t // tm,),
        in_specs=[blk(D_MODEL), _full((1, D_MODEL)), _full((D_FF, D_MODEL)), _full((D_FF, D_MODEL)),
                  _full((D_FF, D_MODEL)), _full((3, D_FF)), _full((1, D_FF))],
        out_specs=[blk(D_FF), blk(D_FF), blk(D_MODEL), blk(D_MODEL)],
        out_shape=[jax.ShapeDtypeStruct((t, D_FF), BF16), jax.ShapeDtypeStruct((t, D_FF), BF16),
                   jax.ShapeDtypeStruct((t, D_MODEL), BF16), jax.ShapeDtypeStruct((t, D_MODEL), F32)],
        scratch_shapes=[pltpu.VMEM((8, D_FF), F32)],
        compiler_params=_cparams("arbitrary"),
    )(x1, g_ffn, w_gate_t, w_up_t, w_down, fcw, fcb)


def _ple_fwd_bwd(x2, p, target, g_ple, w_pg, w_pp, tm):
    t = x2.shape[0]

    def body(x_ref, p_ref, t_ref, g_ref, wg_ref, wp_ref, dx_ref, dxb_ref, loss_ref, dwg_ref, dwp_ref, dg_ref):
        @pl.when(pl.program_id(0) == 0)
        def _():
            loss_ref[...] = jnp.zeros_like(loss_ref)
            dwg_ref[...] = jnp.zeros_like(dwg_ref)
            dwp_ref[...] = jnp.zeros_like(dwp_ref)
            dg_ref[...] = jnp.zeros_like(dg_ref)

        xv = x_ref[...]
        r, xhat = _rms_stats(xv)
        g = g_ref[...]
        h = (xhat * g).astype(BF16)
        pg = _sigmoid(_mm(h, wg_ref[...]))
        pb = p_ref[...].astype(BF16)
        pp = _mm(pb, wp_ref[...])
        err = xv + pg * pp - t_ref[...]
        loss_ref[...] += 0.5 * jnp.sum(jnp.mean(err * err, axis=-1, keepdims=True))
        dx3 = err * (1.0 / D_MODEL)
        d_pp = (dx3 * pg).astype(BF16)
        d_pre = (dx3 * pp * pg * (1.0 - pg)).astype(BF16)
        dwp_ref[...] += _mm_tn(pb, d_pp)
        dwg_ref[...] += _mm_tn(h, d_pre)
        dh = _mm_nt(d_pre, wg_ref[...])
        dg_ref[...] += jnp.sum(dh * xhat, axis=0, keepdims=True)
        dx2 = dx3 + _rms_bwd(dh, xhat, r, g)
        dx_ref[...] = dx2
        dxb_ref[...] = dx2.astype(BF16)

    def blk(c):
        return pl.BlockSpec((tm, c), lambda i: (i, 0))

    return pl.pallas_call(
        body, name="ple_fwd_bwd", grid=(t // tm,),
        in_specs=[blk(D_MODEL), blk(PLE_DIM), blk(D_MODEL), _full((1, D_MODEL)), _full((D_MODEL, D_MODEL)),
                  _full((PLE_DIM, D_MODEL))],
        out_specs=[blk(D_MODEL), blk(D_MODEL), _full((8, 128)), _full((D_MODEL, D_MODEL)),
                   _full((PLE_DIM, D_MODEL)), _full((1, D_MODEL))],
        out_shape=[jax.ShapeDtypeStruct((t, D_MODEL), F32), jax.ShapeDtypeStruct((t, D_MODEL), BF16),
                   jax.ShapeDtypeStruct((8, 128), F32),
                   jax.ShapeDtypeStruct((D_MODEL, D_MODEL), F32), jax.ShapeDtypeStruct((PLE_DIM, D_MODEL), F32),
                   jax.ShapeDtypeStruct((1, D_MODEL), F32)],
        compiler_params=_cparams("arbitrary"),
    )(x2, p, target, g_ple, w_pg, w_pp)


def _ffn_bwd(dx2, h2, gp, up, w_gate, w_up, w_down, fcw, fcb, tm):
    t = dx2.shape[0]
    nblk = t // tm
    fc = D_FF // FF_CHUNKS
    half = tm // FFN_BWD_PARTS

    def body(dx_ref, h_ref, gp_ref, gph_ref, up_ref, wg_ref, wu_ref, wd_ref, cw_ref, cb_ref,
             dh_ref, dwd_hbm, dwu_hbm, dwg_hbm, dcw_ref, dcb_ref, carry_ref, a_scr, dup_scr, dgp_scr,
             dwd_acc, dwu_acc, dwg_acc, stage, stage_sem):
        i = pl.program_id(1)

        @pl.when(i == 0)
        def _():
            carry_ref[...] = jnp.zeros_like(carry_ref)
            dwd_acc[...] = jnp.zeros_like(dwd_acc)
            dwu_acc[...] = jnp.zeros_like(dwu_acc)
            dwg_acc[...] = jnp.zeros_like(dwg_acc)
            dcw_ref[...] = jnp.zeros_like(dcw_ref)
            dcb_ref[...] = jnp.zeros_like(dcb_ref)

        keep = (i < nblk - 1).astype(F32)
        later = carry_ref[...]
        for hf in reversed(range(FFN_BWD_PARTS)):
            rows = slice(hf * half, (hf + 1) * half)
            dxb = dx_ref[rows, :]
            gp_v = gp_ref[rows, :].astype(F32)
            if hf > 0:
                before = gp_ref[hf * half - 16:hf * half, :].astype(F32)
            else:
                before = gph_ref[...].astype(F32) * keep
            gate, gp1, gp2 = _conv_fwd(gp_v, before[15:16, :], before[14:15, :], cw_ref, cb_ref)
            s = _sigmoid(gate)
            silu = gate * s
            up_v = up_ref[rows, :].astype(F32)
            da = _mm_nt(dxb, wd_ref[...])
            a_scr[rows, :] = (silu * up_v).astype(BF16)
            d_up = (da * silu).astype(BF16)
            dup_scr[rows, :] = d_up
            d_gate = da * up_v * (s * (1.0 + gate * (1.0 - s)))
            d_gp = _conv_bwd_input(d_gate, later[0:1, :], later[1:2, :], cw_ref).astype(BF16)
            dgp_scr[rows, :] = d_gp
            later = d_gate[0:8, :]
            dcw_ref[0:1, :] += jnp.sum(d_gate * gp2, axis=0, keepdims=True)
            dcw_ref[1:2, :] += jnp.sum(d_gate * gp1, axis=0, keepdims=True)
            dcw_ref[2:3, :] += jnp.sum(d_gate * gp_v, axis=0, keepdims=True)
            dcb_ref[...] += jnp.sum(d_gate, axis=0, keepdims=True)
            dh_ref[rows, :] = (_mm(d_gp, wg_ref[...]) + _mm(d_up, wu_ref[...])).astype(BF16)
        carry_ref[...] = later
        dwd_acc[...] += _mm_tn(a_scr[...], dx_ref[...])
        dwu_acc[...] += _mm_tn(h_ref[...], dup_scr[...])
        dwg_acc[...] += _mm_tn(h_ref[...], dgp_scr[...])

        @pl.when(i == nblk - 1)
        def _():
            rows = pl.ds(pl.multiple_of(pl.program_id(0) * fc, 16), fc)
            for acc, out, flip in ((dwd_acc, dwd_hbm, False), (dwu_acc, dwu_hbm, True), (dwg_acc, dwg_hbm, True)):
                stage[...] = (acc[...].T if flip else acc[...]).astype(BF16)
                copy = pltpu.make_async_copy(stage, out.at[rows, :], stage_sem)
                copy.start()
                copy.wait()

    def rev(i):
        return nblk - 1 - i

    one = pl.Buffered(1)
    in_specs = [
        pl.BlockSpec((tm, D_MODEL), lambda j, i: (rev(i), 0)),
        pl.BlockSpec((tm, D_MODEL), lambda j, i: (rev(i), 0)),
        pl.BlockSpec((tm, fc), lambda j, i: (rev(i), j)),
        pl.BlockSpec((16, fc), lambda j, i: (jnp.maximum(rev(i) * (tm // 16) - 1, 0), j)),
        pl.BlockSpec((tm, fc), lambda j, i: (rev(i), j)),
        pl.BlockSpec((fc, D_MODEL), lambda j, i: (j, 0), pipeline_mode=one),
        pl.BlockSpec((fc, D_MODEL), lambda j, i: (j, 0), pipeline_mode=one),
        pl.BlockSpec((fc, D_MODEL), lambda j, i: (j, 0), pipeline_mode=one),
        pl.BlockSpec((3, fc), lambda j, i: (0, j)),
        pl.BlockSpec((1, fc), lambda j, i: (0, j)),
    ]
    out_specs = [
        pl.BlockSpec((None, tm, D_MODEL), lambda j, i: (j, rev(i), 0)),
        ANY, ANY, ANY,
        pl.BlockSpec((3, fc), lambda j, i: (0, j)),
        pl.BlockSpec((1, fc), lambda j, i: (0, j)),
    ]
    return pl.pallas_call(
        body, name="ffn_bwd", grid=(FF_CHUNKS, nblk), in_specs=in_specs, out_specs=out_specs,
        out_shape=[jax.ShapeDtypeStruct((FF_CHUNKS, t, D_MODEL), BF16), jax.ShapeDtypeStruct((D_FF, D_MODEL), BF16),
                   jax.ShapeDtypeStruct((D_FF, D_MODEL), BF16), jax.ShapeDtypeStruct((D_FF, D_MODEL), BF16),
                   jax.ShapeDtypeStruct((3, D_FF), F32), jax.ShapeDtypeStruct((1, D_FF), F32)],
        scratch_shapes=[pltpu.VMEM((8, fc), F32), pltpu.VMEM((tm, fc), BF16), pltpu.VMEM((tm, fc), BF16),
                        pltpu.VMEM((tm, fc), BF16), pltpu.VMEM((fc, D_MODEL), F32), pltpu.VMEM((D_MODEL, fc), F32),
                        pltpu.VMEM((D_MODEL, fc), F32), pltpu.VMEM((fc, D_MODEL), BF16), pltpu.SemaphoreType.DMA],
        compiler_params=_cparams("arbitrary", "arbitrary", vmem=V7X_VMEM_LIMIT_LARGE),
    )(dx2, h2, gp, gp, up, w_gate, w_up, w_down, fcw, fcb)


def _outproj_bwd(dh2, dx2, x1, g_ffn, w_out, yc, ya, goc, goa, zconv, conv_w, conv_b, bd, tm):
    t = x1.shape[0]
    nblk = t // tm

    def body(dh_ref, dx2_ref, x1_ref, g_ref, w_ref, yc_ref, ya_ref, goc_ref, goa_ref, zc_ref, zch_ref, cw_ref, cb_ref,
             bd_ref, dx1_ref, dya_ref, dd_ref, dzc_ref, dw_ref, dg_ref, dgoc_ref, dgoa_ref, dcw_ref, dcb_ref,
             carry_ref):
        i = pl.program_id(0)

        @pl.when(i == 0)
        def _():
            carry_ref[...] = jnp.zeros_like(carry_ref)
            for ref in (dw_ref, dg_ref, dgoc_ref, dgoa_ref, dcw_ref, dcb_ref):
                ref[...] = jnp.zeros_like(ref)

        keep = (i < nblk - 1).astype(F32)
        dh2_v = dh_ref[0].astype(F32)
        for j in range(1, FF_CHUNKS):
            dh2_v = dh2_v + dh_ref[j].astype(F32)
        r, xhat = _rms_stats(x1_ref[...])
        dg_ref[...] += jnp.sum(dh2_v * xhat, axis=0, keepdims=True)
        dx1 = dx2_ref[...] + _rms_bwd(dh2_v, xhat, r, g_ref[...])
        dx1_ref[...] = dx1
        dx1b = dx1.astype(BF16)
        dy = _mm_nt(dx1b, w_ref[...])

        yc_v = yc_ref[...].astype(F32)
        rc, ychat = _rms_stats(yc_v)
        dw_ref[0:CONV_W, :] += _mm_tn((ychat * goc_ref[...]).astype(BF16), dx1b)
        dyc = dy[:, 0:CONV_W]
        dgoc_ref[...] += jnp.sum(dyc * ychat, axis=0, keepdims=True)
        d_yc = _rms_bwd(dyc, ychat, rc, goc_ref[...])

        ya_v = ya_ref[...].astype(F32)
        ra, yahat = _rms_stats(ya_v)
        dw_ref[CONV_W:, :] += _mm_tn((yahat * goa_ref[...]).astype(BF16), dx1b)
        dya = dy[:, CONV_W:]
        dgoa_ref[...] += jnp.sum(dya * yahat, axis=0, keepdims=True)
        d_ya = _rms_bwd(dya, yahat, ra, goa_ref[...])
        dya_ref[...] = d_ya
        dd_ref[...] = _seg_sum64(d_ya * ya_v, bd_ref)

        zb = zc_ref[:, 0:CONV_W].astype(F32)
        zc = zc_ref[:, CONV_W:2 * CONV_W].astype(F32)
        zx = zc_ref[:, 2 * CONV_W:3 * CONV_W].astype(F32)
        u = zc * zx
        uh = (zch_ref[:, CONV_W:2 * CONV_W].astype(F32) * zch_ref[:, 2 * CONV_W:3 * CONV_W].astype(F32)) * keep
        cv, u1, u2 = _conv_fwd(u, uh[15:16, :], uh[14:15, :], cw_ref, cb_ref)
        d_cv = d_yc * zb
        d_u = _conv_bwd_input(d_cv, carry_ref[0:1, :], carry_ref[1:2, :], cw_ref)
        carry_ref[...] = d_cv[0:8, :]
        dcw_ref[0:1, :] += jnp.sum(d_cv * u2, axis=0, keepdims=True)
        dcw_ref[1:2, :] += jnp.sum(d_cv * u1, axis=0, keepdims=True)
        dcw_ref[2:3, :] += jnp.sum(d_cv * u, axis=0, keepdims=True)
        dcb_ref[...] += jnp.sum(d_cv, axis=0, keepdims=True)
        dzc_ref[:, 0:CONV_W] = (d_yc * cv).astype(BF16)
        dzc_ref[:, CONV_W:2 * CONV_W] = (d_u * zx).astype(BF16)
        dzc_ref[:, 2 * CONV_W:3 * CONV_W] = (d_u * zc).astype(BF16)

    def rev(i):
        return nblk - 1 - i

    def blk(c):
        return pl.BlockSpec((tm, c), lambda i: (rev(i), 0))

    in_specs = [
        pl.BlockSpec((FF_CHUNKS, tm, D_MODEL), lambda i: (0, rev(i), 0)),
        blk(D_MODEL), blk(D_MODEL), _full((1, D_MODEL)), _full((D_MODEL, D_MODEL)),
        blk(CONV_W), blk(ATTN_W), _full((1, CONV_W)), _full((1, ATTN_W)),
        blk(3 * CONV_W),
        pl.BlockSpec((16, 3 * CONV_W), lambda i: (jnp.maximum(rev(i) * (tm // 16) - 1, 0), 0)),
        _full((3, CONV_W)), _full((1, CONV_W)), _full((256, 256)),
    ]
    out_specs = [blk(D_MODEL), blk(ATTN_W), blk(ATTN_W), blk(3 * CONV_W), _full((D_MODEL, D_MODEL)),
                 _full((1, D_MODEL)), _full((1, CONV_W)), _full((1, ATTN_W)), _full((3, CONV_W)), _full((1, CONV_W))]
    return pl.pallas_call(
        body, name="outproj_bwd", grid=(nblk,), in_specs=in_specs, out_specs=out_specs,
        out_shape=[jax.ShapeDtypeStruct((t, D_MODEL), F32), jax.ShapeDtypeStruct((t, ATTN_W), F32),
                   jax.ShapeDtypeStruct((t, ATTN_W), F32), jax.ShapeDtypeStruct((t, 3 * CONV_W), BF16),
                   jax.ShapeDtypeStruct((D_MODEL, D_MODEL), F32), jax.ShapeDtypeStruct((1, D_MODEL), F32),
                   jax.ShapeDtypeStruct((1, CONV_W), F32), jax.ShapeDtypeStruct((1, ATTN_W), F32),
                   jax.ShapeDtypeStruct((3, CONV_W), F32), jax.ShapeDtypeStruct((1, CONV_W), F32)],
        scratch_shapes=[pltpu.VMEM((8, CONV_W), F32)],
        compiler_params=_cparams("arbitrary"),
    )(dh2, dx2, x1, g_ffn, w_out, yc, ya, goc, goa, zconv, zconv, conv_w, conv_b, bd)


def _attn_bwd(q, k, v, dya, lse, dd, slopes):
    t = q.shape[0]
    nsb = t // SUPER

    def body(q_ref, kc_ref, kp_ref, vc_ref, vp_ref, dy_ref, l_ref, d_ref, sl_ref, dq_ref, dk_ref, dv_ref,
             kk, vv, dkacc, dvacc):
        s = pl.program_id(1)

        @pl.when(s == 0)
        def _():
            dkacc[...] = jnp.zeros_like(dkacc)
            dvacc[...] = jnp.zeros_like(dvacc)

        dkacc[0:SUPER, :] = dkacc[SUPER:, :]
        dvacc[0:SUPER, :] = dvacc[SUPER:, :]
        dkacc[SUPER:, :] = jnp.zeros((SUPER, QK_BLOCK), F32)
        dvacc[SUPER:, :] = jnp.zeros((SUPER, QK_BLOCK), F32)

        @pl.when(s < nsb)
        def _():
            kk[0:SUPER, :] = kp_ref[...]
            kk[SUPER:, :] = kc_ref[...]
            vv[0:SUPER, :] = vp_ref[...]
            vv[SUPER:, :] = vc_ref[...]
            head0 = lax.broadcasted_iota(jnp.int32, (QK_BLOCK, QK_BLOCK), 1) < HEAD_DIM

            for b, dil in enumerate(DILATIONS):
                bias, own_half = _attn_bias(sl_ref, dil)

                def unit(u, carry, b=b, dil=dil, bias=bias, own_half=own_half):
                    start = _unit_start(u, dil)
                    first_key = SUPER + start - QK_BLOCK * dil
                    qrows = _rows(start, QK_BLOCK, dil)
                    krows = _rows(first_key, KEYS, dil)
                    q2 = _stack_heads(q_ref[qrows, :].astype(BF16), head0)
                    dy2 = _stack_heads(dy_ref[qrows, :].astype(BF16), head0)
                    lv, dv_ = l_ref[qrows, :], d_ref[qrows, :]
                    l2 = jnp.concatenate([lv[:, 0:1], lv[:, HEAD_DIM:HEAD_DIM + 1]], axis=0)
                    d2 = jnp.concatenate([dv_[:, 0:1], dv_[:, HEAD_DIM:HEAD_DIM + 1]], axis=0)
                    k2 = kk[krows, :].astype(BF16)
                    v2 = vv[krows, :].astype(BF16)
                    has_prev = jnp.logical_or(s > 0, start >= QK_BLOCK * dil)
                    sc = jnp.where(jnp.logical_or(own_half, has_prev), _mm_nt(q2, k2) + bias, -jnp.inf)
                    prob = jnp.exp(sc - l2)
                    ds = (prob * (_mm_nt(dy2, v2) - d2)).astype(BF16)
                    dvacc[krows, :] += _mm_tn(prob.astype(BF16), dy2)
                    dkacc[krows, :] += _mm_tn(ds, q2)
                    dq2 = _mm(ds, k2)
                    dq = jnp.where(head0, dq2[0:QK_BLOCK], dq2[QK_BLOCK:]) * ATTN_SCALE
                    if b == 0:
                        dq_ref[qrows, :] = dq
                    else:
                        dq_ref[qrows, :] += dq
                    return carry

                lax.fori_loop(0, SUPER // QK_BLOCK, unit, 0, unroll=8)

        dk_ref[...] = dkacc[0:SUPER, :]
        dv_ref[...] = dvacc[0:SUPER, :].astype(BF16)

    def cur_map(p, s):
        return (jnp.minimum(s, nsb - 1), p)

    def prev_map(p, s):
        return (jnp.clip(s - 1, 0, nsb - 1), p)

    cur = pl.BlockSpec((SUPER, QK_BLOCK), cur_map)
    prev = pl.BlockSpec((SUPER, QK_BLOCK), prev_map)
    return pl.pallas_call(
        body, name="attn_bwd", grid=(4, nsb + 1),
        in_specs=[cur, cur, prev, cur, prev, cur, cur, cur, pl.BlockSpec((1, 2, QK_BLOCK), lambda p, s: (p, 0, 0))],
        out_specs=[cur, prev, prev],
        out_shape=[jax.ShapeDtypeStruct((t, ATTN_W), F32), jax.ShapeDtypeStruct((t, ATTN_W), F32),
                   jax.ShapeDtypeStruct((t, ATTN_W), BF16)],
        scratch_shapes=[pltpu.VMEM((2 * SUPER, QK_BLOCK), F32)] * 4,
        compiler_params=_cparams("parallel", "arbitrary"),
    )(q, k, k, v, v, dya, lse, dd, slopes)


def _attn_bwd_per_branch_unused(q, k, v, dya, lse, dd, slopes, dil):
    t = q.shape[0]
    length = t // dil
    chunk = _attn_chunk(t, dil)
    nch = length // chunk
    nb = chunk // QK_BLOCK
    nblocks = length // QK_BLOCK
    view = (length, dil * ATTN_W)
    ext = chunk + QK_BLOCK

    def body(q_ref, dy_ref, l_ref, d_ref, k_ref, v_ref, qn_ref, dyn_ref, ln_ref, dn_ref, kh_ref, vh_ref, sl_ref,
             dq_ref, dk_ref, dv_ref, qbuf, dybuf, lbuf, dbuf, kbuf, vbuf, dkacc, dvacc):
        c = pl.program_id(2)
        qbuf[0:chunk, :] = q_ref[...]
        qbuf[chunk:, :] = qn_ref[...]
        dybuf[0:chunk, :] = dy_ref[...].astype(BF16)
        dybuf[chunk:, :] = dyn_ref[...].astype(BF16)
        lbuf[0:chunk, :] = l_ref[...]
        lbuf[chunk:, :] = ln_ref[...]
        dbuf[0:chunk, :] = d_ref[...]
        dbuf[chunk:, :] = dn_ref[...]
        kbuf[0:QK_BLOCK, :] = kh_ref[...]
        kbuf[QK_BLOCK:, :] = k_ref[...]
        vbuf[0:QK_BLOCK, :] = vh_ref[...]
        vbuf[QK_BLOCK:, :] = v_ref[...]
        valid_cur, valid_prev, dist_cur, dist_prev, head0 = _attn_masks(dil)

        def pair(qb, dyb, lv, dv_, kb, vb, valid, dist):
            dq = jnp.zeros((QK_BLOCK, QK_BLOCK), F32)
            dk = jnp.zeros((QK_BLOCK, QK_BLOCK), F32)
            dvv = jnp.zeros((QK_BLOCK, QK_BLOCK), F32)
            for hh in range(2):
                sl = sl_ref[0, hh:hh + 1, :]
                hm = head0 if hh == 0 else jnp.logical_not(head0)
                col = hh * HEAD_DIM
                qm = jnp.where(hm, qb, jnp.zeros_like(qb))
                dym = jnp.where(hm, dyb, jnp.zeros_like(dyb))
                s = jnp.where(valid, _mm_nt(qm, kb) - sl * dist, -jnp.inf)
                prob = jnp.exp(s - lv[:, col:col + 1])
                ds = (prob * (_mm_nt(dym, vb) - dv_[:, col:col + 1])).astype(BF16)
                dvv += _mm_tn(prob.astype(BF16), dym)
                dk += _mm_tn(ds, qm)
                dq += jnp.where(hm, _mm(ds, kb), 0.0)
            return dq, dk, dvv

        def blk(j, carry):
            off = pl.multiple_of(j * QK_BLOCK, QK_BLOCK)
            nxt = pl.multiple_of(off + QK_BLOCK, QK_BLOCK)
            qb = qbuf[pl.ds(off, QK_BLOCK), :]
            dyb = dybuf[pl.ds(off, QK_BLOCK), :]
            lv = lbuf[pl.ds(off, QK_BLOCK), :]
            dv_ = dbuf[pl.ds(off, QK_BLOCK), :]
            dq_c, dk_c, dv_c = pair(qb, dyb, lv, dv_, kbuf[pl.ds(nxt, QK_BLOCK), :], vbuf[pl.ds(nxt, QK_BLOCK), :],
                                    valid_cur, dist_cur)
            dkacc[pl.ds(nxt, QK_BLOCK), :] = dk_c
            dvacc[pl.ds(nxt, QK_BLOCK), :] = dv_c
            has_prev = jnp.logical_or(c > 0, j > 0)
            dq_p, dk_p, dv_p = pair(qb, dyb, lv, dv_, kbuf[pl.ds(off, QK_BLOCK), :], vbuf[pl.ds(off, QK_BLOCK), :],
                                    jnp.logical_and(valid_prev, has_prev), dist_prev)

            @pl.when(j > 0)
            def _():
                dkacc[pl.ds(off, QK_BLOCK), :] += dk_p
                dvacc[pl.ds(off, QK_BLOCK), :] += dv_p

            dq_ref[pl.ds(off, QK_BLOCK), :] = (dq_c + dq_p) * ATTN_SCALE
            return carry

        lax.fori_loop(0, nb, blk, 0)

        @pl.when(c < nch - 1)
        def _():
            _, dk_p, dv_p = pair(qbuf[chunk:, :], dybuf[chunk:, :], lbuf[chunk:, :], dbuf[chunk:, :],
                                 kbuf[chunk:, :], vbuf[chunk:, :], valid_prev, dist_prev)
            dkacc[chunk:, :] += dk_p
            dvacc[chunk:, :] += dv_p

        dk_ref[...] = dkacc[QK_BLOCK:, :]
        dv_ref[...] = dvacc[QK_BLOCK:, :]

    def cmap(p, r, c):
        return (c, r * 4 + p)

    def before(p, r, c):
        return (jnp.maximum(c * nb - 1, 0), r * 4 + p)

    def after(p, r, c):
        return (jnp.minimum((c + 1) * nb, nblocks - 1), r * 4 + p)

    main = pl.BlockSpec((chunk, QK_BLOCK), cmap)
    hb = pl.BlockSpec((QK_BLOCK, QK_BLOCK), before)
    ha = pl.BlockSpec((QK_BLOCK, QK_BLOCK), after)
    qv, kv, vv = q.reshape(view), k.reshape(view), v.reshape(view)
    dyv, lv, ddv = dya.reshape(view), lse.reshape(view), dd.reshape(view)
    outs = pl.pallas_call(
        body, name=f"attn_bwd_d{dil}", grid=(4, dil, nch),
        in_specs=[main] * 6 + [ha] * 4 + [hb] * 2 + [pl.BlockSpec((1, 2, QK_BLOCK), lambda p, r, c: (p, 0, 0))],
        out_specs=[main] * 3,
        out_shape=[jax.ShapeDtypeStruct(view, F32)] * 3,
        scratch_shapes=[pltpu.VMEM((ext, QK_BLOCK), BF16), pltpu.VMEM((ext, QK_BLOCK), BF16),
                        pltpu.VMEM((ext, QK_BLOCK), F32), pltpu.VMEM((ext, QK_BLOCK), F32),
                        pltpu.VMEM((ext, QK_BLOCK), BF16), pltpu.VMEM((ext, QK_BLOCK), BF16),
                        pltpu.VMEM((ext, QK_BLOCK), F32), pltpu.VMEM((ext, QK_BLOCK), F32)],
        compiler_params=_cparams("arbitrary", "arbitrary", "arbitrary"),
    )(qv, dyv, lv, ddv, kv, vv, qv, dyv, lv, ddv, kv, vv, slopes)
    return [o.reshape(t, ATTN_W) for o in outs]


def _inproj_bwd(dq, dk, dv, dzconv, zqk, x, dx1, g_mix, w_in, qg, kg, bd, tm):
    t = x.shape[0]
    nblk = t // tm
    shard = IN_COLS // N_DEV

    def body(dq_ref, dk_ref, dv_ref, dzc_ref, zqk_ref, x_ref, dx1_ref, g_ref, w_ref, qg_ref,
             kg_ref, bd_ref, dx_ref, dw_hbm, dg_ref, dqg_ref, dkg_ref, dw_ref, stage, stage_sem):
        @pl.when(pl.program_id(0) == 0)
        def _():
            for ref in (dw_ref, dg_ref, dqg_ref, dkg_ref):
                ref[...] = jnp.zeros_like(ref)

        parts = [dzc_ref[...]]
        for j, (dn_ref, gain_ref, dgain_ref) in enumerate(((dq_ref, qg_ref, dqg_ref), (dk_ref, kg_ref, dkg_ref))):
            dn = dn_ref[...]
            z = zqk_ref[:, j * ATTN_W:(j + 1) * ATTN_W].astype(F32)
            r = lax.rsqrt(_seg_sum64(z * z, bd_ref) * (1.0 / HEAD_DIM) + EPS)
            zhat = z * r
            dgain_ref[...] += jnp.sum(dn * zhat, axis=0, keepdims=True)
            gd = dn * gain_ref[...]
            parts.append((r * (gd - zhat * (_seg_sum64(gd * zhat, bd_ref) * (1.0 / HEAD_DIM)))).astype(BF16))
        parts.append(dv_ref[...].astype(BF16))
        dz = jnp.concatenate(parts, axis=1)

        r, xhat = _rms_stats(x_ref[...])
        g = g_ref[...]
        dw_ref[...] += _mm_tn((xhat * g).astype(BF16), dz)
        dh = _mm_nt(dz, w_ref[...])
        dg_ref[...] += jnp.sum(dh * xhat, axis=0, keepdims=True)
        dx_ref[...] = dx1_ref[...] + _rms_bwd(dh, xhat, r, g)

        @pl.when(pl.program_id(0) == nblk - 1)
        def _():
            for k in range(N_DEV):
                stage[...] = dw_ref[:, k * shard:(k + 1) * shard].astype(BF16)
                copy = pltpu.make_async_copy(stage, dw_hbm.at[k], stage_sem)
                copy.start()
                copy.wait()

    def blk(c):
        return pl.BlockSpec((tm, c), lambda i: (i, 0))

    return pl.pallas_call(
        body, name="inproj_bwd", grid=(nblk,),
        in_specs=[blk(ATTN_W)] * 3 + [blk(3 * CONV_W), blk(2 * ATTN_W), blk(D_MODEL), blk(D_MODEL), _full((1, D_MODEL)),
                                      _full((D_MODEL, IN_COLS)), _full((1, ATTN_W)), _full((1, ATTN_W)),
                                      _full((256, 256))],
        out_specs=[blk(D_MODEL), ANY, _full((1, D_MODEL)), _full((1, ATTN_W)), _full((1, ATTN_W))],
        out_shape=[jax.ShapeDtypeStruct((t, D_MODEL), F32), jax.ShapeDtypeStruct((N_DEV, D_MODEL, shard), BF16),
                   jax.ShapeDtypeStruct((1, D_MODEL), F32), jax.ShapeDtypeStruct((1, ATTN_W), F32),
                   jax.ShapeDtypeStruct((1, ATTN_W), F32)],
        scratch_shapes=[pltpu.VMEM((D_MODEL, IN_COLS), F32), pltpu.VMEM((D_MODEL, shard), BF16),
                        pltpu.SemaphoreType.DMA],
        compiler_params=_cparams("arbitrary"),
    )(dq, dk, dv, dzconv, zqk, x, dx1, g_mix, w_in, qg, kg, bd)


def _ordered_after(a, token):
    return a if token is None else a + token[0:1, 0:1].reshape((1,) * a.ndim)


def _local_step(x, p, target, w, tms, hooks=None):
    hooks = hooks or {}
    bd = jnp.kron(jnp.eye(4, dtype=F32), jnp.ones((HEAD_DIM, HEAD_DIM), F32)).astype(BF16)
    qg = jnp.tile(w["q_norm_g"], (1, 8))
    kg = jnp.tile(w["k_norm_g"], (1, 8))
    slopes = jnp.exp2(-jnp.arange(1, 9, dtype=F32))
    slopes = jnp.broadcast_to(slopes.reshape(4, 2, 1), (4, 2, QK_BLOCK))

    zconv, zqk, yc, q, k, v = _inproj_fwd(x, w["g_mix"], w["w_in"], w["conv_w"], w["conv_b"], qg, kg, bd, tms[0])
    ya, lse = _attn_fwd(q, k, v, slopes)
    if "late_weights" in hooks:
        w = {**w, **hooks["late_weights"](lse)}
    x1 = _outproj_fwd(ya, yc, x, w["g_out_conv"], w["g_out_attn"], w["w_out"], tms[0])
    gp, up, h2, x2 = _ffn_fwd(x1, w["g_ffn"], w["w_gate"], w["w_up"], w["w_down"], w["ffn_conv_w"], w["ffn_conv_b"],
                              tms[1])
    dx2, dx2b, loss, dw_pg, dw_pp, dg_ple = _ple_fwd_bwd(x2, p, target, w["g_ple"], w["w_ple_gate"], w["w_ple_proj"], tms[0])
    dh2, dw_down, dw_up, dw_gate, dfcw, dfcb = _ffn_bwd(dx2b, h2, gp, up, w["w_gate"], w["w_up"], w["w_down"],
                                                        w["ffn_conv_w"], w["ffn_conv_b"], tms[0])
    token = None
    if "ffn_grads" in hooks:
        token = hooks["ffn_grads"]({"w_ple_gate": dw_pg, "w_ple_proj": dw_pp, "w_down": dw_down, "w_up": dw_up,
                                    "w_gate": dw_gate})
    dx1, dya, dd, dzconv, dw_out, dg_ffn, dgoc, dgoa, dcw, dcb = _outproj_bwd(
        dh2, dx2, x1, _ordered_after(w["g_ffn"], token), w["w_out"], yc, ya, w["g_out_conv"], w["g_out_attn"], zconv,
        w["conv_w"], w["conv_b"], bd, tms[1])
    token = hooks["outproj_done"](dx1) if "outproj_done" in hooks else None
    dq, dk, dv = _attn_bwd(q, k, v, dya, lse, dd, _ordered_after(slopes, token))
    dx, dw_in, dg_mix, dqg, dkg = _inproj_bwd(dq, dk, dv, dzconv, zqk, x, dx1, w["g_mix"], w["w_in"], qg, kg, bd,
                                              tms[0])
    grads = {
        "g_mix": dg_mix, "w_in": dw_in, "conv_w": dcw, "conv_b": dcb,
        "q_norm_g": dqg.reshape(8, HEAD_DIM).sum(0, keepdims=True),
        "k_norm_g": dkg.reshape(8, HEAD_DIM).sum(0, keepdims=True),
        "g_out_conv": dgoc, "g_out_attn": dgoa, "w_out": dw_out, "g_ffn": dg_ffn, "w_gate": dw_gate, "w_up": dw_up,
        "ffn_conv_w": dfcw, "ffn_conv_b": dfcb, "w_down": dw_down, "g_ple": dg_ple, "w_ple_gate": dw_pg,
        "w_ple_proj": dw_pp,
    }
    return loss, dx, grads


ANY = pl.BlockSpec(memory_space=pl.ANY)
MESH = pl.DeviceIdType.MESH


def _all_gather(shards, name):
    n = len(shards)

    def body(*refs):
        ins, outs = refs[:n], refs[n:2 * n]
        send_sems, recv_sems, local_sems = refs[2 * n:]
        x, y, c = lax.axis_index("x"), lax.axis_index("y"), lax.axis_index("c")
        me, sibling = (x, y, c), (x, y, 1 - c)
        chips = [(1 - x, y), (x, 1 - y), (1 - x, 1 - y)]

        def slot(dev):
            return 4 * dev[0] + 2 * dev[1] + dev[2]

        def copy(b, k, block, to, src=None):
            dst = outs[b].at[slot(block)]
            return pltpu.make_async_remote_copy(
                src_ref=dst if src is None else src, dst_ref=dst, send_sem=send_sems.at[b, k],
                recv_sem=recv_sems.at[b, k], device_id=to, device_id_type=MESH)

        mine = [pltpu.make_async_copy(ins[b], outs[b].at[slot(me)], local_sems.at[b]) for b in range(n)]
        first, passed = [], []
        for b in range(n):
            mine[b].start()
            first.append(copy(b, 0, me, sibling, src=ins[b]))
            first += [copy(b, 1 + j, me, (*chip, c), src=ins[b]) for j, chip in enumerate(chips)]
        for cp in first:
            cp.start()
        for j, chip in enumerate(chips):
            for b in range(n):
                copy(b, 1 + j, (*chip, c), me).wait_recv()
                fwd = copy(b, 4 + j, (*chip, c), sibling)
                fwd.start()
                passed.append(fwd)
        for b in range(n):
            copy(b, 0, sibling, me).wait_recv()
            for j, chip in enumerate(chips):
                copy(b, 4 + j, (*chip, 1 - c), me).wait_recv()
        for cp in first + passed:
            cp.wait_send()
        for cp in mine:
            cp.wait()

    return pl.pallas_call(
        body, name=name,
        in_specs=[ANY] * n, out_specs=[ANY] * n,
        out_shape=[jax.ShapeDtypeStruct((N_DEV,) + s.shape, s.dtype) for s in shards],
        scratch_shapes=[pltpu.SemaphoreType.DMA((n, 7)), pltpu.SemaphoreType.DMA((n, 7)),
                        pltpu.SemaphoreType.DMA((n,))],
    )(*shards)


HBM = pl.BlockSpec(memory_space=pltpu.HBM)
SEM = pl.BlockSpec(memory_space=pltpu.SEMAPHORE)
EFFECT = pltpu.SideEffectType.DATAFLOW_SIDE_EFFECTING
FLIPS = ((0, 0, 1), (0, 1, 0), (0, 1, 1), (1, 0, 0), (1, 0, 1), (1, 1, 0), (1, 1, 1))


def _flip_peers():
    pos = (lax.axis_index("x"), lax.axis_index("y"), lax.axis_index("c"))
    return [tuple(1 - a if f else a for a, f in zip(pos, flip)) for flip in FLIPS]


def _hbm(a):
    return pltpu.with_memory_space_constraint(a, pltpu.HBM)


def _split_start(name, srcs, lands, plan, n_copies, after):
    n, m = len(srcs), len(lands)

    def body(*refs):
        send_sems, recv_sems, token = refs[n + m + 1], refs[n + m + 2], refs[-1]
        for i, (src, dst, peer) in enumerate(plan(refs[:n], refs[n:n + m])):
            pltpu.make_async_remote_copy(src_ref=src, dst_ref=dst, send_sem=send_sems.at[i], recv_sem=recv_sems.at[i],
                                         device_id=peer, device_id_type=MESH).start()
        token[...] = jnp.zeros_like(token)

    outs = pl.pallas_call(
        body, name=name + "_start",
        in_specs=[HBM] * (n + m) + [ANY],
        out_specs=[SEM, SEM] + [HBM] * (n + m) + [pl.BlockSpec(memory_space=pltpu.VMEM)],
        out_shape=[pltpu.SemaphoreType.DMA((n_copies,)), pltpu.SemaphoreType.DMA((n_copies,))]
        + [pltpu.HBM(a.shape, a.dtype) for a in list(srcs) + list(lands)] + [jax.ShapeDtypeStruct((8, 128), F32)],
        input_output_aliases={i: 2 + i for i in range(n + m)},
        compiler_params=pltpu.CompilerParams(has_side_effects=EFFECT),
    )(*[_hbm(a) for a in list(srcs) + list(lands)], after)
    return (outs[0], outs[1], outs[2:2 + n], outs[2 + n:2 + n + m]), outs[-1]


def _split_wait(name, started, plan, after):
    send_sems, recv_sems, srcs, lands = started
    n, m = len(srcs), len(lands)

    def body(*refs):
        send_ref, recv_ref = refs[n + m], refs[n + m + 1]
        for i, (src, dst, peer) in enumerate(plan(refs[:n], refs[n:n + m])):
            copy = pltpu.make_async_remote_copy(src_ref=src, dst_ref=dst, send_sem=send_ref.at[i],
                                                recv_sem=recv_ref.at[i], device_id=peer, device_id_type=MESH)
            copy.wait_send()
            copy.wait_recv()

    outs = pl.pallas_call(
        body, name=name + "_wait",
        in_specs=[HBM] * (n + m) + [SEM, SEM, ANY],
        out_specs=[HBM] * (n + m),
        out_shape=[pltpu.HBM(a.shape, a.dtype) for a in list(srcs) + list(lands)],
        input_output_aliases={i: i for i in range(n + m)},
        compiler_params=pltpu.CompilerParams(has_side_effects=EFFECT),
    )(*srcs, *lands, send_sems, recv_sems, after)
    return outs[:n], outs[n:]


def _gather_plan(srcs, lands):
    slot = 4 * lax.axis_index("x") + 2 * lax.axis_index("y") + lax.axis_index("c")
    return [(src, land.at[slot], peer) for src, land in zip(srcs, lands) for peer in _flip_peers()]


def _sibling_plan(srcs, lands):
    x, y, c = lax.axis_index("x"), lax.axis_index("y"), lax.axis_index("c")
    return [(src.at[k, 1 - c], land.at[k], (x, y, 1 - c)) for src, land in zip(srcs, lands) for k in range(N_CHIP)]


def _chip_plan(srcs, lands):
    x, y, c = lax.axis_index("x"), lax.axis_index("y"), lax.axis_index("c")
    return [(src.at[2 * cx + cy], land.at[2 * x + y], (cx, cy, c))
            for src, land in zip(srcs, lands) for cx, cy in ((1 - x, y), (x, 1 - y), (1 - x, 1 - y))]


def _row_tile(rows):
    for tr in range(min(rows, 512), 15, -16):
        if rows % tr == 0:
            return tr
    return rows


def _sibling_exchange(gs):
    n = len(gs)

    def body(*refs):
        g_refs, land_refs = refs[:n], refs[n:2 * n]
        send_sems, recv_sems = refs[2 * n:]
        x, y, c = lax.axis_index("x"), lax.axis_index("y"), lax.axis_index("c")
        copies = [pltpu.make_async_remote_copy(
            src_ref=g_refs[b].at[k, 1 - c], dst_ref=land_refs[b].at[k], send_sem=send_sems.at[b, k],
            recv_sem=recv_sems.at[b, k], device_id=(x, y, 1 - c), device_id_type=MESH)
            for b in range(n) for k in range(N_CHIP)]
        for cp in copies:
            cp.start()
        for cp in copies:
            cp.wait()

    return pl.pallas_call(
        body, name="rs_sibling_exchange", in_specs=[ANY] * n, out_specs=[ANY] * n,
        out_shape=[jax.ShapeDtypeStruct((N_CHIP,) + g.shape[2:], g.dtype) for g in gs],
        scratch_shapes=[pltpu.SemaphoreType.DMA((n, N_CHIP)), pltpu.SemaphoreType.DMA((n, N_CHIP))],
    )(*gs)


def _pair_sum(g, land, core, name):
    rows, cols = land.shape[1:]
    tr = _row_tile(rows)

    def body(c_ref, g_ref, l_ref, o_ref):
        o_ref[...] = (g_ref[...].astype(F32) + l_ref[...].astype(F32)).astype(o_ref.dtype)

    return pl.pallas_call(
        body, name=f"rs_pair_sum_{name}",
        grid_spec=pltpu.PrefetchScalarGridSpec(
            num_scalar_prefetch=1, grid=(N_CHIP, rows // tr),
            in_specs=[pl.BlockSpec((None, None, tr, cols), lambda k, i, c_ref: (k, c_ref[0], i, 0)),
                      pl.BlockSpec((None, tr, cols), lambda k, i, c_ref: (k, i, 0))],
            out_specs=pl.BlockSpec((None, tr, cols), lambda k, i, c_ref: (k, i, 0))),
        out_shape=jax.ShapeDtypeStruct(land.shape, land.dtype),
        compiler_params=_cparams("parallel", "parallel"),
    )(core, g, land)


def _chip_exchange(parts):
    n = len(parts)

    def body(*refs):
        p_refs, land_refs = refs[:n], refs[n:2 * n]
        send_sems, recv_sems, local_sems = refs[2 * n:]
        x, y, c = lax.axis_index("x"), lax.axis_index("y"), lax.axis_index("c")
        mine = 2 * x + y
        chips = [(1 - x, y), (x, 1 - y), (1 - x, 1 - y)]
        own = [pltpu.make_async_copy(p_refs[b].at[mine], land_refs[b].at[mine], local_sems.at[b]) for b in range(n)]
        for cp in own:
            cp.start()
        copies = [pltpu.make_async_remote_copy(
            src_ref=p_refs[b].at[2 * cx + cy], dst_ref=land_refs[b].at[mine], send_sem=send_sems.at[b, j],
            recv_sem=recv_sems.at[b, j], device_id=(cx, cy, c), device_id_type=MESH)
            for b in range(n) for j, (cx, cy) in enumerate(chips)]
        for cp in copies:
            cp.start()
        for b in range(n):
            for j, (cx, cy) in enumerate(chips):
                pltpu.make_async_remote_copy(
                    src_ref=p_refs[b].at[mine], dst_ref=land_refs[b].at[2 * cx + cy], send_sem=send_sems.at[b, j],
                    recv_sem=recv_sems.at[b, j], device_id=(cx, cy, c), device_id_type=MESH).wait_recv()
        for cp in copies:
            cp.wait_send()
        for cp in own:
            cp.wait()

    return pl.pallas_call(
        body, name="rs_chip_exchange", in_specs=[ANY] * n, out_specs=[ANY] * n,
        out_shape=[jax.ShapeDtypeStruct(p.shape, p.dtype) for p in parts],
        scratch_shapes=[pltpu.SemaphoreType.DMA((n, 3)), pltpu.SemaphoreType.DMA((n, 3)),
                        pltpu.SemaphoreType.DMA((n,))],
    )(*parts)


def _adamw(own, arrived, chip, w, m, v, name):
    k, rows, cols = arrived.shape
    tr = _row_tile(rows)
    c1 = 1.0 / (1.0 - ADAM_B1 ** ADAM_STEP)
    c2 = 1.0 / (1.0 - ADAM_B2 ** ADAM_STEP)

    def body(chip_ref, o_ref, p_ref, w_ref, m_ref, v_ref, g_ref, d_ref, nm_ref, nv_ref):
        def slab(j):
            return jnp.where(chip_ref[0] == j, o_ref[j], p_ref[j]).astype(F32)

        g = slab(0)
        for j in range(1, k):
            g = g + slab(j)
        g_ref[...] = g
        nm = ADAM_B1 * m_ref[...] + (1.0 - ADAM_B1) * g
        nv = ADAM_B2 * v_ref[...] + (1.0 - ADAM_B2) * (g * g)
        nm_ref[...] = nm
        nv_ref[...] = nv
        d_ref[...] = -ADAM_LR * ((nm * c1) / (jnp.sqrt(nv * c2) + ADAM_EPS) + ADAM_WD * w_ref[...])

    blk = pl.BlockSpec((tr, cols), lambda i, c: (i, 0))
    stack = pl.BlockSpec((k, tr, cols), lambda i, c: (0, i, 0))
    return pl.pallas_call(
        body, name=name,
        grid_spec=pltpu.PrefetchScalarGridSpec(num_scalar_prefetch=1, grid=(rows // tr,),
                                               in_specs=[stack, stack, blk, blk, blk], out_specs=[blk] * 4),
        out_shape=[jax.ShapeDtypeStruct((rows, cols), F32)] * 4,
        compiler_params=_cparams("parallel"),
    )(chip, own, arrived, w, m, v)


SMALL_LAYOUT = (("g_mix", 0, 1024), ("conv_b", 1, 512), ("q_norm_g", 2, 64), ("k_norm_g", 3, 64),
                ("g_out_conv", 4, 512), ("g_out_attn", 5, 512), ("g_ffn", 6, 1024), ("ffn_conv_b", 7, 2816),
                ("g_ple", 10, 1024))
CONV_W_ROW = 11
FFN_CONV_W_ROW = 14
LOSS_ROW = 23


def _row_pieces(cols):
    return [(c, min(1024, cols - c)) for c in range(0, cols, 1024)]


def _pack_small(grads, loss_tile):
    names = [n for n, _, _ in SMALL_LAYOUT]

    def body(*refs):
        ins, cw_ref, fcw_ref, loss_ref, out_ref = refs[:len(names)], refs[-4], refs[-3], refs[-2], refs[-1]
        out_ref[...] = jnp.zeros_like(out_ref)
        for ref, (_, row, cols) in zip(ins, SMALL_LAYOUT):
            for j, (c, width) in enumerate(_row_pieces(cols)):
                out_ref[row + j:row + j + 1, 0:width] = ref[:, c:c + width]
        for k in range(3):
            out_ref[CONV_W_ROW + k:CONV_W_ROW + k + 1, 0:CONV_W] = cw_ref[k:k + 1, :]
            for j, (c, width) in enumerate(_row_pieces(D_FF)):
                row = FFN_CONV_W_ROW + 3 * k + j
                out_ref[row:row + 1, 0:width] = fcw_ref[k:k + 1, c:c + width]
        out_ref[LOSS_ROW:LOSS_ROW + 1, 0:128] = loss_ref[0:1, :]

    return pl.pallas_call(
        body, name="pack_small_grads", out_shape=jax.ShapeDtypeStruct((SMALL_ROWS, 1024), F32),
    )(*[grads[n] for n in names], grads["conv_w"], grads["ffn_conv_w"], loss_tile)


def _adamw_small(arrived, conv_parts, fconv_parts, wts, mom, var):
    names = [n for n, _, _ in SMALL_LAYOUT] + ["conv_w", "ffn_conv_w"]
    c1 = 1.0 / (1.0 - ADAM_B1 ** ADAM_STEP)
    c2 = 1.0 / (1.0 - ADAM_B2 ** ADAM_STEP)
    n = len(names)

    def body(*refs):
        land, cw_ref, fcw_ref = refs[0], refs[1], refs[2]
        state = refs[3:3 + 3 * n]
        outs = refs[3 + 3 * n:]

        def total(piece):
            acc = piece(0)
            for d in range(1, N_DEV):
                acc = acc + piece(d)
            return acc

        for i, name in enumerate(names):
            if name == "conv_w":
                g = total(lambda d: cw_ref[d])
            elif name == "ffn_conv_w":
                g = total(lambda d: fcw_ref[d])
            else:
                _, row, cols = SMALL_LAYOUT[i]
                pieces = [total(lambda d, j=j, width=width: land[d, row + j:row + j + 1, 0:width])
                          for j, (_, width) in enumerate(_row_pieces(cols))]
                g = pieces[0] if len(pieces) == 1 else jnp.concatenate(pieces, axis=1)
            w_ref, m_ref, v_ref = state[3 * i:3 * i + 3]
            nm = ADAM_B1 * m_ref[...] + (1.0 - ADAM_B1) * g
            nv = ADAM_B2 * v_ref[...] + (1.0 - ADAM_B2) * (g * g)
            outs[4 * i][...] = g
            outs[4 * i + 1][...] = -ADAM_LR * ((nm * c1) / (jnp.sqrt(nv * c2) + ADAM_EPS) + ADAM_WD * w_ref[...])
            outs[4 * i + 2][...] = nm
            outs[4 * i + 3][...] = nv
        outs[-1][...] = total(lambda d: land[d, LOSS_ROW:LOSS_ROW + 1, 0:128])

    state = [a[nm_] for nm_ in names for a in (wts, mom, var)]
    shapes = [jax.ShapeDtypeStruct(wts[nm_].shape, F32) for nm_ in names for _ in range(4)]
    outs = pl.pallas_call(
        body, name="adamw_small", out_shape=shapes + [jax.ShapeDtypeStruct((1, 128), F32)],
    )(arrived, conv_parts, fconv_parts, *state)
    return {nm_: tuple(outs[4 * i:4 * i + 4]) for i, nm_ in enumerate(names)}, outs[-1][0, 0]


COL_SHARDED = ("w_in", "w_ple_proj")
TRANSPOSED = ("w_gate", "w_up")
REPLICATED = (("g_mix", 1024), ("conv_b", 512), ("q_norm_g", 64), ("k_norm_g", 64), ("g_out_conv", 512),
              ("g_out_attn", 512), ("g_ffn", 1024), ("ffn_conv_b", 2816), ("g_ple", 1024))
CONV_SHARDED = (("conv_w", CONV_W), ("ffn_conv_w", D_FF))


def _gathered_to_full(name, gathered):
    if name in COL_SHARDED:
        return gathered.transpose(1, 0, 2).reshape(gathered.shape[1], -1)
    return gathered.reshape(-1, gathered.shape[2])


def _full_to_stacked(name, grad, shard_shape):
    sr, sc = shard_shape
    if grad.ndim == 3:
        a = grad
    elif name in COL_SHARDED:
        a = grad.reshape(sr, N_DEV, sc).transpose(1, 0, 2)
    else:
        a = grad.reshape(N_DEV, sr, sc)
    return a.astype(BF16).reshape(N_CHIP, 2, sr, sc)


def _pad_rows(vec, rows):
    return jnp.pad(vec, (0, rows * 1024 - vec.shape[0])).reshape(rows, 1024)


def kernel(x, p, g_mix, w_in, conv_w, conv_b, q_norm_g, k_norm_g, g_out_conv, g_out_attn, w_out, g_ffn, w_gate, w_up, ffn_conv_w, ffn_conv_b, w_down, g_ple, w_ple_gate, w_ple_proj, loss_target, m_g_mix, m_w_in, m_conv_w, m_conv_b, m_q_norm_g, m_k_norm_g, m_g_out_conv, m_g_out_attn, m_w_out, m_g_ffn, m_w_gate, m_w_up, m_ffn_conv_w, m_ffn_conv_b, m_w_down, m_g_ple, m_w_ple_gate, m_w_ple_proj, v_g_mix, v_w_in, v_conv_w, v_conv_b, v_q_norm_g, v_k_norm_g, v_g_out_conv, v_g_out_attn, v_w_out, v_g_ffn, v_w_gate, v_w_up, v_ffn_conv_w, v_ffn_conv_b, v_w_down, v_g_ple, v_w_ple_gate, v_w_ple_proj):
    args = dict(locals())
    names = ["g_mix", "w_in", "conv_w", "conv_b", "q_norm_g", "k_norm_g", "g_out_conv", "g_out_attn", "w_out", "g_ffn",
             "w_gate", "w_up", "ffn_conv_w", "ffn_conv_b", "w_down", "g_ple", "w_ple_gate", "w_ple_proj"]
    big = [n for n, _ in BIG_ROWS]
    conv = [n for n, _ in CONV_SHARDED]
    def local(prefix):
        out = {n: (args[prefix + n][0] if n in big or n in conv else args[prefix + n]) for n in names}
        out.update({n: out[n].T for n in TRANSPOSED})
        return out

    wts, mom, var = local(""), local("m_"), local("v_")
    shard_shapes = {n: wts[n].shape for n in big}
    dev = 4 * lax.axis_index("x") + 2 * lax.axis_index("y") + lax.axis_index("c")
    core = lax.axis_index("c").astype(jnp.int32).reshape(1)

    conv_local = _pad_rows(jnp.concatenate([wts[n].reshape(-1) for n in conv]), 8).reshape(8, 1024)
    late = [n for n in big if n != "w_in"]
    w_in_all, conv_all = _all_gather([wts["w_in"].astype(BF16), conv_local], "gather_weights")
    late_shards = [wts[n].astype(BF16) for n in late]
    gathering, token = _split_start("gather_late_weights", late_shards,
                                    [lax.empty((N_DEV,) + s.shape, BF16) for s in late_shards], _gather_plan,
                                    7 * len(late), w_in_all)
    full = dict(wts)
    full["w_in"] = _gathered_to_full("w_in", w_in_all)
    full["g_mix"] = _ordered_after(wts["g_mix"], token)
    flying = {}

    def late_weights(after):
        shards, lands = _split_wait("gather_late_weights", gathering, _gather_plan, after)
        return {n: _gathered_to_full(n, lax.dynamic_update_slice(land, shard[None], (dev, 0, 0)))
                for n, land, shard in zip(late, lands, shards)}

    early = ["w_ple_gate", "w_ple_proj", "w_down", "w_up", "w_gate"]

    def ffn_grads(g):
        stacked = [_full_to_stacked(n, g[n], shard_shapes[n]) for n in early]
        flying["sibling"], tok = _split_start("rs_sibling_early", stacked,
                                              [lax.empty((N_CHIP,) + s.shape[2:], BF16) for s in stacked],
                                              _sibling_plan, N_CHIP * len(early), g["w_down"])
        return tok

    def outproj_done(after):
        stacked, landed = _split_wait("rs_sibling_early", flying["sibling"], _sibling_plan, after)
        parts = [_pair_sum(g, l, core, n) for n, g, l in zip(early, stacked, landed)]
        flying["chip"], tok = _split_start("rs_chip_early", parts, [lax.empty(q.shape, BF16) for q in parts],
                                           _chip_plan, 3 * len(early), landed[0])
        return tok

    off = 0
    for n, width in CONV_SHARDED:
        sc = width // N_DEV
        a = conv_all.reshape(N_DEV, -1)[:, off:off + 3 * sc].reshape(N_DEV, 3, sc)
        full[n] = a.transpose(1, 0, 2).reshape(3, width)
        off += 3 * sc

    loss, dx, grads = _local_step(x[0], p[0, 0], loss_target[0], full, (512, 256),
                                  {"late_weights": late_weights, "ffn_grads": ffn_grads, "outproj_done": outproj_done})

    chip = (2 * lax.axis_index("x") + lax.axis_index("y")).astype(jnp.int32).reshape(1)

    def adamw_of(group, parts, arrived):
        return {n: _adamw(own, got, chip, wts[n], mom[n], var[n], f"adamw_{n}")
                for n, own, got in zip(group, parts, arrived)}

    last = [n for n in big if n not in early]
    stacked = [_full_to_stacked(n, grads[n], shard_shapes[n]) for n in last]
    flying["sibling_last"], tok = _split_start("rs_sibling_last", stacked,
                                               [lax.empty((N_CHIP,) + s.shape[2:], BF16) for s in stacked],
                                               _sibling_plan, N_CHIP * len(last), dx)
    (small_all,) = _all_gather([_ordered_after(_pack_small(grads, loss), tok)], "gather_small_grads")
    stacked, landed = _split_wait("rs_sibling_last", flying["sibling_last"], _sibling_plan, small_all)
    parts = [_pair_sum(g, l, core, n) for n, g, l in zip(last, stacked, landed)]
    flying["chip_last"], tok = _split_start("rs_chip_last", parts, [lax.empty(q.shape, BF16) for q in parts],
                                            _chip_plan, 3 * len(last), landed[0])

    parts, arrived = _split_wait("rs_chip_early", flying["chip"], _chip_plan, tok)
    out = adamw_of(early, parts, arrived)
    small_all = _ordered_after(small_all, tok)
    taps = small_all[:, CONV_W_ROW:CONV_W_ROW + 3, 0:CONV_W]
    ftaps = small_all[:, FFN_CONV_W_ROW:FFN_CONV_W_ROW + 9, :].reshape(N_DEV, 3, 3 * 1024)
    small_out, loss_total = _adamw_small(
        small_all, lax.dynamic_slice(taps, (0, 0, dev * (CONV_W // N_DEV)), (N_DEV, 3, CONV_W // N_DEV)),
        lax.dynamic_slice(ftaps, (0, 0, dev * (D_FF // N_DEV)), (N_DEV, 3, D_FF // N_DEV)), wts, mom, var)
    out.update(small_out)
    parts, arrived = _split_wait("rs_chip_last", flying["chip_last"], _chip_plan, small_out["g_mix"][0])
    out.update(adamw_of(last, parts, arrived))
    def result(n, which):
        a = out[n][which]
        return (a.T if n in TRANSPOSED else a).reshape(args[n].shape)

    return (loss_total, dx[None], *[result(n, which) for which in range(4) for n in names])
```

```python
import jax
import jax.numpy as jnp
from jax import lax
from jax.experimental import pallas as pl
from jax.experimental.pallas import tpu as pltpu

F32 = jnp.float32
BF16 = jnp.bfloat16

D_MODEL = 1024
CONV_W = 512
ATTN_W = 512
HEAD_DIM = 64
D_FF = 2816
PLE_DIM = 256
IN_COLS = 3 * CONV_W + 3 * ATTN_W
EPS = 1e-6
QK_BLOCK = 128
DILATIONS = (1, 4, 16)
ATTN_SCALE = HEAD_DIM ** -0.5

ADAM_LR = 0.001
ADAM_B1 = 0.9
ADAM_B2 = 0.999
ADAM_EPS = 1e-08
ADAM_WD = 0.01
ADAM_STEP = 10

N_DEV = 8
N_CHIP = 4
V7X_VMEM_LIMIT = 56 * 1024 * 1024
V7X_VMEM_LIMIT_LARGE = 62 * 1024 * 1024
FF_CHUNKS = 2
FFN_BWD_PARTS = 1

BIG = ("w_in", "w_out", "w_gate", "w_up", "w_down", "w_ple_gate", "w_ple_proj")
SMALL_ROWS = 24


def _cparams(*sem, vmem=V7X_VMEM_LIMIT):
    return pltpu.CompilerParams(dimension_semantics=sem, vmem_limit_bytes=vmem)


def _mm(a, b):
    return jnp.dot(a, b, preferred_element_type=F32)


def _mm_nt(a, b):
    return lax.dot_general(a, b, (((1,), (1,)), ((), ())), preferred_element_type=F32)


def _mm_tn(a, b):
    return lax.dot_general(a, b, (((0,), (0,)), ((), ())), preferred_element_type=F32)


def _full(shape):
    nd = len(shape)
    return pl.BlockSpec(shape, lambda *_: (0,) * nd)


def _rms_stats(x):
    r = lax.rsqrt(jnp.mean(x * x, axis=-1, keepdims=True) + EPS)
    return r, x * r


def _rms_bwd(dy, xhat, r, g):
    gd = dy * g
    return r * (gd - xhat * jnp.mean(gd * xhat, axis=-1, keepdims=True))


def _seg_sum64(v, bd_ref):
    outs = []
    for c in range(0, v.shape[1], 256):
        vc = v[:, c:c + 256]
        hi = vc.astype(BF16)
        lo = (vc - hi.astype(F32)).astype(BF16)
        outs.append(_mm(hi, bd_ref[...]) + _mm(lo, bd_ref[...]))
    return outs[0] if len(outs) == 1 else jnp.concatenate(outs, axis=1)


def _shift_rows(u, k, edge_rows):
    out = pltpu.roll(u, k, 0)
    row = lax.broadcasted_iota(jnp.int32, (8, u.shape[1]), 0)
    head = out[0:8]
    for j in range(k):
        head = jnp.where(row == j, edge_rows[k - 1 - j], head)
    return jnp.concatenate([head, out[8:]], axis=0)


def _shift_rows_up(u, k, edge_rows):
    n = u.shape[0]
    out = pltpu.roll(u, n - k, 0)
    row = lax.broadcasted_iota(jnp.int32, (8, u.shape[1]), 0)
    tail = out[n - 8:n]
    for j in range(k):
        tail = jnp.where(row == 8 - k + j, edge_rows[j], tail)
    return jnp.concatenate([out[0:n - 8], tail], axis=0)


def _conv_fwd(u, c1, c2, w_ref, b_ref):
    u1 = _shift_rows(u, 1, (c1,))
    u2 = _shift_rows(u, 2, (c1, c2))
    y = u2 * w_ref[0:1, :] + u1 * w_ref[1:2, :] + u * w_ref[2:3, :] + b_ref[...]
    return y, u1, u2


def _conv_bwd_input(dy, n1row, n2row, w_ref):
    d1 = _shift_rows_up(dy, 1, (n1row,))
    d2 = _shift_rows_up(dy, 2, (n1row, n2row))
    return dy * w_ref[2:3, :] + d1 * w_ref[1:2, :] + d2 * w_ref[0:1, :]


def _sigmoid(x):
    return 1.0 / (1.0 + jnp.exp(-x))


def _inproj_fwd(x, g_mix, w_in, conv_w, conv_b, qg, kg, bd, tm):
    t = x.shape[0]

    def body(x_ref, g_ref, w_ref, cw_ref, cb_ref, qg_ref, kg_ref, bd_ref,
             zc_ref, zqk_ref, yc_ref, q_ref, k_ref, v_ref, carry_ref):
        @pl.when(pl.program_id(0) == 0)
        def _():
            carry_ref[...] = jnp.zeros_like(carry_ref)

        _, xhat = _rms_stats(x_ref[...])
        h = (xhat * g_ref[...]).astype(BF16)
        zconv = _mm(h, w_ref[:, 0:3 * CONV_W])
        zc_ref[...] = zconv.astype(BF16)
        u = zconv[:, CONV_W:2 * CONV_W] * zconv[:, 2 * CONV_W:3 * CONV_W]
        cv, _, _ = _conv_fwd(u, carry_ref[7:8, :], carry_ref[6:7, :], cw_ref, cb_ref)
        yc_ref[...] = (zconv[:, 0:CONV_W] * cv).astype(BF16)
        carry_ref[...] = u[tm - 8:tm, :]

        zqk = _mm(h, w_ref[:, 3 * CONV_W:3 * CONV_W + 2 * ATTN_W])
        zqk_ref[...] = zqk.astype(BF16)
        for j, (gain_ref, out_ref, scale) in enumerate(((qg_ref, q_ref, ATTN_SCALE), (kg_ref, k_ref, 1.0))):
            z = zqk[:, j * ATTN_W:(j + 1) * ATTN_W]
            r = lax.rsqrt(_seg_sum64(z * z, bd_ref) * (1.0 / HEAD_DIM) + EPS)
            out_ref[...] = z * r * gain_ref[...] * scale
        v_ref[...] = _mm(h, w_ref[:, 3 * CONV_W + 2 * ATTN_W:IN_COLS])

    def blk(c):
        return pl.BlockSpec((tm, c), lambda i: (i, 0))

    return pl.pallas_call(
        body, name="inproj_fwd", grid=(t // tm,),
        in_specs=[blk(D_MODEL), _full((1, D_MODEL)), _full((D_MODEL, IN_COLS)), _full((3, CONV_W)),
                  _full((1, CONV_W)), _full((1, ATTN_W)), _full((1, ATTN_W)), _full((256, 256))],
        out_specs=[blk(3 * CONV_W), blk(2 * ATTN_W), blk(CONV_W), blk(ATTN_W), blk(ATTN_W), blk(ATTN_W)],
        out_shape=[jax.ShapeDtypeStruct((t, 3 * CONV_W), BF16), jax.ShapeDtypeStruct((t, 2 * ATTN_W), BF16),
                   jax.ShapeDtypeStruct((t, CONV_W), BF16), jax.ShapeDtypeStruct((t, ATTN_W), F32),
                   jax.ShapeDtypeStruct((t, ATTN_W), F32), jax.ShapeDtypeStruct((t, ATTN_W), F32)],
        scratch_shapes=[pltpu.VMEM((8, CONV_W), F32)],
        compiler_params=_cparams("arbitrary"),
    )(x, g_mix, w_in, conv_w, conv_b, qg, kg, bd)


SUPER = 16 * QK_BLOCK
KEYS = 2 * QK_BLOCK


def _rows(start, size, dil):
    return pl.ds(start, size) if dil == 1 else pl.ds(start, size, stride=dil)


def _attn_bias(sl_ref, dil):
    qi = lax.broadcasted_iota(jnp.int32, (KEYS, KEYS), 0)
    kj = lax.broadcasted_iota(jnp.int32, (KEYS, KEYS), 1)
    step = jnp.bitwise_and(qi, QK_BLOCK - 1) + QK_BLOCK - kj
    slope = jnp.where(qi < QK_BLOCK, sl_ref[0, 0:1, 0:1], sl_ref[0, 1:2, 0:1])
    bias = jnp.where(jnp.logical_and(step >= 0, step <= QK_BLOCK), -slope * (step * dil).astype(F32), -jnp.inf)
    return bias, kj >= QK_BLOCK


def _unit_start(u, dil):
    if dil == 1:
        return pl.multiple_of(u * QK_BLOCK, QK_BLOCK)
    if dil == 4:
        return jnp.bitwise_and(u, 3) + (u // 4) * (4 * QK_BLOCK)
    return u


def _stack_heads(a, head0):
    zero = jnp.zeros_like(a)
    return jnp.concatenate([jnp.where(head0, a, zero), jnp.where(head0, zero, a)], axis=0)


def _attn_fwd(q, k, v, slopes):
    t = q.shape[0]
    nsb = t // SUPER

    def body(q_ref, kc_ref, kp_ref, vc_ref, vp_ref, sl_ref, o_ref, l_ref, kk, vv, ob, lb):
        s = pl.program_id(1)
        kk[0:SUPER, :] = kp_ref[...]
        kk[SUPER:, :] = kc_ref[...]
        vv[0:SUPER, :] = vp_ref[...]
        vv[SUPER:, :] = vc_ref[...]
        head0 = lax.broadcasted_iota(jnp.int32, (QK_BLOCK, QK_BLOCK), 1) < HEAD_DIM

        for b, dil in enumerate(DILATIONS):
            bias, own_half = _attn_bias(sl_ref, dil)

            def unit(u, carry, b=b, dil=dil, bias=bias, own_half=own_half):
                start = _unit_start(u, dil)
                first_key = SUPER + start - QK_BLOCK * dil
                q2 = _stack_heads(q_ref[_rows(start, QK_BLOCK, dil), :].astype(BF16), head0)
                k2 = kk[_rows(first_key, KEYS, dil), :].astype(BF16)
                v2 = vv[_rows(first_key, KEYS, dil), :].astype(BF16)
                has_prev = jnp.logical_or(s > 0, start >= QK_BLOCK * dil)
                sc = jnp.where(jnp.logical_or(own_half, has_prev), _mm_nt(q2, k2) + bias, -jnp.inf)
                m = jnp.max(sc, axis=-1, keepdims=True)
                e = jnp.exp(sc - m)
                den = jnp.sum(e, axis=-1, keepdims=True)
                o2 = _mm(e.astype(BF16), v2) / den
                l2 = m + jnp.log(den)
                ob[b, _rows(start, QK_BLOCK, dil), :] = jnp.where(head0, o2[0:QK_BLOCK], o2[QK_BLOCK:])
                lb[b, _rows(start, QK_BLOCK, dil), :] = jnp.where(head0, l2[0:QK_BLOCK], l2[QK_BLOCK:])
                return carry

            lax.fori_loop(0, SUPER // QK_BLOCK, unit, 0, unroll=16)

        def merge(i, carry):
            rows = pl.ds(pl.multiple_of(i * 256, 256), 256)
            la, lb_, lc = lb[0, rows, :], lb[1, rows, :], lb[2, rows, :]
            mx = jnp.maximum(jnp.maximum(la, lb_), lc)
            wa, wb, wc = jnp.exp(la - mx), jnp.exp(lb_ - mx), jnp.exp(lc - mx)
            sw = wa + wb + wc
            o_ref[rows, :] = ((wa * ob[0, rows, :] + wb * ob[1, rows, :] + wc * ob[2, rows, :]) / sw).astype(BF16)
            l_ref[rows, :] = mx + jnp.log(sw)
            return carry

        lax.fori_loop(0, SUPER // 256, merge, 0)

    cur = pl.BlockSpec((SUPER, QK_BLOCK), lambda p, s: (s, p))
    prev = pl.BlockSpec((SUPER, QK_BLOCK), lambda p, s: (jnp.maximum(s - 1, 0), p))
    return pl.pallas_call(
        body, name="attn_fwd", grid=(4, nsb),
        in_specs=[cur, cur, prev, cur, prev, pl.BlockSpec((1, 2, QK_BLOCK), lambda p, s: (p, 0, 0))],
        out_specs=[cur, cur],
        out_shape=[jax.ShapeDtypeStruct((t, ATTN_W), BF16), jax.ShapeDtypeStruct((t, ATTN_W), F32)],
        scratch_shapes=[pltpu.VMEM((2 * SUPER, QK_BLOCK), F32), pltpu.VMEM((2 * SUPER, QK_BLOCK), F32),
                        pltpu.VMEM((3, SUPER, QK_BLOCK), F32), pltpu.VMEM((3, SUPER, QK_BLOCK), F32)],
        compiler_params=_cparams("parallel", "arbitrary"),
    )(q, k, k, v, v, slopes)


def _outproj_fwd(ya, yc, x, goc, goa, w_out, tm):
    t = x.shape[0]

    def body(ya_ref, yc_ref, x_ref, goc_ref, goa_ref, w_ref, x1_ref):
        _, ychat = _rms_stats(yc_ref[...].astype(F32))
        _, yahat = _rms_stats(ya_ref[...].astype(F32))
        acc = _mm((ychat * goc_ref[...]).astype(BF16), w_ref[0:CONV_W, :])
        acc += _mm((yahat * goa_ref[...]).astype(BF16), w_ref[CONV_W:, :])
        x1_ref[...] = x_ref[...] + acc

    def blk(c):
        return pl.BlockSpec((tm, c), lambda i: (i, 0))

    return pl.pallas_call(
        body, name="outproj_fwd", grid=(t // tm,),
        in_specs=[blk(ATTN_W), blk(CONV_W), blk(D_MODEL), _full((1, CONV_W)), _full((1, ATTN_W)),
                  _full((D_MODEL, D_MODEL))],
        out_specs=blk(D_MODEL),
        out_shape=jax.ShapeDtypeStruct((t, D_MODEL), F32),
        compiler_params=_cparams("parallel"),
    )(ya, yc, x, goc, goa, w_out)


def _ffn_fwd(x1, g_ffn, w_gate_t, w_up_t, w_down, fcw, fcb, tm):
    t = x1.shape[0]

    def body(x_ref, g_ref, wg_ref, wu_ref, wd_ref, cw_ref, cb_ref, gp_ref, up_ref, h_ref, x2_ref, carry_ref):
        @pl.when(pl.program_id(0) == 0)
        def _():
            carry_ref[...] = jnp.zeros_like(carry_ref)

        xv = x_ref[...]
        _, xhat = _rms_stats(xv)
        h = (xhat * g_ref[...]).astype(BF16)
        h_ref[...] = h
        gp = _mm_nt(h, wg_ref[...])
        gp_ref[...] = gp.astype(BF16)
        gate, _, _ = _conv_fwd(gp, carry_ref[7:8, :], carry_ref[6:7, :], cw_ref, cb_ref)
        carry_ref[...] = gp[tm - 8:tm, :]
        up = _mm_nt(h, wu_ref[...])
        up_ref[...] = up.astype(BF16)
        a = (gate * _sigmoid(gate) * up).astype(BF16)
        x2_ref[...] = xv + _mm(a, wd_ref[...])

    def blk(c):
        return pl.BlockSpec((tm, c), lambda i: (i, 0))

    return pl.pallas_call(
        body, name="ffn_fwd", grid=(t // tm,),
        in_specs=[blk(D_MODEL), _full((1, D_MODEL)), _full((D_FF, D_MODEL)), _full((D_FF, D_MODEL)),
                  _full((D_FF, D_MODEL)), _full((3, D_FF)), _full((1, D_FF))],
        out_specs=[blk(D_FF), blk(D_FF), blk(D_MODEL), blk(D_MODEL)],
        out_shape=[jax.ShapeDtypeStruct((t, D_FF), BF16), jax.ShapeDtypeStruct((t, D_FF), BF16),
                   jax.ShapeDtypeStruct((t, D_MODEL), BF16), jax.ShapeDtypeStruct((t, D_MODEL), F32)],
        scratch_shapes=[pltpu.VMEM((8, D_FF), F32)],
        compiler_params=_cparams("arbitrary"),
    )(x1, g_ffn, w_gate_t, w_up_t, w_down, fcw, fcb)


def _ple_fwd_bwd(x2, p, target, g_ple, w_pg, w_pp, tm):
    t = x2.shape[0]

    def body(x_ref, p_ref, t_ref, g_ref, wg_ref, wp_ref, dx_ref, dxb_ref, loss_ref, dwg_ref, dwp_ref, dg_ref):
        @pl.when(pl.program_id(0) == 0)
        def _():
            loss_ref[...] = jnp.zeros_like(loss_ref)
            dwg_ref[...] = jnp.zeros_like(dwg_ref)
            dwp_ref[...] = jnp.zeros_like(dwp_ref)
            dg_ref[...] = jnp.zeros_like(dg_ref)

        xv = x_ref[...]
        r, xhat = _rms_stats(xv)
        g = g_ref[...]
        h = (xhat * g).astype(BF16)
        pg = _sigmoid(_mm(h, wg_ref[...]))
        pb = p_ref[...].astype(BF16)
        pp = _mm(pb, wp_ref[...])
        err = xv + pg * pp - t_ref[...]
        loss_ref[...] += 0.5 * jnp.sum(jnp.mean(err * err, axis=-1, keepdims=True))
        dx3 = err * (1.0 / D_MODEL)
        d_pp = (dx3 * pg).astype(BF16)
        d_pre = (dx3 * pp * pg * (1.0 - pg)).astype(BF16)
        dwp_ref[...] += _mm_tn(pb, d_pp)
        dwg_ref[...] += _mm_tn(h, d_pre)
        dh = _mm_nt(d_pre, wg_ref[...])
        dg_ref[...] += jnp.sum(dh * xhat, axis=0, keepdims=True)
        dx2 = dx3 + _rms_bwd(dh, xhat, r, g)
        dx_ref[...] = dx2
        dxb_ref[...] = dx2.astype(BF16)

    def blk(c):
        return pl.BlockSpec((tm, c), lambda i: (i, 0))

    return pl.pallas_call(
        body, name="ple_fwd_bwd", grid=(t // tm,),
        in_specs=[blk(D_MODEL), blk(PLE_DIM), blk(D_MODEL), _full((1, D_MODEL)), _full((D_MODEL, D_MODEL)),
                  _full((PLE_DIM, D_MODEL))],
        out_specs=[blk(D_MODEL), blk(D_MODEL), _full((8, 128)), _full((D_MODEL, D_MODEL)),
                   _full((PLE_DIM, D_MODEL)), _full((1, D_MODEL))],
        out_shape=[jax.ShapeDtypeStruct((t, D_MODEL), F32), jax.ShapeDtypeStruct((t, D_MODEL), BF16),
                   jax.ShapeDtypeStruct((8, 128), F32),
                   jax.ShapeDtypeStruct((D_MODEL, D_MODEL), F32), jax.ShapeDtypeStruct((PLE_DIM, D_MODEL), F32),
                   jax.ShapeDtypeStruct((1, D_MODEL), F32)],
        compiler_params=_cparams("arbitrary"),
    )(x2, p, target, g_ple, w_pg, w_pp)


def _ffn_bwd(dx2, h2, gp, up, w_gate, w_up, w_down, fcw, fcb, tm):
    t = dx2.shape[0]
    nblk = t // tm
    fc = D_FF // FF_CHUNKS
    half = tm // FFN_BWD_PARTS

    def body(dx_ref, h_ref, gp_ref, gph_ref, up_ref, wg_ref, wu_ref, wd_ref, cw_ref, cb_ref,
             dh_ref, dwd_hbm, dwu_hbm, dwg_hbm, dcw_ref, dcb_ref, carry_ref, a_scr, dup_scr, dgp_scr,
             dwd_acc, dwu_acc, dwg_acc, stage, stage_sem):
        i = pl.program_id(1)

        @pl.when(i == 0)
        def _():
            carry_ref[...] = jnp.zeros_like(carry_ref)
            dwd_acc[...] = jnp.zeros_like(dwd_acc)
            dwu_acc[...] = jnp.zeros_like(dwu_acc)
            dwg_acc[...] = jnp.zeros_like(dwg_acc)
            dcw_ref[...] = jnp.zeros_like(dcw_ref)
            dcb_ref[...] = jnp.zeros_like(dcb_ref)

        keep = (i < nblk - 1).astype(F32)
        later = carry_ref[...]
        for hf in reversed(range(FFN_BWD_PARTS)):
            rows = slice(hf * half, (hf + 1) * half)
            dxb = dx_ref[rows, :]
            gp_v = gp_ref[rows, :].astype(F32)
            if hf > 0:
                before = gp_ref[hf * half - 16:hf * half, :].astype(F32)
            else:
                before = gph_ref[...].astype(F32) * keep
            gate, gp1, gp2 = _conv_fwd(gp_v, before[15:16, :], before[14:15, :], cw_ref, cb_ref)
            s = _sigmoid(gate)
            silu = gate * s
            up_v = up_ref[rows, :].astype(F32)
            da = _mm_nt(dxb, wd_ref[...])
            a_scr[rows, :] = (silu * up_v).astype(BF16)
            d_up = (da * silu).astype(BF16)
            dup_scr[rows, :] = d_up
            d_gate = da * up_v * (s * (1.0 + gate * (1.0 - s)))
            d_gp = _conv_bwd_input(d_gate, later[0:1, :], later[1:2, :], cw_ref).astype(BF16)
            dgp_scr[rows, :] = d_gp
            later = d_gate[0:8, :]
            dcw_ref[0:1, :] += jnp.sum(d_gate * gp2, axis=0, keepdims=True)
            dcw_ref[1:2, :] += jnp.sum(d_gate * gp1, axis=0, keepdims=True)
            dcw_ref[2:3, :] += jnp.sum(d_gate * gp_v, axis=0, keepdims=True)
            dcb_ref[...] += jnp.sum(d_gate, axis=0, keepdims=True)
            dh_ref[rows, :] = (_mm(d_gp, wg_ref[...]) + _mm(d_up, wu_ref[...])).astype(BF16)
        carry_ref[...] = later
        dwd_acc[...] += _mm_tn(a_scr[...], dx_ref[...])
        dwu_acc[...] += _mm_tn(h_ref[...], dup_scr[...])
        dwg_acc[...] += _mm_tn(h_ref[...], dgp_scr[...])

        @pl.when(i == nblk - 1)
        def _():
            rows = pl.ds(pl.multiple_of(pl.program_id(0) * fc, 16), fc)
            for acc, out, flip in ((dwd_acc, dwd_hbm, False), (dwu_acc, dwu_hbm, True), (dwg_acc, dwg_hbm, True)):
                stage[...] = (acc[...].T if flip else acc[...]).astype(BF16)
                copy = pltpu.make_async_copy(stage, out.at[rows, :], stage_sem)
                copy.start()
                copy.wait()

    def rev(i):
        return nblk - 1 - i

    one = pl.Buffered(1)
    in_specs = [
        pl.BlockSpec((tm, D_MODEL), lambda j, i: (rev(i), 0)),
        pl.BlockSpec((tm, D_MODEL), lambda j, i: (rev(i), 0)),
        pl.BlockSpec((tm, fc), lambda j, i: (rev(i), j)),
        pl.BlockSpec((16, fc), lambda j, i: (jnp.maximum(rev(i) * (tm // 16) - 1, 0), j)),
        pl.BlockSpec((tm, fc), lambda j, i: (rev(i), j)),
        pl.BlockSpec((fc, D_MODEL), lambda j, i: (j, 0), pipeline_mode=one),
        pl.BlockSpec((fc, D_MODEL), lambda j, i: (j, 0), pipeline_mode=one),
        pl.BlockSpec((fc, D_MODEL), lambda j, i: (j, 0), pipeline_mode=one),
        pl.BlockSpec((3, fc), lambda j, i: (0, j)),
        pl.BlockSpec((1, fc), lambda j, i: (0, j)),
    ]
    out_specs = [
        pl.BlockSpec((None, tm, D_MODEL), lambda j, i: (j, rev(i), 0)),
        ANY, ANY, ANY,
        pl.BlockSpec((3, fc), lambda j, i: (0, j)),
        pl.BlockSpec((1, fc), lambda j, i: (0, j)),
    ]
    return pl.pallas_call(
        body, name="ffn_bwd", grid=(FF_CHUNKS, nblk), in_specs=in_specs, out_specs=out_specs,
        out_shape=[jax.ShapeDtypeStruct((FF_CHUNKS, t, D_MODEL), BF16), jax.ShapeDtypeStruct((D_FF, D_MODEL), BF16),
                   jax.ShapeDtypeStruct((D_FF, D_MODEL), BF16), jax.ShapeDtypeStruct((D_FF, D_MODEL), BF16),
                   jax.ShapeDtypeStruct((3, D_FF), F32), jax.ShapeDtypeStruct((1, D_FF), F32)],
        scratch_shapes=[pltpu.VMEM((8, fc), F32), pltpu.VMEM((tm, fc), BF16), pltpu.VMEM((tm, fc), BF16),
                        pltpu.VMEM((tm, fc), BF16), pltpu.VMEM((fc, D_MODEL), F32), pltpu.VMEM((D_MODEL, fc), F32),
                        pltpu.VMEM((D_MODEL, fc), F32), pltpu.VMEM((fc, D_MODEL), BF16), pltpu.SemaphoreType.DMA],
        compiler_params=_cparams("arbitrary", "arbitrary", vmem=V7X_VMEM_LIMIT_LARGE),
    )(dx2, h2, gp, gp, up, w_gate, w_up, w_down, fcw, fcb)


def _outproj_bwd(dh2, dx2, x1, g_ffn, w_out, yc, ya, goc, goa, zconv, conv_w, conv_b, bd, tm):
    t = x1.shape[0]
    nblk = t // tm

    def body(dh_ref, dx2_ref, x1_ref, g_ref, w_ref, yc_ref, ya_ref, goc_ref, goa_ref, zc_ref, zch_ref, cw_ref, cb_ref,
             bd_ref, dx1_ref, dya_ref, dd_ref, dzc_ref, dw_ref, dg_ref, dgoc_ref, dgoa_ref, dcw_ref, dcb_ref,
             carry_ref):
        i = pl.program_id(0)

        @pl.when(i == 0)
        def _():
            carry_ref[...] = jnp.zeros_like(carry_ref)
            for ref in (dw_ref, dg_ref, dgoc_ref, dgoa_ref, dcw_ref, dcb_ref):
                ref[...] = jnp.zeros_like(ref)

        keep = (i < nblk - 1).astype(F32)
        dh2_v = dh_ref[0].astype(F32)
        for j in range(1, FF_CHUNKS):
            dh2_v = dh2_v + dh_ref[j].astype(F32)
        r, xhat = _rms_stats(x1_ref[...])
        dg_ref[...] += jnp.sum(dh2_v * xhat, axis=0, keepdims=True)
        dx1 = dx2_ref[...] + _rms_bwd(dh2_v, xhat, r, g_ref[...])
        dx1_ref[...] = dx1
        dx1b = dx1.astype(BF16)
        dy = _mm_nt(dx1b, w_ref[...])

        yc_v = yc_ref[...].astype(F32)
        rc, ychat = _rms_stats(yc_v)
        dw_ref[0:CONV_W, :] += _mm_tn((ychat * goc_ref[...]).astype(BF16), dx1b)
        dyc = dy[:, 0:CONV_W]
        dgoc_ref[...] += jnp.sum(dyc * ychat, axis=0, keepdims=True)
        d_yc = _rms_bwd(dyc, ychat, rc, goc_ref[...])

        ya_v = ya_ref[...].astype(F32)
        ra, yahat = _rms_stats(ya_v)
        dw_ref[CONV_W:, :] += _mm_tn((yahat * goa_ref[...]).astype(BF16), dx1b)
        dya = dy[:, CONV_W:]
        dgoa_ref[...] += jnp.sum(dya * yahat, axis=0, keepdims=True)
        d_ya = _rms_bwd(dya, yahat, ra, goa_ref[...])
        dya_ref[...] = d_ya
        dd_ref[...] = _seg_sum64(d_ya * ya_v, bd_ref)

        zb = zc_ref[:, 0:CONV_W].astype(F32)
        zc = zc_ref[:, CONV_W:2 * CONV_W].astype(F32)
        zx = zc_ref[:, 2 * CONV_W:3 * CONV_W].astype(F32)
        u = zc * zx
        uh = (zch_ref[:, CONV_W:2 * CONV_W].astype(F32) * zch_ref[:, 2 * CONV_W:3 * CONV_W].astype(F32)) * keep
        cv, u1, u2 = _conv_fwd(u, uh[15:16, :], uh[14:15, :], cw_ref, cb_ref)
        d_cv = d_yc * zb
        d_u = _conv_bwd_input(d_cv, carry_ref[0:1, :], carry_ref[1:2, :], cw_ref)
        carry_ref[...] = d_cv[0:8, :]
        dcw_ref[0:1, :] += jnp.sum(d_cv * u2, axis=0, keepdims=True)
        dcw_ref[1:2, :] += jnp.sum(d_cv * u1, axis=0, keepdims=True)
        dcw_ref[2:3, :] += jnp.sum(d_cv * u, axis=0, keepdims=True)
        dcb_ref[...] += jnp.sum(d_cv, axis=0, keepdims=True)
        dzc_ref[:, 0:CONV_W] = (d_yc * cv).astype(BF16)
        dzc_ref[:, CONV_W:2 * CONV_W] = (d_u * zx).astype(BF16)
        dzc_ref[:, 2 * CONV_W:3 * CONV_W] = (d_u * zc).astype(BF16)

    def rev(i):
        return nblk - 1 - i

    def blk(c):
        return pl.BlockSpec((tm, c), lambda i: (rev(i), 0))

    in_specs = [
        pl.BlockSpec((FF_CHUNKS, tm, D_MODEL), lambda i: (0, rev(i), 0)),
        blk(D_MODEL), blk(D_MODEL), _full((1, D_MODEL)), _full((D_MODEL, D_MODEL)),
        blk(CONV_W), blk(ATTN_W), _full((1, CONV_W)), _full((1, ATTN_W)),
        blk(3 * CONV_W),
        pl.BlockSpec((16, 3 * CONV_W), lambda i: (jnp.maximum(rev(i) * (tm // 16) - 1, 0), 0)),
        _full((3, CONV_W)), _full((1, CONV_W)), _full((256, 256)),
    ]
    out_specs = [blk(D_MODEL), blk(ATTN_W), blk(ATTN_W), blk(3 * CONV_W), _full((D_MODEL, D_MODEL)),
                 _full((1, D_MODEL)), _full((1, CONV_W)), _full((1, ATTN_W)), _full((3, CONV_W)), _full((1, CONV_W))]
    return pl.pallas_call(
        body, name="outproj_bwd", grid=(nblk,), in_specs=in_specs, out_specs=out_specs,
        out_shape=[jax.ShapeDtypeStruct((t, D_MODEL), F32), jax.ShapeDtypeStruct((t, ATTN_W), F32),
                   jax.ShapeDtypeStruct((t, ATTN_W), F32), jax.ShapeDtypeStruct((t, 3 * CONV_W), BF16),
                   jax.ShapeDtypeStruct((D_MODEL, D_MODEL), F32), jax.ShapeDtypeStruct((1, D_MODEL), F32),
                   jax.ShapeDtypeStruct((1, CONV_W), F32), jax.ShapeDtypeStruct((1, ATTN_W), F32),
                   jax.ShapeDtypeStruct((3, CONV_W), F32), jax.ShapeDtypeStruct((1, CONV_W), F32)],
        scratch_shapes=[pltpu.VMEM((8, CONV_W), F32)],
        compiler_params=_cparams("arbitrary"),
    )(dh2, dx2, x1, g_ffn, w_out, yc, ya, goc, goa, zconv, zconv, conv_w, conv_b, bd)


def _attn_bwd(q, k, v, dya, lse, dd, slopes):
    t = q.shape[0]
    nsb = t // SUPER

    def body(q_ref, kc_ref, kp_ref, vc_ref, vp_ref, dy_ref, l_ref, d_ref, sl_ref, dq_ref, dk_ref, dv_ref,
             kk, vv, dkacc, dvacc):
        s = pl.program_id(1)

        @pl.when(s == 0)
        def _():
            dkacc[...] = jnp.zeros_like(dkacc)
            dvacc[...] = jnp.zeros_like(dvacc)

        dkacc[0:SUPER, :] = dkacc[SUPER:, :]
        dvacc[0:SUPER, :] = dvacc[SUPER:, :]
        dkacc[SUPER:, :] = jnp.zeros((SUPER, QK_BLOCK), F32)
        dvacc[SUPER:, :] = jnp.zeros((SUPER, QK_BLOCK), F32)

        @pl.when(s < nsb)
        def _():
            kk[0:SUPER, :] = kp_ref[...]
            kk[SUPER:, :] = kc_ref[...]
            vv[0:SUPER, :] = vp_ref[...]
            vv[SUPER:, :] = vc_ref[...]
            head0 = lax.broadcasted_iota(jnp.int32, (QK_BLOCK, QK_BLOCK), 1) < HEAD_DIM

            for b, dil in enumerate(DILATIONS):
                bias, own_half = _attn_bias(sl_ref, dil)

                def unit(u, carry, b=b, dil=dil, bias=bias, own_half=own_half):
                    start = _unit_start(u, dil)
                    first_key = SUPER + start - QK_BLOCK * dil
                    qrows = _rows(start, QK_BLOCK, dil)
                    krows = _rows(first_key, KEYS, dil)
                    q2 = _stack_heads(q_ref[qrows, :].astype(BF16), head0)
                    dy2 = _stack_heads(dy_ref[qrows, :].astype(BF16), head0)
                    lv, dv_ = l_ref[qrows, :], d_ref[qrows, :]
                    l2 = jnp.concatenate([lv[:, 0:1], lv[:, HEAD_DIM:HEAD_DIM + 1]], axis=0)
                    d2 = jnp.concatenate([dv_[:, 0:1], dv_[:, HEAD_DIM:HEAD_DIM + 1]], axis=0)
                    k2 = kk[krows, :].astype(BF16)
                    v2 = vv[krows, :].astype(BF16)
                    has_prev = jnp.logical_or(s > 0, start >= QK_BLOCK * dil)
                    sc = jnp.where(jnp.logical_or(own_half, has_prev), _mm_nt(q2, k2) + bias, -jnp.inf)
                    prob = jnp.exp(sc - l2)
                    ds = (prob * (_mm_nt(dy2, v2) - d2)).astype(BF16)
                    dvacc[krows, :] += _mm_tn(prob.astype(BF16), dy2)
                    dkacc[krows, :] += _mm_tn(ds, q2)
                    dq2 = _mm(ds, k2)
                    dq = jnp.where(head0, dq2[0:QK_BLOCK], dq2[QK_BLOCK:]) * ATTN_SCALE
                    if b == 0:
                        dq_ref[qrows, :] = dq
                    else:
                        dq_ref[qrows, :] += dq
                    return carry

                lax.fori_loop(0, SUPER // QK_BLOCK, unit, 0, unroll=8)

        dk_ref[...] = dkacc[0:SUPER, :]
        dv_ref[...] = dvacc[0:SUPER, :].astype(BF16)

    def cur_map(p, s):
        return (jnp.minimum(s, nsb - 1), p)

    def prev_map(p, s):
        return (jnp.clip(s - 1, 0, nsb - 1), p)

    cur = pl.BlockSpec((SUPER, QK_BLOCK), cur_map)
    prev = pl.BlockSpec((SUPER, QK_BLOCK), prev_map)
    return pl.pallas_call(
        body, name="attn_bwd", grid=(4, nsb + 1),
        in_specs=[cur, cur, prev, cur, prev, cur, cur, cur, pl.BlockSpec((1, 2, QK_BLOCK), lambda p, s: (p, 0, 0))],
        out_specs=[cur, prev, prev],
        out_shape=[jax.ShapeDtypeStruct((t, ATTN_W), F32), jax.ShapeDtypeStruct((t, ATTN_W), F32),
                   jax.ShapeDtypeStruct((t, ATTN_W), BF16)],
        scratch_shapes=[pltpu.VMEM((2 * SUPER, QK_BLOCK), F32)] * 4,
        compiler_params=_cparams("parallel", "arbitrary"),
    )(q, k, k, v, v, dya, lse, dd, slopes)


def _inproj_bwd(dq, dk, dv, dzconv, zqk, x, dx1, g_mix, w_in, qg, kg, bd, tm):
    t = x.shape[0]
    nblk = t // tm
    shard = IN_COLS // N_DEV

    def body(dq_ref, dk_ref, dv_ref, dzc_ref, zqk_ref, x_ref, dx1_ref, g_ref, w_ref, qg_ref,
             kg_ref, bd_ref, dx_ref, dw_hbm, dg_ref, dqg_ref, dkg_ref, dw_ref, stage, stage_sem):
        @pl.when(pl.program_id(0) == 0)
        def _():
            for ref in (dw_ref, dg_ref, dqg_ref, dkg_ref):
                ref[...] = jnp.zeros_like(ref)

        parts = [dzc_ref[...]]
        for j, (dn_ref, gain_ref, dgain_ref) in enumerate(((dq_ref, qg_ref, dqg_ref), (dk_ref, kg_ref, dkg_ref))):
            dn = dn_ref[...]
            z = zqk_ref[:, j * ATTN_W:(j + 1) * ATTN_W].astype(F32)
            r = lax.rsqrt(_seg_sum64(z * z, bd_ref) * (1.0 / HEAD_DIM) + EPS)
            zhat = z * r
            dgain_ref[...] += jnp.sum(dn * zhat, axis=0, keepdims=True)
            gd = dn * gain_ref[...]
            parts.append((r * (gd - zhat * (_seg_sum64(gd * zhat, bd_ref) * (1.0 / HEAD_DIM)))).astype(BF16))
        parts.append(dv_ref[...].astype(BF16))
        dz = jnp.concatenate(parts, axis=1)

        r, xhat = _rms_stats(x_ref[...])
        g = g_ref[...]
        dw_ref[...] += _mm_tn((xhat * g).astype(BF16), dz)
        dh = _mm_nt(dz, w_ref[...])
        dg_ref[...] += jnp.sum(dh * xhat, axis=0, keepdims=True)
        dx_ref[...] = dx1_ref[...] + _rms_bwd(dh, xhat, r, g)

        @pl.when(pl.program_id(0) == nblk - 1)
        def _():
            for k in range(N_DEV):
                stage[...] = dw_ref[:, k * shard:(k + 1) * shard].astype(BF16)
                copy = pltpu.make_async_copy(stage, dw_hbm.at[k], stage_sem)
                copy.start()
                copy.wait()

    def blk(c):
        return pl.BlockSpec((tm, c), lambda i: (i, 0))

    return pl.pallas_call(
        body, name="inproj_bwd", grid=(nblk,),
        in_specs=[blk(ATTN_W)] * 3 + [blk(3 * CONV_W), blk(2 * ATTN_W), blk(D_MODEL), blk(D_MODEL), _full((1, D_MODEL)),
                                      _full((D_MODEL, IN_COLS)), _full((1, ATTN_W)), _full((1, ATTN_W)),
                                      _full((256, 256))],
        out_specs=[blk(D_MODEL), ANY, _full((1, D_MODEL)), _full((1, ATTN_W)), _full((1, ATTN_W))],
        out_shape=[jax.ShapeDtypeStruct((t, D_MODEL), F32), jax.ShapeDtypeStruct((N_DEV, D_MODEL, shard), BF16),
                   jax.ShapeDtypeStruct((1, D_MODEL), F32), jax.ShapeDtypeStruct((1, ATTN_W), F32),
                   jax.ShapeDtypeStruct((1, ATTN_W), F32)],
        scratch_shapes=[pltpu.VMEM((D_MODEL, IN_COLS), F32), pltpu.VMEM((D_MODEL, shard), BF16),
                        pltpu.SemaphoreType.DMA],
        compiler_params=_cparams("arbitrary"),
    )(dq, dk, dv, dzconv, zqk, x, dx1, g_mix, w_in, qg, kg, bd)


def _ordered_after(a, token):
    return a if token is None else a + token[0:1, 0:1].reshape((1,) * a.ndim)


def _local_step(x, p, target, w, tms, hooks=None):
    hooks = hooks or {}
    bd = jnp.kron(jnp.eye(4, dtype=F32), jnp.ones((HEAD_DIM, HEAD_DIM), F32)).astype(BF16)
    qg = jnp.tile(w["q_norm_g"], (1, 8))
    kg = jnp.tile(w["k_norm_g"], (1, 8))
    slopes = jnp.exp2(-jnp.arange(1, 9, dtype=F32))
    slopes = jnp.broadcast_to(slopes.reshape(4, 2, 1), (4, 2, QK_BLOCK))

    zconv, zqk, yc, q, k, v = _inproj_fwd(x, w["g_mix"], w["w_in"], w["conv_w"], w["conv_b"], qg, kg, bd, tms[0])
    ya, lse = _attn_fwd(q, k, v, slopes)
    if "late_weights" in hooks:
        w = {**w, **hooks["late_weights"](lse)}
    x1 = _outproj_fwd(ya, yc, x, w["g_out_conv"], w["g_out_attn"], w["w_out"], tms[0])
    gp, up, h2, x2 = _ffn_fwd(x1, w["g_ffn"], w["w_gate"], w["w_up"], w["w_down"], w["ffn_conv_w"], w["ffn_conv_b"],
                              tms[1])
    dx2, dx2b, loss, dw_pg, dw_pp, dg_ple = _ple_fwd_bwd(x2, p, target, w["g_ple"], w["w_ple_gate"], w["w_ple_proj"], tms[0])
    dh2, dw_down, dw_up, dw_gate, dfcw, dfcb = _ffn_bwd(dx2b, h2, gp, up, w["w_gate"], w["w_up"], w["w_down"],
                                                        w["ffn_conv_w"], w["ffn_conv_b"], tms[0])
    token = None
    if "ffn_grads" in hooks:
        token = hooks["ffn_grads"]({"w_ple_gate": dw_pg, "w_ple_proj": dw_pp, "w_down": dw_down, "w_up": dw_up,
                                    "w_gate": dw_gate})
    dx1, dya, dd, dzconv, dw_out, dg_ffn, dgoc, dgoa, dcw, dcb = _outproj_bwd(
        dh2, dx2, x1, _ordered_after(w["g_ffn"], token), w["w_out"], yc, ya, w["g_out_conv"], w["g_out_attn"], zconv,
        w["conv_w"], w["conv_b"], bd, tms[1])
    token = hooks["outproj_done"](dx1) if "outproj_done" in hooks else None
    dq, dk, dv = _attn_bwd(q, k, v, dya, lse, dd, _ordered_after(slopes, token))
    dx, dw_in, dg_mix, dqg, dkg = _inproj_bwd(dq, dk, dv, dzconv, zqk, x, dx1, w["g_mix"], w["w_in"], qg, kg, bd,
                                              tms[0])
    grads = {
        "g_mix": dg_mix, "w_in": dw_in, "conv_w": dcw, "conv_b": dcb,
        "q_norm_g": dqg.reshape(8, HEAD_DIM).sum(0, keepdims=True),
        "k_norm_g": dkg.reshape(8, HEAD_DIM).sum(0, keepdims=True),
        "g_out_conv": dgoc, "g_out_attn": dgoa, "w_out": dw_out, "g_ffn": dg_ffn, "w_gate": dw_gate, "w_up": dw_up,
        "ffn_conv_w": dfcw, "ffn_conv_b": dfcb, "w_down": dw_down, "g_ple": dg_ple, "w_ple_gate": dw_pg,
        "w_ple_proj": dw_pp,
    }
    return loss, dx, grads


ANY = pl.BlockSpec(memory_space=pl.ANY)
MESH = pl.DeviceIdType.MESH


def _all_gather(shards, name):
    n = len(shards)

    def body(*refs):
        ins, outs = refs[:n], refs[n:2 * n]
        send_sems, recv_sems, local_sems = refs[2 * n:]
        x, y, c = lax.axis_index("x"), lax.axis_index("y"), lax.axis_index("c")
        me, sibling = (x, y, c), (x, y, 1 - c)
        chips = [(1 - x, y), (x, 1 - y), (1 - x, 1 - y)]

        def slot(dev):
            return 4 * dev[0] + 2 * dev[1] + dev[2]

        def copy(b, k, block, to, src=None):
            dst = outs[b].at[slot(block)]
            return pltpu.make_async_remote_copy(
                src_ref=dst if src is None else src, dst_ref=dst, send_sem=send_sems.at[b, k],
                recv_sem=recv_sems.at[b, k], device_id=to, device_id_type=MESH)

        mine = [pltpu.make_async_copy(ins[b], outs[b].at[slot(me)], local_sems.at[b]) for b in range(n)]
        first, passed = [], []
        for b in range(n):
            mine[b].start()
            first.append(copy(b, 0, me, sibling, src=ins[b]))
            first += [copy(b, 1 + j, me, (*chip, c), src=ins[b]) for j, chip in enumerate(chips)]
        for cp in first:
            cp.start()
        for j, chip in enumerate(chips):
            for b in range(n):
                copy(b, 1 + j, (*chip, c), me).wait_recv()
                fwd = copy(b, 4 + j, (*chip, c), sibling)
                fwd.start()
                passed.append(fwd)
        for b in range(n):
            copy(b, 0, sibling, me).wait_recv()
            for j, chip in enumerate(chips):
                copy(b, 4 + j, (*chip, 1 - c), me).wait_recv()
        for cp in first + passed:
            cp.wait_send()
        for cp in mine:
            cp.wait()

    return pl.pallas_call(
        body, name=name,
        in_specs=[ANY] * n, out_specs=[ANY] * n,
        out_shape=[jax.ShapeDtypeStruct((N_DEV,) + s.shape, s.dtype) for s in shards],
        scratch_shapes=[pltpu.SemaphoreType.DMA((n, 7)), pltpu.SemaphoreType.DMA((n, 7)),
                        pltpu.SemaphoreType.DMA((n,))],
    )(*shards)


HBM = pl.BlockSpec(memory_space=pltpu.HBM)
SEM = pl.BlockSpec(memory_space=pltpu.SEMAPHORE)
EFFECT = pltpu.SideEffectType.DATAFLOW_SIDE_EFFECTING
FLIPS = ((0, 0, 1), (0, 1, 0), (0, 1, 1), (1, 0, 0), (1, 0, 1), (1, 1, 0), (1, 1, 1))


def _flip_peers():
    pos = (lax.axis_index("x"), lax.axis_index("y"), lax.axis_index("c"))
    return [tuple(1 - a if f else a for a, f in zip(pos, flip)) for flip in FLIPS]


def _hbm(a):
    return pltpu.with_memory_space_constraint(a, pltpu.HBM)


def _split_start(name, srcs, lands, plan, n_copies, after):
    n, m = len(srcs), len(lands)

    def body(*refs):
        send_sems, recv_sems, token = refs[n + m + 1], refs[n + m + 2], refs[-1]
        for i, (src, dst, peer) in enumerate(plan(refs[:n], refs[n:n + m])):
            pltpu.make_async_remote_copy(src_ref=src, dst_ref=dst, send_sem=send_sems.at[i], recv_sem=recv_sems.at[i],
                                         device_id=peer, device_id_type=MESH).start()
        token[...] = jnp.zeros_like(token)

    outs = pl.pallas_call(
        body, name=name + "_start",
        in_specs=[HBM] * (n + m) + [ANY],
        out_specs=[SEM, SEM] + [HBM] * (n + m) + [pl.BlockSpec(memory_space=pltpu.VMEM)],
        out_shape=[pltpu.SemaphoreType.DMA((n_copies,)), pltpu.SemaphoreType.DMA((n_copies,))]
        + [pltpu.HBM(a.shape, a.dtype) for a in list(srcs) + list(lands)] + [jax.ShapeDtypeStruct((8, 128), F32)],
        input_output_aliases={i: 2 + i for i in range(n + m)},
        compiler_params=pltpu.CompilerParams(has_side_effects=EFFECT),
    )(*[_hbm(a) for a in list(srcs) + list(lands)], after)
    return (outs[0], outs[1], outs[2:2 + n], outs[2 + n:2 + n + m]), outs[-1]


def _split_wait(name, started, plan, after):
    send_sems, recv_sems, srcs, lands = started
    n, m = len(srcs), len(lands)

    def body(*refs):
        send_ref, recv_ref = refs[n + m], refs[n + m + 1]
        for i, (src, dst, peer) in enumerate(plan(refs[:n], refs[n:n + m])):
            copy = pltpu.make_async_remote_copy(src_ref=src, dst_ref=dst, send_sem=send_ref.at[i],
                                                recv_sem=recv_ref.at[i], device_id=peer, device_id_type=MESH)
            copy.wait_send()
            copy.wait_recv()

    outs = pl.pallas_call(
        body, name=name + "_wait",
        in_specs=[HBM] * (n + m) + [SEM, SEM, ANY],
        out_specs=[HBM] * (n + m),
        out_shape=[pltpu.HBM(a.shape, a.dtype) for a in list(srcs) + list(lands)],
        input_output_aliases={i: i for i in range(n + m)},
        compiler_params=pltpu.CompilerParams(has_side_effects=EFFECT),
    )(*srcs, *lands, send_sems, recv_sems, after)
    return outs[:n], outs[n:]


def _gather_plan(srcs, lands):
    slot = 4 * lax.axis_index("x") + 2 * lax.axis_index("y") + lax.axis_index("c")
    return [(src, land.at[slot], peer) for src, land in zip(srcs, lands) for peer in _flip_peers()]


def _all_gather_direct(shard, name):
    def body(src, out, send_sems, recv_sems, local_sem):
        slot = 4 * lax.axis_index("x") + 2 * lax.axis_index("y") + lax.axis_index("c")
        mine = pltpu.make_async_copy(src, out.at[slot], local_sem)
        mine.start()
        copies = [pltpu.make_async_remote_copy(src_ref=s, dst_ref=d, send_sem=send_sems.at[i], recv_sem=recv_sems.at[i],
                                               device_id=peer, device_id_type=MESH)
                  for i, (s, d, peer) in enumerate(_gather_plan([src], [out]))]
        for cp in copies:
            cp.start()
        for cp in copies:
            cp.wait()
        mine.wait()

    return pl.pallas_call(
        body, name=name, in_specs=[ANY], out_specs=ANY,
        out_shape=jax.ShapeDtypeStruct((N_DEV,) + shard.shape, shard.dtype),
        scratch_shapes=[pltpu.SemaphoreType.DMA((N_DEV - 1,)), pltpu.SemaphoreType.DMA((N_DEV - 1,)),
                        pltpu.SemaphoreType.DMA],
    )(shard)


def _sibling_plan(srcs, lands):
    x, y, c = lax.axis_index("x"), lax.axis_index("y"), lax.axis_index("c")
    return [(src.at[k, 1 - c], land.at[k], (x, y, 1 - c)) for src, land in zip(srcs, lands) for k in range(N_CHIP)]


def _chip_plan(srcs, lands):
    x, y, c = lax.axis_index("x"), lax.axis_index("y"), lax.axis_index("c")
    return [(src.at[2 * cx + cy], land.at[2 * x + y], (cx, cy, c))
            for src, land in zip(srcs, lands) for cx, cy in ((1 - x, y), (x, 1 - y), (1 - x, 1 - y))]


def _row_tile(rows):
    for tr in range(min(rows, 512), 15, -16):
        if rows % tr == 0:
            return tr
    return rows


def _pair_sum(g, land, core, name):
    rows, cols = land.shape[1:]
    tr = _row_tile(rows)

    def body(c_ref, g_ref, l_ref, o_ref):
        o_ref[...] = (g_ref[...].astype(F32) + l_ref[...].astype(F32)).astype(o_ref.dtype)

    return pl.pallas_call(
        body, name=f"rs_pair_sum_{name}",
        grid_spec=pltpu.PrefetchScalarGridSpec(
            num_scalar_prefetch=1, grid=(N_CHIP, rows // tr),
            in_specs=[pl.BlockSpec((None, None, tr, cols), lambda k, i, c_ref: (k, c_ref[0], i, 0)),
                      pl.BlockSpec((None, tr, cols), lambda k, i, c_ref: (k, i, 0))],
            out_specs=pl.BlockSpec((None, tr, cols), lambda k, i, c_ref: (k, i, 0))),
        out_shape=jax.ShapeDtypeStruct(land.shape, land.dtype),
        compiler_params=_cparams("parallel", "parallel"),
    )(core, g, land)


def _adamw(own, arrived, chip, w, m, v, name):
    k, rows, cols = arrived.shape
    tr = _row_tile(rows)
    c1 = 1.0 / (1.0 - ADAM_B1 ** ADAM_STEP)
    c2 = 1.0 / (1.0 - ADAM_B2 ** ADAM_STEP)

    def body(chip_ref, o_ref, p_ref, w_ref, m_ref, v_ref, g_ref, d_ref, nm_ref, nv_ref):
        def slab(j):
            return jnp.where(chip_ref[0] == j, o_ref[j], p_ref[j]).astype(F32)

        g = slab(0)
        for j in range(1, k):
            g = g + slab(j)
        g_ref[...] = g
        nm = ADAM_B1 * m_ref[...] + (1.0 - ADAM_B1) * g
        nv = ADAM_B2 * v_ref[...] + (1.0 - ADAM_B2) * (g * g)
        nm_ref[...] = nm
        nv_ref[...] = nv
        d_ref[...] = -ADAM_LR * ((nm * c1) / (jnp.sqrt(nv * c2) + ADAM_EPS) + ADAM_WD * w_ref[...])

    blk = pl.BlockSpec((tr, cols), lambda i, c: (i, 0))
    stack = pl.BlockSpec((k, tr, cols), lambda i, c: (0, i, 0))
    return pl.pallas_call(
        body, name=name,
        grid_spec=pltpu.PrefetchScalarGridSpec(num_scalar_prefetch=1, grid=(rows // tr,),
                                               in_specs=[stack, stack, blk, blk, blk], out_specs=[blk] * 4),
        out_shape=[jax.ShapeDtypeStruct((rows, cols), F32)] * 4,
        compiler_params=_cparams("parallel"),
    )(chip, own, arrived, w, m, v)


SMALL_LAYOUT = (("g_mix", 0, 1024), ("conv_b", 1, 512), ("q_norm_g", 2, 64), ("k_norm_g", 3, 64),
                ("g_out_conv", 4, 512), ("g_out_attn", 5, 512), ("g_ffn", 6, 1024), ("ffn_conv_b", 7, 2816),
                ("g_ple", 10, 1024))
CONV_W_ROW = 11
FFN_CONV_W_ROW = 14
LOSS_ROW = 23


def _row_pieces(cols):
    return [(c, min(1024, cols - c)) for c in range(0, cols, 1024)]


def _pack_small(grads, loss_tile):
    names = [n for n, _, _ in SMALL_LAYOUT]

    def body(*refs):
        ins, cw_ref, fcw_ref, loss_ref, out_ref = refs[:len(names)], refs[-4], refs[-3], refs[-2], refs[-1]
        out_ref[...] = jnp.zeros_like(out_ref)
        for ref, (_, row, cols) in zip(ins, SMALL_LAYOUT):
            for j, (c, width) in enumerate(_row_pieces(cols)):
                out_ref[row + j:row + j + 1, 0:width] = ref[:, c:c + width]
        for k in range(3):
            out_ref[CONV_W_ROW + k:CONV_W_ROW + k + 1, 0:CONV_W] = cw_ref[k:k + 1, :]
            for j, (c, width) in enumerate(_row_pieces(D_FF)):
                row = FFN_CONV_W_ROW + 3 * k + j
                out_ref[row:row + 1, 0:width] = fcw_ref[k:k + 1, c:c + width]
        out_ref[LOSS_ROW:LOSS_ROW + 1, 0:128] = loss_ref[0:1, :]

    return pl.pallas_call(
        body, name="pack_small_grads", out_shape=jax.ShapeDtypeStruct((SMALL_ROWS, 1024), F32),
    )(*[grads[n] for n in names], grads["conv_w"], grads["ffn_conv_w"], loss_tile)


def _adamw_small(arrived, conv_parts, fconv_parts, wts, mom, var):
    names = [n for n, _, _ in SMALL_LAYOUT] + ["conv_w", "ffn_conv_w"]
    c1 = 1.0 / (1.0 - ADAM_B1 ** ADAM_STEP)
    c2 = 1.0 / (1.0 - ADAM_B2 ** ADAM_STEP)
    n = len(names)

    def body(*refs):
        land, cw_ref, fcw_ref = refs[0], refs[1], refs[2]
        state = refs[3:3 + 3 * n]
        outs = refs[3 + 3 * n:]

        def total(piece):
            acc = piece(0)
            for d in range(1, N_DEV):
                acc = acc + piece(d)
            return acc

        for i, name in enumerate(names):
            if name == "conv_w":
                g = total(lambda d: cw_ref[d])
            elif name == "ffn_conv_w":
                g = total(lambda d: fcw_ref[d])
            else:
                _, row, cols = SMALL_LAYOUT[i]
                pieces = [total(lambda d, j=j, width=width: land[d, row + j:row + j + 1, 0:width])
                          for j, (_, width) in enumerate(_row_pieces(cols))]
                g = pieces[0] if len(pieces) == 1 else jnp.concatenate(pieces, axis=1)
            w_ref, m_ref, v_ref = state[3 * i:3 * i + 3]
            nm = ADAM_B1 * m_ref[...] + (1.0 - ADAM_B1) * g
            nv = ADAM_B2 * v_ref[...] + (1.0 - ADAM_B2) * (g * g)
            outs[4 * i][...] = g
            outs[4 * i + 1][...] = -ADAM_LR * ((nm * c1) / (jnp.sqrt(nv * c2) + ADAM_EPS) + ADAM_WD * w_ref[...])
            outs[4 * i + 2][...] = nm
            outs[4 * i + 3][...] = nv
        outs[-1][...] = total(lambda d: land[d, LOSS_ROW:LOSS_ROW + 1, 0:128])

    state = [a[nm_] for nm_ in names for a in (wts, mom, var)]
    shapes = [jax.ShapeDtypeStruct(wts[nm_].shape, F32) for nm_ in names for _ in range(4)]
    outs = pl.pallas_call(
        body, name="adamw_small", out_shape=shapes + [jax.ShapeDtypeStruct((1, 128), F32)],
    )(arrived, conv_parts, fconv_parts, *state)
    return {nm_: tuple(outs[4 * i:4 * i + 4]) for i, nm_ in enumerate(names)}, outs[-1][0, 0]


COL_SHARDED = ("w_in", "w_ple_proj")
TRANSPOSED = ("w_gate", "w_up")
CONV_SHARDED = (("conv_w", CONV_W), ("ffn_conv_w", D_FF))


def _gathered_to_full(name, gathered):
    if name in COL_SHARDED:
        return gathered.transpose(1, 0, 2).reshape(gathered.shape[1], -1)
    return gathered.reshape(-1, gathered.shape[2])


def _full_to_stacked(name, grad, shard_shape):
    sr, sc = shard_shape
    if grad.ndim == 3:
        a = grad
    elif name in COL_SHARDED:
        a = grad.reshape(sr, N_DEV, sc).transpose(1, 0, 2)
    else:
        a = grad.reshape(N_DEV, sr, sc)
    return a.astype(BF16).reshape(N_CHIP, 2, sr, sc)


def _pad_rows(vec, rows):
    return jnp.pad(vec, (0, rows * 1024 - vec.shape[0])).reshape(rows, 1024)


def kernel(x, p, g_mix, w_in, conv_w, conv_b, q_norm_g, k_norm_g, g_out_conv, g_out_attn, w_out, g_ffn, w_gate, w_up, ffn_conv_w, ffn_conv_b, w_down, g_ple, w_ple_gate, w_ple_proj, loss_target, m_g_mix, m_w_in, m_conv_w, m_conv_b, m_q_norm_g, m_k_norm_g, m_g_out_conv, m_g_out_attn, m_w_out, m_g_ffn, m_w_gate, m_w_up, m_ffn_conv_w, m_ffn_conv_b, m_w_down, m_g_ple, m_w_ple_gate, m_w_ple_proj, v_g_mix, v_w_in, v_conv_w, v_conv_b, v_q_norm_g, v_k_norm_g, v_g_out_conv, v_g_out_attn, v_w_out, v_g_ffn, v_w_gate, v_w_up, v_ffn_conv_w, v_ffn_conv_b, v_w_down, v_g_ple, v_w_ple_gate, v_w_ple_proj):
    args = dict(locals())
    names = ["g_mix", "w_in", "conv_w", "conv_b", "q_norm_g", "k_norm_g", "g_out_conv", "g_out_attn", "w_out", "g_ffn",
             "w_gate", "w_up", "ffn_conv_w", "ffn_conv_b", "w_down", "g_ple", "w_ple_gate", "w_ple_proj"]
    big = list(BIG)
    conv = [n for n, _ in CONV_SHARDED]

    def local(prefix):
        out = {n: (args[prefix + n][0] if n in big or n in conv else args[prefix + n]) for n in names}
        out.update({n: out[n].T for n in TRANSPOSED})
        return out

    wts, mom, var = local(""), local("m_"), local("v_")
    shard_shapes = {n: wts[n].shape for n in big}
    dev = 4 * lax.axis_index("x") + 2 * lax.axis_index("y") + lax.axis_index("c")
    core = lax.axis_index("c").astype(jnp.int32).reshape(1)

    conv_local = _pad_rows(jnp.concatenate([wts[n].reshape(-1) for n in conv]), 8).reshape(8, 1024)
    late = [n for n in big if n != "w_in"]
    w_in_all, conv_all = _all_gather([wts["w_in"].astype(BF16), conv_local], "gather_weights")
    late_shards = [wts[n].astype(BF16) for n in late]
    gathering, token = _split_start("gather_late_weights", late_shards,
                                    [lax.empty((N_DEV,) + s.shape, BF16) for s in late_shards], _gather_plan,
                                    7 * len(late), w_in_all)
    full = dict(wts)
    full["w_in"] = _gathered_to_full("w_in", w_in_all)
    full["g_mix"] = _ordered_after(wts["g_mix"], token)
    flying = {}

    def late_weights(after):
        shards, lands = _split_wait("gather_late_weights", gathering, _gather_plan, after)
        return {n: _gathered_to_full(n, lax.dynamic_update_slice(land, shard[None], (dev, 0, 0)))
                for n, land, shard in zip(late, lands, shards)}

    early = ["w_ple_gate", "w_ple_proj", "w_down", "w_up", "w_gate"]

    def ffn_grads(g):
        stacked = [_full_to_stacked(n, g[n], shard_shapes[n]) for n in early]
        flying["sibling"], tok = _split_start("rs_sibling_early", stacked,
                                              [lax.empty((N_CHIP,) + s.shape[2:], BF16) for s in stacked],
                                              _sibling_plan, N_CHIP * len(early), g["w_down"])
        return tok

    def outproj_done(after):
        stacked, landed = _split_wait("rs_sibling_early", flying["sibling"], _sibling_plan, after)
        parts = [_pair_sum(g, l, core, n) for n, g, l in zip(early, stacked, landed)]
        flying["chip"], tok = _split_start("rs_chip_early", parts, [lax.empty(q.shape, BF16) for q in parts],
                                           _chip_plan, 3 * len(early), landed[0])
        return tok

    off = 0
    for n, width in CONV_SHARDED:
        sc = width // N_DEV
        a = conv_all.reshape(N_DEV, -1)[:, off:off + 3 * sc].reshape(N_DEV, 3, sc)
        full[n] = a.transpose(1, 0, 2).reshape(3, width)
        off += 3 * sc

    loss, dx, grads = _local_step(x[0], p[0, 0], loss_target[0], full, (512, 256),
                                  {"late_weights": late_weights, "ffn_grads": ffn_grads, "outproj_done": outproj_done})

    chip = (2 * lax.axis_index("x") + lax.axis_index("y")).astype(jnp.int32).reshape(1)

    def adamw_of(group, parts, arrived):
        return {n: _adamw(own, got, chip, wts[n], mom[n], var[n], f"adamw_{n}")
                for n, own, got in zip(group, parts, arrived)}

    last = [n for n in big if n not in early]
    stacked = [_full_to_stacked(n, grads[n], shard_shapes[n]) for n in last]
    flying["sibling_last"], tok = _split_start("rs_sibling_last", stacked,
                                               [lax.empty((N_CHIP,) + s.shape[2:], BF16) for s in stacked],
                                               _sibling_plan, N_CHIP * len(last), dx)
    small_all = _all_gather_direct(_ordered_after(_pack_small(grads, loss), tok), "gather_small_grads")
    stacked, landed = _split_wait("rs_sibling_last", flying["sibling_last"], _sibling_plan, small_all)
    parts = [_pair_sum(g, l, core, n) for n, g, l in zip(last, stacked, landed)]
    flying["chip_last"], tok = _split_start("rs_chip_last", parts, [lax.empty(q.shape, BF16) for q in parts],
                                            _chip_plan, 3 * len(last), landed[0])

    parts, arrived = _split_wait("rs_chip_early", flying["chip"], _chip_plan, tok)
    out = adamw_of(early, parts, arrived)
    small_all = _ordered_after(small_all, tok)
    taps = small_all[:, CONV_W_ROW:CONV_W_ROW + 3, 0:CONV_W]
    ftaps = small_all[:, FFN_CONV_W_ROW:FFN_CONV_W_ROW + 9, :].reshape(N_DEV, 3, 3 * 1024)
    small_out, loss_total = _adamw_small(
        small_all, lax.dynamic_slice(taps, (0, 0, dev * (CONV_W // N_DEV)), (N_DEV, 3, CONV_W // N_DEV)),
        lax.dynamic_slice(ftaps, (0, 0, dev * (D_FF // N_DEV)), (N_DEV, 3, D_FF // N_DEV)), wts, mom, var)
    out.update(small_out)
    parts, arrived = _split_wait("rs_chip_last", flying["chip_last"], _chip_plan, small_out["g_mix"][0])
    out.update(adamw_of(last, parts, arrived))
    def result(n, which):
        a = out[n][which]
        return (a.T if n in TRANSPOSED else a).reshape(args[n].shape)

    return (loss_total, dx[None], *[result(n, which) for which in range(4) for n in names])
```

```python
import jax
import jax.numpy as jnp
from jax import lax
from jax.experimental import pallas as pl
from jax.experimental.pallas import tpu as pltpu

F32 = jnp.float32
BF16 = jnp.bfloat16

D_MODEL = 1024
CONV_W = 512
ATTN_W = 512
HEAD_DIM = 64
D_FF = 2816
PLE_DIM = 256
IN_COLS = 3 * CONV_W + 3 * ATTN_W
EPS = 1e-6
QK_BLOCK = 128
DILATIONS = (1, 4, 16)
ATTN_SCALE = HEAD_DIM ** -0.5

ADAM_LR = 0.001
ADAM_B1 = 0.9
ADAM_B2 = 0.999
ADAM_EPS = 1e-08
ADAM_WD = 0.01
ADAM_STEP = 10

N_DEV = 8
N_CHIP = 4
V7X_VMEM_LIMIT = 56 * 1024 * 1024
V7X_VMEM_LIMIT_LARGE = 62 * 1024 * 1024
FF_CHUNKS = 2
FFN_BWD_PARTS = 1

BIG = ("w_in", "w_out", "w_gate", "w_up", "w_down", "w_ple_gate", "w_ple_proj")
SMALL_ROWS = 24


def _cparams(*sem, vmem=V7X_VMEM_LIMIT):
    return pltpu.CompilerParams(dimension_semantics=sem, vmem_limit_bytes=vmem)


def _mm(a, b):
    return jnp.dot(a, b, preferred_element_type=F32)


def _mm_nt(a, b):
    return lax.dot_general(a, b, (((1,), (1,)), ((), ())), preferred_element_type=F32)


def _mm_tn(a, b):
    return lax.dot_general(a, b, (((0,), (0,)), ((), ())), preferred_element_type=F32)


def _full(shape):
    nd = len(shape)
    return pl.BlockSpec(shape, lambda *_: (0,) * nd)


def _rms_stats(x):
    r = lax.rsqrt(jnp.mean(x * x, axis=-1, keepdims=True) + EPS)
    return r, x * r


def _rms_bwd(dy, xhat, r, g):
    gd = dy * g
    return r * (gd - xhat * jnp.mean(gd * xhat, axis=-1, keepdims=True))


def _seg_sum64(v, bd_ref):
    outs = []
    for c in range(0, v.shape[1], 256):
        vc = v[:, c:c + 256]
        hi = vc.astype(BF16)
        lo = (vc - hi.astype(F32)).astype(BF16)
        outs.append(_mm(hi, bd_ref[...]) + _mm(lo, bd_ref[...]))
    return outs[0] if len(outs) == 1 else jnp.concatenate(outs, axis=1)


def _shift_rows(u, k, edge_rows):
    out = pltpu.roll(u, k, 0)
    row = lax.broadcasted_iota(jnp.int32, (8, u.shape[1]), 0)
    head = out[0:8]
    for j in range(k):
        head = jnp.where(row == j, edge_rows[k - 1 - j], head)
    return jnp.concatenate([head, out[8:]], axis=0)


def _shift_rows_up(u, k, edge_rows):
    n = u.shape[0]
    out = pltpu.roll(u, n - k, 0)
    row = lax.broadcasted_iota(jnp.int32, (8, u.shape[1]), 0)
    tail = out[n - 8:n]
    for j in range(k):
        tail = jnp.where(row == 8 - k + j, edge_rows[j], tail)
    return jnp.concatenate([out[0:n - 8], tail], axis=0)


def _conv_fwd(u, c1, c2, w_ref, b_ref):
    u1 = _shift_rows(u, 1, (c1,))
    u2 = _shift_rows(u, 2, (c1, c2))
    y = u2 * w_ref[0:1, :] + u1 * w_ref[1:2, :] + u * w_ref[2:3, :] + b_ref[...]
    return y, u1, u2


def _conv_bwd_input(dy, n1row, n2row, w_ref):
    d1 = _shift_rows_up(dy, 1, (n1row,))
    d2 = _shift_rows_up(dy, 2, (n1row, n2row))
    return dy * w_ref[2:3, :] + d1 * w_ref[1:2, :] + d2 * w_ref[0:1, :]


def _sigmoid(x):
    return 1.0 / (1.0 + jnp.exp(-x))


def _inproj_fwd(x, g_mix, w_in, conv_w, conv_b, qg, kg, bd, tm):
    t = x.shape[0]

    def body(x_ref, g_ref, w_ref, cw_ref, cb_ref, qg_ref, kg_ref, bd_ref,
             zc_ref, zqk_ref, yc_ref, q_ref, k_ref, v_ref, carry_ref):
        @pl.when(pl.program_id(0) == 0)
        def _():
            carry_ref[...] = jnp.zeros_like(carry_ref)

        _, xhat = _rms_stats(x_ref[...])
        h = (xhat * g_ref[...]).astype(BF16)
        zconv = _mm(h, w_ref[:, 0:3 * CONV_W])
        zc_ref[...] = zconv.astype(BF16)
        u = zconv[:, CONV_W:2 * CONV_W] * zconv[:, 2 * CONV_W:3 * CONV_W]
        cv, _, _ = _conv_fwd(u, carry_ref[7:8, :], carry_ref[6:7, :], cw_ref, cb_ref)
        yc_ref[...] = (zconv[:, 0:CONV_W] * cv).astype(BF16)
        carry_ref[...] = u[tm - 8:tm, :]

        zqk = _mm(h, w_ref[:, 3 * CONV_W:3 * CONV_W + 2 * ATTN_W])
        zqk_ref[...] = zqk.astype(BF16)
        for j, (gain_ref, out_ref, scale) in enumerate(((qg_ref, q_ref, ATTN_SCALE), (kg_ref, k_ref, 1.0))):
            z = zqk[:, j * ATTN_W:(j + 1) * ATTN_W]
            r = lax.rsqrt(_seg_sum64(z * z, bd_ref) * (1.0 / HEAD_DIM) + EPS)
            out_ref[...] = z * r * gain_ref[...] * scale
        v_ref[...] = _mm(h, w_ref[:, 3 * CONV_W + 2 * ATTN_W:IN_COLS])

    def blk(c):
        return pl.BlockSpec((tm, c), lambda i: (i, 0))

    return pl.pallas_call(
        body, name="inproj_fwd", grid=(t // tm,),
        in_specs=[blk(D_MODEL), _full((1, D_MODEL)), _full((D_MODEL, IN_COLS)), _full((3, CONV_W)),
                  _full((1, CONV_W)), _full((1, ATTN_W)), _full((1, ATTN_W)), _full((256, 256))],
        out_specs=[blk(3 * CONV_W), blk(2 * ATTN_W), blk(CONV_W), blk(ATTN_W), blk(ATTN_W), blk(ATTN_W)],
        out_shape=[jax.ShapeDtypeStruct((t, 3 * CONV_W), BF16), jax.ShapeDtypeStruct((t, 2 * ATTN_W), BF16),
                   jax.ShapeDtypeStruct((t, CONV_W), BF16), jax.ShapeDtypeStruct((t, ATTN_W), F32),
                   jax.ShapeDtypeStruct((t, ATTN_W), F32), jax.ShapeDtypeStruct((t, ATTN_W), F32)],
        scratch_shapes=[pltpu.VMEM((8, CONV_W), F32)],
        compiler_params=_cparams("arbitrary"),
    )(x, g_mix, w_in, conv_w, conv_b, qg, kg, bd)


SUPER = 16 * QK_BLOCK
KEYS = 2 * QK_BLOCK
UNITS = SUPER // QK_BLOCK


def _rows(start, size, dil):
    return pl.ds(start, size) if dil == 1 else pl.ds(start, size, stride=dil)


def _attn_bias(sl_ref, dil):
    qi = lax.broadcasted_iota(jnp.int32, (KEYS, KEYS), 0)
    kj = lax.broadcasted_iota(jnp.int32, (KEYS, KEYS), 1)
    step = jnp.bitwise_and(qi, QK_BLOCK - 1) + QK_BLOCK - kj
    slope = jnp.where(qi < QK_BLOCK, sl_ref[0, 0:1, 0:1], sl_ref[0, 1:2, 0:1])
    bias = jnp.where(jnp.logical_and(step >= 0, step <= QK_BLOCK), -slope * (step * dil).astype(F32), -jnp.inf)
    return bias, kj >= QK_BLOCK


def _unit_start(u, dil):
    if dil == 1:
        return pl.multiple_of(u * QK_BLOCK, QK_BLOCK)
    if dil == 4:
        return jnp.bitwise_and(u, 3) + (u // 4) * (4 * QK_BLOCK)
    return u


def _stack_heads(a, head0):
    zero = jnp.zeros_like(a)
    return jnp.concatenate([jnp.where(head0, a, zero), jnp.where(head0, zero, a)], axis=0)


def _attn_fwd(q, k, v, slopes):
    t = q.shape[0]
    nsb = t // SUPER

    def body(q_ref, kc_ref, kp_ref, vc_ref, vp_ref, sl_ref, o_ref, l_ref, e_ref, m_ref, kk, vv, ob, lb):
        s = pl.program_id(1)
        kk[0:SUPER, :] = kp_ref[...]
        kk[SUPER:, :] = kc_ref[...]
        vv[0:SUPER, :] = vp_ref[...]
        vv[SUPER:, :] = vc_ref[...]
        head0 = lax.broadcasted_iota(jnp.int32, (QK_BLOCK, QK_BLOCK), 1) < HEAD_DIM

        for b, dil in enumerate(DILATIONS):
            bias, own_half = _attn_bias(sl_ref, dil)

            def unit(u, carry, b=b, dil=dil, bias=bias, own_half=own_half):
                start = _unit_start(u, dil)
                first_key = SUPER + start - QK_BLOCK * dil
                q2 = _stack_heads(q_ref[_rows(start, QK_BLOCK, dil), :].astype(BF16), head0)
                k2 = kk[_rows(first_key, KEYS, dil), :].astype(BF16)
                v2 = vv[_rows(first_key, KEYS, dil), :].astype(BF16)
                has_prev = jnp.logical_or(s > 0, start >= QK_BLOCK * dil)
                sc = jnp.where(jnp.logical_or(own_half, has_prev), _mm_nt(q2, k2) + bias, -jnp.inf)
                m = jnp.max(sc, axis=-1, keepdims=True)
                e = jnp.exp(sc - m)
                den = jnp.sum(e, axis=-1, keepdims=True)
                eb = e.astype(BF16)
                e_ref[b * UNITS + u] = eb
                o2 = _mm(eb, v2) / den
                l2 = m + jnp.log(den)
                ob[b, _rows(start, QK_BLOCK, dil), :] = jnp.where(head0, o2[0:QK_BLOCK], o2[QK_BLOCK:])
                lb[b, _rows(start, QK_BLOCK, dil), :] = jnp.where(head0, l2[0:QK_BLOCK], l2[QK_BLOCK:])
                m_ref[b, _rows(start, QK_BLOCK, dil), :] = jnp.where(head0, m[0:QK_BLOCK], m[QK_BLOCK:])
                return carry

            lax.fori_loop(0, UNITS, unit, 0, unroll=16)

        def merge(i, carry):
            rows = pl.ds(pl.multiple_of(i * 256, 256), 256)
            la, lb_, lc = lb[0, rows, :], lb[1, rows, :], lb[2, rows, :]
            mx = jnp.maximum(jnp.maximum(la, lb_), lc)
            wa, wb, wc = jnp.exp(la - mx), jnp.exp(lb_ - mx), jnp.exp(lc - mx)
            sw = wa + wb + wc
            o_ref[rows, :] = ((wa * ob[0, rows, :] + wb * ob[1, rows, :] + wc * ob[2, rows, :]) / sw).astype(BF16)
            l_ref[rows, :] = mx + jnp.log(sw)
            return carry

        lax.fori_loop(0, SUPER // 256, merge, 0)

    cur = pl.BlockSpec((SUPER, QK_BLOCK), lambda p, s: (s, p))
    prev = pl.BlockSpec((SUPER, QK_BLOCK), lambda p, s: (jnp.maximum(s - 1, 0), p))
    return pl.pallas_call(
        body, name="attn_fwd", grid=(4, nsb),
        in_specs=[cur, cur, prev, cur, prev, pl.BlockSpec((1, 2, QK_BLOCK), lambda p, s: (p, 0, 0))],
        out_specs=[cur, cur, pl.BlockSpec((None, None, 3 * UNITS, KEYS, KEYS), lambda p, s: (p, s, 0, 0, 0)),
                   pl.BlockSpec((3, SUPER, QK_BLOCK), lambda p, s: (0, s, p))],
        out_shape=[jax.ShapeDtypeStruct((t, ATTN_W), BF16), jax.ShapeDtypeStruct((t, ATTN_W), F32),
                   jax.ShapeDtypeStruct((4, nsb, 3 * UNITS, KEYS, KEYS), BF16),
                   jax.ShapeDtypeStruct((3, t, ATTN_W), F32)],
        scratch_shapes=[pltpu.VMEM((2 * SUPER, QK_BLOCK), F32), pltpu.VMEM((2 * SUPER, QK_BLOCK), F32),
                        pltpu.VMEM((3, SUPER, QK_BLOCK), F32), pltpu.VMEM((3, SUPER, QK_BLOCK), F32)],
        compiler_params=_cparams("parallel", "arbitrary"),
    )(q, k, k, v, v, slopes)


def _outproj_fwd(ya, yc, x, goc, goa, w_out, tm):
    t = x.shape[0]

    def body(ya_ref, yc_ref, x_ref, goc_ref, goa_ref, w_ref, x1_ref):
        _, ychat = _rms_stats(yc_ref[...].astype(F32))
        _, yahat = _rms_stats(ya_ref[...].astype(F32))
        acc = _mm((ychat * goc_ref[...]).astype(BF16), w_ref[0:CONV_W, :])
        acc += _mm((yahat * goa_ref[...]).astype(BF16), w_ref[CONV_W:, :])
        x1_ref[...] = x_ref[...] + acc

    def blk(c):
        return pl.BlockSpec((tm, c), lambda i: (i, 0))

    return pl.pallas_call(
        body, name="outproj_fwd", grid=(t // tm,),
        in_specs=[blk(ATTN_W), blk(CONV_W), blk(D_MODEL), _full((1, CONV_W)), _full((1, ATTN_W)),
                  _full((D_MODEL, D_MODEL))],
        out_specs=blk(D_MODEL),
        out_shape=jax.ShapeDtypeStruct((t, D_MODEL), F32),
        compiler_params=_cparams("parallel"),
    )(ya, yc, x, goc, goa, w_out)


def _ffn_fwd(x1, g_ffn, w_gate_t, w_up_t, w_down, fcw, fcb, tm):
    t = x1.shape[0]

    def body(x_ref, g_ref, wg_ref, wu_ref, wd_ref, cw_ref, cb_ref, gp_ref, up_ref, h_ref, x2_ref, carry_ref):
        @pl.when(pl.program_id(0) == 0)
        def _():
            carry_ref[...] = jnp.zeros_like(carry_ref)

        xv = x_ref[...]
        _, xhat = _rms_stats(xv)
        h = (xhat * g_ref[...]).astype(BF16)
        h_ref[...] = h
        gp = _mm_nt(h, wg_ref[...])
        gp_ref[...] = gp.astype(BF16)
        gate, _, _ = _conv_fwd(gp, carry_ref[7:8, :], carry_ref[6:7, :], cw_ref, cb_ref)
        carry_ref[...] = gp[tm - 8:tm, :]
        up = _mm_nt(h, wu_ref[...])
        up_ref[...] = up.astype(BF16)
        a = (gate * _sigmoid(gate) * up).astype(BF16)
        x2_ref[...] = xv + _mm(a, wd_ref[...])

    def blk(c):
        return pl.BlockSpec((tm, c), lambda i: (i, 0))

    return pl.pallas_call(
        body, name="ffn_fwd", grid=(t // tm,),
        in_specs=[blk(D_MODEL), _full((1, D_MODEL)), _full((D_FF, D_MODEL)), _full((D_FF, D_MODEL)),
                  _full((D_FF, D_MODEL)), _full((3, D_FF)), _full((1, D_FF))],
        out_specs=[blk(D_FF), blk(D_FF), blk(D_MODEL), blk(D_MODEL)],
        out_shape=[jax.ShapeDtypeStruct((t, D_FF), BF16), jax.ShapeDtypeStruct((t, D_FF), BF16),
                   jax.ShapeDtypeStruct((t, D_MODEL), BF16), jax.ShapeDtypeStruct((t, D_MODEL), F32)],
        scratch_shapes=[pltpu.VMEM((8, D_FF), F32)],
        compiler_params=_cparams("arbitrary"),
    )(x1, g_ffn, w_gate_t, w_up_t, w_down, fcw, fcb)


def _ple_fwd_bwd(x2, p, target, g_ple, w_pg, w_pp, tm):
    t = x2.shape[0]

    def body(x_ref, p_ref, t_ref, g_ref, wg_ref, wp_ref, dx_ref, dxb_ref, loss_ref, dwg_ref, dwp_ref, dg_ref):
        @pl.when(pl.program_id(0) == 0)
        def _():
            loss_ref[...] = jnp.zeros_like(loss_ref)
            dwg_ref[...] = jnp.zeros_like(dwg_ref)
            dwp_ref[...] = jnp.zeros_like(dwp_ref)
            dg_ref[...] = jnp.zeros_like(dg_ref)

        xv = x_ref[...]
        r, xhat = _rms_stats(xv)
        g = g_ref[...]
        h = (xhat * g).astype(BF16)
        pg = _sigmoid(_mm(h, wg_ref[...]))
        pb = p_ref[...].astype(BF16)
        pp = _mm(pb, wp_ref[...])
        err = xv + pg * pp - t_ref[...]
        loss_ref[...] += 0.5 * jnp.sum(jnp.mean(err * err, axis=-1, keepdims=True))
        dx3 = err * (1.0 / D_MODEL)
        d_pp = (dx3 * pg).astype(BF16)
        d_pre = (dx3 * pp * pg * (1.0 - pg)).astype(BF16)
        dwp_ref[...] += _mm_tn(pb, d_pp)
        dwg_ref[...] += _mm_tn(h, d_pre)
        dh = _mm_nt(d_pre, wg_ref[...])
        dg_ref[...] += jnp.sum(dh * xhat, axis=0, keepdims=True)
        dx2 = dx3 + _rms_bwd(dh, xhat, r, g)
        dx_ref[...] = dx2
        dxb_ref[...] = dx2.astype(BF16)

    def blk(c):
        return pl.BlockSpec((tm, c), lambda i: (i, 0))

    return pl.pallas_call(
        body, name="ple_fwd_bwd", grid=(t // tm,),
        in_specs=[blk(D_MODEL), blk(PLE_DIM), blk(D_MODEL), _full((1, D_MODEL)), _full((D_MODEL, D_MODEL)),
                  _full((PLE_DIM, D_MODEL))],
        out_specs=[blk(D_MODEL), blk(D_MODEL), _full((8, 128)), _full((D_MODEL, D_MODEL)),
                   _full((PLE_DIM, D_MODEL)), _full((1, D_MODEL))],
        out_shape=[jax.ShapeDtypeStruct((t, D_MODEL), F32), jax.ShapeDtypeStruct((t, D_MODEL), BF16),
                   jax.ShapeDtypeStruct((8, 128), F32),
                   jax.ShapeDtypeStruct((D_MODEL, D_MODEL), F32), jax.ShapeDtypeStruct((PLE_DIM, D_MODEL), F32),
                   jax.ShapeDtypeStruct((1, D_MODEL), F32)],
        compiler_params=_cparams("arbitrary"),
    )(x2, p, target, g_ple, w_pg, w_pp)


def _ffn_bwd(dx2, h2, gp, up, w_gate, w_up, w_down, fcw, fcb, tm):
    t = dx2.shape[0]
    nblk = t // tm
    fc = D_FF // FF_CHUNKS
    half = tm // FFN_BWD_PARTS

    def body(dx_ref, h_ref, gp_ref, gph_ref, up_ref, wg_ref, wu_ref, wd_ref, cw_ref, cb_ref,
             dh_ref, dwd_hbm, dwu_hbm, dwg_hbm, dcw_ref, dcb_ref, carry_ref, a_scr, dup_scr, dgp_scr,
             dwd_acc, dwu_acc, dwg_acc, stage, stage_sem):
        i = pl.program_id(1)

        @pl.when(i == 0)
        def _():
            carry_ref[...] = jnp.zeros_like(carry_ref)
            dwd_acc[...] = jnp.zeros_like(dwd_acc)
            dwu_acc[...] = jnp.zeros_like(dwu_acc)
            dwg_acc[...] = jnp.zeros_like(dwg_acc)
            dcw_ref[...] = jnp.zeros_like(dcw_ref)
            dcb_ref[...] = jnp.zeros_like(dcb_ref)

        keep = (i < nblk - 1).astype(F32)
        later = carry_ref[...]
        for hf in reversed(range(FFN_BWD_PARTS)):
            rows = slice(hf * half, (hf + 1) * half)
            dxb = dx_ref[rows, :]
            gp_v = gp_ref[rows, :].astype(F32)
            if hf > 0:
                before = gp_ref[hf * half - 16:hf * half, :].astype(F32)
            else:
                before = gph_ref[...].astype(F32) * keep
            gate, gp1, gp2 = _conv_fwd(gp_v, before[15:16, :], before[14:15, :], cw_ref, cb_ref)
            s = _sigmoid(gate)
            silu = gate * s
            up_v = up_ref[rows, :].astype(F32)
            da = _mm_nt(dxb, wd_ref[...])
            a_scr[rows, :] = (silu * up_v).astype(BF16)
            d_up = (da * silu).astype(BF16)
            dup_scr[rows, :] = d_up
            d_gate = da * up_v * (s * (1.0 + gate * (1.0 - s)))
            d_gp = _conv_bwd_input(d_gate, later[0:1, :], later[1:2, :], cw_ref).astype(BF16)
            dgp_scr[rows, :] = d_gp
            later = d_gate[0:8, :]
            dcw_ref[0:1, :] += jnp.sum(d_gate * gp2, axis=0, keepdims=True)
            dcw_ref[1:2, :] += jnp.sum(d_gate * gp1, axis=0, keepdims=True)
            dcw_ref[2:3, :] += jnp.sum(d_gate * gp_v, axis=0, keepdims=True)
            dcb_ref[...] += jnp.sum(d_gate, axis=0, keepdims=True)
            dh_ref[rows, :] = (_mm(d_gp, wg_ref[...]) + _mm(d_up, wu_ref[...])).astype(BF16)
        carry_ref[...] = later
        dwd_acc[...] += _mm_tn(a_scr[...], dx_ref[...])
        dwu_acc[...] += _mm_tn(h_ref[...], dup_scr[...])
        dwg_acc[...] += _mm_tn(h_ref[...], dgp_scr[...])

        @pl.when(i == nblk - 1)
        def _():
            rows = pl.ds(pl.multiple_of(pl.program_id(0) * fc, 16), fc)
            for acc, out, flip in ((dwd_acc, dwd_hbm, False), (dwu_acc, dwu_hbm, True), (dwg_acc, dwg_hbm, True)):
                stage[...] = (acc[...].T if flip else acc[...]).astype(BF16)
                copy = pltpu.make_async_copy(stage, out.at[rows, :], stage_sem)
                copy.start()
                copy.wait()

    def rev(i):
        return nblk - 1 - i

    one = pl.Buffered(1)
    in_specs = [
        pl.BlockSpec((tm, D_MODEL), lambda j, i: (rev(i), 0)),
        pl.BlockSpec((tm, D_MODEL), lambda j, i: (rev(i), 0)),
        pl.BlockSpec((tm, fc), lambda j, i: (rev(i), j)),
        pl.BlockSpec((16, fc), lambda j, i: (jnp.maximum(rev(i) * (tm // 16) - 1, 0), j)),
        pl.BlockSpec((tm, fc), lambda j, i: (rev(i), j)),
        pl.BlockSpec((fc, D_MODEL), lambda j, i: (j, 0), pipeline_mode=one),
        pl.BlockSpec((fc, D_MODEL), lambda j, i: (j, 0), pipeline_mode=one),
        pl.BlockSpec((fc, D_MODEL), lambda j, i: (j, 0), pipeline_mode=one),
        pl.BlockSpec((3, fc), lambda j, i: (0, j)),
        pl.BlockSpec((1, fc), lambda j, i: (0, j)),
    ]
    out_specs = [
        pl.BlockSpec((None, tm, D_MODEL), lambda j, i: (j, rev(i), 0)),
        ANY, ANY, ANY,
        pl.BlockSpec((3, fc), lambda j, i: (0, j)),
        pl.BlockSpec((1, fc), lambda j, i: (0, j)),
    ]
    return pl.pallas_call(
        body, name="ffn_bwd", grid=(FF_CHUNKS, nblk), in_specs=in_specs, out_specs=out_specs,
        out_shape=[jax.ShapeDtypeStruct((FF_CHUNKS, t, D_MODEL), BF16), jax.ShapeDtypeStruct((D_FF, D_MODEL), BF16),
                   jax.ShapeDtypeStruct((D_FF, D_MODEL), BF16), jax.ShapeDtypeStruct((D_FF, D_MODEL), BF16),
                   jax.ShapeDtypeStruct((3, D_FF), F32), jax.ShapeDtypeStruct((1, D_FF), F32)],
        scratch_shapes=[pltpu.VMEM((8, fc), F32), pltpu.VMEM((tm, fc), BF16), pltpu.VMEM((tm, fc), BF16),
                        pltpu.VMEM((tm, fc), BF16), pltpu.VMEM((fc, D_MODEL), F32), pltpu.VMEM((D_MODEL, fc), F32),
                        pltpu.VMEM((D_MODEL, fc), F32), pltpu.VMEM((fc, D_MODEL), BF16), pltpu.SemaphoreType.DMA],
        compiler_params=_cparams("arbitrary", "arbitrary", vmem=V7X_VMEM_LIMIT_LARGE),
    )(dx2, h2, gp, gp, up, w_gate, w_up, w_down, fcw, fcb)


def _outproj_bwd(dh2, dx2, x1, g_ffn, w_out, yc, ya, goc, goa, zconv, conv_w, conv_b, bd, tm):
    t = x1.shape[0]
    nblk = t // tm

    def body(dh_ref, dx2_ref, x1_ref, g_ref, w_ref, yc_ref, ya_ref, goc_ref, goa_ref, zc_ref, zch_ref, cw_ref, cb_ref,
             bd_ref, dx1_ref, dya_ref, dd_ref, dzc_ref, dw_ref, dg_ref, dgoc_ref, dgoa_ref, dcw_ref, dcb_ref,
             carry_ref):
        i = pl.program_id(0)

        @pl.when(i == 0)
        def _():
            carry_ref[...] = jnp.zeros_like(carry_ref)
            for ref in (dw_ref, dg_ref, dgoc_ref, dgoa_ref, dcw_ref, dcb_ref):
                ref[...] = jnp.zeros_like(ref)

        keep = (i < nblk - 1).astype(F32)
        dh2_v = dh_ref[0].astype(F32)
        for j in range(1, FF_CHUNKS):
            dh2_v = dh2_v + dh_ref[j].astype(F32)
        r, xhat = _rms_stats(x1_ref[...])
        dg_ref[...] += jnp.sum(dh2_v * xhat, axis=0, keepdims=True)
        dx1 = dx2_ref[...] + _rms_bwd(dh2_v, xhat, r, g_ref[...])
        dx1_ref[...] = dx1
        dx1b = dx1.astype(BF16)
        dy = _mm_nt(dx1b, w_ref[...])

        yc_v = yc_ref[...].astype(F32)
        rc, ychat = _rms_stats(yc_v)
        dw_ref[0:CONV_W, :] += _mm_tn((ychat * goc_ref[...]).astype(BF16), dx1b)
        dyc = dy[:, 0:CONV_W]
        dgoc_ref[...] += jnp.sum(dyc * ychat, axis=0, keepdims=True)
        d_yc = _rms_bwd(dyc, ychat, rc, goc_ref[...])

        ya_v = ya_ref[...].astype(F32)
        ra, yahat = _rms_stats(ya_v)
        dw_ref[CONV_W:, :] += _mm_tn((yahat * goa_ref[...]).astype(BF16), dx1b)
        dya = dy[:, CONV_W:]
        dgoa_ref[...] += jnp.sum(dya * yahat, axis=0, keepdims=True)
        d_ya = _rms_bwd(dya, yahat, ra, goa_ref[...])
        dya_ref[...] = d_ya
        dd_ref[...] = _seg_sum64(d_ya * ya_v, bd_ref)

        zb = zc_ref[:, 0:CONV_W].astype(F32)
        zc = zc_ref[:, CONV_W:2 * CONV_W].astype(F32)
        zx = zc_ref[:, 2 * CONV_W:3 * CONV_W].astype(F32)
        u = zc * zx
        uh = (zch_ref[:, CONV_W:2 * CONV_W].astype(F32) * zch_ref[:, 2 * CONV_W:3 * CONV_W].astype(F32)) * keep
        cv, u1, u2 = _conv_fwd(u, uh[15:16, :], uh[14:15, :], cw_ref, cb_ref)
        d_cv = d_yc * zb
        d_u = _conv_bwd_input(d_cv, carry_ref[0:1, :], carry_ref[1:2, :], cw_ref)
        carry_ref[...] = d_cv[0:8, :]
        dcw_ref[0:1, :] += jnp.sum(d_cv * u2, axis=0, keepdims=True)
        dcw_ref[1:2, :] += jnp.sum(d_cv * u1, axis=0, keepdims=True)
        dcw_ref[2:3, :] += jnp.sum(d_cv * u, axis=0, keepdims=True)
        dcb_ref[...] += jnp.sum(d_cv, axis=0, keepdims=True)
        dzc_ref[:, 0:CONV_W] = (d_yc * cv).astype(BF16)
        dzc_ref[:, CONV_W:2 * CONV_W] = (d_u * zx).astype(BF16)
        dzc_ref[:, 2 * CONV_W:3 * CONV_W] = (d_u * zc).astype(BF16)

    def rev(i):
        return nblk - 1 - i

    def blk(c):
        return pl.BlockSpec((tm, c), lambda i: (rev(i), 0))

    in_specs = [
        pl.BlockSpec((FF_CHUNKS, tm, D_MODEL), lambda i: (0, rev(i), 0)),
        blk(D_MODEL), blk(D_MODEL), _full((1, D_MODEL)), _full((D_MODEL, D_MODEL)),
        blk(CONV_W), blk(ATTN_W), _full((1, CONV_W)), _full((1, ATTN_W)),
        blk(3 * CONV_W),
        pl.BlockSpec((16, 3 * CONV_W), lambda i: (jnp.maximum(rev(i) * (tm // 16) - 1, 0), 0)),
        _full((3, CONV_W)), _full((1, CONV_W)), _full((256, 256)),
    ]
    out_specs = [blk(D_MODEL), blk(ATTN_W), blk(ATTN_W), blk(3 * CONV_W), _full((D_MODEL, D_MODEL)),
                 _full((1, D_MODEL)), _full((1, CONV_W)), _full((1, ATTN_W)), _full((3, CONV_W)), _full((1, CONV_W))]
    return pl.pallas_call(
        body, name="outproj_bwd", grid=(nblk,), in_specs=in_specs, out_specs=out_specs,
        out_shape=[jax.ShapeDtypeStruct((t, D_MODEL), F32), jax.ShapeDtypeStruct((t, ATTN_W), F32),
                   jax.ShapeDtypeStruct((t, ATTN_W), F32), jax.ShapeDtypeStruct((t, 3 * CONV_W), BF16),
                   jax.ShapeDtypeStruct((D_MODEL, D_MODEL), F32), jax.ShapeDtypeStruct((1, D_MODEL), F32),
                   jax.ShapeDtypeStruct((1, CONV_W), F32), jax.ShapeDtypeStruct((1, ATTN_W), F32),
                   jax.ShapeDtypeStruct((3, CONV_W), F32), jax.ShapeDtypeStruct((1, CONV_W), F32)],
        scratch_shapes=[pltpu.VMEM((8, CONV_W), F32)],
        compiler_params=_cparams("arbitrary"),
    )(dh2, dx2, x1, g_ffn, w_out, yc, ya, goc, goa, zconv, zconv, conv_w, conv_b, bd)


def _attn_bwd(q, k, v, dya, lse, dd, e_all, m_all, after):
    t = q.shape[0]
    nsb = t // SUPER

    def body(q_ref, kc_ref, kp_ref, vc_ref, vp_ref, dy_ref, l_ref, d_ref, e_ref, m_ref, after_ref,
             dq_ref, dk_ref, dv_ref, kk, vv, dkacc, dvacc, dwide):
        s = pl.program_id(1)

        @pl.when(s == 0)
        def _():
            dkacc[...] = jnp.zeros_like(dkacc)
            dvacc[...] = jnp.zeros_like(dvacc)

        dkacc[0:SUPER, :] = dkacc[SUPER:, :]
        dvacc[0:SUPER, :] = dvacc[SUPER:, :]
        dkacc[SUPER:, :] = jnp.zeros((SUPER, QK_BLOCK), F32)
        dvacc[SUPER:, :] = jnp.zeros((SUPER, QK_BLOCK), F32)

        @pl.when(s < nsb)
        def _():
            kk[0:SUPER, :] = kp_ref[...]
            kk[SUPER:, :] = kc_ref[...]
            vv[0:SUPER, :] = vp_ref[...]
            vv[SUPER:, :] = vc_ref[...]
            head0 = lax.broadcasted_iota(jnp.int32, (QK_BLOCK, QK_BLOCK), 1) < HEAD_DIM

            def widened(a):
                other = pltpu.roll(a, HEAD_DIM, 1)
                first = lax.broadcasted_iota(jnp.int32, a.shape, 1) < HEAD_DIM
                return jnp.where(first, a, other), jnp.where(first, other, a)

            def stacked(h0, h1):
                return jnp.concatenate([jnp.concatenate([h0, h0], axis=1), jnp.concatenate([h1, h1], axis=1)], axis=0)

            def widen_dd(i, carry):
                rows = pl.ds(pl.multiple_of(i * 256, 256), 256)
                dwide[0, rows, :], dwide[1, rows, :] = widened(d_ref[rows, :])
                return carry

            lax.fori_loop(0, SUPER // 256, widen_dd, 0)

            for b, dil in enumerate(DILATIONS):
                def unit(u, carry, b=b, dil=dil):
                    start = _unit_start(u, dil)
                    first_key = SUPER + start - QK_BLOCK * dil
                    qrows = _rows(start, QK_BLOCK, dil)
                    krows = _rows(first_key, KEYS, dil)
                    q2 = _stack_heads(q_ref[qrows, :].astype(BF16), head0)
                    dy2 = _stack_heads(dy_ref[qrows, :].astype(BF16), head0)
                    g2 = stacked(*widened(jnp.exp(m_ref[b, qrows, :] - l_ref[qrows, :])))
                    d2 = stacked(dwide[0, qrows, :], dwide[1, qrows, :])
                    k2 = kk[krows, :].astype(BF16)
                    v2 = vv[krows, :].astype(BF16)
                    prob = e_ref[b * UNITS + u].astype(F32) * g2
                    ds = (prob * (_mm_nt(dy2, v2) - d2)).astype(BF16)
                    dvacc[krows, :] += _mm_tn(prob.astype(BF16), dy2)
                    dkacc[krows, :] += _mm_tn(ds, q2)
                    dq2 = _mm(ds, k2)
                    dq = jnp.where(head0, dq2[0:QK_BLOCK], dq2[QK_BLOCK:]) * ATTN_SCALE
                    if b == 0:
                        dq_ref[qrows, :] = dq
                    else:
                        dq_ref[qrows, :] += dq
                    return carry

                lax.fori_loop(0, UNITS, unit, 0, unroll=8)

        dk_ref[...] = dkacc[0:SUPER, :]
        dv_ref[...] = dvacc[0:SUPER, :].astype(BF16)

    def cur_map(p, s):
        return (jnp.minimum(s, nsb - 1), p)

    def prev_map(p, s):
        return (jnp.clip(s - 1, 0, nsb - 1), p)

    cur = pl.BlockSpec((SUPER, QK_BLOCK), cur_map)
    prev = pl.BlockSpec((SUPER, QK_BLOCK), prev_map)
    return pl.pallas_call(
        body, name="attn_bwd", grid=(4, nsb + 1),
        in_specs=[cur, cur, prev, cur, prev, cur, cur, cur,
                  pl.BlockSpec((None, None, 3 * UNITS, KEYS, KEYS), lambda p, s: (p, jnp.minimum(s, nsb - 1), 0, 0, 0)),
                  pl.BlockSpec((3, SUPER, QK_BLOCK), lambda p, s: (0, jnp.minimum(s, nsb - 1), p)),
                  pl.BlockSpec(memory_space=pl.ANY)],
        out_specs=[cur, prev, prev],
        out_shape=[jax.ShapeDtypeStruct((t, ATTN_W), F32), jax.ShapeDtypeStruct((t, ATTN_W), F32),
                   jax.ShapeDtypeStruct((t, ATTN_W), BF16)],
        scratch_shapes=[pltpu.VMEM((2 * SUPER, QK_BLOCK), F32)] * 4 + [pltpu.VMEM((2, SUPER, QK_BLOCK), F32)],
        compiler_params=_cparams("parallel", "arbitrary"),
    )(q, k, k, v, v, dya, lse, dd, e_all, m_all, after)


def _inproj_bwd(dq, dk, dv, dzconv, zqk, x, dx1, g_mix, w_in, qg, kg, bd, tm):
    t = x.shape[0]
    nblk = t // tm
    shard = IN_COLS // N_DEV

    def body(dq_ref, dk_ref, dv_ref, dzc_ref, zqk_ref, x_ref, dx1_ref, g_ref, w_ref, qg_ref,
             kg_ref, bd_ref, dx_ref, dw_hbm, dg_ref, dqg_ref, dkg_ref, dw_ref, stage, stage_sem):
        @pl.when(pl.program_id(0) == 0)
        def _():
            for ref in (dw_ref, dg_ref, dqg_ref, dkg_ref):
                ref[...] = jnp.zeros_like(ref)

        parts = [dzc_ref[...]]
        for j, (dn_ref, gain_ref, dgain_ref) in enumerate(((dq_ref, qg_ref, dqg_ref), (dk_ref, kg_ref, dkg_ref))):
            dn = dn_ref[...]
            z = zqk_ref[:, j * ATTN_W:(j + 1) * ATTN_W].astype(F32)
            r = lax.rsqrt(_seg_sum64(z * z, bd_ref) * (1.0 / HEAD_DIM) + EPS)
            zhat = z * r
            dgain_ref[...] += jnp.sum(dn * zhat, axis=0, keepdims=True)
            gd = dn * gain_ref[...]
            parts.append((r * (gd - zhat * (_seg_sum64(gd * zhat, bd_ref) * (1.0 / HEAD_DIM)))).astype(BF16))
        parts.append(dv_ref[...].astype(BF16))
        dz = jnp.concatenate(parts, axis=1)

        r, xhat = _rms_stats(x_ref[...])
        g = g_ref[...]
        dw_ref[...] += _mm_tn((xhat * g).astype(BF16), dz)
        dh = _mm_nt(dz, w_ref[...])
        dg_ref[...] += jnp.sum(dh * xhat, axis=0, keepdims=True)
        dx_ref[...] = dx1_ref[...] + _rms_bwd(dh, xhat, r, g)

        @pl.when(pl.program_id(0) == nblk - 1)
        def _():
            for k in range(N_DEV):
                stage[...] = dw_ref[:, k * shard:(k + 1) * shard].astype(BF16)
                copy = pltpu.make_async_copy(stage, dw_hbm.at[k], stage_sem)
                copy.start()
                copy.wait()

    def blk(c):
        return pl.BlockSpec((tm, c), lambda i: (i, 0))

    return pl.pallas_call(
        body, name="inproj_bwd", grid=(nblk,),
        in_specs=[blk(ATTN_W)] * 3 + [blk(3 * CONV_W), blk(2 * ATTN_W), blk(D_MODEL), blk(D_MODEL), _full((1, D_MODEL)),
                                      _full((D_MODEL, IN_COLS)), _full((1, ATTN_W)), _full((1, ATTN_W)),
                                      _full((256, 256))],
        out_specs=[blk(D_MODEL), ANY, _full((1, D_MODEL)), _full((1, ATTN_W)), _full((1, ATTN_W))],
        out_shape=[jax.ShapeDtypeStruct((t, D_MODEL), F32), jax.ShapeDtypeStruct((N_DEV, D_MODEL, shard), BF16),
                   jax.ShapeDtypeStruct((1, D_MODEL), F32), jax.ShapeDtypeStruct((1, ATTN_W), F32),
                   jax.ShapeDtypeStruct((1, ATTN_W), F32)],
        scratch_shapes=[pltpu.VMEM((D_MODEL, IN_COLS), F32), pltpu.VMEM((D_MODEL, shard), BF16),
                        pltpu.SemaphoreType.DMA],
        compiler_params=_cparams("arbitrary"),
    )(dq, dk, dv, dzconv, zqk, x, dx1, g_mix, w_in, qg, kg, bd)


def _ordered_after(a, token):
    return a if token is None else a + token[0:1, 0:1].reshape((1,) * a.ndim)


def _local_step(x, p, target, w, tms, hooks=None):
    hooks = hooks or {}
    bd = jnp.kron(jnp.eye(4, dtype=F32), jnp.ones((HEAD_DIM, HEAD_DIM), F32)).astype(BF16)
    qg = jnp.tile(w["q_norm_g"], (1, 8))
    kg = jnp.tile(w["k_norm_g"], (1, 8))
    slopes = jnp.exp2(-jnp.arange(1, 9, dtype=F32))
    slopes = jnp.broadcast_to(slopes.reshape(4, 2, 1), (4, 2, QK_BLOCK))

    zconv, zqk, yc, q, k, v = _inproj_fwd(x, w["g_mix"], w["w_in"], w["conv_w"], w["conv_b"], qg, kg, bd, tms[0])
    ya, lse, e_all, m_all = _attn_fwd(q, k, v, slopes)
    if "late_weights" in hooks:
        w = {**w, **hooks["late_weights"](lse)}
    x1 = _outproj_fwd(ya, yc, x, w["g_out_conv"], w["g_out_attn"], w["w_out"], tms[0])
    gp, up, h2, x2 = _ffn_fwd(x1, w["g_ffn"], w["w_gate"], w["w_up"], w["w_down"], w["ffn_conv_w"], w["ffn_conv_b"],
                              tms[1])
    dx2, dx2b, loss, dw_pg, dw_pp, dg_ple = _ple_fwd_bwd(x2, p, target, w["g_ple"], w["w_ple_gate"], w["w_ple_proj"], tms[0])
    dh2, dw_down, dw_up, dw_gate, dfcw, dfcb = _ffn_bwd(dx2b, h2, gp, up, w["w_gate"], w["w_up"], w["w_down"],
                                                        w["ffn_conv_w"], w["ffn_conv_b"], tms[0])
    token = None
    if "ffn_grads" in hooks:
        token = hooks["ffn_grads"]({"w_ple_gate": dw_pg, "w_ple_proj": dw_pp, "w_down": dw_down, "w_up": dw_up,
                                    "w_gate": dw_gate})
    dx1, dya, dd, dzconv, dw_out, dg_ffn, dgoc, dgoa, dcw, dcb = _outproj_bwd(
        dh2, dx2, x1, _ordered_after(w["g_ffn"], token), w["w_out"], yc, ya, w["g_out_conv"], w["g_out_attn"], zconv,
        w["conv_w"], w["conv_b"], bd, tms[1])
    token = hooks["outproj_done"](dx1) if "outproj_done" in hooks else None
    dq, dk, dv = _attn_bwd(q, k, v, dya, lse, dd, e_all, m_all, slopes if token is None else token)
    dx, dw_in, dg_mix, dqg, dkg = _inproj_bwd(dq, dk, dv, dzconv, zqk, x, dx1, w["g_mix"], w["w_in"], qg, kg, bd,
                                              tms[0])
    grads = {
        "g_mix": dg_mix, "w_in": dw_in, "conv_w": dcw, "conv_b": dcb,
        "q_norm_g": dqg.reshape(8, HEAD_DIM).sum(0, keepdims=True),
        "k_norm_g": dkg.reshape(8, HEAD_DIM).sum(0, keepdims=True),
        "g_out_conv": dgoc, "g_out_attn": dgoa, "w_out": dw_out, "g_ffn": dg_ffn, "w_gate": dw_gate, "w_up": dw_up,
        "ffn_conv_w": dfcw, "ffn_conv_b": dfcb, "w_down": dw_down, "g_ple": dg_ple, "w_ple_gate": dw_pg,
        "w_ple_proj": dw_pp,
    }
    return loss, dx, grads


ANY = pl.BlockSpec(memory_space=pl.ANY)
MESH = pl.DeviceIdType.MESH


def _all_gather(shards, name):
    n = len(shards)

    def body(*refs):
        ins, outs = refs[:n], refs[n:2 * n]
        send_sems, recv_sems, local_sems = refs[2 * n:]
        x, y, c = lax.axis_index("x"), lax.axis_index("y"), lax.axis_index("c")
        me, sibling = (x, y, c), (x, y, 1 - c)
        chips = [(1 - x, y), (x, 1 - y), (1 - x, 1 - y)]

        def slot(dev):
            return 4 * dev[0] + 2 * dev[1] + dev[2]

        def copy(b, k, block, to, src=None):
            dst = outs[b].at[slot(block)]
            return pltpu.make_async_remote_copy(
                src_ref=dst if src is None else src, dst_ref=dst, send_sem=send_sems.at[b, k],
                recv_sem=recv_sems.at[b, k], device_id=to, device_id_type=MESH)

        mine = [pltpu.make_async_copy(ins[b], outs[b].at[slot(me)], local_sems.at[b]) for b in range(n)]
        first, passed = [], []
        for b in range(n):
            mine[b].start()
            first.append(copy(b, 0, me, sibling, src=ins[b]))
            first += [copy(b, 1 + j, me, (*chip, c), src=ins[b]) for j, chip in enumerate(chips)]
        for cp in first:
            cp.start()
        for j, chip in enumerate(chips):
            for b in range(n):
                copy(b, 1 + j, (*chip, c), me).wait_recv()
                fwd = copy(b, 4 + j, (*chip, c), sibling)
                fwd.start()
                passed.append(fwd)
        for b in range(n):
            copy(b, 0, sibling, me).wait_recv()
            for j, chip in enumerate(chips):
                copy(b, 4 + j, (*chip, 1 - c), me).wait_recv()
        for cp in first + passed:
            cp.wait_send()
        for cp in mine:
            cp.wait()

    return pl.pallas_call(
        body, name=name,
        in_specs=[ANY] * n, out_specs=[ANY] * n,
        out_shape=[jax.ShapeDtypeStruct((N_DEV,) + s.shape, s.dtype) for s in shards],
        scratch_shapes=[pltpu.SemaphoreType.DMA((n, 7)), pltpu.SemaphoreType.DMA((n, 7)),
                        pltpu.SemaphoreType.DMA((n,))],
    )(*shards)


HBM = pl.BlockSpec(memory_space=pltpu.HBM)
SEM = pl.BlockSpec(memory_space=pltpu.SEMAPHORE)
EFFECT = pltpu.SideEffectType.DATAFLOW_SIDE_EFFECTING
FLIPS = ((0, 0, 1), (0, 1, 0), (0, 1, 1), (1, 0, 0), (1, 0, 1), (1, 1, 0), (1, 1, 1))


def _flip_peers():
    pos = (lax.axis_index("x"), lax.axis_index("y"), lax.axis_index("c"))
    return [tuple(1 - a if f else a for a, f in zip(pos, flip)) for flip in FLIPS]


def _hbm(a):
    return pltpu.with_memory_space_constraint(a, pltpu.HBM)


def _split_start(name, srcs, lands, plan, n_copies, after):
    n, m = len(srcs), len(lands)

    def body(*refs):
        send_sems, recv_sems, token = refs[n + m + 1], refs[n + m + 2], refs[-1]
        for i, (src, dst, peer) in enumerate(plan(refs[:n], refs[n:n + m])):
            pltpu.make_async_remote_copy(src_ref=src, dst_ref=dst, send_sem=send_sems.at[i], recv_sem=recv_sems.at[i],
                                         device_id=peer, device_id_type=MESH).start()
        token[...] = jnp.zeros_like(token)

    outs = pl.pallas_call(
        body, name=name + "_start",
        in_specs=[HBM] * (n + m) + [ANY],
        out_specs=[SEM, SEM] + [HBM] * (n + m) + [pl.BlockSpec(memory_space=pltpu.VMEM)],
        out_shape=[pltpu.SemaphoreType.DMA((n_copies,)), pltpu.SemaphoreType.DMA((n_copies,))]
        + [pltpu.HBM(a.shape, a.dtype) for a in list(srcs) + list(lands)] + [jax.ShapeDtypeStruct((8, 128), F32)],
        input_output_aliases={i: 2 + i for i in range(n + m)},
        compiler_params=pltpu.CompilerParams(has_side_effects=EFFECT),
    )(*[_hbm(a) for a in list(srcs) + list(lands)], after)
    return (outs[0], outs[1], outs[2:2 + n], outs[2 + n:2 + n + m]), outs[-1]


def _split_wait(name, started, plan, after):
    send_sems, recv_sems, srcs, lands = started
    n, m = len(srcs), len(lands)

    def body(*refs):
        send_ref, recv_ref = refs[n + m], refs[n + m + 1]
        for i, (src, dst, peer) in enumerate(plan(refs[:n], refs[n:n + m])):
            copy = pltpu.make_async_remote_copy(src_ref=src, dst_ref=dst, send_sem=send_ref.at[i],
                                                recv_sem=recv_ref.at[i], device_id=peer, device_id_type=MESH)
            copy.wait_send()
            copy.wait_recv()

    outs = pl.pallas_call(
        body, name=name + "_wait",
        in_specs=[HBM] * (n + m) + [SEM, SEM, ANY],
        out_specs=[HBM] * (n + m),
        out_shape=[pltpu.HBM(a.shape, a.dtype) for a in list(srcs) + list(lands)],
        input_output_aliases={i: i for i in range(n + m)},
        compiler_params=pltpu.CompilerParams(has_side_effects=EFFECT),
    )(*srcs, *lands, send_sems, recv_sems, after)
    return outs[:n], outs[n:]


def _gather_plan(srcs, lands):
    slot = 4 * lax.axis_index("x") + 2 * lax.axis_index("y") + lax.axis_index("c")
    return [(src, land.at[slot], peer) for src, land in zip(srcs, lands) for peer in _flip_peers()]


def _sibling_plan(srcs, lands):
    x, y, c = lax.axis_index("x"), lax.axis_index("y"), lax.axis_index("c")
    return [(src.at[k, 1 - c], land.at[k], (x, y, 1 - c)) for src, land in zip(srcs, lands) for k in range(N_CHIP)]


def _chip_plan(srcs, lands):
    x, y, c = lax.axis_index("x"), lax.axis_index("y"), lax.axis_index("c")
    return [(src.at[2 * cx + cy], land.at[2 * x + y], (cx, cy, c))
            for src, land in zip(srcs, lands) for cx, cy in ((1 - x, y), (x, 1 - y), (1 - x, 1 - y))]


def _row_tile(rows):
    for tr in range(min(rows, 512), 15, -16):
        if rows % tr == 0:
            return tr
    return rows


def _pair_sum(g, land, core, name):
    rows, cols = land.shape[1:]
    tr = _row_tile(rows)

    def body(c_ref, g_ref, l_ref, o_ref):
        o_ref[...] = (g_ref[...].astype(F32) + l_ref[...].astype(F32)).astype(o_ref.dtype)

    return pl.pallas_call(
        body, name=f"rs_pair_sum_{name}",
        grid_spec=pltpu.PrefetchScalarGridSpec(
            num_scalar_prefetch=1, grid=(N_CHIP, rows // tr),
            in_specs=[pl.BlockSpec((None, None, tr, cols), lambda k, i, c_ref: (k, c_ref[0], i, 0)),
                      pl.BlockSpec((None, tr, cols), lambda k, i, c_ref: (k, i, 0))],
            out_specs=pl.BlockSpec((None, tr, cols), lambda k, i, c_ref: (k, i, 0))),
        out_shape=jax.ShapeDtypeStruct(land.shape, land.dtype),
        compiler_params=_cparams("parallel", "parallel"),
    )(core, g, land)


def _adamw(own, arrived, chip, w, m, v, name):
    k, rows, cols = arrived.shape
    tr = _row_tile(rows)
    c1 = 1.0 / (1.0 - ADAM_B1 ** ADAM_STEP)
    c2 = 1.0 / (1.0 - ADAM_B2 ** ADAM_STEP)

    def body(chip_ref, o_ref, p_ref, w_ref, m_ref, v_ref, g_ref, d_ref, nm_ref, nv_ref):
        def slab(j):
            return jnp.where(chip_ref[0] == j, o_ref[j], p_ref[j]).astype(F32)

        g = slab(0)
        for j in range(1, k):
            g = g + slab(j)
        g_ref[...] = g
        nm = ADAM_B1 * m_ref[...] + (1.0 - ADAM_B1) * g
        nv = ADAM_B2 * v_ref[...] + (1.0 - ADAM_B2) * (g * g)
        nm_ref[...] = nm
        nv_ref[...] = nv
        d_ref[...] = -ADAM_LR * ((nm * c1) / (jnp.sqrt(nv * c2) + ADAM_EPS) + ADAM_WD * w_ref[...])

    blk = pl.BlockSpec((tr, cols), lambda i, c: (i, 0))
    stack = pl.BlockSpec((k, tr, cols), lambda i, c: (0, i, 0))
    return pl.pallas_call(
        body, name=name,
        grid_spec=pltpu.PrefetchScalarGridSpec(num_scalar_prefetch=1, grid=(rows // tr,),
                                               in_specs=[stack, stack, blk, blk, blk], out_specs=[blk] * 4),
        out_shape=[jax.ShapeDtypeStruct((rows, cols), F32)] * 4,
        compiler_params=_cparams("parallel"),
    )(chip, own, arrived, w, m, v)


SMALL_LAYOUT = (("g_mix", 0, 1024), ("conv_b", 1, 512), ("q_norm_g", 2, 64), ("k_norm_g", 3, 64),
                ("g_out_conv", 4, 512), ("g_out_attn", 5, 512), ("g_ffn", 6, 1024), ("ffn_conv_b", 7, 2816),
                ("g_ple", 10, 1024))
CONV_W_ROW = 11
FFN_CONV_W_ROW = 14
LOSS_ROW = 23


def _row_pieces(cols):
    return [(c, min(1024, cols - c)) for c in range(0, cols, 1024)]


def _pack_small(grads, loss_tile):
    names = [n for n, _, _ in SMALL_LAYOUT]

    def body(*refs):
        ins, cw_ref, fcw_ref, loss_ref, out_ref = refs[:len(names)], refs[-4], refs[-3], refs[-2], refs[-1]
        out_ref[...] = jnp.zeros_like(out_ref)
        for ref, (_, row, cols) in zip(ins, SMALL_LAYOUT):
            for j, (c, width) in enumerate(_row_pieces(cols)):
                out_ref[row + j:row + j + 1, 0:width] = ref[:, c:c + width]
        for k in range(3):
            out_ref[CONV_W_ROW + k:CONV_W_ROW + k + 1, 0:CONV_W] = cw_ref[k:k + 1, :]
            for j, (c, width) in enumerate(_row_pieces(D_FF)):
                row = FFN_CONV_W_ROW + 3 * k + j
                out_ref[row:row + 1, 0:width] = fcw_ref[k:k + 1, c:c + width]
        out_ref[LOSS_ROW:LOSS_ROW + 1, 0:128] = loss_ref[0:1, :]

    return pl.pallas_call(
        body, name="pack_small_grads", out_shape=jax.ShapeDtypeStruct((SMALL_ROWS, 1024), F32),
    )(*[grads[n] for n in names], grads["conv_w"], grads["ffn_conv_w"], loss_tile)


def _adamw_small(arrived, conv_parts, fconv_parts, wts, mom, var):
    names = [n for n, _, _ in SMALL_LAYOUT] + ["conv_w", "ffn_conv_w"]
    c1 = 1.0 / (1.0 - ADAM_B1 ** ADAM_STEP)
    c2 = 1.0 / (1.0 - ADAM_B2 ** ADAM_STEP)
    n = len(names)

    def body(*refs):
        land, cw_ref, fcw_ref = refs[0], refs[1], refs[2]
        state = refs[3:3 + 3 * n]
        outs = refs[3 + 3 * n:]

        def total(piece):
            acc = piece(0)
            for d in range(1, N_DEV):
                acc = acc + piece(d)
            return acc

        for i, name in enumerate(names):
            if name == "conv_w":
                g = total(lambda d: cw_ref[d])
            elif name == "ffn_conv_w":
                g = total(lambda d: fcw_ref[d])
            else:
                _, row, cols = SMALL_LAYOUT[i]
                pieces = [total(lambda d, j=j, width=width: land[d, row + j:row + j + 1, 0:width])
                          for j, (_, width) in enumerate(_row_pieces(cols))]
                g = pieces[0] if len(pieces) == 1 else jnp.concatenate(pieces, axis=1)
            w_ref, m_ref, v_ref = state[3 * i:3 * i + 3]
            nm = ADAM_B1 * m_ref[...] + (1.0 - ADAM_B1) * g
            nv = ADAM_B2 * v_ref[...] + (1.0 - ADAM_B2) * (g * g)
            outs[4 * i][...] = g
            outs[4 * i + 1][...] = -ADAM_LR * ((nm * c1) / (jnp.sqrt(nv * c2) + ADAM_EPS) + ADAM_WD * w_ref[...])
            outs[4 * i + 2][...] = nm
            outs[4 * i + 3][...] = nv
        outs[-1][...] = total(lambda d: land[d, LOSS_ROW:LOSS_ROW + 1, 0:128])

    state = [a[nm_] for nm_ in names for a in (wts, mom, var)]
    shapes = [jax.ShapeDtypeStruct(wts[nm_].shape, F32) for nm_ in names for _ in range(4)]
    outs = pl.pallas_call(
        body, name="adamw_small", out_shape=shapes + [jax.ShapeDtypeStruct((1, 128), F32)],
    )(arrived, conv_parts, fconv_parts, *state)
    return {nm_: tuple(outs[4 * i:4 * i + 4]) for i, nm_ in enumerate(names)}, outs[-1][0, 0]


COL_SHARDED = ("w_in", "w_ple_proj")
TRANSPOSED = ("w_gate", "w_up")
CONV_SHARDED = (("conv_w", CONV_W), ("ffn_conv_w", D_FF))


def _gathered_to_full(name, gathered):
    if name in COL_SHARDED:
        return gathered.transpose(1, 0, 2).reshape(gathered.shape[1], -1)
    return gathered.reshape(-1, gathered.shape[2])


def _full_to_stacked(name, grad, shard_shape):
    sr, sc = shard_shape
    if grad.ndim == 3:
        a = grad
    elif name in COL_SHARDED:
        a = grad.reshape(sr, N_DEV, sc).transpose(1, 0, 2)
    else:
        a = grad.reshape(N_DEV, sr, sc)
    return a.astype(BF16).reshape(N_CHIP, 2, sr, sc)


def _pad_rows(vec, rows):
    return jnp.pad(vec, (0, rows * 1024 - vec.shape[0])).reshape(rows, 1024)


def kernel(x, p, g_mix, w_in, conv_w, conv_b, q_norm_g, k_norm_g, g_out_conv, g_out_attn, w_out, g_ffn, w_gate, w_up, ffn_conv_w, ffn_conv_b, w_down, g_ple, w_ple_gate, w_ple_proj, loss_target, m_g_mix, m_w_in, m_conv_w, m_conv_b, m_q_norm_g, m_k_norm_g, m_g_out_conv, m_g_out_attn, m_w_out, m_g_ffn, m_w_gate, m_w_up, m_ffn_conv_w, m_ffn_conv_b, m_w_down, m_g_ple, m_w_ple_gate, m_w_ple_proj, v_g_mix, v_w_in, v_conv_w, v_conv_b, v_q_norm_g, v_k_norm_g, v_g_out_conv, v_g_out_attn, v_w_out, v_g_ffn, v_w_gate, v_w_up, v_ffn_conv_w, v_ffn_conv_b, v_w_down, v_g_ple, v_w_ple_gate, v_w_ple_proj):
    args = dict(locals())
    names = ["g_mix", "w_in", "conv_w", "conv_b", "q_norm_g", "k_norm_g", "g_out_conv", "g_out_attn", "w_out", "g_ffn",
             "w_gate", "w_up", "ffn_conv_w", "ffn_conv_b", "w_down", "g_ple", "w_ple_gate", "w_ple_proj"]
    big = list(BIG)
    conv = [n for n, _ in CONV_SHARDED]

    def local(prefix):
        out = {n: (args[prefix + n][0] if n in big or n in conv else args[prefix + n]) for n in names}
        out.update({n: out[n].T for n in TRANSPOSED})
        return out

    wts, mom, var = local(""), local("m_"), local("v_")
    shard_shapes = {n: wts[n].shape for n in big}
    dev = 4 * lax.axis_index("x") + 2 * lax.axis_index("y") + lax.axis_index("c")
    core = lax.axis_index("c").astype(jnp.int32).reshape(1)

    conv_local = _pad_rows(jnp.concatenate([wts[n].reshape(-1) for n in conv]), 8).reshape(8, 1024)
    late = [n for n in big if n != "w_in"]
    w_in_all, conv_all = _all_gather([wts["w_in"].astype(BF16), conv_local], "gather_weights")
    late_shards = [wts[n].astype(BF16) for n in late]
    gathering, token = _split_start("gather_late_weights", late_shards,
                                    [lax.empty((N_DEV,) + s.shape, BF16) for s in late_shards], _gather_plan,
                                    7 * len(late), w_in_all)
    full = dict(wts)
    full["w_in"] = _gathered_to_full("w_in", w_in_all)
    full["g_mix"] = _ordered_after(wts["g_mix"], token)
    flying = {}

    def late_weights(after):
        shards, lands = _split_wait("gather_late_weights", gathering, _gather_plan, after)
        return {n: _gathered_to_full(n, lax.dynamic_update_slice(land, shard[None], (dev, 0, 0)))
                for n, land, shard in zip(late, lands, shards)}

    early = ["w_ple_gate", "w_ple_proj", "w_down", "w_up", "w_gate"]

    def ffn_grads(g):
        stacked = [_full_to_stacked(n, g[n], shard_shapes[n]) for n in early]
        flying["sibling"], tok = _split_start("rs_sibling_early", stacked,
                                              [lax.empty((N_CHIP,) + s.shape[2:], BF16) for s in stacked],
                                              _sibling_plan, N_CHIP * len(early), g["w_down"])
        return tok

    def outproj_done(after):
        stacked, landed = _split_wait("rs_sibling_early", flying["sibling"], _sibling_plan, after)
        parts = [_pair_sum(g, l, core, n) for n, g, l in zip(early, stacked, landed)]
        flying["chip"], tok = _split_start("rs_chip_early", parts, [lax.empty(q.shape, BF16) for q in parts],
                                           _chip_plan, 3 * len(early), landed[0])
        return tok

    off = 0
    for n, width in CONV_SHARDED:
        sc = width // N_DEV
        a = conv_all.reshape(N_DEV, -1)[:, off:off + 3 * sc].reshape(N_DEV, 3, sc)
        full[n] = a.transpose(1, 0, 2).reshape(3, width)
        off += 3 * sc

    loss, dx, grads = _local_step(x[0], p[0, 0], loss_target[0], full, (512, 256),
                                  {"late_weights": late_weights, "ffn_grads": ffn_grads, "outproj_done": outproj_done})

    chip = (2 * lax.axis_index("x") + lax.axis_index("y")).astype(jnp.int32).reshape(1)

    def adamw_of(group, parts, arrived):
        return {n: _adamw(own, got, chip, wts[n], mom[n], var[n], f"adamw_{n}")
                for n, own, got in zip(group, parts, arrived)}

    last = [n for n in big if n not in early]
    stacked = [_full_to_stacked(n, grads[n], shard_shapes[n]) for n in last]
    flying["sibling_last"], tok = _split_start("rs_sibling_last", stacked,
                                               [lax.empty((N_CHIP,) + s.shape[2:], BF16) for s in stacked],
                                               _sibling_plan, N_CHIP * len(last), dx)
    (small_all,) = _all_gather([_ordered_after(_pack_small(grads, loss), tok)], "gather_small_grads")
    stacked, landed = _split_wait("rs_sibling_last", flying["sibling_last"], _sibling_plan, small_all)
    parts = [_pair_sum(g, l, core, n) for n, g, l in zip(last, stacked, landed)]
    flying["chip_last"], tok = _split_start("rs_chip_last", parts, [lax.empty(q.shape, BF16) for q in parts],
                                            _chip_plan, 3 * len(last), landed[0])

    parts, arrived = _split_wait("rs_chip_early", flying["chip"], _chip_plan, tok)
    out = adamw_of(early, parts, arrived)
    small_all = _ordered_after(small_all, tok)
    taps = small_all[:, CONV_W_ROW:CONV_W_ROW + 3, 0:CONV_W]
    ftaps = small_all[:, FFN_CONV_W_ROW:FFN_CONV_W_ROW + 9, :].reshape(N_DEV, 3, 3 * 1024)
    small_out, loss_total = _adamw_small(
        small_all, lax.dynamic_slice(taps, (0, 0, dev * (CONV_W // N_DEV)), (N_DEV, 3, CONV_W // N_DEV)),
        lax.dynamic_slice(ftaps, (0, 0, dev * (D_FF // N_DEV)), (N_DEV, 3, D_FF // N_DEV)), wts, mom, var)
    out.update(small_out)
    parts, arrived = _split_wait("rs_chip_last", flying["chip_last"], _chip_plan, small_out["g_mix"][0])
    out.update(adamw_of(last, parts, arrived))
    def result(n, which):
        a = out[n][which]
        return (a.T if n in TRANSPOSED else a).reshape(args[n].shape)

    return (loss_total, dx[None], *[result(n, which) for which in range(4) for n in names])
```

```python
import jax
import jax.numpy as jnp
from jax import lax
from jax.experimental import pallas as pl
from jax.experimental.pallas import tpu as pltpu

F32 = jnp.float32
BF16 = jnp.bfloat16

D_MODEL = 1024
CONV_W = 512
ATTN_W = 512
HEAD_DIM = 64
D_FF = 2816
PLE_DIM = 256
IN_COLS = 3 * CONV_W + 3 * ATTN_W
EPS = 1e-6
QK_BLOCK = 128
DILATIONS = (1, 4, 16)
ATTN_SCALE = HEAD_DIM ** -0.5

ADAM_LR = 0.001
ADAM_B1 = 0.9
ADAM_B2 = 0.999
ADAM_EPS = 1e-08
ADAM_WD = 0.01
ADAM_STEP = 10

N_DEV = 8
N_CHIP = 4
V7X_VMEM_LIMIT = 56 * 1024 * 1024
V7X_VMEM_LIMIT_LARGE = 62 * 1024 * 1024
FF_CHUNKS = 2
FFN_BWD_PARTS = 1

BIG = ("w_in", "w_out", "w_gate", "w_up", "w_down", "w_ple_gate", "w_ple_proj")
SMALL_ROWS = 24


def _cparams(*sem, vmem=V7X_VMEM_LIMIT):
    return pltpu.CompilerParams(dimension_semantics=sem, vmem_limit_bytes=vmem)


def _mm(a, b):
    return jnp.dot(a, b, preferred_element_type=F32)


def _mm_nt(a, b):
    return lax.dot_general(a, b, (((1,), (1,)), ((), ())), preferred_element_type=F32)


def _mm_tn(a, b):
    return lax.dot_general(a, b, (((0,), (0,)), ((), ())), preferred_element_type=F32)


def _full(shape):
    nd = len(shape)
    return pl.BlockSpec(shape, lambda *_: (0,) * nd)


def _rms_stats(x):
    r = lax.rsqrt(jnp.mean(x * x, axis=-1, keepdims=True) + EPS)
    return r, x * r


def _rms_bwd(dy, xhat, r, g):
    gd = dy * g
    return r * (gd - xhat * jnp.mean(gd * xhat, axis=-1, keepdims=True))


def _seg_sum64(v, bd_ref):
    outs = []
    for c in range(0, v.shape[1], 256):
        vc = v[:, c:c + 256]
        hi = vc.astype(BF16)
        lo = (vc - hi.astype(F32)).astype(BF16)
        outs.append(_mm(hi, bd_ref[...]) + _mm(lo, bd_ref[...]))
    return outs[0] if len(outs) == 1 else jnp.concatenate(outs, axis=1)


def _shift_rows(u, k, edge_rows):
    out = pltpu.roll(u, k, 0)
    row = lax.broadcasted_iota(jnp.int32, (8, u.shape[1]), 0)
    head = out[0:8]
    for j in range(k):
        head = jnp.where(row == j, edge_rows[k - 1 - j], head)
    return jnp.concatenate([head, out[8:]], axis=0)


def _shift_rows_up(u, k, edge_rows):
    n = u.shape[0]
    out = pltpu.roll(u, n - k, 0)
    row = lax.broadcasted_iota(jnp.int32, (8, u.shape[1]), 0)
    tail = out[n - 8:n]
    for j in range(k):
        tail = jnp.where(row == 8 - k + j, edge_rows[j], tail)
    return jnp.concatenate([out[0:n - 8], tail], axis=0)


def _conv_fwd(u, c1, c2, w_ref, b_ref):
    u1 = _shift_rows(u, 1, (c1,))
    u2 = _shift_rows(u, 2, (c1, c2))
    y = u2 * w_ref[0:1, :] + u1 * w_ref[1:2, :] + u * w_ref[2:3, :] + b_ref[...]
    return y, u1, u2


def _conv_bwd_input(dy, n1row, n2row, w_ref):
    d1 = _shift_rows_up(dy, 1, (n1row,))
    d2 = _shift_rows_up(dy, 2, (n1row, n2row))
    return dy * w_ref[2:3, :] + d1 * w_ref[1:2, :] + d2 * w_ref[0:1, :]


def _sigmoid(x):
    return 1.0 / (1.0 + jnp.exp(-x))


def _inproj_fwd(x, g_mix, w_in, conv_w, conv_b, qg, kg, bd, tm):
    t = x.shape[0]

    def body(x_ref, g_ref, w_ref, cw_ref, cb_ref, qg_ref, kg_ref, bd_ref,
             zc_ref, zqk_ref, yc_ref, q_ref, k_ref, v_ref, carry_ref):
        @pl.when(pl.program_id(0) == 0)
        def _():
            carry_ref[...] = jnp.zeros_like(carry_ref)

        _, xhat = _rms_stats(x_ref[...])
        h = (xhat * g_ref[...]).astype(BF16)
        zconv = _mm(h, w_ref[:, 0:3 * CONV_W])
        zc_ref[...] = zconv.astype(BF16)
        u = zconv[:, CONV_W:2 * CONV_W] * zconv[:, 2 * CONV_W:3 * CONV_W]
        cv, _, _ = _conv_fwd(u, carry_ref[7:8, :], carry_ref[6:7, :], cw_ref, cb_ref)
        yc_ref[...] = (zconv[:, 0:CONV_W] * cv).astype(BF16)
        carry_ref[...] = u[tm - 8:tm, :]

        zqk = _mm(h, w_ref[:, 3 * CONV_W:3 * CONV_W + 2 * ATTN_W])
        zqk_ref[...] = zqk.astype(BF16)
        for j, (gain_ref, out_ref, scale) in enumerate(((qg_ref, q_ref, ATTN_SCALE), (kg_ref, k_ref, 1.0))):
            z = zqk[:, j * ATTN_W:(j + 1) * ATTN_W]
            r = lax.rsqrt(_seg_sum64(z * z, bd_ref) * (1.0 / HEAD_DIM) + EPS)
            out_ref[...] = z * r * gain_ref[...] * scale
        v_ref[...] = _mm(h, w_ref[:, 3 * CONV_W + 2 * ATTN_W:IN_COLS])

    def blk(c):
        return pl.BlockSpec((tm, c), lambda i: (i, 0))

    return pl.pallas_call(
        body, name="inproj_fwd", grid=(t // tm,),
        in_specs=[blk(D_MODEL), _full((1, D_MODEL)), _full((D_MODEL, IN_COLS)), _full((3, CONV_W)),
                  _full((1, CONV_W)), _full((1, ATTN_W)), _full((1, ATTN_W)), _full((256, 256))],
        out_specs=[blk(3 * CONV_W), blk(2 * ATTN_W), blk(CONV_W), blk(ATTN_W), blk(ATTN_W), blk(ATTN_W)],
        out_shape=[jax.ShapeDtypeStruct((t, 3 * CONV_W), BF16), jax.ShapeDtypeStruct((t, 2 * ATTN_W), BF16),
                   jax.ShapeDtypeStruct((t, CONV_W), BF16), jax.ShapeDtypeStruct((t, ATTN_W), F32),
                   jax.ShapeDtypeStruct((t, ATTN_W), F32), jax.ShapeDtypeStruct((t, ATTN_W), F32)],
        scratch_shapes=[pltpu.VMEM((8, CONV_W), F32)],
        compiler_params=_cparams("arbitrary"),
    )(x, g_mix, w_in, conv_w, conv_b, qg, kg, bd)


SUPER = 16 * QK_BLOCK
KEYS = 2 * QK_BLOCK
UNITS = SUPER // QK_BLOCK


def _rows(start, size, dil):
    return pl.ds(start, size) if dil == 1 else pl.ds(start, size, stride=dil)


def _attn_bias(sl_ref, dil):
    qi = lax.broadcasted_iota(jnp.int32, (KEYS, KEYS), 0)
    kj = lax.broadcasted_iota(jnp.int32, (KEYS, KEYS), 1)
    step = jnp.bitwise_and(qi, QK_BLOCK - 1) + QK_BLOCK - kj
    slope = jnp.where(qi < QK_BLOCK, sl_ref[0, 0:1, 0:1], sl_ref[0, 1:2, 0:1])
    bias = jnp.where(jnp.logical_and(step >= 0, step <= QK_BLOCK), -slope * (step * dil).astype(F32), -jnp.inf)
    return bias, kj >= QK_BLOCK


def _unit_start(u, dil):
    if dil == 1:
        return pl.multiple_of(u * QK_BLOCK, QK_BLOCK)
    if dil == 4:
        return jnp.bitwise_and(u, 3) + (u // 4) * (4 * QK_BLOCK)
    return u


def _stack_heads(a, head0):
    zero = jnp.zeros_like(a)
    return jnp.concatenate([jnp.where(head0, a, zero), jnp.where(head0, zero, a)], axis=0)


def _attn_fwd(q, k, v, slopes):
    t = q.shape[0]
    nsb = t // SUPER

    def body(q_ref, kc_ref, kp_ref, vc_ref, vp_ref, sl_ref, o_ref, l_ref, e_ref, m_ref, kk, vv, ob, lb):
        s = pl.program_id(1)
        kk[0:SUPER, :] = kp_ref[...]
        kk[SUPER:, :] = kc_ref[...]
        vv[0:SUPER, :] = vp_ref[...]
        vv[SUPER:, :] = vc_ref[...]
        head0 = lax.broadcasted_iota(jnp.int32, (QK_BLOCK, QK_BLOCK), 1) < HEAD_DIM

        for b, dil in enumerate(DILATIONS):
            bias, own_half = _attn_bias(sl_ref, dil)

            def unit(u, carry, b=b, dil=dil, bias=bias, own_half=own_half):
                start = _unit_start(u, dil)
                first_key = SUPER + start - QK_BLOCK * dil
                q2 = _stack_heads(q_ref[_rows(start, QK_BLOCK, dil), :].astype(BF16), head0)
                k2 = kk[_rows(first_key, KEYS, dil), :].astype(BF16)
                v2 = vv[_rows(first_key, KEYS, dil), :].astype(BF16)
                has_prev = jnp.logical_or(s > 0, start >= QK_BLOCK * dil)
                sc = jnp.where(jnp.logical_or(own_half, has_prev), _mm_nt(q2, k2) + bias, -jnp.inf)
                m = jnp.max(sc, axis=-1, keepdims=True)
                e = jnp.exp(sc - m)
                den = jnp.sum(e, axis=-1, keepdims=True)
                eb = e.astype(BF16)
                e_ref[b * UNITS + u] = eb
                o2 = _mm(eb, v2) / den
                l2 = m + jnp.log(den)
                ob[b, _rows(start, QK_BLOCK, dil), :] = jnp.where(head0, o2[0:QK_BLOCK], o2[QK_BLOCK:])
                lb[b, _rows(start, QK_BLOCK, dil), :] = jnp.where(head0, l2[0:QK_BLOCK], l2[QK_BLOCK:])
                m_ref[b, _rows(start, QK_BLOCK, dil), :] = jnp.where(head0, m[0:QK_BLOCK], m[QK_BLOCK:])
                return carry

            lax.fori_loop(0, UNITS, unit, 0, unroll=16)

        def merge(i, carry):
            rows = pl.ds(pl.multiple_of(i * 256, 256), 256)
            la, lb_, lc = lb[0, rows, :], lb[1, rows, :], lb[2, rows, :]
            mx = jnp.maximum(jnp.maximum(la, lb_), lc)
            wa, wb, wc = jnp.exp(la - mx), jnp.exp(lb_ - mx), jnp.exp(lc - mx)
            sw = wa + wb + wc
            o_ref[rows, :] = ((wa * ob[0, rows, :] + wb * ob[1, rows, :] + wc * ob[2, rows, :]) / sw).astype(BF16)
            l_ref[rows, :] = mx + jnp.log(sw)
            return carry

        lax.fori_loop(0, SUPER // 256, merge, 0)

    cur = pl.BlockSpec((SUPER, QK_BLOCK), lambda p, s: (s, p))
    prev = pl.BlockSpec((SUPER, QK_BLOCK), lambda p, s: (jnp.maximum(s - 1, 0), p))
    return pl.pallas_call(
        body, name="attn_fwd", grid=(4, nsb),
        in_specs=[cur, cur, prev, cur, prev, pl.BlockSpec((1, 2, QK_BLOCK), lambda p, s: (p, 0, 0))],
        out_specs=[cur, cur, pl.BlockSpec((None, None, 3 * UNITS, KEYS, KEYS), lambda p, s: (p, s, 0, 0, 0)),
                   pl.BlockSpec((3, SUPER, QK_BLOCK), lambda p, s: (0, s, p))],
        out_shape=[jax.ShapeDtypeStruct((t, ATTN_W), BF16), jax.ShapeDtypeStruct((t, ATTN_W), F32),
                   jax.ShapeDtypeStruct((4, nsb, 3 * UNITS, KEYS, KEYS), BF16),
                   jax.ShapeDtypeStruct((3, t, ATTN_W), F32)],
        scratch_shapes=[pltpu.VMEM((2 * SUPER, QK_BLOCK), F32), pltpu.VMEM((2 * SUPER, QK_BLOCK), F32),
                        pltpu.VMEM((3, SUPER, QK_BLOCK), F32), pltpu.VMEM((3, SUPER, QK_BLOCK), F32)],
        compiler_params=_cparams("parallel", "arbitrary"),
    )(q, k, k, v, v, slopes)


def _outproj_fwd(ya, yc, x, goc, goa, w_out, tm):
    t = x.shape[0]

    def body(ya_ref, yc_ref, x_ref, goc_ref, goa_ref, w_ref, x1_ref):
        _, ychat = _rms_stats(yc_ref[...].astype(F32))
        _, yahat = _rms_stats(ya_ref[...].astype(F32))
        acc = _mm((ychat * goc_ref[...]).astype(BF16), w_ref[0:CONV_W, :])
        acc += _mm((yahat * goa_ref[...]).astype(BF16), w_ref[CONV_W:, :])
        x1_ref[...] = x_ref[...] + acc

    def blk(c):
        return pl.BlockSpec((tm, c), lambda i: (i, 0))

    return pl.pallas_call(
        body, name="outproj_fwd", grid=(t // tm,),
        in_specs=[blk(ATTN_W), blk(CONV_W), blk(D_MODEL), _full((1, CONV_W)), _full((1, ATTN_W)),
                  _full((D_MODEL, D_MODEL))],
        out_specs=blk(D_MODEL),
        out_shape=jax.ShapeDtypeStruct((t, D_MODEL), F32),
        compiler_params=_cparams("parallel"),
    )(ya, yc, x, goc, goa, w_out)


def _ffn_fwd(x1, g_ffn, w_gate_t, w_up_t, w_down, fcw, fcb, tm):
    t = x1.shape[0]

    def body(x_ref, g_ref, wg_ref, wu_ref, wd_ref, cw_ref, cb_ref, gp_ref, up_ref, h_ref, x2_ref, carry_ref):
        @pl.when(pl.program_id(0) == 0)
        def _():
            carry_ref[...] = jnp.zeros_like(carry_ref)

        xv = x_ref[...]
        _, xhat = _rms_stats(xv)
        h = (xhat * g_ref[...]).astype(BF16)
        h_ref[...] = h
        gp = _mm_nt(h, wg_ref[...])
        gp_ref[...] = gp.astype(BF16)
        gate, _, _ = _conv_fwd(gp, carry_ref[7:8, :], carry_ref[6:7, :], cw_ref, cb_ref)
        carry_ref[...] = gp[tm - 8:tm, :]
        up = _mm_nt(h, wu_ref[...])
        up_ref[...] = up.astype(BF16)
        a = (gate * _sigmoid(gate) * up).astype(BF16)
        x2_ref[...] = xv + _mm(a, wd_ref[...])

    def blk(c):
        return pl.BlockSpec((tm, c), lambda i: (i, 0))

    return pl.pallas_call(
        body, name="ffn_fwd", grid=(t // tm,),
        in_specs=[blk(D_MODEL), _full((1, D_MODEL)), _full((D_FF, D_MODEL)), _full((D_FF, D_MODEL)),
                  _full((D_FF, D_MODEL)), _full((3, D_FF)), _full((1, D_FF))],
        out_specs=[blk(D_FF), blk(D_FF), blk(D_MODEL), blk(D_MODEL)],
        out_shape=[jax.ShapeDtypeStruct((t, D_FF), BF16), jax.ShapeDtypeStruct((t, D_FF), BF16),
                   jax.ShapeDtypeStruct((t, D_MODEL), BF16), jax.ShapeDtypeStruct((t, D_MODEL), F32)],
        scratch_shapes=[pltpu.VMEM((8, D_FF), F32)],
        compiler_params=_cparams("arbitrary"),
    )(x1, g_ffn, w_gate_t, w_up_t, w_down, fcw, fcb)


def _ple_fwd_bwd(x2, p, target, g_ple, w_pg, w_pp, tm):
    t = x2.shape[0]

    def body(x_ref, p_ref, t_ref, g_ref, wg_ref, wp_ref, dx_ref, dxb_ref, loss_ref, dwg_ref, dwp_ref, dg_ref):
        @pl.when(pl.program_id(0) == 0)
        def _():
            loss_ref[...] = jnp.zeros_like(loss_ref)
            dwg_ref[...] = jnp.zeros_like(dwg_ref)
            dwp_ref[...] = jnp.zeros_like(dwp_ref)
            dg_ref[...] = jnp.zeros_like(dg_ref)

        xv = x_ref[...]
        r, xhat = _rms_stats(xv)
        g = g_ref[...]
        h = (xhat * g).astype(BF16)
        pg = _sigmoid(_mm(h, wg_ref[...]))
        pb = p_ref[...].astype(BF16)
        pp = _mm(pb, wp_ref[...])
        err = xv + pg * pp - t_ref[...]
        loss_ref[...] += 0.5 * jnp.sum(jnp.mean(err * err, axis=-1, keepdims=True))
        dx3 = err * (1.0 / D_MODEL)
        d_pp = (dx3 * pg).astype(BF16)
        d_pre = (dx3 * pp * pg * (1.0 - pg)).astype(BF16)
        dwp_ref[...] += _mm_tn(pb, d_pp)
        dwg_ref[...] += _mm_tn(h, d_pre)
        dh = _mm_nt(d_pre, wg_ref[...])
        dg_ref[...] += jnp.sum(dh * xhat, axis=0, keepdims=True)
        dx2 = dx3 + _rms_bwd(dh, xhat, r, g)
        dx_ref[...] = dx2
        dxb_ref[...] = dx2.astype(BF16)

    def blk(c):
        return pl.BlockSpec((tm, c), lambda i: (i, 0))

    return pl.pallas_call(
        body, name="ple_fwd_bwd", grid=(t // tm,),
        in_specs=[blk(D_MODEL), blk(PLE_DIM), blk(D_MODEL), _full((1, D_MODEL)), _full((D_MODEL, D_MODEL)),
                  _full((PLE_DIM, D_MODEL))],
        out_specs=[blk(D_MODEL), blk(D_MODEL), _full((8, 128)), _full((D_MODEL, D_MODEL)),
                   _full((PLE_DIM, D_MODEL)), _full((1, D_MODEL))],
        out_shape=[jax.ShapeDtypeStruct((t, D_MODEL), F32), jax.ShapeDtypeStruct((t, D_MODEL), BF16),
                   jax.ShapeDtypeStruct((8, 128), F32),
                   jax.ShapeDtypeStruct((D_MODEL, D_MODEL), F32), jax.ShapeDtypeStruct((PLE_DIM, D_MODEL), F32),
                   jax.ShapeDtypeStruct((1, D_MODEL), F32)],
        compiler_params=_cparams("arbitrary"),
    )(x2, p, target, g_ple, w_pg, w_pp)


def _ffn_bwd(dx2, h2, gp, up, w_gate, w_up, w_down, fcw, fcb, tm):
    t = dx2.shape[0]
    nblk = t // tm
    fc = D_FF // FF_CHUNKS
    half = tm // FFN_BWD_PARTS

    def body(dx_ref, h_ref, gp_ref, gph_ref, up_ref, wg_ref, wu_ref, wd_ref, cw_ref, cb_ref,
             dh_ref, dwd_hbm, dwu_hbm, dwg_hbm, dcw_ref, dcb_ref, carry_ref, a_scr, dup_scr, dgp_scr,
             dwd_acc, dwu_acc, dwg_acc, stage, stage_sem):
        i = pl.program_id(1)

        @pl.when(i == 0)
        def _():
            carry_ref[...] = jnp.zeros_like(carry_ref)
            dwd_acc[...] = jnp.zeros_like(dwd_acc)
            dwu_acc[...] = jnp.zeros_like(dwu_acc)
            dwg_acc[...] = jnp.zeros_like(dwg_acc)
            dcw_ref[...] = jnp.zeros_like(dcw_ref)
            dcb_ref[...] = jnp.zeros_like(dcb_ref)

        keep = (i < nblk - 1).astype(F32)
        later = carry_ref[...]
        for hf in reversed(range(FFN_BWD_PARTS)):
            rows = slice(hf * half, (hf + 1) * half)
            dxb = dx_ref[rows, :]
            gp_v = gp_ref[rows, :].astype(F32)
            if hf > 0:
                before = gp_ref[hf * half - 16:hf * half, :].astype(F32)
            else:
                before = gph_ref[...].astype(F32) * keep
            gate, gp1, gp2 = _conv_fwd(gp_v, before[15:16, :], before[14:15, :], cw_ref, cb_ref)
            s = _sigmoid(gate)
            silu = gate * s
            up_v = up_ref[rows, :].astype(F32)
            da = _mm_nt(dxb, wd_ref[...])
            a_scr[rows, :] = (silu * up_v).astype(BF16)
            d_up = (da * silu).astype(BF16)
            dup_scr[rows, :] = d_up
            d_gate = da * up_v * (s * (1.0 + gate * (1.0 - s)))
            d_gp = _conv_bwd_input(d_gate, later[0:1, :], later[1:2, :], cw_ref).astype(BF16)
            dgp_scr[rows, :] = d_gp
            later = d_gate[0:8, :]
            dcw_ref[0:1, :] += jnp.sum(d_gate * gp2, axis=0, keepdims=True)
            dcw_ref[1:2, :] += jnp.sum(d_gate * gp1, axis=0, keepdims=True)
            dcw_ref[2:3, :] += jnp.sum(d_gate * gp_v, axis=0, keepdims=True)
            dcb_ref[...] += jnp.sum(d_gate, axis=0, keepdims=True)
            dh_ref[rows, :] = (_mm(d_gp, wg_ref[...]) + _mm(d_up, wu_ref[...])).astype(BF16)
        carry_ref[...] = later
        dwd_acc[...] += _mm_tn(a_scr[...], dx_ref[...])
        dwu_acc[...] += _mm_tn(h_ref[...], dup_scr[...])
        dwg_acc[...] += _mm_tn(h_ref[...], dgp_scr[...])

        @pl.when(i == nblk - 1)
        def _():
            rows = pl.ds(pl.multiple_of(pl.program_id(0) * fc, 16), fc)
            for acc, out, flip in ((dwd_acc, dwd_hbm, False), (dwu_acc, dwu_hbm, True), (dwg_acc, dwg_hbm, True)):
                stage[...] = (acc[...].T if flip else acc[...]).astype(BF16)
                copy = pltpu.make_async_copy(stage, out.at[rows, :], stage_sem)
                copy.start()
                copy.wait()

    def rev(i):
        return nblk - 1 - i

    one = pl.Buffered(1)
    in_specs = [
        pl.BlockSpec((tm, D_MODEL), lambda j, i: (rev(i), 0)),
        pl.BlockSpec((tm, D_MODEL), lambda j, i: (rev(i), 0)),
        pl.BlockSpec((tm, fc), lambda j, i: (rev(i), j)),
        pl.BlockSpec((16, fc), lambda j, i: (jnp.maximum(rev(i) * (tm // 16) - 1, 0), j)),
        pl.BlockSpec((tm, fc), lambda j, i: (rev(i), j)),
        pl.BlockSpec((fc, D_MODEL), lambda j, i: (j, 0), pipeline_mode=one),
        pl.BlockSpec((fc, D_MODEL), lambda j, i: (j, 0), pipeline_mode=one),
        pl.BlockSpec((fc, D_MODEL), lambda j, i: (j, 0), pipeline_mode=one),
        pl.BlockSpec((3, fc), lambda j, i: (0, j)),
        pl.BlockSpec((1, fc), lambda j, i: (0, j)),
    ]
    out_specs = [
        pl.BlockSpec((None, tm, D_MODEL), lambda j, i: (j, rev(i), 0)),
        ANY, ANY, ANY,
        pl.BlockSpec((3, fc), lambda j, i: (0, j)),
        pl.BlockSpec((1, fc), lambda j, i: (0, j)),
    ]
    return pl.pallas_call(
        body, name="ffn_bwd", grid=(FF_CHUNKS, nblk), in_specs=in_specs, out_specs=out_specs,
        out_shape=[jax.ShapeDtypeStruct((FF_CHUNKS, t, D_MODEL), BF16), jax.ShapeDtypeStruct((D_FF, D_MODEL), BF16),
                   jax.ShapeDtypeStruct((D_FF, D_MODEL), BF16), jax.ShapeDtypeStruct((D_FF, D_MODEL), BF16),
                   jax.ShapeDtypeStruct((3, D_FF), F32), jax.ShapeDtypeStruct((1, D_FF), F32)],
        scratch_shapes=[pltpu.VMEM((8, fc), F32), pltpu.VMEM((tm, fc), BF16), pltpu.VMEM((tm, fc), BF16),
                        pltpu.VMEM((tm, fc), BF16), pltpu.VMEM((fc, D_MODEL), F32), pltpu.VMEM((D_MODEL, fc), F32),
                        pltpu.VMEM((D_MODEL, fc), F32), pltpu.VMEM((fc, D_MODEL), BF16), pltpu.SemaphoreType.DMA],
        compiler_params=_cparams("arbitrary", "arbitrary", vmem=V7X_VMEM_LIMIT_LARGE),
    )(dx2, h2, gp, gp, up, w_gate, w_up, w_down, fcw, fcb)


def _outproj_bwd(dh2, dx2, x1, g_ffn, w_out, yc, ya, goc, goa, zconv, conv_w, conv_b, bd, tm):
    t = x1.shape[0]
    nblk = t // tm

    def body(dh_ref, dx2_ref, x1_ref, g_ref, w_ref, yc_ref, ya_ref, goc_ref, goa_ref, zc_ref, zch_ref, cw_ref, cb_ref,
             bd_ref, dx1_ref, dya_ref, dd_ref, dzc_ref, dw_ref, dg_ref, dgoc_ref, dgoa_ref, dcw_ref, dcb_ref,
             carry_ref):
        i = pl.program_id(0)

        @pl.when(i == 0)
        def _():
            carry_ref[...] = jnp.zeros_like(carry_ref)
            for ref in (dw_ref, dg_ref, dgoc_ref, dgoa_ref, dcw_ref, dcb_ref):
                ref[...] = jnp.zeros_like(ref)

        keep = (i < nblk - 1).astype(F32)
        dh2_v = dh_ref[0].astype(F32)
        for j in range(1, FF_CHUNKS):
            dh2_v = dh2_v + dh_ref[j].astype(F32)
        r, xhat = _rms_stats(x1_ref[...])
        dg_ref[...] += jnp.sum(dh2_v * xhat, axis=0, keepdims=True)
        dx1 = dx2_ref[...] + _rms_bwd(dh2_v, xhat, r, g_ref[...])
        dx1_ref[...] = dx1
        dx1b = dx1.astype(BF16)
        dy = _mm_nt(dx1b, w_ref[...])

        yc_v = yc_ref[...].astype(F32)
        rc, ychat = _rms_stats(yc_v)
        dw_ref[0:CONV_W, :] += _mm_tn((ychat * goc_ref[...]).astype(BF16), dx1b)
        dyc = dy[:, 0:CONV_W]
        dgoc_ref[...] += jnp.sum(dyc * ychat, axis=0, keepdims=True)
        d_yc = _rms_bwd(dyc, ychat, rc, goc_ref[...])

        ya_v = ya_ref[...].astype(F32)
        ra, yahat = _rms_stats(ya_v)
        dw_ref[CONV_W:, :] += _mm_tn((yahat * goa_ref[...]).astype(BF16), dx1b)
        dya = dy[:, CONV_W:]
        dgoa_ref[...] += jnp.sum(dya * yahat, axis=0, keepdims=True)
        d_ya = _rms_bwd(dya, yahat, ra, goa_ref[...])
        dya_ref[...] = d_ya
        dd_ref[...] = _seg_sum64(d_ya * ya_v, bd_ref)

        zb = zc_ref[:, 0:CONV_W].astype(F32)
        zc = zc_ref[:, CONV_W:2 * CONV_W].astype(F32)
        zx = zc_ref[:, 2 * CONV_W:3 * CONV_W].astype(F32)
        u = zc * zx
        uh = (zch_ref[:, CONV_W:2 * CONV_W].astype(F32) * zch_ref[:, 2 * CONV_W:3 * CONV_W].astype(F32)) * keep
        cv, u1, u2 = _conv_fwd(u, uh[15:16, :], uh[14:15, :], cw_ref, cb_ref)
        d_cv = d_yc * zb
        d_u = _conv_bwd_input(d_cv, carry_ref[0:1, :], carry_ref[1:2, :], cw_ref)
        carry_ref[...] = d_cv[0:8, :]
        dcw_ref[0:1, :] += jnp.sum(d_cv * u2, axis=0, keepdims=True)
        dcw_ref[1:2, :] += jnp.sum(d_cv * u1, axis=0, keepdims=True)
        dcw_ref[2:3, :] += jnp.sum(d_cv * u, axis=0, keepdims=True)
        dcb_ref[...] += jnp.sum(d_cv, axis=0, keepdims=True)
        dzc_ref[:, 0:CONV_W] = (d_yc * cv).astype(BF16)
        dzc_ref[:, CONV_W:2 * CONV_W] = (d_u * zx).astype(BF16)
        dzc_ref[:, 2 * CONV_W:3 * CONV_W] = (d_u * zc).astype(BF16)

    def rev(i):
        return nblk - 1 - i

    def blk(c):
        return pl.BlockSpec((tm, c), lambda i: (rev(i), 0))

    in_specs = [
        pl.BlockSpec((FF_CHUNKS, tm, D_MODEL), lambda i: (0, rev(i), 0)),
        blk(D_MODEL), blk(D_MODEL), _full((1, D_MODEL)), _full((D_MODEL, D_MODEL)),
        blk(CONV_W), blk(ATTN_W), _full((1, CONV_W)), _full((1, ATTN_W)),
        blk(3 * CONV_W),
        pl.BlockSpec((16, 3 * CONV_W), lambda i: (jnp.maximum(rev(i) * (tm // 16) - 1, 0), 0)),
        _full((3, CONV_W)), _full((1, CONV_W)), _full((256, 256)),
    ]
    out_specs = [blk(D_MODEL), blk(ATTN_W), blk(ATTN_W), blk(3 * CONV_W), _full((D_MODEL, D_MODEL)),
                 _full((1, D_MODEL)), _full((1, CONV_W)), _full((1, ATTN_W)), _full((3, CONV_W)), _full((1, CONV_W))]
    return pl.pallas_call(
        body, name="outproj_bwd", grid=(nblk,), in_specs=in_specs, out_specs=out_specs,
        out_shape=[jax.ShapeDtypeStruct((t, D_MODEL), F32), jax.ShapeDtypeStruct((t, ATTN_W), F32),
                   jax.ShapeDtypeStruct((t, ATTN_W), F32), jax.ShapeDtypeStruct((t, 3 * CONV_W), BF16),
                   jax.ShapeDtypeStruct((D_MODEL, D_MODEL), F32), jax.ShapeDtypeStruct((1, D_MODEL), F32),
                   jax.ShapeDtypeStruct((1, CONV_W), F32), jax.ShapeDtypeStruct((1, ATTN_W), F32),
                   jax.ShapeDtypeStruct((3, CONV_W), F32), jax.ShapeDtypeStruct((1, CONV_W), F32)],
        scratch_shapes=[pltpu.VMEM((8, CONV_W), F32)],
        compiler_params=_cparams("arbitrary"),
    )(dh2, dx2, x1, g_ffn, w_out, yc, ya, goc, goa, zconv, zconv, conv_w, conv_b, bd)


def _attn_bwd(q, k, v, dya, lse, dd, e_all, m_all, after):
    t = q.shape[0]
    nsb = t // SUPER

    def body(q_ref, kc_ref, kp_ref, vc_ref, vp_ref, dy_ref, l_ref, d_ref, e_ref, m_ref, after_ref,
             dq_ref, dk_ref, dv_ref, kk, vv, dkacc, dvacc, dwide):
        s = pl.program_id(1)

        @pl.when(s == 0)
        def _():
            dkacc[...] = jnp.zeros_like(dkacc)
            dvacc[...] = jnp.zeros_like(dvacc)

        dkacc[0:SUPER, :] = dkacc[SUPER:, :]
        dvacc[0:SUPER, :] = dvacc[SUPER:, :]
        dkacc[SUPER:, :] = jnp.zeros((SUPER, QK_BLOCK), F32)
        dvacc[SUPER:, :] = jnp.zeros((SUPER, QK_BLOCK), F32)

        @pl.when(s < nsb)
        def _():
            kk[0:SUPER, :] = kp_ref[...]
            kk[SUPER:, :] = kc_ref[...]
            vv[0:SUPER, :] = vp_ref[...]
            vv[SUPER:, :] = vc_ref[...]
            head0 = lax.broadcasted_iota(jnp.int32, (QK_BLOCK, QK_BLOCK), 1) < HEAD_DIM

            def widened(a):
                other = pltpu.roll(a, HEAD_DIM, 1)
                first = lax.broadcasted_iota(jnp.int32, a.shape, 1) < HEAD_DIM
                return jnp.where(first, a, other), jnp.where(first, other, a)

            def stacked(h0, h1):
                return jnp.concatenate([jnp.concatenate([h0, h0], axis=1), jnp.concatenate([h1, h1], axis=1)], axis=0)

            def widen_dd(i, carry):
                rows = pl.ds(pl.multiple_of(i * 256, 256), 256)
                dwide[0, rows, :], dwide[1, rows, :] = widened(d_ref[rows, :])
                return carry

            lax.fori_loop(0, SUPER // 256, widen_dd, 0)

            for b, dil in enumerate(DILATIONS):
                def unit(u, carry, b=b, dil=dil):
                    start = _unit_start(u, dil)
                    first_key = SUPER + start - QK_BLOCK * dil
                    qrows = _rows(start, QK_BLOCK, dil)
                    krows = _rows(first_key, KEYS, dil)
                    q2 = _stack_heads(q_ref[qrows, :].astype(BF16), head0)
                    dy2 = _stack_heads(dy_ref[qrows, :].astype(BF16), head0)
                    g2 = stacked(*widened(jnp.exp(m_ref[b, qrows, :] - l_ref[qrows, :])))
                    d2 = stacked(dwide[0, qrows, :], dwide[1, qrows, :])
                    k2 = kk[krows, :].astype(BF16)
                    v2 = vv[krows, :].astype(BF16)
                    prob = e_ref[b * UNITS + u].astype(F32) * g2
                    ds = (prob * (_mm_nt(dy2, v2) - d2)).astype(BF16)
                    dvacc[krows, :] += _mm_tn(prob.astype(BF16), dy2)
                    dkacc[krows, :] += _mm_tn(ds, q2)
                    dq2 = _mm(ds, k2)
                    dq = jnp.where(head0, dq2[0:QK_BLOCK], dq2[QK_BLOCK:]) * ATTN_SCALE
                    if b == 0:
                        dq_ref[qrows, :] = dq
                    else:
                        dq_ref[qrows, :] += dq
                    return carry

                lax.fori_loop(0, UNITS, unit, 0, unroll=8)

        dk_ref[...] = dkacc[0:SUPER, :]
        dv_ref[...] = dvacc[0:SUPER, :].astype(BF16)

    def cur_map(p, s):
        return (jnp.minimum(s, nsb - 1), p)

    def prev_map(p, s):
        return (jnp.clip(s - 1, 0, nsb - 1), p)

    cur = pl.BlockSpec((SUPER, QK_BLOCK), cur_map)
    prev = pl.BlockSpec((SUPER, QK_BLOCK), prev_map)
    return pl.pallas_call(
        body, name="attn_bwd", grid=(4, nsb + 1),
        in_specs=[cur, cur, prev, cur, prev, cur, cur, cur,
                  pl.BlockSpec((None, None, 3 * UNITS, KEYS, KEYS), lambda p, s: (p, jnp.minimum(s, nsb - 1), 0, 0, 0)),
                  pl.BlockSpec((3, SUPER, QK_BLOCK), lambda p, s: (0, jnp.minimum(s, nsb - 1), p)),
                  pl.BlockSpec(memory_space=pl.ANY)],
        out_specs=[cur, prev, prev],
        out_shape=[jax.ShapeDtypeStruct((t, ATTN_W), F32), jax.ShapeDtypeStruct((t, ATTN_W), F32),
                   jax.ShapeDtypeStruct((t, ATTN_W), BF16)],
        scratch_shapes=[pltpu.VMEM((2 * SUPER, QK_BLOCK), F32)] * 4 + [pltpu.VMEM((2, SUPER, QK_BLOCK), F32)],
        compiler_params=_cparams("parallel", "arbitrary"),
    )(q, k, k, v, v, dya, lse, dd, e_all, m_all, after)


def _inproj_bwd(dq, dk, dv, dzconv, zqk, x, dx1, g_mix, w_in, qg, kg, bd, tm):
    t = x.shape[0]
    nblk = t // tm
    shard = IN_COLS // N_DEV

    def body(dq_ref, dk_ref, dv_ref, dzc_ref, zqk_ref, x_ref, dx1_ref, g_ref, w_ref, qg_ref,
             kg_ref, bd_ref, dx_ref, dw_hbm, dg_ref, dqg_ref, dkg_ref, dw_ref, stage, stage_sem):
        @pl.when(pl.program_id(0) == 0)
        def _():
            for ref in (dw_ref, dg_ref, dqg_ref, dkg_ref):
                ref[...] = jnp.zeros_like(ref)

        parts = [dzc_ref[...]]
        for j, (dn_ref, gain_ref, dgain_ref) in enumerate(((dq_ref, qg_ref, dqg_ref), (dk_ref, kg_ref, dkg_ref))):
            dn = dn_ref[...]
            z = zqk_ref[:, j * ATTN_W:(j + 1) * ATTN_W].astype(F32)
            r = lax.rsqrt(_seg_sum64(z * z, bd_ref) * (1.0 / HEAD_DIM) + EPS)
            zhat = z * r
            dgain_ref[...] += jnp.sum(dn * zhat, axis=0, keepdims=True)
            gd = dn * gain_ref[...]
            parts.append((r * (gd - zhat * (_seg_sum64(gd * zhat, bd_ref) * (1.0 / HEAD_DIM)))).astype(BF16))
        parts.append(dv_ref[...].astype(BF16))
        dz = jnp.concatenate(parts, axis=1)

        r, xhat = _rms_stats(x_ref[...])
        g = g_ref[...]
        dw_ref[...] += _mm_tn((xhat * g).astype(BF16), dz)
        dh = _mm_nt(dz, w_ref[...])
        dg_ref[...] += jnp.sum(dh * xhat, axis=0, keepdims=True)
        dx_ref[...] = dx1_ref[...] + _rms_bwd(dh, xhat, r, g)

        @pl.when(pl.program_id(0) == nblk - 1)
        def _():
            for k in range(N_DEV):
                stage[...] = dw_ref[:, k * shard:(k + 1) * shard].astype(BF16)
                copy = pltpu.make_async_copy(stage, dw_hbm.at[k], stage_sem)
                copy.start()
                copy.wait()

    def blk(c):
        return pl.BlockSpec((tm, c), lambda i: (i, 0))

    return pl.pallas_call(
        body, name="inproj_bwd", grid=(nblk,),
        in_specs=[blk(ATTN_W)] * 3 + [blk(3 * CONV_W), blk(2 * ATTN_W), blk(D_MODEL), blk(D_MODEL), _full((1, D_MODEL)),
                                      _full((D_MODEL, IN_COLS)), _full((1, ATTN_W)), _full((1, ATTN_W)),
                                      _full((256, 256))],
        out_specs=[blk(D_MODEL), ANY, _full((1, D_MODEL)), _full((1, ATTN_W)), _full((1, ATTN_W))],
        out_shape=[jax.ShapeDtypeStruct((t, D_MODEL), F32), jax.ShapeDtypeStruct((N_DEV, D_MODEL, shard), BF16),
                   jax.ShapeDtypeStruct((1, D_MODEL), F32), jax.ShapeDtypeStruct((1, ATTN_W), F32),
                   jax.ShapeDtypeStruct((1, ATTN_W), F32)],
        scratch_shapes=[pltpu.VMEM((D_MODEL, IN_COLS), F32), pltpu.VMEM((D_MODEL, shard), BF16),
                        pltpu.SemaphoreType.DMA],
        compiler_params=_cparams("arbitrary"),
    )(dq, dk, dv, dzconv, zqk, x, dx1, g_mix, w_in, qg, kg, bd)


def _ordered_after(a, token):
    return a if token is None else a + token[0:1, 0:1].reshape((1,) * a.ndim)


def _local_step(x, p, target, w, tms, hooks=None):
    hooks = hooks or {}
    bd = jnp.kron(jnp.eye(4, dtype=F32), jnp.ones((HEAD_DIM, HEAD_DIM), F32)).astype(BF16)
    qg = jnp.tile(w["q_norm_g"], (1, 8))
    kg = jnp.tile(w["k_norm_g"], (1, 8))
    slopes = jnp.exp2(-jnp.arange(1, 9, dtype=F32))
    slopes = jnp.broadcast_to(slopes.reshape(4, 2, 1), (4, 2, QK_BLOCK))

    zconv, zqk, yc, q, k, v = _inproj_fwd(x, w["g_mix"], w["w_in"], w["conv_w"], w["conv_b"], qg, kg, bd, tms[0])
    ya, lse, e_all, m_all = _attn_fwd(q, k, v, slopes)
    if "late_weights" in hooks:
        w = {**w, **hooks["late_weights"](lse)}
    x1 = _outproj_fwd(ya, yc, x, w["g_out_conv"], w["g_out_attn"], w["w_out"], tms[0])
    gp, up, h2, x2 = _ffn_fwd(x1, w["g_ffn"], w["w_gate"], w["w_up"], w["w_down"], w["ffn_conv_w"], w["ffn_conv_b"],
                              tms[1])
    dx2, dx2b, loss, dw_pg, dw_pp, dg_ple = _ple_fwd_bwd(x2, p, target, w["g_ple"], w["w_ple_gate"], w["w_ple_proj"], tms[0])
    dh2, dw_down, dw_up, dw_gate, dfcw, dfcb = _ffn_bwd(dx2b, h2, gp, up, w["w_gate"], w["w_up"], w["w_down"],
                                                        w["ffn_conv_w"], w["ffn_conv_b"], tms[0])
    token = None
    if "ffn_grads" in hooks:
        token = hooks["ffn_grads"]({"w_ple_gate": dw_pg, "w_ple_proj": dw_pp, "w_down": dw_down, "w_up": dw_up,
                                    "w_gate": dw_gate})
    dx1, dya, dd, dzconv, dw_out, dg_ffn, dgoc, dgoa, dcw, dcb = _outproj_bwd(
        dh2, dx2, x1, _ordered_after(w["g_ffn"], token), w["w_out"], yc, ya, w["g_out_conv"], w["g_out_attn"], zconv,
        w["conv_w"], w["conv_b"], bd, tms[1])
    token = hooks["outproj_done"](dx1) if "outproj_done" in hooks else None
    dq, dk, dv = _attn_bwd(q, k, v, dya, lse, dd, e_all, m_all, slopes if token is None else token)
    dx, dw_in, dg_mix, dqg, dkg = _inproj_bwd(dq, dk, dv, dzconv, zqk, x, dx1, w["g_mix"], w["w_in"], qg, kg, bd,
                                              tms[0])
    grads = {
        "g_mix": dg_mix, "w_in": dw_in, "conv_w": dcw, "conv_b": dcb,
        "q_norm_g": dqg.reshape(8, HEAD_DIM).sum(0, keepdims=True),
        "k_norm_g": dkg.reshape(8, HEAD_DIM).sum(0, keepdims=True),
        "g_out_conv": dgoc, "g_out_attn": dgoa, "w_out": dw_out, "g_ffn": dg_ffn, "w_gate": dw_gate, "w_up": dw_up,
        "ffn_conv_w": dfcw, "ffn_conv_b": dfcb, "w_down": dw_down, "g_ple": dg_ple, "w_ple_gate": dw_pg,
        "w_ple_proj": dw_pp,
    }
    return loss, dx, grads


ANY = pl.BlockSpec(memory_space=pl.ANY)
MESH = pl.DeviceIdType.MESH


def _all_gather(shards, name):
    n = len(shards)

    def body(*refs):
        ins, outs = refs[:n], refs[n:2 * n]
        send_sems, recv_sems, local_sems = refs[2 * n:]
        x, y, c = lax.axis_index("x"), lax.axis_index("y"), lax.axis_index("c")
        me, sibling = (x, y, c), (x, y, 1 - c)
        chips = [(1 - x, y), (x, 1 - y), (1 - x, 1 - y)]

        def slot(dev):
            return 4 * dev[0] + 2 * dev[1] + dev[2]

        def copy(b, k, block, to, src=None):
            dst = outs[b].at[slot(block)]
            return pltpu.make_async_remote_copy(
                src_ref=dst if src is None else src, dst_ref=dst, send_sem=send_sems.at[b, k],
                recv_sem=recv_sems.at[b, k], device_id=to, device_id_type=MESH)

        mine = [pltpu.make_async_copy(ins[b], outs[b].at[slot(me)], local_sems.at[b]) for b in range(n)]
        first, passed = [], []
        for b in range(n):
            mine[b].start()
            first.append(copy(b, 0, me, sibling, src=ins[b]))
            first += [copy(b, 1 + j, me, (*chip, c), src=ins[b]) for j, chip in enumerate(chips)]
        for cp in first:
            cp.start()
        for j, chip in enumerate(chips):
            for b in range(n):
                copy(b, 1 + j, (*chip, c), me).wait_recv()
                fwd = copy(b, 4 + j, (*chip, c), sibling)
                fwd.start()
                passed.append(fwd)
        for b in range(n):
            copy(b, 0, sibling, me).wait_recv()
            for j, chip in enumerate(chips):
                copy(b, 4 + j, (*chip, 1 - c), me).wait_recv()
        for cp in first + passed:
            cp.wait_send()
        for cp in mine:
            cp.wait()

    return pl.pallas_call(
        body, name=name,
        in_specs=[ANY] * n, out_specs=[ANY] * n,
        out_shape=[jax.ShapeDtypeStruct((N_DEV,) + s.shape, s.dtype) for s in shards],
        scratch_shapes=[pltpu.SemaphoreType.DMA((n, 7)), pltpu.SemaphoreType.DMA((n, 7)),
                        pltpu.SemaphoreType.DMA((n,))],
    )(*shards)


HBM = pl.BlockSpec(memory_space=pltpu.HBM)
SEM = pl.BlockSpec(memory_space=pltpu.SEMAPHORE)
EFFECT = pltpu.SideEffectType.DATAFLOW_SIDE_EFFECTING
FLIPS = ((0, 0, 1), (0, 1, 0), (0, 1, 1), (1, 0, 0), (1, 0, 1), (1, 1, 0), (1, 1, 1))


def _flip_peers():
    pos = (lax.axis_index("x"), lax.axis_index("y"), lax.axis_index("c"))
    return [tuple(1 - a if f else a for a, f in zip(pos, flip)) for flip in FLIPS]


def _hbm(a):
    return pltpu.with_memory_space_constraint(a, pltpu.HBM)


def _split_start(name, srcs, lands, plan, n_copies, after):
    n, m = len(srcs), len(lands)

    def body(*refs):
        send_sems, recv_sems, token = refs[n + m + 1], refs[n + m + 2], refs[-1]
        for i, (src, dst, peer) in enumerate(plan(refs[:n], refs[n:n + m])):
            pltpu.make_async_remote_copy(src_ref=src, dst_ref=dst, send_sem=send_sems.at[i], recv_sem=recv_sems.at[i],
                                         device_id=peer, device_id_type=MESH).start()
        token[...] = jnp.zeros_like(token)

    outs = pl.pallas_call(
        body, name=name + "_start",
        in_specs=[HBM] * (n + m) + [ANY],
        out_specs=[SEM, SEM] + [HBM] * (n + m) + [pl.BlockSpec(memory_space=pltpu.VMEM)],
        out_shape=[pltpu.SemaphoreType.DMA((n_copies,)), pltpu.SemaphoreType.DMA((n_copies,))]
        + [pltpu.HBM(a.shape, a.dtype) for a in list(srcs) + list(lands)] + [jax.ShapeDtypeStruct((8, 128), F32)],
        input_output_aliases={i: 2 + i for i in range(n + m)},
        compiler_params=pltpu.CompilerParams(has_side_effects=EFFECT),
    )(*[_hbm(a) for a in list(srcs) + list(lands)], after)
    return (outs[0], outs[1], outs[2:2 + n], outs[2 + n:2 + n + m]), outs[-1]


def _split_wait(name, started, plan, after):
    send_sems, recv_sems, srcs, lands = started
    n, m = len(srcs), len(lands)

    def body(*refs):
        send_ref, recv_ref = refs[n + m], refs[n + m + 1]
        for i, (src, dst, peer) in enumerate(plan(refs[:n], refs[n:n + m])):
            copy = pltpu.make_async_remote_copy(src_ref=src, dst_ref=dst, send_sem=send_ref.at[i],
                                                recv_sem=recv_ref.at[i], device_id=peer, device_id_type=MESH)
            copy.wait_send()
            copy.wait_recv()

    outs = pl.pallas_call(
        body, name=name + "_wait",
        in_specs=[HBM] * (n + m) + [SEM, SEM, ANY],
        out_specs=[HBM] * (n + m),
        out_shape=[pltpu.HBM(a.shape, a.dtype) for a in list(srcs) + list(lands)],
        input_output_aliases={i: i for i in range(n + m)},
        compiler_params=pltpu.CompilerParams(has_side_effects=EFFECT),
    )(*srcs, *lands, send_sems, recv_sems, after)
    return outs[:n], outs[n:]


def _gather_plan(srcs, lands):
    slot = 4 * lax.axis_index("x") + 2 * lax.axis_index("y") + lax.axis_index("c")
    return [(src, land.at[slot], peer) for src, land in zip(srcs, lands) for peer in _flip_peers()]


def _sibling_plan(srcs, lands):
    x, y, c = lax.axis_index("x"), lax.axis_index("y"), lax.axis_index("c")
    return [(src.at[k, 1 - c], land.at[k], (x, y, 1 - c)) for src, land in zip(srcs, lands) for k in range(N_CHIP)]


def _chip_plan(srcs, lands):
    x, y, c = lax.axis_index("x"), lax.axis_index("y"), lax.axis_index("c")
    return [(src.at[2 * cx + cy], land.at[2 * x + y], (cx, cy, c))
            for src, land in zip(srcs, lands) for cx, cy in ((1 - x, y), (x, 1 - y), (1 - x, 1 - y))]


def _row_tile(rows):
    for tr in range(min(rows, 512), 15, -16):
        if rows % tr == 0:
            return tr
    return rows


def _pair_sums(gs, lands, core, name):
    n = len(gs)

    def body(c_ref, *refs):
        for b in range(n):
            out = refs[2 * n + b]
            out[...] = (refs[b][...].astype(F32) + refs[n + b][...].astype(F32)).astype(out.dtype)

    def slab(a):
        return pl.BlockSpec((None,) + a.shape[1:], lambda k, c_ref: (k, 0, 0))

    return pl.pallas_call(
        body, name=name,
        grid_spec=pltpu.PrefetchScalarGridSpec(
            num_scalar_prefetch=1, grid=(N_CHIP,),
            in_specs=[pl.BlockSpec((None, None) + g.shape[2:], lambda k, c_ref: (k, c_ref[0], 0, 0)) for g in gs]
            + [slab(a) for a in lands],
            out_specs=[slab(a) for a in lands]),
        out_shape=[jax.ShapeDtypeStruct(a.shape, a.dtype) for a in lands],
        compiler_params=_cparams("parallel"),
    )(core, *gs, *lands)


def _adamw(own, arrived, chip, w, m, v, name):
    k, rows, cols = arrived.shape
    tr = _row_tile(rows)
    c1 = 1.0 / (1.0 - ADAM_B1 ** ADAM_STEP)
    c2 = 1.0 / (1.0 - ADAM_B2 ** ADAM_STEP)

    def body(chip_ref, o_ref, p_ref, w_ref, m_ref, v_ref, g_ref, d_ref, nm_ref, nv_ref):
        def slab(j):
            return jnp.where(chip_ref[0] == j, o_ref[j], p_ref[j]).astype(F32)

        g = slab(0)
        for j in range(1, k):
            g = g + slab(j)
        g_ref[...] = g
        nm = ADAM_B1 * m_ref[...] + (1.0 - ADAM_B1) * g
        nv = ADAM_B2 * v_ref[...] + (1.0 - ADAM_B2) * (g * g)
        nm_ref[...] = nm
        nv_ref[...] = nv
        d_ref[...] = -ADAM_LR * ((nm * c1) / (jnp.sqrt(nv * c2) + ADAM_EPS) + ADAM_WD * w_ref[...])

    blk = pl.BlockSpec((tr, cols), lambda i, c: (i, 0))
    stack = pl.BlockSpec((k, tr, cols), lambda i, c: (0, i, 0))
    return pl.pallas_call(
        body, name=name,
        grid_spec=pltpu.PrefetchScalarGridSpec(num_scalar_prefetch=1, grid=(rows // tr,),
                                               in_specs=[stack, stack, blk, blk, blk], out_specs=[blk] * 4),
        out_shape=[jax.ShapeDtypeStruct((rows, cols), F32)] * 4,
        compiler_params=_cparams("parallel"),
    )(chip, own, arrived, w, m, v)


SMALL_LAYOUT = (("g_mix", 0, 1024), ("conv_b", 1, 512), ("q_norm_g", 2, 64), ("k_norm_g", 3, 64),
                ("g_out_conv", 4, 512), ("g_out_attn", 5, 512), ("g_ffn", 6, 1024), ("ffn_conv_b", 7, 2816),
                ("g_ple", 10, 1024))
CONV_W_ROW = 11
FFN_CONV_W_ROW = 14
LOSS_ROW = 23


def _row_pieces(cols):
    return [(c, min(1024, cols - c)) for c in range(0, cols, 1024)]


def _pack_small(grads, loss_tile):
    names = [n for n, _, _ in SMALL_LAYOUT]

    def body(*refs):
        ins, cw_ref, fcw_ref, loss_ref, out_ref = refs[:len(names)], refs[-4], refs[-3], refs[-2], refs[-1]
        out_ref[...] = jnp.zeros_like(out_ref)
        for ref, (_, row, cols) in zip(ins, SMALL_LAYOUT):
            for j, (c, width) in enumerate(_row_pieces(cols)):
                out_ref[row + j:row + j + 1, 0:width] = ref[:, c:c + width]
        for k in range(3):
            out_ref[CONV_W_ROW + k:CONV_W_ROW + k + 1, 0:CONV_W] = cw_ref[k:k + 1, :]
            for j, (c, width) in enumerate(_row_pieces(D_FF)):
                row = FFN_CONV_W_ROW + 3 * k + j
                out_ref[row:row + 1, 0:width] = fcw_ref[k:k + 1, c:c + width]
        out_ref[LOSS_ROW:LOSS_ROW + 1, 0:128] = loss_ref[0:1, :]

    return pl.pallas_call(
        body, name="pack_small_grads", out_shape=jax.ShapeDtypeStruct((SMALL_ROWS, 1024), F32),
    )(*[grads[n] for n in names], grads["conv_w"], grads["ffn_conv_w"], loss_tile)


def _adamw_small(arrived, conv_parts, fconv_parts, wts, mom, var):
    names = [n for n, _, _ in SMALL_LAYOUT] + ["conv_w", "ffn_conv_w"]
    c1 = 1.0 / (1.0 - ADAM_B1 ** ADAM_STEP)
    c2 = 1.0 / (1.0 - ADAM_B2 ** ADAM_STEP)
    n = len(names)

    def body(*refs):
        land, cw_ref, fcw_ref = refs[0], refs[1], refs[2]
        state = refs[3:3 + 3 * n]
        outs = refs[3 + 3 * n:]

        def total(piece):
            acc = piece(0)
            for d in range(1, N_DEV):
                acc = acc + piece(d)
            return acc

        for i, name in enumerate(names):
            if name == "conv_w":
                g = total(lambda d: cw_ref[d])
            elif name == "ffn_conv_w":
                g = total(lambda d: fcw_ref[d])
            else:
                _, row, cols = SMALL_LAYOUT[i]
                pieces = [total(lambda d, j=j, width=width: land[d, row + j:row + j + 1, 0:width])
                          for j, (_, width) in enumerate(_row_pieces(cols))]
                g = pieces[0] if len(pieces) == 1 else jnp.concatenate(pieces, axis=1)
            w_ref, m_ref, v_ref = state[3 * i:3 * i + 3]
            nm = ADAM_B1 * m_ref[...] + (1.0 - ADAM_B1) * g
            nv = ADAM_B2 * v_ref[...] + (1.0 - ADAM_B2) * (g * g)
            outs[4 * i][...] = g
            outs[4 * i + 1][...] = -ADAM_LR * ((nm * c1) / (jnp.sqrt(nv * c2) + ADAM_EPS) + ADAM_WD * w_ref[...])
            outs[4 * i + 2][...] = nm
            outs[4 * i + 3][...] = nv
        outs[-1][...] = total(lambda d: land[d, LOSS_ROW:LOSS_ROW + 1, 0:128])

    state = [a[nm_] for nm_ in names for a in (wts, mom, var)]
    shapes = [jax.ShapeDtypeStruct(wts[nm_].shape, F32) for nm_ in names for _ in range(4)]
    outs = pl.pallas_call(
        body, name="adamw_small", out_shape=shapes + [jax.ShapeDtypeStruct((1, 128), F32)],
    )(arrived, conv_parts, fconv_parts, *state)
    return {nm_: tuple(outs[4 * i:4 * i + 4]) for i, nm_ in enumerate(names)}, outs[-1][0, 0]


COL_SHARDED = ("w_in", "w_ple_proj")
TRANSPOSED = ("w_gate", "w_up")
CONV_SHARDED = (("conv_w", CONV_W), ("ffn_conv_w", D_FF))


def _gathered_to_full(name, gathered):
    if name in COL_SHARDED:
        return gathered.transpose(1, 0, 2).reshape(gathered.shape[1], -1)
    return gathered.reshape(-1, gathered.shape[2])


def _full_to_stacked(name, grad, shard_shape):
    sr, sc = shard_shape
    if grad.ndim == 3:
        a = grad
    elif name in COL_SHARDED:
        a = grad.reshape(sr, N_DEV, sc).transpose(1, 0, 2)
    else:
        a = grad.reshape(N_DEV, sr, sc)
    return a.astype(BF16).reshape(N_CHIP, 2, sr, sc)


def _pad_rows(vec, rows):
    return jnp.pad(vec, (0, rows * 1024 - vec.shape[0])).reshape(rows, 1024)


def kernel(x, p, g_mix, w_in, conv_w, conv_b, q_norm_g, k_norm_g, g_out_conv, g_out_attn, w_out, g_ffn, w_gate, w_up, ffn_conv_w, ffn_conv_b, w_down, g_ple, w_ple_gate, w_ple_proj, loss_target, m_g_mix, m_w_in, m_conv_w, m_conv_b, m_q_norm_g, m_k_norm_g, m_g_out_conv, m_g_out_attn, m_w_out, m_g_ffn, m_w_gate, m_w_up, m_ffn_conv_w, m_ffn_conv_b, m_w_down, m_g_ple, m_w_ple_gate, m_w_ple_proj, v_g_mix, v_w_in, v_conv_w, v_conv_b, v_q_norm_g, v_k_norm_g, v_g_out_conv, v_g_out_attn, v_w_out, v_g_ffn, v_w_gate, v_w_up, v_ffn_conv_w, v_ffn_conv_b, v_w_down, v_g_ple, v_w_ple_gate, v_w_ple_proj):
    args = dict(locals())
    names = ["g_mix", "w_in", "conv_w", "conv_b", "q_norm_g", "k_norm_g", "g_out_conv", "g_out_attn", "w_out", "g_ffn",
             "w_gate", "w_up", "ffn_conv_w", "ffn_conv_b", "w_down", "g_ple", "w_ple_gate", "w_ple_proj"]
    big = list(BIG)
    conv = [n for n, _ in CONV_SHARDED]

    def local(prefix):
        out = {n: (args[prefix + n][0] if n in big or n in conv else args[prefix + n]) for n in names}
        out.update({n: out[n].T for n in TRANSPOSED})
        return out

    wts, mom, var = local(""), local("m_"), local("v_")
    shard_shapes = {n: wts[n].shape for n in big}
    dev = 4 * lax.axis_index("x") + 2 * lax.axis_index("y") + lax.axis_index("c")
    core = lax.axis_index("c").astype(jnp.int32).reshape(1)

    conv_local = _pad_rows(jnp.concatenate([wts[n].reshape(-1) for n in conv]), 8).reshape(8, 1024)
    late = [n for n in big if n != "w_in"]
    w_in_all, conv_all = _all_gather([wts["w_in"].astype(BF16), conv_local], "gather_weights")
    late_shards = [wts[n].astype(BF16) for n in late]
    gathering, token = _split_start("gather_late_weights", late_shards,
                                    [lax.empty((N_DEV,) + s.shape, BF16) for s in late_shards], _gather_plan,
                                    7 * len(late), w_in_all)
    full = dict(wts)
    full["w_in"] = _gathered_to_full("w_in", w_in_all)
    full["g_mix"] = _ordered_after(wts["g_mix"], token)
    flying = {}

    def late_weights(after):
        shards, lands = _split_wait("gather_late_weights", gathering, _gather_plan, after)
        return {n: _gathered_to_full(n, lax.dynamic_update_slice(land, shard[None], (dev, 0, 0)))
                for n, land, shard in zip(late, lands, shards)}

    early = ["w_ple_gate", "w_ple_proj", "w_down", "w_up", "w_gate"]

    def ffn_grads(g):
        stacked = [_full_to_stacked(n, g[n], shard_shapes[n]) for n in early]
        flying["sibling"], tok = _split_start("rs_sibling_early", stacked,
                                              [lax.empty((N_CHIP,) + s.shape[2:], BF16) for s in stacked],
                                              _sibling_plan, N_CHIP * len(early), g["w_down"])
        return tok

    def outproj_done(after):
        stacked, landed = _split_wait("rs_sibling_early", flying["sibling"], _sibling_plan, after)
        parts = _pair_sums(stacked, landed, core, "rs_pair_sums_early")
        flying["chip"], tok = _split_start("rs_chip_early", parts, [lax.empty(q.shape, BF16) for q in parts],
                                           _chip_plan, 3 * len(early), landed[0])
        return tok

    off = 0
    for n, width in CONV_SHARDED:
        sc = width // N_DEV
        a = conv_all.reshape(N_DEV, -1)[:, off:off + 3 * sc].reshape(N_DEV, 3, sc)
        full[n] = a.transpose(1, 0, 2).reshape(3, width)
        off += 3 * sc

    loss, dx, grads = _local_step(x[0], p[0, 0], loss_target[0], full, (512, 256),
                                  {"late_weights": late_weights, "ffn_grads": ffn_grads, "outproj_done": outproj_done})

    chip = (2 * lax.axis_index("x") + lax.axis_index("y")).astype(jnp.int32).reshape(1)

    def adamw_of(group, parts, arrived):
        return {n: _adamw(own, got, chip, wts[n], mom[n], var[n], f"adamw_{n}")
                for n, own, got in zip(group, parts, arrived)}

    last = [n for n in big if n not in early]
    stacked = [_full_to_stacked(n, grads[n], shard_shapes[n]) for n in last]
    flying["sibling_last"], tok = _split_start("rs_sibling_last", stacked,
                                               [lax.empty((N_CHIP,) + s.shape[2:], BF16) for s in stacked],
                                               _sibling_plan, N_CHIP * len(last), dx)
    (small_all,) = _all_gather([_ordered_after(_pack_small(grads, loss), tok)], "gather_small_grads")
    stacked, landed = _split_wait("rs_sibling_last", flying["sibling_last"], _sibling_plan, small_all)
    parts = _pair_sums(stacked, landed, core, "rs_pair_sums_last")
    flying["chip_last"], tok = _split_start("rs_chip_last", parts, [lax.empty(q.shape, BF16) for q in parts],
                                            _chip_plan, 3 * len(last), landed[0])

    parts, arrived = _split_wait("rs_chip_early", flying["chip"], _chip_plan, tok)
    out = adamw_of(early, parts, arrived)
    small_all = _ordered_after(small_all, tok)
    taps = small_all[:, CONV_W_ROW:CONV_W_ROW + 3, 0:CONV_W]
    ftaps = small_all[:, FFN_CONV_W_ROW:FFN_CONV_W_ROW + 9, :].reshape(N_DEV, 3, 3 * 1024)
    small_out, loss_total = _adamw_small(
        small_all, lax.dynamic_slice(taps, (0, 0, dev * (CONV_W // N_DEV)), (N_DEV, 3, CONV_W // N_DEV)),
        lax.dynamic_slice(ftaps, (0, 0, dev * (D_FF // N_DEV)), (N_DEV, 3, D_FF // N_DEV)), wts, mom, var)
    out.update(small_out)
    parts, arrived = _split_wait("rs_chip_last", flying["chip_last"], _chip_plan, small_out["g_mix"][0])
    out.update(adamw_of(last, parts, arrived))
    def result(n, which):
        a = out[n][which]
        return (a.T if n in TRANSPOSED else a).reshape(args[n].shape)

    return (loss_total, dx[None], *[result(n, which) for which in range(4) for n in names])
```

```python
import jax
import jax.numpy as jnp
from jax import lax
from jax.experimental import pallas as pl
from jax.experimental.pallas import tpu as pltpu

F32 = jnp.float32
BF16 = jnp.bfloat16

D_MODEL = 1024
CONV_W = 512
ATTN_W = 512
HEAD_DIM = 64
D_FF = 2816
PLE_DIM = 256
IN_COLS = 3 * CONV_W + 3 * ATTN_W
EPS = 1e-6
QK_BLOCK = 128
DILATIONS = (1, 4, 16)
ATTN_SCALE = HEAD_DIM ** -0.5

ADAM_LR = 0.001
ADAM_B1 = 0.9
ADAM_B2 = 0.999
ADAM_EPS = 1e-08
ADAM_WD = 0.01
ADAM_STEP = 10

N_DEV = 8
N_CHIP = 4
V7X_VMEM_LIMIT = 56 * 1024 * 1024
V7X_VMEM_LIMIT_LARGE = 62 * 1024 * 1024
FF_CHUNKS = 2
FFN_BWD_PARTS = 1

BIG = ("w_in", "w_out", "w_gate", "w_up", "w_down", "w_ple_gate", "w_ple_proj")
SMALL_ROWS = 24


def _cparams(*sem, vmem=V7X_VMEM_LIMIT):
    return pltpu.CompilerParams(dimension_semantics=sem, vmem_limit_bytes=vmem)


def _mm(a, b):
    return jnp.dot(a, b, preferred_element_type=F32)


def _mm_nt(a, b):
    return lax.dot_general(a, b, (((1,), (1,)), ((), ())), preferred_element_type=F32)


def _mm_tn(a, b):
    return lax.dot_general(a, b, (((0,), (0,)), ((), ())), preferred_element_type=F32)


def _full(shape):
    nd = len(shape)
    return pl.BlockSpec(shape, lambda *_: (0,) * nd)


def _rms_stats(x):
    r = lax.rsqrt(jnp.mean(x * x, axis=-1, keepdims=True) + EPS)
    return r, x * r


def _rms_bwd(dy, xhat, r, g):
    gd = dy * g
    return r * (gd - xhat * jnp.mean(gd * xhat, axis=-1, keepdims=True))


def _seg_sum64(v, bd_ref):
    outs = []
    for c in range(0, v.shape[1], 256):
        vc = v[:, c:c + 256]
        hi = vc.astype(BF16)
        lo = (vc - hi.astype(F32)).astype(BF16)
        outs.append(_mm(hi, bd_ref[...]) + _mm(lo, bd_ref[...]))
    return outs[0] if len(outs) == 1 else jnp.concatenate(outs, axis=1)


def _shift_rows(u, k, edge_rows):
    out = pltpu.roll(u, k, 0)
    row = lax.broadcasted_iota(jnp.int32, (8, u.shape[1]), 0)
    head = out[0:8]
    for j in range(k):
        head = jnp.where(row == j, edge_rows[k - 1 - j], head)
    return jnp.concatenate([head, out[8:]], axis=0)


def _shift_rows_up(u, k, edge_rows):
    n = u.shape[0]
    out = pltpu.roll(u, n - k, 0)
    row = lax.broadcasted_iota(jnp.int32, (8, u.shape[1]), 0)
    tail = out[n - 8:n]
    for j in range(k):
        tail = jnp.where(row == 8 - k + j, edge_rows[j], tail)
    return jnp.concatenate([out[0:n - 8], tail], axis=0)


def _conv_fwd(u, c1, c2, w_ref, b_ref):
    u1 = _shift_rows(u, 1, (c1,))
    u2 = _shift_rows(u, 2, (c1, c2))
    y = u2 * w_ref[0:1, :] + u1 * w_ref[1:2, :] + u * w_ref[2:3, :] + b_ref[...]
    return y, u1, u2


def _conv_bwd_input(dy, n1row, n2row, w_ref):
    d1 = _shift_rows_up(dy, 1, (n1row,))
    d2 = _shift_rows_up(dy, 2, (n1row, n2row))
    return dy * w_ref[2:3, :] + d1 * w_ref[1:2, :] + d2 * w_ref[0:1, :]


def _sigmoid(x):
    return 1.0 / (1.0 + jnp.exp(-x))


def _inproj_fwd(x, g_mix, w_in, conv_w, conv_b, qg, kg, bd, tm):
    t = x.shape[0]

    def body(x_ref, g_ref, w_ref, cw_ref, cb_ref, qg_ref, kg_ref, bd_ref,
             zc_ref, zqk_ref, yc_ref, q_ref, k_ref, v_ref, carry_ref):
        @pl.when(pl.program_id(0) == 0)
        def _():
            carry_ref[...] = jnp.zeros_like(carry_ref)

        _, xhat = _rms_stats(x_ref[...])
        h = (xhat * g_ref[...]).astype(BF16)
        zconv = _mm(h, w_ref[:, 0:3 * CONV_W])
        zc_ref[...] = zconv.astype(BF16)
        u = zconv[:, CONV_W:2 * CONV_W] * zconv[:, 2 * CONV_W:3 * CONV_W]
        cv, _, _ = _conv_fwd(u, carry_ref[7:8, :], carry_ref[6:7, :], cw_ref, cb_ref)
        yc_ref[...] = (zconv[:, 0:CONV_W] * cv).astype(BF16)
        carry_ref[...] = u[tm - 8:tm, :]

        zqk = _mm(h, w_ref[:, 3 * CONV_W:3 * CONV_W + 2 * ATTN_W])
        zqk_ref[...] = zqk.astype(BF16)
        for j, (gain_ref, out_ref, scale) in enumerate(((qg_ref, q_ref, ATTN_SCALE), (kg_ref, k_ref, 1.0))):
            z = zqk[:, j * ATTN_W:(j + 1) * ATTN_W]
            r = lax.rsqrt(_seg_sum64(z * z, bd_ref) * (1.0 / HEAD_DIM) + EPS)
            out_ref[...] = z * r * gain_ref[...] * scale
        v_ref[...] = _mm(h, w_ref[:, 3 * CONV_W + 2 * ATTN_W:IN_COLS])

    def blk(c):
        return pl.BlockSpec((tm, c), lambda i: (i, 0))

    return pl.pallas_call(
        body, name="inproj_fwd", grid=(t // tm,),
        in_specs=[blk(D_MODEL), _full((1, D_MODEL)), _full((D_MODEL, IN_COLS)), _full((3, CONV_W)),
                  _full((1, CONV_W)), _full((1, ATTN_W)), _full((1, ATTN_W)), _full((256, 256))],
        out_specs=[blk(3 * CONV_W), blk(2 * ATTN_W), blk(CONV_W), blk(ATTN_W), blk(ATTN_W), blk(ATTN_W)],
        out_shape=[jax.ShapeDtypeStruct((t, 3 * CONV_W), BF16), jax.ShapeDtypeStruct((t, 2 * ATTN_W), BF16),
                   jax.ShapeDtypeStruct((t, CONV_W), BF16), jax.ShapeDtypeStruct((t, ATTN_W), F32),
                   jax.ShapeDtypeStruct((t, ATTN_W), F32), jax.ShapeDtypeStruct((t, ATTN_W), F32)],
        scratch_shapes=[pltpu.VMEM((8, CONV_W), F32)],
        compiler_params=_cparams("arbitrary"),
    )(x, g_mix, w_in, conv_w, conv_b, qg, kg, bd)


SUPER = 16 * QK_BLOCK
KEYS = 2 * QK_BLOCK
UNITS = SUPER // QK_BLOCK


def _rows(start, size, dil):
    return pl.ds(start, size) if dil == 1 else pl.ds(start, size, stride=dil)


def _attn_bias(sl_ref, dil):
    qi = lax.broadcasted_iota(jnp.int32, (KEYS, KEYS), 0)
    kj = lax.broadcasted_iota(jnp.int32, (KEYS, KEYS), 1)
    step = jnp.bitwise_and(qi, QK_BLOCK - 1) + QK_BLOCK - kj
    slope = jnp.where(qi < QK_BLOCK, sl_ref[0, 0:1, 0:1], sl_ref[0, 1:2, 0:1])
    bias = jnp.where(jnp.logical_and(step >= 0, step <= QK_BLOCK), -slope * (step * dil).astype(F32), -jnp.inf)
    return bias, kj >= QK_BLOCK


def _unit_start(u, dil):
    if dil == 1:
        return pl.multiple_of(u * QK_BLOCK, QK_BLOCK)
    if dil == 4:
        return jnp.bitwise_and(u, 3) + (u // 4) * (4 * QK_BLOCK)
    return u


def _stack_heads(a, head0):
    zero = jnp.zeros_like(a)
    return jnp.concatenate([jnp.where(head0, a, zero), jnp.where(head0, zero, a)], axis=0)


def _attn_fwd(q, k, v, slopes):
    t = q.shape[0]
    nsb = t // SUPER

    def body(q_ref, kc_ref, kp_ref, vc_ref, vp_ref, sl_ref, o_ref, l_ref, e_ref, m_ref, kk, vv, ob, lb):
        s = pl.program_id(1)
        kk[0:SUPER, :] = kp_ref[...]
        kk[SUPER:, :] = kc_ref[...]
        vv[0:SUPER, :] = vp_ref[...]
        vv[SUPER:, :] = vc_ref[...]
        head0 = lax.broadcasted_iota(jnp.int32, (QK_BLOCK, QK_BLOCK), 1) < HEAD_DIM

        for b, dil in enumerate(DILATIONS):
            bias, own_half = _attn_bias(sl_ref, dil)

            def unit(u, carry, b=b, dil=dil, bias=bias, own_half=own_half):
                start = _unit_start(u, dil)
                first_key = SUPER + start - QK_BLOCK * dil
                q2 = _stack_heads(q_ref[_rows(start, QK_BLOCK, dil), :].astype(BF16), head0)
                k2 = kk[_rows(first_key, KEYS, dil), :].astype(BF16)
                v2 = vv[_rows(first_key, KEYS, dil), :].astype(BF16)
                has_prev = jnp.logical_or(s > 0, start >= QK_BLOCK * dil)
                sc = jnp.where(jnp.logical_or(own_half, has_prev), _mm_nt(q2, k2) + bias, -jnp.inf)
                m = jnp.max(sc, axis=-1, keepdims=True)
                e = jnp.exp(sc - m)
                den = jnp.sum(e, axis=-1, keepdims=True)
                eb = e.astype(BF16)
                e_ref[b * UNITS + u] = eb
                o2 = _mm(eb, v2) / den
                l2 = m + jnp.log(den)
                ob[b, _rows(start, QK_BLOCK, dil), :] = jnp.where(head0, o2[0:QK_BLOCK], o2[QK_BLOCK:])
                lb[b, _rows(start, QK_BLOCK, dil), :] = jnp.where(head0, l2[0:QK_BLOCK], l2[QK_BLOCK:])
                m_ref[b, _rows(start, QK_BLOCK, dil), :] = jnp.where(head0, m[0:QK_BLOCK], m[QK_BLOCK:])
                return carry

            lax.fori_loop(0, UNITS, unit, 0, unroll=16)

        def merge(i, carry):
            rows = pl.ds(pl.multiple_of(i * 256, 256), 256)
            la, lb_, lc = lb[0, rows, :], lb[1, rows, :], lb[2, rows, :]
            mx = jnp.maximum(jnp.maximum(la, lb_), lc)
            wa, wb, wc = jnp.exp(la - mx), jnp.exp(lb_ - mx), jnp.exp(lc - mx)
            sw = wa + wb + wc
            o_ref[rows, :] = ((wa * ob[0, rows, :] + wb * ob[1, rows, :] + wc * ob[2, rows, :]) / sw).astype(BF16)
            l_ref[rows, :] = mx + jnp.log(sw)
            return carry

        lax.fori_loop(0, SUPER // 256, merge, 0)

    cur = pl.BlockSpec((SUPER, QK_BLOCK), lambda p, s: (s, p))
    prev = pl.BlockSpec((SUPER, QK_BLOCK), lambda p, s: (jnp.maximum(s - 1, 0), p))
    return pl.pallas_call(
        body, name="attn_fwd", grid=(4, nsb),
        in_specs=[cur, cur, prev, cur, prev, pl.BlockSpec((1, 2, QK_BLOCK), lambda p, s: (p, 0, 0))],
        out_specs=[cur, cur, pl.BlockSpec((None, None, 3 * UNITS, KEYS, KEYS), lambda p, s: (p, s, 0, 0, 0)),
                   pl.BlockSpec((3, SUPER, QK_BLOCK), lambda p, s: (0, s, p))],
        out_shape=[jax.ShapeDtypeStruct((t, ATTN_W), BF16), jax.ShapeDtypeStruct((t, ATTN_W), F32),
                   jax.ShapeDtypeStruct((4, nsb, 3 * UNITS, KEYS, KEYS), BF16),
                   jax.ShapeDtypeStruct((3, t, ATTN_W), F32)],
        scratch_shapes=[pltpu.VMEM((2 * SUPER, QK_BLOCK), F32), pltpu.VMEM((2 * SUPER, QK_BLOCK), F32),
                        pltpu.VMEM((3, SUPER, QK_BLOCK), F32), pltpu.VMEM((3, SUPER, QK_BLOCK), F32)],
        compiler_params=_cparams("parallel", "arbitrary"),
    )(q, k, k, v, v, slopes)


def _outproj_fwd(ya, yc, x, goc, goa, w_out, tm):
    t = x.shape[0]

    def body(ya_ref, yc_ref, x_ref, goc_ref, goa_ref, w_ref, x1_ref):
        _, ychat = _rms_stats(yc_ref[...].astype(F32))
        _, yahat = _rms_stats(ya_ref[...].astype(F32))
        acc = _mm((ychat * goc_ref[...]).astype(BF16), w_ref[0:CONV_W, :])
        acc += _mm((yahat * goa_ref[...]).astype(BF16), w_ref[CONV_W:, :])
        x1_ref[...] = x_ref[...] + acc

    def blk(c):
        return pl.BlockSpec((tm, c), lambda i: (i, 0))

    return pl.pallas_call(
        body, name="outproj_fwd", grid=(t // tm,),
        in_specs=[blk(ATTN_W), blk(CONV_W), blk(D_MODEL), _full((1, CONV_W)), _full((1, ATTN_W)),
                  _full((D_MODEL, D_MODEL))],
        out_specs=blk(D_MODEL),
        out_shape=jax.ShapeDtypeStruct((t, D_MODEL), F32),
        compiler_params=_cparams("parallel"),
    )(ya, yc, x, goc, goa, w_out)


def _ffn_fwd(x1, g_ffn, w_gate_t, w_up_t, w_down, fcw, fcb, tm):
    t = x1.shape[0]

    def body(x_ref, g_ref, wg_ref, wu_ref, wd_ref, cw_ref, cb_ref, gp_ref, up_ref, h_ref, x2_ref, carry_ref):
        @pl.when(pl.program_id(0) == 0)
        def _():
            carry_ref[...] = jnp.zeros_like(carry_ref)

        xv = x_ref[...]
        _, xhat = _rms_stats(xv)
        h = (xhat * g_ref[...]).astype(BF16)
        h_ref[...] = h
        gp = _mm_nt(h, wg_ref[...])
        gp_ref[...] = gp.astype(BF16)
        gate, _, _ = _conv_fwd(gp, carry_ref[7:8, :], carry_ref[6:7, :], cw_ref, cb_ref)
        carry_ref[...] = gp[tm - 8:tm, :]
        up = _mm_nt(h, wu_ref[...])
        up_ref[...] = up.astype(BF16)
        a = (gate * _sigmoid(gate) * up).astype(BF16)
        x2_ref[...] = xv + _mm(a, wd_ref[...])

    def blk(c):
        return pl.BlockSpec((tm, c), lambda i: (i, 0))

    return pl.pallas_call(
        body, name="ffn_fwd", grid=(t // tm,),
        in_specs=[blk(D_MODEL), _full((1, D_MODEL)), _full((D_FF, D_MODEL)), _full((D_FF, D_MODEL)),
                  _full((D_FF, D_MODEL)), _full((3, D_FF)), _full((1, D_FF))],
        out_specs=[blk(D_FF), blk(D_FF), blk(D_MODEL), blk(D_MODEL)],
        out_shape=[jax.ShapeDtypeStruct((t, D_FF), BF16), jax.ShapeDtypeStruct((t, D_FF), BF16),
                   jax.ShapeDtypeStruct((t, D_MODEL), BF16), jax.ShapeDtypeStruct((t, D_MODEL), F32)],
        scratch_shapes=[pltpu.VMEM((8, D_FF), F32)],
        compiler_params=_cparams("arbitrary"),
    )(x1, g_ffn, w_gate_t, w_up_t, w_down, fcw, fcb)


def _ple_fwd_bwd(x2, p, target, g_ple, w_pg, w_pp, tm):
    t = x2.shape[0]

    def body(x_ref, p_ref, t_ref, g_ref, wg_ref, wp_ref, dx_ref, dxb_ref, loss_ref, dwg_ref, dwp_ref, dg_ref):
        @pl.when(pl.program_id(0) == 0)
        def _():
            loss_ref[...] = jnp.zeros_like(loss_ref)
            dwg_ref[...] = jnp.zeros_like(dwg_ref)
            dwp_ref[...] = jnp.zeros_like(dwp_ref)
            dg_ref[...] = jnp.zeros_like(dg_ref)

        xv = x_ref[...]
        r, xhat = _rms_stats(xv)
        g = g_ref[...]
        h = (xhat * g).astype(BF16)
        pg = _sigmoid(_mm(h, wg_ref[...]))
        pb = p_ref[...].astype(BF16)
        pp = _mm(pb, wp_ref[...])
        err = xv + pg * pp - t_ref[...]
        loss_ref[...] += 0.5 * jnp.sum(jnp.mean(err * err, axis=-1, keepdims=True))
        dx3 = err * (1.0 / D_MODEL)
        d_pp = (dx3 * pg).astype(BF16)
        d_pre = (dx3 * pp * pg * (1.0 - pg)).astype(BF16)
        dwp_ref[...] += _mm_tn(pb, d_pp)
        dwg_ref[...] += _mm_tn(h, d_pre)
        dh = _mm_nt(d_pre, wg_ref[...])
        dg_ref[...] += jnp.sum(dh * xhat, axis=0, keepdims=True)
        dx2 = dx3 + _rms_bwd(dh, xhat, r, g)
        dx_ref[...] = dx2
        dxb_ref[...] = dx2.astype(BF16)

    def blk(c):
        return pl.BlockSpec((tm, c), lambda i: (i, 0))

    return pl.pallas_call(
        body, name="ple_fwd_bwd", grid=(t // tm,),
        in_specs=[blk(D_MODEL), blk(PLE_DIM), blk(D_MODEL), _full((1, D_MODEL)), _full((D_MODEL, D_MODEL)),
                  _full((PLE_DIM, D_MODEL))],
        out_specs=[blk(D_MODEL), blk(D_MODEL), _full((8, 128)), _full((D_MODEL, D_MODEL)),
                   _full((PLE_DIM, D_MODEL)), _full((1, D_MODEL))],
        out_shape=[jax.ShapeDtypeStruct((t, D_MODEL), F32), jax.ShapeDtypeStruct((t, D_MODEL), BF16),
                   jax.ShapeDtypeStruct((8, 128), F32),
                   jax.ShapeDtypeStruct((D_MODEL, D_MODEL), F32), jax.ShapeDtypeStruct((PLE_DIM, D_MODEL), F32),
                   jax.ShapeDtypeStruct((1, D_MODEL), F32)],
        compiler_params=_cparams("arbitrary"),
    )(x2, p, target, g_ple, w_pg, w_pp)


def _ffn_bwd(dx2, h2, gp, up, w_gate, w_up, w_down, fcw, fcb, tm):
    t = dx2.shape[0]
    nblk = t // tm
    fc = D_FF // FF_CHUNKS
    half = tm // FFN_BWD_PARTS

    def body(dx_ref, h_ref, gp_ref, gph_ref, up_ref, wg_ref, wu_ref, wd_ref, cw_ref, cb_ref,
             dh_ref, dwd_hbm, dwu_hbm, dwg_hbm, dcw_ref, dcb_ref, carry_ref, a_scr, dup_scr, dgp_scr,
             dwd_acc, dwu_acc, dwg_acc, stage, stage_sem):
        i = pl.program_id(1)

        @pl.when(i == 0)
        def _():
            carry_ref[...] = jnp.zeros_like(carry_ref)
            dwd_acc[...] = jnp.zeros_like(dwd_acc)
            dwu_acc[...] = jnp.zeros_like(dwu_acc)
            dwg_acc[...] = jnp.zeros_like(dwg_acc)
            dcw_ref[...] = jnp.zeros_like(dcw_ref)
            dcb_ref[...] = jnp.zeros_like(dcb_ref)

        keep = (i < nblk - 1).astype(F32)
        later = carry_ref[...]
        for hf in reversed(range(FFN_BWD_PARTS)):
            rows = slice(hf * half, (hf + 1) * half)
            dxb = dx_ref[rows, :]
            gp_v = gp_ref[rows, :].astype(F32)
            if hf > 0:
                before = gp_ref[hf * half - 16:hf * half, :].astype(F32)
            else:
                before = gph_ref[...].astype(F32) * keep
            gate, gp1, gp2 = _conv_fwd(gp_v, before[15:16, :], before[14:15, :], cw_ref, cb_ref)
            s = _sigmoid(gate)
            silu = gate * s
            up_v = up_ref[rows, :].astype(F32)
            da = _mm_nt(dxb, wd_ref[...])
            a_scr[rows, :] = (silu * up_v).astype(BF16)
            d_up = (da * silu).astype(BF16)
            dup_scr[rows, :] = d_up
            d_gate = da * up_v * (s * (1.0 + gate * (1.0 - s)))
            d_gp = _conv_bwd_input(d_gate, later[0:1, :], later[1:2, :], cw_ref).astype(BF16)
            dgp_scr[rows, :] = d_gp
            later = d_gate[0:8, :]
            dcw_ref[0:1, :] += jnp.sum(d_gate * gp2, axis=0, keepdims=True)
            dcw_ref[1:2, :] += jnp.sum(d_gate * gp1, axis=0, keepdims=True)
            dcw_ref[2:3, :] += jnp.sum(d_gate * gp_v, axis=0, keepdims=True)
            dcb_ref[...] += jnp.sum(d_gate, axis=0, keepdims=True)
            dh_ref[rows, :] = (_mm(d_gp, wg_ref[...]) + _mm(d_up, wu_ref[...])).astype(BF16)
        carry_ref[...] = later
        dwd_acc[...] += _mm_tn(a_scr[...], dx_ref[...])
        dwu_acc[...] += _mm_tn(h_ref[...], dup_scr[...])
        dwg_acc[...] += _mm_tn(h_ref[...], dgp_scr[...])

        @pl.when(i == nblk - 1)
        def _():
            rows = pl.ds(pl.multiple_of(pl.program_id(0) * fc, 16), fc)
            for acc, out, flip in ((dwd_acc, dwd_hbm, False), (dwu_acc, dwu_hbm, True), (dwg_acc, dwg_hbm, True)):
                stage[...] = (acc[...].T if flip else acc[...]).astype(BF16)
                copy = pltpu.make_async_copy(stage, out.at[rows, :], stage_sem)
                copy.start()
                copy.wait()

    def rev(i):
        return nblk - 1 - i

    one = pl.Buffered(1)
    in_specs = [
        pl.BlockSpec((tm, D_MODEL), lambda j, i: (rev(i), 0)),
        pl.BlockSpec((tm, D_MODEL), lambda j, i: (rev(i), 0)),
        pl.BlockSpec((tm, fc), lambda j, i: (rev(i), j)),
        pl.BlockSpec((16, fc), lambda j, i: (jnp.maximum(rev(i) * (tm // 16) - 1, 0), j)),
        pl.BlockSpec((tm, fc), lambda j, i: (rev(i), j)),
        pl.BlockSpec((fc, D_MODEL), lambda j, i: (j, 0), pipeline_mode=one),
        pl.BlockSpec((fc, D_MODEL), lambda j, i: (j, 0), pipeline_mode=one),
        pl.BlockSpec((fc, D_MODEL), lambda j, i: (j, 0), pipeline_mode=one),
        pl.BlockSpec((3, fc), lambda j, i: (0, j)),
        pl.BlockSpec((1, fc), lambda j, i: (0, j)),
    ]
    out_specs = [
        pl.BlockSpec((None, tm, D_MODEL), lambda j, i: (j, rev(i), 0)),
        ANY, ANY, ANY,
        pl.BlockSpec((3, fc), lambda j, i: (0, j)),
        pl.BlockSpec((1, fc), lambda j, i: (0, j)),
    ]
    return pl.pallas_call(
        body, name="ffn_bwd", grid=(FF_CHUNKS, nblk), in_specs=in_specs, out_specs=out_specs,
        out_shape=[jax.ShapeDtypeStruct((FF_CHUNKS, t, D_MODEL), BF16), jax.ShapeDtypeStruct((D_FF, D_MODEL), BF16),
                   jax.ShapeDtypeStruct((D_FF, D_MODEL), BF16), jax.ShapeDtypeStruct((D_FF, D_MODEL), BF16),
                   jax.ShapeDtypeStruct((3, D_FF), F32), jax.ShapeDtypeStruct((1, D_FF), F32)],
        scratch_shapes=[pltpu.VMEM((8, fc), F32), pltpu.VMEM((tm, fc), BF16), pltpu.VMEM((tm, fc), BF16),
                        pltpu.VMEM((tm, fc), BF16), pltpu.VMEM((fc, D_MODEL), F32), pltpu.VMEM((D_MODEL, fc), F32),
                        pltpu.VMEM((D_MODEL, fc), F32), pltpu.VMEM((fc, D_MODEL), BF16), pltpu.SemaphoreType.DMA],
        compiler_params=_cparams("arbitrary", "arbitrary", vmem=V7X_VMEM_LIMIT_LARGE),
    )(dx2, h2, gp, gp, up, w_gate, w_up, w_down, fcw, fcb)


def _outproj_bwd(dh2, dx2, x1, g_ffn, w_out, yc, ya, goc, goa, zconv, conv_w, conv_b, bd, tm):
    t = x1.shape[0]
    nblk = t // tm

    def body(dh_ref, dx2_ref, x1_ref, g_ref, w_ref, yc_ref, ya_ref, goc_ref, goa_ref, zc_ref, zch_ref, cw_ref, cb_ref,
             bd_ref, dx1_ref, dya_ref, dd_ref, dzc_ref, dw_ref, dg_ref, dgoc_ref, dgoa_ref, dcw_ref, dcb_ref,
             carry_ref):
        i = pl.program_id(0)

        @pl.when(i == 0)
        def _():
            carry_ref[...] = jnp.zeros_like(carry_ref)
            for ref in (dw_ref, dg_ref, dgoc_ref, dgoa_ref, dcw_ref, dcb_ref):
                ref[...] = jnp.zeros_like(ref)

        keep = (i < nblk - 1).astype(F32)
        dh2_v = dh_ref[0].astype(F32)
        for j in range(1, FF_CHUNKS):
            dh2_v = dh2_v + dh_ref[j].astype(F32)
        r, xhat = _rms_stats(x1_ref[...])
        dg_ref[...] += jnp.sum(dh2_v * xhat, axis=0, keepdims=True)
        dx1 = dx2_ref[...] + _rms_bwd(dh2_v, xhat, r, g_ref[...])
        dx1_ref[...] = dx1
        dx1b = dx1.astype(BF16)
        dy = _mm_nt(dx1b, w_ref[...])

        yc_v = yc_ref[...].astype(F32)
        rc, ychat = _rms_stats(yc_v)
        dw_ref[0:CONV_W, :] += _mm_tn((ychat * goc_ref[...]).astype(BF16), dx1b)
        dyc = dy[:, 0:CONV_W]
        dgoc_ref[...] += jnp.sum(dyc * ychat, axis=0, keepdims=True)
        d_yc = _rms_bwd(dyc, ychat, rc, goc_ref[...])

        ya_v = ya_ref[...].astype(F32)
        ra, yahat = _rms_stats(ya_v)
        dw_ref[CONV_W:, :] += _mm_tn((yahat * goa_ref[...]).astype(BF16), dx1b)
        dya = dy[:, CONV_W:]
        dgoa_ref[...] += jnp.sum(dya * yahat, axis=0, keepdims=True)
        d_ya = _rms_bwd(dya, yahat, ra, goa_ref[...])
        dya_ref[...] = d_ya
        dd_ref[...] = _seg_sum64(d_ya * ya_v, bd_ref)

        zb = zc_ref[:, 0:CONV_W].astype(F32)
        zc = zc_ref[:, CONV_W:2 * CONV_W].astype(F32)
        zx = zc_ref[:, 2 * CONV_W:3 * CONV_W].astype(F32)
        u = zc * zx
        uh = (zch_ref[:, CONV_W:2 * CONV_W].astype(F32) * zch_ref[:, 2 * CONV_W:3 * CONV_W].astype(F32)) * keep
        cv, u1, u2 = _conv_fwd(u, uh[15:16, :], uh[14:15, :], cw_ref, cb_ref)
        d_cv = d_yc * zb
        d_u = _conv_bwd_input(d_cv, carry_ref[0:1, :], carry_ref[1:2, :], cw_ref)
        carry_ref[...] = d_cv[0:8, :]
        dcw_ref[0:1, :] += jnp.sum(d_cv * u2, axis=0, keepdims=True)
        dcw_ref[1:2, :] += jnp.sum(d_cv * u1, axis=0, keepdims=True)
        dcw_ref[2:3, :] += jnp.sum(d_cv * u, axis=0, keepdims=True)
        dcb_ref[...] += jnp.sum(d_cv, axis=0, keepdims=True)
        dzc_ref[:, 0:CONV_W] = (d_yc * cv).astype(BF16)
        dzc_ref[:, CONV_W:2 * CONV_W] = (d_u * zx).astype(BF16)
        dzc_ref[:, 2 * CONV_W:3 * CONV_W] = (d_u * zc).astype(BF16)

    def rev(i):
        return nblk - 1 - i

    def blk(c):
        return pl.BlockSpec((tm, c), lambda i: (rev(i), 0))

    in_specs = [
        pl.BlockSpec((FF_CHUNKS, tm, D_MODEL), lambda i: (0, rev(i), 0)),
        blk(D_MODEL), blk(D_MODEL), _full((1, D_MODEL)), _full((D_MODEL, D_MODEL)),
        blk(CONV_W), blk(ATTN_W), _full((1, CONV_W)), _full((1, ATTN_W)),
        blk(3 * CONV_W),
        pl.BlockSpec((16, 3 * CONV_W), lambda i: (jnp.maximum(rev(i) * (tm // 16) - 1, 0), 0)),
        _full((3, CONV_W)), _full((1, CONV_W)), _full((256, 256)),
    ]
    out_specs = [blk(D_MODEL), blk(ATTN_W), blk(ATTN_W), blk(3 * CONV_W), _full((D_MODEL, D_MODEL)),
                 _full((1, D_MODEL)), _full((1, CONV_W)), _full((1, ATTN_W)), _full((3, CONV_W)), _full((1, CONV_W))]
    return pl.pallas_call(
        body, name="outproj_bwd", grid=(nblk,), in_specs=in_specs, out_specs=out_specs,
        out_shape=[jax.ShapeDtypeStruct((t, D_MODEL), F32), jax.ShapeDtypeStruct((t, ATTN_W), F32),
                   jax.ShapeDtypeStruct((t, ATTN_W), F32), jax.ShapeDtypeStruct((t, 3 * CONV_W), BF16),
                   jax.ShapeDtypeStruct((D_MODEL, D_MODEL), F32), jax.ShapeDtypeStruct((1, D_MODEL), F32),
                   jax.ShapeDtypeStruct((1, CONV_W), F32), jax.ShapeDtypeStruct((1, ATTN_W), F32),
                   jax.ShapeDtypeStruct((3, CONV_W), F32), jax.ShapeDtypeStruct((1, CONV_W), F32)],
        scratch_shapes=[pltpu.VMEM((8, CONV_W), F32)],
        compiler_params=_cparams("arbitrary"),
    )(dh2, dx2, x1, g_ffn, w_out, yc, ya, goc, goa, zconv, zconv, conv_w, conv_b, bd)


def _attn_bwd(q, k, v, dya, lse, dd, e_all, m_all, after):
    t = q.shape[0]
    nsb = t // SUPER

    def body(q_ref, kc_ref, kp_ref, vc_ref, vp_ref, dy_ref, l_ref, d_ref, e_ref, m_ref, after_ref,
             dq_ref, dk_ref, dv_ref, kk, vv, dkacc, dvacc, dwide):
        s = pl.program_id(1)

        @pl.when(s == 0)
        def _():
            dkacc[...] = jnp.zeros_like(dkacc)
            dvacc[...] = jnp.zeros_like(dvacc)

        dkacc[0:SUPER, :] = dkacc[SUPER:, :]
        dvacc[0:SUPER, :] = dvacc[SUPER:, :]
        dkacc[SUPER:, :] = jnp.zeros((SUPER, QK_BLOCK), F32)
        dvacc[SUPER:, :] = jnp.zeros((SUPER, QK_BLOCK), F32)

        @pl.when(s < nsb)
        def _():
            kk[0:SUPER, :] = kp_ref[...]
            kk[SUPER:, :] = kc_ref[...]
            vv[0:SUPER, :] = vp_ref[...]
            vv[SUPER:, :] = vc_ref[...]
            head0 = lax.broadcasted_iota(jnp.int32, (QK_BLOCK, QK_BLOCK), 1) < HEAD_DIM

            def widened(a):
                other = pltpu.roll(a, HEAD_DIM, 1)
                first = lax.broadcasted_iota(jnp.int32, a.shape, 1) < HEAD_DIM
                return jnp.where(first, a, other), jnp.where(first, other, a)

            def stacked(h0, h1):
                return jnp.concatenate([jnp.concatenate([h0, h0], axis=1), jnp.concatenate([h1, h1], axis=1)], axis=0)

            def widen_dd(i, carry):
                rows = pl.ds(pl.multiple_of(i * 256, 256), 256)
                dwide[0, rows, :], dwide[1, rows, :] = widened(d_ref[rows, :])
                return carry

            lax.fori_loop(0, SUPER // 256, widen_dd, 0)

            for b, dil in enumerate(DILATIONS):
                def unit(u, carry, b=b, dil=dil):
                    start = _unit_start(u, dil)
                    first_key = SUPER + start - QK_BLOCK * dil
                    qrows = _rows(start, QK_BLOCK, dil)
                    krows = _rows(first_key, KEYS, dil)
                    q2 = _stack_heads(q_ref[qrows, :].astype(BF16), head0)
                    dy2 = _stack_heads(dy_ref[qrows, :].astype(BF16), head0)
                    g2 = stacked(*widened(jnp.exp(m_ref[b, qrows, :] - l_ref[qrows, :])))
                    d2 = stacked(dwide[0, qrows, :], dwide[1, qrows, :])
                    k2 = kk[krows, :].astype(BF16)
                    v2 = vv[krows, :].astype(BF16)
                    prob = e_ref[b * UNITS + u].astype(F32) * g2
                    ds = (prob * (_mm_nt(dy2, v2) - d2)).astype(BF16)
                    dvacc[krows, :] += _mm_tn(prob.astype(BF16), dy2)
                    dkacc[krows, :] += _mm_tn(ds, q2)
                    dq2 = _mm(ds, k2)
                    dq = jnp.where(head0, dq2[0:QK_BLOCK], dq2[QK_BLOCK:]) * ATTN_SCALE
                    if b == 0:
                        dq_ref[qrows, :] = dq
                    else:
                        dq_ref[qrows, :] += dq
                    return carry

                lax.fori_loop(0, UNITS, unit, 0, unroll=8)

        dk_ref[...] = dkacc[0:SUPER, :]
        dv_ref[...] = dvacc[0:SUPER, :].astype(BF16)

    def cur_map(p, s):
        return (jnp.minimum(s, nsb - 1), p)

    def prev_map(p, s):
        return (jnp.clip(s - 1, 0, nsb - 1), p)

    cur = pl.BlockSpec((SUPER, QK_BLOCK), cur_map)
    prev = pl.BlockSpec((SUPER, QK_BLOCK), prev_map)
    return pl.pallas_call(
        body, name="attn_bwd", grid=(4, nsb + 1),
        in_specs=[cur, cur, prev, cur, prev, cur, cur, cur,
                  pl.BlockSpec((None, None, 3 * UNITS, KEYS, KEYS), lambda p, s: (p, jnp.minimum(s, nsb - 1), 0, 0, 0)),
                  pl.BlockSpec((3, SUPER, QK_BLOCK), lambda p, s: (0, jnp.minimum(s, nsb - 1), p)),
                  pl.BlockSpec(memory_space=pl.ANY)],
        out_specs=[cur, prev, prev],
        out_shape=[jax.ShapeDtypeStruct((t, ATTN_W), F32), jax.ShapeDtypeStruct((t, ATTN_W), F32),
                   jax.ShapeDtypeStruct((t, ATTN_W), BF16)],
        scratch_shapes=[pltpu.VMEM((2 * SUPER, QK_BLOCK), F32)] * 4 + [pltpu.VMEM((2, SUPER, QK_BLOCK), F32)],
        compiler_params=_cparams("parallel", "arbitrary"),
    )(q, k, k, v, v, dya, lse, dd, e_all, m_all, after)


def _inproj_bwd(dq, dk, dv, dzconv, zqk, x, dx1, g_mix, w_in, qg, kg, bd, tm):
    t = x.shape[0]
    nblk = t // tm
    shard = IN_COLS // N_DEV

    def body(dq_ref, dk_ref, dv_ref, dzc_ref, zqk_ref, x_ref, dx1_ref, g_ref, w_ref, qg_ref,
             kg_ref, bd_ref, dx_ref, dw_hbm, dg_ref, dqg_ref, dkg_ref, dw_ref, stage, stage_sem):
        @pl.when(pl.program_id(0) == 0)
        def _():
            for ref in (dw_ref, dg_ref, dqg_ref, dkg_ref):
                ref[...] = jnp.zeros_like(ref)

        parts = [dzc_ref[...]]
        for j, (dn_ref, gain_ref, dgain_ref) in enumerate(((dq_ref, qg_ref, dqg_ref), (dk_ref, kg_ref, dkg_ref))):
            dn = dn_ref[...]
            z = zqk_ref[:, j * ATTN_W:(j + 1) * ATTN_W].astype(F32)
            r = lax.rsqrt(_seg_sum64(z * z, bd_ref) * (1.0 / HEAD_DIM) + EPS)
            zhat = z * r
            dgain_ref[...] += jnp.sum(dn * zhat, axis=0, keepdims=True)
            gd = dn * gain_ref[...]
            parts.append((r * (gd - zhat * (_seg_sum64(gd * zhat, bd_ref) * (1.0 / HEAD_DIM)))).astype(BF16))
        parts.append(dv_ref[...].astype(BF16))
        dz = jnp.concatenate(parts, axis=1)

        r, xhat = _rms_stats(x_ref[...])
        g = g_ref[...]
        dw_ref[...] += _mm_tn((xhat * g).astype(BF16), dz)
        dh = _mm_nt(dz, w_ref[...])
        dg_ref[...] += jnp.sum(dh * xhat, axis=0, keepdims=True)
        dx_ref[...] = dx1_ref[...] + _rms_bwd(dh, xhat, r, g)

        @pl.when(pl.program_id(0) == nblk - 1)
        def _():
            for k in range(N_DEV):
                stage[...] = dw_ref[:, k * shard:(k + 1) * shard].astype(BF16)
                copy = pltpu.make_async_copy(stage, dw_hbm.at[k], stage_sem)
                copy.start()
                copy.wait()

    def blk(c):
        return pl.BlockSpec((tm, c), lambda i: (i, 0))

    return pl.pallas_call(
        body, name="inproj_bwd", grid=(nblk,),
        in_specs=[blk(ATTN_W)] * 3 + [blk(3 * CONV_W), blk(2 * ATTN_W), blk(D_MODEL), blk(D_MODEL), _full((1, D_MODEL)),
                                      _full((D_MODEL, IN_COLS)), _full((1, ATTN_W)), _full((1, ATTN_W)),
                                      _full((256, 256))],
        out_specs=[blk(D_MODEL), ANY, _full((1, D_MODEL)), _full((1, ATTN_W)), _full((1, ATTN_W))],
        out_shape=[jax.ShapeDtypeStruct((t, D_MODEL), F32), jax.ShapeDtypeStruct((N_DEV, D_MODEL, shard), BF16),
                   jax.ShapeDtypeStruct((1, D_MODEL), F32), jax.ShapeDtypeStruct((1, ATTN_W), F32),
                   jax.ShapeDtypeStruct((1, ATTN_W), F32)],
        scratch_shapes=[pltpu.VMEM((D_MODEL, IN_COLS), F32), pltpu.VMEM((D_MODEL, shard), BF16),
                        pltpu.SemaphoreType.DMA],
        compiler_params=_cparams("arbitrary"),
    )(dq, dk, dv, dzconv, zqk, x, dx1, g_mix, w_in, qg, kg, bd)


def _ordered_after(a, token):
    return a if token is None else a + token[0:1, 0:1].reshape((1,) * a.ndim)


def _local_step(x, p, target, w, tms, hooks=None):
    hooks = hooks or {}
    bd = jnp.kron(jnp.eye(4, dtype=F32), jnp.ones((HEAD_DIM, HEAD_DIM), F32)).astype(BF16)
    qg = jnp.tile(w["q_norm_g"], (1, 8))
    kg = jnp.tile(w["k_norm_g"], (1, 8))
    slopes = jnp.exp2(-jnp.arange(1, 9, dtype=F32))
    slopes = jnp.broadcast_to(slopes.reshape(4, 2, 1), (4, 2, QK_BLOCK))

    zconv, zqk, yc, q, k, v = _inproj_fwd(x, w["g_mix"], w["w_in"], w["conv_w"], w["conv_b"], qg, kg, bd, tms[0])
    ya, lse, e_all, m_all = _attn_fwd(q, k, v, slopes)
    if "late_weights" in hooks:
        w = {**w, **hooks["late_weights"](lse)}
    x1 = _outproj_fwd(ya, yc, x, w["g_out_conv"], w["g_out_attn"], w["w_out"], tms[0])
    gp, up, h2, x2 = _ffn_fwd(x1, w["g_ffn"], w["w_gate"], w["w_up"], w["w_down"], w["ffn_conv_w"], w["ffn_conv_b"],
                              tms[1])
    dx2, dx2b, loss, dw_pg, dw_pp, dg_ple = _ple_fwd_bwd(x2, p, target, w["g_ple"], w["w_ple_gate"], w["w_ple_proj"], tms[0])
    dh2, dw_down, dw_up, dw_gate, dfcw, dfcb = _ffn_bwd(dx2b, h2, gp, up, w["w_gate"], w["w_up"], w["w_down"],
                                                        w["ffn_conv_w"], w["ffn_conv_b"], tms[0])
    token = None
    if "ffn_grads" in hooks:
        token = hooks["ffn_grads"]({"w_ple_gate": dw_pg, "w_ple_proj": dw_pp, "w_down": dw_down, "w_up": dw_up,
                                    "w_gate": dw_gate})
    dx1, dya, dd, dzconv, dw_out, dg_ffn, dgoc, dgoa, dcw, dcb = _outproj_bwd(
        dh2, dx2, x1, _ordered_after(w["g_ffn"], token), w["w_out"], yc, ya, w["g_out_conv"], w["g_out_attn"], zconv,
        w["conv_w"], w["conv_b"], bd, tms[1])
    token = hooks["outproj_done"](dx1) if "outproj_done" in hooks else None
    dq, dk, dv = _attn_bwd(q, k, v, dya, lse, dd, e_all, m_all, slopes if token is None else token)
    dx, dw_in, dg_mix, dqg, dkg = _inproj_bwd(dq, dk, dv, dzconv, zqk, x, dx1, w["g_mix"], w["w_in"], qg, kg, bd,
                                              tms[0])
    grads = {
        "g_mix": dg_mix, "w_in": dw_in, "conv_w": dcw, "conv_b": dcb,
        "q_norm_g": dqg.reshape(8, HEAD_DIM).sum(0, keepdims=True),
        "k_norm_g": dkg.reshape(8, HEAD_DIM).sum(0, keepdims=True),
        "g_out_conv": dgoc, "g_out_attn": dgoa, "w_out": dw_out, "g_ffn": dg_ffn, "w_gate": dw_gate, "w_up": dw_up,
        "ffn_conv_w": dfcw, "ffn_conv_b": dfcb, "w_down": dw_down, "g_ple": dg_ple, "w_ple_gate": dw_pg,
        "w_ple_proj": dw_pp,
    }
    return loss, dx, grads


ANY = pl.BlockSpec(memory_space=pl.ANY)
MESH = pl.DeviceIdType.MESH


def _all_gather(shards, name):
    n = len(shards)

    def body(*refs):
        ins, outs = refs[:n], refs[n:2 * n]
        send_sems, recv_sems, local_sems = refs[2 * n:]
        x, y, c = lax.axis_index("x"), lax.axis_index("y"), lax.axis_index("c")
        me, sibling = (x, y, c), (x, y, 1 - c)
        chips = [(1 - x, y), (x, 1 - y), (1 - x, 1 - y)]

        def slot(dev):
            return 4 * dev[0] + 2 * dev[1] + dev[2]

        def copy(b, k, block, to, src=None):
            dst = outs[b].at[slot(block)]
            return pltpu.make_async_remote_copy(
                src_ref=dst if src is None else src, dst_ref=dst, send_sem=send_sems.at[b, k],
                recv_sem=recv_sems.at[b, k], device_id=to, device_id_type=MESH)

        mine = [pltpu.make_async_copy(ins[b], outs[b].at[slot(me)], local_sems.at[b]) for b in range(n)]
        first, passed = [], []
        for b in range(n):
            mine[b].start()
            first.append(copy(b, 0, me, sibling, src=ins[b]))
            first += [copy(b, 1 + j, me, (*chip, c), src=ins[b]) for j, chip in enumerate(chips)]
        for cp in first:
            cp.start()
        for j, chip in enumerate(chips):
            for b in range(n):
                copy(b, 1 + j, (*chip, c), me).wait_recv()
                fwd = copy(b, 4 + j, (*chip, c), sibling)
                fwd.start()
                passed.append(fwd)
        for b in range(n):
            copy(b, 0, sibling, me).wait_recv()
            for j, chip in enumerate(chips):
                copy(b, 4 + j, (*chip, 1 - c), me).wait_recv()
        for cp in first + passed:
            cp.wait_send()
        for cp in mine:
            cp.wait()

    return pl.pallas_call(
        body, name=name,
        in_specs=[ANY] * n, out_specs=[ANY] * n,
        out_shape=[jax.ShapeDtypeStruct((N_DEV,) + s.shape, s.dtype) for s in shards],
        scratch_shapes=[pltpu.SemaphoreType.DMA((n, 7)), pltpu.SemaphoreType.DMA((n, 7)),
                        pltpu.SemaphoreType.DMA((n,))],
    )(*shards)


HBM = pl.BlockSpec(memory_space=pltpu.HBM)
SEM = pl.BlockSpec(memory_space=pltpu.SEMAPHORE)
EFFECT = pltpu.SideEffectType.DATAFLOW_SIDE_EFFECTING
FLIPS = ((0, 0, 1), (0, 1, 0), (0, 1, 1), (1, 0, 0), (1, 0, 1), (1, 1, 0), (1, 1, 1))


def _flip_peers():
    pos = (lax.axis_index("x"), lax.axis_index("y"), lax.axis_index("c"))
    return [tuple(1 - a if f else a for a, f in zip(pos, flip)) for flip in FLIPS]


def _hbm(a):
    return pltpu.with_memory_space_constraint(a, pltpu.HBM)


def _split_start(name, srcs, lands, plan, n_copies, after):
    n, m = len(srcs), len(lands)

    def body(*refs):
        send_sems, recv_sems, token = refs[n + m + 1], refs[n + m + 2], refs[-1]
        for i, (src, dst, peer) in enumerate(plan(refs[:n], refs[n:n + m])):
            pltpu.make_async_remote_copy(src_ref=src, dst_ref=dst, send_sem=send_sems.at[i], recv_sem=recv_sems.at[i],
                                         device_id=peer, device_id_type=MESH).start()
        token[...] = jnp.zeros_like(token)

    outs = pl.pallas_call(
        body, name=name + "_start",
        in_specs=[HBM] * (n + m) + [ANY],
        out_specs=[SEM, SEM] + [HBM] * (n + m) + [pl.BlockSpec(memory_space=pltpu.VMEM)],
        out_shape=[pltpu.SemaphoreType.DMA((n_copies,)), pltpu.SemaphoreType.DMA((n_copies,))]
        + [pltpu.HBM(a.shape, a.dtype) for a in list(srcs) + list(lands)] + [jax.ShapeDtypeStruct((8, 128), F32)],
        input_output_aliases={i: 2 + i for i in range(n + m)},
        compiler_params=pltpu.CompilerParams(has_side_effects=EFFECT),
    )(*[_hbm(a) for a in list(srcs) + list(lands)], after)
    return (outs[0], outs[1], outs[2:2 + n], outs[2 + n:2 + n + m]), outs[-1]


def _split_wait(name, started, plan, after):
    send_sems, recv_sems, srcs, lands = started
    n, m = len(srcs), len(lands)

    def body(*refs):
        send_ref, recv_ref = refs[n + m], refs[n + m + 1]
        for i, (src, dst, peer) in enumerate(plan(refs[:n], refs[n:n + m])):
            copy = pltpu.make_async_remote_copy(src_ref=src, dst_ref=dst, send_sem=send_ref.at[i],
                                                recv_sem=recv_ref.at[i], device_id=peer, device_id_type=MESH)
            copy.wait_send()
            copy.wait_recv()

    outs = pl.pallas_call(
        body, name=name + "_wait",
        in_specs=[HBM] * (n + m) + [SEM, SEM, ANY],
        out_specs=[HBM] * (n + m),
        out_shape=[pltpu.HBM(a.shape, a.dtype) for a in list(srcs) + list(lands)],
        input_output_aliases={i: i for i in range(n + m)},
        compiler_params=pltpu.CompilerParams(has_side_effects=EFFECT),
    )(*srcs, *lands, send_sems, recv_sems, after)
    return outs[:n], outs[n:]


def _gather_plan(srcs, lands):
    slot = 4 * lax.axis_index("x") + 2 * lax.axis_index("y") + lax.axis_index("c")
    return [(src, land.at[slot], peer) for src, land in zip(srcs, lands) for peer in _flip_peers()]


def _sibling_plan(srcs, lands):
    x, y, c = lax.axis_index("x"), lax.axis_index("y"), lax.axis_index("c")
    return [(src.at[k, 1 - c], land.at[k], (x, y, 1 - c)) for src, land in zip(srcs, lands) for k in range(N_CHIP)]


def _chip_plan(srcs, lands):
    x, y, c = lax.axis_index("x"), lax.axis_index("y"), lax.axis_index("c")
    return [(src.at[2 * cx + cy], land.at[2 * x + y], (cx, cy, c))
            for src, land in zip(srcs, lands) for cx, cy in ((1 - x, y), (x, 1 - y), (1 - x, 1 - y))]


def _row_tile(rows, most=64):
    for tr in range(min(rows, most) // 16 * 16, 15, -16):
        if rows % tr == 0:
            return tr
    return rows


def _pair_sums(gs, lands, core, name):
    n = len(gs)

    def body(c_ref, *refs):
        for b in range(n):
            out = refs[2 * n + b]
            out[...] = (refs[b][...].astype(F32) + refs[n + b][...].astype(F32)).astype(out.dtype)

    def slab(a):
        return pl.BlockSpec((None,) + a.shape[1:], lambda k, c_ref: (k, 0, 0))

    return pl.pallas_call(
        body, name=name,
        grid_spec=pltpu.PrefetchScalarGridSpec(
            num_scalar_prefetch=1, grid=(N_CHIP,),
            in_specs=[pl.BlockSpec((None, None) + g.shape[2:], lambda k, c_ref: (k, c_ref[0], 0, 0)) for g in gs]
            + [slab(a) for a in lands],
            out_specs=[slab(a) for a in lands]),
        out_shape=[jax.ShapeDtypeStruct(a.shape, a.dtype) for a in lands],
        compiler_params=_cparams("parallel"),
    )(core, *gs, *lands)


def _adamw(own, arrived, chip, w, m, v, name):
    k, rows, cols = arrived.shape
    tr = _row_tile(rows)
    c1 = 1.0 / (1.0 - ADAM_B1 ** ADAM_STEP)
    c2 = 1.0 / (1.0 - ADAM_B2 ** ADAM_STEP)

    def body(chip_ref, o_ref, p_ref, w_ref, m_ref, v_ref, g_ref, d_ref, nm_ref, nv_ref):
        def slab(j):
            return jnp.where(chip_ref[0] == j, o_ref[j], p_ref[j]).astype(F32)

        g = slab(0)
        for j in range(1, k):
            g = g + slab(j)
        g_ref[...] = g
        nm = ADAM_B1 * m_ref[...] + (1.0 - ADAM_B1) * g
        nv = ADAM_B2 * v_ref[...] + (1.0 - ADAM_B2) * (g * g)
        nm_ref[...] = nm
        nv_ref[...] = nv
        d_ref[...] = -ADAM_LR * ((nm * c1) / (jnp.sqrt(nv * c2) + ADAM_EPS) + ADAM_WD * w_ref[...])

    blk = pl.BlockSpec((tr, cols), lambda i, c: (i, 0))
    stack = pl.BlockSpec((k, tr, cols), lambda i, c: (0, i, 0))
    return pl.pallas_call(
        body, name=name,
        grid_spec=pltpu.PrefetchScalarGridSpec(num_scalar_prefetch=1, grid=(rows // tr,),
                                               in_specs=[stack, stack, blk, blk, blk], out_specs=[blk] * 4),
        out_shape=[jax.ShapeDtypeStruct((rows, cols), F32)] * 4,
        compiler_params=_cparams("parallel"),
    )(chip, own, arrived, w, m, v)


SMALL_LAYOUT = (("g_mix", 0, 1024), ("conv_b", 1, 512), ("q_norm_g", 2, 64), ("k_norm_g", 3, 64),
                ("g_out_conv", 4, 512), ("g_out_attn", 5, 512), ("g_ffn", 6, 1024), ("ffn_conv_b", 7, 2816),
                ("g_ple", 10, 1024))
CONV_W_ROW = 11
FFN_CONV_W_ROW = 14
LOSS_ROW = 23


def _row_pieces(cols):
    return [(c, min(1024, cols - c)) for c in range(0, cols, 1024)]


def _pack_small(grads, loss_tile):
    names = [n for n, _, _ in SMALL_LAYOUT]

    def body(*refs):
        ins, cw_ref, fcw_ref, loss_ref, out_ref = refs[:len(names)], refs[-4], refs[-3], refs[-2], refs[-1]
        out_ref[...] = jnp.zeros_like(out_ref)
        for ref, (_, row, cols) in zip(ins, SMALL_LAYOUT):
            for j, (c, width) in enumerate(_row_pieces(cols)):
                out_ref[row + j:row + j + 1, 0:width] = ref[:, c:c + width]
        for k in range(3):
            out_ref[CONV_W_ROW + k:CONV_W_ROW + k + 1, 0:CONV_W] = cw_ref[k:k + 1, :]
            for j, (c, width) in enumerate(_row_pieces(D_FF)):
                row = FFN_CONV_W_ROW + 3 * k + j
                out_ref[row:row + 1, 0:width] = fcw_ref[k:k + 1, c:c + width]
        out_ref[LOSS_ROW:LOSS_ROW + 1, 0:128] = loss_ref[0:1, :]

    return pl.pallas_call(
        body, name="pack_small_grads", out_shape=jax.ShapeDtypeStruct((SMALL_ROWS, 1024), F32),
    )(*[grads[n] for n in names], grads["conv_w"], grads["ffn_conv_w"], loss_tile)


def _adamw_small(arrived, conv_parts, fconv_parts, wts, mom, var):
    names = [n for n, _, _ in SMALL_LAYOUT] + ["conv_w", "ffn_conv_w"]
    c1 = 1.0 / (1.0 - ADAM_B1 ** ADAM_STEP)
    c2 = 1.0 / (1.0 - ADAM_B2 ** ADAM_STEP)
    n = len(names)

    def body(*refs):
        land, cw_ref, fcw_ref = refs[0], refs[1], refs[2]
        state = refs[3:3 + 3 * n]
        outs = refs[3 + 3 * n:]

        def total(piece):
            acc = piece(0)
            for d in range(1, N_DEV):
                acc = acc + piece(d)
            return acc

        for i, name in enumerate(names):
            if name == "conv_w":
                g = total(lambda d: cw_ref[d])
            elif name == "ffn_conv_w":
                g = total(lambda d: fcw_ref[d])
            else:
                _, row, cols = SMALL_LAYOUT[i]
                pieces = [total(lambda d, j=j, width=width: land[d, row + j:row + j + 1, 0:width])
                          for j, (_, width) in enumerate(_row_pieces(cols))]
                g = pieces[0] if len(pieces) == 1 else jnp.concatenate(pieces, axis=1)
            w_ref, m_ref, v_ref = state[3 * i:3 * i + 3]
            nm = ADAM_B1 * m_ref[...] + (1.0 - ADAM_B1) * g
            nv = ADAM_B2 * v_ref[...] + (1.0 - ADAM_B2) * (g * g)
            outs[4 * i][...] = g
            outs[4 * i + 1][...] = -ADAM_LR * ((nm * c1) / (jnp.sqrt(nv * c2) + ADAM_EPS) + ADAM_WD * w_ref[...])
            outs[4 * i + 2][...] = nm
            outs[4 * i + 3][...] = nv
        outs[-1][...] = total(lambda d: land[d, LOSS_ROW:LOSS_ROW + 1, 0:128])

    state = [a[nm_] for nm_ in names for a in (wts, mom, var)]
    shapes = [jax.ShapeDtypeStruct(wts[nm_].shape, F32) for nm_ in names for _ in range(4)]
    outs = pl.pallas_call(
        body, name="adamw_small", out_shape=shapes + [jax.ShapeDtypeStruct((1, 128), F32)],
    )(arrived, conv_parts, fconv_parts, *state)
    return {nm_: tuple(outs[4 * i:4 * i + 4]) for i, nm_ in enumerate(names)}, outs[-1][0, 0]


COL_SHARDED = ("w_in", "w_ple_proj")
TRANSPOSED = ("w_gate", "w_up")
CONV_SHARDED = (("conv_w", CONV_W), ("ffn_conv_w", D_FF))


def _gathered_to_full(name, gathered):
    if name in COL_SHARDED:
        return gathered.transpose(1, 0, 2).reshape(gathered.shape[1], -1)
    return gathered.reshape(-1, gathered.shape[2])


def _full_to_stacked(name, grad, shard_shape):
    sr, sc = shard_shape
    if grad.ndim == 3:
        a = grad
    elif name in COL_SHARDED:
        a = grad.reshape(sr, N_DEV, sc).transpose(1, 0, 2)
    else:
        a = grad.reshape(N_DEV, sr, sc)
    return a.astype(BF16).reshape(N_CHIP, 2, sr, sc)


def _pad_rows(vec, rows):
    return jnp.pad(vec, (0, rows * 1024 - vec.shape[0])).reshape(rows, 1024)


def kernel(x, p, g_mix, w_in, conv_w, conv_b, q_norm_g, k_norm_g, g_out_conv, g_out_attn, w_out, g_ffn, w_gate, w_up, ffn_conv_w, ffn_conv_b, w_down, g_ple, w_ple_gate, w_ple_proj, loss_target, m_g_mix, m_w_in, m_conv_w, m_conv_b, m_q_norm_g, m_k_norm_g, m_g_out_conv, m_g_out_attn, m_w_out, m_g_ffn, m_w_gate, m_w_up, m_ffn_conv_w, m_ffn_conv_b, m_w_down, m_g_ple, m_w_ple_gate, m_w_ple_proj, v_g_mix, v_w_in, v_conv_w, v_conv_b, v_q_norm_g, v_k_norm_g, v_g_out_conv, v_g_out_attn, v_w_out, v_g_ffn, v_w_gate, v_w_up, v_ffn_conv_w, v_ffn_conv_b, v_w_down, v_g_ple, v_w_ple_gate, v_w_ple_proj):
    args = dict(locals())
    names = ["g_mix", "w_in", "conv_w", "conv_b", "q_norm_g", "k_norm_g", "g_out_conv", "g_out_attn", "w_out", "g_ffn",
             "w_gate", "w_up", "ffn_conv_w", "ffn_conv_b", "w_down", "g_ple", "w_ple_gate", "w_ple_proj"]
    big = list(BIG)
    conv = [n for n, _ in CONV_SHARDED]

    def local(prefix):
        out = {n: (args[prefix + n][0] if n in big or n in conv else args[prefix + n]) for n in names}
        out.update({n: out[n].T for n in TRANSPOSED})
        return out

    wts, mom, var = local(""), local("m_"), local("v_")
    shard_shapes = {n: wts[n].shape for n in big}
    dev = 4 * lax.axis_index("x") + 2 * lax.axis_index("y") + lax.axis_index("c")
    core = lax.axis_index("c").astype(jnp.int32).reshape(1)

    conv_local = _pad_rows(jnp.concatenate([wts[n].reshape(-1) for n in conv]), 8).reshape(8, 1024)
    late = [n for n in big if n != "w_in"]
    w_in_all, conv_all = _all_gather([wts["w_in"].astype(BF16), conv_local], "gather_weights")
    late_shards = [wts[n].astype(BF16) for n in late]
    gathering, token = _split_start("gather_late_weights", late_shards,
                                    [lax.empty((N_DEV,) + s.shape, BF16) for s in late_shards], _gather_plan,
                                    7 * len(late), w_in_all)
    full = dict(wts)
    full["w_in"] = _gathered_to_full("w_in", w_in_all)
    full["g_mix"] = _ordered_after(wts["g_mix"], token)
    flying = {}

    def late_weights(after):
        shards, lands = _split_wait("gather_late_weights", gathering, _gather_plan, after)
        return {n: _gathered_to_full(n, lax.dynamic_update_slice(land, shard[None], (dev, 0, 0)))
                for n, land, shard in zip(late, lands, shards)}

    early = ["w_ple_gate", "w_ple_proj", "w_down", "w_up", "w_gate"]

    def ffn_grads(g):
        stacked = [_full_to_stacked(n, g[n], shard_shapes[n]) for n in early]
        flying["sibling"], tok = _split_start("rs_sibling_early", stacked,
                                              [lax.empty((N_CHIP,) + s.shape[2:], BF16) for s in stacked],
                                              _sibling_plan, N_CHIP * len(early), g["w_down"])
        return tok

    def outproj_done(after):
        stacked, landed = _split_wait("rs_sibling_early", flying["sibling"], _sibling_plan, after)
        parts = _pair_sums(stacked, landed, core, "rs_pair_sums_early")
        flying["chip"], tok = _split_start("rs_chip_early", parts, [lax.empty(q.shape, BF16) for q in parts],
                                           _chip_plan, 3 * len(early), landed[0])
        return tok

    off = 0
    for n, width in CONV_SHARDED:
        sc = width // N_DEV
        a = conv_all.reshape(N_DEV, -1)[:, off:off + 3 * sc].reshape(N_DEV, 3, sc)
        full[n] = a.transpose(1, 0, 2).reshape(3, width)
        off += 3 * sc

    loss, dx, grads = _local_step(x[0], p[0, 0], loss_target[0], full, (512, 256),
                                  {"late_weights": late_weights, "ffn_grads": ffn_grads, "outproj_done": outproj_done})

    chip = (2 * lax.axis_index("x") + lax.axis_index("y")).astype(jnp.int32).reshape(1)

    def adamw_of(group, parts, arrived):
        return {n: _adamw(own, got, chip, wts[n], mom[n], var[n], f"adamw_{n}")
                for n, own, got in zip(group, parts, arrived)}

    last = [n for n in big if n not in early]
    stacked = [_full_to_stacked(n, grads[n], shard_shapes[n]) for n in last]
    flying["sibling_last"], tok = _split_start("rs_sibling_last", stacked,
                                               [lax.empty((N_CHIP,) + s.shape[2:], BF16) for s in stacked],
                                               _sibling_plan, N_CHIP * len(last), dx)
    (small_all,) = _all_gather([_ordered_after(_pack_small(grads, loss), tok)], "gather_small_grads")
    stacked, landed = _split_wait("rs_sibling_last", flying["sibling_last"], _sibling_plan, small_all)
    parts = _pair_sums(stacked, landed, core, "rs_pair_sums_last")
    flying["chip_last"], tok = _split_start("rs_chip_last", parts, [lax.empty(q.shape, BF16) for q in parts],
                                            _chip_plan, 3 * len(last), landed[0])

    parts, arrived = _split_wait("rs_chip_early", flying["chip"], _chip_plan, tok)
    out = adamw_of(early, parts, arrived)
    small_all = _ordered_after(small_all, tok)
    taps = small_all[:, CONV_W_ROW:CONV_W_ROW + 3, 0:CONV_W]
    ftaps = small_all[:, FFN_CONV_W_ROW:FFN_CONV_W_ROW + 9, :].reshape(N_DEV, 3, 3 * 1024)
    small_out, loss_total = _adamw_small(
        small_all, lax.dynamic_slice(taps, (0, 0, dev * (CONV_W // N_DEV)), (N_DEV, 3, CONV_W // N_DEV)),
        lax.dynamic_slice(ftaps, (0, 0, dev * (D_FF // N_DEV)), (N_DEV, 3, D_FF // N_DEV)), wts, mom, var)
    out.update(small_out)
    parts, arrived = _split_wait("rs_chip_last", flying["chip_last"], _chip_plan, small_out["g_mix"][0])
    out.update(adamw_of(last, parts, arrived))
    def result(n, which):
        a = out[n][which]
        return (a.T if n in TRANSPOSED else a).reshape(args[n].shape)

    return (loss_total, dx[None], *[result(n, which) for which in range(4) for n in names])
```

```python
import jax
import jax.numpy as jnp
from jax import lax
from jax.experimental import pallas as pl
from jax.experimental.pallas import tpu as pltpu

F32 = jnp.float32
BF16 = jnp.bfloat16

D_MODEL = 1024
CONV_W = 512
ATTN_W = 512
HEAD_DIM = 64
D_FF = 2816
PLE_DIM = 256
IN_COLS = 3 * CONV_W + 3 * ATTN_W
EPS = 1e-6
QK_BLOCK = 128
DILATIONS = (1, 4, 16)
ATTN_SCALE = HEAD_DIM ** -0.5

ADAM_LR = 0.001
ADAM_B1 = 0.9
ADAM_B2 = 0.999
ADAM_EPS = 1e-08
ADAM_WD = 0.01
ADAM_STEP = 10

N_DEV = 8
N_CHIP = 4
V7X_VMEM_LIMIT = 56 * 1024 * 1024
V7X_VMEM_LIMIT_LARGE = 62 * 1024 * 1024
FF_CHUNKS = 2
FFN_BWD_PARTS = 1

BIG = ("w_in", "w_out", "w_gate", "w_up", "w_down", "w_ple_gate", "w_ple_proj")
SMALL_ROWS = 24


def _cparams(*sem, vmem=V7X_VMEM_LIMIT):
    return pltpu.CompilerParams(dimension_semantics=sem, vmem_limit_bytes=vmem)


def _mm(a, b):
    return jnp.dot(a, b, preferred_element_type=F32)


def _mm_nt(a, b):
    return lax.dot_general(a, b, (((1,), (1,)), ((), ())), preferred_element_type=F32)


def _mm_tn(a, b):
    return lax.dot_general(a, b, (((0,), (0,)), ((), ())), preferred_element_type=F32)


def _full(shape):
    nd = len(shape)
    return pl.BlockSpec(shape, lambda *_: (0,) * nd)


def _rms_stats(x):
    r = lax.rsqrt(jnp.mean(x * x, axis=-1, keepdims=True) + EPS)
    return r, x * r


def _rms_bwd(dy, xhat, r, g):
    gd = dy * g
    return r * (gd - xhat * jnp.mean(gd * xhat, axis=-1, keepdims=True))


def _seg_sum64(v, bd_ref):
    outs = []
    for c in range(0, v.shape[1], 256):
        vc = v[:, c:c + 256]
        hi = vc.astype(BF16)
        lo = (vc - hi.astype(F32)).astype(BF16)
        outs.append(_mm(hi, bd_ref[...]) + _mm(lo, bd_ref[...]))
    return outs[0] if len(outs) == 1 else jnp.concatenate(outs, axis=1)


def _shift_rows(u, k, edge_rows):
    out = pltpu.roll(u, k, 0)
    row = lax.broadcasted_iota(jnp.int32, (8, u.shape[1]), 0)
    head = out[0:8]
    for j in range(k):
        head = jnp.where(row == j, edge_rows[k - 1 - j], head)
    return jnp.concatenate([head, out[8:]], axis=0)


def _shift_rows_up(u, k, edge_rows):
    n = u.shape[0]
    out = pltpu.roll(u, n - k, 0)
    row = lax.broadcasted_iota(jnp.int32, (8, u.shape[1]), 0)
    tail = out[n - 8:n]
    for j in range(k):
        tail = jnp.where(row == 8 - k + j, edge_rows[j], tail)
    return jnp.concatenate([out[0:n - 8], tail], axis=0)


def _conv_fwd(u, c1, c2, w_ref, b_ref):
    u1 = _shift_rows(u, 1, (c1,))
    u2 = _shift_rows(u, 2, (c1, c2))
    y = u2 * w_ref[0:1, :] + u1 * w_ref[1:2, :] + u * w_ref[2:3, :] + b_ref[...]
    return y, u1, u2


def _conv_bwd_input(dy, n1row, n2row, w_ref):
    d1 = _shift_rows_up(dy, 1, (n1row,))
    d2 = _shift_rows_up(dy, 2, (n1row, n2row))
    return dy * w_ref[2:3, :] + d1 * w_ref[1:2, :] + d2 * w_ref[0:1, :]


def _sigmoid(x):
    return 1.0 / (1.0 + jnp.exp(-x))


def _inproj_fwd(x, g_mix, w_in, conv_w, conv_b, qg, kg, bd, tm):
    t = x.shape[0]

    def body(x_ref, g_ref, w_ref, cw_ref, cb_ref, qg_ref, kg_ref, bd_ref,
             zc_ref, zqk_ref, yc_ref, q_ref, k_ref, v_ref, carry_ref):
        @pl.when(pl.program_id(0) == 0)
        def _():
            carry_ref[...] = jnp.zeros_like(carry_ref)

        _, xhat = _rms_stats(x_ref[...])
        h = (xhat * g_ref[...]).astype(BF16)
        zconv = _mm(h, w_ref[:, 0:3 * CONV_W])
        zc_ref[...] = zconv.astype(BF16)
        u = zconv[:, CONV_W:2 * CONV_W] * zconv[:, 2 * CONV_W:3 * CONV_W]
        cv, _, _ = _conv_fwd(u, carry_ref[7:8, :], carry_ref[6:7, :], cw_ref, cb_ref)
        yc_ref[...] = (zconv[:, 0:CONV_W] * cv).astype(BF16)
        carry_ref[...] = u[tm - 8:tm, :]

        zqk = _mm(h, w_ref[:, 3 * CONV_W:3 * CONV_W + 2 * ATTN_W])
        zqk_ref[...] = zqk.astype(BF16)
        for j, (gain_ref, out_ref, scale) in enumerate(((qg_ref, q_ref, ATTN_SCALE), (kg_ref, k_ref, 1.0))):
            z = zqk[:, j * ATTN_W:(j + 1) * ATTN_W]
            r = lax.rsqrt(_seg_sum64(z * z, bd_ref) * (1.0 / HEAD_DIM) + EPS)
            out_ref[...] = z * r * gain_ref[...] * scale
        v_ref[...] = _mm(h, w_ref[:, 3 * CONV_W + 2 * ATTN_W:IN_COLS])

    def blk(c):
        return pl.BlockSpec((tm, c), lambda i: (i, 0))

    return pl.pallas_call(
        body, name="inproj_fwd", grid=(t // tm,),
        in_specs=[blk(D_MODEL), _full((1, D_MODEL)), _full((D_MODEL, IN_COLS)), _full((3, CONV_W)),
                  _full((1, CONV_W)), _full((1, ATTN_W)), _full((1, ATTN_W)), _full((256, 256))],
        out_specs=[blk(3 * CONV_W), blk(2 * ATTN_W), blk(CONV_W), blk(ATTN_W), blk(ATTN_W), blk(ATTN_W)],
        out_shape=[jax.ShapeDtypeStruct((t, 3 * CONV_W), BF16), jax.ShapeDtypeStruct((t, 2 * ATTN_W), BF16),
                   jax.ShapeDtypeStruct((t, CONV_W), BF16), jax.ShapeDtypeStruct((t, ATTN_W), F32),
                   jax.ShapeDtypeStruct((t, ATTN_W), F32), jax.ShapeDtypeStruct((t, ATTN_W), F32)],
        scratch_shapes=[pltpu.VMEM((8, CONV_W), F32)],
        compiler_params=_cparams("arbitrary"),
    )(x, g_mix, w_in, conv_w, conv_b, qg, kg, bd)


SUPER = 16 * QK_BLOCK
KEYS = 2 * QK_BLOCK
UNITS = SUPER // QK_BLOCK


def _rows(start, size, dil):
    return pl.ds(start, size) if dil == 1 else pl.ds(start, size, stride=dil)


def _attn_bias(sl_ref, dil):
    qi = lax.broadcasted_iota(jnp.int32, (KEYS, KEYS), 0)
    kj = lax.broadcasted_iota(jnp.int32, (KEYS, KEYS), 1)
    step = jnp.bitwise_and(qi, QK_BLOCK - 1) + QK_BLOCK - kj
    slope = jnp.where(qi < QK_BLOCK, sl_ref[0, 0:1, 0:1], sl_ref[0, 1:2, 0:1])
    bias = jnp.where(jnp.logical_and(step >= 0, step <= QK_BLOCK), -slope * (step * dil).astype(F32), -jnp.inf)
    return bias, kj >= QK_BLOCK


def _unit_start(u, dil):
    if dil == 1:
        return pl.multiple_of(u * QK_BLOCK, QK_BLOCK)
    if dil == 4:
        return jnp.bitwise_and(u, 3) + (u // 4) * (4 * QK_BLOCK)
    return u


def _stack_heads(a, head0):
    zero = jnp.zeros_like(a)
    return jnp.concatenate([jnp.where(head0, a, zero), jnp.where(head0, zero, a)], axis=0)


def _attn_fwd(q, k, v, slopes):
    t = q.shape[0]
    nsb = t // SUPER

    def body(q_ref, kc_ref, kp_ref, vc_ref, vp_ref, sl_ref, o_ref, l_ref, e_ref, m_ref, kk, vv, ob, lb):
        s = pl.program_id(1)
        kk[0:SUPER, :] = kp_ref[...]
        kk[SUPER:, :] = kc_ref[...]
        vv[0:SUPER, :] = vp_ref[...]
        vv[SUPER:, :] = vc_ref[...]
        head0 = lax.broadcasted_iota(jnp.int32, (QK_BLOCK, QK_BLOCK), 1) < HEAD_DIM

        for b, dil in enumerate(DILATIONS):
            bias, own_half = _attn_bias(sl_ref, dil)

            def unit(u, carry, b=b, dil=dil, bias=bias, own_half=own_half):
                start = _unit_start(u, dil)
                first_key = SUPER + start - QK_BLOCK * dil
                q2 = _stack_heads(q_ref[_rows(start, QK_BLOCK, dil), :].astype(BF16), head0)
                k2 = kk[_rows(first_key, KEYS, dil), :].astype(BF16)
                v2 = vv[_rows(first_key, KEYS, dil), :].astype(BF16)
                has_prev = jnp.logical_or(s > 0, start >= QK_BLOCK * dil)
                sc = jnp.where(jnp.logical_or(own_half, has_prev), _mm_nt(q2, k2) + bias, -jnp.inf)
                m = jnp.max(sc, axis=-1, keepdims=True)
                e = jnp.exp(sc - m)
                den = jnp.sum(e, axis=-1, keepdims=True)
                eb = e.astype(BF16)
                e_ref[b * UNITS + u] = eb
                o2 = _mm(eb, v2) / den
                l2 = m + jnp.log(den)
                ob[b, _rows(start, QK_BLOCK, dil), :] = jnp.where(head0, o2[0:QK_BLOCK], o2[QK_BLOCK:])
                lb[b, _rows(start, QK_BLOCK, dil), :] = jnp.where(head0, l2[0:QK_BLOCK], l2[QK_BLOCK:])
                m_ref[b, _rows(start, QK_BLOCK, dil), :] = jnp.where(head0, m[0:QK_BLOCK], m[QK_BLOCK:])
                return carry

            lax.fori_loop(0, UNITS, unit, 0, unroll=16)

        def merge(i, carry):
            rows = pl.ds(pl.multiple_of(i * 256, 256), 256)
            la, lb_, lc = lb[0, rows, :], lb[1, rows, :], lb[2, rows, :]
            mx = jnp.maximum(jnp.maximum(la, lb_), lc)
            wa, wb, wc = jnp.exp(la - mx), jnp.exp(lb_ - mx), jnp.exp(lc - mx)
            sw = wa + wb + wc
            o_ref[rows, :] = ((wa * ob[0, rows, :] + wb * ob[1, rows, :] + wc * ob[2, rows, :]) / sw).astype(BF16)
            l_ref[rows, :] = mx + jnp.log(sw)
            return carry

        lax.fori_loop(0, SUPER // 256, merge, 0)

    cur = pl.BlockSpec((SUPER, QK_BLOCK), lambda p, s: (s, p))
    prev = pl.BlockSpec((SUPER, QK_BLOCK), lambda p, s: (jnp.maximum(s - 1, 0), p))
    return pl.pallas_call(
        body, name="attn_fwd", grid=(4, nsb),
        in_specs=[cur, cur, prev, cur, prev, pl.BlockSpec((1, 2, QK_BLOCK), lambda p, s: (p, 0, 0))],
        out_specs=[cur, cur, pl.BlockSpec((None, None, 3 * UNITS, KEYS, KEYS), lambda p, s: (p, s, 0, 0, 0)),
                   pl.BlockSpec((3, SUPER, QK_BLOCK), lambda p, s: (0, s, p))],
        out_shape=[jax.ShapeDtypeStruct((t, ATTN_W), BF16), jax.ShapeDtypeStruct((t, ATTN_W), F32),
                   jax.ShapeDtypeStruct((4, nsb, 3 * UNITS, KEYS, KEYS), BF16),
                   jax.ShapeDtypeStruct((3, t, ATTN_W), F32)],
        scratch_shapes=[pltpu.VMEM((2 * SUPER, QK_BLOCK), F32), pltpu.VMEM((2 * SUPER, QK_BLOCK), F32),
                        pltpu.VMEM((3, SUPER, QK_BLOCK), F32), pltpu.VMEM((3, SUPER, QK_BLOCK), F32)],
        compiler_params=_cparams("parallel", "arbitrary"),
    )(q, k, k, v, v, slopes)


def _outproj_fwd(ya, yc, x, goc, goa, w_out, tm):
    t = x.shape[0]

    def body(ya_ref, yc_ref, x_ref, goc_ref, goa_ref, w_ref, x1_ref):
        _, ychat = _rms_stats(yc_ref[...].astype(F32))
        _, yahat = _rms_stats(ya_ref[...].astype(F32))
        acc = _mm((ychat * goc_ref[...]).astype(BF16), w_ref[0:CONV_W, :])
        acc += _mm((yahat * goa_ref[...]).astype(BF16), w_ref[CONV_W:, :])
        x1_ref[...] = x_ref[...] + acc

    def blk(c):
        return pl.BlockSpec((tm, c), lambda i: (i, 0))

    return pl.pallas_call(
        body, name="outproj_fwd", grid=(t // tm,),
        in_specs=[blk(ATTN_W), blk(CONV_W), blk(D_MODEL), _full((1, CONV_W)), _full((1, ATTN_W)),
                  _full((D_MODEL, D_MODEL))],
        out_specs=blk(D_MODEL),
        out_shape=jax.ShapeDtypeStruct((t, D_MODEL), F32),
        compiler_params=_cparams("parallel"),
    )(ya, yc, x, goc, goa, w_out)


def _ffn_fwd(x1, g_ffn, w_gate_t, w_up_t, w_down, fcw, fcb, tm):
    t = x1.shape[0]

    def body(x_ref, g_ref, wg_ref, wu_ref, wd_ref, cw_ref, cb_ref, gp_ref, up_ref, h_ref, x2_ref, carry_ref):
        @pl.when(pl.program_id(0) == 0)
        def _():
            carry_ref[...] = jnp.zeros_like(carry_ref)

        xv = x_ref[...]
        _, xhat = _rms_stats(xv)
        h = (xhat * g_ref[...]).astype(BF16)
        h_ref[...] = h
        gp = _mm_nt(h, wg_ref[...])
        gp_ref[...] = gp.astype(BF16)
        gate, _, _ = _conv_fwd(gp, carry_ref[7:8, :], carry_ref[6:7, :], cw_ref, cb_ref)
        carry_ref[...] = gp[tm - 8:tm, :]
        up = _mm_nt(h, wu_ref[...])
        up_ref[...] = up.astype(BF16)
        a = (gate * _sigmoid(gate) * up).astype(BF16)
        x2_ref[...] = xv + _mm(a, wd_ref[...])

    def blk(c):
        return pl.BlockSpec((tm, c), lambda i: (i, 0))

    return pl.pallas_call(
        body, name="ffn_fwd", grid=(t // tm,),
        in_specs=[blk(D_MODEL), _full((1, D_MODEL)), _full((D_FF, D_MODEL)), _full((D_FF, D_MODEL)),
                  _full((D_FF, D_MODEL)), _full((3, D_FF)), _full((1, D_FF))],
        out_specs=[blk(D_FF), blk(D_FF), blk(D_MODEL), blk(D_MODEL)],
        out_shape=[jax.ShapeDtypeStruct((t, D_FF), BF16), jax.ShapeDtypeStruct((t, D_FF), BF16),
                   jax.ShapeDtypeStruct((t, D_MODEL), BF16), jax.ShapeDtypeStruct((t, D_MODEL), F32)],
        scratch_shapes=[pltpu.VMEM((8, D_FF), F32)],
        compiler_params=_cparams("arbitrary"),
    )(x1, g_ffn, w_gate_t, w_up_t, w_down, fcw, fcb)


def _ple_fwd_bwd(x2, p, target, g_ple, w_pg, w_pp, tm):
    t = x2.shape[0]

    def body(x_ref, p_ref, t_ref, g_ref, wg_ref, wp_ref, dx_ref, dxb_ref, loss_ref, dwg_ref, dwp_ref, dg_ref):
        @pl.when(pl.program_id(0) == 0)
        def _():
            loss_ref[...] = jnp.zeros_like(loss_ref)
            dwg_ref[...] = jnp.zeros_like(dwg_ref)
            dwp_ref[...] = jnp.zeros_like(dwp_ref)
            dg_ref[...] = jnp.zeros_like(dg_ref)

        xv = x_ref[...]
        r, xhat = _rms_stats(xv)
        g = g_ref[...]
        h = (xhat * g).astype(BF16)
        pg = _sigmoid(_mm(h, wg_ref[...]))
        pb = p_ref[...].astype(BF16)
        pp = _mm(pb, wp_ref[...])
        err = xv + pg * pp - t_ref[...]
        loss_ref[...] += 0.5 * jnp.sum(jnp.mean(err * err, axis=-1, keepdims=True))
        dx3 = err * (1.0 / D_MODEL)
        d_pp = (dx3 * pg).astype(BF16)
        d_pre = (dx3 * pp * pg * (1.0 - pg)).astype(BF16)
        dwp_ref[...] += _mm_tn(pb, d_pp)
        dwg_ref[...] += _mm_tn(h, d_pre)
        dh = _mm_nt(d_pre, wg_ref[...])
        dg_ref[...] += jnp.sum(dh * xhat, axis=0, keepdims=True)
        dx2 = dx3 + _rms_bwd(dh, xhat, r, g)
        dx_ref[...] = dx2
        dxb_ref[...] = dx2.astype(BF16)

    def blk(c):
        return pl.BlockSpec((tm, c), lambda i: (i, 0))

    return pl.pallas_call(
        body, name="ple_fwd_bwd", grid=(t // tm,),
        in_specs=[blk(D_MODEL), blk(PLE_DIM), blk(D_MODEL), _full((1, D_MODEL)), _full((D_MODEL, D_MODEL)),
                  _full((PLE_DIM, D_MODEL))],
        out_specs=[blk(D_MODEL), blk(D_MODEL), _full((8, 128)), _full((D_MODEL, D_MODEL)),
                   _full((PLE_DIM, D_MODEL)), _full((1, D_MODEL))],
        out_shape=[jax.ShapeDtypeStruct((t, D_MODEL), F32), jax.ShapeDtypeStruct((t, D_MODEL), BF16),
                   jax.ShapeDtypeStruct((8, 128), F32),
                   jax.ShapeDtypeStruct((D_MODEL, D_MODEL), F32), jax.ShapeDtypeStruct((PLE_DIM, D_MODEL), F32),
                   jax.ShapeDtypeStruct((1, D_MODEL), F32)],
        compiler_params=_cparams("arbitrary"),
    )(x2, p, target, g_ple, w_pg, w_pp)


def _ffn_bwd(dx2, h2, gp, up, w_gate, w_up, w_down, fcw, fcb, tm):
    t = dx2.shape[0]
    nblk = t // tm
    fc = D_FF // FF_CHUNKS
    half = tm // FFN_BWD_PARTS

    def body(dx_ref, h_ref, gp_ref, gph_ref, up_ref, wg_ref, wu_ref, wd_ref, cw_ref, cb_ref,
             dh_ref, dwd_hbm, dwu_hbm, dwg_hbm, dcw_ref, dcb_ref, carry_ref, a_scr, dup_scr, dgp_scr,
             dwd_acc, dwu_acc, dwg_acc, stage, stage_sem):
        i = pl.program_id(1)

        @pl.when(i == 0)
        def _():
            carry_ref[...] = jnp.zeros_like(carry_ref)
            dwd_acc[...] = jnp.zeros_like(dwd_acc)
            dwu_acc[...] = jnp.zeros_like(dwu_acc)
            dwg_acc[...] = jnp.zeros_like(dwg_acc)
            dcw_ref[...] = jnp.zeros_like(dcw_ref)
            dcb_ref[...] = jnp.zeros_like(dcb_ref)

        keep = (i < nblk - 1).astype(F32)
        later = carry_ref[...]
        for hf in reversed(range(FFN_BWD_PARTS)):
            rows = slice(hf * half, (hf + 1) * half)
            dxb = dx_ref[rows, :]
            gp_v = gp_ref[rows, :].astype(F32)
            if hf > 0:
                before = gp_ref[hf * half - 16:hf * half, :].astype(F32)
            else:
                before = gph_ref[...].astype(F32) * keep
            gate, gp1, gp2 = _conv_fwd(gp_v, before[15:16, :], before[14:15, :], cw_ref, cb_ref)
            s = _sigmoid(gate)
            silu = gate * s
            up_v = up_ref[rows, :].astype(F32)
            da = _mm_nt(dxb, wd_ref[...])
            a_scr[rows, :] = (silu * up_v).astype(BF16)
            d_up = (da * silu).astype(BF16)
            dup_scr[rows, :] = d_up
            d_gate = da * up_v * (s * (1.0 + gate * (1.0 - s)))
            d_gp = _conv_bwd_input(d_gate, later[0:1, :], later[1:2, :], cw_ref).astype(BF16)
            dgp_scr[rows, :] = d_gp
            later = d_gate[0:8, :]
            dcw_ref[0:1, :] += jnp.sum(d_gate * gp2, axis=0, keepdims=True)
            dcw_ref[1:2, :] += jnp.sum(d_gate * gp1, axis=0, keepdims=True)
            dcw_ref[2:3, :] += jnp.sum(d_gate * gp_v, axis=0, keepdims=True)
            dcb_ref[...] += jnp.sum(d_gate, axis=0, keepdims=True)
            dh_ref[rows, :] = (_mm(d_gp, wg_ref[...]) + _mm(d_up, wu_ref[...])).astype(BF16)
        carry_ref[...] = later
        dwd_acc[...] += _mm_tn(a_scr[...], dx_ref[...])
        dwu_acc[...] += _mm_tn(h_ref[...], dup_scr[...])
        dwg_acc[...] += _mm_tn(h_ref[...], dgp_scr[...])

        @pl.when(i == nblk - 1)
        def _():
            rows = pl.ds(pl.multiple_of(pl.program_id(0) * fc, 16), fc)
            for acc, out, flip in ((dwd_acc, dwd_hbm, False), (dwu_acc, dwu_hbm, True), (dwg_acc, dwg_hbm, True)):
                stage[...] = (acc[...].T if flip else acc[...]).astype(BF16)
                copy = pltpu.make_async_copy(stage, out.at[rows, :], stage_sem)
                copy.start()
                copy.wait()

    def rev(i):
        return nblk - 1 - i

    one = pl.Buffered(1)
    in_specs = [
        pl.BlockSpec((tm, D_MODEL), lambda j, i: (rev(i), 0)),
        pl.BlockSpec((tm, D_MODEL), lambda j, i: (rev(i), 0)),
        pl.BlockSpec((tm, fc), lambda j, i: (rev(i), j)),
        pl.BlockSpec((16, fc), lambda j, i: (jnp.maximum(rev(i) * (tm // 16) - 1, 0), j)),
        pl.BlockSpec((tm, fc), lambda j, i: (rev(i), j)),
        pl.BlockSpec((fc, D_MODEL), lambda j, i: (j, 0), pipeline_mode=one),
        pl.BlockSpec((fc, D_MODEL), lambda j, i: (j, 0), pipeline_mode=one),
        pl.BlockSpec((fc, D_MODEL), lambda j, i: (j, 0), pipeline_mode=one),
        pl.BlockSpec((3, fc), lambda j, i: (0, j)),
        pl.BlockSpec((1, fc), lambda j, i: (0, j)),
    ]
    out_specs = [
        pl.BlockSpec((None, tm, D_MODEL), lambda j, i: (j, rev(i), 0)),
        ANY, ANY, ANY,
        pl.BlockSpec((3, fc), lambda j, i: (0, j)),
        pl.BlockSpec((1, fc), lambda j, i: (0, j)),
    ]
    return pl.pallas_call(
        body, name="ffn_bwd", grid=(FF_CHUNKS, nblk), in_specs=in_specs, out_specs=out_specs,
        out_shape=[jax.ShapeDtypeStruct((FF_CHUNKS, t, D_MODEL), BF16), jax.ShapeDtypeStruct((D_FF, D_MODEL), BF16),
                   jax.ShapeDtypeStruct((D_FF, D_MODEL), BF16), jax.ShapeDtypeStruct((D_FF, D_MODEL), BF16),
                   jax.ShapeDtypeStruct((3, D_FF), F32), jax.ShapeDtypeStruct((1, D_FF), F32)],
        scratch_shapes=[pltpu.VMEM((8, fc), F32), pltpu.VMEM((tm, fc), BF16), pltpu.VMEM((tm, fc), BF16),
                        pltpu.VMEM((tm, fc), BF16), pltpu.VMEM((fc, D_MODEL), F32), pltpu.VMEM((D_MODEL, fc), F32),
                        pltpu.VMEM((D_MODEL, fc), F32), pltpu.VMEM((fc, D_MODEL), BF16), pltpu.SemaphoreType.DMA],
        compiler_params=_cparams("arbitrary", "arbitrary", vmem=V7X_VMEM_LIMIT_LARGE),
    )(dx2, h2, gp, gp, up, w_gate, w_up, w_down, fcw, fcb)


def _outproj_bwd(dh2, dx2, x1, g_ffn, w_out, yc, ya, goc, goa, zconv, conv_w, conv_b, bd, tm):
    t = x1.shape[0]
    nblk = t // tm

    def body(dh_ref, dx2_ref, x1_ref, g_ref, w_ref, yc_ref, ya_ref, goc_ref, goa_ref, zc_ref, zch_ref, cw_ref, cb_ref,
             bd_ref, dx1_ref, dya_ref, dd_ref, dzc_ref, dw_ref, dg_ref, dgoc_ref, dgoa_ref, dcw_ref, dcb_ref,
             carry_ref):
        i = pl.program_id(0)

        @pl.when(i == 0)
        def _():
            carry_ref[...] = jnp.zeros_like(carry_ref)
            for ref in (dw_ref, dg_ref, dgoc_ref, dgoa_ref, dcw_ref, dcb_ref):
                ref[...] = jnp.zeros_like(ref)

        keep = (i < nblk - 1).astype(F32)
        dh2_v = dh_ref[0].astype(F32)
        for j in range(1, FF_CHUNKS):
            dh2_v = dh2_v + dh_ref[j].astype(F32)
        r, xhat = _rms_stats(x1_ref[...])
        dg_ref[...] += jnp.sum(dh2_v * xhat, axis=0, keepdims=True)
        dx1 = dx2_ref[...] + _rms_bwd(dh2_v, xhat, r, g_ref[...])
        dx1_ref[...] = dx1
        dx1b = dx1.astype(BF16)
        dy = _mm_nt(dx1b, w_ref[...])

        yc_v = yc_ref[...].astype(F32)
        rc, ychat = _rms_stats(yc_v)
        dw_ref[0:CONV_W, :] += _mm_tn((ychat * goc_ref[...]).astype(BF16), dx1b)
        dyc = dy[:, 0:CONV_W]
        dgoc_ref[...] += jnp.sum(dyc * ychat, axis=0, keepdims=True)
        d_yc = _rms_bwd(dyc, ychat, rc, goc_ref[...])

        ya_v = ya_ref[...].astype(F32)
        ra, yahat = _rms_stats(ya_v)
        dw_ref[CONV_W:, :] += _mm_tn((yahat * goa_ref[...]).astype(BF16), dx1b)
        dya = dy[:, CONV_W:]
        dgoa_ref[...] += jnp.sum(dya * yahat, axis=0, keepdims=True)
        d_ya = _rms_bwd(dya, yahat, ra, goa_ref[...])
        dya_ref[...] = d_ya
        dd_ref[...] = _seg_sum64(d_ya * ya_v, bd_ref)

        zb = zc_ref[:, 0:CONV_W].astype(F32)
        zc = zc_ref[:, CONV_W:2 * CONV_W].astype(F32)
        zx = zc_ref[:, 2 * CONV_W:3 * CONV_W].astype(F32)
        u = zc * zx
        uh = (zch_ref[:, CONV_W:2 * CONV_W].astype(F32) * zch_ref[:, 2 * CONV_W:3 * CONV_W].astype(F32)) * keep
        cv, u1, u2 = _conv_fwd(u, uh[15:16, :], uh[14:15, :], cw_ref, cb_ref)
        d_cv = d_yc * zb
        d_u = _conv_bwd_input(d_cv, carry_ref[0:1, :], carry_ref[1:2, :], cw_ref)
        carry_ref[...] = d_cv[0:8, :]
        dcw_ref[0:1, :] += jnp.sum(d_cv * u2, axis=0, keepdims=True)
        dcw_ref[1:2, :] += jnp.sum(d_cv * u1, axis=0, keepdims=True)
        dcw_ref[2:3, :] += jnp.sum(d_cv * u, axis=0, keepdims=True)
        dcb_ref[...] += jnp.sum(d_cv, axis=0, keepdims=True)
        dzc_ref[:, 0:CONV_W] = (d_yc * cv).astype(BF16)
        dzc_ref[:, CONV_W:2 * CONV_W] = (d_u * zx).astype(BF16)
        dzc_ref[:, 2 * CONV_W:3 * CONV_W] = (d_u * zc).astype(BF16)

    def rev(i):
        return nblk - 1 - i

    def blk(c):
        return pl.BlockSpec((tm, c), lambda i: (rev(i), 0))

    in_specs = [
        pl.BlockSpec((FF_CHUNKS, tm, D_MODEL), lambda i: (0, rev(i), 0)),
        blk(D_MODEL), blk(D_MODEL), _full((1, D_MODEL)), _full((D_MODEL, D_MODEL)),
        blk(CONV_W), blk(ATTN_W), _full((1, CONV_W)), _full((1, ATTN_W)),
        blk(3 * CONV_W),
        pl.BlockSpec((16, 3 * CONV_W), lambda i: (jnp.maximum(rev(i) * (tm // 16) - 1, 0), 0)),
        _full((3, CONV_W)), _full((1, CONV_W)), _full((256, 256)),
    ]
    out_specs = [blk(D_MODEL), blk(ATTN_W), blk(ATTN_W), blk(3 * CONV_W), _full((D_MODEL, D_MODEL)),
                 _full((1, D_MODEL)), _full((1, CONV_W)), _full((1, ATTN_W)), _full((3, CONV_W)), _full((1, CONV_W))]
    return pl.pallas_call(
        body, name="outproj_bwd", grid=(nblk,), in_specs=in_specs, out_specs=out_specs,
        out_shape=[jax.ShapeDtypeStruct((t, D_MODEL), F32), jax.ShapeDtypeStruct((t, ATTN_W), F32),
                   jax.ShapeDtypeStruct((t, ATTN_W), F32), jax.ShapeDtypeStruct((t, 3 * CONV_W), BF16),
                   jax.ShapeDtypeStruct((D_MODEL, D_MODEL), F32), jax.ShapeDtypeStruct((1, D_MODEL), F32),
                   jax.ShapeDtypeStruct((1, CONV_W), F32), jax.ShapeDtypeStruct((1, ATTN_W), F32),
                   jax.ShapeDtypeStruct((3, CONV_W), F32), jax.ShapeDtypeStruct((1, CONV_W), F32)],
        scratch_shapes=[pltpu.VMEM((8, CONV_W), F32)],
        compiler_params=_cparams("arbitrary"),
    )(dh2, dx2, x1, g_ffn, w_out, yc, ya, goc, goa, zconv, zconv, conv_w, conv_b, bd)


def _attn_bwd(q, k, v, dya, lse, dd, e_all, m_all, after):
    t = q.shape[0]
    nsb = t // SUPER

    def body(q_ref, kc_ref, kp_ref, vc_ref, vp_ref, dy_ref, l_ref, d_ref, e_ref, m_ref, after_ref,
             dq_ref, dk_ref, dv_ref, kk, vv, dkacc, dvacc, dwide):
        s = pl.program_id(1)

        @pl.when(s == 0)
        def _():
            dkacc[...] = jnp.zeros_like(dkacc)
            dvacc[...] = jnp.zeros_like(dvacc)

        dkacc[0:SUPER, :] = dkacc[SUPER:, :]
        dvacc[0:SUPER, :] = dvacc[SUPER:, :]
        dkacc[SUPER:, :] = jnp.zeros((SUPER, QK_BLOCK), F32)
        dvacc[SUPER:, :] = jnp.zeros((SUPER, QK_BLOCK), F32)

        @pl.when(s < nsb)
        def _():
            kk[0:SUPER, :] = kp_ref[...]
            kk[SUPER:, :] = kc_ref[...]
            vv[0:SUPER, :] = vp_ref[...]
            vv[SUPER:, :] = vc_ref[...]
            head0 = lax.broadcasted_iota(jnp.int32, (QK_BLOCK, QK_BLOCK), 1) < HEAD_DIM

            def widened(a):
                other = pltpu.roll(a, HEAD_DIM, 1)
                first = lax.broadcasted_iota(jnp.int32, a.shape, 1) < HEAD_DIM
                return jnp.where(first, a, other), jnp.where(first, other, a)

            def stacked(h0, h1):
                return jnp.concatenate([jnp.concatenate([h0, h0], axis=1), jnp.concatenate([h1, h1], axis=1)], axis=0)

            def widen_dd(i, carry):
                rows = pl.ds(pl.multiple_of(i * 256, 256), 256)
                dwide[0, rows, :], dwide[1, rows, :] = widened(d_ref[rows, :])
                return carry

            lax.fori_loop(0, SUPER // 256, widen_dd, 0)

            for b, dil in enumerate(DILATIONS):
                def unit(u, carry, b=b, dil=dil):
                    start = _unit_start(u, dil)
                    first_key = SUPER + start - QK_BLOCK * dil
                    qrows = _rows(start, QK_BLOCK, dil)
                    krows = _rows(first_key, KEYS, dil)
                    q2 = _stack_heads(q_ref[qrows, :].astype(BF16), head0)
                    dy2 = _stack_heads(dy_ref[qrows, :].astype(BF16), head0)
                    g2 = stacked(*widened(jnp.exp(m_ref[b, qrows, :] - l_ref[qrows, :])))
                    d2 = stacked(dwide[0, qrows, :], dwide[1, qrows, :])
                    k2 = kk[krows, :].astype(BF16)
                    v2 = vv[krows, :].astype(BF16)
                    prob = e_ref[b * UNITS + u].astype(F32) * g2
                    ds = (prob * (_mm_nt(dy2, v2) - d2)).astype(BF16)
                    dvacc[krows, :] += _mm_tn(prob.astype(BF16), dy2)
                    dkacc[krows, :] += _mm_tn(ds, q2)
                    dq2 = _mm(ds, k2)
                    dq = jnp.where(head0, dq2[0:QK_BLOCK], dq2[QK_BLOCK:]) * ATTN_SCALE
                    if b == 0:
                        dq_ref[qrows, :] = dq
                    else:
                        dq_ref[qrows, :] += dq
                    return carry

                lax.fori_loop(0, UNITS, unit, 0, unroll=8)

        dk_ref[...] = dkacc[0:SUPER, :]
        dv_ref[...] = dvacc[0:SUPER, :].astype(BF16)

    def cur_map(p, s):
        return (jnp.minimum(s, nsb - 1), p)

    def prev_map(p, s):
        return (jnp.clip(s - 1, 0, nsb - 1), p)

    cur = pl.BlockSpec((SUPER, QK_BLOCK), cur_map)
    prev = pl.BlockSpec((SUPER, QK_BLOCK), prev_map)
    return pl.pallas_call(
        body, name="attn_bwd", grid=(4, nsb + 1),
        in_specs=[cur, cur, prev, cur, prev, cur, cur, cur,
                  pl.BlockSpec((None, None, 3 * UNITS, KEYS, KEYS), lambda p, s: (p, jnp.minimum(s, nsb - 1), 0, 0, 0)),
                  pl.BlockSpec((3, SUPER, QK_BLOCK), lambda p, s: (0, jnp.minimum(s, nsb - 1), p)),
                  pl.BlockSpec(memory_space=pl.ANY)],
        out_specs=[cur, prev, prev],
        out_shape=[jax.ShapeDtypeStruct((t, ATTN_W), F32), jax.ShapeDtypeStruct((t, ATTN_W), F32),
                   jax.ShapeDtypeStruct((t, ATTN_W), BF16)],
        scratch_shapes=[pltpu.VMEM((2 * SUPER, QK_BLOCK), F32)] * 4 + [pltpu.VMEM((2, SUPER, QK_BLOCK), F32)],
        compiler_params=_cparams("parallel", "arbitrary"),
    )(q, k, k, v, v, dya, lse, dd, e_all, m_all, after)


def _inproj_bwd(dq, dk, dv, dzconv, zqk, x, dx1, g_mix, w_in, qg, kg, bd, tm):
    t = x.shape[0]
    nblk = t // tm
    shard = IN_COLS // N_DEV

    def body(dq_ref, dk_ref, dv_ref, dzc_ref, zqk_ref, x_ref, dx1_ref, g_ref, w_ref, qg_ref,
             kg_ref, bd_ref, dx_ref, dw_hbm, dg_ref, dqg_ref, dkg_ref, dw_ref, stage, stage_sem):
        @pl.when(pl.program_id(0) == 0)
        def _():
            for ref in (dw_ref, dg_ref, dqg_ref, dkg_ref):
                ref[...] = jnp.zeros_like(ref)

        parts = [dzc_ref[...]]
        for j, (dn_ref, gain_ref, dgain_ref) in enumerate(((dq_ref, qg_ref, dqg_ref), (dk_ref, kg_ref, dkg_ref))):
            dn = dn_ref[...]
            z = zqk_ref[:, j * ATTN_W:(j + 1) * ATTN_W].astype(F32)
            r = lax.rsqrt(_seg_sum64(z * z, bd_ref) * (1.0 / HEAD_DIM) + EPS)
            zhat = z * r
            dgain_ref[...] += jnp.sum(dn * zhat, axis=0, keepdims=True)
            gd = dn * gain_ref[...]
            parts.append((r * (gd - zhat * (_seg_sum64(gd * zhat, bd_ref) * (1.0 / HEAD_DIM)))).astype(BF16))
        parts.append(dv_ref[...].astype(BF16))
        dz = jnp.concatenate(parts, axis=1)

        r, xhat = _rms_stats(x_ref[...])
        g = g_ref[...]
        dw_ref[...] += _mm_tn((xhat * g).astype(BF16), dz)
        dh = _mm_nt(dz, w_ref[...])
        dg_ref[...] += jnp.sum(dh * xhat, axis=0, keepdims=True)
        dx_ref[...] = dx1_ref[...] + _rms_bwd(dh, xhat, r, g)

        @pl.when(pl.program_id(0) == nblk - 1)
        def _():
            for k in range(N_DEV):
                stage[...] = dw_ref[:, k * shard:(k + 1) * shard].astype(BF16)
                copy = pltpu.make_async_copy(stage, dw_hbm.at[k], stage_sem)
                copy.start()
                copy.wait()

    def blk(c):
        return pl.BlockSpec((tm, c), lambda i: (i, 0))

    return pl.pallas_call(
        body, name="inproj_bwd", grid=(nblk,),
        in_specs=[blk(ATTN_W)] * 3 + [blk(3 * CONV_W), blk(2 * ATTN_W), blk(D_MODEL), blk(D_MODEL), _full((1, D_MODEL)),
                                      _full((D_MODEL, IN_COLS)), _full((1, ATTN_W)), _full((1, ATTN_W)),
                                      _full((256, 256))],
        out_specs=[blk(D_MODEL), ANY, _full((1, D_MODEL)), _full((1, ATTN_W)), _full((1, ATTN_W))],
        out_shape=[jax.ShapeDtypeStruct((t, D_MODEL), F32), jax.ShapeDtypeStruct((N_DEV, D_MODEL, shard), BF16),
                   jax.ShapeDtypeStruct((1, D_MODEL), F32), jax.ShapeDtypeStruct((1, ATTN_W), F32),
                   jax.ShapeDtypeStruct((1, ATTN_W), F32)],
        scratch_shapes=[pltpu.VMEM((D_MODEL, IN_COLS), F32), pltpu.VMEM((D_MODEL, shard), BF16),
                        pltpu.SemaphoreType.DMA],
        compiler_params=_cparams("arbitrary"),
    )(dq, dk, dv, dzconv, zqk, x, dx1, g_mix, w_in, qg, kg, bd)


def _ordered_after(a, token):
    return a if token is None else a + token[0:1, 0:1].reshape((1,) * a.ndim)


def _local_step(x, p, target, w, tms, hooks=None):
    hooks = hooks or {}
    bd = jnp.kron(jnp.eye(4, dtype=F32), jnp.ones((HEAD_DIM, HEAD_DIM), F32)).astype(BF16)
    qg = jnp.tile(w["q_norm_g"], (1, 8))
    kg = jnp.tile(w["k_norm_g"], (1, 8))
    slopes = jnp.exp2(-jnp.arange(1, 9, dtype=F32))
    slopes = jnp.broadcast_to(slopes.reshape(4, 2, 1), (4, 2, QK_BLOCK))

    zconv, zqk, yc, q, k, v = _inproj_fwd(x, w["g_mix"], w["w_in"], w["conv_w"], w["conv_b"], qg, kg, bd, tms[0])
    ya, lse, e_all, m_all = _attn_fwd(q, k, v, slopes)
    if "late_weights" in hooks:
        w = {**w, **hooks["late_weights"](lse)}
    x1 = _outproj_fwd(ya, yc, x, w["g_out_conv"], w["g_out_attn"], w["w_out"], tms[0])
    gp, up, h2, x2 = _ffn_fwd(x1, w["g_ffn"], w["w_gate"], w["w_up"], w["w_down"], w["ffn_conv_w"], w["ffn_conv_b"],
                              tms[1])
    dx2, dx2b, loss, dw_pg, dw_pp, dg_ple = _ple_fwd_bwd(x2, p, target, w["g_ple"], w["w_ple_gate"], w["w_ple_proj"], tms[0])
    dh2, dw_down, dw_up, dw_gate, dfcw, dfcb = _ffn_bwd(dx2b, h2, gp, up, w["w_gate"], w["w_up"], w["w_down"],
                                                        w["ffn_conv_w"], w["ffn_conv_b"], tms[0])
    token = None
    if "ffn_grads" in hooks:
        token = hooks["ffn_grads"]({"w_ple_gate": dw_pg, "w_ple_proj": dw_pp, "w_down": dw_down, "w_up": dw_up,
                                    "w_gate": dw_gate})
    dx1, dya, dd, dzconv, dw_out, dg_ffn, dgoc, dgoa, dcw, dcb = _outproj_bwd(
        dh2, dx2, x1, _ordered_after(w["g_ffn"], token), w["w_out"], yc, ya, w["g_out_conv"], w["g_out_attn"], zconv,
        w["conv_w"], w["conv_b"], bd, tms[1])
    token = hooks["outproj_done"](dx1) if "outproj_done" in hooks else None
    dq, dk, dv = _attn_bwd(q, k, v, dya, lse, dd, e_all, m_all, slopes if token is None else token)
    dx, dw_in, dg_mix, dqg, dkg = _inproj_bwd(dq, dk, dv, dzconv, zqk, x, dx1, w["g_mix"], w["w_in"], qg, kg, bd,
                                              tms[0])
    grads = {
        "g_mix": dg_mix, "w_in": dw_in, "conv_w": dcw, "conv_b": dcb,
        "q_norm_g": dqg.reshape(8, HEAD_DIM).sum(0, keepdims=True),
        "k_norm_g": dkg.reshape(8, HEAD_DIM).sum(0, keepdims=True),
        "g_out_conv": dgoc, "g_out_attn": dgoa, "w_out": dw_out, "g_ffn": dg_ffn, "w_gate": dw_gate, "w_up": dw_up,
        "ffn_conv_w": dfcw, "ffn_conv_b": dfcb, "w_down": dw_down, "g_ple": dg_ple, "w_ple_gate": dw_pg,
        "w_ple_proj": dw_pp,
    }
    return loss, dx, grads


ANY = pl.BlockSpec(memory_space=pl.ANY)
MESH = pl.DeviceIdType.MESH


def _all_gather(shards, name):
    n = len(shards)

    def body(*refs):
        ins, outs = refs[:n], refs[n:2 * n]
        send_sems, recv_sems, local_sems = refs[2 * n:]
        x, y, c = lax.axis_index("x"), lax.axis_index("y"), lax.axis_index("c")
        me, sibling = (x, y, c), (x, y, 1 - c)
        chips = [(1 - x, y), (x, 1 - y), (1 - x, 1 - y)]

        def slot(dev):
            return 4 * dev[0] + 2 * dev[1] + dev[2]

        def copy(b, k, block, to, src=None):
            dst = outs[b].at[slot(block)]
            return pltpu.make_async_remote_copy(
                src_ref=dst if src is None else src, dst_ref=dst, send_sem=send_sems.at[b, k],
                recv_sem=recv_sems.at[b, k], device_id=to, device_id_type=MESH)

        mine = [pltpu.make_async_copy(ins[b], outs[b].at[slot(me)], local_sems.at[b]) for b in range(n)]
        first, passed = [], []
        for b in range(n):
            mine[b].start()
            first.append(copy(b, 0, me, sibling, src=ins[b]))
            first += [copy(b, 1 + j, me, (*chip, c), src=ins[b]) for j, chip in enumerate(chips)]
        for cp in first:
            cp.start()
        for j, chip in enumerate(chips):
            for b in range(n):
                copy(b, 1 + j, (*chip, c), me).wait_recv()
                fwd = copy(b, 4 + j, (*chip, c), sibling)
                fwd.start()
                passed.append(fwd)
        for b in range(n):
            copy(b, 0, sibling, me).wait_recv()
            for j, chip in enumerate(chips):
                copy(b, 4 + j, (*chip, 1 - c), me).wait_recv()
        for cp in first + passed:
            cp.wait_send()
        for cp in mine:
            cp.wait()

    return pl.pallas_call(
        body, name=name,
        in_specs=[ANY] * n, out_specs=[ANY] * n,
        out_shape=[jax.ShapeDtypeStruct((N_DEV,) + s.shape, s.dtype) for s in shards],
        scratch_shapes=[pltpu.SemaphoreType.DMA((n, 7)), pltpu.SemaphoreType.DMA((n, 7)),
                        pltpu.SemaphoreType.DMA((n,))],
    )(*shards)


HBM = pl.BlockSpec(memory_space=pltpu.HBM)
SEM = pl.BlockSpec(memory_space=pltpu.SEMAPHORE)
EFFECT = pltpu.SideEffectType.DATAFLOW_SIDE_EFFECTING
FLIPS = ((0, 0, 1), (0, 1, 0), (0, 1, 1), (1, 0, 0), (1, 0, 1), (1, 1, 0), (1, 1, 1))


def _flip_peers():
    pos = (lax.axis_index("x"), lax.axis_index("y"), lax.axis_index("c"))
    return [tuple(1 - a if f else a for a, f in zip(pos, flip)) for flip in FLIPS]


def _hbm(a):
    return pltpu.with_memory_space_constraint(a, pltpu.HBM)


def _split_start(name, srcs, lands, plan, n_copies, after):
    n, m = len(srcs), len(lands)

    def body(*refs):
        send_sems, recv_sems, token = refs[n + m + 1], refs[n + m + 2], refs[-1]
        for i, (src, dst, peer) in enumerate(plan(refs[:n], refs[n:n + m])):
            pltpu.make_async_remote_copy(src_ref=src, dst_ref=dst, send_sem=send_sems.at[i], recv_sem=recv_sems.at[i],
                                         device_id=peer, device_id_type=MESH).start()
        token[...] = jnp.zeros_like(token)

    outs = pl.pallas_call(
        body, name=name + "_start",
        in_specs=[HBM] * (n + m) + [ANY],
        out_specs=[SEM, SEM] + [HBM] * (n + m) + [pl.BlockSpec(memory_space=pltpu.VMEM)],
        out_shape=[pltpu.SemaphoreType.DMA((n_copies,)), pltpu.SemaphoreType.DMA((n_copies,))]
        + [pltpu.HBM(a.shape, a.dtype) for a in list(srcs) + list(lands)] + [jax.ShapeDtypeStruct((8, 128), F32)],
        input_output_aliases={i: 2 + i for i in range(n + m)},
        compiler_params=pltpu.CompilerParams(has_side_effects=EFFECT),
    )(*[_hbm(a) for a in list(srcs) + list(lands)], after)
    return (outs[0], outs[1], outs[2:2 + n], outs[2 + n:2 + n + m]), outs[-1]


def _split_wait(name, started, plan, after):
    send_sems, recv_sems, srcs, lands = started
    n, m = len(srcs), len(lands)

    def body(*refs):
        send_ref, recv_ref = refs[n + m], refs[n + m + 1]
        for i, (src, dst, peer) in enumerate(plan(refs[:n], refs[n:n + m])):
            copy = pltpu.make_async_remote_copy(src_ref=src, dst_ref=dst, send_sem=send_ref.at[i],
                                                recv_sem=recv_ref.at[i], device_id=peer, device_id_type=MESH)
            copy.wait_send()
            copy.wait_recv()

    outs = pl.pallas_call(
        body, name=name + "_wait",
        in_specs=[HBM] * (n + m) + [SEM, SEM, ANY],
        out_specs=[HBM] * (n + m),
        out_shape=[pltpu.HBM(a.shape, a.dtype) for a in list(srcs) + list(lands)],
        input_output_aliases={i: i for i in range(n + m)},
        compiler_params=pltpu.CompilerParams(has_side_effects=EFFECT),
    )(*srcs, *lands, send_sems, recv_sems, after)
    return outs[:n], outs[n:]


def _gather_plan(srcs, lands):
    slot = 4 * lax.axis_index("x") + 2 * lax.axis_index("y") + lax.axis_index("c")
    return [(src, land.at[slot], peer) for src, land in zip(srcs, lands) for peer in _flip_peers()]


def _sibling_plan(srcs, lands):
    x, y, c = lax.axis_index("x"), lax.axis_index("y"), lax.axis_index("c")
    return [(src.at[k, 1 - c], land.at[k], (x, y, 1 - c)) for src, land in zip(srcs, lands) for k in range(N_CHIP)]


def _chip_plan(srcs, lands):
    x, y, c = lax.axis_index("x"), lax.axis_index("y"), lax.axis_index("c")
    return [(src.at[2 * cx + cy], land.at[2 * x + y], (cx, cy, c))
            for src, land in zip(srcs, lands) for cx, cy in ((1 - x, y), (x, 1 - y), (1 - x, 1 - y))]


def _row_tile(rows):
    for tr in range(min(rows, 512), 15, -16):
        if rows % tr == 0:
            return tr
    return rows


def _pair_sums(gs, lands, core, name):
    n = len(gs)

    def body(c_ref, *refs):
        for b in range(n):
            out = refs[2 * n + b]
            out[...] = (refs[b][...].astype(F32) + refs[n + b][...].astype(F32)).astype(out.dtype)

    def slab(a):
        return pl.BlockSpec((None,) + a.shape[1:], lambda k, c_ref: (k, 0, 0))

    return pl.pallas_call(
        body, name=name,
        grid_spec=pltpu.PrefetchScalarGridSpec(
            num_scalar_prefetch=1, grid=(N_CHIP,),
            in_specs=[pl.BlockSpec((None, None) + g.shape[2:], lambda k, c_ref: (k, c_ref[0], 0, 0)) for g in gs]
            + [slab(a) for a in lands],
            out_specs=[slab(a) for a in lands]),
        out_shape=[jax.ShapeDtypeStruct(a.shape, a.dtype) for a in lands],
        compiler_params=_cparams("parallel"),
    )(core, *gs, *lands)


def _adamw(owns, arriveds, chip, ws, ms, vs, name):
    n = len(ws)
    k, rows, cols = arriveds[0].shape
    tr = _row_tile(rows) if n == 1 else _row_tile(rows // 2)
    c1 = 1.0 / (1.0 - ADAM_B1 ** ADAM_STEP)
    c2 = 1.0 / (1.0 - ADAM_B2 ** ADAM_STEP)

    def body(chip_ref, *refs):
        for b in range(n):
            o_ref, p_ref, w_ref, m_ref, v_ref = refs[b], refs[n + b], refs[2 * n + b], refs[3 * n + b], refs[4 * n + b]
            g_ref, d_ref, nm_ref, nv_ref = refs[5 * n + 4 * b:5 * n + 4 * b + 4]

            def slab(j):
                return jnp.where(chip_ref[0] == j, o_ref[j], p_ref[j]).astype(F32)

            g = slab(0)
            for j in range(1, k):
                g = g + slab(j)
            g_ref[...] = g
            nm = ADAM_B1 * m_ref[...] + (1.0 - ADAM_B1) * g
            nv = ADAM_B2 * v_ref[...] + (1.0 - ADAM_B2) * (g * g)
            nm_ref[...] = nm
            nv_ref[...] = nv
            d_ref[...] = -ADAM_LR * ((nm * c1) / (jnp.sqrt(nv * c2) + ADAM_EPS) + ADAM_WD * w_ref[...])

    blk = pl.BlockSpec((tr, cols), lambda i, c: (i, 0))
    stack = pl.BlockSpec((k, tr, cols), lambda i, c: (0, i, 0))
    outs = pl.pallas_call(
        body, name=name,
        grid_spec=pltpu.PrefetchScalarGridSpec(num_scalar_prefetch=1, grid=(rows // tr,),
                                               in_specs=[stack] * (2 * n) + [blk] * (3 * n), out_specs=[blk] * (4 * n)),
        out_shape=[jax.ShapeDtypeStruct((rows, cols), F32)] * (4 * n),
        compiler_params=_cparams("parallel"),
    )(chip, *owns, *arriveds, *ws, *ms, *vs)
    return [tuple(outs[4 * b:4 * b + 4]) for b in range(n)]


SMALL_LAYOUT = (("g_mix", 0, 1024), ("conv_b", 1, 512), ("q_norm_g", 2, 64), ("k_norm_g", 3, 64),
                ("g_out_conv", 4, 512), ("g_out_attn", 5, 512), ("g_ffn", 6, 1024), ("ffn_conv_b", 7, 2816),
                ("g_ple", 10, 1024))
CONV_W_ROW = 11
FFN_CONV_W_ROW = 14
LOSS_ROW = 23


def _row_pieces(cols):
    return [(c, min(1024, cols - c)) for c in range(0, cols, 1024)]


def _pack_small(grads, loss_tile):
    names = [n for n, _, _ in SMALL_LAYOUT]

    def body(*refs):
        ins, cw_ref, fcw_ref, loss_ref, out_ref = refs[:len(names)], refs[-4], refs[-3], refs[-2], refs[-1]
        out_ref[...] = jnp.zeros_like(out_ref)
        for ref, (_, row, cols) in zip(ins, SMALL_LAYOUT):
            for j, (c, width) in enumerate(_row_pieces(cols)):
                out_ref[row + j:row + j + 1, 0:width] = ref[:, c:c + width]
        for k in range(3):
            out_ref[CONV_W_ROW + k:CONV_W_ROW + k + 1, 0:CONV_W] = cw_ref[k:k + 1, :]
            for j, (c, width) in enumerate(_row_pieces(D_FF)):
                row = FFN_CONV_W_ROW + 3 * k + j
                out_ref[row:row + 1, 0:width] = fcw_ref[k:k + 1, c:c + width]
        out_ref[LOSS_ROW:LOSS_ROW + 1, 0:128] = loss_ref[0:1, :]

    return pl.pallas_call(
        body, name="pack_small_grads", out_shape=jax.ShapeDtypeStruct((SMALL_ROWS, 1024), F32),
    )(*[grads[n] for n in names], grads["conv_w"], grads["ffn_conv_w"], loss_tile)


def _adamw_small(arrived, conv_parts, fconv_parts, wts, mom, var):
    names = [n for n, _, _ in SMALL_LAYOUT] + ["conv_w", "ffn_conv_w"]
    c1 = 1.0 / (1.0 - ADAM_B1 ** ADAM_STEP)
    c2 = 1.0 / (1.0 - ADAM_B2 ** ADAM_STEP)
    n = len(names)

    def body(*refs):
        land, cw_ref, fcw_ref = refs[0], refs[1], refs[2]
        state = refs[3:3 + 3 * n]
        outs = refs[3 + 3 * n:]

        def total(piece):
            acc = piece(0)
            for d in range(1, N_DEV):
                acc = acc + piece(d)
            return acc

        for i, name in enumerate(names):
            if name == "conv_w":
                g = total(lambda d: cw_ref[d])
            elif name == "ffn_conv_w":
                g = total(lambda d: fcw_ref[d])
            else:
                _, row, cols = SMALL_LAYOUT[i]
                pieces = [total(lambda d, j=j, width=width: land[d, row + j:row + j + 1, 0:width])
                          for j, (_, width) in enumerate(_row_pieces(cols))]
                g = pieces[0] if len(pieces) == 1 else jnp.concatenate(pieces, axis=1)
            w_ref, m_ref, v_ref = state[3 * i:3 * i + 3]
            nm = ADAM_B1 * m_ref[...] + (1.0 - ADAM_B1) * g
            nv = ADAM_B2 * v_ref[...] + (1.0 - ADAM_B2) * (g * g)
            outs[4 * i][...] = g
            outs[4 * i + 1][...] = -ADAM_LR * ((nm * c1) / (jnp.sqrt(nv * c2) + ADAM_EPS) + ADAM_WD * w_ref[...])
            outs[4 * i + 2][...] = nm
            outs[4 * i + 3][...] = nv
        outs[-1][...] = total(lambda d: land[d, LOSS_ROW:LOSS_ROW + 1, 0:128])

    state = [a[nm_] for nm_ in names for a in (wts, mom, var)]
    shapes = [jax.ShapeDtypeStruct(wts[nm_].shape, F32) for nm_ in names for _ in range(4)]
    outs = pl.pallas_call(
        body, name="adamw_small", out_shape=shapes + [jax.ShapeDtypeStruct((1, 128), F32)],
    )(arrived, conv_parts, fconv_parts, *state)
    return {nm_: tuple(outs[4 * i:4 * i + 4]) for i, nm_ in enumerate(names)}, outs[-1][0, 0]


COL_SHARDED = ("w_in", "w_ple_proj")
TRANSPOSED = ("w_gate", "w_up")
CONV_SHARDED = (("conv_w", CONV_W), ("ffn_conv_w", D_FF))


def _gathered_to_full(name, gathered):
    if name in COL_SHARDED:
        return gathered.transpose(1, 0, 2).reshape(gathered.shape[1], -1)
    return gathered.reshape(-1, gathered.shape[2])


def _full_to_stacked(name, grad, shard_shape):
    sr, sc = shard_shape
    if grad.ndim == 3:
        a = grad
    elif name in COL_SHARDED:
        a = grad.reshape(sr, N_DEV, sc).transpose(1, 0, 2)
    else:
        a = grad.reshape(N_DEV, sr, sc)
    return a.astype(BF16).reshape(N_CHIP, 2, sr, sc)


def _pad_rows(vec, rows):
    return jnp.pad(vec, (0, rows * 1024 - vec.shape[0])).reshape(rows, 1024)


def kernel(x, p, g_mix, w_in, conv_w, conv_b, q_norm_g, k_norm_g, g_out_conv, g_out_attn, w_out, g_ffn, w_gate, w_up, ffn_conv_w, ffn_conv_b, w_down, g_ple, w_ple_gate, w_ple_proj, loss_target, m_g_mix, m_w_in, m_conv_w, m_conv_b, m_q_norm_g, m_k_norm_g, m_g_out_conv, m_g_out_attn, m_w_out, m_g_ffn, m_w_gate, m_w_up, m_ffn_conv_w, m_ffn_conv_b, m_w_down, m_g_ple, m_w_ple_gate, m_w_ple_proj, v_g_mix, v_w_in, v_conv_w, v_conv_b, v_q_norm_g, v_k_norm_g, v_g_out_conv, v_g_out_attn, v_w_out, v_g_ffn, v_w_gate, v_w_up, v_ffn_conv_w, v_ffn_conv_b, v_w_down, v_g_ple, v_w_ple_gate, v_w_ple_proj):
    args = dict(locals())
    names = ["g_mix", "w_in", "conv_w", "conv_b", "q_norm_g", "k_norm_g", "g_out_conv", "g_out_attn", "w_out", "g_ffn",
             "w_gate", "w_up", "ffn_conv_w", "ffn_conv_b", "w_down", "g_ple", "w_ple_gate", "w_ple_proj"]
    big = list(BIG)
    conv = [n for n, _ in CONV_SHARDED]

    def local(prefix):
        out = {n: (args[prefix + n][0] if n in big or n in conv else args[prefix + n]) for n in names}
        out.update({n: out[n].T for n in TRANSPOSED})
        return out

    wts, mom, var = local(""), local("m_"), local("v_")
    shard_shapes = {n: wts[n].shape for n in big}
    dev = 4 * lax.axis_index("x") + 2 * lax.axis_index("y") + lax.axis_index("c")
    core = lax.axis_index("c").astype(jnp.int32).reshape(1)

    conv_local = _pad_rows(jnp.concatenate([wts[n].reshape(-1) for n in conv]), 8).reshape(8, 1024)
    late = [n for n in big if n != "w_in"]
    w_in_all, conv_all = _all_gather([wts["w_in"].astype(BF16), conv_local], "gather_weights")
    late_shards = [wts[n].astype(BF16) for n in late]
    gathering, token = _split_start("gather_late_weights", late_shards,
                                    [lax.empty((N_DEV,) + s.shape, BF16) for s in late_shards], _gather_plan,
                                    7 * len(late), w_in_all)
    full = dict(wts)
    full["w_in"] = _gathered_to_full("w_in", w_in_all)
    full["g_mix"] = _ordered_after(wts["g_mix"], token)
    flying = {}

    def late_weights(after):
        shards, lands = _split_wait("gather_late_weights", gathering, _gather_plan, after)
        return {n: _gathered_to_full(n, lax.dynamic_update_slice(land, shard[None], (dev, 0, 0)))
                for n, land, shard in zip(late, lands, shards)}

    early = ["w_ple_gate", "w_ple_proj", "w_down", "w_up", "w_gate"]

    def ffn_grads(g):
        stacked = [_full_to_stacked(n, g[n], shard_shapes[n]) for n in early]
        flying["sibling"], tok = _split_start("rs_sibling_early", stacked,
                                              [lax.empty((N_CHIP,) + s.shape[2:], BF16) for s in stacked],
                                              _sibling_plan, N_CHIP * len(early), g["w_down"])
        return tok

    def outproj_done(after):
        stacked, landed = _split_wait("rs_sibling_early", flying["sibling"], _sibling_plan, after)
        parts = _pair_sums(stacked, landed, core, "rs_pair_sums_early")
        flying["chip"], tok = _split_start("rs_chip_early", parts, [lax.empty(q.shape, BF16) for q in parts],
                                           _chip_plan, 3 * len(early), landed[0])
        return tok

    off = 0
    for n, width in CONV_SHARDED:
        sc = width // N_DEV
        a = conv_all.reshape(N_DEV, -1)[:, off:off + 3 * sc].reshape(N_DEV, 3, sc)
        full[n] = a.transpose(1, 0, 2).reshape(3, width)
        off += 3 * sc

    loss, dx, grads = _local_step(x[0], p[0, 0], loss_target[0], full, (512, 256),
                                  {"late_weights": late_weights, "ffn_grads": ffn_grads, "outproj_done": outproj_done})

    chip = (2 * lax.axis_index("x") + lax.axis_index("y")).astype(jnp.int32).reshape(1)

    def adamw_of(group, parts, arrived):
        out, by_shape = {}, {}
        for n, own, got in zip(group, parts, arrived):
            by_shape.setdefault(got.shape, []).append((n, own, got))
        for members in by_shape.values():
            ns = [n for n, _, _ in members]
            results = _adamw([own for _, own, _ in members], [got for _, _, got in members], chip,
                             [wts[n] for n in ns], [mom[n] for n in ns], [var[n] for n in ns], "adamw_" + "_".join(ns))
            out.update(zip(ns, results))
        return out

    last = [n for n in big if n not in early]
    stacked = [_full_to_stacked(n, grads[n], shard_shapes[n]) for n in last]
    flying["sibling_last"], tok = _split_start("rs_sibling_last", stacked,
                                               [lax.empty((N_CHIP,) + s.shape[2:], BF16) for s in stacked],
                                               _sibling_plan, N_CHIP * len(last), dx)
    (small_all,) = _all_gather([_ordered_after(_pack_small(grads, loss), tok)], "gather_small_grads")
    stacked, landed = _split_wait("rs_sibling_last", flying["sibling_last"], _sibling_plan, small_all)
    parts = _pair_sums(stacked, landed, core, "rs_pair_sums_last")
    flying["chip_last"], tok = _split_start("rs_chip_last", parts, [lax.empty(q.shape, BF16) for q in parts],
                                            _chip_plan, 3 * len(last), landed[0])

    parts, arrived = _split_wait("rs_chip_early", flying["chip"], _chip_plan, tok)
    out = adamw_of(early, parts, arrived)
    small_all = _ordered_after(small_all, tok)
    taps = small_all[:, CONV_W_ROW:CONV_W_ROW + 3, 0:CONV_W]
    ftaps = small_all[:, FFN_CONV_W_ROW:FFN_CONV_W_ROW + 9, :].reshape(N_DEV, 3, 3 * 1024)
    small_out, loss_total = _adamw_small(
        small_all, lax.dynamic_slice(taps, (0, 0, dev * (CONV_W // N_DEV)), (N_DEV, 3, CONV_W // N_DEV)),
        lax.dynamic_slice(ftaps, (0, 0, dev * (D_FF // N_DEV)), (N_DEV, 3, D_FF // N_DEV)), wts, mom, var)
    out.update(small_out)
    parts, arrived = _split_wait("rs_chip_last", flying["chip_last"], _chip_plan, small_out["g_mix"][0])
    out.update(adamw_of(last, parts, arrived))
    def result(n, which):
        a = out[n][which]
        return (a.T if n in TRANSPOSED else a).reshape(args[n].shape)

    return (loss_total, dx[None], *[result(n, which) for which in range(4) for n in names])
```

```python
import jax
import jax.numpy as jnp
from jax import lax
from jax.experimental import pallas as pl
from jax.experimental.pallas import tpu as pltpu

F32 = jnp.float32
BF16 = jnp.bfloat16

D_MODEL = 1024
CONV_W = 512
ATTN_W = 512
HEAD_DIM = 64
D_FF = 2816
PLE_DIM = 256
IN_COLS = 3 * CONV_W + 3 * ATTN_W
EPS = 1e-6
QK_BLOCK = 128
DILATIONS = (1, 4, 16)
ATTN_SCALE = HEAD_DIM ** -0.5

ADAM_LR = 0.001
ADAM_B1 = 0.9
ADAM_B2 = 0.999
ADAM_EPS = 1e-08
ADAM_WD = 0.01
ADAM_STEP = 10

N_DEV = 8
N_CHIP = 4
V7X_VMEM_LIMIT = 56 * 1024 * 1024
V7X_VMEM_LIMIT_LARGE = 62 * 1024 * 1024
FF_CHUNKS = 2
FFN_BWD_PARTS = 1

BIG = ("w_in", "w_out", "w_gate", "w_up", "w_down", "w_ple_gate", "w_ple_proj")
SMALL_ROWS = 24


def _cparams(*sem, vmem=V7X_VMEM_LIMIT):
    return pltpu.CompilerParams(dimension_semantics=sem, vmem_limit_bytes=vmem)


def _mm(a, b):
    return jnp.dot(a, b, preferred_element_type=F32)


def _mm_nt(a, b):
    return lax.dot_general(a, b, (((1,), (1,)), ((), ())), preferred_element_type=F32)


def _mm_tn(a, b):
    return lax.dot_general(a, b, (((0,), (0,)), ((), ())), preferred_element_type=F32)


def _full(shape):
    nd = len(shape)
    return pl.BlockSpec(shape, lambda *_: (0,) * nd)


def _rms_stats(x):
    r = lax.rsqrt(jnp.mean(x * x, axis=-1, keepdims=True) + EPS)
    return r, x * r


def _rms_bwd(dy, xhat, r, g):
    gd = dy * g
    return r * (gd - xhat * jnp.mean(gd * xhat, axis=-1, keepdims=True))


def _seg_sum64(v, bd_ref):
    outs = []
    for c in range(0, v.shape[1], 256):
        vc = v[:, c:c + 256]
        hi = vc.astype(BF16)
        lo = (vc - hi.astype(F32)).astype(BF16)
        outs.append(_mm(hi, bd_ref[...]) + _mm(lo, bd_ref[...]))
    return outs[0] if len(outs) == 1 else jnp.concatenate(outs, axis=1)


def _shift_rows(u, k, edge_rows):
    out = pltpu.roll(u, k, 0)
    row = lax.broadcasted_iota(jnp.int32, (8, u.shape[1]), 0)
    head = out[0:8]
    for j in range(k):
        head = jnp.where(row == j, edge_rows[k - 1 - j], head)
    return jnp.concatenate([head, out[8:]], axis=0)


def _shift_rows_up(u, k, edge_rows):
    n = u.shape[0]
    out = pltpu.roll(u, n - k, 0)
    row = lax.broadcasted_iota(jnp.int32, (8, u.shape[1]), 0)
    tail = out[n - 8:n]
    for j in range(k):
        tail = jnp.where(row == 8 - k + j, edge_rows[j], tail)
    return jnp.concatenate([out[0:n - 8], tail], axis=0)


def _conv_fwd(u, c1, c2, w_ref, b_ref):
    u1 = _shift_rows(u, 1, (c1,))
    u2 = _shift_rows(u, 2, (c1, c2))
    y = u2 * w_ref[0:1, :] + u1 * w_ref[1:2, :] + u * w_ref[2:3, :] + b_ref[...]
    return y, u1, u2


def _conv_bwd_input(dy, n1row, n2row, w_ref):
    d1 = _shift_rows_up(dy, 1, (n1row,))
    d2 = _shift_rows_up(dy, 2, (n1row, n2row))
    return dy * w_ref[2:3, :] + d1 * w_ref[1:2, :] + d2 * w_ref[0:1, :]


def _sigmoid(x):
    return 1.0 / (1.0 + jnp.exp(-x))


def _inproj_fwd(x, g_mix, w_in, conv_w, conv_b, qg, kg, bd, tm):
    t = x.shape[0]

    def body(x_ref, g_ref, w_ref, cw_ref, cb_ref, qg_ref, kg_ref, bd_ref,
             zc_ref, zqk_ref, yc_ref, q_ref, k_ref, v_ref, carry_ref):
        @pl.when(pl.program_id(0) == 0)
        def _():
            carry_ref[...] = jnp.zeros_like(carry_ref)

        _, xhat = _rms_stats(x_ref[...])
        h = (xhat * g_ref[...]).astype(BF16)
        zconv = _mm(h, w_ref[:, 0:3 * CONV_W])
        zc_ref[...] = zconv.astype(BF16)
        u = zconv[:, CONV_W:2 * CONV_W] * zconv[:, 2 * CONV_W:3 * CONV_W]
        cv, _, _ = _conv_fwd(u, carry_ref[7:8, :], carry_ref[6:7, :], cw_ref, cb_ref)
        yc_ref[...] = (zconv[:, 0:CONV_W] * cv).astype(BF16)
        carry_ref[...] = u[tm - 8:tm, :]

        zqk = _mm(h, w_ref[:, 3 * CONV_W:3 * CONV_W + 2 * ATTN_W])
        zqk_ref[...] = zqk.astype(BF16)
        for j, (gain_ref, out_ref, scale) in enumerate(((qg_ref, q_ref, ATTN_SCALE), (kg_ref, k_ref, 1.0))):
            z = zqk[:, j * ATTN_W:(j + 1) * ATTN_W]
            r = lax.rsqrt(_seg_sum64(z * z, bd_ref) * (1.0 / HEAD_DIM) + EPS)
            out_ref[...] = z * r * gain_ref[...] * scale
        v_ref[...] = _mm(h, w_ref[:, 3 * CONV_W + 2 * ATTN_W:IN_COLS])

    def blk(c):
        return pl.BlockSpec((tm, c), lambda i: (i, 0))

    return pl.pallas_call(
        body, name="inproj_fwd", grid=(t // tm,),
        in_specs=[blk(D_MODEL), _full((1, D_MODEL)), _full((D_MODEL, IN_COLS)), _full((3, CONV_W)),
                  _full((1, CONV_W)), _full((1, ATTN_W)), _full((1, ATTN_W)), _full((256, 256))],
        out_specs=[blk(3 * CONV_W), blk(2 * ATTN_W), blk(CONV_W), blk(ATTN_W), blk(ATTN_W), blk(ATTN_W)],
        out_shape=[jax.ShapeDtypeStruct((t, 3 * CONV_W), BF16), jax.ShapeDtypeStruct((t, 2 * ATTN_W), BF16),
                   jax.ShapeDtypeStruct((t, CONV_W), BF16), jax.ShapeDtypeStruct((t, ATTN_W), F32),
                   jax.ShapeDtypeStruct((t, ATTN_W), F32), jax.ShapeDtypeStruct((t, ATTN_W), F32)],
        scratch_shapes=[pltpu.VMEM((8, CONV_W), F32)],
        compiler_params=_cparams("arbitrary"),
    )(x, g_mix, w_in, conv_w, conv_b, qg, kg, bd)


SUPER = 16 * QK_BLOCK
KEYS = 2 * QK_BLOCK
UNITS = SUPER // QK_BLOCK


def _rows(start, size, dil):
    return pl.ds(start, size) if dil == 1 else pl.ds(start, size, stride=dil)


def _attn_bias(sl_ref, dil):
    qi = lax.broadcasted_iota(jnp.int32, (KEYS, KEYS), 0)
    kj = lax.broadcasted_iota(jnp.int32, (KEYS, KEYS), 1)
    step = jnp.bitwise_and(qi, QK_BLOCK - 1) + QK_BLOCK - kj
    slope = jnp.where(qi < QK_BLOCK, sl_ref[0, 0:1, 0:1], sl_ref[0, 1:2, 0:1])
    bias = jnp.where(jnp.logical_and(step >= 0, step <= QK_BLOCK), -slope * (step * dil).astype(F32), -jnp.inf)
    return bias, kj >= QK_BLOCK


def _unit_start(u, dil):
    if dil == 1:
        return pl.multiple_of(u * QK_BLOCK, QK_BLOCK)
    if dil == 4:
        return jnp.bitwise_and(u, 3) + (u // 4) * (4 * QK_BLOCK)
    return u


def _stack_heads(a, head0):
    zero = jnp.zeros_like(a)
    return jnp.concatenate([jnp.where(head0, a, zero), jnp.where(head0, zero, a)], axis=0)


def _attn_fwd(q, k, v, slopes):
    t = q.shape[0]
    nsb = t // SUPER

    def body(q_ref, kc_ref, kp_ref, vc_ref, vp_ref, sl_ref, o_ref, l_ref, e_ref, m_ref, kk, vv, ob, lb):
        s = pl.program_id(1)
        kk[0:SUPER, :] = kp_ref[...]
        kk[SUPER:, :] = kc_ref[...]
        vv[0:SUPER, :] = vp_ref[...]
        vv[SUPER:, :] = vc_ref[...]
        head0 = lax.broadcasted_iota(jnp.int32, (QK_BLOCK, QK_BLOCK), 1) < HEAD_DIM

        for b, dil in enumerate(DILATIONS):
            bias, own_half = _attn_bias(sl_ref, dil)

            def unit(u, carry, b=b, dil=dil, bias=bias, own_half=own_half):
                start = _unit_start(u, dil)
                first_key = SUPER + start - QK_BLOCK * dil
                q2 = _stack_heads(q_ref[_rows(start, QK_BLOCK, dil), :].astype(BF16), head0)
                k2 = kk[_rows(first_key, KEYS, dil), :].astype(BF16)
                v2 = vv[_rows(first_key, KEYS, dil), :].astype(BF16)
                has_prev = jnp.logical_or(s > 0, start >= QK_BLOCK * dil)
                sc = jnp.where(jnp.logical_or(own_half, has_prev), _mm_nt(q2, k2) + bias, -jnp.inf)
                m = jnp.max(sc, axis=-1, keepdims=True)
                e = jnp.exp(sc - m)
                den = jnp.sum(e, axis=-1, keepdims=True)
                eb = e.astype(BF16)
                e_ref[b * UNITS + u] = eb
                o2 = _mm(eb, v2) / den
                l2 = m + jnp.log(den)
                ob[b, _rows(start, QK_BLOCK, dil), :] = jnp.where(head0, o2[0:QK_BLOCK], o2[QK_BLOCK:])
                lb[b, _rows(start, QK_BLOCK, dil), :] = jnp.where(head0, l2[0:QK_BLOCK], l2[QK_BLOCK:])
                m_ref[b, _rows(start, QK_BLOCK, dil), :] = jnp.where(head0, m[0:QK_BLOCK], m[QK_BLOCK:])
                return carry

            lax.fori_loop(0, UNITS, unit, 0, unroll=16)

        def merge(i, carry):
            rows = pl.ds(pl.multiple_of(i * 256, 256), 256)
            la, lb_, lc = lb[0, rows, :], lb[1, rows, :], lb[2, rows, :]
            mx = jnp.maximum(jnp.maximum(la, lb_), lc)
            wa, wb, wc = jnp.exp(la - mx), jnp.exp(lb_ - mx), jnp.exp(lc - mx)
            sw = wa + wb + wc
            o_ref[rows, :] = ((wa * ob[0, rows, :] + wb * ob[1, rows, :] + wc * ob[2, rows, :]) / sw).astype(BF16)
            l_ref[rows, :] = mx + jnp.log(sw)
            return carry

        lax.fori_loop(0, SUPER // 256, merge, 0)

    cur = pl.BlockSpec((SUPER, QK_BLOCK), lambda p, s: (s, p))
    prev = pl.BlockSpec((SUPER, QK_BLOCK), lambda p, s: (jnp.maximum(s - 1, 0), p))
    return pl.pallas_call(
        body, name="attn_fwd", grid=(4, nsb),
        in_specs=[cur, cur, prev, cur, prev, pl.BlockSpec((1, 2, QK_BLOCK), lambda p, s: (p, 0, 0))],
        out_specs=[cur, cur, pl.BlockSpec((None, None, 3 * UNITS, KEYS, KEYS), lambda p, s: (p, s, 0, 0, 0)),
                   pl.BlockSpec((3, SUPER, QK_BLOCK), lambda p, s: (0, s, p))],
        out_shape=[jax.ShapeDtypeStruct((t, ATTN_W), BF16), jax.ShapeDtypeStruct((t, ATTN_W), F32),
                   jax.ShapeDtypeStruct((4, nsb, 3 * UNITS, KEYS, KEYS), BF16),
                   jax.ShapeDtypeStruct((3, t, ATTN_W), F32)],
        scratch_shapes=[pltpu.VMEM((2 * SUPER, QK_BLOCK), F32), pltpu.VMEM((2 * SUPER, QK_BLOCK), F32),
                        pltpu.VMEM((3, SUPER, QK_BLOCK), F32), pltpu.VMEM((3, SUPER, QK_BLOCK), F32)],
        compiler_params=_cparams("parallel", "arbitrary"),
    )(q, k, k, v, v, slopes)


def _outproj_fwd(ya, yc, x, goc, goa, w_out, tm):
    t = x.shape[0]

    def body(ya_ref, yc_ref, x_ref, goc_ref, goa_ref, w_ref, x1_ref):
        _, ychat = _rms_stats(yc_ref[...].astype(F32))
        _, yahat = _rms_stats(ya_ref[...].astype(F32))
        acc = _mm((ychat * goc_ref[...]).astype(BF16), w_ref[0:CONV_W, :])
        acc += _mm((yahat * goa_ref[...]).astype(BF16), w_ref[CONV_W:, :])
        x1_ref[...] = x_ref[...] + acc

    def blk(c):
        return pl.BlockSpec((tm, c), lambda i: (i, 0))

    return pl.pallas_call(
        body, name="outproj_fwd", grid=(t // tm,),
        in_specs=[blk(ATTN_W), blk(CONV_W), blk(D_MODEL), _full((1, CONV_W)), _full((1, ATTN_W)),
                  _full((D_MODEL, D_MODEL))],
        out_specs=blk(D_MODEL),
        out_shape=jax.ShapeDtypeStruct((t, D_MODEL), F32),
        compiler_params=_cparams("parallel"),
    )(ya, yc, x, goc, goa, w_out)


def _ffn_fwd(x1, g_ffn, w_gate_t, w_up_t, w_down, fcw, fcb, tm):
    t = x1.shape[0]

    def body(x_ref, g_ref, wg_ref, wu_ref, wd_ref, cw_ref, cb_ref, gp_ref, up_ref, h_ref, x2_ref, carry_ref):
        @pl.when(pl.program_id(0) == 0)
        def _():
            carry_ref[...] = jnp.zeros_like(carry_ref)

        xv = x_ref[...]
        _, xhat = _rms_stats(xv)
        h = (xhat * g_ref[...]).astype(BF16)
        h_ref[...] = h
        gp = _mm_nt(h, wg_ref[...])
        gp_ref[...] = gp.astype(BF16)
        gate, _, _ = _conv_fwd(gp, carry_ref[7:8, :], carry_ref[6:7, :], cw_ref, cb_ref)
        carry_ref[...] = gp[tm - 8:tm, :]
        up = _mm_nt(h, wu_ref[...])
        up_ref[...] = up.astype(BF16)
        a = (gate * _sigmoid(gate) * up).astype(BF16)
        x2_ref[...] = xv + _mm(a, wd_ref[...])

    def blk(c):
        return pl.BlockSpec((tm, c), lambda i: (i, 0))

    return pl.pallas_call(
        body, name="ffn_fwd", grid=(t // tm,),
        in_specs=[blk(D_MODEL), _full((1, D_MODEL)), _full((D_FF, D_MODEL)), _full((D_FF, D_MODEL)),
                  _full((D_FF, D_MODEL)), _full((3, D_FF)), _full((1, D_FF))],
        out_specs=[blk(D_FF), blk(D_FF), blk(D_MODEL), blk(D_MODEL)],
        out_shape=[jax.ShapeDtypeStruct((t, D_FF), BF16), jax.ShapeDtypeStruct((t, D_FF), BF16),
                   jax.ShapeDtypeStruct((t, D_MODEL), BF16), jax.ShapeDtypeStruct((t, D_MODEL), F32)],
        scratch_shapes=[pltpu.VMEM((8, D_FF), F32)],
        compiler_params=_cparams("arbitrary"),
    )(x1, g_ffn, w_gate_t, w_up_t, w_down, fcw, fcb)


def _ple_fwd_bwd(x2, p, target, g_ple, w_pg, w_pp, tm):
    t = x2.shape[0]

    def body(x_ref, p_ref, t_ref, g_ref, wg_ref, wp_ref, dx_ref, dxb_ref, loss_ref, dwg_ref, dwp_ref, dg_ref):
        @pl.when(pl.program_id(0) == 0)
        def _():
            loss_ref[...] = jnp.zeros_like(loss_ref)
            dwg_ref[...] = jnp.zeros_like(dwg_ref)
            dwp_ref[...] = jnp.zeros_like(dwp_ref)
            dg_ref[...] = jnp.zeros_like(dg_ref)

        xv = x_ref[...]
        r, xhat = _rms_stats(xv)
        g = g_ref[...]
        h = (xhat * g).astype(BF16)
        pg = _sigmoid(_mm(h, wg_ref[...]))
        pb = p_ref[...].astype(BF16)
        pp = _mm(pb, wp_ref[...])
        err = xv + pg * pp - t_ref[...]
        loss_ref[...] += 0.5 * jnp.sum(jnp.mean(err * err, axis=-1, keepdims=True))
        dx3 = err * (1.0 / D_MODEL)
        d_pp = (dx3 * pg).astype(BF16)
        d_pre = (dx3 * pp * pg * (1.0 - pg)).astype(BF16)
        dwp_ref[...] += _mm_tn(pb, d_pp)
        dwg_ref[...] += _mm_tn(h, d_pre)
        dh = _mm_nt(d_pre, wg_ref[...])
        dg_ref[...] += jnp.sum(dh * xhat, axis=0, keepdims=True)
        dx2 = dx3 + _rms_bwd(dh, xhat, r, g)
        dx_ref[...] = dx2
        dxb_ref[...] = dx2.astype(BF16)

    def blk(c):
        return pl.BlockSpec((tm, c), lambda i: (i, 0))

    return pl.pallas_call(
        body, name="ple_fwd_bwd", grid=(t // tm,),
        in_specs=[blk(D_MODEL), blk(PLE_DIM), blk(D_MODEL), _full((1, D_MODEL)), _full((D_MODEL, D_MODEL)),
                  _full((PLE_DIM, D_MODEL))],
        out_specs=[blk(D_MODEL), blk(D_MODEL), _full((8, 128)), _full((D_MODEL, D_MODEL)),
                   _full((PLE_DIM, D_MODEL)), _full((1, D_MODEL))],
        out_shape=[jax.ShapeDtypeStruct((t, D_MODEL), F32), jax.ShapeDtypeStruct((t, D_MODEL), BF16),
                   jax.ShapeDtypeStruct((8, 128), F32),
                   jax.ShapeDtypeStruct((D_MODEL, D_MODEL), F32), jax.ShapeDtypeStruct((PLE_DIM, D_MODEL), F32),
                   jax.ShapeDtypeStruct((1, D_MODEL), F32)],
        compiler_params=_cparams("arbitrary"),
    )(x2, p, target, g_ple, w_pg, w_pp)


def _ffn_bwd(dx2, h2, gp, up, w_gate, w_up, w_down, fcw, fcb, tm):
    t = dx2.shape[0]
    nblk = t // tm
    fc = D_FF // FF_CHUNKS
    half = tm // FFN_BWD_PARTS

    def body(dx_ref, h_ref, gp_ref, gph_ref, up_ref, wg_ref, wu_ref, wd_ref, cw_ref, cb_ref,
             dh_ref, dwd_hbm, dwu_hbm, dwg_hbm, dcw_ref, dcb_ref, carry_ref, a_scr, dup_scr, dgp_scr,
             dwd_acc, dwu_acc, dwg_acc, stage, stage_sem):
        i = pl.program_id(1)

        @pl.when(i == 0)
        def _():
            carry_ref[...] = jnp.zeros_like(carry_ref)
            dwd_acc[...] = jnp.zeros_like(dwd_acc)
            dwu_acc[...] = jnp.zeros_like(dwu_acc)
            dwg_acc[...] = jnp.zeros_like(dwg_acc)
            dcw_ref[...] = jnp.zeros_like(dcw_ref)
            dcb_ref[...] = jnp.zeros_like(dcb_ref)

        keep = (i < nblk - 1).astype(F32)
        later = carry_ref[...]
        for hf in reversed(range(FFN_BWD_PARTS)):
            rows = slice(hf * half, (hf + 1) * half)
            dxb = dx_ref[rows, :]
            gp_v = gp_ref[rows, :].astype(F32)
            if hf > 0:
                before = gp_ref[hf * half - 16:hf * half, :].astype(F32)
            else:
                before = gph_ref[...].astype(F32) * keep
            gate, gp1, gp2 = _conv_fwd(gp_v, before[15:16, :], before[14:15, :], cw_ref, cb_ref)
            s = _sigmoid(gate)
            silu = gate * s
            up_v = up_ref[rows, :].astype(F32)
            da = _mm_nt(dxb, wd_ref[...])
            a_scr[rows, :] = (silu * up_v).astype(BF16)
            d_up = (da * silu).astype(BF16)
            dup_scr[rows, :] = d_up
            d_gate = da * up_v * (s * (1.0 + gate * (1.0 - s)))
            d_gp = _conv_bwd_input(d_gate, later[0:1, :], later[1:2, :], cw_ref).astype(BF16)
            dgp_scr[rows, :] = d_gp
            later = d_gate[0:8, :]
            dcw_ref[0:1, :] += jnp.sum(d_gate * gp2, axis=0, keepdims=True)
            dcw_ref[1:2, :] += jnp.sum(d_gate * gp1, axis=0, keepdims=True)
            dcw_ref[2:3, :] += jnp.sum(d_gate * gp_v, axis=0, keepdims=True)
            dcb_ref[...] += jnp.sum(d_gate, axis=0, keepdims=True)
            dh_ref[rows, :] = (_mm(d_gp, wg_ref[...]) + _mm(d_up, wu_ref[...])).astype(BF16)
        carry_ref[...] = later
        dwd_acc[...] += _mm_tn(a_scr[...], dx_ref[...])
        dwu_acc[...] += _mm_tn(h_ref[...], dup_scr[...])
        dwg_acc[...] += _mm_tn(h_ref[...], dgp_scr[...])

        @pl.when(i == nblk - 1)
        def _():
            rows = pl.ds(pl.multiple_of(pl.program_id(0) * fc, 16), fc)
            for acc, out, flip in ((dwd_acc, dwd_hbm, False), (dwu_acc, dwu_hbm, True), (dwg_acc, dwg_hbm, True)):
                stage[...] = (acc[...].T if flip else acc[...]).astype(BF16)
                copy = pltpu.make_async_copy(stage, out.at[rows, :], stage_sem)
                copy.start()
                copy.wait()

    def rev(i):
        return nblk - 1 - i

    one = pl.Buffered(1)
    in_specs = [
        pl.BlockSpec((tm, D_MODEL), lambda j, i: (rev(i), 0)),
        pl.BlockSpec((tm, D_MODEL), lambda j, i: (rev(i), 0)),
        pl.BlockSpec((tm, fc), lambda j, i: (rev(i), j)),
        pl.BlockSpec((16, fc), lambda j, i: (jnp.maximum(rev(i) * (tm // 16) - 1, 0), j)),
        pl.BlockSpec((tm, fc), lambda j, i: (rev(i), j)),
        pl.BlockSpec((fc, D_MODEL), lambda j, i: (j, 0), pipeline_mode=one),
        pl.BlockSpec((fc, D_MODEL), lambda j, i: (j, 0), pipeline_mode=one),
        pl.BlockSpec((fc, D_MODEL), lambda j, i: (j, 0), pipeline_mode=one),
        pl.BlockSpec((3, fc), lambda j, i: (0, j)),
        pl.BlockSpec((1, fc), lambda j, i: (0, j)),
    ]
    out_specs = [
        pl.BlockSpec((None, tm, D_MODEL), lambda j, i: (j, rev(i), 0)),
        ANY, ANY, ANY,
        pl.BlockSpec((3, fc), lambda j, i: (0, j)),
        pl.BlockSpec((1, fc), lambda j, i: (0, j)),
    ]
    return pl.pallas_call(
        body, name="ffn_bwd", grid=(FF_CHUNKS, nblk), in_specs=in_specs, out_specs=out_specs,
        out_shape=[jax.ShapeDtypeStruct((FF_CHUNKS, t, D_MODEL), BF16), jax.ShapeDtypeStruct((D_FF, D_MODEL), BF16),
                   jax.ShapeDtypeStruct((D_FF, D_MODEL), BF16), jax.ShapeDtypeStruct((D_FF, D_MODEL), BF16),
                   jax.ShapeDtypeStruct((3, D_FF), F32), jax.ShapeDtypeStruct((1, D_FF), F32)],
        scratch_shapes=[pltpu.VMEM((8, fc), F32), pltpu.VMEM((tm, fc), BF16), pltpu.VMEM((tm, fc), BF16),
                        pltpu.VMEM((tm, fc), BF16), pltpu.VMEM((fc, D_MODEL), F32), pltpu.VMEM((D_MODEL, fc), F32),
                        pltpu.VMEM((D_MODEL, fc), F32), pltpu.VMEM((fc, D_MODEL), BF16), pltpu.SemaphoreType.DMA],
        compiler_params=_cparams("arbitrary", "arbitrary", vmem=V7X_VMEM_LIMIT_LARGE),
    )(dx2, h2, gp, gp, up, w_gate, w_up, w_down, fcw, fcb)


def _outproj_bwd(dh2, dx2, x1, g_ffn, w_out, yc, ya, goc, goa, zconv, conv_w, conv_b, bd, tm):
    t = x1.shape[0]
    nblk = t // tm

    def body(dh_ref, dx2_ref, x1_ref, g_ref, w_ref, yc_ref, ya_ref, goc_ref, goa_ref, zc_ref, zch_ref, cw_ref, cb_ref,
             bd_ref, dx1_ref, dya_ref, dd_ref, dzc_ref, dw_ref, dg_ref, dgoc_ref, dgoa_ref, dcw_ref, dcb_ref,
             carry_ref):
        i = pl.program_id(0)

        @pl.when(i == 0)
        def _():
            carry_ref[...] = jnp.zeros_like(carry_ref)
            for ref in (dw_ref, dg_ref, dgoc_ref, dgoa_ref, dcw_ref, dcb_ref):
                ref[...] = jnp.zeros_like(ref)

        keep = (i < nblk - 1).astype(F32)
        dh2_v = dh_ref[0].astype(F32)
        for j in range(1, FF_CHUNKS):
            dh2_v = dh2_v + dh_ref[j].astype(F32)
        r, xhat = _rms_stats(x1_ref[...])
        dg_ref[...] += jnp.sum(dh2_v * xhat, axis=0, keepdims=True)
        dx1 = dx2_ref[...] + _rms_bwd(dh2_v, xhat, r, g_ref[...])
        dx1_ref[...] = dx1
        dx1b = dx1.astype(BF16)
        dy = _mm_nt(dx1b, w_ref[...])

        yc_v = yc_ref[...].astype(F32)
        rc, ychat = _rms_stats(yc_v)
        dw_ref[0:CONV_W, :] += _mm_tn((ychat * goc_ref[...]).astype(BF16), dx1b)
        dyc = dy[:, 0:CONV_W]
        dgoc_ref[...] += jnp.sum(dyc * ychat, axis=0, keepdims=True)
        d_yc = _rms_bwd(dyc, ychat, rc, goc_ref[...])

        ya_v = ya_ref[...].astype(F32)
        ra, yahat = _rms_stats(ya_v)
        dw_ref[CONV_W:, :] += _mm_tn((yahat * goa_ref[...]).astype(BF16), dx1b)
        dya = dy[:, CONV_W:]
        dgoa_ref[...] += jnp.sum(dya * yahat, axis=0, keepdims=True)
        d_ya = _rms_bwd(dya, yahat, ra, goa_ref[...])
        dya_ref[...] = d_ya
        dd_ref[...] = _seg_sum64(d_ya * ya_v, bd_ref)

        zb = zc_ref[:, 0:CONV_W].astype(F32)
        zc = zc_ref[:, CONV_W:2 * CONV_W].astype(F32)
        zx = zc_ref[:, 2 * CONV_W:3 * CONV_W].astype(F32)
        u = zc * zx
        uh = (zch_ref[:, CONV_W:2 * CONV_W].astype(F32) * zch_ref[:, 2 * CONV_W:3 * CONV_W].astype(F32)) * keep
        cv, u1, u2 = _conv_fwd(u, uh[15:16, :], uh[14:15, :], cw_ref, cb_ref)
        d_cv = d_yc * zb
        d_u = _conv_bwd_input(d_cv, carry_ref[0:1, :], carry_ref[1:2, :], cw_ref)
        carry_ref[...] = d_cv[0:8, :]
        dcw_ref[0:1, :] += jnp.sum(d_cv * u2, axis=0, keepdims=True)
        dcw_ref[1:2, :] += jnp.sum(d_cv * u1, axis=0, keepdims=True)
        dcw_ref[2:3, :] += jnp.sum(d_cv * u, axis=0, keepdims=True)
        dcb_ref[...] += jnp.sum(d_cv, axis=0, keepdims=True)
        dzc_ref[:, 0:CONV_W] = (d_yc * cv).astype(BF16)
        dzc_ref[:, CONV_W:2 * CONV_W] = (d_u * zx).astype(BF16)
        dzc_ref[:, 2 * CONV_W:3 * CONV_W] = (d_u * zc).astype(BF16)

    def rev(i):
        return nblk - 1 - i

    def blk(c):
        return pl.BlockSpec((tm, c), lambda i: (rev(i), 0))

    in_specs = [
        pl.BlockSpec((FF_CHUNKS, tm, D_MODEL), lambda i: (0, rev(i), 0)),
        blk(D_MODEL), blk(D_MODEL), _full((1, D_MODEL)), _full((D_MODEL, D_MODEL)),
        blk(CONV_W), blk(ATTN_W), _full((1, CONV_W)), _full((1, ATTN_W)),
        blk(3 * CONV_W),
        pl.BlockSpec((16, 3 * CONV_W), lambda i: (jnp.maximum(rev(i) * (tm // 16) - 1, 0), 0)),
        _full((3, CONV_W)), _full((1, CONV_W)), _full((256, 256)),
    ]
    out_specs = [blk(D_MODEL), blk(ATTN_W), blk(ATTN_W), blk(3 * CONV_W), _full((D_MODEL, D_MODEL)),
                 _full((1, D_MODEL)), _full((1, CONV_W)), _full((1, ATTN_W)), _full((3, CONV_W)), _full((1, CONV_W))]
    return pl.pallas_call(
        body, name="outproj_bwd", grid=(nblk,), in_specs=in_specs, out_specs=out_specs,
        out_shape=[jax.ShapeDtypeStruct((t, D_MODEL), F32), jax.ShapeDtypeStruct((t, ATTN_W), F32),
                   jax.ShapeDtypeStruct((t, ATTN_W), F32), jax.ShapeDtypeStruct((t, 3 * CONV_W), BF16),
                   jax.ShapeDtypeStruct((D_MODEL, D_MODEL), F32), jax.ShapeDtypeStruct((1, D_MODEL), F32),
                   jax.ShapeDtypeStruct((1, CONV_W), F32), jax.ShapeDtypeStruct((1, ATTN_W), F32),
                   jax.ShapeDtypeStruct((3, CONV_W), F32), jax.ShapeDtypeStruct((1, CONV_W), F32)],
        scratch_shapes=[pltpu.VMEM((8, CONV_W), F32)],
        compiler_params=_cparams("arbitrary"),
    )(dh2, dx2, x1, g_ffn, w_out, yc, ya, goc, goa, zconv, zconv, conv_w, conv_b, bd)


def _attn_bwd(q, k, v, dya, lse, dd, e_all, m_all, after):
    t = q.shape[0]
    nsb = t // SUPER

    def body(q_ref, kc_ref, kp_ref, vc_ref, vp_ref, dy_ref, l_ref, d_ref, e_ref, m_ref, after_ref,
             dq_ref, dk_ref, dv_ref, kk, vv, dkacc, dvacc, dwide):
        s = pl.program_id(1)

        @pl.when(s == 0)
        def _():
            dkacc[...] = jnp.zeros_like(dkacc)
            dvacc[...] = jnp.zeros_like(dvacc)

        dkacc[0:SUPER, :] = dkacc[SUPER:, :]
        dvacc[0:SUPER, :] = dvacc[SUPER:, :]
        dkacc[SUPER:, :] = jnp.zeros((SUPER, QK_BLOCK), F32)
        dvacc[SUPER:, :] = jnp.zeros((SUPER, QK_BLOCK), F32)

        @pl.when(s < nsb)
        def _():
            kk[0:SUPER, :] = kp_ref[...]
            kk[SUPER:, :] = kc_ref[...]
            vv[0:SUPER, :] = vp_ref[...]
            vv[SUPER:, :] = vc_ref[...]
            head0 = lax.broadcasted_iota(jnp.int32, (QK_BLOCK, QK_BLOCK), 1) < HEAD_DIM

            def widened(a):
                other = pltpu.roll(a, HEAD_DIM, 1)
                first = lax.broadcasted_iota(jnp.int32, a.shape, 1) < HEAD_DIM
                return jnp.where(first, a, other), jnp.where(first, other, a)

            def stacked(h0, h1):
                return jnp.concatenate([jnp.concatenate([h0, h0], axis=1), jnp.concatenate([h1, h1], axis=1)], axis=0)

            def widen_dd(i, carry):
                rows = pl.ds(pl.multiple_of(i * 256, 256), 256)
                dwide[0, rows, :], dwide[1, rows, :] = widened(d_ref[rows, :])
                return carry

            lax.fori_loop(0, SUPER // 256, widen_dd, 0)

            for b, dil in enumerate(DILATIONS):
                def unit(u, carry, b=b, dil=dil):
                    start = _unit_start(u, dil)
                    first_key = SUPER + start - QK_BLOCK * dil
                    qrows = _rows(start, QK_BLOCK, dil)
                    krows = _rows(first_key, KEYS, dil)
                    q2 = _stack_heads(q_ref[qrows, :].astype(BF16), head0)
                    dy2 = _stack_heads(dy_ref[qrows, :].astype(BF16), head0)
                    g2 = stacked(*widened(jnp.exp(m_ref[b, qrows, :] - l_ref[qrows, :])))
                    d2 = stacked(dwide[0, qrows, :], dwide[1, qrows, :])
                    k2 = kk[krows, :].astype(BF16)
                    v2 = vv[krows, :].astype(BF16)
                    prob = e_ref[b * UNITS + u].astype(F32) * g2
                    ds = (prob * (_mm_nt(dy2, v2) - d2)).astype(BF16)
                    dvacc[krows, :] += _mm_tn(prob.astype(BF16), dy2)
                    dkacc[krows, :] += _mm_tn(ds, q2)
                    dq2 = _mm(ds, k2)
                    dq = jnp.where(head0, dq2[0:QK_BLOCK], dq2[QK_BLOCK:]) * ATTN_SCALE
                    if b == 0:
                        dq_ref[qrows, :] = dq
                    else:
                        dq_ref[qrows, :] += dq
                    return carry

                lax.fori_loop(0, UNITS, unit, 0, unroll=8)

        dk_ref[...] = dkacc[0:SUPER, :]
        dv_ref[...] = dvacc[0:SUPER, :].astype(BF16)

    def cur_map(p, s):
        return (jnp.minimum(s, nsb - 1), p)

    def prev_map(p, s):
        return (jnp.clip(s - 1, 0, nsb - 1), p)

    cur = pl.BlockSpec((SUPER, QK_BLOCK), cur_map)
    prev = pl.BlockSpec((SUPER, QK_BLOCK), prev_map)
    return pl.pallas_call(
        body, name="attn_bwd", grid=(4, nsb + 1),
        in_specs=[cur, cur, prev, cur, prev, cur, cur, cur,
                  pl.BlockSpec((None, None, 3 * UNITS, KEYS, KEYS), lambda p, s: (p, jnp.minimum(s, nsb - 1), 0, 0, 0)),
                  pl.BlockSpec((3, SUPER, QK_BLOCK), lambda p, s: (0, jnp.minimum(s, nsb - 1), p)),
                  pl.BlockSpec(memory_space=pl.ANY)],
        out_specs=[cur, prev, prev],
        out_shape=[jax.ShapeDtypeStruct((t, ATTN_W), F32), jax.ShapeDtypeStruct((t, ATTN_W), F32),
                   jax.ShapeDtypeStruct((t, ATTN_W), BF16)],
        scratch_shapes=[pltpu.VMEM((2 * SUPER, QK_BLOCK), F32)] * 4 + [pltpu.VMEM((2, SUPER, QK_BLOCK), F32)],
        compiler_params=_cparams("parallel", "arbitrary"),
    )(q, k, k, v, v, dya, lse, dd, e_all, m_all, after)


def _inproj_bwd(dq, dk, dv, dzconv, zqk, x, dx1, g_mix, w_in, qg, kg, bd, tm):
    t = x.shape[0]
    nblk = t // tm
    shard = IN_COLS // N_DEV

    def body(dq_ref, dk_ref, dv_ref, dzc_ref, zqk_ref, x_ref, dx1_ref, g_ref, w_ref, qg_ref,
             kg_ref, bd_ref, dx_ref, dw_hbm, dg_ref, dqg_ref, dkg_ref, dw_ref, stage, stage_sem):
        @pl.when(pl.program_id(0) == 0)
        def _():
            for ref in (dw_ref, dg_ref, dqg_ref, dkg_ref):
                ref[...] = jnp.zeros_like(ref)

        parts = [dzc_ref[...]]
        for j, (dn_ref, gain_ref, dgain_ref) in enumerate(((dq_ref, qg_ref, dqg_ref), (dk_ref, kg_ref, dkg_ref))):
            dn = dn_ref[...]
            z = zqk_ref[:, j * ATTN_W:(j + 1) * ATTN_W].astype(F32)
            r = lax.rsqrt(_seg_sum64(z * z, bd_ref) * (1.0 / HEAD_DIM) + EPS)
            zhat = z * r
            dgain_ref[...] += jnp.sum(dn * zhat, axis=0, keepdims=True)
            gd = dn * gain_ref[...]
            parts.append((r * (gd - zhat * (_seg_sum64(gd * zhat, bd_ref) * (1.0 / HEAD_DIM)))).astype(BF16))
        parts.append(dv_ref[...].astype(BF16))
        dz = jnp.concatenate(parts, axis=1)

        r, xhat = _rms_stats(x_ref[...])
        g = g_ref[...]
        dw_ref[...] += _mm_tn((xhat * g).astype(BF16), dz)
        dh = _mm_nt(dz, w_ref[...])
        dg_ref[...] += jnp.sum(dh * xhat, axis=0, keepdims=True)
        dx_ref[...] = dx1_ref[...] + _rms_bwd(dh, xhat, r, g)

        @pl.when(pl.program_id(0) == nblk - 1)
        def _():
            for k in range(N_DEV):
                stage[...] = dw_ref[:, k * shard:(k + 1) * shard].astype(BF16)
                copy = pltpu.make_async_copy(stage, dw_hbm.at[k], stage_sem)
                copy.start()
                copy.wait()

    def blk(c):
        return pl.BlockSpec((tm, c), lambda i: (i, 0))

    return pl.pallas_call(
        body, name="inproj_bwd", grid=(nblk,),
        in_specs=[blk(ATTN_W)] * 3 + [blk(3 * CONV_W), blk(2 * ATTN_W), blk(D_MODEL), blk(D_MODEL), _full((1, D_MODEL)),
                                      _full((D_MODEL, IN_COLS)), _full((1, ATTN_W)), _full((1, ATTN_W)),
                                      _full((256, 256))],
        out_specs=[blk(D_MODEL), ANY, _full((1, D_MODEL)), _full((1, ATTN_W)), _full((1, ATTN_W))],
        out_shape=[jax.ShapeDtypeStruct((t, D_MODEL), F32), jax.ShapeDtypeStruct((N_DEV, D_MODEL, shard), BF16),
                   jax.ShapeDtypeStruct((1, D_MODEL), F32), jax.ShapeDtypeStruct((1, ATTN_W), F32),
                   jax.ShapeDtypeStruct((1, ATTN_W), F32)],
        scratch_shapes=[pltpu.VMEM((D_MODEL, IN_COLS), F32), pltpu.VMEM((D_MODEL, shard), BF16),
                        pltpu.SemaphoreType.DMA],
        compiler_params=_cparams("arbitrary"),
    )(dq, dk, dv, dzconv, zqk, x, dx1, g_mix, w_in, qg, kg, bd)


def _ordered_after(a, token):
    return a if token is None else a + token[0:1, 0:1].reshape((1,) * a.ndim)


def _local_step(x, p, target, w, tms, hooks=None):
    hooks = hooks or {}
    bd = jnp.kron(jnp.eye(4, dtype=F32), jnp.ones((HEAD_DIM, HEAD_DIM), F32)).astype(BF16)
    qg = jnp.tile(w["q_norm_g"], (1, 8))
    kg = jnp.tile(w["k_norm_g"], (1, 8))
    slopes = jnp.exp2(-jnp.arange(1, 9, dtype=F32))
    slopes = jnp.broadcast_to(slopes.reshape(4, 2, 1), (4, 2, QK_BLOCK))

    zconv, zqk, yc, q, k, v = _inproj_fwd(x, w["g_mix"], w["w_in"], w["conv_w"], w["conv_b"], qg, kg, bd, tms[0])
    ya, lse, e_all, m_all = _attn_fwd(q, k, v, slopes)
    if "late_weights" in hooks:
        w = {**w, **hooks["late_weights"](lse)}
    x1 = _outproj_fwd(ya, yc, x, w["g_out_conv"], w["g_out_attn"], w["w_out"], tms[0])
    gp, up, h2, x2 = _ffn_fwd(x1, w["g_ffn"], w["w_gate"], w["w_up"], w["w_down"], w["ffn_conv_w"], w["ffn_conv_b"],
                              tms[1])
    dx2, dx2b, loss, dw_pg, dw_pp, dg_ple = _ple_fwd_bwd(x2, p, target, w["g_ple"], w["w_ple_gate"], w["w_ple_proj"], tms[0])
    dh2, dw_down, dw_up, dw_gate, dfcw, dfcb = _ffn_bwd(dx2b, h2, gp, up, w["w_gate"], w["w_up"], w["w_down"],
                                                        w["ffn_conv_w"], w["ffn_conv_b"], tms[0])
    token = None
    if "ffn_grads" in hooks:
        token = hooks["ffn_grads"]({"w_ple_gate": dw_pg, "w_ple_proj": dw_pp, "w_down": dw_down, "w_up": dw_up,
                                    "w_gate": dw_gate})
    dx1, dya, dd, dzconv, dw_out, dg_ffn, dgoc, dgoa, dcw, dcb = _outproj_bwd(
        dh2, dx2, x1, _ordered_after(w["g_ffn"], token), w["w_out"], yc, ya, w["g_out_conv"], w["g_out_attn"], zconv,
        w["conv_w"], w["conv_b"], bd, tms[1])
    token = hooks["outproj_done"](dx1) if "outproj_done" in hooks else None
    dq, dk, dv = _attn_bwd(q, k, v, dya, lse, dd, e_all, m_all, slopes if token is None else token)
    dx, dw_in, dg_mix, dqg, dkg = _inproj_bwd(dq, dk, dv, dzconv, zqk, x, dx1, w["g_mix"], w["w_in"], qg, kg, bd,
                                              tms[0])
    grads = {
        "g_mix": dg_mix, "w_in": dw_in, "conv_w": dcw, "conv_b": dcb,
        "q_norm_g": dqg.reshape(8, HEAD_DIM).sum(0, keepdims=True),
        "k_norm_g": dkg.reshape(8, HEAD_DIM).sum(0, keepdims=True),
        "g_out_conv": dgoc, "g_out_attn": dgoa, "w_out": dw_out, "g_ffn": dg_ffn, "w_gate": dw_gate, "w_up": dw_up,
        "ffn_conv_w": dfcw, "ffn_conv_b": dfcb, "w_down": dw_down, "g_ple": dg_ple, "w_ple_gate": dw_pg,
        "w_ple_proj": dw_pp,
    }
    return loss, dx, grads


ANY = pl.BlockSpec(memory_space=pl.ANY)
MESH = pl.DeviceIdType.MESH


def _all_gather(shards, name):
    n = len(shards)

    def body(*refs):
        ins, outs = refs[:n], refs[n:2 * n]
        send_sems, recv_sems, local_sems = refs[2 * n:]
        x, y, c = lax.axis_index("x"), lax.axis_index("y"), lax.axis_index("c")
        me, sibling = (x, y, c), (x, y, 1 - c)
        chips = [(1 - x, y), (x, 1 - y), (1 - x, 1 - y)]

        def slot(dev):
            return 4 * dev[0] + 2 * dev[1] + dev[2]

        def copy(b, k, block, to, src=None):
            dst = outs[b].at[slot(block)]
            return pltpu.make_async_remote_copy(
                src_ref=dst if src is None else src, dst_ref=dst, send_sem=send_sems.at[b, k],
                recv_sem=recv_sems.at[b, k], device_id=to, device_id_type=MESH)

        mine = [pltpu.make_async_copy(ins[b], outs[b].at[slot(me)], local_sems.at[b]) for b in range(n)]
        first, passed = [], []
        for b in range(n):
            mine[b].start()
            first.append(copy(b, 0, me, sibling, src=ins[b]))
            first += [copy(b, 1 + j, me, (*chip, c), src=ins[b]) for j, chip in enumerate(chips)]
        for cp in first:
            cp.start()
        for j, chip in enumerate(chips):
            for b in range(n):
                copy(b, 1 + j, (*chip, c), me).wait_recv()
                fwd = copy(b, 4 + j, (*chip, c), sibling)
                fwd.start()
                passed.append(fwd)
        for b in range(n):
            copy(b, 0, sibling, me).wait_recv()
            for j, chip in enumerate(chips):
                copy(b, 4 + j, (*chip, 1 - c), me).wait_recv()
        for cp in first + passed:
            cp.wait_send()
        for cp in mine:
            cp.wait()

    return pl.pallas_call(
        body, name=name,
        in_specs=[ANY] * n, out_specs=[ANY] * n,
        out_shape=[jax.ShapeDtypeStruct((N_DEV,) + s.shape, s.dtype) for s in shards],
        scratch_shapes=[pltpu.SemaphoreType.DMA((n, 7)), pltpu.SemaphoreType.DMA((n, 7)),
                        pltpu.SemaphoreType.DMA((n,))],
    )(*shards)


HBM = pl.BlockSpec(memory_space=pltpu.HBM)
SEM = pl.BlockSpec(memory_space=pltpu.SEMAPHORE)
EFFECT = pltpu.SideEffectType.DATAFLOW_SIDE_EFFECTING
FLIPS = ((0, 0, 1), (0, 1, 0), (0, 1, 1), (1, 0, 0), (1, 0, 1), (1, 1, 0), (1, 1, 1))


def _flip_peers():
    pos = (lax.axis_index("x"), lax.axis_index("y"), lax.axis_index("c"))
    return [tuple(1 - a if f else a for a, f in zip(pos, flip)) for flip in FLIPS]


def _hbm(a):
    return pltpu.with_memory_space_constraint(a, pltpu.HBM)


def _split_start(name, srcs, lands, plan, n_copies, after):
    n, m = len(srcs), len(lands)

    def body(*refs):
        send_sems, recv_sems, token = refs[n + m + 1], refs[n + m + 2], refs[-1]
        for i, (src, dst, peer) in enumerate(plan(refs[:n], refs[n:n + m])):
            pltpu.make_async_remote_copy(src_ref=src, dst_ref=dst, send_sem=send_sems.at[i], recv_sem=recv_sems.at[i],
                                         device_id=peer, device_id_type=MESH).start()
        token[...] = jnp.zeros_like(token)

    outs = pl.pallas_call(
        body, name=name + "_start",
        in_specs=[HBM] * (n + m) + [ANY],
        out_specs=[SEM, SEM] + [HBM] * (n + m) + [pl.BlockSpec(memory_space=pltpu.VMEM)],
        out_shape=[pltpu.SemaphoreType.DMA((n_copies,)), pltpu.SemaphoreType.DMA((n_copies,))]
        + [pltpu.HBM(a.shape, a.dtype) for a in list(srcs) + list(lands)] + [jax.ShapeDtypeStruct((8, 128), F32)],
        input_output_aliases={i: 2 + i for i in range(n + m)},
        compiler_params=pltpu.CompilerParams(has_side_effects=EFFECT),
    )(*[_hbm(a) for a in list(srcs) + list(lands)], after)
    return (outs[0], outs[1], outs[2:2 + n], outs[2 + n:2 + n + m]), outs[-1]


def _split_wait(name, started, plan, after):
    send_sems, recv_sems, srcs, lands = started
    n, m = len(srcs), len(lands)

    def body(*refs):
        send_ref, recv_ref = refs[n + m], refs[n + m + 1]
        for i, (src, dst, peer) in enumerate(plan(refs[:n], refs[n:n + m])):
            copy = pltpu.make_async_remote_copy(src_ref=src, dst_ref=dst, send_sem=send_ref.at[i],
                                                recv_sem=recv_ref.at[i], device_id=peer, device_id_type=MESH)
            copy.wait_send()
            copy.wait_recv()

    outs = pl.pallas_call(
        body, name=name + "_wait",
        in_specs=[HBM] * (n + m) + [SEM, SEM, ANY],
        out_specs=[HBM] * (n + m),
        out_shape=[pltpu.HBM(a.shape, a.dtype) for a in list(srcs) + list(lands)],
        input_output_aliases={i: i for i in range(n + m)},
        compiler_params=pltpu.CompilerParams(has_side_effects=EFFECT),
    )(*srcs, *lands, send_sems, recv_sems, after)
    return outs[:n], outs[n:]


def _gather_plan(srcs, lands):
    slot = 4 * lax.axis_index("x") + 2 * lax.axis_index("y") + lax.axis_index("c")
    return [(src, land.at[slot], peer) for src, land in zip(srcs, lands) for peer in _flip_peers()]


def _sibling_plan(srcs, lands):
    x, y, c = lax.axis_index("x"), lax.axis_index("y"), lax.axis_index("c")
    return [(src.at[k, 1 - c], land.at[k], (x, y, 1 - c)) for src, land in zip(srcs, lands) for k in range(N_CHIP)]


def _chip_plan(srcs, lands):
    x, y, c = lax.axis_index("x"), lax.axis_index("y"), lax.axis_index("c")
    return [(src.at[2 * cx + cy], land.at[2 * x + y], (cx, cy, c))
            for src, land in zip(srcs, lands) for cx, cy in ((1 - x, y), (x, 1 - y), (1 - x, 1 - y))]


def _row_tile(rows):
    for tr in range(min(rows, 512), 15, -16):
        if rows % tr == 0:
            return tr
    return rows


def _pair_sums(gs, lands, core, name):
    n = len(gs)

    def body(c_ref, *refs):
        for b in range(n):
            out = refs[2 * n + b]
            out[...] = (refs[b][...].astype(F32) + refs[n + b][...].astype(F32)).astype(out.dtype)

    def slab(a):
        return pl.BlockSpec((None,) + a.shape[1:], lambda k, c_ref: (k, 0, 0))

    return pl.pallas_call(
        body, name=name,
        grid_spec=pltpu.PrefetchScalarGridSpec(
            num_scalar_prefetch=1, grid=(N_CHIP,),
            in_specs=[pl.BlockSpec((None, None) + g.shape[2:], lambda k, c_ref: (k, c_ref[0], 0, 0)) for g in gs]
            + [slab(a) for a in lands],
            out_specs=[slab(a) for a in lands]),
        out_shape=[jax.ShapeDtypeStruct(a.shape, a.dtype) for a in lands],
        compiler_params=_cparams("parallel"),
    )(core, *gs, *lands)


def _adamw(owns, arriveds, chip, ws, ms, vs, name):
    n = len(ws)
    k, rows, cols = arriveds[0].shape
    tr = _row_tile(rows) if n == 1 else _row_tile(rows // 2)
    c1 = 1.0 / (1.0 - ADAM_B1 ** ADAM_STEP)
    c2 = 1.0 / (1.0 - ADAM_B2 ** ADAM_STEP)

    def body(chip_ref, *refs):
        for b in range(n):
            o_ref, p_ref, w_ref, m_ref, v_ref = refs[b], refs[n + b], refs[2 * n + b], refs[3 * n + b], refs[4 * n + b]
            g_ref, d_ref, nm_ref, nv_ref = refs[5 * n + 4 * b:5 * n + 4 * b + 4]

            def slab(j):
                return jnp.where(chip_ref[0] == j, o_ref[j], p_ref[j]).astype(F32)

            g = slab(0)
            for j in range(1, k):
                g = g + slab(j)
            g_ref[...] = g
            nm = ADAM_B1 * m_ref[...] + (1.0 - ADAM_B1) * g
            nv = ADAM_B2 * v_ref[...] + (1.0 - ADAM_B2) * (g * g)
            nm_ref[...] = nm
            nv_ref[...] = nv
            d_ref[...] = -ADAM_LR * ((nm * c1) / (jnp.sqrt(nv * c2) + ADAM_EPS) + ADAM_WD * w_ref[...])

    blk = pl.BlockSpec((tr, cols), lambda i, c: (i, 0))
    stack = pl.BlockSpec((k, tr, cols), lambda i, c: (0, i, 0))
    outs = pl.pallas_call(
        body, name=name,
        grid_spec=pltpu.PrefetchScalarGridSpec(num_scalar_prefetch=1, grid=(rows // tr,),
                                               in_specs=[stack] * (2 * n) + [blk] * (3 * n), out_specs=[blk] * (4 * n)),
        out_shape=[jax.ShapeDtypeStruct((rows, cols), F32)] * (4 * n),
        compiler_params=_cparams("parallel"),
    )(chip, *owns, *arriveds, *ws, *ms, *vs)
    return [tuple(outs[4 * b:4 * b + 4]) for b in range(n)]


SMALL_LAYOUT = (("g_mix", 0, 1024), ("conv_b", 1, 512), ("q_norm_g", 2, 64), ("k_norm_g", 3, 64),
                ("g_out_conv", 4, 512), ("g_out_attn", 5, 512), ("g_ffn", 6, 1024), ("ffn_conv_b", 7, 2816),
                ("g_ple", 10, 1024))
CONV_W_ROW = 11
FFN_CONV_W_ROW = 14
LOSS_ROW = 23


def _row_pieces(cols):
    return [(c, min(1024, cols - c)) for c in range(0, cols, 1024)]


def _pack_small(grads, loss_tile):
    names = [n for n, _, _ in SMALL_LAYOUT]

    def body(*refs):
        ins, cw_ref, fcw_ref, loss_ref, out_ref = refs[:len(names)], refs[-4], refs[-3], refs[-2], refs[-1]
        out_ref[...] = jnp.zeros_like(out_ref)
        for ref, (_, row, cols) in zip(ins, SMALL_LAYOUT):
            for j, (c, width) in enumerate(_row_pieces(cols)):
                out_ref[row + j:row + j + 1, 0:width] = ref[:, c:c + width]
        for k in range(3):
            out_ref[CONV_W_ROW + k:CONV_W_ROW + k + 1, 0:CONV_W] = cw_ref[k:k + 1, :]
            for j, (c, width) in enumerate(_row_pieces(D_FF)):
                row = FFN_CONV_W_ROW + 3 * k + j
                out_ref[row:row + 1, 0:width] = fcw_ref[k:k + 1, c:c + width]
        out_ref[LOSS_ROW:LOSS_ROW + 1, 0:128] = loss_ref[0:1, :]

    return pl.pallas_call(
        body, name="pack_small_grads", out_shape=jax.ShapeDtypeStruct((SMALL_ROWS, 1024), F32),
    )(*[grads[n] for n in names], grads["conv_w"], grads["ffn_conv_w"], loss_tile)


def _adamw_small(arrived, conv_parts, fconv_parts, wts, mom, var):
    names = [n for n, _, _ in SMALL_LAYOUT] + ["conv_w", "ffn_conv_w"]
    c1 = 1.0 / (1.0 - ADAM_B1 ** ADAM_STEP)
    c2 = 1.0 / (1.0 - ADAM_B2 ** ADAM_STEP)
    n = len(names)

    def body(*refs):
        land, cw_ref, fcw_ref = refs[0], refs[1], refs[2]
        state = refs[3:3 + 3 * n]
        outs = refs[3 + 3 * n:]

        def total(piece):
            acc = piece(0)
            for d in range(1, N_DEV):
                acc = acc + piece(d)
            return acc

        for i, name in enumerate(names):
            if name == "conv_w":
                g = total(lambda d: cw_ref[d])
            elif name == "ffn_conv_w":
                g = total(lambda d: fcw_ref[d])
            else:
                _, row, cols = SMALL_LAYOUT[i]
                pieces = [total(lambda d, j=j, width=width: land[d, row + j:row + j + 1, 0:width])
                          for j, (_, width) in enumerate(_row_pieces(cols))]
                g = pieces[0] if len(pieces) == 1 else jnp.concatenate(pieces, axis=1)
            w_ref, m_ref, v_ref = state[3 * i:3 * i + 3]
            nm = ADAM_B1 * m_ref[...] + (1.0 - ADAM_B1) * g
            nv = ADAM_B2 * v_ref[...] + (1.0 - ADAM_B2) * (g * g)
            outs[4 * i][...] = g
            outs[4 * i + 1][...] = -ADAM_LR * ((nm * c1) / (jnp.sqrt(nv * c2) + ADAM_EPS) + ADAM_WD * w_ref[...])
            outs[4 * i + 2][...] = nm
            outs[4 * i + 3][...] = nv
        outs[-1][...] = total(lambda d: land[d, LOSS_ROW:LOSS_ROW + 1, 0:128])

    state = [a[nm_] for nm_ in names for a in (wts, mom, var)]
    shapes = [jax.ShapeDtypeStruct(wts[nm_].shape, F32) for nm_ in names for _ in range(4)]
    outs = pl.pallas_call(
        body, name="adamw_small", out_shape=shapes + [jax.ShapeDtypeStruct((1, 128), F32)],
    )(arrived, conv_parts, fconv_parts, *state)
    return {nm_: tuple(outs[4 * i:4 * i + 4]) for i, nm_ in enumerate(names)}, outs[-1][0, 0]


COL_SHARDED = ("w_in", "w_ple_proj")
TRANSPOSED = ("w_gate", "w_up")
CONV_SHARDED = (("conv_w", CONV_W), ("ffn_conv_w", D_FF))


def _gathered_to_full(name, gathered):
    if name in COL_SHARDED:
        return gathered.transpose(1, 0, 2).reshape(gathered.shape[1], -1)
    return gathered.reshape(-1, gathered.shape[2])


def _full_to_stacked(name, grad, shard_shape):
    sr, sc = shard_shape
    if grad.ndim == 3:
        a = grad
    elif name in COL_SHARDED:
        a = grad.reshape(sr, N_DEV, sc).transpose(1, 0, 2)
    else:
        a = grad.reshape(N_DEV, sr, sc)
    return a.astype(BF16).reshape(N_CHIP, 2, sr, sc)


def _pad_rows(vec, rows):
    return jnp.pad(vec, (0, rows * 1024 - vec.shape[0])).reshape(rows, 1024)


def kernel(x, p, g_mix, w_in, conv_w, conv_b, q_norm_g, k_norm_g, g_out_conv, g_out_attn, w_out, g_ffn, w_gate, w_up, ffn_conv_w, ffn_conv_b, w_down, g_ple, w_ple_gate, w_ple_proj, loss_target, m_g_mix, m_w_in, m_conv_w, m_conv_b, m_q_norm_g, m_k_norm_g, m_g_out_conv, m_g_out_attn, m_w_out, m_g_ffn, m_w_gate, m_w_up, m_ffn_conv_w, m_ffn_conv_b, m_w_down, m_g_ple, m_w_ple_gate, m_w_ple_proj, v_g_mix, v_w_in, v_conv_w, v_conv_b, v_q_norm_g, v_k_norm_g, v_g_out_conv, v_g_out_attn, v_w_out, v_g_ffn, v_w_gate, v_w_up, v_ffn_conv_w, v_ffn_conv_b, v_w_down, v_g_ple, v_w_ple_gate, v_w_ple_proj):
    args = dict(locals())
    names = ["g_mix", "w_in", "conv_w", "conv_b", "q_norm_g", "k_norm_g", "g_out_conv", "g_out_attn", "w_out", "g_ffn",
             "w_gate", "w_up", "ffn_conv_w", "ffn_conv_b", "w_down", "g_ple", "w_ple_gate", "w_ple_proj"]
    big = list(BIG)
    conv = [n for n, _ in CONV_SHARDED]

    def local(prefix):
        out = {n: (args[prefix + n][0] if n in big or n in conv else args[prefix + n]) for n in names}
        out.update({n: out[n].T for n in TRANSPOSED})
        return out

    wts, mom, var = local(""), local("m_"), local("v_")
    shard_shapes = {n: wts[n].shape for n in big}
    dev = 4 * lax.axis_index("x") + 2 * lax.axis_index("y") + lax.axis_index("c")
    core = lax.axis_index("c").astype(jnp.int32).reshape(1)

    conv_local = _pad_rows(jnp.concatenate([wts[n].reshape(-1) for n in conv]), 8).reshape(8, 1024)
    late = [n for n in big if n != "w_in"]
    w_in_all, conv_all = _all_gather([wts["w_in"].astype(BF16), conv_local], "gather_weights")
    late_shards = [wts[n].astype(BF16) for n in late]
    gathering, token = _split_start("gather_late_weights", late_shards,
                                    [lax.empty((N_DEV,) + s.shape, BF16) for s in late_shards], _gather_plan,
                                    7 * len(late), w_in_all)
    full = dict(wts)
    full["w_in"] = _gathered_to_full("w_in", w_in_all)
    full["g_mix"] = _ordered_after(wts["g_mix"], token)
    flying = {}

    def late_weights(after):
        shards, lands = _split_wait("gather_late_weights", gathering, _gather_plan, after)
        return {n: _gathered_to_full(n, lax.dynamic_update_slice(land, shard[None], (dev, 0, 0)))
                for n, land, shard in zip(late, lands, shards)}

    early = ["w_ple_gate", "w_ple_proj", "w_down", "w_up", "w_gate"]

    def ffn_grads(g):
        stacked = [_full_to_stacked(n, g[n], shard_shapes[n]) for n in early]
        flying["sibling"], tok = _split_start("rs_sibling_early", stacked,
                                              [lax.empty((N_CHIP,) + s.shape[2:], BF16) for s in stacked],
                                              _sibling_plan, N_CHIP * len(early), g["w_down"])
        return tok

    def outproj_done(after):
        stacked, landed = _split_wait("rs_sibling_early", flying["sibling"], _sibling_plan, after)
        parts = _pair_sums(stacked, landed, core, "rs_pair_sums_early")
        flying["chip"], tok = _split_start("rs_chip_early", parts, [lax.empty(q.shape, BF16) for q in parts],
                                           _chip_plan, 3 * len(early), landed[0])
        return tok

    off = 0
    for n, width in CONV_SHARDED:
        sc = width // N_DEV
        a = conv_all.reshape(N_DEV, -1)[:, off:off + 3 * sc].reshape(N_DEV, 3, sc)
        full[n] = a.transpose(1, 0, 2).reshape(3, width)
        off += 3 * sc

    loss, dx, grads = _local_step(x[0], p[0, 0], loss_target[0], full, (512, 256),
                                  {"late_weights": late_weights, "ffn_grads": ffn_grads, "outproj_done": outproj_done})

    chip = (2 * lax.axis_index("x") + lax.axis_index("y")).astype(jnp.int32).reshape(1)

    def adamw_of(group, parts, arrived):
        out, by_shape = {}, {}
        for n, own, got in zip(group, parts, arrived):
            by_shape.setdefault(got.shape, []).append((n, own, got))
        for members in by_shape.values():
            ns = [n for n, _, _ in members]
            results = _adamw([own for _, own, _ in members], [got for _, _, got in members], chip,
                             [wts[n] for n in ns], [mom[n] for n in ns], [var[n] for n in ns], "adamw_" + "_".join(ns))
            out.update(zip(ns, results))
        return out

    last = [n for n in big if n not in early]
    stacked = [_full_to_stacked(n, grads[n], shard_shapes[n]) for n in last]
    flying["sibling_last"], tok = _split_start("rs_sibling_last", stacked,
                                               [lax.empty((N_CHIP,) + s.shape[2:], BF16) for s in stacked],
                                               _sibling_plan, N_CHIP * len(last), dx)
    (small_all,) = _all_gather([_ordered_after(_pack_small(grads, loss), tok)], "gather_small_grads")
    stacked, landed = _split_wait("rs_sibling_last", flying["sibling_last"], _sibling_plan, small_all)
    parts = _pair_sums(stacked, landed, core, "rs_pair_sums_last")
    flying["chip_last"], tok = _split_start("rs_chip_last", parts, [lax.empty(q.shape, BF16) for q in parts],
                                            _chip_plan, 3 * len(last), landed[0])

    parts, arrived = _split_wait("rs_chip_early", flying["chip"], _chip_plan, tok)
    out = adamw_of(early, parts, arrived)
    small_all = _ordered_after(small_all, tok)
    taps = small_all[:, CONV_W_ROW:CONV_W_ROW + 3, 0:CONV_W]
    ftaps = small_all[:, FFN_CONV_W_ROW:FFN_CONV_W_ROW + 9, :].reshape(N_DEV, 3, 3 * 1024)
    small_out, loss_total = _adamw_small(
        small_all, lax.dynamic_slice(taps, (0, 0, dev * (CONV_W // N_DEV)), (N_DEV, 3, CONV_W // N_DEV)),
        lax.dynamic_slice(ftaps, (0, 0, dev * (D_FF // N_DEV)), (N_DEV, 3, D_FF // N_DEV)), wts, mom, var)
    out.update(small_out)
    parts, arrived = _split_wait("rs_chip_last", flying["chip_last"], _chip_plan, out["w_gate"][0])
    out.update(adamw_of(last, parts, arrived))
    def result(n, which):
        a = out[n][which]
        return (a.T if n in TRANSPOSED else a).reshape(args[n].shape)

    return (loss_total, dx[None], *[result(n, which) for which in range(4) for n in names])
```

```python
import jax
import jax.numpy as jnp
from jax import lax
from jax.experimental import pallas as pl
from jax.experimental.pallas import tpu as pltpu

F32 = jnp.float32
BF16 = jnp.bfloat16

D_MODEL = 1024
CONV_W = 512
ATTN_W = 512
HEAD_DIM = 64
D_FF = 2816
PLE_DIM = 256
IN_COLS = 3 * CONV_W + 3 * ATTN_W
EPS = 1e-6
QK_BLOCK = 128
DILATIONS = (1, 4, 16)
ATTN_SCALE = HEAD_DIM ** -0.5

ADAM_LR = 0.001
ADAM_B1 = 0.9
ADAM_B2 = 0.999
ADAM_EPS = 1e-08
ADAM_WD = 0.01
ADAM_STEP = 10

N_DEV = 8
N_CHIP = 4
V7X_VMEM_LIMIT = 56 * 1024 * 1024
V7X_VMEM_LIMIT_LARGE = 62 * 1024 * 1024
FF_CHUNKS = 2
FFN_BWD_PARTS = 1

BIG = ("w_in", "w_out", "w_gate", "w_up", "w_down", "w_ple_gate", "w_ple_proj")
SMALL_ROWS = 24


def _cparams(*sem, vmem=V7X_VMEM_LIMIT):
    return pltpu.CompilerParams(dimension_semantics=sem, vmem_limit_bytes=vmem)


def _mm(a, b):
    return jnp.dot(a, b, preferred_element_type=F32)


def _mm_nt(a, b):
    return lax.dot_general(a, b, (((1,), (1,)), ((), ())), preferred_element_type=F32)


def _mm_tn(a, b):
    return lax.dot_general(a, b, (((0,), (0,)), ((), ())), preferred_element_type=F32)


def _full(shape):
    nd = len(shape)
    return pl.BlockSpec(shape, lambda *_: (0,) * nd)


def _rms_stats(x):
    r = lax.rsqrt(jnp.mean(x * x, axis=-1, keepdims=True) + EPS)
    return r, x * r


def _rms_bwd(dy, xhat, r, g):
    gd = dy * g
    return r * (gd - xhat * jnp.mean(gd * xhat, axis=-1, keepdims=True))


def _seg_sum64(v, bd_ref):
    outs = []
    for c in range(0, v.shape[1], 256):
        vc = v[:, c:c + 256]
        hi = vc.astype(BF16)
        lo = (vc - hi.astype(F32)).astype(BF16)
        outs.append(_mm(hi, bd_ref[...]) + _mm(lo, bd_ref[...]))
    return outs[0] if len(outs) == 1 else jnp.concatenate(outs, axis=1)


def _shift_rows(u, k, edge_rows):
    out = pltpu.roll(u, k, 0)
    row = lax.broadcasted_iota(jnp.int32, (8, u.shape[1]), 0)
    head = out[0:8]
    for j in range(k):
        head = jnp.where(row == j, edge_rows[k - 1 - j], head)
    return jnp.concatenate([head, out[8:]], axis=0)


def _shift_rows_up(u, k, edge_rows):
    n = u.shape[0]
    out = pltpu.roll(u, n - k, 0)
    row = lax.broadcasted_iota(jnp.int32, (8, u.shape[1]), 0)
    tail = out[n - 8:n]
    for j in range(k):
        tail = jnp.where(row == 8 - k + j, edge_rows[j], tail)
    return jnp.concatenate([out[0:n - 8], tail], axis=0)


def _conv_fwd(u, c1, c2, w_ref, b_ref):
    u1 = _shift_rows(u, 1, (c1,))
    u2 = _shift_rows(u, 2, (c1, c2))
    y = u2 * w_ref[0:1, :] + u1 * w_ref[1:2, :] + u * w_ref[2:3, :] + b_ref[...]
    return y, u1, u2


def _conv_bwd_input(dy, n1row, n2row, w_ref):
    d1 = _shift_rows_up(dy, 1, (n1row,))
    d2 = _shift_rows_up(dy, 2, (n1row, n2row))
    return dy * w_ref[2:3, :] + d1 * w_ref[1:2, :] + d2 * w_ref[0:1, :]


def _sigmoid(x):
    return 1.0 / (1.0 + jnp.exp(-x))


def _inproj_fwd(x, g_mix, w_in, conv_w, conv_b, qg, kg, bd, tm):
    t = x.shape[0]

    def body(x_ref, g_ref, w_ref, cw_ref, cb_ref, qg_ref, kg_ref, bd_ref,
             zc_ref, zqk_ref, yc_ref, q_ref, k_ref, v_ref, carry_ref):
        @pl.when(pl.program_id(0) == 0)
        def _():
            carry_ref[...] = jnp.zeros_like(carry_ref)

        _, xhat = _rms_stats(x_ref[...])
        h = (xhat * g_ref[...]).astype(BF16)
        zconv = _mm(h, w_ref[:, 0:3 * CONV_W])
        zc_ref[...] = zconv.astype(BF16)
        u = zconv[:, CONV_W:2 * CONV_W] * zconv[:, 2 * CONV_W:3 * CONV_W]
        cv, _, _ = _conv_fwd(u, carry_ref[7:8, :], carry_ref[6:7, :], cw_ref, cb_ref)
        yc_ref[...] = (zconv[:, 0:CONV_W] * cv).astype(BF16)
        carry_ref[...] = u[tm - 8:tm, :]

        zqk = _mm(h, w_ref[:, 3 * CONV_W:3 * CONV_W + 2 * ATTN_W])
        zqk_ref[...] = zqk.astype(BF16)
        for j, (gain_ref, out_ref, scale) in enumerate(((qg_ref, q_ref, ATTN_SCALE), (kg_ref, k_ref, 1.0))):
            z = zqk[:, j * ATTN_W:(j + 1) * ATTN_W]
            r = lax.rsqrt(_seg_sum64(z * z, bd_ref) * (1.0 / HEAD_DIM) + EPS)
            out_ref[...] = z * r * gain_ref[...] * scale
        v_ref[...] = _mm(h, w_ref[:, 3 * CONV_W + 2 * ATTN_W:IN_COLS])

    def blk(c):
        return pl.BlockSpec((tm, c), lambda i: (i, 0))

    return pl.pallas_call(
        body, name="inproj_fwd", grid=(t // tm,),
        in_specs=[blk(D_MODEL), _full((1, D_MODEL)), _full((D_MODEL, IN_COLS)), _full((3, CONV_W)),
                  _full((1, CONV_W)), _full((1, ATTN_W)), _full((1, ATTN_W)), _full((256, 256))],
        out_specs=[blk(3 * CONV_W), blk(2 * ATTN_W), blk(CONV_W), blk(ATTN_W), blk(ATTN_W), blk(ATTN_W)],
        out_shape=[jax.ShapeDtypeStruct((t, 3 * CONV_W), BF16), jax.ShapeDtypeStruct((t, 2 * ATTN_W), BF16),
                   jax.ShapeDtypeStruct((t, CONV_W), BF16), jax.ShapeDtypeStruct((t, ATTN_W), F32),
                   jax.ShapeDtypeStruct((t, ATTN_W), F32), jax.ShapeDtypeStruct((t, ATTN_W), F32)],
        scratch_shapes=[pltpu.VMEM((8, CONV_W), F32)],
        compiler_params=_cparams("arbitrary"),
    )(x, g_mix, w_in, conv_w, conv_b, qg, kg, bd)


SUPER = 16 * QK_BLOCK
KEYS = 2 * QK_BLOCK
UNITS = SUPER // QK_BLOCK


def _rows(start, size, dil):
    return pl.ds(start, size) if dil == 1 else pl.ds(start, size, stride=dil)


def _attn_bias(sl_ref, dil):
    qi = lax.broadcasted_iota(jnp.int32, (KEYS, KEYS), 0)
    kj = lax.broadcasted_iota(jnp.int32, (KEYS, KEYS), 1)
    step = jnp.bitwise_and(qi, QK_BLOCK - 1) + QK_BLOCK - kj
    slope = jnp.where(qi < QK_BLOCK, sl_ref[0, 0:1, 0:1], sl_ref[0, 1:2, 0:1])
    bias = jnp.where(jnp.logical_and(step >= 0, step <= QK_BLOCK), -slope * (step * dil).astype(F32), -jnp.inf)
    return bias, kj >= QK_BLOCK


def _unit_start(u, dil):
    if dil == 1:
        return pl.multiple_of(u * QK_BLOCK, QK_BLOCK)
    if dil == 4:
        return jnp.bitwise_and(u, 3) + (u // 4) * (4 * QK_BLOCK)
    return u


def _stack_heads(a, head0):
    zero = jnp.zeros_like(a)
    return jnp.concatenate([jnp.where(head0, a, zero), jnp.where(head0, zero, a)], axis=0)


def _attn_fwd(q, k, v, slopes):
    t = q.shape[0]
    nsb = t // SUPER

    def body(q_ref, kc_ref, kp_ref, vc_ref, vp_ref, sl_ref, o_ref, l_ref, e_ref, m_ref, kk, vv, ob, lb):
        s = pl.program_id(1)
        kk[0:SUPER, :] = kp_ref[...]
        kk[SUPER:, :] = kc_ref[...]
        vv[0:SUPER, :] = vp_ref[...]
        vv[SUPER:, :] = vc_ref[...]
        head0 = lax.broadcasted_iota(jnp.int32, (QK_BLOCK, QK_BLOCK), 1) < HEAD_DIM

        for b, dil in enumerate(DILATIONS):
            bias, own_half = _attn_bias(sl_ref, dil)

            def unit(u, carry, b=b, dil=dil, bias=bias, own_half=own_half):
                start = _unit_start(u, dil)
                first_key = SUPER + start - QK_BLOCK * dil
                q2 = _stack_heads(q_ref[_rows(start, QK_BLOCK, dil), :].astype(BF16), head0)
                k2 = kk[_rows(first_key, KEYS, dil), :].astype(BF16)
                v2 = vv[_rows(first_key, KEYS, dil), :].astype(BF16)
                has_prev = jnp.logical_or(s > 0, start >= QK_BLOCK * dil)
                sc = jnp.where(jnp.logical_or(own_half, has_prev), _mm_nt(q2, k2) + bias, -jnp.inf)
                m = jnp.max(sc, axis=-1, keepdims=True)
                e = jnp.exp(sc - m)
                den = jnp.sum(e, axis=-1, keepdims=True)
                eb = e.astype(BF16)
                e_ref[b * UNITS + u] = eb
                o2 = _mm(eb, v2) / den
                l2 = m + jnp.log(den)
                ob[b, _rows(start, QK_BLOCK, dil), :] = jnp.where(head0, o2[0:QK_BLOCK], o2[QK_BLOCK:])
                lb[b, _rows(start, QK_BLOCK, dil), :] = jnp.where(head0, l2[0:QK_BLOCK], l2[QK_BLOCK:])
                m_ref[b, _rows(start, QK_BLOCK, dil), :] = jnp.where(head0, m[0:QK_BLOCK], m[QK_BLOCK:])
                return carry

            lax.fori_loop(0, UNITS, unit, 0, unroll=16)

        def merge(i, carry):
            rows = pl.ds(pl.multiple_of(i * 256, 256), 256)
            la, lb_, lc = lb[0, rows, :], lb[1, rows, :], lb[2, rows, :]
            mx = jnp.maximum(jnp.maximum(la, lb_), lc)
            wa, wb, wc = jnp.exp(la - mx), jnp.exp(lb_ - mx), jnp.exp(lc - mx)
            sw = wa + wb + wc
            o_ref[rows, :] = ((wa * ob[0, rows, :] + wb * ob[1, rows, :] + wc * ob[2, rows, :]) / sw).astype(BF16)
            l_ref[rows, :] = mx + jnp.log(sw)
            return carry

        lax.fori_loop(0, SUPER // 256, merge, 0)

    cur = pl.BlockSpec((SUPER, QK_BLOCK), lambda p, s: (s, p))
    prev = pl.BlockSpec((SUPER, QK_BLOCK), lambda p, s: (jnp.maximum(s - 1, 0), p))
    return pl.pallas_call(
        body, name="attn_fwd", grid=(4, nsb),
        in_specs=[cur, cur, prev, cur, prev, pl.BlockSpec((1, 2, QK_BLOCK), lambda p, s: (p, 0, 0))],
        out_specs=[cur, cur, pl.BlockSpec((None, None, 3 * UNITS, KEYS, KEYS), lambda p, s: (p, s, 0, 0, 0)),
                   pl.BlockSpec((3, SUPER, QK_BLOCK), lambda p, s: (0, s, p))],
        out_shape=[jax.ShapeDtypeStruct((t, ATTN_W), BF16), jax.ShapeDtypeStruct((t, ATTN_W), F32),
                   jax.ShapeDtypeStruct((4, nsb, 3 * UNITS, KEYS, KEYS), BF16),
                   jax.ShapeDtypeStruct((3, t, ATTN_W), F32)],
        scratch_shapes=[pltpu.VMEM((2 * SUPER, QK_BLOCK), F32), pltpu.VMEM((2 * SUPER, QK_BLOCK), F32),
                        pltpu.VMEM((3, SUPER, QK_BLOCK), F32), pltpu.VMEM((3, SUPER, QK_BLOCK), F32)],
        compiler_params=_cparams("parallel", "arbitrary"),
    )(q, k, k, v, v, slopes)


def _outproj_fwd(ya, yc, x, goc, goa, w_out, tm):
    t = x.shape[0]

    def body(ya_ref, yc_ref, x_ref, goc_ref, goa_ref, w_ref, x1_ref):
        _, ychat = _rms_stats(yc_ref[...].astype(F32))
        _, yahat = _rms_stats(ya_ref[...].astype(F32))
        acc = _mm((ychat * goc_ref[...]).astype(BF16), w_ref[0:CONV_W, :])
        acc += _mm((yahat * goa_ref[...]).astype(BF16), w_ref[CONV_W:, :])
        x1_ref[...] = x_ref[...] + acc

    def blk(c):
        return pl.BlockSpec((tm, c), lambda i: (i, 0))

    return pl.pallas_call(
        body, name="outproj_fwd", grid=(t // tm,),
        in_specs=[blk(ATTN_W), blk(CONV_W), blk(D_MODEL), _full((1, CONV_W)), _full((1, ATTN_W)),
                  _full((D_MODEL, D_MODEL))],
        out_specs=blk(D_MODEL),
        out_shape=jax.ShapeDtypeStruct((t, D_MODEL), F32),
        compiler_params=_cparams("parallel"),
    )(ya, yc, x, goc, goa, w_out)


def _ffn_fwd(x1, g_ffn, w_gate_t, w_up_t, w_down, fcw, fcb, tm):
    t = x1.shape[0]

    def body(x_ref, g_ref, wg_ref, wu_ref, wd_ref, cw_ref, cb_ref, gp_ref, up_ref, h_ref, x2_ref, carry_ref):
        @pl.when(pl.program_id(0) == 0)
        def _():
            carry_ref[...] = jnp.zeros_like(carry_ref)

        xv = x_ref[...]
        _, xhat = _rms_stats(xv)
        h = (xhat * g_ref[...]).astype(BF16)
        h_ref[...] = h
        gp = _mm_nt(h, wg_ref[...])
        gp_ref[...] = gp.astype(BF16)
        gate, _, _ = _conv_fwd(gp, carry_ref[7:8, :], carry_ref[6:7, :], cw_ref, cb_ref)
        carry_ref[...] = gp[tm - 8:tm, :]
        up = _mm_nt(h, wu_ref[...])
        up_ref[...] = up.astype(BF16)
        a = (gate * _sigmoid(gate) * up).astype(BF16)
        x2_ref[...] = xv + _mm(a, wd_ref[...])

    def blk(c):
        return pl.BlockSpec((tm, c), lambda i: (i, 0))

    return pl.pallas_call(
        body, name="ffn_fwd", grid=(t // tm,),
        in_specs=[blk(D_MODEL), _full((1, D_MODEL)), _full((D_FF, D_MODEL)), _full((D_FF, D_MODEL)),
                  _full((D_FF, D_MODEL)), _full((3, D_FF)), _full((1, D_FF))],
        out_specs=[blk(D_FF), blk(D_FF), blk(D_MODEL), blk(D_MODEL)],
        out_shape=[jax.ShapeDtypeStruct((t, D_FF), BF16), jax.ShapeDtypeStruct((t, D_FF), BF16),
                   jax.ShapeDtypeStruct((t, D_MODEL), BF16), jax.ShapeDtypeStruct((t, D_MODEL), F32)],
        scratch_shapes=[pltpu.VMEM((8, D_FF), F32)],
        compiler_params=_cparams("arbitrary"),
    )(x1, g_ffn, w_gate_t, w_up_t, w_down, fcw, fcb)


def _ple_fwd_bwd(x2, p, target, g_ple, w_pg, w_pp, tm):
    t = x2.shape[0]

    def body(x_ref, p_ref, t_ref, g_ref, wg_ref, wp_ref, dx_ref, dxb_ref, loss_ref, dwg_ref, dwp_ref, dg_ref):
        @pl.when(pl.program_id(0) == 0)
        def _():
            loss_ref[...] = jnp.zeros_like(loss_ref)
            dwg_ref[...] = jnp.zeros_like(dwg_ref)
            dwp_ref[...] = jnp.zeros_like(dwp_ref)
            dg_ref[...] = jnp.zeros_like(dg_ref)

        xv = x_ref[...]
        r, xhat = _rms_stats(xv)
        g = g_ref[...]
        h = (xhat * g).astype(BF16)
        pg = _sigmoid(_mm(h, wg_ref[...]))
        pb = p_ref[...].astype(BF16)
        pp = _mm(pb, wp_ref[...])
        err = xv + pg * pp - t_ref[...]
        loss_ref[...] += 0.5 * jnp.sum(jnp.mean(err * err, axis=-1, keepdims=True))
        dx3 = err * (1.0 / D_MODEL)
        d_pp = (dx3 * pg).astype(BF16)
        d_pre = (dx3 * pp * pg * (1.0 - pg)).astype(BF16)
        dwp_ref[...] += _mm_tn(pb, d_pp)
        dwg_ref[...] += _mm_tn(h, d_pre)
        dh = _mm_nt(d_pre, wg_ref[...])
        dg_ref[...] += jnp.sum(dh * xhat, axis=0, keepdims=True)
        dx2 = dx3 + _rms_bwd(dh, xhat, r, g)
        dx_ref[...] = dx2
        dxb_ref[...] = dx2.astype(BF16)

    def blk(c):
        return pl.BlockSpec((tm, c), lambda i: (i, 0))

    return pl.pallas_call(
        body, name="ple_fwd_bwd", grid=(t // tm,),
        in_specs=[blk(D_MODEL), blk(PLE_DIM), blk(D_MODEL), _full((1, D_MODEL)), _full((D_MODEL, D_MODEL)),
                  _full((PLE_DIM, D_MODEL))],
        out_specs=[blk(D_MODEL), blk(D_MODEL), _full((8, 128)), _full((D_MODEL, D_MODEL)),
                   _full((PLE_DIM, D_MODEL)), _full((1, D_MODEL))],
        out_shape=[jax.ShapeDtypeStruct((t, D_MODEL), F32), jax.ShapeDtypeStruct((t, D_MODEL), BF16),
                   jax.ShapeDtypeStruct((8, 128), F32),
                   jax.ShapeDtypeStruct((D_MODEL, D_MODEL), F32), jax.ShapeDtypeStruct((PLE_DIM, D_MODEL), F32),
                   jax.ShapeDtypeStruct((1, D_MODEL), F32)],
        compiler_params=_cparams("arbitrary"),
    )(x2, p, target, g_ple, w_pg, w_pp)


def _ffn_bwd(dx2, h2, gp, up, w_gate, w_up, w_down, fcw, fcb, tm):
    t = dx2.shape[0]
    nblk = t // tm
    fc = D_FF // FF_CHUNKS
    half = tm // FFN_BWD_PARTS

    def body(dx_ref, h_ref, gp_ref, gph_ref, up_ref, wg_ref, wu_ref, wd_ref, cw_ref, cb_ref,
             dh_ref, dwd_hbm, dwu_hbm, dwg_hbm, dcw_ref, dcb_ref, carry_ref, a_scr, dup_scr, dgp_scr,
             dwd_acc, dwu_acc, dwg_acc, stage, stage_sem):
        i = pl.program_id(1)

        @pl.when(i == 0)
        def _():
            carry_ref[...] = jnp.zeros_like(carry_ref)
            dwd_acc[...] = jnp.zeros_like(dwd_acc)
            dwu_acc[...] = jnp.zeros_like(dwu_acc)
            dwg_acc[...] = jnp.zeros_like(dwg_acc)
            dcw_ref[...] = jnp.zeros_like(dcw_ref)
            dcb_ref[...] = jnp.zeros_like(dcb_ref)

        keep = (i < nblk - 1).astype(F32)
        later = carry_ref[...]
        for hf in reversed(range(FFN_BWD_PARTS)):
            rows = slice(hf * half, (hf + 1) * half)
            dxb = dx_ref[rows, :]
            gp_v = gp_ref[rows, :].astype(F32)
            if hf > 0:
                before = gp_ref[hf * half - 16:hf * half, :].astype(F32)
            else:
                before = gph_ref[...].astype(F32) * keep
            gate, gp1, gp2 = _conv_fwd(gp_v, before[15:16, :], before[14:15, :], cw_ref, cb_ref)
            s = _sigmoid(gate)
            silu = gate * s
            up_v = up_ref[rows, :].astype(F32)
            da = _mm_nt(dxb, wd_ref[...])
            a_scr[rows, :] = (silu * up_v).astype(BF16)
            d_up = (da * silu).astype(BF16)
            dup_scr[rows, :] = d_up
            d_gate = da * up_v * (s * (1.0 + gate * (1.0 - s)))
            d_gp = _conv_bwd_input(d_gate, later[0:1, :], later[1:2, :], cw_ref).astype(BF16)
            dgp_scr[rows, :] = d_gp
            later = d_gate[0:8, :]
            dcw_ref[0:1, :] += jnp.sum(d_gate * gp2, axis=0, keepdims=True)
            dcw_ref[1:2, :] += jnp.sum(d_gate * gp1, axis=0, keepdims=True)
            dcw_ref[2:3, :] += jnp.sum(d_gate * gp_v, axis=0, keepdims=True)
            dcb_ref[...] += jnp.sum(d_gate, axis=0, keepdims=True)
            dh_ref[rows, :] = (_mm(d_gp, wg_ref[...]) + _mm(d_up, wu_ref[...])).astype(BF16)
        carry_ref[...] = later
        dwd_acc[...] += _mm_tn(a_scr[...], dx_ref[...])
        dwu_acc[...] += _mm_tn(h_ref[...], dup_scr[...])
        dwg_acc[...] += _mm_tn(h_ref[...], dgp_scr[...])

        @pl.when(i == nblk - 1)
        def _():
            rows = pl.ds(pl.multiple_of(pl.program_id(0) * fc, 16), fc)
            for acc, out, flip in ((dwd_acc, dwd_hbm, False), (dwu_acc, dwu_hbm, True), (dwg_acc, dwg_hbm, True)):
                stage[...] = (acc[...].T if flip else acc[...]).astype(BF16)
                copy = pltpu.make_async_copy(stage, out.at[rows, :], stage_sem)
                copy.start()
                copy.wait()

    def rev(i):
        return nblk - 1 - i

    one = pl.Buffered(1)
    in_specs = [
        pl.BlockSpec((tm, D_MODEL), lambda j, i: (rev(i), 0)),
        pl.BlockSpec((tm, D_MODEL), lambda j, i: (rev(i), 0)),
        pl.BlockSpec((tm, fc), lambda j, i: (rev(i), j)),
        pl.BlockSpec((16, fc), lambda j, i: (jnp.maximum(rev(i) * (tm // 16) - 1, 0), j)),
        pl.BlockSpec((tm, fc), lambda j, i: (rev(i), j)),
        pl.BlockSpec((fc, D_MODEL), lambda j, i: (j, 0), pipeline_mode=one),
        pl.BlockSpec((fc, D_MODEL), lambda j, i: (j, 0), pipeline_mode=one),
        pl.BlockSpec((fc, D_MODEL), lambda j, i: (j, 0), pipeline_mode=one),
        pl.BlockSpec((3, fc), lambda j, i: (0, j)),
        pl.BlockSpec((1, fc), lambda j, i: (0, j)),
    ]
    out_specs = [
        pl.BlockSpec((None, tm, D_MODEL), lambda j, i: (j, rev(i), 0)),
        ANY, ANY, ANY,
        pl.BlockSpec((3, fc), lambda j, i: (0, j)),
        pl.BlockSpec((1, fc), lambda j, i: (0, j)),
    ]
    return pl.pallas_call(
        body, name="ffn_bwd", grid=(FF_CHUNKS, nblk), in_specs=in_specs, out_specs=out_specs,
        out_shape=[jax.ShapeDtypeStruct((FF_CHUNKS, t, D_MODEL), BF16), jax.ShapeDtypeStruct((D_FF, D_MODEL), BF16),
                   jax.ShapeDtypeStruct((D_FF, D_MODEL), BF16), jax.ShapeDtypeStruct((D_FF, D_MODEL), BF16),
                   jax.ShapeDtypeStruct((3, D_FF), F32), jax.ShapeDtypeStruct((1, D_FF), F32)],
        scratch_shapes=[pltpu.VMEM((8, fc), F32), pltpu.VMEM((tm, fc), BF16), pltpu.VMEM((tm, fc), BF16),
                        pltpu.VMEM((tm, fc), BF16), pltpu.VMEM((fc, D_MODEL), F32), pltpu.VMEM((D_MODEL, fc), F32),
                        pltpu.VMEM((D_MODEL, fc), F32), pltpu.VMEM((fc, D_MODEL), BF16), pltpu.SemaphoreType.DMA],
        compiler_params=_cparams("arbitrary", "arbitrary", vmem=V7X_VMEM_LIMIT_LARGE),
    )(dx2, h2, gp, gp, up, w_gate, w_up, w_down, fcw, fcb)


def _outproj_bwd(dh2, dx2, x1, g_ffn, w_out, yc, ya, goc, goa, zconv, conv_w, conv_b, bd, tm):
    t = x1.shape[0]
    nblk = t // tm

    def body(dh_ref, dx2_ref, x1_ref, g_ref, w_ref, yc_ref, ya_ref, goc_ref, goa_ref, zc_ref, zch_ref, cw_ref, cb_ref,
             bd_ref, dx1_ref, dya_ref, dd_ref, dzc_ref, dw_ref, dg_ref, dgoc_ref, dgoa_ref, dcw_ref, dcb_ref,
             carry_ref):
        i = pl.program_id(0)

        @pl.when(i == 0)
        def _():
            carry_ref[...] = jnp.zeros_like(carry_ref)
            for ref in (dw_ref, dg_ref, dgoc_ref, dgoa_ref, dcw_ref, dcb_ref):
                ref[...] = jnp.zeros_like(ref)

        keep = (i < nblk - 1).astype(F32)
        dh2_v = dh_ref[0].astype(F32)
        for j in range(1, FF_CHUNKS):
            dh2_v = dh2_v + dh_ref[j].astype(F32)
        r, xhat = _rms_stats(x1_ref[...])
        dg_ref[...] += jnp.sum(dh2_v * xhat, axis=0, keepdims=True)
        dx1 = dx2_ref[...] + _rms_bwd(dh2_v, xhat, r, g_ref[...])
        dx1_ref[...] = dx1
        dx1b = dx1.astype(BF16)
        dy = _mm_nt(dx1b, w_ref[...])

        yc_v = yc_ref[...].astype(F32)
        rc, ychat = _rms_stats(yc_v)
        dw_ref[0:CONV_W, :] += _mm_tn((ychat * goc_ref[...]).astype(BF16), dx1b)
        dyc = dy[:, 0:CONV_W]
        dgoc_ref[...] += jnp.sum(dyc * ychat, axis=0, keepdims=True)
        d_yc = _rms_bwd(dyc, ychat, rc, goc_ref[...])

        ya_v = ya_ref[...].astype(F32)
        ra, yahat = _rms_stats(ya_v)
        dw_ref[CONV_W:, :] += _mm_tn((yahat * goa_ref[...]).astype(BF16), dx1b)
        dya = dy[:, CONV_W:]
        dgoa_ref[...] += jnp.sum(dya * yahat, axis=0, keepdims=True)
        d_ya = _rms_bwd(dya, yahat, ra, goa_ref[...])
        dya_ref[...] = d_ya
        dd_ref[...] = _seg_sum64(d_ya * ya_v, bd_ref)

        zb = zc_ref[:, 0:CONV_W].astype(F32)
        zc = zc_ref[:, CONV_W:2 * CONV_W].astype(F32)
        zx = zc_ref[:, 2 * CONV_W:3 * CONV_W].astype(F32)
        u = zc * zx
        uh = (zch_ref[:, CONV_W:2 * CONV_W].astype(F32) * zch_ref[:, 2 * CONV_W:3 * CONV_W].astype(F32)) * keep
        cv, u1, u2 = _conv_fwd(u, uh[15:16, :], uh[14:15, :], cw_ref, cb_ref)
        d_cv = d_yc * zb
        d_u = _conv_bwd_input(d_cv, carry_ref[0:1, :], carry_ref[1:2, :], cw_ref)
        carry_ref[...] = d_cv[0:8, :]
        dcw_ref[0:1, :] += jnp.sum(d_cv * u2, axis=0, keepdims=True)
        dcw_ref[1:2, :] += jnp.sum(d_cv * u1, axis=0, keepdims=True)
        dcw_ref[2:3, :] += jnp.sum(d_cv * u, axis=0, keepdims=True)
        dcb_ref[...] += jnp.sum(d_cv, axis=0, keepdims=True)
        dzc_ref[:, 0:CONV_W] = (d_yc * cv).astype(BF16)
        dzc_ref[:, CONV_W:2 * CONV_W] = (d_u * zx).astype(BF16)
        dzc_ref[:, 2 * CONV_W:3 * CONV_W] = (d_u * zc).astype(BF16)

    def rev(i):
        return nblk - 1 - i

    def blk(c):
        return pl.BlockSpec((tm, c), lambda i: (rev(i), 0))

    in_specs = [
        pl.BlockSpec((FF_CHUNKS, tm, D_MODEL), lambda i: (0, rev(i), 0)),
        blk(D_MODEL), blk(D_MODEL), _full((1, D_MODEL)), _full((D_MODEL, D_MODEL)),
        blk(CONV_W), blk(ATTN_W), _full((1, CONV_W)), _full((1, ATTN_W)),
        blk(3 * CONV_W),
        pl.BlockSpec((16, 3 * CONV_W), lambda i: (jnp.maximum(rev(i) * (tm // 16) - 1, 0), 0)),
        _full((3, CONV_W)), _full((1, CONV_W)), _full((256, 256)),
    ]
    out_specs = [blk(D_MODEL), blk(ATTN_W), blk(ATTN_W), blk(3 * CONV_W), _full((D_MODEL, D_MODEL)),
                 _full((1, D_MODEL)), _full((1, CONV_W)), _full((1, ATTN_W)), _full((3, CONV_W)), _full((1, CONV_W))]
    return pl.pallas_call(
        body, name="outproj_bwd", grid=(nblk,), in_specs=in_specs, out_specs=out_specs,
        out_shape=[jax.ShapeDtypeStruct((t, D_MODEL), F32), jax.ShapeDtypeStruct((t, ATTN_W), F32),
                   jax.ShapeDtypeStruct((t, ATTN_W), F32), jax.ShapeDtypeStruct((t, 3 * CONV_W), BF16),
                   jax.ShapeDtypeStruct((D_MODEL, D_MODEL), F32), jax.ShapeDtypeStruct((1, D_MODEL), F32),
                   jax.ShapeDtypeStruct((1, CONV_W), F32), jax.ShapeDtypeStruct((1, ATTN_W), F32),
                   jax.ShapeDtypeStruct((3, CONV_W), F32), jax.ShapeDtypeStruct((1, CONV_W), F32)],
        scratch_shapes=[pltpu.VMEM((8, CONV_W), F32)],
        compiler_params=_cparams("arbitrary"),
    )(dh2, dx2, x1, g_ffn, w_out, yc, ya, goc, goa, zconv, zconv, conv_w, conv_b, bd)


def _attn_bwd(q, k, v, dya, lse, dd, e_all, m_all, after):
    t = q.shape[0]
    nsb = t // SUPER

    def body(q_ref, kc_ref, kp_ref, vc_ref, vp_ref, dy_ref, l_ref, d_ref, e_ref, m_ref, after_ref,
             dq_ref, dk_ref, dv_ref, kk, vv, dkacc, dvacc, dwide):
        s = pl.program_id(1)

        @pl.when(s == 0)
        def _():
            dkacc[...] = jnp.zeros_like(dkacc)
            dvacc[...] = jnp.zeros_like(dvacc)

        dkacc[0:SUPER, :] = dkacc[SUPER:, :]
        dvacc[0:SUPER, :] = dvacc[SUPER:, :]
        dkacc[SUPER:, :] = jnp.zeros((SUPER, QK_BLOCK), F32)
        dvacc[SUPER:, :] = jnp.zeros((SUPER, QK_BLOCK), F32)

        @pl.when(s < nsb)
        def _():
            kk[0:SUPER, :] = kp_ref[...]
            kk[SUPER:, :] = kc_ref[...]
            vv[0:SUPER, :] = vp_ref[...]
            vv[SUPER:, :] = vc_ref[...]
            head0 = lax.broadcasted_iota(jnp.int32, (QK_BLOCK, QK_BLOCK), 1) < HEAD_DIM

            def widened(a):
                other = pltpu.roll(a, HEAD_DIM, 1)
                first = lax.broadcasted_iota(jnp.int32, a.shape, 1) < HEAD_DIM
                return jnp.where(first, a, other), jnp.where(first, other, a)

            def stacked(h0, h1):
                return jnp.concatenate([jnp.concatenate([h0, h0], axis=1), jnp.concatenate([h1, h1], axis=1)], axis=0)

            def widen_dd(i, carry):
                rows = pl.ds(pl.multiple_of(i * 256, 256), 256)
                dwide[0, rows, :], dwide[1, rows, :] = widened(d_ref[rows, :])
                return carry

            lax.fori_loop(0, SUPER // 256, widen_dd, 0)

            for b, dil in enumerate(DILATIONS):
                def unit(u, carry, b=b, dil=dil):
                    start = _unit_start(u, dil)
                    first_key = SUPER + start - QK_BLOCK * dil
                    qrows = _rows(start, QK_BLOCK, dil)
                    krows = _rows(first_key, KEYS, dil)
                    q2 = _stack_heads(q_ref[qrows, :].astype(BF16), head0)
                    dy2 = _stack_heads(dy_ref[qrows, :].astype(BF16), head0)
                    g2 = stacked(*widened(jnp.exp(m_ref[b, qrows, :] - l_ref[qrows, :])))
                    d2 = stacked(dwide[0, qrows, :], dwide[1, qrows, :])
                    k2 = kk[krows, :].astype(BF16)
                    v2 = vv[krows, :].astype(BF16)
                    prob = e_ref[b * UNITS + u].astype(F32) * g2
                    ds = (prob * (_mm_nt(dy2, v2) - d2)).astype(BF16)
                    dvacc[krows, :] += _mm_tn(prob.astype(BF16), dy2)
                    dkacc[krows, :] += _mm_tn(ds, q2)
                    dq2 = _mm(ds, k2)
                    dq = jnp.where(head0, dq2[0:QK_BLOCK], dq2[QK_BLOCK:]) * ATTN_SCALE
                    if b == 0:
                        dq_ref[qrows, :] = dq
                    else:
                        dq_ref[qrows, :] += dq
                    return carry

                lax.fori_loop(0, UNITS, unit, 0, unroll=8)

        dk_ref[...] = dkacc[0:SUPER, :]
        dv_ref[...] = dvacc[0:SUPER, :].astype(BF16)

    def cur_map(p, s):
        return (jnp.minimum(s, nsb - 1), p)

    def prev_map(p, s):
        return (jnp.clip(s - 1, 0, nsb - 1), p)

    cur = pl.BlockSpec((SUPER, QK_BLOCK), cur_map)
    prev = pl.BlockSpec((SUPER, QK_BLOCK), prev_map)
    return pl.pallas_call(
        body, name="attn_bwd", grid=(4, nsb + 1),
        in_specs=[cur, cur, prev, cur, prev, cur, cur, cur,
                  pl.BlockSpec((None, None, 3 * UNITS, KEYS, KEYS), lambda p, s: (p, jnp.minimum(s, nsb - 1), 0, 0, 0)),
                  pl.BlockSpec((3, SUPER, QK_BLOCK), lambda p, s: (0, jnp.minimum(s, nsb - 1), p)),
                  pl.BlockSpec(memory_space=pl.ANY)],
        out_specs=[cur, prev, prev],
        out_shape=[jax.ShapeDtypeStruct((t, ATTN_W), F32), jax.ShapeDtypeStruct((t, ATTN_W), F32),
                   jax.ShapeDtypeStruct((t, ATTN_W), BF16)],
        scratch_shapes=[pltpu.VMEM((2 * SUPER, QK_BLOCK), F32)] * 4 + [pltpu.VMEM((2, SUPER, QK_BLOCK), F32)],
        compiler_params=_cparams("parallel", "arbitrary"),
    )(q, k, k, v, v, dya, lse, dd, e_all, m_all, after)


def _inproj_bwd(dq, dk, dv, dzconv, zqk, x, dx1, g_mix, w_in, qg, kg, bd, tm):
    t = x.shape[0]
    nblk = t // tm
    shard = IN_COLS // N_DEV

    def body(dq_ref, dk_ref, dv_ref, dzc_ref, zqk_ref, x_ref, dx1_ref, g_ref, w_ref, qg_ref,
             kg_ref, bd_ref, dx_ref, dw_hbm, dg_ref, dqg_ref, dkg_ref, dw_ref, stage, stage_sem):
        @pl.when(pl.program_id(0) == 0)
        def _():
            for ref in (dw_ref, dg_ref, dqg_ref, dkg_ref):
                ref[...] = jnp.zeros_like(ref)

        parts = [dzc_ref[...]]
        for j, (dn_ref, gain_ref, dgain_ref) in enumerate(((dq_ref, qg_ref, dqg_ref), (dk_ref, kg_ref, dkg_ref))):
            dn = dn_ref[...]
            z = zqk_ref[:, j * ATTN_W:(j + 1) * ATTN_W].astype(F32)
            r = lax.rsqrt(_seg_sum64(z * z, bd_ref) * (1.0 / HEAD_DIM) + EPS)
            zhat = z * r
            dgain_ref[...] += jnp.sum(dn * zhat, axis=0, keepdims=True)
            gd = dn * gain_ref[...]
            parts.append((r * (gd - zhat * (_seg_sum64(gd * zhat, bd_ref) * (1.0 / HEAD_DIM)))).astype(BF16))
        parts.append(dv_ref[...].astype(BF16))
        dz = jnp.concatenate(parts, axis=1)

        r, xhat = _rms_stats(x_ref[...])
        g = g_ref[...]
        dw_ref[...] += _mm_tn((xhat * g).astype(BF16), dz)
        dh = _mm_nt(dz, w_ref[...])
        dg_ref[...] += jnp.sum(dh * xhat, axis=0, keepdims=True)
        dx_ref[...] = dx1_ref[...] + _rms_bwd(dh, xhat, r, g)

        @pl.when(pl.program_id(0) == nblk - 1)
        def _():
            for k in range(N_DEV):
                stage[...] = dw_ref[:, k * shard:(k + 1) * shard].astype(BF16)
                copy = pltpu.make_async_copy(stage, dw_hbm.at[k], stage_sem)
                copy.start()
                copy.wait()

    def blk(c):
        return pl.BlockSpec((tm, c), lambda i: (i, 0))

    return pl.pallas_call(
        body, name="inproj_bwd", grid=(nblk,),
        in_specs=[blk(ATTN_W)] * 3 + [blk(3 * CONV_W), blk(2 * ATTN_W), blk(D_MODEL), blk(D_MODEL), _full((1, D_MODEL)),
                                      _full((D_MODEL, IN_COLS)), _full((1, ATTN_W)), _full((1, ATTN_W)),
                                      _full((256, 256))],
        out_specs=[blk(D_MODEL), ANY, _full((1, D_MODEL)), _full((1, ATTN_W)), _full((1, ATTN_W))],
        out_shape=[jax.ShapeDtypeStruct((t, D_MODEL), F32), jax.ShapeDtypeStruct((N_DEV, D_MODEL, shard), BF16),
                   jax.ShapeDtypeStruct((1, D_MODEL), F32), jax.ShapeDtypeStruct((1, ATTN_W), F32),
                   jax.ShapeDtypeStruct((1, ATTN_W), F32)],
        scratch_shapes=[pltpu.VMEM((D_MODEL, IN_COLS), F32), pltpu.VMEM((D_MODEL, shard), BF16),
                        pltpu.SemaphoreType.DMA],
        compiler_params=_cparams("arbitrary"),
    )(dq, dk, dv, dzconv, zqk, x, dx1, g_mix, w_in, qg, kg, bd)


def _ordered_after(a, token):
    return a if token is None else a + token[0:1, 0:1].reshape((1,) * a.ndim)


def _local_step(x, p, target, w, tms, hooks=None):
    hooks = hooks or {}
    bd = jnp.kron(jnp.eye(4, dtype=F32), jnp.ones((HEAD_DIM, HEAD_DIM), F32)).astype(BF16)
    qg = jnp.tile(w["q_norm_g"], (1, 8))
    kg = jnp.tile(w["k_norm_g"], (1, 8))
    slopes = jnp.exp2(-jnp.arange(1, 9, dtype=F32))
    slopes = jnp.broadcast_to(slopes.reshape(4, 2, 1), (4, 2, QK_BLOCK))

    zconv, zqk, yc, q, k, v = _inproj_fwd(x, w["g_mix"], w["w_in"], w["conv_w"], w["conv_b"], qg, kg, bd, tms[0])
    ya, lse, e_all, m_all = _attn_fwd(q, k, v, slopes)
    if "late_weights" in hooks:
        w = {**w, **hooks["late_weights"](lse)}
    x1 = _outproj_fwd(ya, yc, x, w["g_out_conv"], w["g_out_attn"], w["w_out"], tms[0])
    gp, up, h2, x2 = _ffn_fwd(x1, w["g_ffn"], w["w_gate"], w["w_up"], w["w_down"], w["ffn_conv_w"], w["ffn_conv_b"],
                              tms[1])
    dx2, dx2b, loss, dw_pg, dw_pp, dg_ple = _ple_fwd_bwd(x2, p, target, w["g_ple"], w["w_ple_gate"], w["w_ple_proj"], tms[0])
    dh2, dw_down, dw_up, dw_gate, dfcw, dfcb = _ffn_bwd(dx2b, h2, gp, up, w["w_gate"], w["w_up"], w["w_down"],
                                                        w["ffn_conv_w"], w["ffn_conv_b"], tms[0])
    token = None
    if "ffn_grads" in hooks:
        token = hooks["ffn_grads"]({"w_ple_gate": dw_pg, "w_ple_proj": dw_pp, "w_down": dw_down, "w_up": dw_up,
                                    "w_gate": dw_gate})
    dx1, dya, dd, dzconv, dw_out, dg_ffn, dgoc, dgoa, dcw, dcb = _outproj_bwd(
        dh2, dx2, x1, _ordered_after(w["g_ffn"], token), w["w_out"], yc, ya, w["g_out_conv"], w["g_out_attn"], zconv,
        w["conv_w"], w["conv_b"], bd, tms[1])
    token = hooks["outproj_done"](dx1) if "outproj_done" in hooks else None
    dq, dk, dv = _attn_bwd(q, k, v, dya, lse, dd, e_all, m_all, slopes if token is None else token)
    dx, dw_in, dg_mix, dqg, dkg = _inproj_bwd(dq, dk, dv, dzconv, zqk, x, dx1, w["g_mix"], w["w_in"], qg, kg, bd,
                                              tms[0])
    grads = {
        "g_mix": dg_mix, "w_in": dw_in, "conv_w": dcw, "conv_b": dcb,
        "q_norm_g": dqg.reshape(8, HEAD_DIM).sum(0, keepdims=True),
        "k_norm_g": dkg.reshape(8, HEAD_DIM).sum(0, keepdims=True),
        "g_out_conv": dgoc, "g_out_attn": dgoa, "w_out": dw_out, "g_ffn": dg_ffn, "w_gate": dw_gate, "w_up": dw_up,
        "ffn_conv_w": dfcw, "ffn_conv_b": dfcb, "w_down": dw_down, "g_ple": dg_ple, "w_ple_gate": dw_pg,
        "w_ple_proj": dw_pp,
    }
    return loss, dx, grads


ANY = pl.BlockSpec(memory_space=pl.ANY)
MESH = pl.DeviceIdType.MESH


def _all_gather(shards, name):
    n = len(shards)

    def body(*refs):
        ins, outs = refs[:n], refs[n:2 * n]
        send_sems, recv_sems, local_sems = refs[2 * n:]
        x, y, c = lax.axis_index("x"), lax.axis_index("y"), lax.axis_index("c")
        me, sibling = (x, y, c), (x, y, 1 - c)
        chips = [(1 - x, y), (x, 1 - y), (1 - x, 1 - y)]

        def slot(dev):
            return 4 * dev[0] + 2 * dev[1] + dev[2]

        def copy(b, k, block, to, src=None):
            dst = outs[b].at[slot(block)]
            return pltpu.make_async_remote_copy(
                src_ref=dst if src is None else src, dst_ref=dst, send_sem=send_sems.at[b, k],
                recv_sem=recv_sems.at[b, k], device_id=to, device_id_type=MESH)

        mine = [pltpu.make_async_copy(ins[b], outs[b].at[slot(me)], local_sems.at[b]) for b in range(n)]
        first, passed = [], []
        for b in range(n):
            mine[b].start()
            first.append(copy(b, 0, me, sibling, src=ins[b]))
            first += [copy(b, 1 + j, me, (*chip, c), src=ins[b]) for j, chip in enumerate(chips)]
        for cp in first:
            cp.start()
        for j, chip in enumerate(chips):
            for b in range(n):
                copy(b, 1 + j, (*chip, c), me).wait_recv()
                fwd = copy(b, 4 + j, (*chip, c), sibling)
                fwd.start()
                passed.append(fwd)
        for b in range(n):
            copy(b, 0, sibling, me).wait_recv()
            for j, chip in enumerate(chips):
                copy(b, 4 + j, (*chip, 1 - c), me).wait_recv()
        for cp in first + passed:
            cp.wait_send()
        for cp in mine:
            cp.wait()

    return pl.pallas_call(
        body, name=name,
        in_specs=[ANY] * n, out_specs=[ANY] * n,
        out_shape=[jax.ShapeDtypeStruct((N_DEV,) + s.shape, s.dtype) for s in shards],
        scratch_shapes=[pltpu.SemaphoreType.DMA((n, 7)), pltpu.SemaphoreType.DMA((n, 7)),
                        pltpu.SemaphoreType.DMA((n,))],
    )(*shards)


HBM = pl.BlockSpec(memory_space=pltpu.HBM)
SEM = pl.BlockSpec(memory_space=pltpu.SEMAPHORE)
EFFECT = pltpu.SideEffectType.DATAFLOW_SIDE_EFFECTING
FLIPS = ((0, 0, 1), (0, 1, 0), (0, 1, 1), (1, 0, 0), (1, 0, 1), (1, 1, 0), (1, 1, 1))


def _flip_peers():
    pos = (lax.axis_index("x"), lax.axis_index("y"), lax.axis_index("c"))
    return [tuple(1 - a if f else a for a, f in zip(pos, flip)) for flip in FLIPS]


def _hbm(a):
    return pltpu.with_memory_space_constraint(a, pltpu.HBM)


def _split_start(name, srcs, lands, plan, n_copies, after):
    n, m = len(srcs), len(lands)

    def body(*refs):
        send_sems, recv_sems, token = refs[n + m + 1], refs[n + m + 2], refs[-1]
        for i, (src, dst, peer) in enumerate(plan(refs[:n], refs[n:n + m])):
            pltpu.make_async_remote_copy(src_ref=src, dst_ref=dst, send_sem=send_sems.at[i], recv_sem=recv_sems.at[i],
                                         device_id=peer, device_id_type=MESH).start()
        token[...] = jnp.zeros_like(token)

    outs = pl.pallas_call(
        body, name=name + "_start",
        in_specs=[HBM] * (n + m) + [ANY],
        out_specs=[SEM, SEM] + [HBM] * (n + m) + [pl.BlockSpec(memory_space=pltpu.VMEM)],
        out_shape=[pltpu.SemaphoreType.DMA((n_copies,)), pltpu.SemaphoreType.DMA((n_copies,))]
        + [pltpu.HBM(a.shape, a.dtype) for a in list(srcs) + list(lands)] + [jax.ShapeDtypeStruct((8, 128), F32)],
        input_output_aliases={i: 2 + i for i in range(n + m)},
        compiler_params=pltpu.CompilerParams(has_side_effects=EFFECT),
    )(*[_hbm(a) for a in list(srcs) + list(lands)], after)
    return (outs[0], outs[1], outs[2:2 + n], outs[2 + n:2 + n + m]), outs[-1]


def _split_wait(name, started, plan, after):
    send_sems, recv_sems, srcs, lands = started
    n, m = len(srcs), len(lands)

    def body(*refs):
        send_ref, recv_ref = refs[n + m], refs[n + m + 1]
        for i, (src, dst, peer) in enumerate(plan(refs[:n], refs[n:n + m])):
            copy = pltpu.make_async_remote_copy(src_ref=src, dst_ref=dst, send_sem=send_ref.at[i],
                                                recv_sem=recv_ref.at[i], device_id=peer, device_id_type=MESH)
            copy.wait_send()
            copy.wait_recv()

    outs = pl.pallas_call(
        body, name=name + "_wait",
        in_specs=[HBM] * (n + m) + [SEM, SEM, ANY],
        out_specs=[HBM] * (n + m),
        out_shape=[pltpu.HBM(a.shape, a.dtype) for a in list(srcs) + list(lands)],
        input_output_aliases={i: i for i in range(n + m)},
        compiler_params=pltpu.CompilerParams(has_side_effects=EFFECT),
    )(*srcs, *lands, send_sems, recv_sems, after)
    return outs[:n], outs[n:]


def _gather_plan(srcs, lands):
    slot = 4 * lax.axis_index("x") + 2 * lax.axis_index("y") + lax.axis_index("c")
    return [(src, land.at[slot], peer) for src, land in zip(srcs, lands) for peer in _flip_peers()]


def _sibling_plan(srcs, lands):
    x, y, c = lax.axis_index("x"), lax.axis_index("y"), lax.axis_index("c")
    return [(src.at[k, 1 - c], land.at[k], (x, y, 1 - c)) for src, land in zip(srcs, lands) for k in range(N_CHIP)]


def _chip_plan(srcs, lands):
    x, y, c = lax.axis_index("x"), lax.axis_index("y"), lax.axis_index("c")
    return [(src.at[2 * cx + cy], land.at[2 * x + y], (cx, cy, c))
            for src, land in zip(srcs, lands) for cx, cy in ((1 - x, y), (x, 1 - y), (1 - x, 1 - y))]


def _row_tile(rows):
    for tr in range(min(rows, 512), 15, -16):
        if rows % tr == 0:
            return tr
    return rows


def _pair_sums(gs, lands, core, name):
    n = len(gs)

    def body(c_ref, *refs):
        for b in range(n):
            out = refs[2 * n + b]
            out[...] = (refs[b][...].astype(F32) + refs[n + b][...].astype(F32)).astype(out.dtype)

    def slab(a):
        return pl.BlockSpec((None,) + a.shape[1:], lambda k, c_ref: (k, 0, 0))

    return pl.pallas_call(
        body, name=name,
        grid_spec=pltpu.PrefetchScalarGridSpec(
            num_scalar_prefetch=1, grid=(N_CHIP,),
            in_specs=[pl.BlockSpec((None, None) + g.shape[2:], lambda k, c_ref: (k, c_ref[0], 0, 0)) for g in gs]
            + [slab(a) for a in lands],
            out_specs=[slab(a) for a in lands]),
        out_shape=[jax.ShapeDtypeStruct(a.shape, a.dtype) for a in lands],
        compiler_params=_cparams("parallel"),
    )(core, *gs, *lands)


def _adamw(own, arrived, chip, w, m, v, name):
    k, rows, cols = arrived.shape
    tr = _row_tile(rows)
    c1 = 1.0 / (1.0 - ADAM_B1 ** ADAM_STEP)
    c2 = 1.0 / (1.0 - ADAM_B2 ** ADAM_STEP)

    def body(chip_ref, o_ref, p_ref, w_ref, m_ref, v_ref, g_ref, d_ref, nm_ref, nv_ref):
        def slab(j):
            return jnp.where(chip_ref[0] == j, o_ref[j], p_ref[j]).astype(F32)

        g = slab(0)
        for j in range(1, k):
            g = g + slab(j)
        g_ref[...] = g
        nm = ADAM_B1 * m_ref[...] + (1.0 - ADAM_B1) * g
        nv = ADAM_B2 * v_ref[...] + (1.0 - ADAM_B2) * (g * g)
        nm_ref[...] = nm
        nv_ref[...] = nv
        d_ref[...] = -ADAM_LR * ((nm * c1) / (jnp.sqrt(nv * c2) + ADAM_EPS) + ADAM_WD * w_ref[...])

    blk = pl.BlockSpec((tr, cols), lambda i, c: (i, 0))
    stack = pl.BlockSpec((k, tr, cols), lambda i, c: (0, i, 0))
    return pl.pallas_call(
        body, name=name,
        grid_spec=pltpu.PrefetchScalarGridSpec(num_scalar_prefetch=1, grid=(rows // tr,),
                                               in_specs=[stack, stack, blk, blk, blk], out_specs=[blk] * 4),
        out_shape=[jax.ShapeDtypeStruct((rows, cols), F32)] * 4,
        compiler_params=_cparams("parallel"),
    )(chip, own, arrived, w, m, v)


SMALL_LAYOUT = (("g_mix", 0, 1024), ("conv_b", 1, 512), ("q_norm_g", 2, 64), ("k_norm_g", 3, 64),
                ("g_out_conv", 4, 512), ("g_out_attn", 5, 512), ("g_ffn", 6, 1024), ("ffn_conv_b", 7, 2816),
                ("g_ple", 10, 1024))
CONV_W_ROW = 11
FFN_CONV_W_ROW = 14
LOSS_ROW = 23


def _row_pieces(cols):
    return [(c, min(1024, cols - c)) for c in range(0, cols, 1024)]


def _pack_small(grads, loss_tile):
    names = [n for n, _, _ in SMALL_LAYOUT]

    def body(*refs):
        ins, cw_ref, fcw_ref, loss_ref, out_ref = refs[:len(names)], refs[-4], refs[-3], refs[-2], refs[-1]
        out_ref[...] = jnp.zeros_like(out_ref)
        for ref, (_, row, cols) in zip(ins, SMALL_LAYOUT):
            for j, (c, width) in enumerate(_row_pieces(cols)):
                out_ref[row + j:row + j + 1, 0:width] = ref[:, c:c + width]
        for k in range(3):
            out_ref[CONV_W_ROW + k:CONV_W_ROW + k + 1, 0:CONV_W] = cw_ref[k:k + 1, :]
            for j, (c, width) in enumerate(_row_pieces(D_FF)):
                row = FFN_CONV_W_ROW + 3 * k + j
                out_ref[row:row + 1, 0:width] = fcw_ref[k:k + 1, c:c + width]
        out_ref[LOSS_ROW:LOSS_ROW + 1, 0:128] = loss_ref[0:1, :]

    return pl.pallas_call(
        body, name="pack_small_grads", out_shape=jax.ShapeDtypeStruct((SMALL_ROWS, 1024), F32),
    )(*[grads[n] for n in names], grads["conv_w"], grads["ffn_conv_w"], loss_tile)


def _adamw_small(arrived, conv_parts, fconv_parts, wts, mom, var):
    names = [n for n, _, _ in SMALL_LAYOUT] + ["conv_w", "ffn_conv_w"]
    c1 = 1.0 / (1.0 - ADAM_B1 ** ADAM_STEP)
    c2 = 1.0 / (1.0 - ADAM_B2 ** ADAM_STEP)
    n = len(names)

    def body(*refs):
        land, cw_ref, fcw_ref = refs[0], refs[1], refs[2]
        state = refs[3:3 + 3 * n]
        outs = refs[3 + 3 * n:]

        def total(piece):
            acc = piece(0)
            for d in range(1, N_DEV):
                acc = acc + piece(d)
            return acc

        for i, name in enumerate(names):
            if name == "conv_w":
                g = total(lambda d: cw_ref[d])
            elif name == "ffn_conv_w":
                g = total(lambda d: fcw_ref[d])
            else:
                _, row, cols = SMALL_LAYOUT[i]
                pieces = [total(lambda d, j=j, width=width: land[d, row + j:row + j + 1, 0:width])
                          for j, (_, width) in enumerate(_row_pieces(cols))]
                g = pieces[0] if len(pieces) == 1 else jnp.concatenate(pieces, axis=1)
            w_ref, m_ref, v_ref = state[3 * i:3 * i + 3]
            nm = ADAM_B1 * m_ref[...] + (1.0 - ADAM_B1) * g
            nv = ADAM_B2 * v_ref[...] + (1.0 - ADAM_B2) * (g * g)
            outs[4 * i][...] = g
            outs[4 * i + 1][...] = -ADAM_LR * ((nm * c1) / (jnp.sqrt(nv * c2) + ADAM_EPS) + ADAM_WD * w_ref[...])
            outs[4 * i + 2][...] = nm
            outs[4 * i + 3][...] = nv
        outs[-1][...] = total(lambda d: land[d, LOSS_ROW:LOSS_ROW + 1, 0:128])

    state = [a[nm_] for nm_ in names for a in (wts, mom, var)]
    shapes = [jax.ShapeDtypeStruct(wts[nm_].shape, F32) for nm_ in names for _ in range(4)]
    outs = pl.pallas_call(
        body, name="adamw_small", out_shape=shapes + [jax.ShapeDtypeStruct((1, 128), F32)],
    )(arrived, conv_parts, fconv_parts, *state)
    return {nm_: tuple(outs[4 * i:4 * i + 4]) for i, nm_ in enumerate(names)}, outs[-1][0, 0]


COL_SHARDED = ("w_in", "w_ple_proj")
TRANSPOSED = ("w_gate", "w_up")
CONV_SHARDED = (("conv_w", CONV_W), ("ffn_conv_w", D_FF))


def _gathered_to_full(name, gathered):
    if name in COL_SHARDED:
        return gathered.transpose(1, 0, 2).reshape(gathered.shape[1], -1)
    return gathered.reshape(-1, gathered.shape[2])


def _full_to_stacked(name, grad, shard_shape):
    sr, sc = shard_shape
    if grad.ndim == 3:
        a = grad
    elif name in COL_SHARDED:
        a = grad.reshape(sr, N_DEV, sc).transpose(1, 0, 2)
    else:
        a = grad.reshape(N_DEV, sr, sc)
    return a.astype(BF16).reshape(N_CHIP, 2, sr, sc)


def _pad_rows(vec, rows):
    return jnp.pad(vec, (0, rows * 1024 - vec.shape[0])).reshape(rows, 1024)


def kernel(x, p, g_mix, w_in, conv_w, conv_b, q_norm_g, k_norm_g, g_out_conv, g_out_attn, w_out, g_ffn, w_gate, w_up, ffn_conv_w, ffn_conv_b, w_down, g_ple, w_ple_gate, w_ple_proj, loss_target, m_g_mix, m_w_in, m_conv_w, m_conv_b, m_q_norm_g, m_k_norm_g, m_g_out_conv, m_g_out_attn, m_w_out, m_g_ffn, m_w_gate, m_w_up, m_ffn_conv_w, m_ffn_conv_b, m_w_down, m_g_ple, m_w_ple_gate, m_w_ple_proj, v_g_mix, v_w_in, v_conv_w, v_conv_b, v_q_norm_g, v_k_norm_g, v_g_out_conv, v_g_out_attn, v_w_out, v_g_ffn, v_w_gate, v_w_up, v_ffn_conv_w, v_ffn_conv_b, v_w_down, v_g_ple, v_w_ple_gate, v_w_ple_proj):
    args = dict(locals())
    names = ["g_mix", "w_in", "conv_w", "conv_b", "q_norm_g", "k_norm_g", "g_out_conv", "g_out_attn", "w_out", "g_ffn",
             "w_gate", "w_up", "ffn_conv_w", "ffn_conv_b", "w_down", "g_ple", "w_ple_gate", "w_ple_proj"]
    big = list(BIG)
    conv = [n for n, _ in CONV_SHARDED]

    def local(prefix):
        out = {n: (args[prefix + n][0] if n in big or n in conv else args[prefix + n]) for n in names}
        out.update({n: out[n].T for n in TRANSPOSED})
        return out

    wts, mom, var = local(""), local("m_"), local("v_")
    shard_shapes = {n: wts[n].shape for n in big}
    dev = 4 * lax.axis_index("x") + 2 * lax.axis_index("y") + lax.axis_index("c")
    core = lax.axis_index("c").astype(jnp.int32).reshape(1)

    conv_local = _pad_rows(jnp.concatenate([wts[n].reshape(-1) for n in conv]), 8).reshape(8, 1024)
    late = [n for n in big if n != "w_in"]
    w_in_all, conv_all = _all_gather([wts["w_in"].astype(BF16), conv_local], "gather_weights")
    late_shards = [wts[n].astype(BF16) for n in late]
    gathering, token = _split_start("gather_late_weights", late_shards,
                                    [lax.empty((N_DEV,) + s.shape, BF16) for s in late_shards], _gather_plan,
                                    7 * len(late), w_in_all)
    full = dict(wts)
    full["w_in"] = _gathered_to_full("w_in", w_in_all)
    full["g_mix"] = _ordered_after(wts["g_mix"], token)
    flying = {}

    def late_weights(after):
        shards, lands = _split_wait("gather_late_weights", gathering, _gather_plan, after)
        return {n: _gathered_to_full(n, lax.dynamic_update_slice(land, shard[None], (dev, 0, 0)))
                for n, land, shard in zip(late, lands, shards)}

    early = ["w_ple_gate", "w_ple_proj", "w_down", "w_up", "w_gate"]

    def ffn_grads(g):
        stacked = [_full_to_stacked(n, g[n], shard_shapes[n]) for n in early]
        flying["sibling"], tok = _split_start("rs_sibling_early", stacked,
                                              [lax.empty((N_CHIP,) + s.shape[2:], BF16) for s in stacked],
                                              _sibling_plan, N_CHIP * len(early), g["w_down"])
        return tok

    def outproj_done(after):
        stacked, landed = _split_wait("rs_sibling_early", flying["sibling"], _sibling_plan, after)
        parts = _pair_sums(stacked, landed, core, "rs_pair_sums_early")
        flying["chip"], tok = _split_start("rs_chip_early", parts, [lax.empty(q.shape, BF16) for q in parts],
                                           _chip_plan, 3 * len(early), landed[0])
        return tok

    off = 0
    for n, width in CONV_SHARDED:
        sc = width // N_DEV
        a = conv_all.reshape(N_DEV, -1)[:, off:off + 3 * sc].reshape(N_DEV, 3, sc)
        full[n] = a.transpose(1, 0, 2).reshape(3, width)
        off += 3 * sc

    loss, dx, grads = _local_step(x[0], p[0, 0], loss_target[0], full, (512, 256),
                                  {"late_weights": late_weights, "ffn_grads": ffn_grads, "outproj_done": outproj_done})

    chip = (2 * lax.axis_index("x") + lax.axis_index("y")).astype(jnp.int32).reshape(1)

    def adamw_of(group, parts, arrived):
        return {n: _adamw(own, got, chip, wts[n], mom[n], var[n], f"adamw_{n}")
                for n, own, got in zip(group, parts, arrived)}

    last = [n for n in big if n not in early]
    stacked = [_full_to_stacked(n, grads[n], shard_shapes[n]) for n in last]
    flying["sibling_last"], tok = _split_start("rs_sibling_last", stacked,
                                               [lax.empty((N_CHIP,) + s.shape[2:], BF16) for s in stacked],
                                               _sibling_plan, N_CHIP * len(last), dx)
    packed = _pack_small(grads, loss)
    flying["small"], tok = _split_start("gather_small_grads", [packed], [lax.empty((N_DEV,) + packed.shape, F32)],
                                        _gather_plan, N_DEV - 1, tok)
    stacked, landed = _split_wait("rs_sibling_last", flying["sibling_last"], _sibling_plan, tok)
    parts = _pair_sums(stacked, landed, core, "rs_pair_sums_last")
    flying["chip_last"], tok = _split_start("rs_chip_last", parts, [lax.empty(q.shape, BF16) for q in parts],
                                            _chip_plan, 3 * len(last), landed[0])

    parts, arrived = _split_wait("rs_chip_early", flying["chip"], _chip_plan, tok)
    out = adamw_of(early, parts, arrived)
    (packed,), (small_all,) = _split_wait("gather_small_grads", flying["small"], _gather_plan, out[early[-1]][0])
    small_all = lax.dynamic_update_slice(small_all, packed[None], (dev, 0, 0))
    taps = small_all[:, CONV_W_ROW:CONV_W_ROW + 3, 0:CONV_W]
    ftaps = small_all[:, FFN_CONV_W_ROW:FFN_CONV_W_ROW + 9, :].reshape(N_DEV, 3, 3 * 1024)
    small_out, loss_total = _adamw_small(
        small_all, lax.dynamic_slice(taps, (0, 0, dev * (CONV_W // N_DEV)), (N_DEV, 3, CONV_W // N_DEV)),
        lax.dynamic_slice(ftaps, (0, 0, dev * (D_FF // N_DEV)), (N_DEV, 3, D_FF // N_DEV)), wts, mom, var)
    out.update(small_out)
    parts, arrived = _split_wait("rs_chip_last", flying["chip_last"], _chip_plan, small_out["g_mix"][0])
    out.update(adamw_of(last, parts, arrived))
    def result(n, which):
        a = out[n][which]
        return (a.T if n in TRANSPOSED else a).reshape(args[n].shape)

    return (loss_total, dx[None], *[result(n, which) for which in range(4) for n in names])
```

```python
import jax
import jax.numpy as jnp
from jax import lax
from jax.experimental import pallas as pl
from jax.experimental.pallas import tpu as pltpu

F32 = jnp.float32
BF16 = jnp.bfloat16

D_MODEL = 1024
CONV_W = 512
ATTN_W = 512
HEAD_DIM = 64
D_FF = 2816
PLE_DIM = 256
IN_COLS = 3 * CONV_W + 3 * ATTN_W
EPS = 1e-6
QK_BLOCK = 128
DILATIONS = (1, 4, 16)
ATTN_SCALE = HEAD_DIM ** -0.5

ADAM_LR = 0.001
ADAM_B1 = 0.9
ADAM_B2 = 0.999
ADAM_EPS = 1e-08
ADAM_WD = 0.01
ADAM_STEP = 10

N_DEV = 8
N_CHIP = 4
V7X_VMEM_LIMIT = 56 * 1024 * 1024
V7X_VMEM_LIMIT_LARGE = 62 * 1024 * 1024
FF_CHUNKS = 2
FFN_BWD_PARTS = 1

BIG = ("w_in", "w_out", "w_gate", "w_up", "w_down", "w_ple_gate", "w_ple_proj")
SMALL_ROWS = 24


def _cparams(*sem, vmem=V7X_VMEM_LIMIT):
    return pltpu.CompilerParams(dimension_semantics=sem, vmem_limit_bytes=vmem)


def _mm(a, b):
    return jnp.dot(a, b, preferred_element_type=F32)


def _mm_nt(a, b):
    return lax.dot_general(a, b, (((1,), (1,)), ((), ())), preferred_element_type=F32)


def _mm_tn(a, b):
    return lax.dot_general(a, b, (((0,), (0,)), ((), ())), preferred_element_type=F32)


def _full(shape):
    nd = len(shape)
    return pl.BlockSpec(shape, lambda *_: (0,) * nd)


def _rms_stats(x):
    r = lax.rsqrt(jnp.mean(x * x, axis=-1, keepdims=True) + EPS)
    return r, x * r


def _rms_bwd(dy, xhat, r, g):
    gd = dy * g
    return r * (gd - xhat * jnp.mean(gd * xhat, axis=-1, keepdims=True))


def _seg_sum64(v, bd_ref):
    outs = []
    for c in range(0, v.shape[1], 256):
        vc = v[:, c:c + 256]
        hi = vc.astype(BF16)
        lo = (vc - hi.astype(F32)).astype(BF16)
        outs.append(_mm(hi, bd_ref[...]) + _mm(lo, bd_ref[...]))
    return outs[0] if len(outs) == 1 else jnp.concatenate(outs, axis=1)


def _shift_rows(u, k, edge_rows):
    out = pltpu.roll(u, k, 0)
    row = lax.broadcasted_iota(jnp.int32, (8, u.shape[1]), 0)
    head = out[0:8]
    for j in range(k):
        head = jnp.where(row == j, edge_rows[k - 1 - j], head)
    return jnp.concatenate([head, out[8:]], axis=0)


def _shift_rows_up(u, k, edge_rows):
    n = u.shape[0]
    out = pltpu.roll(u, n - k, 0)
    row = lax.broadcasted_iota(jnp.int32, (8, u.shape[1]), 0)
    tail = out[n - 8:n]
    for j in range(k):
        tail = jnp.where(row == 8 - k + j, edge_rows[j], tail)
    return jnp.concatenate([out[0:n - 8], tail], axis=0)


def _conv_fwd(u, c1, c2, w_ref, b_ref):
    u1 = _shift_rows(u, 1, (c1,))
    u2 = _shift_rows(u, 2, (c1, c2))
    y = u2 * w_ref[0:1, :] + u1 * w_ref[1:2, :] + u * w_ref[2:3, :] + b_ref[...]
    return y, u1, u2


def _conv_bwd_input(dy, n1row, n2row, w_ref):
    d1 = _shift_rows_up(dy, 1, (n1row,))
    d2 = _shift_rows_up(dy, 2, (n1row, n2row))
    return dy * w_ref[2:3, :] + d1 * w_ref[1:2, :] + d2 * w_ref[0:1, :]


def _sigmoid(x):
    return 1.0 / (1.0 + jnp.exp(-x))


def _inproj_fwd(x, g_mix, w_in, conv_w, conv_b, qg, kg, bd, tm):
    t = x.shape[0]

    def body(x_ref, g_ref, w_ref, cw_ref, cb_ref, qg_ref, kg_ref, bd_ref,
             zc_ref, zqk_ref, yc_ref, q_ref, k_ref, v_ref, carry_ref):
        @pl.when(pl.program_id(0) == 0)
        def _():
            carry_ref[...] = jnp.zeros_like(carry_ref)

        _, xhat = _rms_stats(x_ref[...])
        h = (xhat * g_ref[...]).astype(BF16)
        zconv = _mm(h, w_ref[:, 0:3 * CONV_W])
        zc_ref[...] = zconv.astype(BF16)
        u = zconv[:, CONV_W:2 * CONV_W] * zconv[:, 2 * CONV_W:3 * CONV_W]
        cv, _, _ = _conv_fwd(u, carry_ref[7:8, :], carry_ref[6:7, :], cw_ref, cb_ref)
        yc_ref[...] = (zconv[:, 0:CONV_W] * cv).astype(BF16)
        carry_ref[...] = u[tm - 8:tm, :]

        zqk = _mm(h, w_ref[:, 3 * CONV_W:3 * CONV_W + 2 * ATTN_W])
        zqk_ref[...] = zqk.astype(BF16)
        for j, (gain_ref, out_ref, scale) in enumerate(((qg_ref, q_ref, ATTN_SCALE), (kg_ref, k_ref, 1.0))):
            z = zqk[:, j * ATTN_W:(j + 1) * ATTN_W]
            r = lax.rsqrt(_seg_sum64(z * z, bd_ref) * (1.0 / HEAD_DIM) + EPS)
            out_ref[...] = z * r * gain_ref[...] * scale
        v_ref[...] = _mm(h, w_ref[:, 3 * CONV_W + 2 * ATTN_W:IN_COLS])

    def blk(c):
        return pl.BlockSpec((tm, c), lambda i: (i, 0))

    return pl.pallas_call(
        body, name="inproj_fwd", grid=(t // tm,),
        in_specs=[blk(D_MODEL), _full((1, D_MODEL)), _full((D_MODEL, IN_COLS)), _full((3, CONV_W)),
                  _full((1, CONV_W)), _full((1, ATTN_W)), _full((1, ATTN_W)), _full((256, 256))],
        out_specs=[blk(3 * CONV_W), blk(2 * ATTN_W), blk(CONV_W), blk(ATTN_W), blk(ATTN_W), blk(ATTN_W)],
        out_shape=[jax.ShapeDtypeStruct((t, 3 * CONV_W), BF16), jax.ShapeDtypeStruct((t, 2 * ATTN_W), BF16),
                   jax.ShapeDtypeStruct((t, CONV_W), BF16), jax.ShapeDtypeStruct((t, ATTN_W), F32),
                   jax.ShapeDtypeStruct((t, ATTN_W), F32), jax.ShapeDtypeStruct((t, ATTN_W), F32)],
        scratch_shapes=[pltpu.VMEM((8, CONV_W), F32)],
        compiler_params=_cparams("arbitrary"),
    )(x, g_mix, w_in, conv_w, conv_b, qg, kg, bd)


SUPER = 16 * QK_BLOCK
KEYS = 2 * QK_BLOCK
UNITS = SUPER // QK_BLOCK


def _rows(start, size, dil):
    return pl.ds(start, size) if dil == 1 else pl.ds(start, size, stride=dil)


def _attn_bias(sl_ref, dil):
    qi = lax.broadcasted_iota(jnp.int32, (KEYS, KEYS), 0)
    kj = lax.broadcasted_iota(jnp.int32, (KEYS, KEYS), 1)
    step = jnp.bitwise_and(qi, QK_BLOCK - 1) + QK_BLOCK - kj
    slope = jnp.where(qi < QK_BLOCK, sl_ref[0, 0:1, 0:1], sl_ref[0, 1:2, 0:1])
    bias = jnp.where(jnp.logical_and(step >= 0, step <= QK_BLOCK), -slope * (step * dil).astype(F32), -jnp.inf)
    return bias, kj >= QK_BLOCK


def _unit_start(u, dil):
    if dil == 1:
        return pl.multiple_of(u * QK_BLOCK, QK_BLOCK)
    if dil == 4:
        return jnp.bitwise_and(u, 3) + (u // 4) * (4 * QK_BLOCK)
    return u


def _stack_heads(a, head0):
    zero = jnp.zeros_like(a)
    return jnp.concatenate([jnp.where(head0, a, zero), jnp.where(head0, zero, a)], axis=0)


def _attn_fwd(q, k, v, slopes):
    t = q.shape[0]
    nsb = t // SUPER

    def body(q_ref, kc_ref, kp_ref, vc_ref, vp_ref, sl_ref, o_ref, l_ref, e_ref, m_ref, kk, vv, ob, lb):
        s = pl.program_id(1)
        kk[0:SUPER, :] = kp_ref[...]
        kk[SUPER:, :] = kc_ref[...]
        vv[0:SUPER, :] = vp_ref[...]
        vv[SUPER:, :] = vc_ref[...]
        head0 = lax.broadcasted_iota(jnp.int32, (QK_BLOCK, QK_BLOCK), 1) < HEAD_DIM

        for b, dil in enumerate(DILATIONS):
            bias, own_half = _attn_bias(sl_ref, dil)

            def unit(u, carry, b=b, dil=dil, bias=bias, own_half=own_half):
                start = _unit_start(u, dil)
                first_key = SUPER + start - QK_BLOCK * dil
                q2 = _stack_heads(q_ref[_rows(start, QK_BLOCK, dil), :].astype(BF16), head0)
                k2 = kk[_rows(first_key, KEYS, dil), :].astype(BF16)
                v2 = vv[_rows(first_key, KEYS, dil), :].astype(BF16)
                has_prev = jnp.logical_or(s > 0, start >= QK_BLOCK * dil)
                sc = jnp.where(jnp.logical_or(own_half, has_prev), _mm_nt(q2, k2) + bias, -jnp.inf)
                m = jnp.max(sc, axis=-1, keepdims=True)
                e = jnp.exp(sc - m)
                den = jnp.sum(e, axis=-1, keepdims=True)
                eb = e.astype(BF16)
                e_ref[b * UNITS + u] = eb
                o2 = _mm(eb, v2) / den
                l2 = m + jnp.log(den)
                ob[b, _rows(start, QK_BLOCK, dil), :] = jnp.where(head0, o2[0:QK_BLOCK], o2[QK_BLOCK:])
                lb[b, _rows(start, QK_BLOCK, dil), :] = jnp.where(head0, l2[0:QK_BLOCK], l2[QK_BLOCK:])
                m_ref[b, _rows(start, QK_BLOCK, dil), :] = jnp.where(head0, m[0:QK_BLOCK], m[QK_BLOCK:])
                return carry

            lax.fori_loop(0, UNITS, unit, 0, unroll=16)

        def merge(i, carry):
            rows = pl.ds(pl.multiple_of(i * 256, 256), 256)
            la, lb_, lc = lb[0, rows, :], lb[1, rows, :], lb[2, rows, :]
            mx = jnp.maximum(jnp.maximum(la, lb_), lc)
            wa, wb, wc = jnp.exp(la - mx), jnp.exp(lb_ - mx), jnp.exp(lc - mx)
            sw = wa + wb + wc
            o_ref[rows, :] = ((wa * ob[0, rows, :] + wb * ob[1, rows, :] + wc * ob[2, rows, :]) / sw).astype(BF16)
            l_ref[rows, :] = mx + jnp.log(sw)
            return carry

        lax.fori_loop(0, SUPER // 256, merge, 0)

    cur = pl.BlockSpec((SUPER, QK_BLOCK), lambda p, s: (s, p))
    prev = pl.BlockSpec((SUPER, QK_BLOCK), lambda p, s: (jnp.maximum(s - 1, 0), p))
    return pl.pallas_call(
        body, name="attn_fwd", grid=(4, nsb),
        in_specs=[cur, cur, prev, cur, prev, pl.BlockSpec((1, 2, QK_BLOCK), lambda p, s: (p, 0, 0))],
        out_specs=[cur, cur, pl.BlockSpec((None, None, 3 * UNITS, KEYS, KEYS), lambda p, s: (p, s, 0, 0, 0)),
                   pl.BlockSpec((3, SUPER, QK_BLOCK), lambda p, s: (0, s, p))],
        out_shape=[jax.ShapeDtypeStruct((t, ATTN_W), BF16), jax.ShapeDtypeStruct((t, ATTN_W), F32),
                   jax.ShapeDtypeStruct((4, nsb, 3 * UNITS, KEYS, KEYS), BF16),
                   jax.ShapeDtypeStruct((3, t, ATTN_W), F32)],
        scratch_shapes=[pltpu.VMEM((2 * SUPER, QK_BLOCK), F32), pltpu.VMEM((2 * SUPER, QK_BLOCK), F32),
                        pltpu.VMEM((3, SUPER, QK_BLOCK), F32), pltpu.VMEM((3, SUPER, QK_BLOCK), F32)],
        compiler_params=_cparams("parallel", "arbitrary"),
    )(q, k, k, v, v, slopes)


def _outproj_fwd(ya, yc, x, goc, goa, w_out, tm):
    t = x.shape[0]

    def body(ya_ref, yc_ref, x_ref, goc_ref, goa_ref, w_ref, x1_ref):
        _, ychat = _rms_stats(yc_ref[...].astype(F32))
        _, yahat = _rms_stats(ya_ref[...].astype(F32))
        acc = _mm((ychat * goc_ref[...]).astype(BF16), w_ref[0:CONV_W, :])
        acc += _mm((yahat * goa_ref[...]).astype(BF16), w_ref[CONV_W:, :])
        x1_ref[...] = x_ref[...] + acc

    def blk(c):
        return pl.BlockSpec((tm, c), lambda i: (i, 0))

    return pl.pallas_call(
        body, name="outproj_fwd", grid=(t // tm,),
        in_specs=[blk(ATTN_W), blk(CONV_W), blk(D_MODEL), _full((1, CONV_W)), _full((1, ATTN_W)),
                  _full((D_MODEL, D_MODEL))],
        out_specs=blk(D_MODEL),
        out_shape=jax.ShapeDtypeStruct((t, D_MODEL), F32),
        compiler_params=_cparams("parallel"),
    )(ya, yc, x, goc, goa, w_out)


def _ffn_fwd(x1, g_ffn, w_gate_t, w_up_t, w_down, fcw, fcb, tm):
    t = x1.shape[0]

    def body(x_ref, g_ref, wg_ref, wu_ref, wd_ref, cw_ref, cb_ref, gp_ref, up_ref, h_ref, x2_ref, carry_ref):
        @pl.when(pl.program_id(0) == 0)
        def _():
            carry_ref[...] = jnp.zeros_like(carry_ref)

        xv = x_ref[...]
        _, xhat = _rms_stats(xv)
        h = (xhat * g_ref[...]).astype(BF16)
        h_ref[...] = h
        gp = _mm_nt(h, wg_ref[...])
        gp_ref[...] = gp.astype(BF16)
        gate, _, _ = _conv_fwd(gp, carry_ref[7:8, :], carry_ref[6:7, :], cw_ref, cb_ref)
        carry_ref[...] = gp[tm - 8:tm, :]
        up = _mm_nt(h, wu_ref[...])
        up_ref[...] = up.astype(BF16)
        a = (gate * _sigmoid(gate) * up).astype(BF16)
        x2_ref[...] = xv + _mm(a, wd_ref[...])

    def blk(c):
        return pl.BlockSpec((tm, c), lambda i: (i, 0))

    return pl.pallas_call(
        body, name="ffn_fwd", grid=(t // tm,),
        in_specs=[blk(D_MODEL), _full((1, D_MODEL)), _full((D_FF, D_MODEL)), _full((D_FF, D_MODEL)),
                  _full((D_FF, D_MODEL)), _full((3, D_FF)), _full((1, D_FF))],
        out_specs=[blk(D_FF), blk(D_FF), blk(D_MODEL), blk(D_MODEL)],
        out_shape=[jax.ShapeDtypeStruct((t, D_FF), BF16), jax.ShapeDtypeStruct((t, D_FF), BF16),
                   jax.ShapeDtypeStruct((t, D_MODEL), BF16), jax.ShapeDtypeStruct((t, D_MODEL), F32)],
        scratch_shapes=[pltpu.VMEM((8, D_FF), F32)],
        compiler_params=_cparams("arbitrary"),
    )(x1, g_ffn, w_gate_t, w_up_t, w_down, fcw, fcb)


def _ple_fwd_bwd(x2, p, target, g_ple, w_pg, w_pp, tm):
    t = x2.shape[0]

    def body(x_ref, p_ref, t_ref, g_ref, wg_ref, wp_ref, dx_ref, dxb_ref, loss_ref, dwg_ref, dwp_ref, dg_ref):
        @pl.when(pl.program_id(0) == 0)
        def _():
            loss_ref[...] = jnp.zeros_like(loss_ref)
            dwg_ref[...] = jnp.zeros_like(dwg_ref)
            dwp_ref[...] = jnp.zeros_like(dwp_ref)
            dg_ref[...] = jnp.zeros_like(dg_ref)

        xv = x_ref[...]
        r, xhat = _rms_stats(xv)
        g = g_ref[...]
        h = (xhat * g).astype(BF16)
        pg = _sigmoid(_mm(h, wg_ref[...]))
        pb = p_ref[...].astype(BF16)
        pp = _mm(pb, wp_ref[...])
        err = xv + pg * pp - t_ref[...]
        loss_ref[...] += 0.5 * jnp.sum(jnp.mean(err * err, axis=-1, keepdims=True))
        dx3 = err * (1.0 / D_MODEL)
        d_pp = (dx3 * pg).astype(BF16)
        d_pre = (dx3 * pp * pg * (1.0 - pg)).astype(BF16)
        dwp_ref[...] += _mm_tn(pb, d_pp)
        dwg_ref[...] += _mm_tn(h, d_pre)
        dh = _mm_nt(d_pre, wg_ref[...])
        dg_ref[...] += jnp.sum(dh * xhat, axis=0, keepdims=True)
        dx2 = dx3 + _rms_bwd(dh, xhat, r, g)
        dx_ref[...] = dx2
        dxb_ref[...] = dx2.astype(BF16)

    def blk(c):
        return pl.BlockSpec((tm, c), lambda i: (i, 0))

    return pl.pallas_call(
        body, name="ple_fwd_bwd", grid=(t // tm,),
        in_specs=[blk(D_MODEL), blk(PLE_DIM), blk(D_MODEL), _full((1, D_MODEL)), _full((D_MODEL, D_MODEL)),
                  _full((PLE_DIM, D_MODEL))],
        out_specs=[blk(D_MODEL), blk(D_MODEL), _full((8, 128)), _full((D_MODEL, D_MODEL)),
                   _full((PLE_DIM, D_MODEL)), _full((1, D_MODEL))],
        out_shape=[jax.ShapeDtypeStruct((t, D_MODEL), F32), jax.ShapeDtypeStruct((t, D_MODEL), BF16),
                   jax.ShapeDtypeStruct((8, 128), F32),
                   jax.ShapeDtypeStruct((D_MODEL, D_MODEL), F32), jax.ShapeDtypeStruct((PLE_DIM, D_MODEL), F32),
                   jax.ShapeDtypeStruct((1, D_MODEL), F32)],
        compiler_params=_cparams("arbitrary"),
    )(x2, p, target, g_ple, w_pg, w_pp)


def _ffn_bwd(dx2, h2, gp, up, w_gate, w_up, w_down, fcw, fcb, tm):
    t = dx2.shape[0]
    nblk = t // tm
    fc = D_FF // FF_CHUNKS
    half = tm // FFN_BWD_PARTS

    def body(dx_ref, h_ref, gp_ref, gph_ref, up_ref, wg_ref, wu_ref, wd_ref, cw_ref, cb_ref,
             dh_ref, dwd_hbm, dwu_hbm, dwg_hbm, dcw_ref, dcb_ref, carry_ref, a_scr, dup_scr, dgp_scr,
             dwd_acc, dwu_acc, dwg_acc, stage, stage_sem):
        i = pl.program_id(1)

        @pl.when(i == 0)
        def _():
            carry_ref[...] = jnp.zeros_like(carry_ref)
            dwd_acc[...] = jnp.zeros_like(dwd_acc)
            dwu_acc[...] = jnp.zeros_like(dwu_acc)
            dwg_acc[...] = jnp.zeros_like(dwg_acc)
            dcw_ref[...] = jnp.zeros_like(dcw_ref)
            dcb_ref[...] = jnp.zeros_like(dcb_ref)

        keep = (i < nblk - 1).astype(F32)
        later = carry_ref[...]
        for hf in reversed(range(FFN_BWD_PARTS)):
            rows = slice(hf * half, (hf + 1) * half)
            dxb = dx_ref[rows, :]
            gp_v = gp_ref[rows, :].astype(F32)
            if hf > 0:
                before = gp_ref[hf * half - 16:hf * half, :].astype(F32)
            else:
                before = gph_ref[...].astype(F32) * keep
            gate, gp1, gp2 = _conv_fwd(gp_v, before[15:16, :], before[14:15, :], cw_ref, cb_ref)
            s = _sigmoid(gate)
            silu = gate * s
            up_v = up_ref[rows, :].astype(F32)
            da = _mm_nt(dxb, wd_ref[...])
            a_scr[rows, :] = (silu * up_v).astype(BF16)
            d_up = (da * silu).astype(BF16)
            dup_scr[rows, :] = d_up
            d_gate = da * up_v * (s * (1.0 + gate * (1.0 - s)))
            d_gp = _conv_bwd_input(d_gate, later[0:1, :], later[1:2, :], cw_ref).astype(BF16)
            dgp_scr[rows, :] = d_gp
            later = d_gate[0:8, :]
            dcw_ref[0:1, :] += jnp.sum(d_gate * gp2, axis=0, keepdims=True)
            dcw_ref[1:2, :] += jnp.sum(d_gate * gp1, axis=0, keepdims=True)
            dcw_ref[2:3, :] += jnp.sum(d_gate * gp_v, axis=0, keepdims=True)
            dcb_ref[...] += jnp.sum(d_gate, axis=0, keepdims=True)
            dh_ref[rows, :] = (_mm(d_gp, wg_ref[...]) + _mm(d_up, wu_ref[...])).astype(BF16)
        carry_ref[...] = later
        dwd_acc[...] += _mm_tn(a_scr[...], dx_ref[...])
        dwu_acc[...] += _mm_tn(h_ref[...], dup_scr[...])
        dwg_acc[...] += _mm_tn(h_ref[...], dgp_scr[...])

        @pl.when(i == nblk - 1)
        def _():
            rows = pl.ds(pl.multiple_of(pl.program_id(0) * fc, 16), fc)
            for acc, out, flip in ((dwd_acc, dwd_hbm, False), (dwu_acc, dwu_hbm, True), (dwg_acc, dwg_hbm, True)):
                stage[...] = (acc[...].T if flip else acc[...]).astype(BF16)
                copy = pltpu.make_async_copy(stage, out.at[rows, :], stage_sem)
                copy.start()
                copy.wait()

    def rev(i):
        return nblk - 1 - i

    one = pl.Buffered(1)
    in_specs = [
        pl.BlockSpec((tm, D_MODEL), lambda j, i: (rev(i), 0)),
        pl.BlockSpec((tm, D_MODEL), lambda j, i: (rev(i), 0)),
        pl.BlockSpec((tm, fc), lambda j, i: (rev(i), j)),
        pl.BlockSpec((16, fc), lambda j, i: (jnp.maximum(rev(i) * (tm // 16) - 1, 0), j)),
        pl.BlockSpec((tm, fc), lambda j, i: (rev(i), j)),
        pl.BlockSpec((fc, D_MODEL), lambda j, i: (j, 0), pipeline_mode=one),
        pl.BlockSpec((fc, D_MODEL), lambda j, i: (j, 0), pipeline_mode=one),
        pl.BlockSpec((fc, D_MODEL), lambda j, i: (j, 0), pipeline_mode=one),
        pl.BlockSpec((3, fc), lambda j, i: (0, j)),
        pl.BlockSpec((1, fc), lambda j, i: (0, j)),
    ]
    out_specs = [
        pl.BlockSpec((None, tm, D_MODEL), lambda j, i: (j, rev(i), 0)),
        ANY, ANY, ANY,
        pl.BlockSpec((3, fc), lambda j, i: (0, j)),
        pl.BlockSpec((1, fc), lambda j, i: (0, j)),
    ]
    return pl.pallas_call(
        body, name="ffn_bwd", grid=(FF_CHUNKS, nblk), in_specs=in_specs, out_specs=out_specs,
        out_shape=[jax.ShapeDtypeStruct((FF_CHUNKS, t, D_MODEL), BF16), jax.ShapeDtypeStruct((D_FF, D_MODEL), BF16),
                   jax.ShapeDtypeStruct((D_FF, D_MODEL), BF16), jax.ShapeDtypeStruct((D_FF, D_MODEL), BF16),
                   jax.ShapeDtypeStruct((3, D_FF), F32), jax.ShapeDtypeStruct((1, D_FF), F32)],
        scratch_shapes=[pltpu.VMEM((8, fc), F32), pltpu.VMEM((tm, fc), BF16), pltpu.VMEM((tm, fc), BF16),
                        pltpu.VMEM((tm, fc), BF16), pltpu.VMEM((fc, D_MODEL), F32), pltpu.VMEM((D_MODEL, fc), F32),
                        pltpu.VMEM((D_MODEL, fc), F32), pltpu.VMEM((fc, D_MODEL), BF16), pltpu.SemaphoreType.DMA],
        compiler_params=_cparams("arbitrary", "arbitrary", vmem=V7X_VMEM_LIMIT_LARGE),
    )(dx2, h2, gp, gp, up, w_gate, w_up, w_down, fcw, fcb)


def _outproj_bwd(dh2, dx2, x1, g_ffn, w_out, yc, ya, goc, goa, zconv, conv_w, conv_b, bd, tm):
    t = x1.shape[0]
    nblk = t // tm

    def body(dh_ref, dx2_ref, x1_ref, g_ref, w_ref, yc_ref, ya_ref, goc_ref, goa_ref, zc_ref, zch_ref, cw_ref, cb_ref,
             bd_ref, dx1_ref, dya_ref, dd_ref, dzc_ref, dw_ref, dg_ref, dgoc_ref, dgoa_ref, dcw_ref, dcb_ref,
             carry_ref):
        i = pl.program_id(0)

        @pl.when(i == 0)
        def _():
            carry_ref[...] = jnp.zeros_like(carry_ref)
            for ref in (dw_ref, dg_ref, dgoc_ref, dgoa_ref, dcw_ref, dcb_ref):
                ref[...] = jnp.zeros_like(ref)

        keep = (i < nblk - 1).astype(F32)
        dh2_v = dh_ref[0].astype(F32)
        for j in range(1, FF_CHUNKS):
            dh2_v = dh2_v + dh_ref[j].astype(F32)
        r, xhat = _rms_stats(x1_ref[...])
        dg_ref[...] += jnp.sum(dh2_v * xhat, axis=0, keepdims=True)
        dx1 = dx2_ref[...] + _rms_bwd(dh2_v, xhat, r, g_ref[...])
        dx1_ref[...] = dx1
        dx1b = dx1.astype(BF16)
        dy = _mm_nt(dx1b, w_ref[...])

        yc_v = yc_ref[...].astype(F32)
        rc, ychat = _rms_stats(yc_v)
        dw_ref[0:CONV_W, :] += _mm_tn((ychat * goc_ref[...]).astype(BF16), dx1b)
        dyc = dy[:, 0:CONV_W]
        dgoc_ref[...] += jnp.sum(dyc * ychat, axis=0, keepdims=True)
        d_yc = _rms_bwd(dyc, ychat, rc, goc_ref[...])

        ya_v = ya_ref[...].astype(F32)
        ra, yahat = _rms_stats(ya_v)
        dw_ref[CONV_W:, :] += _mm_tn((yahat * goa_ref[...]).astype(BF16), dx1b)
        dya = dy[:, CONV_W:]
        dgoa_ref[...] += jnp.sum(dya * yahat, axis=0, keepdims=True)
        d_ya = _rms_bwd(dya, yahat, ra, goa_ref[...])
        dya_ref[...] = d_ya
        dd_ref[...] = _seg_sum64(d_ya * ya_v, bd_ref)

        zb = zc_ref[:, 0:CONV_W].astype(F32)
        zc = zc_ref[:, CONV_W:2 * CONV_W].astype(F32)
        zx = zc_ref[:, 2 * CONV_W:3 * CONV_W].astype(F32)
        u = zc * zx
        uh = (zch_ref[:, CONV_W:2 * CONV_W].astype(F32) * zch_ref[:, 2 * CONV_W:3 * CONV_W].astype(F32)) * keep
        cv, u1, u2 = _conv_fwd(u, uh[15:16, :], uh[14:15, :], cw_ref, cb_ref)
        d_cv = d_yc * zb
        d_u = _conv_bwd_input(d_cv, carry_ref[0:1, :], carry_ref[1:2, :], cw_ref)
        carry_ref[...] = d_cv[0:8, :]
        dcw_ref[0:1, :] += jnp.sum(d_cv * u2, axis=0, keepdims=True)
        dcw_ref[1:2, :] += jnp.sum(d_cv * u1, axis=0, keepdims=True)
        dcw_ref[2:3, :] += jnp.sum(d_cv * u, axis=0, keepdims=True)
        dcb_ref[...] += jnp.sum(d_cv, axis=0, keepdims=True)
        dzc_ref[:, 0:CONV_W] = (d_yc * cv).astype(BF16)
        dzc_ref[:, CONV_W:2 * CONV_W] = (d_u * zx).astype(BF16)
        dzc_ref[:, 2 * CONV_W:3 * CONV_W] = (d_u * zc).astype(BF16)

    def rev(i):
        return nblk - 1 - i

    def blk(c):
        return pl.BlockSpec((tm, c), lambda i: (rev(i), 0))

    in_specs = [
        pl.BlockSpec((FF_CHUNKS, tm, D_MODEL), lambda i: (0, rev(i), 0)),
        blk(D_MODEL), blk(D_MODEL), _full((1, D_MODEL)), _full((D_MODEL, D_MODEL)),
        blk(CONV_W), blk(ATTN_W), _full((1, CONV_W)), _full((1, ATTN_W)),
        blk(3 * CONV_W),
        pl.BlockSpec((16, 3 * CONV_W), lambda i: (jnp.maximum(rev(i) * (tm // 16) - 1, 0), 0)),
        _full((3, CONV_W)), _full((1, CONV_W)), _full((256, 256)),
    ]
    out_specs = [blk(D_MODEL), blk(ATTN_W), blk(ATTN_W), blk(3 * CONV_W), _full((D_MODEL, D_MODEL)),
                 _full((1, D_MODEL)), _full((1, CONV_W)), _full((1, ATTN_W)), _full((3, CONV_W)), _full((1, CONV_W))]
    return pl.pallas_call(
        body, name="outproj_bwd", grid=(nblk,), in_specs=in_specs, out_specs=out_specs,
        out_shape=[jax.ShapeDtypeStruct((t, D_MODEL), F32), jax.ShapeDtypeStruct((t, ATTN_W), F32),
                   jax.ShapeDtypeStruct((t, ATTN_W), F32), jax.ShapeDtypeStruct((t, 3 * CONV_W), BF16),
                   jax.ShapeDtypeStruct((D_MODEL, D_MODEL), F32), jax.ShapeDtypeStruct((1, D_MODEL), F32),
                   jax.ShapeDtypeStruct((1, CONV_W), F32), jax.ShapeDtypeStruct((1, ATTN_W), F32),
                   jax.ShapeDtypeStruct((3, CONV_W), F32), jax.ShapeDtypeStruct((1, CONV_W), F32)],
        scratch_shapes=[pltpu.VMEM((8, CONV_W), F32)],
        compiler_params=_cparams("arbitrary"),
    )(dh2, dx2, x1, g_ffn, w_out, yc, ya, goc, goa, zconv, zconv, conv_w, conv_b, bd)


def _attn_bwd(q, k, v, dya, lse, dd, e_all, m_all, after):
    t = q.shape[0]
    nsb = t // SUPER

    def body(q_ref, kc_ref, kp_ref, vc_ref, vp_ref, dy_ref, l_ref, d_ref, e_ref, m_ref, after_ref,
             dq_ref, dk_ref, dv_ref, kk, vv, dkacc, dvacc, dwide):
        s = pl.program_id(1)

        @pl.when(s == 0)
        def _():
            dkacc[...] = jnp.zeros_like(dkacc)
            dvacc[...] = jnp.zeros_like(dvacc)

        dkacc[0:SUPER, :] = dkacc[SUPER:, :]
        dvacc[0:SUPER, :] = dvacc[SUPER:, :]
        dkacc[SUPER:, :] = jnp.zeros((SUPER, QK_BLOCK), F32)
        dvacc[SUPER:, :] = jnp.zeros((SUPER, QK_BLOCK), F32)

        @pl.when(s < nsb)
        def _():
            kk[0:SUPER, :] = kp_ref[...]
            kk[SUPER:, :] = kc_ref[...]
            vv[0:SUPER, :] = vp_ref[...]
            vv[SUPER:, :] = vc_ref[...]
            head0 = lax.broadcasted_iota(jnp.int32, (QK_BLOCK, QK_BLOCK), 1) < HEAD_DIM

            def widened(a):
                other = pltpu.roll(a, HEAD_DIM, 1)
                first = lax.broadcasted_iota(jnp.int32, a.shape, 1) < HEAD_DIM
                return jnp.where(first, a, other), jnp.where(first, other, a)

            def stacked(h0, h1):
                return jnp.concatenate([jnp.concatenate([h0, h0], axis=1), jnp.concatenate([h1, h1], axis=1)], axis=0)

            def widen_dd(i, carry):
                rows = pl.ds(pl.multiple_of(i * 256, 256), 256)
                dwide[0, rows, :], dwide[1, rows, :] = widened(d_ref[rows, :])
                return carry

            lax.fori_loop(0, SUPER // 256, widen_dd, 0)

            for b, dil in enumerate(DILATIONS):
                def unit(u, carry, b=b, dil=dil):
                    start = _unit_start(u, dil)
                    first_key = SUPER + start - QK_BLOCK * dil
                    qrows = _rows(start, QK_BLOCK, dil)
                    krows = _rows(first_key, KEYS, dil)
                    q2 = _stack_heads(q_ref[qrows, :].astype(BF16), head0)
                    dy2 = _stack_heads(dy_ref[qrows, :].astype(BF16), head0)
                    g2 = stacked(*widened(jnp.exp(m_ref[b, qrows, :] - l_ref[qrows, :])))
                    d2 = stacked(dwide[0, qrows, :], dwide[1, qrows, :])
                    k2 = kk[krows, :].astype(BF16)
                    v2 = vv[krows, :].astype(BF16)
                    prob = e_ref[b * UNITS + u].astype(F32) * g2
                    ds = (prob * (_mm_nt(dy2, v2) - d2)).astype(BF16)
                    dvacc[krows, :] += _mm_tn(prob.astype(BF16), dy2)
                    dkacc[krows, :] += _mm_tn(ds, q2)
                    dq2 = _mm(ds, k2)
                    dq = jnp.where(head0, dq2[0:QK_BLOCK], dq2[QK_BLOCK:]) * ATTN_SCALE
                    if b == 0:
                        dq_ref[qrows, :] = dq
                    else:
                        dq_ref[qrows, :] += dq
                    return carry

                lax.fori_loop(0, UNITS, unit, 0, unroll=8)

        dk_ref[...] = dkacc[0:SUPER, :]
        dv_ref[...] = dvacc[0:SUPER, :].astype(BF16)

    def cur_map(p, s):
        return (jnp.minimum(s, nsb - 1), p)

    def prev_map(p, s):
        return (jnp.clip(s - 1, 0, nsb - 1), p)

    cur = pl.BlockSpec((SUPER, QK_BLOCK), cur_map)
    prev = pl.BlockSpec((SUPER, QK_BLOCK), prev_map)
    return pl.pallas_call(
        body, name="attn_bwd", grid=(4, nsb + 1),
        in_specs=[cur, cur, prev, cur, prev, cur, cur, cur,
                  pl.BlockSpec((None, None, 3 * UNITS, KEYS, KEYS), lambda p, s: (p, jnp.minimum(s, nsb - 1), 0, 0, 0)),
                  pl.BlockSpec((3, SUPER, QK_BLOCK), lambda p, s: (0, jnp.minimum(s, nsb - 1), p)),
                  pl.BlockSpec(memory_space=pl.ANY)],
        out_specs=[cur, prev, prev],
        out_shape=[jax.ShapeDtypeStruct((t, ATTN_W), F32), jax.ShapeDtypeStruct((t, ATTN_W), F32),
                   jax.ShapeDtypeStruct((t, ATTN_W), BF16)],
        scratch_shapes=[pltpu.VMEM((2 * SUPER, QK_BLOCK), F32)] * 4 + [pltpu.VMEM((2, SUPER, QK_BLOCK), F32)],
        compiler_params=_cparams("parallel", "arbitrary"),
    )(q, k, k, v, v, dya, lse, dd, e_all, m_all, after)


def _inproj_bwd(dq, dk, dv, dzconv, zqk, x, dx1, g_mix, w_in, qg, kg, bd, tm):
    t = x.shape[0]
    nblk = t // tm
    shard = IN_COLS // N_DEV

    def body(dq_ref, dk_ref, dv_ref, dzc_ref, zqk_ref, x_ref, dx1_ref, g_ref, w_ref, qg_ref,
             kg_ref, bd_ref, dx_ref, dw_hbm, dg_ref, dqg_ref, dkg_ref, dw_ref, stage, stage_sem):
        @pl.when(pl.program_id(0) == 0)
        def _():
            for ref in (dw_ref, dg_ref, dqg_ref, dkg_ref):
                ref[...] = jnp.zeros_like(ref)

        parts = [dzc_ref[...]]
        for j, (dn_ref, gain_ref, dgain_ref) in enumerate(((dq_ref, qg_ref, dqg_ref), (dk_ref, kg_ref, dkg_ref))):
            dn = dn_ref[...]
            z = zqk_ref[:, j * ATTN_W:(j + 1) * ATTN_W].astype(F32)
            r = lax.rsqrt(_seg_sum64(z * z, bd_ref) * (1.0 / HEAD_DIM) + EPS)
            zhat = z * r
            dgain_ref[...] += jnp.sum(dn * zhat, axis=0, keepdims=True)
            gd = dn * gain_ref[...]
            parts.append((r * (gd - zhat * (_seg_sum64(gd * zhat, bd_ref) * (1.0 / HEAD_DIM)))).astype(BF16))
        parts.append(dv_ref[...].astype(BF16))
        dz = jnp.concatenate(parts, axis=1)

        r, xhat = _rms_stats(x_ref[...])
        g = g_ref[...]
        dw_ref[...] += _mm_tn((xhat * g).astype(BF16), dz)
        dh = _mm_nt(dz, w_ref[...])
        dg_ref[...] += jnp.sum(dh * xhat, axis=0, keepdims=True)
        dx_ref[...] = dx1_ref[...] + _rms_bwd(dh, xhat, r, g)

        @pl.when(pl.program_id(0) == nblk - 1)
        def _():
            for k in range(N_DEV):
                stage[...] = dw_ref[:, k * shard:(k + 1) * shard].astype(BF16)
                copy = pltpu.make_async_copy(stage, dw_hbm.at[k], stage_sem)
                copy.start()
                copy.wait()

    def blk(c):
        return pl.BlockSpec((tm, c), lambda i: (i, 0))

    return pl.pallas_call(
        body, name="inproj_bwd", grid=(nblk,),
        in_specs=[blk(ATTN_W)] * 3 + [blk(3 * CONV_W), blk(2 * ATTN_W), blk(D_MODEL), blk(D_MODEL), _full((1, D_MODEL)),
                                      _full((D_MODEL, IN_COLS)), _full((1, ATTN_W)), _full((1, ATTN_W)),
                                      _full((256, 256))],
        out_specs=[blk(D_MODEL), ANY, _full((1, D_MODEL)), _full((1, ATTN_W)), _full((1, ATTN_W))],
        out_shape=[jax.ShapeDtypeStruct((t, D_MODEL), F32), jax.ShapeDtypeStruct((N_DEV, D_MODEL, shard), BF16),
                   jax.ShapeDtypeStruct((1, D_MODEL), F32), jax.ShapeDtypeStruct((1, ATTN_W), F32),
                   jax.ShapeDtypeStruct((1, ATTN_W), F32)],
        scratch_shapes=[pltpu.VMEM((D_MODEL, IN_COLS), F32), pltpu.VMEM((D_MODEL, shard), BF16),
                        pltpu.SemaphoreType.DMA],
        compiler_params=_cparams("arbitrary"),
    )(dq, dk, dv, dzconv, zqk, x, dx1, g_mix, w_in, qg, kg, bd)


def _ordered_after(a, token):
    return a if token is None else a + token[0:1, 0:1].reshape((1,) * a.ndim)


def _local_step(x, p, target, w, tms, hooks=None):
    hooks = hooks or {}
    bd = jnp.kron(jnp.eye(4, dtype=F32), jnp.ones((HEAD_DIM, HEAD_DIM), F32)).astype(BF16)
    qg = jnp.tile(w["q_norm_g"], (1, 8))
    kg = jnp.tile(w["k_norm_g"], (1, 8))
    slopes = jnp.exp2(-jnp.arange(1, 9, dtype=F32))
    slopes = jnp.broadcast_to(slopes.reshape(4, 2, 1), (4, 2, QK_BLOCK))

    zconv, zqk, yc, q, k, v = _inproj_fwd(x, w["g_mix"], w["w_in"], w["conv_w"], w["conv_b"], qg, kg, bd, tms[0])
    ya, lse, e_all, m_all = _attn_fwd(q, k, v, slopes)
    if "late_weights" in hooks:
        w = {**w, **hooks["late_weights"](lse)}
    x1 = _outproj_fwd(ya, yc, x, w["g_out_conv"], w["g_out_attn"], w["w_out"], tms[0])
    gp, up, h2, x2 = _ffn_fwd(x1, w["g_ffn"], w["w_gate"], w["w_up"], w["w_down"], w["ffn_conv_w"], w["ffn_conv_b"],
                              tms[1])
    dx2, dx2b, loss, dw_pg, dw_pp, dg_ple = _ple_fwd_bwd(x2, p, target, w["g_ple"], w["w_ple_gate"], w["w_ple_proj"], tms[0])
    dh2, dw_down, dw_up, dw_gate, dfcw, dfcb = _ffn_bwd(dx2b, h2, gp, up, w["w_gate"], w["w_up"], w["w_down"],
                                                        w["ffn_conv_w"], w["ffn_conv_b"], tms[0])
    token = None
    if "ffn_grads" in hooks:
        token = hooks["ffn_grads"]({"w_ple_gate": dw_pg, "w_ple_proj": dw_pp, "w_down": dw_down, "w_up": dw_up,
                                    "w_gate": dw_gate})
    dx1, dya, dd, dzconv, dw_out, dg_ffn, dgoc, dgoa, dcw, dcb = _outproj_bwd(
        dh2, dx2, x1, _ordered_after(w["g_ffn"], token), w["w_out"], yc, ya, w["g_out_conv"], w["g_out_attn"], zconv,
        w["conv_w"], w["conv_b"], bd, tms[1])
    token = hooks["outproj_done"](dx1) if "outproj_done" in hooks else None
    dq, dk, dv = _attn_bwd(q, k, v, dya, lse, dd, e_all, m_all, slopes if token is None else token)
    dx, dw_in, dg_mix, dqg, dkg = _inproj_bwd(dq, dk, dv, dzconv, zqk, x, dx1, w["g_mix"], w["w_in"], qg, kg, bd,
                                              tms[0])
    grads = {
        "g_mix": dg_mix, "w_in": dw_in, "conv_w": dcw, "conv_b": dcb,
        "q_norm_g": dqg.reshape(8, HEAD_DIM).sum(0, keepdims=True),
        "k_norm_g": dkg.reshape(8, HEAD_DIM).sum(0, keepdims=True),
        "g_out_conv": dgoc, "g_out_attn": dgoa, "w_out": dw_out, "g_ffn": dg_ffn, "w_gate": dw_gate, "w_up": dw_up,
        "ffn_conv_w": dfcw, "ffn_conv_b": dfcb, "w_down": dw_down, "g_ple": dg_ple, "w_ple_gate": dw_pg,
        "w_ple_proj": dw_pp,
    }
    return loss, dx, grads


ANY = pl.BlockSpec(memory_space=pl.ANY)
MESH = pl.DeviceIdType.MESH


def _all_gather(shards, name):
    n = len(shards)

    def body(*refs):
        ins, outs = refs[:n], refs[n:2 * n]
        send_sems, recv_sems, local_sems = refs[2 * n:]
        x, y, c = lax.axis_index("x"), lax.axis_index("y"), lax.axis_index("c")
        me, sibling = (x, y, c), (x, y, 1 - c)
        chips = [(1 - x, y), (x, 1 - y), (1 - x, 1 - y)]

        def slot(dev):
            return 4 * dev[0] + 2 * dev[1] + dev[2]

        def copy(b, k, block, to, src=None):
            dst = outs[b].at[slot(block)]
            return pltpu.make_async_remote_copy(
                src_ref=dst if src is None else src, dst_ref=dst, send_sem=send_sems.at[b, k],
                recv_sem=recv_sems.at[b, k], device_id=to, device_id_type=MESH)

        mine = [pltpu.make_async_copy(ins[b], outs[b].at[slot(me)], local_sems.at[b]) for b in range(n)]
        first, passed = [], []
        for b in range(n):
            mine[b].start()
            first.append(copy(b, 0, me, sibling, src=ins[b]))
            first += [copy(b, 1 + j, me, (*chip, c), src=ins[b]) for j, chip in enumerate(chips)]
        for cp in first:
            cp.start()
        for j, chip in enumerate(chips):
            for b in range(n):
                copy(b, 1 + j, (*chip, c), me).wait_recv()
                fwd = copy(b, 4 + j, (*chip, c), sibling)
                fwd.start()
                passed.append(fwd)
        for b in range(n):
            copy(b, 0, sibling, me).wait_recv()
            for j, chip in enumerate(chips):
                copy(b, 4 + j, (*chip, 1 - c), me).wait_recv()
        for cp in first + passed:
            cp.wait_send()
        for cp in mine:
            cp.wait()

    return pl.pallas_call(
        body, name=name,
        in_specs=[ANY] * n, out_specs=[ANY] * n,
        out_shape=[jax.ShapeDtypeStruct((N_DEV,) + s.shape, s.dtype) for s in shards],
        scratch_shapes=[pltpu.SemaphoreType.DMA((n, 7)), pltpu.SemaphoreType.DMA((n, 7)),
                        pltpu.SemaphoreType.DMA((n,))],
    )(*shards)


HBM = pl.BlockSpec(memory_space=pltpu.HBM)
SEM = pl.BlockSpec(memory_space=pltpu.SEMAPHORE)
EFFECT = pltpu.SideEffectType.DATAFLOW_SIDE_EFFECTING
FLIPS = ((0, 0, 1), (0, 1, 0), (0, 1, 1), (1, 0, 0), (1, 0, 1), (1, 1, 0), (1, 1, 1))


def _flip_peers():
    pos = (lax.axis_index("x"), lax.axis_index("y"), lax.axis_index("c"))
    return [tuple(1 - a if f else a for a, f in zip(pos, flip)) for flip in FLIPS]


def _hbm(a):
    return pltpu.with_memory_space_constraint(a, pltpu.HBM)


def _split_start(name, srcs, lands, plan, n_copies, after):
    n, m = len(srcs), len(lands)

    def body(*refs):
        send_sems, recv_sems, token = refs[n + m + 1], refs[n + m + 2], refs[-1]
        for i, (src, dst, peer) in enumerate(plan(refs[:n], refs[n:n + m])):
            pltpu.make_async_remote_copy(src_ref=src, dst_ref=dst, send_sem=send_sems.at[i], recv_sem=recv_sems.at[i],
                                         device_id=peer, device_id_type=MESH).start()
        token[...] = jnp.zeros_like(token)

    outs = pl.pallas_call(
        body, name=name + "_start",
        in_specs=[HBM] * (n + m) + [ANY],
        out_specs=[SEM, SEM] + [HBM] * (n + m) + [pl.BlockSpec(memory_space=pltpu.VMEM)],
        out_shape=[pltpu.SemaphoreType.DMA((n_copies,)), pltpu.SemaphoreType.DMA((n_copies,))]
        + [pltpu.HBM(a.shape, a.dtype) for a in list(srcs) + list(lands)] + [jax.ShapeDtypeStruct((8, 128), F32)],
        input_output_aliases={i: 2 + i for i in range(n + m)},
        compiler_params=pltpu.CompilerParams(has_side_effects=EFFECT),
    )(*[_hbm(a) for a in list(srcs) + list(lands)], after)
    return (outs[0], outs[1], outs[2:2 + n], outs[2 + n:2 + n + m]), outs[-1]


def _split_wait(name, started, plan, after):
    send_sems, recv_sems, srcs, lands = started
    n, m = len(srcs), len(lands)

    def body(*refs):
        send_ref, recv_ref = refs[n + m], refs[n + m + 1]
        for i, (src, dst, peer) in enumerate(plan(refs[:n], refs[n:n + m])):
            copy = pltpu.make_async_remote_copy(src_ref=src, dst_ref=dst, send_sem=send_ref.at[i],
                                                recv_sem=recv_ref.at[i], device_id=peer, device_id_type=MESH)
            copy.wait_send()
            copy.wait_recv()

    outs = pl.pallas_call(
        body, name=name + "_wait",
        in_specs=[HBM] * (n + m) + [SEM, SEM, ANY],
        out_specs=[HBM] * (n + m),
        out_shape=[pltpu.HBM(a.shape, a.dtype) for a in list(srcs) + list(lands)],
        input_output_aliases={i: i for i in range(n + m)},
        compiler_params=pltpu.CompilerParams(has_side_effects=EFFECT),
    )(*srcs, *lands, send_sems, recv_sems, after)
    return outs[:n], outs[n:]


def _gather_plan(srcs, lands):
    slot = 4 * lax.axis_index("x") + 2 * lax.axis_index("y") + lax.axis_index("c")
    return [(src, land.at[slot], peer) for src, land in zip(srcs, lands) for peer in _flip_peers()]


def _first_hop_plan(srcs, lands):
    x, y, c = lax.axis_index("x"), lax.axis_index("y"), lax.axis_index("c")
    peers = [(x, y, 1 - c), (1 - x, y, c), (x, 1 - y, c), (1 - x, 1 - y, c)]
    return [(src, land.at[4 * x + 2 * y + c], peer) for src, land in zip(srcs, lands) for peer in peers]


def _forward_to_sibling(shards, lands, name):
    n = len(lands)

    def body(*refs):
        shard_refs, land_refs = refs[:n], refs[2 * n:3 * n]
        send_sems, recv_sems, local_sems = refs[3 * n:]
        x, y, c = lax.axis_index("x"), lax.axis_index("y"), lax.axis_index("c")
        chips = [(1 - x, y), (x, 1 - y), (1 - x, 1 - y)]
        mine = [pltpu.make_async_copy(shard_refs[b], land_refs[b].at[4 * x + 2 * y + c], local_sems.at[b])
                for b in range(n)]
        for cp in mine:
            cp.start()
        copies = [pltpu.make_async_remote_copy(
            src_ref=land_refs[b].at[4 * cx + 2 * cy + c], dst_ref=land_refs[b].at[4 * cx + 2 * cy + c],
            send_sem=send_sems.at[b, j], recv_sem=recv_sems.at[b, j], device_id=(x, y, 1 - c), device_id_type=MESH)
            for b in range(n) for j, (cx, cy) in enumerate(chips)]
        for cp in copies:
            cp.start()
        for cp in copies:
            cp.wait()
        for cp in mine:
            cp.wait()

    return pl.pallas_call(
        body, name=name, in_specs=[ANY] * (2 * n), out_specs=[ANY] * n,
        out_shape=[jax.ShapeDtypeStruct(a.shape, a.dtype) for a in lands],
        input_output_aliases={n + b: b for b in range(n)},
        scratch_shapes=[pltpu.SemaphoreType.DMA((n, 3)), pltpu.SemaphoreType.DMA((n, 3)),
                        pltpu.SemaphoreType.DMA((n,))],
    )(*shards, *lands)


def _sibling_plan(srcs, lands):
    x, y, c = lax.axis_index("x"), lax.axis_index("y"), lax.axis_index("c")
    return [(src.at[k, 1 - c], land.at[k], (x, y, 1 - c)) for src, land in zip(srcs, lands) for k in range(N_CHIP)]


def _chip_plan(srcs, lands):
    x, y, c = lax.axis_index("x"), lax.axis_index("y"), lax.axis_index("c")
    return [(src.at[2 * cx + cy], land.at[2 * x + y], (cx, cy, c))
            for src, land in zip(srcs, lands) for cx, cy in ((1 - x, y), (x, 1 - y), (1 - x, 1 - y))]


def _row_tile(rows):
    for tr in range(min(rows, 512), 15, -16):
        if rows % tr == 0:
            return tr
    return rows


def _pair_sums(gs, lands, core, name):
    n = len(gs)

    def body(c_ref, *refs):
        for b in range(n):
            out = refs[2 * n + b]
            out[...] = (refs[b][...].astype(F32) + refs[n + b][...].astype(F32)).astype(out.dtype)

    def slab(a):
        return pl.BlockSpec((None,) + a.shape[1:], lambda k, c_ref: (k, 0, 0))

    return pl.pallas_call(
        body, name=name,
        grid_spec=pltpu.PrefetchScalarGridSpec(
            num_scalar_prefetch=1, grid=(N_CHIP,),
            in_specs=[pl.BlockSpec((None, None) + g.shape[2:], lambda k, c_ref: (k, c_ref[0], 0, 0)) for g in gs]
            + [slab(a) for a in lands],
            out_specs=[slab(a) for a in lands]),
        out_shape=[jax.ShapeDtypeStruct(a.shape, a.dtype) for a in lands],
        compiler_params=_cparams("parallel"),
    )(core, *gs, *lands)


def _adamw(own, arrived, chip, w, m, v, name):
    k, rows, cols = arrived.shape
    tr = _row_tile(rows)
    c1 = 1.0 / (1.0 - ADAM_B1 ** ADAM_STEP)
    c2 = 1.0 / (1.0 - ADAM_B2 ** ADAM_STEP)

    def body(chip_ref, o_ref, p_ref, w_ref, m_ref, v_ref, g_ref, d_ref, nm_ref, nv_ref):
        def slab(j):
            return jnp.where(chip_ref[0] == j, o_ref[j], p_ref[j]).astype(F32)

        g = slab(0)
        for j in range(1, k):
            g = g + slab(j)
        g_ref[...] = g
        nm = ADAM_B1 * m_ref[...] + (1.0 - ADAM_B1) * g
        nv = ADAM_B2 * v_ref[...] + (1.0 - ADAM_B2) * (g * g)
        nm_ref[...] = nm
        nv_ref[...] = nv
        d_ref[...] = -ADAM_LR * ((nm * c1) / (jnp.sqrt(nv * c2) + ADAM_EPS) + ADAM_WD * w_ref[...])

    blk = pl.BlockSpec((tr, cols), lambda i, c: (i, 0))
    stack = pl.BlockSpec((k, tr, cols), lambda i, c: (0, i, 0))
    return pl.pallas_call(
        body, name=name,
        grid_spec=pltpu.PrefetchScalarGridSpec(num_scalar_prefetch=1, grid=(rows // tr,),
                                               in_specs=[stack, stack, blk, blk, blk], out_specs=[blk] * 4),
        out_shape=[jax.ShapeDtypeStruct((rows, cols), F32)] * 4,
        compiler_params=_cparams("parallel"),
    )(chip, own, arrived, w, m, v)


SMALL_LAYOUT = (("g_mix", 0, 1024), ("conv_b", 1, 512), ("q_norm_g", 2, 64), ("k_norm_g", 3, 64),
                ("g_out_conv", 4, 512), ("g_out_attn", 5, 512), ("g_ffn", 6, 1024), ("ffn_conv_b", 7, 2816),
                ("g_ple", 10, 1024))
CONV_W_ROW = 11
FFN_CONV_W_ROW = 14
LOSS_ROW = 23


def _row_pieces(cols):
    return [(c, min(1024, cols - c)) for c in range(0, cols, 1024)]


def _pack_small(grads, loss_tile):
    names = [n for n, _, _ in SMALL_LAYOUT]

    def body(*refs):
        ins, cw_ref, fcw_ref, loss_ref, out_ref = refs[:len(names)], refs[-4], refs[-3], refs[-2], refs[-1]
        out_ref[...] = jnp.zeros_like(out_ref)
        for ref, (_, row, cols) in zip(ins, SMALL_LAYOUT):
            for j, (c, width) in enumerate(_row_pieces(cols)):
                out_ref[row + j:row + j + 1, 0:width] = ref[:, c:c + width]
        for k in range(3):
            out_ref[CONV_W_ROW + k:CONV_W_ROW + k + 1, 0:CONV_W] = cw_ref[k:k + 1, :]
            for j, (c, width) in enumerate(_row_pieces(D_FF)):
                row = FFN_CONV_W_ROW + 3 * k + j
                out_ref[row:row + 1, 0:width] = fcw_ref[k:k + 1, c:c + width]
        out_ref[LOSS_ROW:LOSS_ROW + 1, 0:128] = loss_ref[0:1, :]

    return pl.pallas_call(
        body, name="pack_small_grads", out_shape=jax.ShapeDtypeStruct((SMALL_ROWS, 1024), F32),
    )(*[grads[n] for n in names], grads["conv_w"], grads["ffn_conv_w"], loss_tile)


def _adamw_small(arrived, conv_parts, fconv_parts, wts, mom, var):
    names = [n for n, _, _ in SMALL_LAYOUT] + ["conv_w", "ffn_conv_w"]
    c1 = 1.0 / (1.0 - ADAM_B1 ** ADAM_STEP)
    c2 = 1.0 / (1.0 - ADAM_B2 ** ADAM_STEP)
    n = len(names)

    def body(*refs):
        land, cw_ref, fcw_ref = refs[0], refs[1], refs[2]
        state = refs[3:3 + 3 * n]
        outs = refs[3 + 3 * n:]

        def total(piece):
            acc = piece(0)
            for d in range(1, N_DEV):
                acc = acc + piece(d)
            return acc

        for i, name in enumerate(names):
            if name == "conv_w":
                g = total(lambda d: cw_ref[d])
            elif name == "ffn_conv_w":
                g = total(lambda d: fcw_ref[d])
            else:
                _, row, cols = SMALL_LAYOUT[i]
                pieces = [total(lambda d, j=j, width=width: land[d, row + j:row + j + 1, 0:width])
                          for j, (_, width) in enumerate(_row_pieces(cols))]
                g = pieces[0] if len(pieces) == 1 else jnp.concatenate(pieces, axis=1)
            w_ref, m_ref, v_ref = state[3 * i:3 * i + 3]
            nm = ADAM_B1 * m_ref[...] + (1.0 - ADAM_B1) * g
            nv = ADAM_B2 * v_ref[...] + (1.0 - ADAM_B2) * (g * g)
            outs[4 * i][...] = g
            outs[4 * i + 1][...] = -ADAM_LR * ((nm * c1) / (jnp.sqrt(nv * c2) + ADAM_EPS) + ADAM_WD * w_ref[...])
            outs[4 * i + 2][...] = nm
            outs[4 * i + 3][...] = nv
        outs[-1][...] = total(lambda d: land[d, LOSS_ROW:LOSS_ROW + 1, 0:128])

    state = [a[nm_] for nm_ in names for a in (wts, mom, var)]
    shapes = [jax.ShapeDtypeStruct(wts[nm_].shape, F32) for nm_ in names for _ in range(4)]
    outs = pl.pallas_call(
        body, name="adamw_small", out_shape=shapes + [jax.ShapeDtypeStruct((1, 128), F32)],
    )(arrived, conv_parts, fconv_parts, *state)
    return {nm_: tuple(outs[4 * i:4 * i + 4]) for i, nm_ in enumerate(names)}, outs[-1][0, 0]


COL_SHARDED = ("w_in", "w_ple_proj")
TRANSPOSED = ("w_gate", "w_up")
CONV_SHARDED = (("conv_w", CONV_W), ("ffn_conv_w", D_FF))


def _gathered_to_full(name, gathered):
    if name in COL_SHARDED:
        return gathered.transpose(1, 0, 2).reshape(gathered.shape[1], -1)
    return gathered.reshape(-1, gathered.shape[2])


def _full_to_stacked(name, grad, shard_shape):
    sr, sc = shard_shape
    if grad.ndim == 3:
        a = grad
    elif name in COL_SHARDED:
        a = grad.reshape(sr, N_DEV, sc).transpose(1, 0, 2)
    else:
        a = grad.reshape(N_DEV, sr, sc)
    return a.astype(BF16).reshape(N_CHIP, 2, sr, sc)


def _pad_rows(vec, rows):
    return jnp.pad(vec, (0, rows * 1024 - vec.shape[0])).reshape(rows, 1024)


def kernel(x, p, g_mix, w_in, conv_w, conv_b, q_norm_g, k_norm_g, g_out_conv, g_out_attn, w_out, g_ffn, w_gate, w_up, ffn_conv_w, ffn_conv_b, w_down, g_ple, w_ple_gate, w_ple_proj, loss_target, m_g_mix, m_w_in, m_conv_w, m_conv_b, m_q_norm_g, m_k_norm_g, m_g_out_conv, m_g_out_attn, m_w_out, m_g_ffn, m_w_gate, m_w_up, m_ffn_conv_w, m_ffn_conv_b, m_w_down, m_g_ple, m_w_ple_gate, m_w_ple_proj, v_g_mix, v_w_in, v_conv_w, v_conv_b, v_q_norm_g, v_k_norm_g, v_g_out_conv, v_g_out_attn, v_w_out, v_g_ffn, v_w_gate, v_w_up, v_ffn_conv_w, v_ffn_conv_b, v_w_down, v_g_ple, v_w_ple_gate, v_w_ple_proj):
    args = dict(locals())
    names = ["g_mix", "w_in", "conv_w", "conv_b", "q_norm_g", "k_norm_g", "g_out_conv", "g_out_attn", "w_out", "g_ffn",
             "w_gate", "w_up", "ffn_conv_w", "ffn_conv_b", "w_down", "g_ple", "w_ple_gate", "w_ple_proj"]
    big = list(BIG)
    conv = [n for n, _ in CONV_SHARDED]

    def local(prefix):
        out = {n: (args[prefix + n][0] if n in big or n in conv else args[prefix + n]) for n in names}
        out.update({n: out[n].T for n in TRANSPOSED})
        return out

    wts, mom, var = local(""), local("m_"), local("v_")
    shard_shapes = {n: wts[n].shape for n in big}
    dev = 4 * lax.axis_index("x") + 2 * lax.axis_index("y") + lax.axis_index("c")
    core = lax.axis_index("c").astype(jnp.int32).reshape(1)

    conv_local = _pad_rows(jnp.concatenate([wts[n].reshape(-1) for n in conv]), 8).reshape(8, 1024)
    late = [n for n in big if n != "w_in"]
    first = [wts["w_in"].astype(BF16), conv_local]
    first_hop, token = _split_start("gather_weights", first, [lax.empty((N_DEV,) + s.shape, s.dtype) for s in first],
                                    _first_hop_plan, 4 * len(first), wts["g_mix"])
    late_shards = [wts[n].astype(BF16) for n in late]
    gathering, token = _split_start("gather_late_weights", late_shards,
                                    [lax.empty((N_DEV,) + s.shape, BF16) for s in late_shards], _gather_plan,
                                    7 * len(late), token)
    w_in_all, conv_all = _forward_to_sibling(*_split_wait("gather_weights", first_hop, _first_hop_plan, token),
                                             "gather_weights_forward")
    full = dict(wts)
    full["w_in"] = _gathered_to_full("w_in", w_in_all)
    full["g_mix"] = _ordered_after(wts["g_mix"], token)
    flying = {}

    def late_weights(after):
        shards, lands = _split_wait("gather_late_weights", gathering, _gather_plan, after)
        return {n: _gathered_to_full(n, lax.dynamic_update_slice(land, shard[None], (dev, 0, 0)))
                for n, land, shard in zip(late, lands, shards)}

    early = ["w_ple_gate", "w_ple_proj", "w_down", "w_up", "w_gate"]

    def ffn_grads(g):
        stacked = [_full_to_stacked(n, g[n], shard_shapes[n]) for n in early]
        flying["sibling"], tok = _split_start("rs_sibling_early", stacked,
                                              [lax.empty((N_CHIP,) + s.shape[2:], BF16) for s in stacked],
                                              _sibling_plan, N_CHIP * len(early), g["w_down"])
        return tok

    def outproj_done(after):
        stacked, landed = _split_wait("rs_sibling_early", flying["sibling"], _sibling_plan, after)
        parts = _pair_sums(stacked, landed, core, "rs_pair_sums_early")
        flying["chip"], tok = _split_start("rs_chip_early", parts, [lax.empty(q.shape, BF16) for q in parts],
                                           _chip_plan, 3 * len(early), landed[0])
        return tok

    off = 0
    for n, width in CONV_SHARDED:
        sc = width // N_DEV
        a = conv_all.reshape(N_DEV, -1)[:, off:off + 3 * sc].reshape(N_DEV, 3, sc)
        full[n] = a.transpose(1, 0, 2).reshape(3, width)
        off += 3 * sc

    loss, dx, grads = _local_step(x[0], p[0, 0], loss_target[0], full, (512, 256),
                                  {"late_weights": late_weights, "ffn_grads": ffn_grads, "outproj_done": outproj_done})

    chip = (2 * lax.axis_index("x") + lax.axis_index("y")).astype(jnp.int32).reshape(1)

    def adamw_of(group, parts, arrived):
        return {n: _adamw(own, got, chip, wts[n], mom[n], var[n], f"adamw_{n}")
                for n, own, got in zip(group, parts, arrived)}

    last = [n for n in big if n not in early]
    stacked = [_full_to_stacked(n, grads[n], shard_shapes[n]) for n in last]
    flying["sibling_last"], tok = _split_start("rs_sibling_last", stacked,
                                               [lax.empty((N_CHIP,) + s.shape[2:], BF16) for s in stacked],
                                               _sibling_plan, N_CHIP * len(last), dx)
    packed = _pack_small(grads, loss)
    flying["small"], tok = _split_start("gather_small_grads", [packed], [lax.empty((N_DEV,) + packed.shape, F32)],
                                        _gather_plan, N_DEV - 1, tok)
    stacked, landed = _split_wait("rs_sibling_last", flying["sibling_last"], _sibling_plan, tok)
    parts = _pair_sums(stacked, landed, core, "rs_pair_sums_last")
    flying["chip_last"], tok = _split_start("rs_chip_last", parts, [lax.empty(q.shape, BF16) for q in parts],
                                            _chip_plan, 3 * len(last), landed[0])

    parts, arrived = _split_wait("rs_chip_early", flying["chip"], _chip_plan, tok)
    out = adamw_of(early, parts, arrived)
    (packed,), (small_all,) = _split_wait("gather_small_grads", flying["small"], _gather_plan, out[early[-1]][0])
    small_all = lax.dynamic_update_slice(small_all, packed[None], (dev, 0, 0))
    taps = small_all[:, CONV_W_ROW:CONV_W_ROW + 3, 0:CONV_W]
    ftaps = small_all[:, FFN_CONV_W_ROW:FFN_CONV_W_ROW + 9, :].reshape(N_DEV, 3, 3 * 1024)
    small_out, loss_total = _adamw_small(
        small_all, lax.dynamic_slice(taps, (0, 0, dev * (CONV_W // N_DEV)), (N_DEV, 3, CONV_W // N_DEV)),
        lax.dynamic_slice(ftaps, (0, 0, dev * (D_FF // N_DEV)), (N_DEV, 3, D_FF // N_DEV)), wts, mom, var)
    out.update(small_out)
    parts, arrived = _split_wait("rs_chip_last", flying["chip_last"], _chip_plan, small_out["g_mix"][0])
    out.update(adamw_of(last, parts, arrived))
    def result(n, which):
        a = out[n][which]
        return (a.T if n in TRANSPOSED else a).reshape(args[n].shape)

    return (loss_total, dx[None], *[result(n, which) for which in range(4) for n in names])
```

```python
import jax
import jax.numpy as jnp
from jax import lax
from jax.experimental import pallas as pl
from jax.experimental.pallas import tpu as pltpu

F32 = jnp.float32
BF16 = jnp.bfloat16

D_MODEL = 1024
CONV_W = 512
ATTN_W = 512
HEAD_DIM = 64
D_FF = 2816
PLE_DIM = 256
IN_COLS = 3 * CONV_W + 3 * ATTN_W
EPS = 1e-6
QK_BLOCK = 128
DILATIONS = (1, 4, 16)
ATTN_SCALE = HEAD_DIM ** -0.5

ADAM_LR = 0.001
ADAM_B1 = 0.9
ADAM_B2 = 0.999
ADAM_EPS = 1e-08
ADAM_WD = 0.01
ADAM_STEP = 10

N_DEV = 8
N_CHIP = 4
V7X_VMEM_LIMIT = 56 * 1024 * 1024
V7X_VMEM_LIMIT_LARGE = 62 * 1024 * 1024
FF_CHUNKS = 2
FFN_BWD_PARTS = 1

BIG = ("w_in", "w_out", "w_gate", "w_up", "w_down", "w_ple_gate", "w_ple_proj")
SMALL_ROWS = 24


def _cparams(*sem, vmem=V7X_VMEM_LIMIT):
    return pltpu.CompilerParams(dimension_semantics=sem, vmem_limit_bytes=vmem)


def _mm(a, b):
    return jnp.dot(a, b, preferred_element_type=F32)


def _mm_nt(a, b):
    return lax.dot_general(a, b, (((1,), (1,)), ((), ())), preferred_element_type=F32)


def _mm_tn(a, b):
    return lax.dot_general(a, b, (((0,), (0,)), ((), ())), preferred_element_type=F32)


def _full(shape):
    nd = len(shape)
    return pl.BlockSpec(shape, lambda *_: (0,) * nd)


def _rms_stats(x):
    r = lax.rsqrt(jnp.mean(x * x, axis=-1, keepdims=True) + EPS)
    return r, x * r


def _rms_bwd(dy, xhat, r, g):
    gd = dy * g
    return r * (gd - xhat * jnp.mean(gd * xhat, axis=-1, keepdims=True))


def _seg_sum64(v, bd_ref):
    outs = []
    for c in range(0, v.shape[1], 256):
        vc = v[:, c:c + 256]
        hi = vc.astype(BF16)
        lo = (vc - hi.astype(F32)).astype(BF16)
        outs.append(_mm(hi, bd_ref[...]) + _mm(lo, bd_ref[...]))
    return outs[0] if len(outs) == 1 else jnp.concatenate(outs, axis=1)


def _shift_rows(u, k, edge_rows):
    out = pltpu.roll(u, k, 0)
    row = lax.broadcasted_iota(jnp.int32, (8, u.shape[1]), 0)
    head = out[0:8]
    for j in range(k):
        head = jnp.where(row == j, edge_rows[k - 1 - j], head)
    return jnp.concatenate([head, out[8:]], axis=0)


def _shift_rows_up(u, k, edge_rows):
    n = u.shape[0]
    out = pltpu.roll(u, n - k, 0)
    row = lax.broadcasted_iota(jnp.int32, (8, u.shape[1]), 0)
    tail = out[n - 8:n]
    for j in range(k):
        tail = jnp.where(row == 8 - k + j, edge_rows[j], tail)
    return jnp.concatenate([out[0:n - 8], tail], axis=0)


def _conv_fwd(u, c1, c2, w_ref, b_ref):
    u1 = _shift_rows(u, 1, (c1,))
    u2 = _shift_rows(u, 2, (c1, c2))
    y = u2 * w_ref[0:1, :] + u1 * w_ref[1:2, :] + u * w_ref[2:3, :] + b_ref[...]
    return y, u1, u2


def _conv_bwd_input(dy, n1row, n2row, w_ref):
    d1 = _shift_rows_up(dy, 1, (n1row,))
    d2 = _shift_rows_up(dy, 2, (n1row, n2row))
    return dy * w_ref[2:3, :] + d1 * w_ref[1:2, :] + d2 * w_ref[0:1, :]


def _sigmoid(x):
    return 1.0 / (1.0 + jnp.exp(-x))


def _inproj_fwd(x, g_mix, w_in, conv_w, conv_b, qg, kg, bd, tm):
    t = x.shape[0]

    def body(x_ref, g_ref, w_ref, cw_ref, cb_ref, qg_ref, kg_ref, bd_ref,
             zc_ref, zqk_ref, yc_ref, q_ref, k_ref, v_ref, carry_ref):
        @pl.when(pl.program_id(0) == 0)
        def _():
            carry_ref[...] = jnp.zeros_like(carry_ref)

        _, xhat = _rms_stats(x_ref[...])
        h = (xhat * g_ref[...]).astype(BF16)
        zconv = _mm(h, w_ref[:, 0:3 * CONV_W])
        zc_ref[...] = zconv.astype(BF16)
        u = zconv[:, CONV_W:2 * CONV_W] * zconv[:, 2 * CONV_W:3 * CONV_W]
        cv, _, _ = _conv_fwd(u, carry_ref[7:8, :], carry_ref[6:7, :], cw_ref, cb_ref)
        yc_ref[...] = (zconv[:, 0:CONV_W] * cv).astype(BF16)
        carry_ref[...] = u[tm - 8:tm, :]

        zqk = _mm(h, w_ref[:, 3 * CONV_W:3 * CONV_W + 2 * ATTN_W])
        zqk_ref[...] = zqk.astype(BF16)
        for j, (gain_ref, out_ref, scale) in enumerate(((qg_ref, q_ref, ATTN_SCALE), (kg_ref, k_ref, 1.0))):
            z = zqk[:, j * ATTN_W:(j + 1) * ATTN_W]
            r = lax.rsqrt(_seg_sum64(z * z, bd_ref) * (1.0 / HEAD_DIM) + EPS)
            out_ref[...] = z * r * gain_ref[...] * scale
        v_ref[...] = _mm(h, w_ref[:, 3 * CONV_W + 2 * ATTN_W:IN_COLS])

    def blk(c):
        return pl.BlockSpec((tm, c), lambda i: (i, 0))

    return pl.pallas_call(
        body, name="inproj_fwd", grid=(t // tm,),
        in_specs=[blk(D_MODEL), _full((1, D_MODEL)), _full((D_MODEL, IN_COLS)), _full((3, CONV_W)),
                  _full((1, CONV_W)), _full((1, ATTN_W)), _full((1, ATTN_W)), _full((256, 256))],
        out_specs=[blk(3 * CONV_W), blk(2 * ATTN_W), blk(CONV_W), blk(ATTN_W), blk(ATTN_W), blk(ATTN_W)],
        out_shape=[jax.ShapeDtypeStruct((t, 3 * CONV_W), BF16), jax.ShapeDtypeStruct((t, 2 * ATTN_W), BF16),
                   jax.ShapeDtypeStruct((t, CONV_W), BF16), jax.ShapeDtypeStruct((t, ATTN_W), F32),
                   jax.ShapeDtypeStruct((t, ATTN_W), F32), jax.ShapeDtypeStruct((t, ATTN_W), F32)],
        scratch_shapes=[pltpu.VMEM((8, CONV_W), F32)],
        compiler_params=_cparams("arbitrary"),
    )(x, g_mix, w_in, conv_w, conv_b, qg, kg, bd)


SUPER = 16 * QK_BLOCK
KEYS = 2 * QK_BLOCK
UNITS = SUPER // QK_BLOCK


def _rows(start, size, dil):
    return pl.ds(start, size) if dil == 1 else pl.ds(start, size, stride=dil)


def _attn_bias(sl_ref, dil):
    qi = lax.broadcasted_iota(jnp.int32, (KEYS, KEYS), 0)
    kj = lax.broadcasted_iota(jnp.int32, (KEYS, KEYS), 1)
    step = jnp.bitwise_and(qi, QK_BLOCK - 1) + QK_BLOCK - kj
    slope = jnp.where(qi < QK_BLOCK, sl_ref[0, 0:1, 0:1], sl_ref[0, 1:2, 0:1])
    bias = jnp.where(jnp.logical_and(step >= 0, step <= QK_BLOCK), -slope * (step * dil).astype(F32), -jnp.inf)
    return bias, kj >= QK_BLOCK


def _unit_start(u, dil):
    if dil == 1:
        return pl.multiple_of(u * QK_BLOCK, QK_BLOCK)
    if dil == 4:
        return jnp.bitwise_and(u, 3) + (u // 4) * (4 * QK_BLOCK)
    return u


def _stack_heads(a, head0):
    zero = jnp.zeros_like(a)
    return jnp.concatenate([jnp.where(head0, a, zero), jnp.where(head0, zero, a)], axis=0)


def _attn_fwd(q, k, v, slopes):
    t = q.shape[0]
    nsb = t // SUPER

    def body(q_ref, kc_ref, kp_ref, vc_ref, vp_ref, sl_ref, o_ref, l_ref, e_ref, m_ref, kk, vv, ob, lb):
        s = pl.program_id(1)
        kk[0:SUPER, :] = kp_ref[...]
        kk[SUPER:, :] = kc_ref[...]
        vv[0:SUPER, :] = vp_ref[...]
        vv[SUPER:, :] = vc_ref[...]
        head0 = lax.broadcasted_iota(jnp.int32, (QK_BLOCK, QK_BLOCK), 1) < HEAD_DIM

        for b, dil in enumerate(DILATIONS):
            bias, own_half = _attn_bias(sl_ref, dil)

            def unit(u, carry, b=b, dil=dil, bias=bias, own_half=own_half):
                start = _unit_start(u, dil)
                first_key = SUPER + start - QK_BLOCK * dil
                q2 = _stack_heads(q_ref[_rows(start, QK_BLOCK, dil), :].astype(BF16), head0)
                k2 = kk[_rows(first_key, KEYS, dil), :].astype(BF16)
                v2 = vv[_rows(first_key, KEYS, dil), :].astype(BF16)
                has_prev = jnp.logical_or(s > 0, start >= QK_BLOCK * dil)
                sc = jnp.where(jnp.logical_or(own_half, has_prev), _mm_nt(q2, k2) + bias, -jnp.inf)
                m = jnp.max(sc, axis=-1, keepdims=True)
                e = jnp.exp(sc - m)
                den = jnp.sum(e, axis=-1, keepdims=True)
                eb = e.astype(BF16)
                e_ref[b * UNITS + u] = eb
                o2 = _mm(eb, v2) / den
                l2 = m + jnp.log(den)
                ob[b, _rows(start, QK_BLOCK, dil), :] = jnp.where(head0, o2[0:QK_BLOCK], o2[QK_BLOCK:])
                lb[b, _rows(start, QK_BLOCK, dil), :] = jnp.where(head0, l2[0:QK_BLOCK], l2[QK_BLOCK:])
                m_ref[b, _rows(start, QK_BLOCK, dil), :] = jnp.where(head0, m[0:QK_BLOCK], m[QK_BLOCK:])
                return carry

            lax.fori_loop(0, UNITS, unit, 0, unroll=16)

        def merge(i, carry):
            rows = pl.ds(pl.multiple_of(i * 256, 256), 256)
            la, lb_, lc = lb[0, rows, :], lb[1, rows, :], lb[2, rows, :]
            mx = jnp.maximum(jnp.maximum(la, lb_), lc)
            wa, wb, wc = jnp.exp(la - mx), jnp.exp(lb_ - mx), jnp.exp(lc - mx)
            sw = wa + wb + wc
            o_ref[rows, :] = ((wa * ob[0, rows, :] + wb * ob[1, rows, :] + wc * ob[2, rows, :]) / sw).astype(BF16)
            l_ref[rows, :] = mx + jnp.log(sw)
            return carry

        lax.fori_loop(0, SUPER // 256, merge, 0)

    cur = pl.BlockSpec((SUPER, QK_BLOCK), lambda p, s: (s, p))
    prev = pl.BlockSpec((SUPER, QK_BLOCK), lambda p, s: (jnp.maximum(s - 1, 0), p))
    return pl.pallas_call(
        body, name="attn_fwd", grid=(4, nsb),
        in_specs=[cur, cur, prev, cur, prev, pl.BlockSpec((1, 2, QK_BLOCK), lambda p, s: (p, 0, 0))],
        out_specs=[cur, cur, pl.BlockSpec((None, None, 3 * UNITS, KEYS, KEYS), lambda p, s: (p, s, 0, 0, 0)),
                   pl.BlockSpec((3, SUPER, QK_BLOCK), lambda p, s: (0, s, p))],
        out_shape=[jax.ShapeDtypeStruct((t, ATTN_W), BF16), jax.ShapeDtypeStruct((t, ATTN_W), F32),
                   jax.ShapeDtypeStruct((4, nsb, 3 * UNITS, KEYS, KEYS), BF16),
                   jax.ShapeDtypeStruct((3, t, ATTN_W), F32)],
        scratch_shapes=[pltpu.VMEM((2 * SUPER, QK_BLOCK), F32), pltpu.VMEM((2 * SUPER, QK_BLOCK), F32),
                        pltpu.VMEM((3, SUPER, QK_BLOCK), F32), pltpu.VMEM((3, SUPER, QK_BLOCK), F32)],
        compiler_params=_cparams("parallel", "arbitrary"),
    )(q, k, k, v, v, slopes)


def _outproj_fwd(ya, yc, x, goc, goa, w_out, tm):
    t = x.shape[0]

    def body(ya_ref, yc_ref, x_ref, goc_ref, goa_ref, w_ref, x1_ref):
        _, ychat = _rms_stats(yc_ref[...].astype(F32))
        _, yahat = _rms_stats(ya_ref[...].astype(F32))
        acc = _mm((ychat * goc_ref[...]).astype(BF16), w_ref[0:CONV_W, :])
        acc += _mm((yahat * goa_ref[...]).astype(BF16), w_ref[CONV_W:, :])
        x1_ref[...] = x_ref[...] + acc

    def blk(c):
        return pl.BlockSpec((tm, c), lambda i: (i, 0))

    return pl.pallas_call(
        body, name="outproj_fwd", grid=(t // tm,),
        in_specs=[blk(ATTN_W), blk(CONV_W), blk(D_MODEL), _full((1, CONV_W)), _full((1, ATTN_W)),
                  _full((D_MODEL, D_MODEL))],
        out_specs=blk(D_MODEL),
        out_shape=jax.ShapeDtypeStruct((t, D_MODEL), F32),
        compiler_params=_cparams("parallel"),
    )(ya, yc, x, goc, goa, w_out)


def _ffn_fwd(x1, g_ffn, w_gate_t, w_up_t, w_down, fcw, fcb, tm):
    t = x1.shape[0]

    def body(x_ref, g_ref, wg_ref, wu_ref, wd_ref, cw_ref, cb_ref, gp_ref, up_ref, h_ref, x2_ref, carry_ref):
        @pl.when(pl.program_id(0) == 0)
        def _():
            carry_ref[...] = jnp.zeros_like(carry_ref)

        xv = x_ref[...]
        _, xhat = _rms_stats(xv)
        h = (xhat * g_ref[...]).astype(BF16)
        h_ref[...] = h
        gp = _mm_nt(h, wg_ref[...])
        gp_ref[...] = gp.astype(BF16)
        gate, _, _ = _conv_fwd(gp, carry_ref[7:8, :], carry_ref[6:7, :], cw_ref, cb_ref)
        carry_ref[...] = gp[tm - 8:tm, :]
        up = _mm_nt(h, wu_ref[...])
        up_ref[...] = up.astype(BF16)
        a = (gate * _sigmoid(gate) * up).astype(BF16)
        x2_ref[...] = xv + _mm(a, wd_ref[...])

    def blk(c):
        return pl.BlockSpec((tm, c), lambda i: (i, 0))

    return pl.pallas_call(
        body, name="ffn_fwd", grid=(t // tm,),
        in_specs=[blk(D_MODEL), _full((1, D_MODEL)), _full((D_FF, D_MODEL)), _full((D_FF, D_MODEL)),
                  _full((D_FF, D_MODEL)), _full((3, D_FF)), _full((1, D_FF))],
        out_specs=[blk(D_FF), blk(D_FF), blk(D_MODEL), blk(D_MODEL)],
        out_shape=[jax.ShapeDtypeStruct((t, D_FF), BF16), jax.ShapeDtypeStruct((t, D_FF), BF16),
                   jax.ShapeDtypeStruct((t, D_MODEL), BF16), jax.ShapeDtypeStruct((t, D_MODEL), F32)],
        scratch_shapes=[pltpu.VMEM((8, D_FF), F32)],
        compiler_params=_cparams("arbitrary"),
    )(x1, g_ffn, w_gate_t, w_up_t, w_down, fcw, fcb)


def _ple_fwd_bwd(x2, p, target, g_ple, w_pg, w_pp, tm):
    t = x2.shape[0]

    def body(x_ref, p_ref, t_ref, g_ref, wg_ref, wp_ref, dx_ref, dxb_ref, loss_ref, dwg_ref, dwp_ref, dg_ref):
        @pl.when(pl.program_id(0) == 0)
        def _():
            loss_ref[...] = jnp.zeros_like(loss_ref)
            dwg_ref[...] = jnp.zeros_like(dwg_ref)
            dwp_ref[...] = jnp.zeros_like(dwp_ref)
            dg_ref[...] = jnp.zeros_like(dg_ref)

        xv = x_ref[...]
        r, xhat = _rms_stats(xv)
        g = g_ref[...]
        h = (xhat * g).astype(BF16)
        pg = _sigmoid(_mm(h, wg_ref[...]))
        pb = p_ref[...].astype(BF16)
        pp = _mm(pb, wp_ref[...])
        err = xv + pg * pp - t_ref[...]
        loss_ref[...] += 0.5 * jnp.sum(jnp.mean(err * err, axis=-1, keepdims=True))
        dx3 = err * (1.0 / D_MODEL)
        d_pp = (dx3 * pg).astype(BF16)
        d_pre = (dx3 * pp * pg * (1.0 - pg)).astype(BF16)
        dwp_ref[...] += _mm_tn(pb, d_pp)
        dwg_ref[...] += _mm_tn(h, d_pre)
        dh = _mm_nt(d_pre, wg_ref[...])
        dg_ref[...] += jnp.sum(dh * xhat, axis=0, keepdims=True)
        dx2 = dx3 + _rms_bwd(dh, xhat, r, g)
        dx_ref[...] = dx2
        dxb_ref[...] = dx2.astype(BF16)

    def blk(c):
        return pl.BlockSpec((tm, c), lambda i: (i, 0))

    return pl.pallas_call(
        body, name="ple_fwd_bwd", grid=(t // tm,),
        in_specs=[blk(D_MODEL), blk(PLE_DIM), blk(D_MODEL), _full((1, D_MODEL)), _full((D_MODEL, D_MODEL)),
                  _full((PLE_DIM, D_MODEL))],
        out_specs=[blk(D_MODEL), blk(D_MODEL), _full((8, 128)), _full((D_MODEL, D_MODEL)),
                   _full((PLE_DIM, D_MODEL)), _full((1, D_MODEL))],
        out_shape=[jax.ShapeDtypeStruct((t, D_MODEL), F32), jax.ShapeDtypeStruct((t, D_MODEL), BF16),
                   jax.ShapeDtypeStruct((8, 128), F32),
                   jax.ShapeDtypeStruct((D_MODEL, D_MODEL), F32), jax.ShapeDtypeStruct((PLE_DIM, D_MODEL), F32),
                   jax.ShapeDtypeStruct((1, D_MODEL), F32)],
        compiler_params=_cparams("arbitrary"),
    )(x2, p, target, g_ple, w_pg, w_pp)


def _ffn_bwd(dx2, h2, gp, up, w_gate, w_up, w_down, fcw, fcb, tm):
    t = dx2.shape[0]
    nblk = t // tm
    fc = D_FF // FF_CHUNKS
    half = tm // FFN_BWD_PARTS

    def body(dx_ref, h_ref, gp_ref, gph_ref, up_ref, wg_ref, wu_ref, wd_ref, cw_ref, cb_ref,
             dh_ref, dwd_hbm, dwu_hbm, dwg_hbm, dcw_ref, dcb_ref, carry_ref, a_scr, dup_scr, dgp_scr,
             dwd_acc, dwu_acc, dwg_acc, stage, stage_sem):
        i = pl.program_id(1)

        @pl.when(i == 0)
        def _():
            carry_ref[...] = jnp.zeros_like(carry_ref)
            dwd_acc[...] = jnp.zeros_like(dwd_acc)
            dwu_acc[...] = jnp.zeros_like(dwu_acc)
            dwg_acc[...] = jnp.zeros_like(dwg_acc)
            dcw_ref[...] = jnp.zeros_like(dcw_ref)
            dcb_ref[...] = jnp.zeros_like(dcb_ref)

        keep = (i < nblk - 1).astype(F32)
        later = carry_ref[...]
        for hf in reversed(range(FFN_BWD_PARTS)):
            rows = slice(hf * half, (hf + 1) * half)
            dxb = dx_ref[rows, :]
            gp_v = gp_ref[rows, :].astype(F32)
            if hf > 0:
                before = gp_ref[hf * half - 16:hf * half, :].astype(F32)
            else:
                before = gph_ref[...].astype(F32) * keep
            gate, gp1, gp2 = _conv_fwd(gp_v, before[15:16, :], before[14:15, :], cw_ref, cb_ref)
            s = _sigmoid(gate)
            silu = gate * s
            up_v = up_ref[rows, :].astype(F32)
            da = _mm_nt(dxb, wd_ref[...])
            a_scr[rows, :] = (silu * up_v).astype(BF16)
            d_up = (da * silu).astype(BF16)
            dup_scr[rows, :] = d_up
            d_gate = da * up_v * (s * (1.0 + gate * (1.0 - s)))
            d_gp = _conv_bwd_input(d_gate, later[0:1, :], later[1:2, :], cw_ref).astype(BF16)
            dgp_scr[rows, :] = d_gp
            later = d_gate[0:8, :]
            dcw_ref[0:1, :] += jnp.sum(d_gate * gp2, axis=0, keepdims=True)
            dcw_ref[1:2, :] += jnp.sum(d_gate * gp1, axis=0, keepdims=True)
            dcw_ref[2:3, :] += jnp.sum(d_gate * gp_v, axis=0, keepdims=True)
            dcb_ref[...] += jnp.sum(d_gate, axis=0, keepdims=True)
            dh_ref[rows, :] = (_mm(d_gp, wg_ref[...]) + _mm(d_up, wu_ref[...])).astype(BF16)
        carry_ref[...] = later
        dwd_acc[...] += _mm_tn(a_scr[...], dx_ref[...])
        dwu_acc[...] += _mm_tn(h_ref[...], dup_scr[...])
        dwg_acc[...] += _mm_tn(h_ref[...], dgp_scr[...])

        @pl.when(i == nblk - 1)
        def _():
            rows = pl.ds(pl.multiple_of(pl.program_id(0) * fc, 16), fc)
            for acc, out, flip in ((dwd_acc, dwd_hbm, False), (dwu_acc, dwu_hbm, True), (dwg_acc, dwg_hbm, True)):
                stage[...] = (acc[...].T if flip else acc[...]).astype(BF16)
                copy = pltpu.make_async_copy(stage, out.at[rows, :], stage_sem)
                copy.start()
                copy.wait()

    def rev(i):
        return nblk - 1 - i

    one = pl.Buffered(1)
    in_specs = [
        pl.BlockSpec((tm, D_MODEL), lambda j, i: (rev(i), 0)),
        pl.BlockSpec((tm, D_MODEL), lambda j, i: (rev(i), 0)),
        pl.BlockSpec((tm, fc), lambda j, i: (rev(i), j)),
        pl.BlockSpec((16, fc), lambda j, i: (jnp.maximum(rev(i) * (tm // 16) - 1, 0), j)),
        pl.BlockSpec((tm, fc), lambda j, i: (rev(i), j)),
        pl.BlockSpec((fc, D_MODEL), lambda j, i: (j, 0), pipeline_mode=one),
        pl.BlockSpec((fc, D_MODEL), lambda j, i: (j, 0), pipeline_mode=one),
        pl.BlockSpec((fc, D_MODEL), lambda j, i: (j, 0), pipeline_mode=one),
        pl.BlockSpec((3, fc), lambda j, i: (0, j)),
        pl.BlockSpec((1, fc), lambda j, i: (0, j)),
    ]
    out_specs = [
        pl.BlockSpec((None, tm, D_MODEL), lambda j, i: (j, rev(i), 0)),
        ANY, ANY, ANY,
        pl.BlockSpec((3, fc), lambda j, i: (0, j)),
        pl.BlockSpec((1, fc), lambda j, i: (0, j)),
    ]
    return pl.pallas_call(
        body, name="ffn_bwd", grid=(FF_CHUNKS, nblk), in_specs=in_specs, out_specs=out_specs,
        out_shape=[jax.ShapeDtypeStruct((FF_CHUNKS, t, D_MODEL), BF16), jax.ShapeDtypeStruct((D_FF, D_MODEL), BF16),
                   jax.ShapeDtypeStruct((D_FF, D_MODEL), BF16), jax.ShapeDtypeStruct((D_FF, D_MODEL), BF16),
                   jax.ShapeDtypeStruct((3, D_FF), F32), jax.ShapeDtypeStruct((1, D_FF), F32)],
        scratch_shapes=[pltpu.VMEM((8, fc), F32), pltpu.VMEM((tm, fc), BF16), pltpu.VMEM((tm, fc), BF16),
                        pltpu.VMEM((tm, fc), BF16), pltpu.VMEM((fc, D_MODEL), F32), pltpu.VMEM((D_MODEL, fc), F32),
                        pltpu.VMEM((D_MODEL, fc), F32), pltpu.VMEM((fc, D_MODEL), BF16), pltpu.SemaphoreType.DMA],
        compiler_params=_cparams("arbitrary", "arbitrary", vmem=V7X_VMEM_LIMIT_LARGE),
    )(dx2, h2, gp, gp, up, w_gate, w_up, w_down, fcw, fcb)


def _outproj_bwd(dh2, dx2, x1, g_ffn, w_out, yc, ya, goc, goa, zconv, conv_w, conv_b, bd, tm):
    t = x1.shape[0]
    nblk = t // tm

    def body(dh_ref, dx2_ref, x1_ref, g_ref, w_ref, yc_ref, ya_ref, goc_ref, goa_ref, zc_ref, zch_ref, cw_ref, cb_ref,
             bd_ref, dx1_ref, dya_ref, dd_ref, dzc_ref, dw_ref, dg_ref, dgoc_ref, dgoa_ref, dcw_ref, dcb_ref,
             carry_ref):
        i = pl.program_id(0)

        @pl.when(i == 0)
        def _():
            carry_ref[...] = jnp.zeros_like(carry_ref)
            for ref in (dw_ref, dg_ref, dgoc_ref, dgoa_ref, dcw_ref, dcb_ref):
                ref[...] = jnp.zeros_like(ref)

        keep = (i < nblk - 1).astype(F32)
        dh2_v = dh_ref[0].astype(F32)
        for j in range(1, FF_CHUNKS):
            dh2_v = dh2_v + dh_ref[j].astype(F32)
        r, xhat = _rms_stats(x1_ref[...])
        dg_ref[...] += jnp.sum(dh2_v * xhat, axis=0, keepdims=True)
        dx1 = dx2_ref[...] + _rms_bwd(dh2_v, xhat, r, g_ref[...])
        dx1_ref[...] = dx1
        dx1b = dx1.astype(BF16)
        dy = _mm_nt(dx1b, w_ref[...])

        yc_v = yc_ref[...].astype(F32)
        rc, ychat = _rms_stats(yc_v)
        dw_ref[0:CONV_W, :] += _mm_tn((ychat * goc_ref[...]).astype(BF16), dx1b)
        dyc = dy[:, 0:CONV_W]
        dgoc_ref[...] += jnp.sum(dyc * ychat, axis=0, keepdims=True)
        d_yc = _rms_bwd(dyc, ychat, rc, goc_ref[...])

        ya_v = ya_ref[...].astype(F32)
        ra, yahat = _rms_stats(ya_v)
        dw_ref[CONV_W:, :] += _mm_tn((yahat * goa_ref[...]).astype(BF16), dx1b)
        dya = dy[:, CONV_W:]
        dgoa_ref[...] += jnp.sum(dya * yahat, axis=0, keepdims=True)
        d_ya = _rms_bwd(dya, yahat, ra, goa_ref[...])
        dya_ref[...] = d_ya
        dd_ref[...] = _seg_sum64(d_ya * ya_v, bd_ref)

        zb = zc_ref[:, 0:CONV_W].astype(F32)
        zc = zc_ref[:, CONV_W:2 * CONV_W].astype(F32)
        zx = zc_ref[:, 2 * CONV_W:3 * CONV_W].astype(F32)
        u = zc * zx
        uh = (zch_ref[:, CONV_W:2 * CONV_W].astype(F32) * zch_ref[:, 2 * CONV_W:3 * CONV_W].astype(F32)) * keep
        cv, u1, u2 = _conv_fwd(u, uh[15:16, :], uh[14:15, :], cw_ref, cb_ref)
        d_cv = d_yc * zb
        d_u = _conv_bwd_input(d_cv, carry_ref[0:1, :], carry_ref[1:2, :], cw_ref)
        carry_ref[...] = d_cv[0:8, :]
        dcw_ref[0:1, :] += jnp.sum(d_cv * u2, axis=0, keepdims=True)
        dcw_ref[1:2, :] += jnp.sum(d_cv * u1, axis=0, keepdims=True)
        dcw_ref[2:3, :] += jnp.sum(d_cv * u, axis=0, keepdims=True)
        dcb_ref[...] += jnp.sum(d_cv, axis=0, keepdims=True)
        dzc_ref[:, 0:CONV_W] = (d_yc * cv).astype(BF16)
        dzc_ref[:, CONV_W:2 * CONV_W] = (d_u * zx).astype(BF16)
        dzc_ref[:, 2 * CONV_W:3 * CONV_W] = (d_u * zc).astype(BF16)

    def rev(i):
        return nblk - 1 - i

    def blk(c):
        return pl.BlockSpec((tm, c), lambda i: (rev(i), 0))

    in_specs = [
        pl.BlockSpec((FF_CHUNKS, tm, D_MODEL), lambda i: (0, rev(i), 0)),
        blk(D_MODEL), blk(D_MODEL), _full((1, D_MODEL)), _full((D_MODEL, D_MODEL)),
        blk(CONV_W), blk(ATTN_W), _full((1, CONV_W)), _full((1, ATTN_W)),
        blk(3 * CONV_W),
        pl.BlockSpec((16, 3 * CONV_W), lambda i: (jnp.maximum(rev(i) * (tm // 16) - 1, 0), 0)),
        _full((3, CONV_W)), _full((1, CONV_W)), _full((256, 256)),
    ]
    out_specs = [blk(D_MODEL), blk(ATTN_W), blk(ATTN_W), blk(3 * CONV_W), _full((D_MODEL, D_MODEL)),
                 _full((1, D_MODEL)), _full((1, CONV_W)), _full((1, ATTN_W)), _full((3, CONV_W)), _full((1, CONV_W))]
    return pl.pallas_call(
        body, name="outproj_bwd", grid=(nblk,), in_specs=in_specs, out_specs=out_specs,
        out_shape=[jax.ShapeDtypeStruct((t, D_MODEL), F32), jax.ShapeDtypeStruct((t, ATTN_W), F32),
                   jax.ShapeDtypeStruct((t, ATTN_W), F32), jax.ShapeDtypeStruct((t, 3 * CONV_W), BF16),
                   jax.ShapeDtypeStruct((D_MODEL, D_MODEL), F32), jax.ShapeDtypeStruct((1, D_MODEL), F32),
                   jax.ShapeDtypeStruct((1, CONV_W), F32), jax.ShapeDtypeStruct((1, ATTN_W), F32),
                   jax.ShapeDtypeStruct((3, CONV_W), F32), jax.ShapeDtypeStruct((1, CONV_W), F32)],
        scratch_shapes=[pltpu.VMEM((8, CONV_W), F32)],
        compiler_params=_cparams("arbitrary"),
    )(dh2, dx2, x1, g_ffn, w_out, yc, ya, goc, goa, zconv, zconv, conv_w, conv_b, bd)


def _attn_bwd(q, k, v, dya, lse, dd, e_all, m_all, after):
    t = q.shape[0]
    nsb = t // SUPER

    def body(q_ref, kc_ref, kp_ref, vc_ref, vp_ref, dy_ref, l_ref, d_ref, e_ref, m_ref, after_ref,
             dq_ref, dk_ref, dv_ref, kk, vv, dkacc, dvacc, dwide):
        s = pl.program_id(1)

        @pl.when(s == 0)
        def _():
            dkacc[...] = jnp.zeros_like(dkacc)
            dvacc[...] = jnp.zeros_like(dvacc)

        dkacc[0:SUPER, :] = dkacc[SUPER:, :]
        dvacc[0:SUPER, :] = dvacc[SUPER:, :]
        dkacc[SUPER:, :] = jnp.zeros((SUPER, QK_BLOCK), F32)
        dvacc[SUPER:, :] = jnp.zeros((SUPER, QK_BLOCK), F32)

        @pl.when(s < nsb)
        def _():
            kk[0:SUPER, :] = kp_ref[...]
            kk[SUPER:, :] = kc_ref[...]
            vv[0:SUPER, :] = vp_ref[...]
            vv[SUPER:, :] = vc_ref[...]
            head0 = lax.broadcasted_iota(jnp.int32, (QK_BLOCK, QK_BLOCK), 1) < HEAD_DIM

            def widened(a):
                other = pltpu.roll(a, HEAD_DIM, 1)
                first = lax.broadcasted_iota(jnp.int32, a.shape, 1) < HEAD_DIM
                return jnp.where(first, a, other), jnp.where(first, other, a)

            def stacked(h0, h1):
                return jnp.concatenate([jnp.concatenate([h0, h0], axis=1), jnp.concatenate([h1, h1], axis=1)], axis=0)

            def widen_dd(i, carry):
                rows = pl.ds(pl.multiple_of(i * 256, 256), 256)
                dwide[0, rows, :], dwide[1, rows, :] = widened(d_ref[rows, :])
                return carry

            lax.fori_loop(0, SUPER // 256, widen_dd, 0)

            for b, dil in enumerate(DILATIONS):
                def unit(u, carry, b=b, dil=dil):
                    start = _unit_start(u, dil)
                    first_key = SUPER + start - QK_BLOCK * dil
                    qrows = _rows(start, QK_BLOCK, dil)
                    krows = _rows(first_key, KEYS, dil)
                    q2 = _stack_heads(q_ref[qrows, :].astype(BF16), head0)
                    dy2 = _stack_heads(dy_ref[qrows, :].astype(BF16), head0)
                    g2 = stacked(*widened(jnp.exp(m_ref[b, qrows, :] - l_ref[qrows, :])))
                    d2 = stacked(dwide[0, qrows, :], dwide[1, qrows, :])
                    k2 = kk[krows, :].astype(BF16)
                    v2 = vv[krows, :].astype(BF16)
                    prob = e_ref[b * UNITS + u].astype(F32) * g2
                    ds = (prob * (_mm_nt(dy2, v2) - d2)).astype(BF16)
                    dvacc[krows, :] += _mm_tn(prob.astype(BF16), dy2)
                    dkacc[krows, :] += _mm_tn(ds, q2)
                    dq2 = _mm(ds, k2)
                    dq = jnp.where(head0, dq2[0:QK_BLOCK], dq2[QK_BLOCK:]) * ATTN_SCALE
                    if b == 0:
                        dq_ref[qrows, :] = dq
                    else:
                        dq_ref[qrows, :] += dq
                    return carry

                lax.fori_loop(0, UNITS, unit, 0, unroll=8)

        dk_ref[...] = dkacc[0:SUPER, :]
        dv_ref[...] = dvacc[0:SUPER, :].astype(BF16)

    def cur_map(p, s):
        return (jnp.minimum(s, nsb - 1), p)

    def prev_map(p, s):
        return (jnp.clip(s - 1, 0, nsb - 1), p)

    cur = pl.BlockSpec((SUPER, QK_BLOCK), cur_map)
    prev = pl.BlockSpec((SUPER, QK_BLOCK), prev_map)
    return pl.pallas_call(
        body, name="attn_bwd", grid=(4, nsb + 1),
        in_specs=[cur, cur, prev, cur, prev, cur, cur, cur,
                  pl.BlockSpec((None, None, 3 * UNITS, KEYS, KEYS), lambda p, s: (p, jnp.minimum(s, nsb - 1), 0, 0, 0)),
                  pl.BlockSpec((3, SUPER, QK_BLOCK), lambda p, s: (0, jnp.minimum(s, nsb - 1), p)),
                  pl.BlockSpec(memory_space=pl.ANY)],
        out_specs=[cur, prev, prev],
        out_shape=[jax.ShapeDtypeStruct((t, ATTN_W), F32), jax.ShapeDtypeStruct((t, ATTN_W), F32),
                   jax.ShapeDtypeStruct((t, ATTN_W), BF16)],
        scratch_shapes=[pltpu.VMEM((2 * SUPER, QK_BLOCK), F32)] * 4 + [pltpu.VMEM((2, SUPER, QK_BLOCK), F32)],
        compiler_params=_cparams("parallel", "arbitrary"),
    )(q, k, k, v, v, dya, lse, dd, e_all, m_all, after)


def _inproj_bwd(dq, dk, dv, dzconv, zqk, x, dx1, g_mix, w_in, qg, kg, bd, tm):
    t = x.shape[0]
    nblk = t // tm
    shard = IN_COLS // N_DEV

    def body(dq_ref, dk_ref, dv_ref, dzc_ref, zqk_ref, x_ref, dx1_ref, g_ref, w_ref, qg_ref,
             kg_ref, bd_ref, dx_ref, dw_hbm, dg_ref, dqg_ref, dkg_ref, dw_ref, stage, stage_sem):
        @pl.when(pl.program_id(0) == 0)
        def _():
            for ref in (dw_ref, dg_ref, dqg_ref, dkg_ref):
                ref[...] = jnp.zeros_like(ref)

        parts = [dzc_ref[...]]
        for j, (dn_ref, gain_ref, dgain_ref) in enumerate(((dq_ref, qg_ref, dqg_ref), (dk_ref, kg_ref, dkg_ref))):
            dn = dn_ref[...]
            z = zqk_ref[:, j * ATTN_W:(j + 1) * ATTN_W].astype(F32)
            r = lax.rsqrt(_seg_sum64(z * z, bd_ref) * (1.0 / HEAD_DIM) + EPS)
            zhat = z * r
            dgain_ref[...] += jnp.sum(dn * zhat, axis=0, keepdims=True)
            gd = dn * gain_ref[...]
            parts.append((r * (gd - zhat * (_seg_sum64(gd * zhat, bd_ref) * (1.0 / HEAD_DIM)))).astype(BF16))
        parts.append(dv_ref[...].astype(BF16))
        dz = jnp.concatenate(parts, axis=1)

        r, xhat = _rms_stats(x_ref[...])
        g = g_ref[...]
        dw_ref[...] += _mm_tn((xhat * g).astype(BF16), dz)
        dh = _mm_nt(dz, w_ref[...])
        dg_ref[...] += jnp.sum(dh * xhat, axis=0, keepdims=True)
        dx_ref[...] = dx1_ref[...] + _rms_bwd(dh, xhat, r, g)

        @pl.when(pl.program_id(0) == nblk - 1)
        def _():
            for k in range(N_DEV):
                stage[...] = dw_ref[:, k * shard:(k + 1) * shard].astype(BF16)
                copy = pltpu.make_async_copy(stage, dw_hbm.at[k], stage_sem)
                copy.start()
                copy.wait()

    def blk(c):
        return pl.BlockSpec((tm, c), lambda i: (i, 0))

    return pl.pallas_call(
        body, name="inproj_bwd", grid=(nblk,),
        in_specs=[blk(ATTN_W)] * 3 + [blk(3 * CONV_W), blk(2 * ATTN_W), blk(D_MODEL), blk(D_MODEL), _full((1, D_MODEL)),
                                      _full((D_MODEL, IN_COLS)), _full((1, ATTN_W)), _full((1, ATTN_W)),
                                      _full((256, 256))],
        out_specs=[blk(D_MODEL), ANY, _full((1, D_MODEL)), _full((1, ATTN_W)), _full((1, ATTN_W))],
        out_shape=[jax.ShapeDtypeStruct((t, D_MODEL), F32), jax.ShapeDtypeStruct((N_DEV, D_MODEL, shard), BF16),
                   jax.ShapeDtypeStruct((1, D_MODEL), F32), jax.ShapeDtypeStruct((1, ATTN_W), F32),
                   jax.ShapeDtypeStruct((1, ATTN_W), F32)],
        scratch_shapes=[pltpu.VMEM((D_MODEL, IN_COLS), F32), pltpu.VMEM((D_MODEL, shard), BF16),
                        pltpu.SemaphoreType.DMA],
        compiler_params=_cparams("arbitrary"),
    )(dq, dk, dv, dzconv, zqk, x, dx1, g_mix, w_in, qg, kg, bd)


def _ordered_after(a, token):
    return a if token is None else a + token[0:1, 0:1].reshape((1,) * a.ndim)


def _local_step(x, p, target, w, tms, hooks=None):
    hooks = hooks or {}
    bd = jnp.kron(jnp.eye(4, dtype=F32), jnp.ones((HEAD_DIM, HEAD_DIM), F32)).astype(BF16)
    qg = jnp.tile(w["q_norm_g"], (1, 8))
    kg = jnp.tile(w["k_norm_g"], (1, 8))
    slopes = jnp.exp2(-jnp.arange(1, 9, dtype=F32))
    slopes = jnp.broadcast_to(slopes.reshape(4, 2, 1), (4, 2, QK_BLOCK))

    zconv, zqk, yc, q, k, v = _inproj_fwd(x, w["g_mix"], w["w_in"], w["conv_w"], w["conv_b"], qg, kg, bd, tms[0])
    ya, lse, e_all, m_all = _attn_fwd(q, k, v, slopes)
    if "late_weights" in hooks:
        w = {**w, **hooks["late_weights"](lse)}
    x1 = _outproj_fwd(ya, yc, x, w["g_out_conv"], w["g_out_attn"], w["w_out"], tms[0])
    gp, up, h2, x2 = _ffn_fwd(x1, w["g_ffn"], w["w_gate"], w["w_up"], w["w_down"], w["ffn_conv_w"], w["ffn_conv_b"],
                              tms[1])
    dx2, dx2b, loss, dw_pg, dw_pp, dg_ple = _ple_fwd_bwd(x2, p, target, w["g_ple"], w["w_ple_gate"], w["w_ple_proj"], tms[0])
    dh2, dw_down, dw_up, dw_gate, dfcw, dfcb = _ffn_bwd(dx2b, h2, gp, up, w["w_gate"], w["w_up"], w["w_down"],
                                                        w["ffn_conv_w"], w["ffn_conv_b"], tms[0])
    token = None
    if "ffn_grads" in hooks:
        token = hooks["ffn_grads"]({"w_ple_gate": dw_pg, "w_ple_proj": dw_pp, "w_down": dw_down, "w_up": dw_up,
                                    "w_gate": dw_gate, "ffn_conv_b": dfcb})
    dx1, dya, dd, dzconv, dw_out, dg_ffn, dgoc, dgoa, dcw, dcb = _outproj_bwd(
        dh2, dx2, x1, _ordered_after(w["g_ffn"], token), w["w_out"], yc, ya, w["g_out_conv"], w["g_out_attn"], zconv,
        w["conv_w"], w["conv_b"], bd, tms[1])
    token = hooks["outproj_done"](dx1) if "outproj_done" in hooks else None
    dq, dk, dv = _attn_bwd(q, k, v, dya, lse, dd, e_all, m_all, slopes if token is None else token)
    dx, dw_in, dg_mix, dqg, dkg = _inproj_bwd(dq, dk, dv, dzconv, zqk, x, dx1, w["g_mix"], w["w_in"], qg, kg, bd,
                                              tms[0])
    grads = {
        "g_mix": dg_mix, "w_in": dw_in, "conv_w": dcw, "conv_b": dcb,
        "q_norm_g": dqg.reshape(8, HEAD_DIM).sum(0, keepdims=True),
        "k_norm_g": dkg.reshape(8, HEAD_DIM).sum(0, keepdims=True),
        "g_out_conv": dgoc, "g_out_attn": dgoa, "w_out": dw_out, "g_ffn": dg_ffn, "w_gate": dw_gate, "w_up": dw_up,
        "ffn_conv_w": dfcw, "ffn_conv_b": dfcb, "w_down": dw_down, "g_ple": dg_ple, "w_ple_gate": dw_pg,
        "w_ple_proj": dw_pp,
    }
    return loss, dx, grads


ANY = pl.BlockSpec(memory_space=pl.ANY)
MESH = pl.DeviceIdType.MESH


def _all_gather(shards, name):
    n = len(shards)

    def body(*refs):
        ins, outs = refs[:n], refs[n:2 * n]
        send_sems, recv_sems, local_sems = refs[2 * n:]
        x, y, c = lax.axis_index("x"), lax.axis_index("y"), lax.axis_index("c")
        me, sibling = (x, y, c), (x, y, 1 - c)
        chips = [(1 - x, y), (x, 1 - y), (1 - x, 1 - y)]

        def slot(dev):
            return 4 * dev[0] + 2 * dev[1] + dev[2]

        def copy(b, k, block, to, src=None):
            dst = outs[b].at[slot(block)]
            return pltpu.make_async_remote_copy(
                src_ref=dst if src is None else src, dst_ref=dst, send_sem=send_sems.at[b, k],
                recv_sem=recv_sems.at[b, k], device_id=to, device_id_type=MESH)

        mine = [pltpu.make_async_copy(ins[b], outs[b].at[slot(me)], local_sems.at[b]) for b in range(n)]
        first, passed = [], []
        for b in range(n):
            mine[b].start()
            first.append(copy(b, 0, me, sibling, src=ins[b]))
            first += [copy(b, 1 + j, me, (*chip, c), src=ins[b]) for j, chip in enumerate(chips)]
        for cp in first:
            cp.start()
        for j, chip in enumerate(chips):
            for b in range(n):
                copy(b, 1 + j, (*chip, c), me).wait_recv()
                fwd = copy(b, 4 + j, (*chip, c), sibling)
                fwd.start()
                passed.append(fwd)
        for b in range(n):
            copy(b, 0, sibling, me).wait_recv()
            for j, chip in enumerate(chips):
                copy(b, 4 + j, (*chip, 1 - c), me).wait_recv()
        for cp in first + passed:
            cp.wait_send()
        for cp in mine:
            cp.wait()

    return pl.pallas_call(
        body, name=name,
        in_specs=[ANY] * n, out_specs=[ANY] * n,
        out_shape=[jax.ShapeDtypeStruct((N_DEV,) + s.shape, s.dtype) for s in shards],
        scratch_shapes=[pltpu.SemaphoreType.DMA((n, 7)), pltpu.SemaphoreType.DMA((n, 7)),
                        pltpu.SemaphoreType.DMA((n,))],
    )(*shards)


HBM = pl.BlockSpec(memory_space=pltpu.HBM)
SEM = pl.BlockSpec(memory_space=pltpu.SEMAPHORE)
EFFECT = pltpu.SideEffectType.DATAFLOW_SIDE_EFFECTING
FLIPS = ((0, 0, 1), (0, 1, 0), (0, 1, 1), (1, 0, 0), (1, 0, 1), (1, 1, 0), (1, 1, 1))


def _flip_peers():
    pos = (lax.axis_index("x"), lax.axis_index("y"), lax.axis_index("c"))
    return [tuple(1 - a if f else a for a, f in zip(pos, flip)) for flip in FLIPS]


def _hbm(a):
    return pltpu.with_memory_space_constraint(a, pltpu.HBM)


def _own_copies(own, src_refs, land_refs, send_sems, n_remote):
    return [pltpu.make_async_copy(src, dst, send_sems.at[n_remote + i])
            for i, (src, dst) in enumerate(own(src_refs, land_refs) if own else [])]


def _split_start(name, srcs, lands, plan, n_copies, after, own=None):
    n, m = len(srcs), len(lands)

    def body(*refs):
        send_sems, recv_sems, token = refs[n + m + 1], refs[n + m + 2], refs[-1]
        for i, (src, dst, peer) in enumerate(plan(refs[:n], refs[n:n + m])):
            pltpu.make_async_remote_copy(src_ref=src, dst_ref=dst, send_sem=send_sems.at[i], recv_sem=recv_sems.at[i],
                                         device_id=peer, device_id_type=MESH).start()
        for copy in _own_copies(own, refs[:n], refs[n:n + m], send_sems, n_copies):
            copy.start()
        token[...] = jnp.zeros_like(token)

    outs = pl.pallas_call(
        body, name=name + "_start",
        in_specs=[HBM] * (n + m) + [ANY],
        out_specs=[SEM, SEM] + [HBM] * (n + m) + [pl.BlockSpec(memory_space=pltpu.VMEM)],
        out_shape=[pltpu.SemaphoreType.DMA((n_copies + (n if own else 0),)), pltpu.SemaphoreType.DMA((n_copies,))]
        + [pltpu.HBM(a.shape, a.dtype) for a in list(srcs) + list(lands)] + [jax.ShapeDtypeStruct((8, 128), F32)],
        input_output_aliases={i: 2 + i for i in range(n + m)},
        compiler_params=pltpu.CompilerParams(has_side_effects=EFFECT),
    )(*[_hbm(a) for a in list(srcs) + list(lands)], after)
    return (outs[0], outs[1], outs[2:2 + n], outs[2 + n:2 + n + m]), outs[-1]


def _split_wait(name, started, plan, after, own=None):
    send_sems, recv_sems, srcs, lands = started
    n, m = len(srcs), len(lands)

    def body(*refs):
        send_ref, recv_ref = refs[n + m], refs[n + m + 1]
        copies = plan(refs[:n], refs[n:n + m])
        for i, (src, dst, peer) in enumerate(copies):
            copy = pltpu.make_async_remote_copy(src_ref=src, dst_ref=dst, send_sem=send_ref.at[i],
                                                recv_sem=recv_ref.at[i], device_id=peer, device_id_type=MESH)
            copy.wait_send()
            copy.wait_recv()
        for copy in _own_copies(own, refs[:n], refs[n:n + m], send_ref, len(copies)):
            copy.wait()

    outs = pl.pallas_call(
        body, name=name + "_wait",
        in_specs=[HBM] * (n + m) + [SEM, SEM, ANY],
        out_specs=[HBM] * (n + m),
        out_shape=[pltpu.HBM(a.shape, a.dtype) for a in list(srcs) + list(lands)],
        input_output_aliases={i: i for i in range(n + m)},
        compiler_params=pltpu.CompilerParams(has_side_effects=EFFECT),
    )(*srcs, *lands, send_sems, recv_sems, after)
    return outs[:n], outs[n:]


def _gather_plan(srcs, lands):
    slot = 4 * lax.axis_index("x") + 2 * lax.axis_index("y") + lax.axis_index("c")
    return [(src, land.at[slot], peer) for src, land in zip(srcs, lands) for peer in _flip_peers()]


def _own_slot(srcs, lands):
    slot = 4 * lax.axis_index("x") + 2 * lax.axis_index("y") + lax.axis_index("c")
    return [(src, land.at[slot]) for src, land in zip(srcs, lands)]


def _sibling_plan(srcs, lands):
    x, y, c = lax.axis_index("x"), lax.axis_index("y"), lax.axis_index("c")
    return [(src.at[k, 1 - c], land.at[k], (x, y, 1 - c)) for src, land in zip(srcs, lands) for k in range(N_CHIP)]


def _chip_plan(srcs, lands):
    x, y, c = lax.axis_index("x"), lax.axis_index("y"), lax.axis_index("c")
    return [(src.at[2 * cx + cy], land.at[2 * x + y], (cx, cy, c))
            for src, land in zip(srcs, lands) for cx, cy in ((1 - x, y), (x, 1 - y), (1 - x, 1 - y))]


def _row_tile(rows):
    for tr in range(min(rows, 512), 15, -16):
        if rows % tr == 0:
            return tr
    return rows


def _pair_sums(gs, lands, core, name):
    n = len(gs)

    def body(c_ref, *refs):
        for b in range(n):
            out = refs[2 * n + b]
            out[...] = (refs[b][...].astype(F32) + refs[n + b][...].astype(F32)).astype(out.dtype)

    def slab(a):
        return pl.BlockSpec((None,) + a.shape[1:], lambda k, c_ref: (k, 0, 0))

    return pl.pallas_call(
        body, name=name,
        grid_spec=pltpu.PrefetchScalarGridSpec(
            num_scalar_prefetch=1, grid=(N_CHIP,),
            in_specs=[pl.BlockSpec((None, None) + g.shape[2:], lambda k, c_ref: (k, c_ref[0], 0, 0)) for g in gs]
            + [slab(a) for a in lands],
            out_specs=[slab(a) for a in lands]),
        out_shape=[jax.ShapeDtypeStruct(a.shape, a.dtype) for a in lands],
        compiler_params=_cparams("parallel"),
    )(core, *gs, *lands)


def _adamw(own, arrived, chip, w, m, v, name):
    k, rows, cols = arrived.shape
    tr = _row_tile(rows)
    c1 = 1.0 / (1.0 - ADAM_B1 ** ADAM_STEP)
    c2 = 1.0 / (1.0 - ADAM_B2 ** ADAM_STEP)

    def body(chip_ref, o_ref, p_ref, w_ref, m_ref, v_ref, g_ref, d_ref, nm_ref, nv_ref):
        def slab(j):
            return jnp.where(chip_ref[0] == j, o_ref[j], p_ref[j]).astype(F32)

        g = slab(0)
        for j in range(1, k):
            g = g + slab(j)
        g_ref[...] = g
        nm = ADAM_B1 * m_ref[...] + (1.0 - ADAM_B1) * g
        nv = ADAM_B2 * v_ref[...] + (1.0 - ADAM_B2) * (g * g)
        nm_ref[...] = nm
        nv_ref[...] = nv
        d_ref[...] = -ADAM_LR * ((nm * c1) / (jnp.sqrt(nv * c2) + ADAM_EPS) + ADAM_WD * w_ref[...])

    blk = pl.BlockSpec((tr, cols), lambda i, c: (i, 0))
    stack = pl.BlockSpec((k, tr, cols), lambda i, c: (0, i, 0))
    return pl.pallas_call(
        body, name=name,
        grid_spec=pltpu.PrefetchScalarGridSpec(num_scalar_prefetch=1, grid=(rows // tr,),
                                               in_specs=[stack, stack, blk, blk, blk], out_specs=[blk] * 4),
        out_shape=[jax.ShapeDtypeStruct((rows, cols), F32)] * 4,
        compiler_params=_cparams("parallel"),
    )(chip, own, arrived, w, m, v)


SMALL_LAYOUT = (("g_mix", 0, 1024), ("conv_b", 1, 512), ("q_norm_g", 2, 64), ("k_norm_g", 3, 64),
                ("g_out_conv", 4, 512), ("g_out_attn", 5, 512), ("g_ffn", 6, 1024), ("ffn_conv_b", 7, 2816),
                ("g_ple", 10, 1024))
CONV_W_ROW = 11
FFN_CONV_W_ROW = 14
LOSS_ROW = 23


def _row_pieces(cols):
    return [(c, min(1024, cols - c)) for c in range(0, cols, 1024)]


def _pack_small(grads, loss_tile):
    names = [n for n, _, _ in SMALL_LAYOUT]

    def body(*refs):
        ins, cw_ref, fcw_ref, loss_ref, out_ref = refs[:len(names)], refs[-4], refs[-3], refs[-2], refs[-1]
        out_ref[...] = jnp.zeros_like(out_ref)
        for ref, (_, row, cols) in zip(ins, SMALL_LAYOUT):
            for j, (c, width) in enumerate(_row_pieces(cols)):
                out_ref[row + j:row + j + 1, 0:width] = ref[:, c:c + width]
        for k in range(3):
            out_ref[CONV_W_ROW + k:CONV_W_ROW + k + 1, 0:CONV_W] = cw_ref[k:k + 1, :]
            for j, (c, width) in enumerate(_row_pieces(D_FF)):
                row = FFN_CONV_W_ROW + 3 * k + j
                out_ref[row:row + 1, 0:width] = fcw_ref[k:k + 1, c:c + width]
        out_ref[LOSS_ROW:LOSS_ROW + 1, 0:128] = loss_ref[0:1, :]

    return pl.pallas_call(
        body, name="pack_small_grads", out_shape=jax.ShapeDtypeStruct((SMALL_ROWS, 1024), F32),
    )(*[grads[n] for n in names], grads["conv_w"], grads["ffn_conv_w"], loss_tile)


def _adamw_small(arrived, conv_parts, fconv_parts, wts, mom, var):
    names = [n for n, _, _ in SMALL_LAYOUT] + ["conv_w", "ffn_conv_w"]
    c1 = 1.0 / (1.0 - ADAM_B1 ** ADAM_STEP)
    c2 = 1.0 / (1.0 - ADAM_B2 ** ADAM_STEP)
    n = len(names)

    def body(*refs):
        land, cw_ref, fcw_ref = refs[0], refs[1], refs[2]
        state = refs[3:3 + 3 * n]
        outs = refs[3 + 3 * n:]

        def total(piece):
            acc = piece(0)
            for d in range(1, N_DEV):
                acc = acc + piece(d)
            return acc

        for i, name in enumerate(names):
            if name == "conv_w":
                g = total(lambda d: cw_ref[d])
            elif name == "ffn_conv_w":
                g = total(lambda d: fcw_ref[d])
            else:
                _, row, cols = SMALL_LAYOUT[i]
                pieces = [total(lambda d, j=j, width=width: land[d, row + j:row + j + 1, 0:width])
                          for j, (_, width) in enumerate(_row_pieces(cols))]
                g = pieces[0] if len(pieces) == 1 else jnp.concatenate(pieces, axis=1)
            w_ref, m_ref, v_ref = state[3 * i:3 * i + 3]
            nm = ADAM_B1 * m_ref[...] + (1.0 - ADAM_B1) * g
            nv = ADAM_B2 * v_ref[...] + (1.0 - ADAM_B2) * (g * g)
            outs[4 * i][...] = g
            outs[4 * i + 1][...] = -ADAM_LR * ((nm * c1) / (jnp.sqrt(nv * c2) + ADAM_EPS) + ADAM_WD * w_ref[...])
            outs[4 * i + 2][...] = nm
            outs[4 * i + 3][...] = nv
        outs[-1][...] = total(lambda d: land[d, LOSS_ROW:LOSS_ROW + 1, 0:128])

    state = [a[nm_] for nm_ in names for a in (wts, mom, var)]
    shapes = [jax.ShapeDtypeStruct(wts[nm_].shape, F32) for nm_ in names for _ in range(4)]
    outs = pl.pallas_call(
        body, name="adamw_small", out_shape=shapes + [jax.ShapeDtypeStruct((1, 128), F32)],
    )(arrived, conv_parts, fconv_parts, *state)
    return {nm_: tuple(outs[4 * i:4 * i + 4]) for i, nm_ in enumerate(names)}, outs[-1][0, 0]


COL_SHARDED = ("w_in", "w_ple_proj")
TRANSPOSED = ("w_gate", "w_up")
CONV_SHARDED = (("conv_w", CONV_W), ("ffn_conv_w", D_FF))


def _gathered_to_full(name, gathered):
    if name in COL_SHARDED:
        return gathered.transpose(1, 0, 2).reshape(gathered.shape[1], -1)
    return gathered.reshape(-1, gathered.shape[2])


def _full_to_stacked(name, grad, shard_shape):
    sr, sc = shard_shape
    if grad.ndim == 3:
        a = grad
    elif name in COL_SHARDED:
        a = grad.reshape(sr, N_DEV, sc).transpose(1, 0, 2)
    else:
        a = grad.reshape(N_DEV, sr, sc)
    return a.astype(BF16).reshape(N_CHIP, 2, sr, sc)


def _pad_rows(vec, rows):
    return jnp.pad(vec, (0, rows * 1024 - vec.shape[0])).reshape(rows, 1024)


def kernel(x, p, g_mix, w_in, conv_w, conv_b, q_norm_g, k_norm_g, g_out_conv, g_out_attn, w_out, g_ffn, w_gate, w_up, ffn_conv_w, ffn_conv_b, w_down, g_ple, w_ple_gate, w_ple_proj, loss_target, m_g_mix, m_w_in, m_conv_w, m_conv_b, m_q_norm_g, m_k_norm_g, m_g_out_conv, m_g_out_attn, m_w_out, m_g_ffn, m_w_gate, m_w_up, m_ffn_conv_w, m_ffn_conv_b, m_w_down, m_g_ple, m_w_ple_gate, m_w_ple_proj, v_g_mix, v_w_in, v_conv_w, v_conv_b, v_q_norm_g, v_k_norm_g, v_g_out_conv, v_g_out_attn, v_w_out, v_g_ffn, v_w_gate, v_w_up, v_ffn_conv_w, v_ffn_conv_b, v_w_down, v_g_ple, v_w_ple_gate, v_w_ple_proj):
    args = dict(locals())
    names = ["g_mix", "w_in", "conv_w", "conv_b", "q_norm_g", "k_norm_g", "g_out_conv", "g_out_attn", "w_out", "g_ffn",
             "w_gate", "w_up", "ffn_conv_w", "ffn_conv_b", "w_down", "g_ple", "w_ple_gate", "w_ple_proj"]
    big = list(BIG)
    conv = [n for n, _ in CONV_SHARDED]

    def local(prefix):
        out = {n: (args[prefix + n][0] if n in big or n in conv else args[prefix + n]) for n in names}
        out.update({n: out[n].T for n in TRANSPOSED})
        return out

    wts, mom, var = local(""), local("m_"), local("v_")
    shard_shapes = {n: wts[n].shape for n in big}
    dev = 4 * lax.axis_index("x") + 2 * lax.axis_index("y") + lax.axis_index("c")
    core = lax.axis_index("c").astype(jnp.int32).reshape(1)

    conv_local = _pad_rows(jnp.concatenate([wts[n].reshape(-1) for n in conv]), 8).reshape(8, 1024)
    late = [n for n in big if n != "w_in"]
    w_in_all, conv_all = _all_gather([wts["w_in"].astype(BF16), conv_local], "gather_weights")
    late_shards = [wts[n].astype(BF16) for n in late]
    gathering, token = _split_start("gather_late_weights", late_shards,
                                    [lax.empty((N_DEV,) + s.shape, BF16) for s in late_shards], _gather_plan,
                                    7 * len(late), w_in_all, own=_own_slot)
    full = dict(wts)
    full["w_in"] = _gathered_to_full("w_in", w_in_all)
    full["g_mix"] = _ordered_after(wts["g_mix"], token)
    flying = {}

    def late_weights(after):
        _, lands = _split_wait("gather_late_weights", gathering, _gather_plan, after, own=_own_slot)
        return {n: _gathered_to_full(n, land) for n, land in zip(late, lands)}

    early = ["w_ple_gate", "w_ple_proj", "w_down", "w_up", "w_gate"]

    def ffn_grads(g):
        stacked = [_full_to_stacked(n, g[n], shard_shapes[n]) for n in early]
        flying["sibling"], tok = _split_start("rs_sibling_early", stacked,
                                              [lax.empty((N_CHIP,) + s.shape[2:], BF16) for s in stacked],
                                              _sibling_plan, N_CHIP * len(early), g["ffn_conv_b"])
        return tok

    def outproj_done(after):
        stacked, landed = _split_wait("rs_sibling_early", flying["sibling"], _sibling_plan, after)
        parts = _pair_sums(stacked, landed, core, "rs_pair_sums_early")
        flying["chip"], tok = _split_start("rs_chip_early", parts, [lax.empty(q.shape, BF16) for q in parts],
                                           _chip_plan, 3 * len(early), landed[0])
        return tok

    off = 0
    for n, width in CONV_SHARDED:
        sc = width // N_DEV
        a = conv_all.reshape(N_DEV, -1)[:, off:off + 3 * sc].reshape(N_DEV, 3, sc)
        full[n] = a.transpose(1, 0, 2).reshape(3, width)
        off += 3 * sc

    loss, dx, grads = _local_step(x[0], p[0, 0], loss_target[0], full, (512, 256),
                                  {"late_weights": late_weights, "ffn_grads": ffn_grads, "outproj_done": outproj_done})

    chip = (2 * lax.axis_index("x") + lax.axis_index("y")).astype(jnp.int32).reshape(1)

    def adamw_of(group, parts, arrived):
        return {n: _adamw(own, got, chip, wts[n], mom[n], var[n], f"adamw_{n}")
                for n, own, got in zip(group, parts, arrived)}

    last = [n for n in big if n not in early]
    stacked = [_full_to_stacked(n, grads[n], shard_shapes[n]) for n in last]
    flying["sibling_last"], tok = _split_start("rs_sibling_last", stacked,
                                               [lax.empty((N_CHIP,) + s.shape[2:], BF16) for s in stacked],
                                               _sibling_plan, N_CHIP * len(last), dx)
    packed = _pack_small(grads, loss)
    flying["small"], tok = _split_start("gather_small_grads", [packed], [lax.empty((N_DEV,) + packed.shape, F32)],
                                        _gather_plan, N_DEV - 1, tok, own=_own_slot)
    stacked, landed = _split_wait("rs_sibling_last", flying["sibling_last"], _sibling_plan, tok)
    parts = _pair_sums(stacked, landed, core, "rs_pair_sums_last")
    flying["chip_last"], tok = _split_start("rs_chip_last", parts, [lax.empty(q.shape, BF16) for q in parts],
                                            _chip_plan, 3 * len(last), landed[0])

    parts, arrived = _split_wait("rs_chip_early", flying["chip"], _chip_plan, tok)
    out = adamw_of(early, parts, arrived)
    _, (small_all,) = _split_wait("gather_small_grads", flying["small"], _gather_plan, out[early[-1]][0],
                                  own=_own_slot)
    taps = small_all[:, CONV_W_ROW:CONV_W_ROW + 3, 0:CONV_W]
    ftaps = small_all[:, FFN_CONV_W_ROW:FFN_CONV_W_ROW + 9, :].reshape(N_DEV, 3, 3 * 1024)
    small_out, loss_total = _adamw_small(
        small_all, lax.dynamic_slice(taps, (0, 0, dev * (CONV_W // N_DEV)), (N_DEV, 3, CONV_W // N_DEV)),
        lax.dynamic_slice(ftaps, (0, 0, dev * (D_FF // N_DEV)), (N_DEV, 3, D_FF // N_DEV)), wts, mom, var)
    out.update(small_out)
    parts, arrived = _split_wait("rs_chip_last", flying["chip_last"], _chip_plan, small_out["g_mix"][0])
    out.update(adamw_of(last, parts, arrived))
    def result(n, which):
        a = out[n][which]
        return (a.T if n in TRANSPOSED else a).reshape(args[n].shape)

    return (loss_total, dx[None], *[result(n, which) for which in range(4) for n in names])
```

```python
import jax
import jax.numpy as jnp
from jax import lax
from jax.experimental import pallas as pl
from jax.experimental.pallas import tpu as pltpu

F32 = jnp.float32
BF16 = jnp.bfloat16

D_MODEL = 1024
CONV_W = 512
ATTN_W = 512
HEAD_DIM = 64
D_FF = 2816
PLE_DIM = 256
IN_COLS = 3 * CONV_W + 3 * ATTN_W
EPS = 1e-6
QK_BLOCK = 128
DILATIONS = (1, 4, 16)
ATTN_SCALE = HEAD_DIM ** -0.5

ADAM_LR = 0.001
ADAM_B1 = 0.9
ADAM_B2 = 0.999
ADAM_EPS = 1e-08
ADAM_WD = 0.01
ADAM_STEP = 10

N_DEV = 8
N_CHIP = 4
V7X_VMEM_LIMIT = 56 * 1024 * 1024
V7X_VMEM_LIMIT_LARGE = 62 * 1024 * 1024
FF_CHUNKS = 2
FFN_BWD_PARTS = 1

BIG = ("w_in", "w_out", "w_gate", "w_up", "w_down", "w_ple_gate", "w_ple_proj")
SMALL_ROWS = 24


def _cparams(*sem, vmem=V7X_VMEM_LIMIT):
    return pltpu.CompilerParams(dimension_semantics=sem, vmem_limit_bytes=vmem)


def _mm(a, b):
    return jnp.dot(a, b, preferred_element_type=F32)


def _mm_nt(a, b):
    return lax.dot_general(a, b, (((1,), (1,)), ((), ())), preferred_element_type=F32)


def _mm_tn(a, b):
    return lax.dot_general(a, b, (((0,), (0,)), ((), ())), preferred_element_type=F32)


def _full(shape):
    nd = len(shape)
    return pl.BlockSpec(shape, lambda *_: (0,) * nd)


def _rms_stats(x):
    r = lax.rsqrt(jnp.mean(x * x, axis=-1, keepdims=True) + EPS)
    return r, x * r


def _rms_bwd(dy, xhat, r, g):
    gd = dy * g
    return r * (gd - xhat * jnp.mean(gd * xhat, axis=-1, keepdims=True))


def _seg_sum64(v, bd_ref):
    outs = []
    for c in range(0, v.shape[1], 256):
        vc = v[:, c:c + 256]
        hi = vc.astype(BF16)
        lo = (vc - hi.astype(F32)).astype(BF16)
        outs.append(_mm(hi, bd_ref[...]) + _mm(lo, bd_ref[...]))
    return outs[0] if len(outs) == 1 else jnp.concatenate(outs, axis=1)


def _shift_rows(u, k, edge_rows):
    out = pltpu.roll(u, k, 0)
    row = lax.broadcasted_iota(jnp.int32, (8, u.shape[1]), 0)
    head = out[0:8]
    for j in range(k):
        head = jnp.where(row == j, edge_rows[k - 1 - j], head)
    return jnp.concatenate([head, out[8:]], axis=0)


def _shift_rows_up(u, k, edge_rows):
    n = u.shape[0]
    out = pltpu.roll(u, n - k, 0)
    row = lax.broadcasted_iota(jnp.int32, (8, u.shape[1]), 0)
    tail = out[n - 8:n]
    for j in range(k):
        tail = jnp.where(row == 8 - k + j, edge_rows[j], tail)
    return jnp.concatenate([out[0:n - 8], tail], axis=0)


def _conv_fwd(u, c1, c2, w_ref, b_ref):
    u1 = _shift_rows(u, 1, (c1,))
    u2 = _shift_rows(u, 2, (c1, c2))
    y = u2 * w_ref[0:1, :] + u1 * w_ref[1:2, :] + u * w_ref[2:3, :] + b_ref[...]
    return y, u1, u2


def _conv_bwd_input(dy, n1row, n2row, w_ref):
    d1 = _shift_rows_up(dy, 1, (n1row,))
    d2 = _shift_rows_up(dy, 2, (n1row, n2row))
    return dy * w_ref[2:3, :] + d1 * w_ref[1:2, :] + d2 * w_ref[0:1, :]


def _sigmoid(x):
    return 1.0 / (1.0 + jnp.exp(-x))


def _inproj_fwd(x, g_mix, w_in, conv_w, conv_b, qg, kg, bd, tm):
    t = x.shape[0]

    def body(x_ref, g_ref, w_ref, cw_ref, cb_ref, qg_ref, kg_ref, bd_ref,
             zc_ref, zqk_ref, yc_ref, q_ref, k_ref, v_ref, carry_ref):
        @pl.when(pl.program_id(0) == 0)
        def _():
            carry_ref[...] = jnp.zeros_like(carry_ref)

        _, xhat = _rms_stats(x_ref[...])
        h = (xhat * g_ref[...]).astype(BF16)
        zconv = _mm(h, w_ref[:, 0:3 * CONV_W])
        zc_ref[...] = zconv.astype(BF16)
        u = zconv[:, CONV_W:2 * CONV_W] * zconv[:, 2 * CONV_W:3 * CONV_W]
        cv, _, _ = _conv_fwd(u, carry_ref[7:8, :], carry_ref[6:7, :], cw_ref, cb_ref)
        yc_ref[...] = (zconv[:, 0:CONV_W] * cv).astype(BF16)
        carry_ref[...] = u[tm - 8:tm, :]

        zqk = _mm(h, w_ref[:, 3 * CONV_W:3 * CONV_W + 2 * ATTN_W])
        zqk_ref[...] = zqk.astype(BF16)
        for j, (gain_ref, out_ref, scale) in enumerate(((qg_ref, q_ref, ATTN_SCALE), (kg_ref, k_ref, 1.0))):
            z = zqk[:, j * ATTN_W:(j + 1) * ATTN_W]
            r = lax.rsqrt(_seg_sum64(z * z, bd_ref) * (1.0 / HEAD_DIM) + EPS)
            out_ref[...] = z * r * gain_ref[...] * scale
        v_ref[...] = _mm(h, w_ref[:, 3 * CONV_W + 2 * ATTN_W:IN_COLS])

    def blk(c):
        return pl.BlockSpec((tm, c), lambda i: (i, 0))

    return pl.pallas_call(
        body, name="inproj_fwd", grid=(t // tm,),
        in_specs=[blk(D_MODEL), _full((1, D_MODEL)), _full((D_MODEL, IN_COLS)), _full((3, CONV_W)),
                  _full((1, CONV_W)), _full((1, ATTN_W)), _full((1, ATTN_W)), _full((256, 256))],
        out_specs=[blk(3 * CONV_W), blk(2 * ATTN_W), blk(CONV_W), blk(ATTN_W), blk(ATTN_W), blk(ATTN_W)],
        out_shape=[jax.ShapeDtypeStruct((t, 3 * CONV_W), BF16), jax.ShapeDtypeStruct((t, 2 * ATTN_W), BF16),
                   jax.ShapeDtypeStruct((t, CONV_W), BF16), jax.ShapeDtypeStruct((t, ATTN_W), F32),
                   jax.ShapeDtypeStruct((t, ATTN_W), F32), jax.ShapeDtypeStruct((t, ATTN_W), F32)],
        scratch_shapes=[pltpu.VMEM((8, CONV_W), F32)],
        compiler_params=_cparams("arbitrary"),
    )(x, g_mix, w_in, conv_w, conv_b, qg, kg, bd)


SUPER = 16 * QK_BLOCK
KEYS = 2 * QK_BLOCK
UNITS = SUPER // QK_BLOCK


def _rows(start, size, dil):
    return pl.ds(start, size) if dil == 1 else pl.ds(start, size, stride=dil)


def _attn_bias(sl_ref, dil):
    qi = lax.broadcasted_iota(jnp.int32, (KEYS, KEYS), 0)
    kj = lax.broadcasted_iota(jnp.int32, (KEYS, KEYS), 1)
    step = jnp.bitwise_and(qi, QK_BLOCK - 1) + QK_BLOCK - kj
    slope = jnp.where(qi < QK_BLOCK, sl_ref[0, 0:1, 0:1], sl_ref[0, 1:2, 0:1])
    bias = jnp.where(jnp.logical_and(step >= 0, step <= QK_BLOCK), -slope * (step * dil).astype(F32), -jnp.inf)
    return bias, kj >= QK_BLOCK


def _unit_start(u, dil):
    if dil == 1:
        return pl.multiple_of(u * QK_BLOCK, QK_BLOCK)
    if dil == 4:
        return jnp.bitwise_and(u, 3) + (u // 4) * (4 * QK_BLOCK)
    return u


def _stack_heads(a, head0):
    zero = jnp.zeros_like(a)
    return jnp.concatenate([jnp.where(head0, a, zero), jnp.where(head0, zero, a)], axis=0)


def _attn_fwd(q, k, v, slopes):
    t = q.shape[0]
    nsb = t // SUPER

    def body(q_ref, kc_ref, kp_ref, vc_ref, vp_ref, sl_ref, o_ref, l_ref, e_ref, m_ref, kk, vv, ob, lb):
        s = pl.program_id(1)
        kk[0:SUPER, :] = kp_ref[...]
        kk[SUPER:, :] = kc_ref[...]
        vv[0:SUPER, :] = vp_ref[...]
        vv[SUPER:, :] = vc_ref[...]
        head0 = lax.broadcasted_iota(jnp.int32, (QK_BLOCK, QK_BLOCK), 1) < HEAD_DIM

        for b, dil in enumerate(DILATIONS):
            bias, own_half = _attn_bias(sl_ref, dil)

            def unit(u, carry, b=b, dil=dil, bias=bias, own_half=own_half):
                start = _unit_start(u, dil)
                first_key = SUPER + start - QK_BLOCK * dil
                q2 = _stack_heads(q_ref[_rows(start, QK_BLOCK, dil), :].astype(BF16), head0)
                k2 = kk[_rows(first_key, KEYS, dil), :].astype(BF16)
                v2 = vv[_rows(first_key, KEYS, dil), :].astype(BF16)
                has_prev = jnp.logical_or(s > 0, start >= QK_BLOCK * dil)
                sc = jnp.where(jnp.logical_or(own_half, has_prev), _mm_nt(q2, k2) + bias, -jnp.inf)
                m = jnp.max(sc, axis=-1, keepdims=True)
                e = jnp.exp(sc - m)
                den = jnp.sum(e, axis=-1, keepdims=True)
                eb = e.astype(BF16)
                e_ref[b * UNITS + u] = eb
                o2 = _mm(eb, v2) / den
                l2 = m + jnp.log(den)
                ob[b, _rows(start, QK_BLOCK, dil), :] = jnp.where(head0, o2[0:QK_BLOCK], o2[QK_BLOCK:])
                lb[b, _rows(start, QK_BLOCK, dil), :] = jnp.where(head0, l2[0:QK_BLOCK], l2[QK_BLOCK:])
                m_ref[b, _rows(start, QK_BLOCK, dil), :] = jnp.where(head0, m[0:QK_BLOCK], m[QK_BLOCK:])
                return carry

            lax.fori_loop(0, UNITS, unit, 0, unroll=16)

        def merge(i, carry):
            rows = pl.ds(pl.multiple_of(i * 256, 256), 256)
            la, lb_, lc = lb[0, rows, :], lb[1, rows, :], lb[2, rows, :]
            mx = jnp.maximum(jnp.maximum(la, lb_), lc)
            wa, wb, wc = jnp.exp(la - mx), jnp.exp(lb_ - mx), jnp.exp(lc - mx)
            sw = wa + wb + wc
            o_ref[rows, :] = ((wa * ob[0, rows, :] + wb * ob[1, rows, :] + wc * ob[2, rows, :]) / sw).astype(BF16)
            l_ref[rows, :] = mx + jnp.log(sw)
            return carry

        lax.fori_loop(0, SUPER // 256, merge, 0)

    cur = pl.BlockSpec((SUPER, QK_BLOCK), lambda p, s: (s, p))
    prev = pl.BlockSpec((SUPER, QK_BLOCK), lambda p, s: (jnp.maximum(s - 1, 0), p))
    return pl.pallas_call(
        body, name="attn_fwd", grid=(4, nsb),
        in_specs=[cur, cur, prev, cur, prev, pl.BlockSpec((1, 2, QK_BLOCK), lambda p, s: (p, 0, 0))],
        out_specs=[cur, cur, pl.BlockSpec((None, None, 3 * UNITS, KEYS, KEYS), lambda p, s: (p, s, 0, 0, 0)),
                   pl.BlockSpec((3, SUPER, QK_BLOCK), lambda p, s: (0, s, p))],
        out_shape=[jax.ShapeDtypeStruct((t, ATTN_W), BF16), jax.ShapeDtypeStruct((t, ATTN_W), F32),
                   jax.ShapeDtypeStruct((4, nsb, 3 * UNITS, KEYS, KEYS), BF16),
                   jax.ShapeDtypeStruct((3, t, ATTN_W), F32)],
        scratch_shapes=[pltpu.VMEM((2 * SUPER, QK_BLOCK), F32), pltpu.VMEM((2 * SUPER, QK_BLOCK), F32),
                        pltpu.VMEM((3, SUPER, QK_BLOCK), F32), pltpu.VMEM((3, SUPER, QK_BLOCK), F32)],
        compiler_params=_cparams("parallel", "arbitrary"),
    )(q, k, k, v, v, slopes)


def _outproj_fwd(ya, yc, x, goc, goa, w_out, tm):
    t = x.shape[0]

    def body(ya_ref, yc_ref, x_ref, goc_ref, goa_ref, w_ref, x1_ref):
        _, ychat = _rms_stats(yc_ref[...].astype(F32))
        _, yahat = _rms_stats(ya_ref[...].astype(F32))
        acc = _mm((ychat * goc_ref[...]).astype(BF16), w_ref[0:CONV_W, :])
        acc += _mm((yahat * goa_ref[...]).astype(BF16), w_ref[CONV_W:, :])
        x1_ref[...] = x_ref[...] + acc

    def blk(c):
        return pl.BlockSpec((tm, c), lambda i: (i, 0))

    return pl.pallas_call(
        body, name="outproj_fwd", grid=(t // tm,),
        in_specs=[blk(ATTN_W), blk(CONV_W), blk(D_MODEL), _full((1, CONV_W)), _full((1, ATTN_W)),
                  _full((D_MODEL, D_MODEL))],
        out_specs=blk(D_MODEL),
        out_shape=jax.ShapeDtypeStruct((t, D_MODEL), F32),
        compiler_params=_cparams("parallel"),
    )(ya, yc, x, goc, goa, w_out)


def _ffn_fwd(x1, g_ffn, w_gate_t, w_up_t, w_down, fcw, fcb, tm):
    t = x1.shape[0]

    def body(x_ref, g_ref, wg_ref, wu_ref, wd_ref, cw_ref, cb_ref, gp_ref, up_ref, h_ref, x2_ref, carry_ref):
        @pl.when(pl.program_id(0) == 0)
        def _():
            carry_ref[...] = jnp.zeros_like(carry_ref)

        xv = x_ref[...]
        _, xhat = _rms_stats(xv)
        h = (xhat * g_ref[...]).astype(BF16)
        h_ref[...] = h
        gp = _mm_nt(h, wg_ref[...])
        gp_ref[...] = gp.astype(BF16)
        gate, _, _ = _conv_fwd(gp, carry_ref[7:8, :], carry_ref[6:7, :], cw_ref, cb_ref)
        carry_ref[...] = gp[tm - 8:tm, :]
        up = _mm_nt(h, wu_ref[...])
        up_ref[...] = up.astype(BF16)
        a = (gate * _sigmoid(gate) * up).astype(BF16)
        x2_ref[...] = xv + _mm(a, wd_ref[...])

    def blk(c):
        return pl.BlockSpec((tm, c), lambda i: (i, 0))

    return pl.pallas_call(
        body, name="ffn_fwd", grid=(t // tm,),
        in_specs=[blk(D_MODEL), _full((1, D_MODEL)), _full((D_FF, D_MODEL)), _full((D_FF, D_MODEL)),
                  _full((D_FF, D_MODEL)), _full((3, D_FF)), _full((1, D_FF))],
        out_specs=[blk(D_FF), blk(D_FF), blk(D_MODEL), blk(D_MODEL)],
        out_shape=[jax.ShapeDtypeStruct((t, D_FF), BF16), jax.ShapeDtypeStruct((t, D_FF), BF16),
                   jax.ShapeDtypeStruct((t, D_MODEL), BF16), jax.ShapeDtypeStruct((t, D_MODEL), F32)],
        scratch_shapes=[pltpu.VMEM((8, D_FF), F32)],
        compiler_params=_cparams("arbitrary"),
    )(x1, g_ffn, w_gate_t, w_up_t, w_down, fcw, fcb)


def _ple_fwd_bwd(x2, p, target, g_ple, w_pg, w_pp, tm):
    t = x2.shape[0]

    def body(x_ref, p_ref, t_ref, g_ref, wg_ref, wp_ref, dx_ref, dxb_ref, loss_ref, dwgb_ref, dwp_ref, dg_ref,
             dwg_ref):
        @pl.when(pl.program_id(0) == 0)
        def _():
            loss_ref[...] = jnp.zeros_like(loss_ref)
            dwg_ref[...] = jnp.zeros_like(dwg_ref)
            dwp_ref[...] = jnp.zeros_like(dwp_ref)
            dg_ref[...] = jnp.zeros_like(dg_ref)

        xv = x_ref[...]
        r, xhat = _rms_stats(xv)
        g = g_ref[...]
        h = (xhat * g).astype(BF16)
        pg = _sigmoid(_mm(h, wg_ref[...]))
        pb = p_ref[...].astype(BF16)
        pp = _mm(pb, wp_ref[...])
        err = xv + pg * pp - t_ref[...]
        loss_ref[...] += 0.5 * jnp.sum(jnp.mean(err * err, axis=-1, keepdims=True))
        dx3 = err * (1.0 / D_MODEL)
        d_pp = (dx3 * pg).astype(BF16)
        d_pre = (dx3 * pp * pg * (1.0 - pg)).astype(BF16)
        dwp_ref[...] += _mm_tn(pb, d_pp)
        dwg_ref[...] += _mm_tn(h, d_pre)
        dh = _mm_nt(d_pre, wg_ref[...])
        dg_ref[...] += jnp.sum(dh * xhat, axis=0, keepdims=True)
        dx2 = dx3 + _rms_bwd(dh, xhat, r, g)
        dx_ref[...] = dx2
        dxb_ref[...] = dx2.astype(BF16)

        @pl.when(pl.program_id(0) == t // tm - 1)
        def _():
            dwgb_ref[...] = dwg_ref[...].astype(BF16)

    def blk(c):
        return pl.BlockSpec((tm, c), lambda i: (i, 0))

    return pl.pallas_call(
        body, name="ple_fwd_bwd", grid=(t // tm,),
        in_specs=[blk(D_MODEL), blk(PLE_DIM), blk(D_MODEL), _full((1, D_MODEL)), _full((D_MODEL, D_MODEL)),
                  _full((PLE_DIM, D_MODEL))],
        out_specs=[blk(D_MODEL), blk(D_MODEL), _full((8, 128)), _full((D_MODEL, D_MODEL)),
                   _full((PLE_DIM, D_MODEL)), _full((1, D_MODEL))],
        out_shape=[jax.ShapeDtypeStruct((t, D_MODEL), F32), jax.ShapeDtypeStruct((t, D_MODEL), BF16),
                   jax.ShapeDtypeStruct((8, 128), F32),
                   jax.ShapeDtypeStruct((D_MODEL, D_MODEL), BF16), jax.ShapeDtypeStruct((PLE_DIM, D_MODEL), F32),
                   jax.ShapeDtypeStruct((1, D_MODEL), F32)],
        scratch_shapes=[pltpu.VMEM((D_MODEL, D_MODEL), F32)],
        compiler_params=_cparams("arbitrary"),
    )(x2, p, target, g_ple, w_pg, w_pp)


def _ffn_bwd(dx2, h2, gp, up, w_gate, w_up, w_down, fcw, fcb, tm):
    t = dx2.shape[0]
    nblk = t // tm
    fc = D_FF // FF_CHUNKS
    half = tm // FFN_BWD_PARTS

    def body(dx_ref, h_ref, gp_ref, gph_ref, up_ref, wg_ref, wu_ref, wd_ref, cw_ref, cb_ref,
             dh_ref, dwd_hbm, dwu_hbm, dwg_hbm, dcw_ref, dcb_ref, carry_ref, a_scr, dup_scr, dgp_scr,
             dwd_acc, dwu_acc, dwg_acc, stage, stage_sem):
        i = pl.program_id(1)

        @pl.when(i == 0)
        def _():
            carry_ref[...] = jnp.zeros_like(carry_ref)
            dwd_acc[...] = jnp.zeros_like(dwd_acc)
            dwu_acc[...] = jnp.zeros_like(dwu_acc)
            dwg_acc[...] = jnp.zeros_like(dwg_acc)
            dcw_ref[...] = jnp.zeros_like(dcw_ref)
            dcb_ref[...] = jnp.zeros_like(dcb_ref)

        keep = (i < nblk - 1).astype(F32)
        later = carry_ref[...]
        for hf in reversed(range(FFN_BWD_PARTS)):
            rows = slice(hf * half, (hf + 1) * half)
            dxb = dx_ref[rows, :]
            gp_v = gp_ref[rows, :].astype(F32)
            if hf > 0:
                before = gp_ref[hf * half - 16:hf * half, :].astype(F32)
            else:
                before = gph_ref[...].astype(F32) * keep
            gate, gp1, gp2 = _conv_fwd(gp_v, before[15:16, :], before[14:15, :], cw_ref, cb_ref)
            s = _sigmoid(gate)
            silu = gate * s
            up_v = up_ref[rows, :].astype(F32)
            da = _mm_nt(dxb, wd_ref[...])
            a_scr[rows, :] = (silu * up_v).astype(BF16)
            d_up = (da * silu).astype(BF16)
            dup_scr[rows, :] = d_up
            d_gate = da * up_v * (s * (1.0 + gate * (1.0 - s)))
            d_gp = _conv_bwd_input(d_gate, later[0:1, :], later[1:2, :], cw_ref).astype(BF16)
            dgp_scr[rows, :] = d_gp
            later = d_gate[0:8, :]
            dcw_ref[0:1, :] += jnp.sum(d_gate * gp2, axis=0, keepdims=True)
            dcw_ref[1:2, :] += jnp.sum(d_gate * gp1, axis=0, keepdims=True)
            dcw_ref[2:3, :] += jnp.sum(d_gate * gp_v, axis=0, keepdims=True)
            dcb_ref[...] += jnp.sum(d_gate, axis=0, keepdims=True)
            dh_ref[rows, :] = (_mm(d_gp, wg_ref[...]) + _mm(d_up, wu_ref[...])).astype(BF16)
        carry_ref[...] = later
        dwd_acc[...] += _mm_tn(a_scr[...], dx_ref[...])
        dwu_acc[...] += _mm_tn(h_ref[...], dup_scr[...])
        dwg_acc[...] += _mm_tn(h_ref[...], dgp_scr[...])

        @pl.when(i == nblk - 1)
        def _():
            rows = pl.ds(pl.multiple_of(pl.program_id(0) * fc, 16), fc)
            for acc, out, flip in ((dwd_acc, dwd_hbm, False), (dwu_acc, dwu_hbm, True), (dwg_acc, dwg_hbm, True)):
                stage[...] = (acc[...].T if flip else acc[...]).astype(BF16)
                copy = pltpu.make_async_copy(stage, out.at[rows, :], stage_sem)
                copy.start()
                copy.wait()

    def rev(i):
        return nblk - 1 - i

    one = pl.Buffered(1)
    in_specs = [
        pl.BlockSpec((tm, D_MODEL), lambda j, i: (rev(i), 0)),
        pl.BlockSpec((tm, D_MODEL), lambda j, i: (rev(i), 0)),
        pl.BlockSpec((tm, fc), lambda j, i: (rev(i), j)),
        pl.BlockSpec((16, fc), lambda j, i: (jnp.maximum(rev(i) * (tm // 16) - 1, 0), j)),
        pl.BlockSpec((tm, fc), lambda j, i: (rev(i), j)),
        pl.BlockSpec((fc, D_MODEL), lambda j, i: (j, 0), pipeline_mode=one),
        pl.BlockSpec((fc, D_MODEL), lambda j, i: (j, 0), pipeline_mode=one),
        pl.BlockSpec((fc, D_MODEL), lambda j, i: (j, 0), pipeline_mode=one),
        pl.BlockSpec((3, fc), lambda j, i: (0, j)),
        pl.BlockSpec((1, fc), lambda j, i: (0, j)),
    ]
    out_specs = [
        pl.BlockSpec((None, tm, D_MODEL), lambda j, i: (j, rev(i), 0)),
        ANY, ANY, ANY,
        pl.BlockSpec((3, fc), lambda j, i: (0, j)),
        pl.BlockSpec((1, fc), lambda j, i: (0, j)),
    ]
    return pl.pallas_call(
        body, name="ffn_bwd", grid=(FF_CHUNKS, nblk), in_specs=in_specs, out_specs=out_specs,
        out_shape=[jax.ShapeDtypeStruct((FF_CHUNKS, t, D_MODEL), BF16), jax.ShapeDtypeStruct((D_FF, D_MODEL), BF16),
                   jax.ShapeDtypeStruct((D_FF, D_MODEL), BF16), jax.ShapeDtypeStruct((D_FF, D_MODEL), BF16),
                   jax.ShapeDtypeStruct((3, D_FF), F32), jax.ShapeDtypeStruct((1, D_FF), F32)],
        scratch_shapes=[pltpu.VMEM((8, fc), F32), pltpu.VMEM((tm, fc), BF16), pltpu.VMEM((tm, fc), BF16),
                        pltpu.VMEM((tm, fc), BF16), pltpu.VMEM((fc, D_MODEL), F32), pltpu.VMEM((D_MODEL, fc), F32),
                        pltpu.VMEM((D_MODEL, fc), F32), pltpu.VMEM((fc, D_MODEL), BF16), pltpu.SemaphoreType.DMA],
        compiler_params=_cparams("arbitrary", "arbitrary", vmem=V7X_VMEM_LIMIT_LARGE),
    )(dx2, h2, gp, gp, up, w_gate, w_up, w_down, fcw, fcb)


def _outproj_bwd(dh2, dx2, x1, g_ffn, w_out, yc, ya, goc, goa, zconv, conv_w, conv_b, bd, tm):
    t = x1.shape[0]
    nblk = t // tm

    def body(dh_ref, dx2_ref, x1_ref, g_ref, w_ref, yc_ref, ya_ref, goc_ref, goa_ref, zc_ref, zch_ref, cw_ref, cb_ref,
             bd_ref, dx1_ref, dya_ref, dd_ref, dzc_ref, dwb_ref, dg_ref, dgoc_ref, dgoa_ref, dcw_ref, dcb_ref,
             carry_ref, dw_ref):
        i = pl.program_id(0)

        @pl.when(i == 0)
        def _():
            carry_ref[...] = jnp.zeros_like(carry_ref)
            for ref in (dw_ref, dg_ref, dgoc_ref, dgoa_ref, dcw_ref, dcb_ref):
                ref[...] = jnp.zeros_like(ref)

        keep = (i < nblk - 1).astype(F32)
        dh2_v = dh_ref[0].astype(F32)
        for j in range(1, FF_CHUNKS):
            dh2_v = dh2_v + dh_ref[j].astype(F32)
        r, xhat = _rms_stats(x1_ref[...])
        dg_ref[...] += jnp.sum(dh2_v * xhat, axis=0, keepdims=True)
        dx1 = dx2_ref[...] + _rms_bwd(dh2_v, xhat, r, g_ref[...])
        dx1_ref[...] = dx1
        dx1b = dx1.astype(BF16)
        dy = _mm_nt(dx1b, w_ref[...])

        yc_v = yc_ref[...].astype(F32)
        rc, ychat = _rms_stats(yc_v)
        dw_ref[0:CONV_W, :] += _mm_tn((ychat * goc_ref[...]).astype(BF16), dx1b)
        dyc = dy[:, 0:CONV_W]
        dgoc_ref[...] += jnp.sum(dyc * ychat, axis=0, keepdims=True)
        d_yc = _rms_bwd(dyc, ychat, rc, goc_ref[...])

        ya_v = ya_ref[...].astype(F32)
        ra, yahat = _rms_stats(ya_v)
        dw_ref[CONV_W:, :] += _mm_tn((yahat * goa_ref[...]).astype(BF16), dx1b)
        dya = dy[:, CONV_W:]
        dgoa_ref[...] += jnp.sum(dya * yahat, axis=0, keepdims=True)
        d_ya = _rms_bwd(dya, yahat, ra, goa_ref[...])
        dya_ref[...] = d_ya
        dd_ref[...] = _seg_sum64(d_ya * ya_v, bd_ref)

        zb = zc_ref[:, 0:CONV_W].astype(F32)
        zc = zc_ref[:, CONV_W:2 * CONV_W].astype(F32)
        zx = zc_ref[:, 2 * CONV_W:3 * CONV_W].astype(F32)
        u = zc * zx
        uh = (zch_ref[:, CONV_W:2 * CONV_W].astype(F32) * zch_ref[:, 2 * CONV_W:3 * CONV_W].astype(F32)) * keep
        cv, u1, u2 = _conv_fwd(u, uh[15:16, :], uh[14:15, :], cw_ref, cb_ref)
        d_cv = d_yc * zb
        d_u = _conv_bwd_input(d_cv, carry_ref[0:1, :], carry_ref[1:2, :], cw_ref)
        carry_ref[...] = d_cv[0:8, :]
        dcw_ref[0:1, :] += jnp.sum(d_cv * u2, axis=0, keepdims=True)
        dcw_ref[1:2, :] += jnp.sum(d_cv * u1, axis=0, keepdims=True)
        dcw_ref[2:3, :] += jnp.sum(d_cv * u, axis=0, keepdims=True)
        dcb_ref[...] += jnp.sum(d_cv, axis=0, keepdims=True)
        dzc_ref[:, 0:CONV_W] = (d_yc * cv).astype(BF16)
        dzc_ref[:, CONV_W:2 * CONV_W] = (d_u * zx).astype(BF16)
        dzc_ref[:, 2 * CONV_W:3 * CONV_W] = (d_u * zc).astype(BF16)

        @pl.when(i == nblk - 1)
        def _():
            dwb_ref[...] = dw_ref[...].astype(BF16)

    def rev(i):
        return nblk - 1 - i

    def blk(c):
        return pl.BlockSpec((tm, c), lambda i: (rev(i), 0))

    in_specs = [
        pl.BlockSpec((FF_CHUNKS, tm, D_MODEL), lambda i: (0, rev(i), 0)),
        blk(D_MODEL), blk(D_MODEL), _full((1, D_MODEL)), _full((D_MODEL, D_MODEL)),
        blk(CONV_W), blk(ATTN_W), _full((1, CONV_W)), _full((1, ATTN_W)),
        blk(3 * CONV_W),
        pl.BlockSpec((16, 3 * CONV_W), lambda i: (jnp.maximum(rev(i) * (tm // 16) - 1, 0), 0)),
        _full((3, CONV_W)), _full((1, CONV_W)), _full((256, 256)),
    ]
    out_specs = [blk(D_MODEL), blk(ATTN_W), blk(ATTN_W), blk(3 * CONV_W), _full((D_MODEL, D_MODEL)),
                 _full((1, D_MODEL)), _full((1, CONV_W)), _full((1, ATTN_W)), _full((3, CONV_W)), _full((1, CONV_W))]
    return pl.pallas_call(
        body, name="outproj_bwd", grid=(nblk,), in_specs=in_specs, out_specs=out_specs,
        out_shape=[jax.ShapeDtypeStruct((t, D_MODEL), F32), jax.ShapeDtypeStruct((t, ATTN_W), F32),
                   jax.ShapeDtypeStruct((t, ATTN_W), F32), jax.ShapeDtypeStruct((t, 3 * CONV_W), BF16),
                   jax.ShapeDtypeStruct((D_MODEL, D_MODEL), BF16), jax.ShapeDtypeStruct((1, D_MODEL), F32),
                   jax.ShapeDtypeStruct((1, CONV_W), F32), jax.ShapeDtypeStruct((1, ATTN_W), F32),
                   jax.ShapeDtypeStruct((3, CONV_W), F32), jax.ShapeDtypeStruct((1, CONV_W), F32)],
        scratch_shapes=[pltpu.VMEM((8, CONV_W), F32), pltpu.VMEM((D_MODEL, D_MODEL), F32)],
        compiler_params=_cparams("arbitrary"),
    )(dh2, dx2, x1, g_ffn, w_out, yc, ya, goc, goa, zconv, zconv, conv_w, conv_b, bd)


def _attn_bwd(q, k, v, dya, lse, dd, e_all, m_all, after):
    t = q.shape[0]
    nsb = t // SUPER

    def body(q_ref, kc_ref, kp_ref, vc_ref, vp_ref, dy_ref, l_ref, d_ref, e_ref, m_ref, after_ref,
             dq_ref, dk_ref, dv_ref, kk, vv, dkacc, dvacc, dwide):
        s = pl.program_id(1)

        @pl.when(s == 0)
        def _():
            dkacc[...] = jnp.zeros_like(dkacc)
            dvacc[...] = jnp.zeros_like(dvacc)

        dkacc[0:SUPER, :] = dkacc[SUPER:, :]
        dvacc[0:SUPER, :] = dvacc[SUPER:, :]
        dkacc[SUPER:, :] = jnp.zeros((SUPER, QK_BLOCK), F32)
        dvacc[SUPER:, :] = jnp.zeros((SUPER, QK_BLOCK), F32)

        @pl.when(s < nsb)
        def _():
            kk[0:SUPER, :] = kp_ref[...]
            kk[SUPER:, :] = kc_ref[...]
            vv[0:SUPER, :] = vp_ref[...]
            vv[SUPER:, :] = vc_ref[...]
            head0 = lax.broadcasted_iota(jnp.int32, (QK_BLOCK, QK_BLOCK), 1) < HEAD_DIM

            def widened(a):
                other = pltpu.roll(a, HEAD_DIM, 1)
                first = lax.broadcasted_iota(jnp.int32, a.shape, 1) < HEAD_DIM
                return jnp.where(first, a, other), jnp.where(first, other, a)

            def stacked(h0, h1):
                return jnp.concatenate([jnp.concatenate([h0, h0], axis=1), jnp.concatenate([h1, h1], axis=1)], axis=0)

            def widen_dd(i, carry):
                rows = pl.ds(pl.multiple_of(i * 256, 256), 256)
                dwide[0, rows, :], dwide[1, rows, :] = widened(d_ref[rows, :])
                return carry

            lax.fori_loop(0, SUPER // 256, widen_dd, 0)

            for b, dil in enumerate(DILATIONS):
                def unit(u, carry, b=b, dil=dil):
                    start = _unit_start(u, dil)
                    first_key = SUPER + start - QK_BLOCK * dil
                    qrows = _rows(start, QK_BLOCK, dil)
                    krows = _rows(first_key, KEYS, dil)
                    q2 = _stack_heads(q_ref[qrows, :].astype(BF16), head0)
                    dy2 = _stack_heads(dy_ref[qrows, :].astype(BF16), head0)
                    g2 = stacked(*widened(jnp.exp(m_ref[b, qrows, :] - l_ref[qrows, :])))
                    d2 = stacked(dwide[0, qrows, :], dwide[1, qrows, :])
                    k2 = kk[krows, :].astype(BF16)
                    v2 = vv[krows, :].astype(BF16)
                    prob = e_ref[b * UNITS + u].astype(F32) * g2
                    ds = (prob * (_mm_nt(dy2, v2) - d2)).astype(BF16)
                    dvacc[krows, :] += _mm_tn(prob.astype(BF16), dy2)
                    dkacc[krows, :] += _mm_tn(ds, q2)
                    dq2 = _mm(ds, k2)
                    dq = jnp.where(head0, dq2[0:QK_BLOCK], dq2[QK_BLOCK:]) * ATTN_SCALE
                    if b == 0:
                        dq_ref[qrows, :] = dq
                    else:
                        dq_ref[qrows, :] += dq
                    return carry

                lax.fori_loop(0, UNITS, unit, 0, unroll=8)

        dk_ref[...] = dkacc[0:SUPER, :]
        dv_ref[...] = dvacc[0:SUPER, :].astype(BF16)

    def cur_map(p, s):
        return (jnp.minimum(s, nsb - 1), p)

    def prev_map(p, s):
        return (jnp.clip(s - 1, 0, nsb - 1), p)

    cur = pl.BlockSpec((SUPER, QK_BLOCK), cur_map)
    prev = pl.BlockSpec((SUPER, QK_BLOCK), prev_map)
    return pl.pallas_call(
        body, name="attn_bwd", grid=(4, nsb + 1),
        in_specs=[cur, cur, prev, cur, prev, cur, cur, cur,
                  pl.BlockSpec((None, None, 3 * UNITS, KEYS, KEYS), lambda p, s: (p, jnp.minimum(s, nsb - 1), 0, 0, 0)),
                  pl.BlockSpec((3, SUPER, QK_BLOCK), lambda p, s: (0, jnp.minimum(s, nsb - 1), p)),
                  pl.BlockSpec(memory_space=pl.ANY)],
        out_specs=[cur, prev, prev],
        out_shape=[jax.ShapeDtypeStruct((t, ATTN_W), F32), jax.ShapeDtypeStruct((t, ATTN_W), F32),
                   jax.ShapeDtypeStruct((t, ATTN_W), BF16)],
        scratch_shapes=[pltpu.VMEM((2 * SUPER, QK_BLOCK), F32)] * 4 + [pltpu.VMEM((2, SUPER, QK_BLOCK), F32)],
        compiler_params=_cparams("parallel", "arbitrary"),
    )(q, k, k, v, v, dya, lse, dd, e_all, m_all, after)


def _inproj_bwd(dq, dk, dv, dzconv, zqk, x, dx1, g_mix, w_in, qg, kg, bd, tm):
    t = x.shape[0]
    nblk = t // tm
    shard = IN_COLS // N_DEV

    def body(dq_ref, dk_ref, dv_ref, dzc_ref, zqk_ref, x_ref, dx1_ref, g_ref, w_ref, qg_ref,
             kg_ref, bd_ref, dx_ref, dw_hbm, dg_ref, dqg_ref, dkg_ref, dw_ref, stage, stage_sem):
        @pl.when(pl.program_id(0) == 0)
        def _():
            for ref in (dw_ref, dg_ref, dqg_ref, dkg_ref):
                ref[...] = jnp.zeros_like(ref)

        parts = [dzc_ref[...]]
        for j, (dn_ref, gain_ref, dgain_ref) in enumerate(((dq_ref, qg_ref, dqg_ref), (dk_ref, kg_ref, dkg_ref))):
            dn = dn_ref[...]
            z = zqk_ref[:, j * ATTN_W:(j + 1) * ATTN_W].astype(F32)
            r = lax.rsqrt(_seg_sum64(z * z, bd_ref) * (1.0 / HEAD_DIM) + EPS)
            zhat = z * r
            dgain_ref[...] += jnp.sum(dn * zhat, axis=0, keepdims=True)
            gd = dn * gain_ref[...]
            parts.append((r * (gd - zhat * (_seg_sum64(gd * zhat, bd_ref) * (1.0 / HEAD_DIM)))).astype(BF16))
        parts.append(dv_ref[...].astype(BF16))
        dz = jnp.concatenate(parts, axis=1)

        r, xhat = _rms_stats(x_ref[...])
        g = g_ref[...]
        dw_ref[...] += _mm_tn((xhat * g).astype(BF16), dz)
        dh = _mm_nt(dz, w_ref[...])
        dg_ref[...] += jnp.sum(dh * xhat, axis=0, keepdims=True)
        dx_ref[...] = dx1_ref[...] + _rms_bwd(dh, xhat, r, g)

        @pl.when(pl.program_id(0) == nblk - 1)
        def _():
            for k in range(N_DEV):
                stage[...] = dw_ref[:, k * shard:(k + 1) * shard].astype(BF16)
                copy = pltpu.make_async_copy(stage, dw_hbm.at[k], stage_sem)
                copy.start()
                copy.wait()

    def blk(c):
        return pl.BlockSpec((tm, c), lambda i: (i, 0))

    return pl.pallas_call(
        body, name="inproj_bwd", grid=(nblk,),
        in_specs=[blk(ATTN_W)] * 3 + [blk(3 * CONV_W), blk(2 * ATTN_W), blk(D_MODEL), blk(D_MODEL), _full((1, D_MODEL)),
                                      _full((D_MODEL, IN_COLS)), _full((1, ATTN_W)), _full((1, ATTN_W)),
                                      _full((256, 256))],
        out_specs=[blk(D_MODEL), ANY, _full((1, D_MODEL)), _full((1, ATTN_W)), _full((1, ATTN_W))],
        out_shape=[jax.ShapeDtypeStruct((t, D_MODEL), F32), jax.ShapeDtypeStruct((N_DEV, D_MODEL, shard), BF16),
                   jax.ShapeDtypeStruct((1, D_MODEL), F32), jax.ShapeDtypeStruct((1, ATTN_W), F32),
                   jax.ShapeDtypeStruct((1, ATTN_W), F32)],
        scratch_shapes=[pltpu.VMEM((D_MODEL, IN_COLS), F32), pltpu.VMEM((D_MODEL, shard), BF16),
                        pltpu.SemaphoreType.DMA],
        compiler_params=_cparams("arbitrary"),
    )(dq, dk, dv, dzconv, zqk, x, dx1, g_mix, w_in, qg, kg, bd)


def _ordered_after(a, token):
    return a if token is None else a + token


def _local_step(x, p, target, w, tms, hooks=None):
    hooks = hooks or {}
    bd = jnp.kron(jnp.eye(4, dtype=F32), jnp.ones((HEAD_DIM, HEAD_DIM), F32)).astype(BF16)
    qg = jnp.tile(w["q_norm_g"], (1, 8))
    kg = jnp.tile(w["k_norm_g"], (1, 8))
    slopes = jnp.exp2(-jnp.arange(1, 9, dtype=F32))
    slopes = jnp.broadcast_to(slopes.reshape(4, 2, 1), (4, 2, QK_BLOCK))

    zconv, zqk, yc, q, k, v = _inproj_fwd(x, w["g_mix"], w["w_in"], w["conv_w"], w["conv_b"], qg, kg, bd, tms[0])
    ya, lse, e_all, m_all = _attn_fwd(q, k, v, slopes)
    if "late_weights" in hooks:
        w = {**w, **hooks["late_weights"](lse)}
    x1 = _outproj_fwd(ya, yc, x, w["g_out_conv"], w["g_out_attn"], w["w_out"], tms[0])
    gp, up, h2, x2 = _ffn_fwd(x1, w["g_ffn"], w["w_gate"], w["w_up"], w["w_down"], w["ffn_conv_w"], w["ffn_conv_b"],
                              tms[1])
    dx2, dx2b, loss, dw_pg, dw_pp, dg_ple = _ple_fwd_bwd(x2, p, target, w["g_ple"], w["w_ple_gate"], w["w_ple_proj"], tms[0])
    dh2, dw_down, dw_up, dw_gate, dfcw, dfcb = _ffn_bwd(dx2b, h2, gp, up, w["w_gate"], w["w_up"], w["w_down"],
                                                        w["ffn_conv_w"], w["ffn_conv_b"], tms[0])
    token = None
    if "ffn_grads" in hooks:
        token = hooks["ffn_grads"]({"w_ple_gate": dw_pg, "w_ple_proj": dw_pp, "w_down": dw_down, "w_up": dw_up,
                                    "w_gate": dw_gate, "ffn_conv_b": dfcb})
    dx1, dya, dd, dzconv, dw_out, dg_ffn, dgoc, dgoa, dcw, dcb = _outproj_bwd(
        dh2, dx2, x1, _ordered_after(w["g_ffn"], token), w["w_out"], yc, ya, w["g_out_conv"], w["g_out_attn"], zconv,
        w["conv_w"], w["conv_b"], bd, tms[1])
    token = hooks["outproj_done"](dx1) if "outproj_done" in hooks else None
    dq, dk, dv = _attn_bwd(q, k, v, dya, lse, dd, e_all, m_all, slopes if token is None else token)
    dx, dw_in, dg_mix, dqg, dkg = _inproj_bwd(dq, dk, dv, dzconv, zqk, x, dx1, w["g_mix"], w["w_in"], qg, kg, bd,
                                              tms[0])
    grads = {
        "g_mix": dg_mix, "w_in": dw_in, "conv_w": dcw, "conv_b": dcb,
        "q_norm_g": dqg, "k_norm_g": dkg,
        "g_out_conv": dgoc, "g_out_attn": dgoa, "w_out": dw_out, "g_ffn": dg_ffn, "w_gate": dw_gate, "w_up": dw_up,
        "ffn_conv_w": dfcw, "ffn_conv_b": dfcb, "w_down": dw_down, "g_ple": dg_ple, "w_ple_gate": dw_pg,
        "w_ple_proj": dw_pp,
    }
    return loss, dx, grads


ANY = pl.BlockSpec(memory_space=pl.ANY)
MESH = pl.DeviceIdType.MESH


def _all_gather(shards, name):
    n = len(shards)

    def body(*refs):
        ins, outs = refs[:n], refs[n:2 * n]
        send_sems, recv_sems, local_sems = refs[2 * n:]
        x, y, c = lax.axis_index("x"), lax.axis_index("y"), lax.axis_index("c")
        me, sibling = (x, y, c), (x, y, 1 - c)
        chips = [(1 - x, y), (x, 1 - y), (1 - x, 1 - y)]

        def slot(dev):
            return 4 * dev[0] + 2 * dev[1] + dev[2]

        def copy(b, k, block, to, src=None):
            dst = outs[b].at[slot(block)]
            return pltpu.make_async_remote_copy(
                src_ref=dst if src is None else src, dst_ref=dst, send_sem=send_sems.at[b, k],
                recv_sem=recv_sems.at[b, k], device_id=to, device_id_type=MESH)

        mine = [pltpu.make_async_copy(ins[b], outs[b].at[slot(me)], local_sems.at[b]) for b in range(n)]
        first, passed = [], []
        for b in range(n):
            mine[b].start()
            first.append(copy(b, 0, me, sibling, src=ins[b]))
            first += [copy(b, 1 + j, me, (*chip, c), src=ins[b]) for j, chip in enumerate(chips)]
        for cp in first:
            cp.start()
        for j, chip in enumerate(chips):
            for b in range(n):
                copy(b, 1 + j, (*chip, c), me).wait_recv()
                fwd = copy(b, 4 + j, (*chip, c), sibling)
                fwd.start()
                passed.append(fwd)
        for b in range(n):
            copy(b, 0, sibling, me).wait_recv()
            for j, chip in enumerate(chips):
                copy(b, 4 + j, (*chip, 1 - c), me).wait_recv()
        for cp in first + passed:
            cp.wait_send()
        for cp in mine:
            cp.wait()

    return pl.pallas_call(
        body, name=name,
        in_specs=[ANY] * n, out_specs=[ANY] * n,
        out_shape=[jax.ShapeDtypeStruct((N_DEV,) + s.shape, s.dtype) for s in shards],
        scratch_shapes=[pltpu.SemaphoreType.DMA((n, 7)), pltpu.SemaphoreType.DMA((n, 7)),
                        pltpu.SemaphoreType.DMA((n,))],
    )(*shards)


HBM = pl.BlockSpec(memory_space=pltpu.HBM)
SEM = pl.BlockSpec(memory_space=pltpu.SEMAPHORE)
EFFECT = pltpu.SideEffectType.DATAFLOW_SIDE_EFFECTING
FLIPS = ((0, 0, 1), (0, 1, 0), (0, 1, 1), (1, 0, 0), (1, 0, 1), (1, 1, 0), (1, 1, 1))


def _flip_peers():
    pos = (lax.axis_index("x"), lax.axis_index("y"), lax.axis_index("c"))
    return [tuple(1 - a if f else a for a, f in zip(pos, flip)) for flip in FLIPS]


def _hbm(a):
    return pltpu.with_memory_space_constraint(a, pltpu.HBM)


def _own_copies(own, src_refs, land_refs, send_sems, n_remote):
    return [pltpu.make_async_copy(src, dst, send_sems.at[n_remote + i])
            for i, (src, dst) in enumerate(own(src_refs, land_refs) if own else [])]


def _split_start(name, srcs, lands, plan, n_copies, after, own=None):
    n, m = len(srcs), len(lands)

    def body(*refs):
        send_sems, recv_sems, token = refs[n + m + 1], refs[n + m + 2], refs[-1]
        for i, (src, dst, peer) in enumerate(plan(refs[:n], refs[n:n + m])):
            pltpu.make_async_remote_copy(src_ref=src, dst_ref=dst, send_sem=send_sems.at[i], recv_sem=recv_sems.at[i],
                                         device_id=peer, device_id_type=MESH).start()
        for copy in _own_copies(own, refs[:n], refs[n:n + m], send_sems, n_copies):
            copy.start()
        token[...] = jnp.zeros_like(token)

    outs = pl.pallas_call(
        body, name=name + "_start",
        in_specs=[HBM] * (n + m) + [ANY],
        out_specs=[SEM, SEM] + [HBM] * (n + m) + [pl.BlockSpec(memory_space=pltpu.VMEM)],
        out_shape=[pltpu.SemaphoreType.DMA((n_copies + (n if own else 0),)), pltpu.SemaphoreType.DMA((n_copies,))]
        + [pltpu.HBM(a.shape, a.dtype) for a in list(srcs) + list(lands)] + [jax.ShapeDtypeStruct((1, D_MODEL), F32)],
        input_output_aliases={i: 2 + i for i in range(n + m)},
        compiler_params=pltpu.CompilerParams(has_side_effects=EFFECT),
    )(*[_hbm(a) for a in list(srcs) + list(lands)], after)
    return (outs[0], outs[1], outs[2:2 + n], outs[2 + n:2 + n + m]), outs[-1]


def _split_wait(name, started, plan, after, own=None):
    send_sems, recv_sems, srcs, lands = started
    n, m = len(srcs), len(lands)

    def body(*refs):
        send_ref, recv_ref = refs[n + m], refs[n + m + 1]
        copies = plan(refs[:n], refs[n:n + m])
        for i, (src, dst, peer) in enumerate(copies):
            copy = pltpu.make_async_remote_copy(src_ref=src, dst_ref=dst, send_sem=send_ref.at[i],
                                                recv_sem=recv_ref.at[i], device_id=peer, device_id_type=MESH)
            copy.wait_send()
            copy.wait_recv()
        for copy in _own_copies(own, refs[:n], refs[n:n + m], send_ref, len(copies)):
            copy.wait()

    outs = pl.pallas_call(
        body, name=name + "_wait",
        in_specs=[HBM] * (n + m) + [SEM, SEM, ANY],
        out_specs=[HBM] * (n + m),
        out_shape=[pltpu.HBM(a.shape, a.dtype) for a in list(srcs) + list(lands)],
        input_output_aliases={i: i for i in range(n + m)},
        compiler_params=pltpu.CompilerParams(has_side_effects=EFFECT),
    )(*srcs, *lands, send_sems, recv_sems, after)
    return outs[:n], outs[n:]


def _gather_plan(srcs, lands):
    slot = 4 * lax.axis_index("x") + 2 * lax.axis_index("y") + lax.axis_index("c")
    return [(src, land.at[slot], peer) for src, land in zip(srcs, lands) for peer in _flip_peers()]


def _own_slot(srcs, lands):
    slot = 4 * lax.axis_index("x") + 2 * lax.axis_index("y") + lax.axis_index("c")
    return [(src, land.at[slot]) for src, land in zip(srcs, lands)]


def _sibling_plan(srcs, lands):
    x, y, c = lax.axis_index("x"), lax.axis_index("y"), lax.axis_index("c")
    return [(src.at[k, 1 - c], land.at[k], (x, y, 1 - c)) for src, land in zip(srcs, lands) for k in range(N_CHIP)]


def _chip_plan(srcs, lands):
    x, y, c = lax.axis_index("x"), lax.axis_index("y"), lax.axis_index("c")
    return [(src.at[2 * cx + cy], land.at[2 * x + y], (cx, cy, c))
            for src, land in zip(srcs, lands) for cx, cy in ((1 - x, y), (x, 1 - y), (1 - x, 1 - y))]


def _row_tile(rows):
    for tr in range(min(rows, 512), 15, -16):
        if rows % tr == 0:
            return tr
    return rows


def _pair_sums(gs, lands, core, name):
    n = len(gs)

    def body(c_ref, *refs):
        for b in range(n):
            out = refs[2 * n + b]
            out[...] = (refs[b][...].astype(F32) + refs[n + b][...].astype(F32)).astype(out.dtype)

    def slab(a):
        return pl.BlockSpec((None,) + a.shape[1:], lambda k, c_ref: (k, 0, 0))

    return pl.pallas_call(
        body, name=name,
        grid_spec=pltpu.PrefetchScalarGridSpec(
            num_scalar_prefetch=1, grid=(N_CHIP,),
            in_specs=[pl.BlockSpec((None, None) + g.shape[2:], lambda k, c_ref: (k, c_ref[0], 0, 0)) for g in gs]
            + [slab(a) for a in lands],
            out_specs=[slab(a) for a in lands]),
        out_shape=[jax.ShapeDtypeStruct(a.shape, a.dtype) for a in lands],
        compiler_params=_cparams("parallel"),
    )(core, *gs, *lands)


def _adamw(own, arrived, chip, w, m, v, name):
    k, rows, cols = arrived.shape
    tr = _row_tile(rows)
    c1 = 1.0 / (1.0 - ADAM_B1 ** ADAM_STEP)
    c2 = 1.0 / (1.0 - ADAM_B2 ** ADAM_STEP)

    def body(chip_ref, o_ref, p_ref, w_ref, m_ref, v_ref, g_ref, d_ref, nm_ref, nv_ref):
        def slab(j):
            return jnp.where(chip_ref[0] == j, o_ref[j], p_ref[j]).astype(F32)

        g = slab(0)
        for j in range(1, k):
            g = g + slab(j)
        g_ref[...] = g
        nm = ADAM_B1 * m_ref[...] + (1.0 - ADAM_B1) * g
        nv = ADAM_B2 * v_ref[...] + (1.0 - ADAM_B2) * (g * g)
        nm_ref[...] = nm
        nv_ref[...] = nv
        d_ref[...] = -ADAM_LR * ((nm * c1) / (jnp.sqrt(nv * c2) + ADAM_EPS) + ADAM_WD * w_ref[...])

    blk = pl.BlockSpec((tr, cols), lambda i, c: (i, 0))
    stack = pl.BlockSpec((k, tr, cols), lambda i, c: (0, i, 0))
    return pl.pallas_call(
        body, name=name,
        grid_spec=pltpu.PrefetchScalarGridSpec(num_scalar_prefetch=1, grid=(rows // tr,),
                                               in_specs=[stack, stack, blk, blk, blk], out_specs=[blk] * 4),
        out_shape=[jax.ShapeDtypeStruct((rows, cols), F32)] * 4,
        compiler_params=_cparams("parallel"),
    )(chip, own, arrived, w, m, v)


SMALL_LAYOUT = (("g_mix", 0, 1024), ("conv_b", 1, 512), ("q_norm_g", 2, 64), ("k_norm_g", 3, 64),
                ("g_out_conv", 4, 512), ("g_out_attn", 5, 512), ("g_ffn", 6, 1024), ("ffn_conv_b", 7, 2816),
                ("g_ple", 10, 1024))
CONV_W_ROW = 11
FFN_CONV_W_ROW = 14
LOSS_ROW = 23


def _row_pieces(cols):
    return [(c, min(1024, cols - c)) for c in range(0, cols, 1024)]


def _pack_small(grads, loss_tile):
    names = [n for n, _, _ in SMALL_LAYOUT]

    def body(*refs):
        ins, cw_ref, fcw_ref, loss_ref, out_ref = refs[:len(names)], refs[-4], refs[-3], refs[-2], refs[-1]
        out_ref[...] = jnp.zeros_like(out_ref)
        for ref, (_, row, cols) in zip(ins, SMALL_LAYOUT):
            if ref.shape[1] == ATTN_W and cols == HEAD_DIM:
                out_ref[row:row + 1, 0:cols] = sum(ref[:, h:h + cols] for h in range(0, ATTN_W, cols))
                continue
            for j, (c, width) in enumerate(_row_pieces(cols)):
                out_ref[row + j:row + j + 1, 0:width] = ref[:, c:c + width]
        for k in range(3):
            out_ref[CONV_W_ROW + k:CONV_W_ROW + k + 1, 0:CONV_W] = cw_ref[k:k + 1, :]
            for j, (c, width) in enumerate(_row_pieces(D_FF)):
                row = FFN_CONV_W_ROW + 3 * k + j
                out_ref[row:row + 1, 0:width] = fcw_ref[k:k + 1, c:c + width]
        out_ref[LOSS_ROW:LOSS_ROW + 1, 0:128] = loss_ref[0:1, :]

    return pl.pallas_call(
        body, name="pack_small_grads", out_shape=jax.ShapeDtypeStruct((SMALL_ROWS, 1024), F32),
    )(*[grads[n] for n in names], grads["conv_w"], grads["ffn_conv_w"], loss_tile)


def _adamw_small(arrived, conv_parts, fconv_parts, wts, mom, var):
    names = [n for n, _, _ in SMALL_LAYOUT] + ["conv_w", "ffn_conv_w"]
    c1 = 1.0 / (1.0 - ADAM_B1 ** ADAM_STEP)
    c2 = 1.0 / (1.0 - ADAM_B2 ** ADAM_STEP)
    n = len(names)

    def body(*refs):
        land, cw_ref, fcw_ref = refs[0], refs[1], refs[2]
        state = refs[3:3 + 3 * n]
        outs = refs[3 + 3 * n:]

        def total(piece):
            acc = piece(0)
            for d in range(1, N_DEV):
                acc = acc + piece(d)
            return acc

        for i, name in enumerate(names):
            if name == "conv_w":
                g = total(lambda d: cw_ref[d])
            elif name == "ffn_conv_w":
                g = total(lambda d: fcw_ref[d])
            else:
                _, row, cols = SMALL_LAYOUT[i]
                pieces = [total(lambda d, j=j, width=width: land[d, row + j:row + j + 1, 0:width])
                          for j, (_, width) in enumerate(_row_pieces(cols))]
                g = pieces[0] if len(pieces) == 1 else jnp.concatenate(pieces, axis=1)
            w_ref, m_ref, v_ref = state[3 * i:3 * i + 3]
            nm = ADAM_B1 * m_ref[...] + (1.0 - ADAM_B1) * g
            nv = ADAM_B2 * v_ref[...] + (1.0 - ADAM_B2) * (g * g)
            outs[4 * i][...] = g
            outs[4 * i + 1][...] = -ADAM_LR * ((nm * c1) / (jnp.sqrt(nv * c2) + ADAM_EPS) + ADAM_WD * w_ref[...])
            outs[4 * i + 2][...] = nm
            outs[4 * i + 3][...] = nv
        outs[-1][...] = total(lambda d: land[d, LOSS_ROW:LOSS_ROW + 1, 0:128])

    state = [a[nm_] for nm_ in names for a in (wts, mom, var)]
    shapes = [jax.ShapeDtypeStruct(wts[nm_].shape, F32) for nm_ in names for _ in range(4)]
    outs = pl.pallas_call(
        body, name="adamw_small", out_shape=shapes + [jax.ShapeDtypeStruct((1, 128), F32)],
    )(arrived, conv_parts, fconv_parts, *state)
    return {nm_: tuple(outs[4 * i:4 * i + 4]) for i, nm_ in enumerate(names)}, outs[-1][0, 0]


COL_SHARDED = ("w_in", "w_ple_proj")
TRANSPOSED = ("w_gate", "w_up")
CONV_SHARDED = (("conv_w", CONV_W), ("ffn_conv_w", D_FF))


def _gathered_to_full(name, gathered):
    if name in COL_SHARDED:
        return gathered.transpose(1, 0, 2).reshape(gathered.shape[1], -1)
    return gathered.reshape(-1, gathered.shape[2])


def _full_to_stacked(name, grad, shard_shape):
    sr, sc = shard_shape
    if grad.ndim == 3:
        a = grad
    elif name in COL_SHARDED:
        a = grad.reshape(sr, N_DEV, sc).transpose(1, 0, 2)
    else:
        a = grad.reshape(N_DEV, sr, sc)
    return a.astype(BF16).reshape(N_CHIP, 2, sr, sc)


def _pad_rows(vec, rows):
    return jnp.pad(vec, (0, rows * 1024 - vec.shape[0])).reshape(rows, 1024)


def kernel(x, p, g_mix, w_in, conv_w, conv_b, q_norm_g, k_norm_g, g_out_conv, g_out_attn, w_out, g_ffn, w_gate, w_up, ffn_conv_w, ffn_conv_b, w_down, g_ple, w_ple_gate, w_ple_proj, loss_target, m_g_mix, m_w_in, m_conv_w, m_conv_b, m_q_norm_g, m_k_norm_g, m_g_out_conv, m_g_out_attn, m_w_out, m_g_ffn, m_w_gate, m_w_up, m_ffn_conv_w, m_ffn_conv_b, m_w_down, m_g_ple, m_w_ple_gate, m_w_ple_proj, v_g_mix, v_w_in, v_conv_w, v_conv_b, v_q_norm_g, v_k_norm_g, v_g_out_conv, v_g_out_attn, v_w_out, v_g_ffn, v_w_gate, v_w_up, v_ffn_conv_w, v_ffn_conv_b, v_w_down, v_g_ple, v_w_ple_gate, v_w_ple_proj):
    args = dict(locals())
    names = ["g_mix", "w_in", "conv_w", "conv_b", "q_norm_g", "k_norm_g", "g_out_conv", "g_out_attn", "w_out", "g_ffn",
             "w_gate", "w_up", "ffn_conv_w", "ffn_conv_b", "w_down", "g_ple", "w_ple_gate", "w_ple_proj"]
    big = list(BIG)
    conv = [n for n, _ in CONV_SHARDED]

    def local(prefix):
        out = {n: (args[prefix + n][0] if n in big or n in conv else args[prefix + n]) for n in names}
        out.update({n: out[n].T for n in TRANSPOSED})
        return out

    wts, mom, var = local(""), local("m_"), local("v_")
    shard_shapes = {n: wts[n].shape for n in big}
    dev = 4 * lax.axis_index("x") + 2 * lax.axis_index("y") + lax.axis_index("c")
    core = lax.axis_index("c").astype(jnp.int32).reshape(1)

    conv_local = _pad_rows(jnp.concatenate([wts[n].reshape(-1) for n in conv]), 8).reshape(8, 1024)
    late = [n for n in big if n != "w_in"]
    w_in_all, conv_all = _all_gather([wts["w_in"].astype(BF16), conv_local], "gather_weights")
    late_shards = [wts[n].astype(BF16) for n in late]
    gathering, token = _split_start("gather_late_weights", late_shards,
                                    [lax.empty((N_DEV,) + s.shape, BF16) for s in late_shards], _gather_plan,
                                    7 * len(late), w_in_all, own=_own_slot)
    full = dict(wts)
    full["w_in"] = _gathered_to_full("w_in", w_in_all)
    full["g_mix"] = _ordered_after(wts["g_mix"], token)
    flying = {}

    def late_weights(after):
        _, lands = _split_wait("gather_late_weights", gathering, _gather_plan, after, own=_own_slot)
        return {n: _gathered_to_full(n, land) for n, land in zip(late, lands)}

    early = ["w_ple_gate", "w_ple_proj", "w_down", "w_up", "w_gate"]

    def ffn_grads(g):
        stacked = [_full_to_stacked(n, g[n], shard_shapes[n]) for n in early]
        flying["sibling"], tok = _split_start("rs_sibling_early", stacked,
                                              [lax.empty((N_CHIP,) + s.shape[2:], BF16) for s in stacked],
                                              _sibling_plan, N_CHIP * len(early), g["ffn_conv_b"])
        return tok

    def outproj_done(after):
        stacked, landed = _split_wait("rs_sibling_early", flying["sibling"], _sibling_plan, after)
        parts = _pair_sums(stacked, landed, core, "rs_pair_sums_early")
        flying["chip"], tok = _split_start("rs_chip_early", parts, [lax.empty(q.shape, BF16) for q in parts],
                                           _chip_plan, 3 * len(early), landed[0])
        return tok

    off = 0
    for n, width in CONV_SHARDED:
        sc = width // N_DEV
        a = conv_all.reshape(N_DEV, -1)[:, off:off + 3 * sc].reshape(N_DEV, 3, sc)
        full[n] = a.transpose(1, 0, 2).reshape(3, width)
        off += 3 * sc

    loss, dx, grads = _local_step(x[0], p[0, 0], loss_target[0], full, (512, 256),
                                  {"late_weights": late_weights, "ffn_grads": ffn_grads, "outproj_done": outproj_done})

    chip = (2 * lax.axis_index("x") + lax.axis_index("y")).astype(jnp.int32).reshape(1)

    def adamw_of(group, parts, arrived):
        return {n: _adamw(own, got, chip, wts[n], mom[n], var[n], f"adamw_{n}")
                for n, own, got in zip(group, parts, arrived)}

    last = [n for n in big if n not in early]
    stacked = [_full_to_stacked(n, grads[n], shard_shapes[n]) for n in last]
    flying["sibling_last"], tok = _split_start("rs_sibling_last", stacked,
                                               [lax.empty((N_CHIP,) + s.shape[2:], BF16) for s in stacked],
                                               _sibling_plan, N_CHIP * len(last), dx)
    packed = _pack_small(grads, loss)
    flying["small"], tok = _split_start("gather_small_grads", [packed], [lax.empty((N_DEV,) + packed.shape, F32)],
                                        _gather_plan, N_DEV - 1, tok, own=_own_slot)
    stacked, landed = _split_wait("rs_sibling_last", flying["sibling_last"], _sibling_plan, tok)
    parts = _pair_sums(stacked, landed, core, "rs_pair_sums_last")
    flying["chip_last"], tok = _split_start("rs_chip_last", parts, [lax.empty(q.shape, BF16) for q in parts],
                                            _chip_plan, 3 * len(last), landed[0])

    parts, arrived = _split_wait("rs_chip_early", flying["chip"], _chip_plan, tok)
    out = adamw_of(early, parts, arrived)
    _, (small_all,) = _split_wait("gather_small_grads", flying["small"], _gather_plan, out[early[-1]][0],
                                  own=_own_slot)
    taps = small_all[:, CONV_W_ROW:CONV_W_ROW + 3, 0:CONV_W]
    ftaps = small_all[:, FFN_CONV_W_ROW:FFN_CONV_W_ROW + 9, :].reshape(N_DEV, 3, 3 * 1024)
    small_out, loss_total = _adamw_small(
        small_all, lax.dynamic_slice(taps, (0, 0, dev * (CONV_W // N_DEV)), (N_DEV, 3, CONV_W // N_DEV)),
        lax.dynamic_slice(ftaps, (0, 0, dev * (D_FF // N_DEV)), (N_DEV, 3, D_FF // N_DEV)), wts, mom, var)
    out.update(small_out)
    parts, arrived = _split_wait("rs_chip_last", flying["chip_last"], _chip_plan, small_out["g_mix"][0])
    out.update(adamw_of(last, parts, arrived))
    def result(n, which):
        a = out[n][which]
        return (a.T if n in TRANSPOSED else a).reshape(args[n].shape)

    return (loss_total, dx[None], *[result(n, which) for which in range(4) for n in names])
```

```python
import jax
import jax.numpy as jnp
import numpy as np
from jax import lax
from jax.experimental import pallas as pl
from jax.experimental.pallas import tpu as pltpu

F32 = jnp.float32
BF16 = jnp.bfloat16

D_MODEL = 1024
CONV_W = 512
ATTN_W = 512
HEAD_DIM = 64
D_FF = 2816
PLE_DIM = 256
IN_COLS = 3 * CONV_W + 3 * ATTN_W
EPS = 1e-6
QK_BLOCK = 128
DILATIONS = (1, 4, 16)
ATTN_SCALE = HEAD_DIM ** -0.5

ADAM_LR = 0.001
ADAM_B1 = 0.9
ADAM_B2 = 0.999
ADAM_EPS = 1e-08
ADAM_WD = 0.01
ADAM_STEP = 10

N_DEV = 8
N_CHIP = 4
V7X_VMEM_LIMIT = 56 * 1024 * 1024
V7X_VMEM_LIMIT_LARGE = 62 * 1024 * 1024
FF_CHUNKS = 2
FFN_BWD_PARTS = 1

BIG = ("w_in", "w_out", "w_gate", "w_up", "w_down", "w_ple_gate", "w_ple_proj")
SMALL_ROWS = 24


def _cparams(*sem, vmem=V7X_VMEM_LIMIT):
    return pltpu.CompilerParams(dimension_semantics=sem, vmem_limit_bytes=vmem)


def _mm(a, b):
    return jnp.dot(a, b, preferred_element_type=F32)


def _mm_nt(a, b):
    return lax.dot_general(a, b, (((1,), (1,)), ((), ())), preferred_element_type=F32)


def _mm_tn(a, b):
    return lax.dot_general(a, b, (((0,), (0,)), ((), ())), preferred_element_type=F32)


def _full(shape):
    nd = len(shape)
    return pl.BlockSpec(shape, lambda *_: (0,) * nd)


def _rms_stats(x):
    r = lax.rsqrt(jnp.mean(x * x, axis=-1, keepdims=True) + EPS)
    return r, x * r


def _rms_bwd(dy, xhat, r, g):
    gd = dy * g
    return r * (gd - xhat * jnp.mean(gd * xhat, axis=-1, keepdims=True))


def _seg_sum64(v, bd_ref):
    outs = []
    for c in range(0, v.shape[1], 256):
        vc = v[:, c:c + 256]
        hi = vc.astype(BF16)
        lo = (vc - hi.astype(F32)).astype(BF16)
        outs.append(_mm(hi, bd_ref[...]) + _mm(lo, bd_ref[...]))
    return outs[0] if len(outs) == 1 else jnp.concatenate(outs, axis=1)


def _shift_rows(u, k, edge_rows):
    out = pltpu.roll(u, k, 0)
    row = lax.broadcasted_iota(jnp.int32, (8, u.shape[1]), 0)
    head = out[0:8]
    for j in range(k):
        head = jnp.where(row == j, edge_rows[k - 1 - j], head)
    return jnp.concatenate([head, out[8:]], axis=0)


def _shift_rows_up(u, k, edge_rows):
    n = u.shape[0]
    out = pltpu.roll(u, n - k, 0)
    row = lax.broadcasted_iota(jnp.int32, (8, u.shape[1]), 0)
    tail = out[n - 8:n]
    for j in range(k):
        tail = jnp.where(row == 8 - k + j, edge_rows[j], tail)
    return jnp.concatenate([out[0:n - 8], tail], axis=0)


def _conv_fwd(u, c1, c2, w_ref, b_ref):
    u1 = _shift_rows(u, 1, (c1,))
    u2 = _shift_rows(u, 2, (c1, c2))
    y = u2 * w_ref[0:1, :] + u1 * w_ref[1:2, :] + u * w_ref[2:3, :] + b_ref[...]
    return y, u1, u2


def _conv_bwd_input(dy, n1row, n2row, w_ref):
    d1 = _shift_rows_up(dy, 1, (n1row,))
    d2 = _shift_rows_up(dy, 2, (n1row, n2row))
    return dy * w_ref[2:3, :] + d1 * w_ref[1:2, :] + d2 * w_ref[0:1, :]


def _sigmoid(x):
    return 1.0 / (1.0 + jnp.exp(-x))


def _inproj_fwd(x, g_mix, w_in, conv_w, conv_b, qg, kg, bd, tm):
    t = x.shape[0]

    def body(x_ref, g_ref, w_ref, cw_ref, cb_ref, qg_ref, kg_ref, bd_ref,
             zc_ref, zqk_ref, yc_ref, q_ref, k_ref, v_ref, carry_ref):
        @pl.when(pl.program_id(0) == 0)
        def _():
            carry_ref[...] = jnp.zeros_like(carry_ref)

        _, xhat = _rms_stats(x_ref[...])
        h = (xhat * g_ref[...]).astype(BF16)
        zconv = _mm(h, w_ref[:, 0:3 * CONV_W])
        zc_ref[...] = zconv.astype(BF16)
        u = zconv[:, CONV_W:2 * CONV_W] * zconv[:, 2 * CONV_W:3 * CONV_W]
        cv, _, _ = _conv_fwd(u, carry_ref[7:8, :], carry_ref[6:7, :], cw_ref, cb_ref)
        yc_ref[...] = (zconv[:, 0:CONV_W] * cv).astype(BF16)
        carry_ref[...] = u[tm - 8:tm, :]

        zqk = _mm(h, w_ref[:, 3 * CONV_W:3 * CONV_W + 2 * ATTN_W])
        zqk_ref[...] = zqk.astype(BF16)
        for j, (gain_ref, out_ref, scale) in enumerate(((qg_ref, q_ref, ATTN_SCALE), (kg_ref, k_ref, 1.0))):
            z = zqk[:, j * ATTN_W:(j + 1) * ATTN_W]
            r = lax.rsqrt(_seg_sum64(z * z, bd_ref) * (1.0 / HEAD_DIM) + EPS)
            out_ref[...] = z * r * gain_ref[...] * scale
        v_ref[...] = _mm(h, w_ref[:, 3 * CONV_W + 2 * ATTN_W:IN_COLS])

    def blk(c):
        return pl.BlockSpec((tm, c), lambda i: (i, 0))

    return pl.pallas_call(
        body, name="inproj_fwd", grid=(t // tm,),
        in_specs=[blk(D_MODEL), _full((1, D_MODEL)), _full((D_MODEL, IN_COLS)), _full((3, CONV_W)),
                  _full((1, CONV_W)), _full((1, ATTN_W)), _full((1, ATTN_W)), _full((256, 256))],
        out_specs=[blk(3 * CONV_W), blk(2 * ATTN_W), blk(CONV_W), blk(ATTN_W), blk(ATTN_W), blk(ATTN_W)],
        out_shape=[jax.ShapeDtypeStruct((t, 3 * CONV_W), BF16), jax.ShapeDtypeStruct((t, 2 * ATTN_W), BF16),
                   jax.ShapeDtypeStruct((t, CONV_W), BF16), jax.ShapeDtypeStruct((t, ATTN_W), F32),
                   jax.ShapeDtypeStruct((t, ATTN_W), F32), jax.ShapeDtypeStruct((t, ATTN_W), F32)],
        scratch_shapes=[pltpu.VMEM((8, CONV_W), F32)],
        compiler_params=_cparams("arbitrary"),
    )(x, g_mix, w_in, conv_w, conv_b, qg, kg, bd)


SUPER = 16 * QK_BLOCK
KEYS = 2 * QK_BLOCK
UNITS = SUPER // QK_BLOCK


def _rows(start, size, dil):
    return pl.ds(start, size) if dil == 1 else pl.ds(start, size, stride=dil)


def _attn_bias(sl_ref, dil):
    qi = lax.broadcasted_iota(jnp.int32, (KEYS, KEYS), 0)
    kj = lax.broadcasted_iota(jnp.int32, (KEYS, KEYS), 1)
    step = jnp.bitwise_and(qi, QK_BLOCK - 1) + QK_BLOCK - kj
    slope = jnp.where(qi < QK_BLOCK, sl_ref[0, 0:1, 0:1], sl_ref[0, 1:2, 0:1])
    bias = jnp.where(jnp.logical_and(step >= 0, step <= QK_BLOCK), -slope * (step * dil).astype(F32), -jnp.inf)
    return bias, kj >= QK_BLOCK


def _unit_start(u, dil):
    if dil == 1:
        return pl.multiple_of(u * QK_BLOCK, QK_BLOCK)
    if dil == 4:
        return jnp.bitwise_and(u, 3) + (u // 4) * (4 * QK_BLOCK)
    return u


def _stack_heads(a, head0):
    zero = jnp.zeros_like(a)
    return jnp.concatenate([jnp.where(head0, a, zero), jnp.where(head0, zero, a)], axis=0)


def _attn_fwd(q, k, v, slopes):
    t = q.shape[0]
    nsb = t // SUPER

    def body(q_ref, kc_ref, kp_ref, vc_ref, vp_ref, sl_ref, o_ref, l_ref, e_ref, m_ref, kk, vv, ob, lb):
        s = pl.program_id(1)
        kk[0:SUPER, :] = kp_ref[...]
        kk[SUPER:, :] = kc_ref[...]
        vv[0:SUPER, :] = vp_ref[...]
        vv[SUPER:, :] = vc_ref[...]
        head0 = lax.broadcasted_iota(jnp.int32, (QK_BLOCK, QK_BLOCK), 1) < HEAD_DIM

        for b, dil in enumerate(DILATIONS):
            bias, own_half = _attn_bias(sl_ref, dil)

            def unit(u, carry, b=b, dil=dil, bias=bias, own_half=own_half):
                start = _unit_start(u, dil)
                first_key = SUPER + start - QK_BLOCK * dil
                q2 = _stack_heads(q_ref[_rows(start, QK_BLOCK, dil), :].astype(BF16), head0)
                k2 = kk[_rows(first_key, KEYS, dil), :].astype(BF16)
                v2 = vv[_rows(first_key, KEYS, dil), :].astype(BF16)
                has_prev = jnp.logical_or(s > 0, start >= QK_BLOCK * dil)
                sc = jnp.where(jnp.logical_or(own_half, has_prev), _mm_nt(q2, k2) + bias, -jnp.inf)
                m = jnp.max(sc, axis=-1, keepdims=True)
                e = jnp.exp(sc - m)
                den = jnp.sum(e, axis=-1, keepdims=True)
                eb = e.astype(BF16)
                e_ref[b * UNITS + u] = eb
                o2 = _mm(eb, v2) / den
                l2 = m + jnp.log(den)
                ob[b, _rows(start, QK_BLOCK, dil), :] = jnp.where(head0, o2[0:QK_BLOCK], o2[QK_BLOCK:])
                lb[b, _rows(start, QK_BLOCK, dil), :] = jnp.where(head0, l2[0:QK_BLOCK], l2[QK_BLOCK:])
                m_ref[b, _rows(start, QK_BLOCK, dil), :] = jnp.where(head0, m[0:QK_BLOCK], m[QK_BLOCK:])
                return carry

            lax.fori_loop(0, UNITS, unit, 0, unroll=16)

        def merge(i, carry):
            rows = pl.ds(pl.multiple_of(i * 256, 256), 256)
            la, lb_, lc = lb[0, rows, :], lb[1, rows, :], lb[2, rows, :]
            mx = jnp.maximum(jnp.maximum(la, lb_), lc)
            wa, wb, wc = jnp.exp(la - mx), jnp.exp(lb_ - mx), jnp.exp(lc - mx)
            sw = wa + wb + wc
            o_ref[rows, :] = ((wa * ob[0, rows, :] + wb * ob[1, rows, :] + wc * ob[2, rows, :]) / sw).astype(BF16)
            l_ref[rows, :] = mx + jnp.log(sw)
            return carry

        lax.fori_loop(0, SUPER // 256, merge, 0)

    cur = pl.BlockSpec((SUPER, QK_BLOCK), lambda p, s: (s, p))
    prev = pl.BlockSpec((SUPER, QK_BLOCK), lambda p, s: (jnp.maximum(s - 1, 0), p))
    return pl.pallas_call(
        body, name="attn_fwd", grid=(4, nsb),
        in_specs=[cur, cur, prev, cur, prev, pl.BlockSpec((1, 2, QK_BLOCK), lambda p, s: (p, 0, 0))],
        out_specs=[cur, cur, pl.BlockSpec((None, None, 3 * UNITS, KEYS, KEYS), lambda p, s: (p, s, 0, 0, 0)),
                   pl.BlockSpec((3, SUPER, QK_BLOCK), lambda p, s: (0, s, p))],
        out_shape=[jax.ShapeDtypeStruct((t, ATTN_W), BF16), jax.ShapeDtypeStruct((t, ATTN_W), F32),
                   jax.ShapeDtypeStruct((4, nsb, 3 * UNITS, KEYS, KEYS), BF16),
                   jax.ShapeDtypeStruct((3, t, ATTN_W), F32)],
        scratch_shapes=[pltpu.VMEM((2 * SUPER, QK_BLOCK), F32), pltpu.VMEM((2 * SUPER, QK_BLOCK), F32),
                        pltpu.VMEM((3, SUPER, QK_BLOCK), F32), pltpu.VMEM((3, SUPER, QK_BLOCK), F32)],
        compiler_params=_cparams("parallel", "arbitrary"),
    )(q, k, k, v, v, slopes)


def _outproj_fwd(ya, yc, x, goc, goa, w_out, tm):
    t = x.shape[0]

    def body(ya_ref, yc_ref, x_ref, goc_ref, goa_ref, w_ref, x1_ref):
        _, ychat = _rms_stats(yc_ref[...].astype(F32))
        _, yahat = _rms_stats(ya_ref[...].astype(F32))
        acc = _mm((ychat * goc_ref[...]).astype(BF16), w_ref[0:CONV_W, :])
        acc += _mm((yahat * goa_ref[...]).astype(BF16), w_ref[CONV_W:, :])
        x1_ref[...] = x_ref[...] + acc

    def blk(c):
        return pl.BlockSpec((tm, c), lambda i: (i, 0))

    return pl.pallas_call(
        body, name="outproj_fwd", grid=(t // tm,),
        in_specs=[blk(ATTN_W), blk(CONV_W), blk(D_MODEL), _full((1, CONV_W)), _full((1, ATTN_W)),
                  _full((D_MODEL, D_MODEL))],
        out_specs=blk(D_MODEL),
        out_shape=jax.ShapeDtypeStruct((t, D_MODEL), F32),
        compiler_params=_cparams("parallel"),
    )(ya, yc, x, goc, goa, w_out)


def _ffn_fwd(x1, g_ffn, w_gate_t, w_up_t, w_down, fcw, fcb, tm):
    t = x1.shape[0]

    def body(x_ref, g_ref, wg_ref, wu_ref, wd_ref, cw_ref, cb_ref, gp_ref, up_ref, h_ref, x2_ref, carry_ref):
        @pl.when(pl.program_id(0) == 0)
        def _():
            carry_ref[...] = jnp.zeros_like(carry_ref)

        xv = x_ref[...]
        _, xhat = _rms_stats(xv)
        h = (xhat * g_ref[...]).astype(BF16)
        h_ref[...] = h
        gp = _mm_nt(h, wg_ref[...])
        gp_ref[...] = gp.astype(BF16)
        gate, _, _ = _conv_fwd(gp, carry_ref[7:8, :], carry_ref[6:7, :], cw_ref, cb_ref)
        carry_ref[...] = gp[tm - 8:tm, :]
        up = _mm_nt(h, wu_ref[...])
        up_ref[...] = up.astype(BF16)
        a = (gate * _sigmoid(gate) * up).astype(BF16)
        x2_ref[...] = xv + _mm(a, wd_ref[...])

    def blk(c):
        return pl.BlockSpec((tm, c), lambda i: (i, 0))

    return pl.pallas_call(
        body, name="ffn_fwd", grid=(t // tm,),
        in_specs=[blk(D_MODEL), _full((1, D_MODEL)), _full((D_FF, D_MODEL)), _full((D_FF, D_MODEL)),
                  _full((D_FF, D_MODEL)), _full((3, D_FF)), _full((1, D_FF))],
        out_specs=[blk(D_FF), blk(D_FF), blk(D_MODEL), blk(D_MODEL)],
        out_shape=[jax.ShapeDtypeStruct((t, D_FF), BF16), jax.ShapeDtypeStruct((t, D_FF), BF16),
                   jax.ShapeDtypeStruct((t, D_MODEL), BF16), jax.ShapeDtypeStruct((t, D_MODEL), F32)],
        scratch_shapes=[pltpu.VMEM((8, D_FF), F32)],
        compiler_params=_cparams("arbitrary"),
    )(x1, g_ffn, w_gate_t, w_up_t, w_down, fcw, fcb)


def _ple_fwd_bwd(x2, p, target, g_ple, w_pg, w_pp, tm):
    t = x2.shape[0]

    def body(x_ref, p_ref, t_ref, g_ref, wg_ref, wp_ref, dx_ref, dxb_ref, loss_ref, dwgb_ref, dwp_ref, dg_ref,
             dwg_ref):
        @pl.when(pl.program_id(0) == 0)
        def _():
            loss_ref[...] = jnp.zeros_like(loss_ref)
            dwg_ref[...] = jnp.zeros_like(dwg_ref)
            dwp_ref[...] = jnp.zeros_like(dwp_ref)
            dg_ref[...] = jnp.zeros_like(dg_ref)

        xv = x_ref[...]
        r, xhat = _rms_stats(xv)
        g = g_ref[...]
        h = (xhat * g).astype(BF16)
        pg = _sigmoid(_mm(h, wg_ref[...]))
        pb = p_ref[...].astype(BF16)
        pp = _mm(pb, wp_ref[...])
        err = xv + pg * pp - t_ref[...]
        loss_ref[...] += 0.5 * jnp.sum(jnp.mean(err * err, axis=-1, keepdims=True))
        dx3 = err * (1.0 / D_MODEL)
        d_pp = (dx3 * pg).astype(BF16)
        d_pre = (dx3 * pp * pg * (1.0 - pg)).astype(BF16)
        dwp_ref[...] += _mm_tn(pb, d_pp)
        dwg_ref[...] += _mm_tn(h, d_pre)
        dh = _mm_nt(d_pre, wg_ref[...])
        dg_ref[...] += jnp.sum(dh * xhat, axis=0, keepdims=True)
        dx2 = dx3 + _rms_bwd(dh, xhat, r, g)
        dx_ref[...] = dx2
        dxb_ref[...] = dx2.astype(BF16)

        @pl.when(pl.program_id(0) == t // tm - 1)
        def _():
            dwgb_ref[...] = dwg_ref[...].astype(BF16)

    def blk(c):
        return pl.BlockSpec((tm, c), lambda i: (i, 0))

    return pl.pallas_call(
        body, name="ple_fwd_bwd", grid=(t // tm,),
        in_specs=[blk(D_MODEL), blk(PLE_DIM), blk(D_MODEL), _full((1, D_MODEL)), _full((D_MODEL, D_MODEL)),
                  _full((PLE_DIM, D_MODEL))],
        out_specs=[blk(D_MODEL), blk(D_MODEL), _full((8, 128)), _full((D_MODEL, D_MODEL)),
                   _full((PLE_DIM, D_MODEL)), _full((1, D_MODEL))],
        out_shape=[jax.ShapeDtypeStruct((t, D_MODEL), F32), jax.ShapeDtypeStruct((t, D_MODEL), BF16),
                   jax.ShapeDtypeStruct((8, 128), F32),
                   jax.ShapeDtypeStruct((D_MODEL, D_MODEL), BF16), jax.ShapeDtypeStruct((PLE_DIM, D_MODEL), F32),
                   jax.ShapeDtypeStruct((1, D_MODEL), F32)],
        scratch_shapes=[pltpu.VMEM((D_MODEL, D_MODEL), F32)],
        compiler_params=_cparams("arbitrary"),
    )(x2, p, target, g_ple, w_pg, w_pp)


def _ffn_bwd(dx2, h2, gp, up, w_gate, w_up, w_down, fcw, fcb, tm):
    t = dx2.shape[0]
    nblk = t // tm
    fc = D_FF // FF_CHUNKS
    half = tm // FFN_BWD_PARTS

    def body(dx_ref, h_ref, gp_ref, gph_ref, up_ref, wg_ref, wu_ref, wd_ref, cw_ref, cb_ref,
             dh_ref, dwd_hbm, dwu_hbm, dwg_hbm, dcw_ref, dcb_ref, carry_ref, a_scr, dup_scr, dgp_scr,
             dwd_acc, dwu_acc, dwg_acc, stage, stage_sem):
        i = pl.program_id(1)

        @pl.when(i == 0)
        def _():
            carry_ref[...] = jnp.zeros_like(carry_ref)
            dwd_acc[...] = jnp.zeros_like(dwd_acc)
            dwu_acc[...] = jnp.zeros_like(dwu_acc)
            dwg_acc[...] = jnp.zeros_like(dwg_acc)
            dcw_ref[...] = jnp.zeros_like(dcw_ref)
            dcb_ref[...] = jnp.zeros_like(dcb_ref)

        keep = (i < nblk - 1).astype(F32)
        later = carry_ref[...]
        for hf in reversed(range(FFN_BWD_PARTS)):
            rows = slice(hf * half, (hf + 1) * half)
            dxb = dx_ref[rows, :]
            gp_v = gp_ref[rows, :].astype(F32)
            if hf > 0:
                before = gp_ref[hf * half - 16:hf * half, :].astype(F32)
            else:
                before = gph_ref[...].astype(F32) * keep
            gate, gp1, gp2 = _conv_fwd(gp_v, before[15:16, :], before[14:15, :], cw_ref, cb_ref)
            s = _sigmoid(gate)
            silu = gate * s
            up_v = up_ref[rows, :].astype(F32)
            da = _mm_nt(dxb, wd_ref[...])
            a_scr[rows, :] = (silu * up_v).astype(BF16)
            d_up = (da * silu).astype(BF16)
            dup_scr[rows, :] = d_up
            d_gate = da * up_v * (s * (1.0 + gate * (1.0 - s)))
            d_gp = _conv_bwd_input(d_gate, later[0:1, :], later[1:2, :], cw_ref).astype(BF16)
            dgp_scr[rows, :] = d_gp
            later = d_gate[0:8, :]
            dcw_ref[0:1, :] += jnp.sum(d_gate * gp2, axis=0, keepdims=True)
            dcw_ref[1:2, :] += jnp.sum(d_gate * gp1, axis=0, keepdims=True)
            dcw_ref[2:3, :] += jnp.sum(d_gate * gp_v, axis=0, keepdims=True)
            dcb_ref[...] += jnp.sum(d_gate, axis=0, keepdims=True)
            dh_ref[rows, :] = (_mm(d_gp, wg_ref[...]) + _mm(d_up, wu_ref[...])).astype(BF16)
        carry_ref[...] = later
        dwd_acc[...] += _mm_tn(a_scr[...], dx_ref[...])
        dwu_acc[...] += _mm_tn(h_ref[...], dup_scr[...])
        dwg_acc[...] += _mm_tn(h_ref[...], dgp_scr[...])

        @pl.when(i == nblk - 1)
        def _():
            rows = pl.ds(pl.multiple_of(pl.program_id(0) * fc, 16), fc)
            for acc, out, flip in ((dwd_acc, dwd_hbm, False), (dwu_acc, dwu_hbm, True), (dwg_acc, dwg_hbm, True)):
                stage[...] = (acc[...].T if flip else acc[...]).astype(BF16)
                copy = pltpu.make_async_copy(stage, out.at[rows, :], stage_sem)
                copy.start()
                copy.wait()

    def rev(i):
        return nblk - 1 - i

    one = pl.Buffered(1)
    in_specs = [
        pl.BlockSpec((tm, D_MODEL), lambda j, i: (rev(i), 0)),
        pl.BlockSpec((tm, D_MODEL), lambda j, i: (rev(i), 0)),
        pl.BlockSpec((tm, fc), lambda j, i: (rev(i), j)),
        pl.BlockSpec((16, fc), lambda j, i: (jnp.maximum(rev(i) * (tm // 16) - 1, 0), j)),
        pl.BlockSpec((tm, fc), lambda j, i: (rev(i), j)),
        pl.BlockSpec((fc, D_MODEL), lambda j, i: (j, 0), pipeline_mode=one),
        pl.BlockSpec((fc, D_MODEL), lambda j, i: (j, 0), pipeline_mode=one),
        pl.BlockSpec((fc, D_MODEL), lambda j, i: (j, 0), pipeline_mode=one),
        pl.BlockSpec((3, fc), lambda j, i: (0, j)),
        pl.BlockSpec((1, fc), lambda j, i: (0, j)),
    ]
    out_specs = [
        pl.BlockSpec((None, tm, D_MODEL), lambda j, i: (j, rev(i), 0)),
        ANY, ANY, ANY,
        pl.BlockSpec((3, fc), lambda j, i: (0, j)),
        pl.BlockSpec((1, fc), lambda j, i: (0, j)),
    ]
    return pl.pallas_call(
        body, name="ffn_bwd", grid=(FF_CHUNKS, nblk), in_specs=in_specs, out_specs=out_specs,
        out_shape=[jax.ShapeDtypeStruct((FF_CHUNKS, t, D_MODEL), BF16), jax.ShapeDtypeStruct((D_FF, D_MODEL), BF16),
                   jax.ShapeDtypeStruct((D_FF, D_MODEL), BF16), jax.ShapeDtypeStruct((D_FF, D_MODEL), BF16),
                   jax.ShapeDtypeStruct((3, D_FF), F32), jax.ShapeDtypeStruct((1, D_FF), F32)],
        scratch_shapes=[pltpu.VMEM((8, fc), F32), pltpu.VMEM((tm, fc), BF16), pltpu.VMEM((tm, fc), BF16),
                        pltpu.VMEM((tm, fc), BF16), pltpu.VMEM((fc, D_MODEL), F32), pltpu.VMEM((D_MODEL, fc), F32),
                        pltpu.VMEM((D_MODEL, fc), F32), pltpu.VMEM((fc, D_MODEL), BF16), pltpu.SemaphoreType.DMA],
        compiler_params=_cparams("arbitrary", "arbitrary", vmem=V7X_VMEM_LIMIT_LARGE),
    )(dx2, h2, gp, gp, up, w_gate, w_up, w_down, fcw, fcb)


def _outproj_bwd(dh2, dx2, x1, g_ffn, w_out, yc, ya, goc, goa, zconv, conv_w, conv_b, bd, tm):
    t = x1.shape[0]
    nblk = t // tm

    def body(dh_ref, dx2_ref, x1_ref, g_ref, w_ref, yc_ref, ya_ref, goc_ref, goa_ref, zc_ref, zch_ref, cw_ref, cb_ref,
             bd_ref, dx1_ref, dya_ref, dd_ref, dzc_ref, dwb_ref, dg_ref, dgoc_ref, dgoa_ref, dcw_ref, dcb_ref,
             carry_ref, dw_ref):
        i = pl.program_id(0)

        @pl.when(i == 0)
        def _():
            carry_ref[...] = jnp.zeros_like(carry_ref)
            for ref in (dw_ref, dg_ref, dgoc_ref, dgoa_ref, dcw_ref, dcb_ref):
                ref[...] = jnp.zeros_like(ref)

        keep = (i < nblk - 1).astype(F32)
        dh2_v = dh_ref[0].astype(F32)
        for j in range(1, FF_CHUNKS):
            dh2_v = dh2_v + dh_ref[j].astype(F32)
        r, xhat = _rms_stats(x1_ref[...])
        dg_ref[...] += jnp.sum(dh2_v * xhat, axis=0, keepdims=True)
        dx1 = dx2_ref[...] + _rms_bwd(dh2_v, xhat, r, g_ref[...])
        dx1_ref[...] = dx1
        dx1b = dx1.astype(BF16)
        dy = _mm_nt(dx1b, w_ref[...])

        yc_v = yc_ref[...].astype(F32)
        rc, ychat = _rms_stats(yc_v)
        dw_ref[0:CONV_W, :] += _mm_tn((ychat * goc_ref[...]).astype(BF16), dx1b)
        dyc = dy[:, 0:CONV_W]
        dgoc_ref[...] += jnp.sum(dyc * ychat, axis=0, keepdims=True)
        d_yc = _rms_bwd(dyc, ychat, rc, goc_ref[...])

        ya_v = ya_ref[...].astype(F32)
        ra, yahat = _rms_stats(ya_v)
        dw_ref[CONV_W:, :] += _mm_tn((yahat * goa_ref[...]).astype(BF16), dx1b)
        dya = dy[:, CONV_W:]
        dgoa_ref[...] += jnp.sum(dya * yahat, axis=0, keepdims=True)
        d_ya = _rms_bwd(dya, yahat, ra, goa_ref[...])
        dya_ref[...] = d_ya
        dd_ref[...] = _seg_sum64(d_ya * ya_v, bd_ref)

        zb = zc_ref[:, 0:CONV_W].astype(F32)
        zc = zc_ref[:, CONV_W:2 * CONV_W].astype(F32)
        zx = zc_ref[:, 2 * CONV_W:3 * CONV_W].astype(F32)
        u = zc * zx
        uh = (zch_ref[:, CONV_W:2 * CONV_W].astype(F32) * zch_ref[:, 2 * CONV_W:3 * CONV_W].astype(F32)) * keep
        cv, u1, u2 = _conv_fwd(u, uh[15:16, :], uh[14:15, :], cw_ref, cb_ref)
        d_cv = d_yc * zb
        d_u = _conv_bwd_input(d_cv, carry_ref[0:1, :], carry_ref[1:2, :], cw_ref)
        carry_ref[...] = d_cv[0:8, :]
        dcw_ref[0:1, :] += jnp.sum(d_cv * u2, axis=0, keepdims=True)
        dcw_ref[1:2, :] += jnp.sum(d_cv * u1, axis=0, keepdims=True)
        dcw_ref[2:3, :] += jnp.sum(d_cv * u, axis=0, keepdims=True)
        dcb_ref[...] += jnp.sum(d_cv, axis=0, keepdims=True)
        dzc_ref[:, 0:CONV_W] = (d_yc * cv).astype(BF16)
        dzc_ref[:, CONV_W:2 * CONV_W] = (d_u * zx).astype(BF16)
        dzc_ref[:, 2 * CONV_W:3 * CONV_W] = (d_u * zc).astype(BF16)

        @pl.when(i == nblk - 1)
        def _():
            dwb_ref[...] = dw_ref[...].astype(BF16)

    def rev(i):
        return nblk - 1 - i

    def blk(c):
        return pl.BlockSpec((tm, c), lambda i: (rev(i), 0))

    in_specs = [
        pl.BlockSpec((FF_CHUNKS, tm, D_MODEL), lambda i: (0, rev(i), 0)),
        blk(D_MODEL), blk(D_MODEL), _full((1, D_MODEL)), _full((D_MODEL, D_MODEL)),
        blk(CONV_W), blk(ATTN_W), _full((1, CONV_W)), _full((1, ATTN_W)),
        blk(3 * CONV_W),
        pl.BlockSpec((16, 3 * CONV_W), lambda i: (jnp.maximum(rev(i) * (tm // 16) - 1, 0), 0)),
        _full((3, CONV_W)), _full((1, CONV_W)), _full((256, 256)),
    ]
    out_specs = [blk(D_MODEL), blk(ATTN_W), blk(ATTN_W), blk(3 * CONV_W), _full((D_MODEL, D_MODEL)),
                 _full((1, D_MODEL)), _full((1, CONV_W)), _full((1, ATTN_W)), _full((3, CONV_W)), _full((1, CONV_W))]
    return pl.pallas_call(
        body, name="outproj_bwd", grid=(nblk,), in_specs=in_specs, out_specs=out_specs,
        out_shape=[jax.ShapeDtypeStruct((t, D_MODEL), F32), jax.ShapeDtypeStruct((t, ATTN_W), F32),
                   jax.ShapeDtypeStruct((t, ATTN_W), F32), jax.ShapeDtypeStruct((t, 3 * CONV_W), BF16),
                   jax.ShapeDtypeStruct((D_MODEL, D_MODEL), BF16), jax.ShapeDtypeStruct((1, D_MODEL), F32),
                   jax.ShapeDtypeStruct((1, CONV_W), F32), jax.ShapeDtypeStruct((1, ATTN_W), F32),
                   jax.ShapeDtypeStruct((3, CONV_W), F32), jax.ShapeDtypeStruct((1, CONV_W), F32)],
        scratch_shapes=[pltpu.VMEM((8, CONV_W), F32), pltpu.VMEM((D_MODEL, D_MODEL), F32)],
        compiler_params=_cparams("arbitrary"),
    )(dh2, dx2, x1, g_ffn, w_out, yc, ya, goc, goa, zconv, zconv, conv_w, conv_b, bd)


def _attn_bwd(q, k, v, dya, lse, dd, e_all, m_all, after):
    t = q.shape[0]
    nsb = t // SUPER

    def body(q_ref, kc_ref, kp_ref, vc_ref, vp_ref, dy_ref, l_ref, d_ref, e_ref, m_ref, after_ref,
             dq_ref, dk_ref, dv_ref, kk, vv, dkacc, dvacc, dwide):
        s = pl.program_id(1)

        @pl.when(s == 0)
        def _():
            dkacc[...] = jnp.zeros_like(dkacc)
            dvacc[...] = jnp.zeros_like(dvacc)

        dkacc[0:SUPER, :] = dkacc[SUPER:, :]
        dvacc[0:SUPER, :] = dvacc[SUPER:, :]
        dkacc[SUPER:, :] = jnp.zeros((SUPER, QK_BLOCK), F32)
        dvacc[SUPER:, :] = jnp.zeros((SUPER, QK_BLOCK), F32)

        @pl.when(s < nsb)
        def _():
            kk[0:SUPER, :] = kp_ref[...]
            kk[SUPER:, :] = kc_ref[...]
            vv[0:SUPER, :] = vp_ref[...]
            vv[SUPER:, :] = vc_ref[...]
            head0 = lax.broadcasted_iota(jnp.int32, (QK_BLOCK, QK_BLOCK), 1) < HEAD_DIM

            def widened(a):
                other = pltpu.roll(a, HEAD_DIM, 1)
                first = lax.broadcasted_iota(jnp.int32, a.shape, 1) < HEAD_DIM
                return jnp.where(first, a, other), jnp.where(first, other, a)

            def stacked(h0, h1):
                return jnp.concatenate([jnp.concatenate([h0, h0], axis=1), jnp.concatenate([h1, h1], axis=1)], axis=0)

            def widen_dd(i, carry):
                rows = pl.ds(pl.multiple_of(i * 256, 256), 256)
                dwide[0, rows, :], dwide[1, rows, :] = widened(d_ref[rows, :])
                return carry

            lax.fori_loop(0, SUPER // 256, widen_dd, 0)

            for b, dil in enumerate(DILATIONS):
                def unit(u, carry, b=b, dil=dil):
                    start = _unit_start(u, dil)
                    first_key = SUPER + start - QK_BLOCK * dil
                    qrows = _rows(start, QK_BLOCK, dil)
                    krows = _rows(first_key, KEYS, dil)
                    q2 = _stack_heads(q_ref[qrows, :].astype(BF16), head0)
                    dy2 = _stack_heads(dy_ref[qrows, :].astype(BF16), head0)
                    g2 = stacked(*widened(jnp.exp(m_ref[b, qrows, :] - l_ref[qrows, :])))
                    d2 = stacked(dwide[0, qrows, :], dwide[1, qrows, :])
                    k2 = kk[krows, :].astype(BF16)
                    v2 = vv[krows, :].astype(BF16)
                    prob = e_ref[b * UNITS + u].astype(F32) * g2
                    ds = (prob * (_mm_nt(dy2, v2) - d2)).astype(BF16)
                    dvacc[krows, :] += _mm_tn(prob.astype(BF16), dy2)
                    dkacc[krows, :] += _mm_tn(ds, q2)
                    dq2 = _mm(ds, k2)
                    dq = jnp.where(head0, dq2[0:QK_BLOCK], dq2[QK_BLOCK:]) * ATTN_SCALE
                    if b == 0:
                        dq_ref[qrows, :] = dq
                    else:
                        dq_ref[qrows, :] += dq
                    return carry

                lax.fori_loop(0, UNITS, unit, 0, unroll=8)

        dk_ref[...] = dkacc[0:SUPER, :]
        dv_ref[...] = dvacc[0:SUPER, :].astype(BF16)

    def cur_map(p, s):
        return (jnp.minimum(s, nsb - 1), p)

    def prev_map(p, s):
        return (jnp.clip(s - 1, 0, nsb - 1), p)

    cur = pl.BlockSpec((SUPER, QK_BLOCK), cur_map)
    prev = pl.BlockSpec((SUPER, QK_BLOCK), prev_map)
    return pl.pallas_call(
        body, name="attn_bwd", grid=(4, nsb + 1),
        in_specs=[cur, cur, prev, cur, prev, cur, cur, cur,
                  pl.BlockSpec((None, None, 3 * UNITS, KEYS, KEYS), lambda p, s: (p, jnp.minimum(s, nsb - 1), 0, 0, 0)),
                  pl.BlockSpec((3, SUPER, QK_BLOCK), lambda p, s: (0, jnp.minimum(s, nsb - 1), p)),
                  pl.BlockSpec(memory_space=pl.ANY)],
        out_specs=[cur, prev, prev],
        out_shape=[jax.ShapeDtypeStruct((t, ATTN_W), F32), jax.ShapeDtypeStruct((t, ATTN_W), F32),
                   jax.ShapeDtypeStruct((t, ATTN_W), BF16)],
        scratch_shapes=[pltpu.VMEM((2 * SUPER, QK_BLOCK), F32)] * 4 + [pltpu.VMEM((2, SUPER, QK_BLOCK), F32)],
        compiler_params=_cparams("parallel", "arbitrary"),
    )(q, k, k, v, v, dya, lse, dd, e_all, m_all, after)


def _inproj_bwd(dq, dk, dv, dzconv, zqk, x, dx1, g_mix, w_in, qg, kg, bd, tm):
    t = x.shape[0]
    nblk = t // tm
    shard = IN_COLS // N_DEV

    def body(dq_ref, dk_ref, dv_ref, dzc_ref, zqk_ref, x_ref, dx1_ref, g_ref, w_ref, qg_ref,
             kg_ref, bd_ref, dx_ref, dw_hbm, dg_ref, dqg_ref, dkg_ref, dw_ref, stage, stage_sem):
        @pl.when(pl.program_id(0) == 0)
        def _():
            for ref in (dw_ref, dg_ref, dqg_ref, dkg_ref):
                ref[...] = jnp.zeros_like(ref)

        parts = [dzc_ref[...]]
        for j, (dn_ref, gain_ref, dgain_ref) in enumerate(((dq_ref, qg_ref, dqg_ref), (dk_ref, kg_ref, dkg_ref))):
            dn = dn_ref[...]
            z = zqk_ref[:, j * ATTN_W:(j + 1) * ATTN_W].astype(F32)
            r = lax.rsqrt(_seg_sum64(z * z, bd_ref) * (1.0 / HEAD_DIM) + EPS)
            zhat = z * r
            dgain_ref[...] += jnp.sum(dn * zhat, axis=0, keepdims=True)
            gd = dn * gain_ref[...]
            parts.append((r * (gd - zhat * (_seg_sum64(gd * zhat, bd_ref) * (1.0 / HEAD_DIM)))).astype(BF16))
        parts.append(dv_ref[...].astype(BF16))
        dz = jnp.concatenate(parts, axis=1)

        r, xhat = _rms_stats(x_ref[...])
        g = g_ref[...]
        dw_ref[...] += _mm_tn((xhat * g).astype(BF16), dz)
        dh = _mm_nt(dz, w_ref[...])
        dg_ref[...] += jnp.sum(dh * xhat, axis=0, keepdims=True)
        dx_ref[...] = dx1_ref[...] + _rms_bwd(dh, xhat, r, g)

        @pl.when(pl.program_id(0) == nblk - 1)
        def _():
            for k in range(N_DEV):
                stage[...] = dw_ref[:, k * shard:(k + 1) * shard].astype(BF16)
                copy = pltpu.make_async_copy(stage, dw_hbm.at[k], stage_sem)
                copy.start()
                copy.wait()

    def blk(c):
        return pl.BlockSpec((tm, c), lambda i: (i, 0))

    return pl.pallas_call(
        body, name="inproj_bwd", grid=(nblk,),
        in_specs=[blk(ATTN_W)] * 3 + [blk(3 * CONV_W), blk(2 * ATTN_W), blk(D_MODEL), blk(D_MODEL), _full((1, D_MODEL)),
                                      _full((D_MODEL, IN_COLS)), _full((1, ATTN_W)), _full((1, ATTN_W)),
                                      _full((256, 256))],
        out_specs=[blk(D_MODEL), ANY, _full((1, D_MODEL)), _full((1, ATTN_W)), _full((1, ATTN_W))],
        out_shape=[jax.ShapeDtypeStruct((t, D_MODEL), F32), jax.ShapeDtypeStruct((N_DEV, D_MODEL, shard), BF16),
                   jax.ShapeDtypeStruct((1, D_MODEL), F32), jax.ShapeDtypeStruct((1, ATTN_W), F32),
                   jax.ShapeDtypeStruct((1, ATTN_W), F32)],
        scratch_shapes=[pltpu.VMEM((D_MODEL, IN_COLS), F32), pltpu.VMEM((D_MODEL, shard), BF16),
                        pltpu.SemaphoreType.DMA],
        compiler_params=_cparams("arbitrary"),
    )(dq, dk, dv, dzconv, zqk, x, dx1, g_mix, w_in, qg, kg, bd)


def _ordered_after(a, token):
    return a if token is None else a + token


def _local_step(x, p, target, w, tms, hooks=None):
    hooks = hooks or {}
    bd = jnp.asarray(np.kron(np.eye(4, dtype=np.float32), np.ones((HEAD_DIM, HEAD_DIM), np.float32)), BF16)
    qg = jnp.tile(w["q_norm_g"], (1, 8))
    kg = jnp.tile(w["k_norm_g"], (1, 8))
    slopes = np.exp2(-np.arange(1, 9, dtype=np.float32))
    slopes = jnp.asarray(np.broadcast_to(slopes.reshape(4, 2, 1), (4, 2, QK_BLOCK)))

    zconv, zqk, yc, q, k, v = _inproj_fwd(x, w["g_mix"], w["w_in"], w["conv_w"], w["conv_b"], qg, kg, bd, tms[0])
    ya, lse, e_all, m_all = _attn_fwd(q, k, v, slopes)
    if "late_weights" in hooks:
        w = {**w, **hooks["late_weights"](lse)}
    x1 = _outproj_fwd(ya, yc, x, w["g_out_conv"], w["g_out_attn"], w["w_out"], tms[0])
    gp, up, h2, x2 = _ffn_fwd(x1, w["g_ffn"], w["w_gate"], w["w_up"], w["w_down"], w["ffn_conv_w"], w["ffn_conv_b"],
                              tms[1])
    dx2, dx2b, loss, dw_pg, dw_pp, dg_ple = _ple_fwd_bwd(x2, p, target, w["g_ple"], w["w_ple_gate"], w["w_ple_proj"], tms[0])
    dh2, dw_down, dw_up, dw_gate, dfcw, dfcb = _ffn_bwd(dx2b, h2, gp, up, w["w_gate"], w["w_up"], w["w_down"],
                                                        w["ffn_conv_w"], w["ffn_conv_b"], tms[0])
    token = None
    if "ffn_grads" in hooks:
        token = hooks["ffn_grads"]({"w_ple_gate": dw_pg, "w_ple_proj": dw_pp, "w_down": dw_down, "w_up": dw_up,
                                    "w_gate": dw_gate, "ffn_conv_b": dfcb})
    dx1, dya, dd, dzconv, dw_out, dg_ffn, dgoc, dgoa, dcw, dcb = _outproj_bwd(
        dh2, dx2, x1, _ordered_after(w["g_ffn"], token), w["w_out"], yc, ya, w["g_out_conv"], w["g_out_attn"], zconv,
        w["conv_w"], w["conv_b"], bd, tms[1])
    token = hooks["outproj_done"](dx1) if "outproj_done" in hooks else None
    dq, dk, dv = _attn_bwd(q, k, v, dya, lse, dd, e_all, m_all, slopes if token is None else token)
    dx, dw_in, dg_mix, dqg, dkg = _inproj_bwd(dq, dk, dv, dzconv, zqk, x, dx1, w["g_mix"], w["w_in"], qg, kg, bd,
                                              tms[0])
    grads = {
        "g_mix": dg_mix, "w_in": dw_in, "conv_w": dcw, "conv_b": dcb,
        "q_norm_g": dqg, "k_norm_g": dkg,
        "g_out_conv": dgoc, "g_out_attn": dgoa, "w_out": dw_out, "g_ffn": dg_ffn, "w_gate": dw_gate, "w_up": dw_up,
        "ffn_conv_w": dfcw, "ffn_conv_b": dfcb, "w_down": dw_down, "g_ple": dg_ple, "w_ple_gate": dw_pg,
        "w_ple_proj": dw_pp,
    }
    return loss, dx, grads


ANY = pl.BlockSpec(memory_space=pl.ANY)
MESH = pl.DeviceIdType.MESH


def _all_gather(shards, name):
    n = len(shards)

    def body(*refs):
        ins, outs = refs[:n], refs[n:2 * n]
        send_sems, recv_sems, local_sems = refs[2 * n:]
        x, y, c = lax.axis_index("x"), lax.axis_index("y"), lax.axis_index("c")
        me, sibling = (x, y, c), (x, y, 1 - c)
        chips = [(1 - x, y), (x, 1 - y), (1 - x, 1 - y)]

        def slot(dev):
            return 4 * dev[0] + 2 * dev[1] + dev[2]

        def copy(b, k, block, to, src=None):
            dst = outs[b].at[slot(block)]
            return pltpu.make_async_remote_copy(
                src_ref=dst if src is None else src, dst_ref=dst, send_sem=send_sems.at[b, k],
                recv_sem=recv_sems.at[b, k], device_id=to, device_id_type=MESH)

        mine = [pltpu.make_async_copy(ins[b], outs[b].at[slot(me)], local_sems.at[b]) for b in range(n)]
        first, passed = [], []
        for b in range(n):
            mine[b].start()
            first.append(copy(b, 0, me, sibling, src=ins[b]))
            first += [copy(b, 1 + j, me, (*chip, c), src=ins[b]) for j, chip in enumerate(chips)]
        for cp in first:
            cp.start()
        for j, chip in enumerate(chips):
            for b in range(n):
                copy(b, 1 + j, (*chip, c), me).wait_recv()
                fwd = copy(b, 4 + j, (*chip, c), sibling)
                fwd.start()
                passed.append(fwd)
        for b in range(n):
            copy(b, 0, sibling, me).wait_recv()
            for j, chip in enumerate(chips):
                copy(b, 4 + j, (*chip, 1 - c), me).wait_recv()
        for cp in first + passed:
            cp.wait_send()
        for cp in mine:
            cp.wait()

    return pl.pallas_call(
        body, name=name,
        in_specs=[ANY] * n, out_specs=[ANY] * n,
        out_shape=[jax.ShapeDtypeStruct((N_DEV,) + s.shape, s.dtype) for s in shards],
        scratch_shapes=[pltpu.SemaphoreType.DMA((n, 7)), pltpu.SemaphoreType.DMA((n, 7)),
                        pltpu.SemaphoreType.DMA((n,))],
    )(*shards)


HBM = pl.BlockSpec(memory_space=pltpu.HBM)
SEM = pl.BlockSpec(memory_space=pltpu.SEMAPHORE)
EFFECT = pltpu.SideEffectType.DATAFLOW_SIDE_EFFECTING
FLIPS = ((0, 0, 1), (0, 1, 0), (0, 1, 1), (1, 0, 0), (1, 0, 1), (1, 1, 0), (1, 1, 1))


def _flip_peers():
    pos = (lax.axis_index("x"), lax.axis_index("y"), lax.axis_index("c"))
    return [tuple(1 - a if f else a for a, f in zip(pos, flip)) for flip in FLIPS]


def _hbm(a):
    return pltpu.with_memory_space_constraint(a, pltpu.HBM)


def _own_copies(own, src_refs, land_refs, send_sems, n_remote):
    return [pltpu.make_async_copy(src, dst, send_sems.at[n_remote + i])
            for i, (src, dst) in enumerate(own(src_refs, land_refs) if own else [])]


def _split_start(name, srcs, lands, plan, n_copies, after, own=None):
    n, m = len(srcs), len(lands)

    def body(*refs):
        send_sems, recv_sems, token = refs[n + m + 1], refs[n + m + 2], refs[-1]
        for i, (src, dst, peer) in enumerate(plan(refs[:n], refs[n:n + m])):
            pltpu.make_async_remote_copy(src_ref=src, dst_ref=dst, send_sem=send_sems.at[i], recv_sem=recv_sems.at[i],
                                         device_id=peer, device_id_type=MESH).start()
        for copy in _own_copies(own, refs[:n], refs[n:n + m], send_sems, n_copies):
            copy.start()
        token[...] = jnp.zeros_like(token)

    outs = pl.pallas_call(
        body, name=name + "_start",
        in_specs=[HBM] * (n + m) + [ANY],
        out_specs=[SEM, SEM] + [HBM] * (n + m) + [pl.BlockSpec(memory_space=pltpu.VMEM)],
        out_shape=[pltpu.SemaphoreType.DMA((n_copies + (n if own else 0),)), pltpu.SemaphoreType.DMA((n_copies,))]
        + [pltpu.HBM(a.shape, a.dtype) for a in list(srcs) + list(lands)] + [jax.ShapeDtypeStruct((1, D_MODEL), F32)],
        input_output_aliases={i: 2 + i for i in range(n + m)},
        compiler_params=pltpu.CompilerParams(has_side_effects=EFFECT),
    )(*[_hbm(a) for a in list(srcs) + list(lands)], after)
    return (outs[0], outs[1], outs[2:2 + n], outs[2 + n:2 + n + m]), outs[-1]


def _split_wait(name, started, plan, after, own=None):
    send_sems, recv_sems, srcs, lands = started
    n, m = len(srcs), len(lands)

    def body(*refs):
        send_ref, recv_ref = refs[n + m], refs[n + m + 1]
        copies = plan(refs[:n], refs[n:n + m])
        for i, (src, dst, peer) in enumerate(copies):
            copy = pltpu.make_async_remote_copy(src_ref=src, dst_ref=dst, send_sem=send_ref.at[i],
                                                recv_sem=recv_ref.at[i], device_id=peer, device_id_type=MESH)
            copy.wait_send()
            copy.wait_recv()
        for copy in _own_copies(own, refs[:n], refs[n:n + m], send_ref, len(copies)):
            copy.wait()

    outs = pl.pallas_call(
        body, name=name + "_wait",
        in_specs=[HBM] * (n + m) + [SEM, SEM, ANY],
        out_specs=[HBM] * (n + m),
        out_shape=[pltpu.HBM(a.shape, a.dtype) for a in list(srcs) + list(lands)],
        input_output_aliases={i: i for i in range(n + m)},
        compiler_params=pltpu.CompilerParams(has_side_effects=EFFECT),
    )(*srcs, *lands, send_sems, recv_sems, after)
    return outs[:n], outs[n:]


def _gather_plan(srcs, lands):
    slot = 4 * lax.axis_index("x") + 2 * lax.axis_index("y") + lax.axis_index("c")
    return [(src, land.at[slot], peer) for src, land in zip(srcs, lands) for peer in _flip_peers()]


def _own_slot(srcs, lands):
    slot = 4 * lax.axis_index("x") + 2 * lax.axis_index("y") + lax.axis_index("c")
    return [(src, land.at[slot]) for src, land in zip(srcs, lands)]


def _sibling_plan(srcs, lands):
    x, y, c = lax.axis_index("x"), lax.axis_index("y"), lax.axis_index("c")
    return [(src.at[k, 1 - c], land.at[k], (x, y, 1 - c)) for src, land in zip(srcs, lands) for k in range(N_CHIP)]


def _chip_plan(srcs, lands):
    x, y, c = lax.axis_index("x"), lax.axis_index("y"), lax.axis_index("c")
    return [(src.at[2 * cx + cy], land.at[2 * x + y], (cx, cy, c))
            for src, land in zip(srcs, lands) for cx, cy in ((1 - x, y), (x, 1 - y), (1 - x, 1 - y))]


def _row_tile(rows):
    for tr in range(min(rows, 512), 15, -16):
        if rows % tr == 0:
            return tr
    return rows


def _pair_sums(gs, lands, core, name):
    n = len(gs)

    def body(c_ref, *refs):
        for b in range(n):
            out = refs[2 * n + b]
            out[...] = (refs[b][...].astype(F32) + refs[n + b][...].astype(F32)).astype(out.dtype)

    def slab(a):
        return pl.BlockSpec((None,) + a.shape[1:], lambda k, c_ref: (k, 0, 0))

    return pl.pallas_call(
        body, name=name,
        grid_spec=pltpu.PrefetchScalarGridSpec(
            num_scalar_prefetch=1, grid=(N_CHIP,),
            in_specs=[pl.BlockSpec((None, None) + g.shape[2:], lambda k, c_ref: (k, c_ref[0], 0, 0)) for g in gs]
            + [slab(a) for a in lands],
            out_specs=[slab(a) for a in lands]),
        out_shape=[jax.ShapeDtypeStruct(a.shape, a.dtype) for a in lands],
        compiler_params=_cparams("parallel"),
    )(core, *gs, *lands)


def _adamw(own, arrived, chip, w, m, v, name):
    k, rows, cols = arrived.shape
    tr = _row_tile(rows)
    c1 = 1.0 / (1.0 - ADAM_B1 ** ADAM_STEP)
    c2 = 1.0 / (1.0 - ADAM_B2 ** ADAM_STEP)

    def body(chip_ref, o_ref, p_ref, w_ref, m_ref, v_ref, g_ref, d_ref, nm_ref, nv_ref):
        def slab(j):
            return jnp.where(chip_ref[0] == j, o_ref[j], p_ref[j]).astype(F32)

        g = slab(0)
        for j in range(1, k):
            g = g + slab(j)
        g_ref[...] = g
        nm = ADAM_B1 * m_ref[...] + (1.0 - ADAM_B1) * g
        nv = ADAM_B2 * v_ref[...] + (1.0 - ADAM_B2) * (g * g)
        nm_ref[...] = nm
        nv_ref[...] = nv
        d_ref[...] = -ADAM_LR * ((nm * c1) / (jnp.sqrt(nv * c2) + ADAM_EPS) + ADAM_WD * w_ref[...])

    blk = pl.BlockSpec((tr, cols), lambda i, c: (i, 0))
    stack = pl.BlockSpec((k, tr, cols), lambda i, c: (0, i, 0))
    return pl.pallas_call(
        body, name=name,
        grid_spec=pltpu.PrefetchScalarGridSpec(num_scalar_prefetch=1, grid=(rows // tr,),
                                               in_specs=[stack, stack, blk, blk, blk], out_specs=[blk] * 4),
        out_shape=[jax.ShapeDtypeStruct((rows, cols), F32)] * 4,
        compiler_params=_cparams("parallel"),
    )(chip, own, arrived, w, m, v)


SMALL_LAYOUT = (("g_mix", 0, 1024), ("conv_b", 1, 512), ("q_norm_g", 2, 64), ("k_norm_g", 3, 64),
                ("g_out_conv", 4, 512), ("g_out_attn", 5, 512), ("g_ffn", 6, 1024), ("ffn_conv_b", 7, 2816),
                ("g_ple", 10, 1024))
CONV_W_ROW = 11
FFN_CONV_W_ROW = 14
LOSS_ROW = 23


def _row_pieces(cols):
    return [(c, min(1024, cols - c)) for c in range(0, cols, 1024)]


def _pack_small(grads, loss_tile):
    names = [n for n, _, _ in SMALL_LAYOUT]

    def body(*refs):
        ins, cw_ref, fcw_ref, loss_ref, out_ref = refs[:len(names)], refs[-4], refs[-3], refs[-2], refs[-1]
        out_ref[...] = jnp.zeros_like(out_ref)
        for ref, (_, row, cols) in zip(ins, SMALL_LAYOUT):
            if ref.shape[1] == ATTN_W and cols == HEAD_DIM:
                out_ref[row:row + 1, 0:cols] = sum(ref[:, h:h + cols] for h in range(0, ATTN_W, cols))
                continue
            for j, (c, width) in enumerate(_row_pieces(cols)):
                out_ref[row + j:row + j + 1, 0:width] = ref[:, c:c + width]
        for k in range(3):
            out_ref[CONV_W_ROW + k:CONV_W_ROW + k + 1, 0:CONV_W] = cw_ref[k:k + 1, :]
            for j, (c, width) in enumerate(_row_pieces(D_FF)):
                row = FFN_CONV_W_ROW + 3 * k + j
                out_ref[row:row + 1, 0:width] = fcw_ref[k:k + 1, c:c + width]
        out_ref[LOSS_ROW:LOSS_ROW + 1, 0:128] = loss_ref[0:1, :]

    return pl.pallas_call(
        body, name="pack_small_grads", out_shape=jax.ShapeDtypeStruct((SMALL_ROWS, 1024), F32),
    )(*[grads[n] for n in names], grads["conv_w"], grads["ffn_conv_w"], loss_tile)


def _adamw_small(arrived, conv_parts, fconv_parts, wts, mom, var):
    names = [n for n, _, _ in SMALL_LAYOUT] + ["conv_w", "ffn_conv_w"]
    c1 = 1.0 / (1.0 - ADAM_B1 ** ADAM_STEP)
    c2 = 1.0 / (1.0 - ADAM_B2 ** ADAM_STEP)
    n = len(names)

    def body(*refs):
        land, cw_ref, fcw_ref = refs[0], refs[1], refs[2]
        state = refs[3:3 + 3 * n]
        outs = refs[3 + 3 * n:]

        def total(piece):
            acc = piece(0)
            for d in range(1, N_DEV):
                acc = acc + piece(d)
            return acc

        for i, name in enumerate(names):
            if name == "conv_w":
                g = total(lambda d: cw_ref[d])
            elif name == "ffn_conv_w":
                g = total(lambda d: fcw_ref[d])
            else:
                _, row, cols = SMALL_LAYOUT[i]
                pieces = [total(lambda d, j=j, width=width: land[d, row + j:row + j + 1, 0:width])
                          for j, (_, width) in enumerate(_row_pieces(cols))]
                g = pieces[0] if len(pieces) == 1 else jnp.concatenate(pieces, axis=1)
            w_ref, m_ref, v_ref = state[3 * i:3 * i + 3]
            nm = ADAM_B1 * m_ref[...] + (1.0 - ADAM_B1) * g
            nv = ADAM_B2 * v_ref[...] + (1.0 - ADAM_B2) * (g * g)
            outs[4 * i][...] = g
            outs[4 * i + 1][...] = -ADAM_LR * ((nm * c1) / (jnp.sqrt(nv * c2) + ADAM_EPS) + ADAM_WD * w_ref[...])
            outs[4 * i + 2][...] = nm
            outs[4 * i + 3][...] = nv
        outs[-1][...] = total(lambda d: land[d, LOSS_ROW:LOSS_ROW + 1, 0:128])

    state = [a[nm_] for nm_ in names for a in (wts, mom, var)]
    shapes = [jax.ShapeDtypeStruct(wts[nm_].shape, F32) for nm_ in names for _ in range(4)]
    outs = pl.pallas_call(
        body, name="adamw_small", out_shape=shapes + [jax.ShapeDtypeStruct((1, 128), F32)],
    )(arrived, conv_parts, fconv_parts, *state)
    return {nm_: tuple(outs[4 * i:4 * i + 4]) for i, nm_ in enumerate(names)}, outs[-1][0, 0]


COL_SHARDED = ("w_in", "w_ple_proj")
TRANSPOSED = ("w_gate", "w_up")
CONV_SHARDED = (("conv_w", CONV_W), ("ffn_conv_w", D_FF))


def _gathered_to_full(name, gathered):
    if name in COL_SHARDED:
        return gathered.transpose(1, 0, 2).reshape(gathered.shape[1], -1)
    return gathered.reshape(-1, gathered.shape[2])


def _full_to_stacked(name, grad, shard_shape):
    sr, sc = shard_shape
    if grad.ndim == 3:
        a = grad
    elif name in COL_SHARDED:
        a = grad.reshape(sr, N_DEV, sc).transpose(1, 0, 2)
    else:
        a = grad.reshape(N_DEV, sr, sc)
    return a.astype(BF16).reshape(N_CHIP, 2, sr, sc)


def _pad_rows(vec, rows):
    return jnp.pad(vec, (0, rows * 1024 - vec.shape[0])).reshape(rows, 1024)


def kernel(x, p, g_mix, w_in, conv_w, conv_b, q_norm_g, k_norm_g, g_out_conv, g_out_attn, w_out, g_ffn, w_gate, w_up, ffn_conv_w, ffn_conv_b, w_down, g_ple, w_ple_gate, w_ple_proj, loss_target, m_g_mix, m_w_in, m_conv_w, m_conv_b, m_q_norm_g, m_k_norm_g, m_g_out_conv, m_g_out_attn, m_w_out, m_g_ffn, m_w_gate, m_w_up, m_ffn_conv_w, m_ffn_conv_b, m_w_down, m_g_ple, m_w_ple_gate, m_w_ple_proj, v_g_mix, v_w_in, v_conv_w, v_conv_b, v_q_norm_g, v_k_norm_g, v_g_out_conv, v_g_out_attn, v_w_out, v_g_ffn, v_w_gate, v_w_up, v_ffn_conv_w, v_ffn_conv_b, v_w_down, v_g_ple, v_w_ple_gate, v_w_ple_proj):
    args = dict(locals())
    names = ["g_mix", "w_in", "conv_w", "conv_b", "q_norm_g", "k_norm_g", "g_out_conv", "g_out_attn", "w_out", "g_ffn",
             "w_gate", "w_up", "ffn_conv_w", "ffn_conv_b", "w_down", "g_ple", "w_ple_gate", "w_ple_proj"]
    big = list(BIG)
    conv = [n for n, _ in CONV_SHARDED]

    def local(prefix):
        out = {n: (args[prefix + n][0] if n in big or n in conv else args[prefix + n]) for n in names}
        out.update({n: out[n].T for n in TRANSPOSED})
        return out

    wts, mom, var = local(""), local("m_"), local("v_")
    shard_shapes = {n: wts[n].shape for n in big}
    dev = 4 * lax.axis_index("x") + 2 * lax.axis_index("y") + lax.axis_index("c")
    core = lax.axis_index("c").astype(jnp.int32).reshape(1)

    conv_local = _pad_rows(jnp.concatenate([wts[n].reshape(-1) for n in conv]), 8).reshape(8, 1024)
    late = [n for n in big if n != "w_in"]
    w_in_all, conv_all = _all_gather([wts["w_in"].astype(BF16), conv_local], "gather_weights")
    late_shards = [wts[n].astype(BF16) for n in late]
    gathering, token = _split_start("gather_late_weights", late_shards,
                                    [lax.empty((N_DEV,) + s.shape, BF16) for s in late_shards], _gather_plan,
                                    7 * len(late), w_in_all, own=_own_slot)
    full = dict(wts)
    full["w_in"] = _gathered_to_full("w_in", w_in_all)
    full["g_mix"] = _ordered_after(wts["g_mix"], token)
    flying = {}

    def late_weights(after):
        _, lands = _split_wait("gather_late_weights", gathering, _gather_plan, after, own=_own_slot)
        return {n: _gathered_to_full(n, land) for n, land in zip(late, lands)}

    early = ["w_ple_gate", "w_ple_proj", "w_down", "w_up", "w_gate"]

    def ffn_grads(g):
        stacked = [_full_to_stacked(n, g[n], shard_shapes[n]) for n in early]
        flying["sibling"], tok = _split_start("rs_sibling_early", stacked,
                                              [lax.empty((N_CHIP,) + s.shape[2:], BF16) for s in stacked],
                                              _sibling_plan, N_CHIP * len(early), g["ffn_conv_b"])
        return tok

    def outproj_done(after):
        stacked, landed = _split_wait("rs_sibling_early", flying["sibling"], _sibling_plan, after)
        parts = _pair_sums(stacked, landed, core, "rs_pair_sums_early")
        flying["chip"], tok = _split_start("rs_chip_early", parts, [lax.empty(q.shape, BF16) for q in parts],
                                           _chip_plan, 3 * len(early), landed[0])
        return tok

    off = 0
    for n, width in CONV_SHARDED:
        sc = width // N_DEV
        a = conv_all.reshape(N_DEV, -1)[:, off:off + 3 * sc].reshape(N_DEV, 3, sc)
        full[n] = a.transpose(1, 0, 2).reshape(3, width)
        off += 3 * sc

    loss, dx, grads = _local_step(x[0], p[0, 0], loss_target[0], full, (512, 256),
                                  {"late_weights": late_weights, "ffn_grads": ffn_grads, "outproj_done": outproj_done})

    chip = (2 * lax.axis_index("x") + lax.axis_index("y")).astype(jnp.int32).reshape(1)

    def adamw_of(group, parts, arrived):
        return {n: _adamw(own, got, chip, wts[n], mom[n], var[n], f"adamw_{n}")
                for n, own, got in zip(group, parts, arrived)}

    last = [n for n in big if n not in early]
    stacked = [_full_to_stacked(n, grads[n], shard_shapes[n]) for n in last]
    flying["sibling_last"], tok = _split_start("rs_sibling_last", stacked,
                                               [lax.empty((N_CHIP,) + s.shape[2:], BF16) for s in stacked],
                                               _sibling_plan, N_CHIP * len(last), dx)
    packed = _pack_small(grads, loss)
    flying["small"], tok = _split_start("gather_small_grads", [packed], [lax.empty((N_DEV,) + packed.shape, F32)],
                                        _gather_plan, N_DEV - 1, tok, own=_own_slot)
    stacked, landed = _split_wait("rs_sibling_last", flying["sibling_last"], _sibling_plan, tok)
    parts = _pair_sums(stacked, landed, core, "rs_pair_sums_last")
    flying["chip_last"], tok = _split_start("rs_chip_last", parts, [lax.empty(q.shape, BF16) for q in parts],
                                            _chip_plan, 3 * len(last), landed[0])

    parts, arrived = _split_wait("rs_chip_early", flying["chip"], _chip_plan, tok)
    out = adamw_of(early, parts, arrived)
    _, (small_all,) = _split_wait("gather_small_grads", flying["small"], _gather_plan, out[early[-1]][0],
                                  own=_own_slot)
    taps = small_all[:, CONV_W_ROW:CONV_W_ROW + 3, 0:CONV_W]
    ftaps = small_all[:, FFN_CONV_W_ROW:FFN_CONV_W_ROW + 9, :].reshape(N_DEV, 3, 3 * 1024)
    small_out, loss_total = _adamw_small(
        small_all, lax.dynamic_slice(taps, (0, 0, dev * (CONV_W // N_DEV)), (N_DEV, 3, CONV_W // N_DEV)),
        lax.dynamic_slice(ftaps, (0, 0, dev * (D_FF // N_DEV)), (N_DEV, 3, D_FF // N_DEV)), wts, mom, var)
    out.update(small_out)
    parts, arrived = _split_wait("rs_chip_last", flying["chip_last"], _chip_plan, small_out["g_mix"][0])
    out.update(adamw_of(last, parts, arrived))
    def result(n, which):
        a = out[n][which]
        return (a.T if n in TRANSPOSED else a).reshape(args[n].shape)

    return (loss_total, dx[None], *[result(n, which) for which in range(4) for n in names])
```

```python
import jax
import jax.numpy as jnp
import numpy as np
from jax import lax
from jax.experimental import pallas as pl
from jax.experimental.pallas import tpu as pltpu

F32 = jnp.float32
BF16 = jnp.bfloat16

D_MODEL = 1024
CONV_W = 512
ATTN_W = 512
HEAD_DIM = 64
D_FF = 2816
PLE_DIM = 256
IN_COLS = 3 * CONV_W + 3 * ATTN_W
EPS = 1e-6
QK_BLOCK = 128
DILATIONS = (1, 4, 16)
ATTN_SCALE = HEAD_DIM ** -0.5

ADAM_LR = 0.001
ADAM_B1 = 0.9
ADAM_B2 = 0.999
ADAM_EPS = 1e-08
ADAM_WD = 0.01
ADAM_STEP = 10

N_DEV = 8
N_CHIP = 4
V7X_VMEM_LIMIT = 56 * 1024 * 1024
V7X_VMEM_LIMIT_LARGE = 62 * 1024 * 1024
FF_CHUNKS = 2
FFN_BWD_PARTS = 1

BIG = ("w_in", "w_out", "w_gate", "w_up", "w_down", "w_ple_gate", "w_ple_proj")
SMALL_ROWS = 24


def _cparams(*sem, vmem=V7X_VMEM_LIMIT):
    return pltpu.CompilerParams(dimension_semantics=sem, vmem_limit_bytes=vmem)


def _mm(a, b):
    return jnp.dot(a, b, preferred_element_type=F32)


def _mm_nt(a, b):
    return lax.dot_general(a, b, (((1,), (1,)), ((), ())), preferred_element_type=F32)


def _mm_tn(a, b):
    return lax.dot_general(a, b, (((0,), (0,)), ((), ())), preferred_element_type=F32)


def _full(shape):
    nd = len(shape)
    return pl.BlockSpec(shape, lambda *_: (0,) * nd)


def _rms_stats(x):
    r = lax.rsqrt(jnp.mean(x * x, axis=-1, keepdims=True) + EPS)
    return r, x * r


def _rms_bwd(dy, xhat, r, g):
    gd = dy * g
    return r * (gd - xhat * jnp.mean(gd * xhat, axis=-1, keepdims=True))


def _seg_sum64(v, bd_ref):
    outs = []
    for c in range(0, v.shape[1], 256):
        vc = v[:, c:c + 256]
        hi = vc.astype(BF16)
        lo = (vc - hi.astype(F32)).astype(BF16)
        outs.append(_mm(hi, bd_ref[...]) + _mm(lo, bd_ref[...]))
    return outs[0] if len(outs) == 1 else jnp.concatenate(outs, axis=1)


def _shift_rows(u, k, edge_rows):
    out = pltpu.roll(u, k, 0)
    row = lax.broadcasted_iota(jnp.int32, (8, u.shape[1]), 0)
    head = out[0:8]
    for j in range(k):
        head = jnp.where(row == j, edge_rows[k - 1 - j], head)
    return jnp.concatenate([head, out[8:]], axis=0)


def _shift_rows_up(u, k, edge_rows):
    n = u.shape[0]
    out = pltpu.roll(u, n - k, 0)
    row = lax.broadcasted_iota(jnp.int32, (8, u.shape[1]), 0)
    tail = out[n - 8:n]
    for j in range(k):
        tail = jnp.where(row == 8 - k + j, edge_rows[j], tail)
    return jnp.concatenate([out[0:n - 8], tail], axis=0)


def _conv_fwd(u, c1, c2, w_ref, b_ref):
    u1 = _shift_rows(u, 1, (c1,))
    u2 = _shift_rows(u, 2, (c1, c2))
    y = u2 * w_ref[0:1, :] + u1 * w_ref[1:2, :] + u * w_ref[2:3, :] + b_ref[...]
    return y, u1, u2


def _conv_bwd_input(dy, n1row, n2row, w_ref):
    d1 = _shift_rows_up(dy, 1, (n1row,))
    d2 = _shift_rows_up(dy, 2, (n1row, n2row))
    return dy * w_ref[2:3, :] + d1 * w_ref[1:2, :] + d2 * w_ref[0:1, :]


def _sigmoid(x):
    return 1.0 / (1.0 + jnp.exp(-x))


def _inproj_fwd(x, g_mix, w_in, conv_w, conv_b, qg, kg, bd, tm):
    t = x.shape[0]

    def body(x_ref, g_ref, w_ref, cw_ref, cb_ref, qg_ref, kg_ref, bd_ref,
             zc_ref, zqk_ref, yc_ref, q_ref, k_ref, v_ref, carry_ref):
        @pl.when(pl.program_id(0) == 0)
        def _():
            carry_ref[...] = jnp.zeros_like(carry_ref)

        _, xhat = _rms_stats(x_ref[...])
        h = (xhat * g_ref[...]).astype(BF16)
        zconv = _mm(h, w_ref[:, 0:3 * CONV_W])
        zc_ref[...] = zconv.astype(BF16)
        u = zconv[:, CONV_W:2 * CONV_W] * zconv[:, 2 * CONV_W:3 * CONV_W]
        cv, _, _ = _conv_fwd(u, carry_ref[7:8, :], carry_ref[6:7, :], cw_ref, cb_ref)
        yc_ref[...] = (zconv[:, 0:CONV_W] * cv).astype(BF16)
        carry_ref[...] = u[tm - 8:tm, :]

        zqk = _mm(h, w_ref[:, 3 * CONV_W:3 * CONV_W + 2 * ATTN_W])
        zqk_ref[...] = zqk.astype(BF16)
        for j, (gain_ref, out_ref, scale) in enumerate(((qg_ref, q_ref, ATTN_SCALE), (kg_ref, k_ref, 1.0))):
            z = zqk[:, j * ATTN_W:(j + 1) * ATTN_W]
            r = lax.rsqrt(_seg_sum64(z * z, bd_ref) * (1.0 / HEAD_DIM) + EPS)
            out_ref[...] = z * r * gain_ref[...] * scale
        v_ref[...] = _mm(h, w_ref[:, 3 * CONV_W + 2 * ATTN_W:IN_COLS])

    def blk(c):
        return pl.BlockSpec((tm, c), lambda i: (i, 0))

    return pl.pallas_call(
        body, name="inproj_fwd", grid=(t // tm,),
        in_specs=[blk(D_MODEL), _full((1, D_MODEL)), _full((D_MODEL, IN_COLS)), _full((3, CONV_W)),
                  _full((1, CONV_W)), _full((1, ATTN_W)), _full((1, ATTN_W)), _full((256, 256))],
        out_specs=[blk(3 * CONV_W), blk(2 * ATTN_W), blk(CONV_W), blk(ATTN_W), blk(ATTN_W), blk(ATTN_W)],
        out_shape=[jax.ShapeDtypeStruct((t, 3 * CONV_W), BF16), jax.ShapeDtypeStruct((t, 2 * ATTN_W), BF16),
                   jax.ShapeDtypeStruct((t, CONV_W), BF16), jax.ShapeDtypeStruct((t, ATTN_W), F32),
                   jax.ShapeDtypeStruct((t, ATTN_W), F32), jax.ShapeDtypeStruct((t, ATTN_W), F32)],
        scratch_shapes=[pltpu.VMEM((8, CONV_W), F32)],
        compiler_params=_cparams("arbitrary"),
    )(x, g_mix, w_in, conv_w, conv_b, qg, kg, bd)


SUPER = 16 * QK_BLOCK
KEYS = 2 * QK_BLOCK
UNITS = SUPER // QK_BLOCK


def _rows(start, size, dil):
    return pl.ds(start, size) if dil == 1 else pl.ds(start, size, stride=dil)


def _attn_bias(sl_ref, dil):
    qi = lax.broadcasted_iota(jnp.int32, (KEYS, KEYS), 0)
    kj = lax.broadcasted_iota(jnp.int32, (KEYS, KEYS), 1)
    step = jnp.bitwise_and(qi, QK_BLOCK - 1) + QK_BLOCK - kj
    slope = jnp.where(qi < QK_BLOCK, sl_ref[0, 0:1, 0:1], sl_ref[0, 1:2, 0:1])
    bias = jnp.where(jnp.logical_and(step >= 0, step <= QK_BLOCK), -slope * (step * dil).astype(F32), -jnp.inf)
    return bias, kj >= QK_BLOCK


def _unit_start(u, dil):
    if dil == 1:
        return pl.multiple_of(u * QK_BLOCK, QK_BLOCK)
    if dil == 4:
        return jnp.bitwise_and(u, 3) + (u // 4) * (4 * QK_BLOCK)
    return u


def _stack_heads(a, head0):
    zero = jnp.zeros_like(a)
    return jnp.concatenate([jnp.where(head0, a, zero), jnp.where(head0, zero, a)], axis=0)


def _attn_fwd(q, k, v, slopes):
    t = q.shape[0]
    nsb = t // SUPER

    def body(q_ref, kc_ref, kp_ref, vc_ref, vp_ref, sl_ref, o_ref, l_ref, e_ref, m_ref, kk, vv, ob, lb):
        s = pl.program_id(1)
        kk[0:SUPER, :] = kp_ref[...]
        kk[SUPER:, :] = kc_ref[...]
        vv[0:SUPER, :] = vp_ref[...]
        vv[SUPER:, :] = vc_ref[...]
        head0 = lax.broadcasted_iota(jnp.int32, (QK_BLOCK, QK_BLOCK), 1) < HEAD_DIM

        for b, dil in enumerate(DILATIONS):
            bias, own_half = _attn_bias(sl_ref, dil)

            def unit(u, carry, b=b, dil=dil, bias=bias, own_half=own_half):
                start = _unit_start(u, dil)
                first_key = SUPER + start - QK_BLOCK * dil
                q2 = _stack_heads(q_ref[_rows(start, QK_BLOCK, dil), :].astype(BF16), head0)
                k2 = kk[_rows(first_key, KEYS, dil), :].astype(BF16)
                v2 = vv[_rows(first_key, KEYS, dil), :].astype(BF16)
                has_prev = jnp.logical_or(s > 0, start >= QK_BLOCK * dil)
                sc = jnp.where(jnp.logical_or(own_half, has_prev), _mm_nt(q2, k2) + bias, -jnp.inf)
                m = jnp.max(sc, axis=-1, keepdims=True)
                e = jnp.exp(sc - m)
                den = jnp.sum(e, axis=-1, keepdims=True)
                eb = e.astype(BF16)
                e_ref[b * UNITS + u] = eb
                o2 = _mm(eb, v2) / den
                l2 = m + jnp.log(den)
                ob[b, _rows(start, QK_BLOCK, dil), :] = jnp.where(head0, o2[0:QK_BLOCK], o2[QK_BLOCK:])
                lb[b, _rows(start, QK_BLOCK, dil), :] = jnp.where(head0, l2[0:QK_BLOCK], l2[QK_BLOCK:])
                m_ref[b, _rows(start, QK_BLOCK, dil), :] = jnp.where(head0, m[0:QK_BLOCK], m[QK_BLOCK:])
                return carry

            lax.fori_loop(0, UNITS, unit, 0, unroll=16)

        def merge(i, carry):
            rows = pl.ds(pl.multiple_of(i * 256, 256), 256)
            la, lb_, lc = lb[0, rows, :], lb[1, rows, :], lb[2, rows, :]
            mx = jnp.maximum(jnp.maximum(la, lb_), lc)
            wa, wb, wc = jnp.exp(la - mx), jnp.exp(lb_ - mx), jnp.exp(lc - mx)
            sw = wa + wb + wc
            o_ref[rows, :] = ((wa * ob[0, rows, :] + wb * ob[1, rows, :] + wc * ob[2, rows, :]) / sw).astype(BF16)
            l_ref[rows, :] = mx + jnp.log(sw)
            return carry

        lax.fori_loop(0, SUPER // 256, merge, 0)

    cur = pl.BlockSpec((SUPER, QK_BLOCK), lambda p, s: (s, p))
    prev = pl.BlockSpec((SUPER, QK_BLOCK), lambda p, s: (jnp.maximum(s - 1, 0), p))
    return pl.pallas_call(
        body, name="attn_fwd", grid=(4, nsb),
        in_specs=[cur, cur, prev, cur, prev, pl.BlockSpec((1, 2, QK_BLOCK), lambda p, s: (p, 0, 0))],
        out_specs=[cur, cur, pl.BlockSpec((None, None, 3 * UNITS, KEYS, KEYS), lambda p, s: (p, s, 0, 0, 0)),
                   pl.BlockSpec((3, SUPER, QK_BLOCK), lambda p, s: (0, s, p))],
        out_shape=[jax.ShapeDtypeStruct((t, ATTN_W), BF16), jax.ShapeDtypeStruct((t, ATTN_W), F32),
                   jax.ShapeDtypeStruct((4, nsb, 3 * UNITS, KEYS, KEYS), BF16),
                   jax.ShapeDtypeStruct((3, t, ATTN_W), F32)],
        scratch_shapes=[pltpu.VMEM((2 * SUPER, QK_BLOCK), F32), pltpu.VMEM((2 * SUPER, QK_BLOCK), F32),
                        pltpu.VMEM((3, SUPER, QK_BLOCK), F32), pltpu.VMEM((3, SUPER, QK_BLOCK), F32)],
        compiler_params=_cparams("parallel", "arbitrary"),
    )(q, k, k, v, v, slopes)


def _outproj_fwd(ya, yc, x, goc, goa, w_out, tm):
    t = x.shape[0]

    def body(ya_ref, yc_ref, x_ref, goc_ref, goa_ref, w_ref, x1_ref):
        _, ychat = _rms_stats(yc_ref[...].astype(F32))
        _, yahat = _rms_stats(ya_ref[...].astype(F32))
        acc = _mm((ychat * goc_ref[...]).astype(BF16), w_ref[0:CONV_W, :])
        acc += _mm((yahat * goa_ref[...]).astype(BF16), w_ref[CONV_W:, :])
        x1_ref[...] = x_ref[...] + acc

    def blk(c):
        return pl.BlockSpec((tm, c), lambda i: (i, 0))

    return pl.pallas_call(
        body, name="outproj_fwd", grid=(t // tm,),
        in_specs=[blk(ATTN_W), blk(CONV_W), blk(D_MODEL), _full((1, CONV_W)), _full((1, ATTN_W)),
                  _full((D_MODEL, D_MODEL))],
        out_specs=blk(D_MODEL),
        out_shape=jax.ShapeDtypeStruct((t, D_MODEL), F32),
        compiler_params=_cparams("parallel"),
    )(ya, yc, x, goc, goa, w_out)


def _ffn_fwd(x1, g_ffn, w_gate_t, w_up_t, w_down, fcw, fcb, tm):
    t = x1.shape[0]

    def body(x_ref, g_ref, wg_ref, wu_ref, wd_ref, cw_ref, cb_ref, gp_ref, up_ref, h_ref, x2_ref, carry_ref):
        @pl.when(pl.program_id(0) == 0)
        def _():
            carry_ref[...] = jnp.zeros_like(carry_ref)

        xv = x_ref[...]
        _, xhat = _rms_stats(xv)
        h = (xhat * g_ref[...]).astype(BF16)
        h_ref[...] = h
        gp = _mm_nt(h, wg_ref[...])
        gp_ref[...] = gp.astype(BF16)
        gate, _, _ = _conv_fwd(gp, carry_ref[7:8, :], carry_ref[6:7, :], cw_ref, cb_ref)
        carry_ref[...] = gp[tm - 8:tm, :]
        up = _mm_nt(h, wu_ref[...])
        up_ref[...] = up.astype(BF16)
        a = (gate * _sigmoid(gate) * up).astype(BF16)
        x2_ref[...] = xv + _mm(a, wd_ref[...])

    def blk(c):
        return pl.BlockSpec((tm, c), lambda i: (i, 0))

    def weight():
        return pl.BlockSpec((D_FF, D_MODEL), lambda i: (0, 0), pipeline_mode=pl.Buffered(1))

    return pl.pallas_call(
        body, name="ffn_fwd", grid=(t // tm,),
        in_specs=[blk(D_MODEL), _full((1, D_MODEL)), weight(), weight(), weight(), _full((3, D_FF)), _full((1, D_FF))],
        out_specs=[blk(D_FF), blk(D_FF), blk(D_MODEL), blk(D_MODEL)],
        out_shape=[jax.ShapeDtypeStruct((t, D_FF), BF16), jax.ShapeDtypeStruct((t, D_FF), BF16),
                   jax.ShapeDtypeStruct((t, D_MODEL), BF16), jax.ShapeDtypeStruct((t, D_MODEL), F32)],
        scratch_shapes=[pltpu.VMEM((8, D_FF), F32)],
        compiler_params=_cparams("arbitrary", vmem=V7X_VMEM_LIMIT_LARGE),
    )(x1, g_ffn, w_gate_t, w_up_t, w_down, fcw, fcb)


def _ple_fwd_bwd(x2, p, target, g_ple, w_pg, w_pp, tm):
    t = x2.shape[0]

    def body(x_ref, p_ref, t_ref, g_ref, wg_ref, wp_ref, dx_ref, dxb_ref, loss_ref, dwgb_ref, dwp_ref, dg_ref,
             dwg_ref):
        @pl.when(pl.program_id(0) == 0)
        def _():
            loss_ref[...] = jnp.zeros_like(loss_ref)
            dwg_ref[...] = jnp.zeros_like(dwg_ref)
            dwp_ref[...] = jnp.zeros_like(dwp_ref)
            dg_ref[...] = jnp.zeros_like(dg_ref)

        xv = x_ref[...]
        r, xhat = _rms_stats(xv)
        g = g_ref[...]
        h = (xhat * g).astype(BF16)
        pg = _sigmoid(_mm(h, wg_ref[...]))
        pb = p_ref[...].astype(BF16)
        pp = _mm(pb, wp_ref[...])
        err = xv + pg * pp - t_ref[...]
        loss_ref[...] += 0.5 * jnp.sum(jnp.mean(err * err, axis=-1, keepdims=True))
        dx3 = err * (1.0 / D_MODEL)
        d_pp = (dx3 * pg).astype(BF16)
        d_pre = (dx3 * pp * pg * (1.0 - pg)).astype(BF16)
        dwp_ref[...] += _mm_tn(pb, d_pp)
        dwg_ref[...] += _mm_tn(h, d_pre)
        dh = _mm_nt(d_pre, wg_ref[...])
        dg_ref[...] += jnp.sum(dh * xhat, axis=0, keepdims=True)
        dx2 = dx3 + _rms_bwd(dh, xhat, r, g)
        dx_ref[...] = dx2
        dxb_ref[...] = dx2.astype(BF16)

        @pl.when(pl.program_id(0) == t // tm - 1)
        def _():
            dwgb_ref[...] = dwg_ref[...].astype(BF16)

    def blk(c):
        return pl.BlockSpec((tm, c), lambda i: (i, 0))

    return pl.pallas_call(
        body, name="ple_fwd_bwd", grid=(t // tm,),
        in_specs=[blk(D_MODEL), blk(PLE_DIM), blk(D_MODEL), _full((1, D_MODEL)), _full((D_MODEL, D_MODEL)),
                  _full((PLE_DIM, D_MODEL))],
        out_specs=[blk(D_MODEL), blk(D_MODEL), _full((8, 128)), _full((D_MODEL, D_MODEL)),
                   _full((PLE_DIM, D_MODEL)), _full((1, D_MODEL))],
        out_shape=[jax.ShapeDtypeStruct((t, D_MODEL), F32), jax.ShapeDtypeStruct((t, D_MODEL), BF16),
                   jax.ShapeDtypeStruct((8, 128), F32),
                   jax.ShapeDtypeStruct((D_MODEL, D_MODEL), BF16), jax.ShapeDtypeStruct((PLE_DIM, D_MODEL), F32),
                   jax.ShapeDtypeStruct((1, D_MODEL), F32)],
        scratch_shapes=[pltpu.VMEM((D_MODEL, D_MODEL), F32)],
        compiler_params=_cparams("arbitrary"),
    )(x2, p, target, g_ple, w_pg, w_pp)


def _ffn_bwd(dx2, h2, gp, up, w_gate, w_up, w_down, fcw, fcb, tm):
    t = dx2.shape[0]
    nblk = t // tm
    fc = D_FF // FF_CHUNKS
    half = tm // FFN_BWD_PARTS

    def body(dx_ref, h_ref, gp_ref, gph_ref, up_ref, wg_ref, wu_ref, wd_ref, cw_ref, cb_ref,
             dh_ref, dwd_hbm, dwu_hbm, dwg_hbm, dcw_ref, dcb_ref, carry_ref, a_scr, dup_scr, dgp_scr,
             dwd_acc, dwu_acc, dwg_acc, stage, stage_sem):
        i = pl.program_id(1)

        @pl.when(i == 0)
        def _():
            carry_ref[...] = jnp.zeros_like(carry_ref)
            dwd_acc[...] = jnp.zeros_like(dwd_acc)
            dwu_acc[...] = jnp.zeros_like(dwu_acc)
            dwg_acc[...] = jnp.zeros_like(dwg_acc)
            dcw_ref[...] = jnp.zeros_like(dcw_ref)
            dcb_ref[...] = jnp.zeros_like(dcb_ref)

        keep = (i < nblk - 1).astype(F32)
        later = carry_ref[...]
        for hf in reversed(range(FFN_BWD_PARTS)):
            rows = slice(hf * half, (hf + 1) * half)
            dxb = dx_ref[rows, :]
            gp_v = gp_ref[rows, :].astype(F32)
            if hf > 0:
                before = gp_ref[hf * half - 16:hf * half, :].astype(F32)
            else:
                before = gph_ref[...].astype(F32) * keep
            gate, gp1, gp2 = _conv_fwd(gp_v, before[15:16, :], before[14:15, :], cw_ref, cb_ref)
            s = _sigmoid(gate)
            silu = gate * s
            up_v = up_ref[rows, :].astype(F32)
            da = _mm_nt(dxb, wd_ref[...])
            a_scr[rows, :] = (silu * up_v).astype(BF16)
            d_up = (da * silu).astype(BF16)
            dup_scr[rows, :] = d_up
            d_gate = da * up_v * (s * (1.0 + gate * (1.0 - s)))
            d_gp = _conv_bwd_input(d_gate, later[0:1, :], later[1:2, :], cw_ref).astype(BF16)
            dgp_scr[rows, :] = d_gp
            later = d_gate[0:8, :]
            dcw_ref[0:1, :] += jnp.sum(d_gate * gp2, axis=0, keepdims=True)
            dcw_ref[1:2, :] += jnp.sum(d_gate * gp1, axis=0, keepdims=True)
            dcw_ref[2:3, :] += jnp.sum(d_gate * gp_v, axis=0, keepdims=True)
            dcb_ref[...] += jnp.sum(d_gate, axis=0, keepdims=True)
            dh_ref[rows, :] = (_mm(d_gp, wg_ref[...]) + _mm(d_up, wu_ref[...])).astype(BF16)
        carry_ref[...] = later
        dwd_acc[...] += _mm_tn(a_scr[...], dx_ref[...])
        dwu_acc[...] += _mm_tn(h_ref[...], dup_scr[...])
        dwg_acc[...] += _mm_tn(h_ref[...], dgp_scr[...])

        @pl.when(i == nblk - 1)
        def _():
            rows = pl.ds(pl.multiple_of(pl.program_id(0) * fc, 16), fc)
            for acc, out, flip in ((dwd_acc, dwd_hbm, False), (dwu_acc, dwu_hbm, True), (dwg_acc, dwg_hbm, True)):
                stage[...] = (acc[...].T if flip else acc[...]).astype(BF16)
                copy = pltpu.make_async_copy(stage, out.at[rows, :], stage_sem)
                copy.start()
                copy.wait()

    def rev(i):
        return nblk - 1 - i

    one = pl.Buffered(1)
    in_specs = [
        pl.BlockSpec((tm, D_MODEL), lambda j, i: (rev(i), 0)),
        pl.BlockSpec((tm, D_MODEL), lambda j, i: (rev(i), 0)),
        pl.BlockSpec((tm, fc), lambda j, i: (rev(i), j)),
        pl.BlockSpec((16, fc), lambda j, i: (jnp.maximum(rev(i) * (tm // 16) - 1, 0), j)),
        pl.BlockSpec((tm, fc), lambda j, i: (rev(i), j)),
        pl.BlockSpec((fc, D_MODEL), lambda j, i: (j, 0), pipeline_mode=one),
        pl.BlockSpec((fc, D_MODEL), lambda j, i: (j, 0), pipeline_mode=one),
        pl.BlockSpec((fc, D_MODEL), lambda j, i: (j, 0), pipeline_mode=one),
        pl.BlockSpec((3, fc), lambda j, i: (0, j)),
        pl.BlockSpec((1, fc), lambda j, i: (0, j)),
    ]
    out_specs = [
        pl.BlockSpec((None, tm, D_MODEL), lambda j, i: (j, rev(i), 0)),
        ANY, ANY, ANY,
        pl.BlockSpec((3, fc), lambda j, i: (0, j)),
        pl.BlockSpec((1, fc), lambda j, i: (0, j)),
    ]
    return pl.pallas_call(
        body, name="ffn_bwd", grid=(FF_CHUNKS, nblk), in_specs=in_specs, out_specs=out_specs,
        out_shape=[jax.ShapeDtypeStruct((FF_CHUNKS, t, D_MODEL), BF16), jax.ShapeDtypeStruct((D_FF, D_MODEL), BF16),
                   jax.ShapeDtypeStruct((D_FF, D_MODEL), BF16), jax.ShapeDtypeStruct((D_FF, D_MODEL), BF16),
                   jax.ShapeDtypeStruct((3, D_FF), F32), jax.ShapeDtypeStruct((1, D_FF), F32)],
        scratch_shapes=[pltpu.VMEM((8, fc), F32), pltpu.VMEM((tm, fc), BF16), pltpu.VMEM((tm, fc), BF16),
                        pltpu.VMEM((tm, fc), BF16), pltpu.VMEM((fc, D_MODEL), F32), pltpu.VMEM((D_MODEL, fc), F32),
                        pltpu.VMEM((D_MODEL, fc), F32), pltpu.VMEM((fc, D_MODEL), BF16), pltpu.SemaphoreType.DMA],
        compiler_params=_cparams("arbitrary", "arbitrary", vmem=V7X_VMEM_LIMIT_LARGE),
    )(dx2, h2, gp, gp, up, w_gate, w_up, w_down, fcw, fcb)


def _outproj_bwd(dh2, dx2, x1, g_ffn, w_out, yc, ya, goc, goa, zconv, conv_w, conv_b, bd, tm):
    t = x1.shape[0]
    nblk = t // tm

    def body(dh_ref, dx2_ref, x1_ref, g_ref, w_ref, yc_ref, ya_ref, goc_ref, goa_ref, zc_ref, zch_ref, cw_ref, cb_ref,
             bd_ref, dx1_ref, dya_ref, dd_ref, dzc_ref, dwb_ref, dg_ref, dgoc_ref, dgoa_ref, dcw_ref, dcb_ref,
             carry_ref, dw_ref):
        i = pl.program_id(0)

        @pl.when(i == 0)
        def _():
            carry_ref[...] = jnp.zeros_like(carry_ref)
            for ref in (dw_ref, dg_ref, dgoc_ref, dgoa_ref, dcw_ref, dcb_ref):
                ref[...] = jnp.zeros_like(ref)

        keep = (i < nblk - 1).astype(F32)
        dh2_v = dh_ref[0].astype(F32)
        for j in range(1, FF_CHUNKS):
            dh2_v = dh2_v + dh_ref[j].astype(F32)
        r, xhat = _rms_stats(x1_ref[...])
        dg_ref[...] += jnp.sum(dh2_v * xhat, axis=0, keepdims=True)
        dx1 = dx2_ref[...] + _rms_bwd(dh2_v, xhat, r, g_ref[...])
        dx1_ref[...] = dx1
        dx1b = dx1.astype(BF16)
        dy = _mm_nt(dx1b, w_ref[...])

        yc_v = yc_ref[...].astype(F32)
        rc, ychat = _rms_stats(yc_v)
        dw_ref[0:CONV_W, :] += _mm_tn((ychat * goc_ref[...]).astype(BF16), dx1b)
        dyc = dy[:, 0:CONV_W]
        dgoc_ref[...] += jnp.sum(dyc * ychat, axis=0, keepdims=True)
        d_yc = _rms_bwd(dyc, ychat, rc, goc_ref[...])

        ya_v = ya_ref[...].astype(F32)
        ra, yahat = _rms_stats(ya_v)
        dw_ref[CONV_W:, :] += _mm_tn((yahat * goa_ref[...]).astype(BF16), dx1b)
        dya = dy[:, CONV_W:]
        dgoa_ref[...] += jnp.sum(dya * yahat, axis=0, keepdims=True)
        d_ya = _rms_bwd(dya, yahat, ra, goa_ref[...])
        dya_ref[...] = d_ya
        dd_ref[...] = _seg_sum64(d_ya * ya_v, bd_ref)

        zb = zc_ref[:, 0:CONV_W].astype(F32)
        zc = zc_ref[:, CONV_W:2 * CONV_W].astype(F32)
        zx = zc_ref[:, 2 * CONV_W:3 * CONV_W].astype(F32)
        u = zc * zx
        uh = (zch_ref[:, CONV_W:2 * CONV_W].astype(F32) * zch_ref[:, 2 * CONV_W:3 * CONV_W].astype(F32)) * keep
        cv, u1, u2 = _conv_fwd(u, uh[15:16, :], uh[14:15, :], cw_ref, cb_ref)
        d_cv = d_yc * zb
        d_u = _conv_bwd_input(d_cv, carry_ref[0:1, :], carry_ref[1:2, :], cw_ref)
        carry_ref[...] = d_cv[0:8, :]
        dcw_ref[0:1, :] += jnp.sum(d_cv * u2, axis=0, keepdims=True)
        dcw_ref[1:2, :] += jnp.sum(d_cv * u1, axis=0, keepdims=True)
        dcw_ref[2:3, :] += jnp.sum(d_cv * u, axis=0, keepdims=True)
        dcb_ref[...] += jnp.sum(d_cv, axis=0, keepdims=True)
        dzc_ref[:, 0:CONV_W] = (d_yc * cv).astype(BF16)
        dzc_ref[:, CONV_W:2 * CONV_W] = (d_u * zx).astype(BF16)
        dzc_ref[:, 2 * CONV_W:3 * CONV_W] = (d_u * zc).astype(BF16)

        @pl.when(i == nblk - 1)
        def _():
            dwb_ref[...] = dw_ref[...].astype(BF16)

    def rev(i):
        return nblk - 1 - i

    def blk(c):
        return pl.BlockSpec((tm, c), lambda i: (rev(i), 0))

    in_specs = [
        pl.BlockSpec((FF_CHUNKS, tm, D_MODEL), lambda i: (0, rev(i), 0)),
        blk(D_MODEL), blk(D_MODEL), _full((1, D_MODEL)), _full((D_MODEL, D_MODEL)),
        blk(CONV_W), blk(ATTN_W), _full((1, CONV_W)), _full((1, ATTN_W)),
        blk(3 * CONV_W),
        pl.BlockSpec((16, 3 * CONV_W), lambda i: (jnp.maximum(rev(i) * (tm // 16) - 1, 0), 0)),
        _full((3, CONV_W)), _full((1, CONV_W)), _full((256, 256)),
    ]
    out_specs = [blk(D_MODEL), blk(ATTN_W), blk(ATTN_W), blk(3 * CONV_W), _full((D_MODEL, D_MODEL)),
                 _full((1, D_MODEL)), _full((1, CONV_W)), _full((1, ATTN_W)), _full((3, CONV_W)), _full((1, CONV_W))]
    return pl.pallas_call(
        body, name="outproj_bwd", grid=(nblk,), in_specs=in_specs, out_specs=out_specs,
        out_shape=[jax.ShapeDtypeStruct((t, D_MODEL), F32), jax.ShapeDtypeStruct((t, ATTN_W), F32),
                   jax.ShapeDtypeStruct((t, ATTN_W), F32), jax.ShapeDtypeStruct((t, 3 * CONV_W), BF16),
                   jax.ShapeDtypeStruct((D_MODEL, D_MODEL), BF16), jax.ShapeDtypeStruct((1, D_MODEL), F32),
                   jax.ShapeDtypeStruct((1, CONV_W), F32), jax.ShapeDtypeStruct((1, ATTN_W), F32),
                   jax.ShapeDtypeStruct((3, CONV_W), F32), jax.ShapeDtypeStruct((1, CONV_W), F32)],
        scratch_shapes=[pltpu.VMEM((8, CONV_W), F32), pltpu.VMEM((D_MODEL, D_MODEL), F32)],
        compiler_params=_cparams("arbitrary"),
    )(dh2, dx2, x1, g_ffn, w_out, yc, ya, goc, goa, zconv, zconv, conv_w, conv_b, bd)


def _attn_bwd(q, k, v, dya, lse, dd, e_all, m_all, after):
    t = q.shape[0]
    nsb = t // SUPER

    def body(q_ref, kc_ref, kp_ref, vc_ref, vp_ref, dy_ref, l_ref, d_ref, e_ref, m_ref, after_ref,
             dq_ref, dk_ref, dv_ref, kk, vv, dkacc, dvacc, dwide):
        s = pl.program_id(1)

        @pl.when(s == 0)
        def _():
            dkacc[...] = jnp.zeros_like(dkacc)
            dvacc[...] = jnp.zeros_like(dvacc)

        dkacc[0:SUPER, :] = dkacc[SUPER:, :]
        dvacc[0:SUPER, :] = dvacc[SUPER:, :]
        dkacc[SUPER:, :] = jnp.zeros((SUPER, QK_BLOCK), F32)
        dvacc[SUPER:, :] = jnp.zeros((SUPER, QK_BLOCK), F32)

        @pl.when(s < nsb)
        def _():
            kk[0:SUPER, :] = kp_ref[...]
            kk[SUPER:, :] = kc_ref[...]
            vv[0:SUPER, :] = vp_ref[...]
            vv[SUPER:, :] = vc_ref[...]
            head0 = lax.broadcasted_iota(jnp.int32, (QK_BLOCK, QK_BLOCK), 1) < HEAD_DIM

            def widened(a):
                other = pltpu.roll(a, HEAD_DIM, 1)
                first = lax.broadcasted_iota(jnp.int32, a.shape, 1) < HEAD_DIM
                return jnp.where(first, a, other), jnp.where(first, other, a)

            def stacked(h0, h1):
                return jnp.concatenate([jnp.concatenate([h0, h0], axis=1), jnp.concatenate([h1, h1], axis=1)], axis=0)

            def widen_dd(i, carry):
                rows = pl.ds(pl.multiple_of(i * 256, 256), 256)
                dwide[0, rows, :], dwide[1, rows, :] = widened(d_ref[rows, :])
                return carry

            lax.fori_loop(0, SUPER // 256, widen_dd, 0)

            for b, dil in enumerate(DILATIONS):
                def unit(u, carry, b=b, dil=dil):
                    start = _unit_start(u, dil)
                    first_key = SUPER + start - QK_BLOCK * dil
                    qrows = _rows(start, QK_BLOCK, dil)
                    krows = _rows(first_key, KEYS, dil)
                    q2 = _stack_heads(q_ref[qrows, :].astype(BF16), head0)
                    dy2 = _stack_heads(dy_ref[qrows, :].astype(BF16), head0)
                    g2 = stacked(*widened(jnp.exp(m_ref[b, qrows, :] - l_ref[qrows, :])))
                    d2 = stacked(dwide[0, qrows, :], dwide[1, qrows, :])
                    k2 = kk[krows, :].astype(BF16)
                    v2 = vv[krows, :].astype(BF16)
                    prob = e_ref[b * UNITS + u].astype(F32) * g2
                    ds = (prob * (_mm_nt(dy2, v2) - d2)).astype(BF16)
                    dvacc[krows, :] += _mm_tn(prob.astype(BF16), dy2)
                    dkacc[krows, :] += _mm_tn(ds, q2)
                    dq2 = _mm(ds, k2)
                    dq = jnp.where(head0, dq2[0:QK_BLOCK], dq2[QK_BLOCK:]) * ATTN_SCALE
                    if b == 0:
                        dq_ref[qrows, :] = dq
                    else:
                        dq_ref[qrows, :] += dq
                    return carry

                lax.fori_loop(0, UNITS, unit, 0, unroll=8)

        dk_ref[...] = dkacc[0:SUPER, :]
        dv_ref[...] = dvacc[0:SUPER, :].astype(BF16)

    def cur_map(p, s):
        return (jnp.minimum(s, nsb - 1), p)

    def prev_map(p, s):
        return (jnp.clip(s - 1, 0, nsb - 1), p)

    cur = pl.BlockSpec((SUPER, QK_BLOCK), cur_map)
    prev = pl.BlockSpec((SUPER, QK_BLOCK), prev_map)
    return pl.pallas_call(
        body, name="attn_bwd", grid=(4, nsb + 1),
        in_specs=[cur, cur, prev, cur, prev, cur, cur, cur,
                  pl.BlockSpec((None, None, 3 * UNITS, KEYS, KEYS), lambda p, s: (p, jnp.minimum(s, nsb - 1), 0, 0, 0)),
                  pl.BlockSpec((3, SUPER, QK_BLOCK), lambda p, s: (0, jnp.minimum(s, nsb - 1), p)),
                  pl.BlockSpec(memory_space=pl.ANY)],
        out_specs=[cur, prev, prev],
        out_shape=[jax.ShapeDtypeStruct((t, ATTN_W), F32), jax.ShapeDtypeStruct((t, ATTN_W), F32),
                   jax.ShapeDtypeStruct((t, ATTN_W), BF16)],
        scratch_shapes=[pltpu.VMEM((2 * SUPER, QK_BLOCK), F32)] * 4 + [pltpu.VMEM((2, SUPER, QK_BLOCK), F32)],
        compiler_params=_cparams("parallel", "arbitrary"),
    )(q, k, k, v, v, dya, lse, dd, e_all, m_all, after)


def _inproj_bwd(dq, dk, dv, dzconv, zqk, x, dx1, g_mix, w_in, qg, kg, bd, tm):
    t = x.shape[0]
    nblk = t // tm
    shard = IN_COLS // N_DEV

    def body(dq_ref, dk_ref, dv_ref, dzc_ref, zqk_ref, x_ref, dx1_ref, g_ref, w_ref, qg_ref,
             kg_ref, bd_ref, dx_ref, dw_hbm, dg_ref, dqg_ref, dkg_ref, dw_ref, stage, stage_sem):
        @pl.when(pl.program_id(0) == 0)
        def _():
            for ref in (dw_ref, dg_ref, dqg_ref, dkg_ref):
                ref[...] = jnp.zeros_like(ref)

        parts = [dzc_ref[...]]
        for j, (dn_ref, gain_ref, dgain_ref) in enumerate(((dq_ref, qg_ref, dqg_ref), (dk_ref, kg_ref, dkg_ref))):
            dn = dn_ref[...]
            z = zqk_ref[:, j * ATTN_W:(j + 1) * ATTN_W].astype(F32)
            r = lax.rsqrt(_seg_sum64(z * z, bd_ref) * (1.0 / HEAD_DIM) + EPS)
            zhat = z * r
            dgain_ref[...] += jnp.sum(dn * zhat, axis=0, keepdims=True)
            gd = dn * gain_ref[...]
            parts.append((r * (gd - zhat * (_seg_sum64(gd * zhat, bd_ref) * (1.0 / HEAD_DIM)))).astype(BF16))
        parts.append(dv_ref[...].astype(BF16))
        dz = jnp.concatenate(parts, axis=1)

        r, xhat = _rms_stats(x_ref[...])
        g = g_ref[...]
        dw_ref[...] += _mm_tn((xhat * g).astype(BF16), dz)
        dh = _mm_nt(dz, w_ref[...])
        dg_ref[...] += jnp.sum(dh * xhat, axis=0, keepdims=True)
        dx_ref[...] = dx1_ref[...] + _rms_bwd(dh, xhat, r, g)

        @pl.when(pl.program_id(0) == nblk - 1)
        def _():
            for k in range(N_DEV):
                stage[...] = dw_ref[:, k * shard:(k + 1) * shard].astype(BF16)
                copy = pltpu.make_async_copy(stage, dw_hbm.at[k], stage_sem)
                copy.start()
                copy.wait()

    def blk(c):
        return pl.BlockSpec((tm, c), lambda i: (i, 0))

    return pl.pallas_call(
        body, name="inproj_bwd", grid=(nblk,),
        in_specs=[blk(ATTN_W)] * 3 + [blk(3 * CONV_W), blk(2 * ATTN_W), blk(D_MODEL), blk(D_MODEL), _full((1, D_MODEL)),
                                      _full((D_MODEL, IN_COLS)), _full((1, ATTN_W)), _full((1, ATTN_W)),
                                      _full((256, 256))],
        out_specs=[blk(D_MODEL), ANY, _full((1, D_MODEL)), _full((1, ATTN_W)), _full((1, ATTN_W))],
        out_shape=[jax.ShapeDtypeStruct((t, D_MODEL), F32), jax.ShapeDtypeStruct((N_DEV, D_MODEL, shard), BF16),
                   jax.ShapeDtypeStruct((1, D_MODEL), F32), jax.ShapeDtypeStruct((1, ATTN_W), F32),
                   jax.ShapeDtypeStruct((1, ATTN_W), F32)],
        scratch_shapes=[pltpu.VMEM((D_MODEL, IN_COLS), F32), pltpu.VMEM((D_MODEL, shard), BF16),
                        pltpu.SemaphoreType.DMA],
        compiler_params=_cparams("arbitrary"),
    )(dq, dk, dv, dzconv, zqk, x, dx1, g_mix, w_in, qg, kg, bd)


def _ordered_after(a, token):
    return a if token is None else a + token


def _local_step(x, p, target, w, tms, hooks=None):
    hooks = hooks or {}
    bd = jnp.asarray(np.kron(np.eye(4, dtype=np.float32), np.ones((HEAD_DIM, HEAD_DIM), np.float32)), BF16)
    qg = jnp.tile(w["q_norm_g"], (1, 8))
    kg = jnp.tile(w["k_norm_g"], (1, 8))
    slopes = np.exp2(-np.arange(1, 9, dtype=np.float32))
    slopes = jnp.asarray(np.broadcast_to(slopes.reshape(4, 2, 1), (4, 2, QK_BLOCK)))

    zconv, zqk, yc, q, k, v = _inproj_fwd(x, w["g_mix"], w["w_in"], w["conv_w"], w["conv_b"], qg, kg, bd, tms[0])
    ya, lse, e_all, m_all = _attn_fwd(q, k, v, slopes)
    if "late_weights" in hooks:
        w = {**w, **hooks["late_weights"](lse)}
    x1 = _outproj_fwd(ya, yc, x, w["g_out_conv"], w["g_out_attn"], w["w_out"], tms[0])
    gp, up, h2, x2 = _ffn_fwd(x1, w["g_ffn"], w["w_gate"], w["w_up"], w["w_down"], w["ffn_conv_w"], w["ffn_conv_b"],
                              tms[0])
    dx2, dx2b, loss, dw_pg, dw_pp, dg_ple = _ple_fwd_bwd(x2, p, target, w["g_ple"], w["w_ple_gate"], w["w_ple_proj"], tms[0])
    dh2, dw_down, dw_up, dw_gate, dfcw, dfcb = _ffn_bwd(dx2b, h2, gp, up, w["w_gate"], w["w_up"], w["w_down"],
                                                        w["ffn_conv_w"], w["ffn_conv_b"], tms[0])
    token = None
    if "ffn_grads" in hooks:
        token = hooks["ffn_grads"]({"w_ple_gate": dw_pg, "w_ple_proj": dw_pp, "w_down": dw_down, "w_up": dw_up,
                                    "w_gate": dw_gate, "ffn_conv_b": dfcb})
    dx1, dya, dd, dzconv, dw_out, dg_ffn, dgoc, dgoa, dcw, dcb = _outproj_bwd(
        dh2, dx2, x1, _ordered_after(w["g_ffn"], token), w["w_out"], yc, ya, w["g_out_conv"], w["g_out_attn"], zconv,
        w["conv_w"], w["conv_b"], bd, tms[1])
    token = hooks["outproj_done"](dx1) if "outproj_done" in hooks else None
    dq, dk, dv = _attn_bwd(q, k, v, dya, lse, dd, e_all, m_all, slopes if token is None else token)
    dx, dw_in, dg_mix, dqg, dkg = _inproj_bwd(dq, dk, dv, dzconv, zqk, x, dx1, w["g_mix"], w["w_in"], qg, kg, bd,
                                              tms[0])
    grads = {
        "g_mix": dg_mix, "w_in": dw_in, "conv_w": dcw, "conv_b": dcb,
        "q_norm_g": dqg, "k_norm_g": dkg,
        "g_out_conv": dgoc, "g_out_attn": dgoa, "w_out": dw_out, "g_ffn": dg_ffn, "w_gate": dw_gate, "w_up": dw_up,
        "ffn_conv_w": dfcw, "ffn_conv_b": dfcb, "w_down": dw_down, "g_ple": dg_ple, "w_ple_gate": dw_pg,
        "w_ple_proj": dw_pp,
    }
    return loss, dx, grads


ANY = pl.BlockSpec(memory_space=pl.ANY)
MESH = pl.DeviceIdType.MESH


def _all_gather(shards, name):
    n = len(shards)

    def body(*refs):
        ins, outs = refs[:n], refs[n:2 * n]
        send_sems, recv_sems, local_sems = refs[2 * n:]
        x, y, c = lax.axis_index("x"), lax.axis_index("y"), lax.axis_index("c")
        me, sibling = (x, y, c), (x, y, 1 - c)
        chips = [(1 - x, y), (x, 1 - y), (1 - x, 1 - y)]

        def slot(dev):
            return 4 * dev[0] + 2 * dev[1] + dev[2]

        def copy(b, k, block, to, src=None):
            dst = outs[b].at[slot(block)]
            return pltpu.make_async_remote_copy(
                src_ref=dst if src is None else src, dst_ref=dst, send_sem=send_sems.at[b, k],
                recv_sem=recv_sems.at[b, k], device_id=to, device_id_type=MESH)

        mine = [pltpu.make_async_copy(ins[b], outs[b].at[slot(me)], local_sems.at[b]) for b in range(n)]
        first, passed = [], []
        for b in range(n):
            mine[b].start()
            first.append(copy(b, 0, me, sibling, src=ins[b]))
            first += [copy(b, 1 + j, me, (*chip, c), src=ins[b]) for j, chip in enumerate(chips)]
        for cp in first:
            cp.start()
        for j, chip in enumerate(chips):
            for b in range(n):
                copy(b, 1 + j, (*chip, c), me).wait_recv()
                fwd = copy(b, 4 + j, (*chip, c), sibling)
                fwd.start()
                passed.append(fwd)
        for b in range(n):
            copy(b, 0, sibling, me).wait_recv()
            for j, chip in enumerate(chips):
                copy(b, 4 + j, (*chip, 1 - c), me).wait_recv()
        for cp in first + passed:
            cp.wait_send()
        for cp in mine:
            cp.wait()

    return pl.pallas_call(
        body, name=name,
        in_specs=[ANY] * n, out_specs=[ANY] * n,
        out_shape=[jax.ShapeDtypeStruct((N_DEV,) + s.shape, s.dtype) for s in shards],
        scratch_shapes=[pltpu.SemaphoreType.DMA((n, 7)), pltpu.SemaphoreType.DMA((n, 7)),
                        pltpu.SemaphoreType.DMA((n,))],
    )(*shards)


HBM = pl.BlockSpec(memory_space=pltpu.HBM)
SEM = pl.BlockSpec(memory_space=pltpu.SEMAPHORE)
EFFECT = pltpu.SideEffectType.DATAFLOW_SIDE_EFFECTING
FLIPS = ((0, 0, 1), (0, 1, 0), (0, 1, 1), (1, 0, 0), (1, 0, 1), (1, 1, 0), (1, 1, 1))


def _flip_peers():
    pos = (lax.axis_index("x"), lax.axis_index("y"), lax.axis_index("c"))
    return [tuple(1 - a if f else a for a, f in zip(pos, flip)) for flip in FLIPS]


def _hbm(a):
    return pltpu.with_memory_space_constraint(a, pltpu.HBM)


def _own_copies(own, src_refs, land_refs, send_sems, n_remote):
    return [pltpu.make_async_copy(src, dst, send_sems.at[n_remote + i])
            for i, (src, dst) in enumerate(own(src_refs, land_refs) if own else [])]


def _split_start(name, srcs, lands, plan, n_copies, after, own=None):
    n, m = len(srcs), len(lands)

    def body(*refs):
        send_sems, recv_sems, token = refs[n + m + 1], refs[n + m + 2], refs[-1]
        for i, (src, dst, peer) in enumerate(plan(refs[:n], refs[n:n + m])):
            pltpu.make_async_remote_copy(src_ref=src, dst_ref=dst, send_sem=send_sems.at[i], recv_sem=recv_sems.at[i],
                                         device_id=peer, device_id_type=MESH).start()
        for copy in _own_copies(own, refs[:n], refs[n:n + m], send_sems, n_copies):
            copy.start()
        token[...] = jnp.zeros_like(token)

    outs = pl.pallas_call(
        body, name=name + "_start",
        in_specs=[HBM] * (n + m) + [ANY],
        out_specs=[SEM, SEM] + [HBM] * (n + m) + [pl.BlockSpec(memory_space=pltpu.VMEM)],
        out_shape=[pltpu.SemaphoreType.DMA((n_copies + (n if own else 0),)), pltpu.SemaphoreType.DMA((n_copies,))]
        + [pltpu.HBM(a.shape, a.dtype) for a in list(srcs) + list(lands)] + [jax.ShapeDtypeStruct((1, D_MODEL), F32)],
        input_output_aliases={i: 2 + i for i in range(n + m)},
        compiler_params=pltpu.CompilerParams(has_side_effects=EFFECT),
    )(*[_hbm(a) for a in list(srcs) + list(lands)], after)
    return (outs[0], outs[1], outs[2:2 + n], outs[2 + n:2 + n + m]), outs[-1]


def _split_wait(name, started, plan, after, own=None):
    send_sems, recv_sems, srcs, lands = started
    n, m = len(srcs), len(lands)

    def body(*refs):
        send_ref, recv_ref = refs[n + m], refs[n + m + 1]
        copies = plan(refs[:n], refs[n:n + m])
        for i, (src, dst, peer) in enumerate(copies):
            copy = pltpu.make_async_remote_copy(src_ref=src, dst_ref=dst, send_sem=send_ref.at[i],
                                                recv_sem=recv_ref.at[i], device_id=peer, device_id_type=MESH)
            copy.wait_send()
            copy.wait_recv()
        for copy in _own_copies(own, refs[:n], refs[n:n + m], send_ref, len(copies)):
            copy.wait()

    outs = pl.pallas_call(
        body, name=name + "_wait",
        in_specs=[HBM] * (n + m) + [SEM, SEM, ANY],
        out_specs=[HBM] * (n + m),
        out_shape=[pltpu.HBM(a.shape, a.dtype) for a in list(srcs) + list(lands)],
        input_output_aliases={i: i for i in range(n + m)},
        compiler_params=pltpu.CompilerParams(has_side_effects=EFFECT),
    )(*srcs, *lands, send_sems, recv_sems, after)
    return outs[:n], outs[n:]


def _gather_plan(srcs, lands):
    slot = 4 * lax.axis_index("x") + 2 * lax.axis_index("y") + lax.axis_index("c")
    return [(src, land.at[slot], peer) for src, land in zip(srcs, lands) for peer in _flip_peers()]


def _own_slot(srcs, lands):
    slot = 4 * lax.axis_index("x") + 2 * lax.axis_index("y") + lax.axis_index("c")
    return [(src, land.at[slot]) for src, land in zip(srcs, lands)]


def _sibling_plan(srcs, lands):
    x, y, c = lax.axis_index("x"), lax.axis_index("y"), lax.axis_index("c")
    return [(src.at[k, 1 - c], land.at[k], (x, y, 1 - c)) for src, land in zip(srcs, lands) for k in range(N_CHIP)]


def _chip_plan(srcs, lands):
    x, y, c = lax.axis_index("x"), lax.axis_index("y"), lax.axis_index("c")
    return [(src.at[2 * cx + cy], land.at[2 * x + y], (cx, cy, c))
            for src, land in zip(srcs, lands) for cx, cy in ((1 - x, y), (x, 1 - y), (1 - x, 1 - y))]


def _row_tile(rows):
    for tr in range(min(rows, 512), 15, -16):
        if rows % tr == 0:
            return tr
    return rows


def _pair_sums(gs, lands, core, name):
    n = len(gs)

    def body(c_ref, *refs):
        for b in range(n):
            out = refs[2 * n + b]
            out[...] = (refs[b][...].astype(F32) + refs[n + b][...].astype(F32)).astype(out.dtype)

    def slab(a):
        return pl.BlockSpec((None,) + a.shape[1:], lambda k, c_ref: (k, 0, 0))

    return pl.pallas_call(
        body, name=name,
        grid_spec=pltpu.PrefetchScalarGridSpec(
            num_scalar_prefetch=1, grid=(N_CHIP,),
            in_specs=[pl.BlockSpec((None, None) + g.shape[2:], lambda k, c_ref: (k, c_ref[0], 0, 0)) for g in gs]
            + [slab(a) for a in lands],
            out_specs=[slab(a) for a in lands]),
        out_shape=[jax.ShapeDtypeStruct(a.shape, a.dtype) for a in lands],
        compiler_params=_cparams("parallel"),
    )(core, *gs, *lands)


def _adamw(own, arrived, chip, w, m, v, name):
    k, rows, cols = arrived.shape
    tr = _row_tile(rows)
    c1 = 1.0 / (1.0 - ADAM_B1 ** ADAM_STEP)
    c2 = 1.0 / (1.0 - ADAM_B2 ** ADAM_STEP)

    def body(chip_ref, o_ref, p_ref, w_ref, m_ref, v_ref, g_ref, d_ref, nm_ref, nv_ref):
        def slab(j):
            return jnp.where(chip_ref[0] == j, o_ref[j], p_ref[j]).astype(F32)

        g = slab(0)
        for j in range(1, k):
            g = g + slab(j)
        g_ref[...] = g
        nm = ADAM_B1 * m_ref[...] + (1.0 - ADAM_B1) * g
        nv = ADAM_B2 * v_ref[...] + (1.0 - ADAM_B2) * (g * g)
        nm_ref[...] = nm
        nv_ref[...] = nv
        d_ref[...] = -ADAM_LR * ((nm * c1) / (jnp.sqrt(nv * c2) + ADAM_EPS) + ADAM_WD * w_ref[...])

    blk = pl.BlockSpec((tr, cols), lambda i, c: (i, 0))
    stack = pl.BlockSpec((k, tr, cols), lambda i, c: (0, i, 0))
    return pl.pallas_call(
        body, name=name,
        grid_spec=pltpu.PrefetchScalarGridSpec(num_scalar_prefetch=1, grid=(rows // tr,),
                                               in_specs=[stack, stack, blk, blk, blk], out_specs=[blk] * 4),
        out_shape=[jax.ShapeDtypeStruct((rows, cols), F32)] * 4,
        compiler_params=_cparams("parallel"),
    )(chip, own, arrived, w, m, v)


SMALL_LAYOUT = (("g_mix", 0, 1024), ("conv_b", 1, 512), ("q_norm_g", 2, 64), ("k_norm_g", 3, 64),
                ("g_out_conv", 4, 512), ("g_out_attn", 5, 512), ("g_ffn", 6, 1024), ("ffn_conv_b", 7, 2816),
                ("g_ple", 10, 1024))
CONV_W_ROW = 11
FFN_CONV_W_ROW = 14
LOSS_ROW = 23


def _row_pieces(cols):
    return [(c, min(1024, cols - c)) for c in range(0, cols, 1024)]


def _pack_small(grads, loss_tile):
    names = [n for n, _, _ in SMALL_LAYOUT]

    def body(*refs):
        ins, cw_ref, fcw_ref, loss_ref, out_ref = refs[:len(names)], refs[-4], refs[-3], refs[-2], refs[-1]
        out_ref[...] = jnp.zeros_like(out_ref)
        for ref, (_, row, cols) in zip(ins, SMALL_LAYOUT):
            if ref.shape[1] == ATTN_W and cols == HEAD_DIM:
                out_ref[row:row + 1, 0:cols] = sum(ref[:, h:h + cols] for h in range(0, ATTN_W, cols))
                continue
            for j, (c, width) in enumerate(_row_pieces(cols)):
                out_ref[row + j:row + j + 1, 0:width] = ref[:, c:c + width]
        for k in range(3):
            out_ref[CONV_W_ROW + k:CONV_W_ROW + k + 1, 0:CONV_W] = cw_ref[k:k + 1, :]
            for j, (c, width) in enumerate(_row_pieces(D_FF)):
                row = FFN_CONV_W_ROW + 3 * k + j
                out_ref[row:row + 1, 0:width] = fcw_ref[k:k + 1, c:c + width]
        out_ref[LOSS_ROW:LOSS_ROW + 1, 0:128] = loss_ref[0:1, :]

    return pl.pallas_call(
        body, name="pack_small_grads", out_shape=jax.ShapeDtypeStruct((SMALL_ROWS, 1024), F32),
    )(*[grads[n] for n in names], grads["conv_w"], grads["ffn_conv_w"], loss_tile)


def _adamw_small(arrived, conv_parts, fconv_parts, wts, mom, var):
    names = [n for n, _, _ in SMALL_LAYOUT] + ["conv_w", "ffn_conv_w"]
    c1 = 1.0 / (1.0 - ADAM_B1 ** ADAM_STEP)
    c2 = 1.0 / (1.0 - ADAM_B2 ** ADAM_STEP)
    n = len(names)

    def body(*refs):
        land, cw_ref, fcw_ref = refs[0], refs[1], refs[2]
        state = refs[3:3 + 3 * n]
        outs = refs[3 + 3 * n:]

        def total(piece):
            acc = piece(0)
            for d in range(1, N_DEV):
                acc = acc + piece(d)
            return acc

        for i, name in enumerate(names):
            if name == "conv_w":
                g = total(lambda d: cw_ref[d])
            elif name == "ffn_conv_w":
                g = total(lambda d: fcw_ref[d])
            else:
                _, row, cols = SMALL_LAYOUT[i]
                pieces = [total(lambda d, j=j, width=width: land[d, row + j:row + j + 1, 0:width])
                          for j, (_, width) in enumerate(_row_pieces(cols))]
                g = pieces[0] if len(pieces) == 1 else jnp.concatenate(pieces, axis=1)
            w_ref, m_ref, v_ref = state[3 * i:3 * i + 3]
            nm = ADAM_B1 * m_ref[...] + (1.0 - ADAM_B1) * g
            nv = ADAM_B2 * v_ref[...] + (1.0 - ADAM_B2) * (g * g)
            outs[4 * i][...] = g
            outs[4 * i + 1][...] = -ADAM_LR * ((nm * c1) / (jnp.sqrt(nv * c2) + ADAM_EPS) + ADAM_WD * w_ref[...])
            outs[4 * i + 2][...] = nm
            outs[4 * i + 3][...] = nv
        outs[-1][...] = total(lambda d: land[d, LOSS_ROW:LOSS_ROW + 1, 0:128])

    state = [a[nm_] for nm_ in names for a in (wts, mom, var)]
    shapes = [jax.ShapeDtypeStruct(wts[nm_].shape, F32) for nm_ in names for _ in range(4)]
    outs = pl.pallas_call(
        body, name="adamw_small", out_shape=shapes + [jax.ShapeDtypeStruct((1, 128), F32)],
    )(arrived, conv_parts, fconv_parts, *state)
    return {nm_: tuple(outs[4 * i:4 * i + 4]) for i, nm_ in enumerate(names)}, outs[-1][0, 0]


COL_SHARDED = ("w_in", "w_ple_proj")
TRANSPOSED = ("w_gate", "w_up")
CONV_SHARDED = (("conv_w", CONV_W), ("ffn_conv_w", D_FF))


def _gathered_to_full(name, gathered):
    if name in COL_SHARDED:
        return gathered.transpose(1, 0, 2).reshape(gathered.shape[1], -1)
    return gathered.reshape(-1, gathered.shape[2])


def _full_to_stacked(name, grad, shard_shape):
    sr, sc = shard_shape
    if grad.ndim == 3:
        a = grad
    elif name in COL_SHARDED:
        a = grad.reshape(sr, N_DEV, sc).transpose(1, 0, 2)
    else:
        a = grad.reshape(N_DEV, sr, sc)
    return a.astype(BF16).reshape(N_CHIP, 2, sr, sc)


def _pad_rows(vec, rows):
    return jnp.pad(vec, (0, rows * 1024 - vec.shape[0])).reshape(rows, 1024)


def kernel(x, p, g_mix, w_in, conv_w, conv_b, q_norm_g, k_norm_g, g_out_conv, g_out_attn, w_out, g_ffn, w_gate, w_up, ffn_conv_w, ffn_conv_b, w_down, g_ple, w_ple_gate, w_ple_proj, loss_target, m_g_mix, m_w_in, m_conv_w, m_conv_b, m_q_norm_g, m_k_norm_g, m_g_out_conv, m_g_out_attn, m_w_out, m_g_ffn, m_w_gate, m_w_up, m_ffn_conv_w, m_ffn_conv_b, m_w_down, m_g_ple, m_w_ple_gate, m_w_ple_proj, v_g_mix, v_w_in, v_conv_w, v_conv_b, v_q_norm_g, v_k_norm_g, v_g_out_conv, v_g_out_attn, v_w_out, v_g_ffn, v_w_gate, v_w_up, v_ffn_conv_w, v_ffn_conv_b, v_w_down, v_g_ple, v_w_ple_gate, v_w_ple_proj):
    args = dict(locals())
    names = ["g_mix", "w_in", "conv_w", "conv_b", "q_norm_g", "k_norm_g", "g_out_conv", "g_out_attn", "w_out", "g_ffn",
             "w_gate", "w_up", "ffn_conv_w", "ffn_conv_b", "w_down", "g_ple", "w_ple_gate", "w_ple_proj"]
    big = list(BIG)
    conv = [n for n, _ in CONV_SHARDED]

    def local(prefix):
        out = {n: (args[prefix + n][0] if n in big or n in conv else args[prefix + n]) for n in names}
        out.update({n: out[n].T for n in TRANSPOSED})
        return out

    wts, mom, var = local(""), local("m_"), local("v_")
    shard_shapes = {n: wts[n].shape for n in big}
    dev = 4 * lax.axis_index("x") + 2 * lax.axis_index("y") + lax.axis_index("c")
    core = lax.axis_index("c").astype(jnp.int32).reshape(1)

    conv_local = _pad_rows(jnp.concatenate([wts[n].reshape(-1) for n in conv]), 8).reshape(8, 1024)
    late = [n for n in big if n != "w_in"]
    w_in_all, conv_all = _all_gather([wts["w_in"].astype(BF16), conv_local], "gather_weights")
    late_shards = [wts[n].astype(BF16) for n in late]
    gathering, token = _split_start("gather_late_weights", late_shards,
                                    [lax.empty((N_DEV,) + s.shape, BF16) for s in late_shards], _gather_plan,
                                    7 * len(late), w_in_all, own=_own_slot)
    full = dict(wts)
    full["w_in"] = _gathered_to_full("w_in", w_in_all)
    full["g_mix"] = _ordered_after(wts["g_mix"], token)
    flying = {}

    def late_weights(after):
        _, lands = _split_wait("gather_late_weights", gathering, _gather_plan, after, own=_own_slot)
        return {n: _gathered_to_full(n, land) for n, land in zip(late, lands)}

    early = ["w_ple_gate", "w_ple_proj", "w_down", "w_up", "w_gate"]

    def ffn_grads(g):
        stacked = [_full_to_stacked(n, g[n], shard_shapes[n]) for n in early]
        flying["sibling"], tok = _split_start("rs_sibling_early", stacked,
                                              [lax.empty((N_CHIP,) + s.shape[2:], BF16) for s in stacked],
                                              _sibling_plan, N_CHIP * len(early), g["ffn_conv_b"])
        return tok

    def outproj_done(after):
        stacked, landed = _split_wait("rs_sibling_early", flying["sibling"], _sibling_plan, after)
        parts = _pair_sums(stacked, landed, core, "rs_pair_sums_early")
        flying["chip"], tok = _split_start("rs_chip_early", parts, [lax.empty(q.shape, BF16) for q in parts],
                                           _chip_plan, 3 * len(early), landed[0])
        return tok

    off = 0
    for n, width in CONV_SHARDED:
        sc = width // N_DEV
        a = conv_all.reshape(N_DEV, -1)[:, off:off + 3 * sc].reshape(N_DEV, 3, sc)
        full[n] = a.transpose(1, 0, 2).reshape(3, width)
        off += 3 * sc

    loss, dx, grads = _local_step(x[0], p[0, 0], loss_target[0], full, (512, 256),
                                  {"late_weights": late_weights, "ffn_grads": ffn_grads, "outproj_done": outproj_done})

    chip = (2 * lax.axis_index("x") + lax.axis_index("y")).astype(jnp.int32).reshape(1)

    def adamw_of(group, parts, arrived):
        return {n: _adamw(own, got, chip, wts[n], mom[n], var[n], f"adamw_{n}")
                for n, own, got in zip(group, parts, arrived)}

    last = [n for n in big if n not in early]
    stacked = [_full_to_stacked(n, grads[n], shard_shapes[n]) for n in last]
    flying["sibling_last"], tok = _split_start("rs_sibling_last", stacked,
                                               [lax.empty((N_CHIP,) + s.shape[2:], BF16) for s in stacked],
                                               _sibling_plan, N_CHIP * len(last), dx)
    packed = _pack_small(grads, loss)
    flying["small"], tok = _split_start("gather_small_grads", [packed], [lax.empty((N_DEV,) + packed.shape, F32)],
                                        _gather_plan, N_DEV - 1, tok, own=_own_slot)
    stacked, landed = _split_wait("rs_sibling_last", flying["sibling_last"], _sibling_plan, tok)
    parts = _pair_sums(stacked, landed, core, "rs_pair_sums_last")
    flying["chip_last"], tok = _split_start("rs_chip_last", parts, [lax.empty(q.shape, BF16) for q in parts],
                                            _chip_plan, 3 * len(last), landed[0])

    parts, arrived = _split_wait("rs_chip_early", flying["chip"], _chip_plan, tok)
    out = adamw_of(early, parts, arrived)
    _, (small_all,) = _split_wait("gather_small_grads", flying["small"], _gather_plan, out[early[-1]][0],
                                  own=_own_slot)
    taps = small_all[:, CONV_W_ROW:CONV_W_ROW + 3, 0:CONV_W]
    ftaps = small_all[:, FFN_CONV_W_ROW:FFN_CONV_W_ROW + 9, :].reshape(N_DEV, 3, 3 * 1024)
    small_out, loss_total = _adamw_small(
        small_all, lax.dynamic_slice(taps, (0, 0, dev * (CONV_W // N_DEV)), (N_DEV, 3, CONV_W // N_DEV)),
        lax.dynamic_slice(ftaps, (0, 0, dev * (D_FF // N_DEV)), (N_DEV, 3, D_FF // N_DEV)), wts, mom, var)
    out.update(small_out)
    parts, arrived = _split_wait("rs_chip_last", flying["chip_last"], _chip_plan, small_out["g_mix"][0])
    out.update(adamw_of(last, parts, arrived))
    def result(n, which):
        a = out[n][which]
        return (a.T if n in TRANSPOSED else a).reshape(args[n].shape)

    return (loss_total, dx[None], *[result(n, which) for which in range(4) for n in names])
```

```python
import jax
import jax.numpy as jnp
import numpy as np
from jax import lax
from jax.experimental import pallas as pl
from jax.experimental.pallas import tpu as pltpu

F32 = jnp.float32
BF16 = jnp.bfloat16

D_MODEL = 1024
CONV_W = 512
ATTN_W = 512
HEAD_DIM = 64
D_FF = 2816
PLE_DIM = 256
IN_COLS = 3 * CONV_W + 3 * ATTN_W
EPS = 1e-6
QK_BLOCK = 128
DILATIONS = (1, 4, 16)
ATTN_SCALE = HEAD_DIM ** -0.5

ADAM_LR = 0.001
ADAM_B1 = 0.9
ADAM_B2 = 0.999
ADAM_EPS = 1e-08
ADAM_WD = 0.01
ADAM_STEP = 10

N_DEV = 8
N_CHIP = 4
V7X_VMEM_LIMIT = 56 * 1024 * 1024
V7X_VMEM_LIMIT_LARGE = 62 * 1024 * 1024
FF_CHUNKS = 2
FFN_BWD_PARTS = 1

BIG = ("w_in", "w_out", "w_gate", "w_up", "w_down", "w_ple_gate", "w_ple_proj")
SMALL_ROWS = 24


def _cparams(*sem, vmem=V7X_VMEM_LIMIT):
    return pltpu.CompilerParams(dimension_semantics=sem, vmem_limit_bytes=vmem)


def _mm(a, b):
    return jnp.dot(a, b, preferred_element_type=F32)


def _mm_nt(a, b):
    return lax.dot_general(a, b, (((1,), (1,)), ((), ())), preferred_element_type=F32)


def _mm_tn(a, b):
    return lax.dot_general(a, b, (((0,), (0,)), ((), ())), preferred_element_type=F32)


def _full(shape):
    nd = len(shape)
    return pl.BlockSpec(shape, lambda *_: (0,) * nd)


def _rms_stats(x):
    r = lax.rsqrt(jnp.mean(x * x, axis=-1, keepdims=True) + EPS)
    return r, x * r


def _rms_bwd(dy, xhat, r, g):
    gd = dy * g
    return r * (gd - xhat * jnp.mean(gd * xhat, axis=-1, keepdims=True))


def _seg_sum64(v, bd_ref):
    outs = []
    for c in range(0, v.shape[1], 256):
        vc = v[:, c:c + 256]
        hi = vc.astype(BF16)
        lo = (vc - hi.astype(F32)).astype(BF16)
        outs.append(_mm(hi, bd_ref[...]) + _mm(lo, bd_ref[...]))
    return outs[0] if len(outs) == 1 else jnp.concatenate(outs, axis=1)


def _shift_rows(u, k, edge_rows):
    out = pltpu.roll(u, k, 0)
    row = lax.broadcasted_iota(jnp.int32, (8, u.shape[1]), 0)
    head = out[0:8]
    for j in range(k):
        head = jnp.where(row == j, edge_rows[k - 1 - j], head)
    return jnp.concatenate([head, out[8:]], axis=0)


def _shift_rows_up(u, k, edge_rows):
    n = u.shape[0]
    out = pltpu.roll(u, n - k, 0)
    row = lax.broadcasted_iota(jnp.int32, (8, u.shape[1]), 0)
    tail = out[n - 8:n]
    for j in range(k):
        tail = jnp.where(row == 8 - k + j, edge_rows[j], tail)
    return jnp.concatenate([out[0:n - 8], tail], axis=0)


def _conv_fwd(u, c1, c2, w_ref, b_ref):
    u1 = _shift_rows(u, 1, (c1,))
    u2 = _shift_rows(u, 2, (c1, c2))
    y = u2 * w_ref[0:1, :] + u1 * w_ref[1:2, :] + u * w_ref[2:3, :] + b_ref[...]
    return y, u1, u2


def _conv_bwd_input(dy, n1row, n2row, w_ref):
    d1 = _shift_rows_up(dy, 1, (n1row,))
    d2 = _shift_rows_up(dy, 2, (n1row, n2row))
    return dy * w_ref[2:3, :] + d1 * w_ref[1:2, :] + d2 * w_ref[0:1, :]


def _sigmoid(x):
    return 1.0 / (1.0 + jnp.exp(-x))


def _inproj_fwd(x, g_mix, w_in, conv_w, conv_b, qg, kg, bd, tm):
    t = x.shape[0]

    def body(x_ref, g_ref, w_ref, cw_ref, cb_ref, qg_ref, kg_ref, bd_ref,
             zc_ref, zqk_ref, yc_ref, q_ref, k_ref, v_ref, carry_ref):
        @pl.when(pl.program_id(0) == 0)
        def _():
            carry_ref[...] = jnp.zeros_like(carry_ref)

        _, xhat = _rms_stats(x_ref[...])
        h = (xhat * g_ref[...]).astype(BF16)
        zconv = _mm(h, w_ref[:, 0:3 * CONV_W])
        zc_ref[...] = zconv.astype(BF16)
        u = zconv[:, CONV_W:2 * CONV_W] * zconv[:, 2 * CONV_W:3 * CONV_W]
        cv, _, _ = _conv_fwd(u, carry_ref[7:8, :], carry_ref[6:7, :], cw_ref, cb_ref)
        yc_ref[...] = (zconv[:, 0:CONV_W] * cv).astype(BF16)
        carry_ref[...] = u[tm - 8:tm, :]

        zqk = _mm(h, w_ref[:, 3 * CONV_W:3 * CONV_W + 2 * ATTN_W])
        zqk_ref[...] = zqk.astype(BF16)
        for j, (gain_ref, out_ref, scale) in enumerate(((qg_ref, q_ref, ATTN_SCALE), (kg_ref, k_ref, 1.0))):
            z = zqk[:, j * ATTN_W:(j + 1) * ATTN_W]
            r = lax.rsqrt(_seg_sum64(z * z, bd_ref) * (1.0 / HEAD_DIM) + EPS)
            out_ref[...] = z * r * gain_ref[...] * scale
        v_ref[...] = _mm(h, w_ref[:, 3 * CONV_W + 2 * ATTN_W:IN_COLS])

    def blk(c):
        return pl.BlockSpec((tm, c), lambda i: (i, 0))

    return pl.pallas_call(
        body, name="inproj_fwd", grid=(t // tm,),
        in_specs=[blk(D_MODEL), _full((1, D_MODEL)), _full((D_MODEL, IN_COLS)), _full((3, CONV_W)),
                  _full((1, CONV_W)), _full((1, ATTN_W)), _full((1, ATTN_W)), _full((256, 256))],
        out_specs=[blk(3 * CONV_W), blk(2 * ATTN_W), blk(CONV_W), blk(ATTN_W), blk(ATTN_W), blk(ATTN_W)],
        out_shape=[jax.ShapeDtypeStruct((t, 3 * CONV_W), BF16), jax.ShapeDtypeStruct((t, 2 * ATTN_W), BF16),
                   jax.ShapeDtypeStruct((t, CONV_W), BF16), jax.ShapeDtypeStruct((t, ATTN_W), F32),
                   jax.ShapeDtypeStruct((t, ATTN_W), F32), jax.ShapeDtypeStruct((t, ATTN_W), F32)],
        scratch_shapes=[pltpu.VMEM((8, CONV_W), F32)],
        compiler_params=_cparams("arbitrary"),
    )(x, g_mix, w_in, conv_w, conv_b, qg, kg, bd)


SUPER = 16 * QK_BLOCK
KEYS = 2 * QK_BLOCK
UNITS = SUPER // QK_BLOCK


def _rows(start, size, dil):
    return pl.ds(start, size) if dil == 1 else pl.ds(start, size, stride=dil)


def _attn_bias(sl_ref, dil):
    qi = lax.broadcasted_iota(jnp.int32, (KEYS, KEYS), 0)
    kj = lax.broadcasted_iota(jnp.int32, (KEYS, KEYS), 1)
    step = jnp.bitwise_and(qi, QK_BLOCK - 1) + QK_BLOCK - kj
    slope = jnp.where(qi < QK_BLOCK, sl_ref[0, 0:1, 0:1], sl_ref[0, 1:2, 0:1])
    bias = jnp.where(jnp.logical_and(step >= 0, step <= QK_BLOCK), -slope * (step * dil).astype(F32), -jnp.inf)
    return bias, kj >= QK_BLOCK


def _unit_start(u, dil):
    if dil == 1:
        return pl.multiple_of(u * QK_BLOCK, QK_BLOCK)
    if dil == 4:
        return jnp.bitwise_and(u, 3) + (u // 4) * (4 * QK_BLOCK)
    return u


def _stack_heads(a, head0):
    zero = jnp.zeros_like(a)
    return jnp.concatenate([jnp.where(head0, a, zero), jnp.where(head0, zero, a)], axis=0)


def _attn_fwd(q, k, v, slopes):
    t = q.shape[0]
    nsb = t // SUPER

    def body(q_ref, kc_ref, kp_ref, vc_ref, vp_ref, sl_ref, o_ref, l_ref, e_ref, m_ref, kk, vv, ob, lb):
        s = pl.program_id(1)
        kk[0:SUPER, :] = kp_ref[...]
        kk[SUPER:, :] = kc_ref[...]
        vv[0:SUPER, :] = vp_ref[...]
        vv[SUPER:, :] = vc_ref[...]
        head0 = lax.broadcasted_iota(jnp.int32, (QK_BLOCK, QK_BLOCK), 1) < HEAD_DIM

        for b, dil in enumerate(DILATIONS):
            bias, own_half = _attn_bias(sl_ref, dil)

            def unit(u, carry, b=b, dil=dil, bias=bias, own_half=own_half):
                start = _unit_start(u, dil)
                first_key = SUPER + start - QK_BLOCK * dil
                q2 = _stack_heads(q_ref[_rows(start, QK_BLOCK, dil), :].astype(BF16), head0)
                k2 = kk[_rows(first_key, KEYS, dil), :].astype(BF16)
                v2 = vv[_rows(first_key, KEYS, dil), :].astype(BF16)
                has_prev = jnp.logical_or(s > 0, start >= QK_BLOCK * dil)
                sc = jnp.where(jnp.logical_or(own_half, has_prev), _mm_nt(q2, k2) + bias, -jnp.inf)
                m = jnp.max(sc, axis=-1, keepdims=True)
                e = jnp.exp(sc - m)
                den = jnp.sum(e, axis=-1, keepdims=True)
                eb = e.astype(BF16)
                e_ref[b * UNITS + u] = eb
                o2 = _mm(eb, v2) / den
                l2 = m + jnp.log(den)
                ob[b, _rows(start, QK_BLOCK, dil), :] = jnp.where(head0, o2[0:QK_BLOCK], o2[QK_BLOCK:])
                lb[b, _rows(start, QK_BLOCK, dil), :] = jnp.where(head0, l2[0:QK_BLOCK], l2[QK_BLOCK:])
                m_ref[b, _rows(start, QK_BLOCK, dil), :] = jnp.where(head0, m[0:QK_BLOCK], m[QK_BLOCK:])
                return carry

            lax.fori_loop(0, UNITS, unit, 0, unroll=16)

        def merge(i, carry):
            rows = pl.ds(pl.multiple_of(i * 256, 256), 256)
            la, lb_, lc = lb[0, rows, :], lb[1, rows, :], lb[2, rows, :]
            mx = jnp.maximum(jnp.maximum(la, lb_), lc)
            wa, wb, wc = jnp.exp(la - mx), jnp.exp(lb_ - mx), jnp.exp(lc - mx)
            sw = wa + wb + wc
            o_ref[rows, :] = ((wa * ob[0, rows, :] + wb * ob[1, rows, :] + wc * ob[2, rows, :]) / sw).astype(BF16)
            l_ref[rows, :] = mx + jnp.log(sw)
            return carry

        lax.fori_loop(0, SUPER // 256, merge, 0)

    cur = pl.BlockSpec((SUPER, QK_BLOCK), lambda p, s: (s, p))
    prev = pl.BlockSpec((SUPER, QK_BLOCK), lambda p, s: (jnp.maximum(s - 1, 0), p))
    return pl.pallas_call(
        body, name="attn_fwd", grid=(4, nsb),
        in_specs=[cur, cur, prev, cur, prev, pl.BlockSpec((1, 2, QK_BLOCK), lambda p, s: (p, 0, 0))],
        out_specs=[cur, cur, pl.BlockSpec((None, None, 3 * UNITS, KEYS, KEYS), lambda p, s: (p, s, 0, 0, 0)),
                   pl.BlockSpec((3, SUPER, QK_BLOCK), lambda p, s: (0, s, p))],
        out_shape=[jax.ShapeDtypeStruct((t, ATTN_W), BF16), jax.ShapeDtypeStruct((t, ATTN_W), F32),
                   jax.ShapeDtypeStruct((4, nsb, 3 * UNITS, KEYS, KEYS), BF16),
                   jax.ShapeDtypeStruct((3, t, ATTN_W), F32)],
        scratch_shapes=[pltpu.VMEM((2 * SUPER, QK_BLOCK), F32), pltpu.VMEM((2 * SUPER, QK_BLOCK), F32),
                        pltpu.VMEM((3, SUPER, QK_BLOCK), F32), pltpu.VMEM((3, SUPER, QK_BLOCK), F32)],
        compiler_params=_cparams("parallel", "arbitrary"),
    )(q, k, k, v, v, slopes)


def _outproj_fwd(ya, yc, x, goc, goa, w_out, tm):
    t = x.shape[0]

    def body(ya_ref, yc_ref, x_ref, goc_ref, goa_ref, w_ref, x1_ref):
        _, ychat = _rms_stats(yc_ref[...].astype(F32))
        _, yahat = _rms_stats(ya_ref[...].astype(F32))
        acc = _mm((ychat * goc_ref[...]).astype(BF16), w_ref[0:CONV_W, :])
        acc += _mm((yahat * goa_ref[...]).astype(BF16), w_ref[CONV_W:, :])
        x1_ref[...] = x_ref[...] + acc

    def blk(c):
        return pl.BlockSpec((tm, c), lambda i: (i, 0))

    return pl.pallas_call(
        body, name="outproj_fwd", grid=(t // tm,),
        in_specs=[blk(ATTN_W), blk(CONV_W), blk(D_MODEL), _full((1, CONV_W)), _full((1, ATTN_W)),
                  _full((D_MODEL, D_MODEL))],
        out_specs=blk(D_MODEL),
        out_shape=jax.ShapeDtypeStruct((t, D_MODEL), F32),
        compiler_params=_cparams("parallel"),
    )(ya, yc, x, goc, goa, w_out)


def _ffn_fwd(x1, g_ffn, w_gate_t, w_up_t, w_down, fcw, fcb, tm):
    t = x1.shape[0]

    def body(x_ref, g_ref, wg_ref, wu_ref, wd_ref, cw_ref, cb_ref, gp_ref, up_ref, h_ref, x2_ref, carry_ref):
        @pl.when(pl.program_id(0) == 0)
        def _():
            carry_ref[...] = jnp.zeros_like(carry_ref)

        xv = x_ref[...]
        _, xhat = _rms_stats(xv)
        h = (xhat * g_ref[...]).astype(BF16)
        h_ref[...] = h
        gp = _mm_nt(h, wg_ref[...])
        gp_ref[...] = gp.astype(BF16)
        gate, _, _ = _conv_fwd(gp, carry_ref[7:8, :], carry_ref[6:7, :], cw_ref, cb_ref)
        carry_ref[...] = gp[tm - 8:tm, :]
        up = _mm_nt(h, wu_ref[...])
        up_ref[...] = up.astype(BF16)
        a = (gate * _sigmoid(gate) * up).astype(BF16)
        x2_ref[...] = xv + _mm(a, wd_ref[...])

    def blk(c):
        return pl.BlockSpec((tm, c), lambda i: (i, 0))

    def weight():
        return pl.BlockSpec((D_FF, D_MODEL), lambda i: (0, 0), pipeline_mode=pl.Buffered(1))

    return pl.pallas_call(
        body, name="ffn_fwd", grid=(t // tm,),
        in_specs=[blk(D_MODEL), _full((1, D_MODEL)), weight(), weight(), weight(), _full((3, D_FF)), _full((1, D_FF))],
        out_specs=[blk(D_FF), blk(D_FF), blk(D_MODEL), blk(D_MODEL)],
        out_shape=[jax.ShapeDtypeStruct((t, D_FF), BF16), jax.ShapeDtypeStruct((t, D_FF), BF16),
                   jax.ShapeDtypeStruct((t, D_MODEL), BF16), jax.ShapeDtypeStruct((t, D_MODEL), F32)],
        scratch_shapes=[pltpu.VMEM((8, D_FF), F32)],
        compiler_params=_cparams("arbitrary", vmem=V7X_VMEM_LIMIT_LARGE),
    )(x1, g_ffn, w_gate_t, w_up_t, w_down, fcw, fcb)


def _ple_fwd_bwd(x2, p, target, g_ple, w_pg, w_pp, tm):
    t = x2.shape[0]

    def body(x_ref, p_ref, t_ref, g_ref, wg_ref, wp_ref, dx_ref, dxb_ref, loss_ref, dwgb_ref, dwp_ref, dg_ref,
             dwg_ref):
        @pl.when(pl.program_id(0) == 0)
        def _():
            loss_ref[...] = jnp.zeros_like(loss_ref)
            dwg_ref[...] = jnp.zeros_like(dwg_ref)
            dwp_ref[...] = jnp.zeros_like(dwp_ref)
            dg_ref[...] = jnp.zeros_like(dg_ref)

        xv = x_ref[...]
        r, xhat = _rms_stats(xv)
        g = g_ref[...]
        h = (xhat * g).astype(BF16)
        pg = _sigmoid(_mm(h, wg_ref[...]))
        pb = p_ref[...].astype(BF16)
        pp = _mm(pb, wp_ref[...])
        err = xv + pg * pp - t_ref[...]
        loss_ref[...] += 0.5 * jnp.sum(jnp.mean(err * err, axis=-1, keepdims=True))
        dx3 = err * (1.0 / D_MODEL)
        d_pp = (dx3 * pg).astype(BF16)
        d_pre = (dx3 * pp * pg * (1.0 - pg)).astype(BF16)
        dwp_ref[...] += _mm_tn(pb, d_pp)
        dwg_ref[...] += _mm_tn(h, d_pre)
        dh = _mm_nt(d_pre, wg_ref[...])
        dg_ref[...] += jnp.sum(dh * xhat, axis=0, keepdims=True)
        dx2 = dx3 + _rms_bwd(dh, xhat, r, g)
        dx_ref[...] = dx2
        dxb_ref[...] = dx2.astype(BF16)

        @pl.when(pl.program_id(0) == t // tm - 1)
        def _():
            dwgb_ref[...] = dwg_ref[...].astype(BF16)

    def blk(c):
        return pl.BlockSpec((tm, c), lambda i: (i, 0))

    return pl.pallas_call(
        body, name="ple_fwd_bwd", grid=(t // tm,),
        in_specs=[blk(D_MODEL), blk(PLE_DIM), blk(D_MODEL), _full((1, D_MODEL)), _full((D_MODEL, D_MODEL)),
                  _full((PLE_DIM, D_MODEL))],
        out_specs=[blk(D_MODEL), blk(D_MODEL), _full((8, 128)), _full((D_MODEL, D_MODEL)),
                   _full((PLE_DIM, D_MODEL)), _full((1, D_MODEL))],
        out_shape=[jax.ShapeDtypeStruct((t, D_MODEL), F32), jax.ShapeDtypeStruct((t, D_MODEL), BF16),
                   jax.ShapeDtypeStruct((8, 128), F32),
                   jax.ShapeDtypeStruct((D_MODEL, D_MODEL), BF16), jax.ShapeDtypeStruct((PLE_DIM, D_MODEL), F32),
                   jax.ShapeDtypeStruct((1, D_MODEL), F32)],
        scratch_shapes=[pltpu.VMEM((D_MODEL, D_MODEL), F32)],
        compiler_params=_cparams("arbitrary"),
    )(x2, p, target, g_ple, w_pg, w_pp)


def _ffn_bwd(dx2, h2, gp, up, w_gate, w_up, w_down, fcw, fcb, tm):
    t = dx2.shape[0]
    nblk = t // tm
    fc = D_FF // FF_CHUNKS
    half = tm // FFN_BWD_PARTS

    def body(dx_ref, h_ref, gp_ref, gph_ref, up_ref, wg_ref, wu_ref, wd_ref, cw_ref, cb_ref,
             dh_ref, dwd_hbm, dwu_hbm, dwg_hbm, dcw_ref, dcb_ref, carry_ref, a_scr, dup_scr, dgp_scr,
             dwd_acc, dwu_acc, dwg_acc, stage, stage_sem):
        i = pl.program_id(1)

        @pl.when(i == 0)
        def _():
            carry_ref[...] = jnp.zeros_like(carry_ref)
            dwd_acc[...] = jnp.zeros_like(dwd_acc)
            dwu_acc[...] = jnp.zeros_like(dwu_acc)
            dwg_acc[...] = jnp.zeros_like(dwg_acc)
            dcw_ref[...] = jnp.zeros_like(dcw_ref)
            dcb_ref[...] = jnp.zeros_like(dcb_ref)

        keep = (i < nblk - 1).astype(F32)
        later = carry_ref[...]
        for hf in reversed(range(FFN_BWD_PARTS)):
            rows = slice(hf * half, (hf + 1) * half)
            dxb = dx_ref[rows, :]
            gp_v = gp_ref[rows, :].astype(F32)
            if hf > 0:
                before = gp_ref[hf * half - 16:hf * half, :].astype(F32)
            else:
                before = gph_ref[...].astype(F32) * keep
            gate, gp1, gp2 = _conv_fwd(gp_v, before[15:16, :], before[14:15, :], cw_ref, cb_ref)
            s = _sigmoid(gate)
            silu = gate * s
            up_v = up_ref[rows, :].astype(F32)
            da = _mm_nt(dxb, wd_ref[...])
            a_scr[rows, :] = (silu * up_v).astype(BF16)
            d_up = (da * silu).astype(BF16)
            dup_scr[rows, :] = d_up
            d_gate = da * up_v * (s * (1.0 + gate * (1.0 - s)))
            d_gp = _conv_bwd_input(d_gate, later[0:1, :], later[1:2, :], cw_ref).astype(BF16)
            dgp_scr[rows, :] = d_gp
            later = d_gate[0:8, :]
            dcw_ref[0:1, :] += jnp.sum(d_gate * gp2, axis=0, keepdims=True)
            dcw_ref[1:2, :] += jnp.sum(d_gate * gp1, axis=0, keepdims=True)
            dcw_ref[2:3, :] += jnp.sum(d_gate * gp_v, axis=0, keepdims=True)
            dcb_ref[...] += jnp.sum(d_gate, axis=0, keepdims=True)
            dh_ref[rows, :] = (_mm(d_gp, wg_ref[...]) + _mm(d_up, wu_ref[...])).astype(BF16)
        carry_ref[...] = later
        dwd_acc[...] += _mm_tn(a_scr[...], dx_ref[...])
        dwu_acc[...] += _mm_tn(h_ref[...], dup_scr[...])
        dwg_acc[...] += _mm_tn(h_ref[...], dgp_scr[...])

        @pl.when(i == nblk - 1)
        def _():
            rows = pl.ds(pl.multiple_of(pl.program_id(0) * fc, 16), fc)
            for acc, out, flip in ((dwd_acc, dwd_hbm, False), (dwu_acc, dwu_hbm, True), (dwg_acc, dwg_hbm, True)):
                stage[...] = (acc[...].T if flip else acc[...]).astype(BF16)
                copy = pltpu.make_async_copy(stage, out.at[rows, :], stage_sem)
                copy.start()
                copy.wait()

    def rev(i):
        return nblk - 1 - i

    one = pl.Buffered(1)
    in_specs = [
        pl.BlockSpec((tm, D_MODEL), lambda j, i: (rev(i), 0)),
        pl.BlockSpec((tm, D_MODEL), lambda j, i: (rev(i), 0)),
        pl.BlockSpec((tm, fc), lambda j, i: (rev(i), j)),
        pl.BlockSpec((16, fc), lambda j, i: (jnp.maximum(rev(i) * (tm // 16) - 1, 0), j)),
        pl.BlockSpec((tm, fc), lambda j, i: (rev(i), j)),
        pl.BlockSpec((fc, D_MODEL), lambda j, i: (j, 0), pipeline_mode=one),
        pl.BlockSpec((fc, D_MODEL), lambda j, i: (j, 0), pipeline_mode=one),
        pl.BlockSpec((fc, D_MODEL), lambda j, i: (j, 0), pipeline_mode=one),
        pl.BlockSpec((3, fc), lambda j, i: (0, j)),
        pl.BlockSpec((1, fc), lambda j, i: (0, j)),
    ]
    out_specs = [
        pl.BlockSpec((None, tm, D_MODEL), lambda j, i: (j, rev(i), 0)),
        ANY, ANY, ANY,
        pl.BlockSpec((3, fc), lambda j, i: (0, j)),
        pl.BlockSpec((1, fc), lambda j, i: (0, j)),
    ]
    return pl.pallas_call(
        body, name="ffn_bwd", grid=(FF_CHUNKS, nblk), in_specs=in_specs, out_specs=out_specs,
        out_shape=[jax.ShapeDtypeStruct((FF_CHUNKS, t, D_MODEL), BF16), jax.ShapeDtypeStruct((D_FF, D_MODEL), BF16),
                   jax.ShapeDtypeStruct((D_FF, D_MODEL), BF16), jax.ShapeDtypeStruct((D_FF, D_MODEL), BF16),
                   jax.ShapeDtypeStruct((3, D_FF), F32), jax.ShapeDtypeStruct((1, D_FF), F32)],
        scratch_shapes=[pltpu.VMEM((8, fc), F32), pltpu.VMEM((tm, fc), BF16), pltpu.VMEM((tm, fc), BF16),
                        pltpu.VMEM((tm, fc), BF16), pltpu.VMEM((fc, D_MODEL), F32), pltpu.VMEM((D_MODEL, fc), F32),
                        pltpu.VMEM((D_MODEL, fc), F32), pltpu.VMEM((fc, D_MODEL), BF16), pltpu.SemaphoreType.DMA],
        compiler_params=_cparams("arbitrary", "arbitrary", vmem=V7X_VMEM_LIMIT_LARGE),
    )(dx2, h2, gp, gp, up, w_gate, w_up, w_down, fcw, fcb)


def _outproj_bwd(dh2, dx2, x1, g_ffn, w_out, yc, ya, goc, goa, zconv, conv_w, conv_b, bd, tm):
    t = x1.shape[0]
    nblk = t // tm

    def body(dh_ref, dx2_ref, x1_ref, g_ref, w_ref, yc_ref, ya_ref, goc_ref, goa_ref, zc_ref, zch_ref, cw_ref, cb_ref,
             bd_ref, dx1_ref, dya_ref, dd_ref, dzc_ref, dwb_ref, dg_ref, dgoc_ref, dgoa_ref, dcw_ref, dcb_ref,
             carry_ref, dw_ref):
        i = pl.program_id(0)

        @pl.when(i == 0)
        def _():
            carry_ref[...] = jnp.zeros_like(carry_ref)
            for ref in (dw_ref, dg_ref, dgoc_ref, dgoa_ref, dcw_ref, dcb_ref):
                ref[...] = jnp.zeros_like(ref)

        keep = (i < nblk - 1).astype(F32)
        dh2_v = dh_ref[0].astype(F32)
        for j in range(1, FF_CHUNKS):
            dh2_v = dh2_v + dh_ref[j].astype(F32)
        r, xhat = _rms_stats(x1_ref[...])
        dg_ref[...] += jnp.sum(dh2_v * xhat, axis=0, keepdims=True)
        dx1 = dx2_ref[...] + _rms_bwd(dh2_v, xhat, r, g_ref[...])
        dx1_ref[...] = dx1
        dx1b = dx1.astype(BF16)
        dy = _mm_nt(dx1b, w_ref[...])

        yc_v = yc_ref[...].astype(F32)
        rc, ychat = _rms_stats(yc_v)
        dw_ref[0:CONV_W, :] += _mm_tn((ychat * goc_ref[...]).astype(BF16), dx1b)
        dyc = dy[:, 0:CONV_W]
        dgoc_ref[...] += jnp.sum(dyc * ychat, axis=0, keepdims=True)
        d_yc = _rms_bwd(dyc, ychat, rc, goc_ref[...])

        ya_v = ya_ref[...].astype(F32)
        ra, yahat = _rms_stats(ya_v)
        dw_ref[CONV_W:, :] += _mm_tn((yahat * goa_ref[...]).astype(BF16), dx1b)
        dya = dy[:, CONV_W:]
        dgoa_ref[...] += jnp.sum(dya * yahat, axis=0, keepdims=True)
        d_ya = _rms_bwd(dya, yahat, ra, goa_ref[...])
        dya_ref[...] = d_ya
        dd_ref[...] = _seg_sum64(d_ya * ya_v, bd_ref)

        zb = zc_ref[:, 0:CONV_W].astype(F32)
        zc = zc_ref[:, CONV_W:2 * CONV_W].astype(F32)
        zx = zc_ref[:, 2 * CONV_W:3 * CONV_W].astype(F32)
        u = zc * zx
        uh = (zch_ref[:, CONV_W:2 * CONV_W].astype(F32) * zch_ref[:, 2 * CONV_W:3 * CONV_W].astype(F32)) * keep
        cv, u1, u2 = _conv_fwd(u, uh[15:16, :], uh[14:15, :], cw_ref, cb_ref)
        d_cv = d_yc * zb
        d_u = _conv_bwd_input(d_cv, carry_ref[0:1, :], carry_ref[1:2, :], cw_ref)
        carry_ref[...] = d_cv[0:8, :]
        dcw_ref[0:1, :] += jnp.sum(d_cv * u2, axis=0, keepdims=True)
        dcw_ref[1:2, :] += jnp.sum(d_cv * u1, axis=0, keepdims=True)
        dcw_ref[2:3, :] += jnp.sum(d_cv * u, axis=0, keepdims=True)
        dcb_ref[...] += jnp.sum(d_cv, axis=0, keepdims=True)
        dzc_ref[:, 0:CONV_W] = (d_yc * cv).astype(BF16)
        dzc_ref[:, CONV_W:2 * CONV_W] = (d_u * zx).astype(BF16)
        dzc_ref[:, 2 * CONV_W:3 * CONV_W] = (d_u * zc).astype(BF16)

        @pl.when(i == nblk - 1)
        def _():
            dwb_ref[...] = dw_ref[...].astype(BF16)

    def rev(i):
        return nblk - 1 - i

    def blk(c):
        return pl.BlockSpec((tm, c), lambda i: (rev(i), 0))

    in_specs = [
        pl.BlockSpec((FF_CHUNKS, tm, D_MODEL), lambda i: (0, rev(i), 0)),
        blk(D_MODEL), blk(D_MODEL), _full((1, D_MODEL)),
        pl.BlockSpec((D_MODEL, D_MODEL), lambda i: (0, 0), pipeline_mode=pl.Buffered(1)),
        blk(CONV_W), blk(ATTN_W), _full((1, CONV_W)), _full((1, ATTN_W)),
        blk(3 * CONV_W),
        pl.BlockSpec((16, 3 * CONV_W), lambda i: (jnp.maximum(rev(i) * (tm // 16) - 1, 0), 0)),
        _full((3, CONV_W)), _full((1, CONV_W)), _full((256, 256)),
    ]
    out_specs = [blk(D_MODEL), blk(ATTN_W), blk(ATTN_W), blk(3 * CONV_W), _full((D_MODEL, D_MODEL)),
                 _full((1, D_MODEL)), _full((1, CONV_W)), _full((1, ATTN_W)), _full((3, CONV_W)), _full((1, CONV_W))]
    return pl.pallas_call(
        body, name="outproj_bwd", grid=(nblk,), in_specs=in_specs, out_specs=out_specs,
        out_shape=[jax.ShapeDtypeStruct((t, D_MODEL), F32), jax.ShapeDtypeStruct((t, ATTN_W), F32),
                   jax.ShapeDtypeStruct((t, ATTN_W), F32), jax.ShapeDtypeStruct((t, 3 * CONV_W), BF16),
                   jax.ShapeDtypeStruct((D_MODEL, D_MODEL), BF16), jax.ShapeDtypeStruct((1, D_MODEL), F32),
                   jax.ShapeDtypeStruct((1, CONV_W), F32), jax.ShapeDtypeStruct((1, ATTN_W), F32),
                   jax.ShapeDtypeStruct((3, CONV_W), F32), jax.ShapeDtypeStruct((1, CONV_W), F32)],
        scratch_shapes=[pltpu.VMEM((8, CONV_W), F32), pltpu.VMEM((D_MODEL, D_MODEL), F32)],
        compiler_params=_cparams("arbitrary", vmem=V7X_VMEM_LIMIT_LARGE),
    )(dh2, dx2, x1, g_ffn, w_out, yc, ya, goc, goa, zconv, zconv, conv_w, conv_b, bd)


def _attn_bwd(q, k, v, dya, lse, dd, e_all, m_all, after):
    t = q.shape[0]
    nsb = t // SUPER

    def body(q_ref, kc_ref, kp_ref, vc_ref, vp_ref, dy_ref, l_ref, d_ref, e_ref, m_ref, after_ref,
             dq_ref, dk_ref, dv_ref, kk, vv, dkacc, dvacc, dwide):
        s = pl.program_id(1)

        @pl.when(s == 0)
        def _():
            dkacc[...] = jnp.zeros_like(dkacc)
            dvacc[...] = jnp.zeros_like(dvacc)

        dkacc[0:SUPER, :] = dkacc[SUPER:, :]
        dvacc[0:SUPER, :] = dvacc[SUPER:, :]
        dkacc[SUPER:, :] = jnp.zeros((SUPER, QK_BLOCK), F32)
        dvacc[SUPER:, :] = jnp.zeros((SUPER, QK_BLOCK), F32)

        @pl.when(s < nsb)
        def _():
            kk[0:SUPER, :] = kp_ref[...]
            kk[SUPER:, :] = kc_ref[...]
            vv[0:SUPER, :] = vp_ref[...]
            vv[SUPER:, :] = vc_ref[...]
            head0 = lax.broadcasted_iota(jnp.int32, (QK_BLOCK, QK_BLOCK), 1) < HEAD_DIM

            def widened(a):
                other = pltpu.roll(a, HEAD_DIM, 1)
                first = lax.broadcasted_iota(jnp.int32, a.shape, 1) < HEAD_DIM
                return jnp.where(first, a, other), jnp.where(first, other, a)

            def stacked(h0, h1):
                return jnp.concatenate([jnp.concatenate([h0, h0], axis=1), jnp.concatenate([h1, h1], axis=1)], axis=0)

            def widen_dd(i, carry):
                rows = pl.ds(pl.multiple_of(i * 256, 256), 256)
                dwide[0, rows, :], dwide[1, rows, :] = widened(d_ref[rows, :])
                return carry

            lax.fori_loop(0, SUPER // 256, widen_dd, 0)

            for b, dil in enumerate(DILATIONS):
                def unit(u, carry, b=b, dil=dil):
                    start = _unit_start(u, dil)
                    first_key = SUPER + start - QK_BLOCK * dil
                    qrows = _rows(start, QK_BLOCK, dil)
                    krows = _rows(first_key, KEYS, dil)
                    q2 = _stack_heads(q_ref[qrows, :].astype(BF16), head0)
                    dy2 = _stack_heads(dy_ref[qrows, :].astype(BF16), head0)
                    g2 = stacked(*widened(jnp.exp(m_ref[b, qrows, :] - l_ref[qrows, :])))
                    d2 = stacked(dwide[0, qrows, :], dwide[1, qrows, :])
                    k2 = kk[krows, :].astype(BF16)
                    v2 = vv[krows, :].astype(BF16)
                    prob = e_ref[b * UNITS + u].astype(F32) * g2
                    ds = (prob * (_mm_nt(dy2, v2) - d2)).astype(BF16)
                    dvacc[krows, :] += _mm_tn(prob.astype(BF16), dy2)
                    dkacc[krows, :] += _mm_tn(ds, q2)
                    dq2 = _mm(ds, k2)
                    dq = jnp.where(head0, dq2[0:QK_BLOCK], dq2[QK_BLOCK:]) * ATTN_SCALE
                    if b == 0:
                        dq_ref[qrows, :] = dq
                    else:
                        dq_ref[qrows, :] += dq
                    return carry

                lax.fori_loop(0, UNITS, unit, 0, unroll=8)

        dk_ref[...] = dkacc[0:SUPER, :]
        dv_ref[...] = dvacc[0:SUPER, :].astype(BF16)

    def cur_map(p, s):
        return (jnp.minimum(s, nsb - 1), p)

    def prev_map(p, s):
        return (jnp.clip(s - 1, 0, nsb - 1), p)

    cur = pl.BlockSpec((SUPER, QK_BLOCK), cur_map)
    prev = pl.BlockSpec((SUPER, QK_BLOCK), prev_map)
    return pl.pallas_call(
        body, name="attn_bwd", grid=(4, nsb + 1),
        in_specs=[cur, cur, prev, cur, prev, cur, cur, cur,
                  pl.BlockSpec((None, None, 3 * UNITS, KEYS, KEYS), lambda p, s: (p, jnp.minimum(s, nsb - 1), 0, 0, 0)),
                  pl.BlockSpec((3, SUPER, QK_BLOCK), lambda p, s: (0, jnp.minimum(s, nsb - 1), p)),
                  pl.BlockSpec(memory_space=pl.ANY)],
        out_specs=[cur, prev, prev],
        out_shape=[jax.ShapeDtypeStruct((t, ATTN_W), F32), jax.ShapeDtypeStruct((t, ATTN_W), F32),
                   jax.ShapeDtypeStruct((t, ATTN_W), BF16)],
        scratch_shapes=[pltpu.VMEM((2 * SUPER, QK_BLOCK), F32)] * 4 + [pltpu.VMEM((2, SUPER, QK_BLOCK), F32)],
        compiler_params=_cparams("parallel", "arbitrary"),
    )(q, k, k, v, v, dya, lse, dd, e_all, m_all, after)


def _inproj_bwd(dq, dk, dv, dzconv, zqk, x, dx1, g_mix, w_in, qg, kg, bd, tm):
    t = x.shape[0]
    nblk = t // tm
    shard = IN_COLS // N_DEV

    def body(dq_ref, dk_ref, dv_ref, dzc_ref, zqk_ref, x_ref, dx1_ref, g_ref, w_ref, qg_ref,
             kg_ref, bd_ref, dx_ref, dw_hbm, dg_ref, dqg_ref, dkg_ref, dw_ref, stage, stage_sem):
        @pl.when(pl.program_id(0) == 0)
        def _():
            for ref in (dw_ref, dg_ref, dqg_ref, dkg_ref):
                ref[...] = jnp.zeros_like(ref)

        parts = [dzc_ref[...]]
        for j, (dn_ref, gain_ref, dgain_ref) in enumerate(((dq_ref, qg_ref, dqg_ref), (dk_ref, kg_ref, dkg_ref))):
            dn = dn_ref[...]
            z = zqk_ref[:, j * ATTN_W:(j + 1) * ATTN_W].astype(F32)
            r = lax.rsqrt(_seg_sum64(z * z, bd_ref) * (1.0 / HEAD_DIM) + EPS)
            zhat = z * r
            dgain_ref[...] += jnp.sum(dn * zhat, axis=0, keepdims=True)
            gd = dn * gain_ref[...]
            parts.append((r * (gd - zhat * (_seg_sum64(gd * zhat, bd_ref) * (1.0 / HEAD_DIM)))).astype(BF16))
        parts.append(dv_ref[...].astype(BF16))
        dz = jnp.concatenate(parts, axis=1)

        r, xhat = _rms_stats(x_ref[...])
        g = g_ref[...]
        dw_ref[...] += _mm_tn((xhat * g).astype(BF16), dz)
        dh = _mm_nt(dz, w_ref[...])
        dg_ref[...] += jnp.sum(dh * xhat, axis=0, keepdims=True)
        dx_ref[...] = dx1_ref[...] + _rms_bwd(dh, xhat, r, g)

        @pl.when(pl.program_id(0) == nblk - 1)
        def _():
            for k in range(N_DEV):
                stage[...] = dw_ref[:, k * shard:(k + 1) * shard].astype(BF16)
                copy = pltpu.make_async_copy(stage, dw_hbm.at[k], stage_sem)
                copy.start()
                copy.wait()

    def blk(c):
        return pl.BlockSpec((tm, c), lambda i: (i, 0))

    return pl.pallas_call(
        body, name="inproj_bwd", grid=(nblk,),
        in_specs=[blk(ATTN_W)] * 3 + [blk(3 * CONV_W), blk(2 * ATTN_W), blk(D_MODEL), blk(D_MODEL), _full((1, D_MODEL)),
                                      _full((D_MODEL, IN_COLS)), _full((1, ATTN_W)), _full((1, ATTN_W)),
                                      _full((256, 256))],
        out_specs=[blk(D_MODEL), ANY, _full((1, D_MODEL)), _full((1, ATTN_W)), _full((1, ATTN_W))],
        out_shape=[jax.ShapeDtypeStruct((t, D_MODEL), F32), jax.ShapeDtypeStruct((N_DEV, D_MODEL, shard), BF16),
                   jax.ShapeDtypeStruct((1, D_MODEL), F32), jax.ShapeDtypeStruct((1, ATTN_W), F32),
                   jax.ShapeDtypeStruct((1, ATTN_W), F32)],
        scratch_shapes=[pltpu.VMEM((D_MODEL, IN_COLS), F32), pltpu.VMEM((D_MODEL, shard), BF16),
                        pltpu.SemaphoreType.DMA],
        compiler_params=_cparams("arbitrary"),
    )(dq, dk, dv, dzconv, zqk, x, dx1, g_mix, w_in, qg, kg, bd)


def _ordered_after(a, token):
    return a if token is None else a + token


def _local_step(x, p, target, w, tms, hooks=None):
    hooks = hooks or {}
    bd = jnp.asarray(np.kron(np.eye(4, dtype=np.float32), np.ones((HEAD_DIM, HEAD_DIM), np.float32)), BF16)
    qg = jnp.tile(w["q_norm_g"], (1, 8))
    kg = jnp.tile(w["k_norm_g"], (1, 8))
    slopes = np.exp2(-np.arange(1, 9, dtype=np.float32))
    slopes = jnp.asarray(np.broadcast_to(slopes.reshape(4, 2, 1), (4, 2, QK_BLOCK)))

    zconv, zqk, yc, q, k, v = _inproj_fwd(x, w["g_mix"], w["w_in"], w["conv_w"], w["conv_b"], qg, kg, bd, tms[0])
    ya, lse, e_all, m_all = _attn_fwd(q, k, v, slopes)
    if "late_weights" in hooks:
        w = {**w, **hooks["late_weights"](lse)}
    x1 = _outproj_fwd(ya, yc, x, w["g_out_conv"], w["g_out_attn"], w["w_out"], tms[0])
    gp, up, h2, x2 = _ffn_fwd(x1, w["g_ffn"], w["w_gate"], w["w_up"], w["w_down"], w["ffn_conv_w"], w["ffn_conv_b"],
                              tms[0])
    dx2, dx2b, loss, dw_pg, dw_pp, dg_ple = _ple_fwd_bwd(x2, p, target, w["g_ple"], w["w_ple_gate"], w["w_ple_proj"], tms[0])
    dh2, dw_down, dw_up, dw_gate, dfcw, dfcb = _ffn_bwd(dx2b, h2, gp, up, w["w_gate"], w["w_up"], w["w_down"],
                                                        w["ffn_conv_w"], w["ffn_conv_b"], tms[0])
    token = None
    if "ffn_grads" in hooks:
        token = hooks["ffn_grads"]({"w_ple_gate": dw_pg, "w_ple_proj": dw_pp, "w_down": dw_down, "w_up": dw_up,
                                    "w_gate": dw_gate, "ffn_conv_b": dfcb})
    dx1, dya, dd, dzconv, dw_out, dg_ffn, dgoc, dgoa, dcw, dcb = _outproj_bwd(
        dh2, dx2, x1, _ordered_after(w["g_ffn"], token), w["w_out"], yc, ya, w["g_out_conv"], w["g_out_attn"], zconv,
        w["conv_w"], w["conv_b"], bd, tms[0])
    token = hooks["outproj_done"](dx1) if "outproj_done" in hooks else None
    dq, dk, dv = _attn_bwd(q, k, v, dya, lse, dd, e_all, m_all, slopes if token is None else token)
    dx, dw_in, dg_mix, dqg, dkg = _inproj_bwd(dq, dk, dv, dzconv, zqk, x, dx1, w["g_mix"], w["w_in"], qg, kg, bd,
                                              tms[0])
    grads = {
        "g_mix": dg_mix, "w_in": dw_in, "conv_w": dcw, "conv_b": dcb,
        "q_norm_g": dqg, "k_norm_g": dkg,
        "g_out_conv": dgoc, "g_out_attn": dgoa, "w_out": dw_out, "g_ffn": dg_ffn, "w_gate": dw_gate, "w_up": dw_up,
        "ffn_conv_w": dfcw, "ffn_conv_b": dfcb, "w_down": dw_down, "g_ple": dg_ple, "w_ple_gate": dw_pg,
        "w_ple_proj": dw_pp,
    }
    return loss, dx, grads


ANY = pl.BlockSpec(memory_space=pl.ANY)
MESH = pl.DeviceIdType.MESH


def _all_gather(shards, name):
    n = len(shards)

    def body(*refs):
        ins, outs = refs[:n], refs[n:2 * n]
        send_sems, recv_sems, local_sems = refs[2 * n:]
        x, y, c = lax.axis_index("x"), lax.axis_index("y"), lax.axis_index("c")
        me, sibling = (x, y, c), (x, y, 1 - c)
        chips = [(1 - x, y), (x, 1 - y), (1 - x, 1 - y)]

        def slot(dev):
            return 4 * dev[0] + 2 * dev[1] + dev[2]

        def copy(b, k, block, to, src=None):
            dst = outs[b].at[slot(block)]
            return pltpu.make_async_remote_copy(
                src_ref=dst if src is None else src, dst_ref=dst, send_sem=send_sems.at[b, k],
                recv_sem=recv_sems.at[b, k], device_id=to, device_id_type=MESH)

        mine = [pltpu.make_async_copy(ins[b], outs[b].at[slot(me)], local_sems.at[b]) for b in range(n)]
        first, passed = [], []
        for b in range(n):
            mine[b].start()
            first.append(copy(b, 0, me, sibling, src=ins[b]))
            first += [copy(b, 1 + j, me, (*chip, c), src=ins[b]) for j, chip in enumerate(chips)]
        for cp in first:
            cp.start()
        for j, chip in enumerate(chips):
            for b in range(n):
                copy(b, 1 + j, (*chip, c), me).wait_recv()
                fwd = copy(b, 4 + j, (*chip, c), sibling)
                fwd.start()
                passed.append(fwd)
        for b in range(n):
            copy(b, 0, sibling, me).wait_recv()
            for j, chip in enumerate(chips):
                copy(b, 4 + j, (*chip, 1 - c), me).wait_recv()
        for cp in first + passed:
            cp.wait_send()
        for cp in mine:
            cp.wait()

    return pl.pallas_call(
        body, name=name,
        in_specs=[ANY] * n, out_specs=[ANY] * n,
        out_shape=[jax.ShapeDtypeStruct((N_DEV,) + s.shape, s.dtype) for s in shards],
        scratch_shapes=[pltpu.SemaphoreType.DMA((n, 7)), pltpu.SemaphoreType.DMA((n, 7)),
                        pltpu.SemaphoreType.DMA((n,))],
    )(*shards)


HBM = pl.BlockSpec(memory_space=pltpu.HBM)
SEM = pl.BlockSpec(memory_space=pltpu.SEMAPHORE)
EFFECT = pltpu.SideEffectType.DATAFLOW_SIDE_EFFECTING
FLIPS = ((0, 0, 1), (0, 1, 0), (0, 1, 1), (1, 0, 0), (1, 0, 1), (1, 1, 0), (1, 1, 1))


def _flip_peers():
    pos = (lax.axis_index("x"), lax.axis_index("y"), lax.axis_index("c"))
    return [tuple(1 - a if f else a for a, f in zip(pos, flip)) for flip in FLIPS]


def _hbm(a):
    return pltpu.with_memory_space_constraint(a, pltpu.HBM)


def _own_copies(own, src_refs, land_refs, send_sems, n_remote):
    return [pltpu.make_async_copy(src, dst, send_sems.at[n_remote + i])
            for i, (src, dst) in enumerate(own(src_refs, land_refs) if own else [])]


def _split_start(name, srcs, lands, plan, n_copies, after, own=None):
    n, m = len(srcs), len(lands)

    def body(*refs):
        send_sems, recv_sems, token = refs[n + m + 1], refs[n + m + 2], refs[-1]
        for i, (src, dst, peer) in enumerate(plan(refs[:n], refs[n:n + m])):
            pltpu.make_async_remote_copy(src_ref=src, dst_ref=dst, send_sem=send_sems.at[i], recv_sem=recv_sems.at[i],
                                         device_id=peer, device_id_type=MESH).start()
        for copy in _own_copies(own, refs[:n], refs[n:n + m], send_sems, n_copies):
            copy.start()
        token[...] = jnp.zeros_like(token)

    outs = pl.pallas_call(
        body, name=name + "_start",
        in_specs=[HBM] * (n + m) + [ANY],
        out_specs=[SEM, SEM] + [HBM] * (n + m) + [pl.BlockSpec(memory_space=pltpu.VMEM)],
        out_shape=[pltpu.SemaphoreType.DMA((n_copies + (n if own else 0),)), pltpu.SemaphoreType.DMA((n_copies,))]
        + [pltpu.HBM(a.shape, a.dtype) for a in list(srcs) + list(lands)] + [jax.ShapeDtypeStruct((1, D_MODEL), F32)],
        input_output_aliases={i: 2 + i for i in range(n + m)},
        compiler_params=pltpu.CompilerParams(has_side_effects=EFFECT),
    )(*[_hbm(a) for a in list(srcs) + list(lands)], after)
    return (outs[0], outs[1], outs[2:2 + n], outs[2 + n:2 + n + m]), outs[-1]


def _split_wait(name, started, plan, after, own=None):
    send_sems, recv_sems, srcs, lands = started
    n, m = len(srcs), len(lands)

    def body(*refs):
        send_ref, recv_ref = refs[n + m], refs[n + m + 1]
        copies = plan(refs[:n], refs[n:n + m])
        for i, (src, dst, peer) in enumerate(copies):
            copy = pltpu.make_async_remote_copy(src_ref=src, dst_ref=dst, send_sem=send_ref.at[i],
                                                recv_sem=recv_ref.at[i], device_id=peer, device_id_type=MESH)
            copy.wait_send()
            copy.wait_recv()
        for copy in _own_copies(own, refs[:n], refs[n:n + m], send_ref, len(copies)):
            copy.wait()

    outs = pl.pallas_call(
        body, name=name + "_wait",
        in_specs=[HBM] * (n + m) + [SEM, SEM, ANY],
        out_specs=[HBM] * (n + m),
        out_shape=[pltpu.HBM(a.shape, a.dtype) for a in list(srcs) + list(lands)],
        input_output_aliases={i: i for i in range(n + m)},
        compiler_params=pltpu.CompilerParams(has_side_effects=EFFECT),
    )(*srcs, *lands, send_sems, recv_sems, after)
    return outs[:n], outs[n:]


def _gather_plan(srcs, lands):
    slot = 4 * lax.axis_index("x") + 2 * lax.axis_index("y") + lax.axis_index("c")
    return [(src, land.at[slot], peer) for src, land in zip(srcs, lands) for peer in _flip_peers()]


def _own_slot(srcs, lands):
    slot = 4 * lax.axis_index("x") + 2 * lax.axis_index("y") + lax.axis_index("c")
    return [(src, land.at[slot]) for src, land in zip(srcs, lands)]


def _sibling_plan(srcs, lands):
    x, y, c = lax.axis_index("x"), lax.axis_index("y"), lax.axis_index("c")
    return [(src.at[k, 1 - c], land.at[k], (x, y, 1 - c)) for src, land in zip(srcs, lands) for k in range(N_CHIP)]


def _chip_plan(srcs, lands):
    x, y, c = lax.axis_index("x"), lax.axis_index("y"), lax.axis_index("c")
    return [(src.at[2 * cx + cy], land.at[2 * x + y], (cx, cy, c))
            for src, land in zip(srcs, lands) for cx, cy in ((1 - x, y), (x, 1 - y), (1 - x, 1 - y))]


def _row_tile(rows):
    for tr in range(min(rows, 512), 15, -16):
        if rows % tr == 0:
            return tr
    return rows


def _pair_sums(gs, lands, core, name):
    n = len(gs)

    def body(c_ref, *refs):
        for b in range(n):
            out = refs[2 * n + b]
            out[...] = (refs[b][...].astype(F32) + refs[n + b][...].astype(F32)).astype(out.dtype)

    def slab(a):
        return pl.BlockSpec((None,) + a.shape[1:], lambda k, c_ref: (k, 0, 0))

    return pl.pallas_call(
        body, name=name,
        grid_spec=pltpu.PrefetchScalarGridSpec(
            num_scalar_prefetch=1, grid=(N_CHIP,),
            in_specs=[pl.BlockSpec((None, None) + g.shape[2:], lambda k, c_ref: (k, c_ref[0], 0, 0)) for g in gs]
            + [slab(a) for a in lands],
            out_specs=[slab(a) for a in lands]),
        out_shape=[jax.ShapeDtypeStruct(a.shape, a.dtype) for a in lands],
        compiler_params=_cparams("parallel"),
    )(core, *gs, *lands)


def _adamw(own, arrived, chip, w, m, v, name):
    k, rows, cols = arrived.shape
    tr = _row_tile(rows)
    c1 = 1.0 / (1.0 - ADAM_B1 ** ADAM_STEP)
    c2 = 1.0 / (1.0 - ADAM_B2 ** ADAM_STEP)

    def body(chip_ref, o_ref, p_ref, w_ref, m_ref, v_ref, g_ref, d_ref, nm_ref, nv_ref):
        def slab(j):
            return jnp.where(chip_ref[0] == j, o_ref[j], p_ref[j]).astype(F32)

        g = slab(0)
        for j in range(1, k):
            g = g + slab(j)
        g_ref[...] = g
        nm = ADAM_B1 * m_ref[...] + (1.0 - ADAM_B1) * g
        nv = ADAM_B2 * v_ref[...] + (1.0 - ADAM_B2) * (g * g)
        nm_ref[...] = nm
        nv_ref[...] = nv
        d_ref[...] = -ADAM_LR * ((nm * c1) / (jnp.sqrt(nv * c2) + ADAM_EPS) + ADAM_WD * w_ref[...])

    blk = pl.BlockSpec((tr, cols), lambda i, c: (i, 0))
    stack = pl.BlockSpec((k, tr, cols), lambda i, c: (0, i, 0))
    return pl.pallas_call(
        body, name=name,
        grid_spec=pltpu.PrefetchScalarGridSpec(num_scalar_prefetch=1, grid=(rows // tr,),
                                               in_specs=[stack, stack, blk, blk, blk], out_specs=[blk] * 4),
        out_shape=[jax.ShapeDtypeStruct((rows, cols), F32)] * 4,
        compiler_params=_cparams("parallel"),
    )(chip, own, arrived, w, m, v)


SMALL_LAYOUT = (("g_mix", 0, 1024), ("conv_b", 1, 512), ("q_norm_g", 2, 64), ("k_norm_g", 3, 64),
                ("g_out_conv", 4, 512), ("g_out_attn", 5, 512), ("g_ffn", 6, 1024), ("ffn_conv_b", 7, 2816),
                ("g_ple", 10, 1024))
CONV_W_ROW = 11
FFN_CONV_W_ROW = 14
LOSS_ROW = 23


def _row_pieces(cols):
    return [(c, min(1024, cols - c)) for c in range(0, cols, 1024)]


def _pack_small(grads, loss_tile):
    names = [n for n, _, _ in SMALL_LAYOUT]

    def body(*refs):
        ins, cw_ref, fcw_ref, loss_ref, out_ref = refs[:len(names)], refs[-4], refs[-3], refs[-2], refs[-1]
        out_ref[...] = jnp.zeros_like(out_ref)
        for ref, (_, row, cols) in zip(ins, SMALL_LAYOUT):
            if ref.shape[1] == ATTN_W and cols == HEAD_DIM:
                out_ref[row:row + 1, 0:cols] = sum(ref[:, h:h + cols] for h in range(0, ATTN_W, cols))
                continue
            for j, (c, width) in enumerate(_row_pieces(cols)):
                out_ref[row + j:row + j + 1, 0:width] = ref[:, c:c + width]
        for k in range(3):
            out_ref[CONV_W_ROW + k:CONV_W_ROW + k + 1, 0:CONV_W] = cw_ref[k:k + 1, :]
            for j, (c, width) in enumerate(_row_pieces(D_FF)):
                row = FFN_CONV_W_ROW + 3 * k + j
                out_ref[row:row + 1, 0:width] = fcw_ref[k:k + 1, c:c + width]
        out_ref[LOSS_ROW:LOSS_ROW + 1, 0:128] = loss_ref[0:1, :]

    return pl.pallas_call(
        body, name="pack_small_grads", out_shape=jax.ShapeDtypeStruct((SMALL_ROWS, 1024), F32),
    )(*[grads[n] for n in names], grads["conv_w"], grads["ffn_conv_w"], loss_tile)


def _adamw_small(arrived, conv_parts, fconv_parts, wts, mom, var):
    names = [n for n, _, _ in SMALL_LAYOUT] + ["conv_w", "ffn_conv_w"]
    c1 = 1.0 / (1.0 - ADAM_B1 ** ADAM_STEP)
    c2 = 1.0 / (1.0 - ADAM_B2 ** ADAM_STEP)
    n = len(names)

    def body(*refs):
        land, cw_ref, fcw_ref = refs[0], refs[1], refs[2]
        state = refs[3:3 + 3 * n]
        outs = refs[3 + 3 * n:]

        def total(piece):
            acc = piece(0)
            for d in range(1, N_DEV):
                acc = acc + piece(d)
            return acc

        for i, name in enumerate(names):
            if name == "conv_w":
                g = total(lambda d: cw_ref[d])
            elif name == "ffn_conv_w":
                g = total(lambda d: fcw_ref[d])
            else:
                _, row, cols = SMALL_LAYOUT[i]
                pieces = [total(lambda d, j=j, width=width: land[d, row + j:row + j + 1, 0:width])
                          for j, (_, width) in enumerate(_row_pieces(cols))]
                g = pieces[0] if len(pieces) == 1 else jnp.concatenate(pieces, axis=1)
            w_ref, m_ref, v_ref = state[3 * i:3 * i + 3]
            nm = ADAM_B1 * m_ref[...] + (1.0 - ADAM_B1) * g
            nv = ADAM_B2 * v_ref[...] + (1.0 - ADAM_B2) * (g * g)
            outs[4 * i][...] = g
            outs[4 * i + 1][...] = -ADAM_LR * ((nm * c1) / (jnp.sqrt(nv * c2) + ADAM_EPS) + ADAM_WD * w_ref[...])
            outs[4 * i + 2][...] = nm
            outs[4 * i + 3][...] = nv
        outs[-1][...] = total(lambda d: land[d, LOSS_ROW:LOSS_ROW + 1, 0:128])

    state = [a[nm_] for nm_ in names for a in (wts, mom, var)]
    shapes = [jax.ShapeDtypeStruct(wts[nm_].shape, F32) for nm_ in names for _ in range(4)]
    outs = pl.pallas_call(
        body, name="adamw_small", out_shape=shapes + [jax.ShapeDtypeStruct((1, 128), F32)],
    )(arrived, conv_parts, fconv_parts, *state)
    return {nm_: tuple(outs[4 * i:4 * i + 4]) for i, nm_ in enumerate(names)}, outs[-1][0, 0]


COL_SHARDED = ("w_in", "w_ple_proj")
TRANSPOSED = ("w_gate", "w_up")
CONV_SHARDED = (("conv_w", CONV_W), ("ffn_conv_w", D_FF))


def _gathered_to_full(name, gathered):
    if name in COL_SHARDED:
        return gathered.transpose(1, 0, 2).reshape(gathered.shape[1], -1)
    return gathered.reshape(-1, gathered.shape[2])


def _full_to_stacked(name, grad, shard_shape):
    sr, sc = shard_shape
    if grad.ndim == 3:
        a = grad
    elif name in COL_SHARDED:
        a = grad.reshape(sr, N_DEV, sc).transpose(1, 0, 2)
    else:
        a = grad.reshape(N_DEV, sr, sc)
    return a.astype(BF16).reshape(N_CHIP, 2, sr, sc)


def _pad_rows(vec, rows):
    return jnp.pad(vec, (0, rows * 1024 - vec.shape[0])).reshape(rows, 1024)


def kernel(x, p, g_mix, w_in, conv_w, conv_b, q_norm_g, k_norm_g, g_out_conv, g_out_attn, w_out, g_ffn, w_gate, w_up, ffn_conv_w, ffn_conv_b, w_down, g_ple, w_ple_gate, w_ple_proj, loss_target, m_g_mix, m_w_in, m_conv_w, m_conv_b, m_q_norm_g, m_k_norm_g, m_g_out_conv, m_g_out_attn, m_w_out, m_g_ffn, m_w_gate, m_w_up, m_ffn_conv_w, m_ffn_conv_b, m_w_down, m_g_ple, m_w_ple_gate, m_w_ple_proj, v_g_mix, v_w_in, v_conv_w, v_conv_b, v_q_norm_g, v_k_norm_g, v_g_out_conv, v_g_out_attn, v_w_out, v_g_ffn, v_w_gate, v_w_up, v_ffn_conv_w, v_ffn_conv_b, v_w_down, v_g_ple, v_w_ple_gate, v_w_ple_proj):
    args = dict(locals())
    names = ["g_mix", "w_in", "conv_w", "conv_b", "q_norm_g", "k_norm_g", "g_out_conv", "g_out_attn", "w_out", "g_ffn",
             "w_gate", "w_up", "ffn_conv_w", "ffn_conv_b", "w_down", "g_ple", "w_ple_gate", "w_ple_proj"]
    big = list(BIG)
    conv = [n for n, _ in CONV_SHARDED]

    def local(prefix):
        out = {n: (args[prefix + n][0] if n in big or n in conv else args[prefix + n]) for n in names}
        out.update({n: out[n].T for n in TRANSPOSED})
        return out

    wts, mom, var = local(""), local("m_"), local("v_")
    shard_shapes = {n: wts[n].shape for n in big}
    dev = 4 * lax.axis_index("x") + 2 * lax.axis_index("y") + lax.axis_index("c")
    core = lax.axis_index("c").astype(jnp.int32).reshape(1)

    conv_local = _pad_rows(jnp.concatenate([wts[n].reshape(-1) for n in conv]), 8).reshape(8, 1024)
    late = [n for n in big if n != "w_in"]
    w_in_all, conv_all = _all_gather([wts["w_in"].astype(BF16), conv_local], "gather_weights")
    late_shards = [wts[n].astype(BF16) for n in late]
    gathering, token = _split_start("gather_late_weights", late_shards,
                                    [lax.empty((N_DEV,) + s.shape, BF16) for s in late_shards], _gather_plan,
                                    7 * len(late), w_in_all, own=_own_slot)
    full = dict(wts)
    full["w_in"] = _gathered_to_full("w_in", w_in_all)
    full["g_mix"] = _ordered_after(wts["g_mix"], token)
    flying = {}

    def late_weights(after):
        _, lands = _split_wait("gather_late_weights", gathering, _gather_plan, after, own=_own_slot)
        return {n: _gathered_to_full(n, land) for n, land in zip(late, lands)}

    early = ["w_ple_gate", "w_ple_proj", "w_down", "w_up", "w_gate"]

    def ffn_grads(g):
        stacked = [_full_to_stacked(n, g[n], shard_shapes[n]) for n in early]
        flying["sibling"], tok = _split_start("rs_sibling_early", stacked,
                                              [lax.empty((N_CHIP,) + s.shape[2:], BF16) for s in stacked],
                                              _sibling_plan, N_CHIP * len(early), g["ffn_conv_b"])
        return tok

    def outproj_done(after):
        stacked, landed = _split_wait("rs_sibling_early", flying["sibling"], _sibling_plan, after)
        parts = _pair_sums(stacked, landed, core, "rs_pair_sums_early")
        flying["chip"], tok = _split_start("rs_chip_early", parts, [lax.empty(q.shape, BF16) for q in parts],
                                           _chip_plan, 3 * len(early), landed[0])
        return tok

    off = 0
    for n, width in CONV_SHARDED:
        sc = width // N_DEV
        a = conv_all.reshape(N_DEV, -1)[:, off:off + 3 * sc].reshape(N_DEV, 3, sc)
        full[n] = a.transpose(1, 0, 2).reshape(3, width)
        off += 3 * sc

    loss, dx, grads = _local_step(x[0], p[0, 0], loss_target[0], full, (512, 256),
                                  {"late_weights": late_weights, "ffn_grads": ffn_grads, "outproj_done": outproj_done})

    chip = (2 * lax.axis_index("x") + lax.axis_index("y")).astype(jnp.int32).reshape(1)

    def adamw_of(group, parts, arrived):
        return {n: _adamw(own, got, chip, wts[n], mom[n], var[n], f"adamw_{n}")
                for n, own, got in zip(group, parts, arrived)}

    last = [n for n in big if n not in early]
    stacked = [_full_to_stacked(n, grads[n], shard_shapes[n]) for n in last]
    flying["sibling_last"], tok = _split_start("rs_sibling_last", stacked,
                                               [lax.empty((N_CHIP,) + s.shape[2:], BF16) for s in stacked],
                                               _sibling_plan, N_CHIP * len(last), dx)
    packed = _pack_small(grads, loss)
    flying["small"], tok = _split_start("gather_small_grads", [packed], [lax.empty((N_DEV,) + packed.shape, F32)],
                                        _gather_plan, N_DEV - 1, tok, own=_own_slot)
    stacked, landed = _split_wait("rs_sibling_last", flying["sibling_last"], _sibling_plan, tok)
    parts = _pair_sums(stacked, landed, core, "rs_pair_sums_last")
    flying["chip_last"], tok = _split_start("rs_chip_last", parts, [lax.empty(q.shape, BF16) for q in parts],
                                            _chip_plan, 3 * len(last), landed[0])

    parts, arrived = _split_wait("rs_chip_early", flying["chip"], _chip_plan, tok)
    out = adamw_of(early, parts, arrived)
    _, (small_all,) = _split_wait("gather_small_grads", flying["small"], _gather_plan, out[early[-1]][0],
                                  own=_own_slot)
    taps = small_all[:, CONV_W_ROW:CONV_W_ROW + 3, 0:CONV_W]
    ftaps = small_all[:, FFN_CONV_W_ROW:FFN_CONV_W_ROW + 9, :].reshape(N_DEV, 3, 3 * 1024)
    small_out, loss_total = _adamw_small(
        small_all, lax.dynamic_slice(taps, (0, 0, dev * (CONV_W // N_DEV)), (N_DEV, 3, CONV_W // N_DEV)),
        lax.dynamic_slice(ftaps, (0, 0, dev * (D_FF // N_DEV)), (N_DEV, 3, D_FF // N_DEV)), wts, mom, var)
    out.update(small_out)
    parts, arrived = _split_wait("rs_chip_last", flying["chip_last"], _chip_plan, small_out["g_mix"][0])
    out.update(adamw_of(last, parts, arrived))
    def result(n, which):
        a = out[n][which]
        return (a.T if n in TRANSPOSED else a).reshape(args[n].shape)

    return (loss_total, dx[None], *[result(n, which) for which in range(4) for n in names])
```

```python
import jax
import jax.numpy as jnp
import numpy as np
from jax import lax
from jax.experimental import pallas as pl
from jax.experimental.pallas import tpu as pltpu

F32 = jnp.float32
BF16 = jnp.bfloat16

D_MODEL = 1024
CONV_W = 512
ATTN_W = 512
HEAD_DIM = 64
D_FF = 2816
PLE_DIM = 256
IN_COLS = 3 * CONV_W + 3 * ATTN_W
EPS = 1e-6
QK_BLOCK = 128
DILATIONS = (1, 4, 16)
ATTN_SCALE = HEAD_DIM ** -0.5

ADAM_LR = 0.001
ADAM_B1 = 0.9
ADAM_B2 = 0.999
ADAM_EPS = 1e-08
ADAM_WD = 0.01
ADAM_STEP = 10

N_DEV = 8
N_CHIP = 4
V7X_VMEM_LIMIT = 56 * 1024 * 1024
V7X_VMEM_LIMIT_LARGE = 62 * 1024 * 1024
FF_CHUNKS = 2
FFN_BWD_PARTS = 1

BIG = ("w_in", "w_out", "w_gate", "w_up", "w_down", "w_ple_gate", "w_ple_proj")
SMALL_ROWS = 24


def _cparams(*sem, vmem=V7X_VMEM_LIMIT):
    return pltpu.CompilerParams(dimension_semantics=sem, vmem_limit_bytes=vmem)


def _mm(a, b):
    return jnp.dot(a, b, preferred_element_type=F32)


def _mm_nt(a, b):
    return lax.dot_general(a, b, (((1,), (1,)), ((), ())), preferred_element_type=F32)


def _mm_tn(a, b):
    return lax.dot_general(a, b, (((0,), (0,)), ((), ())), preferred_element_type=F32)


def _full(shape):
    nd = len(shape)
    return pl.BlockSpec(shape, lambda *_: (0,) * nd)


def _rms_stats(x):
    r = lax.rsqrt(jnp.mean(x * x, axis=-1, keepdims=True) + EPS)
    return r, x * r


def _rms_bwd(dy, xhat, r, g):
    gd = dy * g
    return r * (gd - xhat * jnp.mean(gd * xhat, axis=-1, keepdims=True))


def _seg_sum64(v, bd_ref):
    outs = []
    for c in range(0, v.shape[1], 256):
        vc = v[:, c:c + 256]
        hi = vc.astype(BF16)
        lo = (vc - hi.astype(F32)).astype(BF16)
        outs.append(_mm(hi, bd_ref[...]) + _mm(lo, bd_ref[...]))
    return outs[0] if len(outs) == 1 else jnp.concatenate(outs, axis=1)


def _shift_rows(u, k, edge_rows):
    out = pltpu.roll(u, k, 0)
    row = lax.broadcasted_iota(jnp.int32, (8, u.shape[1]), 0)
    head = out[0:8]
    for j in range(k):
        head = jnp.where(row == j, edge_rows[k - 1 - j], head)
    return jnp.concatenate([head, out[8:]], axis=0)


def _shift_rows_up(u, k, edge_rows):
    n = u.shape[0]
    out = pltpu.roll(u, n - k, 0)
    row = lax.broadcasted_iota(jnp.int32, (8, u.shape[1]), 0)
    tail = out[n - 8:n]
    for j in range(k):
        tail = jnp.where(row == 8 - k + j, edge_rows[j], tail)
    return jnp.concatenate([out[0:n - 8], tail], axis=0)


def _conv_fwd(u, c1, c2, w_ref, b_ref):
    u1 = _shift_rows(u, 1, (c1,))
    u2 = _shift_rows(u, 2, (c1, c2))
    y = u2 * w_ref[0:1, :] + u1 * w_ref[1:2, :] + u * w_ref[2:3, :] + b_ref[...]
    return y, u1, u2


def _conv_bwd_input(dy, n1row, n2row, w_ref):
    d1 = _shift_rows_up(dy, 1, (n1row,))
    d2 = _shift_rows_up(dy, 2, (n1row, n2row))
    return dy * w_ref[2:3, :] + d1 * w_ref[1:2, :] + d2 * w_ref[0:1, :]


def _sigmoid(x):
    return 1.0 / (1.0 + jnp.exp(-x))


def _inproj_fwd(x, g_mix, w_in, conv_w, conv_b, qg, kg, bd, tm):
    t = x.shape[0]

    def body(x_ref, g_ref, w_ref, cw_ref, cb_ref, qg_ref, kg_ref, bd_ref,
             zc_ref, zqk_ref, yc_ref, q_ref, k_ref, v_ref, carry_ref):
        @pl.when(pl.program_id(0) == 0)
        def _():
            carry_ref[...] = jnp.zeros_like(carry_ref)

        _, xhat = _rms_stats(x_ref[...])
        h = (xhat * g_ref[...]).astype(BF16)
        zconv = _mm(h, w_ref[:, 0:3 * CONV_W])
        zc_ref[...] = zconv.astype(BF16)
        u = zconv[:, CONV_W:2 * CONV_W] * zconv[:, 2 * CONV_W:3 * CONV_W]
        cv, _, _ = _conv_fwd(u, carry_ref[7:8, :], carry_ref[6:7, :], cw_ref, cb_ref)
        yc_ref[...] = (zconv[:, 0:CONV_W] * cv).astype(BF16)
        carry_ref[...] = u[tm - 8:tm, :]

        zqk = _mm(h, w_ref[:, 3 * CONV_W:3 * CONV_W + 2 * ATTN_W])
        zqk_ref[...] = zqk.astype(BF16)
        for j, (gain_ref, out_ref, scale) in enumerate(((qg_ref, q_ref, ATTN_SCALE), (kg_ref, k_ref, 1.0))):
            z = zqk[:, j * ATTN_W:(j + 1) * ATTN_W]
            r = lax.rsqrt(_seg_sum64(z * z, bd_ref) * (1.0 / HEAD_DIM) + EPS)
            out_ref[...] = z * r * gain_ref[...] * scale
        v_ref[...] = _mm(h, w_ref[:, 3 * CONV_W + 2 * ATTN_W:IN_COLS])

    def blk(c):
        return pl.BlockSpec((tm, c), lambda i: (i, 0))

    return pl.pallas_call(
        body, name="inproj_fwd", grid=(t // tm,),
        in_specs=[blk(D_MODEL), _full((1, D_MODEL)), _full((D_MODEL, IN_COLS)), _full((3, CONV_W)),
                  _full((1, CONV_W)), _full((1, ATTN_W)), _full((1, ATTN_W)), _full((256, 256))],
        out_specs=[blk(3 * CONV_W), blk(2 * ATTN_W), blk(CONV_W), blk(ATTN_W), blk(ATTN_W), blk(ATTN_W)],
        out_shape=[jax.ShapeDtypeStruct((t, 3 * CONV_W), BF16), jax.ShapeDtypeStruct((t, 2 * ATTN_W), BF16),
                   jax.ShapeDtypeStruct((t, CONV_W), BF16), jax.ShapeDtypeStruct((t, ATTN_W), F32),
                   jax.ShapeDtypeStruct((t, ATTN_W), F32), jax.ShapeDtypeStruct((t, ATTN_W), F32)],
        scratch_shapes=[pltpu.VMEM((8, CONV_W), F32)],
        compiler_params=_cparams("arbitrary"),
    )(x, g_mix, w_in, conv_w, conv_b, qg, kg, bd)


SUPER = 16 * QK_BLOCK
KEYS = 2 * QK_BLOCK
UNITS = SUPER // QK_BLOCK


def _rows(start, size, dil):
    return pl.ds(start, size) if dil == 1 else pl.ds(start, size, stride=dil)


def _attn_bias(sl_ref, dil):
    qi = lax.broadcasted_iota(jnp.int32, (KEYS, KEYS), 0)
    kj = lax.broadcasted_iota(jnp.int32, (KEYS, KEYS), 1)
    step = jnp.bitwise_and(qi, QK_BLOCK - 1) + QK_BLOCK - kj
    slope = jnp.where(qi < QK_BLOCK, sl_ref[0, 0:1, 0:1], sl_ref[0, 1:2, 0:1])
    bias = jnp.where(jnp.logical_and(step >= 0, step <= QK_BLOCK), -slope * (step * dil).astype(F32), -jnp.inf)
    return bias, kj >= QK_BLOCK


def _unit_start(u, dil):
    if dil == 1:
        return pl.multiple_of(u * QK_BLOCK, QK_BLOCK)
    if dil == 4:
        return jnp.bitwise_and(u, 3) + (u // 4) * (4 * QK_BLOCK)
    return u


def _stack_heads(a, head0):
    zero = jnp.zeros_like(a)
    return jnp.concatenate([jnp.where(head0, a, zero), jnp.where(head0, zero, a)], axis=0)


def _attn_fwd(q, k, v, slopes):
    t = q.shape[0]
    nsb = t // SUPER

    def body(q_ref, kc_ref, kp_ref, vc_ref, vp_ref, sl_ref, o_ref, l_ref, e_ref, m_ref, kk, vv, ob, lb):
        s = pl.program_id(1)
        kk[0:SUPER, :] = kp_ref[...]
        kk[SUPER:, :] = kc_ref[...]
        vv[0:SUPER, :] = vp_ref[...]
        vv[SUPER:, :] = vc_ref[...]
        head0 = lax.broadcasted_iota(jnp.int32, (QK_BLOCK, QK_BLOCK), 1) < HEAD_DIM

        for b, dil in enumerate(DILATIONS):
            bias, own_half = _attn_bias(sl_ref, dil)

            def unit(u, carry, b=b, dil=dil, bias=bias, own_half=own_half):
                start = _unit_start(u, dil)
                first_key = SUPER + start - QK_BLOCK * dil
                q2 = _stack_heads(q_ref[_rows(start, QK_BLOCK, dil), :].astype(BF16), head0)
                k2 = kk[_rows(first_key, KEYS, dil), :].astype(BF16)
                v2 = vv[_rows(first_key, KEYS, dil), :].astype(BF16)
                has_prev = jnp.logical_or(s > 0, start >= QK_BLOCK * dil)
                sc = jnp.where(jnp.logical_or(own_half, has_prev), _mm_nt(q2, k2) + bias, -jnp.inf)
                m = jnp.max(sc, axis=-1, keepdims=True)
                e = jnp.exp(sc - m)
                den = jnp.sum(e, axis=-1, keepdims=True)
                eb = e.astype(BF16)
                e_ref[b * UNITS + u] = eb
                o2 = _mm(eb, v2) / den
                l2 = m + jnp.log(den)
                ob[b, _rows(start, QK_BLOCK, dil), :] = jnp.where(head0, o2[0:QK_BLOCK], o2[QK_BLOCK:])
                lb[b, _rows(start, QK_BLOCK, dil), :] = jnp.where(head0, l2[0:QK_BLOCK], l2[QK_BLOCK:])
                m_ref[b, _rows(start, QK_BLOCK, dil), :] = jnp.where(head0, m[0:QK_BLOCK], m[QK_BLOCK:])
                return carry

            lax.fori_loop(0, UNITS, unit, 0, unroll=16)

        def merge(i, carry):
            rows = pl.ds(pl.multiple_of(i * 256, 256), 256)
            la, lb_, lc = lb[0, rows, :], lb[1, rows, :], lb[2, rows, :]
            mx = jnp.maximum(jnp.maximum(la, lb_), lc)
            wa, wb, wc = jnp.exp(la - mx), jnp.exp(lb_ - mx), jnp.exp(lc - mx)
            sw = wa + wb + wc
            o_ref[rows, :] = ((wa * ob[0, rows, :] + wb * ob[1, rows, :] + wc * ob[2, rows, :]) / sw).astype(BF16)
            l_ref[rows, :] = mx + jnp.log(sw)
            return carry

        lax.fori_loop(0, SUPER // 256, merge, 0)

    cur = pl.BlockSpec((SUPER, QK_BLOCK), lambda p, s: (s, p))
    prev = pl.BlockSpec((SUPER, QK_BLOCK), lambda p, s: (jnp.maximum(s - 1, 0), p))
    return pl.pallas_call(
        body, name="attn_fwd", grid=(4, nsb),
        in_specs=[cur, cur, prev, cur, prev, pl.BlockSpec((1, 2, QK_BLOCK), lambda p, s: (p, 0, 0))],
        out_specs=[cur, cur, pl.BlockSpec((None, None, 3 * UNITS, KEYS, KEYS), lambda p, s: (p, s, 0, 0, 0)),
                   pl.BlockSpec((3, SUPER, QK_BLOCK), lambda p, s: (0, s, p))],
        out_shape=[jax.ShapeDtypeStruct((t, ATTN_W), BF16), jax.ShapeDtypeStruct((t, ATTN_W), F32),
                   jax.ShapeDtypeStruct((4, nsb, 3 * UNITS, KEYS, KEYS), BF16),
                   jax.ShapeDtypeStruct((3, t, ATTN_W), F32)],
        scratch_shapes=[pltpu.VMEM((2 * SUPER, QK_BLOCK), F32), pltpu.VMEM((2 * SUPER, QK_BLOCK), F32),
                        pltpu.VMEM((3, SUPER, QK_BLOCK), F32), pltpu.VMEM((3, SUPER, QK_BLOCK), F32)],
        compiler_params=_cparams("parallel", "arbitrary"),
    )(q, k, k, v, v, slopes)


def _outproj_fwd(ya, yc, x, goc, goa, w_out, tm):
    t = x.shape[0]

    def body(ya_ref, yc_ref, x_ref, goc_ref, goa_ref, w_ref, x1_ref):
        _, ychat = _rms_stats(yc_ref[...].astype(F32))
        _, yahat = _rms_stats(ya_ref[...].astype(F32))
        acc = _mm((ychat * goc_ref[...]).astype(BF16), w_ref[0:CONV_W, :])
        acc += _mm((yahat * goa_ref[...]).astype(BF16), w_ref[CONV_W:, :])
        x1_ref[...] = x_ref[...] + acc

    def blk(c):
        return pl.BlockSpec((tm, c), lambda i: (i, 0))

    return pl.pallas_call(
        body, name="outproj_fwd", grid=(t // tm,),
        in_specs=[blk(ATTN_W), blk(CONV_W), blk(D_MODEL), _full((1, CONV_W)), _full((1, ATTN_W)),
                  _full((D_MODEL, D_MODEL))],
        out_specs=blk(D_MODEL),
        out_shape=jax.ShapeDtypeStruct((t, D_MODEL), F32),
        compiler_params=_cparams("parallel"),
    )(ya, yc, x, goc, goa, w_out)


def _ffn_fwd(x1, g_ffn, w_gate_t, w_up_t, w_down, fcw, fcb, tm):
    t = x1.shape[0]

    def body(x_ref, g_ref, wg_ref, wu_ref, wd_ref, cw_ref, cb_ref, gp_ref, up_ref, h_ref, x2_ref, carry_ref):
        @pl.when(pl.program_id(0) == 0)
        def _():
            carry_ref[...] = jnp.zeros_like(carry_ref)

        xv = x_ref[...]
        _, xhat = _rms_stats(xv)
        h = (xhat * g_ref[...]).astype(BF16)
        h_ref[...] = h
        gp = _mm_nt(h, wg_ref[...])
        gp_ref[...] = gp.astype(BF16)
        gate, _, _ = _conv_fwd(gp, carry_ref[7:8, :], carry_ref[6:7, :], cw_ref, cb_ref)
        carry_ref[...] = gp[tm - 8:tm, :]
        up = _mm_nt(h, wu_ref[...])
        up_ref[...] = up.astype(BF16)
        a = (gate * _sigmoid(gate) * up).astype(BF16)
        x2_ref[...] = xv + _mm(a, wd_ref[...])

    def blk(c):
        return pl.BlockSpec((tm, c), lambda i: (i, 0))

    def weight():
        return pl.BlockSpec((D_FF, D_MODEL), lambda i: (0, 0), pipeline_mode=pl.Buffered(1))

    return pl.pallas_call(
        body, name="ffn_fwd", grid=(t // tm,),
        in_specs=[blk(D_MODEL), _full((1, D_MODEL)), weight(), weight(), weight(), _full((3, D_FF)), _full((1, D_FF))],
        out_specs=[blk(D_FF), blk(D_FF), blk(D_MODEL), blk(D_MODEL)],
        out_shape=[jax.ShapeDtypeStruct((t, D_FF), BF16), jax.ShapeDtypeStruct((t, D_FF), BF16),
                   jax.ShapeDtypeStruct((t, D_MODEL), BF16), jax.ShapeDtypeStruct((t, D_MODEL), F32)],
        scratch_shapes=[pltpu.VMEM((8, D_FF), F32)],
        compiler_params=_cparams("arbitrary", vmem=V7X_VMEM_LIMIT_LARGE),
    )(x1, g_ffn, w_gate_t, w_up_t, w_down, fcw, fcb)


def _ple_fwd_bwd(x2, p, target, g_ple, w_pg, w_pp, tm):
    t = x2.shape[0]

    def body(x_ref, p_ref, t_ref, g_ref, wg_ref, wp_ref, dx_ref, dxb_ref, loss_ref, dwgb_ref, dwp_ref, dg_ref,
             dwg_ref):
        @pl.when(pl.program_id(0) == 0)
        def _():
            loss_ref[...] = jnp.zeros_like(loss_ref)
            dwg_ref[...] = jnp.zeros_like(dwg_ref)
            dwp_ref[...] = jnp.zeros_like(dwp_ref)
            dg_ref[...] = jnp.zeros_like(dg_ref)

        xv = x_ref[...]
        r, xhat = _rms_stats(xv)
        g = g_ref[...]
        h = (xhat * g).astype(BF16)
        pg = _sigmoid(_mm(h, wg_ref[...]))
        pb = p_ref[...].astype(BF16)
        pp = _mm(pb, wp_ref[...])
        err = xv + pg * pp - t_ref[...]
        loss_ref[...] += 0.5 * jnp.sum(jnp.mean(err * err, axis=-1, keepdims=True))
        dx3 = err * (1.0 / D_MODEL)
        d_pp = (dx3 * pg).astype(BF16)
        d_pre = (dx3 * pp * pg * (1.0 - pg)).astype(BF16)
        dwp_ref[...] += _mm_tn(pb, d_pp)
        dwg_ref[...] += _mm_tn(h, d_pre)
        dh = _mm_nt(d_pre, wg_ref[...])
        dg_ref[...] += jnp.sum(dh * xhat, axis=0, keepdims=True)
        dx2 = dx3 + _rms_bwd(dh, xhat, r, g)
        dx_ref[...] = dx2
        dxb_ref[...] = dx2.astype(BF16)

        @pl.when(pl.program_id(0) == t // tm - 1)
        def _():
            dwgb_ref[...] = dwg_ref[...].astype(BF16)

    def blk(c):
        return pl.BlockSpec((tm, c), lambda i: (i, 0))

    return pl.pallas_call(
        body, name="ple_fwd_bwd", grid=(t // tm,),
        in_specs=[blk(D_MODEL), blk(PLE_DIM), blk(D_MODEL), _full((1, D_MODEL)), _full((D_MODEL, D_MODEL)),
                  _full((PLE_DIM, D_MODEL))],
        out_specs=[blk(D_MODEL), blk(D_MODEL), _full((8, 128)), _full((D_MODEL, D_MODEL)),
                   _full((PLE_DIM, D_MODEL)), _full((1, D_MODEL))],
        out_shape=[jax.ShapeDtypeStruct((t, D_MODEL), F32), jax.ShapeDtypeStruct((t, D_MODEL), BF16),
                   jax.ShapeDtypeStruct((8, 128), F32),
                   jax.ShapeDtypeStruct((D_MODEL, D_MODEL), BF16), jax.ShapeDtypeStruct((PLE_DIM, D_MODEL), F32),
                   jax.ShapeDtypeStruct((1, D_MODEL), F32)],
        scratch_shapes=[pltpu.VMEM((D_MODEL, D_MODEL), F32)],
        compiler_params=_cparams("arbitrary"),
    )(x2, p, target, g_ple, w_pg, w_pp)


def _ffn_bwd(dx2, h2, gp, up, w_gate, w_up, w_down, fcw, fcb, tm):
    t = dx2.shape[0]
    nblk = t // tm
    fc = D_FF // FF_CHUNKS
    half = tm // FFN_BWD_PARTS

    def body(dx_ref, h_ref, gp_ref, gph_ref, up_ref, wg_ref, wu_ref, wd_ref, cw_ref, cb_ref,
             dh_ref, dwd_hbm, dwu_hbm, dwg_hbm, dcw_ref, dcb_ref, carry_ref, a_scr, dup_scr, dgp_scr,
             dwd_acc, dwu_acc, dwg_acc, stage, stage_sem):
        i = pl.program_id(1)

        @pl.when(i == 0)
        def _():
            carry_ref[...] = jnp.zeros_like(carry_ref)
            dwd_acc[...] = jnp.zeros_like(dwd_acc)
            dwu_acc[...] = jnp.zeros_like(dwu_acc)
            dwg_acc[...] = jnp.zeros_like(dwg_acc)
            dcw_ref[...] = jnp.zeros_like(dcw_ref)
            dcb_ref[...] = jnp.zeros_like(dcb_ref)

        keep = (i < nblk - 1).astype(F32)
        later = carry_ref[...]
        for hf in reversed(range(FFN_BWD_PARTS)):
            rows = slice(hf * half, (hf + 1) * half)
            dxb = dx_ref[rows, :]
            gp_v = gp_ref[rows, :].astype(F32)
            if hf > 0:
                before = gp_ref[hf * half - 16:hf * half, :].astype(F32)
            else:
                before = gph_ref[...].astype(F32) * keep
            gate, gp1, gp2 = _conv_fwd(gp_v, before[15:16, :], before[14:15, :], cw_ref, cb_ref)
            s = _sigmoid(gate)
            silu = gate * s
            up_v = up_ref[rows, :].astype(F32)
            da = _mm_nt(dxb, wd_ref[...])
            a_scr[rows, :] = (silu * up_v).astype(BF16)
            d_up = (da * silu).astype(BF16)
            dup_scr[rows, :] = d_up
            d_gate = da * up_v * (s * (1.0 + gate * (1.0 - s)))
            d_gp = _conv_bwd_input(d_gate, later[0:1, :], later[1:2, :], cw_ref).astype(BF16)
            dgp_scr[rows, :] = d_gp
            later = d_gate[0:8, :]
            dcw_ref[0:1, :] += jnp.sum(d_gate * gp2, axis=0, keepdims=True)
            dcw_ref[1:2, :] += jnp.sum(d_gate * gp1, axis=0, keepdims=True)
            dcw_ref[2:3, :] += jnp.sum(d_gate * gp_v, axis=0, keepdims=True)
            dcb_ref[...] += jnp.sum(d_gate, axis=0, keepdims=True)
            dh_ref[rows, :] = (_mm(d_gp, wg_ref[...]) + _mm(d_up, wu_ref[...])).astype(BF16)
        carry_ref[...] = later
        dwd_acc[...] += _mm_tn(a_scr[...], dx_ref[...])
        dwu_acc[...] += _mm_tn(h_ref[...], dup_scr[...])
        dwg_acc[...] += _mm_tn(h_ref[...], dgp_scr[...])

        @pl.when(i == nblk - 1)
        def _():
            rows = pl.ds(pl.multiple_of(pl.program_id(0) * fc, 16), fc)
            for acc, out, flip in ((dwd_acc, dwd_hbm, False), (dwu_acc, dwu_hbm, True), (dwg_acc, dwg_hbm, True)):
                stage[...] = (acc[...].T if flip else acc[...]).astype(BF16)
                copy = pltpu.make_async_copy(stage, out.at[rows, :], stage_sem)
                copy.start()
                copy.wait()

    def rev(i):
        return nblk - 1 - i

    one = pl.Buffered(1)
    in_specs = [
        pl.BlockSpec((tm, D_MODEL), lambda j, i: (rev(i), 0)),
        pl.BlockSpec((tm, D_MODEL), lambda j, i: (rev(i), 0)),
        pl.BlockSpec((tm, fc), lambda j, i: (rev(i), j)),
        pl.BlockSpec((16, fc), lambda j, i: (jnp.maximum(rev(i) * (tm // 16) - 1, 0), j)),
        pl.BlockSpec((tm, fc), lambda j, i: (rev(i), j)),
        pl.BlockSpec((fc, D_MODEL), lambda j, i: (j, 0), pipeline_mode=one),
        pl.BlockSpec((fc, D_MODEL), lambda j, i: (j, 0), pipeline_mode=one),
        pl.BlockSpec((fc, D_MODEL), lambda j, i: (j, 0), pipeline_mode=one),
        pl.BlockSpec((3, fc), lambda j, i: (0, j)),
        pl.BlockSpec((1, fc), lambda j, i: (0, j)),
    ]
    out_specs = [
        pl.BlockSpec((None, tm, D_MODEL), lambda j, i: (j, rev(i), 0)),
        ANY, ANY, ANY,
        pl.BlockSpec((3, fc), lambda j, i: (0, j)),
        pl.BlockSpec((1, fc), lambda j, i: (0, j)),
    ]
    return pl.pallas_call(
        body, name="ffn_bwd", grid=(FF_CHUNKS, nblk), in_specs=in_specs, out_specs=out_specs,
        out_shape=[jax.ShapeDtypeStruct((FF_CHUNKS, t, D_MODEL), BF16), jax.ShapeDtypeStruct((D_FF, D_MODEL), BF16),
                   jax.ShapeDtypeStruct((D_FF, D_MODEL), BF16), jax.ShapeDtypeStruct((D_FF, D_MODEL), BF16),
                   jax.ShapeDtypeStruct((3, D_FF), F32), jax.ShapeDtypeStruct((1, D_FF), F32)],
        scratch_shapes=[pltpu.VMEM((8, fc), F32), pltpu.VMEM((tm, fc), BF16), pltpu.VMEM((tm, fc), BF16),
                        pltpu.VMEM((tm, fc), BF16), pltpu.VMEM((fc, D_MODEL), F32), pltpu.VMEM((D_MODEL, fc), F32),
                        pltpu.VMEM((D_MODEL, fc), F32), pltpu.VMEM((fc, D_MODEL), BF16), pltpu.SemaphoreType.DMA],
        compiler_params=_cparams("arbitrary", "arbitrary", vmem=V7X_VMEM_LIMIT_LARGE),
    )(dx2, h2, gp, gp, up, w_gate, w_up, w_down, fcw, fcb)


def _outproj_bwd(dh2, dx2, x1, g_ffn, w_out, yc, ya, goc, goa, zconv, conv_w, conv_b, bd, tm):
    t = x1.shape[0]
    nblk = t // tm

    def body(dh_ref, dx2_ref, x1_ref, g_ref, w_ref, yc_ref, ya_ref, goc_ref, goa_ref, zc_ref, zch_ref, cw_ref, cb_ref,
             bd_ref, dx1_ref, dya_ref, dd_ref, dzc_ref, dwb_ref, dg_ref, dgoc_ref, dgoa_ref, dcw_ref, dcb_ref,
             carry_ref, dw_ref):
        i = pl.program_id(0)

        @pl.when(i == 0)
        def _():
            carry_ref[...] = jnp.zeros_like(carry_ref)
            for ref in (dw_ref, dg_ref, dgoc_ref, dgoa_ref, dcw_ref, dcb_ref):
                ref[...] = jnp.zeros_like(ref)

        keep = (i < nblk - 1).astype(F32)
        dh2_v = dh_ref[0].astype(F32)
        for j in range(1, FF_CHUNKS):
            dh2_v = dh2_v + dh_ref[j].astype(F32)
        r, xhat = _rms_stats(x1_ref[...])
        dg_ref[...] += jnp.sum(dh2_v * xhat, axis=0, keepdims=True)
        dx1 = dx2_ref[...] + _rms_bwd(dh2_v, xhat, r, g_ref[...])
        dx1_ref[...] = dx1
        dx1b = dx1.astype(BF16)
        dy = _mm_nt(dx1b, w_ref[...])

        yc_v = yc_ref[...].astype(F32)
        rc, ychat = _rms_stats(yc_v)
        dw_ref[0:CONV_W, :] += _mm_tn((ychat * goc_ref[...]).astype(BF16), dx1b)
        dyc = dy[:, 0:CONV_W]
        dgoc_ref[...] += jnp.sum(dyc * ychat, axis=0, keepdims=True)
        d_yc = _rms_bwd(dyc, ychat, rc, goc_ref[...])

        ya_v = ya_ref[...].astype(F32)
        ra, yahat = _rms_stats(ya_v)
        dw_ref[CONV_W:, :] += _mm_tn((yahat * goa_ref[...]).astype(BF16), dx1b)
        dya = dy[:, CONV_W:]
        dgoa_ref[...] += jnp.sum(dya * yahat, axis=0, keepdims=True)
        d_ya = _rms_bwd(dya, yahat, ra, goa_ref[...])
        dya_ref[...] = d_ya
        dd_ref[...] = _seg_sum64(d_ya * ya_v, bd_ref)

        zb = zc_ref[:, 0:CONV_W].astype(F32)
        zc = zc_ref[:, CONV_W:2 * CONV_W].astype(F32)
        zx = zc_ref[:, 2 * CONV_W:3 * CONV_W].astype(F32)
        u = zc * zx
        uh = (zch_ref[:, CONV_W:2 * CONV_W].astype(F32) * zch_ref[:, 2 * CONV_W:3 * CONV_W].astype(F32)) * keep
        cv, u1, u2 = _conv_fwd(u, uh[15:16, :], uh[14:15, :], cw_ref, cb_ref)
        d_cv = d_yc * zb
        d_u = _conv_bwd_input(d_cv, carry_ref[0:1, :], carry_ref[1:2, :], cw_ref)
        carry_ref[...] = d_cv[0:8, :]
        dcw_ref[0:1, :] += jnp.sum(d_cv * u2, axis=0, keepdims=True)
        dcw_ref[1:2, :] += jnp.sum(d_cv * u1, axis=0, keepdims=True)
        dcw_ref[2:3, :] += jnp.sum(d_cv * u, axis=0, keepdims=True)
        dcb_ref[...] += jnp.sum(d_cv, axis=0, keepdims=True)
        dzc_ref[:, 0:CONV_W] = (d_yc * cv).astype(BF16)
        dzc_ref[:, CONV_W:2 * CONV_W] = (d_u * zx).astype(BF16)
        dzc_ref[:, 2 * CONV_W:3 * CONV_W] = (d_u * zc).astype(BF16)

        @pl.when(i == nblk - 1)
        def _():
            dwb_ref[...] = dw_ref[...].astype(BF16)

    def rev(i):
        return nblk - 1 - i

    def blk(c):
        return pl.BlockSpec((tm, c), lambda i: (rev(i), 0))

    in_specs = [
        pl.BlockSpec((FF_CHUNKS, tm, D_MODEL), lambda i: (0, rev(i), 0)),
        blk(D_MODEL), blk(D_MODEL), _full((1, D_MODEL)),
        pl.BlockSpec((D_MODEL, D_MODEL), lambda i: (0, 0), pipeline_mode=pl.Buffered(1)),
        blk(CONV_W), blk(ATTN_W), _full((1, CONV_W)), _full((1, ATTN_W)),
        blk(3 * CONV_W),
        pl.BlockSpec((16, 3 * CONV_W), lambda i: (jnp.maximum(rev(i) * (tm // 16) - 1, 0), 0)),
        _full((3, CONV_W)), _full((1, CONV_W)), _full((256, 256)),
    ]
    out_specs = [blk(D_MODEL), blk(ATTN_W), blk(ATTN_W), blk(3 * CONV_W), _full((D_MODEL, D_MODEL)),
                 _full((1, D_MODEL)), _full((1, CONV_W)), _full((1, ATTN_W)), _full((3, CONV_W)), _full((1, CONV_W))]
    return pl.pallas_call(
        body, name="outproj_bwd", grid=(nblk,), in_specs=in_specs, out_specs=out_specs,
        out_shape=[jax.ShapeDtypeStruct((t, D_MODEL), F32), jax.ShapeDtypeStruct((t, ATTN_W), F32),
                   jax.ShapeDtypeStruct((t, ATTN_W), F32), jax.ShapeDtypeStruct((t, 3 * CONV_W), BF16),
                   jax.ShapeDtypeStruct((D_MODEL, D_MODEL), BF16), jax.ShapeDtypeStruct((1, D_MODEL), F32),
                   jax.ShapeDtypeStruct((1, CONV_W), F32), jax.ShapeDtypeStruct((1, ATTN_W), F32),
                   jax.ShapeDtypeStruct((3, CONV_W), F32), jax.ShapeDtypeStruct((1, CONV_W), F32)],
        scratch_shapes=[pltpu.VMEM((8, CONV_W), F32), pltpu.VMEM((D_MODEL, D_MODEL), F32)],
        compiler_params=_cparams("arbitrary", vmem=V7X_VMEM_LIMIT_LARGE),
    )(dh2, dx2, x1, g_ffn, w_out, yc, ya, goc, goa, zconv, zconv, conv_w, conv_b, bd)


def _attn_bwd(q, k, v, dya, lse, dd, e_all, m_all, after):
    t = q.shape[0]
    nsb = t // SUPER

    def body(q_ref, kc_ref, kp_ref, vc_ref, vp_ref, dy_ref, l_ref, d_ref, e_ref, m_ref, after_ref,
             dq_ref, dk_ref, dv_ref, kk, vv, dkacc, dvacc, dwide):
        s = pl.program_id(1)

        @pl.when(s == 0)
        def _():
            dkacc[...] = jnp.zeros_like(dkacc)
            dvacc[...] = jnp.zeros_like(dvacc)

        dkacc[0:SUPER, :] = dkacc[SUPER:, :]
        dvacc[0:SUPER, :] = dvacc[SUPER:, :]
        dkacc[SUPER:, :] = jnp.zeros((SUPER, QK_BLOCK), F32)
        dvacc[SUPER:, :] = jnp.zeros((SUPER, QK_BLOCK), F32)

        @pl.when(s < nsb)
        def _():
            kk[0:SUPER, :] = kp_ref[...]
            kk[SUPER:, :] = kc_ref[...]
            vv[0:SUPER, :] = vp_ref[...]
            vv[SUPER:, :] = vc_ref[...]
            head0 = lax.broadcasted_iota(jnp.int32, (QK_BLOCK, QK_BLOCK), 1) < HEAD_DIM

            def widened(a):
                other = pltpu.roll(a, HEAD_DIM, 1)
                first = lax.broadcasted_iota(jnp.int32, a.shape, 1) < HEAD_DIM
                return jnp.where(first, a, other), jnp.where(first, other, a)

            def stacked(h0, h1):
                return jnp.concatenate([jnp.concatenate([h0, h0], axis=1), jnp.concatenate([h1, h1], axis=1)], axis=0)

            def widen_dd(i, carry):
                rows = pl.ds(pl.multiple_of(i * 256, 256), 256)
                dwide[0, rows, :], dwide[1, rows, :] = widened(d_ref[rows, :])
                return carry

            lax.fori_loop(0, SUPER // 256, widen_dd, 0)

            for b, dil in enumerate(DILATIONS):
                def unit(u, carry, b=b, dil=dil):
                    start = _unit_start(u, dil)
                    first_key = SUPER + start - QK_BLOCK * dil
                    qrows = _rows(start, QK_BLOCK, dil)
                    krows = _rows(first_key, KEYS, dil)
                    q2 = _stack_heads(q_ref[qrows, :].astype(BF16), head0)
                    dy2 = _stack_heads(dy_ref[qrows, :].astype(BF16), head0)
                    g2 = stacked(*widened(jnp.exp(m_ref[b, qrows, :] - l_ref[qrows, :])))
                    d2 = stacked(dwide[0, qrows, :], dwide[1, qrows, :])
                    k2 = kk[krows, :].astype(BF16)
                    v2 = vv[krows, :].astype(BF16)
                    prob = e_ref[b * UNITS + u].astype(F32) * g2
                    ds = (prob * (_mm_nt(dy2, v2) - d2)).astype(BF16)
                    dvacc[krows, :] += _mm_tn(prob.astype(BF16), dy2)
                    dkacc[krows, :] += _mm_tn(ds, q2)
                    dq2 = _mm(ds, k2)
                    dq = jnp.where(head0, dq2[0:QK_BLOCK], dq2[QK_BLOCK:]) * ATTN_SCALE
                    if b == 0:
                        dq_ref[qrows, :] = dq
                    else:
                        dq_ref[qrows, :] += dq
                    return carry

                lax.fori_loop(0, UNITS, unit, 0, unroll=8)

        dk_ref[...] = dkacc[0:SUPER, :]
        dv_ref[...] = dvacc[0:SUPER, :].astype(BF16)

    def cur_map(p, s):
        return (jnp.minimum(s, nsb - 1), p)

    def prev_map(p, s):
        return (jnp.clip(s - 1, 0, nsb - 1), p)

    cur = pl.BlockSpec((SUPER, QK_BLOCK), cur_map)
    prev = pl.BlockSpec((SUPER, QK_BLOCK), prev_map)
    return pl.pallas_call(
        body, name="attn_bwd", grid=(4, nsb + 1),
        in_specs=[cur, cur, prev, cur, prev, cur, cur, cur,
                  pl.BlockSpec((None, None, 3 * UNITS, KEYS, KEYS), lambda p, s: (p, jnp.minimum(s, nsb - 1), 0, 0, 0)),
                  pl.BlockSpec((3, SUPER, QK_BLOCK), lambda p, s: (0, jnp.minimum(s, nsb - 1), p)),
                  pl.BlockSpec(memory_space=pl.ANY)],
        out_specs=[cur, prev, prev],
        out_shape=[jax.ShapeDtypeStruct((t, ATTN_W), F32), jax.ShapeDtypeStruct((t, ATTN_W), F32),
                   jax.ShapeDtypeStruct((t, ATTN_W), BF16)],
        scratch_shapes=[pltpu.VMEM((2 * SUPER, QK_BLOCK), F32)] * 4 + [pltpu.VMEM((2, SUPER, QK_BLOCK), F32)],
        compiler_params=_cparams("parallel", "arbitrary"),
    )(q, k, k, v, v, dya, lse, dd, e_all, m_all, after)


def _inproj_bwd(dq, dk, dv, dzconv, zqk, x, dx1, g_mix, w_in, qg, kg, bd, tm):
    t = x.shape[0]
    nblk = t // tm
    shard = IN_COLS // N_DEV

    def body(dq_ref, dk_ref, dv_ref, dzc_ref, zqk_ref, x_ref, dx1_ref, g_ref, w_ref, qg_ref,
             kg_ref, bd_ref, dx_ref, dw_hbm, dg_ref, dqg_ref, dkg_ref, dw_ref, stage, stage_sem):
        @pl.when(pl.program_id(0) == 0)
        def _():
            for ref in (dw_ref, dg_ref, dqg_ref, dkg_ref):
                ref[...] = jnp.zeros_like(ref)

        parts = [dzc_ref[...]]
        for j, (dn_ref, gain_ref, dgain_ref) in enumerate(((dq_ref, qg_ref, dqg_ref), (dk_ref, kg_ref, dkg_ref))):
            dn = dn_ref[...]
            z = zqk_ref[:, j * ATTN_W:(j + 1) * ATTN_W].astype(F32)
            r = lax.rsqrt(_seg_sum64(z * z, bd_ref) * (1.0 / HEAD_DIM) + EPS)
            zhat = z * r
            dgain_ref[...] += jnp.sum(dn * zhat, axis=0, keepdims=True)
            gd = dn * gain_ref[...]
            parts.append((r * (gd - zhat * (_seg_sum64(gd * zhat, bd_ref) * (1.0 / HEAD_DIM)))).astype(BF16))
        parts.append(dv_ref[...].astype(BF16))
        dz = jnp.concatenate(parts, axis=1)

        r, xhat = _rms_stats(x_ref[...])
        g = g_ref[...]
        dw_ref[...] += _mm_tn((xhat * g).astype(BF16), dz)
        dh = _mm_nt(dz, w_ref[...])
        dg_ref[...] += jnp.sum(dh * xhat, axis=0, keepdims=True)
        dx_ref[...] = dx1_ref[...] + _rms_bwd(dh, xhat, r, g)

        @pl.when(pl.program_id(0) == nblk - 1)
        def _():
            for k in range(N_DEV):
                stage[...] = dw_ref[:, k * shard:(k + 1) * shard].astype(BF16)
                copy = pltpu.make_async_copy(stage, dw_hbm.at[k], stage_sem)
                copy.start()
                copy.wait()

    def blk(c):
        return pl.BlockSpec((tm, c), lambda i: (i, 0))

    return pl.pallas_call(
        body, name="inproj_bwd", grid=(nblk,),
        in_specs=[blk(ATTN_W)] * 3 + [blk(3 * CONV_W), blk(2 * ATTN_W), blk(D_MODEL), blk(D_MODEL), _full((1, D_MODEL)),
                                      _full((D_MODEL, IN_COLS)), _full((1, ATTN_W)), _full((1, ATTN_W)),
                                      _full((256, 256))],
        out_specs=[blk(D_MODEL), ANY, _full((1, D_MODEL)), _full((1, ATTN_W)), _full((1, ATTN_W))],
        out_shape=[jax.ShapeDtypeStruct((t, D_MODEL), F32), jax.ShapeDtypeStruct((N_DEV, D_MODEL, shard), BF16),
                   jax.ShapeDtypeStruct((1, D_MODEL), F32), jax.ShapeDtypeStruct((1, ATTN_W), F32),
                   jax.ShapeDtypeStruct((1, ATTN_W), F32)],
        scratch_shapes=[pltpu.VMEM((D_MODEL, IN_COLS), F32), pltpu.VMEM((D_MODEL, shard), BF16),
                        pltpu.SemaphoreType.DMA],
        compiler_params=_cparams("arbitrary"),
    )(dq, dk, dv, dzconv, zqk, x, dx1, g_mix, w_in, qg, kg, bd)


def _ordered_after(a, token):
    return a if token is None else a + token


def _local_step(x, p, target, w, tms, hooks=None):
    hooks = hooks or {}
    bd = jnp.asarray(np.kron(np.eye(4, dtype=np.float32), np.ones((HEAD_DIM, HEAD_DIM), np.float32)), BF16)
    qg = jnp.tile(w["q_norm_g"], (1, 8))
    kg = jnp.tile(w["k_norm_g"], (1, 8))
    slopes = np.exp2(-np.arange(1, 9, dtype=np.float32))
    slopes = jnp.asarray(np.broadcast_to(slopes.reshape(4, 2, 1), (4, 2, QK_BLOCK)))

    zconv, zqk, yc, q, k, v = _inproj_fwd(x, w["g_mix"], w["w_in"], w["conv_w"], w["conv_b"], qg, kg, bd, tms[0])
    ya, lse, e_all, m_all = _attn_fwd(q, k, v, slopes)
    if "late_weights" in hooks:
        w = {**w, **hooks["late_weights"](lse)}
    x1 = _outproj_fwd(ya, yc, x, w["g_out_conv"], w["g_out_attn"], w["w_out"], tms[1])
    gp, up, h2, x2 = _ffn_fwd(x1, w["g_ffn"], w["w_gate"], w["w_up"], w["w_down"], w["ffn_conv_w"], w["ffn_conv_b"],
                              tms[0])
    dx2, dx2b, loss, dw_pg, dw_pp, dg_ple = _ple_fwd_bwd(x2, p, target, w["g_ple"], w["w_ple_gate"], w["w_ple_proj"], tms[0])
    dh2, dw_down, dw_up, dw_gate, dfcw, dfcb = _ffn_bwd(dx2b, h2, gp, up, w["w_gate"], w["w_up"], w["w_down"],
                                                        w["ffn_conv_w"], w["ffn_conv_b"], tms[0])
    token = None
    if "ffn_grads" in hooks:
        token = hooks["ffn_grads"]({"w_ple_gate": dw_pg, "w_ple_proj": dw_pp, "w_down": dw_down, "w_up": dw_up,
                                    "w_gate": dw_gate, "ffn_conv_b": dfcb})
    dx1, dya, dd, dzconv, dw_out, dg_ffn, dgoc, dgoa, dcw, dcb = _outproj_bwd(
        dh2, dx2, x1, _ordered_after(w["g_ffn"], token), w["w_out"], yc, ya, w["g_out_conv"], w["g_out_attn"], zconv,
        w["conv_w"], w["conv_b"], bd, tms[0])
    token = hooks["outproj_done"](dx1) if "outproj_done" in hooks else None
    dq, dk, dv = _attn_bwd(q, k, v, dya, lse, dd, e_all, m_all, slopes if token is None else token)
    dx, dw_in, dg_mix, dqg, dkg = _inproj_bwd(dq, dk, dv, dzconv, zqk, x, dx1, w["g_mix"], w["w_in"], qg, kg, bd,
                                              tms[0])
    grads = {
        "g_mix": dg_mix, "w_in": dw_in, "conv_w": dcw, "conv_b": dcb,
        "q_norm_g": dqg, "k_norm_g": dkg,
        "g_out_conv": dgoc, "g_out_attn": dgoa, "w_out": dw_out, "g_ffn": dg_ffn, "w_gate": dw_gate, "w_up": dw_up,
        "ffn_conv_w": dfcw, "ffn_conv_b": dfcb, "w_down": dw_down, "g_ple": dg_ple, "w_ple_gate": dw_pg,
        "w_ple_proj": dw_pp,
    }
    return loss, dx, grads


ANY = pl.BlockSpec(memory_space=pl.ANY)
MESH = pl.DeviceIdType.MESH


def _all_gather(shards, name):
    n = len(shards)

    def body(*refs):
        ins, outs = refs[:n], refs[n:2 * n]
        send_sems, recv_sems, local_sems = refs[2 * n:]
        x, y, c = lax.axis_index("x"), lax.axis_index("y"), lax.axis_index("c")
        me, sibling = (x, y, c), (x, y, 1 - c)
        chips = [(1 - x, y), (x, 1 - y), (1 - x, 1 - y)]

        def slot(dev):
            return 4 * dev[0] + 2 * dev[1] + dev[2]

        def copy(b, k, block, to, src=None):
            dst = outs[b].at[slot(block)]
            return pltpu.make_async_remote_copy(
                src_ref=dst if src is None else src, dst_ref=dst, send_sem=send_sems.at[b, k],
                recv_sem=recv_sems.at[b, k], device_id=to, device_id_type=MESH)

        mine = [pltpu.make_async_copy(ins[b], outs[b].at[slot(me)], local_sems.at[b]) for b in range(n)]
        first, passed = [], []
        for b in range(n):
            mine[b].start()
            first.append(copy(b, 0, me, sibling, src=ins[b]))
            first += [copy(b, 1 + j, me, (*chip, c), src=ins[b]) for j, chip in enumerate(chips)]
        for cp in first:
            cp.start()
        for j, chip in enumerate(chips):
            for b in range(n):
                copy(b, 1 + j, (*chip, c), me).wait_recv()
                fwd = copy(b, 4 + j, (*chip, c), sibling)
                fwd.start()
                passed.append(fwd)
        for b in range(n):
            copy(b, 0, sibling, me).wait_recv()
            for j, chip in enumerate(chips):
                copy(b, 4 + j, (*chip, 1 - c), me).wait_recv()
        for cp in first + passed:
            cp.wait_send()
        for cp in mine:
            cp.wait()

    return pl.pallas_call(
        body, name=name,
        in_specs=[ANY] * n, out_specs=[ANY] * n,
        out_shape=[jax.ShapeDtypeStruct((N_DEV,) + s.shape, s.dtype) for s in shards],
        scratch_shapes=[pltpu.SemaphoreType.DMA((n, 7)), pltpu.SemaphoreType.DMA((n, 7)),
                        pltpu.SemaphoreType.DMA((n,))],
    )(*shards)


HBM = pl.BlockSpec(memory_space=pltpu.HBM)
SEM = pl.BlockSpec(memory_space=pltpu.SEMAPHORE)
EFFECT = pltpu.SideEffectType.DATAFLOW_SIDE_EFFECTING
FLIPS = ((0, 0, 1), (0, 1, 0), (0, 1, 1), (1, 0, 0), (1, 0, 1), (1, 1, 0), (1, 1, 1))


def _flip_peers():
    pos = (lax.axis_index("x"), lax.axis_index("y"), lax.axis_index("c"))
    return [tuple(1 - a if f else a for a, f in zip(pos, flip)) for flip in FLIPS]


def _hbm(a):
    return pltpu.with_memory_space_constraint(a, pltpu.HBM)


def _own_copies(own, src_refs, land_refs, send_sems, n_remote):
    return [pltpu.make_async_copy(src, dst, send_sems.at[n_remote + i])
            for i, (src, dst) in enumerate(own(src_refs, land_refs) if own else [])]


def _split_start(name, srcs, lands, plan, n_copies, after, own=None):
    n, m = len(srcs), len(lands)

    def body(*refs):
        send_sems, recv_sems, token = refs[n + m + 1], refs[n + m + 2], refs[-1]
        for i, (src, dst, peer) in enumerate(plan(refs[:n], refs[n:n + m])):
            pltpu.make_async_remote_copy(src_ref=src, dst_ref=dst, send_sem=send_sems.at[i], recv_sem=recv_sems.at[i],
                                         device_id=peer, device_id_type=MESH).start()
        for copy in _own_copies(own, refs[:n], refs[n:n + m], send_sems, n_copies):
            copy.start()
        token[...] = jnp.zeros_like(token)

    outs = pl.pallas_call(
        body, name=name + "_start",
        in_specs=[HBM] * (n + m) + [ANY],
        out_specs=[SEM, SEM] + [HBM] * (n + m) + [pl.BlockSpec(memory_space=pltpu.VMEM)],
        out_shape=[pltpu.SemaphoreType.DMA((n_copies + (n if own else 0),)), pltpu.SemaphoreType.DMA((n_copies,))]
        + [pltpu.HBM(a.shape, a.dtype) for a in list(srcs) + list(lands)] + [jax.ShapeDtypeStruct((1, D_MODEL), F32)],
        input_output_aliases={i: 2 + i for i in range(n + m)},
        compiler_params=pltpu.CompilerParams(has_side_effects=EFFECT),
    )(*[_hbm(a) for a in list(srcs) + list(lands)], after)
    return (outs[0], outs[1], outs[2:2 + n], outs[2 + n:2 + n + m]), outs[-1]


def _split_wait(name, started, plan, after, own=None):
    send_sems, recv_sems, srcs, lands = started
    n, m = len(srcs), len(lands)

    def body(*refs):
        send_ref, recv_ref = refs[n + m], refs[n + m + 1]
        copies = plan(refs[:n], refs[n:n + m])
        for i, (src, dst, peer) in enumerate(copies):
            copy = pltpu.make_async_remote_copy(src_ref=src, dst_ref=dst, send_sem=send_ref.at[i],
                                                recv_sem=recv_ref.at[i], device_id=peer, device_id_type=MESH)
            copy.wait_send()
            copy.wait_recv()
        for copy in _own_copies(own, refs[:n], refs[n:n + m], send_ref, len(copies)):
            copy.wait()

    outs = pl.pallas_call(
        body, name=name + "_wait",
        in_specs=[HBM] * (n + m) + [SEM, SEM, ANY],
        out_specs=[HBM] * (n + m),
        out_shape=[pltpu.HBM(a.shape, a.dtype) for a in list(srcs) + list(lands)],
        input_output_aliases={i: i for i in range(n + m)},
        compiler_params=pltpu.CompilerParams(has_side_effects=EFFECT),
    )(*srcs, *lands, send_sems, recv_sems, after)
    return outs[:n], outs[n:]


def _gather_plan(srcs, lands):
    slot = 4 * lax.axis_index("x") + 2 * lax.axis_index("y") + lax.axis_index("c")
    return [(src, land.at[slot], peer) for src, land in zip(srcs, lands) for peer in _flip_peers()]


def _own_slot(srcs, lands):
    slot = 4 * lax.axis_index("x") + 2 * lax.axis_index("y") + lax.axis_index("c")
    return [(src, land.at[slot]) for src, land in zip(srcs, lands)]


def _sibling_plan(srcs, lands):
    x, y, c = lax.axis_index("x"), lax.axis_index("y"), lax.axis_index("c")
    return [(src.at[k, 1 - c], land.at[k], (x, y, 1 - c)) for src, land in zip(srcs, lands) for k in range(N_CHIP)]


def _chip_plan(srcs, lands):
    x, y, c = lax.axis_index("x"), lax.axis_index("y"), lax.axis_index("c")
    return [(src.at[2 * cx + cy], land.at[2 * x + y], (cx, cy, c))
            for src, land in zip(srcs, lands) for cx, cy in ((1 - x, y), (x, 1 - y), (1 - x, 1 - y))]


def _row_tile(rows):
    for tr in range(min(rows, 512), 15, -16):
        if rows % tr == 0:
            return tr
    return rows


def _pair_sums(gs, lands, core, name):
    n = len(gs)

    def body(c_ref, *refs):
        for b in range(n):
            out = refs[2 * n + b]
            out[...] = (refs[b][...].astype(F32) + refs[n + b][...].astype(F32)).astype(out.dtype)

    def slab(a):
        return pl.BlockSpec((None,) + a.shape[1:], lambda k, c_ref: (k, 0, 0))

    return pl.pallas_call(
        body, name=name,
        grid_spec=pltpu.PrefetchScalarGridSpec(
            num_scalar_prefetch=1, grid=(N_CHIP,),
            in_specs=[pl.BlockSpec((None, None) + g.shape[2:], lambda k, c_ref: (k, c_ref[0], 0, 0)) for g in gs]
            + [slab(a) for a in lands],
            out_specs=[slab(a) for a in lands]),
        out_shape=[jax.ShapeDtypeStruct(a.shape, a.dtype) for a in lands],
        compiler_params=_cparams("parallel"),
    )(core, *gs, *lands)


def _adamw(own, arrived, chip, w, m, v, name):
    k, rows, cols = arrived.shape
    tr = _row_tile(rows)
    c1 = 1.0 / (1.0 - ADAM_B1 ** ADAM_STEP)
    c2 = 1.0 / (1.0 - ADAM_B2 ** ADAM_STEP)

    def body(chip_ref, o_ref, p_ref, w_ref, m_ref, v_ref, g_ref, d_ref, nm_ref, nv_ref):
        def slab(j):
            return jnp.where(chip_ref[0] == j, o_ref[j], p_ref[j]).astype(F32)

        g = slab(0)
        for j in range(1, k):
            g = g + slab(j)
        g_ref[...] = g
        nm = ADAM_B1 * m_ref[...] + (1.0 - ADAM_B1) * g
        nv = ADAM_B2 * v_ref[...] + (1.0 - ADAM_B2) * (g * g)
        nm_ref[...] = nm
        nv_ref[...] = nv
        d_ref[...] = -ADAM_LR * ((nm * c1) / (jnp.sqrt(nv * c2) + ADAM_EPS) + ADAM_WD * w_ref[...])

    blk = pl.BlockSpec((tr, cols), lambda i, c: (i, 0))
    stack = pl.BlockSpec((k, tr, cols), lambda i, c: (0, i, 0))
    return pl.pallas_call(
        body, name=name,
        grid_spec=pltpu.PrefetchScalarGridSpec(num_scalar_prefetch=1, grid=(rows // tr,),
                                               in_specs=[stack, stack, blk, blk, blk], out_specs=[blk] * 4),
        out_shape=[jax.ShapeDtypeStruct((rows, cols), F32)] * 4,
        compiler_params=_cparams("parallel"),
    )(chip, own, arrived, w, m, v)


SMALL_LAYOUT = (("g_mix", 0, 1024), ("conv_b", 1, 512), ("q_norm_g", 2, 64), ("k_norm_g", 3, 64),
                ("g_out_conv", 4, 512), ("g_out_attn", 5, 512), ("g_ffn", 6, 1024), ("ffn_conv_b", 7, 2816),
                ("g_ple", 10, 1024))
CONV_W_ROW = 11
FFN_CONV_W_ROW = 14
LOSS_ROW = 23


def _row_pieces(cols):
    return [(c, min(1024, cols - c)) for c in range(0, cols, 1024)]


def _pack_small(grads, loss_tile):
    names = [n for n, _, _ in SMALL_LAYOUT]

    def body(*refs):
        ins, cw_ref, fcw_ref, loss_ref, out_ref = refs[:len(names)], refs[-4], refs[-3], refs[-2], refs[-1]
        out_ref[...] = jnp.zeros_like(out_ref)
        for ref, (_, row, cols) in zip(ins, SMALL_LAYOUT):
            if ref.shape[1] == ATTN_W and cols == HEAD_DIM:
                out_ref[row:row + 1, 0:cols] = sum(ref[:, h:h + cols] for h in range(0, ATTN_W, cols))
                continue
            for j, (c, width) in enumerate(_row_pieces(cols)):
                out_ref[row + j:row + j + 1, 0:width] = ref[:, c:c + width]
        for k in range(3):
            out_ref[CONV_W_ROW + k:CONV_W_ROW + k + 1, 0:CONV_W] = cw_ref[k:k + 1, :]
            for j, (c, width) in enumerate(_row_pieces(D_FF)):
                row = FFN_CONV_W_ROW + 3 * k + j
                out_ref[row:row + 1, 0:width] = fcw_ref[k:k + 1, c:c + width]
        out_ref[LOSS_ROW:LOSS_ROW + 1, 0:128] = loss_ref[0:1, :]

    return pl.pallas_call(
        body, name="pack_small_grads", out_shape=jax.ShapeDtypeStruct((SMALL_ROWS, 1024), F32),
    )(*[grads[n] for n in names], grads["conv_w"], grads["ffn_conv_w"], loss_tile)


def _adamw_small(arrived, conv_parts, fconv_parts, wts, mom, var):
    names = [n for n, _, _ in SMALL_LAYOUT] + ["conv_w", "ffn_conv_w"]
    c1 = 1.0 / (1.0 - ADAM_B1 ** ADAM_STEP)
    c2 = 1.0 / (1.0 - ADAM_B2 ** ADAM_STEP)
    n = len(names)

    def body(*refs):
        land, cw_ref, fcw_ref = refs[0], refs[1], refs[2]
        state = refs[3:3 + 3 * n]
        outs = refs[3 + 3 * n:]

        def total(piece):
            acc = piece(0)
            for d in range(1, N_DEV):
                acc = acc + piece(d)
            return acc

        for i, name in enumerate(names):
            if name == "conv_w":
                g = total(lambda d: cw_ref[d])
            elif name == "ffn_conv_w":
                g = total(lambda d: fcw_ref[d])
            else:
                _, row, cols = SMALL_LAYOUT[i]
                pieces = [total(lambda d, j=j, width=width: land[d, row + j:row + j + 1, 0:width])
                          for j, (_, width) in enumerate(_row_pieces(cols))]
                g = pieces[0] if len(pieces) == 1 else jnp.concatenate(pieces, axis=1)
            w_ref, m_ref, v_ref = state[3 * i:3 * i + 3]
            nm = ADAM_B1 * m_ref[...] + (1.0 - ADAM_B1) * g
            nv = ADAM_B2 * v_ref[...] + (1.0 - ADAM_B2) * (g * g)
            outs[4 * i][...] = g
            outs[4 * i + 1][...] = -ADAM_LR * ((nm * c1) / (jnp.sqrt(nv * c2) + ADAM_EPS) + ADAM_WD * w_ref[...])
            outs[4 * i + 2][...] = nm
            outs[4 * i + 3][...] = nv
        outs[-1][...] = total(lambda d: land[d, LOSS_ROW:LOSS_ROW + 1, 0:128])

    state = [a[nm_] for nm_ in names for a in (wts, mom, var)]
    shapes = [jax.ShapeDtypeStruct(wts[nm_].shape, F32) for nm_ in names for _ in range(4)]
    outs = pl.pallas_call(
        body, name="adamw_small", out_shape=shapes + [jax.ShapeDtypeStruct((1, 128), F32)],
    )(arrived, conv_parts, fconv_parts, *state)
    return {nm_: tuple(outs[4 * i:4 * i + 4]) for i, nm_ in enumerate(names)}, outs[-1][0, 0]


COL_SHARDED = ("w_in", "w_ple_proj")
TRANSPOSED = ("w_gate", "w_up")
CONV_SHARDED = (("conv_w", CONV_W), ("ffn_conv_w", D_FF))


def _gathered_to_full(name, gathered):
    if name in COL_SHARDED:
        return gathered.transpose(1, 0, 2).reshape(gathered.shape[1], -1)
    return gathered.reshape(-1, gathered.shape[2])


def _full_to_stacked(name, grad, shard_shape):
    sr, sc = shard_shape
    if grad.ndim == 3:
        a = grad
    elif name in COL_SHARDED:
        a = grad.reshape(sr, N_DEV, sc).transpose(1, 0, 2)
    else:
        a = grad.reshape(N_DEV, sr, sc)
    return a.astype(BF16).reshape(N_CHIP, 2, sr, sc)


def _pad_rows(vec, rows):
    return jnp.pad(vec, (0, rows * 1024 - vec.shape[0])).reshape(rows, 1024)


def kernel(x, p, g_mix, w_in, conv_w, conv_b, q_norm_g, k_norm_g, g_out_conv, g_out_attn, w_out, g_ffn, w_gate, w_up, ffn_conv_w, ffn_conv_b, w_down, g_ple, w_ple_gate, w_ple_proj, loss_target, m_g_mix, m_w_in, m_conv_w, m_conv_b, m_q_norm_g, m_k_norm_g, m_g_out_conv, m_g_out_attn, m_w_out, m_g_ffn, m_w_gate, m_w_up, m_ffn_conv_w, m_ffn_conv_b, m_w_down, m_g_ple, m_w_ple_gate, m_w_ple_proj, v_g_mix, v_w_in, v_conv_w, v_conv_b, v_q_norm_g, v_k_norm_g, v_g_out_conv, v_g_out_attn, v_w_out, v_g_ffn, v_w_gate, v_w_up, v_ffn_conv_w, v_ffn_conv_b, v_w_down, v_g_ple, v_w_ple_gate, v_w_ple_proj):
    args = dict(locals())
    names = ["g_mix", "w_in", "conv_w", "conv_b", "q_norm_g", "k_norm_g", "g_out_conv", "g_out_attn", "w_out", "g_ffn",
             "w_gate", "w_up", "ffn_conv_w", "ffn_conv_b", "w_down", "g_ple", "w_ple_gate", "w_ple_proj"]
    big = list(BIG)
    conv = [n for n, _ in CONV_SHARDED]

    def local(prefix):
        out = {n: (args[prefix + n][0] if n in big or n in conv else args[prefix + n]) for n in names}
        out.update({n: out[n].T for n in TRANSPOSED})
        return out

    wts, mom, var = local(""), local("m_"), local("v_")
    shard_shapes = {n: wts[n].shape for n in big}
    dev = 4 * lax.axis_index("x") + 2 * lax.axis_index("y") + lax.axis_index("c")
    core = lax.axis_index("c").astype(jnp.int32).reshape(1)

    conv_local = _pad_rows(jnp.concatenate([wts[n].reshape(-1) for n in conv]), 8).reshape(8, 1024)
    late = [n for n in big if n != "w_in"]
    w_in_all, conv_all = _all_gather([wts["w_in"].astype(BF16), conv_local], "gather_weights")
    late_shards = [wts[n].astype(BF16) for n in late]
    gathering, token = _split_start("gather_late_weights", late_shards,
                                    [lax.empty((N_DEV,) + s.shape, BF16) for s in late_shards], _gather_plan,
                                    7 * len(late), w_in_all, own=_own_slot)
    full = dict(wts)
    full["w_in"] = _gathered_to_full("w_in", w_in_all)
    full["g_mix"] = _ordered_after(wts["g_mix"], token)
    flying = {}

    def late_weights(after):
        _, lands = _split_wait("gather_late_weights", gathering, _gather_plan, after, own=_own_slot)
        return {n: _gathered_to_full(n, land) for n, land in zip(late, lands)}

    early = ["w_ple_gate", "w_ple_proj", "w_down", "w_up", "w_gate"]

    def ffn_grads(g):
        stacked = [_full_to_stacked(n, g[n], shard_shapes[n]) for n in early]
        flying["sibling"], tok = _split_start("rs_sibling_early", stacked,
                                              [lax.empty((N_CHIP,) + s.shape[2:], BF16) for s in stacked],
                                              _sibling_plan, N_CHIP * len(early), g["ffn_conv_b"])
        return tok

    def outproj_done(after):
        stacked, landed = _split_wait("rs_sibling_early", flying["sibling"], _sibling_plan, after)
        parts = _pair_sums(stacked, landed, core, "rs_pair_sums_early")
        flying["chip"], tok = _split_start("rs_chip_early", parts, [lax.empty(q.shape, BF16) for q in parts],
                                           _chip_plan, 3 * len(early), landed[0])
        return tok

    off = 0
    for n, width in CONV_SHARDED:
        sc = width // N_DEV
        a = conv_all.reshape(N_DEV, -1)[:, off:off + 3 * sc].reshape(N_DEV, 3, sc)
        full[n] = a.transpose(1, 0, 2).reshape(3, width)
        off += 3 * sc

    loss, dx, grads = _local_step(x[0], p[0, 0], loss_target[0], full, (512, 1024),
                                  {"late_weights": late_weights, "ffn_grads": ffn_grads, "outproj_done": outproj_done})

    chip = (2 * lax.axis_index("x") + lax.axis_index("y")).astype(jnp.int32).reshape(1)

    def adamw_of(group, parts, arrived):
        return {n: _adamw(own, got, chip, wts[n], mom[n], var[n], f"adamw_{n}")
                for n, own, got in zip(group, parts, arrived)}

    last = [n for n in big if n not in early]
    stacked = [_full_to_stacked(n, grads[n], shard_shapes[n]) for n in last]
    flying["sibling_last"], tok = _split_start("rs_sibling_last", stacked,
                                               [lax.empty((N_CHIP,) + s.shape[2:], BF16) for s in stacked],
                                               _sibling_plan, N_CHIP * len(last), dx)
    packed = _pack_small(grads, loss)
    flying["small"], tok = _split_start("gather_small_grads", [packed], [lax.empty((N_DEV,) + packed.shape, F32)],
                                        _gather_plan, N_DEV - 1, tok, own=_own_slot)
    stacked, landed = _split_wait("rs_sibling_last", flying["sibling_last"], _sibling_plan, tok)
    parts = _pair_sums(stacked, landed, core, "rs_pair_sums_last")
    flying["chip_last"], tok = _split_start("rs_chip_last", parts, [lax.empty(q.shape, BF16) for q in parts],
                                            _chip_plan, 3 * len(last), landed[0])

    parts, arrived = _split_wait("rs_chip_early", flying["chip"], _chip_plan, tok)
    out = adamw_of(early, parts, arrived)
    _, (small_all,) = _split_wait("gather_small_grads", flying["small"], _gather_plan, out[early[-1]][0],
                                  own=_own_slot)
    taps = small_all[:, CONV_W_ROW:CONV_W_ROW + 3, 0:CONV_W]
    ftaps = small_all[:, FFN_CONV_W_ROW:FFN_CONV_W_ROW + 9, :].reshape(N_DEV, 3, 3 * 1024)
    small_out, loss_total = _adamw_small(
        small_all, lax.dynamic_slice(taps, (0, 0, dev * (CONV_W // N_DEV)), (N_DEV, 3, CONV_W // N_DEV)),
        lax.dynamic_slice(ftaps, (0, 0, dev * (D_FF // N_DEV)), (N_DEV, 3, D_FF // N_DEV)), wts, mom, var)
    out.update(small_out)
    parts, arrived = _split_wait("rs_chip_last", flying["chip_last"], _chip_plan, small_out["g_mix"][0])
    out.update(adamw_of(last, parts, arrived))
    def result(n, which):
        a = out[n][which]
        return (a.T if n in TRANSPOSED else a).reshape(args[n].shape)

    return (loss_total, dx[None], *[result(n, which) for which in range(4) for n in names])
```

```python
import jax
import jax.numpy as jnp
import numpy as np
from jax import lax
from jax.experimental import pallas as pl
from jax.experimental.pallas import tpu as pltpu

F32 = jnp.float32
BF16 = jnp.bfloat16

D_MODEL = 1024
CONV_W = 512
ATTN_W = 512
HEAD_DIM = 64
D_FF = 2816
PLE_DIM = 256
IN_COLS = 3 * CONV_W + 3 * ATTN_W
EPS = 1e-6
QK_BLOCK = 128
DILATIONS = (1, 4, 16)
ATTN_SCALE = HEAD_DIM ** -0.5

ADAM_LR = 0.001
ADAM_B1 = 0.9
ADAM_B2 = 0.999
ADAM_EPS = 1e-08
ADAM_WD = 0.01
ADAM_STEP = 10

N_DEV = 8
N_CHIP = 4
V7X_VMEM_LIMIT = 56 * 1024 * 1024
V7X_VMEM_LIMIT_LARGE = 62 * 1024 * 1024
FF_CHUNKS = 2
FFN_BWD_PARTS = 1

BIG = ("w_in", "w_out", "w_gate", "w_up", "w_down", "w_ple_gate", "w_ple_proj")
SMALL_ROWS = 24


def _cparams(*sem, vmem=V7X_VMEM_LIMIT):
    return pltpu.CompilerParams(dimension_semantics=sem, vmem_limit_bytes=vmem)


def _mm(a, b):
    return jnp.dot(a, b, preferred_element_type=F32)


def _mm_nt(a, b):
    return lax.dot_general(a, b, (((1,), (1,)), ((), ())), preferred_element_type=F32)


def _mm_tn(a, b):
    return lax.dot_general(a, b, (((0,), (0,)), ((), ())), preferred_element_type=F32)


def _full(shape):
    nd = len(shape)
    return pl.BlockSpec(shape, lambda *_: (0,) * nd)


def _rms_stats(x):
    r = lax.rsqrt(jnp.mean(x * x, axis=-1, keepdims=True) + EPS)
    return r, x * r


def _rms_bwd(dy, xhat, r, g):
    gd = dy * g
    return r * (gd - xhat * jnp.mean(gd * xhat, axis=-1, keepdims=True))


def _seg_sum64(v, bd_ref):
    outs = []
    for c in range(0, v.shape[1], 256):
        vc = v[:, c:c + 256]
        hi = vc.astype(BF16)
        lo = (vc - hi.astype(F32)).astype(BF16)
        outs.append(_mm(hi, bd_ref[...]) + _mm(lo, bd_ref[...]))
    return outs[0] if len(outs) == 1 else jnp.concatenate(outs, axis=1)


def _shift_rows(u, k, edge_rows):
    out = pltpu.roll(u, k, 0)
    row = lax.broadcasted_iota(jnp.int32, (8, u.shape[1]), 0)
    head = out[0:8]
    for j in range(k):
        head = jnp.where(row == j, edge_rows[k - 1 - j], head)
    return jnp.concatenate([head, out[8:]], axis=0)


def _shift_rows_up(u, k, edge_rows):
    n = u.shape[0]
    out = pltpu.roll(u, n - k, 0)
    row = lax.broadcasted_iota(jnp.int32, (8, u.shape[1]), 0)
    tail = out[n - 8:n]
    for j in range(k):
        tail = jnp.where(row == 8 - k + j, edge_rows[j], tail)
    return jnp.concatenate([out[0:n - 8], tail], axis=0)


def _conv_fwd(u, c1, c2, w_ref, b_ref):
    u1 = _shift_rows(u, 1, (c1,))
    u2 = _shift_rows(u, 2, (c1, c2))
    y = u2 * w_ref[0:1, :] + u1 * w_ref[1:2, :] + u * w_ref[2:3, :] + b_ref[...]
    return y, u1, u2


def _conv_bwd_input(dy, n1row, n2row, w_ref):
    d1 = _shift_rows_up(dy, 1, (n1row,))
    d2 = _shift_rows_up(dy, 2, (n1row, n2row))
    return dy * w_ref[2:3, :] + d1 * w_ref[1:2, :] + d2 * w_ref[0:1, :]


def _sigmoid(x):
    return 1.0 / (1.0 + jnp.exp(-x))


def _inproj_fwd(x, g_mix, w_in, conv_w, conv_b, qg, kg, bd, tm):
    t = x.shape[0]

    def body(x_ref, g_ref, w_ref, cw_ref, cb_ref, qg_ref, kg_ref, bd_ref,
             zc_ref, zqk_ref, yc_ref, q_ref, k_ref, v_ref, carry_ref):
        @pl.when(pl.program_id(0) == 0)
        def _():
            carry_ref[...] = jnp.zeros_like(carry_ref)

        _, xhat = _rms_stats(x_ref[...])
        h = (xhat * g_ref[...]).astype(BF16)
        zconv = _mm(h, w_ref[:, 0:3 * CONV_W])
        zc_ref[...] = zconv.astype(BF16)
        u = zconv[:, CONV_W:2 * CONV_W] * zconv[:, 2 * CONV_W:3 * CONV_W]
        cv, _, _ = _conv_fwd(u, carry_ref[7:8, :], carry_ref[6:7, :], cw_ref, cb_ref)
        yc_ref[...] = (zconv[:, 0:CONV_W] * cv).astype(BF16)
        carry_ref[...] = u[tm - 8:tm, :]

        zqk = _mm(h, w_ref[:, 3 * CONV_W:3 * CONV_W + 2 * ATTN_W])
        zqk_ref[...] = zqk.astype(BF16)
        for j, (gain_ref, out_ref, scale) in enumerate(((qg_ref, q_ref, ATTN_SCALE), (kg_ref, k_ref, 1.0))):
            z = zqk[:, j * ATTN_W:(j + 1) * ATTN_W]
            r = lax.rsqrt(_seg_sum64(z * z, bd_ref) * (1.0 / HEAD_DIM) + EPS)
            out_ref[...] = z * r * gain_ref[...] * scale
        v_ref[...] = _mm(h, w_ref[:, 3 * CONV_W + 2 * ATTN_W:IN_COLS])

    def blk(c):
        return pl.BlockSpec((tm, c), lambda i: (i, 0))

    return pl.pallas_call(
        body, name="inproj_fwd", grid=(t // tm,),
        in_specs=[blk(D_MODEL), _full((1, D_MODEL)), _full((D_MODEL, IN_COLS)), _full((3, CONV_W)),
                  _full((1, CONV_W)), _full((1, ATTN_W)), _full((1, ATTN_W)), _full((256, 256))],
        out_specs=[blk(3 * CONV_W), blk(2 * ATTN_W), blk(CONV_W), blk(ATTN_W), blk(ATTN_W), blk(ATTN_W)],
        out_shape=[jax.ShapeDtypeStruct((t, 3 * CONV_W), BF16), jax.ShapeDtypeStruct((t, 2 * ATTN_W), BF16),
                   jax.ShapeDtypeStruct((t, CONV_W), BF16), jax.ShapeDtypeStruct((t, ATTN_W), F32),
                   jax.ShapeDtypeStruct((t, ATTN_W), F32), jax.ShapeDtypeStruct((t, ATTN_W), F32)],
        scratch_shapes=[pltpu.VMEM((8, CONV_W), F32)],
        compiler_params=_cparams("arbitrary"),
    )(x, g_mix, w_in, conv_w, conv_b, qg, kg, bd)


SUPER = 16 * QK_BLOCK
KEYS = 2 * QK_BLOCK
UNITS = SUPER // QK_BLOCK


def _rows(start, size, dil):
    return pl.ds(start, size) if dil == 1 else pl.ds(start, size, stride=dil)


def _attn_bias(sl_ref, dil):
    qi = lax.broadcasted_iota(jnp.int32, (KEYS, KEYS), 0)
    kj = lax.broadcasted_iota(jnp.int32, (KEYS, KEYS), 1)
    step = jnp.bitwise_and(qi, QK_BLOCK - 1) + QK_BLOCK - kj
    slope = jnp.where(qi < QK_BLOCK, sl_ref[0, 0:1, 0:1], sl_ref[0, 1:2, 0:1])
    bias = jnp.where(jnp.logical_and(step >= 0, step <= QK_BLOCK), -slope * (step * dil).astype(F32), -jnp.inf)
    return bias, kj >= QK_BLOCK


def _unit_start(u, dil):
    if dil == 1:
        return pl.multiple_of(u * QK_BLOCK, QK_BLOCK)
    if dil == 4:
        return jnp.bitwise_and(u, 3) + (u // 4) * (4 * QK_BLOCK)
    return u


def _stack_heads(a, head0):
    zero = jnp.zeros_like(a)
    return jnp.concatenate([jnp.where(head0, a, zero), jnp.where(head0, zero, a)], axis=0)


def _attn_fwd(q, k, v, slopes):
    t = q.shape[0]
    nsb = t // SUPER

    def body(q_ref, kc_ref, kp_ref, vc_ref, vp_ref, sl_ref, o_ref, l_ref, e_ref, m_ref, kk, vv, ob, lb):
        s = pl.program_id(1)
        kk[0:SUPER, :] = kp_ref[...]
        kk[SUPER:, :] = kc_ref[...]
        vv[0:SUPER, :] = vp_ref[...]
        vv[SUPER:, :] = vc_ref[...]
        head0 = lax.broadcasted_iota(jnp.int32, (QK_BLOCK, QK_BLOCK), 1) < HEAD_DIM

        for b, dil in enumerate(DILATIONS):
            bias, own_half = _attn_bias(sl_ref, dil)

            def unit(u, carry, b=b, dil=dil, bias=bias, own_half=own_half):
                start = _unit_start(u, dil)
                first_key = SUPER + start - QK_BLOCK * dil
                q2 = _stack_heads(q_ref[_rows(start, QK_BLOCK, dil), :].astype(BF16), head0)
                k2 = kk[_rows(first_key, KEYS, dil), :].astype(BF16)
                v2 = vv[_rows(first_key, KEYS, dil), :].astype(BF16)
                has_prev = jnp.logical_or(s > 0, start >= QK_BLOCK * dil)
                sc = jnp.where(jnp.logical_or(own_half, has_prev), _mm_nt(q2, k2) + bias, -jnp.inf)
                m = jnp.max(sc, axis=-1, keepdims=True)
                e = jnp.exp(sc - m)
                den = jnp.sum(e, axis=-1, keepdims=True)
                eb = e.astype(BF16)
                e_ref[b * UNITS + u] = eb
                o2 = _mm(eb, v2) / den
                l2 = m + jnp.log(den)
                ob[b, _rows(start, QK_BLOCK, dil), :] = jnp.where(head0, o2[0:QK_BLOCK], o2[QK_BLOCK:])
                lb[b, _rows(start, QK_BLOCK, dil), :] = jnp.where(head0, l2[0:QK_BLOCK], l2[QK_BLOCK:])
                m_ref[b, _rows(start, QK_BLOCK, dil), :] = jnp.where(head0, m[0:QK_BLOCK], m[QK_BLOCK:])
                return carry

            lax.fori_loop(0, UNITS, unit, 0, unroll=16)

        def merge(i, carry):
            rows = pl.ds(pl.multiple_of(i * 256, 256), 256)
            la, lb_, lc = lb[0, rows, :], lb[1, rows, :], lb[2, rows, :]
            mx = jnp.maximum(jnp.maximum(la, lb_), lc)
            wa, wb, wc = jnp.exp(la - mx), jnp.exp(lb_ - mx), jnp.exp(lc - mx)
            sw = wa + wb + wc
            o_ref[rows, :] = ((wa * ob[0, rows, :] + wb * ob[1, rows, :] + wc * ob[2, rows, :]) / sw).astype(BF16)
            l_ref[rows, :] = mx + jnp.log(sw)
            return carry

        lax.fori_loop(0, SUPER // 256, merge, 0)

    cur = pl.BlockSpec((SUPER, QK_BLOCK), lambda p, s: (s, p))
    prev = pl.BlockSpec((SUPER, QK_BLOCK), lambda p, s: (jnp.maximum(s - 1, 0), p))
    return pl.pallas_call(
        body, name="attn_fwd", grid=(4, nsb),
        in_specs=[cur, cur, prev, cur, prev, pl.BlockSpec((1, 2, QK_BLOCK), lambda p, s: (p, 0, 0))],
        out_specs=[cur, cur, pl.BlockSpec((None, None, 3 * UNITS, KEYS, KEYS), lambda p, s: (p, s, 0, 0, 0)),
                   pl.BlockSpec((3, SUPER, QK_BLOCK), lambda p, s: (0, s, p))],
        out_shape=[jax.ShapeDtypeStruct((t, ATTN_W), BF16), jax.ShapeDtypeStruct((t, ATTN_W), F32),
                   jax.ShapeDtypeStruct((4, nsb, 3 * UNITS, KEYS, KEYS), BF16),
                   jax.ShapeDtypeStruct((3, t, ATTN_W), F32)],
        scratch_shapes=[pltpu.VMEM((2 * SUPER, QK_BLOCK), F32), pltpu.VMEM((2 * SUPER, QK_BLOCK), F32),
                        pltpu.VMEM((3, SUPER, QK_BLOCK), F32), pltpu.VMEM((3, SUPER, QK_BLOCK), F32)],
        compiler_params=_cparams("parallel", "arbitrary"),
    )(q, k, k, v, v, slopes)


def _outproj_fwd(ya, yc, x, goc, goa, w_out, tm):
    t = x.shape[0]

    def body(ya_ref, yc_ref, x_ref, goc_ref, goa_ref, w_ref, x1_ref):
        _, ychat = _rms_stats(yc_ref[...].astype(F32))
        _, yahat = _rms_stats(ya_ref[...].astype(F32))
        acc = _mm((ychat * goc_ref[...]).astype(BF16), w_ref[0:CONV_W, :])
        acc += _mm((yahat * goa_ref[...]).astype(BF16), w_ref[CONV_W:, :])
        x1_ref[...] = x_ref[...] + acc

    def blk(c):
        return pl.BlockSpec((tm, c), lambda i: (i, 0))

    return pl.pallas_call(
        body, name="outproj_fwd", grid=(t // tm,),
        in_specs=[blk(ATTN_W), blk(CONV_W), blk(D_MODEL), _full((1, CONV_W)), _full((1, ATTN_W)),
                  _full((D_MODEL, D_MODEL))],
        out_specs=blk(D_MODEL),
        out_shape=jax.ShapeDtypeStruct((t, D_MODEL), F32),
        compiler_params=_cparams("parallel"),
    )(ya, yc, x, goc, goa, w_out)


def _ffn_fwd(x1, g_ffn, w_gate_t, w_up_t, w_down, fcw, fcb, tm):
    t = x1.shape[0]

    def body(x_ref, g_ref, wg_ref, wu_ref, wd_ref, cw_ref, cb_ref, gp_ref, up_ref, h_ref, x2_ref, carry_ref):
        @pl.when(pl.program_id(0) == 0)
        def _():
            carry_ref[...] = jnp.zeros_like(carry_ref)

        xv = x_ref[...]
        _, xhat = _rms_stats(xv)
        h = (xhat * g_ref[...]).astype(BF16)
        h_ref[...] = h
        gp = _mm_nt(h, wg_ref[...])
        gp_ref[...] = gp.astype(BF16)
        gate, _, _ = _conv_fwd(gp, carry_ref[7:8, :], carry_ref[6:7, :], cw_ref, cb_ref)
        carry_ref[...] = gp[tm - 8:tm, :]
        up = _mm_nt(h, wu_ref[...])
        up_ref[...] = up.astype(BF16)
        a = (gate * _sigmoid(gate) * up).astype(BF16)
        x2_ref[...] = xv + _mm(a, wd_ref[...])

    def blk(c):
        return pl.BlockSpec((tm, c), lambda i: (i, 0))

    def weight():
        return pl.BlockSpec((D_FF, D_MODEL), lambda i: (0, 0), pipeline_mode=pl.Buffered(1))

    return pl.pallas_call(
        body, name="ffn_fwd", grid=(t // tm,),
        in_specs=[blk(D_MODEL), _full((1, D_MODEL)), weight(), weight(), weight(), _full((3, D_FF)), _full((1, D_FF))],
        out_specs=[blk(D_FF), blk(D_FF), blk(D_MODEL), blk(D_MODEL)],
        out_shape=[jax.ShapeDtypeStruct((t, D_FF), BF16), jax.ShapeDtypeStruct((t, D_FF), BF16),
                   jax.ShapeDtypeStruct((t, D_MODEL), BF16), jax.ShapeDtypeStruct((t, D_MODEL), F32)],
        scratch_shapes=[pltpu.VMEM((8, D_FF), F32)],
        compiler_params=_cparams("arbitrary", vmem=V7X_VMEM_LIMIT_LARGE),
    )(x1, g_ffn, w_gate_t, w_up_t, w_down, fcw, fcb)


def _ple_fwd_bwd(x2, p, target, g_ple, w_pg, w_pp, tm):
    t = x2.shape[0]

    def body(x_ref, p_ref, t_ref, g_ref, wg_ref, wp_ref, dx_ref, dxb_ref, loss_ref, dwgb_ref, dwp_ref, dg_ref,
             dwg_ref):
        @pl.when(pl.program_id(0) == 0)
        def _():
            loss_ref[...] = jnp.zeros_like(loss_ref)
            dwg_ref[...] = jnp.zeros_like(dwg_ref)
            dwp_ref[...] = jnp.zeros_like(dwp_ref)
            dg_ref[...] = jnp.zeros_like(dg_ref)

        xv = x_ref[...]
        r, xhat = _rms_stats(xv)
        g = g_ref[...]
        h = (xhat * g).astype(BF16)
        pg = _sigmoid(_mm(h, wg_ref[...]))
        pb = p_ref[...].astype(BF16)
        pp = _mm(pb, wp_ref[...])
        err = xv + pg * pp - t_ref[...]
        loss_ref[...] += 0.5 * jnp.sum(jnp.mean(err * err, axis=-1, keepdims=True))
        dx3 = err * (1.0 / D_MODEL)
        d_pp = (dx3 * pg).astype(BF16)
        d_pre = (dx3 * pp * pg * (1.0 - pg)).astype(BF16)
        dwp_ref[...] += _mm_tn(pb, d_pp)
        dwg_ref[...] += _mm_tn(h, d_pre)
        dh = _mm_nt(d_pre, wg_ref[...])
        dg_ref[...] += jnp.sum(dh * xhat, axis=0, keepdims=True)
        dx2 = dx3 + _rms_bwd(dh, xhat, r, g)
        dx_ref[...] = dx2
        dxb_ref[...] = dx2.astype(BF16)

        @pl.when(pl.program_id(0) == t // tm - 1)
        def _():
            dwgb_ref[...] = dwg_ref[...].astype(BF16)

    def blk(c):
        return pl.BlockSpec((tm, c), lambda i: (i, 0))

    return pl.pallas_call(
        body, name="ple_fwd_bwd", grid=(t // tm,),
        in_specs=[blk(D_MODEL), blk(PLE_DIM), blk(D_MODEL), _full((1, D_MODEL)), _full((D_MODEL, D_MODEL)),
                  _full((PLE_DIM, D_MODEL))],
        out_specs=[blk(D_MODEL), blk(D_MODEL), _full((8, 128)), _full((D_MODEL, D_MODEL)),
                   _full((PLE_DIM, D_MODEL)), _full((1, D_MODEL))],
        out_shape=[jax.ShapeDtypeStruct((t, D_MODEL), F32), jax.ShapeDtypeStruct((t, D_MODEL), BF16),
                   jax.ShapeDtypeStruct((8, 128), F32),
                   jax.ShapeDtypeStruct((D_MODEL, D_MODEL), BF16), jax.ShapeDtypeStruct((PLE_DIM, D_MODEL), F32),
                   jax.ShapeDtypeStruct((1, D_MODEL), F32)],
        scratch_shapes=[pltpu.VMEM((D_MODEL, D_MODEL), F32)],
        compiler_params=_cparams("arbitrary"),
    )(x2, p, target, g_ple, w_pg, w_pp)


def _ffn_bwd(dx2, h2, gp, up, w_gate, w_up, w_down, fcw, fcb, tm):
    t = dx2.shape[0]
    nblk = t // tm
    fc = D_FF // FF_CHUNKS
    half = tm // FFN_BWD_PARTS

    def body(dx_ref, h_ref, gp_ref, gph_ref, up_ref, wg_ref, wu_ref, wd_ref, cw_ref, cb_ref,
             dh_ref, dwd_hbm, dwu_hbm, dwg_hbm, dcw_ref, dcb_ref, carry_ref, a_scr, dup_scr, dgp_scr,
             dwd_acc, dwu_acc, dwg_acc, stage, stage_sem):
        i = pl.program_id(1)

        @pl.when(i == 0)
        def _():
            carry_ref[...] = jnp.zeros_like(carry_ref)
            dwd_acc[...] = jnp.zeros_like(dwd_acc)
            dwu_acc[...] = jnp.zeros_like(dwu_acc)
            dwg_acc[...] = jnp.zeros_like(dwg_acc)
            dcw_ref[...] = jnp.zeros_like(dcw_ref)
            dcb_ref[...] = jnp.zeros_like(dcb_ref)

        keep = (i < nblk - 1).astype(F32)
        later = carry_ref[...]
        for hf in reversed(range(FFN_BWD_PARTS)):
            rows = slice(hf * half, (hf + 1) * half)
            dxb = dx_ref[rows, :]
            gp_v = gp_ref[rows, :].astype(F32)
            if hf > 0:
                before = gp_ref[hf * half - 16:hf * half, :].astype(F32)
            else:
                before = gph_ref[...].astype(F32) * keep
            gate, gp1, gp2 = _conv_fwd(gp_v, before[15:16, :], before[14:15, :], cw_ref, cb_ref)
            s = _sigmoid(gate)
            silu = gate * s
            up_v = up_ref[rows, :].astype(F32)
            da = _mm_nt(dxb, wd_ref[...])
            a_scr[rows, :] = (silu * up_v).astype(BF16)
            d_up = (da * silu).astype(BF16)
            dup_scr[rows, :] = d_up
            d_gate = da * up_v * (s * (1.0 + gate * (1.0 - s)))
            d_gp = _conv_bwd_input(d_gate, later[0:1, :], later[1:2, :], cw_ref).astype(BF16)
            dgp_scr[rows, :] = d_gp
            later = d_gate[0:8, :]
            dcw_ref[0:1, :] += jnp.sum(d_gate * gp2, axis=0, keepdims=True)
            dcw_ref[1:2, :] += jnp.sum(d_gate * gp1, axis=0, keepdims=True)
            dcw_ref[2:3, :] += jnp.sum(d_gate * gp_v, axis=0, keepdims=True)
            dcb_ref[...] += jnp.sum(d_gate, axis=0, keepdims=True)
            dh_ref[rows, :] = (_mm(d_gp, wg_ref[...]) + _mm(d_up, wu_ref[...])).astype(BF16)
        carry_ref[...] = later
        dwd_acc[...] += _mm_tn(a_scr[...], dx_ref[...])
        dwu_acc[...] += _mm_tn(h_ref[...], dup_scr[...])
        dwg_acc[...] += _mm_tn(h_ref[...], dgp_scr[...])

        @pl.when(i == nblk - 1)
        def _():
            rows = pl.ds(pl.multiple_of(pl.program_id(0) * fc, 16), fc)
            for acc, out, flip in ((dwd_acc, dwd_hbm, False), (dwu_acc, dwu_hbm, True), (dwg_acc, dwg_hbm, True)):
                stage[...] = (acc[...].T if flip else acc[...]).astype(BF16)
                copy = pltpu.make_async_copy(stage, out.at[rows, :], stage_sem)
                copy.start()
                copy.wait()

    def rev(i):
        return nblk - 1 - i

    one = pl.Buffered(1)
    in_specs = [
        pl.BlockSpec((tm, D_MODEL), lambda j, i: (rev(i), 0)),
        pl.BlockSpec((tm, D_MODEL), lambda j, i: (rev(i), 0)),
        pl.BlockSpec((tm, fc), lambda j, i: (rev(i), j)),
        pl.BlockSpec((16, fc), lambda j, i: (jnp.maximum(rev(i) * (tm // 16) - 1, 0), j)),
        pl.BlockSpec((tm, fc), lambda j, i: (rev(i), j)),
        pl.BlockSpec((fc, D_MODEL), lambda j, i: (j, 0), pipeline_mode=one),
        pl.BlockSpec((fc, D_MODEL), lambda j, i: (j, 0), pipeline_mode=one),
        pl.BlockSpec((fc, D_MODEL), lambda j, i: (j, 0), pipeline_mode=one),
        pl.BlockSpec((3, fc), lambda j, i: (0, j)),
        pl.BlockSpec((1, fc), lambda j, i: (0, j)),
    ]
    out_specs = [
        pl.BlockSpec((None, tm, D_MODEL), lambda j, i: (j, rev(i), 0)),
        ANY, ANY, ANY,
        pl.BlockSpec((3, fc), lambda j, i: (0, j)),
        pl.BlockSpec((1, fc), lambda j, i: (0, j)),
    ]
    return pl.pallas_call(
        body, name="ffn_bwd", grid=(FF_CHUNKS, nblk), in_specs=in_specs, out_specs=out_specs,
        out_shape=[jax.ShapeDtypeStruct((FF_CHUNKS, t, D_MODEL), BF16), jax.ShapeDtypeStruct((D_FF, D_MODEL), BF16),
                   jax.ShapeDtypeStruct((D_FF, D_MODEL), BF16), jax.ShapeDtypeStruct((D_FF, D_MODEL), BF16),
                   jax.ShapeDtypeStruct((3, D_FF), F32), jax.ShapeDtypeStruct((1, D_FF), F32)],
        scratch_shapes=[pltpu.VMEM((8, fc), F32), pltpu.VMEM((tm, fc), BF16), pltpu.VMEM((tm, fc), BF16),
                        pltpu.VMEM((tm, fc), BF16), pltpu.VMEM((fc, D_MODEL), F32), pltpu.VMEM((D_MODEL, fc), F32),
                        pltpu.VMEM((D_MODEL, fc), F32), pltpu.VMEM((fc, D_MODEL), BF16), pltpu.SemaphoreType.DMA],
        compiler_params=_cparams("arbitrary", "arbitrary", vmem=V7X_VMEM_LIMIT_LARGE),
    )(dx2, h2, gp, gp, up, w_gate, w_up, w_down, fcw, fcb)


def _outproj_bwd(dh2, dx2, x1, g_ffn, w_out, yc, ya, goc, goa, zconv, conv_w, conv_b, bd, tm):
    t = x1.shape[0]
    nblk = t // tm

    def body(dh_ref, dx2_ref, x1_ref, g_ref, w_ref, yc_ref, ya_ref, goc_ref, goa_ref, zc_ref, zch_ref, cw_ref, cb_ref,
             bd_ref, dx1_ref, dya_ref, dd_ref, dzc_ref, dwb_ref, dg_ref, dgoc_ref, dgoa_ref, dcw_ref, dcb_ref,
             carry_ref, dw_ref):
        i = pl.program_id(0)

        @pl.when(i == 0)
        def _():
            carry_ref[...] = jnp.zeros_like(carry_ref)
            for ref in (dw_ref, dg_ref, dgoc_ref, dgoa_ref, dcw_ref, dcb_ref):
                ref[...] = jnp.zeros_like(ref)

        keep = (i < nblk - 1).astype(F32)
        dh2_v = dh_ref[0].astype(F32)
        for j in range(1, FF_CHUNKS):
            dh2_v = dh2_v + dh_ref[j].astype(F32)
        r, xhat = _rms_stats(x1_ref[...])
        dg_ref[...] += jnp.sum(dh2_v * xhat, axis=0, keepdims=True)
        dx1 = dx2_ref[...] + _rms_bwd(dh2_v, xhat, r, g_ref[...])
        dx1_ref[...] = dx1
        dx1b = dx1.astype(BF16)
        dy = _mm_nt(dx1b, w_ref[...])

        yc_v = yc_ref[...].astype(F32)
        rc, ychat = _rms_stats(yc_v)
        dw_ref[0:CONV_W, :] += _mm_tn((ychat * goc_ref[...]).astype(BF16), dx1b)
        dyc = dy[:, 0:CONV_W]
        dgoc_ref[...] += jnp.sum(dyc * ychat, axis=0, keepdims=True)
        d_yc = _rms_bwd(dyc, ychat, rc, goc_ref[...])

        ya_v = ya_ref[...].astype(F32)
        ra, yahat = _rms_stats(ya_v)
        dw_ref[CONV_W:, :] += _mm_tn((yahat * goa_ref[...]).astype(BF16), dx1b)
        dya = dy[:, CONV_W:]
        dgoa_ref[...] += jnp.sum(dya * yahat, axis=0, keepdims=True)
        d_ya = _rms_bwd(dya, yahat, ra, goa_ref[...])
        dya_ref[...] = d_ya
        dd_ref[...] = _seg_sum64(d_ya * ya_v, bd_ref)

        zb = zc_ref[:, 0:CONV_W].astype(F32)
        zc = zc_ref[:, CONV_W:2 * CONV_W].astype(F32)
        zx = zc_ref[:, 2 * CONV_W:3 * CONV_W].astype(F32)
        u = zc * zx
        uh = (zch_ref[:, CONV_W:2 * CONV_W].astype(F32) * zch_ref[:, 2 * CONV_W:3 * CONV_W].astype(F32)) * keep
        cv, u1, u2 = _conv_fwd(u, uh[15:16, :], uh[14:15, :], cw_ref, cb_ref)
        d_cv = d_yc * zb
        d_u = _conv_bwd_input(d_cv, carry_ref[0:1, :], carry_ref[1:2, :], cw_ref)
        carry_ref[...] = d_cv[0:8, :]
        dcw_ref[0:1, :] += jnp.sum(d_cv * u2, axis=0, keepdims=True)
        dcw_ref[1:2, :] += jnp.sum(d_cv * u1, axis=0, keepdims=True)
        dcw_ref[2:3, :] += jnp.sum(d_cv * u, axis=0, keepdims=True)
        dcb_ref[...] += jnp.sum(d_cv, axis=0, keepdims=True)
        dzc_ref[:, 0:CONV_W] = (d_yc * cv).astype(BF16)
        dzc_ref[:, CONV_W:2 * CONV_W] = (d_u * zx).astype(BF16)
        dzc_ref[:, 2 * CONV_W:3 * CONV_W] = (d_u * zc).astype(BF16)

        @pl.when(i == nblk - 1)
        def _():
            dwb_ref[...] = dw_ref[...].astype(BF16)

    def rev(i):
        return nblk - 1 - i

    def blk(c):
        return pl.BlockSpec((tm, c), lambda i: (rev(i), 0))

    in_specs = [
        pl.BlockSpec((FF_CHUNKS, tm, D_MODEL), lambda i: (0, rev(i), 0)),
        blk(D_MODEL), blk(D_MODEL), _full((1, D_MODEL)),
        pl.BlockSpec((D_MODEL, D_MODEL), lambda i: (0, 0), pipeline_mode=pl.Buffered(1)),
        blk(CONV_W), blk(ATTN_W), _full((1, CONV_W)), _full((1, ATTN_W)),
        blk(3 * CONV_W),
        pl.BlockSpec((16, 3 * CONV_W), lambda i: (jnp.maximum(rev(i) * (tm // 16) - 1, 0), 0)),
        _full((3, CONV_W)), _full((1, CONV_W)), _full((256, 256)),
    ]
    out_specs = [blk(D_MODEL), blk(ATTN_W), blk(ATTN_W), blk(3 * CONV_W), _full((D_MODEL, D_MODEL)),
                 _full((1, D_MODEL)), _full((1, CONV_W)), _full((1, ATTN_W)), _full((3, CONV_W)), _full((1, CONV_W))]
    return pl.pallas_call(
        body, name="outproj_bwd", grid=(nblk,), in_specs=in_specs, out_specs=out_specs,
        out_shape=[jax.ShapeDtypeStruct((t, D_MODEL), F32), jax.ShapeDtypeStruct((t, ATTN_W), F32),
                   jax.ShapeDtypeStruct((t, ATTN_W), F32), jax.ShapeDtypeStruct((t, 3 * CONV_W), BF16),
                   jax.ShapeDtypeStruct((D_MODEL, D_MODEL), BF16), jax.ShapeDtypeStruct((1, D_MODEL), F32),
                   jax.ShapeDtypeStruct((1, CONV_W), F32), jax.ShapeDtypeStruct((1, ATTN_W), F32),
                   jax.ShapeDtypeStruct((3, CONV_W), F32), jax.ShapeDtypeStruct((1, CONV_W), F32)],
        scratch_shapes=[pltpu.VMEM((8, CONV_W), F32), pltpu.VMEM((D_MODEL, D_MODEL), F32)],
        compiler_params=_cparams("arbitrary", vmem=V7X_VMEM_LIMIT_LARGE),
    )(dh2, dx2, x1, g_ffn, w_out, yc, ya, goc, goa, zconv, zconv, conv_w, conv_b, bd)


def _attn_bwd(q, k, v, dya, lse, dd, e_all, m_all, after):
    t = q.shape[0]
    nsb = t // SUPER

    def body(q_ref, kc_ref, kp_ref, vc_ref, vp_ref, dy_ref, l_ref, d_ref, e_ref, m_ref, after_ref,
             dq_ref, dk_ref, dv_ref, kk, vv, dkacc, dvacc, dwide):
        s = pl.program_id(1)

        @pl.when(s == 0)
        def _():
            dkacc[...] = jnp.zeros_like(dkacc)
            dvacc[...] = jnp.zeros_like(dvacc)

        dkacc[0:SUPER, :] = dkacc[SUPER:, :]
        dvacc[0:SUPER, :] = dvacc[SUPER:, :]
        dkacc[SUPER:, :] = jnp.zeros((SUPER, QK_BLOCK), F32)
        dvacc[SUPER:, :] = jnp.zeros((SUPER, QK_BLOCK), F32)

        @pl.when(s < nsb)
        def _():
            kk[0:SUPER, :] = kp_ref[...]
            kk[SUPER:, :] = kc_ref[...]
            vv[0:SUPER, :] = vp_ref[...]
            vv[SUPER:, :] = vc_ref[...]
            head0 = lax.broadcasted_iota(jnp.int32, (QK_BLOCK, QK_BLOCK), 1) < HEAD_DIM

            def widened(a):
                other = pltpu.roll(a, HEAD_DIM, 1)
                first = lax.broadcasted_iota(jnp.int32, a.shape, 1) < HEAD_DIM
                return jnp.where(first, a, other), jnp.where(first, other, a)

            def stacked(h0, h1):
                return jnp.concatenate([jnp.concatenate([h0, h0], axis=1), jnp.concatenate([h1, h1], axis=1)], axis=0)

            def widen_dd(i, carry):
                rows = pl.ds(pl.multiple_of(i * 256, 256), 256)
                dwide[0, rows, :], dwide[1, rows, :] = widened(d_ref[rows, :])
                return carry

            lax.fori_loop(0, SUPER // 256, widen_dd, 0)

            for b, dil in enumerate(DILATIONS):
                def unit(u, carry, b=b, dil=dil):
                    start = _unit_start(u, dil)
                    first_key = SUPER + start - QK_BLOCK * dil
                    qrows = _rows(start, QK_BLOCK, dil)
                    krows = _rows(first_key, KEYS, dil)
                    q2 = _stack_heads(q_ref[qrows, :].astype(BF16), head0)
                    dy2 = _stack_heads(dy_ref[qrows, :].astype(BF16), head0)
                    g2 = stacked(*widened(jnp.exp(m_ref[b, qrows, :] - l_ref[qrows, :])))
                    d2 = stacked(dwide[0, qrows, :], dwide[1, qrows, :])
                    k2 = kk[krows, :].astype(BF16)
                    v2 = vv[krows, :].astype(BF16)
                    prob = e_ref[b * UNITS + u].astype(F32) * g2
                    ds = (prob * (_mm_nt(dy2, v2) - d2)).astype(BF16)
                    dvacc[krows, :] += _mm_tn(prob.astype(BF16), dy2)
                    dkacc[krows, :] += _mm_tn(ds, q2)
                    dq2 = _mm(ds, k2)
                    dq = jnp.where(head0, dq2[0:QK_BLOCK], dq2[QK_BLOCK:]) * ATTN_SCALE
                    if b == 0:
                        dq_ref[qrows, :] = dq
                    else:
                        dq_ref[qrows, :] += dq
                    return carry

                lax.fori_loop(0, UNITS, unit, 0, unroll=16)

        dk_ref[...] = dkacc[0:SUPER, :]
        dv_ref[...] = dvacc[0:SUPER, :].astype(BF16)

    def cur_map(p, s):
        return (jnp.minimum(s, nsb - 1), p)

    def prev_map(p, s):
        return (jnp.clip(s - 1, 0, nsb - 1), p)

    cur = pl.BlockSpec((SUPER, QK_BLOCK), cur_map)
    prev = pl.BlockSpec((SUPER, QK_BLOCK), prev_map)
    return pl.pallas_call(
        body, name="attn_bwd", grid=(4, nsb + 1),
        in_specs=[cur, cur, prev, cur, prev, cur, cur, cur,
                  pl.BlockSpec((None, None, 3 * UNITS, KEYS, KEYS), lambda p, s: (p, jnp.minimum(s, nsb - 1), 0, 0, 0)),
                  pl.BlockSpec((3, SUPER, QK_BLOCK), lambda p, s: (0, jnp.minimum(s, nsb - 1), p)),
                  pl.BlockSpec(memory_space=pl.ANY)],
        out_specs=[cur, prev, prev],
        out_shape=[jax.ShapeDtypeStruct((t, ATTN_W), F32), jax.ShapeDtypeStruct((t, ATTN_W), F32),
                   jax.ShapeDtypeStruct((t, ATTN_W), BF16)],
        scratch_shapes=[pltpu.VMEM((2 * SUPER, QK_BLOCK), F32)] * 4 + [pltpu.VMEM((2, SUPER, QK_BLOCK), F32)],
        compiler_params=_cparams("parallel", "arbitrary", vmem=V7X_VMEM_LIMIT_LARGE),
    )(q, k, k, v, v, dya, lse, dd, e_all, m_all, after)


def _inproj_bwd(dq, dk, dv, dzconv, zqk, x, dx1, g_mix, w_in, qg, kg, bd, tm):
    t = x.shape[0]
    nblk = t // tm
    shard = IN_COLS // N_DEV

    def body(dq_ref, dk_ref, dv_ref, dzc_ref, zqk_ref, x_ref, dx1_ref, g_ref, w_ref, qg_ref,
             kg_ref, bd_ref, dx_ref, dw_hbm, dg_ref, dqg_ref, dkg_ref, dw_ref, stage, stage_sem):
        @pl.when(pl.program_id(0) == 0)
        def _():
            for ref in (dw_ref, dg_ref, dqg_ref, dkg_ref):
                ref[...] = jnp.zeros_like(ref)

        parts = [dzc_ref[...]]
        for j, (dn_ref, gain_ref, dgain_ref) in enumerate(((dq_ref, qg_ref, dqg_ref), (dk_ref, kg_ref, dkg_ref))):
            dn = dn_ref[...]
            z = zqk_ref[:, j * ATTN_W:(j + 1) * ATTN_W].astype(F32)
            r = lax.rsqrt(_seg_sum64(z * z, bd_ref) * (1.0 / HEAD_DIM) + EPS)
            zhat = z * r
            dgain_ref[...] += jnp.sum(dn * zhat, axis=0, keepdims=True)
            gd = dn * gain_ref[...]
            parts.append((r * (gd - zhat * (_seg_sum64(gd * zhat, bd_ref) * (1.0 / HEAD_DIM)))).astype(BF16))
        parts.append(dv_ref[...].astype(BF16))
        dz = jnp.concatenate(parts, axis=1)

        r, xhat = _rms_stats(x_ref[...])
        g = g_ref[...]
        dw_ref[...] += _mm_tn((xhat * g).astype(BF16), dz)
        dh = _mm_nt(dz, w_ref[...])
        dg_ref[...] += jnp.sum(dh * xhat, axis=0, keepdims=True)
        dx_ref[...] = dx1_ref[...] + _rms_bwd(dh, xhat, r, g)

        @pl.when(pl.program_id(0) == nblk - 1)
        def _():
            for k in range(N_DEV):
                stage[...] = dw_ref[:, k * shard:(k + 1) * shard].astype(BF16)
                copy = pltpu.make_async_copy(stage, dw_hbm.at[k], stage_sem)
                copy.start()
                copy.wait()

    def blk(c):
        return pl.BlockSpec((tm, c), lambda i: (i, 0))

    return pl.pallas_call(
        body, name="inproj_bwd", grid=(nblk,),
        in_specs=[blk(ATTN_W)] * 3 + [blk(3 * CONV_W), blk(2 * ATTN_W), blk(D_MODEL), blk(D_MODEL), _full((1, D_MODEL)),
                                      _full((D_MODEL, IN_COLS)), _full((1, ATTN_W)), _full((1, ATTN_W)),
                                      _full((256, 256))],
        out_specs=[blk(D_MODEL), ANY, _full((1, D_MODEL)), _full((1, ATTN_W)), _full((1, ATTN_W))],
        out_shape=[jax.ShapeDtypeStruct((t, D_MODEL), F32), jax.ShapeDtypeStruct((N_DEV, D_MODEL, shard), BF16),
                   jax.ShapeDtypeStruct((1, D_MODEL), F32), jax.ShapeDtypeStruct((1, ATTN_W), F32),
                   jax.ShapeDtypeStruct((1, ATTN_W), F32)],
        scratch_shapes=[pltpu.VMEM((D_MODEL, IN_COLS), F32), pltpu.VMEM((D_MODEL, shard), BF16),
                        pltpu.SemaphoreType.DMA],
        compiler_params=_cparams("arbitrary"),
    )(dq, dk, dv, dzconv, zqk, x, dx1, g_mix, w_in, qg, kg, bd)


def _ordered_after(a, token):
    return a if token is None else a + token


def _local_step(x, p, target, w, tms, hooks=None):
    hooks = hooks or {}
    bd = jnp.asarray(np.kron(np.eye(4, dtype=np.float32), np.ones((HEAD_DIM, HEAD_DIM), np.float32)), BF16)
    qg = jnp.tile(w["q_norm_g"], (1, 8))
    kg = jnp.tile(w["k_norm_g"], (1, 8))
    slopes = np.exp2(-np.arange(1, 9, dtype=np.float32))
    slopes = jnp.asarray(np.broadcast_to(slopes.reshape(4, 2, 1), (4, 2, QK_BLOCK)))

    zconv, zqk, yc, q, k, v = _inproj_fwd(x, w["g_mix"], w["w_in"], w["conv_w"], w["conv_b"], qg, kg, bd, tms[0])
    ya, lse, e_all, m_all = _attn_fwd(q, k, v, slopes)
    if "late_weights" in hooks:
        w = {**w, **hooks["late_weights"](lse)}
    x1 = _outproj_fwd(ya, yc, x, w["g_out_conv"], w["g_out_attn"], w["w_out"], tms[1])
    gp, up, h2, x2 = _ffn_fwd(x1, w["g_ffn"], w["w_gate"], w["w_up"], w["w_down"], w["ffn_conv_w"], w["ffn_conv_b"],
                              tms[0])
    dx2, dx2b, loss, dw_pg, dw_pp, dg_ple = _ple_fwd_bwd(x2, p, target, w["g_ple"], w["w_ple_gate"], w["w_ple_proj"], tms[0])
    dh2, dw_down, dw_up, dw_gate, dfcw, dfcb = _ffn_bwd(dx2b, h2, gp, up, w["w_gate"], w["w_up"], w["w_down"],
                                                        w["ffn_conv_w"], w["ffn_conv_b"], tms[0])
    token = None
    if "ffn_grads" in hooks:
        token = hooks["ffn_grads"]({"w_ple_gate": dw_pg, "w_ple_proj": dw_pp, "w_down": dw_down, "w_up": dw_up,
                                    "w_gate": dw_gate, "ffn_conv_b": dfcb})
    dx1, dya, dd, dzconv, dw_out, dg_ffn, dgoc, dgoa, dcw, dcb = _outproj_bwd(
        dh2, dx2, x1, _ordered_after(w["g_ffn"], token), w["w_out"], yc, ya, w["g_out_conv"], w["g_out_attn"], zconv,
        w["conv_w"], w["conv_b"], bd, tms[0])
    token = hooks["outproj_done"](dx1) if "outproj_done" in hooks else None
    dq, dk, dv = _attn_bwd(q, k, v, dya, lse, dd, e_all, m_all, slopes if token is None else token)
    dx, dw_in, dg_mix, dqg, dkg = _inproj_bwd(dq, dk, dv, dzconv, zqk, x, dx1, w["g_mix"], w["w_in"], qg, kg, bd,
                                              tms[0])
    grads = {
        "g_mix": dg_mix, "w_in": dw_in, "conv_w": dcw, "conv_b": dcb,
        "q_norm_g": dqg, "k_norm_g": dkg,
        "g_out_conv": dgoc, "g_out_attn": dgoa, "w_out": dw_out, "g_ffn": dg_ffn, "w_gate": dw_gate, "w_up": dw_up,
        "ffn_conv_w": dfcw, "ffn_conv_b": dfcb, "w_down": dw_down, "g_ple": dg_ple, "w_ple_gate": dw_pg,
        "w_ple_proj": dw_pp,
    }
    return loss, dx, grads


ANY = pl.BlockSpec(memory_space=pl.ANY)
MESH = pl.DeviceIdType.MESH


def _all_gather(shards, name):
    n = len(shards)

    def body(*refs):
        ins, outs = refs[:n], refs[n:2 * n]
        send_sems, recv_sems, local_sems = refs[2 * n:]
        x, y, c = lax.axis_index("x"), lax.axis_index("y"), lax.axis_index("c")
        me, sibling = (x, y, c), (x, y, 1 - c)
        chips = [(1 - x, y), (x, 1 - y), (1 - x, 1 - y)]

        def slot(dev):
            return 4 * dev[0] + 2 * dev[1] + dev[2]

        def copy(b, k, block, to, src=None):
            dst = outs[b].at[slot(block)]
            return pltpu.make_async_remote_copy(
                src_ref=dst if src is None else src, dst_ref=dst, send_sem=send_sems.at[b, k],
                recv_sem=recv_sems.at[b, k], device_id=to, device_id_type=MESH)

        mine = [pltpu.make_async_copy(ins[b], outs[b].at[slot(me)], local_sems.at[b]) for b in range(n)]
        first, passed = [], []
        for b in range(n):
            mine[b].start()
            first.append(copy(b, 0, me, sibling, src=ins[b]))
            first += [copy(b, 1 + j, me, (*chip, c), src=ins[b]) for j, chip in enumerate(chips)]
        for cp in first:
            cp.start()
        for j, chip in enumerate(chips):
            for b in range(n):
                copy(b, 1 + j, (*chip, c), me).wait_recv()
                fwd = copy(b, 4 + j, (*chip, c), sibling)
                fwd.start()
                passed.append(fwd)
        for b in range(n):
            copy(b, 0, sibling, me).wait_recv()
            for j, chip in enumerate(chips):
                copy(b, 4 + j, (*chip, 1 - c), me).wait_recv()
        for cp in first + passed:
            cp.wait_send()
        for cp in mine:
            cp.wait()

    return pl.pallas_call(
        body, name=name,
        in_specs=[ANY] * n, out_specs=[ANY] * n,
        out_shape=[jax.ShapeDtypeStruct((N_DEV,) + s.shape, s.dtype) for s in shards],
        scratch_shapes=[pltpu.SemaphoreType.DMA((n, 7)), pltpu.SemaphoreType.DMA((n, 7)),
                        pltpu.SemaphoreType.DMA((n,))],
    )(*shards)


HBM = pl.BlockSpec(memory_space=pltpu.HBM)
SEM = pl.BlockSpec(memory_space=pltpu.SEMAPHORE)
EFFECT = pltpu.SideEffectType.DATAFLOW_SIDE_EFFECTING
FLIPS = ((0, 0, 1), (0, 1, 0), (0, 1, 1), (1, 0, 0), (1, 0, 1), (1, 1, 0), (1, 1, 1))


def _flip_peers():
    pos = (lax.axis_index("x"), lax.axis_index("y"), lax.axis_index("c"))
    return [tuple(1 - a if f else a for a, f in zip(pos, flip)) for flip in FLIPS]


def _hbm(a):
    return pltpu.with_memory_space_constraint(a, pltpu.HBM)


def _own_copies(own, src_refs, land_refs, send_sems, n_remote):
    return [pltpu.make_async_copy(src, dst, send_sems.at[n_remote + i])
            for i, (src, dst) in enumerate(own(src_refs, land_refs) if own else [])]


def _split_start(name, srcs, lands, plan, n_copies, after, own=None):
    n, m = len(srcs), len(lands)

    def body(*refs):
        send_sems, recv_sems, token = refs[n + m + 1], refs[n + m + 2], refs[-1]
        for i, (src, dst, peer) in enumerate(plan(refs[:n], refs[n:n + m])):
            pltpu.make_async_remote_copy(src_ref=src, dst_ref=dst, send_sem=send_sems.at[i], recv_sem=recv_sems.at[i],
                                         device_id=peer, device_id_type=MESH).start()
        for copy in _own_copies(own, refs[:n], refs[n:n + m], send_sems, n_copies):
            copy.start()
        token[...] = jnp.zeros_like(token)

    outs = pl.pallas_call(
        body, name=name + "_start",
        in_specs=[HBM] * (n + m) + [ANY],
        out_specs=[SEM, SEM] + [HBM] * (n + m) + [pl.BlockSpec(memory_space=pltpu.VMEM)],
        out_shape=[pltpu.SemaphoreType.DMA((n_copies + (n if own else 0),)), pltpu.SemaphoreType.DMA((n_copies,))]
        + [pltpu.HBM(a.shape, a.dtype) for a in list(srcs) + list(lands)] + [jax.ShapeDtypeStruct((1, D_MODEL), F32)],
        input_output_aliases={i: 2 + i for i in range(n + m)},
        compiler_params=pltpu.CompilerParams(has_side_effects=EFFECT),
    )(*[_hbm(a) for a in list(srcs) + list(lands)], after)
    return (outs[0], outs[1], outs[2:2 + n], outs[2 + n:2 + n + m]), outs[-1]


def _split_wait(name, started, plan, after, own=None):
    send_sems, recv_sems, srcs, lands = started
    n, m = len(srcs), len(lands)

    def body(*refs):
        send_ref, recv_ref = refs[n + m], refs[n + m + 1]
        copies = plan(refs[:n], refs[n:n + m])
        for i, (src, dst, peer) in enumerate(copies):
            copy = pltpu.make_async_remote_copy(src_ref=src, dst_ref=dst, send_sem=send_ref.at[i],
                                                recv_sem=recv_ref.at[i], device_id=peer, device_id_type=MESH)
            copy.wait_send()
            copy.wait_recv()
        for copy in _own_copies(own, refs[:n], refs[n:n + m], send_ref, len(copies)):
            copy.wait()

    outs = pl.pallas_call(
        body, name=name + "_wait",
        in_specs=[HBM] * (n + m) + [SEM, SEM, ANY],
        out_specs=[HBM] * (n + m),
        out_shape=[pltpu.HBM(a.shape, a.dtype) for a in list(srcs) + list(lands)],
        input_output_aliases={i: i for i in range(n + m)},
        compiler_params=pltpu.CompilerParams(has_side_effects=EFFECT),
    )(*srcs, *lands, send_sems, recv_sems, after)
    return outs[:n], outs[n:]


def _gather_plan(srcs, lands):
    slot = 4 * lax.axis_index("x") + 2 * lax.axis_index("y") + lax.axis_index("c")
    return [(src, land.at[slot], peer) for src, land in zip(srcs, lands) for peer in _flip_peers()]


def _own_slot(srcs, lands):
    slot = 4 * lax.axis_index("x") + 2 * lax.axis_index("y") + lax.axis_index("c")
    return [(src, land.at[slot]) for src, land in zip(srcs, lands)]


def _sibling_plan(srcs, lands):
    x, y, c = lax.axis_index("x"), lax.axis_index("y"), lax.axis_index("c")
    return [(src.at[k, 1 - c], land.at[k], (x, y, 1 - c)) for src, land in zip(srcs, lands) for k in range(N_CHIP)]


def _chip_plan(srcs, lands):
    x, y, c = lax.axis_index("x"), lax.axis_index("y"), lax.axis_index("c")
    return [(src.at[2 * cx + cy], land.at[2 * x + y], (cx, cy, c))
            for src, land in zip(srcs, lands) for cx, cy in ((1 - x, y), (x, 1 - y), (1 - x, 1 - y))]


def _row_tile(rows):
    for tr in range(min(rows, 512), 15, -16):
        if rows % tr == 0:
            return tr
    return rows


def _pair_sums(gs, lands, core, name):
    n = len(gs)

    def body(c_ref, *refs):
        for b in range(n):
            out = refs[2 * n + b]
            out[...] = (refs[b][...].astype(F32) + refs[n + b][...].astype(F32)).astype(out.dtype)

    def slab(a):
        return pl.BlockSpec((None,) + a.shape[1:], lambda k, c_ref: (k, 0, 0))

    return pl.pallas_call(
        body, name=name,
        grid_spec=pltpu.PrefetchScalarGridSpec(
            num_scalar_prefetch=1, grid=(N_CHIP,),
            in_specs=[pl.BlockSpec((None, None) + g.shape[2:], lambda k, c_ref: (k, c_ref[0], 0, 0)) for g in gs]
            + [slab(a) for a in lands],
            out_specs=[slab(a) for a in lands]),
        out_shape=[jax.ShapeDtypeStruct(a.shape, a.dtype) for a in lands],
        compiler_params=_cparams("parallel"),
    )(core, *gs, *lands)


def _adamw(own, arrived, chip, w, m, v, name):
    k, rows, cols = arrived.shape
    tr = _row_tile(rows)
    c1 = 1.0 / (1.0 - ADAM_B1 ** ADAM_STEP)
    c2 = 1.0 / (1.0 - ADAM_B2 ** ADAM_STEP)

    def body(chip_ref, o_ref, p_ref, w_ref, m_ref, v_ref, g_ref, d_ref, nm_ref, nv_ref):
        def slab(j):
            return jnp.where(chip_ref[0] == j, o_ref[j], p_ref[j]).astype(F32)

        g = slab(0)
        for j in range(1, k):
            g = g + slab(j)
        g_ref[...] = g
        nm = ADAM_B1 * m_ref[...] + (1.0 - ADAM_B1) * g
        nv = ADAM_B2 * v_ref[...] + (1.0 - ADAM_B2) * (g * g)
        nm_ref[...] = nm
        nv_ref[...] = nv
        d_ref[...] = -ADAM_LR * ((nm * c1) / (jnp.sqrt(nv * c2) + ADAM_EPS) + ADAM_WD * w_ref[...])

    blk = pl.BlockSpec((tr, cols), lambda i, c: (i, 0))
    stack = pl.BlockSpec((k, tr, cols), lambda i, c: (0, i, 0))
    return pl.pallas_call(
        body, name=name,
        grid_spec=pltpu.PrefetchScalarGridSpec(num_scalar_prefetch=1, grid=(rows // tr,),
                                               in_specs=[stack, stack, blk, blk, blk], out_specs=[blk] * 4),
        out_shape=[jax.ShapeDtypeStruct((rows, cols), F32)] * 4,
        compiler_params=_cparams("parallel"),
    )(chip, own, arrived, w, m, v)


SMALL_LAYOUT = (("g_mix", 0, 1024), ("conv_b", 1, 512), ("q_norm_g", 2, 64), ("k_norm_g", 3, 64),
                ("g_out_conv", 4, 512), ("g_out_attn", 5, 512), ("g_ffn", 6, 1024), ("ffn_conv_b", 7, 2816),
                ("g_ple", 10, 1024))
CONV_W_ROW = 11
FFN_CONV_W_ROW = 14
LOSS_ROW = 23


def _row_pieces(cols):
    return [(c, min(1024, cols - c)) for c in range(0, cols, 1024)]


def _pack_small(grads, loss_tile):
    names = [n for n, _, _ in SMALL_LAYOUT]

    def body(*refs):
        ins, cw_ref, fcw_ref, loss_ref, out_ref = refs[:len(names)], refs[-4], refs[-3], refs[-2], refs[-1]
        out_ref[...] = jnp.zeros_like(out_ref)
        for ref, (_, row, cols) in zip(ins, SMALL_LAYOUT):
            if ref.shape[1] == ATTN_W and cols == HEAD_DIM:
                out_ref[row:row + 1, 0:cols] = sum(ref[:, h:h + cols] for h in range(0, ATTN_W, cols))
                continue
            for j, (c, width) in enumerate(_row_pieces(cols)):
                out_ref[row + j:row + j + 1, 0:width] = ref[:, c:c + width]
        for k in range(3):
            out_ref[CONV_W_ROW + k:CONV_W_ROW + k + 1, 0:CONV_W] = cw_ref[k:k + 1, :]
            for j, (c, width) in enumerate(_row_pieces(D_FF)):
                row = FFN_CONV_W_ROW + 3 * k + j
                out_ref[row:row + 1, 0:width] = fcw_ref[k:k + 1, c:c + width]
        out_ref[LOSS_ROW:LOSS_ROW + 1, 0:128] = loss_ref[0:1, :]

    return pl.pallas_call(
        body, name="pack_small_grads", out_shape=jax.ShapeDtypeStruct((SMALL_ROWS, 1024), F32),
    )(*[grads[n] for n in names], grads["conv_w"], grads["ffn_conv_w"], loss_tile)


def _adamw_small(arrived, conv_parts, fconv_parts, wts, mom, var):
    names = [n for n, _, _ in SMALL_LAYOUT] + ["conv_w", "ffn_conv_w"]
    c1 = 1.0 / (1.0 - ADAM_B1 ** ADAM_STEP)
    c2 = 1.0 / (1.0 - ADAM_B2 ** ADAM_STEP)
    n = len(names)

    def body(*refs):
        land, cw_ref, fcw_ref = refs[0], refs[1], refs[2]
        state = refs[3:3 + 3 * n]
        outs = refs[3 + 3 * n:]

        def total(piece):
            acc = piece(0)
            for d in range(1, N_DEV):
                acc = acc + piece(d)
            return acc

        for i, name in enumerate(names):
            if name == "conv_w":
                g = total(lambda d: cw_ref[d])
            elif name == "ffn_conv_w":
                g = total(lambda d: fcw_ref[d])
            else:
                _, row, cols = SMALL_LAYOUT[i]
                pieces = [total(lambda d, j=j, width=width: land[d, row + j:row + j + 1, 0:width])
                          for j, (_, width) in enumerate(_row_pieces(cols))]
                g = pieces[0] if len(pieces) == 1 else jnp.concatenate(pieces, axis=1)
            w_ref, m_ref, v_ref = state[3 * i:3 * i + 3]
            nm = ADAM_B1 * m_ref[...] + (1.0 - ADAM_B1) * g
            nv = ADAM_B2 * v_ref[...] + (1.0 - ADAM_B2) * (g * g)
            outs[4 * i][...] = g
            outs[4 * i + 1][...] = -ADAM_LR * ((nm * c1) / (jnp.sqrt(nv * c2) + ADAM_EPS) + ADAM_WD * w_ref[...])
            outs[4 * i + 2][...] = nm
            outs[4 * i + 3][...] = nv
        outs[-1][...] = total(lambda d: land[d, LOSS_ROW:LOSS_ROW + 1, 0:128])

    state = [a[nm_] for nm_ in names for a in (wts, mom, var)]
    shapes = [jax.ShapeDtypeStruct(wts[nm_].shape, F32) for nm_ in names for _ in range(4)]
    outs = pl.pallas_call(
        body, name="adamw_small", out_shape=shapes + [jax.ShapeDtypeStruct((1, 128), F32)],
    )(arrived, conv_parts, fconv_parts, *state)
    return {nm_: tuple(outs[4 * i:4 * i + 4]) for i, nm_ in enumerate(names)}, outs[-1][0, 0]


COL_SHARDED = ("w_in", "w_ple_proj")
TRANSPOSED = ("w_gate", "w_up")
CONV_SHARDED = (("conv_w", CONV_W), ("ffn_conv_w", D_FF))


def _gathered_to_full(name, gathered):
    if name in COL_SHARDED:
        return gathered.transpose(1, 0, 2).reshape(gathered.shape[1], -1)
    return gathered.reshape(-1, gathered.shape[2])


def _full_to_stacked(name, grad, shard_shape):
    sr, sc = shard_shape
    if grad.ndim == 3:
        a = grad
    elif name in COL_SHARDED:
        a = grad.reshape(sr, N_DEV, sc).transpose(1, 0, 2)
    else:
        a = grad.reshape(N_DEV, sr, sc)
    return a.astype(BF16).reshape(N_CHIP, 2, sr, sc)


def _pad_rows(vec, rows):
    return jnp.pad(vec, (0, rows * 1024 - vec.shape[0])).reshape(rows, 1024)


def kernel(x, p, g_mix, w_in, conv_w, conv_b, q_norm_g, k_norm_g, g_out_conv, g_out_attn, w_out, g_ffn, w_gate, w_up, ffn_conv_w, ffn_conv_b, w_down, g_ple, w_ple_gate, w_ple_proj, loss_target, m_g_mix, m_w_in, m_conv_w, m_conv_b, m_q_norm_g, m_k_norm_g, m_g_out_conv, m_g_out_attn, m_w_out, m_g_ffn, m_w_gate, m_w_up, m_ffn_conv_w, m_ffn_conv_b, m_w_down, m_g_ple, m_w_ple_gate, m_w_ple_proj, v_g_mix, v_w_in, v_conv_w, v_conv_b, v_q_norm_g, v_k_norm_g, v_g_out_conv, v_g_out_attn, v_w_out, v_g_ffn, v_w_gate, v_w_up, v_ffn_conv_w, v_ffn_conv_b, v_w_down, v_g_ple, v_w_ple_gate, v_w_ple_proj):
    args = dict(locals())
    names = ["g_mix", "w_in", "conv_w", "conv_b", "q_norm_g", "k_norm_g", "g_out_conv", "g_out_attn", "w_out", "g_ffn",
             "w_gate", "w_up", "ffn_conv_w", "ffn_conv_b", "w_down", "g_ple", "w_ple_gate", "w_ple_proj"]
    big = list(BIG)
    conv = [n for n, _ in CONV_SHARDED]

    def local(prefix):
        out = {n: (args[prefix + n][0] if n in big or n in conv else args[prefix + n]) for n in names}
        out.update({n: out[n].T for n in TRANSPOSED})
        return out

    wts, mom, var = local(""), local("m_"), local("v_")
    shard_shapes = {n: wts[n].shape for n in big}
    dev = 4 * lax.axis_index("x") + 2 * lax.axis_index("y") + lax.axis_index("c")
    core = lax.axis_index("c").astype(jnp.int32).reshape(1)

    conv_local = _pad_rows(jnp.concatenate([wts[n].reshape(-1) for n in conv]), 8).reshape(8, 1024)
    late = [n for n in big if n != "w_in"]
    w_in_all, conv_all = _all_gather([wts["w_in"].astype(BF16), conv_local], "gather_weights")
    late_shards = [wts[n].astype(BF16) for n in late]
    gathering, token = _split_start("gather_late_weights", late_shards,
                                    [lax.empty((N_DEV,) + s.shape, BF16) for s in late_shards], _gather_plan,
                                    7 * len(late), w_in_all, own=_own_slot)
    full = dict(wts)
    full["w_in"] = _gathered_to_full("w_in", w_in_all)
    full["g_mix"] = _ordered_after(wts["g_mix"], token)
    flying = {}

    def late_weights(after):
        _, lands = _split_wait("gather_late_weights", gathering, _gather_plan, after, own=_own_slot)
        return {n: _gathered_to_full(n, land) for n, land in zip(late, lands)}

    early = ["w_ple_gate", "w_ple_proj", "w_down", "w_up", "w_gate"]

    def ffn_grads(g):
        stacked = [_full_to_stacked(n, g[n], shard_shapes[n]) for n in early]
        flying["sibling"], tok = _split_start("rs_sibling_early", stacked,
                                              [lax.empty((N_CHIP,) + s.shape[2:], BF16) for s in stacked],
                                              _sibling_plan, N_CHIP * len(early), g["ffn_conv_b"])
        return tok

    def outproj_done(after):
        stacked, landed = _split_wait("rs_sibling_early", flying["sibling"], _sibling_plan, after)
        parts = _pair_sums(stacked, landed, core, "rs_pair_sums_early")
        flying["chip"], tok = _split_start("rs_chip_early", parts, [lax.empty(q.shape, BF16) for q in parts],
                                           _chip_plan, 3 * len(early), landed[0])
        return tok

    off = 0
    for n, width in CONV_SHARDED:
        sc = width // N_DEV
        a = conv_all.reshape(N_DEV, -1)[:, off:off + 3 * sc].reshape(N_DEV, 3, sc)
        full[n] = a.transpose(1, 0, 2).reshape(3, width)
        off += 3 * sc

    loss, dx, grads = _local_step(x[0], p[0, 0], loss_target[0], full, (512, 1024),
                                  {"late_weights": late_weights, "ffn_grads": ffn_grads, "outproj_done": outproj_done})

    chip = (2 * lax.axis_index("x") + lax.axis_index("y")).astype(jnp.int32).reshape(1)

    def adamw_of(group, parts, arrived):
        return {n: _adamw(own, got, chip, wts[n], mom[n], var[n], f"adamw_{n}")
                for n, own, got in zip(group, parts, arrived)}

    last = [n for n in big if n not in early]
    stacked = [_full_to_stacked(n, grads[n], shard_shapes[n]) for n in last]
    flying["sibling_last"], tok = _split_start("rs_sibling_last", stacked,
                                               [lax.empty((N_CHIP,) + s.shape[2:], BF16) for s in stacked],
                                               _sibling_plan, N_CHIP * len(last), dx)
    packed = _pack_small(grads, loss)
    flying["small"], tok = _split_start("gather_small_grads", [packed], [lax.empty((N_DEV,) + packed.shape, F32)],
                                        _gather_plan, N_DEV - 1, tok, own=_own_slot)
    stacked, landed = _split_wait("rs_sibling_last", flying["sibling_last"], _sibling_plan, tok)
    parts = _pair_sums(stacked, landed, core, "rs_pair_sums_last")
    flying["chip_last"], tok = _split_start("rs_chip_last", parts, [lax.empty(q.shape, BF16) for q in parts],
                                            _chip_plan, 3 * len(last), landed[0])

    parts, arrived = _split_wait("rs_chip_early", flying["chip"], _chip_plan, tok)
    out = adamw_of(early, parts, arrived)
    _, (small_all,) = _split_wait("gather_small_grads", flying["small"], _gather_plan, out[early[-1]][0],
                                  own=_own_slot)
    taps = small_all[:, CONV_W_ROW:CONV_W_ROW + 3, 0:CONV_W]
    ftaps = small_all[:, FFN_CONV_W_ROW:FFN_CONV_W_ROW + 9, :].reshape(N_DEV, 3, 3 * 1024)
    small_out, loss_total = _adamw_small(
        small_all, lax.dynamic_slice(taps, (0, 0, dev * (CONV_W // N_DEV)), (N_DEV, 3, CONV_W // N_DEV)),
        lax.dynamic_slice(ftaps, (0, 0, dev * (D_FF // N_DEV)), (N_DEV, 3, D_FF // N_DEV)), wts, mom, var)
    out.update(small_out)
    parts, arrived = _split_wait("rs_chip_last", flying["chip_last"], _chip_plan, small_out["g_mix"][0])
    out.update(adamw_of(last, parts, arrived))
    def result(n, which):
        a = out[n][which]
        return (a.T if n in TRANSPOSED else a).reshape(args[n].shape)

    return (loss_total, dx[None], *[result(n, which) for which in range(4) for n in names])
```

```python
import jax
import jax.numpy as jnp
import numpy as np
from jax import lax
from jax.experimental import pallas as pl
from jax.experimental.pallas import tpu as pltpu

F32 = jnp.float32
BF16 = jnp.bfloat16

D_MODEL = 1024
CONV_W = 512
ATTN_W = 512
HEAD_DIM = 64
D_FF = 2816
PLE_DIM = 256
IN_COLS = 3 * CONV_W + 3 * ATTN_W
EPS = 1e-6
QK_BLOCK = 128
DILATIONS = (1, 4, 16)
ATTN_SCALE = HEAD_DIM ** -0.5

ADAM_LR = 0.001
ADAM_B1 = 0.9
ADAM_B2 = 0.999
ADAM_EPS = 1e-08
ADAM_WD = 0.01
ADAM_STEP = 10

N_DEV = 8
N_CHIP = 4
V7X_VMEM_LIMIT = 56 * 1024 * 1024
V7X_VMEM_LIMIT_LARGE = 62 * 1024 * 1024
FF_CHUNKS = 2
FFN_BWD_PARTS = 1

BIG = ("w_in", "w_out", "w_gate", "w_up", "w_down", "w_ple_gate", "w_ple_proj")
SMALL_ROWS = 24


def _cparams(*sem, vmem=V7X_VMEM_LIMIT):
    return pltpu.CompilerParams(dimension_semantics=sem, vmem_limit_bytes=vmem)


def _mm(a, b):
    return jnp.dot(a, b, preferred_element_type=F32)


def _mm_nt(a, b):
    return lax.dot_general(a, b, (((1,), (1,)), ((), ())), preferred_element_type=F32)


def _mm_tn(a, b):
    return lax.dot_general(a, b, (((0,), (0,)), ((), ())), preferred_element_type=F32)


def _full(shape):
    nd = len(shape)
    return pl.BlockSpec(shape, lambda *_: (0,) * nd)


def _rms_stats(x):
    r = lax.rsqrt(jnp.mean(x * x, axis=-1, keepdims=True) + EPS)
    return r, x * r


def _rms_bwd(dy, xhat, r, g):
    gd = dy * g
    return r * (gd - xhat * jnp.mean(gd * xhat, axis=-1, keepdims=True))


def _seg_sum64(v, bd_ref):
    outs = []
    for c in range(0, v.shape[1], 256):
        vc = v[:, c:c + 256]
        hi = vc.astype(BF16)
        lo = (vc - hi.astype(F32)).astype(BF16)
        outs.append(_mm(hi, bd_ref[...]) + _mm(lo, bd_ref[...]))
    return outs[0] if len(outs) == 1 else jnp.concatenate(outs, axis=1)


def _shift_rows(u, k, edge_rows):
    out = pltpu.roll(u, k, 0)
    row = lax.broadcasted_iota(jnp.int32, (8, u.shape[1]), 0)
    head = out[0:8]
    for j in range(k):
        head = jnp.where(row == j, edge_rows[k - 1 - j], head)
    return jnp.concatenate([head, out[8:]], axis=0)


def _shift_rows_up(u, k, edge_rows):
    n = u.shape[0]
    out = pltpu.roll(u, n - k, 0)
    row = lax.broadcasted_iota(jnp.int32, (8, u.shape[1]), 0)
    tail = out[n - 8:n]
    for j in range(k):
        tail = jnp.where(row == 8 - k + j, edge_rows[j], tail)
    return jnp.concatenate([out[0:n - 8], tail], axis=0)


def _conv_fwd(u, c1, c2, w_ref, b_ref):
    u1 = _shift_rows(u, 1, (c1,))
    u2 = _shift_rows(u, 2, (c1, c2))
    y = u2 * w_ref[0:1, :] + u1 * w_ref[1:2, :] + u * w_ref[2:3, :] + b_ref[...]
    return y, u1, u2


def _conv_bwd_input(dy, n1row, n2row, w_ref):
    d1 = _shift_rows_up(dy, 1, (n1row,))
    d2 = _shift_rows_up(dy, 2, (n1row, n2row))
    return dy * w_ref[2:3, :] + d1 * w_ref[1:2, :] + d2 * w_ref[0:1, :]


def _sigmoid(x):
    return 1.0 / (1.0 + jnp.exp(-x))


def _inproj_fwd(x, g_mix, w_in, conv_w, conv_b, qg, kg, bd, tm):
    t = x.shape[0]

    def body(x_ref, g_ref, w_ref, cw_ref, cb_ref, qg_ref, kg_ref, bd_ref,
             zc_ref, zqk_ref, yc_ref, q_ref, k_ref, v_ref, carry_ref):
        @pl.when(pl.program_id(0) == 0)
        def _():
            carry_ref[...] = jnp.zeros_like(carry_ref)

        _, xhat = _rms_stats(x_ref[...])
        h = (xhat * g_ref[...]).astype(BF16)
        zconv = _mm(h, w_ref[:, 0:3 * CONV_W])
        zc_ref[...] = zconv.astype(BF16)
        u = zconv[:, CONV_W:2 * CONV_W] * zconv[:, 2 * CONV_W:3 * CONV_W]
        cv, _, _ = _conv_fwd(u, carry_ref[7:8, :], carry_ref[6:7, :], cw_ref, cb_ref)
        yc_ref[...] = (zconv[:, 0:CONV_W] * cv).astype(BF16)
        carry_ref[...] = u[tm - 8:tm, :]

        zqk = _mm(h, w_ref[:, 3 * CONV_W:3 * CONV_W + 2 * ATTN_W])
        zqk_ref[...] = zqk.astype(BF16)
        for j, (gain_ref, out_ref, scale) in enumerate(((qg_ref, q_ref, ATTN_SCALE), (kg_ref, k_ref, 1.0))):
            z = zqk[:, j * ATTN_W:(j + 1) * ATTN_W]
            r = lax.rsqrt(_seg_sum64(z * z, bd_ref) * (1.0 / HEAD_DIM) + EPS)
            out_ref[...] = z * r * gain_ref[...] * scale
        v_ref[...] = _mm(h, w_ref[:, 3 * CONV_W + 2 * ATTN_W:IN_COLS])

    def blk(c):
        return pl.BlockSpec((tm, c), lambda i: (i, 0))

    return pl.pallas_call(
        body, name="inproj_fwd", grid=(t // tm,),
        in_specs=[blk(D_MODEL), _full((1, D_MODEL)), _full((D_MODEL, IN_COLS)), _full((3, CONV_W)),
                  _full((1, CONV_W)), _full((1, ATTN_W)), _full((1, ATTN_W)), _full((256, 256))],
        out_specs=[blk(3 * CONV_W), blk(2 * ATTN_W), blk(CONV_W), blk(ATTN_W), blk(ATTN_W), blk(ATTN_W)],
        out_shape=[jax.ShapeDtypeStruct((t, 3 * CONV_W), BF16), jax.ShapeDtypeStruct((t, 2 * ATTN_W), BF16),
                   jax.ShapeDtypeStruct((t, CONV_W), BF16), jax.ShapeDtypeStruct((t, ATTN_W), F32),
                   jax.ShapeDtypeStruct((t, ATTN_W), F32), jax.ShapeDtypeStruct((t, ATTN_W), F32)],
        scratch_shapes=[pltpu.VMEM((8, CONV_W), F32)],
        compiler_params=_cparams("arbitrary"),
    )(x, g_mix, w_in, conv_w, conv_b, qg, kg, bd)


SUPER = 16 * QK_BLOCK
KEYS = 2 * QK_BLOCK
UNITS = SUPER // QK_BLOCK


def _rows(start, size, dil):
    return pl.ds(start, size) if dil == 1 else pl.ds(start, size, stride=dil)


def _attn_bias(sl_ref, dil):
    qi = lax.broadcasted_iota(jnp.int32, (KEYS, KEYS), 0)
    kj = lax.broadcasted_iota(jnp.int32, (KEYS, KEYS), 1)
    step = jnp.bitwise_and(qi, QK_BLOCK - 1) + QK_BLOCK - kj
    slope = jnp.where(qi < QK_BLOCK, sl_ref[0, 0:1, 0:1], sl_ref[0, 1:2, 0:1])
    bias = jnp.where(jnp.logical_and(step >= 0, step <= QK_BLOCK), -slope * (step * dil).astype(F32), -jnp.inf)
    return bias, kj >= QK_BLOCK


def _unit_start(u, dil):
    if dil == 1:
        return pl.multiple_of(u * QK_BLOCK, QK_BLOCK)
    if dil == 4:
        return jnp.bitwise_and(u, 3) + (u // 4) * (4 * QK_BLOCK)
    return u


def _stack_heads(a, head0):
    zero = jnp.zeros_like(a)
    return jnp.concatenate([jnp.where(head0, a, zero), jnp.where(head0, zero, a)], axis=0)


def _attn_fwd(q, k, v, slopes):
    t = q.shape[0]
    nsb = t // SUPER

    def body(q_ref, kc_ref, kp_ref, vc_ref, vp_ref, sl_ref, o_ref, l_ref, e_ref, m_ref, kk, vv, ob, lb):
        s = pl.program_id(1)
        kk[0:SUPER, :] = kp_ref[...]
        kk[SUPER:, :] = kc_ref[...]
        vv[0:SUPER, :] = vp_ref[...]
        vv[SUPER:, :] = vc_ref[...]
        head0 = lax.broadcasted_iota(jnp.int32, (QK_BLOCK, QK_BLOCK), 1) < HEAD_DIM

        for b, dil in enumerate(DILATIONS):
            bias, own_half = _attn_bias(sl_ref, dil)

            def unit(u, carry, b=b, dil=dil, bias=bias, own_half=own_half):
                start = _unit_start(u, dil)
                first_key = SUPER + start - QK_BLOCK * dil
                q2 = _stack_heads(q_ref[_rows(start, QK_BLOCK, dil), :].astype(BF16), head0)
                k2 = kk[_rows(first_key, KEYS, dil), :].astype(BF16)
                v2 = vv[_rows(first_key, KEYS, dil), :].astype(BF16)
                has_prev = jnp.logical_or(s > 0, start >= QK_BLOCK * dil)
                sc = jnp.where(jnp.logical_or(own_half, has_prev), _mm_nt(q2, k2) + bias, -jnp.inf)
                m = jnp.max(sc, axis=-1, keepdims=True)
                e = jnp.exp(sc - m)
                den = jnp.sum(e, axis=-1, keepdims=True)
                eb = e.astype(BF16)
                e_ref[b * UNITS + u] = eb
                o2 = _mm(eb, v2) / den
                l2 = m + jnp.log(den)
                ob[b, _rows(start, QK_BLOCK, dil), :] = jnp.where(head0, o2[0:QK_BLOCK], o2[QK_BLOCK:])
                lb[b, _rows(start, QK_BLOCK, dil), :] = jnp.where(head0, l2[0:QK_BLOCK], l2[QK_BLOCK:])
                m_ref[b, _rows(start, QK_BLOCK, dil), :] = jnp.where(head0, m[0:QK_BLOCK], m[QK_BLOCK:])
                return carry

            lax.fori_loop(0, UNITS, unit, 0, unroll=16)

        def merge(i, carry):
            rows = pl.ds(pl.multiple_of(i * 256, 256), 256)
            la, lb_, lc = lb[0, rows, :], lb[1, rows, :], lb[2, rows, :]
            mx = jnp.maximum(jnp.maximum(la, lb_), lc)
            wa, wb, wc = jnp.exp(la - mx), jnp.exp(lb_ - mx), jnp.exp(lc - mx)
            sw = wa + wb + wc
            o_ref[rows, :] = ((wa * ob[0, rows, :] + wb * ob[1, rows, :] + wc * ob[2, rows, :]) / sw).astype(BF16)
            l_ref[rows, :] = mx + jnp.log(sw)
            return carry

        lax.fori_loop(0, SUPER // 256, merge, 0)

    cur = pl.BlockSpec((SUPER, QK_BLOCK), lambda p, s: (s, p))
    prev = pl.BlockSpec((SUPER, QK_BLOCK), lambda p, s: (jnp.maximum(s - 1, 0), p))
    return pl.pallas_call(
        body, name="attn_fwd", grid=(4, nsb),
        in_specs=[cur, cur, prev, cur, prev, pl.BlockSpec((1, 2, QK_BLOCK), lambda p, s: (p, 0, 0))],
        out_specs=[cur, cur, pl.BlockSpec((None, None, 3 * UNITS, KEYS, KEYS), lambda p, s: (p, s, 0, 0, 0)),
                   pl.BlockSpec((3, SUPER, QK_BLOCK), lambda p, s: (0, s, p))],
        out_shape=[jax.ShapeDtypeStruct((t, ATTN_W), BF16), jax.ShapeDtypeStruct((t, ATTN_W), F32),
                   jax.ShapeDtypeStruct((4, nsb, 3 * UNITS, KEYS, KEYS), BF16),
                   jax.ShapeDtypeStruct((3, t, ATTN_W), F32)],
        scratch_shapes=[pltpu.VMEM((2 * SUPER, QK_BLOCK), F32), pltpu.VMEM((2 * SUPER, QK_BLOCK), F32),
                        pltpu.VMEM((3, SUPER, QK_BLOCK), F32), pltpu.VMEM((3, SUPER, QK_BLOCK), F32)],
        compiler_params=_cparams("parallel", "arbitrary"),
    )(q, k, k, v, v, slopes)


def _outproj_fwd(ya, yc, x, goc, goa, w_out, tm):
    t = x.shape[0]

    def body(ya_ref, yc_ref, x_ref, goc_ref, goa_ref, w_ref, x1_ref):
        _, ychat = _rms_stats(yc_ref[...].astype(F32))
        _, yahat = _rms_stats(ya_ref[...].astype(F32))
        acc = _mm((ychat * goc_ref[...]).astype(BF16), w_ref[0:CONV_W, :])
        acc += _mm((yahat * goa_ref[...]).astype(BF16), w_ref[CONV_W:, :])
        x1_ref[...] = x_ref[...] + acc

    def blk(c):
        return pl.BlockSpec((tm, c), lambda i: (i, 0))

    return pl.pallas_call(
        body, name="outproj_fwd", grid=(t // tm,),
        in_specs=[blk(ATTN_W), blk(CONV_W), blk(D_MODEL), _full((1, CONV_W)), _full((1, ATTN_W)),
                  _full((D_MODEL, D_MODEL))],
        out_specs=blk(D_MODEL),
        out_shape=jax.ShapeDtypeStruct((t, D_MODEL), F32),
        compiler_params=_cparams("parallel"),
    )(ya, yc, x, goc, goa, w_out)


def _ffn_fwd(x1, g_ffn, w_gate_t, w_up_t, w_down, fcw, fcb, tm):
    t = x1.shape[0]

    def body(x_ref, g_ref, wg_ref, wu_ref, wd_ref, cw_ref, cb_ref, gp_ref, up_ref, h_ref, x2_ref, carry_ref):
        @pl.when(pl.program_id(0) == 0)
        def _():
            carry_ref[...] = jnp.zeros_like(carry_ref)

        xv = x_ref[...]
        _, xhat = _rms_stats(xv)
        h = (xhat * g_ref[...]).astype(BF16)
        h_ref[...] = h
        gp = _mm_nt(h, wg_ref[...])
        gp_ref[...] = gp.astype(BF16)
        gate, _, _ = _conv_fwd(gp, carry_ref[7:8, :], carry_ref[6:7, :], cw_ref, cb_ref)
        carry_ref[...] = gp[tm - 8:tm, :]
        up = _mm_nt(h, wu_ref[...])
        up_ref[...] = up.astype(BF16)
        a = (gate * _sigmoid(gate) * up).astype(BF16)
        x2_ref[...] = xv + _mm(a, wd_ref[...])

    def blk(c):
        return pl.BlockSpec((tm, c), lambda i: (i, 0))

    def weight():
        return pl.BlockSpec((D_FF, D_MODEL), lambda i: (0, 0), pipeline_mode=pl.Buffered(1))

    return pl.pallas_call(
        body, name="ffn_fwd", grid=(t // tm,),
        in_specs=[blk(D_MODEL), _full((1, D_MODEL)), weight(), weight(), weight(), _full((3, D_FF)), _full((1, D_FF))],
        out_specs=[blk(D_FF), blk(D_FF), blk(D_MODEL), blk(D_MODEL)],
        out_shape=[jax.ShapeDtypeStruct((t, D_FF), BF16), jax.ShapeDtypeStruct((t, D_FF), BF16),
                   jax.ShapeDtypeStruct((t, D_MODEL), BF16), jax.ShapeDtypeStruct((t, D_MODEL), F32)],
        scratch_shapes=[pltpu.VMEM((8, D_FF), F32)],
        compiler_params=_cparams("arbitrary", vmem=V7X_VMEM_LIMIT_LARGE),
    )(x1, g_ffn, w_gate_t, w_up_t, w_down, fcw, fcb)


def _ple_fwd_bwd(x2, p, target, g_ple, w_pg, w_pp, tm):
    t = x2.shape[0]

    def body(x_ref, p_ref, t_ref, g_ref, wg_ref, wp_ref, dx_ref, dxb_ref, loss_ref, dwgb_ref, dwp_ref, dg_ref,
             dwg_ref):
        @pl.when(pl.program_id(0) == 0)
        def _():
            loss_ref[...] = jnp.zeros_like(loss_ref)
            dwg_ref[...] = jnp.zeros_like(dwg_ref)
            dwp_ref[...] = jnp.zeros_like(dwp_ref)
            dg_ref[...] = jnp.zeros_like(dg_ref)

        xv = x_ref[...]
        r, xhat = _rms_stats(xv)
        g = g_ref[...]
        h = (xhat * g).astype(BF16)
        pg = _sigmoid(_mm(h, wg_ref[...]))
        pb = p_ref[...].astype(BF16)
        pp = _mm(pb, wp_ref[...])
        err = xv + pg * pp - t_ref[...]
        loss_ref[...] += 0.5 * jnp.sum(jnp.mean(err * err, axis=-1, keepdims=True))
        dx3 = err * (1.0 / D_MODEL)
        d_pp = (dx3 * pg).astype(BF16)
        d_pre = (dx3 * pp * pg * (1.0 - pg)).astype(BF16)
        dwp_ref[...] += _mm_tn(pb, d_pp)
        dwg_ref[...] += _mm_tn(h, d_pre)
        dh = _mm_nt(d_pre, wg_ref[...])
        dg_ref[...] += jnp.sum(dh * xhat, axis=0, keepdims=True)
        dx2 = dx3 + _rms_bwd(dh, xhat, r, g)
        dx_ref[...] = dx2
        dxb_ref[...] = dx2.astype(BF16)

        @pl.when(pl.program_id(0) == t // tm - 1)
        def _():
            dwgb_ref[...] = dwg_ref[...].astype(BF16)

    def blk(c):
        return pl.BlockSpec((tm, c), lambda i: (i, 0))

    return pl.pallas_call(
        body, name="ple_fwd_bwd", grid=(t // tm,),
        in_specs=[blk(D_MODEL), blk(PLE_DIM), blk(D_MODEL), _full((1, D_MODEL)), _full((D_MODEL, D_MODEL)),
                  _full((PLE_DIM, D_MODEL))],
        out_specs=[blk(D_MODEL), blk(D_MODEL), _full((8, 128)), _full((D_MODEL, D_MODEL)),
                   _full((PLE_DIM, D_MODEL)), _full((1, D_MODEL))],
        out_shape=[jax.ShapeDtypeStruct((t, D_MODEL), F32), jax.ShapeDtypeStruct((t, D_MODEL), BF16),
                   jax.ShapeDtypeStruct((8, 128), F32),
                   jax.ShapeDtypeStruct((D_MODEL, D_MODEL), BF16), jax.ShapeDtypeStruct((PLE_DIM, D_MODEL), F32),
                   jax.ShapeDtypeStruct((1, D_MODEL), F32)],
        scratch_shapes=[pltpu.VMEM((D_MODEL, D_MODEL), F32)],
        compiler_params=_cparams("arbitrary"),
    )(x2, p, target, g_ple, w_pg, w_pp)


def _ffn_bwd(dx2, h2, gp, up, w_gate, w_up, w_down, fcw, fcb, tm):
    t = dx2.shape[0]
    nblk = t // tm
    fc = D_FF // FF_CHUNKS
    half = tm // FFN_BWD_PARTS

    def body(dx_ref, h_ref, gp_ref, gph_ref, up_ref, wg_ref, wu_ref, wd_ref, cw_ref, cb_ref,
             dh_ref, dwd_hbm, dwu_hbm, dwg_hbm, dcw_ref, dcb_ref, carry_ref, a_scr, dup_scr, dgp_scr,
             dwd_acc, dwu_acc, dwg_acc, stage, stage_sem):
        i = pl.program_id(1)

        @pl.when(i == 0)
        def _():
            carry_ref[...] = jnp.zeros_like(carry_ref)
            dwd_acc[...] = jnp.zeros_like(dwd_acc)
            dwu_acc[...] = jnp.zeros_like(dwu_acc)
            dwg_acc[...] = jnp.zeros_like(dwg_acc)
            dcw_ref[...] = jnp.zeros_like(dcw_ref)
            dcb_ref[...] = jnp.zeros_like(dcb_ref)

        keep = (i < nblk - 1).astype(F32)
        later = carry_ref[...]
        for hf in reversed(range(FFN_BWD_PARTS)):
            rows = slice(hf * half, (hf + 1) * half)
            dxb = dx_ref[rows, :]
            gp_v = gp_ref[rows, :].astype(F32)
            if hf > 0:
                before = gp_ref[hf * half - 16:hf * half, :].astype(F32)
            else:
                before = gph_ref[...].astype(F32) * keep
            gate, gp1, gp2 = _conv_fwd(gp_v, before[15:16, :], before[14:15, :], cw_ref, cb_ref)
            s = _sigmoid(gate)
            silu = gate * s
            up_v = up_ref[rows, :].astype(F32)
            da = _mm_nt(dxb, wd_ref[...])
            a_scr[rows, :] = (silu * up_v).astype(BF16)
            d_up = (da * silu).astype(BF16)
            dup_scr[rows, :] = d_up
            d_gate = da * up_v * (s * (1.0 + gate * (1.0 - s)))
            d_gp = _conv_bwd_input(d_gate, later[0:1, :], later[1:2, :], cw_ref).astype(BF16)
            dgp_scr[rows, :] = d_gp
            later = d_gate[0:8, :]
            dcw_ref[0:1, :] += jnp.sum(d_gate * gp2, axis=0, keepdims=True)
            dcw_ref[1:2, :] += jnp.sum(d_gate * gp1, axis=0, keepdims=True)
            dcw_ref[2:3, :] += jnp.sum(d_gate * gp_v, axis=0, keepdims=True)
            dcb_ref[...] += jnp.sum(d_gate, axis=0, keepdims=True)
            dh_ref[rows, :] = (_mm(d_gp, wg_ref[...]) + _mm(d_up, wu_ref[...])).astype(BF16)
        carry_ref[...] = later
        dwd_acc[...] += _mm_tn(a_scr[...], dx_ref[...])
        dwu_acc[...] += _mm_tn(h_ref[...], dup_scr[...])
        dwg_acc[...] += _mm_tn(h_ref[...], dgp_scr[...])

        @pl.when(i == nblk - 1)
        def _():
            rows = pl.ds(pl.multiple_of(pl.program_id(0) * fc, 16), fc)
            for acc, out, flip in ((dwd_acc, dwd_hbm, False), (dwu_acc, dwu_hbm, True), (dwg_acc, dwg_hbm, True)):
                stage[...] = (acc[...].T if flip else acc[...]).astype(BF16)
                copy = pltpu.make_async_copy(stage, out.at[rows, :], stage_sem)
                copy.start()
                copy.wait()

    def rev(i):
        return nblk - 1 - i

    one = pl.Buffered(1)
    in_specs = [
        pl.BlockSpec((tm, D_MODEL), lambda j, i: (rev(i), 0)),
        pl.BlockSpec((tm, D_MODEL), lambda j, i: (rev(i), 0)),
        pl.BlockSpec((tm, fc), lambda j, i: (rev(i), j)),
        pl.BlockSpec((16, fc), lambda j, i: (jnp.maximum(rev(i) * (tm // 16) - 1, 0), j)),
        pl.BlockSpec((tm, fc), lambda j, i: (rev(i), j)),
        pl.BlockSpec((fc, D_MODEL), lambda j, i: (j, 0), pipeline_mode=one),
        pl.BlockSpec((fc, D_MODEL), lambda j, i: (j, 0), pipeline_mode=one),
        pl.BlockSpec((fc, D_MODEL), lambda j, i: (j, 0), pipeline_mode=one),
        pl.BlockSpec((3, fc), lambda j, i: (0, j)),
        pl.BlockSpec((1, fc), lambda j, i: (0, j)),
    ]
    out_specs = [
        pl.BlockSpec((None, tm, D_MODEL), lambda j, i: (j, rev(i), 0)),
        ANY, ANY, ANY,
        pl.BlockSpec((3, fc), lambda j, i: (0, j)),
        pl.BlockSpec((1, fc), lambda j, i: (0, j)),
    ]
    return pl.pallas_call(
        body, name="ffn_bwd", grid=(FF_CHUNKS, nblk), in_specs=in_specs, out_specs=out_specs,
        out_shape=[jax.ShapeDtypeStruct((FF_CHUNKS, t, D_MODEL), BF16), jax.ShapeDtypeStruct((D_FF, D_MODEL), BF16),
                   jax.ShapeDtypeStruct((D_FF, D_MODEL), BF16), jax.ShapeDtypeStruct((D_FF, D_MODEL), BF16),
                   jax.ShapeDtypeStruct((3, D_FF), F32), jax.ShapeDtypeStruct((1, D_FF), F32)],
        scratch_shapes=[pltpu.VMEM((8, fc), F32), pltpu.VMEM((tm, fc), BF16), pltpu.VMEM((tm, fc), BF16),
                        pltpu.VMEM((tm, fc), BF16), pltpu.VMEM((fc, D_MODEL), F32), pltpu.VMEM((D_MODEL, fc), F32),
                        pltpu.VMEM((D_MODEL, fc), F32), pltpu.VMEM((fc, D_MODEL), BF16), pltpu.SemaphoreType.DMA],
        compiler_params=_cparams("arbitrary", "arbitrary", vmem=V7X_VMEM_LIMIT_LARGE),
    )(dx2, h2, gp, gp, up, w_gate, w_up, w_down, fcw, fcb)


def _outproj_bwd(dh2, dx2, x1, g_ffn, w_out, yc, ya, goc, goa, zconv, conv_w, conv_b, bd, tm):
    t = x1.shape[0]
    nblk = t // tm

    def body(dh_ref, dx2_ref, x1_ref, g_ref, w_ref, yc_ref, ya_ref, goc_ref, goa_ref, zc_ref, zch_ref, cw_ref, cb_ref,
             bd_ref, dx1_ref, dya_ref, dd_ref, dzc_ref, dwb_ref, dg_ref, dgoc_ref, dgoa_ref, dcw_ref, dcb_ref,
             carry_ref, dw_ref):
        i = pl.program_id(0)

        @pl.when(i == 0)
        def _():
            carry_ref[...] = jnp.zeros_like(carry_ref)
            for ref in (dw_ref, dg_ref, dgoc_ref, dgoa_ref, dcw_ref, dcb_ref):
                ref[...] = jnp.zeros_like(ref)

        keep = (i < nblk - 1).astype(F32)
        dh2_v = dh_ref[0].astype(F32)
        for j in range(1, FF_CHUNKS):
            dh2_v = dh2_v + dh_ref[j].astype(F32)
        r, xhat = _rms_stats(x1_ref[...])
        dg_ref[...] += jnp.sum(dh2_v * xhat, axis=0, keepdims=True)
        dx1 = dx2_ref[...] + _rms_bwd(dh2_v, xhat, r, g_ref[...])
        dx1_ref[...] = dx1
        dx1b = dx1.astype(BF16)
        dy = _mm_nt(dx1b, w_ref[...])

        yc_v = yc_ref[...].astype(F32)
        rc, ychat = _rms_stats(yc_v)
        dw_ref[0:CONV_W, :] += _mm_tn((ychat * goc_ref[...]).astype(BF16), dx1b)
        dyc = dy[:, 0:CONV_W]
        dgoc_ref[...] += jnp.sum(dyc * ychat, axis=0, keepdims=True)
        d_yc = _rms_bwd(dyc, ychat, rc, goc_ref[...])

        ya_v = ya_ref[...].astype(F32)
        ra, yahat = _rms_stats(ya_v)
        dw_ref[CONV_W:, :] += _mm_tn((yahat * goa_ref[...]).astype(BF16), dx1b)
        dya = dy[:, CONV_W:]
        dgoa_ref[...] += jnp.sum(dya * yahat, axis=0, keepdims=True)
        d_ya = _rms_bwd(dya, yahat, ra, goa_ref[...])
        dya_ref[...] = d_ya
        dd_ref[...] = _seg_sum64(d_ya * ya_v, bd_ref)

        zb = zc_ref[:, 0:CONV_W].astype(F32)
        zc = zc_ref[:, CONV_W:2 * CONV_W].astype(F32)
        zx = zc_ref[:, 2 * CONV_W:3 * CONV_W].astype(F32)
        u = zc * zx
        uh = (zch_ref[:, CONV_W:2 * CONV_W].astype(F32) * zch_ref[:, 2 * CONV_W:3 * CONV_W].astype(F32)) * keep
        cv, u1, u2 = _conv_fwd(u, uh[15:16, :], uh[14:15, :], cw_ref, cb_ref)
        d_cv = d_yc * zb
        d_u = _conv_bwd_input(d_cv, carry_ref[0:1, :], carry_ref[1:2, :], cw_ref)
        carry_ref[...] = d_cv[0:8, :]
        dcw_ref[0:1, :] += jnp.sum(d_cv * u2, axis=0, keepdims=True)
        dcw_ref[1:2, :] += jnp.sum(d_cv * u1, axis=0, keepdims=True)
        dcw_ref[2:3, :] += jnp.sum(d_cv * u, axis=0, keepdims=True)
        dcb_ref[...] += jnp.sum(d_cv, axis=0, keepdims=True)
        dzc_ref[:, 0:CONV_W] = (d_yc * cv).astype(BF16)
        dzc_ref[:, CONV_W:2 * CONV_W] = (d_u * zx).astype(BF16)
        dzc_ref[:, 2 * CONV_W:3 * CONV_W] = (d_u * zc).astype(BF16)

        @pl.when(i == nblk - 1)
        def _():
            dwb_ref[...] = dw_ref[...].astype(BF16)

    def rev(i):
        return nblk - 1 - i

    def blk(c):
        return pl.BlockSpec((tm, c), lambda i: (rev(i), 0))

    in_specs = [
        pl.BlockSpec((FF_CHUNKS, tm, D_MODEL), lambda i: (0, rev(i), 0)),
        blk(D_MODEL), blk(D_MODEL), _full((1, D_MODEL)),
        pl.BlockSpec((D_MODEL, D_MODEL), lambda i: (0, 0), pipeline_mode=pl.Buffered(1)),
        blk(CONV_W), blk(ATTN_W), _full((1, CONV_W)), _full((1, ATTN_W)),
        blk(3 * CONV_W),
        pl.BlockSpec((16, 3 * CONV_W), lambda i: (jnp.maximum(rev(i) * (tm // 16) - 1, 0), 0)),
        _full((3, CONV_W)), _full((1, CONV_W)), _full((256, 256)),
    ]
    out_specs = [blk(D_MODEL), blk(ATTN_W), blk(ATTN_W), blk(3 * CONV_W), _full((D_MODEL, D_MODEL)),
                 _full((1, D_MODEL)), _full((1, CONV_W)), _full((1, ATTN_W)), _full((3, CONV_W)), _full((1, CONV_W))]
    return pl.pallas_call(
        body, name="outproj_bwd", grid=(nblk,), in_specs=in_specs, out_specs=out_specs,
        out_shape=[jax.ShapeDtypeStruct((t, D_MODEL), F32), jax.ShapeDtypeStruct((t, ATTN_W), F32),
                   jax.ShapeDtypeStruct((t, ATTN_W), F32), jax.ShapeDtypeStruct((t, 3 * CONV_W), BF16),
                   jax.ShapeDtypeStruct((D_MODEL, D_MODEL), BF16), jax.ShapeDtypeStruct((1, D_MODEL), F32),
                   jax.ShapeDtypeStruct((1, CONV_W), F32), jax.ShapeDtypeStruct((1, ATTN_W), F32),
                   jax.ShapeDtypeStruct((3, CONV_W), F32), jax.ShapeDtypeStruct((1, CONV_W), F32)],
        scratch_shapes=[pltpu.VMEM((8, CONV_W), F32), pltpu.VMEM((D_MODEL, D_MODEL), F32)],
        compiler_params=_cparams("arbitrary", vmem=V7X_VMEM_LIMIT_LARGE),
    )(dh2, dx2, x1, g_ffn, w_out, yc, ya, goc, goa, zconv, zconv, conv_w, conv_b, bd)


def _attn_bwd(q, k, v, dya, lse, dd, e_all, m_all, after):
    t = q.shape[0]
    nsb = t // SUPER

    def body(q_ref, kc_ref, kp_ref, vc_ref, vp_ref, dy_ref, l_ref, d_ref, e_ref, m_ref, after_ref,
             dq_ref, dk_ref, dv_ref, kk, vv, dkacc, dvacc, dwide):
        s = pl.program_id(1)

        @pl.when(s == 0)
        def _():
            dkacc[...] = jnp.zeros_like(dkacc)
            dvacc[...] = jnp.zeros_like(dvacc)

        dkacc[0:SUPER, :] = dkacc[SUPER:, :]
        dvacc[0:SUPER, :] = dvacc[SUPER:, :]
        dkacc[SUPER:, :] = jnp.zeros((SUPER, QK_BLOCK), F32)
        dvacc[SUPER:, :] = jnp.zeros((SUPER, QK_BLOCK), F32)

        @pl.when(s < nsb)
        def _():
            kk[0:SUPER, :] = kp_ref[...]
            kk[SUPER:, :] = kc_ref[...]
            vv[0:SUPER, :] = vp_ref[...]
            vv[SUPER:, :] = vc_ref[...]
            head0 = lax.broadcasted_iota(jnp.int32, (QK_BLOCK, QK_BLOCK), 1) < HEAD_DIM

            def widened(a):
                other = pltpu.roll(a, HEAD_DIM, 1)
                first = lax.broadcasted_iota(jnp.int32, a.shape, 1) < HEAD_DIM
                return jnp.where(first, a, other), jnp.where(first, other, a)

            def stacked(h0, h1):
                return jnp.concatenate([jnp.concatenate([h0, h0], axis=1), jnp.concatenate([h1, h1], axis=1)], axis=0)

            def widen_dd(i, carry):
                rows = pl.ds(pl.multiple_of(i * 256, 256), 256)
                dwide[0, rows, :], dwide[1, rows, :] = widened(d_ref[rows, :])
                return carry

            lax.fori_loop(0, SUPER // 256, widen_dd, 0)

            for b, dil in enumerate(DILATIONS):
                def unit(u, carry, b=b, dil=dil):
                    start = _unit_start(u, dil)
                    first_key = SUPER + start - QK_BLOCK * dil
                    qrows = _rows(start, QK_BLOCK, dil)
                    krows = _rows(first_key, KEYS, dil)
                    q2 = _stack_heads(q_ref[qrows, :].astype(BF16), head0)
                    dy2 = _stack_heads(dy_ref[qrows, :].astype(BF16), head0)
                    g2 = stacked(*widened(jnp.exp(m_ref[b, qrows, :] - l_ref[qrows, :])))
                    d2 = stacked(dwide[0, qrows, :], dwide[1, qrows, :])
                    k2 = kk[krows, :].astype(BF16)
                    v2 = vv[krows, :].astype(BF16)
                    prob = e_ref[b * UNITS + u].astype(F32) * g2
                    ds = (prob * (_mm_nt(dy2, v2) - d2)).astype(BF16)
                    dvacc[krows, :] += _mm_tn(prob.astype(BF16), dy2)
                    dkacc[krows, :] += _mm_tn(ds, q2)
                    dq2 = _mm(ds, k2)
                    dq = jnp.where(head0, dq2[0:QK_BLOCK], dq2[QK_BLOCK:]) * ATTN_SCALE
                    if b == 0:
                        dq_ref[qrows, :] = dq
                    else:
                        dq_ref[qrows, :] += dq
                    return carry

                lax.fori_loop(0, UNITS, unit, 0, unroll=16)

        dk_ref[...] = dkacc[0:SUPER, :]
        dv_ref[...] = dvacc[0:SUPER, :].astype(BF16)

    def cur_map(p, s):
        return (jnp.minimum(s, nsb - 1), p)

    def prev_map(p, s):
        return (jnp.clip(s - 1, 0, nsb - 1), p)

    cur = pl.BlockSpec((SUPER, QK_BLOCK), cur_map)
    prev = pl.BlockSpec((SUPER, QK_BLOCK), prev_map)
    return pl.pallas_call(
        body, name="attn_bwd", grid=(4, nsb + 1),
        in_specs=[cur, cur, prev, cur, prev, cur, cur, cur,
                  pl.BlockSpec((None, None, 3 * UNITS, KEYS, KEYS), lambda p, s: (p, jnp.minimum(s, nsb - 1), 0, 0, 0)),
                  pl.BlockSpec((3, SUPER, QK_BLOCK), lambda p, s: (0, jnp.minimum(s, nsb - 1), p)),
                  pl.BlockSpec(memory_space=pl.ANY)],
        out_specs=[cur, prev, prev],
        out_shape=[jax.ShapeDtypeStruct((t, ATTN_W), F32), jax.ShapeDtypeStruct((t, ATTN_W), F32),
                   jax.ShapeDtypeStruct((t, ATTN_W), BF16)],
        scratch_shapes=[pltpu.VMEM((2 * SUPER, QK_BLOCK), F32)] * 4 + [pltpu.VMEM((2, SUPER, QK_BLOCK), F32)],
        compiler_params=_cparams("parallel", "arbitrary", vmem=V7X_VMEM_LIMIT_LARGE),
    )(q, k, k, v, v, dya, lse, dd, e_all, m_all, after)


def _inproj_bwd(dq, dk, dv, dzconv, zqk, x, dx1, g_mix, w_in, qg, kg, bd, tm):
    t = x.shape[0]
    nblk = t // tm
    shard = IN_COLS // N_DEV

    def body(dq_ref, dk_ref, dv_ref, dzc_ref, zqk_ref, x_ref, dx1_ref, g_ref, w_ref, qg_ref,
             kg_ref, bd_ref, dx_ref, dw_hbm, dg_ref, dqg_ref, dkg_ref, dw_ref, stage, stage_sem):
        @pl.when(pl.program_id(0) == 0)
        def _():
            for ref in (dw_ref, dg_ref, dqg_ref, dkg_ref):
                ref[...] = jnp.zeros_like(ref)

        parts = [dzc_ref[...]]
        for j, (dn_ref, gain_ref, dgain_ref) in enumerate(((dq_ref, qg_ref, dqg_ref), (dk_ref, kg_ref, dkg_ref))):
            dn = dn_ref[...]
            z = zqk_ref[:, j * ATTN_W:(j + 1) * ATTN_W].astype(F32)
            r = lax.rsqrt(_seg_sum64(z * z, bd_ref) * (1.0 / HEAD_DIM) + EPS)
            zhat = z * r
            dgain_ref[...] += jnp.sum(dn * zhat, axis=0, keepdims=True)
            gd = dn * gain_ref[...]
            parts.append((r * (gd - zhat * (_seg_sum64(gd * zhat, bd_ref) * (1.0 / HEAD_DIM)))).astype(BF16))
        parts.append(dv_ref[...].astype(BF16))
        dz = jnp.concatenate(parts, axis=1)

        r, xhat = _rms_stats(x_ref[...])
        g = g_ref[...]
        dw_ref[...] += _mm_tn((xhat * g).astype(BF16), dz)
        dh = _mm_nt(dz, w_ref[...])
        dg_ref[...] += jnp.sum(dh * xhat, axis=0, keepdims=True)
        dx_ref[...] = dx1_ref[...] + _rms_bwd(dh, xhat, r, g)

        @pl.when(pl.program_id(0) == nblk - 1)
        def _():
            for k in range(N_DEV):
                stage[...] = dw_ref[:, k * shard:(k + 1) * shard].astype(BF16)
                copy = pltpu.make_async_copy(stage, dw_hbm.at[k], stage_sem)
                copy.start()
                copy.wait()

    def blk(c):
        return pl.BlockSpec((tm, c), lambda i: (i, 0))

    return pl.pallas_call(
        body, name="inproj_bwd", grid=(nblk,),
        in_specs=[blk(ATTN_W)] * 3 + [blk(3 * CONV_W), blk(2 * ATTN_W), blk(D_MODEL), blk(D_MODEL), _full((1, D_MODEL)),
                                      _full((D_MODEL, IN_COLS)), _full((1, ATTN_W)), _full((1, ATTN_W)),
                                      _full((256, 256))],
        out_specs=[blk(D_MODEL), ANY, _full((1, D_MODEL)), _full((1, ATTN_W)), _full((1, ATTN_W))],
        out_shape=[jax.ShapeDtypeStruct((t, D_MODEL), F32), jax.ShapeDtypeStruct((N_DEV, D_MODEL, shard), BF16),
                   jax.ShapeDtypeStruct((1, D_MODEL), F32), jax.ShapeDtypeStruct((1, ATTN_W), F32),
                   jax.ShapeDtypeStruct((1, ATTN_W), F32)],
        scratch_shapes=[pltpu.VMEM((D_MODEL, IN_COLS), F32), pltpu.VMEM((D_MODEL, shard), BF16),
                        pltpu.SemaphoreType.DMA],
        compiler_params=_cparams("arbitrary"),
    )(dq, dk, dv, dzconv, zqk, x, dx1, g_mix, w_in, qg, kg, bd)


def _ordered_after(a, token):
    return a if token is None else a + token


def _local_step(x, p, target, w, tms, hooks=None):
    hooks = hooks or {}
    bd = jnp.asarray(np.kron(np.eye(4, dtype=np.float32), np.ones((HEAD_DIM, HEAD_DIM), np.float32)), BF16)
    qg = jnp.tile(w["q_norm_g"], (1, 8))
    kg = jnp.tile(w["k_norm_g"], (1, 8))
    slopes = np.exp2(-np.arange(1, 9, dtype=np.float32))
    slopes = jnp.asarray(np.broadcast_to(slopes.reshape(4, 2, 1), (4, 2, QK_BLOCK)))

    zconv, zqk, yc, q, k, v = _inproj_fwd(x, w["g_mix"], w["w_in"], w["conv_w"], w["conv_b"], qg, kg, bd, tms[0])
    ya, lse, e_all, m_all = _attn_fwd(q, k, v, slopes)
    if "late_weights" in hooks:
        w = {**w, **hooks["late_weights"](lse)}
    x1 = _outproj_fwd(ya, yc, x, w["g_out_conv"], w["g_out_attn"], w["w_out"], tms[1])
    gp, up, h2, x2 = _ffn_fwd(x1, w["g_ffn"], w["w_gate"], w["w_up"], w["w_down"], w["ffn_conv_w"], w["ffn_conv_b"],
                              tms[0])
    dx2, dx2b, loss, dw_pg, dw_pp, dg_ple = _ple_fwd_bwd(x2, p, target, w["g_ple"], w["w_ple_gate"], w["w_ple_proj"], tms[0])
    dh2, dw_down, dw_up, dw_gate, dfcw, dfcb = _ffn_bwd(dx2b, h2, gp, up, w["w_gate"], w["w_up"], w["w_down"],
                                                        w["ffn_conv_w"], w["ffn_conv_b"], tms[0])
    token = None
    if "ffn_grads" in hooks:
        token = hooks["ffn_grads"]({"w_ple_gate": dw_pg, "w_ple_proj": dw_pp, "w_down": dw_down, "w_up": dw_up,
                                    "w_gate": dw_gate, "ffn_conv_b": dfcb})
    dx1, dya, dd, dzconv, dw_out, dg_ffn, dgoc, dgoa, dcw, dcb = _outproj_bwd(
        dh2, dx2, x1, _ordered_after(w["g_ffn"], token), w["w_out"], yc, ya, w["g_out_conv"], w["g_out_attn"], zconv,
        w["conv_w"], w["conv_b"], bd, tms[0])
    token = hooks["outproj_done"](dx1, dw_out) if "outproj_done" in hooks else None
    dq, dk, dv = _attn_bwd(q, k, v, dya, lse, dd, e_all, m_all, slopes if token is None else token)
    token = hooks["attn_done"](dq) if "attn_done" in hooks else None
    dx, dw_in, dg_mix, dqg, dkg = _inproj_bwd(dq, dk, dv, dzconv, zqk, x, dx1, _ordered_after(w["g_mix"], token),
                                              w["w_in"], qg, kg, bd,
                                              tms[0])
    grads = {
        "g_mix": dg_mix, "w_in": dw_in, "conv_w": dcw, "conv_b": dcb,
        "q_norm_g": dqg, "k_norm_g": dkg,
        "g_out_conv": dgoc, "g_out_attn": dgoa, "w_out": dw_out, "g_ffn": dg_ffn, "w_gate": dw_gate, "w_up": dw_up,
        "ffn_conv_w": dfcw, "ffn_conv_b": dfcb, "w_down": dw_down, "g_ple": dg_ple, "w_ple_gate": dw_pg,
        "w_ple_proj": dw_pp,
    }
    return loss, dx, grads


ANY = pl.BlockSpec(memory_space=pl.ANY)
MESH = pl.DeviceIdType.MESH


def _all_gather(shards, name):
    n = len(shards)

    def body(*refs):
        ins, outs = refs[:n], refs[n:2 * n]
        send_sems, recv_sems, local_sems = refs[2 * n:]
        x, y, c = lax.axis_index("x"), lax.axis_index("y"), lax.axis_index("c")
        me, sibling = (x, y, c), (x, y, 1 - c)
        chips = [(1 - x, y), (x, 1 - y), (1 - x, 1 - y)]

        def slot(dev):
            return 4 * dev[0] + 2 * dev[1] + dev[2]

        def copy(b, k, block, to, src=None):
            dst = outs[b].at[slot(block)]
            return pltpu.make_async_remote_copy(
                src_ref=dst if src is None else src, dst_ref=dst, send_sem=send_sems.at[b, k],
                recv_sem=recv_sems.at[b, k], device_id=to, device_id_type=MESH)

        mine = [pltpu.make_async_copy(ins[b], outs[b].at[slot(me)], local_sems.at[b]) for b in range(n)]
        first, passed = [], []
        for b in range(n):
            mine[b].start()
            first.append(copy(b, 0, me, sibling, src=ins[b]))
            first += [copy(b, 1 + j, me, (*chip, c), src=ins[b]) for j, chip in enumerate(chips)]
        for cp in first:
            cp.start()
        for j, chip in enumerate(chips):
            for b in range(n):
                copy(b, 1 + j, (*chip, c), me).wait_recv()
                fwd = copy(b, 4 + j, (*chip, c), sibling)
                fwd.start()
                passed.append(fwd)
        for b in range(n):
            copy(b, 0, sibling, me).wait_recv()
            for j, chip in enumerate(chips):
                copy(b, 4 + j, (*chip, 1 - c), me).wait_recv()
        for cp in first + passed:
            cp.wait_send()
        for cp in mine:
            cp.wait()

    return pl.pallas_call(
        body, name=name,
        in_specs=[ANY] * n, out_specs=[ANY] * n,
        out_shape=[jax.ShapeDtypeStruct((N_DEV,) + s.shape, s.dtype) for s in shards],
        scratch_shapes=[pltpu.SemaphoreType.DMA((n, 7)), pltpu.SemaphoreType.DMA((n, 7)),
                        pltpu.SemaphoreType.DMA((n,))],
    )(*shards)


HBM = pl.BlockSpec(memory_space=pltpu.HBM)
SEM = pl.BlockSpec(memory_space=pltpu.SEMAPHORE)
EFFECT = pltpu.SideEffectType.DATAFLOW_SIDE_EFFECTING
FLIPS = ((0, 0, 1), (0, 1, 0), (0, 1, 1), (1, 0, 0), (1, 0, 1), (1, 1, 0), (1, 1, 1))


def _flip_peers():
    pos = (lax.axis_index("x"), lax.axis_index("y"), lax.axis_index("c"))
    return [tuple(1 - a if f else a for a, f in zip(pos, flip)) for flip in FLIPS]


def _hbm(a):
    return pltpu.with_memory_space_constraint(a, pltpu.HBM)


def _own_copies(own, src_refs, land_refs, send_sems, n_remote):
    return [pltpu.make_async_copy(src, dst, send_sems.at[n_remote + i])
            for i, (src, dst) in enumerate(own(src_refs, land_refs) if own else [])]


def _split_start(name, srcs, lands, plan, n_copies, after, own=None):
    n, m = len(srcs), len(lands)

    def body(*refs):
        send_sems, recv_sems, token = refs[n + m + 1], refs[n + m + 2], refs[-1]
        for i, (src, dst, peer) in enumerate(plan(refs[:n], refs[n:n + m])):
            pltpu.make_async_remote_copy(src_ref=src, dst_ref=dst, send_sem=send_sems.at[i], recv_sem=recv_sems.at[i],
                                         device_id=peer, device_id_type=MESH).start()
        for copy in _own_copies(own, refs[:n], refs[n:n + m], send_sems, n_copies):
            copy.start()
        token[...] = jnp.zeros_like(token)

    outs = pl.pallas_call(
        body, name=name + "_start",
        in_specs=[HBM] * (n + m) + [ANY],
        out_specs=[SEM, SEM] + [HBM] * (n + m) + [pl.BlockSpec(memory_space=pltpu.VMEM)],
        out_shape=[pltpu.SemaphoreType.DMA((n_copies + (n if own else 0),)), pltpu.SemaphoreType.DMA((n_copies,))]
        + [pltpu.HBM(a.shape, a.dtype) for a in list(srcs) + list(lands)] + [jax.ShapeDtypeStruct((1, D_MODEL), F32)],
        input_output_aliases={i: 2 + i for i in range(n + m)},
        compiler_params=pltpu.CompilerParams(has_side_effects=EFFECT),
    )(*[_hbm(a) for a in list(srcs) + list(lands)], after)
    return (outs[0], outs[1], outs[2:2 + n], outs[2 + n:2 + n + m]), outs[-1]


def _split_wait(name, started, plan, after, own=None):
    send_sems, recv_sems, srcs, lands = started
    n, m = len(srcs), len(lands)

    def body(*refs):
        send_ref, recv_ref = refs[n + m], refs[n + m + 1]
        copies = plan(refs[:n], refs[n:n + m])
        for i, (src, dst, peer) in enumerate(copies):
            copy = pltpu.make_async_remote_copy(src_ref=src, dst_ref=dst, send_sem=send_ref.at[i],
                                                recv_sem=recv_ref.at[i], device_id=peer, device_id_type=MESH)
            copy.wait_send()
            copy.wait_recv()
        for copy in _own_copies(own, refs[:n], refs[n:n + m], send_ref, len(copies)):
            copy.wait()

    outs = pl.pallas_call(
        body, name=name + "_wait",
        in_specs=[HBM] * (n + m) + [SEM, SEM, ANY],
        out_specs=[HBM] * (n + m),
        out_shape=[pltpu.HBM(a.shape, a.dtype) for a in list(srcs) + list(lands)],
        input_output_aliases={i: i for i in range(n + m)},
        compiler_params=pltpu.CompilerParams(has_side_effects=EFFECT),
    )(*srcs, *lands, send_sems, recv_sems, after)
    return outs[:n], outs[n:]


def _gather_plan(srcs, lands):
    slot = 4 * lax.axis_index("x") + 2 * lax.axis_index("y") + lax.axis_index("c")
    return [(src, land.at[slot], peer) for src, land in zip(srcs, lands) for peer in _flip_peers()]


def _own_slot(srcs, lands):
    slot = 4 * lax.axis_index("x") + 2 * lax.axis_index("y") + lax.axis_index("c")
    return [(src, land.at[slot]) for src, land in zip(srcs, lands)]


def _sibling_plan(srcs, lands):
    x, y, c = lax.axis_index("x"), lax.axis_index("y"), lax.axis_index("c")
    return [(src.at[k, 1 - c], land.at[k], (x, y, 1 - c)) for src, land in zip(srcs, lands) for k in range(N_CHIP)]


def _chip_plan(srcs, lands):
    x, y, c = lax.axis_index("x"), lax.axis_index("y"), lax.axis_index("c")
    return [(src.at[2 * cx + cy], land.at[2 * x + y], (cx, cy, c))
            for src, land in zip(srcs, lands) for cx, cy in ((1 - x, y), (x, 1 - y), (1 - x, 1 - y))]


def _row_tile(rows):
    for tr in range(min(rows, 512), 15, -16):
        if rows % tr == 0:
            return tr
    return rows


def _pair_sums(gs, lands, core, name):
    n = len(gs)

    def body(c_ref, *refs):
        for b in range(n):
            out = refs[2 * n + b]
            out[...] = (refs[b][...].astype(F32) + refs[n + b][...].astype(F32)).astype(out.dtype)

    def slab(a):
        return pl.BlockSpec((None,) + a.shape[1:], lambda k, c_ref: (k, 0, 0))

    return pl.pallas_call(
        body, name=name,
        grid_spec=pltpu.PrefetchScalarGridSpec(
            num_scalar_prefetch=1, grid=(N_CHIP,),
            in_specs=[pl.BlockSpec((None, None) + g.shape[2:], lambda k, c_ref: (k, c_ref[0], 0, 0)) for g in gs]
            + [slab(a) for a in lands],
            out_specs=[slab(a) for a in lands]),
        out_shape=[jax.ShapeDtypeStruct(a.shape, a.dtype) for a in lands],
        compiler_params=_cparams("parallel"),
    )(core, *gs, *lands)


def _adamw(own, arrived, chip, w, m, v, name):
    k, rows, cols = arrived.shape
    tr = _row_tile(rows)
    c1 = 1.0 / (1.0 - ADAM_B1 ** ADAM_STEP)
    c2 = 1.0 / (1.0 - ADAM_B2 ** ADAM_STEP)

    def body(chip_ref, o_ref, p_ref, w_ref, m_ref, v_ref, g_ref, d_ref, nm_ref, nv_ref):
        def slab(j):
            return jnp.where(chip_ref[0] == j, o_ref[j], p_ref[j]).astype(F32)

        g = slab(0)
        for j in range(1, k):
            g = g + slab(j)
        g_ref[...] = g
        nm = ADAM_B1 * m_ref[...] + (1.0 - ADAM_B1) * g
        nv = ADAM_B2 * v_ref[...] + (1.0 - ADAM_B2) * (g * g)
        nm_ref[...] = nm
        nv_ref[...] = nv
        d_ref[...] = -ADAM_LR * ((nm * c1) / (jnp.sqrt(nv * c2) + ADAM_EPS) + ADAM_WD * w_ref[...])

    blk = pl.BlockSpec((tr, cols), lambda i, c: (i, 0))
    stack = pl.BlockSpec((k, tr, cols), lambda i, c: (0, i, 0))
    return pl.pallas_call(
        body, name=name,
        grid_spec=pltpu.PrefetchScalarGridSpec(num_scalar_prefetch=1, grid=(rows // tr,),
                                               in_specs=[stack, stack, blk, blk, blk], out_specs=[blk] * 4),
        out_shape=[jax.ShapeDtypeStruct((rows, cols), F32)] * 4,
        compiler_params=_cparams("parallel"),
    )(chip, own, arrived, w, m, v)


SMALL_LAYOUT = (("g_mix", 0, 1024), ("conv_b", 1, 512), ("q_norm_g", 2, 64), ("k_norm_g", 3, 64),
                ("g_out_conv", 4, 512), ("g_out_attn", 5, 512), ("g_ffn", 6, 1024), ("ffn_conv_b", 7, 2816),
                ("g_ple", 10, 1024))
CONV_W_ROW = 11
FFN_CONV_W_ROW = 14
LOSS_ROW = 23


def _row_pieces(cols):
    return [(c, min(1024, cols - c)) for c in range(0, cols, 1024)]


def _pack_small(grads, loss_tile):
    names = [n for n, _, _ in SMALL_LAYOUT]

    def body(*refs):
        ins, cw_ref, fcw_ref, loss_ref, out_ref = refs[:len(names)], refs[-4], refs[-3], refs[-2], refs[-1]
        out_ref[...] = jnp.zeros_like(out_ref)
        for ref, (_, row, cols) in zip(ins, SMALL_LAYOUT):
            if ref.shape[1] == ATTN_W and cols == HEAD_DIM:
                out_ref[row:row + 1, 0:cols] = sum(ref[:, h:h + cols] for h in range(0, ATTN_W, cols))
                continue
            for j, (c, width) in enumerate(_row_pieces(cols)):
                out_ref[row + j:row + j + 1, 0:width] = ref[:, c:c + width]
        for k in range(3):
            out_ref[CONV_W_ROW + k:CONV_W_ROW + k + 1, 0:CONV_W] = cw_ref[k:k + 1, :]
            for j, (c, width) in enumerate(_row_pieces(D_FF)):
                row = FFN_CONV_W_ROW + 3 * k + j
                out_ref[row:row + 1, 0:width] = fcw_ref[k:k + 1, c:c + width]
        out_ref[LOSS_ROW:LOSS_ROW + 1, 0:128] = loss_ref[0:1, :]

    return pl.pallas_call(
        body, name="pack_small_grads", out_shape=jax.ShapeDtypeStruct((SMALL_ROWS, 1024), F32),
    )(*[grads[n] for n in names], grads["conv_w"], grads["ffn_conv_w"], loss_tile)


def _adamw_small(arrived, conv_parts, fconv_parts, wts, mom, var):
    names = [n for n, _, _ in SMALL_LAYOUT] + ["conv_w", "ffn_conv_w"]
    c1 = 1.0 / (1.0 - ADAM_B1 ** ADAM_STEP)
    c2 = 1.0 / (1.0 - ADAM_B2 ** ADAM_STEP)
    n = len(names)

    def body(*refs):
        land, cw_ref, fcw_ref = refs[0], refs[1], refs[2]
        state = refs[3:3 + 3 * n]
        outs = refs[3 + 3 * n:]

        def total(piece):
            acc = piece(0)
            for d in range(1, N_DEV):
                acc = acc + piece(d)
            return acc

        for i, name in enumerate(names):
            if name == "conv_w":
                g = total(lambda d: cw_ref[d])
            elif name == "ffn_conv_w":
                g = total(lambda d: fcw_ref[d])
            else:
                _, row, cols = SMALL_LAYOUT[i]
                pieces = [total(lambda d, j=j, width=width: land[d, row + j:row + j + 1, 0:width])
                          for j, (_, width) in enumerate(_row_pieces(cols))]
                g = pieces[0] if len(pieces) == 1 else jnp.concatenate(pieces, axis=1)
            w_ref, m_ref, v_ref = state[3 * i:3 * i + 3]
            nm = ADAM_B1 * m_ref[...] + (1.0 - ADAM_B1) * g
            nv = ADAM_B2 * v_ref[...] + (1.0 - ADAM_B2) * (g * g)
            outs[4 * i][...] = g
            outs[4 * i + 1][...] = -ADAM_LR * ((nm * c1) / (jnp.sqrt(nv * c2) + ADAM_EPS) + ADAM_WD * w_ref[...])
            outs[4 * i + 2][...] = nm
            outs[4 * i + 3][...] = nv
        outs[-1][...] = total(lambda d: land[d, LOSS_ROW:LOSS_ROW + 1, 0:128])

    state = [a[nm_] for nm_ in names for a in (wts, mom, var)]
    shapes = [jax.ShapeDtypeStruct(wts[nm_].shape, F32) for nm_ in names for _ in range(4)]
    outs = pl.pallas_call(
        body, name="adamw_small", out_shape=shapes + [jax.ShapeDtypeStruct((1, 128), F32)],
    )(arrived, conv_parts, fconv_parts, *state)
    return {nm_: tuple(outs[4 * i:4 * i + 4]) for i, nm_ in enumerate(names)}, outs[-1][0, 0]


COL_SHARDED = ("w_in", "w_ple_proj")
TRANSPOSED = ("w_gate", "w_up")
CONV_SHARDED = (("conv_w", CONV_W), ("ffn_conv_w", D_FF))


def _gathered_to_full(name, gathered):
    if name in COL_SHARDED:
        return gathered.transpose(1, 0, 2).reshape(gathered.shape[1], -1)
    return gathered.reshape(-1, gathered.shape[2])


def _full_to_stacked(name, grad, shard_shape):
    sr, sc = shard_shape
    if grad.ndim == 3:
        a = grad
    elif name in COL_SHARDED:
        a = grad.reshape(sr, N_DEV, sc).transpose(1, 0, 2)
    else:
        a = grad.reshape(N_DEV, sr, sc)
    return a.astype(BF16).reshape(N_CHIP, 2, sr, sc)


def _pad_rows(vec, rows):
    return jnp.pad(vec, (0, rows * 1024 - vec.shape[0])).reshape(rows, 1024)


def kernel(x, p, g_mix, w_in, conv_w, conv_b, q_norm_g, k_norm_g, g_out_conv, g_out_attn, w_out, g_ffn, w_gate, w_up, ffn_conv_w, ffn_conv_b, w_down, g_ple, w_ple_gate, w_ple_proj, loss_target, m_g_mix, m_w_in, m_conv_w, m_conv_b, m_q_norm_g, m_k_norm_g, m_g_out_conv, m_g_out_attn, m_w_out, m_g_ffn, m_w_gate, m_w_up, m_ffn_conv_w, m_ffn_conv_b, m_w_down, m_g_ple, m_w_ple_gate, m_w_ple_proj, v_g_mix, v_w_in, v_conv_w, v_conv_b, v_q_norm_g, v_k_norm_g, v_g_out_conv, v_g_out_attn, v_w_out, v_g_ffn, v_w_gate, v_w_up, v_ffn_conv_w, v_ffn_conv_b, v_w_down, v_g_ple, v_w_ple_gate, v_w_ple_proj):
    args = dict(locals())
    names = ["g_mix", "w_in", "conv_w", "conv_b", "q_norm_g", "k_norm_g", "g_out_conv", "g_out_attn", "w_out", "g_ffn",
             "w_gate", "w_up", "ffn_conv_w", "ffn_conv_b", "w_down", "g_ple", "w_ple_gate", "w_ple_proj"]
    big = list(BIG)
    conv = [n for n, _ in CONV_SHARDED]

    def local(prefix):
        out = {n: (args[prefix + n][0] if n in big or n in conv else args[prefix + n]) for n in names}
        out.update({n: out[n].T for n in TRANSPOSED})
        return out

    wts, mom, var = local(""), local("m_"), local("v_")
    shard_shapes = {n: wts[n].shape for n in big}
    dev = 4 * lax.axis_index("x") + 2 * lax.axis_index("y") + lax.axis_index("c")
    core = lax.axis_index("c").astype(jnp.int32).reshape(1)

    conv_local = _pad_rows(jnp.concatenate([wts[n].reshape(-1) for n in conv]), 8).reshape(8, 1024)
    late = [n for n in big if n != "w_in"]
    w_in_all, conv_all = _all_gather([wts["w_in"].astype(BF16), conv_local], "gather_weights")
    late_shards = [wts[n].astype(BF16) for n in late]
    gathering, token = _split_start("gather_late_weights", late_shards,
                                    [lax.empty((N_DEV,) + s.shape, BF16) for s in late_shards], _gather_plan,
                                    7 * len(late), w_in_all, own=_own_slot)
    full = dict(wts)
    full["w_in"] = _gathered_to_full("w_in", w_in_all)
    full["g_mix"] = _ordered_after(wts["g_mix"], token)
    flying = {}

    def late_weights(after):
        _, lands = _split_wait("gather_late_weights", gathering, _gather_plan, after, own=_own_slot)
        return {n: _gathered_to_full(n, land) for n, land in zip(late, lands)}

    early = ["w_ple_gate", "w_ple_proj", "w_down", "w_up", "w_gate"]

    def ffn_grads(g):
        stacked = [_full_to_stacked(n, g[n], shard_shapes[n]) for n in early]
        flying["sibling"], tok = _split_start("rs_sibling_early", stacked,
                                              [lax.empty((N_CHIP,) + s.shape[2:], BF16) for s in stacked],
                                              _sibling_plan, N_CHIP * len(early), g["ffn_conv_b"])
        return tok

    def outproj_done(after, dw_out):
        stacked, landed = _split_wait("rs_sibling_early", flying["sibling"], _sibling_plan, after)
        parts = _pair_sums(stacked, landed, core, "rs_pair_sums_early")
        flying["chip"], tok = _split_start("rs_chip_early", parts, [lax.empty(q.shape, BF16) for q in parts],
                                           _chip_plan, 3 * len(early), landed[0])
        mid = [_full_to_stacked("w_out", dw_out, shard_shapes["w_out"])]
        flying["sibling_mid"], tok = _split_start("rs_sibling_mid", mid,
                                                  [lax.empty((N_CHIP,) + s.shape[2:], BF16) for s in mid],
                                                  _sibling_plan, N_CHIP, tok)
        return tok

    def attn_done(after):
        stacked, landed = _split_wait("rs_sibling_mid", flying["sibling_mid"], _sibling_plan, after)
        parts = _pair_sums(stacked, landed, core, "rs_pair_sums_mid")
        flying["chip_mid"], tok = _split_start("rs_chip_mid", parts, [lax.empty(q.shape, BF16) for q in parts],
                                               _chip_plan, 3, landed[0])
        return tok

    off = 0
    for n, width in CONV_SHARDED:
        sc = width // N_DEV
        a = conv_all.reshape(N_DEV, -1)[:, off:off + 3 * sc].reshape(N_DEV, 3, sc)
        full[n] = a.transpose(1, 0, 2).reshape(3, width)
        off += 3 * sc

    loss, dx, grads = _local_step(x[0], p[0, 0], loss_target[0], full, (512, 1024),
                                  {"late_weights": late_weights, "ffn_grads": ffn_grads, "outproj_done": outproj_done,
                                   "attn_done": attn_done})

    chip = (2 * lax.axis_index("x") + lax.axis_index("y")).astype(jnp.int32).reshape(1)

    def adamw_of(group, parts, arrived):
        return {n: _adamw(own, got, chip, wts[n], mom[n], var[n], f"adamw_{n}")
                for n, own, got in zip(group, parts, arrived)}

    last = ["w_in"]
    stacked =[_full_to_stacked(n, grads[n], shard_shapes[n]) for n in last]
    flying["sibling_last"], tok = _split_start("rs_sibling_last", stacked,
                                               [lax.empty((N_CHIP,) + s.shape[2:], BF16) for s in stacked],
                                               _sibling_plan, N_CHIP * len(last), dx)
    packed = _pack_small(grads, loss)
    flying["small"], tok = _split_start("gather_small_grads", [packed], [lax.empty((N_DEV,) + packed.shape, F32)],
                                        _gather_plan, N_DEV - 1, tok, own=_own_slot)
    stacked, landed = _split_wait("rs_sibling_last", flying["sibling_last"], _sibling_plan, tok)
    parts = _pair_sums(stacked, landed, core, "rs_pair_sums_last")
    flying["chip_last"], tok = _split_start("rs_chip_last", parts, [lax.empty(q.shape, BF16) for q in parts],
                                            _chip_plan, 3 * len(last), landed[0])

    parts, arrived = _split_wait("rs_chip_early", flying["chip"], _chip_plan, tok)
    out = adamw_of(early, parts, arrived)
    _, (small_all,) = _split_wait("gather_small_grads", flying["small"], _gather_plan, out[early[-1]][0],
                                  own=_own_slot)
    taps = small_all[:, CONV_W_ROW:CONV_W_ROW + 3, 0:CONV_W]
    ftaps = small_all[:, FFN_CONV_W_ROW:FFN_CONV_W_ROW + 9, :].reshape(N_DEV, 3, 3 * 1024)
    small_out, loss_total = _adamw_small(
        small_all, lax.dynamic_slice(taps, (0, 0, dev * (CONV_W // N_DEV)), (N_DEV, 3, CONV_W // N_DEV)),
        lax.dynamic_slice(ftaps, (0, 0, dev * (D_FF // N_DEV)), (N_DEV, 3, D_FF // N_DEV)), wts, mom, var)
    out.update(small_out)
    parts, arrived = _split_wait("rs_chip_mid", flying["chip_mid"], _chip_plan, small_out["g_mix"][0])
    out.update(adamw_of(["w_out"], parts, arrived))
    parts, arrived = _split_wait("rs_chip_last", flying["chip_last"], _chip_plan, out["w_out"][0])
    out.update(adamw_of(last, parts, arrived))
    def result(n, which):
        a = out[n][which]
        return (a.T if n in TRANSPOSED else a).reshape(args[n].shape)

    return (loss_total, dx[None], *[result(n, which) for which in range(4) for n in names])
```
